```python
import math
import jax, jax.numpy as jnp
from jax import lax
import numpy as np

D_MODEL = 1024
BATCH = 8
SEQ = 2048
DEPTH = 1

D_A = D_MODEL
SGU_CHUNK = 128
SGU_GROUPS = 8
RWKV_HEAD = 64
D_B = D_MODEL
N_HEADS_B = D_B // RWKV_HEAD

def _lora_dim(factor, power):
    return max(32, int(round(factor * D_MODEL ** power / 32)) * 32)

LORA_W = _lora_dim(1.8, 0.5)
LORA_A = _lora_dim(1.8, 0.5)
LORA_G = _lora_dim(0.6, 0.8)
C_B = 3 * D_B + LORA_W + LORA_A + LORA_G
P_TOTAL = 2 * D_A + C_B + 2 * D_MODEL
D_FF = 4 * D_MODEL
NORM_EPS = 1e-6
LN_EPS = 1e-5
GN_EPS = 64e-5

kernel_name = "hybrid_gmlp_rwkv7_gated_block"


def _rms_norm(x, g):
    xf = x.astype(jnp.float32)
    y = xf * lax.rsqrt(jnp.mean(xf * xf, axis=-1, keepdims=True) + NORM_EPS)
    return (y * g.astype(jnp.float32)).astype(x.dtype)


def _layer_norm(x, g, b):
    xf = x.astype(jnp.float32)
    mu = jnp.mean(xf, axis=-1, keepdims=True)
    var = jnp.mean(jnp.square(xf - mu), axis=-1, keepdims=True)
    y = (xf - mu) * lax.rsqrt(var + LN_EPS)
    return (y * g.astype(jnp.float32) + b.astype(jnp.float32)).astype(x.dtype)


def _token_shift(p):
    return jnp.pad(p, ((0, 0), (1, 0), (0, 0)))[:, :-1, :]


def _sgu_branch(p_sgu, ln_w, ln_b, sgu_w, sgu_b, w_proj_a):
    B, T, _ = p_sgu.shape
    z = jax.nn.gelu(p_sgu, approximate=False)
    u, v = jnp.split(z, 2, axis=-1)
    v = _layer_norm(v, ln_w, ln_b)
    n_chunks = T // SGU_CHUNK
    dg = D_A // SGU_GROUPS
    v = v.reshape(B, n_chunks, SGU_CHUNK, SGU_GROUPS, dg)
    mask = jnp.tril(jnp.ones((SGU_CHUNK, SGU_CHUNK), dtype=sgu_w.dtype))
    ws = sgu_w * mask[None]
    sv = jnp.einsum('gij,bcjgd->bcigd', ws, v)
    sv = sv + jnp.swapaxes(sgu_b, 0, 1)[None, None, :, :, None]
    s = u * sv.reshape(B, T, D_A)
    return s @ w_proj_a


def _rwkv7_scan(r, w, k, v, kk, a):
    B, T, H, N = r.shape
    xs = tuple(jnp.moveaxis(t, 1, 0) for t in (r, w, k, v, kk, a))

    def step(S, inp):
        r_t, w_t, k_t, v_t, kk_t, a_t = inp
        sa = jnp.einsum('bhvk,bhk->bhv', S, -kk_t)
        S = (S * w_t[:, :, None, :]
             + sa[..., None] * (kk_t * a_t)[:, :, None, :]
             + v_t[..., None] * k_t[:, :, None, :])
        o = jnp.einsum('bhvk,bhk->bhv', S, r_t)
        return S, o

    S0 = jnp.zeros((B, H, N, N), dtype=jnp.float32)
    _, o = lax.scan(step, S0, xs)
    return jnp.moveaxis(o, 0, 1)


def _rwkv7_branch(p_rwkv, shift_b, w_lora_w, w0, a_lora_w, a0, g_lora_w,
                  k_k, k_a, r_k, ln_x_w, ln_x_b, w_proj_b):
    B, T, _ = p_rwkv.shape
    f32 = jnp.float32
    p = p_rwkv.astype(f32)
    sb = shift_b.astype(f32)
    q = p * sb[0] + _token_shift(p) * sb[1]
    cuts = np.cumsum([D_B, D_B, D_B, LORA_W, LORA_A]).tolist()
    r, k, v, xw, xa, xg = jnp.split(q, cuts, axis=-1)
    w = -jax.nn.softplus(-(w0.astype(f32) + jnp.tanh(xw) @ w_lora_w.astype(f32))) - 0.5
    decay = jnp.exp(-jnp.exp(w))
    aa = jax.nn.sigmoid(a0.astype(f32) + xa @ a_lora_w.astype(f32))
    g = jax.nn.sigmoid(xg) @ g_lora_w.astype(f32)
    kk = (k * k_k.astype(f32)).reshape(B, T, N_HEADS_B, RWKV_HEAD)
    kk = kk / jnp.maximum(jnp.linalg.norm(kk, axis=-1, keepdims=True), 1e-12)
    k = k * (1.0 + (aa - 1.0) * k_a.astype(f32))
    hs = lambda t: t.reshape(B, T, N_HEADS_B, RWKV_HEAD)
    rh, kh, vh = hs(r), hs(k), hs(v)
    o = _rwkv7_scan(rh, hs(decay), kh, vh, kk, hs(aa))
    mu = jnp.mean(o, axis=-1, keepdims=True)
    var = jnp.mean(jnp.square(o - mu), axis=-1, keepdims=True)
    o = ((o - mu) * lax.rsqrt(var + GN_EPS)).reshape(B, T, D_B)
    o = o * ln_x_w.astype(f32) + ln_x_b.astype(f32)
    r_k_h = r_k.astype(f32).reshape(N_HEADS_B, RWKV_HEAD)
    bonus = jnp.sum(rh * kh * r_k_h, axis=-1, keepdims=True) * vh
    o = (o + bonus.reshape(B, T, D_B)) * g
    return (o @ w_proj_b.astype(f32)).astype(p_rwkv.dtype)


def _fwd_setup_inputs(seed: int = 0) -> dict:
    key = jax.random.key(seed)
    ks = jax.random.split(key, 32)
    L = DEPTH
    f32 = jnp.float32

    def nrm(k, shape, scale):
        return jax.random.normal(k, shape, f32) * scale

    mu = jax.random.uniform(ks[8], (L, C_B), f32)
    return {
        "x": nrm(ks[0], (BATCH, SEQ, D_MODEL), 1.0),
        "g_mix": 1.0 + nrm(ks[1], (L, D_MODEL), 0.02),
        "w_in": nrm(ks[2], (L, D_MODEL, P_TOTAL), D_MODEL ** -0.5),
        "sgu_ln_w": 1.0 + nrm(ks[3], (L, D_A), 0.02),
        "sgu_ln_b": nrm(ks[4], (L, D_A), 0.02),
        "sgu_w": nrm(ks[5], (L, SGU_GROUPS, SGU_CHUNK, SGU_CHUNK), SGU_CHUNK ** -0.5),
        "sgu_b": 1.0 + nrm(ks[6], (L, SGU_GROUPS, SGU_CHUNK), 0.02),
        "w_proj_a": nrm(ks[7], (L, D_A, D_MODEL), D_A ** -0.5),
        "shift_b": jnp.stack([1.0 - mu, mu], axis=1),
        "w_lora_w": nrm(ks[9], (L, LORA_W, D_B), 0.1 * LORA_W ** -0.5),
        "w0": jax.random.uniform(ks[10], (L, D_B), f32, minval=-4.0, maxval=1.0),
        "a_lora_w": nrm(ks[11], (L, LORA_A, D_B), LORA_A ** -0.5),
        "a0": nrm(ks[12], (L, D_B), 0.1),
        "g_lora_w": nrm(ks[13], (L, LORA_G, D_B), LORA_G ** -0.5),
        "k_k": 0.85 + nrm(ks[14], (L, D_B), 0.02),
        "k_a": 1.0 + nrm(ks[15], (L, D_B), 0.02),
        "r_k": nrm(ks[16], (L, D_B), 0.1),
        "ln_x_w": 1.0 + nrm(ks[17], (L, D_B), 0.02),
        "ln_x_b": nrm(ks[18], (L, D_B), 0.02),
        "w_proj_b": nrm(ks[19], (L, D_B, D_MODEL), D_B ** -0.5),
        "w_out": nrm(ks[20], (L, D_MODEL, D_MODEL), D_MODEL ** -0.5),
        "g_ffn": 1.0 + nrm(ks[21], (L, D_MODEL), 0.02),
        "w_ffn1": nrm(ks[22], (L, D_MODEL, D_FF), D_MODEL ** -0.5),
        "w_ffn2": nrm(ks[23], (L, D_FF, D_MODEL), D_FF ** -0.5),
        "g_final": 1.0 + nrm(ks[24], (D_MODEL,), 0.02),
    }


def _fwd_reference(x, g_mix, w_in, sgu_ln_w, sgu_ln_b, sgu_w, sgu_b, w_proj_a, shift_b,
              w_lora_w, w0, a_lora_w, a0, g_lora_w, k_k, k_a, r_k, ln_x_w, ln_x_b,
              w_proj_b, w_out, g_ffn, w_ffn1, w_ffn2, g_final):
    h = x
    for l in range(DEPTH):
        a = _rms_norm(h, g_mix[l])
        p = a @ w_in[l]
        p_sgu, p_rwkv, p_gate = jnp.split(p, [2 * D_A, 2 * D_A + C_B], axis=-1)
        y_a = _sgu_branch(p_sgu, sgu_ln_w[l], sgu_ln_b[l], sgu_w[l], sgu_b[l], w_proj_a[l])
        y_b = _rwkv7_branch(p_rwkv, shift_b[l], w_lora_w[l], w0[l], a_lora_w[l], a0[l],
                            g_lora_w[l], k_k[l], k_a[l], r_k[l], ln_x_w[l], ln_x_b[l],
                            w_proj_b[l])
        gate_a, gate_b = jnp.split(p_gate, 2, axis=-1)
        mixed = jax.nn.sigmoid(gate_a) * y_a + jax.nn.sigmoid(gate_b) * y_b
        h = h + mixed @ w_out[l]
        f = _rms_norm(h, g_ffn[l])
        h = h + jnp.square(jax.nn.relu(f @ w_ffn1[l])) @ w_ffn2[l]
    return _rms_norm(h, g_final)


import jax as _jax
import jax.numpy as _jnp

TWIN_FORMAT = 'train_step'
FWD_PARAMS = ['x', 'g_mix', 'w_in', 'sgu_ln_w', 'sgu_ln_b', 'sgu_w', 'sgu_b', 'w_proj_a', 'shift_b', 'w_lora_w', 'w0', 'a_lora_w', 'a0', 'g_lora_w', 'k_k', 'k_a', 'r_k', 'ln_x_w', 'ln_x_b', 'w_proj_b', 'w_out', 'g_ffn', 'w_ffn1', 'w_ffn2', 'g_final']
TWIN_WEIGHTS = ['g_mix', 'w_in', 'sgu_ln_w', 'sgu_ln_b', 'sgu_w', 'sgu_b', 'w_proj_a', 'shift_b', 'w_lora_w', 'w0', 'a_lora_w', 'a0', 'g_lora_w', 'k_k', 'k_a', 'r_k', 'ln_x_w', 'ln_x_b', 'w_proj_b', 'w_out', 'g_ffn', 'w_ffn1', 'w_ffn2', 'g_final']
TWIN_DIFF_INPUT = 'x'
TWIN_INPUTS = ['x', 'g_mix', 'w_in', 'sgu_ln_w', 'sgu_ln_b', 'sgu_w', 'sgu_b', 'w_proj_a', 'shift_b', 'w_lora_w', 'w0', 'a_lora_w', 'a0', 'g_lora_w', 'k_k', 'k_a', 'r_k', 'ln_x_w', 'ln_x_b', 'w_proj_b', 'w_out', 'g_ffn', 'w_ffn1', 'w_ffn2', 'g_final', 'loss_target', 'm_g_mix', 'm_w_in', 'm_sgu_ln_w', 'm_sgu_ln_b', 'm_sgu_w', 'm_sgu_b', 'm_w_proj_a', 'm_shift_b', 'm_w_lora_w', 'm_w0', 'm_a_lora_w', 'm_a0', 'm_g_lora_w', 'm_k_k', 'm_k_a', 'm_r_k', 'm_ln_x_w', 'm_ln_x_b', 'm_w_proj_b', 'm_w_out', 'm_g_ffn', 'm_w_ffn1', 'm_w_ffn2', 'm_g_final', 'v_g_mix', 'v_w_in', 'v_sgu_ln_w', 'v_sgu_ln_b', 'v_sgu_w', 'v_sgu_b', 'v_w_proj_a', 'v_shift_b', 'v_w_lora_w', 'v_w0', 'v_a_lora_w', 'v_a0', 'v_g_lora_w', 'v_k_k', 'v_k_a', 'v_r_k', 'v_ln_x_w', 'v_ln_x_b', 'v_w_proj_b', 'v_w_out', 'v_g_ffn', 'v_w_ffn1', 'v_w_ffn2', 'v_g_final']
TWIN_OUTPUTS = ['loss', 'grad_x', 'grad_g_mix', 'grad_w_in', 'grad_sgu_ln_w', 'grad_sgu_ln_b', 'grad_sgu_w', 'grad_sgu_b', 'grad_w_proj_a', 'grad_shift_b', 'grad_w_lora_w', 'grad_w0', 'grad_a_lora_w', 'grad_a0', 'grad_g_lora_w', 'grad_k_k', 'grad_k_a', 'grad_r_k', 'grad_ln_x_w', 'grad_ln_x_b', 'grad_w_proj_b', 'grad_w_out', 'grad_g_ffn', 'grad_w_ffn1', 'grad_w_ffn2', 'grad_g_final', 'delta_g_mix', 'delta_w_in', 'delta_sgu_ln_w', 'delta_sgu_ln_b', 'delta_sgu_w', 'delta_sgu_b', 'delta_w_proj_a', 'delta_shift_b', 'delta_w_lora_w', 'delta_w0', 'delta_a_lora_w', 'delta_a0', 'delta_g_lora_w', 'delta_k_k', 'delta_k_a', 'delta_r_k', 'delta_ln_x_w', 'delta_ln_x_b', 'delta_w_proj_b', 'delta_w_out', 'delta_g_ffn', 'delta_w_ffn1', 'delta_w_ffn2', 'delta_g_final', 'new_m_g_mix', 'new_m_w_in', 'new_m_sgu_ln_w', 'new_m_sgu_ln_b', 'new_m_sgu_w', 'new_m_sgu_b', 'new_m_w_proj_a', 'new_m_shift_b', 'new_m_w_lora_w', 'new_m_w0', 'new_m_a_lora_w', 'new_m_a0', 'new_m_g_lora_w', 'new_m_k_k', 'new_m_k_a', 'new_m_r_k', 'new_m_ln_x_w', 'new_m_ln_x_b', 'new_m_w_proj_b', 'new_m_w_out', 'new_m_g_ffn', 'new_m_w_ffn1', 'new_m_w_ffn2', 'new_m_g_final', 'new_v_g_mix', 'new_v_w_in', 'new_v_sgu_ln_w', 'new_v_sgu_ln_b', 'new_v_sgu_w', 'new_v_sgu_b', 'new_v_w_proj_a', 'new_v_shift_b', 'new_v_w_lora_w', 'new_v_w0', 'new_v_a_lora_w', 'new_v_a0', 'new_v_g_lora_w', 'new_v_k_k', 'new_v_k_a', 'new_v_r_k', 'new_v_ln_x_w', 'new_v_ln_x_b', 'new_v_w_proj_b', 'new_v_w_out', 'new_v_g_ffn', 'new_v_w_ffn1', 'new_v_w_ffn2', 'new_v_g_final']
TWIN_LEAF_KINDS = {'loss': 'loss', 'grad_x': 'grad_x', 'grad_g_mix': 'grad_w', 'grad_w_in': 'grad_w', 'grad_sgu_ln_w': 'grad_w', 'grad_sgu_ln_b': 'grad_w', 'grad_sgu_w': 'grad_w', 'grad_sgu_b': 'grad_w', 'grad_w_proj_a': 'grad_w', 'grad_shift_b': 'grad_w', 'grad_w_lora_w': 'grad_w', 'grad_w0': 'grad_w', 'grad_a_lora_w': 'grad_w', 'grad_a0': 'grad_w', 'grad_g_lora_w': 'grad_w', 'grad_k_k': 'grad_w', 'grad_k_a': 'grad_w', 'grad_r_k': 'grad_w', 'grad_ln_x_w': 'grad_w', 'grad_ln_x_b': 'grad_w', 'grad_w_proj_b': 'grad_w', 'grad_w_out': 'grad_w', 'grad_g_ffn': 'grad_w', 'grad_w_ffn1': 'grad_w', 'grad_w_ffn2': 'grad_w', 'grad_g_final': 'grad_w', 'delta_g_mix': 'delta_w', 'delta_w_in': 'delta_w', 'delta_sgu_ln_w': 'delta_w', 'delta_sgu_ln_b': 'delta_w', 'delta_sgu_w': 'delta_w', 'delta_sgu_b': 'delta_w', 'delta_w_proj_a': 'delta_w', 'delta_shift_b': 'delta_w', 'delta_w_lora_w': 'delta_w', 'delta_w0': 'delta_w', 'delta_a_lora_w': 'delta_w', 'delta_a0': 'delta_w', 'delta_g_lora_w': 'delta_w', 'delta_k_k': 'delta_w', 'delta_k_a': 'delta_w', 'delta_r_k': 'delta_w', 'delta_ln_x_w': 'delta_w', 'delta_ln_x_b': 'delta_w', 'delta_w_proj_b': 'delta_w', 'delta_w_out': 'delta_w', 'delta_g_ffn': 'delta_w', 'delta_w_ffn1': 'delta_w', 'delta_w_ffn2': 'delta_w', 'delta_g_final': 'delta_w', 'new_m_g_mix': 'new_m', 'new_m_w_in': 'new_m', 'new_m_sgu_ln_w': 'new_m', 'new_m_sgu_ln_b': 'new_m', 'new_m_sgu_w': 'new_m', 'new_m_sgu_b': 'new_m', 'new_m_w_proj_a': 'new_m', 'new_m_shift_b': 'new_m', 'new_m_w_lora_w': 'new_m', 'new_m_w0': 'new_m', 'new_m_a_lora_w': 'new_m', 'new_m_a0': 'new_m', 'new_m_g_lora_w': 'new_m', 'new_m_k_k': 'new_m', 'new_m_k_a': 'new_m', 'new_m_r_k': 'new_m', 'new_m_ln_x_w': 'new_m', 'new_m_ln_x_b': 'new_m', 'new_m_w_proj_b': 'new_m', 'new_m_w_out': 'new_m', 'new_m_g_ffn': 'new_m', 'new_m_w_ffn1': 'new_m', 'new_m_w_ffn2': 'new_m', 'new_m_g_final': 'new_m', 'new_v_g_mix': 'new_v', 'new_v_w_in': 'new_v', 'new_v_sgu_ln_w': 'new_v', 'new_v_sgu_ln_b': 'new_v', 'new_v_sgu_w': 'new_v', 'new_v_sgu_b': 'new_v', 'new_v_w_proj_a': 'new_v', 'new_v_shift_b': 'new_v', 'new_v_w_lora_w': 'new_v', 'new_v_w0': 'new_v', 'new_v_a_lora_w': 'new_v', 'new_v_a0': 'new_v', 'new_v_g_lora_w': 'new_v', 'new_v_k_k': 'new_v', 'new_v_k_a': 'new_v', 'new_v_r_k': 'new_v', 'new_v_ln_x_w': 'new_v', 'new_v_ln_x_b': 'new_v', 'new_v_w_proj_b': 'new_v', 'new_v_w_out': 'new_v', 'new_v_g_ffn': 'new_v', 'new_v_w_ffn1': 'new_v', 'new_v_w_ffn2': 'new_v', 'new_v_g_final': 'new_v'}


def _forward(args):
    return _fwd_reference(*[args[k] for k in FWD_PARAMS])


def _output_shape():
    out = _jax.eval_shape(lambda: _forward(_fwd_setup_inputs(0)))
    return out.shape, out.dtype

N_MICROBATCH = 1
ADAM_LR = 0.001
ADAM_B1 = 0.9
ADAM_B2 = 0.999
ADAM_EPS = 1e-08
ADAM_WD = 0.01
ADAM_STEP = 10
PER_EXAMPLE_BATCH_AXIS = {'x': 0, 'loss_target': 0}
SHARED_INPUTS = []
_WEIGHT_DTYPES = {'g_mix': _jnp.float32, 'w_in': _jnp.float32, 'sgu_ln_w': _jnp.float32, 'sgu_ln_b': _jnp.float32, 'sgu_w': _jnp.float32, 'sgu_b': _jnp.float32, 'w_proj_a': _jnp.float32, 'shift_b': _jnp.float32, 'w_lora_w': _jnp.float32, 'w0': _jnp.float32, 'a_lora_w': _jnp.float32, 'a0': _jnp.float32, 'g_lora_w': _jnp.float32, 'k_k': _jnp.float32, 'k_a': _jnp.float32, 'r_k': _jnp.float32, 'ln_x_w': _jnp.float32, 'ln_x_b': _jnp.float32, 'w_proj_b': _jnp.float32, 'w_out': _jnp.float32, 'g_ffn': _jnp.float32, 'w_ffn1': _jnp.float32, 'w_ffn2': _jnp.float32, 'g_final': _jnp.float32}
MOMENT_SCALE = {'g_mix': 1.019946e-01, 'w_in': 3.587431e-02, 'sgu_ln_w': 2.960054e-02, 'sgu_ln_b': 2.888050e-02, 'sgu_w': 2.994704e-02, 'sgu_b': 4.269121e-02, 'w_proj_a': 5.378790e-02, 'shift_b': 4.397453e-02, 'w_lora_w': 2.912746e-03, 'w0': 1.838984e-02, 'a_lora_w': 1.227116e-02, 'a0': 1.461495e-02, 'g_lora_w': 3.583699e-02, 'k_k': 1.703932e-02, 'k_a': 3.805849e-02, 'r_k': 7.826472e-02, 'ln_x_w': 3.771556e-02, 'ln_x_b': 3.783613e-02, 'w_proj_b': 3.570660e-02, 'w_out': 6.462827e-02, 'g_ffn': 1.083084e-01, 'w_ffn1': 5.344452e-02, 'w_ffn2': 9.966124e-02, 'g_final': 1.614361e+01}


def _to_microbatches(a, axis):
    t = _jnp.moveaxis(a, axis, 0)
    t = t.reshape((N_MICROBATCH, t.shape[0] // N_MICROBATCH) + t.shape[1:])
    return _jnp.moveaxis(t, 1, axis + 1)


def setup_inputs(seed: int = 0) -> dict:
    inp = _fwd_setup_inputs(seed)
    key = _jax.random.fold_in(_jax.random.key(seed), 7919)
    shape, _ = _output_shape()
    out = dict(inp)
    out["loss_target"] = _jax.random.normal(_jax.random.fold_in(key, 0), shape, _jnp.float32)
    for i, name in enumerate(TWIN_WEIGHTS):
        w = inp[name].astype(_jnp.float32)
        if MOMENT_SCALE is None:
            s = _jnp.sqrt(_jnp.mean(_jnp.square(w)) + 1e-30)
        else:
            s = MOMENT_SCALE[name]
        km, kv = _jax.random.split(_jax.random.fold_in(key, i + 1))
        out[name] = w
        out["m_" + name] = s * _jax.random.normal(km, w.shape, _jnp.float32)
        out["v_" + name] = (s * s) * _jax.random.uniform(kv, w.shape, _jnp.float32, 0.5, 1.5)
    if N_MICROBATCH > 1:
        for name, axis in PER_EXAMPLE_BATCH_AXIS.items():
            out[name] = _to_microbatches(out[name], axis)
    return {'x': out['x'], 'g_mix': out['g_mix'], 'w_in': out['w_in'], 'sgu_ln_w': out['sgu_ln_w'], 'sgu_ln_b': out['sgu_ln_b'], 'sgu_w': out['sgu_w'], 'sgu_b': out['sgu_b'], 'w_proj_a': out['w_proj_a'], 'shift_b': out['shift_b'], 'w_lora_w': out['w_lora_w'], 'w0': out['w0'], 'a_lora_w': out['a_lora_w'], 'a0': out['a0'], 'g_lora_w': out['g_lora_w'], 'k_k': out['k_k'], 'k_a': out['k_a'], 'r_k': out['r_k'], 'ln_x_w': out['ln_x_w'], 'ln_x_b': out['ln_x_b'], 'w_proj_b': out['w_proj_b'], 'w_out': out['w_out'], 'g_ffn': out['g_ffn'], 'w_ffn1': out['w_ffn1'], 'w_ffn2': out['w_ffn2'], 'g_final': out['g_final'], 'loss_target': out['loss_target'], 'm_g_mix': out['m_g_mix'], 'm_w_in': out['m_w_in'], 'm_sgu_ln_w': out['m_sgu_ln_w'], 'm_sgu_ln_b': out['m_sgu_ln_b'], 'm_sgu_w': out['m_sgu_w'], 'm_sgu_b': out['m_sgu_b'], 'm_w_proj_a': out['m_w_proj_a'], 'm_shift_b': out['m_shift_b'], 'm_w_lora_w': out['m_w_lora_w'], 'm_w0': out['m_w0'], 'm_a_lora_w': out['m_a_lora_w'], 'm_a0': out['m_a0'], 'm_g_lora_w': out['m_g_lora_w'], 'm_k_k': out['m_k_k'], 'm_k_a': out['m_k_a'], 'm_r_k': out['m_r_k'], 'm_ln_x_w': out['m_ln_x_w'], 'm_ln_x_b': out['m_ln_x_b'], 'm_w_proj_b': out['m_w_proj_b'], 'm_w_out': out['m_w_out'], 'm_g_ffn': out['m_g_ffn'], 'm_w_ffn1': out['m_w_ffn1'], 'm_w_ffn2': out['m_w_ffn2'], 'm_g_final': out['m_g_final'], 'v_g_mix': out['v_g_mix'], 'v_w_in': out['v_w_in'], 'v_sgu_ln_w': out['v_sgu_ln_w'], 'v_sgu_ln_b': out['v_sgu_ln_b'], 'v_sgu_w': out['v_sgu_w'], 'v_sgu_b': out['v_sgu_b'], 'v_w_proj_a': out['v_w_proj_a'], 'v_shift_b': out['v_shift_b'], 'v_w_lora_w': out['v_w_lora_w'], 'v_w0': out['v_w0'], 'v_a_lora_w': out['v_a_lora_w'], 'v_a0': out['v_a0'], 'v_g_lora_w': out['v_g_lora_w'], 'v_k_k': out['v_k_k'], 'v_k_a': out['v_k_a'], 'v_r_k': out['v_r_k'], 'v_ln_x_w': out['v_ln_x_w'], 'v_ln_x_b': out['v_ln_x_b'], 'v_w_proj_b': out['v_w_proj_b'], 'v_w_out': out['v_w_out'], 'v_g_ffn': out['v_g_ffn'], 'v_w_ffn1': out['v_w_ffn1'], 'v_w_ffn2': out['v_w_ffn2'], 'v_g_final': out['v_g_final']}


def _loss(weights, diff, rest, loss_target):
    with _jax.named_scope("forward"):
        args = {**rest, TWIN_DIFF_INPUT: diff, **{k: w.astype(_WEIGHT_DTYPES[k]) for k, w in weights.items()}}
        y = _forward(args)
    with _jax.named_scope("loss_head"):
        err = _jnp.square(y.astype(_jnp.float32) - loss_target)
        return 0.5 * _jnp.sum(_jnp.mean(err, axis=-1)) if err.ndim else 0.5 * err


def _adamw(w, g, m, v):
    m = ADAM_B1 * m + (1.0 - ADAM_B1) * g
    v = ADAM_B2 * v + (1.0 - ADAM_B2) * _jnp.square(g)
    m_hat = m / (1.0 - ADAM_B1 ** ADAM_STEP)
    v_hat = v / (1.0 - ADAM_B2 ** ADAM_STEP)
    delta = -ADAM_LR * (m_hat / (_jnp.sqrt(v_hat) + ADAM_EPS) + ADAM_WD * w)
    return delta, m, v


def reference(x, g_mix, w_in, sgu_ln_w, sgu_ln_b, sgu_w, sgu_b, w_proj_a, shift_b, w_lora_w, w0, a_lora_w, a0, g_lora_w, k_k, k_a, r_k, ln_x_w, ln_x_b, w_proj_b, w_out, g_ffn, w_ffn1, w_ffn2, g_final, loss_target, m_g_mix, m_w_in, m_sgu_ln_w, m_sgu_ln_b, m_sgu_w, m_sgu_b, m_w_proj_a, m_shift_b, m_w_lora_w, m_w0, m_a_lora_w, m_a0, m_g_lora_w, m_k_k, m_k_a, m_r_k, m_ln_x_w, m_ln_x_b, m_w_proj_b, m_w_out, m_g_ffn, m_w_ffn1, m_w_ffn2, m_g_final, v_g_mix, v_w_in, v_sgu_ln_w, v_sgu_ln_b, v_sgu_w, v_sgu_b, v_w_proj_a, v_shift_b, v_w_lora_w, v_w0, v_a_lora_w, v_a0, v_g_lora_w, v_k_k, v_k_a, v_r_k, v_ln_x_w, v_ln_x_b, v_w_proj_b, v_w_out, v_g_ffn, v_w_ffn1, v_w_ffn2, v_g_final):
    given = dict(x=x, g_mix=g_mix, w_in=w_in, sgu_ln_w=sgu_ln_w, sgu_ln_b=sgu_ln_b, sgu_w=sgu_w, sgu_b=sgu_b, w_proj_a=w_proj_a, shift_b=shift_b, w_lora_w=w_lora_w, w0=w0, a_lora_w=a_lora_w, a0=a0, g_lora_w=g_lora_w, k_k=k_k, k_a=k_a, r_k=r_k, ln_x_w=ln_x_w, ln_x_b=ln_x_b, w_proj_b=w_proj_b, w_out=w_out, g_ffn=g_ffn, w_ffn1=w_ffn1, w_ffn2=w_ffn2, g_final=g_final, loss_target=loss_target, m_g_mix=m_g_mix, m_w_in=m_w_in, m_sgu_ln_w=m_sgu_ln_w, m_sgu_ln_b=m_sgu_ln_b, m_sgu_w=m_sgu_w, m_sgu_b=m_sgu_b, m_w_proj_a=m_w_proj_a, m_shift_b=m_shift_b, m_w_lora_w=m_w_lora_w, m_w0=m_w0, m_a_lora_w=m_a_lora_w, m_a0=m_a0, m_g_lora_w=m_g_lora_w, m_k_k=m_k_k, m_k_a=m_k_a, m_r_k=m_r_k, m_ln_x_w=m_ln_x_w, m_ln_x_b=m_ln_x_b, m_w_proj_b=m_w_proj_b, m_w_out=m_w_out, m_g_ffn=m_g_ffn, m_w_ffn1=m_w_ffn1, m_w_ffn2=m_w_ffn2, m_g_final=m_g_final, v_g_mix=v_g_mix, v_w_in=v_w_in, v_sgu_ln_w=v_sgu_ln_w, v_sgu_ln_b=v_sgu_ln_b, v_sgu_w=v_sgu_w, v_sgu_b=v_sgu_b, v_w_proj_a=v_w_proj_a, v_shift_b=v_shift_b, v_w_lora_w=v_w_lora_w, v_w0=v_w0, v_a_lora_w=v_a_lora_w, v_a0=v_a0, v_g_lora_w=v_g_lora_w, v_k_k=v_k_k, v_k_a=v_k_a, v_r_k=v_r_k, v_ln_x_w=v_ln_x_w, v_ln_x_b=v_ln_x_b, v_w_proj_b=v_w_proj_b, v_w_out=v_w_out, v_g_ffn=v_g_ffn, v_w_ffn1=v_w_ffn1, v_w_ffn2=v_w_ffn2, v_g_final=v_g_final)
    weights = {n: given[n] for n in TWIN_WEIGHTS}
    shared = {n: given[n] for n in SHARED_INPUTS}
    per_example = {n: given[n] for n in ['x']}
    grad_fn = _jax.value_and_grad(_loss, argnums=(0, 1))

    def one_microbatch(ex, loss_target):
        ex = dict(ex)
        diff = ex.pop(TWIN_DIFF_INPUT)
        return grad_fn(weights, diff, {**shared, **ex}, loss_target)

    if N_MICROBATCH == 1:
        loss, (grad_w, grad_x) = one_microbatch(per_example, given["loss_target"])
    else:
        def body(carry, xs):
            loss_sum, grad_sum = carry
            l_k, (gw_k, gx_k) = one_microbatch(xs[0], xs[1])
            with _jax.named_scope("update"):
                return (loss_sum + l_k, _jax.tree.map(_jnp.add, grad_sum, gw_k)), gx_k

        init = (_jnp.zeros((), _jnp.float32), _jax.tree.map(_jnp.zeros_like, weights))
        (loss, grad_w), grad_x = _jax.lax.scan(body, init, (per_example, given["loss_target"]))
    with _jax.named_scope("update"):
        delta_w, new_m, new_v = {}, {}, {}
        for n in TWIN_WEIGHTS:
            delta_w[n], new_m[n], new_v[n] = _adamw(weights[n], grad_w[n], given["m_" + n], given["v_" + n])
    return (loss, grad_x, *[grad_w[n] for n in TWIN_WEIGHTS], *[delta_w[n] for n in TWIN_WEIGHTS],
            *[new_m[n] for n in TWIN_WEIGHTS], *[new_v[n] for n in TWIN_WEIGHTS])
```

```python
import functools
import numpy as np
import jax
import jax.numpy as jnp
from jax import lax
from jax.experimental import pallas as pl
from jax.experimental.pallas import tpu as pltpu

F32 = jnp.float32
BF16 = jnp.bfloat16
HI = lax.Precision.HIGHEST

D = 1024
NH, HN = 16, 64
SGU_G, SGU_C = 8, 128
L_W, L_A, L_G = 64, 64, 160
C_B = 3 * D + L_W + L_A + L_G
P_TOTAL = 2 * D + C_B + 2 * D
D_FF = 4 * D
RW_INT = 3 * D + 128 + 128 + 256
NORM_EPS, LN_EPS, GN_EPS = 1e-6, 1e-5, 64e-5
N_DEV = 8
LANES = 128
SCAN_C = 16
VMEM_LIMIT = 56 * 1024 * 1024

ADAM_LR, ADAM_B1, ADAM_B2, ADAM_EPS, ADAM_WD, ADAM_STEP = 0.001, 0.9, 0.999, 1e-08, 0.01, 10

SHARDED = [
    ("w_in", (D, P_TOTAL), 1), ("w_proj_a", (D, D), 0), ("shift_b", (2, C_B), 1), ("w_lora_w", (L_W, D), 1),
    ("a_lora_w", (L_A, D), 1), ("g_lora_w", (L_G, D), 1), ("w_proj_b", (D, D), 0), ("w_out", (D, D), 0),
    ("w_ffn1", (D, D_FF), 1), ("w_ffn2", (D_FF, D), 0),
]
REPLICATED = [
    ("g_mix", (1, D)), ("sgu_ln_w", (1, D)), ("sgu_ln_b", (1, D)), ("sgu_w", (1, SGU_G, SGU_C, SGU_C)),
    ("sgu_b", (1, SGU_G, SGU_C)), ("w0", (1, D)), ("a0", (1, D)), ("k_k", (1, D)), ("k_a", (1, D)), ("r_k", (1, D)),
    ("ln_x_w", (1, D)), ("ln_x_b", (1, D)), ("g_ffn", (1, D)), ("g_final", (D,)),
]
WEIGHT_ORDER = ["g_mix", "w_in", "sgu_ln_w", "sgu_ln_b", "sgu_w", "sgu_b", "w_proj_a", "shift_b", "w_lora_w", "w0",
                "a_lora_w", "a0", "g_lora_w", "k_k", "k_a", "r_k", "ln_x_w", "ln_x_b", "w_proj_b", "w_out", "g_ffn",
                "w_ffn1", "w_ffn2", "g_final"]


def _shard_shape(shape, axis):
    s = list(shape)
    s[axis] //= N_DEV
    return tuple(s)


def _round_up(n, m):
    return (n + m - 1) // m * m


def _pick(n, target):
    if n <= target:
        return n
    best = None
    for t in range(LANES, target + 1, LANES):
        if n % t == 0:
            best = t
    assert best is not None, (n, target)
    return best


def _matmul(name, a, b, mode, out_dtype=F32, tm=512, tn=1024, tk=512):
    if mode == "nn":
        (M, K), (K2, N) = a.shape, b.shape
    elif mode == "nt":
        (M, K), (N, K2) = a.shape, b.shape
    else:
        (K, M), (K2, N) = a.shape, b.shape
    assert K == K2, (name, a.shape, b.shape)
    tm, tn, tk = _pick(M, tm), _pick(N, tn), _pick(K, tk)
    nk = K // tk
    dims = {"nn": (((1,), (0,)), ((), ())), "nt": (((1,), (1,)), ((), ())), "tn": (((0,), (0,)), ((), ()))}[mode]

    def body(a_ref, b_ref, o_ref, acc_ref):
        k = pl.program_id(2)

        @pl.when(k == 0)
        def _():
            acc_ref[...] = jnp.zeros_like(acc_ref)

        acc_ref[...] += lax.dot_general(a_ref[...].astype(BF16), b_ref[...].astype(BF16), dims,
                                        preferred_element_type=F32)

        @pl.when(k == nk - 1)
        def _():
            o_ref[...] = acc_ref[...].astype(o_ref.dtype)

    a_spec = {"nn": pl.BlockSpec((tm, tk), lambda i, j, k: (i, k)), "nt": pl.BlockSpec((tm, tk), lambda i, j, k: (i, k)),
              "tn": pl.BlockSpec((tk, tm), lambda i, j, k: (k, i))}[mode]
    b_spec = {"nn": pl.BlockSpec((tk, tn), lambda i, j, k: (k, j)), "nt": pl.BlockSpec((tn, tk), lambda i, j, k: (j, k)),
              "tn": pl.BlockSpec((tk, tn), lambda i, j, k: (k, j))}[mode]
    return pl.pallas_call(
        body, name=name, grid=(M // tm, N // tn, nk), in_specs=[a_spec, b_spec],
        out_specs=pl.BlockSpec((tm, tn), lambda i, j, k: (i, j)),
        out_shape=jax.ShapeDtypeStruct((M, N), out_dtype), scratch_shapes=[pltpu.VMEM((tm, tn), F32)],
        compiler_params=pltpu.CompilerParams(dimension_semantics=("parallel", "parallel", "arbitrary"),
                                             vmem_limit_bytes=VMEM_LIMIT),
    )(a, b)


class Rows:
    def __init__(self, arr, width=None, cb=0):
        self.arr, self.width, self.cb = arr, (arr.shape[1] if width is None else width), cb


class Heads:
    def __init__(self, arr):
        self.arr = arr


class Halo:
    def __init__(self, arr, side):
        self.arr, self.side = arr, side


def _rows_call(name, fn, ins, consts, outs, accs=(), tm=256, with_pid=False):
    T = next(o.arr.shape[1] if isinstance(o, Heads) else o.arr.shape[0] for o in ins if not isinstance(o, Halo))
    tm = min(tm, T)
    n_tiles = T // tm
    n_in, n_c, n_out = len(ins), len(consts), len(outs)
    in_specs, args = [], []
    for o in ins:
        if isinstance(o, Rows):
            in_specs.append(pl.BlockSpec((tm, o.width), lambda i, cb=o.cb: (i, cb)))
        elif isinstance(o, Heads):
            in_specs.append(pl.BlockSpec((NH, tm, HN), lambda i: (0, i, 0)))
        else:
            w = o.arr.shape[1]
            if o.side < 0:
                in_specs.append(pl.BlockSpec((8, w), lambda i: (jnp.maximum(i * (tm // 8) - 1, 0), 0)))
            else:
                in_specs.append(pl.BlockSpec((8, w), lambda i: (jnp.minimum((i + 1) * (tm // 8), T // 8 - 1), 0)))
        args.append(o.arr)
    for c in consts:
        in_specs.append(pl.BlockSpec(c.shape, lambda i, nd=c.ndim: (0,) * nd))
        args.append(c)
    out_specs, out_shape = [], []
    for o in outs:
        if o[0] == "rows":
            out_specs.append(pl.BlockSpec((tm, o[1]), lambda i: (i, 0)))
            out_shape.append(jax.ShapeDtypeStruct((T, o[1]), o[2]))
        else:
            out_specs.append(pl.BlockSpec((NH, tm, HN), lambda i: (0, i, 0)))
            out_shape.append(jax.ShapeDtypeStruct((NH, T, HN), o[1]))
    for shape, dt in accs:
        out_specs.append(pl.BlockSpec(shape, lambda i, nd=len(shape): (0,) * nd))
        out_shape.append(jax.ShapeDtypeStruct(shape, dt))

    def body(*refs):
        i = pl.program_id(0)
        vals = []
        for o, r in zip(ins, refs[:n_in]):
            if isinstance(o, Heads):
                vals.append(jnp.concatenate([r[h] for h in range(NH)], axis=-1))
            else:
                vals.append(r[...])
        vals += [r[...] for r in refs[n_in:n_in + n_c]]
        res = fn(i, n_tiles, *vals) if with_pid else fn(*vals)
        out_refs = refs[n_in + n_c:]
        for o, r, v in zip(outs, out_refs[:n_out], res[:n_out]):
            if o[0] == "rows":
                r[...] = v.astype(r.dtype)
            else:
                for h in range(NH):
                    r[h] = v[:, h * HN:(h + 1) * HN].astype(r.dtype)
        if accs:
            @pl.when(i == 0)
            def _():
                for r in out_refs[n_out:]:
                    r[...] = jnp.zeros_like(r)

            for r, v in zip(out_refs[n_out:], res[n_out:]):
                r[...] += v.astype(r.dtype)

    res = pl.pallas_call(
        body, name=name, grid=(n_tiles,), in_specs=in_specs, out_specs=out_specs, out_shape=out_shape,
        compiler_params=pltpu.CompilerParams(dimension_semantics=("arbitrary",), vmem_limit_bytes=VMEM_LIMIT),
    )(*args)
    return res


def _rms(x, g):
    return x * lax.rsqrt(jnp.mean(x * x, axis=-1, keepdims=True) + NORM_EPS) * g


def _gelu(x):
    return 0.5 * x * (1.0 + lax.erf(x * 0.7071067811865476))


def _sigmoid(x):
    return 1.0 / (1.0 + jnp.exp(-x))


def _bdot(a, b):
    return jnp.dot(a.astype(BF16), b.astype(BF16), preferred_element_type=F32)


def _head_sum(x):
    r = lax.broadcasted_iota(jnp.int32, (LANES, LANES), 0) // HN
    c = lax.broadcasted_iota(jnp.int32, (LANES, LANES), 1) // HN
    ones = (r == c).astype(F32)
    parts = [jnp.dot(x[:, j * LANES:(j + 1) * LANES], ones, precision=HI, preferred_element_type=F32)
             for j in range(D // LANES)]
    return jnp.concatenate(parts, axis=-1)


def _sgu_fn(p, ln_w, ln_b, sw, sbt):
    z = _gelu(p)
    u, v = z[:, :D], z[:, D:]
    mu = jnp.mean(v, axis=-1, keepdims=True)
    var = jnp.mean(jnp.square(v - mu), axis=-1, keepdims=True)
    vn = (v - mu) * lax.rsqrt(var + LN_EPS) * ln_w + ln_b
    ri = lax.broadcasted_iota(jnp.int32, (SGU_C, SGU_C), 0)
    ci = lax.broadcasted_iota(jnp.int32, (SGU_C, SGU_C), 1)
    mask = (ci <= ri).astype(F32)
    dg = D // SGU_G
    parts = []
    for g in range(SGU_G):
        parts.append(_bdot(sw[g] * mask, vn[:, g * dg:(g + 1) * dg]) + sbt[:, g:g + 1])
    return u * jnp.concatenate(parts, axis=-1)


def _pre_fn(qr, qk, qv, qxw, qxa, qxg, wl, w0, al, a0, gl, k_k, k_a):
    w = -jax.nn.softplus(-(w0 + _bdot(jnp.tanh(qxw), wl))) - 0.5
    lw = -jnp.exp(w)
    aa = _sigmoid(a0 + _bdot(qxa, al))
    g = _bdot(_sigmoid(qxg), gl)
    kk = qk * k_k
    kk = kk / jnp.maximum(jnp.sqrt(_head_sum(kk * kk)), 1e-12)
    k2 = qk * (1.0 + (aa - 1.0) * k_a)
    return qr, lw, k2, qv, kk, aa, g


def _post_fn(o, r, k2, v, g, ln_w, ln_b, r_k):
    mu = _head_sum(o) * (1.0 / HN)
    d = o - mu
    var = _head_sum(d * d) * (1.0 / HN)
    on = d * lax.rsqrt(var + GN_EPS) * ln_w + ln_b
    bonus = _head_sum(r * k2 * r_k) * v
    return (on + bonus) * g


def _gate_fn(pg, ya, yb):
    return _sigmoid(pg[:, :D]) * ya + _sigmoid(pg[:, D:]) * yb


def _chunk_fn(S0, r, lw, k, v, kk, a):
    C = SCAN_C

    def bmm(x, y, cx, cy):
        return lax.dot_general(x, y, (((cx,), (cy,)), ((0,), (0,))), precision=HI, preferred_element_type=F32)

    ti = lax.broadcasted_iota(jnp.int32, (C, C), 0)
    tj = lax.broadcasted_iota(jnp.int32, (C, C), 1)
    incl = (tj <= ti).astype(F32)
    strict = (tj < ti).astype(F32)
    cum = bmm(jnp.broadcast_to(incl, (NH, C, C)), lw, 2, 1)
    g_in, g_ex, g_inv = jnp.exp(cum), jnp.exp(cum - lw), jnp.exp(-cum)
    kkt, bt, kt, rt = kk * g_ex, kk * a * g_inv, k * g_inv, r * g_in
    X = -bmm(kkt, bt, 2, 2) * strict
    y = bmm(kkt, S0, 2, 2) + bmm(bmm(kkt, kt, 2, 2) * strict, v, 2, 1)
    pows = [X]
    while 2 ** len(pows) < C:
        pows.append(bmm(pows[-1], pows[-1], 2, 1))
    for Xp in reversed(pows):
        y = y + bmm(Xp, y, 2, 1)
    U = -y
    O = (bmm(rt, S0, 2, 2) + bmm(bmm(rt, bt, 2, 2) * incl, U, 2, 1) + bmm(bmm(rt, kt, 2, 2) * incl, v, 2, 1))
    S1 = (S0 + bmm(U, bt, 1, 1) + bmm(v, kt, 1, 1)) * g_in[:, C - 1:C, :]
    return O, S1


def _scan_fwd(r, lw, k, v, kk, a, tb=256):
    T = r.shape[1]
    tb = min(tb, T)
    n_chunks = tb // SCAN_C

    def body(r_ref, lw_ref, k_ref, v_ref, kk_ref, a_ref, o_ref, s0_ref, s_ref):
        @pl.when(pl.program_id(0) == 0)
        def _():
            s_ref[...] = jnp.zeros_like(s_ref)

        def step(c, carry):
            sl = pl.ds(pl.multiple_of(c * SCAN_C, SCAN_C), SCAN_C)
            S0 = s_ref[...]
            s0_ref[c] = S0
            O, S1 = _chunk_fn(S0, r_ref[:, sl, :], lw_ref[:, sl, :], k_ref[:, sl, :], v_ref[:, sl, :],
                              kk_ref[:, sl, :], a_ref[:, sl, :])
            o_ref[:, sl, :] = O
            s_ref[...] = S1
            return carry

        lax.fori_loop(0, n_chunks, step, 0)

    hm = pl.BlockSpec((NH, tb, HN), lambda i: (0, i, 0))
    return pl.pallas_call(
        body, name="rwkv_scan_fwd", grid=(T // tb,), in_specs=[hm] * 6,
        out_specs=[hm, pl.BlockSpec((n_chunks, NH, HN, HN), lambda i: (i, 0, 0, 0))],
        out_shape=[jax.ShapeDtypeStruct((NH, T, HN), F32), jax.ShapeDtypeStruct((T // SCAN_C, NH, HN, HN), F32)],
        scratch_shapes=[pltpu.VMEM((NH, HN, HN), F32)],
        compiler_params=pltpu.CompilerParams(dimension_semantics=("arbitrary",), vmem_limit_bytes=VMEM_LIMIT),
    )(r, lw, k, v, kk, a)


def _scan_bwd(r, lw, k, v, kk, a, s0s, do, tb=128):
    T = r.shape[1]
    tb = min(tb, T)
    n_chunks = tb // SCAN_C
    nb = T // tb

    def body(r_ref, lw_ref, k_ref, v_ref, kk_ref, a_ref, s0_ref, do_ref, dr, dlw, dk, dv, dkk, da, ds_ref):
        @pl.when(pl.program_id(0) == 0)
        def _():
            ds_ref[...] = jnp.zeros_like(ds_ref)

        def step(j, carry):
            c = n_chunks - 1 - j
            sl = pl.ds(pl.multiple_of(c * SCAN_C, SCAN_C), SCAN_C)
            _, vjp = jax.vjp(_chunk_fn, s0_ref[c], r_ref[:, sl, :], lw_ref[:, sl, :], k_ref[:, sl, :],
                             v_ref[:, sl, :], kk_ref[:, sl, :], a_ref[:, sl, :])
            g = vjp((do_ref[:, sl, :], ds_ref[...]))
            ds_ref[...] = g[0]
            for ref, val in zip((dr, dlw, dk, dv, dkk, da), g[1:]):
                ref[:, sl, :] = val
            return carry

        lax.fori_loop(0, n_chunks, step, 0)

    hm = pl.BlockSpec((NH, tb, HN), lambda i: (0, nb - 1 - i, 0))
    return pl.pallas_call(
        body, name="rwkv_scan_bwd", grid=(nb,),
        in_specs=[hm] * 6 + [pl.BlockSpec((n_chunks, NH, HN, HN), lambda i: (nb - 1 - i, 0, 0, 0)), hm],
        out_specs=[hm] * 6, out_shape=[jax.ShapeDtypeStruct((NH, T, HN), F32)] * 6,
        scratch_shapes=[pltpu.VMEM((NH, HN, HN), F32)],
        compiler_params=pltpu.CompilerParams(dimension_semantics=("arbitrary",), vmem_limit_bytes=VMEM_LIMIT),
    )(r, lw, k, v, kk, a, s0s, do)


def _shift_down(i, p, prev8):
    first = jnp.where(i > 0, prev8[7:8, :], 0.0)
    row = lax.broadcasted_iota(jnp.int32, p.shape, 0)
    return jnp.where(row == 0, first, pltpu.roll(p, 1, axis=0))


def _mix_fwd(p, sb, tm=256):
    def fn(i, n, p, prev8, sb):
        return (p * sb[0:1] + _shift_down(i, p, prev8) * sb[1:2],)
    return _rows_call("shift_mix_fwd", fn, [Rows(p), Halo(p, -1)], [sb], [("rows", p.shape[1], F32)], tm=tm,
                      with_pid=True)[0]


def _mix_bwd(dq, p, sb, tm=256):
    def fn(i, n, dq, next8, p, prev8, sb):
        ps = _shift_down(i, p, prev8)
        d1 = dq * sb[1:2]
        last = jnp.where(i < n - 1, next8[0:1, :] * sb[1:2], 0.0)
        row = lax.broadcasted_iota(jnp.int32, dq.shape, 0)
        up = jnp.where(row == dq.shape[0] - 1, last, pltpu.roll(d1, dq.shape[0] - 1, axis=0))
        return (dq * sb[0:1] + up, jnp.sum(dq * p, axis=0, keepdims=True), jnp.sum(dq * ps, axis=0, keepdims=True))
    w = p.shape[1]
    return _rows_call("shift_mix_bwd", fn, [Rows(dq), Halo(dq, +1), Rows(p), Halo(p, -1)], [sb], [("rows", w, F32)],
                      accs=[((1, w), F32), ((1, w), F32)], tm=tm, with_pid=True)


def _local_step(x, target, W):
    G = {}
    a = _rows_call("norm_mix_fwd", lambda x, g: (_rms(x, g),), [Rows(x)], [W["g_mix"]], [("rows", D, BF16)])[0]
    p_sgu = _matmul("proj_sgu", a, W["w_sgu"], "nn")
    p_rw = _matmul("proj_rwkv", a, W["w_rw"], "nn")
    p_gate = _matmul("proj_gate", a, W["w_gate"], "nn")

    sgu_consts = [W["sgu_ln_w"], W["sgu_ln_b"], W["sgu_w"], W["sgu_bt"]]
    s = _rows_call("sgu_fwd", lambda *t: (_sgu_fn(*t),), [Rows(p_sgu)], sgu_consts, [("rows", D, BF16)], tm=SGU_C)[0]
    y_a = _matmul("proj_a", s, W["w_proj_a"], "nn")

    q = _mix_fwd(p_rw, W["sb"])
    q_ins = [Rows(q, D, 0), Rows(q, D, 1), Rows(q, D, 2), Rows(q, 128, 24), Rows(q, 128, 25), Rows(q, 256, 13)]
    pre_consts = [W["w_lora"], W["w0"], W["a_lora"], W["a0"], W["g_lora"], W["k_k"], W["k_a"]]
    r_h, lw_h, k_h, v_h, kk_h, a_h, g_gate = _rows_call(
        "rwkv_pre_fwd", _pre_fn, q_ins, pre_consts, [("heads", F32)] * 6 + [("rows", D, F32)], tm=128)
    o_h, s0s = _scan_fwd(r_h, lw_h, k_h, v_h, kk_h, a_h)
    post_ins = [Heads(o_h), Heads(r_h), Heads(k_h), Heads(v_h), Rows(g_gate)]
    post_consts = [W["ln_x_w"], W["ln_x_b"], W["r_k"]]
    z_b = _rows_call("rwkv_post_fwd", lambda *t: (_post_fn(*t),), post_ins, post_consts, [("rows", D, BF16)], tm=128)[0]
    y_b = _matmul("proj_b", z_b, W["w_proj_b"], "nn")

    gate_ins = [Rows(p_gate), Rows(y_a), Rows(y_b)]
    mixed = _rows_call("gate_fwd", lambda *t: (_gate_fn(*t),), gate_ins, [], [("rows", D, BF16)])[0]
    mo = _matmul("proj_out", mixed, W["w_out"], "nn")

    def res1(x, mo, g):
        h1 = x + mo
        return h1, _rms(h1, g)
    h1, f = _rows_call("residual_norm_fwd", res1, [Rows(x), Rows(mo)], [W["g_ffn"]],
                       [("rows", D, F32), ("rows", D, BF16)])
    u1 = _matmul("ffn_up", f, W["w_ffn1"], "nn")
    act = _rows_call("ffn_act_fwd", lambda u: (jnp.square(jnp.maximum(u, 0.0)),), [Rows(u1)], [],
                     [("rows", D_FF, BF16)])[0]
    ff = _matmul("ffn_down", act, W["w_ffn2"], "nn")

    def head(h1, ff, tgt, g):
        def f_(h1, ff, g):
            y = _rms(h1 + ff, g)
            return 0.5 * jnp.sum(jnp.mean(jnp.square(y - tgt), axis=-1))
        loss, (dh2, _, dg) = jax.value_and_grad(f_, argnums=(0, 1, 2))(h1, ff, g)
        return dh2, jnp.full((8, LANES), loss, F32), dg
    dh2, loss_acc, G["g_final"] = _rows_call("loss_head", head, [Rows(h1), Rows(ff), Rows(target)], [W["g_final"]],
                                             [("rows", D, F32)], accs=[((8, LANES), F32), ((1, D), F32)])

    d_act = _matmul("ffn_down_dx", dh2, W["w_ffn2"], "nt")
    G["w_ffn2"] = _matmul("ffn_down_dw", act, dh2, "tn")
    d_u1 = _rows_call("ffn_act_bwd", lambda u, d: (d * 2.0 * jnp.maximum(u, 0.0),), [Rows(u1), Rows(d_act)], [],
                      [("rows", D_FF, BF16)])[0]
    d_f = _matmul("ffn_up_dx", d_u1, W["w_ffn1"], "nt")
    G["w_ffn1"] = _matmul("ffn_up_dw", f, d_u1, "tn")

    def res1_bwd(x, mo, d_f, dh2, g):
        _, vjp = jax.vjp(lambda h, g: _rms(h, g), x + mo, g)
        dh, dg = vjp(d_f)
        return dh2 + dh, dg
    dh1, G["g_ffn"] = _rows_call("residual_norm_bwd", res1_bwd, [Rows(x), Rows(mo), Rows(d_f), Rows(dh2)],
                                 [W["g_ffn"]], [("rows", D, F32)], accs=[((1, D), F32)])
    d_mixed = _matmul("proj_out_dx", dh1, W["w_out"], "nt")
    G["w_out"] = _matmul("proj_out_dw", mixed, dh1, "tn")

    def gate_bwd(pg, ya, yb, dm):
        _, vjp = jax.vjp(_gate_fn, pg, ya, yb)
        return vjp(dm)
    d_gate, d_ya, d_yb = _rows_call("gate_bwd", gate_bwd, gate_ins + [Rows(d_mixed)], [],
                                    [("rows", 2 * D, F32), ("rows", D, BF16), ("rows", D, BF16)])

    d_s = _matmul("proj_a_dx", d_ya, W["w_proj_a"], "nt")
    G["w_proj_a"] = _matmul("proj_a_dw", s, d_ya, "tn")

    def sgu_bwd(p, ds, *c):
        _, vjp = jax.vjp(_sgu_fn, p, *c)
        return vjp(ds)
    d_p_sgu, G["sgu_ln_w"], G["sgu_ln_b"], G["sgu_w"], G["sgu_bt"] = _rows_call(
        "sgu_bwd", sgu_bwd, [Rows(p_sgu), Rows(d_s)], sgu_consts, [("rows", 2 * D, F32)],
        accs=[((1, D), F32), ((1, D), F32), ((SGU_G, SGU_C, SGU_C), F32), ((SGU_C, SGU_G), F32)], tm=SGU_C)

    d_zb = _matmul("proj_b_dx", d_yb, W["w_proj_b"], "nt")
    G["w_proj_b"] = _matmul("proj_b_dw", z_b, d_yb, "tn")

    def post_bwd(o, r, k2, v, g, dz, *c):
        _, vjp = jax.vjp(_post_fn, o, r, k2, v, g, *c)
        return vjp(dz)
    do_h, dr1, dk1, dv1, d_g, G["ln_x_w"], G["ln_x_b"], G["r_k"] = _rows_call(
        "rwkv_post_bwd", post_bwd, post_ins + [Rows(d_zb)], post_consts, [("heads", F32)] * 4 + [("rows", D, F32)],
        accs=[((1, D), F32)] * 3, tm=128)
    dr2, dlw, dk2, dv2, dkk, daa = _scan_bwd(r_h, lw_h, k_h, v_h, kk_h, a_h, s0s, do_h)

    def pre_bwd(qr, qk, qv, qxw, qxa, qxg, dr1, dr2, dlw, dk1, dk2, dv1, dv2, dkk, daa, dg, *c):
        _, vjp = jax.vjp(_pre_fn, qr, qk, qv, qxw, qxa, qxg, *c)
        g = vjp((dr1 + dr2, dlw, dk1 + dk2, dv1 + dv2, dkk, daa, dg))
        dq = jnp.concatenate(g[:6], axis=-1)
        return (dq,) + tuple(g[6:])
    pre_b_ins = q_ins + [Heads(dr1), Heads(dr2), Heads(dlw), Heads(dk1), Heads(dk2), Heads(dv1), Heads(dv2),
                         Heads(dkk), Heads(daa), Rows(d_g)]
    d_q, G["w_lora"], G["w0"], G["a_lora"], G["a0"], G["g_lora"], G["k_k"], G["k_a"] = _rows_call(
        "rwkv_pre_bwd", pre_bwd, pre_b_ins, pre_consts, [("rows", RW_INT, F32)],
        accs=[((128, D), F32), ((1, D), F32), ((128, D), F32), ((1, D), F32), ((256, D), F32), ((1, D), F32),
              ((1, D), F32)], tm=128)
    d_p_rw, dsb0, dsb1 = _mix_bwd(d_q, p_rw, W["sb"])
    G["sb"] = jnp.concatenate([dsb0, dsb1], axis=0)

    G["w_sgu"] = _matmul("proj_sgu_dw", a, d_p_sgu, "tn")
    G["w_rw"] = _matmul("proj_rwkv_dw", a, d_p_rw, "tn")
    G["w_gate"] = _matmul("proj_gate_dw", a, d_gate, "tn")
    da1 = _matmul("proj_sgu_dx", d_p_sgu, W["w_sgu"], "nt")
    da2 = _matmul("proj_rwkv_dx", d_p_rw, W["w_rw"], "nt")
    da3 = _matmul("proj_gate_dx", d_gate, W["w_gate"], "nt")

    def norm1_bwd(x, da1, da2, da3, dh1, g):
        _, vjp = jax.vjp(_rms, x, g)
        dx, dg = vjp(da1 + da2 + da3)
        return dh1 + dx, dg
    dx, G["g_mix"] = _rows_call("norm_mix_bwd", norm1_bwd, [Rows(x), Rows(da1), Rows(da2), Rows(da3), Rows(dh1)],
                                [W["g_mix"]], [("rows", D, F32)], accs=[((1, D), F32)])
    return loss_acc[0, 0], dx, G


def _exchange(name, bufs, gather):
    nb = len(bufs)
    n_peers = N_DEV - 1

    def body(*refs):
        in_refs, out_refs = refs[:nb], refs[nb:2 * nb]
        send_sems, recv_sems, local_sems = refs[2 * nb:]
        x, y, c = lax.axis_index("x"), lax.axis_index("y"), lax.axis_index("c")
        me = 4 * x + 2 * y + c

        def src(b, dest):
            return in_refs[b] if gather else in_refs[b].at[dest]

        local = [pltpu.make_async_copy(src(b, me), out_refs[b].at[me], local_sems.at[b]) for b in range(nb)]
        for cp in local:
            cp.start()
        sends, recvs = [], []
        for kbits in range(1, N_DEV):
            px = 1 - x if kbits & 4 else x
            py = 1 - y if kbits & 2 else y
            pc = 1 - c if kbits & 1 else c
            peer = 4 * px + 2 * py + pc
            for b in range(nb):
                s = (kbits - 1) * nb + b
                sends.append(pltpu.make_async_remote_copy(
                    src_ref=src(b, peer), dst_ref=out_refs[b].at[me], send_sem=send_sems.at[s],
                    recv_sem=recv_sems.at[s], device_id=(px, py, pc), device_id_type=pl.DeviceIdType.MESH))
                recvs.append(pltpu.make_async_remote_copy(
                    src_ref=src(b, peer), dst_ref=out_refs[b].at[peer], send_sem=send_sems.at[s],
                    recv_sem=recv_sems.at[s], device_id=(px, py, pc), device_id_type=pl.DeviceIdType.MESH))
        for cp in sends:
            cp.start()
        for cp in recvs:
            cp.wait_recv()
        for cp in sends:
            cp.wait_send()
        for cp in local:
            cp.wait()

    out_shape = [jax.ShapeDtypeStruct((N_DEV,) + (b.shape if gather else b.shape[1:]), b.dtype) for b in bufs]
    any_spec = pl.BlockSpec(memory_space=pl.ANY)
    return pl.pallas_call(
        body, name=name, in_specs=[any_spec] * nb, out_specs=[any_spec] * nb, out_shape=out_shape,
        scratch_shapes=[pltpu.SemaphoreType.DMA((n_peers * nb,)), pltpu.SemaphoreType.DMA((n_peers * nb,)),
                        pltpu.SemaphoreType.DMA((nb,))],
    )(*bufs)


def _adamw(name, slots, w, m, v, tr):
    R, Wd = w.shape

    def body(s_ref, w_ref, m_ref, v_ref, g_out, d_out, m_out, v_out):
        g = s_ref[0]
        for j in range(1, N_DEV):
            g = g + s_ref[j]
        m_new = ADAM_B1 * m_ref[...] + (1.0 - ADAM_B1) * g
        v_new = ADAM_B2 * v_ref[...] + (1.0 - ADAM_B2) * jnp.square(g)
        m_hat = m_new / (1.0 - ADAM_B1 ** ADAM_STEP)
        v_hat = v_new / (1.0 - ADAM_B2 ** ADAM_STEP)
        g_out[...] = g
        d_out[...] = -ADAM_LR * (m_hat / (jnp.sqrt(v_hat) + ADAM_EPS) + ADAM_WD * w_ref[...])
        m_out[...] = m_new
        v_out[...] = v_new

    row = pl.BlockSpec((tr, Wd), lambda i: (i, 0))
    return pl.pallas_call(
        body, name=name, grid=(R // tr,), in_specs=[pl.BlockSpec((N_DEV, tr, Wd), lambda i: (0, i, 0)), row, row, row],
        out_specs=[row] * 4, out_shape=[jax.ShapeDtypeStruct((R, Wd), F32)] * 4,
        compiler_params=pltpu.CompilerParams(dimension_semantics=("parallel",), vmem_limit_bytes=VMEM_LIMIT),
    )(slots, w, m, v)


PACK_W = 1024
BIG_TR = 128
_BIG_SIZES = [int(np.prod(_shard_shape(s, ax))) for _, s, ax in SHARDED]
_BIG_ROWS = _round_up(_round_up(sum(_BIG_SIZES), PACK_W) // PACK_W, BIG_TR)
_SMALL_SIZES = [int(np.prod(s)) for _, s in REPLICATED]
_SMALL_ROWS = _round_up(_round_up(sum(_SMALL_SIZES) + PACK_W, PACK_W) // PACK_W, 8)
_LOSS_AT = sum(_SMALL_SIZES)


def _pack_rows(parts, rows, dtype):
    flat = jnp.concatenate([p.reshape(-1).astype(dtype) for p in parts])
    return jnp.pad(flat, (0, rows * PACK_W - flat.shape[0])).reshape(rows, PACK_W)


def _split3(x):
    hi = x.astype(BF16)
    r1 = x - hi.astype(F32)
    mid = r1.astype(BF16)
    lo = (r1 - mid.astype(F32)).astype(BF16)
    return hi, mid, lo


def _gather_weights(shards):
    parts = []
    for name, _, _ in SHARDED:
        if name == "shift_b":
            parts += list(_split3(shards[name]))
        else:
            parts.append(shards[name])
    n = sum(int(np.prod(p.shape)) for p in parts)
    rows = _round_up(_round_up(n, PACK_W) // PACK_W, 16)
    got = _exchange("weight_all_gather", [_pack_rows(parts, rows, BF16)], gather=True)[0].reshape(N_DEV, -1)
    full, off = {}, 0

    def take(shape, axis):
        nonlocal off
        ss = _shard_shape(shape, axis)
        n = int(np.prod(ss))
        blk = got[:, off:off + n].reshape((N_DEV,) + ss)
        off += n
        if axis == 0:
            return blk.reshape(shape)
        return jnp.transpose(blk, (1, 0, 2)).reshape(shape)

    for name, shape, axis in SHARDED:
        if name == "shift_b":
            full[name] = sum(take(shape, axis).astype(F32) for _ in range(3))
        else:
            full[name] = take(shape, axis)
    return full


def _internal_weights(full, rep):
    w_in = full["w_in"]
    z = lambda r, c, dt=BF16: jnp.zeros((r, c), dt)
    o = 2 * D
    W = {
        "w_sgu": w_in[:, :o],
        "w_rw": jnp.concatenate([w_in[:, o:o + 3 * D], w_in[:, o + 3 * D:o + 3 * D + L_W], z(D, 128 - L_W),
                                 w_in[:, o + 3 * D + L_W:o + 3 * D + L_W + L_A], z(D, 128 - L_A),
                                 w_in[:, o + 3 * D + L_W + L_A:o + C_B], z(D, 256 - L_G)], axis=1),
        "w_gate": w_in[:, o + C_B:],
        "w_lora": jnp.concatenate([full["w_lora_w"], z(128 - L_W, D)], axis=0).astype(F32),
        "a_lora": jnp.concatenate([full["a_lora_w"], z(128 - L_A, D)], axis=0).astype(F32),
        "g_lora": jnp.concatenate([full["g_lora_w"], z(256 - L_G, D)], axis=0).astype(F32),
    }
    sb = full["shift_b"]
    W["sb"] = jnp.concatenate([sb[:, :3 * D], sb[:, 3 * D:3 * D + L_W], z(2, 128 - L_W, F32),
                               sb[:, 3 * D + L_W:3 * D + L_W + L_A], z(2, 128 - L_A, F32),
                               sb[:, 3 * D + L_W + L_A:], z(2, 256 - L_G, F32)], axis=1)
    for n in ("w_proj_a", "w_proj_b", "w_out", "w_ffn1", "w_ffn2"):
        W[n] = full[n]
    for n in ("g_mix", "sgu_ln_w", "sgu_ln_b", "w0", "a0", "k_k", "k_a", "r_k", "ln_x_w", "ln_x_b", "g_ffn"):
        W[n] = rep[n]
    W["g_final"] = rep["g_final"].reshape(1, D)
    W["sgu_w"] = rep["sgu_w"][0]
    W["sgu_bt"] = jnp.transpose(rep["sgu_b"][0])
    return W


def _external_grads(G):
    rw = G["w_rw"]
    sbg = G["sb"]
    cut = lambda t: jnp.concatenate([t[:, :3 * D], t[:, 3 * D:3 * D + L_W], t[:, 3 * D + 128:3 * D + 128 + L_A],
                                     t[:, 3 * D + 256:3 * D + 256 + L_G]], axis=1)
    big = {
        "w_in": jnp.concatenate([G["w_sgu"], cut(rw), G["w_gate"]], axis=1),
        "shift_b": cut(sbg),
        "w_lora_w": G["w_lora"][:L_W], "a_lora_w": G["a_lora"][:L_A], "g_lora_w": G["g_lora"][:L_G],
    }
    for n in ("w_proj_a", "w_proj_b", "w_out", "w_ffn1", "w_ffn2"):
        big[n] = G[n]
    small = {n: G[n] for n in ("g_mix", "sgu_ln_w", "sgu_ln_b", "w0", "a0", "k_k", "k_a", "r_k", "ln_x_w", "ln_x_b",
                               "g_ffn", "g_final")}
    small["sgu_w"] = G["sgu_w"]
    small["sgu_b"] = jnp.transpose(G["sgu_bt"])
    return big, small


def _to_blocks(g, axis):
    if axis == 0:
        return g.reshape(N_DEV, -1)
    r, c = g.shape
    return jnp.transpose(g.reshape(r, N_DEV, c // N_DEV), (1, 0, 2)).reshape(N_DEV, -1)


def kernel(x, g_mix, w_in, sgu_ln_w, sgu_ln_b, sgu_w, sgu_b, w_proj_a, shift_b, w_lora_w, w0, a_lora_w, a0, g_lora_w, k_k, k_a, r_k, ln_x_w, ln_x_b, w_proj_b, w_out, g_ffn, w_ffn1, w_ffn2, g_final, loss_target, m_g_mix, m_w_in, m_sgu_ln_w, m_sgu_ln_b, m_sgu_w, m_sgu_b, m_w_proj_a, m_shift_b, m_w_lora_w, m_w0, m_a_lora_w, m_a0, m_g_lora_w, m_k_k, m_k_a, m_r_k, m_ln_x_w, m_ln_x_b, m_w_proj_b, m_w_out, m_g_ffn, m_w_ffn1, m_w_ffn2, m_g_final, v_g_mix, v_w_in, v_sgu_ln_w, v_sgu_ln_b, v_sgu_w, v_sgu_b, v_w_proj_a, v_shift_b, v_w_lora_w, v_w0, v_a_lora_w, v_a0, v_g_lora_w, v_k_k, v_k_a, v_r_k, v_ln_x_w, v_ln_x_b, v_w_proj_b, v_w_out, v_g_ffn, v_w_ffn1, v_w_ffn2, v_g_final):
    env = dict(locals())
    weights = {n: env[n] for n in WEIGHT_ORDER}
    moms = {n: env["m_" + n] for n in WEIGHT_ORDER}
    vars_ = {n: env["v_" + n] for n in WEIGHT_ORDER}

    shards = {n: weights[n][0] for n, _, _ in SHARDED}
    full = _gather_weights(shards)
    W = _internal_weights(full, {n: weights[n] for n, _ in REPLICATED})
    loss_part, dx, G = _local_step(x[0], loss_target[0], W)
    big, small = _external_grads(G)

    blocks = jnp.concatenate([_to_blocks(big[n], ax) for n, _, ax in SHARDED], axis=1)
    blocks = jnp.pad(blocks, ((0, 0), (0, _BIG_ROWS * PACK_W - blocks.shape[1]))).reshape(N_DEV, _BIG_ROWS, PACK_W)
    small_parts = [small[n] for n, _ in REPLICATED] + [jnp.full((PACK_W,), loss_part, F32)]
    big_slots = _exchange("grad_reduce_scatter", [blocks], gather=False)[0]
    small_slots = _exchange("grad_all_gather", [_pack_rows(small_parts, _SMALL_ROWS, F32)], gather=True)[0]

    def packed(d, names_shapes, rows, sharded):
        return _pack_rows([(d[n][0] if sharded else d[n]) for n, *_ in names_shapes], rows, F32)
    big_out = _adamw("adamw_sharded", big_slots, packed(weights, SHARDED, _BIG_ROWS, True),
                     packed(moms, SHARDED, _BIG_ROWS, True), packed(vars_, SHARDED, _BIG_ROWS, True), BIG_TR)
    small_out = _adamw("adamw_replicated", small_slots, packed(weights, REPLICATED, _SMALL_ROWS, False),
                       packed(moms, REPLICATED, _SMALL_ROWS, False), packed(vars_, REPLICATED, _SMALL_ROWS, False),
                       _SMALL_ROWS)

    outs = [dict(), dict(), dict(), dict()]
    for k in range(4):
        flat = big_out[k].reshape(-1)
        off = 0
        for (n, s, ax), size in zip(SHARDED, _BIG_SIZES):
            outs[k][n] = flat[off:off + size].reshape((1,) + _shard_shape(s, ax))
            off += size
        flat = small_out[k].reshape(-1)
        off = 0
        for (n, s), size in zip(REPLICATED, _SMALL_SIZES):
            outs[k][n] = flat[off:off + size].reshape(s)
            off += size
    loss = small_out[0].reshape(-1)[_LOSS_AT]
    return (loss, dx[None], *[outs[0][n] for n in WEIGHT_ORDER], *[outs[1][n] for n in WEIGHT_ORDER],
            *[outs[2][n] for n in WEIGHT_ORDER], *[outs[3][n] for n in WEIGHT_ORDER])
```

```python
import functools
import numpy as np
import jax
import jax.numpy as jnp
from jax import lax
from jax.experimental import pallas as pl
from jax.experimental.pallas import tpu as pltpu

F32 = jnp.float32
BF16 = jnp.bfloat16
HI = lax.Precision.HIGHEST

D = 1024
NH, HN = 16, 64
SGU_G, SGU_C = 8, 128
L_W, L_A, L_G = 64, 64, 160
C_B = 3 * D + L_W + L_A + L_G
P_TOTAL = 2 * D + C_B + 2 * D
D_FF = 4 * D
RW_INT = 3 * D + 128 + 128 + 256
NORM_EPS, LN_EPS, GN_EPS = 1e-6, 1e-5, 64e-5
N_DEV = 8
LANES = 128
SCAN_C = 16
VMEM_LIMIT = 56 * 1024 * 1024

ADAM_LR, ADAM_B1, ADAM_B2, ADAM_EPS, ADAM_WD, ADAM_STEP = 0.001, 0.9, 0.999, 1e-08, 0.01, 10

SHARDED = [
    ("w_in", (D, P_TOTAL), 1), ("w_proj_a", (D, D), 0), ("shift_b", (2, C_B), 1), ("w_lora_w", (L_W, D), 1),
    ("a_lora_w", (L_A, D), 1), ("g_lora_w", (L_G, D), 1), ("w_proj_b", (D, D), 0), ("w_out", (D, D), 0),
    ("w_ffn1", (D, D_FF), 1), ("w_ffn2", (D_FF, D), 0),
]
REPLICATED = [
    ("g_mix", (1, D)), ("sgu_ln_w", (1, D)), ("sgu_ln_b", (1, D)), ("sgu_w", (1, SGU_G, SGU_C, SGU_C)),
    ("sgu_b", (1, SGU_G, SGU_C)), ("w0", (1, D)), ("a0", (1, D)), ("k_k", (1, D)), ("k_a", (1, D)), ("r_k", (1, D)),
    ("ln_x_w", (1, D)), ("ln_x_b", (1, D)), ("g_ffn", (1, D)), ("g_final", (D,)),
]
WEIGHT_ORDER = ["g_mix", "w_in", "sgu_ln_w", "sgu_ln_b", "sgu_w", "sgu_b", "w_proj_a", "shift_b", "w_lora_w", "w0",
                "a_lora_w", "a0", "g_lora_w", "k_k", "k_a", "r_k", "ln_x_w", "ln_x_b", "w_proj_b", "w_out", "g_ffn",
                "w_ffn1", "w_ffn2", "g_final"]


def _shard_shape(shape, axis):
    s = list(shape)
    s[axis] //= N_DEV
    return tuple(s)


def _round_up(n, m):
    return (n + m - 1) // m * m


def _pick(n, target):
    if n <= target:
        return n
    best = None
    for t in range(LANES, target + 1, LANES):
        if n % t == 0:
            best = t
    assert best is not None, (n, target)
    return best


def _matmul(name, a, b, mode, out_dtype=F32, tm=1024, tn=1024, tk=1024, out_blocks=None):
    b_blocks = b.shape[0] if b.ndim == 3 else None
    bshape = b.shape if b.ndim == 2 else (b.shape[1], b.shape[0] * b.shape[2])
    if mode == "nn":
        (M, K), (K2, N) = a.shape, bshape
    elif mode == "nt":
        (M, K), (N, K2) = a.shape, bshape
    else:
        (K, M), (K2, N) = a.shape, bshape
    assert K == K2, (name, a.shape, b.shape)
    assert b_blocks is None or mode != "tn"
    assert out_blocks is None or mode == "tn"
    tn = min(tn, N // (out_blocks or 1), bshape[1] // b_blocks if (b_blocks and mode == "nn") else tn)
    tk = min(tk, bshape[1] // b_blocks if (b_blocks and mode == "nt") else tk)
    tm, tn, tk = _pick(M, tm), _pick(N, tn), _pick(K, tk)
    nk = K // tk
    dims = {"nn": (((1,), (0,)), ((), ())), "nt": (((1,), (1,)), ((), ())), "tn": (((0,), (0,)), ((), ()))}[mode]

    def body(a_ref, b_ref, o_ref, acc_ref):
        k = pl.program_id(2)

        @pl.when(k == 0)
        def _():
            acc_ref[...] = jnp.zeros_like(acc_ref)

        acc_ref[...] += lax.dot_general(a_ref[...].astype(BF16), b_ref[...].astype(BF16), dims,
                                        preferred_element_type=F32)

        @pl.when(k == nk - 1)
        def _():
            o_ref[...] = acc_ref[...].astype(o_ref.dtype)

    a_spec = {"nn": pl.BlockSpec((tm, tk), lambda i, j, k: (i, k)), "nt": pl.BlockSpec((tm, tk), lambda i, j, k: (i, k)),
              "tn": pl.BlockSpec((tk, tm), lambda i, j, k: (k, i))}[mode]
    b_spec = {"nn": pl.BlockSpec((tk, tn), lambda i, j, k: (k, j)), "nt": pl.BlockSpec((tn, tk), lambda i, j, k: (j, k)),
              "tn": pl.BlockSpec((tk, tn), lambda i, j, k: (k, j))}[mode]
    if b_blocks and mode == "nn":
        per = b.shape[2] // tn
        b_spec = pl.BlockSpec((None, tk, tn), lambda i, j, k: (j // per, k, j % per))
    elif b_blocks:
        per = b.shape[2] // tk
        b_spec = pl.BlockSpec((None, tn, tk), lambda i, j, k: (k // per, j, k % per))
    out_spec = pl.BlockSpec((tm, tn), lambda i, j, k: (i, j))
    out_shape = jax.ShapeDtypeStruct((M, N), out_dtype)
    if out_blocks:
        per_o = N // out_blocks // tn
        out_spec = pl.BlockSpec((None, tm, tn), lambda i, j, k: (j // per_o, i, j % per_o))
        out_shape = jax.ShapeDtypeStruct((out_blocks, M, N // out_blocks), out_dtype)
    return pl.pallas_call(
        body, name=name, grid=(M // tm, N // tn, nk), in_specs=[a_spec, b_spec],
        out_specs=out_spec, out_shape=out_shape, scratch_shapes=[pltpu.VMEM((tm, tn), F32)],
        compiler_params=pltpu.CompilerParams(dimension_semantics=("parallel", "parallel", "arbitrary"),
                                             vmem_limit_bytes=VMEM_LIMIT),
    )(a, b)


class Rows:
    def __init__(self, arr, width=None, cb=0):
        self.arr, self.width, self.cb = arr, (arr.shape[1] if width is None else width), cb


class Heads:
    def __init__(self, arr):
        self.arr = arr


class Halo:
    def __init__(self, arr, side):
        self.arr, self.side = arr, side


def _rows_call(name, fn, ins, consts, outs, accs=(), tm=256, with_pid=False):
    T = next(o.arr.shape[1] if isinstance(o, Heads) else o.arr.shape[0] for o in ins if not isinstance(o, Halo))
    tm = min(tm, T)
    n_tiles = T // tm
    n_in, n_c, n_out = len(ins), len(consts), len(outs)
    in_specs, args = [], []
    for o in ins:
        if isinstance(o, Rows):
            in_specs.append(pl.BlockSpec((tm, o.width), lambda i, cb=o.cb: (i, cb)))
        elif isinstance(o, Heads):
            in_specs.append(pl.BlockSpec((NH, tm, HN), lambda i: (0, i, 0)))
        else:
            w = o.arr.shape[1]
            if o.side < 0:
                in_specs.append(pl.BlockSpec((8, w), lambda i: (jnp.maximum(i * (tm // 8) - 1, 0), 0)))
            else:
                in_specs.append(pl.BlockSpec((8, w), lambda i: (jnp.minimum((i + 1) * (tm // 8), T // 8 - 1), 0)))
        args.append(o.arr)
    for c in consts:
        in_specs.append(pl.BlockSpec(c.shape, lambda i, nd=c.ndim: (0,) * nd))
        args.append(c)
    out_specs, out_shape = [], []
    for o in outs:
        if o[0] == "rows":
            out_specs.append(pl.BlockSpec((tm, o[1]), lambda i: (i, 0)))
            out_shape.append(jax.ShapeDtypeStruct((T, o[1]), o[2]))
        else:
            out_specs.append(pl.BlockSpec((NH, tm, HN), lambda i: (0, i, 0)))
            out_shape.append(jax.ShapeDtypeStruct((NH, T, HN), o[1]))
    for shape, dt in accs:
        out_specs.append(pl.BlockSpec(shape, lambda i, nd=len(shape): (0,) * nd))
        out_shape.append(jax.ShapeDtypeStruct(shape, dt))

    def body(*refs):
        i = pl.program_id(0)
        vals = []
        for o, r in zip(ins, refs[:n_in]):
            if isinstance(o, Heads):
                vals.append(jnp.concatenate([r[h] for h in range(NH)], axis=-1))
            else:
                vals.append(r[...])
        vals += [r[...] for r in refs[n_in:n_in + n_c]]
        res = fn(i, n_tiles, *vals) if with_pid else fn(*vals)
        out_refs = refs[n_in + n_c:]
        for o, r, v in zip(outs, out_refs[:n_out], res[:n_out]):
            if o[0] == "rows":
                r[...] = v.astype(r.dtype)
            else:
                for h in range(NH):
                    r[h] = v[:, h * HN:(h + 1) * HN].astype(r.dtype)
        if accs:
            @pl.when(i == 0)
            def _():
                for r in out_refs[n_out:]:
                    r[...] = jnp.zeros_like(r)

            for r, v in zip(out_refs[n_out:], res[n_out:]):
                r[...] += v.astype(r.dtype)

    res = pl.pallas_call(
        body, name=name, grid=(n_tiles,), in_specs=in_specs, out_specs=out_specs, out_shape=out_shape,
        compiler_params=pltpu.CompilerParams(dimension_semantics=("arbitrary",), vmem_limit_bytes=VMEM_LIMIT),
    )(*args)
    return res


def _rms(x, g):
    return x * lax.rsqrt(jnp.mean(x * x, axis=-1, keepdims=True) + NORM_EPS) * g


def _gelu(x):
    return 0.5 * x * (1.0 + lax.erf(x * 0.7071067811865476))


def _sigmoid(x):
    return 1.0 / (1.0 + jnp.exp(-x))


def _bdot(a, b):
    return jnp.dot(a.astype(BF16), b.astype(BF16), preferred_element_type=F32)


def _head_sum(x):
    r = lax.broadcasted_iota(jnp.int32, (LANES, LANES), 0) // HN
    c = lax.broadcasted_iota(jnp.int32, (LANES, LANES), 1) // HN
    ones = (r == c).astype(F32)
    parts = [jnp.dot(x[:, j * LANES:(j + 1) * LANES], ones, precision=HI, preferred_element_type=F32)
             for j in range(D // LANES)]
    return jnp.concatenate(parts, axis=-1)


def _sgu_fn(p, ln_w, ln_b, sw, sbt):
    z = _gelu(p)
    u, v = z[:, :D], z[:, D:]
    mu = jnp.mean(v, axis=-1, keepdims=True)
    var = jnp.mean(jnp.square(v - mu), axis=-1, keepdims=True)
    vn = (v - mu) * lax.rsqrt(var + LN_EPS) * ln_w + ln_b
    ri = lax.broadcasted_iota(jnp.int32, (SGU_C, SGU_C), 0)
    ci = lax.broadcasted_iota(jnp.int32, (SGU_C, SGU_C), 1)
    mask = (ci <= ri).astype(F32)
    dg = D // SGU_G
    parts = []
    for g in range(SGU_G):
        parts.append(_bdot(sw[g] * mask, vn[:, g * dg:(g + 1) * dg]) + sbt[:, g:g + 1])
    return u * jnp.concatenate(parts, axis=-1)


def _pre_fn(qr, qk, qv, qxw, qxa, qxg, wl, w0, al, a0, gl, k_k, k_a):
    w = -jax.nn.softplus(-(w0 + _bdot(jnp.tanh(qxw), wl))) - 0.5
    lw = -jnp.exp(w)
    aa = _sigmoid(a0 + _bdot(qxa, al))
    g = _bdot(_sigmoid(qxg), gl)
    kk = qk * k_k
    kk = kk / jnp.maximum(jnp.sqrt(_head_sum(kk * kk)), 1e-12)
    k2 = qk * (1.0 + (aa - 1.0) * k_a)
    return qr, lw, k2, qv, kk, aa, g


def _post_fn(o, r, k2, v, g, ln_w, ln_b, r_k):
    mu = _head_sum(o) * (1.0 / HN)
    d = o - mu
    var = _head_sum(d * d) * (1.0 / HN)
    on = d * lax.rsqrt(var + GN_EPS) * ln_w + ln_b
    bonus = _head_sum(r * k2 * r_k) * v
    return (on + bonus) * g


def _gate_fn(pg, ya, yb):
    return _sigmoid(pg[:, :D]) * ya + _sigmoid(pg[:, D:]) * yb


def _chunk_fn(S0, r, lw, k, v, kk, a):
    C = SCAN_C

    def bmm(x, y, cx, cy):
        return lax.dot_general(x, y, (((cx,), (cy,)), ((0,), (0,))), precision=HI, preferred_element_type=F32)

    ti = lax.broadcasted_iota(jnp.int32, (C, C), 0)
    tj = lax.broadcasted_iota(jnp.int32, (C, C), 1)
    incl = (tj <= ti).astype(F32)
    strict = (tj < ti).astype(F32)
    cum = bmm(jnp.broadcast_to(incl, (NH, C, C)), lw, 2, 1)
    g_in, g_ex, g_inv = jnp.exp(cum), jnp.exp(cum - lw), jnp.exp(-cum)
    kkt, bt, kt, rt = kk * g_ex, kk * a * g_inv, k * g_inv, r * g_in
    X = -bmm(kkt, bt, 2, 2) * strict
    y = bmm(kkt, S0, 2, 2) + bmm(bmm(kkt, kt, 2, 2) * strict, v, 2, 1)
    pows = [X]
    while 2 ** len(pows) < C:
        pows.append(bmm(pows[-1], pows[-1], 2, 1))
    for Xp in reversed(pows):
        y = y + bmm(Xp, y, 2, 1)
    U = -y
    O = (bmm(rt, S0, 2, 2) + bmm(bmm(rt, bt, 2, 2) * incl, U, 2, 1) + bmm(bmm(rt, kt, 2, 2) * incl, v, 2, 1))
    S1 = (S0 + bmm(U, bt, 1, 1) + bmm(v, kt, 1, 1)) * g_in[:, C - 1:C, :]
    return O, S1


def _scan_fwd(r, lw, k, v, kk, a, tb=256):
    T = r.shape[1]
    tb = min(tb, T)
    n_chunks = tb // SCAN_C

    def body(r_ref, lw_ref, k_ref, v_ref, kk_ref, a_ref, o_ref, s0_ref, s_ref):
        @pl.when(pl.program_id(0) == 0)
        def _():
            s_ref[...] = jnp.zeros_like(s_ref)

        def step(c, carry):
            sl = pl.ds(pl.multiple_of(c * SCAN_C, SCAN_C), SCAN_C)
            S0 = s_ref[...]
            s0_ref[c] = S0
            O, S1 = _chunk_fn(S0, r_ref[:, sl, :], lw_ref[:, sl, :], k_ref[:, sl, :], v_ref[:, sl, :],
                              kk_ref[:, sl, :], a_ref[:, sl, :])
            o_ref[:, sl, :] = O
            s_ref[...] = S1
            return carry

        lax.fori_loop(0, n_chunks, step, 0)

    hm = pl.BlockSpec((NH, tb, HN), lambda i: (0, i, 0))
    return pl.pallas_call(
        body, name="rwkv_scan_fwd", grid=(T // tb,), in_specs=[hm] * 6,
        out_specs=[hm, pl.BlockSpec((n_chunks, NH, HN, HN), lambda i: (i, 0, 0, 0))],
        out_shape=[jax.ShapeDtypeStruct((NH, T, HN), F32), jax.ShapeDtypeStruct((T // SCAN_C, NH, HN, HN), F32)],
        scratch_shapes=[pltpu.VMEM((NH, HN, HN), F32)],
        compiler_params=pltpu.CompilerParams(dimension_semantics=("arbitrary",), vmem_limit_bytes=VMEM_LIMIT),
    )(r, lw, k, v, kk, a)


def _scan_bwd(r, lw, k, v, kk, a, s0s, do, tb=128):
    T = r.shape[1]
    tb = min(tb, T)
    n_chunks = tb // SCAN_C
    nb = T // tb

    def body(r_ref, lw_ref, k_ref, v_ref, kk_ref, a_ref, s0_ref, do_ref, dr, dlw, dk, dv, dkk, da, ds_ref):
        @pl.when(pl.program_id(0) == 0)
        def _():
            ds_ref[...] = jnp.zeros_like(ds_ref)

        def step(j, carry):
            c = n_chunks - 1 - j
            sl = pl.ds(pl.multiple_of(c * SCAN_C, SCAN_C), SCAN_C)
            _, vjp = jax.vjp(_chunk_fn, s0_ref[c], r_ref[:, sl, :], lw_ref[:, sl, :], k_ref[:, sl, :],
                             v_ref[:, sl, :], kk_ref[:, sl, :], a_ref[:, sl, :])
            g = vjp((do_ref[:, sl, :], ds_ref[...]))
            ds_ref[...] = g[0]
            for ref, val in zip((dr, dlw, dk, dv, dkk, da), g[1:]):
                ref[:, sl, :] = val
            return carry

        lax.fori_loop(0, n_chunks, step, 0)

    hm = pl.BlockSpec((NH, tb, HN), lambda i: (0, nb - 1 - i, 0))
    return pl.pallas_call(
        body, name="rwkv_scan_bwd", grid=(nb,),
        in_specs=[hm] * 6 + [pl.BlockSpec((n_chunks, NH, HN, HN), lambda i: (nb - 1 - i, 0, 0, 0)), hm],
        out_specs=[hm] * 6, out_shape=[jax.ShapeDtypeStruct((NH, T, HN), F32)] * 6,
        scratch_shapes=[pltpu.VMEM((NH, HN, HN), F32)],
        compiler_params=pltpu.CompilerParams(dimension_semantics=("arbitrary",), vmem_limit_bytes=VMEM_LIMIT),
    )(r, lw, k, v, kk, a, s0s, do)


def _shift_down(i, p, prev8):
    first = jnp.where(i > 0, prev8[7:8, :], 0.0)
    row = lax.broadcasted_iota(jnp.int32, p.shape, 0)
    return jnp.where(row == 0, first, pltpu.roll(p, 1, axis=0))


def _mix_fwd(p, sb, tm=256):
    def fn(i, n, p, prev8, sb):
        return (p * sb[0:1] + _shift_down(i, p, prev8) * sb[1:2],)
    return _rows_call("shift_mix_fwd", fn, [Rows(p), Halo(p, -1)], [sb], [("rows", p.shape[1], F32)], tm=tm,
                      with_pid=True)[0]


def _mix_bwd(dq, p, sb, tm=256):
    def fn(i, n, dq, next8, p, prev8, sb):
        ps = _shift_down(i, p, prev8)
        d1 = dq * sb[1:2]
        last = jnp.where(i < n - 1, next8[0:1, :] * sb[1:2], 0.0)
        row = lax.broadcasted_iota(jnp.int32, dq.shape, 0)
        up = jnp.where(row == dq.shape[0] - 1, last, pltpu.roll(d1, dq.shape[0] - 1, axis=0))
        return (dq * sb[0:1] + up, jnp.sum(dq * p, axis=0, keepdims=True), jnp.sum(dq * ps, axis=0, keepdims=True))
    w = p.shape[1]
    return _rows_call("shift_mix_bwd", fn, [Rows(dq), Halo(dq, +1), Rows(p), Halo(p, -1)], [sb], [("rows", w, F32)],
                      accs=[((1, w), F32), ((1, w), F32)], tm=tm, with_pid=True)


def _local_step(x, target, W):
    G = {}
    a = _rows_call("norm_mix_fwd", lambda x, g: (_rms(x, g),), [Rows(x)], [W["g_mix"]], [("rows", D, BF16)])[0]
    p_sgu = _matmul("proj_sgu", a, W["w_sgu"], "nn")
    p_rw = _matmul("proj_rwkv", a, W["w_rw"], "nn")
    p_gate = _matmul("proj_gate", a, W["w_gate"], "nn")

    sgu_consts = [W["sgu_ln_w"], W["sgu_ln_b"], W["sgu_w"], W["sgu_bt"]]
    s = _rows_call("sgu_fwd", lambda *t: (_sgu_fn(*t),), [Rows(p_sgu)], sgu_consts, [("rows", D, BF16)], tm=SGU_C)[0]
    y_a = _matmul("proj_a", s, W["w_proj_a"], "nn")

    q = _mix_fwd(p_rw, W["sb"])
    q_ins = [Rows(q, D, 0), Rows(q, D, 1), Rows(q, D, 2), Rows(q, 128, 24), Rows(q, 128, 25), Rows(q, 256, 13)]
    pre_consts = [W["w_lora"], W["w0"], W["a_lora"], W["a0"], W["g_lora"], W["k_k"], W["k_a"]]
    r_h, lw_h, k_h, v_h, kk_h, a_h, g_gate = _rows_call(
        "rwkv_pre_fwd", _pre_fn, q_ins, pre_consts, [("heads", F32)] * 6 + [("rows", D, F32)], tm=128)
    o_h, s0s = _scan_fwd(r_h, lw_h, k_h, v_h, kk_h, a_h)
    post_ins = [Heads(o_h), Heads(r_h), Heads(k_h), Heads(v_h), Rows(g_gate)]
    post_consts = [W["ln_x_w"], W["ln_x_b"], W["r_k"]]
    z_b = _rows_call("rwkv_post_fwd", lambda *t: (_post_fn(*t),), post_ins, post_consts, [("rows", D, BF16)], tm=128)[0]
    y_b = _matmul("proj_b", z_b, W["w_proj_b"], "nn")

    gate_ins = [Rows(p_gate), Rows(y_a), Rows(y_b)]
    mixed = _rows_call("gate_fwd", lambda *t: (_gate_fn(*t),), gate_ins, [], [("rows", D, BF16)])[0]
    mo = _matmul("proj_out", mixed, W["w_out"], "nn")

    def res1(x, mo, g):
        h1 = x + mo
        return h1, _rms(h1, g)
    h1, f = _rows_call("residual_norm_fwd", res1, [Rows(x), Rows(mo)], [W["g_ffn"]],
                       [("rows", D, F32), ("rows", D, BF16)])
    u1 = _matmul("ffn_up", f, W["w_ffn1"], "nn")
    act = _rows_call("ffn_act_fwd", lambda u: (jnp.square(jnp.maximum(u, 0.0)),), [Rows(u1)], [],
                     [("rows", D_FF, BF16)])[0]
    ff = _matmul("ffn_down", act, W["w_ffn2"], "nn")

    def head(h1, ff, tgt, g):
        def f_(h1, ff, g):
            y = _rms(h1 + ff, g)
            return 0.5 * jnp.sum(jnp.mean(jnp.square(y - tgt), axis=-1))
        loss, (dh2, _, dg) = jax.value_and_grad(f_, argnums=(0, 1, 2))(h1, ff, g)
        return dh2, jnp.full((8, LANES), loss, F32), dg
    dh2, loss_acc, G["g_final"] = _rows_call("loss_head", head, [Rows(h1), Rows(ff), Rows(target)], [W["g_final"]],
                                             [("rows", D, F32)], accs=[((8, LANES), F32), ((1, D), F32)])

    d_act = _matmul("ffn_down_dx", dh2, W["w_ffn2"], "nt")
    G["w_ffn2"] = _matmul("ffn_down_dw", act, dh2, "tn")
    d_u1 = _rows_call("ffn_act_bwd", lambda u, d: (d * 2.0 * jnp.maximum(u, 0.0),), [Rows(u1), Rows(d_act)], [],
                      [("rows", D_FF, BF16)])[0]
    d_f = _matmul("ffn_up_dx", d_u1, W["w_ffn1"], "nt")
    G["w_ffn1"] = _matmul("ffn_up_dw", f, d_u1, "tn", out_blocks=N_DEV)

    def res1_bwd(x, mo, d_f, dh2, g):
        _, vjp = jax.vjp(lambda h, g: _rms(h, g), x + mo, g)
        dh, dg = vjp(d_f)
        return dh2 + dh, dg
    dh1, G["g_ffn"] = _rows_call("residual_norm_bwd", res1_bwd, [Rows(x), Rows(mo), Rows(d_f), Rows(dh2)],
                                 [W["g_ffn"]], [("rows", D, F32)], accs=[((1, D), F32)])
    d_mixed = _matmul("proj_out_dx", dh1, W["w_out"], "nt")
    G["w_out"] = _matmul("proj_out_dw", mixed, dh1, "tn")

    def gate_bwd(pg, ya, yb, dm):
        _, vjp = jax.vjp(_gate_fn, pg, ya, yb)
        return vjp(dm)
    d_gate, d_ya, d_yb = _rows_call("gate_bwd", gate_bwd, gate_ins + [Rows(d_mixed)], [],
                                    [("rows", 2 * D, F32), ("rows", D, BF16), ("rows", D, BF16)])

    d_s = _matmul("proj_a_dx", d_ya, W["w_proj_a"], "nt")
    G["w_proj_a"] = _matmul("proj_a_dw", s, d_ya, "tn")

    def sgu_bwd(p, ds, *c):
        _, vjp = jax.vjp(_sgu_fn, p, *c)
        return vjp(ds)
    d_p_sgu, G["sgu_ln_w"], G["sgu_ln_b"], G["sgu_w"], G["sgu_bt"] = _rows_call(
        "sgu_bwd", sgu_bwd, [Rows(p_sgu), Rows(d_s)], sgu_consts, [("rows", 2 * D, F32)],
        accs=[((1, D), F32), ((1, D), F32), ((SGU_G, SGU_C, SGU_C), F32), ((SGU_C, SGU_G), F32)], tm=SGU_C)

    d_zb = _matmul("proj_b_dx", d_yb, W["w_proj_b"], "nt")
    G["w_proj_b"] = _matmul("proj_b_dw", z_b, d_yb, "tn")

    def post_bwd(o, r, k2, v, g, dz, *c):
        _, vjp = jax.vjp(_post_fn, o, r, k2, v, g, *c)
        return vjp(dz)
    do_h, dr1, dk1, dv1, d_g, G["ln_x_w"], G["ln_x_b"], G["r_k"] = _rows_call(
        "rwkv_post_bwd", post_bwd, post_ins + [Rows(d_zb)], post_consts, [("heads", F32)] * 4 + [("rows", D, F32)],
        accs=[((1, D), F32)] * 3, tm=128)
    dr2, dlw, dk2, dv2, dkk, daa = _scan_bwd(r_h, lw_h, k_h, v_h, kk_h, a_h, s0s, do_h)

    def pre_bwd(qr, qk, qv, qxw, qxa, qxg, dr1, dr2, dlw, dk1, dk2, dv1, dv2, dkk, daa, dg, *c):
        _, vjp = jax.vjp(_pre_fn, qr, qk, qv, qxw, qxa, qxg, *c)
        g = vjp((dr1 + dr2, dlw, dk1 + dk2, dv1 + dv2, dkk, daa, dg))
        dq = jnp.concatenate(g[:6], axis=-1)
        return (dq,) + tuple(g[6:])
    pre_b_ins = q_ins + [Heads(dr1), Heads(dr2), Heads(dlw), Heads(dk1), Heads(dk2), Heads(dv1), Heads(dv2),
                         Heads(dkk), Heads(daa), Rows(d_g)]
    d_q, G["w_lora"], G["w0"], G["a_lora"], G["a0"], G["g_lora"], G["k_k"], G["k_a"] = _rows_call(
        "rwkv_pre_bwd", pre_bwd, pre_b_ins, pre_consts, [("rows", RW_INT, F32)],
        accs=[((128, D), F32), ((1, D), F32), ((128, D), F32), ((1, D), F32), ((256, D), F32), ((1, D), F32),
              ((1, D), F32)], tm=128)
    d_p_rw, dsb0, dsb1 = _mix_bwd(d_q, p_rw, W["sb"])
    G["sb"] = jnp.concatenate([dsb0, dsb1], axis=0)

    G["w_sgu"] = _matmul("proj_sgu_dw", a, d_p_sgu, "tn")
    G["w_rw"] = _matmul("proj_rwkv_dw", a, d_p_rw, "tn")
    G["w_gate"] = _matmul("proj_gate_dw", a, d_gate, "tn")
    da1 = _matmul("proj_sgu_dx", d_p_sgu, W["w_sgu"], "nt")
    da2 = _matmul("proj_rwkv_dx", d_p_rw, W["w_rw"], "nt")
    da3 = _matmul("proj_gate_dx", d_gate, W["w_gate"], "nt")

    def norm1_bwd(x, da1, da2, da3, dh1, g):
        _, vjp = jax.vjp(_rms, x, g)
        dx, dg = vjp(da1 + da2 + da3)
        return dh1 + dx, dg
    dx, G["g_mix"] = _rows_call("norm_mix_bwd", norm1_bwd, [Rows(x), Rows(da1), Rows(da2), Rows(da3), Rows(dh1)],
                                [W["g_mix"]], [("rows", D, F32)], accs=[((1, D), F32)])
    return loss_acc[0, 0], dx, G


def _exchange(name, bufs, gather):
    nb = len(bufs)
    n_peers = N_DEV - 1

    def body(*refs):
        in_refs, out_refs = refs[:nb], refs[nb:2 * nb]
        send_sems, recv_sems, local_sems = refs[2 * nb:]
        x, y, c = lax.axis_index("x"), lax.axis_index("y"), lax.axis_index("c")
        me = 4 * x + 2 * y + c

        def src(b, dest):
            return in_refs[b] if gather else in_refs[b].at[dest]

        local = [pltpu.make_async_copy(src(b, me), out_refs[b].at[me], local_sems.at[b]) for b in range(nb)]
        for cp in local:
            cp.start()
        sends, recvs = [], []
        for kbits in range(1, N_DEV):
            px = 1 - x if kbits & 4 else x
            py = 1 - y if kbits & 2 else y
            pc = 1 - c if kbits & 1 else c
            peer = 4 * px + 2 * py + pc
            for b in range(nb):
                s = (kbits - 1) * nb + b
                sends.append(pltpu.make_async_remote_copy(
                    src_ref=src(b, peer), dst_ref=out_refs[b].at[me], send_sem=send_sems.at[s],
                    recv_sem=recv_sems.at[s], device_id=(px, py, pc), device_id_type=pl.DeviceIdType.MESH))
                recvs.append(pltpu.make_async_remote_copy(
                    src_ref=src(b, peer), dst_ref=out_refs[b].at[peer], send_sem=send_sems.at[s],
                    recv_sem=recv_sems.at[s], device_id=(px, py, pc), device_id_type=pl.DeviceIdType.MESH))
        for cp in sends:
            cp.start()
        for cp in recvs:
            cp.wait_recv()
        for cp in sends:
            cp.wait_send()
        for cp in local:
            cp.wait()

    out_shape = [jax.ShapeDtypeStruct((N_DEV,) + (b.shape if gather else b.shape[1:]), b.dtype) for b in bufs]
    any_spec = pl.BlockSpec(memory_space=pl.ANY)
    return pl.pallas_call(
        body, name=name, in_specs=[any_spec] * nb, out_specs=[any_spec] * nb, out_shape=out_shape,
        scratch_shapes=[pltpu.SemaphoreType.DMA((n_peers * nb,)), pltpu.SemaphoreType.DMA((n_peers * nb,)),
                        pltpu.SemaphoreType.DMA((nb,))],
    )(*bufs)


def _adamw(name, slots, w, m, v, tr=256):
    R, Wd = w.shape
    tr = tr if R % tr == 0 else R

    def body(s_ref, w_ref, m_ref, v_ref, g_out, d_out, m_out, v_out):
        g = s_ref[0]
        for j in range(1, N_DEV):
            g = g + s_ref[j]
        m_new = ADAM_B1 * m_ref[...] + (1.0 - ADAM_B1) * g
        v_new = ADAM_B2 * v_ref[...] + (1.0 - ADAM_B2) * jnp.square(g)
        m_hat = m_new / (1.0 - ADAM_B1 ** ADAM_STEP)
        v_hat = v_new / (1.0 - ADAM_B2 ** ADAM_STEP)
        g_out[...] = g
        d_out[...] = -ADAM_LR * (m_hat / (jnp.sqrt(v_hat) + ADAM_EPS) + ADAM_WD * w_ref[...])
        m_out[...] = m_new
        v_out[...] = v_new

    row = pl.BlockSpec((tr, Wd), lambda i: (i, 0))
    return pl.pallas_call(
        body, name=name, grid=(R // tr,), in_specs=[pl.BlockSpec((N_DEV, tr, Wd), lambda i: (0, i, 0)), row, row, row],
        out_specs=[row] * 4, out_shape=[jax.ShapeDtypeStruct((R, Wd), F32)] * 4,
        compiler_params=pltpu.CompilerParams(dimension_semantics=("parallel",), vmem_limit_bytes=VMEM_LIMIT),
    )(slots, w, m, v)


PACK_W = 1024
_SMALL_SIZES = [int(np.prod(s)) for _, s in REPLICATED]
_SMALL_ROWS = _round_up(_round_up(sum(_SMALL_SIZES) + PACK_W, PACK_W) // PACK_W, 8)
_LOSS_AT = sum(_SMALL_SIZES)
W_IN_SHARD = P_TOTAL // N_DEV


def _pack_rows(parts, rows, dtype):
    flat = jnp.concatenate([p.reshape(-1).astype(dtype) for p in parts])
    return jnp.pad(flat, (0, rows * PACK_W - flat.shape[0])).reshape(rows, PACK_W)


def _w_in_from_blocks(blocks, tm=256):
    o = 2 * D

    def body(g_ref, sgu_ref, rw_ref, gate_ref):
        full = jnp.concatenate([g_ref[j].astype(F32) for j in range(N_DEV)], axis=1)
        z = lambda n: jnp.zeros((tm, n), F32)
        sgu_ref[...] = full[:, :o].astype(BF16)
        c = o + 3 * D
        rw = jnp.concatenate([full[:, o:c], full[:, c:c + L_W], z(128 - L_W), full[:, c + L_W:c + L_W + L_A],
                              z(128 - L_A), full[:, c + L_W + L_A:o + C_B], z(256 - L_G)], axis=1)
        rw_ref[...] = rw.astype(BF16)
        gate_ref[...] = full[:, o + C_B:].astype(BF16)

    widths = (2 * D, RW_INT, 2 * D)
    return pl.pallas_call(
        body, name="w_in_from_blocks", grid=(D // tm,),
        in_specs=[pl.BlockSpec((N_DEV, tm, W_IN_SHARD), lambda i: (0, i, 0))],
        out_specs=[pl.BlockSpec((tm, w), lambda i: (i, 0)) for w in widths],
        out_shape=[jax.ShapeDtypeStruct((D, w), BF16) for w in widths],
        compiler_params=pltpu.CompilerParams(dimension_semantics=("parallel",), vmem_limit_bytes=VMEM_LIMIT),
    )(blocks)


def _w_in_to_blocks(g_sgu, g_rw, g_gate, tm=256):
    def body(sgu_ref, rw_ref, gate_ref, o_ref):
        rw = rw_ref[...]
        c = 3 * D
        full = jnp.concatenate([sgu_ref[...], rw[:, :c], rw[:, c:c + L_W], rw[:, c + 128:c + 128 + L_A],
                                rw[:, c + 256:c + 256 + L_G], gate_ref[...]], axis=1)
        for j in range(N_DEV):
            o_ref[j] = full[:, j * W_IN_SHARD:(j + 1) * W_IN_SHARD]

    widths = (2 * D, RW_INT, 2 * D)
    return pl.pallas_call(
        body, name="w_in_to_blocks", grid=(D // tm,),
        in_specs=[pl.BlockSpec((tm, w), lambda i: (i, 0)) for w in widths],
        out_specs=pl.BlockSpec((N_DEV, tm, W_IN_SHARD), lambda i: (0, i, 0)),
        out_shape=jax.ShapeDtypeStruct((N_DEV, D, W_IN_SHARD), F32),
        compiler_params=pltpu.CompilerParams(dimension_semantics=("parallel",), vmem_limit_bytes=VMEM_LIMIT),
    )(g_sgu, g_rw, g_gate)


def _cols_from_blocks(blk):
    return jnp.transpose(blk, (1, 0, 2)).reshape(blk.shape[1], -1)


def _cols_to_blocks(g):
    r, c = g.shape
    return jnp.transpose(g.reshape(r, N_DEV, c // N_DEV), (1, 0, 2))


def _gather_weights(shards):
    names = [n for n, _, _ in SHARDED]
    bufs = [shards[n] if n == "shift_b" else shards[n].astype(BF16) for n in names]
    got = dict(zip(names, _exchange("weight_all_gather", bufs, gather=True)))
    W = {}
    W["w_sgu"], W["w_rw"], W["w_gate"] = _w_in_from_blocks(got["w_in"])
    for n in ("w_proj_a", "w_proj_b", "w_out", "w_ffn2"):
        W[n] = got[n].reshape(-1, D)
    W["w_ffn1"] = got["w_ffn1"]
    z = lambda r, c, dt: jnp.zeros((r, c), dt)
    W["w_lora"] = jnp.concatenate([_cols_from_blocks(got["w_lora_w"]).astype(F32), z(128 - L_W, D, F32)], axis=0)
    W["a_lora"] = jnp.concatenate([_cols_from_blocks(got["a_lora_w"]).astype(F32), z(128 - L_A, D, F32)], axis=0)
    W["g_lora"] = jnp.concatenate([_cols_from_blocks(got["g_lora_w"]).astype(F32), z(256 - L_G, D, F32)], axis=0)
    sb = _cols_from_blocks(got["shift_b"])
    W["sb"] = jnp.concatenate([sb[:, :3 * D], sb[:, 3 * D:3 * D + L_W], z(2, 128 - L_W, F32),
                               sb[:, 3 * D + L_W:3 * D + L_W + L_A], z(2, 128 - L_A, F32),
                               sb[:, 3 * D + L_W + L_A:], z(2, 256 - L_G, F32)], axis=1)
    return W


def _replicated_weights(rep):
    W = {n: rep[n] for n in ("g_mix", "sgu_ln_w", "sgu_ln_b", "w0", "a0", "k_k", "k_a", "r_k", "ln_x_w", "ln_x_b",
                             "g_ffn")}
    W["g_final"] = rep["g_final"].reshape(1, D)
    W["sgu_w"] = rep["sgu_w"][0]
    W["sgu_bt"] = jnp.transpose(rep["sgu_b"][0])
    return W


def _grad_blocks(G):
    sbg = G["sb"]
    c = 3 * D
    sb = jnp.concatenate([sbg[:, :c], sbg[:, c:c + L_W], sbg[:, c + 128:c + 128 + L_A],
                          sbg[:, c + 256:c + 256 + L_G]], axis=1)
    blocks = {
        "w_in": _w_in_to_blocks(G["w_sgu"], G["w_rw"], G["w_gate"]),
        "shift_b": _cols_to_blocks(sb),
        "w_lora_w": _cols_to_blocks(G["w_lora"][:L_W]), "a_lora_w": _cols_to_blocks(G["a_lora"][:L_A]),
        "g_lora_w": _cols_to_blocks(G["g_lora"][:L_G]),
        "w_ffn1": G["w_ffn1"],
    }
    for n in ("w_proj_a", "w_proj_b", "w_out", "w_ffn2"):
        blocks[n] = G[n].reshape(N_DEV, -1, D)
    return blocks


def _replicated_grads(G):
    small = {n: G[n] for n in ("g_mix", "sgu_ln_w", "sgu_ln_b", "w0", "a0", "k_k", "k_a", "r_k", "ln_x_w", "ln_x_b",
                               "g_ffn", "g_final")}
    small["sgu_w"] = G["sgu_w"]
    small["sgu_b"] = jnp.transpose(G["sgu_bt"])
    return small


def kernel(x, g_mix, w_in, sgu_ln_w, sgu_ln_b, sgu_w, sgu_b, w_proj_a, shift_b, w_lora_w, w0, a_lora_w, a0, g_lora_w, k_k, k_a, r_k, ln_x_w, ln_x_b, w_proj_b, w_out, g_ffn, w_ffn1, w_ffn2, g_final, loss_target, m_g_mix, m_w_in, m_sgu_ln_w, m_sgu_ln_b, m_sgu_w, m_sgu_b, m_w_proj_a, m_shift_b, m_w_lora_w, m_w0, m_a_lora_w, m_a0, m_g_lora_w, m_k_k, m_k_a, m_r_k, m_ln_x_w, m_ln_x_b, m_w_proj_b, m_w_out, m_g_ffn, m_w_ffn1, m_w_ffn2, m_g_final, v_g_mix, v_w_in, v_sgu_ln_w, v_sgu_ln_b, v_sgu_w, v_sgu_b, v_w_proj_a, v_shift_b, v_w_lora_w, v_w0, v_a_lora_w, v_a0, v_g_lora_w, v_k_k, v_k_a, v_r_k, v_ln_x_w, v_ln_x_b, v_w_proj_b, v_w_out, v_g_ffn, v_w_ffn1, v_w_ffn2, v_g_final):
    env = dict(locals())
    weights = {n: env[n] for n in WEIGHT_ORDER}
    moms = {n: env["m_" + n] for n in WEIGHT_ORDER}
    vars_ = {n: env["v_" + n] for n in WEIGHT_ORDER}

    W = _gather_weights({n: weights[n][0] for n, _, _ in SHARDED})
    W.update(_replicated_weights({n: weights[n] for n, _ in REPLICATED}))
    loss_part, dx, G = _local_step(x[0], loss_target[0], W)

    names = [n for n, _, _ in SHARDED]
    blocks = _grad_blocks(G)
    slots = dict(zip(names, _exchange("grad_reduce_scatter", [blocks[n] for n in names], gather=False)))
    small = _replicated_grads(G)
    small_parts = [small[n] for n, _ in REPLICATED] + [jnp.full((PACK_W,), loss_part, F32)]
    small_slots = _exchange("grad_all_gather", [_pack_rows(small_parts, _SMALL_ROWS, F32)], gather=True)[0]

    outs = [dict(), dict(), dict(), dict()]
    for n in names:
        res = _adamw("adamw_" + n, slots[n], weights[n][0], moms[n][0], vars_[n][0])
        for k in range(4):
            outs[k][n] = res[k][None]

    def packed(d):
        return _pack_rows([d[n] for n, _ in REPLICATED], _SMALL_ROWS, F32)
    small_out = _adamw("adamw_replicated", small_slots, packed(weights), packed(moms), packed(vars_))
    for k in range(4):
        flat = small_out[k].reshape(-1)
        off = 0
        for (n, s), size in zip(REPLICATED, _SMALL_SIZES):
            outs[k][n] = flat[off:off + size].reshape(s)
            off += size
    loss = small_out[0].reshape(-1)[_LOSS_AT]
    return (loss, dx[None], *[outs[0][n] for n in WEIGHT_ORDER], *[outs[1][n] for n in WEIGHT_ORDER],
            *[outs[2][n] for n in WEIGHT_ORDER], *[outs[3][n] for n in WEIGHT_ORDER])
```

```python
import functools
import numpy as np
import jax
import jax.numpy as jnp
from jax import lax
from jax.experimental import pallas as pl
from jax.experimental.pallas import tpu as pltpu

F32 = jnp.float32
BF16 = jnp.bfloat16
HI = lax.Precision.HIGHEST

D = 1024
NH, HN = 16, 64
SGU_G, SGU_C = 8, 128
L_W, L_A, L_G = 64, 64, 160
C_B = 3 * D + L_W + L_A + L_G
P_TOTAL = 2 * D + C_B + 2 * D
D_FF = 4 * D
RW_INT = 3 * D + 128 + 128 + 256
NORM_EPS, LN_EPS, GN_EPS = 1e-6, 1e-5, 64e-5
N_DEV = 8
LANES = 128
SCAN_C = 64
SOLVE_B = 16
SCAN_PRECISION = lax.Precision.HIGH
GRAD_PAYLOAD = BF16
VMEM_LIMIT = 56 * 1024 * 1024

ADAM_LR, ADAM_B1, ADAM_B2, ADAM_EPS, ADAM_WD, ADAM_STEP = 0.001, 0.9, 0.999, 1e-08, 0.01, 10

SHARDED = [
    ("w_in", (D, P_TOTAL), 1), ("w_proj_a", (D, D), 0), ("shift_b", (2, C_B), 1), ("w_lora_w", (L_W, D), 1),
    ("a_lora_w", (L_A, D), 1), ("g_lora_w", (L_G, D), 1), ("w_proj_b", (D, D), 0), ("w_out", (D, D), 0),
    ("w_ffn1", (D, D_FF), 1), ("w_ffn2", (D_FF, D), 0),
]
REPLICATED = [
    ("g_mix", (1, D)), ("sgu_ln_w", (1, D)), ("sgu_ln_b", (1, D)), ("sgu_w", (1, SGU_G, SGU_C, SGU_C)),
    ("sgu_b", (1, SGU_G, SGU_C)), ("w0", (1, D)), ("a0", (1, D)), ("k_k", (1, D)), ("k_a", (1, D)), ("r_k", (1, D)),
    ("ln_x_w", (1, D)), ("ln_x_b", (1, D)), ("g_ffn", (1, D)), ("g_final", (D,)),
]
WEIGHT_ORDER = ["g_mix", "w_in", "sgu_ln_w", "sgu_ln_b", "sgu_w", "sgu_b", "w_proj_a", "shift_b", "w_lora_w", "w0",
                "a_lora_w", "a0", "g_lora_w", "k_k", "k_a", "r_k", "ln_x_w", "ln_x_b", "w_proj_b", "w_out", "g_ffn",
                "w_ffn1", "w_ffn2", "g_final"]


def _shard_shape(shape, axis):
    s = list(shape)
    s[axis] //= N_DEV
    return tuple(s)


def _round_up(n, m):
    return (n + m - 1) // m * m


def _pick(n, target):
    if n <= target:
        return n
    best = None
    for t in range(LANES, target + 1, LANES):
        if n % t == 0:
            best = t
    assert best is not None, (n, target)
    return best


def _matmul(name, a, b, mode, out_dtype=F32, tm=1024, tn=1024, tk=1024, out_blocks=None):
    b_blocks = b.shape[0] if b.ndim == 3 else None
    bshape = b.shape if b.ndim == 2 else (b.shape[1], b.shape[0] * b.shape[2])
    if mode == "nn":
        (M, K), (K2, N) = a.shape, bshape
    elif mode == "nt":
        (M, K), (N, K2) = a.shape, bshape
    else:
        (K, M), (K2, N) = a.shape, bshape
    assert K == K2, (name, a.shape, b.shape)
    assert b_blocks is None or mode != "tn"
    assert out_blocks is None or mode == "tn"
    tn = min(tn, N // (out_blocks or 1), bshape[1] // b_blocks if (b_blocks and mode == "nn") else tn)
    tk = min(tk, bshape[1] // b_blocks if (b_blocks and mode == "nt") else tk)
    tm, tn, tk = _pick(M, tm), _pick(N, tn), _pick(K, tk)
    nk = K // tk
    dims = {"nn": (((1,), (0,)), ((), ())), "nt": (((1,), (1,)), ((), ())), "tn": (((0,), (0,)), ((), ()))}[mode]

    def body(a_ref, b_ref, o_ref, acc_ref):
        k = pl.program_id(2)

        @pl.when(k == 0)
        def _():
            acc_ref[...] = jnp.zeros_like(acc_ref)

        acc_ref[...] += lax.dot_general(a_ref[...].astype(BF16), b_ref[...].astype(BF16), dims,
                                        preferred_element_type=F32)

        @pl.when(k == nk - 1)
        def _():
            o_ref[...] = acc_ref[...].astype(o_ref.dtype)

    a_spec = {"nn": pl.BlockSpec((tm, tk), lambda i, j, k: (i, k)), "nt": pl.BlockSpec((tm, tk), lambda i, j, k: (i, k)),
              "tn": pl.BlockSpec((tk, tm), lambda i, j, k: (k, i))}[mode]
    b_spec = {"nn": pl.BlockSpec((tk, tn), lambda i, j, k: (k, j)), "nt": pl.BlockSpec((tn, tk), lambda i, j, k: (j, k)),
              "tn": pl.BlockSpec((tk, tn), lambda i, j, k: (k, j))}[mode]
    if b_blocks and mode == "nn":
        per = b.shape[2] // tn
        b_spec = pl.BlockSpec((None, tk, tn), lambda i, j, k: (j // per, k, j % per))
    elif b_blocks:
        per = b.shape[2] // tk
        b_spec = pl.BlockSpec((None, tn, tk), lambda i, j, k: (k // per, j, k % per))
    out_spec = pl.BlockSpec((tm, tn), lambda i, j, k: (i, j))
    out_shape = jax.ShapeDtypeStruct((M, N), out_dtype)
    if out_blocks:
        per_o = N // out_blocks // tn
        out_spec = pl.BlockSpec((None, tm, tn), lambda i, j, k: (j // per_o, i, j % per_o))
        out_shape = jax.ShapeDtypeStruct((out_blocks, M, N // out_blocks), out_dtype)
    return pl.pallas_call(
        body, name=name, grid=(M // tm, N // tn, nk), in_specs=[a_spec, b_spec],
        out_specs=out_spec, out_shape=out_shape, scratch_shapes=[pltpu.VMEM((tm, tn), F32)],
        compiler_params=pltpu.CompilerParams(dimension_semantics=("parallel", "parallel", "arbitrary"),
                                             vmem_limit_bytes=VMEM_LIMIT),
    )(a, b)


class Rows:
    def __init__(self, arr, width=None, cb=0):
        self.arr, self.width, self.cb = arr, (arr.shape[1] if width is None else width), cb


class Heads:
    def __init__(self, arr):
        self.arr = arr


class Halo:
    def __init__(self, arr, side):
        self.arr, self.side = arr, side


def _rows_call(name, fn, ins, consts, outs, accs=(), tm=256, with_pid=False):
    T = next(o.arr.shape[1] if isinstance(o, Heads) else o.arr.shape[0] for o in ins if not isinstance(o, Halo))
    tm = min(tm, T)
    n_tiles = T // tm
    n_in, n_c, n_out = len(ins), len(consts), len(outs)
    in_specs, args = [], []
    for o in ins:
        if isinstance(o, Rows):
            in_specs.append(pl.BlockSpec((tm, o.width), lambda i, cb=o.cb: (i, cb)))
        elif isinstance(o, Heads):
            in_specs.append(pl.BlockSpec((NH, tm, HN), lambda i: (0, i, 0)))
        else:
            w = o.arr.shape[1]
            if o.side < 0:
                in_specs.append(pl.BlockSpec((8, w), lambda i: (jnp.maximum(i * (tm // 8) - 1, 0), 0)))
            else:
                in_specs.append(pl.BlockSpec((8, w), lambda i: (jnp.minimum((i + 1) * (tm // 8), T // 8 - 1), 0)))
        args.append(o.arr)
    for c in consts:
        in_specs.append(pl.BlockSpec(c.shape, lambda i, nd=c.ndim: (0,) * nd))
        args.append(c)
    out_specs, out_shape = [], []
    for o in outs:
        if o[0] == "rows":
            out_specs.append(pl.BlockSpec((tm, o[1]), lambda i: (i, 0)))
            out_shape.append(jax.ShapeDtypeStruct((T, o[1]), o[2]))
        else:
            out_specs.append(pl.BlockSpec((NH, tm, HN), lambda i: (0, i, 0)))
            out_shape.append(jax.ShapeDtypeStruct((NH, T, HN), o[1]))
    for shape, dt in accs:
        out_specs.append(pl.BlockSpec(shape, lambda i, nd=len(shape): (0,) * nd))
        out_shape.append(jax.ShapeDtypeStruct(shape, dt))

    def body(*refs):
        i = pl.program_id(0)
        vals = []
        for o, r in zip(ins, refs[:n_in]):
            if isinstance(o, Heads):
                vals.append(jnp.concatenate([r[h] for h in range(NH)], axis=-1))
            else:
                vals.append(r[...])
        vals += [r[...] for r in refs[n_in:n_in + n_c]]
        res = fn(i, n_tiles, *vals) if with_pid else fn(*vals)
        out_refs = refs[n_in + n_c:]
        for o, r, v in zip(outs, out_refs[:n_out], res[:n_out]):
            if o[0] == "rows":
                r[...] = v.astype(r.dtype)
            else:
                for h in range(NH):
                    r[h] = v[:, h * HN:(h + 1) * HN].astype(r.dtype)
        if accs:
            @pl.when(i == 0)
            def _():
                for r in out_refs[n_out:]:
                    r[...] = jnp.zeros_like(r)

            for r, v in zip(out_refs[n_out:], res[n_out:]):
                r[...] += v.astype(r.dtype)

    res = pl.pallas_call(
        body, name=name, grid=(n_tiles,), in_specs=in_specs, out_specs=out_specs, out_shape=out_shape,
        compiler_params=pltpu.CompilerParams(dimension_semantics=("arbitrary",), vmem_limit_bytes=VMEM_LIMIT),
    )(*args)
    return res


def _rms(x, g):
    return x * lax.rsqrt(jnp.mean(x * x, axis=-1, keepdims=True) + NORM_EPS) * g


def _gelu(x):
    return 0.5 * x * (1.0 + lax.erf(x * 0.7071067811865476))


def _sigmoid(x):
    return 1.0 / (1.0 + jnp.exp(-x))


def _bdot(a, b):
    return jnp.dot(a.astype(BF16), b.astype(BF16), preferred_element_type=F32)


def _head_sum(x):
    r = lax.broadcasted_iota(jnp.int32, (LANES, LANES), 0) // HN
    c = lax.broadcasted_iota(jnp.int32, (LANES, LANES), 1) // HN
    ones = (r == c).astype(F32)
    parts = [jnp.dot(x[:, j * LANES:(j + 1) * LANES], ones, precision=HI, preferred_element_type=F32)
             for j in range(D // LANES)]
    return jnp.concatenate(parts, axis=-1)


def _sgu_fn(p, ln_w, ln_b, sw, sbt):
    z = _gelu(p)
    u, v = z[:, :D], z[:, D:]
    mu = jnp.mean(v, axis=-1, keepdims=True)
    var = jnp.mean(jnp.square(v - mu), axis=-1, keepdims=True)
    vn = (v - mu) * lax.rsqrt(var + LN_EPS) * ln_w + ln_b
    ri = lax.broadcasted_iota(jnp.int32, (SGU_C, SGU_C), 0)
    ci = lax.broadcasted_iota(jnp.int32, (SGU_C, SGU_C), 1)
    mask = (ci <= ri).astype(F32)
    dg = D // SGU_G
    parts = []
    for g in range(SGU_G):
        parts.append(_bdot(sw[g] * mask, vn[:, g * dg:(g + 1) * dg]) + sbt[:, g:g + 1])
    return u * jnp.concatenate(parts, axis=-1)


def _pre_fn(qr, qk, qv, qxw, qxa, qxg, wl, w0, al, a0, gl, k_k, k_a):
    w = -jax.nn.softplus(-(w0 + _bdot(jnp.tanh(qxw), wl))) - 0.5
    lw = -jnp.exp(w)
    aa = _sigmoid(a0 + _bdot(qxa, al))
    g = _bdot(_sigmoid(qxg), gl)
    kk = qk * k_k
    kk = kk / jnp.maximum(jnp.sqrt(_head_sum(kk * kk)), 1e-12)
    k2 = qk * (1.0 + (aa - 1.0) * k_a)
    return qr, lw, k2, qv, kk, aa, g


def _post_fn(o, r, k2, v, g, ln_w, ln_b, r_k):
    mu = _head_sum(o) * (1.0 / HN)
    d = o - mu
    var = _head_sum(d * d) * (1.0 / HN)
    on = d * lax.rsqrt(var + GN_EPS) * ln_w + ln_b
    bonus = _head_sum(r * k2 * r_k) * v
    return (on + bonus) * g


def _gate_fn(pg, ya, yb):
    return _sigmoid(pg[:, :D]) * ya + _sigmoid(pg[:, D:]) * yb


def _bmm(x, y, cx, cy):
    return lax.dot_general(x, y, (((cx,), (cy,)), ((0,), (0,))), precision=SCAN_PRECISION,
                           preferred_element_type=F32)


def _chunk_fn(S0, r, lw, k, v, kk, a):
    C, B = SCAN_C, SOLVE_B
    bmm = _bmm
    ti = lax.broadcasted_iota(jnp.int32, (C, C), 0)
    tj = lax.broadcasted_iota(jnp.int32, (C, C), 1)
    incl = (tj <= ti).astype(F32)
    strict = (tj < ti).astype(F32)
    same_blk = (ti // B == tj // B).astype(F32)
    cum = lax.dot_general(jnp.broadcast_to(incl, (NH, C, C)), lw, (((2,), (1,)), ((0,), (0,))),
                          precision=lax.Precision.HIGHEST, preferred_element_type=F32)
    g_in, g_ex, g_inv = jnp.exp(cum), jnp.exp(cum - lw), jnp.exp(-cum)
    kkt, rt = kk * g_ex, r * g_in
    bk = jnp.concatenate([kk * a * g_inv, k * g_inv], axis=1)
    A = bmm(kkt, bk, 2, 2)
    M = A[:, :, :C] * strict
    n_mask = jnp.concatenate([jnp.zeros((C, C), F32), strict], axis=1)
    zv = jnp.concatenate([jnp.zeros_like(v), v], axis=1)
    s0_side = bmm(jnp.concatenate([kkt, rt], axis=1), S0, 2, 2)
    y = s0_side[:, :C] + bmm(A * n_mask, zv, 2, 1)

    pows = [-(M * same_blk)]
    while 2 ** len(pows) < B:
        pows.append(bmm(pows[-1], pows[-1], 2, 1))

    def solve_diag(t):
        for Xp in reversed(pows):
            t = t + bmm(Xp, t, 2, 1)
        return t

    y = solve_diag(y)
    if C > B:
        lows = [-solve_diag(M * (1.0 - same_blk))]
        while 2 ** len(lows) < C // B:
            lows.append(bmm(lows[-1], lows[-1], 2, 1))
        for Lp in reversed(lows):
            y = y + bmm(Lp, y, 2, 1)
    z = jnp.concatenate([-y, v], axis=1)
    O = s0_side[:, C:] + bmm(bmm(rt, bk, 2, 2) * jnp.concatenate([incl, incl], axis=1), z, 2, 1)
    S1 = (S0 + bmm(z, bk, 1, 1)) * g_in[:, C - 1:C, :]
    return O, S1


def _scan_fwd(r, lw, k, v, kk, a, tb=256):
    T = r.shape[1]
    tb = min(tb, T)
    n_chunks = tb // SCAN_C

    def body(r_ref, lw_ref, k_ref, v_ref, kk_ref, a_ref, o_ref, s0_ref, s_ref):
        @pl.when(pl.program_id(0) == 0)
        def _():
            s_ref[...] = jnp.zeros_like(s_ref)

        def step(c, carry):
            sl = pl.ds(pl.multiple_of(c * SCAN_C, SCAN_C), SCAN_C)
            S0 = s_ref[...]
            s0_ref[c] = S0
            O, S1 = _chunk_fn(S0, r_ref[:, sl, :], lw_ref[:, sl, :], k_ref[:, sl, :], v_ref[:, sl, :],
                              kk_ref[:, sl, :], a_ref[:, sl, :])
            o_ref[:, sl, :] = O
            s_ref[...] = S1
            return carry

        lax.fori_loop(0, n_chunks, step, 0)

    hm = pl.BlockSpec((NH, tb, HN), lambda i: (0, i, 0))
    return pl.pallas_call(
        body, name="rwkv_scan_fwd", grid=(T // tb,), in_specs=[hm] * 6,
        out_specs=[hm, pl.BlockSpec((n_chunks, NH, HN, HN), lambda i: (i, 0, 0, 0))],
        out_shape=[jax.ShapeDtypeStruct((NH, T, HN), F32), jax.ShapeDtypeStruct((T // SCAN_C, NH, HN, HN), F32)],
        scratch_shapes=[pltpu.VMEM((NH, HN, HN), F32)],
        compiler_params=pltpu.CompilerParams(dimension_semantics=("arbitrary",), vmem_limit_bytes=VMEM_LIMIT),
    )(r, lw, k, v, kk, a)


def _scan_bwd(r, lw, k, v, kk, a, s0s, do, tb=128):
    T = r.shape[1]
    tb = min(tb, T)
    n_chunks = tb // SCAN_C
    nb = T // tb

    def body(r_ref, lw_ref, k_ref, v_ref, kk_ref, a_ref, s0_ref, do_ref, dr, dlw, dk, dv, dkk, da, ds_ref):
        @pl.when(pl.program_id(0) == 0)
        def _():
            ds_ref[...] = jnp.zeros_like(ds_ref)

        def step(j, carry):
            c = n_chunks - 1 - j
            sl = pl.ds(pl.multiple_of(c * SCAN_C, SCAN_C), SCAN_C)
            _, vjp = jax.vjp(_chunk_fn, s0_ref[c], r_ref[:, sl, :], lw_ref[:, sl, :], k_ref[:, sl, :],
                             v_ref[:, sl, :], kk_ref[:, sl, :], a_ref[:, sl, :])
            g = vjp((do_ref[:, sl, :], ds_ref[...]))
            ds_ref[...] = g[0]
            for ref, val in zip((dr, dlw, dk, dv, dkk, da), g[1:]):
                ref[:, sl, :] = val
            return carry

        lax.fori_loop(0, n_chunks, step, 0)

    hm = pl.BlockSpec((NH, tb, HN), lambda i: (0, nb - 1 - i, 0))
    return pl.pallas_call(
        body, name="rwkv_scan_bwd", grid=(nb,),
        in_specs=[hm] * 6 + [pl.BlockSpec((n_chunks, NH, HN, HN), lambda i: (nb - 1 - i, 0, 0, 0)), hm],
        out_specs=[hm] * 6, out_shape=[jax.ShapeDtypeStruct((NH, T, HN), F32)] * 6,
        scratch_shapes=[pltpu.VMEM((NH, HN, HN), F32)],
        compiler_params=pltpu.CompilerParams(dimension_semantics=("arbitrary",), vmem_limit_bytes=VMEM_LIMIT),
    )(r, lw, k, v, kk, a, s0s, do)


def _shift_down(i, p, prev8):
    first = jnp.where(i > 0, prev8[7:8, :], 0.0)
    row = lax.broadcasted_iota(jnp.int32, p.shape, 0)
    return jnp.where(row == 0, first, pltpu.roll(p, 1, axis=0))


def _mix_fwd(p, sb, tm=256):
    def fn(i, n, p, prev8, sb):
        return (p * sb[0:1] + _shift_down(i, p, prev8) * sb[1:2],)
    return _rows_call("shift_mix_fwd", fn, [Rows(p), Halo(p, -1)], [sb], [("rows", p.shape[1], F32)], tm=tm,
                      with_pid=True)[0]


def _mix_bwd(dq, p, sb, tm=256):
    def fn(i, n, dq, next8, p, prev8, sb):
        ps = _shift_down(i, p, prev8)
        d1 = dq * sb[1:2]
        last = jnp.where(i < n - 1, next8[0:1, :] * sb[1:2], 0.0)
        row = lax.broadcasted_iota(jnp.int32, dq.shape, 0)
        up = jnp.where(row == dq.shape[0] - 1, last, pltpu.roll(d1, dq.shape[0] - 1, axis=0))
        return (dq * sb[0:1] + up, jnp.sum(dq * p, axis=0, keepdims=True), jnp.sum(dq * ps, axis=0, keepdims=True))
    w = p.shape[1]
    return _rows_call("shift_mix_bwd", fn, [Rows(dq), Halo(dq, +1), Rows(p), Halo(p, -1)], [sb], [("rows", w, F32)],
                      accs=[((1, w), F32), ((1, w), F32)], tm=tm, with_pid=True)


def _local_step(x, target, W):
    G = {}
    a = _rows_call("norm_mix_fwd", lambda x, g: (_rms(x, g),), [Rows(x)], [W["g_mix"]], [("rows", D, BF16)])[0]
    p_sgu = _matmul("proj_sgu", a, W["w_sgu"], "nn")
    p_rw = _matmul("proj_rwkv", a, W["w_rw"], "nn")
    p_gate = _matmul("proj_gate", a, W["w_gate"], "nn")

    sgu_consts = [W["sgu_ln_w"], W["sgu_ln_b"], W["sgu_w"], W["sgu_bt"]]
    s = _rows_call("sgu_fwd", lambda *t: (_sgu_fn(*t),), [Rows(p_sgu)], sgu_consts, [("rows", D, BF16)], tm=SGU_C)[0]
    y_a = _matmul("proj_a", s, W["w_proj_a"], "nn")

    q = _mix_fwd(p_rw, W["sb"])
    q_ins = [Rows(q, D, 0), Rows(q, D, 1), Rows(q, D, 2), Rows(q, 128, 24), Rows(q, 128, 25), Rows(q, 256, 13)]
    pre_consts = [W["w_lora"], W["w0"], W["a_lora"], W["a0"], W["g_lora"], W["k_k"], W["k_a"]]
    r_h, lw_h, k_h, v_h, kk_h, a_h, g_gate = _rows_call(
        "rwkv_pre_fwd", _pre_fn, q_ins, pre_consts, [("heads", F32)] * 6 + [("rows", D, F32)], tm=128)
    o_h, s0s = _scan_fwd(r_h, lw_h, k_h, v_h, kk_h, a_h)
    post_ins = [Heads(o_h), Heads(r_h), Heads(k_h), Heads(v_h), Rows(g_gate)]
    post_consts = [W["ln_x_w"], W["ln_x_b"], W["r_k"]]
    z_b = _rows_call("rwkv_post_fwd", lambda *t: (_post_fn(*t),), post_ins, post_consts, [("rows", D, BF16)], tm=128)[0]
    y_b = _matmul("proj_b", z_b, W["w_proj_b"], "nn")

    gate_ins = [Rows(p_gate), Rows(y_a), Rows(y_b)]
    mixed = _rows_call("gate_fwd", lambda *t: (_gate_fn(*t),), gate_ins, [], [("rows", D, BF16)])[0]
    mo = _matmul("proj_out", mixed, W["w_out"], "nn")

    def res1(x, mo, g):
        h1 = x + mo
        return h1, _rms(h1, g)
    h1, f = _rows_call("residual_norm_fwd", res1, [Rows(x), Rows(mo)], [W["g_ffn"]],
                       [("rows", D, F32), ("rows", D, BF16)])
    u1 = _matmul("ffn_up", f, W["w_ffn1"], "nn")
    act = _rows_call("ffn_act_fwd", lambda u: (jnp.square(jnp.maximum(u, 0.0)),), [Rows(u1)], [],
                     [("rows", D_FF, BF16)])[0]
    ff = _matmul("ffn_down", act, W["w_ffn2"], "nn")

    def head(h1, ff, tgt, g):
        def f_(h1, ff, g):
            y = _rms(h1 + ff, g)
            return 0.5 * jnp.sum(jnp.mean(jnp.square(y - tgt), axis=-1))
        loss, (dh2, _, dg) = jax.value_and_grad(f_, argnums=(0, 1, 2))(h1, ff, g)
        return dh2, jnp.full((8, LANES), loss, F32), dg
    dh2, loss_acc, G["g_final"] = _rows_call("loss_head", head, [Rows(h1), Rows(ff), Rows(target)], [W["g_final"]],
                                             [("rows", D, F32)], accs=[((8, LANES), F32), ((1, D), F32)])

    d_act = _matmul("ffn_down_dx", dh2, W["w_ffn2"], "nt")
    G["w_ffn2"] = _matmul("ffn_down_dw", act, dh2, "tn", out_dtype=GRAD_PAYLOAD)
    d_u1 = _rows_call("ffn_act_bwd", lambda u, d: (d * 2.0 * jnp.maximum(u, 0.0),), [Rows(u1), Rows(d_act)], [],
                      [("rows", D_FF, BF16)])[0]
    d_f = _matmul("ffn_up_dx", d_u1, W["w_ffn1"], "nt")
    G["w_ffn1"] = _matmul("ffn_up_dw", f, d_u1, "tn", out_blocks=N_DEV, out_dtype=GRAD_PAYLOAD)

    def res1_bwd(x, mo, d_f, dh2, g):
        _, vjp = jax.vjp(lambda h, g: _rms(h, g), x + mo, g)
        dh, dg = vjp(d_f)
        return dh2 + dh, dg
    dh1, G["g_ffn"] = _rows_call("residual_norm_bwd", res1_bwd, [Rows(x), Rows(mo), Rows(d_f), Rows(dh2)],
                                 [W["g_ffn"]], [("rows", D, F32)], accs=[((1, D), F32)])
    d_mixed = _matmul("proj_out_dx", dh1, W["w_out"], "nt")
    G["w_out"] = _matmul("proj_out_dw", mixed, dh1, "tn", out_dtype=GRAD_PAYLOAD)

    def gate_bwd(pg, ya, yb, dm):
        _, vjp = jax.vjp(_gate_fn, pg, ya, yb)
        return vjp(dm)
    d_gate, d_ya, d_yb = _rows_call("gate_bwd", gate_bwd, gate_ins + [Rows(d_mixed)], [],
                                    [("rows", 2 * D, F32), ("rows", D, BF16), ("rows", D, BF16)])

    d_s = _matmul("proj_a_dx", d_ya, W["w_proj_a"], "nt")
    G["w_proj_a"] = _matmul("proj_a_dw", s, d_ya, "tn", out_dtype=GRAD_PAYLOAD)

    def sgu_bwd(p, ds, *c):
        _, vjp = jax.vjp(_sgu_fn, p, *c)
        return vjp(ds)
    d_p_sgu, G["sgu_ln_w"], G["sgu_ln_b"], G["sgu_w"], G["sgu_bt"] = _rows_call(
        "sgu_bwd", sgu_bwd, [Rows(p_sgu), Rows(d_s)], sgu_consts, [("rows", 2 * D, F32)],
        accs=[((1, D), F32), ((1, D), F32), ((SGU_G, SGU_C, SGU_C), F32), ((SGU_C, SGU_G), F32)], tm=SGU_C)

    d_zb = _matmul("proj_b_dx", d_yb, W["w_proj_b"], "nt")
    G["w_proj_b"] = _matmul("proj_b_dw", z_b, d_yb, "tn", out_dtype=GRAD_PAYLOAD)

    def post_bwd(o, r, k2, v, g, dz, *c):
        _, vjp = jax.vjp(_post_fn, o, r, k2, v, g, *c)
        return vjp(dz)
    do_h, dr1, dk1, dv1, d_g, G["ln_x_w"], G["ln_x_b"], G["r_k"] = _rows_call(
        "rwkv_post_bwd", post_bwd, post_ins + [Rows(d_zb)], post_consts, [("heads", F32)] * 4 + [("rows", D, F32)],
        accs=[((1, D), F32)] * 3, tm=128)
    dr2, dlw, dk2, dv2, dkk, daa = _scan_bwd(r_h, lw_h, k_h, v_h, kk_h, a_h, s0s, do_h)

    def pre_bwd(qr, qk, qv, qxw, qxa, qxg, dr1, dr2, dlw, dk1, dk2, dv1, dv2, dkk, daa, dg, *c):
        _, vjp = jax.vjp(_pre_fn, qr, qk, qv, qxw, qxa, qxg, *c)
        g = vjp((dr1 + dr2, dlw, dk1 + dk2, dv1 + dv2, dkk, daa, dg))
        dq = jnp.concatenate(g[:6], axis=-1)
        return (dq,) + tuple(g[6:])
    pre_b_ins = q_ins + [Heads(dr1), Heads(dr2), Heads(dlw), Heads(dk1), Heads(dk2), Heads(dv1), Heads(dv2),
                         Heads(dkk), Heads(daa), Rows(d_g)]
    d_q, G["w_lora"], G["w0"], G["a_lora"], G["a0"], G["g_lora"], G["k_k"], G["k_a"] = _rows_call(
        "rwkv_pre_bwd", pre_bwd, pre_b_ins, pre_consts, [("rows", RW_INT, F32)],
        accs=[((128, D), F32), ((1, D), F32), ((128, D), F32), ((1, D), F32), ((256, D), F32), ((1, D), F32),
              ((1, D), F32)], tm=128)
    d_p_rw, dsb0, dsb1 = _mix_bwd(d_q, p_rw, W["sb"])
    G["sb"] = jnp.concatenate([dsb0, dsb1], axis=0)

    G["w_sgu"] = _matmul("proj_sgu_dw", a, d_p_sgu, "tn")
    G["w_rw"] = _matmul("proj_rwkv_dw", a, d_p_rw, "tn")
    G["w_gate"] = _matmul("proj_gate_dw", a, d_gate, "tn")
    da1 = _matmul("proj_sgu_dx", d_p_sgu, W["w_sgu"], "nt")
    da2 = _matmul("proj_rwkv_dx", d_p_rw, W["w_rw"], "nt")
    da3 = _matmul("proj_gate_dx", d_gate, W["w_gate"], "nt")

    def norm1_bwd(x, da1, da2, da3, dh1, g):
        _, vjp = jax.vjp(_rms, x, g)
        dx, dg = vjp(da1 + da2 + da3)
        return dh1 + dx, dg
    dx, G["g_mix"] = _rows_call("norm_mix_bwd", norm1_bwd, [Rows(x), Rows(da1), Rows(da2), Rows(da3), Rows(dh1)],
                                [W["g_mix"]], [("rows", D, F32)], accs=[((1, D), F32)])
    return loss_acc[0, 0], dx, G


def _exchange(name, bufs, gather):
    nb = len(bufs)
    n_peers = N_DEV - 1

    def body(*refs):
        in_refs, out_refs = refs[:nb], refs[nb:2 * nb]
        send_sems, recv_sems, local_sems = refs[2 * nb:]
        x, y, c = lax.axis_index("x"), lax.axis_index("y"), lax.axis_index("c")
        me = 4 * x + 2 * y + c

        def src(b, dest):
            return in_refs[b] if gather else in_refs[b].at[dest]

        local = [pltpu.make_async_copy(src(b, me), out_refs[b].at[me], local_sems.at[b]) for b in range(nb)]
        for cp in local:
            cp.start()
        sends, recvs = [], []
        for kbits in range(1, N_DEV):
            px = 1 - x if kbits & 4 else x
            py = 1 - y if kbits & 2 else y
            pc = 1 - c if kbits & 1 else c
            peer = 4 * px + 2 * py + pc
            for b in range(nb):
                s = (kbits - 1) * nb + b
                sends.append(pltpu.make_async_remote_copy(
                    src_ref=src(b, peer), dst_ref=out_refs[b].at[me], send_sem=send_sems.at[s],
                    recv_sem=recv_sems.at[s], device_id=(px, py, pc), device_id_type=pl.DeviceIdType.MESH))
                recvs.append(pltpu.make_async_remote_copy(
                    src_ref=src(b, peer), dst_ref=out_refs[b].at[peer], send_sem=send_sems.at[s],
                    recv_sem=recv_sems.at[s], device_id=(px, py, pc), device_id_type=pl.DeviceIdType.MESH))
        for cp in sends:
            cp.start()
        for cp in recvs:
            cp.wait_recv()
        for cp in sends:
            cp.wait_send()
        for cp in local:
            cp.wait()

    out_shape = [jax.ShapeDtypeStruct((N_DEV,) + (b.shape if gather else b.shape[1:]), b.dtype) for b in bufs]
    any_spec = pl.BlockSpec(memory_space=pl.ANY)
    return pl.pallas_call(
        body, name=name, in_specs=[any_spec] * nb, out_specs=[any_spec] * nb, out_shape=out_shape,
        scratch_shapes=[pltpu.SemaphoreType.DMA((n_peers * nb,)), pltpu.SemaphoreType.DMA((n_peers * nb,)),
                        pltpu.SemaphoreType.DMA((nb,))],
    )(*bufs)


def _adamw(name, slots, w, m, v, tr=256):
    R, Wd = w.shape
    tr = tr if R % tr == 0 else R

    def body(s_ref, w_ref, m_ref, v_ref, g_out, d_out, m_out, v_out):
        g = s_ref[0].astype(F32)
        for j in range(1, N_DEV):
            g = g + s_ref[j].astype(F32)
        m_new = ADAM_B1 * m_ref[...] + (1.0 - ADAM_B1) * g
        v_new = ADAM_B2 * v_ref[...] + (1.0 - ADAM_B2) * jnp.square(g)
        m_hat = m_new / (1.0 - ADAM_B1 ** ADAM_STEP)
        v_hat = v_new / (1.0 - ADAM_B2 ** ADAM_STEP)
        g_out[...] = g
        d_out[...] = -ADAM_LR * (m_hat / (jnp.sqrt(v_hat) + ADAM_EPS) + ADAM_WD * w_ref[...])
        m_out[...] = m_new
        v_out[...] = v_new

    row = pl.BlockSpec((tr, Wd), lambda i: (i, 0))
    return pl.pallas_call(
        body, name=name, grid=(R // tr,), in_specs=[pl.BlockSpec((N_DEV, tr, Wd), lambda i: (0, i, 0)), row, row, row],
        out_specs=[row] * 4, out_shape=[jax.ShapeDtypeStruct((R, Wd), F32)] * 4,
        compiler_params=pltpu.CompilerParams(dimension_semantics=("parallel",), vmem_limit_bytes=VMEM_LIMIT),
    )(slots, w, m, v)


PACK_W = 1024
_SMALL_SIZES = [int(np.prod(s)) for _, s in REPLICATED]
_SMALL_ROWS = _round_up(_round_up(sum(_SMALL_SIZES) + PACK_W, PACK_W) // PACK_W, 8)
_LOSS_AT = sum(_SMALL_SIZES)
W_IN_SHARD = P_TOTAL // N_DEV


def _pack_rows(parts, rows, dtype):
    flat = jnp.concatenate([p.reshape(-1).astype(dtype) for p in parts])
    return jnp.pad(flat, (0, rows * PACK_W - flat.shape[0])).reshape(rows, PACK_W)


def _w_in_from_blocks(blocks, tm=256):
    o = 2 * D

    def body(g_ref, sgu_ref, rw_ref, gate_ref):
        full = jnp.concatenate([g_ref[j].astype(F32) for j in range(N_DEV)], axis=1)
        z = lambda n: jnp.zeros((tm, n), F32)
        sgu_ref[...] = full[:, :o].astype(BF16)
        c = o + 3 * D
        rw = jnp.concatenate([full[:, o:c], full[:, c:c + L_W], z(128 - L_W), full[:, c + L_W:c + L_W + L_A],
                              z(128 - L_A), full[:, c + L_W + L_A:o + C_B], z(256 - L_G)], axis=1)
        rw_ref[...] = rw.astype(BF16)
        gate_ref[...] = full[:, o + C_B:].astype(BF16)

    widths = (2 * D, RW_INT, 2 * D)
    return pl.pallas_call(
        body, name="w_in_from_blocks", grid=(D // tm,),
        in_specs=[pl.BlockSpec((N_DEV, tm, W_IN_SHARD), lambda i: (0, i, 0))],
        out_specs=[pl.BlockSpec((tm, w), lambda i: (i, 0)) for w in widths],
        out_shape=[jax.ShapeDtypeStruct((D, w), BF16) for w in widths],
        compiler_params=pltpu.CompilerParams(dimension_semantics=("parallel",), vmem_limit_bytes=VMEM_LIMIT),
    )(blocks)


def _w_in_to_blocks(g_sgu, g_rw, g_gate, tm=256):
    def body(sgu_ref, rw_ref, gate_ref, o_ref):
        rw = rw_ref[...]
        c = 3 * D
        full = jnp.concatenate([sgu_ref[...], rw[:, :c], rw[:, c:c + L_W], rw[:, c + 128:c + 128 + L_A],
                                rw[:, c + 256:c + 256 + L_G], gate_ref[...]], axis=1)
        for j in range(N_DEV):
            o_ref[j] = full[:, j * W_IN_SHARD:(j + 1) * W_IN_SHARD].astype(o_ref.dtype)

    widths = (2 * D, RW_INT, 2 * D)
    return pl.pallas_call(
        body, name="w_in_to_blocks", grid=(D // tm,),
        in_specs=[pl.BlockSpec((tm, w), lambda i: (i, 0)) for w in widths],
        out_specs=pl.BlockSpec((N_DEV, tm, W_IN_SHARD), lambda i: (0, i, 0)),
        out_shape=jax.ShapeDtypeStruct((N_DEV, D, W_IN_SHARD), GRAD_PAYLOAD),
        compiler_params=pltpu.CompilerParams(dimension_semantics=("parallel",), vmem_limit_bytes=VMEM_LIMIT),
    )(g_sgu, g_rw, g_gate)


def _cols_from_blocks(blk):
    return jnp.transpose(blk, (1, 0, 2)).reshape(blk.shape[1], -1)


def _cols_to_blocks(g):
    r, c = g.shape
    return jnp.transpose(g.reshape(r, N_DEV, c // N_DEV), (1, 0, 2))


def _gather_weights(shards):
    names = [n for n, _, _ in SHARDED]
    bufs = [shards[n] if n == "shift_b" else shards[n].astype(BF16) for n in names]
    got = dict(zip(names, _exchange("weight_all_gather", bufs, gather=True)))
    W = {}
    W["w_sgu"], W["w_rw"], W["w_gate"] = _w_in_from_blocks(got["w_in"])
    for n in ("w_proj_a", "w_proj_b", "w_out", "w_ffn2"):
        W[n] = got[n].reshape(-1, D)
    W["w_ffn1"] = got["w_ffn1"]
    z = lambda r, c, dt: jnp.zeros((r, c), dt)
    W["w_lora"] = jnp.concatenate([_cols_from_blocks(got["w_lora_w"]).astype(F32), z(128 - L_W, D, F32)], axis=0)
    W["a_lora"] = jnp.concatenate([_cols_from_blocks(got["a_lora_w"]).astype(F32), z(128 - L_A, D, F32)], axis=0)
    W["g_lora"] = jnp.concatenate([_cols_from_blocks(got["g_lora_w"]).astype(F32), z(256 - L_G, D, F32)], axis=0)
    sb = _cols_from_blocks(got["shift_b"])
    W["sb"] = jnp.concatenate([sb[:, :3 * D], sb[:, 3 * D:3 * D + L_W], z(2, 128 - L_W, F32),
                               sb[:, 3 * D + L_W:3 * D + L_W + L_A], z(2, 128 - L_A, F32),
                               sb[:, 3 * D + L_W + L_A:], z(2, 256 - L_G, F32)], axis=1)
    return W


def _replicated_weights(rep):
    W = {n: rep[n] for n in ("g_mix", "sgu_ln_w", "sgu_ln_b", "w0", "a0", "k_k", "k_a", "r_k", "ln_x_w", "ln_x_b",
                             "g_ffn")}
    W["g_final"] = rep["g_final"].reshape(1, D)
    W["sgu_w"] = rep["sgu_w"][0]
    W["sgu_bt"] = jnp.transpose(rep["sgu_b"][0])
    return W


def _grad_blocks(G):
    sbg = G["sb"]
    c = 3 * D
    sb = jnp.concatenate([sbg[:, :c], sbg[:, c:c + L_W], sbg[:, c + 128:c + 128 + L_A],
                          sbg[:, c + 256:c + 256 + L_G]], axis=1)
    blocks = {
        "w_in": _w_in_to_blocks(G["w_sgu"], G["w_rw"], G["w_gate"]),
        "shift_b": _cols_to_blocks(sb),
        "w_lora_w": _cols_to_blocks(G["w_lora"][:L_W]), "a_lora_w": _cols_to_blocks(G["a_lora"][:L_A]),
        "g_lora_w": _cols_to_blocks(G["g_lora"][:L_G]),
        "w_ffn1": G["w_ffn1"],
    }
    for n in ("w_proj_a", "w_proj_b", "w_out", "w_ffn2"):
        blocks[n] = G[n].reshape(N_DEV, -1, D)
    return blocks


def _replicated_grads(G):
    small = {n: G[n] for n in ("g_mix", "sgu_ln_w", "sgu_ln_b", "w0", "a0", "k_k", "k_a", "r_k", "ln_x_w", "ln_x_b",
                               "g_ffn", "g_final")}
    small["sgu_w"] = G["sgu_w"]
    small["sgu_b"] = jnp.transpose(G["sgu_bt"])
    return small


def kernel(x, g_mix, w_in, sgu_ln_w, sgu_ln_b, sgu_w, sgu_b, w_proj_a, shift_b, w_lora_w, w0, a_lora_w, a0, g_lora_w, k_k, k_a, r_k, ln_x_w, ln_x_b, w_proj_b, w_out, g_ffn, w_ffn1, w_ffn2, g_final, loss_target, m_g_mix, m_w_in, m_sgu_ln_w, m_sgu_ln_b, m_sgu_w, m_sgu_b, m_w_proj_a, m_shift_b, m_w_lora_w, m_w0, m_a_lora_w, m_a0, m_g_lora_w, m_k_k, m_k_a, m_r_k, m_ln_x_w, m_ln_x_b, m_w_proj_b, m_w_out, m_g_ffn, m_w_ffn1, m_w_ffn2, m_g_final, v_g_mix, v_w_in, v_sgu_ln_w, v_sgu_ln_b, v_sgu_w, v_sgu_b, v_w_proj_a, v_shift_b, v_w_lora_w, v_w0, v_a_lora_w, v_a0, v_g_lora_w, v_k_k, v_k_a, v_r_k, v_ln_x_w, v_ln_x_b, v_w_proj_b, v_w_out, v_g_ffn, v_w_ffn1, v_w_ffn2, v_g_final):
    env = dict(locals())
    weights = {n: env[n] for n in WEIGHT_ORDER}
    moms = {n: env["m_" + n] for n in WEIGHT_ORDER}
    vars_ = {n: env["v_" + n] for n in WEIGHT_ORDER}

    W = _gather_weights({n: weights[n][0] for n, _, _ in SHARDED})
    W.update(_replicated_weights({n: weights[n] for n, _ in REPLICATED}))
    loss_part, dx, G = _local_step(x[0], loss_target[0], W)

    names = [n for n, _, _ in SHARDED]
    blocks = _grad_blocks(G)
    slots = dict(zip(names, _exchange("grad_reduce_scatter", [blocks[n] for n in names], gather=False)))
    small = _replicated_grads(G)
    small_parts = [small[n] for n, _ in REPLICATED] + [jnp.full((PACK_W,), loss_part, F32)]
    small_slots = _exchange("grad_all_gather", [_pack_rows(small_parts, _SMALL_ROWS, F32)], gather=True)[0]

    outs = [dict(), dict(), dict(), dict()]
    for n in names:
        res = _adamw("adamw_" + n, slots[n], weights[n][0], moms[n][0], vars_[n][0])
        for k in range(4):
            outs[k][n] = res[k][None]

    def packed(d):
        return _pack_rows([d[n] for n, _ in REPLICATED], _SMALL_ROWS, F32)
    small_out = _adamw("adamw_replicated", small_slots, packed(weights), packed(moms), packed(vars_))
    for k in range(4):
        flat = small_out[k].reshape(-1)
        off = 0
        for (n, s), size in zip(REPLICATED, _SMALL_SIZES):
            outs[k][n] = flat[off:off + size].reshape(s)
            off += size
    loss = small_out[0].reshape(-1)[_LOSS_AT]
    return (loss, dx[None], *[outs[0][n] for n in WEIGHT_ORDER], *[outs[1][n] for n in WEIGHT_ORDER],
            *[outs[2][n] for n in WEIGHT_ORDER], *[outs[3][n] for n in WEIGHT_ORDER])
```

```python
import functools
import numpy as np
import jax
import jax.numpy as jnp
from jax import lax
from jax.experimental import pallas as pl
from jax.experimental.pallas import tpu as pltpu

F32 = jnp.float32
BF16 = jnp.bfloat16
HI = lax.Precision.HIGHEST

D = 1024
NH, HN = 16, 64
SGU_G, SGU_C = 8, 128
L_W, L_A, L_G = 64, 64, 160
C_B = 3 * D + L_W + L_A + L_G
P_TOTAL = 2 * D + C_B + 2 * D
D_FF = 4 * D
RW_INT = 3 * D + 128 + 128 + 256
NORM_EPS, LN_EPS, GN_EPS = 1e-6, 1e-5, 64e-5
N_DEV = 8
LANES = 128
SCAN_C = 64
SOLVE_B = 16
SCAN_PRECISION = lax.Precision.HIGH
GRAD_PAYLOAD = BF16
VMEM_LIMIT = 56 * 1024 * 1024

ADAM_LR, ADAM_B1, ADAM_B2, ADAM_EPS, ADAM_WD, ADAM_STEP = 0.001, 0.9, 0.999, 1e-08, 0.01, 10

SHARDED = [
    ("w_in", (D, P_TOTAL), 1), ("w_proj_a", (D, D), 0), ("shift_b", (2, C_B), 1), ("w_lora_w", (L_W, D), 1),
    ("a_lora_w", (L_A, D), 1), ("g_lora_w", (L_G, D), 1), ("w_proj_b", (D, D), 0), ("w_out", (D, D), 0),
    ("w_ffn1", (D, D_FF), 1), ("w_ffn2", (D_FF, D), 0),
]
REPLICATED = [
    ("g_mix", (1, D)), ("sgu_ln_w", (1, D)), ("sgu_ln_b", (1, D)), ("sgu_w", (1, SGU_G, SGU_C, SGU_C)),
    ("sgu_b", (1, SGU_G, SGU_C)), ("w0", (1, D)), ("a0", (1, D)), ("k_k", (1, D)), ("k_a", (1, D)), ("r_k", (1, D)),
    ("ln_x_w", (1, D)), ("ln_x_b", (1, D)), ("g_ffn", (1, D)), ("g_final", (D,)),
]
WEIGHT_ORDER = ["g_mix", "w_in", "sgu_ln_w", "sgu_ln_b", "sgu_w", "sgu_b", "w_proj_a", "shift_b", "w_lora_w", "w0",
                "a_lora_w", "a0", "g_lora_w", "k_k", "k_a", "r_k", "ln_x_w", "ln_x_b", "w_proj_b", "w_out", "g_ffn",
                "w_ffn1", "w_ffn2", "g_final"]


def _shard_shape(shape, axis):
    s = list(shape)
    s[axis] //= N_DEV
    return tuple(s)


def _round_up(n, m):
    return (n + m - 1) // m * m


def _pick(n, target):
    if n <= target:
        return n
    best = None
    for t in range(LANES, target + 1, LANES):
        if n % t == 0:
            best = t
    assert best is not None, (n, target)
    return best


def _matmul(name, a, b, mode, out_dtype=F32, tm=1024, tn=1024, tk=1024, out_blocks=None):
    b_blocks = b.shape[0] if b.ndim == 3 else None
    bshape = b.shape if b.ndim == 2 else (b.shape[1], b.shape[0] * b.shape[2])
    if mode == "nn":
        (M, K), (K2, N) = a.shape, bshape
    elif mode == "nt":
        (M, K), (N, K2) = a.shape, bshape
    else:
        (K, M), (K2, N) = a.shape, bshape
    assert K == K2, (name, a.shape, b.shape)
    assert b_blocks is None or mode != "tn"
    assert out_blocks is None or mode == "tn"
    tn = min(tn, N // (out_blocks or 1), bshape[1] // b_blocks if (b_blocks and mode == "nn") else tn)
    tk = min(tk, bshape[1] // b_blocks if (b_blocks and mode == "nt") else tk)
    tm, tn, tk = _pick(M, tm), _pick(N, tn), _pick(K, tk)
    nk = K // tk
    dims = {"nn": (((1,), (0,)), ((), ())), "nt": (((1,), (1,)), ((), ())), "tn": (((0,), (0,)), ((), ()))}[mode]

    def body(a_ref, b_ref, o_ref, acc_ref):
        k = pl.program_id(2)

        @pl.when(k == 0)
        def _():
            acc_ref[...] = jnp.zeros_like(acc_ref)

        acc_ref[...] += lax.dot_general(a_ref[...].astype(BF16), b_ref[...].astype(BF16), dims,
                                        preferred_element_type=F32)

        @pl.when(k == nk - 1)
        def _():
            o_ref[...] = acc_ref[...].astype(o_ref.dtype)

    a_spec = {"nn": pl.BlockSpec((tm, tk), lambda i, j, k: (i, k)), "nt": pl.BlockSpec((tm, tk), lambda i, j, k: (i, k)),
              "tn": pl.BlockSpec((tk, tm), lambda i, j, k: (k, i))}[mode]
    b_spec = {"nn": pl.BlockSpec((tk, tn), lambda i, j, k: (k, j)), "nt": pl.BlockSpec((tn, tk), lambda i, j, k: (j, k)),
              "tn": pl.BlockSpec((tk, tn), lambda i, j, k: (k, j))}[mode]
    if b_blocks and mode == "nn":
        per = b.shape[2] // tn
        b_spec = pl.BlockSpec((None, tk, tn), lambda i, j, k: (j // per, k, j % per))
    elif b_blocks:
        per = b.shape[2] // tk
        b_spec = pl.BlockSpec((None, tn, tk), lambda i, j, k: (k // per, j, k % per))
    out_spec = pl.BlockSpec((tm, tn), lambda i, j, k: (i, j))
    out_shape = jax.ShapeDtypeStruct((M, N), out_dtype)
    if out_blocks:
        per_o = N // out_blocks // tn
        out_spec = pl.BlockSpec((None, tm, tn), lambda i, j, k: (j // per_o, i, j % per_o))
        out_shape = jax.ShapeDtypeStruct((out_blocks, M, N // out_blocks), out_dtype)
    return pl.pallas_call(
        body, name=name, grid=(M // tm, N // tn, nk), in_specs=[a_spec, b_spec],
        out_specs=out_spec, out_shape=out_shape, scratch_shapes=[pltpu.VMEM((tm, tn), F32)],
        compiler_params=pltpu.CompilerParams(dimension_semantics=("parallel", "parallel", "arbitrary"),
                                             vmem_limit_bytes=VMEM_LIMIT),
    )(a, b)


class Rows:
    def __init__(self, arr, width=None, cb=0):
        self.arr, self.width, self.cb = arr, (arr.shape[1] if width is None else width), cb


class Heads:
    def __init__(self, arr):
        self.arr = arr


class Halo:
    def __init__(self, arr, side):
        self.arr, self.side = arr, side


def _rows_call(name, fn, ins, consts, outs, accs=(), tm=256, with_pid=False):
    T = next(o.arr.shape[1] if isinstance(o, Heads) else o.arr.shape[0] for o in ins if not isinstance(o, Halo))
    tm = min(tm, T)
    n_tiles = T // tm
    n_in, n_c, n_out = len(ins), len(consts), len(outs)
    in_specs, args = [], []
    for o in ins:
        if isinstance(o, Rows):
            in_specs.append(pl.BlockSpec((tm, o.width), lambda i, cb=o.cb: (i, cb)))
        elif isinstance(o, Heads):
            in_specs.append(pl.BlockSpec((NH, tm, HN), lambda i: (0, i, 0)))
        else:
            w = o.arr.shape[1]
            if o.side < 0:
                in_specs.append(pl.BlockSpec((8, w), lambda i: (jnp.maximum(i * (tm // 8) - 1, 0), 0)))
            else:
                in_specs.append(pl.BlockSpec((8, w), lambda i: (jnp.minimum((i + 1) * (tm // 8), T // 8 - 1), 0)))
        args.append(o.arr)
    for c in consts:
        in_specs.append(pl.BlockSpec(c.shape, lambda i, nd=c.ndim: (0,) * nd))
        args.append(c)
    out_specs, out_shape = [], []
    for o in outs:
        if o[0] == "rows":
            out_specs.append(pl.BlockSpec((tm, o[1]), lambda i: (i, 0)))
            out_shape.append(jax.ShapeDtypeStruct((T, o[1]), o[2]))
        else:
            out_specs.append(pl.BlockSpec((NH, tm, HN), lambda i: (0, i, 0)))
            out_shape.append(jax.ShapeDtypeStruct((NH, T, HN), o[1]))
    for shape, dt in accs:
        out_specs.append(pl.BlockSpec(shape, lambda i, nd=len(shape): (0,) * nd))
        out_shape.append(jax.ShapeDtypeStruct(shape, dt))

    def body(*refs):
        i = pl.program_id(0)
        vals = []
        for o, r in zip(ins, refs[:n_in]):
            if isinstance(o, Heads):
                vals.append(jnp.concatenate([r[h] for h in range(NH)], axis=-1))
            else:
                vals.append(r[...])
        vals += [r[...] for r in refs[n_in:n_in + n_c]]
        res = fn(i, n_tiles, *vals) if with_pid else fn(*vals)
        out_refs = refs[n_in + n_c:]
        for o, r, v in zip(outs, out_refs[:n_out], res[:n_out]):
            if o[0] == "rows":
                r[...] = v.astype(r.dtype)
            else:
                for h in range(NH):
                    r[h] = v[:, h * HN:(h + 1) * HN].astype(r.dtype)
        if accs:
            @pl.when(i == 0)
            def _():
                for r in out_refs[n_out:]:
                    r[...] = jnp.zeros_like(r)

            for r, v in zip(out_refs[n_out:], res[n_out:]):
                r[...] += v.astype(r.dtype)

    res = pl.pallas_call(
        body, name=name, grid=(n_tiles,), in_specs=in_specs, out_specs=out_specs, out_shape=out_shape,
        compiler_params=pltpu.CompilerParams(dimension_semantics=("arbitrary",), vmem_limit_bytes=VMEM_LIMIT),
    )(*args)
    return res


def _rms(x, g):
    return x * lax.rsqrt(jnp.mean(x * x, axis=-1, keepdims=True) + NORM_EPS) * g


def _gelu(x):
    return 0.5 * x * (1.0 + lax.erf(x * 0.7071067811865476))


def _sigmoid(x):
    return 1.0 / (1.0 + jnp.exp(-x))


def _bdot(a, b):
    return jnp.dot(a.astype(BF16), b.astype(BF16), preferred_element_type=F32)


def _head_sum(x):
    r = lax.broadcasted_iota(jnp.int32, (LANES, LANES), 0) // HN
    c = lax.broadcasted_iota(jnp.int32, (LANES, LANES), 1) // HN
    ones = (r == c).astype(F32)
    parts = [jnp.dot(x[:, j * LANES:(j + 1) * LANES], ones, precision=HI, preferred_element_type=F32)
             for j in range(D // LANES)]
    return jnp.concatenate(parts, axis=-1)


def _sgu_fn(p, ln_w, ln_b, sw, sbt):
    z = _gelu(p)
    u, v = z[:, :D], z[:, D:]
    mu = jnp.mean(v, axis=-1, keepdims=True)
    var = jnp.mean(jnp.square(v - mu), axis=-1, keepdims=True)
    vn = (v - mu) * lax.rsqrt(var + LN_EPS) * ln_w + ln_b
    ri = lax.broadcasted_iota(jnp.int32, (SGU_C, SGU_C), 0)
    ci = lax.broadcasted_iota(jnp.int32, (SGU_C, SGU_C), 1)
    mask = (ci <= ri).astype(F32)
    dg = D // SGU_G
    parts = []
    for g in range(SGU_G):
        parts.append(_bdot(sw[g] * mask, vn[:, g * dg:(g + 1) * dg]) + sbt[:, g:g + 1])
    return u * jnp.concatenate(parts, axis=-1)


def _pre_fn(qr, qk, qv, qxw, qxa, qxg, wl, w0, al, a0, gl, k_k, k_a):
    w = -jax.nn.softplus(-(w0 + _bdot(jnp.tanh(qxw), wl))) - 0.5
    lw = -jnp.exp(w)
    aa = _sigmoid(a0 + _bdot(qxa, al))
    g = _bdot(_sigmoid(qxg), gl)
    kk = qk * k_k
    kk = kk / jnp.maximum(jnp.sqrt(_head_sum(kk * kk)), 1e-12)
    k2 = qk * (1.0 + (aa - 1.0) * k_a)
    return qr, lw, k2, qv, kk, aa, g


def _post_fn(o, r, k2, v, g, ln_w, ln_b, r_k):
    mu = _head_sum(o) * (1.0 / HN)
    d = o - mu
    var = _head_sum(d * d) * (1.0 / HN)
    on = d * lax.rsqrt(var + GN_EPS) * ln_w + ln_b
    bonus = _head_sum(r * k2 * r_k) * v
    return (on + bonus) * g


def _gate_fn(pg, ya, yb):
    return _sigmoid(pg[:, :D]) * ya + _sigmoid(pg[:, D:]) * yb


def _bmm(x, y, cx, cy):
    return lax.dot_general(x, y, (((cx,), (cy,)), ((0,), (0,))), precision=SCAN_PRECISION,
                           preferred_element_type=F32)


def _chunk_fn(S0, r, lw, k, v, kk, a):
    C, B = SCAN_C, SOLVE_B
    bmm = _bmm
    ti = lax.broadcasted_iota(jnp.int32, (C, C), 0)
    tj = lax.broadcasted_iota(jnp.int32, (C, C), 1)
    incl = (tj <= ti).astype(F32)
    strict = (tj < ti).astype(F32)
    same_blk = (ti // B == tj // B).astype(F32)
    cum = lax.dot_general(jnp.broadcast_to(incl, (NH, C, C)), lw, (((2,), (1,)), ((0,), (0,))),
                          precision=lax.Precision.HIGHEST, preferred_element_type=F32)
    g_in, g_ex, g_inv = jnp.exp(cum), jnp.exp(cum - lw), jnp.exp(-cum)
    kkt, rt = kk * g_ex, r * g_in
    bk = jnp.concatenate([kk * a * g_inv, k * g_inv], axis=1)
    A = bmm(kkt, bk, 2, 2)
    M = A[:, :, :C] * strict
    n_mask = jnp.concatenate([jnp.zeros((C, C), F32), strict], axis=1)
    zv = jnp.concatenate([jnp.zeros_like(v), v], axis=1)
    s0_side = bmm(jnp.concatenate([kkt, rt], axis=1), S0, 2, 2)
    y = s0_side[:, :C] + bmm(A * n_mask, zv, 2, 1)

    pows = [-(M * same_blk)]
    while 2 ** len(pows) < B:
        pows.append(bmm(pows[-1], pows[-1], 2, 1))

    def solve_diag(t):
        for Xp in reversed(pows):
            t = t + bmm(Xp, t, 2, 1)
        return t

    y = solve_diag(y)
    if C > B:
        lows = [-solve_diag(M * (1.0 - same_blk))]
        while 2 ** len(lows) < C // B:
            lows.append(bmm(lows[-1], lows[-1], 2, 1))
        for Lp in reversed(lows):
            y = y + bmm(Lp, y, 2, 1)
    z = jnp.concatenate([-y, v], axis=1)
    O = s0_side[:, C:] + bmm(bmm(rt, bk, 2, 2) * jnp.concatenate([incl, incl], axis=1), z, 2, 1)
    S1 = (S0 + bmm(z, bk, 1, 1)) * g_in[:, C - 1:C, :]
    return O, S1


def _scan_fwd(r, lw, k, v, kk, a, ex=None, tb=256):
    T = r.shape[1]
    tb = min(tb, T)
    n_chunks = tb // SCAN_C
    nb = T // tb
    nx = ex.nb if ex else 0

    def body(*refs):
        r_ref, lw_ref, k_ref, v_ref, kk_ref, a_ref = refs[:6]
        x_in, (o_ref, s0_ref), x_out = refs[6:6 + nx], refs[6 + nx:8 + nx], refs[8 + nx:8 + 2 * nx]
        s_ref, sems = refs[8 + 2 * nx], refs[9 + 2 * nx:]

        @pl.when(pl.program_id(0) == 0)
        def _():
            s_ref[...] = jnp.zeros_like(s_ref)
            if ex:
                ex.start(x_in, x_out, sems)

        def step(c, carry):
            sl = pl.ds(pl.multiple_of(c * SCAN_C, SCAN_C), SCAN_C)
            S0 = s_ref[...]
            s0_ref[c] = S0
            O, S1 = _chunk_fn(S0, r_ref[:, sl, :], lw_ref[:, sl, :], k_ref[:, sl, :], v_ref[:, sl, :],
                              kk_ref[:, sl, :], a_ref[:, sl, :])
            o_ref[:, sl, :] = O
            s_ref[...] = S1
            return carry

        lax.fori_loop(0, n_chunks, step, 0)

        if ex:
            @pl.when(pl.program_id(0) == nb - 1)
            def _():
                ex.wait(x_in, x_out, sems)

    hm = pl.BlockSpec((NH, tb, HN), lambda i: (0, i, 0))
    res = pl.pallas_call(
        body, name="rwkv_scan_fwd", grid=(nb,), in_specs=[hm] * 6 + (ex.any_specs if ex else []),
        out_specs=[hm, pl.BlockSpec((n_chunks, NH, HN, HN), lambda i: (i, 0, 0, 0))] + (ex.any_specs if ex else []),
        out_shape=[jax.ShapeDtypeStruct((NH, T, HN), F32), jax.ShapeDtypeStruct((T // SCAN_C, NH, HN, HN), F32)]
        + (ex.out_shape if ex else []),
        scratch_shapes=[pltpu.VMEM((NH, HN, HN), F32)] + (ex.sem_shapes if ex else []),
        compiler_params=pltpu.CompilerParams(dimension_semantics=("arbitrary",), vmem_limit_bytes=VMEM_LIMIT),
    )(r, lw, k, v, kk, a, *(ex.bufs if ex else []))
    return res[0], res[1], list(res[2:])


def _scan_bwd(r, lw, k, v, kk, a, s0s, do, ex=None, tb=128):
    T = r.shape[1]
    tb = min(tb, T)
    n_chunks = tb // SCAN_C
    nb = T // tb
    nx = ex.nb if ex else 0

    def body(*refs):
        r_ref, lw_ref, k_ref, v_ref, kk_ref, a_ref, s0_ref, do_ref = refs[:8]
        x_in, (dr, dlw, dk, dv, dkk, da), x_out = refs[8:8 + nx], refs[8 + nx:14 + nx], refs[14 + nx:14 + 2 * nx]
        ds_ref, sems = refs[14 + 2 * nx], refs[15 + 2 * nx:]

        @pl.when(pl.program_id(0) == 0)
        def _():
            ds_ref[...] = jnp.zeros_like(ds_ref)
            if ex:
                ex.start(x_in, x_out, sems)

        def step(j, carry):
            c = n_chunks - 1 - j
            sl = pl.ds(pl.multiple_of(c * SCAN_C, SCAN_C), SCAN_C)
            _, vjp = jax.vjp(_chunk_fn, s0_ref[c], r_ref[:, sl, :], lw_ref[:, sl, :], k_ref[:, sl, :],
                             v_ref[:, sl, :], kk_ref[:, sl, :], a_ref[:, sl, :])
            g = vjp((do_ref[:, sl, :], ds_ref[...]))
            ds_ref[...] = g[0]
            for ref, val in zip((dr, dlw, dk, dv, dkk, da), g[1:]):
                ref[:, sl, :] = val
            return carry

        lax.fori_loop(0, n_chunks, step, 0)

        if ex:
            @pl.when(pl.program_id(0) == nb - 1)
            def _():
                ex.wait(x_in, x_out, sems)

    hm = pl.BlockSpec((NH, tb, HN), lambda i: (0, nb - 1 - i, 0))
    res = pl.pallas_call(
        body, name="rwkv_scan_bwd", grid=(nb,),
        in_specs=[hm] * 6 + [pl.BlockSpec((n_chunks, NH, HN, HN), lambda i: (nb - 1 - i, 0, 0, 0)), hm]
        + (ex.any_specs if ex else []),
        out_specs=[hm] * 6 + (ex.any_specs if ex else []),
        out_shape=[jax.ShapeDtypeStruct((NH, T, HN), F32)] * 6 + (ex.out_shape if ex else []),
        scratch_shapes=[pltpu.VMEM((NH, HN, HN), F32)] + (ex.sem_shapes if ex else []),
        compiler_params=pltpu.CompilerParams(dimension_semantics=("arbitrary",), vmem_limit_bytes=VMEM_LIMIT),
    )(r, lw, k, v, kk, a, s0s, do, *(ex.bufs if ex else []))
    return list(res[:6]), list(res[6:])


def _shift_down(i, p, prev8):
    first = jnp.where(i > 0, prev8[7:8, :], 0.0)
    row = lax.broadcasted_iota(jnp.int32, p.shape, 0)
    return jnp.where(row == 0, first, pltpu.roll(p, 1, axis=0))


def _mix_fwd(p, sb, tm=256):
    def fn(i, n, p, prev8, sb):
        return (p * sb[0:1] + _shift_down(i, p, prev8) * sb[1:2],)
    return _rows_call("shift_mix_fwd", fn, [Rows(p), Halo(p, -1)], [sb], [("rows", p.shape[1], F32)], tm=tm,
                      with_pid=True)[0]


def _mix_bwd(dq, p, sb, tm=256):
    def fn(i, n, dq, next8, p, prev8, sb):
        ps = _shift_down(i, p, prev8)
        d1 = dq * sb[1:2]
        last = jnp.where(i < n - 1, next8[0:1, :] * sb[1:2], 0.0)
        row = lax.broadcasted_iota(jnp.int32, dq.shape, 0)
        up = jnp.where(row == dq.shape[0] - 1, last, pltpu.roll(d1, dq.shape[0] - 1, axis=0))
        return (dq * sb[0:1] + up, jnp.sum(dq * p, axis=0, keepdims=True), jnp.sum(dq * ps, axis=0, keepdims=True))
    w = p.shape[1]
    return _rows_call("shift_mix_bwd", fn, [Rows(dq), Halo(dq, +1), Rows(p), Halo(p, -1)], [sb], [("rows", w, F32)],
                      accs=[((1, w), F32), ((1, w), F32)], tm=tm, with_pid=True)


def _local_step(x, target, W, late_weights=None, early_grads=None):
    G = {}
    a = _rows_call("norm_mix_fwd", lambda x, g: (_rms(x, g),), [Rows(x)], [W["g_mix"]], [("rows", D, BF16)])[0]
    p_sgu = _matmul("proj_sgu", a, W["w_sgu"], "nn")
    p_rw = _matmul("proj_rwkv", a, W["w_rw"], "nn")
    p_gate = _matmul("proj_gate", a, W["w_gate"], "nn")

    sgu_consts = [W["sgu_ln_w"], W["sgu_ln_b"], W["sgu_w"], W["sgu_bt"]]
    s = _rows_call("sgu_fwd", lambda *t: (_sgu_fn(*t),), [Rows(p_sgu)], sgu_consts, [("rows", D, BF16)], tm=SGU_C)[0]

    q = _mix_fwd(p_rw, W["sb"])
    q_ins = [Rows(q, D, 0), Rows(q, D, 1), Rows(q, D, 2), Rows(q, 128, 24), Rows(q, 128, 25), Rows(q, 256, 13)]
    pre_consts = [W["w_lora"], W["w0"], W["a_lora"], W["a0"], W["g_lora"], W["k_k"], W["k_a"]]
    r_h, lw_h, k_h, v_h, kk_h, a_h, g_gate = _rows_call(
        "rwkv_pre_fwd", _pre_fn, q_ins, pre_consts, [("heads", F32)] * 6 + [("rows", D, F32)], tm=128)
    o_h, s0s, got = _scan_fwd(r_h, lw_h, k_h, v_h, kk_h, a_h, ex=late_weights[0] if late_weights else None)
    if late_weights:
        W = {**W, **late_weights[1](got)}
    y_a = _matmul("proj_a", s, W["w_proj_a"], "nn")
    post_ins = [Heads(o_h), Heads(r_h), Heads(k_h), Heads(v_h), Rows(g_gate)]
    post_consts = [W["ln_x_w"], W["ln_x_b"], W["r_k"]]
    z_b = _rows_call("rwkv_post_fwd", lambda *t: (_post_fn(*t),), post_ins, post_consts, [("rows", D, BF16)], tm=128)[0]
    y_b = _matmul("proj_b", z_b, W["w_proj_b"], "nn")

    gate_ins = [Rows(p_gate), Rows(y_a), Rows(y_b)]
    mixed = _rows_call("gate_fwd", lambda *t: (_gate_fn(*t),), gate_ins, [], [("rows", D, BF16)])[0]
    mo = _matmul("proj_out", mixed, W["w_out"], "nn")

    def res1(x, mo, g):
        h1 = x + mo
        return h1, _rms(h1, g)
    h1, f = _rows_call("residual_norm_fwd", res1, [Rows(x), Rows(mo)], [W["g_ffn"]],
                       [("rows", D, F32), ("rows", D, BF16)])
    u1 = _matmul("ffn_up", f, W["w_ffn1"], "nn")
    act = _rows_call("ffn_act_fwd", lambda u: (jnp.square(jnp.maximum(u, 0.0)),), [Rows(u1)], [],
                     [("rows", D_FF, BF16)])[0]
    ff = _matmul("ffn_down", act, W["w_ffn2"], "nn")

    def head(h1, ff, tgt, g):
        def f_(h1, ff, g):
            y = _rms(h1 + ff, g)
            return 0.5 * jnp.sum(jnp.mean(jnp.square(y - tgt), axis=-1))
        loss, (dh2, _, dg) = jax.value_and_grad(f_, argnums=(0, 1, 2))(h1, ff, g)
        return dh2, jnp.full((8, LANES), loss, F32), dg
    dh2, loss_acc, G["g_final"] = _rows_call("loss_head", head, [Rows(h1), Rows(ff), Rows(target)], [W["g_final"]],
                                             [("rows", D, F32)], accs=[((8, LANES), F32), ((1, D), F32)])

    d_act = _matmul("ffn_down_dx", dh2, W["w_ffn2"], "nt")
    G["w_ffn2"] = _matmul("ffn_down_dw", act, dh2, "tn", out_dtype=GRAD_PAYLOAD)
    d_u1 = _rows_call("ffn_act_bwd", lambda u, d: (d * 2.0 * jnp.maximum(u, 0.0),), [Rows(u1), Rows(d_act)], [],
                      [("rows", D_FF, BF16)])[0]
    d_f = _matmul("ffn_up_dx", d_u1, W["w_ffn1"], "nt")
    G["w_ffn1"] = _matmul("ffn_up_dw", f, d_u1, "tn", out_blocks=N_DEV, out_dtype=GRAD_PAYLOAD)

    def res1_bwd(x, mo, d_f, dh2, g):
        _, vjp = jax.vjp(lambda h, g: _rms(h, g), x + mo, g)
        dh, dg = vjp(d_f)
        return dh2 + dh, dg
    dh1, G["g_ffn"] = _rows_call("residual_norm_bwd", res1_bwd, [Rows(x), Rows(mo), Rows(d_f), Rows(dh2)],
                                 [W["g_ffn"]], [("rows", D, F32)], accs=[((1, D), F32)])
    d_mixed = _matmul("proj_out_dx", dh1, W["w_out"], "nt")
    G["w_out"] = _matmul("proj_out_dw", mixed, dh1, "tn", out_dtype=GRAD_PAYLOAD)

    def gate_bwd(pg, ya, yb, dm):
        _, vjp = jax.vjp(_gate_fn, pg, ya, yb)
        return vjp(dm)
    d_gate, d_ya, d_yb = _rows_call("gate_bwd", gate_bwd, gate_ins + [Rows(d_mixed)], [],
                                    [("rows", 2 * D, F32), ("rows", D, BF16), ("rows", D, BF16)])

    d_s = _matmul("proj_a_dx", d_ya, W["w_proj_a"], "nt")
    G["w_proj_a"] = _matmul("proj_a_dw", s, d_ya, "tn", out_dtype=GRAD_PAYLOAD)

    def sgu_bwd(p, ds, *c):
        _, vjp = jax.vjp(_sgu_fn, p, *c)
        return vjp(ds)
    d_p_sgu, G["sgu_ln_w"], G["sgu_ln_b"], G["sgu_w"], G["sgu_bt"] = _rows_call(
        "sgu_bwd", sgu_bwd, [Rows(p_sgu), Rows(d_s)], sgu_consts, [("rows", 2 * D, F32)],
        accs=[((1, D), F32), ((1, D), F32), ((SGU_G, SGU_C, SGU_C), F32), ((SGU_C, SGU_G), F32)], tm=SGU_C)

    d_zb = _matmul("proj_b_dx", d_yb, W["w_proj_b"], "nt")
    G["w_proj_b"] = _matmul("proj_b_dw", z_b, d_yb, "tn", out_dtype=GRAD_PAYLOAD)

    def post_bwd(o, r, k2, v, g, dz, *c):
        _, vjp = jax.vjp(_post_fn, o, r, k2, v, g, *c)
        return vjp(dz)
    do_h, dr1, dk1, dv1, d_g, G["ln_x_w"], G["ln_x_b"], G["r_k"] = _rows_call(
        "rwkv_post_bwd", post_bwd, post_ins + [Rows(d_zb)], post_consts, [("heads", F32)] * 4 + [("rows", D, F32)],
        accs=[((1, D), F32)] * 3, tm=128)
    (dr2, dlw, dk2, dv2, dkk, daa), early = _scan_bwd(r_h, lw_h, k_h, v_h, kk_h, a_h, s0s, do_h,
                                                      ex=early_grads(G) if early_grads else None)

    def pre_bwd(qr, qk, qv, qxw, qxa, qxg, dr1, dr2, dlw, dk1, dk2, dv1, dv2, dkk, daa, dg, *c):
        _, vjp = jax.vjp(_pre_fn, qr, qk, qv, qxw, qxa, qxg, *c)
        g = vjp((dr1 + dr2, dlw, dk1 + dk2, dv1 + dv2, dkk, daa, dg))
        dq = jnp.concatenate(g[:6], axis=-1)
        return (dq,) + tuple(g[6:])
    pre_b_ins = q_ins + [Heads(dr1), Heads(dr2), Heads(dlw), Heads(dk1), Heads(dk2), Heads(dv1), Heads(dv2),
                         Heads(dkk), Heads(daa), Rows(d_g)]
    d_q, G["w_lora"], G["w0"], G["a_lora"], G["a0"], G["g_lora"], G["k_k"], G["k_a"] = _rows_call(
        "rwkv_pre_bwd", pre_bwd, pre_b_ins, pre_consts, [("rows", RW_INT, F32)],
        accs=[((128, D), F32), ((1, D), F32), ((128, D), F32), ((1, D), F32), ((256, D), F32), ((1, D), F32),
              ((1, D), F32)], tm=128)
    d_p_rw, dsb0, dsb1 = _mix_bwd(d_q, p_rw, W["sb"])
    G["sb"] = jnp.concatenate([dsb0, dsb1], axis=0)

    G["w_sgu"] = _matmul("proj_sgu_dw", a, d_p_sgu, "tn")
    G["w_rw"] = _matmul("proj_rwkv_dw", a, d_p_rw, "tn")
    G["w_gate"] = _matmul("proj_gate_dw", a, d_gate, "tn")
    da1 = _matmul("proj_sgu_dx", d_p_sgu, W["w_sgu"], "nt")
    da2 = _matmul("proj_rwkv_dx", d_p_rw, W["w_rw"], "nt")
    da3 = _matmul("proj_gate_dx", d_gate, W["w_gate"], "nt")

    def norm1_bwd(x, da1, da2, da3, dh1, g):
        _, vjp = jax.vjp(_rms, x, g)
        dx, dg = vjp(da1 + da2 + da3)
        return dh1 + dx, dg
    dx, G["g_mix"] = _rows_call("norm_mix_bwd", norm1_bwd, [Rows(x), Rows(da1), Rows(da2), Rows(da3), Rows(dh1)],
                                [W["g_mix"]], [("rows", D, F32)], accs=[((1, D), F32)])
    return loss_acc[0, 0], dx, G, early


class Exchange:
    def __init__(self, bufs, gathers):
        self.bufs, self.gathers, self.nb = list(bufs), list(gathers), len(bufs)
        self.any_specs = [pl.BlockSpec(memory_space=pl.ANY)] * self.nb
        self.out_shape = [jax.ShapeDtypeStruct((N_DEV,) + (b.shape if g else b.shape[1:]), b.dtype)
                          for b, g in zip(self.bufs, self.gathers)]
        n = (N_DEV - 1) * self.nb
        self.sem_shapes = [pltpu.SemaphoreType.DMA((n,)), pltpu.SemaphoreType.DMA((n,)),
                           pltpu.SemaphoreType.DMA((self.nb,))]

    def _copies(self, in_refs, out_refs, sems):
        send_sems, recv_sems, local_sems = sems
        x, y, c = lax.axis_index("x"), lax.axis_index("y"), lax.axis_index("c")
        me = 4 * x + 2 * y + c

        def src(b, dest):
            return in_refs[b] if self.gathers[b] else in_refs[b].at[dest]

        local = [pltpu.make_async_copy(src(b, me), out_refs[b].at[me], local_sems.at[b]) for b in range(self.nb)]
        sends, recvs = [], []
        for kbits in range(1, N_DEV):
            px = 1 - x if kbits & 4 else x
            py = 1 - y if kbits & 2 else y
            pc = 1 - c if kbits & 1 else c
            peer = 4 * px + 2 * py + pc
            for b in range(self.nb):
                s = (kbits - 1) * self.nb + b
                sends.append(pltpu.make_async_remote_copy(
                    src_ref=src(b, peer), dst_ref=out_refs[b].at[me], send_sem=send_sems.at[s],
                    recv_sem=recv_sems.at[s], device_id=(px, py, pc), device_id_type=pl.DeviceIdType.MESH))
                recvs.append(pltpu.make_async_remote_copy(
                    src_ref=src(b, peer), dst_ref=out_refs[b].at[peer], send_sem=send_sems.at[s],
                    recv_sem=recv_sems.at[s], device_id=(px, py, pc), device_id_type=pl.DeviceIdType.MESH))
        return local, sends, recvs

    def start(self, in_refs, out_refs, sems):
        local, sends, _ = self._copies(in_refs, out_refs, sems)
        for cp in local + sends:
            cp.start()

    def wait(self, in_refs, out_refs, sems):
        local, sends, recvs = self._copies(in_refs, out_refs, sems)
        for cp in recvs:
            cp.wait_recv()
        for cp in sends:
            cp.wait_send()
        for cp in local:
            cp.wait()


def _exchange(name, bufs, gather):
    ex = Exchange(bufs, gather if isinstance(gather, (list, tuple)) else [gather] * len(bufs))

    def body(*refs):
        in_refs, out_refs, sems = refs[:ex.nb], refs[ex.nb:2 * ex.nb], refs[2 * ex.nb:]
        ex.start(in_refs, out_refs, sems)
        ex.wait(in_refs, out_refs, sems)

    return pl.pallas_call(body, name=name, in_specs=ex.any_specs, out_specs=ex.any_specs, out_shape=ex.out_shape,
                          scratch_shapes=ex.sem_shapes)(*ex.bufs)


def _adamw(name, slots, w, m, v, tr=256):
    R, Wd = w.shape
    tr = tr if R % tr == 0 else R

    def body(s_ref, w_ref, m_ref, v_ref, g_out, d_out, m_out, v_out):
        g = s_ref[0].astype(F32)
        for j in range(1, N_DEV):
            g = g + s_ref[j].astype(F32)
        m_new = ADAM_B1 * m_ref[...] + (1.0 - ADAM_B1) * g
        v_new = ADAM_B2 * v_ref[...] + (1.0 - ADAM_B2) * jnp.square(g)
        m_hat = m_new / (1.0 - ADAM_B1 ** ADAM_STEP)
        v_hat = v_new / (1.0 - ADAM_B2 ** ADAM_STEP)
        g_out[...] = g
        d_out[...] = -ADAM_LR * (m_hat / (jnp.sqrt(v_hat) + ADAM_EPS) + ADAM_WD * w_ref[...])
        m_out[...] = m_new
        v_out[...] = v_new

    row = pl.BlockSpec((tr, Wd), lambda i: (i, 0))
    return pl.pallas_call(
        body, name=name, grid=(R // tr,), in_specs=[pl.BlockSpec((N_DEV, tr, Wd), lambda i: (0, i, 0)), row, row, row],
        out_specs=[row] * 4, out_shape=[jax.ShapeDtypeStruct((R, Wd), F32)] * 4,
        compiler_params=pltpu.CompilerParams(dimension_semantics=("parallel",), vmem_limit_bytes=VMEM_LIMIT),
    )(slots, w, m, v)


PACK_W = 1024
_SMALL_SIZES = [int(np.prod(s)) for _, s in REPLICATED]
_SMALL_ROWS = _round_up(_round_up(sum(_SMALL_SIZES) + PACK_W, PACK_W) // PACK_W, 8)
_LOSS_AT = sum(_SMALL_SIZES)
W_IN_SHARD = P_TOTAL // N_DEV


def _pack_rows(parts, rows, dtype):
    flat = jnp.concatenate([p.reshape(-1).astype(dtype) for p in parts])
    return jnp.pad(flat, (0, rows * PACK_W - flat.shape[0])).reshape(rows, PACK_W)


def _w_in_from_blocks(blocks, tm=256):
    o = 2 * D

    def body(g_ref, sgu_ref, rw_ref, gate_ref):
        full = jnp.concatenate([g_ref[j].astype(F32) for j in range(N_DEV)], axis=1)
        z = lambda n: jnp.zeros((tm, n), F32)
        sgu_ref[...] = full[:, :o].astype(BF16)
        c = o + 3 * D
        rw = jnp.concatenate([full[:, o:c], full[:, c:c + L_W], z(128 - L_W), full[:, c + L_W:c + L_W + L_A],
                              z(128 - L_A), full[:, c + L_W + L_A:o + C_B], z(256 - L_G)], axis=1)
        rw_ref[...] = rw.astype(BF16)
        gate_ref[...] = full[:, o + C_B:].astype(BF16)

    widths = (2 * D, RW_INT, 2 * D)
    return pl.pallas_call(
        body, name="w_in_from_blocks", grid=(D // tm,),
        in_specs=[pl.BlockSpec((N_DEV, tm, W_IN_SHARD), lambda i: (0, i, 0))],
        out_specs=[pl.BlockSpec((tm, w), lambda i: (i, 0)) for w in widths],
        out_shape=[jax.ShapeDtypeStruct((D, w), BF16) for w in widths],
        compiler_params=pltpu.CompilerParams(dimension_semantics=("parallel",), vmem_limit_bytes=VMEM_LIMIT),
    )(blocks)


def _w_in_to_blocks(g_sgu, g_rw, g_gate, tm=256):
    def body(sgu_ref, rw_ref, gate_ref, o_ref):
        rw = rw_ref[...]
        c = 3 * D
        full = jnp.concatenate([sgu_ref[...], rw[:, :c], rw[:, c:c + L_W], rw[:, c + 128:c + 128 + L_A],
                                rw[:, c + 256:c + 256 + L_G], gate_ref[...]], axis=1)
        for j in range(N_DEV):
            o_ref[j] = full[:, j * W_IN_SHARD:(j + 1) * W_IN_SHARD].astype(o_ref.dtype)

    widths = (2 * D, RW_INT, 2 * D)
    return pl.pallas_call(
        body, name="w_in_to_blocks", grid=(D // tm,),
        in_specs=[pl.BlockSpec((tm, w), lambda i: (i, 0)) for w in widths],
        out_specs=pl.BlockSpec((N_DEV, tm, W_IN_SHARD), lambda i: (0, i, 0)),
        out_shape=jax.ShapeDtypeStruct((N_DEV, D, W_IN_SHARD), GRAD_PAYLOAD),
        compiler_params=pltpu.CompilerParams(dimension_semantics=("parallel",), vmem_limit_bytes=VMEM_LIMIT),
    )(g_sgu, g_rw, g_gate)


def _cols_from_blocks(blk):
    return jnp.transpose(blk, (1, 0, 2)).reshape(blk.shape[1], -1)


def _cols_to_blocks(g):
    r, c = g.shape
    return jnp.transpose(g.reshape(r, N_DEV, c // N_DEV), (1, 0, 2))


FIRST_WEIGHTS = ["w_in", "shift_b", "w_lora_w", "a_lora_w", "g_lora_w"]
LATE_WEIGHTS = ["w_proj_a", "w_proj_b", "w_out", "w_ffn1", "w_ffn2"]


def _late_weights(shards):
    ex = Exchange([shards[n].astype(BF16) for n in LATE_WEIGHTS], [True] * len(LATE_WEIGHTS))

    def finish(results):
        got = dict(zip(LATE_WEIGHTS, results))
        W = {n: got[n].reshape(-1, D) for n in ("w_proj_a", "w_proj_b", "w_out", "w_ffn2")}
        W["w_ffn1"] = got["w_ffn1"]
        return W
    return ex, finish


def _gather_weights(shards):
    bufs = [shards[n] if n == "shift_b" else shards[n].astype(BF16) for n in FIRST_WEIGHTS]
    got = dict(zip(FIRST_WEIGHTS, _exchange("weight_all_gather", bufs, gather=True)))
    W = {}
    W["w_sgu"], W["w_rw"], W["w_gate"] = _w_in_from_blocks(got["w_in"])
    z = lambda r, c, dt: jnp.zeros((r, c), dt)
    W["w_lora"] = jnp.concatenate([_cols_from_blocks(got["w_lora_w"]).astype(F32), z(128 - L_W, D, F32)], axis=0)
    W["a_lora"] = jnp.concatenate([_cols_from_blocks(got["a_lora_w"]).astype(F32), z(128 - L_A, D, F32)], axis=0)
    W["g_lora"] = jnp.concatenate([_cols_from_blocks(got["g_lora_w"]).astype(F32), z(256 - L_G, D, F32)], axis=0)
    sb = _cols_from_blocks(got["shift_b"])
    W["sb"] = jnp.concatenate([sb[:, :3 * D], sb[:, 3 * D:3 * D + L_W], z(2, 128 - L_W, F32),
                               sb[:, 3 * D + L_W:3 * D + L_W + L_A], z(2, 128 - L_A, F32),
                               sb[:, 3 * D + L_W + L_A:], z(2, 256 - L_G, F32)], axis=1)
    return W


def _replicated_weights(rep):
    W = {n: rep[n] for n in ("g_mix", "sgu_ln_w", "sgu_ln_b", "w0", "a0", "k_k", "k_a", "r_k", "ln_x_w", "ln_x_b",
                             "g_ffn")}
    W["g_final"] = rep["g_final"].reshape(1, D)
    W["sgu_w"] = rep["sgu_w"][0]
    W["sgu_bt"] = jnp.transpose(rep["sgu_b"][0])
    return W


def _late_grad_blocks(G):
    blocks = {n: G[n].reshape(N_DEV, -1, D) for n in ("w_proj_a", "w_proj_b", "w_out", "w_ffn2")}
    blocks["w_ffn1"] = G["w_ffn1"]
    return Exchange([blocks[n] for n in LATE_WEIGHTS], [False] * len(LATE_WEIGHTS))


def _first_grad_blocks(G):
    sbg = G["sb"]
    c = 3 * D
    sb = jnp.concatenate([sbg[:, :c], sbg[:, c:c + L_W], sbg[:, c + 128:c + 128 + L_A],
                          sbg[:, c + 256:c + 256 + L_G]], axis=1)
    return {
        "w_in": _w_in_to_blocks(G["w_sgu"], G["w_rw"], G["w_gate"]),
        "shift_b": _cols_to_blocks(sb),
        "w_lora_w": _cols_to_blocks(G["w_lora"][:L_W]), "a_lora_w": _cols_to_blocks(G["a_lora"][:L_A]),
        "g_lora_w": _cols_to_blocks(G["g_lora"][:L_G]),
    }


def _replicated_grads(G):
    small = {n: G[n] for n in ("g_mix", "sgu_ln_w", "sgu_ln_b", "w0", "a0", "k_k", "k_a", "r_k", "ln_x_w", "ln_x_b",
                               "g_ffn", "g_final")}
    small["sgu_w"] = G["sgu_w"]
    small["sgu_b"] = jnp.transpose(G["sgu_bt"])
    return small


def kernel(x, g_mix, w_in, sgu_ln_w, sgu_ln_b, sgu_w, sgu_b, w_proj_a, shift_b, w_lora_w, w0, a_lora_w, a0, g_lora_w, k_k, k_a, r_k, ln_x_w, ln_x_b, w_proj_b, w_out, g_ffn, w_ffn1, w_ffn2, g_final, loss_target, m_g_mix, m_w_in, m_sgu_ln_w, m_sgu_ln_b, m_sgu_w, m_sgu_b, m_w_proj_a, m_shift_b, m_w_lora_w, m_w0, m_a_lora_w, m_a0, m_g_lora_w, m_k_k, m_k_a, m_r_k, m_ln_x_w, m_ln_x_b, m_w_proj_b, m_w_out, m_g_ffn, m_w_ffn1, m_w_ffn2, m_g_final, v_g_mix, v_w_in, v_sgu_ln_w, v_sgu_ln_b, v_sgu_w, v_sgu_b, v_w_proj_a, v_shift_b, v_w_lora_w, v_w0, v_a_lora_w, v_a0, v_g_lora_w, v_k_k, v_k_a, v_r_k, v_ln_x_w, v_ln_x_b, v_w_proj_b, v_w_out, v_g_ffn, v_w_ffn1, v_w_ffn2, v_g_final):
    env = dict(locals())
    weights = {n: env[n] for n in WEIGHT_ORDER}
    moms = {n: env["m_" + n] for n in WEIGHT_ORDER}
    vars_ = {n: env["v_" + n] for n in WEIGHT_ORDER}

    shards = {n: weights[n][0] for n, _, _ in SHARDED}
    W = _gather_weights(shards)
    W.update(_replicated_weights({n: weights[n] for n, _ in REPLICATED}))
    loss_part, dx, G, late_slots = _local_step(x[0], loss_target[0], W, late_weights=_late_weights(shards),
                                               early_grads=_late_grad_blocks)

    slots = dict(zip(LATE_WEIGHTS, late_slots))
    blocks = _first_grad_blocks(G)
    small = _replicated_grads(G)
    small_parts = [small[n] for n, _ in REPLICATED] + [jnp.full((PACK_W,), loss_part, F32)]
    res = _exchange("grad_exchange", [blocks[n] for n in FIRST_WEIGHTS] + [_pack_rows(small_parts, _SMALL_ROWS, F32)],
                    [False] * len(FIRST_WEIGHTS) + [True])
    slots.update(zip(FIRST_WEIGHTS, res[:-1]))
    small_slots = res[-1]

    outs = [dict(), dict(), dict(), dict()]
    for n, _, _ in SHARDED:
        res = _adamw("adamw_" + n, slots[n], weights[n][0], moms[n][0], vars_[n][0])
        for k in range(4):
            outs[k][n] = res[k][None]

    def packed(d):
        return _pack_rows([d[n] for n, _ in REPLICATED], _SMALL_ROWS, F32)
    small_out = _adamw("adamw_replicated", small_slots, packed(weights), packed(moms), packed(vars_))
    for k in range(4):
        flat = small_out[k].reshape(-1)
        off = 0
        for (n, s), size in zip(REPLICATED, _SMALL_SIZES):
            outs[k][n] = flat[off:off + size].reshape(s)
            off += size
    loss = small_out[0].reshape(-1)[_LOSS_AT]
    return (loss, dx[None], *[outs[0][n] for n in WEIGHT_ORDER], *[outs[1][n] for n in WEIGHT_ORDER],
            *[outs[2][n] for n in WEIGHT_ORDER], *[outs[3][n] for n in WEIGHT_ORDER])
```

```python
import functools
import numpy as np
import jax
import jax.numpy as jnp
from jax import lax
from jax.experimental import pallas as pl
from jax.experimental.pallas import tpu as pltpu

F32 = jnp.float32
BF16 = jnp.bfloat16

D = 1024
NH, HN = 16, 64
SGU_G, SGU_C = 8, 128
L_W, L_A, L_G = 64, 64, 160
C_B = 3 * D + L_W + L_A + L_G
P_TOTAL = 2 * D + C_B + 2 * D
D_FF = 4 * D
RW_INT = 3 * D + 128 + 128 + 256
NORM_EPS, LN_EPS, GN_EPS = 1e-6, 1e-5, 64e-5
N_DEV = 8
LANES = 128
SCAN_C = 64
SOLVE_B = 16
SCAN_PRECISION = lax.Precision.HIGH
GRAD_PAYLOAD = BF16
VMEM_LIMIT = 56 * 1024 * 1024

ADAM_LR, ADAM_B1, ADAM_B2, ADAM_EPS, ADAM_WD, ADAM_STEP = 0.001, 0.9, 0.999, 1e-08, 0.01, 10

SHARDED = [
    ("w_in", (D, P_TOTAL), 1), ("w_proj_a", (D, D), 0), ("shift_b", (2, C_B), 1), ("w_lora_w", (L_W, D), 1),
    ("a_lora_w", (L_A, D), 1), ("g_lora_w", (L_G, D), 1), ("w_proj_b", (D, D), 0), ("w_out", (D, D), 0),
    ("w_ffn1", (D, D_FF), 1), ("w_ffn2", (D_FF, D), 0),
]
REPLICATED = [
    ("g_mix", (1, D)), ("sgu_ln_w", (1, D)), ("sgu_ln_b", (1, D)), ("sgu_w", (1, SGU_G, SGU_C, SGU_C)),
    ("sgu_b", (1, SGU_G, SGU_C)), ("w0", (1, D)), ("a0", (1, D)), ("k_k", (1, D)), ("k_a", (1, D)), ("r_k", (1, D)),
    ("ln_x_w", (1, D)), ("ln_x_b", (1, D)), ("g_ffn", (1, D)), ("g_final", (D,)),
]
WEIGHT_ORDER = ["g_mix", "w_in", "sgu_ln_w", "sgu_ln_b", "sgu_w", "sgu_b", "w_proj_a", "shift_b", "w_lora_w", "w0",
                "a_lora_w", "a0", "g_lora_w", "k_k", "k_a", "r_k", "ln_x_w", "ln_x_b", "w_proj_b", "w_out", "g_ffn",
                "w_ffn1", "w_ffn2", "g_final"]


def _shard_shape(shape, axis):
    s = list(shape)
    s[axis] //= N_DEV
    return tuple(s)


def _round_up(n, m):
    return (n + m - 1) // m * m


def _pick(n, target):
    if n <= target:
        return n
    best = None
    for t in range(LANES, target + 1, LANES):
        if n % t == 0:
            best = t
    assert best is not None, (n, target)
    return best


def _matmul(name, a, b, mode, out_dtype=F32, tm=1024, tn=1024, tk=1024, out_blocks=None):
    b_blocks = b.shape[0] if b.ndim == 3 else None
    bshape = b.shape if b.ndim == 2 else (b.shape[1], b.shape[0] * b.shape[2])
    if mode == "nn":
        (M, K), (K2, N) = a.shape, bshape
    elif mode == "nt":
        (M, K), (N, K2) = a.shape, bshape
    else:
        (K, M), (K2, N) = a.shape, bshape
    assert K == K2, (name, a.shape, b.shape)
    assert b_blocks is None or mode != "tn"
    assert out_blocks is None or mode == "tn"
    tn = min(tn, N // (out_blocks or 1), bshape[1] // b_blocks if (b_blocks and mode == "nn") else tn)
    tk = min(tk, bshape[1] // b_blocks if (b_blocks and mode == "nt") else tk)
    tm, tn, tk = _pick(M, tm), _pick(N, tn), _pick(K, tk)
    nk = K // tk
    dims = {"nn": (((1,), (0,)), ((), ())), "nt": (((1,), (1,)), ((), ())), "tn": (((0,), (0,)), ((), ()))}[mode]

    def body(a_ref, b_ref, o_ref, acc_ref):
        k = pl.program_id(2)

        @pl.when(k == 0)
        def _():
            acc_ref[...] = jnp.zeros_like(acc_ref)

        acc_ref[...] += lax.dot_general(a_ref[...].astype(BF16), b_ref[...].astype(BF16), dims,
                                        preferred_element_type=F32)

        @pl.when(k == nk - 1)
        def _():
            o_ref[...] = acc_ref[...].astype(o_ref.dtype)

    a_spec = {"nn": pl.BlockSpec((tm, tk), lambda i, j, k: (i, k)), "nt": pl.BlockSpec((tm, tk), lambda i, j, k: (i, k)),
              "tn": pl.BlockSpec((tk, tm), lambda i, j, k: (k, i))}[mode]
    b_spec = {"nn": pl.BlockSpec((tk, tn), lambda i, j, k: (k, j)), "nt": pl.BlockSpec((tn, tk), lambda i, j, k: (j, k)),
              "tn": pl.BlockSpec((tk, tn), lambda i, j, k: (k, j))}[mode]
    if b_blocks and mode == "nn":
        per = b.shape[2] // tn
        b_spec = pl.BlockSpec((None, tk, tn), lambda i, j, k: (j // per, k, j % per))
    elif b_blocks:
        per = b.shape[2] // tk
        b_spec = pl.BlockSpec((None, tn, tk), lambda i, j, k: (k // per, j, k % per))
    out_spec = pl.BlockSpec((tm, tn), lambda i, j, k: (i, j))
    out_shape = jax.ShapeDtypeStruct((M, N), out_dtype)
    if out_blocks:
        per_o = N // out_blocks // tn
        out_spec = pl.BlockSpec((None, tm, tn), lambda i, j, k: (j // per_o, i, j % per_o))
        out_shape = jax.ShapeDtypeStruct((out_blocks, M, N // out_blocks), out_dtype)
    return pl.pallas_call(
        body, name=name, grid=(M // tm, N // tn, nk), in_specs=[a_spec, b_spec],
        out_specs=out_spec, out_shape=out_shape, scratch_shapes=[pltpu.VMEM((tm, tn), F32)],
        compiler_params=pltpu.CompilerParams(dimension_semantics=("parallel", "parallel", "arbitrary"),
                                             vmem_limit_bytes=VMEM_LIMIT),
    )(a, b)


class Rows:
    def __init__(self, arr, width=None, cb=0):
        self.arr, self.width, self.cb = arr, (arr.shape[1] if width is None else width), cb


class Heads:
    def __init__(self, arr):
        self.arr = arr


class Halo:
    def __init__(self, arr, side):
        self.arr, self.side = arr, side


def _rows_call(name, fn, ins, consts, outs, accs=(), tm=256, with_pid=False):
    T = next(o.arr.shape[1] if isinstance(o, Heads) else o.arr.shape[0] for o in ins if not isinstance(o, Halo))
    tm = min(tm, T)
    n_tiles = T // tm
    n_in, n_c, n_out = len(ins), len(consts), len(outs)
    in_specs, args = [], []
    for o in ins:
        if isinstance(o, Rows):
            in_specs.append(pl.BlockSpec((tm, o.width), lambda i, cb=o.cb: (i, cb)))
        elif isinstance(o, Heads):
            in_specs.append(pl.BlockSpec((NH, tm, HN), lambda i: (0, i, 0)))
        else:
            w = o.arr.shape[1]
            if o.side < 0:
                in_specs.append(pl.BlockSpec((8, w), lambda i: (jnp.maximum(i * (tm // 8) - 1, 0), 0)))
            else:
                in_specs.append(pl.BlockSpec((8, w), lambda i: (jnp.minimum((i + 1) * (tm // 8), T // 8 - 1), 0)))
        args.append(o.arr)
    for c in consts:
        in_specs.append(pl.BlockSpec(c.shape, lambda i, nd=c.ndim: (0,) * nd))
        args.append(c)
    out_specs, out_shape = [], []
    for o in outs:
        if o[0] == "rows":
            out_specs.append(pl.BlockSpec((tm, o[1]), lambda i: (i, 0)))
            out_shape.append(jax.ShapeDtypeStruct((T, o[1]), o[2]))
        else:
            out_specs.append(pl.BlockSpec((NH, tm, HN), lambda i: (0, i, 0)))
            out_shape.append(jax.ShapeDtypeStruct((NH, T, HN), o[1]))
    for shape, dt in accs:
        out_specs.append(pl.BlockSpec(shape, lambda i, nd=len(shape): (0,) * nd))
        out_shape.append(jax.ShapeDtypeStruct(shape, dt))

    def body(*refs):
        i = pl.program_id(0)
        vals = []
        vals = [r[...] for r in refs[:n_in + n_c]]
        res = fn(i, n_tiles, *vals) if with_pid else fn(*vals)
        out_refs = refs[n_in + n_c:]
        for r, v in zip(out_refs[:n_out], res[:n_out]):
            r[...] = v.astype(r.dtype)
        if accs:
            @pl.when(i == 0)
            def _():
                for r in out_refs[n_out:]:
                    r[...] = jnp.zeros_like(r)

            for r, v in zip(out_refs[n_out:], res[n_out:]):
                r[...] += v.astype(r.dtype)

    res = pl.pallas_call(
        body, name=name, grid=(n_tiles,), in_specs=in_specs, out_specs=out_specs, out_shape=out_shape,
        compiler_params=pltpu.CompilerParams(dimension_semantics=("arbitrary",), vmem_limit_bytes=VMEM_LIMIT),
    )(*args)
    return res


def _rms(x, g):
    return x * lax.rsqrt(jnp.mean(x * x, axis=-1, keepdims=True) + NORM_EPS) * g


def _gelu(x):
    return 0.5 * x * (1.0 + lax.erf(x * 0.7071067811865476))


def _sigmoid(x):
    return 1.0 / (1.0 + jnp.exp(-x))


def _bdot(a, b):
    return jnp.dot(a.astype(BF16), b.astype(BF16), preferred_element_type=F32)


def _to_heads(x):
    return jnp.concatenate([x[:, h * HN:(h + 1) * HN][None] for h in range(NH)], axis=0)


def _from_heads(xh):
    return jnp.concatenate([xh[h] for h in range(NH)], axis=-1)


def _sgu_fn(p, ln_w, ln_b, sw, sbt):
    z = _gelu(p)
    u, v = z[:, :D], z[:, D:]
    mu = jnp.mean(v, axis=-1, keepdims=True)
    var = jnp.mean(jnp.square(v - mu), axis=-1, keepdims=True)
    vn = (v - mu) * lax.rsqrt(var + LN_EPS) * ln_w + ln_b
    ri = lax.broadcasted_iota(jnp.int32, (SGU_C, SGU_C), 0)
    ci = lax.broadcasted_iota(jnp.int32, (SGU_C, SGU_C), 1)
    mask = (ci <= ri).astype(F32)
    dg = D // SGU_G
    parts = []
    for g in range(SGU_G):
        parts.append(_bdot(sw[g] * mask, vn[:, g * dg:(g + 1) * dg]) + sbt[:, g:g + 1])
    return u * jnp.concatenate(parts, axis=-1)


def _pre_fn(qr, qk, qv, qxw, qxa, qxg, wl, w0, al, a0, gl, k_k, k_a):
    w = -jax.nn.softplus(-(w0 + _bdot(jnp.tanh(qxw), wl))) - 0.5
    lw = -jnp.exp(w)
    aa = _sigmoid(a0 + _bdot(qxa, al))
    g = _bdot(_sigmoid(qxg), gl)
    kk = _to_heads(qk * k_k)
    kk = kk / jnp.maximum(jnp.sqrt(jnp.sum(kk * kk, axis=-1, keepdims=True)), 1e-12)
    k2 = qk * (1.0 + (aa - 1.0) * k_a)
    return _to_heads(qr), _to_heads(lw), _to_heads(k2), _to_heads(qv), kk, _to_heads(aa), g


def _post_fn(o, r, k2, v, g, ln_w, ln_b, r_k):
    mu = jnp.mean(o, axis=-1, keepdims=True)
    d = o - mu
    var = jnp.mean(d * d, axis=-1, keepdims=True)
    on = d * lax.rsqrt(var + GN_EPS) * ln_w + ln_b
    bonus = jnp.sum(r * k2 * r_k, axis=-1, keepdims=True) * v
    return _from_heads(on + bonus) * g


def _gate_fn(pg, ya, yb):
    return _sigmoid(pg[:, :D]) * ya + _sigmoid(pg[:, D:]) * yb


def _bmm(x, y, cx, cy):
    return lax.dot_general(x, y, (((cx,), (cy,)), ((0,), (0,))), precision=SCAN_PRECISION,
                           preferred_element_type=F32)


def _unit_lower_inverse(M):
    C = M.shape[1]
    ti = lax.broadcasted_iota(jnp.int32, (C, C), 0)
    tj = lax.broadcasted_iota(jnp.int32, (C, C), 1)
    eye = (ti == tj).astype(F32)
    same = lambda b: (ti // b == tj // b).astype(F32)
    X = -(M * same(SOLVE_B))
    inv = eye + X
    span = 1
    while 2 * span < SOLVE_B:
        X = _bmm(X, X, 2, 1)
        inv = inv + _bmm(inv, X, 2, 1)
        span *= 2
    b = SOLVE_B
    while b < C:
        low = M * (same(2 * b) - same(b))
        inv = inv - _bmm(_bmm(inv, low, 2, 1), inv, 2, 1)
        b *= 2
    return inv


@jax.custom_vjp
def _unit_lower_solve(M, y):
    return _bmm(_unit_lower_inverse(M), y, 2, 1)


def _unit_lower_solve_fwd(M, y):
    inv = _unit_lower_inverse(M)
    u = _bmm(inv, y, 2, 1)
    return u, (inv, u)


def _unit_lower_solve_bwd(res, du):
    inv, u = res
    dy = _bmm(inv, du, 1, 1)
    return -_bmm(dy, u, 2, 2), dy


_unit_lower_solve.defvjp(_unit_lower_solve_fwd, _unit_lower_solve_bwd)


def _chunk_fn(S0, r, lw, k, v, kk, a):
    C = SCAN_C
    bmm = _bmm
    ti = lax.broadcasted_iota(jnp.int32, (C, C), 0)
    tj = lax.broadcasted_iota(jnp.int32, (C, C), 1)
    incl = (tj <= ti).astype(F32)
    strict = (tj < ti).astype(F32)
    cum = lax.dot_general(jnp.broadcast_to(incl, (NH, C, C)), lw, (((2,), (1,)), ((0,), (0,))),
                          precision=lax.Precision.HIGHEST, preferred_element_type=F32)
    g_in, g_ex, g_inv = jnp.exp(cum), jnp.exp(cum - lw), jnp.exp(-cum)
    kkt, rt = kk * g_ex, r * g_in
    bk = jnp.concatenate([kk * a * g_inv, k * g_inv], axis=1)
    A = bmm(kkt, bk, 2, 2)
    M = A[:, :, :C] * strict
    n_mask = jnp.concatenate([jnp.zeros((C, C), F32), strict], axis=1)
    zv = jnp.concatenate([jnp.zeros_like(v), v], axis=1)
    s0_side = bmm(jnp.concatenate([kkt, rt], axis=1), S0, 2, 2)
    y = _unit_lower_solve(M, s0_side[:, :C] + bmm(A * n_mask, zv, 2, 1))
    z = jnp.concatenate([-y, v], axis=1)
    O = s0_side[:, C:] + bmm(bmm(rt, bk, 2, 2) * jnp.concatenate([incl, incl], axis=1), z, 2, 1)
    S1 = (S0 + bmm(z, bk, 1, 1)) * g_in[:, C - 1:C, :]
    return O, S1


def _scan_fwd(r, lw, k, v, kk, a, ex=None, tb=256):
    T = r.shape[1]
    tb = min(tb, T)
    n_chunks = tb // SCAN_C
    nb = T // tb
    nx = ex.nb if ex else 0

    def body(*refs):
        r_ref, lw_ref, k_ref, v_ref, kk_ref, a_ref = refs[:6]
        x_in, (o_ref, s0_ref), x_out = refs[6:6 + nx], refs[6 + nx:8 + nx], refs[8 + nx:8 + 2 * nx]
        s_ref, sems = refs[8 + 2 * nx], refs[9 + 2 * nx:]

        @pl.when(pl.program_id(0) == 0)
        def _():
            s_ref[...] = jnp.zeros_like(s_ref)
            if ex:
                ex.start(x_in, x_out, sems)

        def step(c, carry):
            sl = pl.ds(pl.multiple_of(c * SCAN_C, SCAN_C), SCAN_C)
            S0 = s_ref[...]
            s0_ref[c] = S0
            O, S1 = _chunk_fn(S0, r_ref[:, sl, :], lw_ref[:, sl, :], k_ref[:, sl, :], v_ref[:, sl, :],
                              kk_ref[:, sl, :], a_ref[:, sl, :])
            o_ref[:, sl, :] = O
            s_ref[...] = S1
            return carry

        lax.fori_loop(0, n_chunks, step, 0)

        if ex:
            @pl.when(pl.program_id(0) == nb - 1)
            def _():
                ex.wait(x_in, x_out, sems)

    hm = pl.BlockSpec((NH, tb, HN), lambda i: (0, i, 0))
    res = pl.pallas_call(
        body, name="rwkv_scan_fwd", grid=(nb,), in_specs=[hm] * 6 + (ex.any_specs if ex else []),
        out_specs=[hm, pl.BlockSpec((n_chunks, NH, HN, HN), lambda i: (i, 0, 0, 0))] + (ex.any_specs if ex else []),
        out_shape=[jax.ShapeDtypeStruct((NH, T, HN), F32), jax.ShapeDtypeStruct((T // SCAN_C, NH, HN, HN), F32)]
        + (ex.out_shape if ex else []),
        scratch_shapes=[pltpu.VMEM((NH, HN, HN), F32)] + (ex.sem_shapes if ex else []),
        compiler_params=pltpu.CompilerParams(dimension_semantics=("arbitrary",), vmem_limit_bytes=VMEM_LIMIT),
    )(r, lw, k, v, kk, a, *(ex.bufs if ex else []))
    return res[0], res[1], list(res[2:])


def _scan_bwd(r, lw, k, v, kk, a, s0s, do, ex=None, tb=128):
    T = r.shape[1]
    tb = min(tb, T)
    n_chunks = tb // SCAN_C
    nb = T // tb
    nx = ex.nb if ex else 0

    def body(*refs):
        r_ref, lw_ref, k_ref, v_ref, kk_ref, a_ref, s0_ref, do_ref = refs[:8]
        x_in, (dr, dlw, dk, dv, dkk, da), x_out = refs[8:8 + nx], refs[8 + nx:14 + nx], refs[14 + nx:14 + 2 * nx]
        ds_ref, sems = refs[14 + 2 * nx], refs[15 + 2 * nx:]

        @pl.when(pl.program_id(0) == 0)
        def _():
            ds_ref[...] = jnp.zeros_like(ds_ref)
            if ex:
                ex.start(x_in, x_out, sems)

        def step(j, carry):
            c = n_chunks - 1 - j
            sl = pl.ds(pl.multiple_of(c * SCAN_C, SCAN_C), SCAN_C)
            _, vjp = jax.vjp(_chunk_fn, s0_ref[c], r_ref[:, sl, :], lw_ref[:, sl, :], k_ref[:, sl, :],
                             v_ref[:, sl, :], kk_ref[:, sl, :], a_ref[:, sl, :])
            g = vjp((do_ref[:, sl, :], ds_ref[...]))
            ds_ref[...] = g[0]
            for ref, val in zip((dr, dlw, dk, dv, dkk, da), g[1:]):
                ref[:, sl, :] = val
            return carry

        lax.fori_loop(0, n_chunks, step, 0)

        if ex:
            @pl.when(pl.program_id(0) == nb - 1)
            def _():
                ex.wait(x_in, x_out, sems)

    hm = pl.BlockSpec((NH, tb, HN), lambda i: (0, nb - 1 - i, 0))
    res = pl.pallas_call(
        body, name="rwkv_scan_bwd", grid=(nb,),
        in_specs=[hm] * 6 + [pl.BlockSpec((n_chunks, NH, HN, HN), lambda i: (nb - 1 - i, 0, 0, 0)), hm]
        + (ex.any_specs if ex else []),
        out_specs=[hm] * 6 + (ex.any_specs if ex else []),
        out_shape=[jax.ShapeDtypeStruct((NH, T, HN), F32)] * 6 + (ex.out_shape if ex else []),
        scratch_shapes=[pltpu.VMEM((NH, HN, HN), F32)] + (ex.sem_shapes if ex else []),
        compiler_params=pltpu.CompilerParams(dimension_semantics=("arbitrary",), vmem_limit_bytes=VMEM_LIMIT),
    )(r, lw, k, v, kk, a, s0s, do, *(ex.bufs if ex else []))
    return list(res[:6]), list(res[6:])


def _shift_down(i, p, prev8):
    first = jnp.where(i > 0, prev8[7:8, :], 0.0)
    row = lax.broadcasted_iota(jnp.int32, p.shape, 0)
    return jnp.where(row == 0, first, pltpu.roll(p, 1, axis=0))


def _mix_fwd(p, sb, tm=256):
    def fn(i, n, p, prev8, sb):
        return (p * sb[0:1] + _shift_down(i, p, prev8) * sb[1:2],)
    return _rows_call("shift_mix_fwd", fn, [Rows(p), Halo(p, -1)], [sb], [("rows", p.shape[1], F32)], tm=tm,
                      with_pid=True)[0]


def _mix_bwd(dq, p, sb, tm=256):
    def fn(i, n, dq, next8, p, prev8, sb):
        ps = _shift_down(i, p, prev8)
        d1 = dq * sb[1:2]
        last = jnp.where(i < n - 1, next8[0:1, :] * sb[1:2], 0.0)
        row = lax.broadcasted_iota(jnp.int32, dq.shape, 0)
        up = jnp.where(row == dq.shape[0] - 1, last, pltpu.roll(d1, dq.shape[0] - 1, axis=0))
        return (dq * sb[0:1] + up, jnp.sum(dq * p, axis=0, keepdims=True), jnp.sum(dq * ps, axis=0, keepdims=True))
    w = p.shape[1]
    return _rows_call("shift_mix_bwd", fn, [Rows(dq), Halo(dq, +1), Rows(p), Halo(p, -1)], [sb], [("rows", w, F32)],
                      accs=[((1, w), F32), ((1, w), F32)], tm=tm, with_pid=True)


def _local_step(x, target, W, late_weights=None, early_grads=None):
    G = {}
    a = _rows_call("norm_mix_fwd", lambda x, g: (_rms(x, g),), [Rows(x)], [W["g_mix"]], [("rows", D, BF16)])[0]
    p_sgu = _matmul("proj_sgu", a, W["w_sgu"], "nn")
    p_rw = _matmul("proj_rwkv", a, W["w_rw"], "nn")
    p_gate = _matmul("proj_gate", a, W["w_gate"], "nn")

    sgu_consts = [W["sgu_ln_w"], W["sgu_ln_b"], W["sgu_w"], W["sgu_bt"]]
    s = _rows_call("sgu_fwd", lambda *t: (_sgu_fn(*t),), [Rows(p_sgu)], sgu_consts, [("rows", D, BF16)], tm=SGU_C)[0]

    q = _mix_fwd(p_rw, W["sb"])
    q_ins = [Rows(q, D, 0), Rows(q, D, 1), Rows(q, D, 2), Rows(q, 128, 24), Rows(q, 128, 25), Rows(q, 256, 13)]
    pre_consts = [W["w_lora"], W["w0"], W["a_lora"], W["a0"], W["g_lora"], W["k_k"], W["k_a"]]
    r_h, lw_h, k_h, v_h, kk_h, a_h, g_gate = _rows_call(
        "rwkv_pre_fwd", _pre_fn, q_ins, pre_consts, [("heads", F32)] * 6 + [("rows", D, F32)], tm=128)
    o_h, s0s, got = _scan_fwd(r_h, lw_h, k_h, v_h, kk_h, a_h, ex=late_weights[0] if late_weights else None)
    if late_weights:
        W = {**W, **late_weights[1](got)}
    y_a = _matmul("proj_a", s, W["w_proj_a"], "nn")
    post_ins = [Heads(o_h), Heads(r_h), Heads(k_h), Heads(v_h), Rows(g_gate)]
    post_consts = [W[n].reshape(NH, 1, HN) for n in ("ln_x_w", "ln_x_b", "r_k")]
    z_b = _rows_call("rwkv_post_fwd", lambda *t: (_post_fn(*t),), post_ins, post_consts, [("rows", D, BF16)], tm=128)[0]
    y_b = _matmul("proj_b", z_b, W["w_proj_b"], "nn")

    gate_ins = [Rows(p_gate), Rows(y_a), Rows(y_b)]
    mixed = _rows_call("gate_fwd", lambda *t: (_gate_fn(*t),), gate_ins, [], [("rows", D, BF16)])[0]
    mo = _matmul("proj_out", mixed, W["w_out"], "nn")

    def res1(x, mo, g):
        h1 = x + mo
        return h1, _rms(h1, g)
    h1, f = _rows_call("residual_norm_fwd", res1, [Rows(x), Rows(mo)], [W["g_ffn"]],
                       [("rows", D, F32), ("rows", D, BF16)])
    u1 = _matmul("ffn_up", f, W["w_ffn1"], "nn")
    act = _rows_call("ffn_act_fwd", lambda u: (jnp.square(jnp.maximum(u, 0.0)),), [Rows(u1)], [],
                     [("rows", D_FF, BF16)])[0]
    ff = _matmul("ffn_down", act, W["w_ffn2"], "nn")

    def head(h1, ff, tgt, g):
        def f_(h1, ff, g):
            y = _rms(h1 + ff, g)
            return 0.5 * jnp.sum(jnp.mean(jnp.square(y - tgt), axis=-1))
        loss, (dh2, _, dg) = jax.value_and_grad(f_, argnums=(0, 1, 2))(h1, ff, g)
        return dh2, jnp.full((8, LANES), loss, F32), dg
    dh2, loss_acc, G["g_final"] = _rows_call("loss_head", head, [Rows(h1), Rows(ff), Rows(target)], [W["g_final"]],
                                             [("rows", D, F32)], accs=[((8, LANES), F32), ((1, D), F32)])

    d_act = _matmul("ffn_down_dx", dh2, W["w_ffn2"], "nt")
    G["w_ffn2"] = _matmul("ffn_down_dw", act, dh2, "tn", out_dtype=GRAD_PAYLOAD)
    d_u1 = _rows_call("ffn_act_bwd", lambda u, d: (d * 2.0 * jnp.maximum(u, 0.0),), [Rows(u1), Rows(d_act)], [],
                      [("rows", D_FF, BF16)])[0]
    d_f = _matmul("ffn_up_dx", d_u1, W["w_ffn1"], "nt")
    G["w_ffn1"] = _matmul("ffn_up_dw", f, d_u1, "tn", out_blocks=N_DEV, out_dtype=GRAD_PAYLOAD)

    def res1_bwd(x, mo, d_f, dh2, g):
        _, vjp = jax.vjp(lambda h, g: _rms(h, g), x + mo, g)
        dh, dg = vjp(d_f)
        return dh2 + dh, dg
    dh1, G["g_ffn"] = _rows_call("residual_norm_bwd", res1_bwd, [Rows(x), Rows(mo), Rows(d_f), Rows(dh2)],
                                 [W["g_ffn"]], [("rows", D, F32)], accs=[((1, D), F32)])
    d_mixed = _matmul("proj_out_dx", dh1, W["w_out"], "nt")
    G["w_out"] = _matmul("proj_out_dw", mixed, dh1, "tn", out_dtype=GRAD_PAYLOAD)

    def gate_bwd(pg, ya, yb, dm):
        _, vjp = jax.vjp(_gate_fn, pg, ya, yb)
        return vjp(dm)
    d_gate, d_ya, d_yb = _rows_call("gate_bwd", gate_bwd, gate_ins + [Rows(d_mixed)], [],
                                    [("rows", 2 * D, F32), ("rows", D, BF16), ("rows", D, BF16)])

    d_s = _matmul("proj_a_dx", d_ya, W["w_proj_a"], "nt")
    G["w_proj_a"] = _matmul("proj_a_dw", s, d_ya, "tn", out_dtype=GRAD_PAYLOAD)

    def sgu_bwd(p, ds, *c):
        _, vjp = jax.vjp(_sgu_fn, p, *c)
        return vjp(ds)
    d_p_sgu, G["sgu_ln_w"], G["sgu_ln_b"], G["sgu_w"], G["sgu_bt"] = _rows_call(
        "sgu_bwd", sgu_bwd, [Rows(p_sgu), Rows(d_s)], sgu_consts, [("rows", 2 * D, F32)],
        accs=[((1, D), F32), ((1, D), F32), ((SGU_G, SGU_C, SGU_C), F32), ((SGU_C, SGU_G), F32)], tm=SGU_C)

    d_zb = _matmul("proj_b_dx", d_yb, W["w_proj_b"], "nt")
    G["w_proj_b"] = _matmul("proj_b_dw", z_b, d_yb, "tn", out_dtype=GRAD_PAYLOAD)

    def post_bwd(o, r, k2, v, g, dz, *c):
        _, vjp = jax.vjp(_post_fn, o, r, k2, v, g, *c)
        return vjp(dz)
    do_h, dr1, dk1, dv1, d_g, g_lnw, g_lnb, g_rk = _rows_call(
        "rwkv_post_bwd", post_bwd, post_ins + [Rows(d_zb)], post_consts, [("heads", F32)] * 4 + [("rows", D, F32)],
        accs=[((NH, 1, HN), F32)] * 3, tm=128)
    G["ln_x_w"], G["ln_x_b"], G["r_k"] = (t.reshape(1, D) for t in (g_lnw, g_lnb, g_rk))
    (dr2, dlw, dk2, dv2, dkk, daa), early = _scan_bwd(r_h, lw_h, k_h, v_h, kk_h, a_h, s0s, do_h,
                                                      ex=early_grads(G) if early_grads else None)

    def pre_bwd(qr, qk, qv, qxw, qxa, qxg, dr1, dr2, dlw, dk1, dk2, dv1, dv2, dkk, daa, dg, *c):
        _, vjp = jax.vjp(_pre_fn, qr, qk, qv, qxw, qxa, qxg, *c)
        g = vjp((dr1 + dr2, dlw, dk1 + dk2, dv1 + dv2, dkk, daa, dg))
        dq = jnp.concatenate(g[:6], axis=-1)
        return (dq,) + tuple(g[6:])
    pre_b_ins = q_ins + [Heads(dr1), Heads(dr2), Heads(dlw), Heads(dk1), Heads(dk2), Heads(dv1), Heads(dv2),
                         Heads(dkk), Heads(daa), Rows(d_g)]
    d_q, G["w_lora"], G["w0"], G["a_lora"], G["a0"], G["g_lora"], G["k_k"], G["k_a"] = _rows_call(
        "rwkv_pre_bwd", pre_bwd, pre_b_ins, pre_consts, [("rows", RW_INT, F32)],
        accs=[((128, D), F32), ((1, D), F32), ((128, D), F32), ((1, D), F32), ((256, D), F32), ((1, D), F32),
              ((1, D), F32)], tm=128)
    d_p_rw, dsb0, dsb1 = _mix_bwd(d_q, p_rw, W["sb"])
    G["sb"] = jnp.concatenate([dsb0, dsb1], axis=0)

    G["w_sgu"] = _matmul("proj_sgu_dw", a, d_p_sgu, "tn")
    G["w_rw"] = _matmul("proj_rwkv_dw", a, d_p_rw, "tn")
    G["w_gate"] = _matmul("proj_gate_dw", a, d_gate, "tn")
    da1 = _matmul("proj_sgu_dx", d_p_sgu, W["w_sgu"], "nt")
    da2 = _matmul("proj_rwkv_dx", d_p_rw, W["w_rw"], "nt")
    da3 = _matmul("proj_gate_dx", d_gate, W["w_gate"], "nt")

    def norm1_bwd(x, da1, da2, da3, dh1, g):
        _, vjp = jax.vjp(_rms, x, g)
        dx, dg = vjp(da1 + da2 + da3)
        return dh1 + dx, dg
    dx, G["g_mix"] = _rows_call("norm_mix_bwd", norm1_bwd, [Rows(x), Rows(da1), Rows(da2), Rows(da3), Rows(dh1)],
                                [W["g_mix"]], [("rows", D, F32)], accs=[((1, D), F32)])
    return loss_acc[0, 0], dx, G, early


class Exchange:
    def __init__(self, bufs, gathers):
        self.bufs, self.gathers, self.nb = list(bufs), list(gathers), len(bufs)
        self.any_specs = [pl.BlockSpec(memory_space=pl.ANY)] * self.nb
        self.out_shape = [jax.ShapeDtypeStruct((N_DEV,) + (b.shape if g else b.shape[1:]), b.dtype)
                          for b, g in zip(self.bufs, self.gathers)]
        n = (N_DEV - 1) * self.nb
        self.sem_shapes = [pltpu.SemaphoreType.DMA((n,)), pltpu.SemaphoreType.DMA((n,)),
                           pltpu.SemaphoreType.DMA((self.nb,))]

    def _copies(self, in_refs, out_refs, sems):
        send_sems, recv_sems, local_sems = sems
        x, y, c = lax.axis_index("x"), lax.axis_index("y"), lax.axis_index("c")
        me = 4 * x + 2 * y + c

        def src(b, dest):
            return in_refs[b] if self.gathers[b] else in_refs[b].at[dest]

        local = [pltpu.make_async_copy(src(b, me), out_refs[b].at[me], local_sems.at[b]) for b in range(self.nb)]
        sends, recvs = [], []
        for kbits in range(1, N_DEV):
            px = 1 - x if kbits & 4 else x
            py = 1 - y if kbits & 2 else y
            pc = 1 - c if kbits & 1 else c
            peer = 4 * px + 2 * py + pc
            for b in range(self.nb):
                s = (kbits - 1) * self.nb + b
                sends.append(pltpu.make_async_remote_copy(
                    src_ref=src(b, peer), dst_ref=out_refs[b].at[me], send_sem=send_sems.at[s],
                    recv_sem=recv_sems.at[s], device_id=(px, py, pc), device_id_type=pl.DeviceIdType.MESH))
                recvs.append(pltpu.make_async_remote_copy(
                    src_ref=src(b, peer), dst_ref=out_refs[b].at[peer], send_sem=send_sems.at[s],
                    recv_sem=recv_sems.at[s], device_id=(px, py, pc), device_id_type=pl.DeviceIdType.MESH))
        return local, sends, recvs

    def start(self, in_refs, out_refs, sems):
        local, sends, _ = self._copies(in_refs, out_refs, sems)
        for cp in local + sends:
            cp.start()

    def wait(self, in_refs, out_refs, sems):
        local, sends, recvs = self._copies(in_refs, out_refs, sems)
        for cp in recvs:
            cp.wait_recv()
        for cp in sends:
            cp.wait_send()
        for cp in local:
            cp.wait()


def _exchange(name, bufs, gather):
    ex = Exchange(bufs, gather if isinstance(gather, (list, tuple)) else [gather] * len(bufs))

    def body(*refs):
        in_refs, out_refs, sems = refs[:ex.nb], refs[ex.nb:2 * ex.nb], refs[2 * ex.nb:]
        ex.start(in_refs, out_refs, sems)
        ex.wait(in_refs, out_refs, sems)

    return pl.pallas_call(body, name=name, in_specs=ex.any_specs, out_specs=ex.any_specs, out_shape=ex.out_shape,
                          scratch_shapes=ex.sem_shapes)(*ex.bufs)


def _adamw(name, slots, w, m, v, tr=256):
    R, Wd = w.shape[-2:]
    tr = tr if R % tr == 0 else R
    depth_axis = w.ndim == 3

    def body(s_ref, w_ref, m_ref, v_ref, g_out, d_out, m_out, v_out):
        g = s_ref[0].astype(F32)
        for j in range(1, N_DEV):
            g = g + s_ref[j].astype(F32)
        m_new = ADAM_B1 * m_ref[...] + (1.0 - ADAM_B1) * g
        v_new = ADAM_B2 * v_ref[...] + (1.0 - ADAM_B2) * jnp.square(g)
        m_hat = m_new / (1.0 - ADAM_B1 ** ADAM_STEP)
        v_hat = v_new / (1.0 - ADAM_B2 ** ADAM_STEP)
        g_out[...] = g
        d_out[...] = -ADAM_LR * (m_hat / (jnp.sqrt(v_hat) + ADAM_EPS) + ADAM_WD * w_ref[...])
        m_out[...] = m_new
        v_out[...] = v_new

    row = pl.BlockSpec((None, tr, Wd), lambda i: (0, i, 0)) if depth_axis else pl.BlockSpec((tr, Wd), lambda i: (i, 0))
    return pl.pallas_call(
        body, name=name, grid=(R // tr,), in_specs=[pl.BlockSpec((N_DEV, tr, Wd), lambda i: (0, i, 0)), row, row, row],
        out_specs=[row] * 4, out_shape=[jax.ShapeDtypeStruct(w.shape, F32)] * 4,
        compiler_params=pltpu.CompilerParams(dimension_semantics=("parallel",), vmem_limit_bytes=VMEM_LIMIT),
    )(slots, w, m, v)


PACK_W = 1024
_SMALL_SIZES = [int(np.prod(s)) for _, s in REPLICATED]
_SMALL_ROWS = _round_up(_round_up(sum(_SMALL_SIZES) + PACK_W, PACK_W) // PACK_W, 8)
_LOSS_AT = sum(_SMALL_SIZES)
W_IN_SHARD = P_TOTAL // N_DEV


def _pack_rows(parts, rows, dtype):
    flat = jnp.concatenate([p.reshape(-1).astype(dtype) for p in parts])
    return jnp.pad(flat, (0, rows * PACK_W - flat.shape[0])).reshape(rows, PACK_W)


def _w_in_from_blocks(blocks, tm=256):
    o = 2 * D

    def body(g_ref, sgu_ref, rw_ref, gate_ref):
        full = jnp.concatenate([g_ref[j].astype(F32) for j in range(N_DEV)], axis=1)
        z = lambda n: jnp.zeros((tm, n), F32)
        sgu_ref[...] = full[:, :o].astype(BF16)
        c = o + 3 * D
        rw = jnp.concatenate([full[:, o:c], full[:, c:c + L_W], z(128 - L_W), full[:, c + L_W:c + L_W + L_A],
                              z(128 - L_A), full[:, c + L_W + L_A:o + C_B], z(256 - L_G)], axis=1)
        rw_ref[...] = rw.astype(BF16)
        gate_ref[...] = full[:, o + C_B:].astype(BF16)

    widths = (2 * D, RW_INT, 2 * D)
    return pl.pallas_call(
        body, name="w_in_from_blocks", grid=(D // tm,),
        in_specs=[pl.BlockSpec((N_DEV, None, tm, W_IN_SHARD), lambda i: (0, 0, i, 0))],
        out_specs=[pl.BlockSpec((tm, w), lambda i: (i, 0)) for w in widths],
        out_shape=[jax.ShapeDtypeStruct((D, w), BF16) for w in widths],
        compiler_params=pltpu.CompilerParams(dimension_semantics=("parallel",), vmem_limit_bytes=VMEM_LIMIT),
    )(blocks)


def _w_in_to_blocks(g_sgu, g_rw, g_gate, tm=256):
    def body(sgu_ref, rw_ref, gate_ref, o_ref):
        rw = rw_ref[...]
        c = 3 * D
        full = jnp.concatenate([sgu_ref[...], rw[:, :c], rw[:, c:c + L_W], rw[:, c + 128:c + 128 + L_A],
                                rw[:, c + 256:c + 256 + L_G], gate_ref[...]], axis=1)
        for j in range(N_DEV):
            o_ref[j] = full[:, j * W_IN_SHARD:(j + 1) * W_IN_SHARD].astype(o_ref.dtype)

    widths = (2 * D, RW_INT, 2 * D)
    return pl.pallas_call(
        body, name="w_in_to_blocks", grid=(D // tm,),
        in_specs=[pl.BlockSpec((tm, w), lambda i: (i, 0)) for w in widths],
        out_specs=pl.BlockSpec((N_DEV, tm, W_IN_SHARD), lambda i: (0, i, 0)),
        out_shape=jax.ShapeDtypeStruct((N_DEV, D, W_IN_SHARD), GRAD_PAYLOAD),
        compiler_params=pltpu.CompilerParams(dimension_semantics=("parallel",), vmem_limit_bytes=VMEM_LIMIT),
    )(g_sgu, g_rw, g_gate)


def _cols_from_blocks(blk):
    return jnp.transpose(blk, (1, 0, 2)).reshape(blk.shape[1], -1)


def _cols_to_blocks(g):
    r, c = g.shape
    return jnp.transpose(g.reshape(r, N_DEV, c // N_DEV), (1, 0, 2))


FIRST_WEIGHTS = ["w_in", "shift_b", "w_lora_w", "a_lora_w", "g_lora_w"]
LATE_WEIGHTS = ["w_proj_a", "w_proj_b", "w_out", "w_ffn1", "w_ffn2"]


def _late_weights(shards):
    ex = Exchange([shards[n].astype(BF16) for n in LATE_WEIGHTS], [True] * len(LATE_WEIGHTS))

    def finish(results):
        got = dict(zip(LATE_WEIGHTS, results))
        W = {n: got[n].reshape(-1, D) for n in ("w_proj_a", "w_proj_b", "w_out", "w_ffn2")}
        W["w_ffn1"] = got["w_ffn1"].reshape(N_DEV, D, -1)
        return W
    return ex, finish


def _gather_weights(shards):
    bufs = [shards[n] if n == "shift_b" else shards[n].astype(BF16) for n in FIRST_WEIGHTS]
    got = dict(zip(FIRST_WEIGHTS, _exchange("weight_all_gather", bufs, gather=True)))
    W = {}
    W["w_sgu"], W["w_rw"], W["w_gate"] = _w_in_from_blocks(got["w_in"])
    z = lambda r, c, dt: jnp.zeros((r, c), dt)
    W["w_lora"] = jnp.concatenate([_cols_from_blocks(got["w_lora_w"][:, 0]).astype(F32), z(128 - L_W, D, F32)], axis=0)
    W["a_lora"] = jnp.concatenate([_cols_from_blocks(got["a_lora_w"][:, 0]).astype(F32), z(128 - L_A, D, F32)], axis=0)
    W["g_lora"] = jnp.concatenate([_cols_from_blocks(got["g_lora_w"][:, 0]).astype(F32), z(256 - L_G, D, F32)], axis=0)
    sb = _cols_from_blocks(got["shift_b"][:, 0])
    W["sb"] = jnp.concatenate([sb[:, :3 * D], sb[:, 3 * D:3 * D + L_W], z(2, 128 - L_W, F32),
                               sb[:, 3 * D + L_W:3 * D + L_W + L_A], z(2, 128 - L_A, F32),
                               sb[:, 3 * D + L_W + L_A:], z(2, 256 - L_G, F32)], axis=1)
    return W


def _replicated_weights(rep):
    W = {n: rep[n] for n in ("g_mix", "sgu_ln_w", "sgu_ln_b", "w0", "a0", "k_k", "k_a", "r_k", "ln_x_w", "ln_x_b",
                             "g_ffn")}
    W["g_final"] = rep["g_final"].reshape(1, D)
    W["sgu_w"] = rep["sgu_w"][0]
    W["sgu_bt"] = jnp.transpose(rep["sgu_b"][0])
    return W


def _late_grad_blocks(G):
    blocks = {n: G[n].reshape(N_DEV, -1, D) for n in ("w_proj_a", "w_proj_b", "w_out", "w_ffn2")}
    blocks["w_ffn1"] = G["w_ffn1"]
    return Exchange([blocks[n] for n in LATE_WEIGHTS], [False] * len(LATE_WEIGHTS))


def _first_grad_blocks(G):
    sbg = G["sb"]
    c = 3 * D
    sb = jnp.concatenate([sbg[:, :c], sbg[:, c:c + L_W], sbg[:, c + 128:c + 128 + L_A],
                          sbg[:, c + 256:c + 256 + L_G]], axis=1)
    return {
        "w_in": _w_in_to_blocks(G["w_sgu"], G["w_rw"], G["w_gate"]),
        "shift_b": _cols_to_blocks(sb),
        "w_lora_w": _cols_to_blocks(G["w_lora"][:L_W]), "a_lora_w": _cols_to_blocks(G["a_lora"][:L_A]),
        "g_lora_w": _cols_to_blocks(G["g_lora"][:L_G]),
    }


def _replicated_grads(G):
    small = {n: G[n] for n in ("g_mix", "sgu_ln_w", "sgu_ln_b", "w0", "a0", "k_k", "k_a", "r_k", "ln_x_w", "ln_x_b",
                               "g_ffn", "g_final")}
    small["sgu_w"] = G["sgu_w"]
    small["sgu_b"] = jnp.transpose(G["sgu_bt"])
    return small


def kernel(x, g_mix, w_in, sgu_ln_w, sgu_ln_b, sgu_w, sgu_b, w_proj_a, shift_b, w_lora_w, w0, a_lora_w, a0, g_lora_w, k_k, k_a, r_k, ln_x_w, ln_x_b, w_proj_b, w_out, g_ffn, w_ffn1, w_ffn2, g_final, loss_target, m_g_mix, m_w_in, m_sgu_ln_w, m_sgu_ln_b, m_sgu_w, m_sgu_b, m_w_proj_a, m_shift_b, m_w_lora_w, m_w0, m_a_lora_w, m_a0, m_g_lora_w, m_k_k, m_k_a, m_r_k, m_ln_x_w, m_ln_x_b, m_w_proj_b, m_w_out, m_g_ffn, m_w_ffn1, m_w_ffn2, m_g_final, v_g_mix, v_w_in, v_sgu_ln_w, v_sgu_ln_b, v_sgu_w, v_sgu_b, v_w_proj_a, v_shift_b, v_w_lora_w, v_w0, v_a_lora_w, v_a0, v_g_lora_w, v_k_k, v_k_a, v_r_k, v_ln_x_w, v_ln_x_b, v_w_proj_b, v_w_out, v_g_ffn, v_w_ffn1, v_w_ffn2, v_g_final):
    env = dict(locals())
    weights = {n: env[n] for n in WEIGHT_ORDER}
    moms = {n: env["m_" + n] for n in WEIGHT_ORDER}
    vars_ = {n: env["v_" + n] for n in WEIGHT_ORDER}

    shards = {n: weights[n] for n, _, _ in SHARDED}
    W = _gather_weights(shards)
    W.update(_replicated_weights({n: weights[n] for n, _ in REPLICATED}))
    loss_part, dx, G, late_slots = _local_step(x[0], loss_target[0], W, late_weights=_late_weights(shards),
                                               early_grads=_late_grad_blocks)

    slots = dict(zip(LATE_WEIGHTS, late_slots))
    blocks = _first_grad_blocks(G)
    small = _replicated_grads(G)
    small_parts = [small[n] for n, _ in REPLICATED] + [jnp.full((PACK_W,), loss_part, F32)]
    res = _exchange("grad_exchange", [blocks[n] for n in FIRST_WEIGHTS] + [_pack_rows(small_parts, _SMALL_ROWS, F32)],
                    [False] * len(FIRST_WEIGHTS) + [True])
    slots.update(zip(FIRST_WEIGHTS, res[:-1]))
    small_slots = res[-1]

    outs = [dict(), dict(), dict(), dict()]
    for n, _, _ in SHARDED:
        res = _adamw("adamw_" + n, slots[n], weights[n], moms[n], vars_[n])
        for k in range(4):
            outs[k][n] = res[k]

    def packed(d):
        return _pack_rows([d[n] for n, _ in REPLICATED], _SMALL_ROWS, F32)
    small_out = _adamw("adamw_replicated", small_slots, packed(weights), packed(moms), packed(vars_))
    for k in range(4):
        flat = small_out[k].reshape(-1)
        off = 0
        for (n, s), size in zip(REPLICATED, _SMALL_SIZES):
            outs[k][n] = flat[off:off + size].reshape(s)
            off += size
    loss = small_out[0].reshape(-1)[_LOSS_AT]
    return (loss, dx[None], *[outs[0][n] for n in WEIGHT_ORDER], *[outs[1][n] for n in WEIGHT_ORDER],
            *[outs[2][n] for n in WEIGHT_ORDER], *[outs[3][n] for n in WEIGHT_ORDER])
```

```python
import functools
import numpy as np
import jax
import jax.numpy as jnp
from jax import lax
from jax.experimental import pallas as pl
from jax.experimental.pallas import tpu as pltpu

F32 = jnp.float32
BF16 = jnp.bfloat16

D = 1024
NH, HN = 16, 64
SGU_G, SGU_C = 8, 128
L_W, L_A, L_G = 64, 64, 160
C_B = 3 * D + L_W + L_A + L_G
P_TOTAL = 2 * D + C_B + 2 * D
D_FF = 4 * D
RW_INT = 3 * D + 128 + 128 + 256
NORM_EPS, LN_EPS, GN_EPS = 1e-6, 1e-5, 64e-5
N_DEV = 8
LANES = 128
SCAN_C = 64
SOLVE_B = 16
SCAN_PRECISION = lax.Precision.HIGH
GRAD_PAYLOAD = BF16
VMEM_LIMIT = 56 * 1024 * 1024

ADAM_LR, ADAM_B1, ADAM_B2, ADAM_EPS, ADAM_WD, ADAM_STEP = 0.001, 0.9, 0.999, 1e-08, 0.01, 10

SHARDED = [
    ("w_in", (D, P_TOTAL), 1), ("w_proj_a", (D, D), 0), ("shift_b", (2, C_B), 1), ("w_lora_w", (L_W, D), 1),
    ("a_lora_w", (L_A, D), 1), ("g_lora_w", (L_G, D), 1), ("w_proj_b", (D, D), 0), ("w_out", (D, D), 0),
    ("w_ffn1", (D, D_FF), 1), ("w_ffn2", (D_FF, D), 0),
]
REPLICATED = [
    ("g_mix", (1, D)), ("sgu_ln_w", (1, D)), ("sgu_ln_b", (1, D)), ("sgu_w", (1, SGU_G, SGU_C, SGU_C)),
    ("sgu_b", (1, SGU_G, SGU_C)), ("w0", (1, D)), ("a0", (1, D)), ("k_k", (1, D)), ("k_a", (1, D)), ("r_k", (1, D)),
    ("ln_x_w", (1, D)), ("ln_x_b", (1, D)), ("g_ffn", (1, D)), ("g_final", (D,)),
]
WEIGHT_ORDER = ["g_mix", "w_in", "sgu_ln_w", "sgu_ln_b", "sgu_w", "sgu_b", "w_proj_a", "shift_b", "w_lora_w", "w0",
                "a_lora_w", "a0", "g_lora_w", "k_k", "k_a", "r_k", "ln_x_w", "ln_x_b", "w_proj_b", "w_out", "g_ffn",
                "w_ffn1", "w_ffn2", "g_final"]


def _shard_shape(shape, axis):
    s = list(shape)
    s[axis] //= N_DEV
    return tuple(s)


def _round_up(n, m):
    return (n + m - 1) // m * m


def _pick(n, target):
    if n <= target:
        return n
    best = None
    for t in range(LANES, target + 1, LANES):
        if n % t == 0:
            best = t
    assert best is not None, (n, target)
    return best


def _matmul(name, a, b, mode, out_dtype=F32, tm=1024, tn=1024, tk=1024, out_blocks=None):
    b_blocks = b.shape[0] if b.ndim == 3 else None
    bshape = b.shape if b.ndim == 2 else (b.shape[1], b.shape[0] * b.shape[2])
    if mode == "nn":
        (M, K), (K2, N) = a.shape, bshape
    elif mode == "nt":
        (M, K), (N, K2) = a.shape, bshape
    else:
        (K, M), (K2, N) = a.shape, bshape
    assert K == K2, (name, a.shape, b.shape)
    assert b_blocks is None or mode != "tn"
    assert out_blocks is None or mode == "tn"
    tn = min(tn, N // (out_blocks or 1), bshape[1] // b_blocks if (b_blocks and mode == "nn") else tn)
    tk = min(tk, bshape[1] // b_blocks if (b_blocks and mode == "nt") else tk)
    tm, tn, tk = _pick(M, tm), _pick(N, tn), _pick(K, tk)
    nk = K // tk
    dims = {"nn": (((1,), (0,)), ((), ())), "nt": (((1,), (1,)), ((), ())), "tn": (((0,), (0,)), ((), ()))}[mode]

    def body(a_ref, b_ref, o_ref, acc_ref):
        k = pl.program_id(2)

        @pl.when(k == 0)
        def _():
            acc_ref[...] = jnp.zeros_like(acc_ref)

        acc_ref[...] += lax.dot_general(a_ref[...].astype(BF16), b_ref[...].astype(BF16), dims,
                                        preferred_element_type=F32)

        @pl.when(k == nk - 1)
        def _():
            o_ref[...] = acc_ref[...].astype(o_ref.dtype)

    a_spec = {"nn": pl.BlockSpec((tm, tk), lambda i, j, k: (i, k)), "nt": pl.BlockSpec((tm, tk), lambda i, j, k: (i, k)),
              "tn": pl.BlockSpec((tk, tm), lambda i, j, k: (k, i))}[mode]
    b_spec = {"nn": pl.BlockSpec((tk, tn), lambda i, j, k: (k, j)), "nt": pl.BlockSpec((tn, tk), lambda i, j, k: (j, k)),
              "tn": pl.BlockSpec((tk, tn), lambda i, j, k: (k, j))}[mode]
    if b_blocks and mode == "nn":
        per = b.shape[2] // tn
        b_spec = pl.BlockSpec((None, tk, tn), lambda i, j, k: (j // per, k, j % per))
    elif b_blocks:
        per = b.shape[2] // tk
        b_spec = pl.BlockSpec((None, tn, tk), lambda i, j, k: (k // per, j, k % per))
    out_spec = pl.BlockSpec((tm, tn), lambda i, j, k: (i, j))
    out_shape = jax.ShapeDtypeStruct((M, N), out_dtype)
    if out_blocks:
        per_o = N // out_blocks // tn
        out_spec = pl.BlockSpec((None, tm, tn), lambda i, j, k: (j // per_o, i, j % per_o))
        out_shape = jax.ShapeDtypeStruct((out_blocks, M, N // out_blocks), out_dtype)
    return pl.pallas_call(
        body, name=name, grid=(M // tm, N // tn, nk), in_specs=[a_spec, b_spec],
        out_specs=out_spec, out_shape=out_shape, scratch_shapes=[pltpu.VMEM((tm, tn), F32)],
        compiler_params=pltpu.CompilerParams(dimension_semantics=("parallel", "parallel", "arbitrary"),
                                             vmem_limit_bytes=VMEM_LIMIT),
    )(a, b)


class Rows:
    def __init__(self, arr, width=None, cb=0):
        self.arr, self.width, self.cb = arr, (arr.shape[1] if width is None else width), cb


class Heads:
    def __init__(self, arr):
        self.arr = arr


class Halo:
    def __init__(self, arr, side):
        self.arr, self.side = arr, side


def _rows_call(name, fn, ins, consts, outs, accs=(), tm=256, with_pid=False):
    T = next(o.arr.shape[1] if isinstance(o, Heads) else o.arr.shape[0] for o in ins if not isinstance(o, Halo))
    tm = min(tm, T)
    n_tiles = T // tm
    n_in, n_c, n_out = len(ins), len(consts), len(outs)
    in_specs, args = [], []
    for o in ins:
        if isinstance(o, Rows):
            in_specs.append(pl.BlockSpec((tm, o.width), lambda i, cb=o.cb: (i, cb)))
        elif isinstance(o, Heads):
            in_specs.append(pl.BlockSpec((NH, tm, HN), lambda i: (0, i, 0)))
        else:
            w = o.arr.shape[1]
            if o.side < 0:
                in_specs.append(pl.BlockSpec((8, w), lambda i: (jnp.maximum(i * (tm // 8) - 1, 0), 0)))
            else:
                in_specs.append(pl.BlockSpec((8, w), lambda i: (jnp.minimum((i + 1) * (tm // 8), T // 8 - 1), 0)))
        args.append(o.arr)
    for c in consts:
        in_specs.append(pl.BlockSpec(c.shape, lambda i, nd=c.ndim: (0,) * nd))
        args.append(c)
    out_specs, out_shape = [], []
    for o in outs:
        if o[0] == "rows":
            out_specs.append(pl.BlockSpec((tm, o[1]), lambda i: (i, 0)))
            out_shape.append(jax.ShapeDtypeStruct((T, o[1]), o[2]))
        else:
            out_specs.append(pl.BlockSpec((NH, tm, HN), lambda i: (0, i, 0)))
            out_shape.append(jax.ShapeDtypeStruct((NH, T, HN), o[1]))
    for shape, dt in accs:
        out_specs.append(pl.BlockSpec(shape, lambda i, nd=len(shape): (0,) * nd))
        out_shape.append(jax.ShapeDtypeStruct(shape, dt))

    def body(*refs):
        i = pl.program_id(0)
        vals = []
        vals = [r[...] for r in refs[:n_in + n_c]]
        res = fn(i, n_tiles, *vals) if with_pid else fn(*vals)
        out_refs = refs[n_in + n_c:]
        for r, v in zip(out_refs[:n_out], res[:n_out]):
            r[...] = v.astype(r.dtype)
        if accs:
            @pl.when(i == 0)
            def _():
                for r in out_refs[n_out:]:
                    r[...] = jnp.zeros_like(r)

            for r, v in zip(out_refs[n_out:], res[n_out:]):
                r[...] += v.astype(r.dtype)

    res = pl.pallas_call(
        body, name=name, grid=(n_tiles,), in_specs=in_specs, out_specs=out_specs, out_shape=out_shape,
        compiler_params=pltpu.CompilerParams(dimension_semantics=("arbitrary",), vmem_limit_bytes=VMEM_LIMIT),
    )(*args)
    return res


def _rms(x, g):
    return x * lax.rsqrt(jnp.mean(x * x, axis=-1, keepdims=True) + NORM_EPS) * g


def _gelu(x):
    return 0.5 * x * (1.0 + lax.erf(x * 0.7071067811865476))


def _sigmoid(x):
    return 1.0 / (1.0 + jnp.exp(-x))


def _bdot(a, b):
    return jnp.dot(a.astype(BF16), b.astype(BF16), preferred_element_type=F32)


def _to_heads(x):
    return jnp.concatenate([x[:, h * HN:(h + 1) * HN][None] for h in range(NH)], axis=0)


def _from_heads(xh):
    return jnp.concatenate([xh[h] for h in range(NH)], axis=-1)


def _sgu_fn(p, ln_w, ln_b, sw, sbt):
    z = _gelu(p)
    u, v = z[:, :D], z[:, D:]
    mu = jnp.mean(v, axis=-1, keepdims=True)
    var = jnp.mean(jnp.square(v - mu), axis=-1, keepdims=True)
    vn = (v - mu) * lax.rsqrt(var + LN_EPS) * ln_w + ln_b
    ri = lax.broadcasted_iota(jnp.int32, (SGU_C, SGU_C), 0)
    ci = lax.broadcasted_iota(jnp.int32, (SGU_C, SGU_C), 1)
    mask = (ci <= ri).astype(F32)
    dg = D // SGU_G
    parts = []
    for g in range(SGU_G):
        parts.append(_bdot(sw[g] * mask, vn[:, g * dg:(g + 1) * dg]) + sbt[:, g:g + 1])
    return u * jnp.concatenate(parts, axis=-1)


def _pre_fn(qr, qk, qv, qxw, qxa, qxg, wl, w0, al, a0, gl, k_k, k_a):
    w = -jax.nn.softplus(-(w0 + _bdot(jnp.tanh(qxw), wl))) - 0.5
    lw = -jnp.exp(w)
    aa = _sigmoid(a0 + _bdot(qxa, al))
    g = _bdot(_sigmoid(qxg), gl)
    kk = _to_heads(qk * k_k)
    kk = kk / jnp.maximum(jnp.sqrt(jnp.sum(kk * kk, axis=-1, keepdims=True)), 1e-12)
    k2 = qk * (1.0 + (aa - 1.0) * k_a)
    return _to_heads(qr), _to_heads(lw), _to_heads(k2), _to_heads(qv), kk, _to_heads(aa), g


def _post_fn(o, r, k2, v, g, ln_w, ln_b, r_k):
    mu = jnp.mean(o, axis=-1, keepdims=True)
    d = o - mu
    var = jnp.mean(d * d, axis=-1, keepdims=True)
    on = d * lax.rsqrt(var + GN_EPS) * ln_w + ln_b
    bonus = jnp.sum(r * k2 * r_k, axis=-1, keepdims=True) * v
    return _from_heads(on + bonus) * g


def _gate_fn(pg, ya, yb):
    return _sigmoid(pg[:, :D]) * ya + _sigmoid(pg[:, D:]) * yb


def _bmm(x, y, cx, cy):
    return lax.dot_general(x, y, (((cx,), (cy,)), ((0,), (0,))), precision=SCAN_PRECISION,
                           preferred_element_type=F32)


def _unit_lower_inverse(M):
    C = M.shape[1]
    ti = lax.broadcasted_iota(jnp.int32, (C, C), 0)
    tj = lax.broadcasted_iota(jnp.int32, (C, C), 1)
    eye = (ti == tj).astype(F32)
    same = lambda b: (ti // b == tj // b).astype(F32)
    X = -(M * same(SOLVE_B))
    inv = eye + X
    span = 1
    while 2 * span < SOLVE_B:
        X = _bmm(X, X, 2, 1)
        inv = inv + _bmm(inv, X, 2, 1)
        span *= 2
    b = SOLVE_B
    while b < C:
        low = M * (same(2 * b) - same(b))
        inv = inv - _bmm(_bmm(inv, low, 2, 1), inv, 2, 1)
        b *= 2
    return inv


@jax.custom_vjp
def _unit_lower_solve(M, y):
    return _bmm(_unit_lower_inverse(M), y, 2, 1)


def _unit_lower_solve_fwd(M, y):
    inv = _unit_lower_inverse(M)
    u = _bmm(inv, y, 2, 1)
    return u, (inv, u)


def _unit_lower_solve_bwd(res, du):
    inv, u = res
    dy = _bmm(inv, du, 1, 1)
    return -_bmm(dy, u, 2, 2), dy


_unit_lower_solve.defvjp(_unit_lower_solve_fwd, _unit_lower_solve_bwd)


def _chunk_fn(S0, r, lw, k, v, kk, a):
    C = SCAN_C
    bmm = _bmm
    ti = lax.broadcasted_iota(jnp.int32, (C, C), 0)
    tj = lax.broadcasted_iota(jnp.int32, (C, C), 1)
    incl = (tj <= ti).astype(F32)
    strict = (tj < ti).astype(F32)
    cum = lax.dot_general(jnp.broadcast_to(incl, (NH, C, C)), lw, (((2,), (1,)), ((0,), (0,))),
                          precision=lax.Precision.HIGHEST, preferred_element_type=F32)
    g_in, g_ex, g_inv = jnp.exp(cum), jnp.exp(cum - lw), jnp.exp(-cum)
    kkt, rt = kk * g_ex, r * g_in
    bk = jnp.concatenate([kk * a * g_inv, k * g_inv], axis=1)
    A = bmm(kkt, bk, 2, 2)
    M = A[:, :, :C] * strict
    n_mask = jnp.concatenate([jnp.zeros((C, C), F32), strict], axis=1)
    zv = jnp.concatenate([jnp.zeros_like(v), v], axis=1)
    s0_side = bmm(jnp.concatenate([kkt, rt], axis=1), S0, 2, 2)
    y = _unit_lower_solve(M, s0_side[:, :C] + bmm(A * n_mask, zv, 2, 1))
    z = jnp.concatenate([-y, v], axis=1)
    O = s0_side[:, C:] + bmm(bmm(rt, bk, 2, 2) * jnp.concatenate([incl, incl], axis=1), z, 2, 1)
    S1 = (S0 + bmm(z, bk, 1, 1)) * g_in[:, C - 1:C, :]
    return O, S1


def _scan_fwd(r, lw, k, v, kk, a, ex=None, tb=256):
    T = r.shape[1]
    tb = min(tb, T)
    n_chunks = tb // SCAN_C
    nb = T // tb
    nx = ex.nb if ex else 0

    def body(*refs):
        r_ref, lw_ref, k_ref, v_ref, kk_ref, a_ref = refs[:6]
        x_in, (o_ref, s0_ref), x_out = refs[6:6 + nx], refs[6 + nx:8 + nx], refs[8 + nx:8 + 2 * nx]
        s_ref, sems = refs[8 + 2 * nx], refs[9 + 2 * nx:]

        @pl.when(pl.program_id(0) == 0)
        def _():
            s_ref[...] = jnp.zeros_like(s_ref)
            if ex:
                ex.start(x_in, x_out, sems)

        def step(c, carry):
            sl = pl.ds(pl.multiple_of(c * SCAN_C, SCAN_C), SCAN_C)
            S0 = s_ref[...]
            s0_ref[c] = S0
            O, S1 = _chunk_fn(S0, r_ref[:, sl, :], lw_ref[:, sl, :], k_ref[:, sl, :], v_ref[:, sl, :],
                              kk_ref[:, sl, :], a_ref[:, sl, :])
            o_ref[:, sl, :] = O
            s_ref[...] = S1
            return carry

        lax.fori_loop(0, n_chunks, step, 0)

        if ex:
            @pl.when(pl.program_id(0) == nb - 1)
            def _():
                ex.wait(x_in, x_out, sems)

    hm = pl.BlockSpec((NH, tb, HN), lambda i: (0, i, 0))
    res = pl.pallas_call(
        body, name="rwkv_scan_fwd", grid=(nb,), in_specs=[hm] * 6 + (ex.any_specs if ex else []),
        out_specs=[hm, pl.BlockSpec((n_chunks, NH, HN, HN), lambda i: (i, 0, 0, 0))] + (ex.any_specs if ex else []),
        out_shape=[jax.ShapeDtypeStruct((NH, T, HN), F32), jax.ShapeDtypeStruct((T // SCAN_C, NH, HN, HN), F32)]
        + (ex.out_shape if ex else []),
        scratch_shapes=[pltpu.VMEM((NH, HN, HN), F32)] + (ex.sem_shapes if ex else []),
        compiler_params=pltpu.CompilerParams(dimension_semantics=("arbitrary",), vmem_limit_bytes=VMEM_LIMIT),
    )(r, lw, k, v, kk, a, *(ex.bufs if ex else []))
    return res[0], res[1], list(res[2:])


def _scan_bwd(r, lw, k, v, kk, a, s0s, do, ex=None, tb=128):
    T = r.shape[1]
    tb = min(tb, T)
    n_chunks = tb // SCAN_C
    nb = T // tb
    nx = ex.nb if ex else 0

    def body(*refs):
        r_ref, lw_ref, k_ref, v_ref, kk_ref, a_ref, s0_ref, do_ref = refs[:8]
        x_in, (dr, dlw, dk, dv, dkk, da), x_out = refs[8:8 + nx], refs[8 + nx:14 + nx], refs[14 + nx:14 + 2 * nx]
        ds_ref, sems = refs[14 + 2 * nx], refs[15 + 2 * nx:]

        @pl.when(pl.program_id(0) == 0)
        def _():
            ds_ref[...] = jnp.zeros_like(ds_ref)
            if ex:
                ex.start(x_in, x_out, sems)

        def step(j, carry):
            c = n_chunks - 1 - j
            sl = pl.ds(pl.multiple_of(c * SCAN_C, SCAN_C), SCAN_C)
            _, vjp = jax.vjp(_chunk_fn, s0_ref[c], r_ref[:, sl, :], lw_ref[:, sl, :], k_ref[:, sl, :],
                             v_ref[:, sl, :], kk_ref[:, sl, :], a_ref[:, sl, :])
            g = vjp((do_ref[:, sl, :], ds_ref[...]))
            ds_ref[...] = g[0]
            for ref, val in zip((dr, dlw, dk, dv, dkk, da), g[1:]):
                ref[:, sl, :] = val
            return carry

        lax.fori_loop(0, n_chunks, step, 0)

        if ex:
            @pl.when(pl.program_id(0) == nb - 1)
            def _():
                ex.wait(x_in, x_out, sems)

    hm = pl.BlockSpec((NH, tb, HN), lambda i: (0, nb - 1 - i, 0))
    res = pl.pallas_call(
        body, name="rwkv_scan_bwd", grid=(nb,),
        in_specs=[hm] * 6 + [pl.BlockSpec((n_chunks, NH, HN, HN), lambda i: (nb - 1 - i, 0, 0, 0)), hm]
        + (ex.any_specs if ex else []),
        out_specs=[hm] * 6 + (ex.any_specs if ex else []),
        out_shape=[jax.ShapeDtypeStruct((NH, T, HN), F32)] * 6 + (ex.out_shape if ex else []),
        scratch_shapes=[pltpu.VMEM((NH, HN, HN), F32)] + (ex.sem_shapes if ex else []),
        compiler_params=pltpu.CompilerParams(dimension_semantics=("arbitrary",), vmem_limit_bytes=VMEM_LIMIT),
    )(r, lw, k, v, kk, a, s0s, do, *(ex.bufs if ex else []))
    return list(res[:6]), list(res[6:])


def _shift_down(i, p, prev8):
    first = jnp.where(i > 0, prev8[7:8, :], 0.0)
    row = lax.broadcasted_iota(jnp.int32, p.shape, 0)
    return jnp.where(row == 0, first, pltpu.roll(p, 1, axis=0))


def _mix_fwd(p, sb, tm=256):
    def fn(i, n, p, prev8, sb):
        return (p * sb[0:1] + _shift_down(i, p, prev8) * sb[1:2],)
    return _rows_call("shift_mix_fwd", fn, [Rows(p), Halo(p, -1)], [sb], [("rows", p.shape[1], F32)], tm=tm,
                      with_pid=True)[0]


def _mix_bwd(dq, p, sb, tm=256):
    def fn(i, n, dq, next8, p, prev8, sb):
        ps = _shift_down(i, p, prev8)
        d1 = dq * sb[1:2]
        last = jnp.where(i < n - 1, next8[0:1, :] * sb[1:2], 0.0)
        row = lax.broadcasted_iota(jnp.int32, dq.shape, 0)
        up = jnp.where(row == dq.shape[0] - 1, last, pltpu.roll(d1, dq.shape[0] - 1, axis=0))
        return (dq * sb[0:1] + up, jnp.sum(dq * p, axis=0, keepdims=True), jnp.sum(dq * ps, axis=0, keepdims=True))
    w = p.shape[1]
    return _rows_call("shift_mix_bwd", fn, [Rows(dq), Halo(dq, +1), Rows(p), Halo(p, -1)], [sb], [("rows", w, F32)],
                      accs=[((1, w), F32), ((1, w), F32)], tm=tm, with_pid=True)


def _local_step(x, target, W, late_weights=None, early_grads=None):
    G = {}
    a = _rows_call("norm_mix_fwd", lambda x, g: (_rms(x, g),), [Rows(x)], [W["g_mix"]], [("rows", D, BF16)])[0]
    p_sgu = _matmul("proj_sgu", a, W["w_sgu_t"], "nt")
    p_rw = _matmul("proj_rwkv", a, W["w_rw_t"], "nt")
    p_gate = _matmul("proj_gate", a, W["w_gate_t"], "nt")

    sgu_consts = [W["sgu_ln_w"], W["sgu_ln_b"], W["sgu_w"], W["sgu_bt"]]
    s = _rows_call("sgu_fwd", lambda *t: (_sgu_fn(*t),), [Rows(p_sgu)], sgu_consts, [("rows", D, BF16)], tm=SGU_C)[0]

    q = _mix_fwd(p_rw, W["sb"])
    q_ins = [Rows(q, D, 0), Rows(q, D, 1), Rows(q, D, 2), Rows(q, 128, 24), Rows(q, 128, 25), Rows(q, 256, 13)]
    pre_consts = [W["w_lora"], W["w0"], W["a_lora"], W["a0"], W["g_lora"], W["k_k"], W["k_a"]]
    r_h, lw_h, k_h, v_h, kk_h, a_h, g_gate = _rows_call(
        "rwkv_pre_fwd", _pre_fn, q_ins, pre_consts, [("heads", F32)] * 6 + [("rows", D, F32)], tm=128)
    o_h, s0s, got = _scan_fwd(r_h, lw_h, k_h, v_h, kk_h, a_h, ex=late_weights[0] if late_weights else None)
    if late_weights:
        W = {**W, **late_weights[1](got)}
    y_a = _matmul("proj_a", s, W["w_proj_a"], "nn")
    post_ins = [Heads(o_h), Heads(r_h), Heads(k_h), Heads(v_h), Rows(g_gate)]
    post_consts = [W[n].reshape(NH, 1, HN) for n in ("ln_x_w", "ln_x_b", "r_k")]
    z_b = _rows_call("rwkv_post_fwd", lambda *t: (_post_fn(*t),), post_ins, post_consts, [("rows", D, BF16)], tm=128)[0]
    y_b = _matmul("proj_b", z_b, W["w_proj_b"], "nn")

    gate_ins = [Rows(p_gate), Rows(y_a), Rows(y_b)]
    mixed = _rows_call("gate_fwd", lambda *t: (_gate_fn(*t),), gate_ins, [], [("rows", D, BF16)])[0]
    mo = _matmul("proj_out", mixed, W["w_out"], "nn")

    def res1(x, mo, g):
        h1 = x + mo
        return h1, _rms(h1, g)
    h1, f = _rows_call("residual_norm_fwd", res1, [Rows(x), Rows(mo)], [W["g_ffn"]],
                       [("rows", D, F32), ("rows", D, BF16)])
    u1 = _matmul("ffn_up", f, W["w_ffn1"], "nn")
    act = _rows_call("ffn_act_fwd", lambda u: (jnp.square(jnp.maximum(u, 0.0)),), [Rows(u1)], [],
                     [("rows", D_FF, BF16)])[0]
    ff = _matmul("ffn_down", act, W["w_ffn2"], "nn")

    def head(h1, ff, tgt, g):
        def f_(h1, ff, g):
            y = _rms(h1 + ff, g)
            return 0.5 * jnp.sum(jnp.mean(jnp.square(y - tgt), axis=-1))
        loss, (dh2, _, dg) = jax.value_and_grad(f_, argnums=(0, 1, 2))(h1, ff, g)
        return dh2, jnp.full((8, LANES), loss, F32), dg
    dh2, loss_acc, G["g_final"] = _rows_call("loss_head", head, [Rows(h1), Rows(ff), Rows(target)], [W["g_final"]],
                                             [("rows", D, F32)], accs=[((8, LANES), F32), ((1, D), F32)])

    d_act = _matmul("ffn_down_dx", dh2, W["w_ffn2"], "nt")
    G["w_ffn2"] = _matmul("ffn_down_dw", act, dh2, "tn", out_dtype=GRAD_PAYLOAD)
    d_u1 = _rows_call("ffn_act_bwd", lambda u, d: (d * 2.0 * jnp.maximum(u, 0.0),), [Rows(u1), Rows(d_act)], [],
                      [("rows", D_FF, BF16)])[0]
    d_f = _matmul("ffn_up_dx", d_u1, W["w_ffn1"], "nt")
    G["w_ffn1"] = _matmul("ffn_up_dw", f, d_u1, "tn", out_blocks=N_DEV, out_dtype=GRAD_PAYLOAD)

    def res1_bwd(x, mo, d_f, dh2, g):
        _, vjp = jax.vjp(lambda h, g: _rms(h, g), x + mo, g)
        dh, dg = vjp(d_f)
        return dh2 + dh, dg
    dh1, G["g_ffn"] = _rows_call("residual_norm_bwd", res1_bwd, [Rows(x), Rows(mo), Rows(d_f), Rows(dh2)],
                                 [W["g_ffn"]], [("rows", D, F32)], accs=[((1, D), F32)])
    d_mixed = _matmul("proj_out_dx", dh1, W["w_out"], "nt")
    G["w_out"] = _matmul("proj_out_dw", mixed, dh1, "tn", out_dtype=GRAD_PAYLOAD)

    def gate_bwd(pg, ya, yb, dm):
        _, vjp = jax.vjp(_gate_fn, pg, ya, yb)
        return vjp(dm)
    d_gate, d_ya, d_yb = _rows_call("gate_bwd", gate_bwd, gate_ins + [Rows(d_mixed)], [],
                                    [("rows", 2 * D, F32), ("rows", D, BF16), ("rows", D, BF16)])

    d_s = _matmul("proj_a_dx", d_ya, W["w_proj_a"], "nt")
    G["w_proj_a"] = _matmul("proj_a_dw", s, d_ya, "tn", out_dtype=GRAD_PAYLOAD)

    def sgu_bwd(p, ds, *c):
        _, vjp = jax.vjp(_sgu_fn, p, *c)
        return vjp(ds)
    d_p_sgu, G["sgu_ln_w"], G["sgu_ln_b"], G["sgu_w"], G["sgu_bt"] = _rows_call(
        "sgu_bwd", sgu_bwd, [Rows(p_sgu), Rows(d_s)], sgu_consts, [("rows", 2 * D, F32)],
        accs=[((1, D), F32), ((1, D), F32), ((SGU_G, SGU_C, SGU_C), F32), ((SGU_C, SGU_G), F32)], tm=SGU_C)

    d_zb = _matmul("proj_b_dx", d_yb, W["w_proj_b"], "nt")
    G["w_proj_b"] = _matmul("proj_b_dw", z_b, d_yb, "tn", out_dtype=GRAD_PAYLOAD)

    def post_bwd(o, r, k2, v, g, dz, *c):
        _, vjp = jax.vjp(_post_fn, o, r, k2, v, g, *c)
        return vjp(dz)
    do_h, dr1, dk1, dv1, d_g, g_lnw, g_lnb, g_rk = _rows_call(
        "rwkv_post_bwd", post_bwd, post_ins + [Rows(d_zb)], post_consts, [("heads", F32)] * 4 + [("rows", D, F32)],
        accs=[((NH, 1, HN), F32)] * 3, tm=128)
    G["ln_x_w"], G["ln_x_b"], G["r_k"] = (t.reshape(1, D) for t in (g_lnw, g_lnb, g_rk))
    (dr2, dlw, dk2, dv2, dkk, daa), early = _scan_bwd(r_h, lw_h, k_h, v_h, kk_h, a_h, s0s, do_h,
                                                      ex=early_grads(G) if early_grads else None)

    def pre_bwd(qr, qk, qv, qxw, qxa, qxg, dr1, dr2, dlw, dk1, dk2, dv1, dv2, dkk, daa, dg, *c):
        _, vjp = jax.vjp(_pre_fn, qr, qk, qv, qxw, qxa, qxg, *c)
        g = vjp((dr1 + dr2, dlw, dk1 + dk2, dv1 + dv2, dkk, daa, dg))
        dq = jnp.concatenate(g[:6], axis=-1)
        return (dq,) + tuple(g[6:])
    pre_b_ins = q_ins + [Heads(dr1), Heads(dr2), Heads(dlw), Heads(dk1), Heads(dk2), Heads(dv1), Heads(dv2),
                         Heads(dkk), Heads(daa), Rows(d_g)]
    d_q, G["w_lora"], G["w0"], G["a_lora"], G["a0"], G["g_lora"], G["k_k"], G["k_a"] = _rows_call(
        "rwkv_pre_bwd", pre_bwd, pre_b_ins, pre_consts, [("rows", RW_INT, F32)],
        accs=[((128, D), F32), ((1, D), F32), ((128, D), F32), ((1, D), F32), ((256, D), F32), ((1, D), F32),
              ((1, D), F32)], tm=128)
    d_p_rw, dsb0, dsb1 = _mix_bwd(d_q, p_rw, W["sb"])
    G["sb"] = jnp.concatenate([dsb0, dsb1], axis=0)

    G["w_sgu_t"] = _matmul("proj_sgu_dw", d_p_sgu, a, "tn", out_dtype=GRAD_PAYLOAD)
    G["w_rw_t"] = _matmul("proj_rwkv_dw", d_p_rw, a, "tn", out_dtype=GRAD_PAYLOAD)
    G["w_gate_t"] = _matmul("proj_gate_dw", d_gate, a, "tn", out_dtype=GRAD_PAYLOAD)
    da1 = _matmul("proj_sgu_dx", d_p_sgu, W["w_sgu_t"], "nn")
    da2 = _matmul("proj_rwkv_dx", d_p_rw, W["w_rw_t"], "nn")
    da3 = _matmul("proj_gate_dx", d_gate, W["w_gate_t"], "nn")

    def norm1_bwd(x, da1, da2, da3, dh1, g):
        _, vjp = jax.vjp(_rms, x, g)
        dx, dg = vjp(da1 + da2 + da3)
        return dh1 + dx, dg
    dx, G["g_mix"] = _rows_call("norm_mix_bwd", norm1_bwd, [Rows(x), Rows(da1), Rows(da2), Rows(da3), Rows(dh1)],
                                [W["g_mix"]], [("rows", D, F32)], accs=[((1, D), F32)])
    return loss_acc[0, 0], dx, G, early


class Exchange:
    def __init__(self, bufs, gathers):
        self.bufs, self.gathers, self.nb = list(bufs), list(gathers), len(bufs)
        self.any_specs = [pl.BlockSpec(memory_space=pl.ANY)] * self.nb
        self.out_shape = [jax.ShapeDtypeStruct((N_DEV,) + (b.shape if g else b.shape[1:]), b.dtype)
                          for b, g in zip(self.bufs, self.gathers)]
        n = (N_DEV - 1) * self.nb
        self.sem_shapes = [pltpu.SemaphoreType.DMA((n,)), pltpu.SemaphoreType.DMA((n,)),
                           pltpu.SemaphoreType.DMA((self.nb,))]

    def _copies(self, in_refs, out_refs, sems):
        send_sems, recv_sems, local_sems = sems
        x, y, c = lax.axis_index("x"), lax.axis_index("y"), lax.axis_index("c")
        me = 4 * x + 2 * y + c

        def src(b, dest):
            return in_refs[b] if self.gathers[b] else in_refs[b].at[dest]

        local = [pltpu.make_async_copy(src(b, me), out_refs[b].at[me], local_sems.at[b]) for b in range(self.nb)]
        sends, recvs = [], []
        for kbits in range(1, N_DEV):
            px = 1 - x if kbits & 4 else x
            py = 1 - y if kbits & 2 else y
            pc = 1 - c if kbits & 1 else c
            peer = 4 * px + 2 * py + pc
            for b in range(self.nb):
                s = (kbits - 1) * self.nb + b
                sends.append(pltpu.make_async_remote_copy(
                    src_ref=src(b, peer), dst_ref=out_refs[b].at[me], send_sem=send_sems.at[s],
                    recv_sem=recv_sems.at[s], device_id=(px, py, pc), device_id_type=pl.DeviceIdType.MESH))
                recvs.append(pltpu.make_async_remote_copy(
                    src_ref=src(b, peer), dst_ref=out_refs[b].at[peer], send_sem=send_sems.at[s],
                    recv_sem=recv_sems.at[s], device_id=(px, py, pc), device_id_type=pl.DeviceIdType.MESH))
        return local, sends, recvs

    def start(self, in_refs, out_refs, sems):
        local, sends, _ = self._copies(in_refs, out_refs, sems)
        for cp in local + sends:
            cp.start()

    def wait(self, in_refs, out_refs, sems):
        local, sends, recvs = self._copies(in_refs, out_refs, sems)
        for cp in recvs:
            cp.wait_recv()
        for cp in sends:
            cp.wait_send()
        for cp in local:
            cp.wait()


def _exchange(name, bufs, gather):
    ex = Exchange(bufs, gather if isinstance(gather, (list, tuple)) else [gather] * len(bufs))

    def body(*refs):
        in_refs, out_refs, sems = refs[:ex.nb], refs[ex.nb:2 * ex.nb], refs[2 * ex.nb:]
        ex.start(in_refs, out_refs, sems)
        ex.wait(in_refs, out_refs, sems)

    return pl.pallas_call(body, name=name, in_specs=ex.any_specs, out_specs=ex.any_specs, out_shape=ex.out_shape,
                          scratch_shapes=ex.sem_shapes)(*ex.bufs)


def _adamw(name, slots, w, m, v, tr=256):
    R, Wd = w.shape[-2:]
    depth_axis = w.ndim == 3
    if R % tr == 0:
        tc = Wd
    else:
        tr, tc = R, (256 if (Wd % 256 == 0 and R > 256) else Wd)

    def body(s_ref, w_ref, m_ref, v_ref, g_out, d_out, m_out, v_out):
        g = s_ref[0].astype(F32)
        for j in range(1, N_DEV):
            g = g + s_ref[j].astype(F32)
        m_new = ADAM_B1 * m_ref[...] + (1.0 - ADAM_B1) * g
        v_new = ADAM_B2 * v_ref[...] + (1.0 - ADAM_B2) * jnp.square(g)
        m_hat = m_new / (1.0 - ADAM_B1 ** ADAM_STEP)
        v_hat = v_new / (1.0 - ADAM_B2 ** ADAM_STEP)
        g_out[...] = g
        d_out[...] = -ADAM_LR * (m_hat / (jnp.sqrt(v_hat) + ADAM_EPS) + ADAM_WD * w_ref[...])
        m_out[...] = m_new
        v_out[...] = v_new

    if depth_axis:
        row = pl.BlockSpec((None, tr, tc), lambda i, j: (0, i, j))
    else:
        row = pl.BlockSpec((tr, tc), lambda i, j: (i, j))
    return pl.pallas_call(
        body, name=name, grid=(R // tr, Wd // tc),
        in_specs=[pl.BlockSpec((N_DEV, tr, tc), lambda i, j: (0, i, j)), row, row, row],
        out_specs=[row] * 4, out_shape=[jax.ShapeDtypeStruct(w.shape, F32)] * 4,
        compiler_params=pltpu.CompilerParams(dimension_semantics=("parallel", "parallel"),
                                             vmem_limit_bytes=VMEM_LIMIT),
    )(slots, w, m, v)


PACK_W = 1024
_SMALL_SIZES = [int(np.prod(s)) for _, s in REPLICATED]
_SMALL_ROWS = _round_up(_round_up(sum(_SMALL_SIZES) + PACK_W, PACK_W) // PACK_W, 8)
_LOSS_AT = sum(_SMALL_SIZES)
W_IN_SHARD = P_TOTAL // N_DEV


def _pack_rows(parts, rows, dtype):
    flat = jnp.concatenate([p.reshape(-1).astype(dtype) for p in parts])
    return jnp.pad(flat, (0, rows * PACK_W - flat.shape[0])).reshape(rows, PACK_W)


def _w_in_groups_t(wt):
    o, c = 2 * D, 2 * D + 3 * D
    z = lambda r: jnp.zeros((r, D), wt.dtype)
    rw = jnp.concatenate([wt[o:c], wt[c:c + L_W], z(128 - L_W), wt[c + L_W:c + L_W + L_A], z(128 - L_A),
                          wt[c + L_W + L_A:o + C_B], z(256 - L_G)], axis=0)
    return wt[:o], rw, wt[o + C_B:]


def _w_in_grad_t(g_sgu_t, g_rw_t, g_gate_t):
    c = 3 * D
    return jnp.concatenate([g_sgu_t, g_rw_t[:c], g_rw_t[c:c + L_W], g_rw_t[c + 128:c + 128 + L_A],
                            g_rw_t[c + 256:c + 256 + L_G], g_gate_t], axis=0)


def _all_gather_two_level(name, bufs):
    nb = len(bufs)

    def body(*refs):
        in_refs, out_refs = refs[:nb], refs[nb:2 * nb]
        send_sems, recv_sems, local_sems = refs[2 * nb:]
        x, y, c = lax.axis_index("x"), lax.axis_index("y"), lax.axis_index("c")
        me, sibling = (x, y, c), (x, y, 1 - c)
        chips = [(1 - x, y), (x, 1 - y), (1 - x, 1 - y)]

        def slot(b, dev):
            return out_refs[b].at[4 * dev[0] + 2 * dev[1] + dev[2]]

        def copy(b, k, block, to, own=False):
            return pltpu.make_async_remote_copy(
                src_ref=in_refs[b] if own else slot(b, block), dst_ref=slot(b, block),
                send_sem=send_sems.at[7 * b + k], recv_sem=recv_sems.at[7 * b + k], device_id=to,
                device_id_type=pl.DeviceIdType.MESH)

        local = [pltpu.make_async_copy(in_refs[b], slot(b, me), local_sems.at[b]) for b in range(nb)]
        first = [copy(b, 0, me, sibling, own=True) for b in range(nb)]
        first += [copy(b, 1 + j, me, (*chip, c), own=True) for j, chip in enumerate(chips) for b in range(nb)]
        for cp in local + first:
            cp.start()
        passed = []
        for j, chip in enumerate(chips):
            for b in range(nb):
                copy(b, 1 + j, (*chip, c), me).wait_recv()
                passed.append(copy(b, 4 + j, (*chip, c), sibling))
                passed[-1].start()
        for b in range(nb):
            copy(b, 0, sibling, me).wait_recv()
        for j, chip in enumerate(chips):
            for b in range(nb):
                copy(b, 4 + j, (*chip, 1 - c), me).wait_recv()
        for cp in first + passed:
            cp.wait_send()
        for cp in local:
            cp.wait()

    any_spec = pl.BlockSpec(memory_space=pl.ANY)
    return pl.pallas_call(
        body, name=name, in_specs=[any_spec] * nb, out_specs=[any_spec] * nb,
        out_shape=[jax.ShapeDtypeStruct((N_DEV,) + b.shape, b.dtype) for b in bufs],
        scratch_shapes=[pltpu.SemaphoreType.DMA((7 * nb,)), pltpu.SemaphoreType.DMA((7 * nb,)),
                        pltpu.SemaphoreType.DMA((nb,))],
    )(*bufs)


def _cols_from_blocks(blk):
    return jnp.transpose(blk, (1, 0, 2)).reshape(blk.shape[1], -1)


def _cols_to_blocks(g):
    r, c = g.shape
    return jnp.transpose(g.reshape(r, N_DEV, c // N_DEV), (1, 0, 2))


FIRST_WEIGHTS = ["w_in", "shift_b", "w_lora_w", "a_lora_w", "g_lora_w"]
LATE_WEIGHTS = ["w_proj_a", "w_proj_b", "w_out", "w_ffn1", "w_ffn2"]


def _late_weights(shards):
    ex = Exchange([shards[n].astype(BF16) for n in LATE_WEIGHTS], [True] * len(LATE_WEIGHTS))

    def finish(results):
        got = dict(zip(LATE_WEIGHTS, results))
        W = {n: got[n].reshape(-1, D) for n in ("w_proj_a", "w_proj_b", "w_out", "w_ffn2")}
        W["w_ffn1"] = got["w_ffn1"].reshape(N_DEV, D, -1)
        return W
    return ex, finish


def _gather_weights(shards):
    def payload(n):
        if n == "w_in":
            return jnp.transpose(shards[n][0]).astype(BF16)
        return shards[n] if n == "shift_b" else shards[n].astype(BF16)
    got = dict(zip(FIRST_WEIGHTS, _all_gather_two_level("weight_all_gather", [payload(n) for n in FIRST_WEIGHTS])))
    W = {}
    W["w_sgu_t"], W["w_rw_t"], W["w_gate_t"] = _w_in_groups_t(got["w_in"].reshape(P_TOTAL, D))
    z = lambda r, c, dt: jnp.zeros((r, c), dt)
    W["w_lora"] = jnp.concatenate([_cols_from_blocks(got["w_lora_w"][:, 0]).astype(F32), z(128 - L_W, D, F32)], axis=0)
    W["a_lora"] = jnp.concatenate([_cols_from_blocks(got["a_lora_w"][:, 0]).astype(F32), z(128 - L_A, D, F32)], axis=0)
    W["g_lora"] = jnp.concatenate([_cols_from_blocks(got["g_lora_w"][:, 0]).astype(F32), z(256 - L_G, D, F32)], axis=0)
    sb = _cols_from_blocks(got["shift_b"][:, 0])
    W["sb"] = jnp.concatenate([sb[:, :3 * D], sb[:, 3 * D:3 * D + L_W], z(2, 128 - L_W, F32),
                               sb[:, 3 * D + L_W:3 * D + L_W + L_A], z(2, 128 - L_A, F32),
                               sb[:, 3 * D + L_W + L_A:], z(2, 256 - L_G, F32)], axis=1)
    return W


def _replicated_weights(rep):
    W = {n: rep[n] for n in ("g_mix", "sgu_ln_w", "sgu_ln_b", "w0", "a0", "k_k", "k_a", "r_k", "ln_x_w", "ln_x_b",
                             "g_ffn")}
    W["g_final"] = rep["g_final"].reshape(1, D)
    W["sgu_w"] = rep["sgu_w"][0]
    W["sgu_bt"] = jnp.transpose(rep["sgu_b"][0])
    return W


def _late_grad_blocks(G):
    blocks = {n: G[n].reshape(N_DEV, -1, D) for n in ("w_proj_a", "w_proj_b", "w_out", "w_ffn2")}
    blocks["w_ffn1"] = G["w_ffn1"]
    return Exchange([blocks[n] for n in LATE_WEIGHTS], [False] * len(LATE_WEIGHTS))


def _first_grad_blocks(G):
    sbg = G["sb"]
    c = 3 * D
    sb = jnp.concatenate([sbg[:, :c], sbg[:, c:c + L_W], sbg[:, c + 128:c + 128 + L_A],
                          sbg[:, c + 256:c + 256 + L_G]], axis=1)
    return {
        "w_in": _w_in_grad_t(G["w_sgu_t"], G["w_rw_t"], G["w_gate_t"]).reshape(N_DEV, W_IN_SHARD, D),
        "shift_b": _cols_to_blocks(sb),
        "w_lora_w": _cols_to_blocks(G["w_lora"][:L_W]), "a_lora_w": _cols_to_blocks(G["a_lora"][:L_A]),
        "g_lora_w": _cols_to_blocks(G["g_lora"][:L_G]),
    }


def _replicated_grads(G):
    small = {n: G[n] for n in ("g_mix", "sgu_ln_w", "sgu_ln_b", "w0", "a0", "k_k", "k_a", "r_k", "ln_x_w", "ln_x_b",
                               "g_ffn", "g_final")}
    small["sgu_w"] = G["sgu_w"]
    small["sgu_b"] = jnp.transpose(G["sgu_bt"])
    return small


def kernel(x, g_mix, w_in, sgu_ln_w, sgu_ln_b, sgu_w, sgu_b, w_proj_a, shift_b, w_lora_w, w0, a_lora_w, a0, g_lora_w, k_k, k_a, r_k, ln_x_w, ln_x_b, w_proj_b, w_out, g_ffn, w_ffn1, w_ffn2, g_final, loss_target, m_g_mix, m_w_in, m_sgu_ln_w, m_sgu_ln_b, m_sgu_w, m_sgu_b, m_w_proj_a, m_shift_b, m_w_lora_w, m_w0, m_a_lora_w, m_a0, m_g_lora_w, m_k_k, m_k_a, m_r_k, m_ln_x_w, m_ln_x_b, m_w_proj_b, m_w_out, m_g_ffn, m_w_ffn1, m_w_ffn2, m_g_final, v_g_mix, v_w_in, v_sgu_ln_w, v_sgu_ln_b, v_sgu_w, v_sgu_b, v_w_proj_a, v_shift_b, v_w_lora_w, v_w0, v_a_lora_w, v_a0, v_g_lora_w, v_k_k, v_k_a, v_r_k, v_ln_x_w, v_ln_x_b, v_w_proj_b, v_w_out, v_g_ffn, v_w_ffn1, v_w_ffn2, v_g_final):
    env = dict(locals())
    weights = {n: env[n] for n in WEIGHT_ORDER}
    moms = {n: env["m_" + n] for n in WEIGHT_ORDER}
    vars_ = {n: env["v_" + n] for n in WEIGHT_ORDER}

    shards = {n: weights[n] for n, _, _ in SHARDED}
    W = _gather_weights(shards)
    W.update(_replicated_weights({n: weights[n] for n, _ in REPLICATED}))
    loss_part, dx, G, late_slots = _local_step(x[0], loss_target[0], W, late_weights=_late_weights(shards),
                                               early_grads=_late_grad_blocks)

    slots = dict(zip(LATE_WEIGHTS, late_slots))
    blocks = _first_grad_blocks(G)
    small = _replicated_grads(G)
    small_parts = [small[n] for n, _ in REPLICATED] + [jnp.full((PACK_W,), loss_part, F32)]
    res = _exchange("grad_exchange", [blocks[n] for n in FIRST_WEIGHTS] + [_pack_rows(small_parts, _SMALL_ROWS, F32)],
                    [False] * len(FIRST_WEIGHTS) + [True])
    slots.update(zip(FIRST_WEIGHTS, res[:-1]))
    small_slots = res[-1]

    outs = [dict(), dict(), dict(), dict()]
    for n, _, _ in SHARDED:
        if n == "w_in":
            res = _adamw("adamw_" + n, slots[n], *[jnp.transpose(t[0]) for t in (weights[n], moms[n], vars_[n])])
            res = [jnp.transpose(t)[None] for t in res]
        else:
            res = _adamw("adamw_" + n, slots[n], weights[n], moms[n], vars_[n])
        for k in range(4):
            outs[k][n] = res[k]

    def packed(d):
        return _pack_rows([d[n] for n, _ in REPLICATED], _SMALL_ROWS, F32)
    small_out = _adamw("adamw_replicated", small_slots, packed(weights), packed(moms), packed(vars_))
    for k in range(4):
        flat = small_out[k].reshape(-1)
        off = 0
        for (n, s), size in zip(REPLICATED, _SMALL_SIZES):
            outs[k][n] = flat[off:off + size].reshape(s)
            off += size
    loss = small_out[0].reshape(-1)[_LOSS_AT]
    return (loss, dx[None], *[outs[0][n] for n in WEIGHT_ORDER], *[outs[1][n] for n in WEIGHT_ORDER],
            *[outs[2][n] for n in WEIGHT_ORDER], *[outs[3][n] for n in WEIGHT_ORDER])
```

```python
import functools
import numpy as np
import jax
import jax.numpy as jnp
from jax import lax
from jax.experimental import pallas as pl
from jax.experimental.pallas import tpu as pltpu

F32 = jnp.float32
BF16 = jnp.bfloat16

D = 1024
NH, HN = 16, 64
SGU_G, SGU_C = 8, 128
L_W, L_A, L_G = 64, 64, 160
C_B = 3 * D + L_W + L_A + L_G
P_TOTAL = 2 * D + C_B + 2 * D
D_FF = 4 * D
RW_INT = 3 * D + 128 + 128 + 256
NORM_EPS, LN_EPS, GN_EPS = 1e-6, 1e-5, 64e-5
N_DEV = 8
LANES = 128
SCAN_C = 64
SOLVE_B = 16
SCAN_PRECISION = lax.Precision.HIGH
GRAD_PAYLOAD = BF16
VMEM_LIMIT = 56 * 1024 * 1024

ADAM_LR, ADAM_B1, ADAM_B2, ADAM_EPS, ADAM_WD, ADAM_STEP = 0.001, 0.9, 0.999, 1e-08, 0.01, 10

SHARDED = [
    ("w_in", (D, P_TOTAL), 1), ("w_proj_a", (D, D), 0), ("shift_b", (2, C_B), 1), ("w_lora_w", (L_W, D), 1),
    ("a_lora_w", (L_A, D), 1), ("g_lora_w", (L_G, D), 1), ("w_proj_b", (D, D), 0), ("w_out", (D, D), 0),
    ("w_ffn1", (D, D_FF), 1), ("w_ffn2", (D_FF, D), 0),
]
REPLICATED = [
    ("g_mix", (1, D)), ("sgu_ln_w", (1, D)), ("sgu_ln_b", (1, D)), ("sgu_w", (1, SGU_G, SGU_C, SGU_C)),
    ("sgu_b", (1, SGU_G, SGU_C)), ("w0", (1, D)), ("a0", (1, D)), ("k_k", (1, D)), ("k_a", (1, D)), ("r_k", (1, D)),
    ("ln_x_w", (1, D)), ("ln_x_b", (1, D)), ("g_ffn", (1, D)), ("g_final", (D,)),
]
WEIGHT_ORDER = ["g_mix", "w_in", "sgu_ln_w", "sgu_ln_b", "sgu_w", "sgu_b", "w_proj_a", "shift_b", "w_lora_w", "w0",
                "a_lora_w", "a0", "g_lora_w", "k_k", "k_a", "r_k", "ln_x_w", "ln_x_b", "w_proj_b", "w_out", "g_ffn",
                "w_ffn1", "w_ffn2", "g_final"]


def _shard_shape(shape, axis):
    s = list(shape)
    s[axis] //= N_DEV
    return tuple(s)


def _round_up(n, m):
    return (n + m - 1) // m * m


def _pick(n, target):
    if n <= target:
        return n
    best = None
    for t in range(LANES, target + 1, LANES):
        if n % t == 0:
            best = t
    assert best is not None, (n, target)
    return best


def _matmul(name, a, b, mode, out_dtype=F32, tm=1024, tn=1024, tk=1024, out_blocks=None):
    b_blocks = b.shape[0] if b.ndim == 3 else None
    bshape = b.shape if b.ndim == 2 else (b.shape[1], b.shape[0] * b.shape[2])
    if mode == "nn":
        (M, K), (K2, N) = a.shape, bshape
    elif mode == "nt":
        (M, K), (N, K2) = a.shape, bshape
    else:
        (K, M), (K2, N) = a.shape, bshape
    assert K == K2, (name, a.shape, b.shape)
    assert b_blocks is None or mode != "tn"
    assert out_blocks is None or mode == "tn"
    tn = min(tn, N // (out_blocks or 1), bshape[1] // b_blocks if (b_blocks and mode == "nn") else tn)
    tk = min(tk, bshape[1] // b_blocks if (b_blocks and mode == "nt") else tk)
    tm, tn, tk = _pick(M, tm), _pick(N, tn), _pick(K, tk)
    nk = K // tk
    dims = {"nn": (((1,), (0,)), ((), ())), "nt": (((1,), (1,)), ((), ())), "tn": (((0,), (0,)), ((), ()))}[mode]

    def body(a_ref, b_ref, o_ref, acc_ref):
        k = pl.program_id(2)

        @pl.when(k == 0)
        def _():
            acc_ref[...] = jnp.zeros_like(acc_ref)

        acc_ref[...] += lax.dot_general(a_ref[...].astype(BF16), b_ref[...].astype(BF16), dims,
                                        preferred_element_type=F32)

        @pl.when(k == nk - 1)
        def _():
            o_ref[...] = acc_ref[...].astype(o_ref.dtype)

    a_spec = {"nn": pl.BlockSpec((tm, tk), lambda i, j, k: (i, k)), "nt": pl.BlockSpec((tm, tk), lambda i, j, k: (i, k)),
              "tn": pl.BlockSpec((tk, tm), lambda i, j, k: (k, i))}[mode]
    b_spec = {"nn": pl.BlockSpec((tk, tn), lambda i, j, k: (k, j)), "nt": pl.BlockSpec((tn, tk), lambda i, j, k: (j, k)),
              "tn": pl.BlockSpec((tk, tn), lambda i, j, k: (k, j))}[mode]
    if b_blocks and mode == "nn":
        per = b.shape[2] // tn
        b_spec = pl.BlockSpec((None, tk, tn), lambda i, j, k: (j // per, k, j % per))
    elif b_blocks:
        per = b.shape[2] // tk
        b_spec = pl.BlockSpec((None, tn, tk), lambda i, j, k: (k // per, j, k % per))
    out_spec = pl.BlockSpec((tm, tn), lambda i, j, k: (i, j))
    out_shape = jax.ShapeDtypeStruct((M, N), out_dtype)
    if out_blocks:
        per_o = N // out_blocks // tn
        out_spec = pl.BlockSpec((None, tm, tn), lambda i, j, k: (j // per_o, i, j % per_o))
        out_shape = jax.ShapeDtypeStruct((out_blocks, M, N // out_blocks), out_dtype)
    return pl.pallas_call(
        body, name=name, grid=(M // tm, N // tn, nk), in_specs=[a_spec, b_spec],
        out_specs=out_spec, out_shape=out_shape, scratch_shapes=[pltpu.VMEM((tm, tn), F32)],
        compiler_params=pltpu.CompilerParams(dimension_semantics=("parallel", "parallel", "arbitrary"),
                                             vmem_limit_bytes=VMEM_LIMIT),
    )(a, b)


class Rows:
    def __init__(self, arr, width=None, cb=0):
        self.arr, self.width, self.cb = arr, (arr.shape[1] if width is None else width), cb


class Heads:
    def __init__(self, arr):
        self.arr = arr


class Halo:
    def __init__(self, arr, side):
        self.arr, self.side = arr, side


def _rows_call(name, fn, ins, consts, outs, accs=(), tm=256, with_pid=False):
    T = next(o.arr.shape[1] if isinstance(o, Heads) else o.arr.shape[0] for o in ins if not isinstance(o, Halo))
    tm = min(tm, T)
    n_tiles = T // tm
    n_in, n_c, n_out = len(ins), len(consts), len(outs)
    in_specs, args = [], []
    for o in ins:
        if isinstance(o, Rows):
            in_specs.append(pl.BlockSpec((tm, o.width), lambda i, cb=o.cb: (i, cb)))
        elif isinstance(o, Heads):
            in_specs.append(pl.BlockSpec((NH, tm, HN), lambda i: (0, i, 0)))
        else:
            w = o.arr.shape[1]
            if o.side < 0:
                in_specs.append(pl.BlockSpec((8, w), lambda i: (jnp.maximum(i * (tm // 8) - 1, 0), 0)))
            else:
                in_specs.append(pl.BlockSpec((8, w), lambda i: (jnp.minimum((i + 1) * (tm // 8), T // 8 - 1), 0)))
        args.append(o.arr)
    for c in consts:
        in_specs.append(pl.BlockSpec(c.shape, lambda i, nd=c.ndim: (0,) * nd))
        args.append(c)
    out_specs, out_shape = [], []
    for o in outs:
        if o[0] == "rows":
            out_specs.append(pl.BlockSpec((tm, o[1]), lambda i: (i, 0)))
            out_shape.append(jax.ShapeDtypeStruct((T, o[1]), o[2]))
        else:
            out_specs.append(pl.BlockSpec((NH, tm, HN), lambda i: (0, i, 0)))
            out_shape.append(jax.ShapeDtypeStruct((NH, T, HN), o[1]))
    for shape, dt in accs:
        out_specs.append(pl.BlockSpec(shape, lambda i, nd=len(shape): (0,) * nd))
        out_shape.append(jax.ShapeDtypeStruct(shape, dt))

    def body(*refs):
        i = pl.program_id(0)
        vals = []
        vals = [r[...] for r in refs[:n_in + n_c]]
        res = fn(i, n_tiles, *vals) if with_pid else fn(*vals)
        out_refs = refs[n_in + n_c:]
        for r, v in zip(out_refs[:n_out], res[:n_out]):
            r[...] = v.astype(r.dtype)
        if accs:
            @pl.when(i == 0)
            def _():
                for r in out_refs[n_out:]:
                    r[...] = jnp.zeros_like(r)

            for r, v in zip(out_refs[n_out:], res[n_out:]):
                r[...] += v.astype(r.dtype)

    res = pl.pallas_call(
        body, name=name, grid=(n_tiles,), in_specs=in_specs, out_specs=out_specs, out_shape=out_shape,
        compiler_params=pltpu.CompilerParams(dimension_semantics=("arbitrary",), vmem_limit_bytes=VMEM_LIMIT),
    )(*args)
    return res


def _rms(x, g):
    return x * lax.rsqrt(jnp.mean(x * x, axis=-1, keepdims=True) + NORM_EPS) * g


def _gelu(x):
    return 0.5 * x * (1.0 + lax.erf(x * 0.7071067811865476))


def _sigmoid(x):
    return 1.0 / (1.0 + jnp.exp(-x))


def _bdot(a, b):
    return jnp.dot(a.astype(BF16), b.astype(BF16), preferred_element_type=F32)


def _to_heads(x):
    return jnp.concatenate([x[:, h * HN:(h + 1) * HN][None] for h in range(NH)], axis=0)


def _from_heads(xh):
    return jnp.concatenate([xh[h] for h in range(NH)], axis=-1)


def _sgu_fn(p, ln_w, ln_b, sw, sbt):
    z = _gelu(p)
    u, v = z[:, :D], z[:, D:]
    mu = jnp.mean(v, axis=-1, keepdims=True)
    var = jnp.mean(jnp.square(v - mu), axis=-1, keepdims=True)
    vn = (v - mu) * lax.rsqrt(var + LN_EPS) * ln_w + ln_b
    ri = lax.broadcasted_iota(jnp.int32, (SGU_C, SGU_C), 0)
    ci = lax.broadcasted_iota(jnp.int32, (SGU_C, SGU_C), 1)
    mask = (ci <= ri).astype(F32)
    dg = D // SGU_G
    parts = []
    for g in range(SGU_G):
        parts.append(_bdot(sw[g] * mask, vn[:, g * dg:(g + 1) * dg]) + sbt[:, g:g + 1])
    return u * jnp.concatenate(parts, axis=-1)


def _pre_fn(qr, qk, qv, qxw, qxa, qxg, wl, w0, al, a0, gl, k_k, k_a):
    w = -jax.nn.softplus(-(w0 + _bdot(jnp.tanh(qxw), wl))) - 0.5
    lw = -jnp.exp(w)
    aa = _sigmoid(a0 + _bdot(qxa, al))
    g = _bdot(_sigmoid(qxg), gl)
    kk = _to_heads(qk * k_k)
    kk = kk / jnp.maximum(jnp.sqrt(jnp.sum(kk * kk, axis=-1, keepdims=True)), 1e-12)
    k2 = qk * (1.0 + (aa - 1.0) * k_a)
    return _to_heads(qr), _to_heads(lw), _to_heads(k2), _to_heads(qv), kk, _to_heads(aa), g


def _post_fn(o, r, k2, v, g, ln_w, ln_b, r_k):
    mu = jnp.mean(o, axis=-1, keepdims=True)
    d = o - mu
    var = jnp.mean(d * d, axis=-1, keepdims=True)
    on = d * lax.rsqrt(var + GN_EPS) * ln_w + ln_b
    bonus = jnp.sum(r * k2 * r_k, axis=-1, keepdims=True) * v
    return _from_heads(on + bonus) * g


def _gate_fn(pg, ya, yb):
    return _sigmoid(pg[:, :D]) * ya + _sigmoid(pg[:, D:]) * yb


def _bmm(x, y, cx, cy):
    return lax.dot_general(x, y, (((cx,), (cy,)), ((0,), (0,))), precision=SCAN_PRECISION,
                           preferred_element_type=F32)


def _unit_lower_inverse(M):
    C = M.shape[1]
    ti = lax.broadcasted_iota(jnp.int32, (C, C), 0)
    tj = lax.broadcasted_iota(jnp.int32, (C, C), 1)
    eye = (ti == tj).astype(F32)
    same = lambda b: (ti // b == tj // b).astype(F32)
    X = -(M * same(SOLVE_B))
    inv = eye + X
    span = 1
    while 2 * span < SOLVE_B:
        X = _bmm(X, X, 2, 1)
        inv = inv + _bmm(inv, X, 2, 1)
        span *= 2
    b = SOLVE_B
    while b < C:
        low = M * (same(2 * b) - same(b))
        inv = inv - _bmm(_bmm(inv, low, 2, 1), inv, 2, 1)
        b *= 2
    return inv


@jax.custom_vjp
def _unit_lower_solve(M, y):
    return _bmm(_unit_lower_inverse(M), y, 2, 1)


def _unit_lower_solve_fwd(M, y):
    inv = _unit_lower_inverse(M)
    u = _bmm(inv, y, 2, 1)
    return u, (inv, u)


def _unit_lower_solve_bwd(res, du):
    inv, u = res
    dy = _bmm(inv, du, 1, 1)
    return -_bmm(dy, u, 2, 2), dy


_unit_lower_solve.defvjp(_unit_lower_solve_fwd, _unit_lower_solve_bwd)


def _chunk_fn(S0, r, lw, k, v, kk, a):
    C = SCAN_C
    bmm = _bmm
    ti = lax.broadcasted_iota(jnp.int32, (C, C), 0)
    tj = lax.broadcasted_iota(jnp.int32, (C, C), 1)
    incl = (tj <= ti).astype(F32)
    strict = (tj < ti).astype(F32)
    cum = lax.dot_general(jnp.broadcast_to(incl, (NH, C, C)), lw, (((2,), (1,)), ((0,), (0,))),
                          precision=lax.Precision.HIGHEST, preferred_element_type=F32)
    g_in, g_ex, g_inv = jnp.exp(cum), jnp.exp(cum - lw), jnp.exp(-cum)
    kkt, rt = kk * g_ex, r * g_in
    bk = jnp.concatenate([kk * a * g_inv, k * g_inv], axis=1)
    A = bmm(kkt, bk, 2, 2)
    M = A[:, :, :C] * strict
    n_mask = jnp.concatenate([jnp.zeros((C, C), F32), strict], axis=1)
    zv = jnp.concatenate([jnp.zeros_like(v), v], axis=1)
    s0_side = bmm(jnp.concatenate([kkt, rt], axis=1), S0, 2, 2)
    y = _unit_lower_solve(M, s0_side[:, :C] + bmm(A * n_mask, zv, 2, 1))
    z = jnp.concatenate([-y, v], axis=1)
    O = s0_side[:, C:] + bmm(bmm(rt, bk, 2, 2) * jnp.concatenate([incl, incl], axis=1), z, 2, 1)
    S1 = (S0 + bmm(z, bk, 1, 1)) * g_in[:, C - 1:C, :]
    return O, S1


def _scan_fwd(r, lw, k, v, kk, a, ex=None, tb=256):
    T = r.shape[1]
    tb = min(tb, T)
    n_chunks = tb // SCAN_C
    nb = T // tb
    nx = ex.nb if ex else 0

    def body(*refs):
        r_ref, lw_ref, k_ref, v_ref, kk_ref, a_ref = refs[:6]
        x_in, (o_ref, s0_ref), x_out = refs[6:6 + nx], refs[6 + nx:8 + nx], refs[8 + nx:8 + 2 * nx]
        s_ref, sems = refs[8 + 2 * nx], refs[9 + 2 * nx:]

        @pl.when(pl.program_id(0) == 0)
        def _():
            s_ref[...] = jnp.zeros_like(s_ref)
            if ex:
                ex.start(x_in, x_out, sems)

        def step(c, carry):
            sl = pl.ds(pl.multiple_of(c * SCAN_C, SCAN_C), SCAN_C)
            S0 = s_ref[...]
            s0_ref[c] = S0
            O, S1 = _chunk_fn(S0, r_ref[:, sl, :], lw_ref[:, sl, :], k_ref[:, sl, :], v_ref[:, sl, :],
                              kk_ref[:, sl, :], a_ref[:, sl, :])
            o_ref[:, sl, :] = O
            s_ref[...] = S1
            return carry

        lax.fori_loop(0, n_chunks, step, 0)

        if ex:
            @pl.when(pl.program_id(0) == nb - 1)
            def _():
                ex.wait(x_in, x_out, sems)

    hm = pl.BlockSpec((NH, tb, HN), lambda i: (0, i, 0))
    res = pl.pallas_call(
        body, name="rwkv_scan_fwd", grid=(nb,), in_specs=[hm] * 6 + (ex.any_specs if ex else []),
        out_specs=[hm, pl.BlockSpec((n_chunks, NH, HN, HN), lambda i: (i, 0, 0, 0))] + (ex.any_specs if ex else []),
        out_shape=[jax.ShapeDtypeStruct((NH, T, HN), F32), jax.ShapeDtypeStruct((T // SCAN_C, NH, HN, HN), F32)]
        + (ex.out_shape if ex else []),
        scratch_shapes=[pltpu.VMEM((NH, HN, HN), F32)] + (ex.sem_shapes if ex else []),
        compiler_params=pltpu.CompilerParams(dimension_semantics=("arbitrary",), vmem_limit_bytes=VMEM_LIMIT),
    )(r, lw, k, v, kk, a, *(ex.bufs if ex else []))
    return res[0], res[1], list(res[2:])


def _scan_bwd(r, lw, k, v, kk, a, s0s, do, ex=None, tb=128):
    T = r.shape[1]
    tb = min(tb, T)
    n_chunks = tb // SCAN_C
    nb = T // tb
    nx = ex.nb if ex else 0

    def body(*refs):
        r_ref, lw_ref, k_ref, v_ref, kk_ref, a_ref, s0_ref, do_ref = refs[:8]
        x_in, (dr, dlw, dk, dv, dkk, da), x_out = refs[8:8 + nx], refs[8 + nx:14 + nx], refs[14 + nx:14 + 2 * nx]
        ds_ref, sems = refs[14 + 2 * nx], refs[15 + 2 * nx:]

        @pl.when(pl.program_id(0) == 0)
        def _():
            ds_ref[...] = jnp.zeros_like(ds_ref)
            if ex:
                ex.start(x_in, x_out, sems)

        def step(j, carry):
            c = n_chunks - 1 - j
            sl = pl.ds(pl.multiple_of(c * SCAN_C, SCAN_C), SCAN_C)
            _, vjp = jax.vjp(_chunk_fn, s0_ref[c], r_ref[:, sl, :], lw_ref[:, sl, :], k_ref[:, sl, :],
                             v_ref[:, sl, :], kk_ref[:, sl, :], a_ref[:, sl, :])
            g = vjp((do_ref[:, sl, :], ds_ref[...]))
            ds_ref[...] = g[0]
            for ref, val in zip((dr, dlw, dk, dv, dkk, da), g[1:]):
                ref[:, sl, :] = val
            return carry

        lax.fori_loop(0, n_chunks, step, 0)

        if ex:
            @pl.when(pl.program_id(0) == nb - 1)
            def _():
                ex.wait(x_in, x_out, sems)

    hm = pl.BlockSpec((NH, tb, HN), lambda i: (0, nb - 1 - i, 0))
    res = pl.pallas_call(
        body, name="rwkv_scan_bwd", grid=(nb,),
        in_specs=[hm] * 6 + [pl.BlockSpec((n_chunks, NH, HN, HN), lambda i: (nb - 1 - i, 0, 0, 0)), hm]
        + (ex.any_specs if ex else []),
        out_specs=[hm] * 6 + (ex.any_specs if ex else []),
        out_shape=[jax.ShapeDtypeStruct((NH, T, HN), F32)] * 6 + (ex.out_shape if ex else []),
        scratch_shapes=[pltpu.VMEM((NH, HN, HN), F32)] + (ex.sem_shapes if ex else []),
        compiler_params=pltpu.CompilerParams(dimension_semantics=("arbitrary",), vmem_limit_bytes=VMEM_LIMIT),
    )(r, lw, k, v, kk, a, s0s, do, *(ex.bufs if ex else []))
    return list(res[:6]), list(res[6:])


def _shift_down(i, p, prev8):
    first = jnp.where(i > 0, prev8[7:8, :], 0.0)
    row = lax.broadcasted_iota(jnp.int32, p.shape, 0)
    return jnp.where(row == 0, first, pltpu.roll(p, 1, axis=0))


def _mix_fwd(p, sb, tm=256):
    def fn(i, n, p, prev8, sb):
        return (p * sb[0:1] + _shift_down(i, p, prev8) * sb[1:2],)
    return _rows_call("shift_mix_fwd", fn, [Rows(p), Halo(p, -1)], [sb], [("rows", p.shape[1], F32)], tm=tm,
                      with_pid=True)[0]


def _mix_bwd(dq, p, sb, tm=256):
    def fn(i, n, dq, next8, p, prev8, sb):
        ps = _shift_down(i, p, prev8)
        d1 = dq * sb[1:2]
        last = jnp.where(i < n - 1, next8[0:1, :] * sb[1:2], 0.0)
        row = lax.broadcasted_iota(jnp.int32, dq.shape, 0)
        up = jnp.where(row == dq.shape[0] - 1, last, pltpu.roll(d1, dq.shape[0] - 1, axis=0))
        return (dq * sb[0:1] + up, jnp.sum(dq * p, axis=0, keepdims=True), jnp.sum(dq * ps, axis=0, keepdims=True))
    w = p.shape[1]
    return _rows_call("shift_mix_bwd", fn, [Rows(dq), Halo(dq, +1), Rows(p), Halo(p, -1)], [sb], [("rows", w, F32)],
                      accs=[((1, w), F32), ((1, w), F32)], tm=tm, with_pid=True)


def _local_step(x, target, W, late_weights=None, early_grads=None):
    G = {}
    a = _rows_call("norm_mix_fwd", lambda x, g: (_rms(x, g),), [Rows(x)], [W["g_mix"]], [("rows", D, BF16)])[0]
    p_sgu = _matmul("proj_sgu", a, W["w_sgu_t"], "nt")
    p_rw = _matmul("proj_rwkv", a, W["w_rw_t"], "nt")
    p_gate = _matmul("proj_gate", a, W["w_gate_t"], "nt")

    sgu_consts = [W["sgu_ln_w"], W["sgu_ln_b"], W["sgu_w"], W["sgu_bt"]]
    s = _rows_call("sgu_fwd", lambda *t: (_sgu_fn(*t),), [Rows(p_sgu)], sgu_consts, [("rows", D, BF16)], tm=SGU_C)[0]

    q = _mix_fwd(p_rw, W["sb"])
    q_ins = [Rows(q, D, 0), Rows(q, D, 1), Rows(q, D, 2), Rows(q, 128, 24), Rows(q, 128, 25), Rows(q, 256, 13)]
    pre_consts = [W["w_lora"], W["w0"], W["a_lora"], W["a0"], W["g_lora"], W["k_k"], W["k_a"]]
    r_h, lw_h, k_h, v_h, kk_h, a_h, g_gate = _rows_call(
        "rwkv_pre_fwd", _pre_fn, q_ins, pre_consts, [("heads", F32)] * 6 + [("rows", D, F32)], tm=128)
    o_h, s0s, got = _scan_fwd(r_h, lw_h, k_h, v_h, kk_h, a_h, ex=late_weights[0] if late_weights else None)
    if late_weights:
        W = {**W, **late_weights[1](got)}
    y_a = _matmul("proj_a", s, W["w_proj_a"], "nn")
    post_ins = [Heads(o_h), Heads(r_h), Heads(k_h), Heads(v_h), Rows(g_gate)]
    post_consts = [W[n].reshape(NH, 1, HN) for n in ("ln_x_w", "ln_x_b", "r_k")]
    z_b = _rows_call("rwkv_post_fwd", lambda *t: (_post_fn(*t),), post_ins, post_consts, [("rows", D, BF16)], tm=128)[0]
    y_b = _matmul("proj_b", z_b, W["w_proj_b"], "nn")

    gate_ins = [Rows(p_gate), Rows(y_a), Rows(y_b)]
    mixed = _rows_call("gate_fwd", lambda *t: (_gate_fn(*t),), gate_ins, [], [("rows", D, BF16)])[0]
    mo = _matmul("proj_out", mixed, W["w_out"], "nn")

    def res1(x, mo, g):
        h1 = x + mo
        return h1, _rms(h1, g)
    h1, f = _rows_call("residual_norm_fwd", res1, [Rows(x), Rows(mo)], [W["g_ffn"]],
                       [("rows", D, F32), ("rows", D, BF16)])
    u1 = _matmul("ffn_up", f, W["w_ffn1"], "nn")
    act = _rows_call("ffn_act_fwd", lambda u: (jnp.square(jnp.maximum(u, 0.0)),), [Rows(u1)], [],
                     [("rows", D_FF, BF16)])[0]
    ff = _matmul("ffn_down", act, W["w_ffn2"], "nn")

    def head(h1, ff, tgt, g):
        def f_(h1, ff, g):
            y = _rms(h1 + ff, g)
            return 0.5 * jnp.sum(jnp.mean(jnp.square(y - tgt), axis=-1))
        loss, (dh2, _, dg) = jax.value_and_grad(f_, argnums=(0, 1, 2))(h1, ff, g)
        return dh2, jnp.full((8, LANES), loss, F32), dg
    dh2, loss_acc, G["g_final"] = _rows_call("loss_head", head, [Rows(h1), Rows(ff), Rows(target)], [W["g_final"]],
                                             [("rows", D, F32)], accs=[((8, LANES), F32), ((1, D), F32)])

    d_act = _matmul("ffn_down_dx", dh2, W["w_ffn2"], "nt")
    G["w_ffn2"] = _matmul("ffn_down_dw", act, dh2, "tn", out_dtype=GRAD_PAYLOAD)
    d_u1 = _rows_call("ffn_act_bwd", lambda u, d: (d * 2.0 * jnp.maximum(u, 0.0),), [Rows(u1), Rows(d_act)], [],
                      [("rows", D_FF, BF16)])[0]
    d_f = _matmul("ffn_up_dx", d_u1, W["w_ffn1"], "nt")
    G["w_ffn1"] = _matmul("ffn_up_dw", f, d_u1, "tn", out_blocks=N_DEV, out_dtype=GRAD_PAYLOAD)

    def res1_bwd(x, mo, d_f, dh2, g):
        _, vjp = jax.vjp(lambda h, g: _rms(h, g), x + mo, g)
        dh, dg = vjp(d_f)
        return dh2 + dh, dg
    dh1, G["g_ffn"] = _rows_call("residual_norm_bwd", res1_bwd, [Rows(x), Rows(mo), Rows(d_f), Rows(dh2)],
                                 [W["g_ffn"]], [("rows", D, F32)], accs=[((1, D), F32)])
    d_mixed = _matmul("proj_out_dx", dh1, W["w_out"], "nt")
    G["w_out"] = _matmul("proj_out_dw", mixed, dh1, "tn", out_dtype=GRAD_PAYLOAD)

    def gate_bwd(pg, ya, yb, dm):
        _, vjp = jax.vjp(_gate_fn, pg, ya, yb)
        return vjp(dm)
    d_gate, d_ya, d_yb = _rows_call("gate_bwd", gate_bwd, gate_ins + [Rows(d_mixed)], [],
                                    [("rows", 2 * D, F32), ("rows", D, BF16), ("rows", D, BF16)])

    d_s = _matmul("proj_a_dx", d_ya, W["w_proj_a"], "nt")
    G["w_proj_a"] = _matmul("proj_a_dw", s, d_ya, "tn", out_dtype=GRAD_PAYLOAD)

    def sgu_bwd(p, ds, *c):
        _, vjp = jax.vjp(_sgu_fn, p, *c)
        return vjp(ds)
    d_p_sgu, G["sgu_ln_w"], G["sgu_ln_b"], G["sgu_w"], G["sgu_bt"] = _rows_call(
        "sgu_bwd", sgu_bwd, [Rows(p_sgu), Rows(d_s)], sgu_consts, [("rows", 2 * D, F32)],
        accs=[((1, D), F32), ((1, D), F32), ((SGU_G, SGU_C, SGU_C), F32), ((SGU_C, SGU_G), F32)], tm=SGU_C)

    d_zb = _matmul("proj_b_dx", d_yb, W["w_proj_b"], "nt")
    G["w_proj_b"] = _matmul("proj_b_dw", z_b, d_yb, "tn", out_dtype=GRAD_PAYLOAD)

    def post_bwd(o, r, k2, v, g, dz, *c):
        _, vjp = jax.vjp(_post_fn, o, r, k2, v, g, *c)
        return vjp(dz)
    do_h, dr1, dk1, dv1, d_g, g_lnw, g_lnb, g_rk = _rows_call(
        "rwkv_post_bwd", post_bwd, post_ins + [Rows(d_zb)], post_consts, [("heads", F32)] * 4 + [("rows", D, F32)],
        accs=[((NH, 1, HN), F32)] * 3, tm=128)
    G["ln_x_w"], G["ln_x_b"], G["r_k"] = (t.reshape(1, D) for t in (g_lnw, g_lnb, g_rk))
    (dr2, dlw, dk2, dv2, dkk, daa), early = _scan_bwd(r_h, lw_h, k_h, v_h, kk_h, a_h, s0s, do_h,
                                                      ex=early_grads(G) if early_grads else None)

    def pre_bwd(qr, qk, qv, qxw, qxa, qxg, dr1, dr2, dlw, dk1, dk2, dv1, dv2, dkk, daa, dg, *c):
        _, vjp = jax.vjp(_pre_fn, qr, qk, qv, qxw, qxa, qxg, *c)
        g = vjp((dr1 + dr2, dlw, dk1 + dk2, dv1 + dv2, dkk, daa, dg))
        dq = jnp.concatenate(g[:6], axis=-1)
        return (dq,) + tuple(g[6:])
    pre_b_ins = q_ins + [Heads(dr1), Heads(dr2), Heads(dlw), Heads(dk1), Heads(dk2), Heads(dv1), Heads(dv2),
                         Heads(dkk), Heads(daa), Rows(d_g)]
    d_q, G["w_lora"], G["w0"], G["a_lora"], G["a0"], G["g_lora"], G["k_k"], G["k_a"] = _rows_call(
        "rwkv_pre_bwd", pre_bwd, pre_b_ins, pre_consts, [("rows", RW_INT, F32)],
        accs=[((128, D), F32), ((1, D), F32), ((128, D), F32), ((1, D), F32), ((256, D), F32), ((1, D), F32),
              ((1, D), F32)], tm=128)
    d_p_rw, dsb0, dsb1 = _mix_bwd(d_q, p_rw, W["sb"])
    G["sb"] = jnp.concatenate([dsb0, dsb1], axis=0)

    G["w_sgu_t"] = _matmul("proj_sgu_dw", d_p_sgu, a, "tn", out_dtype=GRAD_PAYLOAD)
    G["w_rw_t"] = _matmul("proj_rwkv_dw", d_p_rw, a, "tn", out_dtype=GRAD_PAYLOAD)
    G["w_gate_t"] = _matmul("proj_gate_dw", d_gate, a, "tn", out_dtype=GRAD_PAYLOAD)
    da1 = _matmul("proj_sgu_dx", d_p_sgu, W["w_sgu_t"], "nn")
    da2 = _matmul("proj_rwkv_dx", d_p_rw, W["w_rw_t"], "nn")
    da3 = _matmul("proj_gate_dx", d_gate, W["w_gate_t"], "nn")

    def norm1_bwd(x, da1, da2, da3, dh1, g):
        _, vjp = jax.vjp(_rms, x, g)
        dx, dg = vjp(da1 + da2 + da3)
        return dh1 + dx, dg
    dx, G["g_mix"] = _rows_call("norm_mix_bwd", norm1_bwd, [Rows(x), Rows(da1), Rows(da2), Rows(da3), Rows(dh1)],
                                [W["g_mix"]], [("rows", D, F32)], accs=[((1, D), F32)])
    return loss_acc[0, 0], dx, G, early


class Exchange:
    def __init__(self, bufs, gathers):
        self.bufs, self.gathers, self.nb = list(bufs), list(gathers), len(bufs)
        self.any_specs = [pl.BlockSpec(memory_space=pl.ANY)] * self.nb
        self.out_shape = [jax.ShapeDtypeStruct((N_DEV,) + (b.shape if g else b.shape[1:]), b.dtype)
                          for b, g in zip(self.bufs, self.gathers)]
        n = (N_DEV - 1) * self.nb
        self.sem_shapes = [pltpu.SemaphoreType.DMA((n,)), pltpu.SemaphoreType.DMA((n,)),
                           pltpu.SemaphoreType.DMA((self.nb,))]

    def _copies(self, in_refs, out_refs, sems):
        send_sems, recv_sems, local_sems = sems
        x, y, c = lax.axis_index("x"), lax.axis_index("y"), lax.axis_index("c")
        me = 4 * x + 2 * y + c

        def src(b, dest):
            return in_refs[b] if self.gathers[b] else in_refs[b].at[dest]

        local = [pltpu.make_async_copy(src(b, me), out_refs[b].at[me], local_sems.at[b]) for b in range(self.nb)]
        sends, recvs = [], []
        for kbits in range(1, N_DEV):
            px = 1 - x if kbits & 4 else x
            py = 1 - y if kbits & 2 else y
            pc = 1 - c if kbits & 1 else c
            peer = 4 * px + 2 * py + pc
            for b in range(self.nb):
                s = (kbits - 1) * self.nb + b
                sends.append(pltpu.make_async_remote_copy(
                    src_ref=src(b, peer), dst_ref=out_refs[b].at[me], send_sem=send_sems.at[s],
                    recv_sem=recv_sems.at[s], device_id=(px, py, pc), device_id_type=pl.DeviceIdType.MESH))
                recvs.append(pltpu.make_async_remote_copy(
                    src_ref=src(b, peer), dst_ref=out_refs[b].at[peer], send_sem=send_sems.at[s],
                    recv_sem=recv_sems.at[s], device_id=(px, py, pc), device_id_type=pl.DeviceIdType.MESH))
        return local, sends, recvs

    def start(self, in_refs, out_refs, sems):
        local, sends, _ = self._copies(in_refs, out_refs, sems)
        for cp in local + sends:
            cp.start()

    def wait(self, in_refs, out_refs, sems):
        local, sends, recvs = self._copies(in_refs, out_refs, sems)
        for cp in recvs:
            cp.wait_recv()
        for cp in sends:
            cp.wait_send()
        for cp in local:
            cp.wait()


def _exchange(name, bufs, gather):
    ex = Exchange(bufs, gather if isinstance(gather, (list, tuple)) else [gather] * len(bufs))

    def body(*refs):
        in_refs, out_refs, sems = refs[:ex.nb], refs[ex.nb:2 * ex.nb], refs[2 * ex.nb:]
        ex.start(in_refs, out_refs, sems)
        ex.wait(in_refs, out_refs, sems)

    return pl.pallas_call(body, name=name, in_specs=ex.any_specs, out_specs=ex.any_specs, out_shape=ex.out_shape,
                          scratch_shapes=ex.sem_shapes)(*ex.bufs)


N_CHIP = 4


def _pair_exchange(name, blocks):
    def body(b_ref, keep_ref, got_ref, send_sems, recv_sems, local_sems):
        x, y, c = lax.axis_index("x"), lax.axis_index("y"), lax.axis_index("c")
        local, sends, recvs = [], [], []
        for q in range(N_CHIP):
            local.append(pltpu.make_async_copy(b_ref.at[2 * q + c], keep_ref.at[q], local_sems.at[q]))
            cp = pltpu.make_async_remote_copy(
                src_ref=b_ref.at[2 * q + 1 - c], dst_ref=got_ref.at[q], send_sem=send_sems.at[q],
                recv_sem=recv_sems.at[q], device_id=(x, y, 1 - c), device_id_type=pl.DeviceIdType.MESH)
            sends.append(cp)
            recvs.append(cp)
        for cp in local + sends:
            cp.start()
        for cp in recvs:
            cp.wait_recv()
        for cp in sends:
            cp.wait_send()
        for cp in local:
            cp.wait()

    any_spec = pl.BlockSpec(memory_space=pl.ANY)
    shape = jax.ShapeDtypeStruct((N_CHIP,) + blocks.shape[1:], blocks.dtype)
    return pl.pallas_call(
        body, name=name, in_specs=[any_spec], out_specs=[any_spec, any_spec], out_shape=[shape, shape],
        scratch_shapes=[pltpu.SemaphoreType.DMA((N_CHIP,))] * 3,
    )(blocks)


def _pair_sum(name, a, b):
    n, R, Wd = a.shape

    def body(a_ref, b_ref, o_ref):
        o_ref[...] = (a_ref[...].astype(F32) + b_ref[...].astype(F32)).astype(o_ref.dtype)

    spec = pl.BlockSpec((None, R, Wd), lambda i: (i, 0, 0))
    return pl.pallas_call(
        body, name=name, grid=(n,), in_specs=[spec, spec], out_specs=spec,
        out_shape=jax.ShapeDtypeStruct(a.shape, a.dtype),
        compiler_params=pltpu.CompilerParams(dimension_semantics=("parallel",), vmem_limit_bytes=VMEM_LIMIT),
    )(a, b)


def _chip_exchange(name, sums, ex):
    def body(*refs):
        s_ref, x_in = refs[0], refs[1:1 + ex.nb]
        slots_ref, x_out = refs[1 + ex.nb], refs[2 + ex.nb:2 + 2 * ex.nb]
        send_sems, recv_sems, local_sem = refs[2 + 2 * ex.nb:5 + 2 * ex.nb]
        ex_sems = refs[5 + 2 * ex.nb:]
        x, y, c = lax.axis_index("x"), lax.axis_index("y"), lax.axis_index("c")
        my_q = 2 * x + y
        local = pltpu.make_async_copy(s_ref.at[my_q], slots_ref.at[my_q], local_sem)
        sends, recvs = [], []
        for kbits in range(1, N_CHIP):
            px = 1 - x if kbits & 2 else x
            py = 1 - y if kbits & 1 else y
            peer_q = 2 * px + py
            sends.append(pltpu.make_async_remote_copy(
                src_ref=s_ref.at[peer_q], dst_ref=slots_ref.at[my_q], send_sem=send_sems.at[kbits - 1],
                recv_sem=recv_sems.at[kbits - 1], device_id=(px, py, c), device_id_type=pl.DeviceIdType.MESH))
            recvs.append(pltpu.make_async_remote_copy(
                src_ref=s_ref.at[peer_q], dst_ref=slots_ref.at[peer_q], send_sem=send_sems.at[kbits - 1],
                recv_sem=recv_sems.at[kbits - 1], device_id=(px, py, c), device_id_type=pl.DeviceIdType.MESH))
        for cp in [local] + sends:
            cp.start()
        ex.start(x_in, x_out, ex_sems)
        for cp in recvs:
            cp.wait_recv()
        for cp in sends:
            cp.wait_send()
        local.wait()
        ex.wait(x_in, x_out, ex_sems)

    any_spec = pl.BlockSpec(memory_space=pl.ANY)
    res = pl.pallas_call(
        body, name=name, in_specs=[any_spec] + ex.any_specs, out_specs=[any_spec] + ex.any_specs,
        out_shape=[jax.ShapeDtypeStruct(sums.shape, sums.dtype)] + ex.out_shape,
        scratch_shapes=[pltpu.SemaphoreType.DMA((N_CHIP - 1,)), pltpu.SemaphoreType.DMA((N_CHIP - 1,)),
                        pltpu.SemaphoreType.DMA(())] + ex.sem_shapes,
    )(sums, *ex.bufs)
    return res[0], list(res[1:])


def _adamw(name, slots, w, m, v, tr=256):
    R, Wd = w.shape[-2:]
    depth_axis = w.ndim == 3
    if R % tr == 0:
        tc = Wd
    else:
        tr, tc = R, (256 if (Wd % 256 == 0 and R > 256) else Wd)

    def body(s_ref, w_ref, m_ref, v_ref, g_out, d_out, m_out, v_out):
        g = s_ref[0].astype(F32)
        for j in range(1, slots.shape[0]):
            g = g + s_ref[j].astype(F32)
        m_new = ADAM_B1 * m_ref[...] + (1.0 - ADAM_B1) * g
        v_new = ADAM_B2 * v_ref[...] + (1.0 - ADAM_B2) * jnp.square(g)
        m_hat = m_new / (1.0 - ADAM_B1 ** ADAM_STEP)
        v_hat = v_new / (1.0 - ADAM_B2 ** ADAM_STEP)
        g_out[...] = g
        d_out[...] = -ADAM_LR * (m_hat / (jnp.sqrt(v_hat) + ADAM_EPS) + ADAM_WD * w_ref[...])
        m_out[...] = m_new
        v_out[...] = v_new

    if depth_axis:
        row = pl.BlockSpec((None, tr, tc), lambda i, j: (0, i, j))
    else:
        row = pl.BlockSpec((tr, tc), lambda i, j: (i, j))
    return pl.pallas_call(
        body, name=name, grid=(R // tr, Wd // tc),
        in_specs=[pl.BlockSpec((slots.shape[0], tr, tc), lambda i, j: (0, i, j)), row, row, row],
        out_specs=[row] * 4, out_shape=[jax.ShapeDtypeStruct(w.shape, F32)] * 4,
        compiler_params=pltpu.CompilerParams(dimension_semantics=("parallel", "parallel"),
                                             vmem_limit_bytes=VMEM_LIMIT),
    )(slots, w, m, v)


PACK_W = 1024
_SMALL_SIZES = [int(np.prod(s)) for _, s in REPLICATED]
_SMALL_ROWS = _round_up(_round_up(sum(_SMALL_SIZES) + PACK_W, PACK_W) // PACK_W, 8)
_LOSS_AT = sum(_SMALL_SIZES)
W_IN_SHARD = P_TOTAL // N_DEV


def _pack_rows(parts, rows, dtype):
    flat = jnp.concatenate([p.reshape(-1).astype(dtype) for p in parts])
    return jnp.pad(flat, (0, rows * PACK_W - flat.shape[0])).reshape(rows, PACK_W)


def _w_in_groups_t(blocks):
    def rows(g0, g1):
        parts, g = [], g0
        while g < g1:
            j, r = divmod(g, W_IN_SHARD)
            n = min(W_IN_SHARD - r, g1 - g)
            parts.append(blocks[j, r:r + n])
            g += n
        return parts

    o, c = 2 * D, 2 * D + 3 * D
    z = lambda r: [jnp.zeros((r, D), blocks.dtype)]
    rw = (rows(o, c) + rows(c, c + L_W) + z(128 - L_W) + rows(c + L_W, c + L_W + L_A) + z(128 - L_A)
          + rows(c + L_W + L_A, o + C_B) + z(256 - L_G))
    cat = lambda parts: jnp.concatenate(parts, axis=0)
    return cat(rows(0, o)), cat(rw), cat(rows(o + C_B, P_TOTAL))


def _w_in_grad_blocks(g_sgu_t, g_rw_t, g_gate_t):
    c = 3 * D
    segments = [(g_sgu_t, 0, 2 * D), (g_rw_t, 0, c), (g_rw_t, c, c + L_W), (g_rw_t, c + 128, c + 128 + L_A),
                (g_rw_t, c + 256, c + 256 + L_G), (g_gate_t, 0, 2 * D)]
    blocks = []
    for j in range(N_DEV):
        lo, hi, parts, at = j * W_IN_SHARD, (j + 1) * W_IN_SHARD, [], 0
        for arr, r0, r1 in segments:
            a, b = max(lo, at), min(hi, at + r1 - r0)
            if a < b:
                parts.append(arr[r0 + a - at:r0 + b - at])
            at += r1 - r0
        blocks.append(jnp.concatenate(parts, axis=0)[None])
    return jnp.concatenate(blocks, axis=0)


def _all_gather_two_level(name, bufs):
    nb = len(bufs)

    def body(*refs):
        in_refs, out_refs = refs[:nb], refs[nb:2 * nb]
        send_sems, recv_sems, local_sems = refs[2 * nb:]
        x, y, c = lax.axis_index("x"), lax.axis_index("y"), lax.axis_index("c")
        me, sibling = (x, y, c), (x, y, 1 - c)
        chips = [(1 - x, y), (x, 1 - y), (1 - x, 1 - y)]

        def slot(b, dev):
            return out_refs[b].at[4 * dev[0] + 2 * dev[1] + dev[2]]

        def copy(b, k, block, to, own=False):
            return pltpu.make_async_remote_copy(
                src_ref=in_refs[b] if own else slot(b, block), dst_ref=slot(b, block),
                send_sem=send_sems.at[7 * b + k], recv_sem=recv_sems.at[7 * b + k], device_id=to,
                device_id_type=pl.DeviceIdType.MESH)

        local = [pltpu.make_async_copy(in_refs[b], slot(b, me), local_sems.at[b]) for b in range(nb)]
        first = [copy(b, 0, me, sibling, own=True) for b in range(nb)]
        first += [copy(b, 1 + j, me, (*chip, c), own=True) for j, chip in enumerate(chips) for b in range(nb)]
        for cp in local + first:
            cp.start()
        passed = []
        for j, chip in enumerate(chips):
            for b in range(nb):
                copy(b, 1 + j, (*chip, c), me).wait_recv()
                passed.append(copy(b, 4 + j, (*chip, c), sibling))
                passed[-1].start()
        for b in range(nb):
            copy(b, 0, sibling, me).wait_recv()
        for j, chip in enumerate(chips):
            for b in range(nb):
                copy(b, 4 + j, (*chip, 1 - c), me).wait_recv()
        for cp in first + passed:
            cp.wait_send()
        for cp in local:
            cp.wait()

    any_spec = pl.BlockSpec(memory_space=pl.ANY)
    return pl.pallas_call(
        body, name=name, in_specs=[any_spec] * nb, out_specs=[any_spec] * nb,
        out_shape=[jax.ShapeDtypeStruct((N_DEV,) + b.shape, b.dtype) for b in bufs],
        scratch_shapes=[pltpu.SemaphoreType.DMA((7 * nb,)), pltpu.SemaphoreType.DMA((7 * nb,)),
                        pltpu.SemaphoreType.DMA((nb,))],
    )(*bufs)


def _cols_from_blocks(blk):
    return jnp.transpose(blk, (1, 0, 2)).reshape(blk.shape[1], -1)


def _cols_to_blocks(g):
    r, c = g.shape
    return jnp.transpose(g.reshape(r, N_DEV, c // N_DEV), (1, 0, 2))


FIRST_WEIGHTS = ["w_in", "shift_b", "w_lora_w", "a_lora_w", "g_lora_w"]
LATE_WEIGHTS = ["w_proj_a", "w_proj_b", "w_out", "w_ffn1", "w_ffn2"]


def _late_weights(shards):
    ex = Exchange([shards[n].astype(BF16) for n in LATE_WEIGHTS], [True] * len(LATE_WEIGHTS))

    def finish(results):
        got = dict(zip(LATE_WEIGHTS, results))
        W = {n: got[n].reshape(-1, D) for n in ("w_proj_a", "w_proj_b", "w_out", "w_ffn2")}
        W["w_ffn1"] = got["w_ffn1"].reshape(N_DEV, D, -1)
        return W
    return ex, finish


def _gather_weights(shards):
    def payload(n):
        if n == "w_in":
            return jnp.transpose(shards[n][0]).astype(BF16)
        return shards[n] if n == "shift_b" else shards[n].astype(BF16)
    got = dict(zip(FIRST_WEIGHTS, _all_gather_two_level("weight_all_gather", [payload(n) for n in FIRST_WEIGHTS])))
    W = {}
    W["w_sgu_t"], W["w_rw_t"], W["w_gate_t"] = _w_in_groups_t(got["w_in"])
    z = lambda r, c, dt: jnp.zeros((r, c), dt)
    W["w_lora"] = jnp.concatenate([_cols_from_blocks(got["w_lora_w"][:, 0]).astype(F32), z(128 - L_W, D, F32)], axis=0)
    W["a_lora"] = jnp.concatenate([_cols_from_blocks(got["a_lora_w"][:, 0]).astype(F32), z(128 - L_A, D, F32)], axis=0)
    W["g_lora"] = jnp.concatenate([_cols_from_blocks(got["g_lora_w"][:, 0]).astype(F32), z(256 - L_G, D, F32)], axis=0)
    sb = _cols_from_blocks(got["shift_b"][:, 0])
    W["sb"] = jnp.concatenate([sb[:, :3 * D], sb[:, 3 * D:3 * D + L_W], z(2, 128 - L_W, F32),
                               sb[:, 3 * D + L_W:3 * D + L_W + L_A], z(2, 128 - L_A, F32),
                               sb[:, 3 * D + L_W + L_A:], z(2, 256 - L_G, F32)], axis=1)
    return W


def _replicated_weights(rep):
    W = {n: rep[n] for n in ("g_mix", "sgu_ln_w", "sgu_ln_b", "w0", "a0", "k_k", "k_a", "r_k", "ln_x_w", "ln_x_b",
                             "g_ffn")}
    W["g_final"] = rep["g_final"].reshape(1, D)
    W["sgu_w"] = rep["sgu_w"][0]
    W["sgu_bt"] = jnp.transpose(rep["sgu_b"][0])
    return W


def _late_grad_blocks(G):
    blocks = {n: G[n].reshape(N_DEV, -1, D) for n in ("w_proj_a", "w_proj_b", "w_out", "w_ffn2")}
    blocks["w_ffn1"] = G["w_ffn1"]
    return Exchange([blocks[n] for n in LATE_WEIGHTS], [False] * len(LATE_WEIGHTS))


def _first_grad_blocks(G):
    sbg = G["sb"]
    c = 3 * D
    sb = jnp.concatenate([sbg[:, :c], sbg[:, c:c + L_W], sbg[:, c + 128:c + 128 + L_A],
                          sbg[:, c + 256:c + 256 + L_G]], axis=1)
    return {
        "w_in": _w_in_grad_blocks(G["w_sgu_t"], G["w_rw_t"], G["w_gate_t"]),
        "shift_b": _cols_to_blocks(sb),
        "w_lora_w": _cols_to_blocks(G["w_lora"][:L_W]), "a_lora_w": _cols_to_blocks(G["a_lora"][:L_A]),
        "g_lora_w": _cols_to_blocks(G["g_lora"][:L_G]),
    }


def _replicated_grads(G):
    small = {n: G[n] for n in ("g_mix", "sgu_ln_w", "sgu_ln_b", "w0", "a0", "k_k", "k_a", "r_k", "ln_x_w", "ln_x_b",
                               "g_ffn", "g_final")}
    small["sgu_w"] = G["sgu_w"]
    small["sgu_b"] = jnp.transpose(G["sgu_bt"])
    return small


def kernel(x, g_mix, w_in, sgu_ln_w, sgu_ln_b, sgu_w, sgu_b, w_proj_a, shift_b, w_lora_w, w0, a_lora_w, a0, g_lora_w, k_k, k_a, r_k, ln_x_w, ln_x_b, w_proj_b, w_out, g_ffn, w_ffn1, w_ffn2, g_final, loss_target, m_g_mix, m_w_in, m_sgu_ln_w, m_sgu_ln_b, m_sgu_w, m_sgu_b, m_w_proj_a, m_shift_b, m_w_lora_w, m_w0, m_a_lora_w, m_a0, m_g_lora_w, m_k_k, m_k_a, m_r_k, m_ln_x_w, m_ln_x_b, m_w_proj_b, m_w_out, m_g_ffn, m_w_ffn1, m_w_ffn2, m_g_final, v_g_mix, v_w_in, v_sgu_ln_w, v_sgu_ln_b, v_sgu_w, v_sgu_b, v_w_proj_a, v_shift_b, v_w_lora_w, v_w0, v_a_lora_w, v_a0, v_g_lora_w, v_k_k, v_k_a, v_r_k, v_ln_x_w, v_ln_x_b, v_w_proj_b, v_w_out, v_g_ffn, v_w_ffn1, v_w_ffn2, v_g_final):
    env = dict(locals())
    weights = {n: env[n] for n in WEIGHT_ORDER}
    moms = {n: env["m_" + n] for n in WEIGHT_ORDER}
    vars_ = {n: env["v_" + n] for n in WEIGHT_ORDER}

    shards = {n: weights[n] for n, _, _ in SHARDED}
    W = _gather_weights(shards)
    W.update(_replicated_weights({n: weights[n] for n, _ in REPLICATED}))
    loss_part, dx, G, late_slots = _local_step(x[0], loss_target[0], W, late_weights=_late_weights(shards),
                                               early_grads=_late_grad_blocks)

    slots = dict(zip(LATE_WEIGHTS, late_slots))
    blocks = _first_grad_blocks(G)
    small = _replicated_grads(G)
    small_parts = [small[n] for n, _ in REPLICATED] + [jnp.full((PACK_W,), loss_part, F32)]
    keep, got = _pair_exchange("grad_pair_exchange", blocks["w_in"])
    rest = [n for n in FIRST_WEIGHTS if n != "w_in"]
    ex = Exchange([blocks[n] for n in rest] + [_pack_rows(small_parts, _SMALL_ROWS, F32)], [False] * len(rest) + [True])
    slots["w_in"], res = _chip_exchange("grad_exchange", _pair_sum("grad_pair_sum", keep, got), ex)
    slots.update(zip(rest, res[:-1]))
    small_slots = res[-1]

    outs = [dict(), dict(), dict(), dict()]
    for n, _, _ in SHARDED:
        if n == "w_in":
            res = _adamw("adamw_" + n, slots[n], *[jnp.transpose(t[0]) for t in (weights[n], moms[n], vars_[n])])
            res = [jnp.transpose(t)[None] for t in res]
        else:
            res = _adamw("adamw_" + n, slots[n], weights[n], moms[n], vars_[n])
        for k in range(4):
            outs[k][n] = res[k]

    def packed(d):
        return _pack_rows([d[n] for n, _ in REPLICATED], _SMALL_ROWS, F32)
    small_out = _adamw("adamw_replicated", small_slots, packed(weights), packed(moms), packed(vars_))
    for k in range(4):
        flat = small_out[k].reshape(-1)
        off = 0
        for (n, s), size in zip(REPLICATED, _SMALL_SIZES):
            outs[k][n] = flat[off:off + size].reshape(s)
            off += size
    loss = small_out[0].reshape(-1)[_LOSS_AT]
    return (loss, dx[None], *[outs[0][n] for n in WEIGHT_ORDER], *[outs[1][n] for n in WEIGHT_ORDER],
            *[outs[2][n] for n in WEIGHT_ORDER], *[outs[3][n] for n in WEIGHT_ORDER])
```

```python
import functools
import numpy as np
import jax
import jax.numpy as jnp
from jax import lax
from jax.experimental import pallas as pl
from jax.experimental.pallas import tpu as pltpu

F32 = jnp.float32
BF16 = jnp.bfloat16

D = 1024
NH, HN = 16, 64
SGU_G, SGU_C = 8, 128
L_W, L_A, L_G = 64, 64, 160
C_B = 3 * D + L_W + L_A + L_G
P_TOTAL = 2 * D + C_B + 2 * D
D_FF = 4 * D
RW_INT = 3 * D + 128 + 128 + 256
NORM_EPS, LN_EPS, GN_EPS = 1e-6, 1e-5, 64e-5
N_DEV = 8
LANES = 128
SCAN_C = 64
SOLVE_B = 16
SCAN_PRECISION = lax.Precision.HIGH
GRAD_PAYLOAD = BF16
VMEM_LIMIT = 56 * 1024 * 1024

ADAM_LR, ADAM_B1, ADAM_B2, ADAM_EPS, ADAM_WD, ADAM_STEP = 0.001, 0.9, 0.999, 1e-08, 0.01, 10

SHARDED = [
    ("w_in", (D, P_TOTAL), 1), ("w_proj_a", (D, D), 0), ("shift_b", (2, C_B), 1), ("w_lora_w", (L_W, D), 1),
    ("a_lora_w", (L_A, D), 1), ("g_lora_w", (L_G, D), 1), ("w_proj_b", (D, D), 0), ("w_out", (D, D), 0),
    ("w_ffn1", (D, D_FF), 1), ("w_ffn2", (D_FF, D), 0),
]
REPLICATED = [
    ("g_mix", (1, D)), ("sgu_ln_w", (1, D)), ("sgu_ln_b", (1, D)), ("sgu_w", (1, SGU_G, SGU_C, SGU_C)),
    ("sgu_b", (1, SGU_G, SGU_C)), ("w0", (1, D)), ("a0", (1, D)), ("k_k", (1, D)), ("k_a", (1, D)), ("r_k", (1, D)),
    ("ln_x_w", (1, D)), ("ln_x_b", (1, D)), ("g_ffn", (1, D)), ("g_final", (D,)),
]
WEIGHT_ORDER = ["g_mix", "w_in", "sgu_ln_w", "sgu_ln_b", "sgu_w", "sgu_b", "w_proj_a", "shift_b", "w_lora_w", "w0",
                "a_lora_w", "a0", "g_lora_w", "k_k", "k_a", "r_k", "ln_x_w", "ln_x_b", "w_proj_b", "w_out", "g_ffn",
                "w_ffn1", "w_ffn2", "g_final"]


def _shard_shape(shape, axis):
    s = list(shape)
    s[axis] //= N_DEV
    return tuple(s)


def _round_up(n, m):
    return (n + m - 1) // m * m


def _pick(n, target):
    if n <= target:
        return n
    best = None
    for t in range(LANES, target + 1, LANES):
        if n % t == 0:
            best = t
    assert best is not None, (n, target)
    return best


def _matmul(name, a, b, mode, out_dtype=F32, tm=1024, tn=1024, tk=1024, out_blocks=None):
    b_blocks = b.shape[0] if b.ndim == 3 else None
    bshape = b.shape if b.ndim == 2 else (b.shape[1], b.shape[0] * b.shape[2])
    if mode == "nn":
        (M, K), (K2, N) = a.shape, bshape
    elif mode == "nt":
        (M, K), (N, K2) = a.shape, bshape
    else:
        (K, M), (K2, N) = a.shape, bshape
    assert K == K2, (name, a.shape, b.shape)
    assert b_blocks is None or mode != "tn"
    assert out_blocks is None or mode == "tn"
    tn = min(tn, N // (out_blocks or 1), bshape[1] // b_blocks if (b_blocks and mode == "nn") else tn)
    tk = min(tk, bshape[1] // b_blocks if (b_blocks and mode == "nt") else tk)
    tm, tn, tk = _pick(M, tm), _pick(N, tn), _pick(K, tk)
    nk = K // tk
    dims = {"nn": (((1,), (0,)), ((), ())), "nt": (((1,), (1,)), ((), ())), "tn": (((0,), (0,)), ((), ()))}[mode]

    def body(a_ref, b_ref, o_ref, acc_ref):
        k = pl.program_id(2)

        @pl.when(k == 0)
        def _():
            acc_ref[...] = jnp.zeros_like(acc_ref)

        acc_ref[...] += lax.dot_general(a_ref[...].astype(BF16), b_ref[...].astype(BF16), dims,
                                        preferred_element_type=F32)

        @pl.when(k == nk - 1)
        def _():
            o_ref[...] = acc_ref[...].astype(o_ref.dtype)

    a_spec = {"nn": pl.BlockSpec((tm, tk), lambda i, j, k: (i, k)), "nt": pl.BlockSpec((tm, tk), lambda i, j, k: (i, k)),
              "tn": pl.BlockSpec((tk, tm), lambda i, j, k: (k, i))}[mode]
    b_spec = {"nn": pl.BlockSpec((tk, tn), lambda i, j, k: (k, j)), "nt": pl.BlockSpec((tn, tk), lambda i, j, k: (j, k)),
              "tn": pl.BlockSpec((tk, tn), lambda i, j, k: (k, j))}[mode]
    if b_blocks and mode == "nn":
        per = b.shape[2] // tn
        b_spec = pl.BlockSpec((None, tk, tn), lambda i, j, k: (j // per, k, j % per))
    elif b_blocks:
        per = b.shape[2] // tk
        b_spec = pl.BlockSpec((None, tn, tk), lambda i, j, k: (k // per, j, k % per))
    out_spec = pl.BlockSpec((tm, tn), lambda i, j, k: (i, j))
    out_shape = jax.ShapeDtypeStruct((M, N), out_dtype)
    if out_blocks:
        per_o = N // out_blocks // tn
        out_spec = pl.BlockSpec((None, tm, tn), lambda i, j, k: (j // per_o, i, j % per_o))
        out_shape = jax.ShapeDtypeStruct((out_blocks, M, N // out_blocks), out_dtype)
    return pl.pallas_call(
        body, name=name, grid=(M // tm, N // tn, nk), in_specs=[a_spec, b_spec],
        out_specs=out_spec, out_shape=out_shape, scratch_shapes=[pltpu.VMEM((tm, tn), F32)],
        compiler_params=pltpu.CompilerParams(dimension_semantics=("parallel", "parallel", "arbitrary"),
                                             vmem_limit_bytes=VMEM_LIMIT),
    )(a, b)


class Rows:
    def __init__(self, arr, width=None, cb=0):
        self.arr, self.width, self.cb = arr, (arr.shape[1] if width is None else width), cb


class Heads:
    def __init__(self, arr):
        self.arr = arr


class Halo:
    def __init__(self, arr, side):
        self.arr, self.side = arr, side


def _rows_call(name, fn, ins, consts, outs, accs=(), tm=256, with_pid=False):
    T = next(o.arr.shape[1] if isinstance(o, Heads) else o.arr.shape[0] for o in ins if not isinstance(o, Halo))
    tm = min(tm, T)
    n_tiles = T // tm
    n_in, n_c, n_out = len(ins), len(consts), len(outs)
    in_specs, args = [], []
    for o in ins:
        if isinstance(o, Rows):
            in_specs.append(pl.BlockSpec((tm, o.width), lambda i, cb=o.cb: (i, cb)))
        elif isinstance(o, Heads):
            in_specs.append(pl.BlockSpec((NH, tm, HN), lambda i: (0, i, 0)))
        else:
            w = o.arr.shape[1]
            if o.side < 0:
                in_specs.append(pl.BlockSpec((8, w), lambda i: (jnp.maximum(i * (tm // 8) - 1, 0), 0)))
            else:
                in_specs.append(pl.BlockSpec((8, w), lambda i: (jnp.minimum((i + 1) * (tm // 8), T // 8 - 1), 0)))
        args.append(o.arr)
    for c in consts:
        in_specs.append(pl.BlockSpec(c.shape, lambda i, nd=c.ndim: (0,) * nd))
        args.append(c)
    out_specs, out_shape = [], []
    for o in outs:
        if o[0] == "rows":
            out_specs.append(pl.BlockSpec((tm, o[1]), lambda i: (i, 0)))
            out_shape.append(jax.ShapeDtypeStruct((T, o[1]), o[2]))
        else:
            out_specs.append(pl.BlockSpec((NH, tm, HN), lambda i: (0, i, 0)))
            out_shape.append(jax.ShapeDtypeStruct((NH, T, HN), o[1]))
    for shape, dt in accs:
        out_specs.append(pl.BlockSpec(shape, lambda i, nd=len(shape): (0,) * nd))
        out_shape.append(jax.ShapeDtypeStruct(shape, dt))

    def body(*refs):
        i = pl.program_id(0)
        vals = []
        vals = [r[...] for r in refs[:n_in + n_c]]
        res = fn(i, n_tiles, *vals) if with_pid else fn(*vals)
        out_refs = refs[n_in + n_c:]
        for r, v in zip(out_refs[:n_out], res[:n_out]):
            r[...] = v.astype(r.dtype)
        if accs:
            @pl.when(i == 0)
            def _():
                for r in out_refs[n_out:]:
                    r[...] = jnp.zeros_like(r)

            for r, v in zip(out_refs[n_out:], res[n_out:]):
                r[...] += v.astype(r.dtype)

    res = pl.pallas_call(
        body, name=name, grid=(n_tiles,), in_specs=in_specs, out_specs=out_specs, out_shape=out_shape,
        compiler_params=pltpu.CompilerParams(dimension_semantics=("arbitrary",), vmem_limit_bytes=VMEM_LIMIT),
    )(*args)
    return res


def _rms(x, g):
    return x * lax.rsqrt(jnp.mean(x * x, axis=-1, keepdims=True) + NORM_EPS) * g


def _gelu(x):
    return 0.5 * x * (1.0 + lax.erf(x * 0.7071067811865476))


def _sigmoid(x):
    return 1.0 / (1.0 + jnp.exp(-x))


def _bdot(a, b):
    return jnp.dot(a.astype(BF16), b.astype(BF16), preferred_element_type=F32)


def _to_heads(x):
    return jnp.concatenate([x[:, h * HN:(h + 1) * HN][None] for h in range(NH)], axis=0)


def _from_heads(xh):
    return jnp.concatenate([xh[h] for h in range(NH)], axis=-1)


def _sgu_fn(p, ln_w, ln_b, sw, sbt):
    z = _gelu(p)
    u, v = z[:, :D], z[:, D:]
    mu = jnp.mean(v, axis=-1, keepdims=True)
    var = jnp.mean(jnp.square(v - mu), axis=-1, keepdims=True)
    vn = (v - mu) * lax.rsqrt(var + LN_EPS) * ln_w + ln_b
    ri = lax.broadcasted_iota(jnp.int32, (SGU_C, SGU_C), 0)
    ci = lax.broadcasted_iota(jnp.int32, (SGU_C, SGU_C), 1)
    mask = (ci <= ri).astype(F32)
    dg = D // SGU_G
    parts = []
    for g in range(SGU_G):
        parts.append(_bdot(sw[g] * mask, vn[:, g * dg:(g + 1) * dg]) + sbt[:, g:g + 1])
    return u * jnp.concatenate(parts, axis=-1)


def _pre_fn(qr, qk, qv, qxw, qxa, qxg, wl, w0, al, a0, gl, k_k, k_a):
    w = -jax.nn.softplus(-(w0 + _bdot(jnp.tanh(qxw), wl))) - 0.5
    lw = -jnp.exp(w)
    aa = _sigmoid(a0 + _bdot(qxa, al))
    g = _bdot(_sigmoid(qxg), gl)
    kk = _to_heads(qk * k_k)
    kk = kk / jnp.maximum(jnp.sqrt(jnp.sum(kk * kk, axis=-1, keepdims=True)), 1e-12)
    k2 = qk * (1.0 + (aa - 1.0) * k_a)
    return _to_heads(qr), _to_heads(lw), _to_heads(k2), _to_heads(qv), kk, _to_heads(aa), g


def _post_fn(o, r, k2, v, g, ln_w, ln_b, r_k):
    mu = jnp.mean(o, axis=-1, keepdims=True)
    d = o - mu
    var = jnp.mean(d * d, axis=-1, keepdims=True)
    on = d * lax.rsqrt(var + GN_EPS) * ln_w + ln_b
    bonus = jnp.sum(r * k2 * r_k, axis=-1, keepdims=True) * v
    return _from_heads(on + bonus) * g


def _gate_fn(pg, ya, yb):
    return _sigmoid(pg[:, :D]) * ya + _sigmoid(pg[:, D:]) * yb


def _bmm(x, y, cx, cy):
    return lax.dot_general(x, y, (((cx,), (cy,)), ((0,), (0,))), precision=SCAN_PRECISION,
                           preferred_element_type=F32)


def _unit_lower_inverse(M):
    C = M.shape[1]
    ti = lax.broadcasted_iota(jnp.int32, (C, C), 0)
    tj = lax.broadcasted_iota(jnp.int32, (C, C), 1)
    eye = (ti == tj).astype(F32)
    same = lambda b: (ti // b == tj // b).astype(F32)
    X = -(M * same(SOLVE_B))
    inv = eye + X
    span = 1
    while 2 * span < SOLVE_B:
        X = _bmm(X, X, 2, 1)
        inv = inv + _bmm(inv, X, 2, 1)
        span *= 2
    b = SOLVE_B
    while b < C:
        low = M * (same(2 * b) - same(b))
        inv = inv - _bmm(_bmm(inv, low, 2, 1), inv, 2, 1)
        b *= 2
    return inv


@jax.custom_vjp
def _unit_lower_solve(M, y):
    return _bmm(_unit_lower_inverse(M), y, 2, 1)


def _unit_lower_solve_fwd(M, y):
    inv = _unit_lower_inverse(M)
    u = _bmm(inv, y, 2, 1)
    return u, (inv, u)


def _unit_lower_solve_bwd(res, du):
    inv, u = res
    dy = _bmm(inv, du, 1, 1)
    return -_bmm(dy, u, 2, 2), dy


_unit_lower_solve.defvjp(_unit_lower_solve_fwd, _unit_lower_solve_bwd)


def _chunk_fn(S0, r, lw, k, v, kk, a):
    C = SCAN_C
    bmm = _bmm
    ti = lax.broadcasted_iota(jnp.int32, (C, C), 0)
    tj = lax.broadcasted_iota(jnp.int32, (C, C), 1)
    incl = (tj <= ti).astype(F32)
    strict = (tj < ti).astype(F32)
    cum = lax.dot_general(jnp.broadcast_to(incl, (NH, C, C)), lw, (((2,), (1,)), ((0,), (0,))),
                          precision=lax.Precision.HIGHEST, preferred_element_type=F32)
    g_in, g_ex, g_inv = jnp.exp(cum), jnp.exp(cum - lw), jnp.exp(-cum)
    kkt, rt = kk * g_ex, r * g_in
    bk = jnp.concatenate([kk * a * g_inv, k * g_inv], axis=1)
    A = bmm(kkt, bk, 2, 2)
    M = A[:, :, :C] * strict
    n_mask = jnp.concatenate([jnp.zeros((C, C), F32), strict], axis=1)
    zv = jnp.concatenate([jnp.zeros_like(v), v], axis=1)
    s0_side = bmm(jnp.concatenate([kkt, rt], axis=1), S0, 2, 2)
    y = _unit_lower_solve(M, s0_side[:, :C] + bmm(A * n_mask, zv, 2, 1))
    z = jnp.concatenate([-y, v], axis=1)
    O = s0_side[:, C:] + bmm(bmm(rt, bk, 2, 2) * jnp.concatenate([incl, incl], axis=1), z, 2, 1)
    S1 = (S0 + bmm(z, bk, 1, 1)) * g_in[:, C - 1:C, :]
    return O, S1


def _scan_fwd(r, lw, k, v, kk, a, ex=None, tb=256):
    T = r.shape[1]
    tb = min(tb, T)
    n_chunks = tb // SCAN_C
    nb = T // tb
    nx = ex.nb if ex else 0

    def body(*refs):
        r_ref, lw_ref, k_ref, v_ref, kk_ref, a_ref = refs[:6]
        x_in, (o_ref, s0_ref), x_out = refs[6:6 + nx], refs[6 + nx:8 + nx], refs[8 + nx:8 + 2 * nx]
        s_ref, sems = refs[8 + 2 * nx], refs[9 + 2 * nx:]

        @pl.when(pl.program_id(0) == 0)
        def _():
            s_ref[...] = jnp.zeros_like(s_ref)
            if ex:
                ex.start(x_in, x_out, sems)

        def step(c, carry):
            sl = pl.ds(pl.multiple_of(c * SCAN_C, SCAN_C), SCAN_C)
            S0 = s_ref[...]
            s0_ref[c] = S0
            O, S1 = _chunk_fn(S0, r_ref[:, sl, :], lw_ref[:, sl, :], k_ref[:, sl, :], v_ref[:, sl, :],
                              kk_ref[:, sl, :], a_ref[:, sl, :])
            o_ref[:, sl, :] = O
            s_ref[...] = S1
            return carry

        lax.fori_loop(0, n_chunks, step, 0)

        if ex:
            @pl.when(pl.program_id(0) == nb - 1)
            def _():
                ex.wait(x_in, x_out, sems)

    hm = pl.BlockSpec((NH, tb, HN), lambda i: (0, i, 0))
    res = pl.pallas_call(
        body, name="rwkv_scan_fwd", grid=(nb,), in_specs=[hm] * 6 + (ex.any_specs if ex else []),
        out_specs=[hm, pl.BlockSpec((n_chunks, NH, HN, HN), lambda i: (i, 0, 0, 0))] + (ex.any_specs if ex else []),
        out_shape=[jax.ShapeDtypeStruct((NH, T, HN), F32), jax.ShapeDtypeStruct((T // SCAN_C, NH, HN, HN), F32)]
        + (ex.out_shape if ex else []),
        scratch_shapes=[pltpu.VMEM((NH, HN, HN), F32)] + (ex.sem_shapes if ex else []),
        compiler_params=pltpu.CompilerParams(dimension_semantics=("arbitrary",), vmem_limit_bytes=VMEM_LIMIT),
    )(r, lw, k, v, kk, a, *(ex.bufs if ex else []))
    return res[0], res[1], list(res[2:])


def _scan_bwd(r, lw, k, v, kk, a, s0s, do, ex=None, tb=128):
    T = r.shape[1]
    tb = min(tb, T)
    n_chunks = tb // SCAN_C
    nb = T // tb
    nx = ex.nb if ex else 0

    def body(*refs):
        r_ref, lw_ref, k_ref, v_ref, kk_ref, a_ref, s0_ref, do_ref = refs[:8]
        x_in, (dr, dlw, dk, dv, dkk, da), x_out = refs[8:8 + nx], refs[8 + nx:14 + nx], refs[14 + nx:14 + 2 * nx]
        ds_ref, sems = refs[14 + 2 * nx], refs[15 + 2 * nx:]

        @pl.when(pl.program_id(0) == 0)
        def _():
            ds_ref[...] = jnp.zeros_like(ds_ref)
            if ex:
                ex.start(x_in, x_out, sems)

        def step(j, carry):
            c = n_chunks - 1 - j
            sl = pl.ds(pl.multiple_of(c * SCAN_C, SCAN_C), SCAN_C)
            _, vjp = jax.vjp(_chunk_fn, s0_ref[c], r_ref[:, sl, :], lw_ref[:, sl, :], k_ref[:, sl, :],
                             v_ref[:, sl, :], kk_ref[:, sl, :], a_ref[:, sl, :])
            g = vjp((do_ref[:, sl, :], ds_ref[...]))
            ds_ref[...] = g[0]
            for ref, val in zip((dr, dlw, dk, dv, dkk, da), g[1:]):
                ref[:, sl, :] = val
            return carry

        lax.fori_loop(0, n_chunks, step, 0)

        if ex:
            @pl.when(pl.program_id(0) == nb - 1)
            def _():
                ex.wait(x_in, x_out, sems)

    hm = pl.BlockSpec((NH, tb, HN), lambda i: (0, nb - 1 - i, 0))
    res = pl.pallas_call(
        body, name="rwkv_scan_bwd", grid=(nb,),
        in_specs=[hm] * 6 + [pl.BlockSpec((n_chunks, NH, HN, HN), lambda i: (nb - 1 - i, 0, 0, 0)), hm]
        + (ex.any_specs if ex else []),
        out_specs=[hm] * 6 + (ex.any_specs if ex else []),
        out_shape=[jax.ShapeDtypeStruct((NH, T, HN), F32)] * 6 + (ex.out_shape if ex else []),
        scratch_shapes=[pltpu.VMEM((NH, HN, HN), F32)] + (ex.sem_shapes if ex else []),
        compiler_params=pltpu.CompilerParams(dimension_semantics=("arbitrary",), vmem_limit_bytes=VMEM_LIMIT),
    )(r, lw, k, v, kk, a, s0s, do, *(ex.bufs if ex else []))
    return list(res[:6]), list(res[6:])


def _shift_down(i, p, prev8):
    first = jnp.where(i > 0, prev8[7:8, :], 0.0)
    row = lax.broadcasted_iota(jnp.int32, p.shape, 0)
    return jnp.where(row == 0, first, pltpu.roll(p, 1, axis=0))


def _mix_fwd(p, sb, tm=256):
    def fn(i, n, p, prev8, sb):
        return (p * sb[0:1] + _shift_down(i, p, prev8) * sb[1:2],)
    return _rows_call("shift_mix_fwd", fn, [Rows(p), Halo(p, -1)], [sb], [("rows", p.shape[1], F32)], tm=tm,
                      with_pid=True)[0]


def _mix_bwd(dq, p, sb, tm=256):
    def fn(i, n, dq, next8, p, prev8, sb):
        ps = _shift_down(i, p, prev8)
        d1 = dq * sb[1:2]
        last = jnp.where(i < n - 1, next8[0:1, :] * sb[1:2], 0.0)
        row = lax.broadcasted_iota(jnp.int32, dq.shape, 0)
        up = jnp.where(row == dq.shape[0] - 1, last, pltpu.roll(d1, dq.shape[0] - 1, axis=0))
        return (dq * sb[0:1] + up, jnp.sum(dq * p, axis=0, keepdims=True), jnp.sum(dq * ps, axis=0, keepdims=True))
    w = p.shape[1]
    return _rows_call("shift_mix_bwd", fn, [Rows(dq), Halo(dq, +1), Rows(p), Halo(p, -1)], [sb], [("rows", w, F32)],
                      accs=[((1, w), F32), ((1, w), F32)], tm=tm, with_pid=True)


def _local_step(x, target, W, late_weights=None, early_grads=None):
    G = {}
    a = _rows_call("norm_mix_fwd", lambda x, g: (_rms(x, g),), [Rows(x)], [W["g_mix"]], [("rows", D, BF16)])[0]
    p_sgu = _matmul("proj_sgu", a, W["w_sgu_t"], "nt")
    p_rw = _matmul("proj_rwkv", a, W["w_rw_t"], "nt")
    p_gate = _matmul("proj_gate", a, W["w_gate_t"], "nt")

    sgu_consts = [W["sgu_ln_w"], W["sgu_ln_b"], W["sgu_w"], W["sgu_bt"]]
    s = _rows_call("sgu_fwd", lambda *t: (_sgu_fn(*t),), [Rows(p_sgu)], sgu_consts, [("rows", D, BF16)], tm=SGU_C)[0]

    q = _mix_fwd(p_rw, W["sb"])
    q_ins = [Rows(q, D, 0), Rows(q, D, 1), Rows(q, D, 2), Rows(q, 128, 24), Rows(q, 128, 25), Rows(q, 256, 13)]
    pre_consts = [W["w_lora"], W["w0"], W["a_lora"], W["a0"], W["g_lora"], W["k_k"], W["k_a"]]
    r_h, lw_h, k_h, v_h, kk_h, a_h, g_gate = _rows_call(
        "rwkv_pre_fwd", _pre_fn, q_ins, pre_consts, [("heads", F32)] * 6 + [("rows", D, F32)], tm=128)
    o_h, s0s, got = _scan_fwd(r_h, lw_h, k_h, v_h, kk_h, a_h, ex=late_weights[0] if late_weights else None)
    if late_weights:
        W = {**W, **late_weights[1](got)}
    y_a = _matmul("proj_a", s, W["w_proj_a"], "nn")
    post_ins = [Heads(o_h), Heads(r_h), Heads(k_h), Heads(v_h), Rows(g_gate)]
    post_consts = [W[n].reshape(NH, 1, HN) for n in ("ln_x_w", "ln_x_b", "r_k")]
    z_b = _rows_call("rwkv_post_fwd", lambda *t: (_post_fn(*t),), post_ins, post_consts, [("rows", D, BF16)], tm=128)[0]
    y_b = _matmul("proj_b", z_b, W["w_proj_b"], "nn")

    gate_ins = [Rows(p_gate), Rows(y_a), Rows(y_b)]
    mixed = _rows_call("gate_fwd", lambda *t: (_gate_fn(*t),), gate_ins, [], [("rows", D, BF16)])[0]
    mo = _matmul("proj_out", mixed, W["w_out"], "nn")

    def res1(x, mo, g):
        h1 = x + mo
        return h1, _rms(h1, g)
    h1, f = _rows_call("residual_norm_fwd", res1, [Rows(x), Rows(mo)], [W["g_ffn"]],
                       [("rows", D, F32), ("rows", D, BF16)])
    u1 = _matmul("ffn_up", f, W["w_ffn1"], "nn")
    act = _rows_call("ffn_act_fwd", lambda u: (jnp.square(jnp.maximum(u, 0.0)),), [Rows(u1)], [],
                     [("rows", D_FF, BF16)])[0]
    ff = _matmul("ffn_down", act, W["w_ffn2"], "nn")

    def head(h1, ff, tgt, g):
        def f_(h1, ff, g):
            y = _rms(h1 + ff, g)
            return 0.5 * jnp.sum(jnp.mean(jnp.square(y - tgt), axis=-1))
        loss, (dh2, _, dg) = jax.value_and_grad(f_, argnums=(0, 1, 2))(h1, ff, g)
        return dh2, jnp.full((8, LANES), loss, F32), dg
    dh2, loss_acc, G["g_final"] = _rows_call("loss_head", head, [Rows(h1), Rows(ff), Rows(target)], [W["g_final"]],
                                             [("rows", D, F32)], accs=[((8, LANES), F32), ((1, D), F32)])

    d_act = _matmul("ffn_down_dx", dh2, W["w_ffn2"], "nt")
    G["w_ffn2"] = _matmul("ffn_down_dw", act, dh2, "tn", out_dtype=GRAD_PAYLOAD)
    d_u1 = _rows_call("ffn_act_bwd", lambda u, d: (d * 2.0 * jnp.maximum(u, 0.0),), [Rows(u1), Rows(d_act)], [],
                      [("rows", D_FF, BF16)])[0]
    d_f = _matmul("ffn_up_dx", d_u1, W["w_ffn1"], "nt")
    G["w_ffn1"] = _matmul("ffn_up_dw", f, d_u1, "tn", out_blocks=N_DEV, out_dtype=GRAD_PAYLOAD)

    def res1_bwd(x, mo, d_f, dh2, g):
        _, vjp = jax.vjp(lambda h, g: _rms(h, g), x + mo, g)
        dh, dg = vjp(d_f)
        return dh2 + dh, dg
    dh1, G["g_ffn"] = _rows_call("residual_norm_bwd", res1_bwd, [Rows(x), Rows(mo), Rows(d_f), Rows(dh2)],
                                 [W["g_ffn"]], [("rows", D, F32)], accs=[((1, D), F32)])
    d_mixed = _matmul("proj_out_dx", dh1, W["w_out"], "nt")
    G["w_out"] = _matmul("proj_out_dw", mixed, dh1, "tn", out_dtype=GRAD_PAYLOAD)

    def gate_bwd(pg, ya, yb, dm):
        _, vjp = jax.vjp(_gate_fn, pg, ya, yb)
        return vjp(dm)
    d_gate, d_ya, d_yb = _rows_call("gate_bwd", gate_bwd, gate_ins + [Rows(d_mixed)], [],
                                    [("rows", 2 * D, F32), ("rows", D, BF16), ("rows", D, BF16)])

    d_s = _matmul("proj_a_dx", d_ya, W["w_proj_a"], "nt")
    G["w_proj_a"] = _matmul("proj_a_dw", s, d_ya, "tn", out_dtype=GRAD_PAYLOAD)

    def sgu_bwd(p, ds, *c):
        _, vjp = jax.vjp(_sgu_fn, p, *c)
        return vjp(ds)
    d_p_sgu, G["sgu_ln_w"], G["sgu_ln_b"], G["sgu_w"], G["sgu_bt"] = _rows_call(
        "sgu_bwd", sgu_bwd, [Rows(p_sgu), Rows(d_s)], sgu_consts, [("rows", 2 * D, F32)],
        accs=[((1, D), F32), ((1, D), F32), ((SGU_G, SGU_C, SGU_C), F32), ((SGU_C, SGU_G), F32)], tm=SGU_C)

    d_zb = _matmul("proj_b_dx", d_yb, W["w_proj_b"], "nt")
    G["w_proj_b"] = _matmul("proj_b_dw", z_b, d_yb, "tn", out_dtype=GRAD_PAYLOAD)

    def post_bwd(o, r, k2, v, g, dz, *c):
        _, vjp = jax.vjp(_post_fn, o, r, k2, v, g, *c)
        return vjp(dz)
    do_h, dr1, dk1, dv1, d_g, g_lnw, g_lnb, g_rk = _rows_call(
        "rwkv_post_bwd", post_bwd, post_ins + [Rows(d_zb)], post_consts, [("heads", F32)] * 4 + [("rows", D, F32)],
        accs=[((NH, 1, HN), F32)] * 3, tm=128)
    G["ln_x_w"], G["ln_x_b"], G["r_k"] = (t.reshape(1, D) for t in (g_lnw, g_lnb, g_rk))
    (dr2, dlw, dk2, dv2, dkk, daa), early = _scan_bwd(r_h, lw_h, k_h, v_h, kk_h, a_h, s0s, do_h,
                                                      ex=early_grads(G) if early_grads else None)

    def pre_bwd(qr, qk, qv, qxw, qxa, qxg, dr1, dr2, dlw, dk1, dk2, dv1, dv2, dkk, daa, dg, *c):
        _, vjp = jax.vjp(_pre_fn, qr, qk, qv, qxw, qxa, qxg, *c)
        g = vjp((dr1 + dr2, dlw, dk1 + dk2, dv1 + dv2, dkk, daa, dg))
        dq = jnp.concatenate(g[:6], axis=-1)
        return (dq,) + tuple(g[6:])
    pre_b_ins = q_ins + [Heads(dr1), Heads(dr2), Heads(dlw), Heads(dk1), Heads(dk2), Heads(dv1), Heads(dv2),
                         Heads(dkk), Heads(daa), Rows(d_g)]
    d_q, G["w_lora"], G["w0"], G["a_lora"], G["a0"], G["g_lora"], G["k_k"], G["k_a"] = _rows_call(
        "rwkv_pre_bwd", pre_bwd, pre_b_ins, pre_consts, [("rows", RW_INT, F32)],
        accs=[((128, D), F32), ((1, D), F32), ((128, D), F32), ((1, D), F32), ((256, D), F32), ((1, D), F32),
              ((1, D), F32)], tm=128)
    d_p_rw, dsb0, dsb1 = _mix_bwd(d_q, p_rw, W["sb"])
    G["sb"] = jnp.concatenate([dsb0, dsb1], axis=0)

    G["w_sgu_t"] = _matmul("proj_sgu_dw", d_p_sgu, a, "tn", out_dtype=GRAD_PAYLOAD)
    G["w_rw_t"] = _matmul("proj_rwkv_dw", d_p_rw, a, "tn", out_dtype=GRAD_PAYLOAD)
    G["w_gate_t"] = _matmul("proj_gate_dw", d_gate, a, "tn", out_dtype=GRAD_PAYLOAD)
    da1 = _matmul("proj_sgu_dx", d_p_sgu, W["w_sgu_t"], "nn")
    da2 = _matmul("proj_rwkv_dx", d_p_rw, W["w_rw_t"], "nn")
    da3 = _matmul("proj_gate_dx", d_gate, W["w_gate_t"], "nn")

    def norm1_bwd(x, da1, da2, da3, dh1, g):
        _, vjp = jax.vjp(_rms, x, g)
        dx, dg = vjp(da1 + da2 + da3)
        return dh1 + dx, dg
    dx, G["g_mix"] = _rows_call("norm_mix_bwd", norm1_bwd, [Rows(x), Rows(da1), Rows(da2), Rows(da3), Rows(dh1)],
                                [W["g_mix"]], [("rows", D, F32)], accs=[((1, D), F32)])
    return loss_acc[0, 0], dx, G, early


class Exchange:
    def __init__(self, bufs, gathers):
        self.bufs, self.gathers, self.nb = list(bufs), list(gathers), len(bufs)
        self.any_specs = [pl.BlockSpec(memory_space=pl.ANY)] * self.nb
        self.out_shape = [jax.ShapeDtypeStruct((N_DEV,) + (b.shape if g else b.shape[1:]), b.dtype)
                          for b, g in zip(self.bufs, self.gathers)]
        n = (N_DEV - 1) * self.nb
        self.sem_shapes = [pltpu.SemaphoreType.DMA((n,)), pltpu.SemaphoreType.DMA((n,)),
                           pltpu.SemaphoreType.DMA((self.nb,))]

    def _copies(self, in_refs, out_refs, sems):
        send_sems, recv_sems, local_sems = sems
        x, y, c = lax.axis_index("x"), lax.axis_index("y"), lax.axis_index("c")
        me = 4 * x + 2 * y + c

        def src(b, dest):
            return in_refs[b] if self.gathers[b] else in_refs[b].at[dest]

        local = [pltpu.make_async_copy(src(b, me), out_refs[b].at[me], local_sems.at[b]) for b in range(self.nb)]
        sends, recvs = [], []
        for kbits in range(1, N_DEV):
            px = 1 - x if kbits & 4 else x
            py = 1 - y if kbits & 2 else y
            pc = 1 - c if kbits & 1 else c
            peer = 4 * px + 2 * py + pc
            for b in range(self.nb):
                s = (kbits - 1) * self.nb + b
                sends.append(pltpu.make_async_remote_copy(
                    src_ref=src(b, peer), dst_ref=out_refs[b].at[me], send_sem=send_sems.at[s],
                    recv_sem=recv_sems.at[s], device_id=(px, py, pc), device_id_type=pl.DeviceIdType.MESH))
                recvs.append(pltpu.make_async_remote_copy(
                    src_ref=src(b, peer), dst_ref=out_refs[b].at[peer], send_sem=send_sems.at[s],
                    recv_sem=recv_sems.at[s], device_id=(px, py, pc), device_id_type=pl.DeviceIdType.MESH))
        return local, sends, recvs

    def start(self, in_refs, out_refs, sems):
        local, sends, _ = self._copies(in_refs, out_refs, sems)
        for cp in local + sends:
            cp.start()

    def wait(self, in_refs, out_refs, sems):
        local, sends, recvs = self._copies(in_refs, out_refs, sems)
        for cp in recvs:
            cp.wait_recv()
        for cp in sends:
            cp.wait_send()
        for cp in local:
            cp.wait()


def _exchange(name, bufs, gather):
    ex = Exchange(bufs, gather if isinstance(gather, (list, tuple)) else [gather] * len(bufs))

    def body(*refs):
        in_refs, out_refs, sems = refs[:ex.nb], refs[ex.nb:2 * ex.nb], refs[2 * ex.nb:]
        ex.start(in_refs, out_refs, sems)
        ex.wait(in_refs, out_refs, sems)

    return pl.pallas_call(body, name=name, in_specs=ex.any_specs, out_specs=ex.any_specs, out_shape=ex.out_shape,
                          scratch_shapes=ex.sem_shapes)(*ex.bufs)


N_CHIP = 4


def _pair_exchange(name, blocks):
    def body(b_ref, got_ref, send_sems, recv_sems):
        x, y, c = lax.axis_index("x"), lax.axis_index("y"), lax.axis_index("c")
        copies = [pltpu.make_async_remote_copy(
            src_ref=b_ref.at[2 * q + 1 - c], dst_ref=got_ref.at[q], send_sem=send_sems.at[q],
            recv_sem=recv_sems.at[q], device_id=(x, y, 1 - c), device_id_type=pl.DeviceIdType.MESH)
            for q in range(N_CHIP)]
        for cp in copies:
            cp.start()
        for cp in copies:
            cp.wait_recv()
        for cp in copies:
            cp.wait_send()

    any_spec = pl.BlockSpec(memory_space=pl.ANY)
    return pl.pallas_call(
        body, name=name, in_specs=[any_spec], out_specs=any_spec,
        out_shape=jax.ShapeDtypeStruct((N_CHIP,) + blocks.shape[1:], blocks.dtype),
        scratch_shapes=[pltpu.SemaphoreType.DMA((N_CHIP,))] * 2,
    )(blocks)


def _pair_sum(name, blocks, got, core):
    _, R, Wd = blocks.shape

    def body(c_ref, a_ref, b_ref, o_ref):
        o_ref[...] = (a_ref[...].astype(F32) + b_ref[...].astype(F32)).astype(o_ref.dtype)

    return pl.pallas_call(
        body, name=name,
        grid_spec=pltpu.PrefetchScalarGridSpec(
            num_scalar_prefetch=1, grid=(N_CHIP,),
            in_specs=[pl.BlockSpec((None, R, Wd), lambda q, c_ref: (2 * q + c_ref[0], 0, 0)),
                      pl.BlockSpec((None, R, Wd), lambda q, c_ref: (q, 0, 0))],
            out_specs=pl.BlockSpec((None, R, Wd), lambda q, c_ref: (q, 0, 0))),
        out_shape=jax.ShapeDtypeStruct(got.shape, blocks.dtype),
        compiler_params=pltpu.CompilerParams(dimension_semantics=("parallel",), vmem_limit_bytes=VMEM_LIMIT),
    )(core, blocks, got)


def _chip_exchange(name, sums, ex):
    def body(*refs):
        s_ref, x_in = refs[0], refs[1:1 + ex.nb]
        slots_ref, x_out = refs[1 + ex.nb], refs[2 + ex.nb:2 + 2 * ex.nb]
        send_sems, recv_sems, local_sem = refs[2 + 2 * ex.nb:5 + 2 * ex.nb]
        ex_sems = refs[5 + 2 * ex.nb:]
        x, y, c = lax.axis_index("x"), lax.axis_index("y"), lax.axis_index("c")
        my_q = 2 * x + y
        local = pltpu.make_async_copy(s_ref.at[my_q], slots_ref.at[my_q], local_sem)
        sends, recvs = [], []
        for kbits in range(1, N_CHIP):
            px = 1 - x if kbits & 2 else x
            py = 1 - y if kbits & 1 else y
            peer_q = 2 * px + py
            sends.append(pltpu.make_async_remote_copy(
                src_ref=s_ref.at[peer_q], dst_ref=slots_ref.at[my_q], send_sem=send_sems.at[kbits - 1],
                recv_sem=recv_sems.at[kbits - 1], device_id=(px, py, c), device_id_type=pl.DeviceIdType.MESH))
            recvs.append(pltpu.make_async_remote_copy(
                src_ref=s_ref.at[peer_q], dst_ref=slots_ref.at[peer_q], send_sem=send_sems.at[kbits - 1],
                recv_sem=recv_sems.at[kbits - 1], device_id=(px, py, c), device_id_type=pl.DeviceIdType.MESH))
        for cp in [local] + sends:
            cp.start()
        ex.start(x_in, x_out, ex_sems)
        for cp in recvs:
            cp.wait_recv()
        for cp in sends:
            cp.wait_send()
        local.wait()
        ex.wait(x_in, x_out, ex_sems)

    any_spec = pl.BlockSpec(memory_space=pl.ANY)
    res = pl.pallas_call(
        body, name=name, in_specs=[any_spec] + ex.any_specs, out_specs=[any_spec] + ex.any_specs,
        out_shape=[jax.ShapeDtypeStruct(sums.shape, sums.dtype)] + ex.out_shape,
        scratch_shapes=[pltpu.SemaphoreType.DMA((N_CHIP - 1,)), pltpu.SemaphoreType.DMA((N_CHIP - 1,)),
                        pltpu.SemaphoreType.DMA(())] + ex.sem_shapes,
    )(sums, *ex.bufs)
    return res[0], list(res[1:])


def _adamw(name, slots, w, m, v, tr=256):
    R, Wd = w.shape[-2:]
    depth_axis = w.ndim == 3
    if R % tr == 0:
        tc = Wd
    else:
        tr, tc = R, (256 if (Wd % 256 == 0 and R > 256) else Wd)

    def body(s_ref, w_ref, m_ref, v_ref, g_out, d_out, m_out, v_out):
        g = s_ref[0].astype(F32)
        for j in range(1, slots.shape[0]):
            g = g + s_ref[j].astype(F32)
        m_new = ADAM_B1 * m_ref[...] + (1.0 - ADAM_B1) * g
        v_new = ADAM_B2 * v_ref[...] + (1.0 - ADAM_B2) * jnp.square(g)
        m_hat = m_new / (1.0 - ADAM_B1 ** ADAM_STEP)
        v_hat = v_new / (1.0 - ADAM_B2 ** ADAM_STEP)
        g_out[...] = g
        d_out[...] = -ADAM_LR * (m_hat / (jnp.sqrt(v_hat) + ADAM_EPS) + ADAM_WD * w_ref[...])
        m_out[...] = m_new
        v_out[...] = v_new

    if depth_axis:
        row = pl.BlockSpec((None, tr, tc), lambda i, j: (0, i, j))
    else:
        row = pl.BlockSpec((tr, tc), lambda i, j: (i, j))
    return pl.pallas_call(
        body, name=name, grid=(R // tr, Wd // tc),
        in_specs=[pl.BlockSpec((slots.shape[0], tr, tc), lambda i, j: (0, i, j)), row, row, row],
        out_specs=[row] * 4, out_shape=[jax.ShapeDtypeStruct(w.shape, F32)] * 4,
        compiler_params=pltpu.CompilerParams(dimension_semantics=("parallel", "parallel"),
                                             vmem_limit_bytes=VMEM_LIMIT),
    )(slots, w, m, v)


PACK_W = 1024
_SMALL_SIZES = [int(np.prod(s)) for _, s in REPLICATED]
_SMALL_ROWS = _round_up(_round_up(sum(_SMALL_SIZES) + PACK_W, PACK_W) // PACK_W, 8)
_LOSS_AT = sum(_SMALL_SIZES)
W_IN_SHARD = P_TOTAL // N_DEV


def _pack_rows(parts, rows, dtype):
    flat = jnp.concatenate([p.reshape(-1).astype(dtype) for p in parts])
    return jnp.pad(flat, (0, rows * PACK_W - flat.shape[0])).reshape(rows, PACK_W)


def _w_in_groups_t(blocks):
    def rows(g0, g1):
        parts, g = [], g0
        while g < g1:
            j, r = divmod(g, W_IN_SHARD)
            n = min(W_IN_SHARD - r, g1 - g)
            parts.append(blocks[j, r:r + n])
            g += n
        return parts

    o, c = 2 * D, 2 * D + 3 * D
    z = lambda r: [jnp.zeros((r, D), blocks.dtype)]
    rw = (rows(o, c) + rows(c, c + L_W) + z(128 - L_W) + rows(c + L_W, c + L_W + L_A) + z(128 - L_A)
          + rows(c + L_W + L_A, o + C_B) + z(256 - L_G))
    cat = lambda parts: jnp.concatenate(parts, axis=0)
    return cat(rows(0, o)), cat(rw), cat(rows(o + C_B, P_TOTAL))


def _w_in_grad_blocks(g_sgu_t, g_rw_t, g_gate_t):
    c = 3 * D
    full = jnp.concatenate([g_sgu_t, g_rw_t[:c], g_rw_t[c:c + L_W], g_rw_t[c + 128:c + 128 + L_A],
                            g_rw_t[c + 256:c + 256 + L_G], g_gate_t], axis=0)
    return full.reshape(N_DEV, W_IN_SHARD, D)


def _all_gather_two_level(name, bufs):
    nb = len(bufs)

    def body(*refs):
        in_refs, out_refs = refs[:nb], refs[nb:2 * nb]
        send_sems, recv_sems, local_sems = refs[2 * nb:]
        x, y, c = lax.axis_index("x"), lax.axis_index("y"), lax.axis_index("c")
        me, sibling = (x, y, c), (x, y, 1 - c)
        chips = [(1 - x, y), (x, 1 - y), (1 - x, 1 - y)]

        def slot(b, dev):
            return out_refs[b].at[4 * dev[0] + 2 * dev[1] + dev[2]]

        def copy(b, k, block, to, own=False):
            return pltpu.make_async_remote_copy(
                src_ref=in_refs[b] if own else slot(b, block), dst_ref=slot(b, block),
                send_sem=send_sems.at[7 * b + k], recv_sem=recv_sems.at[7 * b + k], device_id=to,
                device_id_type=pl.DeviceIdType.MESH)

        local = [pltpu.make_async_copy(in_refs[b], slot(b, me), local_sems.at[b]) for b in range(nb)]
        first = [copy(b, 0, me, sibling, own=True) for b in range(nb)]
        first += [copy(b, 1 + j, me, (*chip, c), own=True) for j, chip in enumerate(chips) for b in range(nb)]
        for cp in local + first:
            cp.start()
        passed = []
        for j, chip in enumerate(chips):
            for b in range(nb):
                copy(b, 1 + j, (*chip, c), me).wait_recv()
                passed.append(copy(b, 4 + j, (*chip, c), sibling))
                passed[-1].start()
        for b in range(nb):
            copy(b, 0, sibling, me).wait_recv()
        for j, chip in enumerate(chips):
            for b in range(nb):
                copy(b, 4 + j, (*chip, 1 - c), me).wait_recv()
        for cp in first + passed:
            cp.wait_send()
        for cp in local:
            cp.wait()

    any_spec = pl.BlockSpec(memory_space=pl.ANY)
    return pl.pallas_call(
        body, name=name, in_specs=[any_spec] * nb, out_specs=[any_spec] * nb,
        out_shape=[jax.ShapeDtypeStruct((N_DEV,) + b.shape, b.dtype) for b in bufs],
        scratch_shapes=[pltpu.SemaphoreType.DMA((7 * nb,)), pltpu.SemaphoreType.DMA((7 * nb,)),
                        pltpu.SemaphoreType.DMA((nb,))],
    )(*bufs)


def _cols_from_blocks(blk):
    return jnp.transpose(blk, (1, 0, 2)).reshape(blk.shape[1], -1)


def _cols_to_blocks(g):
    r, c = g.shape
    return jnp.transpose(g.reshape(r, N_DEV, c // N_DEV), (1, 0, 2))


FIRST_WEIGHTS = ["w_in", "shift_b", "w_lora_w", "a_lora_w", "g_lora_w"]
LATE_WEIGHTS = ["w_proj_a", "w_proj_b", "w_out", "w_ffn1", "w_ffn2"]


def _late_weights(shards):
    ex = Exchange([shards[n].astype(BF16) for n in LATE_WEIGHTS], [True] * len(LATE_WEIGHTS))

    def finish(results):
        got = dict(zip(LATE_WEIGHTS, results))
        W = {n: got[n].reshape(-1, D) for n in ("w_proj_a", "w_proj_b", "w_out", "w_ffn2")}
        W["w_ffn1"] = got["w_ffn1"].reshape(N_DEV, D, -1)
        return W
    return ex, finish


def _gather_weights(shards):
    def payload(n):
        if n == "w_in":
            return jnp.transpose(shards[n][0]).astype(BF16)
        return shards[n] if n == "shift_b" else shards[n].astype(BF16)
    got = dict(zip(FIRST_WEIGHTS, _all_gather_two_level("weight_all_gather", [payload(n) for n in FIRST_WEIGHTS])))
    W = {}
    W["w_sgu_t"], W["w_rw_t"], W["w_gate_t"] = _w_in_groups_t(got["w_in"])
    z = lambda r, c, dt: jnp.zeros((r, c), dt)
    W["w_lora"] = jnp.concatenate([_cols_from_blocks(got["w_lora_w"][:, 0]).astype(F32), z(128 - L_W, D, F32)], axis=0)
    W["a_lora"] = jnp.concatenate([_cols_from_blocks(got["a_lora_w"][:, 0]).astype(F32), z(128 - L_A, D, F32)], axis=0)
    W["g_lora"] = jnp.concatenate([_cols_from_blocks(got["g_lora_w"][:, 0]).astype(F32), z(256 - L_G, D, F32)], axis=0)
    sb = _cols_from_blocks(got["shift_b"][:, 0])
    W["sb"] = jnp.concatenate([sb[:, :3 * D], sb[:, 3 * D:3 * D + L_W], z(2, 128 - L_W, F32),
                               sb[:, 3 * D + L_W:3 * D + L_W + L_A], z(2, 128 - L_A, F32),
                               sb[:, 3 * D + L_W + L_A:], z(2, 256 - L_G, F32)], axis=1)
    return W


def _replicated_weights(rep):
    W = {n: rep[n] for n in ("g_mix", "sgu_ln_w", "sgu_ln_b", "w0", "a0", "k_k", "k_a", "r_k", "ln_x_w", "ln_x_b",
                             "g_ffn")}
    W["g_final"] = rep["g_final"].reshape(1, D)
    W["sgu_w"] = rep["sgu_w"][0]
    W["sgu_bt"] = jnp.transpose(rep["sgu_b"][0])
    return W


def _late_grad_blocks(G):
    blocks = {n: G[n].reshape(N_DEV, -1, D) for n in ("w_proj_a", "w_proj_b", "w_out", "w_ffn2")}
    blocks["w_ffn1"] = G["w_ffn1"]
    return Exchange([blocks[n] for n in LATE_WEIGHTS], [False] * len(LATE_WEIGHTS))


def _first_grad_blocks(G):
    sbg = G["sb"]
    c = 3 * D
    sb = jnp.concatenate([sbg[:, :c], sbg[:, c:c + L_W], sbg[:, c + 128:c + 128 + L_A],
                          sbg[:, c + 256:c + 256 + L_G]], axis=1)
    return {
        "w_in": _w_in_grad_blocks(G["w_sgu_t"], G["w_rw_t"], G["w_gate_t"]),
        "shift_b": _cols_to_blocks(sb),
        "w_lora_w": _cols_to_blocks(G["w_lora"][:L_W]), "a_lora_w": _cols_to_blocks(G["a_lora"][:L_A]),
        "g_lora_w": _cols_to_blocks(G["g_lora"][:L_G]),
    }


def _replicated_grads(G):
    small = {n: G[n] for n in ("g_mix", "sgu_ln_w", "sgu_ln_b", "w0", "a0", "k_k", "k_a", "r_k", "ln_x_w", "ln_x_b",
                               "g_ffn", "g_final")}
    small["sgu_w"] = G["sgu_w"]
    small["sgu_b"] = jnp.transpose(G["sgu_bt"])
    return small


def kernel(x, g_mix, w_in, sgu_ln_w, sgu_ln_b, sgu_w, sgu_b, w_proj_a, shift_b, w_lora_w, w0, a_lora_w, a0, g_lora_w, k_k, k_a, r_k, ln_x_w, ln_x_b, w_proj_b, w_out, g_ffn, w_ffn1, w_ffn2, g_final, loss_target, m_g_mix, m_w_in, m_sgu_ln_w, m_sgu_ln_b, m_sgu_w, m_sgu_b, m_w_proj_a, m_shift_b, m_w_lora_w, m_w0, m_a_lora_w, m_a0, m_g_lora_w, m_k_k, m_k_a, m_r_k, m_ln_x_w, m_ln_x_b, m_w_proj_b, m_w_out, m_g_ffn, m_w_ffn1, m_w_ffn2, m_g_final, v_g_mix, v_w_in, v_sgu_ln_w, v_sgu_ln_b, v_sgu_w, v_sgu_b, v_w_proj_a, v_shift_b, v_w_lora_w, v_w0, v_a_lora_w, v_a0, v_g_lora_w, v_k_k, v_k_a, v_r_k, v_ln_x_w, v_ln_x_b, v_w_proj_b, v_w_out, v_g_ffn, v_w_ffn1, v_w_ffn2, v_g_final):
    env = dict(locals())
    weights = {n: env[n] for n in WEIGHT_ORDER}
    moms = {n: env["m_" + n] for n in WEIGHT_ORDER}
    vars_ = {n: env["v_" + n] for n in WEIGHT_ORDER}

    shards = {n: weights[n] for n, _, _ in SHARDED}
    W = _gather_weights(shards)
    W.update(_replicated_weights({n: weights[n] for n, _ in REPLICATED}))
    loss_part, dx, G, late_slots = _local_step(x[0], loss_target[0], W, late_weights=_late_weights(shards),
                                               early_grads=_late_grad_blocks)

    slots = dict(zip(LATE_WEIGHTS, late_slots))
    blocks = _first_grad_blocks(G)
    small = _replicated_grads(G)
    small_parts = [small[n] for n, _ in REPLICATED] + [jnp.full((PACK_W,), loss_part, F32)]
    got = _pair_exchange("grad_pair_exchange", blocks["w_in"])
    core = lax.axis_index("c").astype(jnp.int32).reshape(1)
    rest = [n for n in FIRST_WEIGHTS if n != "w_in"]
    ex = Exchange([blocks[n] for n in rest] + [_pack_rows(small_parts, _SMALL_ROWS, F32)], [False] * len(rest) + [True])
    slots["w_in"], res = _chip_exchange("grad_exchange", _pair_sum("grad_pair_sum", blocks["w_in"], got, core), ex)
    slots.update(zip(rest, res[:-1]))
    small_slots = res[-1]

    outs = [dict(), dict(), dict(), dict()]
    for n, _, _ in SHARDED:
        if n == "w_in":
            res = _adamw("adamw_" + n, slots[n], *[jnp.transpose(t[0]) for t in (weights[n], moms[n], vars_[n])])
            res = [jnp.transpose(t)[None] for t in res]
        else:
            res = _adamw("adamw_" + n, slots[n], weights[n], moms[n], vars_[n])
        for k in range(4):
            outs[k][n] = res[k]

    def packed(d):
        return _pack_rows([d[n] for n, _ in REPLICATED], _SMALL_ROWS, F32)
    small_out = _adamw("adamw_replicated", small_slots, packed(weights), packed(moms), packed(vars_))
    for k in range(4):
        flat = small_out[k].reshape(-1)
        off = 0
        for (n, s), size in zip(REPLICATED, _SMALL_SIZES):
            outs[k][n] = flat[off:off + size].reshape(s)
            off += size
    loss = small_out[0].reshape(-1)[_LOSS_AT]
    return (loss, dx[None], *[outs[0][n] for n in WEIGHT_ORDER], *[outs[1][n] for n in WEIGHT_ORDER],
            *[outs[2][n] for n in WEIGHT_ORDER], *[outs[3][n] for n in WEIGHT_ORDER])
```

```python
import functools
import numpy as np
import jax
import jax.numpy as jnp
from jax import lax
from jax.experimental import pallas as pl
from jax.experimental.pallas import tpu as pltpu

F32 = jnp.float32
BF16 = jnp.bfloat16

D = 1024
NH, HN = 16, 64
SGU_G, SGU_C = 8, 128
L_W, L_A, L_G = 64, 64, 160
C_B = 3 * D + L_W + L_A + L_G
P_TOTAL = 2 * D + C_B + 2 * D
D_FF = 4 * D
RW_INT = 3 * D + 128 + 128 + 256
NORM_EPS, LN_EPS, GN_EPS = 1e-6, 1e-5, 64e-5
N_DEV = 8
LANES = 128
SCAN_C = 64
SOLVE_B = 16
SCAN_PRECISION = lax.Precision.HIGH
GRAD_PAYLOAD = BF16
VMEM_LIMIT = 56 * 1024 * 1024

ADAM_LR, ADAM_B1, ADAM_B2, ADAM_EPS, ADAM_WD, ADAM_STEP = 0.001, 0.9, 0.999, 1e-08, 0.01, 10

SHARDED = [
    ("w_in", (D, P_TOTAL), 1), ("w_proj_a", (D, D), 0), ("shift_b", (2, C_B), 1), ("w_lora_w", (L_W, D), 1),
    ("a_lora_w", (L_A, D), 1), ("g_lora_w", (L_G, D), 1), ("w_proj_b", (D, D), 0), ("w_out", (D, D), 0),
    ("w_ffn1", (D, D_FF), 1), ("w_ffn2", (D_FF, D), 0),
]
REPLICATED = [
    ("g_mix", (1, D)), ("sgu_ln_w", (1, D)), ("sgu_ln_b", (1, D)), ("sgu_w", (1, SGU_G, SGU_C, SGU_C)),
    ("sgu_b", (1, SGU_G, SGU_C)), ("w0", (1, D)), ("a0", (1, D)), ("k_k", (1, D)), ("k_a", (1, D)), ("r_k", (1, D)),
    ("ln_x_w", (1, D)), ("ln_x_b", (1, D)), ("g_ffn", (1, D)), ("g_final", (D,)),
]
WEIGHT_ORDER = ["g_mix", "w_in", "sgu_ln_w", "sgu_ln_b", "sgu_w", "sgu_b", "w_proj_a", "shift_b", "w_lora_w", "w0",
                "a_lora_w", "a0", "g_lora_w", "k_k", "k_a", "r_k", "ln_x_w", "ln_x_b", "w_proj_b", "w_out", "g_ffn",
                "w_ffn1", "w_ffn2", "g_final"]


def _shard_shape(shape, axis):
    s = list(shape)
    s[axis] //= N_DEV
    return tuple(s)


def _round_up(n, m):
    return (n + m - 1) // m * m


def _pick(n, target):
    if n <= target:
        return n
    best = None
    for t in range(LANES, target + 1, LANES):
        if n % t == 0:
            best = t
    assert best is not None, (n, target)
    return best


def _matmul(name, a, b, mode, out_dtype=F32, tm=1024, tn=1024, tk=1024, out_blocks=None):
    b_blocks = b.shape[0] if b.ndim == 3 else None
    bshape = b.shape if b.ndim == 2 else (b.shape[1], b.shape[0] * b.shape[2])
    if mode == "nn":
        (M, K), (K2, N) = a.shape, bshape
    elif mode == "nt":
        (M, K), (N, K2) = a.shape, bshape
    else:
        (K, M), (K2, N) = a.shape, bshape
    assert K == K2, (name, a.shape, b.shape)
    assert b_blocks is None or mode != "tn"
    assert out_blocks is None or mode == "tn"
    tn = min(tn, N // (out_blocks or 1), bshape[1] // b_blocks if (b_blocks and mode == "nn") else tn)
    tk = min(tk, bshape[1] // b_blocks if (b_blocks and mode == "nt") else tk)
    tm, tn, tk = _pick(M, tm), _pick(N, tn), _pick(K, tk)
    nk = K // tk
    dims = {"nn": (((1,), (0,)), ((), ())), "nt": (((1,), (1,)), ((), ())), "tn": (((0,), (0,)), ((), ()))}[mode]

    def body(a_ref, b_ref, o_ref, acc_ref):
        k = pl.program_id(2)

        @pl.when(k == 0)
        def _():
            acc_ref[...] = jnp.zeros_like(acc_ref)

        acc_ref[...] += lax.dot_general(a_ref[...].astype(BF16), b_ref[...].astype(BF16), dims,
                                        preferred_element_type=F32)

        @pl.when(k == nk - 1)
        def _():
            o_ref[...] = acc_ref[...].astype(o_ref.dtype)

    a_spec = {"nn": pl.BlockSpec((tm, tk), lambda i, j, k: (i, k)), "nt": pl.BlockSpec((tm, tk), lambda i, j, k: (i, k)),
              "tn": pl.BlockSpec((tk, tm), lambda i, j, k: (k, i))}[mode]
    b_spec = {"nn": pl.BlockSpec((tk, tn), lambda i, j, k: (k, j)), "nt": pl.BlockSpec((tn, tk), lambda i, j, k: (j, k)),
              "tn": pl.BlockSpec((tk, tn), lambda i, j, k: (k, j))}[mode]
    if b_blocks and mode == "nn":
        per = b.shape[2] // tn
        b_spec = pl.BlockSpec((None, tk, tn), lambda i, j, k: (j // per, k, j % per))
    elif b_blocks:
        per = b.shape[2] // tk
        b_spec = pl.BlockSpec((None, tn, tk), lambda i, j, k: (k // per, j, k % per))
    out_spec = pl.BlockSpec((tm, tn), lambda i, j, k: (i, j))
    out_shape = jax.ShapeDtypeStruct((M, N), out_dtype)
    if out_blocks:
        per_o = N // out_blocks // tn
        out_spec = pl.BlockSpec((None, tm, tn), lambda i, j, k: (j // per_o, i, j % per_o))
        out_shape = jax.ShapeDtypeStruct((out_blocks, M, N // out_blocks), out_dtype)
    return pl.pallas_call(
        body, name=name, grid=(M // tm, N // tn, nk), in_specs=[a_spec, b_spec],
        out_specs=out_spec, out_shape=out_shape, scratch_shapes=[pltpu.VMEM((tm, tn), F32)],
        compiler_params=pltpu.CompilerParams(dimension_semantics=("parallel", "parallel", "arbitrary"),
                                             vmem_limit_bytes=VMEM_LIMIT),
    )(a, b)


class Rows:
    def __init__(self, arr, width=None, cb=0):
        self.arr, self.width, self.cb = arr, (arr.shape[1] if width is None else width), cb


class Heads:
    def __init__(self, arr):
        self.arr = arr


class Halo:
    def __init__(self, arr, side):
        self.arr, self.side = arr, side


def _rows_call(name, fn, ins, consts, outs, accs=(), tm=256, with_pid=False):
    T = next(o.arr.shape[1] if isinstance(o, Heads) else o.arr.shape[0] for o in ins if not isinstance(o, Halo))
    tm = min(tm, T)
    n_tiles = T // tm
    n_in, n_c, n_out = len(ins), len(consts), len(outs)
    in_specs, args = [], []
    for o in ins:
        if isinstance(o, Rows):
            in_specs.append(pl.BlockSpec((tm, o.width), lambda i, cb=o.cb: (i, cb)))
        elif isinstance(o, Heads):
            in_specs.append(pl.BlockSpec((NH, tm, HN), lambda i: (0, i, 0)))
        else:
            w = o.arr.shape[1]
            if o.side < 0:
                in_specs.append(pl.BlockSpec((8, w), lambda i: (jnp.maximum(i * (tm // 8) - 1, 0), 0)))
            else:
                in_specs.append(pl.BlockSpec((8, w), lambda i: (jnp.minimum((i + 1) * (tm // 8), T // 8 - 1), 0)))
        args.append(o.arr)
    for c in consts:
        in_specs.append(pl.BlockSpec(c.shape, lambda i, nd=c.ndim: (0,) * nd))
        args.append(c)
    out_specs, out_shape = [], []
    for o in outs:
        if o[0] == "rows":
            out_specs.append(pl.BlockSpec((tm, o[1]), lambda i: (i, 0)))
            out_shape.append(jax.ShapeDtypeStruct((T, o[1]), o[2]))
        else:
            out_specs.append(pl.BlockSpec((NH, tm, HN), lambda i: (0, i, 0)))
            out_shape.append(jax.ShapeDtypeStruct((NH, T, HN), o[1]))
    for shape, dt in accs:
        out_specs.append(pl.BlockSpec(shape, lambda i, nd=len(shape): (0,) * nd))
        out_shape.append(jax.ShapeDtypeStruct(shape, dt))

    def body(*refs):
        i = pl.program_id(0)
        vals = []
        vals = [r[...] for r in refs[:n_in + n_c]]
        res = fn(i, n_tiles, *vals) if with_pid else fn(*vals)
        out_refs = refs[n_in + n_c:]
        for r, v in zip(out_refs[:n_out], res[:n_out]):
            r[...] = v.astype(r.dtype)
        if accs:
            @pl.when(i == 0)
            def _():
                for r in out_refs[n_out:]:
                    r[...] = jnp.zeros_like(r)

            for r, v in zip(out_refs[n_out:], res[n_out:]):
                r[...] += v.astype(r.dtype)

    res = pl.pallas_call(
        body, name=name, grid=(n_tiles,), in_specs=in_specs, out_specs=out_specs, out_shape=out_shape,
        compiler_params=pltpu.CompilerParams(dimension_semantics=("arbitrary",), vmem_limit_bytes=VMEM_LIMIT),
    )(*args)
    return res


def _rms(x, g):
    return x * lax.rsqrt(jnp.mean(x * x, axis=-1, keepdims=True) + NORM_EPS) * g


def _gelu(x):
    return 0.5 * x * (1.0 + lax.erf(x * 0.7071067811865476))


def _sigmoid(x):
    return 1.0 / (1.0 + jnp.exp(-x))


def _bdot(a, b):
    return jnp.dot(a.astype(BF16), b.astype(BF16), preferred_element_type=F32)


def _to_heads(x):
    return jnp.concatenate([x[:, h * HN:(h + 1) * HN][None] for h in range(NH)], axis=0)


def _from_heads(xh):
    return jnp.concatenate([xh[h] for h in range(NH)], axis=-1)


def _sgu_fn(p, ln_w, ln_b, sw, sbt):
    z = _gelu(p)
    u, v = z[:, :D], z[:, D:]
    mu = jnp.mean(v, axis=-1, keepdims=True)
    var = jnp.mean(jnp.square(v - mu), axis=-1, keepdims=True)
    vn = (v - mu) * lax.rsqrt(var + LN_EPS) * ln_w + ln_b
    ri = lax.broadcasted_iota(jnp.int32, (SGU_C, SGU_C), 0)
    ci = lax.broadcasted_iota(jnp.int32, (SGU_C, SGU_C), 1)
    mask = (ci <= ri).astype(F32)
    dg = D // SGU_G
    parts = []
    for g in range(SGU_G):
        parts.append(_bdot(sw[g] * mask, vn[:, g * dg:(g + 1) * dg]) + sbt[:, g:g + 1])
    return u * jnp.concatenate(parts, axis=-1)


def _pre_fn(qr, qk, qv, qxw, qxa, qxg, wl, w0, al, a0, gl, k_k, k_a):
    w = -jax.nn.softplus(-(w0 + _bdot(jnp.tanh(qxw), wl))) - 0.5
    lw = -jnp.exp(w)
    aa = _sigmoid(a0 + _bdot(qxa, al))
    g = _bdot(_sigmoid(qxg), gl)
    kk = _to_heads(qk * k_k)
    kk = kk / jnp.maximum(jnp.sqrt(jnp.sum(kk * kk, axis=-1, keepdims=True)), 1e-12)
    k2 = qk * (1.0 + (aa - 1.0) * k_a)
    return _to_heads(qr), _to_heads(lw), _to_heads(k2), _to_heads(qv), kk, _to_heads(aa), g


def _post_fn(o, r, k2, v, g, ln_w, ln_b, r_k):
    mu = jnp.mean(o, axis=-1, keepdims=True)
    d = o - mu
    var = jnp.mean(d * d, axis=-1, keepdims=True)
    on = d * lax.rsqrt(var + GN_EPS) * ln_w + ln_b
    bonus = jnp.sum(r * k2 * r_k, axis=-1, keepdims=True) * v
    return _from_heads(on + bonus) * g


def _gate_fn(pg, ya, yb):
    return _sigmoid(pg[:, :D]) * ya + _sigmoid(pg[:, D:]) * yb


def _bmm(x, y, cx, cy):
    return lax.dot_general(x, y, (((cx,), (cy,)), ((0,), (0,))), precision=SCAN_PRECISION,
                           preferred_element_type=F32)


def _unit_lower_inverse(M):
    C = M.shape[1]
    ti = lax.broadcasted_iota(jnp.int32, (C, C), 0)
    tj = lax.broadcasted_iota(jnp.int32, (C, C), 1)
    eye = (ti == tj).astype(F32)
    same = lambda b: (ti // b == tj // b).astype(F32)
    X = -(M * same(SOLVE_B))
    inv = eye + X
    span = 1
    while 2 * span < SOLVE_B:
        X = _bmm(X, X, 2, 1)
        inv = inv + _bmm(inv, X, 2, 1)
        span *= 2
    b = SOLVE_B
    while b < C:
        low = M * (same(2 * b) - same(b))
        inv = inv - _bmm(_bmm(inv, low, 2, 1), inv, 2, 1)
        b *= 2
    return inv


@jax.custom_vjp
def _unit_lower_solve(M, y):
    return _bmm(_unit_lower_inverse(M), y, 2, 1)


def _unit_lower_solve_fwd(M, y):
    inv = _unit_lower_inverse(M)
    u = _bmm(inv, y, 2, 1)
    return u, (inv, u)


def _unit_lower_solve_bwd(res, du):
    inv, u = res
    dy = _bmm(inv, du, 1, 1)
    return -_bmm(dy, u, 2, 2), dy


_unit_lower_solve.defvjp(_unit_lower_solve_fwd, _unit_lower_solve_bwd)


def _chunk_fn(S0, r, lw, k, v, kk, a):
    C = SCAN_C
    bmm = _bmm
    ti = lax.broadcasted_iota(jnp.int32, (C, C), 0)
    tj = lax.broadcasted_iota(jnp.int32, (C, C), 1)
    incl = (tj <= ti).astype(F32)
    strict = (tj < ti).astype(F32)
    cum = lax.dot_general(jnp.broadcast_to(incl, (NH, C, C)), lw, (((2,), (1,)), ((0,), (0,))),
                          precision=lax.Precision.HIGHEST, preferred_element_type=F32)
    g_in, g_ex, g_inv = jnp.exp(cum), jnp.exp(cum - lw), jnp.exp(-cum)
    kkt, rt = kk * g_ex, r * g_in
    bk = jnp.concatenate([kk * a * g_inv, k * g_inv], axis=1)
    A = bmm(kkt, bk, 2, 2)
    M = A[:, :, :C] * strict
    n_mask = jnp.concatenate([jnp.zeros((C, C), F32), strict], axis=1)
    zv = jnp.concatenate([jnp.zeros_like(v), v], axis=1)
    s0_side = bmm(jnp.concatenate([kkt, rt], axis=1), S0, 2, 2)
    y = _unit_lower_solve(M, s0_side[:, :C] + bmm(A * n_mask, zv, 2, 1))
    z = jnp.concatenate([-y, v], axis=1)
    O = s0_side[:, C:] + bmm(bmm(rt, bk, 2, 2) * jnp.concatenate([incl, incl], axis=1), z, 2, 1)
    S1 = (S0 + bmm(z, bk, 1, 1)) * g_in[:, C - 1:C, :]
    return O, S1


def _scan_fwd(r, lw, k, v, kk, a, ex=None, tb=256):
    T = r.shape[1]
    tb = min(tb, T)
    n_chunks = tb // SCAN_C
    nb = T // tb
    nx = ex.nb if ex else 0

    def body(*refs):
        r_ref, lw_ref, k_ref, v_ref, kk_ref, a_ref = refs[:6]
        x_in, (o_ref, s0_ref), x_out = refs[6:6 + nx], refs[6 + nx:8 + nx], refs[8 + nx:8 + 2 * nx]
        s_ref, sems = refs[8 + 2 * nx], refs[9 + 2 * nx:]

        @pl.when(pl.program_id(0) == 0)
        def _():
            s_ref[...] = jnp.zeros_like(s_ref)
            if ex:
                ex.start(x_in, x_out, sems)

        def step(c, carry):
            sl = pl.ds(pl.multiple_of(c * SCAN_C, SCAN_C), SCAN_C)
            S0 = s_ref[...]
            s0_ref[c] = S0
            O, S1 = _chunk_fn(S0, r_ref[:, sl, :], lw_ref[:, sl, :], k_ref[:, sl, :], v_ref[:, sl, :],
                              kk_ref[:, sl, :], a_ref[:, sl, :])
            o_ref[:, sl, :] = O
            s_ref[...] = S1
            return carry

        lax.fori_loop(0, n_chunks, step, 0)

        if ex:
            @pl.when(pl.program_id(0) == nb - 1)
            def _():
                ex.wait(x_in, x_out, sems)

    hm = pl.BlockSpec((NH, tb, HN), lambda i: (0, i, 0))
    res = pl.pallas_call(
        body, name="rwkv_scan_fwd", grid=(nb,), in_specs=[hm] * 6 + (ex.any_specs if ex else []),
        out_specs=[hm, pl.BlockSpec((n_chunks, NH, HN, HN), lambda i: (i, 0, 0, 0))] + (ex.any_specs if ex else []),
        out_shape=[jax.ShapeDtypeStruct((NH, T, HN), F32), jax.ShapeDtypeStruct((T // SCAN_C, NH, HN, HN), F32)]
        + (ex.out_shape if ex else []),
        scratch_shapes=[pltpu.VMEM((NH, HN, HN), F32)] + (ex.sem_shapes if ex else []),
        compiler_params=pltpu.CompilerParams(dimension_semantics=("arbitrary",), vmem_limit_bytes=VMEM_LIMIT),
    )(r, lw, k, v, kk, a, *(ex.bufs if ex else []))
    return res[0], res[1], list(res[2:])


def _scan_bwd(r, lw, k, v, kk, a, s0s, do, ex=None, tb=128):
    T = r.shape[1]
    tb = min(tb, T)
    n_chunks = tb // SCAN_C
    nb = T // tb
    nx = ex.nb if ex else 0

    def body(*refs):
        r_ref, lw_ref, k_ref, v_ref, kk_ref, a_ref, s0_ref, do_ref = refs[:8]
        x_in, (dr, dlw, dk, dv, dkk, da), x_out = refs[8:8 + nx], refs[8 + nx:14 + nx], refs[14 + nx:14 + 2 * nx]
        ds_ref, sems = refs[14 + 2 * nx], refs[15 + 2 * nx:]

        @pl.when(pl.program_id(0) == 0)
        def _():
            ds_ref[...] = jnp.zeros_like(ds_ref)
            if ex:
                ex.start(x_in, x_out, sems)

        def step(j, carry):
            c = n_chunks - 1 - j
            sl = pl.ds(pl.multiple_of(c * SCAN_C, SCAN_C), SCAN_C)
            _, vjp = jax.vjp(_chunk_fn, s0_ref[c], r_ref[:, sl, :], lw_ref[:, sl, :], k_ref[:, sl, :],
                             v_ref[:, sl, :], kk_ref[:, sl, :], a_ref[:, sl, :])
            g = vjp((do_ref[:, sl, :], ds_ref[...]))
            ds_ref[...] = g[0]
            for ref, val in zip((dr, dlw, dk, dv, dkk, da), g[1:]):
                ref[:, sl, :] = val
            return carry

        lax.fori_loop(0, n_chunks, step, 0)

        if ex:
            @pl.when(pl.program_id(0) == nb - 1)
            def _():
                ex.wait(x_in, x_out, sems)

    hm = pl.BlockSpec((NH, tb, HN), lambda i: (0, nb - 1 - i, 0))
    res = pl.pallas_call(
        body, name="rwkv_scan_bwd", grid=(nb,),
        in_specs=[hm] * 6 + [pl.BlockSpec((n_chunks, NH, HN, HN), lambda i: (nb - 1 - i, 0, 0, 0)), hm]
        + (ex.any_specs if ex else []),
        out_specs=[hm] * 6 + (ex.any_specs if ex else []),
        out_shape=[jax.ShapeDtypeStruct((NH, T, HN), F32)] * 6 + (ex.out_shape if ex else []),
        scratch_shapes=[pltpu.VMEM((NH, HN, HN), F32)] + (ex.sem_shapes if ex else []),
        compiler_params=pltpu.CompilerParams(dimension_semantics=("arbitrary",), vmem_limit_bytes=VMEM_LIMIT),
    )(r, lw, k, v, kk, a, s0s, do, *(ex.bufs if ex else []))
    return list(res[:6]), list(res[6:])


def _shift_down(i, p, prev8):
    first = jnp.where(i > 0, prev8[7:8, :], 0.0)
    row = lax.broadcasted_iota(jnp.int32, p.shape, 0)
    return jnp.where(row == 0, first, pltpu.roll(p, 1, axis=0))


def _mix_fwd(p, sb, tm=256):
    def fn(i, n, p, prev8, sb):
        return (p * sb[0:1] + _shift_down(i, p, prev8) * sb[1:2],)
    return _rows_call("shift_mix_fwd", fn, [Rows(p), Halo(p, -1)], [sb], [("rows", p.shape[1], F32)], tm=tm,
                      with_pid=True)[0]


def _mix_bwd(dq, p, sb, tm=256):
    def fn(i, n, dq, next8, p, prev8, sb):
        ps = _shift_down(i, p, prev8)
        d1 = dq * sb[1:2]
        last = jnp.where(i < n - 1, next8[0:1, :] * sb[1:2], 0.0)
        row = lax.broadcasted_iota(jnp.int32, dq.shape, 0)
        up = jnp.where(row == dq.shape[0] - 1, last, pltpu.roll(d1, dq.shape[0] - 1, axis=0))
        return (dq * sb[0:1] + up, jnp.sum(dq * p, axis=0, keepdims=True), jnp.sum(dq * ps, axis=0, keepdims=True))
    w = p.shape[1]
    return _rows_call("shift_mix_bwd", fn, [Rows(dq), Halo(dq, +1), Rows(p), Halo(p, -1)], [sb], [("rows", w, F32)],
                      accs=[((1, w), F32), ((1, w), F32)], tm=tm, with_pid=True)


def _local_step(x, target, W, late_weights=None, early_grads=None):
    G = {}
    a = _rows_call("norm_mix_fwd", lambda x, g: (_rms(x, g),), [Rows(x)], [W["g_mix"]], [("rows", D, BF16)])[0]
    p_sgu = _matmul("proj_sgu", a, W["w_sgu_t"], "nt")
    p_rw = _matmul("proj_rwkv", a, W["w_rw_t"], "nt")
    p_gate = _matmul("proj_gate", a, W["w_gate_t"], "nt")

    sgu_consts = [W["sgu_ln_w"], W["sgu_ln_b"], W["sgu_w"], W["sgu_bt"]]
    s = _rows_call("sgu_fwd", lambda *t: (_sgu_fn(*t),), [Rows(p_sgu)], sgu_consts, [("rows", D, BF16)], tm=SGU_C)[0]

    q = _mix_fwd(p_rw, W["sb"])
    q_ins = [Rows(q, D, 0), Rows(q, D, 1), Rows(q, D, 2), Rows(q, 128, 24), Rows(q, 128, 25), Rows(q, 256, 13)]
    pre_consts = [W["w_lora"], W["w0"], W["a_lora"], W["a0"], W["g_lora"], W["k_k"], W["k_a"]]
    r_h, lw_h, k_h, v_h, kk_h, a_h, g_gate = _rows_call(
        "rwkv_pre_fwd", _pre_fn, q_ins, pre_consts, [("heads", F32)] * 6 + [("rows", D, F32)], tm=128)
    o_h, s0s, got = _scan_fwd(r_h, lw_h, k_h, v_h, kk_h, a_h, ex=late_weights[0] if late_weights else None)
    if late_weights:
        W = {**W, **late_weights[1](got)}
    y_a = _matmul("proj_a", s, W["w_proj_a"], "nn")
    post_ins = [Heads(o_h), Heads(r_h), Heads(k_h), Heads(v_h), Rows(g_gate)]
    post_consts = [W[n].reshape(NH, 1, HN) for n in ("ln_x_w", "ln_x_b", "r_k")]
    z_b = _rows_call("rwkv_post_fwd", lambda *t: (_post_fn(*t),), post_ins, post_consts, [("rows", D, BF16)], tm=128)[0]
    y_b = _matmul("proj_b", z_b, W["w_proj_b"], "nn")

    gate_ins = [Rows(p_gate), Rows(y_a), Rows(y_b)]
    mixed = _rows_call("gate_fwd", lambda *t: (_gate_fn(*t),), gate_ins, [], [("rows", D, BF16)])[0]
    mo = _matmul("proj_out", mixed, W["w_out"], "nn")

    def res1(x, mo, g):
        h1 = x + mo
        return h1, _rms(h1, g)
    h1, f = _rows_call("residual_norm_fwd", res1, [Rows(x), Rows(mo)], [W["g_ffn"]],
                       [("rows", D, F32), ("rows", D, BF16)])
    u1 = _matmul("ffn_up", f, W["w_ffn1"], "nn")
    act = _rows_call("ffn_act_fwd", lambda u: (jnp.square(jnp.maximum(u, 0.0)),), [Rows(u1)], [],
                     [("rows", D_FF, BF16)])[0]
    ff = _matmul("ffn_down", act, W["w_ffn2"], "nn")

    def head(h1, ff, tgt, g):
        def f_(h1, ff, g):
            y = _rms(h1 + ff, g)
            return 0.5 * jnp.sum(jnp.mean(jnp.square(y - tgt), axis=-1))
        loss, (dh2, _, dg) = jax.value_and_grad(f_, argnums=(0, 1, 2))(h1, ff, g)
        return dh2, jnp.full((8, LANES), loss, F32), dg
    dh2, loss_acc, G["g_final"] = _rows_call("loss_head", head, [Rows(h1), Rows(ff), Rows(target)], [W["g_final"]],
                                             [("rows", D, F32)], accs=[((8, LANES), F32), ((1, D), F32)])

    d_act = _matmul("ffn_down_dx", dh2, W["w_ffn2"], "nt")
    G["w_ffn2"] = _matmul("ffn_down_dw", act, dh2, "tn", out_dtype=GRAD_PAYLOAD)
    d_u1 = _rows_call("ffn_act_bwd", lambda u, d: (d * 2.0 * jnp.maximum(u, 0.0),), [Rows(u1), Rows(d_act)], [],
                      [("rows", D_FF, BF16)])[0]
    d_f = _matmul("ffn_up_dx", d_u1, W["w_ffn1"], "nt")
    G["w_ffn1"] = _matmul("ffn_up_dw", f, d_u1, "tn", out_blocks=N_DEV, out_dtype=GRAD_PAYLOAD)

    def res1_bwd(x, mo, d_f, dh2, g):
        _, vjp = jax.vjp(lambda h, g: _rms(h, g), x + mo, g)
        dh, dg = vjp(d_f)
        return dh2 + dh, dg
    dh1, G["g_ffn"] = _rows_call("residual_norm_bwd", res1_bwd, [Rows(x), Rows(mo), Rows(d_f), Rows(dh2)],
                                 [W["g_ffn"]], [("rows", D, F32)], accs=[((1, D), F32)])
    d_mixed = _matmul("proj_out_dx", dh1, W["w_out"], "nt")
    G["w_out"] = _matmul("proj_out_dw", mixed, dh1, "tn", out_dtype=GRAD_PAYLOAD)

    def gate_bwd(pg, ya, yb, dm):
        _, vjp = jax.vjp(_gate_fn, pg, ya, yb)
        return vjp(dm)
    d_gate, d_ya, d_yb = _rows_call("gate_bwd", gate_bwd, gate_ins + [Rows(d_mixed)], [],
                                    [("rows", 2 * D, F32), ("rows", D, BF16), ("rows", D, BF16)])

    d_s = _matmul("proj_a_dx", d_ya, W["w_proj_a"], "nt")
    G["w_proj_a"] = _matmul("proj_a_dw", s, d_ya, "tn", out_dtype=GRAD_PAYLOAD)

    def sgu_bwd(p, ds, *c):
        _, vjp = jax.vjp(_sgu_fn, p, *c)
        return vjp(ds)
    d_p_sgu, G["sgu_ln_w"], G["sgu_ln_b"], G["sgu_w"], G["sgu_bt"] = _rows_call(
        "sgu_bwd", sgu_bwd, [Rows(p_sgu), Rows(d_s)], sgu_consts, [("rows", 2 * D, F32)],
        accs=[((1, D), F32), ((1, D), F32), ((SGU_G, SGU_C, SGU_C), F32), ((SGU_C, SGU_G), F32)], tm=SGU_C)

    d_zb = _matmul("proj_b_dx", d_yb, W["w_proj_b"], "nt")
    G["w_proj_b"] = _matmul("proj_b_dw", z_b, d_yb, "tn", out_dtype=GRAD_PAYLOAD)

    def post_bwd(o, r, k2, v, g, dz, *c):
        _, vjp = jax.vjp(_post_fn, o, r, k2, v, g, *c)
        return vjp(dz)
    do_h, dr1, dk1, dv1, d_g, g_lnw, g_lnb, g_rk = _rows_call(
        "rwkv_post_bwd", post_bwd, post_ins + [Rows(d_zb)], post_consts, [("heads", F32)] * 4 + [("rows", D, F32)],
        accs=[((NH, 1, HN), F32)] * 3, tm=128)
    G["ln_x_w"], G["ln_x_b"], G["r_k"] = (t.reshape(1, D) for t in (g_lnw, g_lnb, g_rk))
    (dr2, dlw, dk2, dv2, dkk, daa), early = _scan_bwd(r_h, lw_h, k_h, v_h, kk_h, a_h, s0s, do_h,
                                                      ex=early_grads(G) if early_grads else None)

    def pre_bwd(qr, qk, qv, qxw, qxa, qxg, dr1, dr2, dlw, dk1, dk2, dv1, dv2, dkk, daa, dg, *c):
        _, vjp = jax.vjp(_pre_fn, qr, qk, qv, qxw, qxa, qxg, *c)
        g = vjp((dr1 + dr2, dlw, dk1 + dk2, dv1 + dv2, dkk, daa, dg))
        dq = jnp.concatenate(g[:6], axis=-1)
        return (dq,) + tuple(g[6:])
    pre_b_ins = q_ins + [Heads(dr1), Heads(dr2), Heads(dlw), Heads(dk1), Heads(dk2), Heads(dv1), Heads(dv2),
                         Heads(dkk), Heads(daa), Rows(d_g)]
    d_q, G["w_lora"], G["w0"], G["a_lora"], G["a0"], G["g_lora"], G["k_k"], G["k_a"] = _rows_call(
        "rwkv_pre_bwd", pre_bwd, pre_b_ins, pre_consts, [("rows", RW_INT, F32)],
        accs=[((128, D), F32), ((1, D), F32), ((128, D), F32), ((1, D), F32), ((256, D), F32), ((1, D), F32),
              ((1, D), F32)], tm=128)
    d_p_rw, dsb0, dsb1 = _mix_bwd(d_q, p_rw, W["sb"])
    G["sb"] = jnp.concatenate([dsb0, dsb1], axis=0)

    G["w_sgu_t"] = _matmul("proj_sgu_dw", d_p_sgu, a, "tn", out_dtype=GRAD_PAYLOAD)
    G["w_rw_t"] = _matmul("proj_rwkv_dw", d_p_rw, a, "tn", out_dtype=GRAD_PAYLOAD)
    G["w_gate_t"] = _matmul("proj_gate_dw", d_gate, a, "tn", out_dtype=GRAD_PAYLOAD)
    da1 = _matmul("proj_sgu_dx", d_p_sgu, W["w_sgu_t"], "nn")
    da2 = _matmul("proj_rwkv_dx", d_p_rw, W["w_rw_t"], "nn")
    da3 = _matmul("proj_gate_dx", d_gate, W["w_gate_t"], "nn")

    def norm1_bwd(x, da1, da2, da3, dh1, g):
        _, vjp = jax.vjp(_rms, x, g)
        dx, dg = vjp(da1 + da2 + da3)
        return dh1 + dx, dg
    dx, G["g_mix"] = _rows_call("norm_mix_bwd", norm1_bwd, [Rows(x), Rows(da1), Rows(da2), Rows(da3), Rows(dh1)],
                                [W["g_mix"]], [("rows", D, F32)], accs=[((1, D), F32)])
    return loss_acc[0, 0], dx, G, early


class Exchange:
    def __init__(self, bufs, gathers):
        self.bufs, self.gathers, self.nb = list(bufs), list(gathers), len(bufs)
        self.any_specs = [pl.BlockSpec(memory_space=pl.ANY)] * self.nb
        self.out_shape = [jax.ShapeDtypeStruct((N_DEV,) + (b.shape if g else b.shape[1:]), b.dtype)
                          for b, g in zip(self.bufs, self.gathers)]
        n = (N_DEV - 1) * self.nb
        self.sem_shapes = [pltpu.SemaphoreType.DMA((n,)), pltpu.SemaphoreType.DMA((n,)),
                           pltpu.SemaphoreType.DMA((self.nb,))]

    def _copies(self, in_refs, out_refs, sems):
        send_sems, recv_sems, local_sems = sems
        x, y, c = lax.axis_index("x"), lax.axis_index("y"), lax.axis_index("c")
        me = 4 * x + 2 * y + c

        def src(b, dest):
            return in_refs[b] if self.gathers[b] else in_refs[b].at[dest]

        local = [pltpu.make_async_copy(src(b, me), out_refs[b].at[me], local_sems.at[b]) for b in range(self.nb)]
        sends, recvs = [], []
        for kbits in range(1, N_DEV):
            px = 1 - x if kbits & 4 else x
            py = 1 - y if kbits & 2 else y
            pc = 1 - c if kbits & 1 else c
            peer = 4 * px + 2 * py + pc
            for b in range(self.nb):
                s = (kbits - 1) * self.nb + b
                sends.append(pltpu.make_async_remote_copy(
                    src_ref=src(b, peer), dst_ref=out_refs[b].at[me], send_sem=send_sems.at[s],
                    recv_sem=recv_sems.at[s], device_id=(px, py, pc), device_id_type=pl.DeviceIdType.MESH))
                recvs.append(pltpu.make_async_remote_copy(
                    src_ref=src(b, peer), dst_ref=out_refs[b].at[peer], send_sem=send_sems.at[s],
                    recv_sem=recv_sems.at[s], device_id=(px, py, pc), device_id_type=pl.DeviceIdType.MESH))
        return local, sends, recvs

    def start(self, in_refs, out_refs, sems):
        local, sends, _ = self._copies(in_refs, out_refs, sems)
        for cp in sends + local:
            cp.start()

    def wait(self, in_refs, out_refs, sems):
        local, sends, recvs = self._copies(in_refs, out_refs, sems)
        for cp in recvs:
            cp.wait_recv()
        for cp in sends:
            cp.wait_send()
        for cp in local:
            cp.wait()


def _exchange(name, bufs, gather):
    ex = Exchange(bufs, gather if isinstance(gather, (list, tuple)) else [gather] * len(bufs))

    def body(*refs):
        in_refs, out_refs, sems = refs[:ex.nb], refs[ex.nb:2 * ex.nb], refs[2 * ex.nb:]
        ex.start(in_refs, out_refs, sems)
        ex.wait(in_refs, out_refs, sems)

    return pl.pallas_call(body, name=name, in_specs=ex.any_specs, out_specs=ex.any_specs, out_shape=ex.out_shape,
                          scratch_shapes=ex.sem_shapes)(*ex.bufs)


N_CHIP = 4


def _pair_exchange(name, blocks):
    def body(b_ref, got_ref, send_sems, recv_sems):
        x, y, c = lax.axis_index("x"), lax.axis_index("y"), lax.axis_index("c")
        copies = [pltpu.make_async_remote_copy(
            src_ref=b_ref.at[2 * q + 1 - c], dst_ref=got_ref.at[q], send_sem=send_sems.at[q],
            recv_sem=recv_sems.at[q], device_id=(x, y, 1 - c), device_id_type=pl.DeviceIdType.MESH)
            for q in range(N_CHIP)]
        for cp in copies:
            cp.start()
        for cp in copies:
            cp.wait_recv()
        for cp in copies:
            cp.wait_send()

    any_spec = pl.BlockSpec(memory_space=pl.ANY)
    return pl.pallas_call(
        body, name=name, in_specs=[any_spec], out_specs=any_spec,
        out_shape=jax.ShapeDtypeStruct((N_CHIP,) + blocks.shape[1:], blocks.dtype),
        scratch_shapes=[pltpu.SemaphoreType.DMA((N_CHIP,))] * 2,
    )(blocks)


def _pair_sum(name, blocks, got, core):
    _, R, Wd = blocks.shape

    def body(c_ref, a_ref, b_ref, o_ref):
        o_ref[...] = (a_ref[...].astype(F32) + b_ref[...].astype(F32)).astype(o_ref.dtype)

    return pl.pallas_call(
        body, name=name,
        grid_spec=pltpu.PrefetchScalarGridSpec(
            num_scalar_prefetch=1, grid=(N_CHIP,),
            in_specs=[pl.BlockSpec((None, R, Wd), lambda q, c_ref: (2 * q + c_ref[0], 0, 0)),
                      pl.BlockSpec((None, R, Wd), lambda q, c_ref: (q, 0, 0))],
            out_specs=pl.BlockSpec((None, R, Wd), lambda q, c_ref: (q, 0, 0))),
        out_shape=jax.ShapeDtypeStruct(got.shape, blocks.dtype),
        compiler_params=pltpu.CompilerParams(dimension_semantics=("parallel",), vmem_limit_bytes=VMEM_LIMIT),
    )(core, blocks, got)


def _chip_exchange(name, sums, ex):
    def body(*refs):
        s_ref, x_in = refs[0], refs[1:1 + ex.nb]
        slots_ref, x_out = refs[1 + ex.nb], refs[2 + ex.nb:2 + 2 * ex.nb]
        send_sems, recv_sems = refs[2 + 2 * ex.nb:4 + 2 * ex.nb]
        ex_sems = refs[4 + 2 * ex.nb:]
        x, y, c = lax.axis_index("x"), lax.axis_index("y"), lax.axis_index("c")
        my_q = 2 * x + y
        sends, recvs = [], []
        for kbits in range(1, N_CHIP):
            px = 1 - x if kbits & 2 else x
            py = 1 - y if kbits & 1 else y
            peer_q = 2 * px + py
            sends.append(pltpu.make_async_remote_copy(
                src_ref=s_ref.at[peer_q], dst_ref=slots_ref.at[my_q], send_sem=send_sems.at[kbits - 1],
                recv_sem=recv_sems.at[kbits - 1], device_id=(px, py, c), device_id_type=pl.DeviceIdType.MESH))
            recvs.append(pltpu.make_async_remote_copy(
                src_ref=s_ref.at[peer_q], dst_ref=slots_ref.at[peer_q], send_sem=send_sems.at[kbits - 1],
                recv_sem=recv_sems.at[kbits - 1], device_id=(px, py, c), device_id_type=pl.DeviceIdType.MESH))
        for cp in sends:
            cp.start()
        ex.start(x_in, x_out, ex_sems)
        for cp in recvs:
            cp.wait_recv()
        for cp in sends:
            cp.wait_send()
        ex.wait(x_in, x_out, ex_sems)

    any_spec = pl.BlockSpec(memory_space=pl.ANY)
    res = pl.pallas_call(
        body, name=name, in_specs=[any_spec] + ex.any_specs, out_specs=[any_spec] + ex.any_specs,
        out_shape=[jax.ShapeDtypeStruct(sums.shape, sums.dtype)] + ex.out_shape,
        scratch_shapes=[pltpu.SemaphoreType.DMA((N_CHIP - 1,)), pltpu.SemaphoreType.DMA((N_CHIP - 1,))]
        + ex.sem_shapes,
    )(sums, *ex.bufs)
    return res[0], list(res[1:])


def _adamw(name, slots, w, m, v, tr=256):
    R, Wd = w.shape[-2:]
    depth_axis = w.ndim == 3
    if R % tr == 0:
        tc = Wd
    else:
        tr, tc = R, (256 if (Wd % 256 == 0 and R > 256) else Wd)

    def body(s_ref, w_ref, m_ref, v_ref, g_out, d_out, m_out, v_out):
        g = s_ref[0].astype(F32)
        for j in range(1, slots.shape[0]):
            g = g + s_ref[j].astype(F32)
        m_new = ADAM_B1 * m_ref[...] + (1.0 - ADAM_B1) * g
        v_new = ADAM_B2 * v_ref[...] + (1.0 - ADAM_B2) * jnp.square(g)
        m_hat = m_new / (1.0 - ADAM_B1 ** ADAM_STEP)
        v_hat = v_new / (1.0 - ADAM_B2 ** ADAM_STEP)
        g_out[...] = g
        d_out[...] = -ADAM_LR * (m_hat / (jnp.sqrt(v_hat) + ADAM_EPS) + ADAM_WD * w_ref[...])
        m_out[...] = m_new
        v_out[...] = v_new

    if depth_axis:
        row = pl.BlockSpec((None, tr, tc), lambda i, j: (0, i, j))
    else:
        row = pl.BlockSpec((tr, tc), lambda i, j: (i, j))
    return pl.pallas_call(
        body, name=name, grid=(R // tr, Wd // tc),
        in_specs=[pl.BlockSpec((slots.shape[0], tr, tc), lambda i, j: (0, i, j)), row, row, row],
        out_specs=[row] * 4, out_shape=[jax.ShapeDtypeStruct(w.shape, F32)] * 4,
        compiler_params=pltpu.CompilerParams(dimension_semantics=("parallel", "parallel"),
                                             vmem_limit_bytes=VMEM_LIMIT),
    )(slots, w, m, v)


PACK_W = 1024
_SMALL_SIZES = [int(np.prod(s)) for _, s in REPLICATED]
_SMALL_ROWS = _round_up(_round_up(sum(_SMALL_SIZES) + PACK_W, PACK_W) // PACK_W, 8)
_LOSS_AT = sum(_SMALL_SIZES)
W_IN_SHARD = P_TOTAL // N_DEV


def _pack_rows(parts, rows, dtype):
    flat = jnp.concatenate([p.reshape(-1).astype(dtype) for p in parts])
    return jnp.pad(flat, (0, rows * PACK_W - flat.shape[0])).reshape(rows, PACK_W)


def _w_in_groups_t(blocks):
    wt = blocks.reshape(P_TOTAL, D)
    o, c = 2 * D, 2 * D + 3 * D
    z = lambda r: jnp.zeros((r, D), wt.dtype)
    rw = jnp.concatenate([wt[o:c], wt[c:c + L_W], z(128 - L_W), wt[c + L_W:c + L_W + L_A], z(128 - L_A),
                          wt[c + L_W + L_A:o + C_B], z(256 - L_G)], axis=0)
    return wt[:o], rw, wt[o + C_B:]


def _w_in_grad_blocks(g_sgu_t, g_rw_t, g_gate_t):
    c = 3 * D
    full = jnp.concatenate([g_sgu_t, g_rw_t[:c], g_rw_t[c:c + L_W], g_rw_t[c + 128:c + 128 + L_A],
                            g_rw_t[c + 256:c + 256 + L_G], g_gate_t], axis=0)
    return full.reshape(N_DEV, W_IN_SHARD, D)


def _mesh_index():
    me = 4 * lax.axis_index("x") + 2 * lax.axis_index("y") + lax.axis_index("c")
    return me.astype(jnp.int32).reshape(1)


def _fill_slot(name, dst, src, idx, src_idx=None):
    R, Wd = dst.shape[1:]
    scalars = [idx] if src_idx is None else [idx, src_idx]
    if src_idx is None:
        src_spec = pl.BlockSpec((R, Wd), lambda i, *s: (0, 0))
    else:
        src_spec = pl.BlockSpec((None, R, Wd), lambda i, *s: (s[1][0], 0, 0))

    def body(*refs):
        src_ref, out_ref = refs[len(scalars) + 1], refs[len(scalars) + 2]
        out_ref[...] = src_ref[...]

    return pl.pallas_call(
        body, name=name,
        grid_spec=pltpu.PrefetchScalarGridSpec(
            num_scalar_prefetch=len(scalars), grid=(1,),
            in_specs=[pl.BlockSpec(memory_space=pl.ANY), src_spec],
            out_specs=pl.BlockSpec((None, R, Wd), lambda i, *s: (s[0][0], 0, 0))),
        out_shape=jax.ShapeDtypeStruct(dst.shape, dst.dtype),
        input_output_aliases={len(scalars): 0},
        compiler_params=pltpu.CompilerParams(vmem_limit_bytes=VMEM_LIMIT),
    )(*scalars, dst, src)


def _all_gather_two_level(name, bufs, skip_own=()):
    nb = len(bufs)

    def body(*refs):
        in_refs, out_refs = refs[:nb], refs[nb:2 * nb]
        send_sems, recv_sems, local_sems = refs[2 * nb:]
        x, y, c = lax.axis_index("x"), lax.axis_index("y"), lax.axis_index("c")
        me, sibling = (x, y, c), (x, y, 1 - c)
        chips = [(1 - x, y), (x, 1 - y), (1 - x, 1 - y)]

        def slot(b, dev):
            return out_refs[b].at[4 * dev[0] + 2 * dev[1] + dev[2]]

        def copy(b, k, block, to, own=False):
            return pltpu.make_async_remote_copy(
                src_ref=in_refs[b] if own else slot(b, block), dst_ref=slot(b, block),
                send_sem=send_sems.at[7 * b + k], recv_sem=recv_sems.at[7 * b + k], device_id=to,
                device_id_type=pl.DeviceIdType.MESH)

        local = [pltpu.make_async_copy(in_refs[b], slot(b, me), local_sems.at[b]) for b in range(nb)
                 if b not in skip_own]
        first = [copy(b, 0, me, sibling, own=True) for b in range(nb)]
        first += [copy(b, 1 + j, me, (*chip, c), own=True) for j, chip in enumerate(chips) for b in range(nb)]
        for cp in first + local:
            cp.start()
        passed = []
        for j, chip in enumerate(chips):
            for b in range(nb):
                copy(b, 1 + j, (*chip, c), me).wait_recv()
                passed.append(copy(b, 4 + j, (*chip, c), sibling))
                passed[-1].start()
        for b in range(nb):
            copy(b, 0, sibling, me).wait_recv()
        for j, chip in enumerate(chips):
            for b in range(nb):
                copy(b, 4 + j, (*chip, 1 - c), me).wait_recv()
        for cp in first + passed:
            cp.wait_send()
        for cp in local:
            cp.wait()

    any_spec = pl.BlockSpec(memory_space=pl.ANY)
    return pl.pallas_call(
        body, name=name, in_specs=[any_spec] * nb, out_specs=[any_spec] * nb,
        out_shape=[jax.ShapeDtypeStruct((N_DEV,) + b.shape, b.dtype) for b in bufs],
        scratch_shapes=[pltpu.SemaphoreType.DMA((7 * nb,)), pltpu.SemaphoreType.DMA((7 * nb,)),
                        pltpu.SemaphoreType.DMA((nb,))],
    )(*bufs)


def _cols_from_blocks(blk):
    return jnp.transpose(blk, (1, 0, 2)).reshape(blk.shape[1], -1)


def _cols_to_blocks(g):
    r, c = g.shape
    return jnp.transpose(g.reshape(r, N_DEV, c // N_DEV), (1, 0, 2))


FIRST_WEIGHTS = ["w_in", "shift_b", "w_lora_w", "a_lora_w", "g_lora_w"]
LATE_WEIGHTS = ["w_proj_a", "w_proj_b", "w_out", "w_ffn1", "w_ffn2"]


def _late_weights(shards):
    ex = Exchange([shards[n].astype(BF16) for n in LATE_WEIGHTS], [True] * len(LATE_WEIGHTS))

    def finish(results):
        got = dict(zip(LATE_WEIGHTS, results))
        W = {n: got[n].reshape(-1, D) for n in ("w_proj_a", "w_proj_b", "w_out", "w_ffn2")}
        W["w_ffn1"] = got["w_ffn1"].reshape(N_DEV, D, -1)
        return W
    return ex, finish


def _gather_weights(shards):
    def payload(n):
        if n == "w_in":
            return jnp.transpose(shards[n][0]).astype(BF16)
        return shards[n] if n == "shift_b" else shards[n].astype(BF16)
    payloads = [payload(n) for n in FIRST_WEIGHTS]
    got = dict(zip(FIRST_WEIGHTS, _all_gather_two_level("weight_all_gather", payloads, skip_own=(0,))))
    got["w_in"] = _fill_slot("w_in_own_slot", got["w_in"], payloads[0], _mesh_index())
    W = {}
    W["w_sgu_t"], W["w_rw_t"], W["w_gate_t"] = _w_in_groups_t(got["w_in"])
    z = lambda r, c, dt: jnp.zeros((r, c), dt)
    W["w_lora"] = jnp.concatenate([_cols_from_blocks(got["w_lora_w"][:, 0]).astype(F32), z(128 - L_W, D, F32)], axis=0)
    W["a_lora"] = jnp.concatenate([_cols_from_blocks(got["a_lora_w"][:, 0]).astype(F32), z(128 - L_A, D, F32)], axis=0)
    W["g_lora"] = jnp.concatenate([_cols_from_blocks(got["g_lora_w"][:, 0]).astype(F32), z(256 - L_G, D, F32)], axis=0)
    sb = _cols_from_blocks(got["shift_b"][:, 0])
    W["sb"] = jnp.concatenate([sb[:, :3 * D], sb[:, 3 * D:3 * D + L_W], z(2, 128 - L_W, F32),
                               sb[:, 3 * D + L_W:3 * D + L_W + L_A], z(2, 128 - L_A, F32),
                               sb[:, 3 * D + L_W + L_A:], z(2, 256 - L_G, F32)], axis=1)
    return W


def _replicated_weights(rep):
    W = {n: rep[n] for n in ("g_mix", "sgu_ln_w", "sgu_ln_b", "w0", "a0", "k_k", "k_a", "r_k", "ln_x_w", "ln_x_b",
                             "g_ffn")}
    W["g_final"] = rep["g_final"].reshape(1, D)
    W["sgu_w"] = rep["sgu_w"][0]
    W["sgu_bt"] = jnp.transpose(rep["sgu_b"][0])
    return W


def _late_grad_blocks(G):
    blocks = {n: G[n].reshape(N_DEV, -1, D) for n in ("w_proj_a", "w_proj_b", "w_out", "w_ffn2")}
    blocks["w_ffn1"] = G["w_ffn1"]
    return Exchange([blocks[n] for n in LATE_WEIGHTS], [False] * len(LATE_WEIGHTS))


def _first_grad_blocks(G):
    sbg = G["sb"]
    c = 3 * D
    sb = jnp.concatenate([sbg[:, :c], sbg[:, c:c + L_W], sbg[:, c + 128:c + 128 + L_A],
                          sbg[:, c + 256:c + 256 + L_G]], axis=1)
    return {
        "w_in": _w_in_grad_blocks(G["w_sgu_t"], G["w_rw_t"], G["w_gate_t"]),
        "shift_b": _cols_to_blocks(sb),
        "w_lora_w": _cols_to_blocks(G["w_lora"][:L_W]), "a_lora_w": _cols_to_blocks(G["a_lora"][:L_A]),
        "g_lora_w": _cols_to_blocks(G["g_lora"][:L_G]),
    }


def _replicated_grads(G):
    small = {n: G[n] for n in ("g_mix", "sgu_ln_w", "sgu_ln_b", "w0", "a0", "k_k", "k_a", "r_k", "ln_x_w", "ln_x_b",
                               "g_ffn", "g_final")}
    small["sgu_w"] = G["sgu_w"]
    small["sgu_b"] = jnp.transpose(G["sgu_bt"])
    return small


def kernel(x, g_mix, w_in, sgu_ln_w, sgu_ln_b, sgu_w, sgu_b, w_proj_a, shift_b, w_lora_w, w0, a_lora_w, a0, g_lora_w, k_k, k_a, r_k, ln_x_w, ln_x_b, w_proj_b, w_out, g_ffn, w_ffn1, w_ffn2, g_final, loss_target, m_g_mix, m_w_in, m_sgu_ln_w, m_sgu_ln_b, m_sgu_w, m_sgu_b, m_w_proj_a, m_shift_b, m_w_lora_w, m_w0, m_a_lora_w, m_a0, m_g_lora_w, m_k_k, m_k_a, m_r_k, m_ln_x_w, m_ln_x_b, m_w_proj_b, m_w_out, m_g_ffn, m_w_ffn1, m_w_ffn2, m_g_final, v_g_mix, v_w_in, v_sgu_ln_w, v_sgu_ln_b, v_sgu_w, v_sgu_b, v_w_proj_a, v_shift_b, v_w_lora_w, v_w0, v_a_lora_w, v_a0, v_g_lora_w, v_k_k, v_k_a, v_r_k, v_ln_x_w, v_ln_x_b, v_w_proj_b, v_w_out, v_g_ffn, v_w_ffn1, v_w_ffn2, v_g_final):
    env = dict(locals())
    weights = {n: env[n] for n in WEIGHT_ORDER}
    moms = {n: env["m_" + n] for n in WEIGHT_ORDER}
    vars_ = {n: env["v_" + n] for n in WEIGHT_ORDER}

    shards = {n: weights[n] for n, _, _ in SHARDED}
    W = _gather_weights(shards)
    W.update(_replicated_weights({n: weights[n] for n, _ in REPLICATED}))
    loss_part, dx, G, late_slots = _local_step(x[0], loss_target[0], W, late_weights=_late_weights(shards),
                                               early_grads=_late_grad_blocks)

    slots = dict(zip(LATE_WEIGHTS, late_slots))
    blocks = _first_grad_blocks(G)
    small = _replicated_grads(G)
    small_parts = [small[n] for n, _ in REPLICATED] + [jnp.full((PACK_W,), loss_part, F32)]
    got = _pair_exchange("grad_pair_exchange", blocks["w_in"])
    core = lax.axis_index("c").astype(jnp.int32).reshape(1)
    rest = [n for n in FIRST_WEIGHTS if n != "w_in"]
    ex = Exchange([blocks[n] for n in rest] + [_pack_rows(small_parts, _SMALL_ROWS, F32)], [False] * len(rest) + [True])
    sums = _pair_sum("grad_pair_sum", blocks["w_in"], got, core)
    chip_slots, res = _chip_exchange("grad_exchange", sums, ex)
    my_chip = (2 * lax.axis_index("x") + lax.axis_index("y")).astype(jnp.int32).reshape(1)
    slots["w_in"] = _fill_slot("grad_own_slot", chip_slots, sums, my_chip, src_idx=my_chip)
    slots.update(zip(rest, res[:-1]))
    small_slots = res[-1]

    outs = [dict(), dict(), dict(), dict()]
    for n, _, _ in SHARDED:
        if n == "w_in":
            res = _adamw("adamw_" + n, slots[n], *[jnp.transpose(t[0]) for t in (weights[n], moms[n], vars_[n])])
            res = [jnp.transpose(t)[None] for t in res]
        else:
            res = _adamw("adamw_" + n, slots[n], weights[n], moms[n], vars_[n])
        for k in range(4):
            outs[k][n] = res[k]

    def packed(d):
        return _pack_rows([d[n] for n, _ in REPLICATED], _SMALL_ROWS, F32)
    small_out = _adamw("adamw_replicated", small_slots, packed(weights), packed(moms), packed(vars_))
    for k in range(4):
        flat = small_out[k].reshape(-1)
        off = 0
        for (n, s), size in zip(REPLICATED, _SMALL_SIZES):
            outs[k][n] = flat[off:off + size].reshape(s)
            off += size
    loss = small_out[0].reshape(-1)[_LOSS_AT]
    return (loss, dx[None], *[outs[0][n] for n in WEIGHT_ORDER], *[outs[1][n] for n in WEIGHT_ORDER],
            *[outs[2][n] for n in WEIGHT_ORDER], *[outs[3][n] for n in WEIGHT_ORDER])
```

```python
import functools
import numpy as np
import jax
import jax.numpy as jnp
from jax import lax
from jax.experimental import pallas as pl
from jax.experimental.pallas import tpu as pltpu

F32 = jnp.float32
BF16 = jnp.bfloat16

D = 1024
NH, HN = 16, 64
SGU_G, SGU_C = 8, 128
L_W, L_A, L_G = 64, 64, 160
C_B = 3 * D + L_W + L_A + L_G
P_TOTAL = 2 * D + C_B + 2 * D
D_FF = 4 * D
RW_INT = 3 * D + 128 + 128 + 256
NORM_EPS, LN_EPS, GN_EPS = 1e-6, 1e-5, 64e-5
N_DEV = 8
LANES = 128
SCAN_C = 64
SOLVE_B = 16
SCAN_PRECISION = lax.Precision.HIGH
GRAD_PAYLOAD = BF16
VMEM_LIMIT = 56 * 1024 * 1024
MATMUL_VMEM_BUDGET = 40 * 1024 * 1024

ADAM_LR, ADAM_B1, ADAM_B2, ADAM_EPS, ADAM_WD, ADAM_STEP = 0.001, 0.9, 0.999, 1e-08, 0.01, 10

SHARDED = [
    ("w_in", (D, P_TOTAL), 1), ("w_proj_a", (D, D), 0), ("shift_b", (2, C_B), 1), ("w_lora_w", (L_W, D), 1),
    ("a_lora_w", (L_A, D), 1), ("g_lora_w", (L_G, D), 1), ("w_proj_b", (D, D), 0), ("w_out", (D, D), 0),
    ("w_ffn1", (D, D_FF), 1), ("w_ffn2", (D_FF, D), 0),
]
REPLICATED = [
    ("g_mix", (1, D)), ("sgu_ln_w", (1, D)), ("sgu_ln_b", (1, D)), ("sgu_w", (1, SGU_G, SGU_C, SGU_C)),
    ("sgu_b", (1, SGU_G, SGU_C)), ("w0", (1, D)), ("a0", (1, D)), ("k_k", (1, D)), ("k_a", (1, D)), ("r_k", (1, D)),
    ("ln_x_w", (1, D)), ("ln_x_b", (1, D)), ("g_ffn", (1, D)), ("g_final", (D,)),
]
WEIGHT_ORDER = ["g_mix", "w_in", "sgu_ln_w", "sgu_ln_b", "sgu_w", "sgu_b", "w_proj_a", "shift_b", "w_lora_w", "w0",
                "a_lora_w", "a0", "g_lora_w", "k_k", "k_a", "r_k", "ln_x_w", "ln_x_b", "w_proj_b", "w_out", "g_ffn",
                "w_ffn1", "w_ffn2", "g_final"]


def _shard_shape(shape, axis):
    s = list(shape)
    s[axis] //= N_DEV
    return tuple(s)


def _round_up(n, m):
    return (n + m - 1) // m * m


def _pick(n, target):
    if n <= target:
        return n
    best = None
    for t in range(LANES, target + 1, LANES):
        if n % t == 0:
            best = t
    assert best is not None, (n, target)
    return best


def _matmul(name, a, b, mode, out_dtype=F32, tm=2048, tn=1024, tk=2048, out_blocks=None, epilogue=None, extras=(),
            out_dtypes=()):
    b_blocks = b.shape[0] if b.ndim == 3 else None
    bshape = b.shape if b.ndim == 2 else (b.shape[1], b.shape[0] * b.shape[2])
    if mode == "nn":
        (M, K), (K2, N) = a.shape, bshape
    elif mode == "nt":
        (M, K), (N, K2) = a.shape, bshape
    else:
        (K, M), (K2, N) = a.shape, bshape
    assert K == K2, (name, a.shape, b.shape)
    assert b_blocks is None or mode != "tn"
    assert out_blocks is None or mode == "tn"
    tn = min(tn, N // (out_blocks or 1), bshape[1] // b_blocks if (b_blocks and mode == "nn") else tn)
    tk = min(tk, bshape[1] // b_blocks if (b_blocks and mode == "nt") else tk)
    tm, tn, tk = _pick(M, tm), _pick(N, tn), _pick(K, tk)

    def vmem_bytes(tm, tk):
        tiles = tm * tk * a.dtype.itemsize + tk * tn * b.dtype.itemsize
        for dt in (out_dtypes if epilogue else (out_dtype,)):
            tiles += tm * tn * jnp.dtype(dt).itemsize
        for x in extras:
            arr = x[0] if isinstance(x, tuple) else x
            tiles += (tm if arr.shape[0] > 1 else 1) * tn * arr.dtype.itemsize
        return 2 * tiles + (tm * tn * 4 if K // tk > 1 else 0)

    while vmem_bytes(tm, tk) > MATMUL_VMEM_BUDGET and tk > 512:
        tk = _pick(K, tk // 2)
    while vmem_bytes(tm, tk) > MATMUL_VMEM_BUDGET and tm > 512:
        tm = _pick(M, tm // 2)
    nk = K // tk
    dims = {"nn": (((1,), (0,)), ((), ())), "nt": (((1,), (1,)), ((), ())), "tn": (((0,), (0,)), ((), ()))}[mode]

    n_x, n_o = len(extras), len(out_dtypes) if epilogue else 1

    def body(a_ref, b_ref, *rest):
        x_refs, o_refs, acc = rest[:n_x], rest[n_x:n_x + n_o], rest[n_x + n_o:]
        part = lax.dot_general(a_ref[...].astype(BF16), b_ref[...].astype(BF16), dims, preferred_element_type=F32)

        def finish(res):
            outs = epilogue(res, *[r[...] for r in x_refs]) if epilogue else (res,)
            for r, v in zip(o_refs, outs):
                r[...] = v.astype(r.dtype)

        if nk == 1:
            finish(part)
            return
        acc_ref, k = acc[0], pl.program_id(2)

        @pl.when(k == 0)
        def _():
            acc_ref[...] = part

        @pl.when(k > 0)
        def _():
            acc_ref[...] += part

        @pl.when(k == nk - 1)
        def _():
            finish(acc_ref[...])

    a_spec = {"nn": pl.BlockSpec((tm, tk), lambda i, j, k: (i, k)), "nt": pl.BlockSpec((tm, tk), lambda i, j, k: (i, k)),
              "tn": pl.BlockSpec((tk, tm), lambda i, j, k: (k, i))}[mode]
    b_spec = {"nn": pl.BlockSpec((tk, tn), lambda i, j, k: (k, j)), "nt": pl.BlockSpec((tn, tk), lambda i, j, k: (j, k)),
              "tn": pl.BlockSpec((tk, tn), lambda i, j, k: (k, j))}[mode]
    if b_blocks and mode == "nn":
        per = b.shape[2] // tn
        b_spec = pl.BlockSpec((None, tk, tn), lambda i, j, k: (j // per, k, j % per))
    elif b_blocks:
        per = b.shape[2] // tk
        b_spec = pl.BlockSpec((None, tn, tk), lambda i, j, k: (k // per, j, k % per))
    out_spec = pl.BlockSpec((tm, tn), lambda i, j, k: (i, j))
    out_shape = jax.ShapeDtypeStruct((M, N), out_dtype)
    if out_blocks:
        per_o = N // out_blocks // tn
        out_spec = pl.BlockSpec((None, tm, tn), lambda i, j, k: (j // per_o, i, j % per_o))
        out_shape = jax.ShapeDtypeStruct((out_blocks, M, N // out_blocks), out_dtype)
    x_specs, x_args = [], []
    for x in extras:
        arr, off = x if isinstance(x, tuple) else (x, 0)
        if arr.shape[0] == 1:
            x_specs.append(pl.BlockSpec((1, tn), lambda i, j, k: (0, j)))
        else:
            x_specs.append(pl.BlockSpec((tm, tn), lambda i, j, k, off=off: (i, j + off)))
        x_args.append(arr)
    res = pl.pallas_call(
        body, name=name, grid=(M // tm, N // tn, nk), in_specs=[a_spec, b_spec] + x_specs,
        out_specs=[out_spec] * n_o if epilogue else out_spec,
        out_shape=[jax.ShapeDtypeStruct((M, N), dt) for dt in out_dtypes] if epilogue else out_shape,
        scratch_shapes=[pltpu.VMEM((tm, tn), F32)] if nk > 1 else [],
        compiler_params=pltpu.CompilerParams(dimension_semantics=("parallel", "parallel", "arbitrary"),
                                             vmem_limit_bytes=VMEM_LIMIT),
    )(a, b, *x_args)
    return res


class Rows:
    def __init__(self, arr, width=None, cb=0):
        self.arr, self.width, self.cb = arr, (arr.shape[1] if width is None else width), cb


class Heads:
    def __init__(self, arr):
        self.arr = arr


class Halo:
    def __init__(self, arr, side):
        self.arr, self.side = arr, side


def _rows_call(name, fn, ins, consts, outs, accs=(), tm=256, with_pid=False):
    T = next(o.arr.shape[1] if isinstance(o, Heads) else o.arr.shape[0] for o in ins if not isinstance(o, Halo))
    tm = min(tm, T)
    n_tiles = T // tm
    n_in, n_c, n_out = len(ins), len(consts), len(outs)
    in_specs, args = [], []
    for o in ins:
        if isinstance(o, Rows):
            in_specs.append(pl.BlockSpec((tm, o.width), lambda i, cb=o.cb: (i, cb)))
        elif isinstance(o, Heads):
            in_specs.append(pl.BlockSpec((NH, tm, HN), lambda i: (0, i, 0)))
        else:
            w = o.arr.shape[1]
            if o.side < 0:
                in_specs.append(pl.BlockSpec((8, w), lambda i: (jnp.maximum(i * (tm // 8) - 1, 0), 0)))
            else:
                in_specs.append(pl.BlockSpec((8, w), lambda i: (jnp.minimum((i + 1) * (tm // 8), T // 8 - 1), 0)))
        args.append(o.arr)
    for c in consts:
        in_specs.append(pl.BlockSpec(c.shape, lambda i, nd=c.ndim: (0,) * nd))
        args.append(c)
    out_specs, out_shape = [], []
    for o in outs:
        if o[0] == "rows":
            out_specs.append(pl.BlockSpec((tm, o[1]), lambda i: (i, 0)))
            out_shape.append(jax.ShapeDtypeStruct((T, o[1]), o[2]))
        else:
            out_specs.append(pl.BlockSpec((NH, tm, HN), lambda i: (0, i, 0)))
            out_shape.append(jax.ShapeDtypeStruct((NH, T, HN), o[1]))
    for shape, dt in accs:
        out_specs.append(pl.BlockSpec(shape, lambda i, nd=len(shape): (0,) * nd))
        out_shape.append(jax.ShapeDtypeStruct(shape, dt))

    def body(*refs):
        i = pl.program_id(0)
        vals = []
        vals = [r[...] for r in refs[:n_in + n_c]]
        res = fn(i, n_tiles, *vals) if with_pid else fn(*vals)
        out_refs = refs[n_in + n_c:]
        for r, v in zip(out_refs[:n_out], res[:n_out]):
            r[...] = v.astype(r.dtype)
        if accs:
            @pl.when(i == 0)
            def _():
                for r in out_refs[n_out:]:
                    r[...] = jnp.zeros_like(r)

            for r, v in zip(out_refs[n_out:], res[n_out:]):
                r[...] += v.astype(r.dtype)

    res = pl.pallas_call(
        body, name=name, grid=(n_tiles,), in_specs=in_specs, out_specs=out_specs, out_shape=out_shape,
        compiler_params=pltpu.CompilerParams(dimension_semantics=("arbitrary",), vmem_limit_bytes=VMEM_LIMIT),
    )(*args)
    return res


def _rms(x, g):
    return x * lax.rsqrt(jnp.mean(x * x, axis=-1, keepdims=True) + NORM_EPS) * g


def _gelu(x):
    return 0.5 * x * (1.0 + lax.erf(x * 0.7071067811865476))


def _sigmoid(x):
    return 1.0 / (1.0 + jnp.exp(-x))


def _bdot(a, b):
    return jnp.dot(a.astype(BF16), b.astype(BF16), preferred_element_type=F32)


def _to_heads(x):
    return jnp.concatenate([x[:, h * HN:(h + 1) * HN][None] for h in range(NH)], axis=0)


def _from_heads(xh):
    return jnp.concatenate([xh[h] for h in range(NH)], axis=-1)


def _sgu_fn(p, ln_w, ln_b, sw, sbt):
    z = _gelu(p)
    u, v = z[:, :D], z[:, D:]
    mu = jnp.mean(v, axis=-1, keepdims=True)
    var = jnp.mean(jnp.square(v - mu), axis=-1, keepdims=True)
    vn = (v - mu) * lax.rsqrt(var + LN_EPS) * ln_w + ln_b
    ri = lax.broadcasted_iota(jnp.int32, (SGU_C, SGU_C), 0)
    ci = lax.broadcasted_iota(jnp.int32, (SGU_C, SGU_C), 1)
    mask = (ci <= ri).astype(F32)
    dg = D // SGU_G
    parts = []
    for g in range(SGU_G):
        parts.append(_bdot(sw[g] * mask, vn[:, g * dg:(g + 1) * dg]) + sbt[:, g:g + 1])
    return u * jnp.concatenate(parts, axis=-1)


def _pre_fn(qr, qk, qv, qxw, qxa, qxg, wl, w0, al, a0, gl, k_k, k_a):
    w = -jax.nn.softplus(-(w0 + _bdot(jnp.tanh(qxw), wl))) - 0.5
    lw = -jnp.exp(w)
    aa = _sigmoid(a0 + _bdot(qxa, al))
    g = _bdot(_sigmoid(qxg), gl)
    kk = _to_heads(qk * k_k)
    kk = kk / jnp.maximum(jnp.sqrt(jnp.sum(kk * kk, axis=-1, keepdims=True)), 1e-12)
    k2 = qk * (1.0 + (aa - 1.0) * k_a)
    return _to_heads(qr), _to_heads(lw), _to_heads(k2), _to_heads(qv), kk, _to_heads(aa), g


def _post_fn(o, r, k2, v, g, ln_w, ln_b, r_k):
    mu = jnp.mean(o, axis=-1, keepdims=True)
    d = o - mu
    var = jnp.mean(d * d, axis=-1, keepdims=True)
    on = d * lax.rsqrt(var + GN_EPS) * ln_w + ln_b
    bonus = jnp.sum(r * k2 * r_k, axis=-1, keepdims=True) * v
    return _from_heads(on + bonus) * g


def _gate_fn(pg, ya, yb):
    return _sigmoid(pg[:, :D]) * ya + _sigmoid(pg[:, D:]) * yb


def _bmm(x, y, cx, cy):
    return lax.dot_general(x, y, (((cx,), (cy,)), ((0,), (0,))), precision=SCAN_PRECISION,
                           preferred_element_type=F32)


def _unit_lower_inverse(M):
    C = M.shape[1]
    ti = lax.broadcasted_iota(jnp.int32, (C, C), 0)
    tj = lax.broadcasted_iota(jnp.int32, (C, C), 1)
    eye = (ti == tj).astype(F32)
    same = lambda b: (ti // b == tj // b).astype(F32)
    X = -(M * same(SOLVE_B))
    inv = eye + X
    span = 1
    while 2 * span < SOLVE_B:
        X = _bmm(X, X, 2, 1)
        inv = inv + _bmm(inv, X, 2, 1)
        span *= 2
    b = SOLVE_B
    while b < C:
        low = M * (same(2 * b) - same(b))
        inv = inv - _bmm(_bmm(inv, low, 2, 1), inv, 2, 1)
        b *= 2
    return inv


@jax.custom_vjp
def _unit_lower_solve(M, y):
    return _bmm(_unit_lower_inverse(M), y, 2, 1)


def _unit_lower_solve_fwd(M, y):
    inv = _unit_lower_inverse(M)
    u = _bmm(inv, y, 2, 1)
    return u, (inv, u)


def _unit_lower_solve_bwd(res, du):
    inv, u = res
    dy = _bmm(inv, du, 1, 1)
    return -_bmm(dy, u, 2, 2), dy


_unit_lower_solve.defvjp(_unit_lower_solve_fwd, _unit_lower_solve_bwd)


def _chunk_fn(S0, r, lw, k, v, kk, a):
    C = SCAN_C
    bmm = _bmm
    ti = lax.broadcasted_iota(jnp.int32, (C, C), 0)
    tj = lax.broadcasted_iota(jnp.int32, (C, C), 1)
    incl = (tj <= ti).astype(F32)
    strict = (tj < ti).astype(F32)
    cum = lax.dot_general(jnp.broadcast_to(incl, (NH, C, C)), lw, (((2,), (1,)), ((0,), (0,))),
                          precision=lax.Precision.HIGHEST, preferred_element_type=F32)
    g_in, g_ex, g_inv = jnp.exp(cum), jnp.exp(cum - lw), jnp.exp(-cum)
    kkt, rt = kk * g_ex, r * g_in
    bk = jnp.concatenate([kk * a * g_inv, k * g_inv], axis=1)
    A = bmm(kkt, bk, 2, 2)
    M = A[:, :, :C] * strict
    n_mask = jnp.concatenate([jnp.zeros((C, C), F32), strict], axis=1)
    zv = jnp.concatenate([jnp.zeros_like(v), v], axis=1)
    s0_side = bmm(jnp.concatenate([kkt, rt], axis=1), S0, 2, 2)
    y = _unit_lower_solve(M, s0_side[:, :C] + bmm(A * n_mask, zv, 2, 1))
    z = jnp.concatenate([-y, v], axis=1)
    O = s0_side[:, C:] + bmm(bmm(rt, bk, 2, 2) * jnp.concatenate([incl, incl], axis=1), z, 2, 1)
    S1 = (S0 + bmm(z, bk, 1, 1)) * g_in[:, C - 1:C, :]
    return O, S1


def _scan_fwd(r, lw, k, v, kk, a, ex=None, tb=256):
    T = r.shape[1]
    tb = min(tb, T)
    n_chunks = tb // SCAN_C
    nb = T // tb
    nx = ex.nb if ex else 0

    def body(*refs):
        r_ref, lw_ref, k_ref, v_ref, kk_ref, a_ref = refs[:6]
        x_in, (o_ref, s0_ref), x_out = refs[6:6 + nx], refs[6 + nx:8 + nx], refs[8 + nx:8 + 2 * nx]
        s_ref, sems = refs[8 + 2 * nx], refs[9 + 2 * nx:]

        @pl.when(pl.program_id(0) == 0)
        def _():
            s_ref[...] = jnp.zeros_like(s_ref)
            if ex:
                ex.start(x_in, x_out, sems)

        def step(c, carry):
            sl = pl.ds(pl.multiple_of(c * SCAN_C, SCAN_C), SCAN_C)
            S0 = s_ref[...]
            s0_ref[c] = S0
            O, S1 = _chunk_fn(S0, r_ref[:, sl, :], lw_ref[:, sl, :], k_ref[:, sl, :], v_ref[:, sl, :],
                              kk_ref[:, sl, :], a_ref[:, sl, :])
            o_ref[:, sl, :] = O
            s_ref[...] = S1
            return carry

        lax.fori_loop(0, n_chunks, step, 0)

        if ex:
            @pl.when(pl.program_id(0) == nb - 1)
            def _():
                ex.wait(x_in, x_out, sems)

    hm = pl.BlockSpec((NH, tb, HN), lambda i: (0, i, 0))
    res = pl.pallas_call(
        body, name="rwkv_scan_fwd", grid=(nb,), in_specs=[hm] * 6 + (ex.any_specs if ex else []),
        out_specs=[hm, pl.BlockSpec((n_chunks, NH, HN, HN), lambda i: (i, 0, 0, 0))] + (ex.any_specs if ex else []),
        out_shape=[jax.ShapeDtypeStruct((NH, T, HN), F32), jax.ShapeDtypeStruct((T // SCAN_C, NH, HN, HN), F32)]
        + (ex.out_shape if ex else []),
        scratch_shapes=[pltpu.VMEM((NH, HN, HN), F32)] + (ex.sem_shapes if ex else []),
        compiler_params=pltpu.CompilerParams(dimension_semantics=("arbitrary",), vmem_limit_bytes=VMEM_LIMIT),
    )(r, lw, k, v, kk, a, *(ex.bufs if ex else []))
    return res[0], res[1], list(res[2:])


def _scan_bwd(r, lw, k, v, kk, a, s0s, do, ex=None, tb=128):
    T = r.shape[1]
    tb = min(tb, T)
    n_chunks = tb // SCAN_C
    nb = T // tb
    nx = ex.nb if ex else 0

    def body(*refs):
        r_ref, lw_ref, k_ref, v_ref, kk_ref, a_ref, s0_ref, do_ref = refs[:8]
        x_in, (dr, dlw, dk, dv, dkk, da), x_out = refs[8:8 + nx], refs[8 + nx:14 + nx], refs[14 + nx:14 + 2 * nx]
        ds_ref, sems = refs[14 + 2 * nx], refs[15 + 2 * nx:]

        @pl.when(pl.program_id(0) == 0)
        def _():
            ds_ref[...] = jnp.zeros_like(ds_ref)
            if ex:
                ex.start(x_in, x_out, sems)

        def step(j, carry):
            c = n_chunks - 1 - j
            sl = pl.ds(pl.multiple_of(c * SCAN_C, SCAN_C), SCAN_C)
            _, vjp = jax.vjp(_chunk_fn, s0_ref[c], r_ref[:, sl, :], lw_ref[:, sl, :], k_ref[:, sl, :],
                             v_ref[:, sl, :], kk_ref[:, sl, :], a_ref[:, sl, :])
            g = vjp((do_ref[:, sl, :], ds_ref[...]))
            ds_ref[...] = g[0]
            for ref, val in zip((dr, dlw, dk, dv, dkk, da), g[1:]):
                ref[:, sl, :] = val
            return carry

        lax.fori_loop(0, n_chunks, step, 0)

        if ex:
            @pl.when(pl.program_id(0) == nb - 1)
            def _():
                ex.wait(x_in, x_out, sems)

    hm = pl.BlockSpec((NH, tb, HN), lambda i: (0, nb - 1 - i, 0))
    res = pl.pallas_call(
        body, name="rwkv_scan_bwd", grid=(nb,),
        in_specs=[hm] * 6 + [pl.BlockSpec((n_chunks, NH, HN, HN), lambda i: (nb - 1 - i, 0, 0, 0)), hm]
        + (ex.any_specs if ex else []),
        out_specs=[hm] * 6 + (ex.any_specs if ex else []),
        out_shape=[jax.ShapeDtypeStruct((NH, T, HN), F32)] * 6 + (ex.out_shape if ex else []),
        scratch_shapes=[pltpu.VMEM((NH, HN, HN), F32)] + (ex.sem_shapes if ex else []),
        compiler_params=pltpu.CompilerParams(dimension_semantics=("arbitrary",), vmem_limit_bytes=VMEM_LIMIT),
    )(r, lw, k, v, kk, a, s0s, do, *(ex.bufs if ex else []))
    return list(res[:6]), list(res[6:])


def _shift_down(i, p, prev8):
    first = jnp.where(i > 0, prev8[7:8, :], 0.0)
    row = lax.broadcasted_iota(jnp.int32, p.shape, 0)
    return jnp.where(row == 0, first, pltpu.roll(p, 1, axis=0))


def _mix_fwd(p, sb, tm=256):
    def fn(i, n, p, prev8, sb):
        return (p * sb[0:1] + _shift_down(i, p, prev8) * sb[1:2],)
    return _rows_call("shift_mix_fwd", fn, [Rows(p), Halo(p, -1)], [sb], [("rows", p.shape[1], F32)], tm=tm,
                      with_pid=True)[0]


def _mix_bwd(dq, p, sb, tm=256):
    def fn(i, n, dq, next8, p, prev8, sb):
        ps = _shift_down(i, p, prev8)
        d1 = dq * sb[1:2]
        last = jnp.where(i < n - 1, next8[0:1, :] * sb[1:2], 0.0)
        row = lax.broadcasted_iota(jnp.int32, dq.shape, 0)
        up = jnp.where(row == dq.shape[0] - 1, last, pltpu.roll(d1, dq.shape[0] - 1, axis=0))
        return (dq * sb[0:1] + up, jnp.sum(dq * p, axis=0, keepdims=True), jnp.sum(dq * ps, axis=0, keepdims=True))
    w = p.shape[1]
    return _rows_call("shift_mix_bwd", fn, [Rows(dq), Halo(dq, +1), Rows(p), Halo(p, -1)], [sb], [("rows", w, F32)],
                      accs=[((1, w), F32), ((1, w), F32)], tm=tm, with_pid=True)


def _local_step(x, target, W, late_weights=None, early_grads=None):
    G = {}
    a = _rows_call("norm_mix_fwd", lambda x, g: (_rms(x, g),), [Rows(x)], [W["g_mix"]], [("rows", D, BF16)])[0]
    p_sgu = _matmul("proj_sgu", a, W["w_sgu_t"], "nt")
    p_rw = _matmul("proj_rwkv", a, W["w_rw_t"], "nt")
    p_gate = _matmul("proj_gate", a, W["w_gate_t"], "nt")

    sgu_consts = [W["sgu_ln_w"], W["sgu_ln_b"], W["sgu_w"], W["sgu_bt"]]
    s = _rows_call("sgu_fwd", lambda *t: (_sgu_fn(*t),), [Rows(p_sgu)], sgu_consts, [("rows", D, BF16)], tm=SGU_C)[0]

    q = _mix_fwd(p_rw, W["sb"])
    q_ins = [Rows(q, D, 0), Rows(q, D, 1), Rows(q, D, 2), Rows(q, 128, 24), Rows(q, 128, 25), Rows(q, 256, 13)]
    pre_consts = [W["w_lora"], W["w0"], W["a_lora"], W["a0"], W["g_lora"], W["k_k"], W["k_a"]]
    r_h, lw_h, k_h, v_h, kk_h, a_h, g_gate = _rows_call(
        "rwkv_pre_fwd", _pre_fn, q_ins, pre_consts, [("heads", F32)] * 6 + [("rows", D, F32)], tm=128)
    o_h, s0s, got = _scan_fwd(r_h, lw_h, k_h, v_h, kk_h, a_h, ex=late_weights[0] if late_weights else None)
    if late_weights:
        W = {**W, **late_weights[1](got)}
    y_a = _matmul("proj_a", s, W["w_proj_a"], "nn")
    post_ins = [Heads(o_h), Heads(r_h), Heads(k_h), Heads(v_h), Rows(g_gate)]
    post_consts = [W[n].reshape(NH, 1, HN) for n in ("ln_x_w", "ln_x_b", "r_k")]
    z_b = _rows_call("rwkv_post_fwd", lambda *t: (_post_fn(*t),), post_ins, post_consts, [("rows", D, BF16)], tm=128)[0]
    y_b, mixed = _matmul("proj_b", z_b, W["w_proj_b"], "nn", extras=[(p_gate, 0), (p_gate, 1), y_a],
                         epilogue=lambda yb, ga, gb, ya: (yb, _sigmoid(ga) * ya + _sigmoid(gb) * yb),
                         out_dtypes=(F32, BF16))
    gate_ins = [Rows(p_gate), Rows(y_a), Rows(y_b)]

    def res1(mo, x, g):
        h1 = x + mo
        return h1, _rms(h1, g)
    h1, f = _matmul("proj_out", mixed, W["w_out"], "nn", extras=[x, W["g_ffn"]], epilogue=res1,
                    out_dtypes=(F32, BF16))

    def relu_sq(u):
        r = jnp.maximum(u, 0.0)
        return r, r * r
    r1, act = _matmul("ffn_up", f, W["w_ffn1"], "nn", epilogue=relu_sq, out_dtypes=(BF16, BF16))
    ff = _matmul("ffn_down", act, W["w_ffn2"], "nn")

    def head(h1, ff, tgt, g):
        def f_(h1, ff, g):
            y = _rms(h1 + ff, g)
            return 0.5 * jnp.sum(jnp.mean(jnp.square(y - tgt), axis=-1))
        loss, (dh2, _, dg) = jax.value_and_grad(f_, argnums=(0, 1, 2))(h1, ff, g)
        return dh2, jnp.full((8, LANES), loss, F32), dg
    dh2, loss_acc, G["g_final"] = _rows_call("loss_head", head, [Rows(h1), Rows(ff), Rows(target)], [W["g_final"]],
                                             [("rows", D, F32)], accs=[((8, LANES), F32), ((1, D), F32)])

    d_u1 = _matmul("ffn_down_dx", dh2, W["w_ffn2"], "nt", extras=[r1], out_dtypes=(BF16,),
                   epilogue=lambda d_act, r: (d_act * 2.0 * r.astype(F32),))[0]
    G["w_ffn2"] = _matmul("ffn_down_dw", act, dh2, "tn", out_dtype=GRAD_PAYLOAD)
    d_f = _matmul("ffn_up_dx", d_u1, W["w_ffn1"], "nt")
    G["w_ffn1"] = _matmul("ffn_up_dw", f, d_u1, "tn", out_blocks=N_DEV, out_dtype=GRAD_PAYLOAD)

    def res1_bwd(h1, d_f, dh2, g):
        _, vjp = jax.vjp(_rms, h1, g)
        dh, dg = vjp(d_f)
        return dh2 + dh, dg
    dh1, G["g_ffn"] = _rows_call("residual_norm_bwd", res1_bwd, [Rows(h1), Rows(d_f), Rows(dh2)],
                                 [W["g_ffn"]], [("rows", D, F32)], accs=[((1, D), F32)])
    d_mixed = _matmul("proj_out_dx", dh1, W["w_out"], "nt")
    G["w_out"] = _matmul("proj_out_dw", mixed, dh1, "tn", out_dtype=GRAD_PAYLOAD)

    def gate_bwd(pg, ya, yb, dm):
        _, vjp = jax.vjp(_gate_fn, pg, ya, yb)
        return vjp(dm)
    d_gate, d_ya, d_yb = _rows_call("gate_bwd", gate_bwd, gate_ins + [Rows(d_mixed)], [],
                                    [("rows", 2 * D, F32), ("rows", D, BF16), ("rows", D, BF16)])

    d_s = _matmul("proj_a_dx", d_ya, W["w_proj_a"], "nt")
    G["w_proj_a"] = _matmul("proj_a_dw", s, d_ya, "tn", out_dtype=GRAD_PAYLOAD)

    def sgu_bwd(p, ds, *c):
        _, vjp = jax.vjp(_sgu_fn, p, *c)
        return vjp(ds)
    d_p_sgu, G["sgu_ln_w"], G["sgu_ln_b"], G["sgu_w"], G["sgu_bt"] = _rows_call(
        "sgu_bwd", sgu_bwd, [Rows(p_sgu), Rows(d_s)], sgu_consts, [("rows", 2 * D, F32)],
        accs=[((1, D), F32), ((1, D), F32), ((SGU_G, SGU_C, SGU_C), F32), ((SGU_C, SGU_G), F32)], tm=SGU_C)

    d_zb = _matmul("proj_b_dx", d_yb, W["w_proj_b"], "nt")
    G["w_proj_b"] = _matmul("proj_b_dw", z_b, d_yb, "tn", out_dtype=GRAD_PAYLOAD)

    def post_bwd(o, r, k2, v, g, dz, *c):
        _, vjp = jax.vjp(_post_fn, o, r, k2, v, g, *c)
        return vjp(dz)
    do_h, dr1, dk1, dv1, d_g, g_lnw, g_lnb, g_rk = _rows_call(
        "rwkv_post_bwd", post_bwd, post_ins + [Rows(d_zb)], post_consts, [("heads", F32)] * 4 + [("rows", D, F32)],
        accs=[((NH, 1, HN), F32)] * 3, tm=128)
    G["ln_x_w"], G["ln_x_b"], G["r_k"] = (t.reshape(1, D) for t in (g_lnw, g_lnb, g_rk))
    (dr2, dlw, dk2, dv2, dkk, daa), early = _scan_bwd(r_h, lw_h, k_h, v_h, kk_h, a_h, s0s, do_h,
                                                      ex=early_grads(G) if early_grads else None)

    def pre_bwd(qr, qk, qv, qxw, qxa, qxg, dr1, dr2, dlw, dk1, dk2, dv1, dv2, dkk, daa, dg, *c):
        _, vjp = jax.vjp(_pre_fn, qr, qk, qv, qxw, qxa, qxg, *c)
        g = vjp((dr1 + dr2, dlw, dk1 + dk2, dv1 + dv2, dkk, daa, dg))
        dq = jnp.concatenate(g[:6], axis=-1)
        return (dq,) + tuple(g[6:])
    pre_b_ins = q_ins + [Heads(dr1), Heads(dr2), Heads(dlw), Heads(dk1), Heads(dk2), Heads(dv1), Heads(dv2),
                         Heads(dkk), Heads(daa), Rows(d_g)]
    d_q, G["w_lora"], G["w0"], G["a_lora"], G["a0"], G["g_lora"], G["k_k"], G["k_a"] = _rows_call(
        "rwkv_pre_bwd", pre_bwd, pre_b_ins, pre_consts, [("rows", RW_INT, F32)],
        accs=[((128, D), F32), ((1, D), F32), ((128, D), F32), ((1, D), F32), ((256, D), F32), ((1, D), F32),
              ((1, D), F32)], tm=128)
    d_p_rw, dsb0, dsb1 = _mix_bwd(d_q, p_rw, W["sb"])
    G["sb"] = jnp.concatenate([dsb0, dsb1], axis=0)

    G["w_sgu_t"] = _matmul("proj_sgu_dw", d_p_sgu, a, "tn", out_dtype=GRAD_PAYLOAD)
    G["w_rw_t"] = _matmul("proj_rwkv_dw", d_p_rw, a, "tn", out_dtype=GRAD_PAYLOAD)
    G["w_gate_t"] = _matmul("proj_gate_dw", d_gate, a, "tn", out_dtype=GRAD_PAYLOAD)
    da1 = _matmul("proj_sgu_dx", d_p_sgu, W["w_sgu_t"], "nn")
    da2 = _matmul("proj_rwkv_dx", d_p_rw, W["w_rw_t"], "nn")
    da3 = _matmul("proj_gate_dx", d_gate, W["w_gate_t"], "nn")

    def norm1_bwd(x, da1, da2, da3, dh1, g):
        _, vjp = jax.vjp(_rms, x, g)
        dx, dg = vjp(da1 + da2 + da3)
        return dh1 + dx, dg
    dx, G["g_mix"] = _rows_call("norm_mix_bwd", norm1_bwd, [Rows(x), Rows(da1), Rows(da2), Rows(da3), Rows(dh1)],
                                [W["g_mix"]], [("rows", D, F32)], accs=[((1, D), F32)])
    return loss_acc[0, 0], dx, G, early


class Exchange:
    def __init__(self, bufs, gathers):
        self.bufs, self.gathers, self.nb = list(bufs), list(gathers), len(bufs)
        self.any_specs = [pl.BlockSpec(memory_space=pl.ANY)] * self.nb
        self.out_shape = [jax.ShapeDtypeStruct((N_DEV,) + (b.shape if g else b.shape[1:]), b.dtype)
                          for b, g in zip(self.bufs, self.gathers)]
        n = (N_DEV - 1) * self.nb
        self.sem_shapes = [pltpu.SemaphoreType.DMA((n,)), pltpu.SemaphoreType.DMA((n,)),
                           pltpu.SemaphoreType.DMA((self.nb,))]

    def _copies(self, in_refs, out_refs, sems):
        send_sems, recv_sems, local_sems = sems
        x, y, c = lax.axis_index("x"), lax.axis_index("y"), lax.axis_index("c")
        me = 4 * x + 2 * y + c

        def src(b, dest):
            return in_refs[b] if self.gathers[b] else in_refs[b].at[dest]

        local = [pltpu.make_async_copy(src(b, me), out_refs[b].at[me], local_sems.at[b]) for b in range(self.nb)]
        sends, recvs = [], []
        for kbits in range(1, N_DEV):
            px = 1 - x if kbits & 4 else x
            py = 1 - y if kbits & 2 else y
            pc = 1 - c if kbits & 1 else c
            peer = 4 * px + 2 * py + pc
            for b in range(self.nb):
                s = (kbits - 1) * self.nb + b
                sends.append(pltpu.make_async_remote_copy(
                    src_ref=src(b, peer), dst_ref=out_refs[b].at[me], send_sem=send_sems.at[s],
                    recv_sem=recv_sems.at[s], device_id=(px, py, pc), device_id_type=pl.DeviceIdType.MESH))
                recvs.append(pltpu.make_async_remote_copy(
                    src_ref=src(b, peer), dst_ref=out_refs[b].at[peer], send_sem=send_sems.at[s],
                    recv_sem=recv_sems.at[s], device_id=(px, py, pc), device_id_type=pl.DeviceIdType.MESH))
        return local, sends, recvs

    def start(self, in_refs, out_refs, sems):
        local, sends, _ = self._copies(in_refs, out_refs, sems)
        for cp in sends + local:
            cp.start()

    def wait(self, in_refs, out_refs, sems):
        local, sends, recvs = self._copies(in_refs, out_refs, sems)
        for cp in recvs:
            cp.wait_recv()
        for cp in sends:
            cp.wait_send()
        for cp in local:
            cp.wait()


def _exchange(name, bufs, gather):
    ex = Exchange(bufs, gather if isinstance(gather, (list, tuple)) else [gather] * len(bufs))

    def body(*refs):
        in_refs, out_refs, sems = refs[:ex.nb], refs[ex.nb:2 * ex.nb], refs[2 * ex.nb:]
        ex.start(in_refs, out_refs, sems)
        ex.wait(in_refs, out_refs, sems)

    return pl.pallas_call(body, name=name, in_specs=ex.any_specs, out_specs=ex.any_specs, out_shape=ex.out_shape,
                          scratch_shapes=ex.sem_shapes)(*ex.bufs)


N_CHIP = 4


def _pair_exchange(name, blocks):
    def body(b_ref, got_ref, send_sems, recv_sems):
        x, y, c = lax.axis_index("x"), lax.axis_index("y"), lax.axis_index("c")
        copies = [pltpu.make_async_remote_copy(
            src_ref=b_ref.at[2 * q + 1 - c], dst_ref=got_ref.at[q], send_sem=send_sems.at[q],
            recv_sem=recv_sems.at[q], device_id=(x, y, 1 - c), device_id_type=pl.DeviceIdType.MESH)
            for q in range(N_CHIP)]
        for cp in copies:
            cp.start()
        for cp in copies:
            cp.wait_recv()
        for cp in copies:
            cp.wait_send()

    any_spec = pl.BlockSpec(memory_space=pl.ANY)
    return pl.pallas_call(
        body, name=name, in_specs=[any_spec], out_specs=any_spec,
        out_shape=jax.ShapeDtypeStruct((N_CHIP,) + blocks.shape[1:], blocks.dtype),
        scratch_shapes=[pltpu.SemaphoreType.DMA((N_CHIP,))] * 2,
    )(blocks)


def _pair_sum(name, blocks, got, core):
    _, R, Wd = blocks.shape

    def body(c_ref, a_ref, b_ref, o_ref):
        o_ref[...] = (a_ref[...].astype(F32) + b_ref[...].astype(F32)).astype(o_ref.dtype)

    return pl.pallas_call(
        body, name=name,
        grid_spec=pltpu.PrefetchScalarGridSpec(
            num_scalar_prefetch=1, grid=(N_CHIP,),
            in_specs=[pl.BlockSpec((None, R, Wd), lambda q, c_ref: (2 * q + c_ref[0], 0, 0)),
                      pl.BlockSpec((None, R, Wd), lambda q, c_ref: (q, 0, 0))],
            out_specs=pl.BlockSpec((None, R, Wd), lambda q, c_ref: (q, 0, 0))),
        out_shape=jax.ShapeDtypeStruct(got.shape, blocks.dtype),
        compiler_params=pltpu.CompilerParams(dimension_semantics=("parallel",), vmem_limit_bytes=VMEM_LIMIT),
    )(core, blocks, got)


def _chip_exchange(name, sums, ex):
    def body(*refs):
        s_ref, x_in = refs[0], refs[1:1 + ex.nb]
        slots_ref, x_out = refs[1 + ex.nb], refs[2 + ex.nb:2 + 2 * ex.nb]
        send_sems, recv_sems = refs[2 + 2 * ex.nb:4 + 2 * ex.nb]
        ex_sems = refs[4 + 2 * ex.nb:]
        x, y, c = lax.axis_index("x"), lax.axis_index("y"), lax.axis_index("c")
        my_q = 2 * x + y
        sends, recvs = [], []
        for kbits in range(1, N_CHIP):
            px = 1 - x if kbits & 2 else x
            py = 1 - y if kbits & 1 else y
            peer_q = 2 * px + py
            sends.append(pltpu.make_async_remote_copy(
                src_ref=s_ref.at[peer_q], dst_ref=slots_ref.at[my_q], send_sem=send_sems.at[kbits - 1],
                recv_sem=recv_sems.at[kbits - 1], device_id=(px, py, c), device_id_type=pl.DeviceIdType.MESH))
            recvs.append(pltpu.make_async_remote_copy(
                src_ref=s_ref.at[peer_q], dst_ref=slots_ref.at[peer_q], send_sem=send_sems.at[kbits - 1],
                recv_sem=recv_sems.at[kbits - 1], device_id=(px, py, c), device_id_type=pl.DeviceIdType.MESH))
        for cp in sends:
            cp.start()
        ex.start(x_in, x_out, ex_sems)
        for cp in recvs:
            cp.wait_recv()
        for cp in sends:
            cp.wait_send()
        ex.wait(x_in, x_out, ex_sems)

    any_spec = pl.BlockSpec(memory_space=pl.ANY)
    res = pl.pallas_call(
        body, name=name, in_specs=[any_spec] + ex.any_specs, out_specs=[any_spec] + ex.any_specs,
        out_shape=[jax.ShapeDtypeStruct(sums.shape, sums.dtype)] + ex.out_shape,
        scratch_shapes=[pltpu.SemaphoreType.DMA((N_CHIP - 1,)), pltpu.SemaphoreType.DMA((N_CHIP - 1,))]
        + ex.sem_shapes,
    )(sums, *ex.bufs)
    return res[0], list(res[1:])


def _adamw(name, slots, w, m, v, tr=256):
    R, Wd = w.shape[-2:]
    depth_axis = w.ndim == 3
    if R % tr == 0:
        tc = Wd
    else:
        tr, tc = R, (256 if (Wd % 256 == 0 and R > 256) else Wd)

    def body(s_ref, w_ref, m_ref, v_ref, g_out, d_out, m_out, v_out):
        g = s_ref[0].astype(F32)
        for j in range(1, slots.shape[0]):
            g = g + s_ref[j].astype(F32)
        m_new = ADAM_B1 * m_ref[...] + (1.0 - ADAM_B1) * g
        v_new = ADAM_B2 * v_ref[...] + (1.0 - ADAM_B2) * jnp.square(g)
        m_hat = m_new / (1.0 - ADAM_B1 ** ADAM_STEP)
        v_hat = v_new / (1.0 - ADAM_B2 ** ADAM_STEP)
        g_out[...] = g
        d_out[...] = -ADAM_LR * (m_hat / (jnp.sqrt(v_hat) + ADAM_EPS) + ADAM_WD * w_ref[...])
        m_out[...] = m_new
        v_out[...] = v_new

    if depth_axis:
        row = pl.BlockSpec((None, tr, tc), lambda i, j: (0, i, j))
    else:
        row = pl.BlockSpec((tr, tc), lambda i, j: (i, j))
    return pl.pallas_call(
        body, name=name, grid=(R // tr, Wd // tc),
        in_specs=[pl.BlockSpec((slots.shape[0], tr, tc), lambda i, j: (0, i, j)), row, row, row],
        out_specs=[row] * 4, out_shape=[jax.ShapeDtypeStruct(w.shape, F32)] * 4,
        compiler_params=pltpu.CompilerParams(dimension_semantics=("parallel", "parallel"),
                                             vmem_limit_bytes=VMEM_LIMIT),
    )(slots, w, m, v)


PACK_W = 1024
PACKED = [(n, s) for n, s in REPLICATED if n != "sgu_w"]
_SMALL_SIZES = [int(np.prod(s)) for _, s in PACKED]
_SMALL_ROWS = _round_up(_round_up(sum(_SMALL_SIZES) + PACK_W, PACK_W) // PACK_W, 8)
_LOSS_AT = sum(_SMALL_SIZES)
W_IN_SHARD = P_TOTAL // N_DEV


def _pack_rows(parts, rows, dtype):
    flat = jnp.concatenate([p.reshape(-1).astype(dtype) for p in parts])
    return jnp.pad(flat, (0, rows * PACK_W - flat.shape[0])).reshape(rows, PACK_W)


def _w_in_groups_t(blocks):
    wt = blocks.reshape(P_TOTAL, D)
    o, c = 2 * D, 2 * D + 3 * D
    z = lambda r: jnp.zeros((r, D), wt.dtype)
    rw = jnp.concatenate([wt[o:c], wt[c:c + L_W], z(128 - L_W), wt[c + L_W:c + L_W + L_A], z(128 - L_A),
                          wt[c + L_W + L_A:o + C_B], z(256 - L_G)], axis=0)
    return wt[:o], rw, wt[o + C_B:]


def _w_in_grad_blocks(g_sgu_t, g_rw_t, g_gate_t):
    c = 3 * D
    full = jnp.concatenate([g_sgu_t, g_rw_t[:c], g_rw_t[c:c + L_W], g_rw_t[c + 128:c + 128 + L_A],
                            g_rw_t[c + 256:c + 256 + L_G], g_gate_t], axis=0)
    return full.reshape(N_DEV, W_IN_SHARD, D)


def _mesh_index():
    me = 4 * lax.axis_index("x") + 2 * lax.axis_index("y") + lax.axis_index("c")
    return me.astype(jnp.int32).reshape(1)


def _fill_slot(name, dst, src, idx, src_idx=None):
    R, Wd = dst.shape[1:]
    scalars = [idx] if src_idx is None else [idx, src_idx]
    if src_idx is None:
        src_spec = pl.BlockSpec((R, Wd), lambda i, *s: (0, 0))
    else:
        src_spec = pl.BlockSpec((None, R, Wd), lambda i, *s: (s[1][0], 0, 0))

    def body(*refs):
        src_ref, out_ref = refs[len(scalars) + 1], refs[len(scalars) + 2]
        out_ref[...] = src_ref[...]

    return pl.pallas_call(
        body, name=name,
        grid_spec=pltpu.PrefetchScalarGridSpec(
            num_scalar_prefetch=len(scalars), grid=(1,),
            in_specs=[pl.BlockSpec(memory_space=pl.ANY), src_spec],
            out_specs=pl.BlockSpec((None, R, Wd), lambda i, *s: (s[0][0], 0, 0))),
        out_shape=jax.ShapeDtypeStruct(dst.shape, dst.dtype),
        input_output_aliases={len(scalars): 0},
        compiler_params=pltpu.CompilerParams(vmem_limit_bytes=VMEM_LIMIT),
    )(*scalars, dst, src)


def _all_gather_two_level(name, bufs, skip_own=()):
    nb = len(bufs)

    def body(*refs):
        in_refs, out_refs = refs[:nb], refs[nb:2 * nb]
        send_sems, recv_sems, local_sems = refs[2 * nb:]
        x, y, c = lax.axis_index("x"), lax.axis_index("y"), lax.axis_index("c")
        me, sibling = (x, y, c), (x, y, 1 - c)
        chips = [(1 - x, y), (x, 1 - y), (1 - x, 1 - y)]

        def slot(b, dev):
            return out_refs[b].at[4 * dev[0] + 2 * dev[1] + dev[2]]

        def copy(b, k, block, to, own=False):
            return pltpu.make_async_remote_copy(
                src_ref=in_refs[b] if own else slot(b, block), dst_ref=slot(b, block),
                send_sem=send_sems.at[7 * b + k], recv_sem=recv_sems.at[7 * b + k], device_id=to,
                device_id_type=pl.DeviceIdType.MESH)

        local = [pltpu.make_async_copy(in_refs[b], slot(b, me), local_sems.at[b]) for b in range(nb)
                 if b not in skip_own]
        first = [copy(b, 0, me, sibling, own=True) for b in range(nb)]
        first += [copy(b, 1 + j, me, (*chip, c), own=True) for j, chip in enumerate(chips) for b in range(nb)]
        for cp in first + local:
            cp.start()
        passed = []
        for j, chip in enumerate(chips):
            for b in range(nb):
                copy(b, 1 + j, (*chip, c), me).wait_recv()
                passed.append(copy(b, 4 + j, (*chip, c), sibling))
                passed[-1].start()
        for b in range(nb):
            copy(b, 0, sibling, me).wait_recv()
        for j, chip in enumerate(chips):
            for b in range(nb):
                copy(b, 4 + j, (*chip, 1 - c), me).wait_recv()
        for cp in first + passed:
            cp.wait_send()
        for cp in local:
            cp.wait()

    any_spec = pl.BlockSpec(memory_space=pl.ANY)
    return pl.pallas_call(
        body, name=name, in_specs=[any_spec] * nb, out_specs=[any_spec] * nb,
        out_shape=[jax.ShapeDtypeStruct((N_DEV,) + b.shape, b.dtype) for b in bufs],
        scratch_shapes=[pltpu.SemaphoreType.DMA((7 * nb,)), pltpu.SemaphoreType.DMA((7 * nb,)),
                        pltpu.SemaphoreType.DMA((nb,))],
    )(*bufs)


def _cols_from_blocks(blk):
    return jnp.transpose(blk, (1, 0, 2)).reshape(blk.shape[1], -1)


def _cols_to_blocks(g):
    r, c = g.shape
    return jnp.transpose(g.reshape(r, N_DEV, c // N_DEV), (1, 0, 2))


FIRST_WEIGHTS = ["w_in", "shift_b", "w_lora_w", "a_lora_w", "g_lora_w"]
LATE_WEIGHTS = ["w_proj_a", "w_proj_b", "w_out", "w_ffn1", "w_ffn2"]


def _late_weights(shards):
    ex = Exchange([shards[n].astype(BF16) for n in LATE_WEIGHTS], [True] * len(LATE_WEIGHTS))

    def finish(results):
        got = dict(zip(LATE_WEIGHTS, results))
        W = {n: got[n].reshape(-1, D) for n in ("w_proj_a", "w_proj_b", "w_out", "w_ffn2")}
        W["w_ffn1"] = got["w_ffn1"].reshape(N_DEV, D, -1)
        return W
    return ex, finish


def _gather_weights(shards):
    def payload(n):
        if n == "w_in":
            return jnp.transpose(shards[n][0]).astype(BF16)
        return shards[n] if n == "shift_b" else shards[n].astype(BF16)
    payloads = [payload(n) for n in FIRST_WEIGHTS]
    got = dict(zip(FIRST_WEIGHTS, _all_gather_two_level("weight_all_gather", payloads, skip_own=(0,))))
    got["w_in"] = _fill_slot("w_in_own_slot", got["w_in"], payloads[0], _mesh_index())
    W = {}
    W["w_sgu_t"], W["w_rw_t"], W["w_gate_t"] = _w_in_groups_t(got["w_in"])
    z = lambda r, c, dt: jnp.zeros((r, c), dt)
    W["w_lora"] = jnp.concatenate([_cols_from_blocks(got["w_lora_w"][:, 0]).astype(F32), z(128 - L_W, D, F32)], axis=0)
    W["a_lora"] = jnp.concatenate([_cols_from_blocks(got["a_lora_w"][:, 0]).astype(F32), z(128 - L_A, D, F32)], axis=0)
    W["g_lora"] = jnp.concatenate([_cols_from_blocks(got["g_lora_w"][:, 0]).astype(F32), z(256 - L_G, D, F32)], axis=0)
    sb = _cols_from_blocks(got["shift_b"][:, 0])
    W["sb"] = jnp.concatenate([sb[:, :3 * D], sb[:, 3 * D:3 * D + L_W], z(2, 128 - L_W, F32),
                               sb[:, 3 * D + L_W:3 * D + L_W + L_A], z(2, 128 - L_A, F32),
                               sb[:, 3 * D + L_W + L_A:], z(2, 256 - L_G, F32)], axis=1)
    return W


def _replicated_weights(rep):
    W = {n: rep[n] for n in ("g_mix", "sgu_ln_w", "sgu_ln_b", "w0", "a0", "k_k", "k_a", "r_k", "ln_x_w", "ln_x_b",
                             "g_ffn")}
    W["g_final"] = rep["g_final"].reshape(1, D)
    W["sgu_w"] = rep["sgu_w"][0]
    W["sgu_bt"] = jnp.transpose(rep["sgu_b"][0])
    return W


def _late_grad_blocks(G):
    blocks = {n: G[n].reshape(N_DEV, -1, D) for n in ("w_proj_a", "w_proj_b", "w_out", "w_ffn2")}
    blocks["w_ffn1"] = G["w_ffn1"]
    return Exchange([blocks[n] for n in LATE_WEIGHTS] + [G["sgu_w"].reshape(SGU_G * SGU_C, SGU_C)],
                    [False] * len(LATE_WEIGHTS) + [True])


def _first_grad_blocks(G):
    sbg = G["sb"]
    c = 3 * D
    sb = jnp.concatenate([sbg[:, :c], sbg[:, c:c + L_W], sbg[:, c + 128:c + 128 + L_A],
                          sbg[:, c + 256:c + 256 + L_G]], axis=1)
    return {
        "w_in": _w_in_grad_blocks(G["w_sgu_t"], G["w_rw_t"], G["w_gate_t"]),
        "shift_b": _cols_to_blocks(sb),
        "w_lora_w": _cols_to_blocks(G["w_lora"][:L_W]), "a_lora_w": _cols_to_blocks(G["a_lora"][:L_A]),
        "g_lora_w": _cols_to_blocks(G["g_lora"][:L_G]),
    }


def _replicated_grads(G):
    small = {n: G[n] for n in ("g_mix", "sgu_ln_w", "sgu_ln_b", "w0", "a0", "k_k", "k_a", "r_k", "ln_x_w", "ln_x_b",
                               "g_ffn", "g_final")}
    small["sgu_w"] = G["sgu_w"]
    small["sgu_b"] = jnp.transpose(G["sgu_bt"])
    return small


def kernel(x, g_mix, w_in, sgu_ln_w, sgu_ln_b, sgu_w, sgu_b, w_proj_a, shift_b, w_lora_w, w0, a_lora_w, a0, g_lora_w, k_k, k_a, r_k, ln_x_w, ln_x_b, w_proj_b, w_out, g_ffn, w_ffn1, w_ffn2, g_final, loss_target, m_g_mix, m_w_in, m_sgu_ln_w, m_sgu_ln_b, m_sgu_w, m_sgu_b, m_w_proj_a, m_shift_b, m_w_lora_w, m_w0, m_a_lora_w, m_a0, m_g_lora_w, m_k_k, m_k_a, m_r_k, m_ln_x_w, m_ln_x_b, m_w_proj_b, m_w_out, m_g_ffn, m_w_ffn1, m_w_ffn2, m_g_final, v_g_mix, v_w_in, v_sgu_ln_w, v_sgu_ln_b, v_sgu_w, v_sgu_b, v_w_proj_a, v_shift_b, v_w_lora_w, v_w0, v_a_lora_w, v_a0, v_g_lora_w, v_k_k, v_k_a, v_r_k, v_ln_x_w, v_ln_x_b, v_w_proj_b, v_w_out, v_g_ffn, v_w_ffn1, v_w_ffn2, v_g_final):
    env = dict(locals())
    weights = {n: env[n] for n in WEIGHT_ORDER}
    moms = {n: env["m_" + n] for n in WEIGHT_ORDER}
    vars_ = {n: env["v_" + n] for n in WEIGHT_ORDER}

    shards = {n: weights[n] for n, _, _ in SHARDED}
    W = _gather_weights(shards)
    W.update(_replicated_weights({n: weights[n] for n, _ in REPLICATED}))
    loss_part, dx, G, late_slots = _local_step(x[0], loss_target[0], W, late_weights=_late_weights(shards),
                                               early_grads=_late_grad_blocks)

    slots = dict(zip(LATE_WEIGHTS, late_slots))
    blocks = _first_grad_blocks(G)
    small = _replicated_grads(G)
    small_parts = [small[n] for n, _ in PACKED] + [jnp.full((PACK_W,), loss_part, F32)]
    got = _pair_exchange("grad_pair_exchange", blocks["w_in"])
    core = lax.axis_index("c").astype(jnp.int32).reshape(1)
    rest = [n for n in FIRST_WEIGHTS if n != "w_in"]
    ex = Exchange([blocks[n] for n in rest] + [_pack_rows(small_parts, _SMALL_ROWS, F32)], [False] * len(rest) + [True])
    sums = _pair_sum("grad_pair_sum", blocks["w_in"], got, core)
    chip_slots, res = _chip_exchange("grad_exchange", sums, ex)
    my_chip = (2 * lax.axis_index("x") + lax.axis_index("y")).astype(jnp.int32).reshape(1)
    slots["w_in"] = _fill_slot("grad_own_slot", chip_slots, sums, my_chip, src_idx=my_chip)
    slots.update(zip(rest, res[:-1]))
    small_slots = res[-1]

    outs = [dict(), dict(), dict(), dict()]
    for n, _, _ in SHARDED:
        if n == "w_in":
            res = _adamw("adamw_" + n, slots[n], *[jnp.transpose(t[0]) for t in (weights[n], moms[n], vars_[n])])
            res = [jnp.transpose(t)[None] for t in res]
        else:
            res = _adamw("adamw_" + n, slots[n], weights[n], moms[n], vars_[n])
        for k in range(4):
            outs[k][n] = res[k]

    sgu_shape = (SGU_G * SGU_C, SGU_C)
    res = _adamw("adamw_sgu_w", late_slots[len(LATE_WEIGHTS)], *[t.reshape(sgu_shape) for t in
                                                                  (weights["sgu_w"], moms["sgu_w"], vars_["sgu_w"])])
    for k in range(4):
        outs[k]["sgu_w"] = res[k].reshape(weights["sgu_w"].shape)

    def packed(d):
        return _pack_rows([d[n] for n, _ in PACKED], _SMALL_ROWS, F32)
    small_out = _adamw("adamw_replicated", small_slots, packed(weights), packed(moms), packed(vars_))
    for k in range(4):
        flat = small_out[k].reshape(-1)
        off = 0
        for (n, s), size in zip(PACKED, _SMALL_SIZES):
            outs[k][n] = flat[off:off + size].reshape(s)
            off += size
    loss = small_out[0].reshape(-1)[_LOSS_AT]
    return (loss, dx[None], *[outs[0][n] for n in WEIGHT_ORDER], *[outs[1][n] for n in WEIGHT_ORDER],
            *[outs[2][n] for n in WEIGHT_ORDER], *[outs[3][n] for n in WEIGHT_ORDER])
```

```python
import functools
import numpy as np
import jax
import jax.numpy as jnp
from jax import lax
from jax.experimental import pallas as pl
from jax.experimental.pallas import tpu as pltpu

F32 = jnp.float32
BF16 = jnp.bfloat16

D = 1024
NH, HN = 16, 64
SGU_G, SGU_C = 8, 128
L_W, L_A, L_G = 64, 64, 160
C_B = 3 * D + L_W + L_A + L_G
P_TOTAL = 2 * D + C_B + 2 * D
D_FF = 4 * D
RW_INT = 3 * D + 128 + 128 + 256
NORM_EPS, LN_EPS, GN_EPS = 1e-6, 1e-5, 64e-5
N_DEV = 8
LANES = 128
SCAN_C = 64
SOLVE_B = 16
SCAN_PRECISION = lax.Precision.HIGH
SCAN_OUT_PRECISION = lax.Precision.DEFAULT
GRAD_PAYLOAD = BF16
VMEM_LIMIT = 56 * 1024 * 1024
MATMUL_VMEM_BUDGET = 40 * 1024 * 1024
STEP_COST_BYTES = 512 * 1024

ADAM_LR, ADAM_B1, ADAM_B2, ADAM_EPS, ADAM_WD, ADAM_STEP = 0.001, 0.9, 0.999, 1e-08, 0.01, 10

SHARDED = [
    ("w_in", (D, P_TOTAL), 1), ("w_proj_a", (D, D), 0), ("shift_b", (2, C_B), 1), ("w_lora_w", (L_W, D), 1),
    ("a_lora_w", (L_A, D), 1), ("g_lora_w", (L_G, D), 1), ("w_proj_b", (D, D), 0), ("w_out", (D, D), 0),
    ("w_ffn1", (D, D_FF), 1), ("w_ffn2", (D_FF, D), 0),
]
REPLICATED = [
    ("g_mix", (1, D)), ("sgu_ln_w", (1, D)), ("sgu_ln_b", (1, D)), ("sgu_w", (1, SGU_G, SGU_C, SGU_C)),
    ("sgu_b", (1, SGU_G, SGU_C)), ("w0", (1, D)), ("a0", (1, D)), ("k_k", (1, D)), ("k_a", (1, D)), ("r_k", (1, D)),
    ("ln_x_w", (1, D)), ("ln_x_b", (1, D)), ("g_ffn", (1, D)), ("g_final", (D,)),
]
WEIGHT_ORDER = ["g_mix", "w_in", "sgu_ln_w", "sgu_ln_b", "sgu_w", "sgu_b", "w_proj_a", "shift_b", "w_lora_w", "w0",
                "a_lora_w", "a0", "g_lora_w", "k_k", "k_a", "r_k", "ln_x_w", "ln_x_b", "w_proj_b", "w_out", "g_ffn",
                "w_ffn1", "w_ffn2", "g_final"]


def _shard_shape(shape, axis):
    s = list(shape)
    s[axis] //= N_DEV
    return tuple(s)


def _round_up(n, m):
    return (n + m - 1) // m * m


def _pick(n, target):
    if n <= target:
        return n
    best = None
    for t in range(LANES, target + 1, LANES):
        if n % t == 0:
            best = t
    assert best is not None, (n, target)
    return best


def _matmul(name, a, b, mode, out_dtype=F32, tm=2048, tn=1024, tk=2048, out_blocks=None, epilogue=None, extras=(),
            out_dtypes=()):
    b_blocks = b.shape[0] if b.ndim == 3 else None
    bshape = b.shape if b.ndim == 2 else (b.shape[1], b.shape[0] * b.shape[2])
    if mode == "nn":
        (M, K), (K2, N) = a.shape, bshape
    elif mode == "nt":
        (M, K), (N, K2) = a.shape, bshape
    else:
        (K, M), (K2, N) = a.shape, bshape
    assert K == K2, (name, a.shape, b.shape)
    assert b_blocks is None or mode != "tn"
    assert out_blocks is None or mode == "tn"
    tn = min(tn, N // (out_blocks or 1), bshape[1] // b_blocks if (b_blocks and mode == "nn") else tn)
    tk = min(tk, bshape[1] // b_blocks if (b_blocks and mode == "nt") else tk)
    tm, tn, tk = _pick(M, tm), _pick(N, tn), _pick(K, tk)

    def vmem_bytes(tm, tk):
        tiles = tm * tk * a.dtype.itemsize + tk * tn * b.dtype.itemsize
        for dt in (out_dtypes if epilogue else (out_dtype,)):
            tiles += tm * tn * jnp.dtype(dt).itemsize
        for x in extras:
            arr = x[0] if isinstance(x, tuple) else x
            tiles += (tm if arr.shape[0] > 1 else 1) * tn * arr.dtype.itemsize
        return 2 * tiles + (tm * tn * 4 if K // tk > 1 else 0)

    def cost(tm, tk):
        steps = (M // tm) * (N // tn) * (K // tk)
        acc_passes = steps * tm * tn * 8 if K // tk > 1 else 0
        return steps * STEP_COST_BYTES + acc_passes + vmem_bytes(tm, tk) // 2

    options = [(m, k) for m in {_pick(M, max(t, LANES)) for t in (tm, tm // 2, tm // 4)}
               for k in {_pick(K, max(t, LANES)) for t in (tk, tk // 2, tk // 4)}
               if vmem_bytes(m, k) <= MATMUL_VMEM_BUDGET]
    tm, tk = min(options, key=lambda o: cost(*o))
    nk = K // tk
    dims = {"nn": (((1,), (0,)), ((), ())), "nt": (((1,), (1,)), ((), ())), "tn": (((0,), (0,)), ((), ()))}[mode]

    n_x, n_o = len(extras), len(out_dtypes) if epilogue else 1

    def body(a_ref, b_ref, *rest):
        x_refs, o_refs, acc = rest[:n_x], rest[n_x:n_x + n_o], rest[n_x + n_o:]
        part = lax.dot_general(a_ref[...].astype(BF16), b_ref[...].astype(BF16), dims, preferred_element_type=F32)

        def finish(res):
            outs = epilogue(res, *[r[...] for r in x_refs]) if epilogue else (res,)
            for r, v in zip(o_refs, outs):
                r[...] = v.astype(r.dtype)

        if nk == 1:
            finish(part)
            return
        acc_ref, k = acc[0], pl.program_id(2)

        @pl.when(k == 0)
        def _():
            acc_ref[...] = part

        @pl.when(k > 0)
        def _():
            acc_ref[...] += part

        @pl.when(k == nk - 1)
        def _():
            finish(acc_ref[...])

    a_spec = {"nn": pl.BlockSpec((tm, tk), lambda i, j, k: (i, k)), "nt": pl.BlockSpec((tm, tk), lambda i, j, k: (i, k)),
              "tn": pl.BlockSpec((tk, tm), lambda i, j, k: (k, i))}[mode]
    b_spec = {"nn": pl.BlockSpec((tk, tn), lambda i, j, k: (k, j)), "nt": pl.BlockSpec((tn, tk), lambda i, j, k: (j, k)),
              "tn": pl.BlockSpec((tk, tn), lambda i, j, k: (k, j))}[mode]
    if b_blocks and mode == "nn":
        per = b.shape[2] // tn
        b_spec = pl.BlockSpec((None, tk, tn), lambda i, j, k: (j // per, k, j % per))
    elif b_blocks:
        per = b.shape[2] // tk
        b_spec = pl.BlockSpec((None, tn, tk), lambda i, j, k: (k // per, j, k % per))
    out_spec = pl.BlockSpec((tm, tn), lambda i, j, k: (i, j))
    out_shape = jax.ShapeDtypeStruct((M, N), out_dtype)
    if out_blocks:
        per_o = N // out_blocks // tn
        out_spec = pl.BlockSpec((None, tm, tn), lambda i, j, k: (j // per_o, i, j % per_o))
        out_shape = jax.ShapeDtypeStruct((out_blocks, M, N // out_blocks), out_dtype)
    x_specs, x_args = [], []
    for x in extras:
        arr, off = x if isinstance(x, tuple) else (x, 0)
        if arr.shape[0] == 1:
            x_specs.append(pl.BlockSpec((1, tn), lambda i, j, k: (0, j)))
        else:
            x_specs.append(pl.BlockSpec((tm, tn), lambda i, j, k, off=off: (i, j + off)))
        x_args.append(arr)
    res = pl.pallas_call(
        body, name=name, grid=(M // tm, N // tn, nk), in_specs=[a_spec, b_spec] + x_specs,
        out_specs=[out_spec] * n_o if epilogue else out_spec,
        out_shape=[jax.ShapeDtypeStruct((M, N), dt) for dt in out_dtypes] if epilogue else out_shape,
        scratch_shapes=[pltpu.VMEM((tm, tn), F32)] if nk > 1 else [],
        compiler_params=pltpu.CompilerParams(dimension_semantics=("parallel", "parallel", "arbitrary"),
                                             vmem_limit_bytes=VMEM_LIMIT),
    )(a, b, *x_args)
    return res


class Rows:
    def __init__(self, arr, width=None, cb=0):
        self.arr, self.width, self.cb = arr, (arr.shape[1] if width is None else width), cb


class Heads:
    def __init__(self, arr):
        self.arr = arr


class Halo:
    def __init__(self, arr, side):
        self.arr, self.side = arr, side


def _rows_call(name, fn, ins, consts, outs, accs=(), tm=256, with_pid=False):
    T = next(o.arr.shape[1] if isinstance(o, Heads) else o.arr.shape[0] for o in ins if not isinstance(o, Halo))
    tm = min(tm, T)
    n_tiles = T // tm
    n_in, n_c, n_out = len(ins), len(consts), len(outs)
    in_specs, args = [], []
    for o in ins:
        if isinstance(o, Rows):
            in_specs.append(pl.BlockSpec((tm, o.width), lambda i, cb=o.cb: (i, cb)))
        elif isinstance(o, Heads):
            in_specs.append(pl.BlockSpec((NH, tm, HN), lambda i: (0, i, 0)))
        else:
            w = o.arr.shape[1]
            if o.side < 0:
                in_specs.append(pl.BlockSpec((8, w), lambda i: (jnp.maximum(i * (tm // 8) - 1, 0), 0)))
            else:
                in_specs.append(pl.BlockSpec((8, w), lambda i: (jnp.minimum((i + 1) * (tm // 8), T // 8 - 1), 0)))
        args.append(o.arr)
    for c in consts:
        in_specs.append(pl.BlockSpec(c.shape, lambda i, nd=c.ndim: (0,) * nd))
        args.append(c)
    out_specs, out_shape = [], []
    for o in outs:
        if o[0] == "rows":
            out_specs.append(pl.BlockSpec((tm, o[1]), lambda i: (i, 0)))
            out_shape.append(jax.ShapeDtypeStruct((T, o[1]), o[2]))
        else:
            out_specs.append(pl.BlockSpec((NH, tm, HN), lambda i: (0, i, 0)))
            out_shape.append(jax.ShapeDtypeStruct((NH, T, HN), o[1]))
    for shape, dt in accs:
        out_specs.append(pl.BlockSpec(shape, lambda i, nd=len(shape): (0,) * nd))
        out_shape.append(jax.ShapeDtypeStruct(shape, dt))

    def body(*refs):
        i = pl.program_id(0)
        vals = []
        vals = [r[...] for r in refs[:n_in + n_c]]
        res = fn(i, n_tiles, *vals) if with_pid else fn(*vals)
        out_refs = refs[n_in + n_c:]
        for r, v in zip(out_refs[:n_out], res[:n_out]):
            r[...] = v.astype(r.dtype)
        if accs:
            @pl.when(i == 0)
            def _():
                for r in out_refs[n_out:]:
                    r[...] = jnp.zeros_like(r)

            for r, v in zip(out_refs[n_out:], res[n_out:]):
                r[...] += v.astype(r.dtype)

    res = pl.pallas_call(
        body, name=name, grid=(n_tiles,), in_specs=in_specs, out_specs=out_specs, out_shape=out_shape,
        compiler_params=pltpu.CompilerParams(dimension_semantics=("arbitrary",), vmem_limit_bytes=VMEM_LIMIT),
    )(*args)
    return res


def _rms(x, g):
    return x * lax.rsqrt(jnp.mean(x * x, axis=-1, keepdims=True) + NORM_EPS) * g


def _gelu(x):
    return 0.5 * x * (1.0 + lax.erf(x * 0.7071067811865476))


def _sigmoid(x):
    return 1.0 / (1.0 + jnp.exp(-x))


def _bdot(a, b):
    return jnp.dot(a.astype(BF16), b.astype(BF16), preferred_element_type=F32)


def _to_heads(x):
    return jnp.concatenate([x[:, h * HN:(h + 1) * HN][None] for h in range(NH)], axis=0)


def _from_heads(xh):
    return jnp.concatenate([xh[h] for h in range(NH)], axis=-1)


def _sgu_fn(p, ln_w, ln_b, sw, sbt):
    z = _gelu(p)
    u, v = z[:, :D], z[:, D:]
    mu = jnp.mean(v, axis=-1, keepdims=True)
    var = jnp.mean(jnp.square(v - mu), axis=-1, keepdims=True)
    vn = (v - mu) * lax.rsqrt(var + LN_EPS) * ln_w + ln_b
    ri = lax.broadcasted_iota(jnp.int32, (SGU_C, SGU_C), 0)
    ci = lax.broadcasted_iota(jnp.int32, (SGU_C, SGU_C), 1)
    mask = (ci <= ri).astype(F32)
    dg = D // SGU_G
    parts = []
    for g in range(SGU_G):
        parts.append(_bdot(sw[g] * mask, vn[:, g * dg:(g + 1) * dg]) + sbt[:, g:g + 1])
    return u * jnp.concatenate(parts, axis=-1)


def _pre_fn(qr, qk, qv, qxw, qxa, qxg, wl, w0, al, a0, gl, k_k, k_a):
    w = -jax.nn.softplus(-(w0 + _bdot(jnp.tanh(qxw), wl))) - 0.5
    lw = -jnp.exp(w)
    aa = _sigmoid(a0 + _bdot(qxa, al))
    g = _bdot(_sigmoid(qxg), gl)
    kk = _to_heads(qk * k_k)
    kk = kk / jnp.maximum(jnp.sqrt(jnp.sum(kk * kk, axis=-1, keepdims=True)), 1e-12)
    k2 = qk * (1.0 + (aa - 1.0) * k_a)
    return _to_heads(qr), _to_heads(lw), _to_heads(k2), _to_heads(qv), kk, _to_heads(aa), g


def _post_fn(o, r, k2, v, g, ln_w, ln_b, r_k):
    mu = jnp.mean(o, axis=-1, keepdims=True)
    d = o - mu
    var = jnp.mean(d * d, axis=-1, keepdims=True)
    on = d * lax.rsqrt(var + GN_EPS) * ln_w + ln_b
    bonus = jnp.sum(r * k2 * r_k, axis=-1, keepdims=True) * v
    return _from_heads(on + bonus) * g


def _gate_fn(pg, ya, yb):
    return _sigmoid(pg[:, :D]) * ya + _sigmoid(pg[:, D:]) * yb


def _bmm(x, y, cx, cy, out_path=False):
    return lax.dot_general(x, y, (((cx,), (cy,)), ((0,), (0,))),
                           precision=SCAN_OUT_PRECISION if out_path else SCAN_PRECISION, preferred_element_type=F32)


def _unit_lower_inverse(M):
    C = M.shape[1]
    ti = lax.broadcasted_iota(jnp.int32, (C, C), 0)
    tj = lax.broadcasted_iota(jnp.int32, (C, C), 1)
    eye = (ti == tj).astype(F32)
    same = lambda b: (ti // b == tj // b).astype(F32)
    X = -(M * same(SOLVE_B))
    inv = eye + X
    span = 1
    while 2 * span < SOLVE_B:
        X = _bmm(X, X, 2, 1)
        inv = inv + _bmm(inv, X, 2, 1)
        span *= 2
    b = SOLVE_B
    while b < C:
        low = M * (same(2 * b) - same(b))
        inv = inv - _bmm(_bmm(inv, low, 2, 1), inv, 2, 1)
        b *= 2
    return inv


@jax.custom_vjp
def _unit_lower_solve(M, y):
    return _bmm(_unit_lower_inverse(M), y, 2, 1)


def _unit_lower_solve_fwd(M, y):
    inv = _unit_lower_inverse(M)
    u = _bmm(inv, y, 2, 1)
    return u, (inv, u)


def _unit_lower_solve_bwd(res, du):
    inv, u = res
    dy = _bmm(inv, du, 1, 1)
    return -_bmm(dy, u, 2, 2), dy


_unit_lower_solve.defvjp(_unit_lower_solve_fwd, _unit_lower_solve_bwd)


def _chunk_fn(S0, r, lw, k, v, kk, a):
    C = SCAN_C
    bmm = _bmm
    ti = lax.broadcasted_iota(jnp.int32, (C, C), 0)
    tj = lax.broadcasted_iota(jnp.int32, (C, C), 1)
    incl = (tj <= ti).astype(F32)
    strict = (tj < ti).astype(F32)
    cum = lax.dot_general(jnp.broadcast_to(incl, (NH, C, C)), lw, (((2,), (1,)), ((0,), (0,))),
                          precision=lax.Precision.HIGHEST, preferred_element_type=F32)
    g_in, g_ex, g_inv = jnp.exp(cum), jnp.exp(cum - lw), jnp.exp(-cum)
    kkt, rt = kk * g_ex, r * g_in
    bk = jnp.concatenate([kk * a * g_inv, k * g_inv], axis=1)
    A = bmm(kkt, bk, 2, 2)
    M = A[:, :, :C] * strict
    n_mask = jnp.concatenate([jnp.zeros((C, C), F32), strict], axis=1)
    zv = jnp.concatenate([jnp.zeros_like(v), v], axis=1)
    s0_side = bmm(jnp.concatenate([kkt, rt], axis=1), S0, 2, 2, out_path=True)
    y = _unit_lower_solve(M, s0_side[:, :C] + bmm(A * n_mask, zv, 2, 1, out_path=True))
    z = jnp.concatenate([-y, v], axis=1)
    attn = bmm(rt, bk, 2, 2) * jnp.concatenate([incl, incl], axis=1)
    O = s0_side[:, C:] + bmm(attn, z, 2, 1, out_path=True)
    S1 = (S0 + bmm(z, bk, 1, 1, out_path=True)) * g_in[:, C - 1:C, :]
    return O, S1


def _scan_fwd(r, lw, k, v, kk, a, ex=None, tb=256):
    T = r.shape[1]
    tb = min(tb, T)
    n_chunks = tb // SCAN_C
    nb = T // tb
    nx = ex.nb if ex else 0

    def body(*refs):
        r_ref, lw_ref, k_ref, v_ref, kk_ref, a_ref = refs[:6]
        x_in, (o_ref, s0_ref), x_out = refs[6:6 + nx], refs[6 + nx:8 + nx], refs[8 + nx:8 + 2 * nx]
        s_ref, sems = refs[8 + 2 * nx], refs[9 + 2 * nx:]

        @pl.when(pl.program_id(0) == 0)
        def _():
            s_ref[...] = jnp.zeros_like(s_ref)
            if ex:
                ex.start(x_in, x_out, sems)

        def step(c, carry):
            sl = pl.ds(pl.multiple_of(c * SCAN_C, SCAN_C), SCAN_C)
            S0 = s_ref[...]
            s0_ref[c] = S0
            O, S1 = _chunk_fn(S0, r_ref[:, sl, :], lw_ref[:, sl, :], k_ref[:, sl, :], v_ref[:, sl, :],
                              kk_ref[:, sl, :], a_ref[:, sl, :])
            o_ref[:, sl, :] = O
            s_ref[...] = S1
            return carry

        lax.fori_loop(0, n_chunks, step, 0)

        if ex:
            @pl.when(pl.program_id(0) == nb - 1)
            def _():
                ex.wait(x_in, x_out, sems)

    hm = pl.BlockSpec((NH, tb, HN), lambda i: (0, i, 0))
    res = pl.pallas_call(
        body, name="rwkv_scan_fwd", grid=(nb,), in_specs=[hm] * 6 + (ex.any_specs if ex else []),
        out_specs=[hm, pl.BlockSpec((n_chunks, NH, HN, HN), lambda i: (i, 0, 0, 0))] + (ex.any_specs if ex else []),
        out_shape=[jax.ShapeDtypeStruct((NH, T, HN), F32), jax.ShapeDtypeStruct((T // SCAN_C, NH, HN, HN), F32)]
        + (ex.out_shape if ex else []),
        scratch_shapes=[pltpu.VMEM((NH, HN, HN), F32)] + (ex.sem_shapes if ex else []),
        compiler_params=pltpu.CompilerParams(dimension_semantics=("arbitrary",), vmem_limit_bytes=VMEM_LIMIT),
    )(r, lw, k, v, kk, a, *(ex.bufs if ex else []))
    return res[0], res[1], list(res[2:])


def _scan_bwd(r, lw, k, v, kk, a, s0s, do, ex=None, tb=128):
    T = r.shape[1]
    tb = min(tb, T)
    n_chunks = tb // SCAN_C
    nb = T // tb
    nx = ex.nb if ex else 0

    def body(*refs):
        r_ref, lw_ref, k_ref, v_ref, kk_ref, a_ref, s0_ref, do_ref = refs[:8]
        x_in, (dr, dlw, dk, dv, dkk, da), x_out = refs[8:8 + nx], refs[8 + nx:14 + nx], refs[14 + nx:14 + 2 * nx]
        ds_ref, sems = refs[14 + 2 * nx], refs[15 + 2 * nx:]

        @pl.when(pl.program_id(0) == 0)
        def _():
            ds_ref[...] = jnp.zeros_like(ds_ref)
            if ex:
                ex.start(x_in, x_out, sems)

        def step(j, carry):
            c = n_chunks - 1 - j
            sl = pl.ds(pl.multiple_of(c * SCAN_C, SCAN_C), SCAN_C)
            _, vjp = jax.vjp(_chunk_fn, s0_ref[c], r_ref[:, sl, :], lw_ref[:, sl, :], k_ref[:, sl, :],
                             v_ref[:, sl, :], kk_ref[:, sl, :], a_ref[:, sl, :])
            g = vjp((do_ref[:, sl, :], ds_ref[...]))
            ds_ref[...] = g[0]
            for ref, val in zip((dr, dlw, dk, dv, dkk, da), g[1:]):
                ref[:, sl, :] = val
            return carry

        lax.fori_loop(0, n_chunks, step, 0)

        if ex:
            @pl.when(pl.program_id(0) == nb - 1)
            def _():
                ex.wait(x_in, x_out, sems)

    hm = pl.BlockSpec((NH, tb, HN), lambda i: (0, nb - 1 - i, 0))
    res = pl.pallas_call(
        body, name="rwkv_scan_bwd", grid=(nb,),
        in_specs=[hm] * 6 + [pl.BlockSpec((n_chunks, NH, HN, HN), lambda i: (nb - 1 - i, 0, 0, 0)), hm]
        + (ex.any_specs if ex else []),
        out_specs=[hm] * 6 + (ex.any_specs if ex else []),
        out_shape=[jax.ShapeDtypeStruct((NH, T, HN), F32)] * 6 + (ex.out_shape if ex else []),
        scratch_shapes=[pltpu.VMEM((NH, HN, HN), F32)] + (ex.sem_shapes if ex else []),
        compiler_params=pltpu.CompilerParams(dimension_semantics=("arbitrary",), vmem_limit_bytes=VMEM_LIMIT),
    )(r, lw, k, v, kk, a, s0s, do, *(ex.bufs if ex else []))
    return list(res[:6]), list(res[6:])


def _shift_down(i, p, prev8):
    first = jnp.where(i > 0, prev8[7:8, :], 0.0)
    row = lax.broadcasted_iota(jnp.int32, p.shape, 0)
    return jnp.where(row == 0, first, pltpu.roll(p, 1, axis=0))


def _mix_fwd(p, sb, tm=256):
    def fn(i, n, p, prev8, sb):
        return (p * sb[0:1] + _shift_down(i, p, prev8) * sb[1:2],)
    return _rows_call("shift_mix_fwd", fn, [Rows(p), Halo(p, -1)], [sb], [("rows", p.shape[1], F32)], tm=tm,
                      with_pid=True)[0]


def _mix_bwd(dq, p, sb, tm=256):
    def fn(i, n, dq, next8, p, prev8, sb):
        ps = _shift_down(i, p, prev8)
        d1 = dq * sb[1:2]
        last = jnp.where(i < n - 1, next8[0:1, :] * sb[1:2], 0.0)
        row = lax.broadcasted_iota(jnp.int32, dq.shape, 0)
        up = jnp.where(row == dq.shape[0] - 1, last, pltpu.roll(d1, dq.shape[0] - 1, axis=0))
        return (dq * sb[0:1] + up, jnp.sum(dq * p, axis=0, keepdims=True), jnp.sum(dq * ps, axis=0, keepdims=True))
    w = p.shape[1]
    return _rows_call("shift_mix_bwd", fn, [Rows(dq), Halo(dq, +1), Rows(p), Halo(p, -1)], [sb], [("rows", w, F32)],
                      accs=[((1, w), F32), ((1, w), F32)], tm=tm, with_pid=True)


def _local_step(x, target, W, late_weights=None, early_grads=None):
    G = {}
    a = _rows_call("norm_mix_fwd", lambda x, g: (_rms(x, g),), [Rows(x)], [W["g_mix"]], [("rows", D, BF16)])[0]
    p_sgu = _matmul("proj_sgu", a, W["w_sgu_t"], "nt")
    p_rw = _matmul("proj_rwkv", a, W["w_rw_t"], "nt")
    p_gate = _matmul("proj_gate", a, W["w_gate_t"], "nt")

    sgu_consts = [W["sgu_ln_w"], W["sgu_ln_b"], W["sgu_w"], W["sgu_bt"]]
    s = _rows_call("sgu_fwd", lambda *t: (_sgu_fn(*t),), [Rows(p_sgu)], sgu_consts, [("rows", D, BF16)], tm=SGU_C)[0]

    q = _mix_fwd(p_rw, W["sb"])
    q_ins = [Rows(q, D, 0), Rows(q, D, 1), Rows(q, D, 2), Rows(q, 128, 24), Rows(q, 128, 25), Rows(q, 256, 13)]
    pre_consts = [W["w_lora"], W["w0"], W["a_lora"], W["a0"], W["g_lora"], W["k_k"], W["k_a"]]
    r_h, lw_h, k_h, v_h, kk_h, a_h, g_gate = _rows_call(
        "rwkv_pre_fwd", _pre_fn, q_ins, pre_consts, [("heads", F32)] * 6 + [("rows", D, F32)], tm=128)
    o_h, s0s, got = _scan_fwd(r_h, lw_h, k_h, v_h, kk_h, a_h, ex=late_weights[0] if late_weights else None)
    if late_weights:
        W = {**W, **late_weights[1](got)}
    y_a = _matmul("proj_a", s, W["w_proj_a"], "nn")
    post_ins = [Heads(o_h), Heads(r_h), Heads(k_h), Heads(v_h), Rows(g_gate)]
    post_consts = [W[n].reshape(NH, 1, HN) for n in ("ln_x_w", "ln_x_b", "r_k")]
    z_b = _rows_call("rwkv_post_fwd", lambda *t: (_post_fn(*t),), post_ins, post_consts, [("rows", D, BF16)], tm=128)[0]
    y_b, mixed = _matmul("proj_b", z_b, W["w_proj_b"], "nn", extras=[(p_gate, 0), (p_gate, 1), y_a],
                         epilogue=lambda yb, ga, gb, ya: (yb, _sigmoid(ga) * ya + _sigmoid(gb) * yb),
                         out_dtypes=(F32, BF16))
    gate_ins = [Rows(p_gate), Rows(y_a), Rows(y_b)]

    def res1(mo, x, g):
        h1 = x + mo
        return h1, _rms(h1, g)
    h1, f = _matmul("proj_out", mixed, W["w_out"], "nn", extras=[x, W["g_ffn"]], epilogue=res1,
                    out_dtypes=(F32, BF16))

    def relu_sq(u):
        r = jnp.maximum(u, 0.0)
        return r, r * r
    r1, act = _matmul("ffn_up", f, W["w_ffn1"], "nn", epilogue=relu_sq, out_dtypes=(BF16, BF16))
    ff = _matmul("ffn_down", act, W["w_ffn2"], "nn")

    def head(h1, ff, tgt, g):
        def f_(h1, ff, g):
            y = _rms(h1 + ff, g)
            return 0.5 * jnp.sum(jnp.mean(jnp.square(y - tgt), axis=-1))
        loss, (dh2, _, dg) = jax.value_and_grad(f_, argnums=(0, 1, 2))(h1, ff, g)
        return dh2, jnp.full((8, LANES), loss, F32), dg
    dh2, loss_acc, G["g_final"] = _rows_call("loss_head", head, [Rows(h1), Rows(ff), Rows(target)], [W["g_final"]],
                                             [("rows", D, F32)], accs=[((8, LANES), F32), ((1, D), F32)])

    d_u1 = _matmul("ffn_down_dx", dh2, W["w_ffn2"], "nt", extras=[r1], out_dtypes=(BF16,),
                   epilogue=lambda d_act, r: (d_act * 2.0 * r.astype(F32),))[0]
    G["w_ffn2"] = _matmul("ffn_down_dw", act, dh2, "tn", out_dtype=GRAD_PAYLOAD)
    d_f = _matmul("ffn_up_dx", d_u1, W["w_ffn1"], "nt")
    G["w_ffn1"] = _matmul("ffn_up_dw", f, d_u1, "tn", out_blocks=N_DEV, out_dtype=GRAD_PAYLOAD)

    def res1_bwd(h1, d_f, dh2, g):
        _, vjp = jax.vjp(_rms, h1, g)
        dh, dg = vjp(d_f)
        return dh2 + dh, dg
    dh1, G["g_ffn"] = _rows_call("residual_norm_bwd", res1_bwd, [Rows(h1), Rows(d_f), Rows(dh2)],
                                 [W["g_ffn"]], [("rows", D, F32)], accs=[((1, D), F32)])
    d_mixed = _matmul("proj_out_dx", dh1, W["w_out"], "nt")
    G["w_out"] = _matmul("proj_out_dw", mixed, dh1, "tn", out_dtype=GRAD_PAYLOAD)

    def gate_bwd(pg, ya, yb, dm):
        _, vjp = jax.vjp(_gate_fn, pg, ya, yb)
        return vjp(dm)
    d_gate, d_ya, d_yb = _rows_call("gate_bwd", gate_bwd, gate_ins + [Rows(d_mixed)], [],
                                    [("rows", 2 * D, F32), ("rows", D, BF16), ("rows", D, BF16)])

    d_s = _matmul("proj_a_dx", d_ya, W["w_proj_a"], "nt")
    G["w_proj_a"] = _matmul("proj_a_dw", s, d_ya, "tn", out_dtype=GRAD_PAYLOAD)

    def sgu_bwd(p, ds, *c):
        _, vjp = jax.vjp(_sgu_fn, p, *c)
        return vjp(ds)
    d_p_sgu, G["sgu_ln_w"], G["sgu_ln_b"], G["sgu_w"], G["sgu_bt"] = _rows_call(
        "sgu_bwd", sgu_bwd, [Rows(p_sgu), Rows(d_s)], sgu_consts, [("rows", 2 * D, F32)],
        accs=[((1, D), F32), ((1, D), F32), ((SGU_G, SGU_C, SGU_C), F32), ((SGU_C, SGU_G), F32)], tm=SGU_C)

    d_zb = _matmul("proj_b_dx", d_yb, W["w_proj_b"], "nt")
    G["w_proj_b"] = _matmul("proj_b_dw", z_b, d_yb, "tn", out_dtype=GRAD_PAYLOAD)

    def post_bwd(o, r, k2, v, g, dz, *c):
        _, vjp = jax.vjp(_post_fn, o, r, k2, v, g, *c)
        return vjp(dz)
    do_h, dr1, dk1, dv1, d_g, g_lnw, g_lnb, g_rk = _rows_call(
        "rwkv_post_bwd", post_bwd, post_ins + [Rows(d_zb)], post_consts, [("heads", F32)] * 4 + [("rows", D, F32)],
        accs=[((NH, 1, HN), F32)] * 3, tm=128)
    G["ln_x_w"], G["ln_x_b"], G["r_k"] = (t.reshape(1, D) for t in (g_lnw, g_lnb, g_rk))
    (dr2, dlw, dk2, dv2, dkk, daa), early = _scan_bwd(r_h, lw_h, k_h, v_h, kk_h, a_h, s0s, do_h,
                                                      ex=early_grads(G) if early_grads else None)

    def pre_bwd(qr, qk, qv, qxw, qxa, qxg, dr1, dr2, dlw, dk1, dk2, dv1, dv2, dkk, daa, dg, *c):
        _, vjp = jax.vjp(_pre_fn, qr, qk, qv, qxw, qxa, qxg, *c)
        g = vjp((dr1 + dr2, dlw, dk1 + dk2, dv1 + dv2, dkk, daa, dg))
        dq = jnp.concatenate(g[:6], axis=-1)
        return (dq,) + tuple(g[6:])
    pre_b_ins = q_ins + [Heads(dr1), Heads(dr2), Heads(dlw), Heads(dk1), Heads(dk2), Heads(dv1), Heads(dv2),
                         Heads(dkk), Heads(daa), Rows(d_g)]
    d_q, G["w_lora"], G["w0"], G["a_lora"], G["a0"], G["g_lora"], G["k_k"], G["k_a"] = _rows_call(
        "rwkv_pre_bwd", pre_bwd, pre_b_ins, pre_consts, [("rows", RW_INT, F32)],
        accs=[((128, D), F32), ((1, D), F32), ((128, D), F32), ((1, D), F32), ((256, D), F32), ((1, D), F32),
              ((1, D), F32)], tm=128)
    d_p_rw, dsb0, dsb1 = _mix_bwd(d_q, p_rw, W["sb"])
    G["sb"] = jnp.concatenate([dsb0, dsb1], axis=0)

    G["w_sgu_t"] = _matmul("proj_sgu_dw", d_p_sgu, a, "tn", out_dtype=GRAD_PAYLOAD)
    G["w_rw_t"] = _matmul("proj_rwkv_dw", d_p_rw, a, "tn", out_dtype=GRAD_PAYLOAD)
    G["w_gate_t"] = _matmul("proj_gate_dw", d_gate, a, "tn", out_dtype=GRAD_PAYLOAD)
    da1 = _matmul("proj_sgu_dx", d_p_sgu, W["w_sgu_t"], "nn")
    da2 = _matmul("proj_rwkv_dx", d_p_rw, W["w_rw_t"], "nn")
    da3 = _matmul("proj_gate_dx", d_gate, W["w_gate_t"], "nn")

    def norm1_bwd(x, da1, da2, da3, dh1, g):
        _, vjp = jax.vjp(_rms, x, g)
        dx, dg = vjp(da1 + da2 + da3)
        return dh1 + dx, dg
    dx, G["g_mix"] = _rows_call("norm_mix_bwd", norm1_bwd, [Rows(x), Rows(da1), Rows(da2), Rows(da3), Rows(dh1)],
                                [W["g_mix"]], [("rows", D, F32)], accs=[((1, D), F32)])
    return loss_acc[0, 0], dx, G, early


class Exchange:
    def __init__(self, bufs, gathers):
        self.bufs, self.gathers, self.nb = list(bufs), list(gathers), len(bufs)
        self.any_specs = [pl.BlockSpec(memory_space=pl.ANY)] * self.nb
        self.out_shape = [jax.ShapeDtypeStruct((N_DEV,) + (b.shape if g else b.shape[1:]), b.dtype)
                          for b, g in zip(self.bufs, self.gathers)]
        n = (N_DEV - 1) * self.nb
        self.sem_shapes = [pltpu.SemaphoreType.DMA((n,)), pltpu.SemaphoreType.DMA((n,)),
                           pltpu.SemaphoreType.DMA((self.nb,))]

    def _copies(self, in_refs, out_refs, sems):
        send_sems, recv_sems, local_sems = sems
        x, y, c = lax.axis_index("x"), lax.axis_index("y"), lax.axis_index("c")
        me = 4 * x + 2 * y + c

        def src(b, dest):
            return in_refs[b] if self.gathers[b] else in_refs[b].at[dest]

        local = [pltpu.make_async_copy(src(b, me), out_refs[b].at[me], local_sems.at[b]) for b in range(self.nb)]
        sends, recvs = [], []
        for kbits in range(1, N_DEV):
            px = 1 - x if kbits & 4 else x
            py = 1 - y if kbits & 2 else y
            pc = 1 - c if kbits & 1 else c
            peer = 4 * px + 2 * py + pc
            for b in range(self.nb):
                s = (kbits - 1) * self.nb + b
                sends.append(pltpu.make_async_remote_copy(
                    src_ref=src(b, peer), dst_ref=out_refs[b].at[me], send_sem=send_sems.at[s],
                    recv_sem=recv_sems.at[s], device_id=(px, py, pc), device_id_type=pl.DeviceIdType.MESH))
                recvs.append(pltpu.make_async_remote_copy(
                    src_ref=src(b, peer), dst_ref=out_refs[b].at[peer], send_sem=send_sems.at[s],
                    recv_sem=recv_sems.at[s], device_id=(px, py, pc), device_id_type=pl.DeviceIdType.MESH))
        return local, sends, recvs

    def start(self, in_refs, out_refs, sems):
        local, sends, _ = self._copies(in_refs, out_refs, sems)
        for cp in sends + local:
            cp.start()

    def wait(self, in_refs, out_refs, sems):
        local, sends, recvs = self._copies(in_refs, out_refs, sems)
        for cp in recvs:
            cp.wait_recv()
        for cp in sends:
            cp.wait_send()
        for cp in local:
            cp.wait()


def _exchange(name, bufs, gather):
    ex = Exchange(bufs, gather if isinstance(gather, (list, tuple)) else [gather] * len(bufs))

    def body(*refs):
        in_refs, out_refs, sems = refs[:ex.nb], refs[ex.nb:2 * ex.nb], refs[2 * ex.nb:]
        ex.start(in_refs, out_refs, sems)
        ex.wait(in_refs, out_refs, sems)

    return pl.pallas_call(body, name=name, in_specs=ex.any_specs, out_specs=ex.any_specs, out_shape=ex.out_shape,
                          scratch_shapes=ex.sem_shapes)(*ex.bufs)


N_CHIP = 4


def _pair_exchange(name, blocks):
    def body(b_ref, got_ref, send_sems, recv_sems):
        x, y, c = lax.axis_index("x"), lax.axis_index("y"), lax.axis_index("c")
        copies = [pltpu.make_async_remote_copy(
            src_ref=b_ref.at[2 * q + 1 - c], dst_ref=got_ref.at[q], send_sem=send_sems.at[q],
            recv_sem=recv_sems.at[q], device_id=(x, y, 1 - c), device_id_type=pl.DeviceIdType.MESH)
            for q in range(N_CHIP)]
        for cp in copies:
            cp.start()
        for cp in copies:
            cp.wait_recv()
        for cp in copies:
            cp.wait_send()

    any_spec = pl.BlockSpec(memory_space=pl.ANY)
    return pl.pallas_call(
        body, name=name, in_specs=[any_spec], out_specs=any_spec,
        out_shape=jax.ShapeDtypeStruct((N_CHIP,) + blocks.shape[1:], blocks.dtype),
        scratch_shapes=[pltpu.SemaphoreType.DMA((N_CHIP,))] * 2,
    )(blocks)


def _pair_sum(name, blocks, got, core):
    _, R, Wd = blocks.shape

    def body(c_ref, a_ref, b_ref, o_ref):
        o_ref[...] = (a_ref[...].astype(F32) + b_ref[...].astype(F32)).astype(o_ref.dtype)

    return pl.pallas_call(
        body, name=name,
        grid_spec=pltpu.PrefetchScalarGridSpec(
            num_scalar_prefetch=1, grid=(N_CHIP,),
            in_specs=[pl.BlockSpec((None, R, Wd), lambda q, c_ref: (2 * q + c_ref[0], 0, 0)),
                      pl.BlockSpec((None, R, Wd), lambda q, c_ref: (q, 0, 0))],
            out_specs=pl.BlockSpec((None, R, Wd), lambda q, c_ref: (q, 0, 0))),
        out_shape=jax.ShapeDtypeStruct(got.shape, blocks.dtype),
        compiler_params=pltpu.CompilerParams(dimension_semantics=("parallel",), vmem_limit_bytes=VMEM_LIMIT),
    )(core, blocks, got)


def _chip_exchange(name, sums, ex):
    def body(*refs):
        s_ref, x_in = refs[0], refs[1:1 + ex.nb]
        slots_ref, x_out = refs[1 + ex.nb], refs[2 + ex.nb:2 + 2 * ex.nb]
        send_sems, recv_sems = refs[2 + 2 * ex.nb:4 + 2 * ex.nb]
        ex_sems = refs[4 + 2 * ex.nb:]
        x, y, c = lax.axis_index("x"), lax.axis_index("y"), lax.axis_index("c")
        my_q = 2 * x + y
        sends, recvs = [], []
        for kbits in range(1, N_CHIP):
            px = 1 - x if kbits & 2 else x
            py = 1 - y if kbits & 1 else y
            peer_q = 2 * px + py
            sends.append(pltpu.make_async_remote_copy(
                src_ref=s_ref.at[peer_q], dst_ref=slots_ref.at[my_q], send_sem=send_sems.at[kbits - 1],
                recv_sem=recv_sems.at[kbits - 1], device_id=(px, py, c), device_id_type=pl.DeviceIdType.MESH))
            recvs.append(pltpu.make_async_remote_copy(
                src_ref=s_ref.at[peer_q], dst_ref=slots_ref.at[peer_q], send_sem=send_sems.at[kbits - 1],
                recv_sem=recv_sems.at[kbits - 1], device_id=(px, py, c), device_id_type=pl.DeviceIdType.MESH))
        for cp in sends:
            cp.start()
        ex.start(x_in, x_out, ex_sems)
        for cp in recvs:
            cp.wait_recv()
        for cp in sends:
            cp.wait_send()
        ex.wait(x_in, x_out, ex_sems)

    any_spec = pl.BlockSpec(memory_space=pl.ANY)
    res = pl.pallas_call(
        body, name=name, in_specs=[any_spec] + ex.any_specs, out_specs=[any_spec] + ex.any_specs,
        out_shape=[jax.ShapeDtypeStruct(sums.shape, sums.dtype)] + ex.out_shape,
        scratch_shapes=[pltpu.SemaphoreType.DMA((N_CHIP - 1,)), pltpu.SemaphoreType.DMA((N_CHIP - 1,))]
        + ex.sem_shapes,
    )(sums, *ex.bufs)
    return res[0], list(res[1:])


def _adamw(name, slots, w, m, v, tr=256):
    R, Wd = w.shape[-2:]
    depth_axis = w.ndim == 3
    if R % tr == 0:
        tc = Wd
    else:
        tr, tc = R, (256 if (Wd % 256 == 0 and R > 256) else Wd)

    def body(s_ref, w_ref, m_ref, v_ref, g_out, d_out, m_out, v_out):
        g = s_ref[0].astype(F32)
        for j in range(1, slots.shape[0]):
            g = g + s_ref[j].astype(F32)
        m_new = ADAM_B1 * m_ref[...] + (1.0 - ADAM_B1) * g
        v_new = ADAM_B2 * v_ref[...] + (1.0 - ADAM_B2) * jnp.square(g)
        m_hat = m_new / (1.0 - ADAM_B1 ** ADAM_STEP)
        v_hat = v_new / (1.0 - ADAM_B2 ** ADAM_STEP)
        g_out[...] = g
        d_out[...] = -ADAM_LR * (m_hat / (jnp.sqrt(v_hat) + ADAM_EPS) + ADAM_WD * w_ref[...])
        m_out[...] = m_new
        v_out[...] = v_new

    if depth_axis:
        row = pl.BlockSpec((None, tr, tc), lambda i, j: (0, i, j))
    else:
        row = pl.BlockSpec((tr, tc), lambda i, j: (i, j))
    return pl.pallas_call(
        body, name=name, grid=(R // tr, Wd // tc),
        in_specs=[pl.BlockSpec((slots.shape[0], tr, tc), lambda i, j: (0, i, j)), row, row, row],
        out_specs=[row] * 4, out_shape=[jax.ShapeDtypeStruct(w.shape, F32)] * 4,
        compiler_params=pltpu.CompilerParams(dimension_semantics=("parallel", "parallel"),
                                             vmem_limit_bytes=VMEM_LIMIT),
    )(slots, w, m, v)


PACK_W = 1024
PACKED = [(n, s) for n, s in REPLICATED if n != "sgu_w"]
_SMALL_SIZES = [int(np.prod(s)) for _, s in PACKED]
_SMALL_ROWS = _round_up(_round_up(sum(_SMALL_SIZES) + PACK_W, PACK_W) // PACK_W, 8)
_LOSS_AT = sum(_SMALL_SIZES)
W_IN_SHARD = P_TOTAL // N_DEV


def _pack_rows(parts, rows, dtype):
    flat = jnp.concatenate([p.reshape(-1).astype(dtype) for p in parts])
    return jnp.pad(flat, (0, rows * PACK_W - flat.shape[0])).reshape(rows, PACK_W)


def _w_in_groups_t(blocks):
    wt = blocks.reshape(P_TOTAL, D)
    o, c = 2 * D, 2 * D + 3 * D
    z = lambda r: jnp.zeros((r, D), wt.dtype)
    rw = jnp.concatenate([wt[o:c], wt[c:c + L_W], z(128 - L_W), wt[c + L_W:c + L_W + L_A], z(128 - L_A),
                          wt[c + L_W + L_A:o + C_B], z(256 - L_G)], axis=0)
    return wt[:o], rw, wt[o + C_B:]


def _w_in_grad_blocks(g_sgu_t, g_rw_t, g_gate_t):
    c = 3 * D
    full = jnp.concatenate([g_sgu_t, g_rw_t[:c], g_rw_t[c:c + L_W], g_rw_t[c + 128:c + 128 + L_A],
                            g_rw_t[c + 256:c + 256 + L_G], g_gate_t], axis=0)
    return full.reshape(N_DEV, W_IN_SHARD, D)


def _mesh_index():
    me = 4 * lax.axis_index("x") + 2 * lax.axis_index("y") + lax.axis_index("c")
    return me.astype(jnp.int32).reshape(1)


def _fill_slot(name, dst, src, idx, src_idx=None):
    R, Wd = dst.shape[1:]
    scalars = [idx] if src_idx is None else [idx, src_idx]
    if src_idx is None:
        src_spec = pl.BlockSpec((R, Wd), lambda i, *s: (0, 0))
    else:
        src_spec = pl.BlockSpec((None, R, Wd), lambda i, *s: (s[1][0], 0, 0))

    def body(*refs):
        src_ref, out_ref = refs[len(scalars) + 1], refs[len(scalars) + 2]
        out_ref[...] = src_ref[...]

    return pl.pallas_call(
        body, name=name,
        grid_spec=pltpu.PrefetchScalarGridSpec(
            num_scalar_prefetch=len(scalars), grid=(1,),
            in_specs=[pl.BlockSpec(memory_space=pl.ANY), src_spec],
            out_specs=pl.BlockSpec((None, R, Wd), lambda i, *s: (s[0][0], 0, 0))),
        out_shape=jax.ShapeDtypeStruct(dst.shape, dst.dtype),
        input_output_aliases={len(scalars): 0},
        compiler_params=pltpu.CompilerParams(vmem_limit_bytes=VMEM_LIMIT),
    )(*scalars, dst, src)


def _all_gather_two_level(name, bufs, skip_own=()):
    nb = len(bufs)

    def body(*refs):
        in_refs, out_refs = refs[:nb], refs[nb:2 * nb]
        send_sems, recv_sems, local_sems = refs[2 * nb:]
        x, y, c = lax.axis_index("x"), lax.axis_index("y"), lax.axis_index("c")
        me, sibling = (x, y, c), (x, y, 1 - c)
        chips = [(1 - x, y), (x, 1 - y), (1 - x, 1 - y)]

        def slot(b, dev):
            return out_refs[b].at[4 * dev[0] + 2 * dev[1] + dev[2]]

        def copy(b, k, block, to, own=False):
            return pltpu.make_async_remote_copy(
                src_ref=in_refs[b] if own else slot(b, block), dst_ref=slot(b, block),
                send_sem=send_sems.at[7 * b + k], recv_sem=recv_sems.at[7 * b + k], device_id=to,
                device_id_type=pl.DeviceIdType.MESH)

        local = [pltpu.make_async_copy(in_refs[b], slot(b, me), local_sems.at[b]) for b in range(nb)
                 if b not in skip_own]
        first = [copy(b, 0, me, sibling, own=True) for b in range(nb)]
        first += [copy(b, 1 + j, me, (*chip, c), own=True) for j, chip in enumerate(chips) for b in range(nb)]
        for cp in first + local:
            cp.start()
        passed = []
        for j, chip in enumerate(chips):
            for b in range(nb):
                copy(b, 1 + j, (*chip, c), me).wait_recv()
                passed.append(copy(b, 4 + j, (*chip, c), sibling))
                passed[-1].start()
        for b in range(nb):
            copy(b, 0, sibling, me).wait_recv()
        for j, chip in enumerate(chips):
            for b in range(nb):
                copy(b, 4 + j, (*chip, 1 - c), me).wait_recv()
        for cp in first + passed:
            cp.wait_send()
        for cp in local:
            cp.wait()

    any_spec = pl.BlockSpec(memory_space=pl.ANY)
    return pl.pallas_call(
        body, name=name, in_specs=[any_spec] * nb, out_specs=[any_spec] * nb,
        out_shape=[jax.ShapeDtypeStruct((N_DEV,) + b.shape, b.dtype) for b in bufs],
        scratch_shapes=[pltpu.SemaphoreType.DMA((7 * nb,)), pltpu.SemaphoreType.DMA((7 * nb,)),
                        pltpu.SemaphoreType.DMA((nb,))],
    )(*bufs)


def _cols_from_blocks(blk):
    return jnp.transpose(blk, (1, 0, 2)).reshape(blk.shape[1], -1)


def _cols_to_blocks(g):
    r, c = g.shape
    return jnp.transpose(g.reshape(r, N_DEV, c // N_DEV), (1, 0, 2))


FIRST_WEIGHTS = ["w_in", "shift_b", "w_lora_w", "a_lora_w", "g_lora_w"]
LATE_WEIGHTS = ["w_proj_a", "w_proj_b", "w_out", "w_ffn1", "w_ffn2"]


def _late_weights(shards):
    ex = Exchange([shards[n].astype(BF16) for n in LATE_WEIGHTS], [True] * len(LATE_WEIGHTS))

    def finish(results):
        got = dict(zip(LATE_WEIGHTS, results))
        W = {n: got[n].reshape(-1, D) for n in ("w_proj_a", "w_proj_b", "w_out", "w_ffn2")}
        W["w_ffn1"] = got["w_ffn1"].reshape(N_DEV, D, -1)
        return W
    return ex, finish


def _gather_weights(shards):
    def payload(n):
        if n == "w_in":
            return jnp.transpose(shards[n][0]).astype(BF16)
        return shards[n] if n == "shift_b" else shards[n].astype(BF16)
    payloads = [payload(n) for n in FIRST_WEIGHTS]
    got = dict(zip(FIRST_WEIGHTS, _all_gather_two_level("weight_all_gather", payloads, skip_own=(0,))))
    got["w_in"] = _fill_slot("w_in_own_slot", got["w_in"], payloads[0], _mesh_index())
    W = {}
    W["w_sgu_t"], W["w_rw_t"], W["w_gate_t"] = _w_in_groups_t(got["w_in"])
    z = lambda r, c, dt: jnp.zeros((r, c), dt)
    W["w_lora"] = jnp.concatenate([_cols_from_blocks(got["w_lora_w"][:, 0]).astype(F32), z(128 - L_W, D, F32)], axis=0)
    W["a_lora"] = jnp.concatenate([_cols_from_blocks(got["a_lora_w"][:, 0]).astype(F32), z(128 - L_A, D, F32)], axis=0)
    W["g_lora"] = jnp.concatenate([_cols_from_blocks(got["g_lora_w"][:, 0]).astype(F32), z(256 - L_G, D, F32)], axis=0)
    sb = _cols_from_blocks(got["shift_b"][:, 0])
    W["sb"] = jnp.concatenate([sb[:, :3 * D], sb[:, 3 * D:3 * D + L_W], z(2, 128 - L_W, F32),
                               sb[:, 3 * D + L_W:3 * D + L_W + L_A], z(2, 128 - L_A, F32),
                               sb[:, 3 * D + L_W + L_A:], z(2, 256 - L_G, F32)], axis=1)
    return W


def _replicated_weights(rep):
    W = {n: rep[n] for n in ("g_mix", "sgu_ln_w", "sgu_ln_b", "w0", "a0", "k_k", "k_a", "r_k", "ln_x_w", "ln_x_b",
                             "g_ffn")}
    W["g_final"] = rep["g_final"].reshape(1, D)
    W["sgu_w"] = rep["sgu_w"][0]
    W["sgu_bt"] = jnp.transpose(rep["sgu_b"][0])
    return W


def _late_grad_blocks(G):
    blocks = {n: G[n].reshape(N_DEV, -1, D) for n in ("w_proj_a", "w_proj_b", "w_out", "w_ffn2")}
    blocks["w_ffn1"] = G["w_ffn1"]
    return Exchange([blocks[n] for n in LATE_WEIGHTS] + [G["sgu_w"].reshape(SGU_G * SGU_C, SGU_C)],
                    [False] * len(LATE_WEIGHTS) + [True])


def _first_grad_blocks(G):
    sbg = G["sb"]
    c = 3 * D
    sb = jnp.concatenate([sbg[:, :c], sbg[:, c:c + L_W], sbg[:, c + 128:c + 128 + L_A],
                          sbg[:, c + 256:c + 256 + L_G]], axis=1)
    return {
        "w_in": _w_in_grad_blocks(G["w_sgu_t"], G["w_rw_t"], G["w_gate_t"]),
        "shift_b": _cols_to_blocks(sb),
        "w_lora_w": _cols_to_blocks(G["w_lora"][:L_W]), "a_lora_w": _cols_to_blocks(G["a_lora"][:L_A]),
        "g_lora_w": _cols_to_blocks(G["g_lora"][:L_G]),
    }


def _replicated_grads(G):
    small = {n: G[n] for n in ("g_mix", "sgu_ln_w", "sgu_ln_b", "w0", "a0", "k_k", "k_a", "r_k", "ln_x_w", "ln_x_b",
                               "g_ffn", "g_final")}
    small["sgu_w"] = G["sgu_w"]
    small["sgu_b"] = jnp.transpose(G["sgu_bt"])
    return small


def kernel(x, g_mix, w_in, sgu_ln_w, sgu_ln_b, sgu_w, sgu_b, w_proj_a, shift_b, w_lora_w, w0, a_lora_w, a0, g_lora_w, k_k, k_a, r_k, ln_x_w, ln_x_b, w_proj_b, w_out, g_ffn, w_ffn1, w_ffn2, g_final, loss_target, m_g_mix, m_w_in, m_sgu_ln_w, m_sgu_ln_b, m_sgu_w, m_sgu_b, m_w_proj_a, m_shift_b, m_w_lora_w, m_w0, m_a_lora_w, m_a0, m_g_lora_w, m_k_k, m_k_a, m_r_k, m_ln_x_w, m_ln_x_b, m_w_proj_b, m_w_out, m_g_ffn, m_w_ffn1, m_w_ffn2, m_g_final, v_g_mix, v_w_in, v_sgu_ln_w, v_sgu_ln_b, v_sgu_w, v_sgu_b, v_w_proj_a, v_shift_b, v_w_lora_w, v_w0, v_a_lora_w, v_a0, v_g_lora_w, v_k_k, v_k_a, v_r_k, v_ln_x_w, v_ln_x_b, v_w_proj_b, v_w_out, v_g_ffn, v_w_ffn1, v_w_ffn2, v_g_final):
    env = dict(locals())
    weights = {n: env[n] for n in WEIGHT_ORDER}
    moms = {n: env["m_" + n] for n in WEIGHT_ORDER}
    vars_ = {n: env["v_" + n] for n in WEIGHT_ORDER}

    shards = {n: weights[n] for n, _, _ in SHARDED}
    W = _gather_weights(shards)
    W.update(_replicated_weights({n: weights[n] for n, _ in REPLICATED}))
    loss_part, dx, G, late_slots = _local_step(x[0], loss_target[0], W, late_weights=_late_weights(shards),
                                               early_grads=_late_grad_blocks)

    slots = dict(zip(LATE_WEIGHTS, late_slots))
    blocks = _first_grad_blocks(G)
    small = _replicated_grads(G)
    small_parts = [small[n] for n, _ in PACKED] + [jnp.full((PACK_W,), loss_part, F32)]
    got = _pair_exchange("grad_pair_exchange", blocks["w_in"])
    core = lax.axis_index("c").astype(jnp.int32).reshape(1)
    rest = [n for n in FIRST_WEIGHTS if n != "w_in"]
    ex = Exchange([blocks[n] for n in rest] + [_pack_rows(small_parts, _SMALL_ROWS, F32)], [False] * len(rest) + [True])
    sums = _pair_sum("grad_pair_sum", blocks["w_in"], got, core)
    chip_slots, res = _chip_exchange("grad_exchange", sums, ex)
    my_chip = (2 * lax.axis_index("x") + lax.axis_index("y")).astype(jnp.int32).reshape(1)
    slots["w_in"] = _fill_slot("grad_own_slot", chip_slots, sums, my_chip, src_idx=my_chip)
    slots.update(zip(rest, res[:-1]))
    small_slots = res[-1]

    outs = [dict(), dict(), dict(), dict()]
    for n, _, _ in SHARDED:
        if n == "w_in":
            res = _adamw("adamw_" + n, slots[n], *[jnp.transpose(t[0]) for t in (weights[n], moms[n], vars_[n])])
            res = [jnp.transpose(t)[None] for t in res]
        else:
            res = _adamw("adamw_" + n, slots[n], weights[n], moms[n], vars_[n])
        for k in range(4):
            outs[k][n] = res[k]

    sgu_shape = (SGU_G * SGU_C, SGU_C)
    res = _adamw("adamw_sgu_w", late_slots[len(LATE_WEIGHTS)], *[t.reshape(sgu_shape) for t in
                                                                  (weights["sgu_w"], moms["sgu_w"], vars_["sgu_w"])])
    for k in range(4):
        outs[k]["sgu_w"] = res[k].reshape(weights["sgu_w"].shape)

    def packed(d):
        return _pack_rows([d[n] for n, _ in PACKED], _SMALL_ROWS, F32)
    small_out = _adamw("adamw_replicated", small_slots, packed(weights), packed(moms), packed(vars_))
    for k in range(4):
        flat = small_out[k].reshape(-1)
        off = 0
        for (n, s), size in zip(PACKED, _SMALL_SIZES):
            outs[k][n] = flat[off:off + size].reshape(s)
            off += size
    loss = small_out[0].reshape(-1)[_LOSS_AT]
    return (loss, dx[None], *[outs[0][n] for n in WEIGHT_ORDER], *[outs[1][n] for n in WEIGHT_ORDER],
            *[outs[2][n] for n in WEIGHT_ORDER], *[outs[3][n] for n in WEIGHT_ORDER])
```

```python
import functools
import numpy as np
import jax
import jax.numpy as jnp
from jax import lax
from jax.experimental import pallas as pl
from jax.experimental.pallas import tpu as pltpu

F32 = jnp.float32
BF16 = jnp.bfloat16

D = 1024
NH, HN = 16, 64
SGU_G, SGU_C = 8, 128
L_W, L_A, L_G = 64, 64, 160
C_B = 3 * D + L_W + L_A + L_G
P_TOTAL = 2 * D + C_B + 2 * D
D_FF = 4 * D
RW_INT = 3 * D + 128 + 128 + 256
NORM_EPS, LN_EPS, GN_EPS = 1e-6, 1e-5, 64e-5
N_DEV = 8
LANES = 128
SCAN_C = 64
SOLVE_B = 16
SCAN_PRECISION = lax.Precision.HIGH
SCAN_OUT_PRECISION = lax.Precision.DEFAULT
GRAD_PAYLOAD = BF16
VMEM_LIMIT = 56 * 1024 * 1024
MATMUL_VMEM_BUDGET = 40 * 1024 * 1024
STEP_COST_BYTES = 512 * 1024
HBM_COST_RATIO = 3

ADAM_LR, ADAM_B1, ADAM_B2, ADAM_EPS, ADAM_WD, ADAM_STEP = 0.001, 0.9, 0.999, 1e-08, 0.01, 10

SHARDED = [
    ("w_in", (D, P_TOTAL), 1), ("w_proj_a", (D, D), 0), ("shift_b", (2, C_B), 1), ("w_lora_w", (L_W, D), 1),
    ("a_lora_w", (L_A, D), 1), ("g_lora_w", (L_G, D), 1), ("w_proj_b", (D, D), 0), ("w_out", (D, D), 0),
    ("w_ffn1", (D, D_FF), 1), ("w_ffn2", (D_FF, D), 0),
]
REPLICATED = [
    ("g_mix", (1, D)), ("sgu_ln_w", (1, D)), ("sgu_ln_b", (1, D)), ("sgu_w", (1, SGU_G, SGU_C, SGU_C)),
    ("sgu_b", (1, SGU_G, SGU_C)), ("w0", (1, D)), ("a0", (1, D)), ("k_k", (1, D)), ("k_a", (1, D)), ("r_k", (1, D)),
    ("ln_x_w", (1, D)), ("ln_x_b", (1, D)), ("g_ffn", (1, D)), ("g_final", (D,)),
]
WEIGHT_ORDER = ["g_mix", "w_in", "sgu_ln_w", "sgu_ln_b", "sgu_w", "sgu_b", "w_proj_a", "shift_b", "w_lora_w", "w0",
                "a_lora_w", "a0", "g_lora_w", "k_k", "k_a", "r_k", "ln_x_w", "ln_x_b", "w_proj_b", "w_out", "g_ffn",
                "w_ffn1", "w_ffn2", "g_final"]


def _shard_shape(shape, axis):
    s = list(shape)
    s[axis] //= N_DEV
    return tuple(s)


def _round_up(n, m):
    return (n + m - 1) // m * m


def _pick(n, target):
    if n <= target:
        return n
    best = None
    for t in range(LANES, target + 1, LANES):
        if n % t == 0:
            best = t
    assert best is not None, (n, target)
    return best


def _matmul(name, a, b, mode, out_dtype=F32, tm=2048, tn=1024, tk=2048, out_blocks=None, epilogue=None, extras=(),
            out_dtypes=()):
    b_blocks = b.shape[0] if b.ndim == 3 else None
    bshape = b.shape if b.ndim == 2 else (b.shape[1], b.shape[0] * b.shape[2])
    if mode == "nn":
        (M, K), (K2, N) = a.shape, bshape
    elif mode == "nt":
        (M, K), (N, K2) = a.shape, bshape
    else:
        (K, M), (K2, N) = a.shape, bshape
    assert K == K2, (name, a.shape, b.shape)
    assert b_blocks is None or mode != "tn"
    assert out_blocks is None or mode == "tn"
    tn = min(tn, N // (out_blocks or 1), bshape[1] // b_blocks if (b_blocks and mode == "nn") else tn)
    tk = min(tk, bshape[1] // b_blocks if (b_blocks and mode == "nt") else tk)
    tm, tn, tk = _pick(M, tm), _pick(N, tn), _pick(K, tk)

    def vmem_bytes(tm, tk):
        tiles = tm * tk * a.dtype.itemsize + tk * tn * b.dtype.itemsize
        for dt in (out_dtypes if epilogue else (out_dtype,)):
            tiles += tm * tn * jnp.dtype(dt).itemsize
        for x in extras:
            arr = x[0] if isinstance(x, tuple) else x
            tiles += (tm if arr.shape[0] > 1 else 1) * tn * arr.dtype.itemsize
        return 2 * tiles + (tm * tn * 4 if K // tk > 1 else 0)

    def cost(tm, tk):
        ni, nj, nk = M // tm, N // tn, K // tk
        steps = ni * nj * nk
        acc_passes = steps * tm * tn * 8 if nk > 1 else 0
        a_reads = M * K * a.dtype.itemsize * (nj if nk > 1 else 1)
        b_reads = K * N * b.dtype.itemsize * (ni if (nj > 1 or nk > 1) else 1)
        return (steps * STEP_COST_BYTES + acc_passes + vmem_bytes(tm, tk) // 2
                + HBM_COST_RATIO * (a_reads + b_reads))

    options = [(m, k) for m in {_pick(M, max(t, LANES)) for t in (tm, tm // 2, tm // 4)}
               for k in {_pick(K, max(t, LANES)) for t in (tk, tk // 2, tk // 4)}
               if vmem_bytes(m, k) <= MATMUL_VMEM_BUDGET]
    tm, tk = min(options, key=lambda o: cost(*o))
    nk = K // tk
    dims = {"nn": (((1,), (0,)), ((), ())), "nt": (((1,), (1,)), ((), ())), "tn": (((0,), (0,)), ((), ()))}[mode]

    n_x, n_o = len(extras), len(out_dtypes) if epilogue else 1

    def body(a_ref, b_ref, *rest):
        x_refs, o_refs, acc = rest[:n_x], rest[n_x:n_x + n_o], rest[n_x + n_o:]
        part = lax.dot_general(a_ref[...].astype(BF16), b_ref[...].astype(BF16), dims, preferred_element_type=F32)

        def finish(res):
            outs = epilogue(res, *[r[...] for r in x_refs]) if epilogue else (res,)
            for r, v in zip(o_refs, outs):
                r[...] = v.astype(r.dtype)

        if nk == 1:
            finish(part)
            return
        acc_ref, k = acc[0], pl.program_id(2)

        @pl.when(k == 0)
        def _():
            acc_ref[...] = part

        @pl.when(k > 0)
        def _():
            acc_ref[...] += part

        @pl.when(k == nk - 1)
        def _():
            finish(acc_ref[...])

    a_spec = {"nn": pl.BlockSpec((tm, tk), lambda i, j, k: (i, k)), "nt": pl.BlockSpec((tm, tk), lambda i, j, k: (i, k)),
              "tn": pl.BlockSpec((tk, tm), lambda i, j, k: (k, i))}[mode]
    b_spec = {"nn": pl.BlockSpec((tk, tn), lambda i, j, k: (k, j)), "nt": pl.BlockSpec((tn, tk), lambda i, j, k: (j, k)),
              "tn": pl.BlockSpec((tk, tn), lambda i, j, k: (k, j))}[mode]
    if b_blocks and mode == "nn":
        per = b.shape[2] // tn
        b_spec = pl.BlockSpec((None, tk, tn), lambda i, j, k: (j // per, k, j % per))
    elif b_blocks:
        per = b.shape[2] // tk
        b_spec = pl.BlockSpec((None, tn, tk), lambda i, j, k: (k // per, j, k % per))
    out_spec = pl.BlockSpec((tm, tn), lambda i, j, k: (i, j))
    out_shape = jax.ShapeDtypeStruct((M, N), out_dtype)
    if out_blocks:
        per_o = N // out_blocks // tn
        out_spec = pl.BlockSpec((None, tm, tn), lambda i, j, k: (j // per_o, i, j % per_o))
        out_shape = jax.ShapeDtypeStruct((out_blocks, M, N // out_blocks), out_dtype)
    x_specs, x_args = [], []
    for x in extras:
        arr, off = x if isinstance(x, tuple) else (x, 0)
        if arr.shape[0] == 1:
            x_specs.append(pl.BlockSpec((1, tn), lambda i, j, k: (0, j)))
        else:
            x_specs.append(pl.BlockSpec((tm, tn), lambda i, j, k, off=off: (i, j + off)))
        x_args.append(arr)
    res = pl.pallas_call(
        body, name=name, grid=(M // tm, N // tn, nk), in_specs=[a_spec, b_spec] + x_specs,
        out_specs=[out_spec] * n_o if epilogue else out_spec,
        out_shape=[jax.ShapeDtypeStruct((M, N), dt) for dt in out_dtypes] if epilogue else out_shape,
        scratch_shapes=[pltpu.VMEM((tm, tn), F32)] if nk > 1 else [],
        compiler_params=pltpu.CompilerParams(dimension_semantics=("parallel", "parallel", "arbitrary"),
                                             vmem_limit_bytes=VMEM_LIMIT),
    )(a, b, *x_args)
    return res


class Rows:
    def __init__(self, arr, width=None, cb=0):
        self.arr, self.width, self.cb = arr, (arr.shape[1] if width is None else width), cb


class Heads:
    def __init__(self, arr):
        self.arr = arr


class Halo:
    def __init__(self, arr, side):
        self.arr, self.side = arr, side


def _rows_call(name, fn, ins, consts, outs, accs=(), tm=256, with_pid=False):
    T = next(o.arr.shape[1] if isinstance(o, Heads) else o.arr.shape[0] for o in ins if not isinstance(o, Halo))
    tm = min(tm, T)
    n_tiles = T // tm
    n_in, n_c, n_out = len(ins), len(consts), len(outs)
    in_specs, args = [], []
    for o in ins:
        if isinstance(o, Rows):
            in_specs.append(pl.BlockSpec((tm, o.width), lambda i, cb=o.cb: (i, cb)))
        elif isinstance(o, Heads):
            in_specs.append(pl.BlockSpec((NH, tm, HN), lambda i: (0, i, 0)))
        else:
            w = o.arr.shape[1]
            if o.side < 0:
                in_specs.append(pl.BlockSpec((8, w), lambda i: (jnp.maximum(i * (tm // 8) - 1, 0), 0)))
            else:
                in_specs.append(pl.BlockSpec((8, w), lambda i: (jnp.minimum((i + 1) * (tm // 8), T // 8 - 1), 0)))
        args.append(o.arr)
    for c in consts:
        in_specs.append(pl.BlockSpec(c.shape, lambda i, nd=c.ndim: (0,) * nd))
        args.append(c)
    out_specs, out_shape = [], []
    for o in outs:
        if o[0] == "rows":
            out_specs.append(pl.BlockSpec((tm, o[1]), lambda i: (i, 0)))
            out_shape.append(jax.ShapeDtypeStruct((T, o[1]), o[2]))
        else:
            out_specs.append(pl.BlockSpec((NH, tm, HN), lambda i: (0, i, 0)))
            out_shape.append(jax.ShapeDtypeStruct((NH, T, HN), o[1]))
    for shape, dt in accs:
        out_specs.append(pl.BlockSpec(shape, lambda i, nd=len(shape): (0,) * nd))
        out_shape.append(jax.ShapeDtypeStruct(shape, dt))

    def body(*refs):
        i = pl.program_id(0)
        vals = []
        vals = [r[...] for r in refs[:n_in + n_c]]
        res = fn(i, n_tiles, *vals) if with_pid else fn(*vals)
        out_refs = refs[n_in + n_c:]
        for r, v in zip(out_refs[:n_out], res[:n_out]):
            r[...] = v.astype(r.dtype)
        if accs:
            @pl.when(i == 0)
            def _():
                for r in out_refs[n_out:]:
                    r[...] = jnp.zeros_like(r)

            for r, v in zip(out_refs[n_out:], res[n_out:]):
                r[...] += v.astype(r.dtype)

    res = pl.pallas_call(
        body, name=name, grid=(n_tiles,), in_specs=in_specs, out_specs=out_specs, out_shape=out_shape,
        compiler_params=pltpu.CompilerParams(dimension_semantics=("arbitrary",), vmem_limit_bytes=VMEM_LIMIT),
    )(*args)
    return res


def _rms(x, g):
    return x * lax.rsqrt(jnp.mean(x * x, axis=-1, keepdims=True) + NORM_EPS) * g


def _gelu(x):
    return 0.5 * x * (1.0 + lax.erf(x * 0.7071067811865476))


def _sigmoid(x):
    return 1.0 / (1.0 + jnp.exp(-x))


def _bdot(a, b):
    return jnp.dot(a.astype(BF16), b.astype(BF16), preferred_element_type=F32)


def _to_heads(x):
    return jnp.concatenate([x[:, h * HN:(h + 1) * HN][None] for h in range(NH)], axis=0)


def _from_heads(xh):
    return jnp.concatenate([xh[h] for h in range(NH)], axis=-1)


def _sgu_fn(p, ln_w, ln_b, sw, sbt):
    z = _gelu(p)
    u, v = z[:, :D], z[:, D:]
    mu = jnp.mean(v, axis=-1, keepdims=True)
    var = jnp.mean(jnp.square(v - mu), axis=-1, keepdims=True)
    vn = (v - mu) * lax.rsqrt(var + LN_EPS) * ln_w + ln_b
    ri = lax.broadcasted_iota(jnp.int32, (SGU_C, SGU_C), 0)
    ci = lax.broadcasted_iota(jnp.int32, (SGU_C, SGU_C), 1)
    mask = (ci <= ri).astype(F32)
    dg = D // SGU_G
    parts = []
    for g in range(SGU_G):
        parts.append(_bdot(sw[g] * mask, vn[:, g * dg:(g + 1) * dg]) + sbt[:, g:g + 1])
    return u * jnp.concatenate(parts, axis=-1)


def _pre_fn(qr, qk, qv, qxw, qxa, qxg, wl, w0, al, a0, gl, k_k, k_a):
    w = -jax.nn.softplus(-(w0 + _bdot(jnp.tanh(qxw), wl))) - 0.5
    lw = -jnp.exp(w)
    aa = _sigmoid(a0 + _bdot(qxa, al))
    g = _bdot(_sigmoid(qxg), gl)
    kk = _to_heads(qk * k_k)
    kk = kk / jnp.maximum(jnp.sqrt(jnp.sum(kk * kk, axis=-1, keepdims=True)), 1e-12)
    k2 = qk * (1.0 + (aa - 1.0) * k_a)
    return _to_heads(qr), _to_heads(lw), _to_heads(k2), _to_heads(qv), kk, _to_heads(aa), g


def _post_fn(o, r, k2, v, g, ln_w, ln_b, r_k):
    mu = jnp.mean(o, axis=-1, keepdims=True)
    d = o - mu
    var = jnp.mean(d * d, axis=-1, keepdims=True)
    on = d * lax.rsqrt(var + GN_EPS) * ln_w + ln_b
    bonus = jnp.sum(r * k2 * r_k, axis=-1, keepdims=True) * v
    return _from_heads(on + bonus) * g


def _gate_fn(pg, ya, yb):
    return _sigmoid(pg[:, :D]) * ya + _sigmoid(pg[:, D:]) * yb


def _bmm(x, y, cx, cy, out_path=False):
    return lax.dot_general(x, y, (((cx,), (cy,)), ((0,), (0,))),
                           precision=SCAN_OUT_PRECISION if out_path else SCAN_PRECISION, preferred_element_type=F32)


def _unit_lower_inverse(M):
    C = M.shape[1]
    ti = lax.broadcasted_iota(jnp.int32, (C, C), 0)
    tj = lax.broadcasted_iota(jnp.int32, (C, C), 1)
    eye = (ti == tj).astype(F32)
    same = lambda b: (ti // b == tj // b).astype(F32)
    X = -(M * same(SOLVE_B))
    inv = eye + X
    span = 1
    while 2 * span < SOLVE_B:
        X = _bmm(X, X, 2, 1)
        inv = inv + _bmm(inv, X, 2, 1)
        span *= 2
    b = SOLVE_B
    while b < C:
        low = M * (same(2 * b) - same(b))
        inv = inv - _bmm(_bmm(inv, low, 2, 1), inv, 2, 1)
        b *= 2
    return inv


@jax.custom_vjp
def _unit_lower_solve(M, y):
    return _bmm(_unit_lower_inverse(M), y, 2, 1)


def _unit_lower_solve_fwd(M, y):
    inv = _unit_lower_inverse(M)
    u = _bmm(inv, y, 2, 1)
    return u, (inv, u)


def _unit_lower_solve_bwd(res, du):
    inv, u = res
    dy = _bmm(inv, du, 1, 1)
    return -_bmm(dy, u, 2, 2), dy


_unit_lower_solve.defvjp(_unit_lower_solve_fwd, _unit_lower_solve_bwd)


def _chunk_fn(S0, r, lw, k, v, kk, a):
    C = SCAN_C
    bmm = _bmm
    ti = lax.broadcasted_iota(jnp.int32, (C, C), 0)
    tj = lax.broadcasted_iota(jnp.int32, (C, C), 1)
    incl = (tj <= ti).astype(F32)
    strict = (tj < ti).astype(F32)
    cum = lax.dot_general(jnp.broadcast_to(incl, (NH, C, C)), lw, (((2,), (1,)), ((0,), (0,))),
                          precision=lax.Precision.HIGHEST, preferred_element_type=F32)
    g_in, g_ex, g_inv = jnp.exp(cum), jnp.exp(cum - lw), jnp.exp(-cum)
    kkt, rt = kk * g_ex, r * g_in
    bk = jnp.concatenate([kk * a * g_inv, k * g_inv], axis=1)
    A = bmm(kkt, bk, 2, 2)
    M = A[:, :, :C] * strict
    n_mask = jnp.concatenate([jnp.zeros((C, C), F32), strict], axis=1)
    zv = jnp.concatenate([jnp.zeros_like(v), v], axis=1)
    s0_side = bmm(jnp.concatenate([kkt, rt], axis=1), S0, 2, 2, out_path=True)
    y = _unit_lower_solve(M, s0_side[:, :C] + bmm(A * n_mask, zv, 2, 1, out_path=True))
    z = jnp.concatenate([-y, v], axis=1)
    attn = bmm(rt, bk, 2, 2) * jnp.concatenate([incl, incl], axis=1)
    O = s0_side[:, C:] + bmm(attn, z, 2, 1, out_path=True)
    S1 = (S0 + bmm(z, bk, 1, 1, out_path=True)) * g_in[:, C - 1:C, :]
    return O, S1


def _scan_fwd(r, lw, k, v, kk, a, ex=None, tb=256):
    T = r.shape[1]
    tb = min(tb, T)
    n_chunks = tb // SCAN_C
    nb = T // tb
    nx = ex.nb if ex else 0

    def body(*refs):
        r_ref, lw_ref, k_ref, v_ref, kk_ref, a_ref = refs[:6]
        x_in, (o_ref, s0_ref), x_out = refs[6:6 + nx], refs[6 + nx:8 + nx], refs[8 + nx:8 + 2 * nx]
        s_ref, sems = refs[8 + 2 * nx], refs[9 + 2 * nx:]

        plan = ex.schedule(nb) if ex else []

        @pl.when(pl.program_id(0) == 0)
        def _():
            s_ref[...] = jnp.zeros_like(s_ref)
            for at, action in plan[:1]:
                action(x_in, x_out, sems)

        def step(c, carry):
            sl = pl.ds(pl.multiple_of(c * SCAN_C, SCAN_C), SCAN_C)
            S0 = s_ref[...]
            s0_ref[c] = S0
            O, S1 = _chunk_fn(S0, r_ref[:, sl, :], lw_ref[:, sl, :], k_ref[:, sl, :], v_ref[:, sl, :],
                              kk_ref[:, sl, :], a_ref[:, sl, :])
            o_ref[:, sl, :] = O
            s_ref[...] = S1
            return carry

        lax.fori_loop(0, n_chunks, step, 0)

        for at, action in plan[1:]:
            pl.when(pl.program_id(0) == at)(functools.partial(action, x_in, x_out, sems))

    hm = pl.BlockSpec((NH, tb, HN), lambda i: (0, i, 0))
    res = pl.pallas_call(
        body, name="rwkv_scan_fwd", grid=(nb,), in_specs=[hm] * 6 + (ex.any_specs if ex else []),
        out_specs=[hm, pl.BlockSpec((n_chunks, NH, HN, HN), lambda i: (i, 0, 0, 0))] + (ex.any_specs if ex else []),
        out_shape=[jax.ShapeDtypeStruct((NH, T, HN), F32), jax.ShapeDtypeStruct((T // SCAN_C, NH, HN, HN), F32)]
        + (ex.out_shape if ex else []),
        scratch_shapes=[pltpu.VMEM((NH, HN, HN), F32)] + (ex.sem_shapes if ex else []),
        compiler_params=pltpu.CompilerParams(dimension_semantics=("arbitrary",), vmem_limit_bytes=VMEM_LIMIT),
    )(r, lw, k, v, kk, a, *(ex.bufs if ex else []))
    return res[0], res[1], list(res[2:])


def _scan_bwd(r, lw, k, v, kk, a, s0s, do, ex=None, tb=128):
    T = r.shape[1]
    tb = min(tb, T)
    n_chunks = tb // SCAN_C
    nb = T // tb
    nx = ex.nb if ex else 0

    def body(*refs):
        r_ref, lw_ref, k_ref, v_ref, kk_ref, a_ref, s0_ref, do_ref = refs[:8]
        x_in, (dr, dlw, dk, dv, dkk, da), x_out = refs[8:8 + nx], refs[8 + nx:14 + nx], refs[14 + nx:14 + 2 * nx]
        ds_ref, sems = refs[14 + 2 * nx], refs[15 + 2 * nx:]

        plan = ex.schedule(nb) if ex else []

        @pl.when(pl.program_id(0) == 0)
        def _():
            ds_ref[...] = jnp.zeros_like(ds_ref)
            for at, action in plan[:1]:
                action(x_in, x_out, sems)

        def step(j, carry):
            c = n_chunks - 1 - j
            sl = pl.ds(pl.multiple_of(c * SCAN_C, SCAN_C), SCAN_C)
            _, vjp = jax.vjp(_chunk_fn, s0_ref[c], r_ref[:, sl, :], lw_ref[:, sl, :], k_ref[:, sl, :],
                             v_ref[:, sl, :], kk_ref[:, sl, :], a_ref[:, sl, :])
            g = vjp((do_ref[:, sl, :], ds_ref[...]))
            ds_ref[...] = g[0]
            for ref, val in zip((dr, dlw, dk, dv, dkk, da), g[1:]):
                ref[:, sl, :] = val
            return carry

        lax.fori_loop(0, n_chunks, step, 0)

        for at, action in plan[1:]:
            pl.when(pl.program_id(0) == at)(functools.partial(action, x_in, x_out, sems))

    hm = pl.BlockSpec((NH, tb, HN), lambda i: (0, nb - 1 - i, 0))
    res = pl.pallas_call(
        body, name="rwkv_scan_bwd", grid=(nb,),
        in_specs=[hm] * 6 + [pl.BlockSpec((n_chunks, NH, HN, HN), lambda i: (nb - 1 - i, 0, 0, 0)), hm]
        + (ex.any_specs if ex else []),
        out_specs=[hm] * 6 + (ex.any_specs if ex else []),
        out_shape=[jax.ShapeDtypeStruct((NH, T, HN), F32)] * 6 + (ex.out_shape if ex else []),
        scratch_shapes=[pltpu.VMEM((NH, HN, HN), F32)] + (ex.sem_shapes if ex else []),
        compiler_params=pltpu.CompilerParams(dimension_semantics=("arbitrary",), vmem_limit_bytes=VMEM_LIMIT),
    )(r, lw, k, v, kk, a, s0s, do, *(ex.bufs if ex else []))
    return list(res[:6]), list(res[6:])


def _shift_down(i, p, prev8):
    first = jnp.where(i > 0, prev8[7:8, :], 0.0)
    row = lax.broadcasted_iota(jnp.int32, p.shape, 0)
    return jnp.where(row == 0, first, pltpu.roll(p, 1, axis=0))


def _mix_fwd(p, sb, tm=256):
    def fn(i, n, p, prev8, sb):
        return (p * sb[0:1] + _shift_down(i, p, prev8) * sb[1:2],)
    return _rows_call("shift_mix_fwd", fn, [Rows(p), Halo(p, -1)], [sb], [("rows", p.shape[1], F32)], tm=tm,
                      with_pid=True)[0]


def _mix_bwd(dq, p, sb, tm=256):
    def fn(i, n, dq, next8, p, prev8, sb):
        ps = _shift_down(i, p, prev8)
        d1 = dq * sb[1:2]
        last = jnp.where(i < n - 1, next8[0:1, :] * sb[1:2], 0.0)
        row = lax.broadcasted_iota(jnp.int32, dq.shape, 0)
        up = jnp.where(row == dq.shape[0] - 1, last, pltpu.roll(d1, dq.shape[0] - 1, axis=0))
        return (dq * sb[0:1] + up, jnp.sum(dq * p, axis=0, keepdims=True), jnp.sum(dq * ps, axis=0, keepdims=True))
    w = p.shape[1]
    return _rows_call("shift_mix_bwd", fn, [Rows(dq), Halo(dq, +1), Rows(p), Halo(p, -1)], [sb], [("rows", w, F32)],
                      accs=[((1, w), F32), ((1, w), F32)], tm=tm, with_pid=True)


def _local_step(x, target, W, late_weights=None, early_grads=None):
    G = {}
    a = _rows_call("norm_mix_fwd", lambda x, g: (_rms(x, g),), [Rows(x)], [W["g_mix"]], [("rows", D, BF16)])[0]
    p_sgu = _matmul("proj_sgu", a, W["w_sgu_t"], "nt")
    p_rw = _matmul("proj_rwkv", a, W["w_rw_t"], "nt")
    p_gate = _matmul("proj_gate", a, W["w_gate_t"], "nt")

    sgu_consts = [W["sgu_ln_w"], W["sgu_ln_b"], W["sgu_w"], W["sgu_bt"]]
    s = _rows_call("sgu_fwd", lambda *t: (_sgu_fn(*t),), [Rows(p_sgu)], sgu_consts, [("rows", D, BF16)], tm=SGU_C)[0]

    q = _mix_fwd(p_rw, W["sb"])
    q_ins = [Rows(q, D, 0), Rows(q, D, 1), Rows(q, D, 2), Rows(q, 128, 24), Rows(q, 128, 25), Rows(q, 256, 13)]
    pre_consts = [W["w_lora"], W["w0"], W["a_lora"], W["a0"], W["g_lora"], W["k_k"], W["k_a"]]
    r_h, lw_h, k_h, v_h, kk_h, a_h, g_gate = _rows_call(
        "rwkv_pre_fwd", _pre_fn, q_ins, pre_consts, [("heads", F32)] * 6 + [("rows", D, F32)], tm=128)
    o_h, s0s, got = _scan_fwd(r_h, lw_h, k_h, v_h, kk_h, a_h, ex=late_weights[0] if late_weights else None)
    if late_weights:
        W = {**W, **late_weights[1](got)}
    y_a = _matmul("proj_a", s, W["w_proj_a"], "nn")
    post_ins = [Heads(o_h), Heads(r_h), Heads(k_h), Heads(v_h), Rows(g_gate)]
    post_consts = [W[n].reshape(NH, 1, HN) for n in ("ln_x_w", "ln_x_b", "r_k")]
    z_b = _rows_call("rwkv_post_fwd", lambda *t: (_post_fn(*t),), post_ins, post_consts, [("rows", D, BF16)], tm=128)[0]
    y_b, mixed = _matmul("proj_b", z_b, W["w_proj_b"], "nn", extras=[(p_gate, 0), (p_gate, 1), y_a],
                         epilogue=lambda yb, ga, gb, ya: (yb, _sigmoid(ga) * ya + _sigmoid(gb) * yb),
                         out_dtypes=(F32, BF16))
    gate_ins = [Rows(p_gate), Rows(y_a), Rows(y_b)]

    def res1(mo, x, g):
        h1 = x + mo
        return h1, _rms(h1, g)
    h1, f = _matmul("proj_out", mixed, W["w_out"], "nn", extras=[x, W["g_ffn"]], epilogue=res1,
                    out_dtypes=(F32, BF16))

    def relu_sq(u):
        r = jnp.maximum(u, 0.0)
        return r, r * r
    r1, act = _matmul("ffn_up", f, W["w_ffn1"], "nn", epilogue=relu_sq, out_dtypes=(BF16, BF16))
    ff = _matmul("ffn_down", act, W["w_ffn2"], "nn")

    def head(h1, ff, tgt, g):
        def f_(h1, ff, g):
            y = _rms(h1 + ff, g)
            return 0.5 * jnp.sum(jnp.mean(jnp.square(y - tgt), axis=-1))
        loss, (dh2, _, dg) = jax.value_and_grad(f_, argnums=(0, 1, 2))(h1, ff, g)
        return dh2, jnp.full((8, LANES), loss, F32), dg
    dh2, loss_acc, G["g_final"] = _rows_call("loss_head", head, [Rows(h1), Rows(ff), Rows(target)], [W["g_final"]],
                                             [("rows", D, F32)], accs=[((8, LANES), F32), ((1, D), F32)])

    d_u1 = _matmul("ffn_down_dx", dh2, W["w_ffn2"], "nt", extras=[r1], out_dtypes=(BF16,),
                   epilogue=lambda d_act, r: (d_act * 2.0 * r.astype(F32),))[0]
    G["w_ffn2"] = _matmul("ffn_down_dw", act, dh2, "tn", out_dtype=GRAD_PAYLOAD)
    d_f = _matmul("ffn_up_dx", d_u1, W["w_ffn1"], "nt")
    G["w_ffn1"] = _matmul("ffn_up_dw", f, d_u1, "tn", out_blocks=N_DEV, out_dtype=GRAD_PAYLOAD)

    def res1_bwd(h1, d_f, dh2, g):
        _, vjp = jax.vjp(_rms, h1, g)
        dh, dg = vjp(d_f)
        return dh2 + dh, dg
    dh1, G["g_ffn"] = _rows_call("residual_norm_bwd", res1_bwd, [Rows(h1), Rows(d_f), Rows(dh2)],
                                 [W["g_ffn"]], [("rows", D, F32)], accs=[((1, D), F32)])
    d_mixed = _matmul("proj_out_dx", dh1, W["w_out"], "nt")
    G["w_out"] = _matmul("proj_out_dw", mixed, dh1, "tn", out_dtype=GRAD_PAYLOAD)

    def gate_bwd(pg, ya, yb, dm):
        _, vjp = jax.vjp(_gate_fn, pg, ya, yb)
        return vjp(dm)
    d_gate, d_ya, d_yb = _rows_call("gate_bwd", gate_bwd, gate_ins + [Rows(d_mixed)], [],
                                    [("rows", 2 * D, F32), ("rows", D, BF16), ("rows", D, BF16)])

    d_s = _matmul("proj_a_dx", d_ya, W["w_proj_a"], "nt")
    G["w_proj_a"] = _matmul("proj_a_dw", s, d_ya, "tn", out_dtype=GRAD_PAYLOAD)

    def sgu_bwd(p, ds, *c):
        _, vjp = jax.vjp(_sgu_fn, p, *c)
        return vjp(ds)
    d_p_sgu, G["sgu_ln_w"], G["sgu_ln_b"], G["sgu_w"], G["sgu_bt"] = _rows_call(
        "sgu_bwd", sgu_bwd, [Rows(p_sgu), Rows(d_s)], sgu_consts, [("rows", 2 * D, F32)],
        accs=[((1, D), F32), ((1, D), F32), ((SGU_G, SGU_C, SGU_C), F32), ((SGU_C, SGU_G), F32)], tm=SGU_C)

    d_zb = _matmul("proj_b_dx", d_yb, W["w_proj_b"], "nt")
    G["w_proj_b"] = _matmul("proj_b_dw", z_b, d_yb, "tn", out_dtype=GRAD_PAYLOAD)

    def post_bwd(o, r, k2, v, g, dz, *c):
        _, vjp = jax.vjp(_post_fn, o, r, k2, v, g, *c)
        return vjp(dz)
    do_h, dr1, dk1, dv1, d_g, g_lnw, g_lnb, g_rk = _rows_call(
        "rwkv_post_bwd", post_bwd, post_ins + [Rows(d_zb)], post_consts, [("heads", F32)] * 4 + [("rows", D, F32)],
        accs=[((NH, 1, HN), F32)] * 3, tm=128)
    G["ln_x_w"], G["ln_x_b"], G["r_k"] = (t.reshape(1, D) for t in (g_lnw, g_lnb, g_rk))
    (dr2, dlw, dk2, dv2, dkk, daa), early = _scan_bwd(r_h, lw_h, k_h, v_h, kk_h, a_h, s0s, do_h,
                                                      ex=early_grads(G) if early_grads else None)

    def pre_bwd(qr, qk, qv, qxw, qxa, qxg, dr1, dr2, dlw, dk1, dk2, dv1, dv2, dkk, daa, dg, *c):
        _, vjp = jax.vjp(_pre_fn, qr, qk, qv, qxw, qxa, qxg, *c)
        g = vjp((dr1 + dr2, dlw, dk1 + dk2, dv1 + dv2, dkk, daa, dg))
        dq = jnp.concatenate(g[:6], axis=-1)
        return (dq,) + tuple(g[6:])
    pre_b_ins = q_ins + [Heads(dr1), Heads(dr2), Heads(dlw), Heads(dk1), Heads(dk2), Heads(dv1), Heads(dv2),
                         Heads(dkk), Heads(daa), Rows(d_g)]
    d_q, G["w_lora"], G["w0"], G["a_lora"], G["a0"], G["g_lora"], G["k_k"], G["k_a"] = _rows_call(
        "rwkv_pre_bwd", pre_bwd, pre_b_ins, pre_consts, [("rows", RW_INT, F32)],
        accs=[((128, D), F32), ((1, D), F32), ((128, D), F32), ((1, D), F32), ((256, D), F32), ((1, D), F32),
              ((1, D), F32)], tm=128)
    d_p_rw, dsb0, dsb1 = _mix_bwd(d_q, p_rw, W["sb"])
    G["sb"] = jnp.concatenate([dsb0, dsb1], axis=0)

    G["w_sgu_t"] = _matmul("proj_sgu_dw", d_p_sgu, a, "tn", out_dtype=GRAD_PAYLOAD)
    G["w_rw_t"] = _matmul("proj_rwkv_dw", d_p_rw, a, "tn", out_dtype=GRAD_PAYLOAD)
    G["w_gate_t"] = _matmul("proj_gate_dw", d_gate, a, "tn", out_dtype=GRAD_PAYLOAD)
    da1 = _matmul("proj_sgu_dx", d_p_sgu, W["w_sgu_t"], "nn")
    da2 = _matmul("proj_rwkv_dx", d_p_rw, W["w_rw_t"], "nn")
    da3 = _matmul("proj_gate_dx", d_gate, W["w_gate_t"], "nn")

    def norm1_bwd(x, da1, da2, da3, dh1, g):
        _, vjp = jax.vjp(_rms, x, g)
        dx, dg = vjp(da1 + da2 + da3)
        return dh1 + dx, dg
    dx, G["g_mix"] = _rows_call("norm_mix_bwd", norm1_bwd, [Rows(x), Rows(da1), Rows(da2), Rows(da3), Rows(dh1)],
                                [W["g_mix"]], [("rows", D, F32)], accs=[((1, D), F32)])
    return loss_acc[0, 0], dx, G, early


class Exchange:
    def __init__(self, bufs, gathers):
        self.bufs, self.gathers, self.nb = list(bufs), list(gathers), len(bufs)
        self.any_specs = [pl.BlockSpec(memory_space=pl.ANY)] * self.nb
        self.out_shape = [jax.ShapeDtypeStruct((N_DEV,) + (b.shape if g else b.shape[1:]), b.dtype)
                          for b, g in zip(self.bufs, self.gathers)]
        n = (N_DEV - 1) * self.nb
        self.sem_shapes = [pltpu.SemaphoreType.DMA((n,)), pltpu.SemaphoreType.DMA((n,)),
                           pltpu.SemaphoreType.DMA((self.nb,))]

    def _copies(self, in_refs, out_refs, sems):
        send_sems, recv_sems, local_sems = sems
        x, y, c = lax.axis_index("x"), lax.axis_index("y"), lax.axis_index("c")
        me = 4 * x + 2 * y + c

        def src(b, dest):
            return in_refs[b] if self.gathers[b] else in_refs[b].at[dest]

        local = [pltpu.make_async_copy(src(b, me), out_refs[b].at[me], local_sems.at[b]) for b in range(self.nb)]
        sends, recvs = [], []
        for kbits in range(1, N_DEV):
            px = 1 - x if kbits & 4 else x
            py = 1 - y if kbits & 2 else y
            pc = 1 - c if kbits & 1 else c
            peer = 4 * px + 2 * py + pc
            for b in range(self.nb):
                s = (kbits - 1) * self.nb + b
                sends.append(pltpu.make_async_remote_copy(
                    src_ref=src(b, peer), dst_ref=out_refs[b].at[me], send_sem=send_sems.at[s],
                    recv_sem=recv_sems.at[s], device_id=(px, py, pc), device_id_type=pl.DeviceIdType.MESH))
                recvs.append(pltpu.make_async_remote_copy(
                    src_ref=src(b, peer), dst_ref=out_refs[b].at[peer], send_sem=send_sems.at[s],
                    recv_sem=recv_sems.at[s], device_id=(px, py, pc), device_id_type=pl.DeviceIdType.MESH))
        return local, sends, recvs

    def start(self, in_refs, out_refs, sems):
        local, sends, _ = self._copies(in_refs, out_refs, sems)
        for cp in sends + local:
            cp.start()

    def wait(self, in_refs, out_refs, sems):
        local, sends, recvs = self._copies(in_refs, out_refs, sems)
        for cp in recvs:
            cp.wait_recv()
        for cp in sends:
            cp.wait_send()
        for cp in local:
            cp.wait()

    def schedule(self, n_steps):
        return [(0, self.start), (n_steps - 1, self.wait)]


def _exchange(name, bufs, gather):
    ex = Exchange(bufs, gather if isinstance(gather, (list, tuple)) else [gather] * len(bufs))

    def body(*refs):
        in_refs, out_refs, sems = refs[:ex.nb], refs[ex.nb:2 * ex.nb], refs[2 * ex.nb:]
        ex.start(in_refs, out_refs, sems)
        ex.wait(in_refs, out_refs, sems)

    return pl.pallas_call(body, name=name, in_specs=ex.any_specs, out_specs=ex.any_specs, out_shape=ex.out_shape,
                          scratch_shapes=ex.sem_shapes)(*ex.bufs)


N_CHIP = 4


def _pair_exchange(name, blocks):
    def body(b_ref, got_ref, send_sems, recv_sems):
        x, y, c = lax.axis_index("x"), lax.axis_index("y"), lax.axis_index("c")
        copies = [pltpu.make_async_remote_copy(
            src_ref=b_ref.at[2 * q + 1 - c], dst_ref=got_ref.at[q], send_sem=send_sems.at[q],
            recv_sem=recv_sems.at[q], device_id=(x, y, 1 - c), device_id_type=pl.DeviceIdType.MESH)
            for q in range(N_CHIP)]
        for cp in copies:
            cp.start()
        for cp in copies:
            cp.wait_recv()
        for cp in copies:
            cp.wait_send()

    any_spec = pl.BlockSpec(memory_space=pl.ANY)
    return pl.pallas_call(
        body, name=name, in_specs=[any_spec], out_specs=any_spec,
        out_shape=jax.ShapeDtypeStruct((N_CHIP,) + blocks.shape[1:], blocks.dtype),
        scratch_shapes=[pltpu.SemaphoreType.DMA((N_CHIP,))] * 2,
    )(blocks)


def _pair_sum(name, blocks, got, core):
    _, R, Wd = blocks.shape

    def body(c_ref, a_ref, b_ref, o_ref):
        o_ref[...] = (a_ref[...].astype(F32) + b_ref[...].astype(F32)).astype(o_ref.dtype)

    return pl.pallas_call(
        body, name=name,
        grid_spec=pltpu.PrefetchScalarGridSpec(
            num_scalar_prefetch=1, grid=(N_CHIP,),
            in_specs=[pl.BlockSpec((None, R, Wd), lambda q, c_ref: (2 * q + c_ref[0], 0, 0)),
                      pl.BlockSpec((None, R, Wd), lambda q, c_ref: (q, 0, 0))],
            out_specs=pl.BlockSpec((None, R, Wd), lambda q, c_ref: (q, 0, 0))),
        out_shape=jax.ShapeDtypeStruct(got.shape, blocks.dtype),
        compiler_params=pltpu.CompilerParams(dimension_semantics=("parallel",), vmem_limit_bytes=VMEM_LIMIT),
    )(core, blocks, got)


def _chip_exchange(name, sums, ex):
    def body(*refs):
        s_ref, x_in = refs[0], refs[1:1 + ex.nb]
        slots_ref, x_out = refs[1 + ex.nb], refs[2 + ex.nb:2 + 2 * ex.nb]
        send_sems, recv_sems = refs[2 + 2 * ex.nb:4 + 2 * ex.nb]
        ex_sems = refs[4 + 2 * ex.nb:]
        x, y, c = lax.axis_index("x"), lax.axis_index("y"), lax.axis_index("c")
        my_q = 2 * x + y
        sends, recvs = [], []
        for kbits in range(1, N_CHIP):
            px = 1 - x if kbits & 2 else x
            py = 1 - y if kbits & 1 else y
            peer_q = 2 * px + py
            sends.append(pltpu.make_async_remote_copy(
                src_ref=s_ref.at[peer_q], dst_ref=slots_ref.at[my_q], send_sem=send_sems.at[kbits - 1],
                recv_sem=recv_sems.at[kbits - 1], device_id=(px, py, c), device_id_type=pl.DeviceIdType.MESH))
            recvs.append(pltpu.make_async_remote_copy(
                src_ref=s_ref.at[peer_q], dst_ref=slots_ref.at[peer_q], send_sem=send_sems.at[kbits - 1],
                recv_sem=recv_sems.at[kbits - 1], device_id=(px, py, c), device_id_type=pl.DeviceIdType.MESH))
        for cp in sends:
            cp.start()
        ex.start(x_in, x_out, ex_sems)
        for cp in recvs:
            cp.wait_recv()
        for cp in sends:
            cp.wait_send()
        ex.wait(x_in, x_out, ex_sems)

    any_spec = pl.BlockSpec(memory_space=pl.ANY)
    res = pl.pallas_call(
        body, name=name, in_specs=[any_spec] + ex.any_specs, out_specs=[any_spec] + ex.any_specs,
        out_shape=[jax.ShapeDtypeStruct(sums.shape, sums.dtype)] + ex.out_shape,
        scratch_shapes=[pltpu.SemaphoreType.DMA((N_CHIP - 1,)), pltpu.SemaphoreType.DMA((N_CHIP - 1,))]
        + ex.sem_shapes,
    )(sums, *ex.bufs)
    return res[0], list(res[1:])


def _adamw(name, slots, w, m, v, tr=256):
    R, Wd = w.shape[-2:]
    depth_axis = w.ndim == 3
    if R % tr == 0:
        tc = Wd
    else:
        tr, tc = R, (256 if (Wd % 256 == 0 and R > 256) else Wd)

    def body(s_ref, w_ref, m_ref, v_ref, g_out, d_out, m_out, v_out):
        g = s_ref[0].astype(F32)
        for j in range(1, slots.shape[0]):
            g = g + s_ref[j].astype(F32)
        m_new = ADAM_B1 * m_ref[...] + (1.0 - ADAM_B1) * g
        v_new = ADAM_B2 * v_ref[...] + (1.0 - ADAM_B2) * jnp.square(g)
        m_hat = m_new / (1.0 - ADAM_B1 ** ADAM_STEP)
        v_hat = v_new / (1.0 - ADAM_B2 ** ADAM_STEP)
        g_out[...] = g
        d_out[...] = -ADAM_LR * (m_hat / (jnp.sqrt(v_hat) + ADAM_EPS) + ADAM_WD * w_ref[...])
        m_out[...] = m_new
        v_out[...] = v_new

    if depth_axis:
        row = pl.BlockSpec((None, tr, tc), lambda i, j: (0, i, j))
    else:
        row = pl.BlockSpec((tr, tc), lambda i, j: (i, j))
    return pl.pallas_call(
        body, name=name, grid=(R // tr, Wd // tc),
        in_specs=[pl.BlockSpec((slots.shape[0], tr, tc), lambda i, j: (0, i, j)), row, row, row],
        out_specs=[row] * 4, out_shape=[jax.ShapeDtypeStruct(w.shape, F32)] * 4,
        compiler_params=pltpu.CompilerParams(dimension_semantics=("parallel", "parallel"),
                                             vmem_limit_bytes=VMEM_LIMIT),
    )(slots, w, m, v)


PACK_W = 1024
PACKED = [(n, s) for n, s in REPLICATED if n != "sgu_w"]
_SMALL_SIZES = [int(np.prod(s)) for _, s in PACKED]
_SMALL_ROWS = _round_up(_round_up(sum(_SMALL_SIZES) + PACK_W, PACK_W) // PACK_W, 8)
_LOSS_AT = sum(_SMALL_SIZES)
W_IN_SHARD = P_TOTAL // N_DEV


def _pack_rows(parts, rows, dtype):
    flat = jnp.concatenate([p.reshape(-1).astype(dtype) for p in parts])
    return jnp.pad(flat, (0, rows * PACK_W - flat.shape[0])).reshape(rows, PACK_W)


def _w_in_groups_t(blocks):
    wt = blocks.reshape(P_TOTAL, D)
    o, c = 2 * D, 2 * D + 3 * D
    z = lambda r: jnp.zeros((r, D), wt.dtype)
    rw = jnp.concatenate([wt[o:c], wt[c:c + L_W], z(128 - L_W), wt[c + L_W:c + L_W + L_A], z(128 - L_A),
                          wt[c + L_W + L_A:o + C_B], z(256 - L_G)], axis=0)
    return wt[:o], rw, wt[o + C_B:]


def _w_in_grad_blocks(g_sgu_t, g_rw_t, g_gate_t):
    c = 3 * D
    full = jnp.concatenate([g_sgu_t, g_rw_t[:c], g_rw_t[c:c + L_W], g_rw_t[c + 128:c + 128 + L_A],
                            g_rw_t[c + 256:c + 256 + L_G], g_gate_t], axis=0)
    return full.reshape(N_DEV, W_IN_SHARD, D)


def _mesh_index():
    me = 4 * lax.axis_index("x") + 2 * lax.axis_index("y") + lax.axis_index("c")
    return me.astype(jnp.int32).reshape(1)


def _fill_slot(name, dst, src, idx, src_idx=None):
    R, Wd = dst.shape[1:]
    scalars = [idx] if src_idx is None else [idx, src_idx]
    if src_idx is None:
        src_spec = pl.BlockSpec((R, Wd), lambda i, *s: (0, 0))
    else:
        src_spec = pl.BlockSpec((None, R, Wd), lambda i, *s: (s[1][0], 0, 0))

    def body(*refs):
        src_ref, out_ref = refs[len(scalars) + 1], refs[len(scalars) + 2]
        out_ref[...] = src_ref[...]

    return pl.pallas_call(
        body, name=name,
        grid_spec=pltpu.PrefetchScalarGridSpec(
            num_scalar_prefetch=len(scalars), grid=(1,),
            in_specs=[pl.BlockSpec(memory_space=pl.ANY), src_spec],
            out_specs=pl.BlockSpec((None, R, Wd), lambda i, *s: (s[0][0], 0, 0))),
        out_shape=jax.ShapeDtypeStruct(dst.shape, dst.dtype),
        input_output_aliases={len(scalars): 0},
        compiler_params=pltpu.CompilerParams(vmem_limit_bytes=VMEM_LIMIT),
    )(*scalars, dst, src)


class TwoLevelGather:
    def __init__(self, bufs, skip_own=()):
        self.bufs, self.nb, self.skip_own = list(bufs), len(bufs), tuple(skip_own)
        self.any_specs = [pl.BlockSpec(memory_space=pl.ANY)] * self.nb
        self.out_shape = [jax.ShapeDtypeStruct((N_DEV,) + b.shape, b.dtype) for b in self.bufs]
        self.sem_shapes = [pltpu.SemaphoreType.DMA((7 * self.nb,)), pltpu.SemaphoreType.DMA((7 * self.nb,)),
                           pltpu.SemaphoreType.DMA((self.nb,))]

    def _copies(self, in_refs, out_refs, sems):
        send_sems, recv_sems, local_sems = sems
        nb = self.nb
        x, y, c = lax.axis_index("x"), lax.axis_index("y"), lax.axis_index("c")
        me, sibling = (x, y, c), (x, y, 1 - c)
        chips = [(1 - x, y), (x, 1 - y), (1 - x, 1 - y)]

        def slot(b, dev):
            return out_refs[b].at[4 * dev[0] + 2 * dev[1] + dev[2]]

        def copy(b, k, block, to, own=False):
            return pltpu.make_async_remote_copy(
                src_ref=in_refs[b] if own else slot(b, block), dst_ref=slot(b, block),
                send_sem=send_sems.at[7 * b + k], recv_sem=recv_sems.at[7 * b + k], device_id=to,
                device_id_type=pl.DeviceIdType.MESH)

        cp = {}
        cp["local"] = [pltpu.make_async_copy(in_refs[b], slot(b, me), local_sems.at[b]) for b in range(nb)
                       if b not in self.skip_own]
        cp["first"] = [copy(b, 0, me, sibling, own=True) for b in range(nb)]
        cp["first"] += [copy(b, 1 + j, me, (*chip, c), own=True) for j, chip in enumerate(chips) for b in range(nb)]
        cp["over_ici"] = [copy(b, 1 + j, (*chip, c), me) for j, chip in enumerate(chips) for b in range(nb)]
        cp["passed"] = [copy(b, 4 + j, (*chip, c), sibling) for j, chip in enumerate(chips) for b in range(nb)]
        cp["from_sibling"] = [copy(b, 0, sibling, me) for b in range(nb)]
        cp["from_sibling"] += [copy(b, 4 + j, (*chip, 1 - c), me) for j, chip in enumerate(chips) for b in range(nb)]
        return cp

    def start(self, in_refs, out_refs, sems):
        cp = self._copies(in_refs, out_refs, sems)
        for c in cp["first"] + cp["local"]:
            c.start()

    def forward(self, in_refs, out_refs, sems):
        cp = self._copies(in_refs, out_refs, sems)
        for arrived, onward in zip(cp["over_ici"], cp["passed"]):
            arrived.wait_recv()
            onward.start()

    def finish(self, in_refs, out_refs, sems):
        cp = self._copies(in_refs, out_refs, sems)
        for c in cp["from_sibling"]:
            c.wait_recv()
        for c in cp["first"] + cp["passed"]:
            c.wait_send()
        for c in cp["local"]:
            c.wait()

    def schedule(self, n_steps):
        return [(0, self.start), (max(n_steps - 3, 0), self.forward), (n_steps - 1, self.finish)]


def _all_gather_two_level(name, bufs, skip_own=()):
    ex = TwoLevelGather(bufs, skip_own)

    def body(*refs):
        args = refs[:ex.nb], refs[ex.nb:2 * ex.nb], refs[2 * ex.nb:]
        ex.start(*args)
        ex.forward(*args)
        ex.finish(*args)

    return pl.pallas_call(body, name=name, in_specs=ex.any_specs, out_specs=ex.any_specs, out_shape=ex.out_shape,
                          scratch_shapes=ex.sem_shapes)(*ex.bufs)


def _cols_from_blocks(blk):
    return jnp.transpose(blk, (1, 0, 2)).reshape(blk.shape[1], -1)


def _cols_to_blocks(g):
    r, c = g.shape
    return jnp.transpose(g.reshape(r, N_DEV, c // N_DEV), (1, 0, 2))


FIRST_WEIGHTS = ["w_in", "shift_b", "w_lora_w", "a_lora_w", "g_lora_w"]
LATE_WEIGHTS = ["w_proj_a", "w_proj_b", "w_out", "w_ffn1", "w_ffn2"]


def _late_weights(shards):
    ex = TwoLevelGather([shards[n].astype(BF16) for n in LATE_WEIGHTS])

    def finish(results):
        got = dict(zip(LATE_WEIGHTS, results))
        W = {n: got[n].reshape(-1, D) for n in ("w_proj_a", "w_proj_b", "w_out", "w_ffn2")}
        W["w_ffn1"] = got["w_ffn1"].reshape(N_DEV, D, -1)
        return W
    return ex, finish


def _gather_weights(shards):
    def payload(n):
        if n == "w_in":
            return jnp.transpose(shards[n][0]).astype(BF16)
        return shards[n] if n == "shift_b" else shards[n].astype(BF16)
    payloads = [payload(n) for n in FIRST_WEIGHTS]
    got = dict(zip(FIRST_WEIGHTS, _all_gather_two_level("weight_all_gather", payloads, skip_own=(0,))))
    got["w_in"] = _fill_slot("w_in_own_slot", got["w_in"], payloads[0], _mesh_index())
    W = {}
    W["w_sgu_t"], W["w_rw_t"], W["w_gate_t"] = _w_in_groups_t(got["w_in"])
    z = lambda r, c, dt: jnp.zeros((r, c), dt)
    W["w_lora"] = jnp.concatenate([_cols_from_blocks(got["w_lora_w"][:, 0]).astype(F32), z(128 - L_W, D, F32)], axis=0)
    W["a_lora"] = jnp.concatenate([_cols_from_blocks(got["a_lora_w"][:, 0]).astype(F32), z(128 - L_A, D, F32)], axis=0)
    W["g_lora"] = jnp.concatenate([_cols_from_blocks(got["g_lora_w"][:, 0]).astype(F32), z(256 - L_G, D, F32)], axis=0)
    sb = _cols_from_blocks(got["shift_b"][:, 0])
    W["sb"] = jnp.concatenate([sb[:, :3 * D], sb[:, 3 * D:3 * D + L_W], z(2, 128 - L_W, F32),
                               sb[:, 3 * D + L_W:3 * D + L_W + L_A], z(2, 128 - L_A, F32),
                               sb[:, 3 * D + L_W + L_A:], z(2, 256 - L_G, F32)], axis=1)
    return W


def _replicated_weights(rep):
    W = {n: rep[n] for n in ("g_mix", "sgu_ln_w", "sgu_ln_b", "w0", "a0", "k_k", "k_a", "r_k", "ln_x_w", "ln_x_b",
                             "g_ffn")}
    W["g_final"] = rep["g_final"].reshape(1, D)
    W["sgu_w"] = rep["sgu_w"][0]
    W["sgu_bt"] = jnp.transpose(rep["sgu_b"][0])
    return W


def _late_grad_blocks(G):
    blocks = {n: G[n].reshape(N_DEV, -1, D) for n in ("w_proj_a", "w_proj_b", "w_out", "w_ffn2")}
    blocks["w_ffn1"] = G["w_ffn1"]
    return Exchange([blocks[n] for n in LATE_WEIGHTS] + [G["sgu_w"].reshape(SGU_G * SGU_C, SGU_C)],
                    [False] * len(LATE_WEIGHTS) + [True])


def _first_grad_blocks(G):
    sbg = G["sb"]
    c = 3 * D
    sb = jnp.concatenate([sbg[:, :c], sbg[:, c:c + L_W], sbg[:, c + 128:c + 128 + L_A],
                          sbg[:, c + 256:c + 256 + L_G]], axis=1)
    return {
        "w_in": _w_in_grad_blocks(G["w_sgu_t"], G["w_rw_t"], G["w_gate_t"]),
        "shift_b": _cols_to_blocks(sb),
        "w_lora_w": _cols_to_blocks(G["w_lora"][:L_W]), "a_lora_w": _cols_to_blocks(G["a_lora"][:L_A]),
        "g_lora_w": _cols_to_blocks(G["g_lora"][:L_G]),
    }


def _replicated_grads(G):
    small = {n: G[n] for n in ("g_mix", "sgu_ln_w", "sgu_ln_b", "w0", "a0", "k_k", "k_a", "r_k", "ln_x_w", "ln_x_b",
                               "g_ffn", "g_final")}
    small["sgu_w"] = G["sgu_w"]
    small["sgu_b"] = jnp.transpose(G["sgu_bt"])
    return small


def kernel(x, g_mix, w_in, sgu_ln_w, sgu_ln_b, sgu_w, sgu_b, w_proj_a, shift_b, w_lora_w, w0, a_lora_w, a0, g_lora_w, k_k, k_a, r_k, ln_x_w, ln_x_b, w_proj_b, w_out, g_ffn, w_ffn1, w_ffn2, g_final, loss_target, m_g_mix, m_w_in, m_sgu_ln_w, m_sgu_ln_b, m_sgu_w, m_sgu_b, m_w_proj_a, m_shift_b, m_w_lora_w, m_w0, m_a_lora_w, m_a0, m_g_lora_w, m_k_k, m_k_a, m_r_k, m_ln_x_w, m_ln_x_b, m_w_proj_b, m_w_out, m_g_ffn, m_w_ffn1, m_w_ffn2, m_g_final, v_g_mix, v_w_in, v_sgu_ln_w, v_sgu_ln_b, v_sgu_w, v_sgu_b, v_w_proj_a, v_shift_b, v_w_lora_w, v_w0, v_a_lora_w, v_a0, v_g_lora_w, v_k_k, v_k_a, v_r_k, v_ln_x_w, v_ln_x_b, v_w_proj_b, v_w_out, v_g_ffn, v_w_ffn1, v_w_ffn2, v_g_final):
    env = dict(locals())
    weights = {n: env[n] for n in WEIGHT_ORDER}
    moms = {n: env["m_" + n] for n in WEIGHT_ORDER}
    vars_ = {n: env["v_" + n] for n in WEIGHT_ORDER}

    shards = {n: weights[n] for n, _, _ in SHARDED}
    W = _gather_weights(shards)
    W.update(_replicated_weights({n: weights[n] for n, _ in REPLICATED}))
    loss_part, dx, G, late_slots = _local_step(x[0], loss_target[0], W, late_weights=_late_weights(shards),
                                               early_grads=_late_grad_blocks)

    slots = dict(zip(LATE_WEIGHTS, late_slots))
    blocks = _first_grad_blocks(G)
    small = _replicated_grads(G)
    small_parts = [small[n] for n, _ in PACKED] + [jnp.full((PACK_W,), loss_part, F32)]
    got = _pair_exchange("grad_pair_exchange", blocks["w_in"])
    core = lax.axis_index("c").astype(jnp.int32).reshape(1)
    rest = [n for n in FIRST_WEIGHTS if n != "w_in"]
    ex = Exchange([blocks[n] for n in rest] + [_pack_rows(small_parts, _SMALL_ROWS, F32)], [False] * len(rest) + [True])
    sums = _pair_sum("grad_pair_sum", blocks["w_in"], got, core)
    chip_slots, res = _chip_exchange("grad_exchange", sums, ex)
    my_chip = (2 * lax.axis_index("x") + lax.axis_index("y")).astype(jnp.int32).reshape(1)
    slots["w_in"] = _fill_slot("grad_own_slot", chip_slots, sums, my_chip, src_idx=my_chip)
    slots.update(zip(rest, res[:-1]))
    small_slots = res[-1]

    outs = [dict(), dict(), dict(), dict()]
    for n, _, _ in SHARDED:
        if n == "w_in":
            res = _adamw("adamw_" + n, slots[n], *[jnp.transpose(t[0]) for t in (weights[n], moms[n], vars_[n])])
            res = [jnp.transpose(t)[None] for t in res]
        else:
            res = _adamw("adamw_" + n, slots[n], weights[n], moms[n], vars_[n])
        for k in range(4):
            outs[k][n] = res[k]

    sgu_shape = (SGU_G * SGU_C, SGU_C)
    res = _adamw("adamw_sgu_w", late_slots[len(LATE_WEIGHTS)], *[t.reshape(sgu_shape) for t in
                                                                  (weights["sgu_w"], moms["sgu_w"], vars_["sgu_w"])])
    for k in range(4):
        outs[k]["sgu_w"] = res[k].reshape(weights["sgu_w"].shape)

    def packed(d):
        return _pack_rows([d[n] for n, _ in PACKED], _SMALL_ROWS, F32)
    small_out = _adamw("adamw_replicated", small_slots, packed(weights), packed(moms), packed(vars_))
    for k in range(4):
        flat = small_out[k].reshape(-1)
        off = 0
        for (n, s), size in zip(PACKED, _SMALL_SIZES):
            outs[k][n] = flat[off:off + size].reshape(s)
            off += size
    loss = small_out[0].reshape(-1)[_LOSS_AT]
    return (loss, dx[None], *[outs[0][n] for n in WEIGHT_ORDER], *[outs[1][n] for n in WEIGHT_ORDER],
            *[outs[2][n] for n in WEIGHT_ORDER], *[outs[3][n] for n in WEIGHT_ORDER])
```

```python
import functools
import numpy as np
import jax
import jax.numpy as jnp
from jax import lax
from jax.experimental import pallas as pl
from jax.experimental.pallas import tpu as pltpu

F32 = jnp.float32
BF16 = jnp.bfloat16

D = 1024
NH, HN = 16, 64
SGU_G, SGU_C = 8, 128
L_W, L_A, L_G = 64, 64, 160
C_B = 3 * D + L_W + L_A + L_G
P_TOTAL = 2 * D + C_B + 2 * D
D_FF = 4 * D
RW_INT = 3 * D + 128 + 128 + 256
NORM_EPS, LN_EPS, GN_EPS = 1e-6, 1e-5, 64e-5
N_DEV = 8
LANES = 128
SCAN_C = 64
SOLVE_B = 16
SCAN_PRECISION = lax.Precision.HIGH
SCAN_OUT_PRECISION = lax.Precision.DEFAULT
GRAD_PAYLOAD = BF16
VMEM_LIMIT = 56 * 1024 * 1024
MATMUL_VMEM_BUDGET = 40 * 1024 * 1024
STEP_COST_BYTES = 512 * 1024
HBM_COST_RATIO = 3

ADAM_LR, ADAM_B1, ADAM_B2, ADAM_EPS, ADAM_WD, ADAM_STEP = 0.001, 0.9, 0.999, 1e-08, 0.01, 10

SHARDED = [
    ("w_in", (D, P_TOTAL), 1), ("w_proj_a", (D, D), 0), ("shift_b", (2, C_B), 1), ("w_lora_w", (L_W, D), 1),
    ("a_lora_w", (L_A, D), 1), ("g_lora_w", (L_G, D), 1), ("w_proj_b", (D, D), 0), ("w_out", (D, D), 0),
    ("w_ffn1", (D, D_FF), 1), ("w_ffn2", (D_FF, D), 0),
]
REPLICATED = [
    ("g_mix", (1, D)), ("sgu_ln_w", (1, D)), ("sgu_ln_b", (1, D)), ("sgu_w", (1, SGU_G, SGU_C, SGU_C)),
    ("sgu_b", (1, SGU_G, SGU_C)), ("w0", (1, D)), ("a0", (1, D)), ("k_k", (1, D)), ("k_a", (1, D)), ("r_k", (1, D)),
    ("ln_x_w", (1, D)), ("ln_x_b", (1, D)), ("g_ffn", (1, D)), ("g_final", (D,)),
]
WEIGHT_ORDER = ["g_mix", "w_in", "sgu_ln_w", "sgu_ln_b", "sgu_w", "sgu_b", "w_proj_a", "shift_b", "w_lora_w", "w0",
                "a_lora_w", "a0", "g_lora_w", "k_k", "k_a", "r_k", "ln_x_w", "ln_x_b", "w_proj_b", "w_out", "g_ffn",
                "w_ffn1", "w_ffn2", "g_final"]


def _shard_shape(shape, axis):
    s = list(shape)
    s[axis] //= N_DEV
    return tuple(s)


def _round_up(n, m):
    return (n + m - 1) // m * m


def _pick(n, target):
    if n <= target:
        return n
    best = None
    for t in range(LANES, target + 1, LANES):
        if n % t == 0:
            best = t
    assert best is not None, (n, target)
    return best


def _matmul(name, a, b, mode, out_dtype=F32, tm=2048, tn=1024, tk=2048, out_blocks=None, epilogue=None, extras=(),
            out_dtypes=()):
    b_blocks = b.shape[0] if b.ndim == 3 else None
    bshape = b.shape if b.ndim == 2 else (b.shape[1], b.shape[0] * b.shape[2])
    if mode == "nn":
        (M, K), (K2, N) = a.shape, bshape
    elif mode == "nt":
        (M, K), (N, K2) = a.shape, bshape
    else:
        (K, M), (K2, N) = a.shape, bshape
    assert K == K2, (name, a.shape, b.shape)
    assert b_blocks is None or mode != "tn"
    assert out_blocks is None or mode == "tn"
    tn = min(tn, N // (out_blocks or 1), bshape[1] // b_blocks if (b_blocks and mode == "nn") else tn)
    tk = min(tk, bshape[1] // b_blocks if (b_blocks and mode == "nt") else tk)
    tm, tn, tk = _pick(M, tm), _pick(N, tn), _pick(K, tk)

    def vmem_bytes(tm, tk):
        tiles = tm * tk * a.dtype.itemsize + tk * tn * b.dtype.itemsize
        for dt in (out_dtypes if epilogue else (out_dtype,)):
            tiles += tm * tn * jnp.dtype(dt).itemsize
        for x in extras:
            arr = x[0] if isinstance(x, tuple) else x
            tiles += (tm if arr.shape[0] > 1 else 1) * tn * arr.dtype.itemsize
        return 2 * tiles + (tm * tn * 4 if K // tk > 1 else 0)

    def cost(tm, tk):
        ni, nj, nk = M // tm, N // tn, K // tk
        steps = ni * nj * nk
        acc_passes = steps * tm * tn * 8 if nk > 1 else 0
        a_reads = M * K * a.dtype.itemsize * (nj if nk > 1 else 1)
        b_reads = K * N * b.dtype.itemsize * (ni if (nj > 1 or nk > 1) else 1)
        return (steps * STEP_COST_BYTES + acc_passes + vmem_bytes(tm, tk) // 2
                + HBM_COST_RATIO * (a_reads + b_reads))

    options = [(m, k) for m in {_pick(M, max(t, LANES)) for t in (tm, tm // 2, tm // 4)}
               for k in {_pick(K, max(t, LANES)) for t in (tk, tk // 2, tk // 4)}
               if vmem_bytes(m, k) <= MATMUL_VMEM_BUDGET]
    tm, tk = min(options, key=lambda o: cost(*o))
    nk = K // tk
    dims = {"nn": (((1,), (0,)), ((), ())), "nt": (((1,), (1,)), ((), ())), "tn": (((0,), (0,)), ((), ()))}[mode]

    n_x, n_o = len(extras), len(out_dtypes) if epilogue else 1

    def body(a_ref, b_ref, *rest):
        x_refs, o_refs, acc = rest[:n_x], rest[n_x:n_x + n_o], rest[n_x + n_o:]
        part = lax.dot_general(a_ref[...].astype(BF16), b_ref[...].astype(BF16), dims, preferred_element_type=F32)

        def finish(res):
            outs = epilogue(res, *[r[...] for r in x_refs]) if epilogue else (res,)
            for r, v in zip(o_refs, outs):
                r[...] = v.astype(r.dtype)

        if nk == 1:
            finish(part)
            return
        acc_ref, k = acc[0], pl.program_id(2)

        @pl.when(k == 0)
        def _():
            acc_ref[...] = part

        @pl.when(k > 0)
        def _():
            acc_ref[...] += part

        @pl.when(k == nk - 1)
        def _():
            finish(acc_ref[...])

    a_spec = {"nn": pl.BlockSpec((tm, tk), lambda i, j, k: (i, k)), "nt": pl.BlockSpec((tm, tk), lambda i, j, k: (i, k)),
              "tn": pl.BlockSpec((tk, tm), lambda i, j, k: (k, i))}[mode]
    b_spec = {"nn": pl.BlockSpec((tk, tn), lambda i, j, k: (k, j)), "nt": pl.BlockSpec((tn, tk), lambda i, j, k: (j, k)),
              "tn": pl.BlockSpec((tk, tn), lambda i, j, k: (k, j))}[mode]
    if b_blocks and mode == "nn":
        per = b.shape[2] // tn
        b_spec = pl.BlockSpec((None, tk, tn), lambda i, j, k: (j // per, k, j % per))
    elif b_blocks:
        per = b.shape[2] // tk
        b_spec = pl.BlockSpec((None, tn, tk), lambda i, j, k: (k // per, j, k % per))
    out_spec = pl.BlockSpec((tm, tn), lambda i, j, k: (i, j))
    out_shape = jax.ShapeDtypeStruct((M, N), out_dtype)
    if out_blocks:
        per_o = N // out_blocks // tn
        out_spec = pl.BlockSpec((None, tm, tn), lambda i, j, k: (j // per_o, i, j % per_o))
        out_shape = jax.ShapeDtypeStruct((out_blocks, M, N // out_blocks), out_dtype)
    x_specs, x_args = [], []
    for x in extras:
        arr, off = x if isinstance(x, tuple) else (x, 0)
        if arr.shape[0] == 1:
            x_specs.append(pl.BlockSpec((1, tn), lambda i, j, k: (0, j)))
        else:
            x_specs.append(pl.BlockSpec((tm, tn), lambda i, j, k, off=off: (i, j + off)))
        x_args.append(arr)
    res = pl.pallas_call(
        body, name=name, grid=(M // tm, N // tn, nk), in_specs=[a_spec, b_spec] + x_specs,
        out_specs=[out_spec] * n_o if epilogue else out_spec,
        out_shape=[jax.ShapeDtypeStruct((M, N), dt) for dt in out_dtypes] if epilogue else out_shape,
        scratch_shapes=[pltpu.VMEM((tm, tn), F32)] if nk > 1 else [],
        compiler_params=pltpu.CompilerParams(dimension_semantics=("parallel", "parallel", "arbitrary"),
                                             vmem_limit_bytes=VMEM_LIMIT),
    )(a, b, *x_args)
    return res


class Rows:
    def __init__(self, arr, width=None, cb=0):
        self.arr, self.width, self.cb = arr, (arr.shape[1] if width is None else width), cb


class Heads:
    def __init__(self, arr):
        self.arr = arr


class Halo:
    def __init__(self, arr, side):
        self.arr, self.side = arr, side


def _rows_call(name, fn, ins, consts, outs, accs=(), tm=256, with_pid=False):
    T = next(o.arr.shape[1] if isinstance(o, Heads) else o.arr.shape[0] for o in ins if not isinstance(o, Halo))
    tm = min(tm, T)
    n_tiles = T // tm
    n_in, n_c, n_out = len(ins), len(consts), len(outs)
    in_specs, args = [], []
    for o in ins:
        if isinstance(o, Rows):
            in_specs.append(pl.BlockSpec((tm, o.width), lambda i, cb=o.cb: (i, cb)))
        elif isinstance(o, Heads):
            in_specs.append(pl.BlockSpec((NH, tm, HN), lambda i: (0, i, 0)))
        else:
            w = o.arr.shape[1]
            if o.side < 0:
                in_specs.append(pl.BlockSpec((8, w), lambda i: (jnp.maximum(i * (tm // 8) - 1, 0), 0)))
            else:
                in_specs.append(pl.BlockSpec((8, w), lambda i: (jnp.minimum((i + 1) * (tm // 8), T // 8 - 1), 0)))
        args.append(o.arr)
    for c in consts:
        in_specs.append(pl.BlockSpec(c.shape, lambda i, nd=c.ndim: (0,) * nd))
        args.append(c)
    out_specs, out_shape = [], []
    for o in outs:
        if o[0] == "rows":
            out_specs.append(pl.BlockSpec((tm, o[1]), lambda i: (i, 0)))
            out_shape.append(jax.ShapeDtypeStruct((T, o[1]), o[2]))
        else:
            out_specs.append(pl.BlockSpec((NH, tm, HN), lambda i: (0, i, 0)))
            out_shape.append(jax.ShapeDtypeStruct((NH, T, HN), o[1]))
    for shape, dt in accs:
        out_specs.append(pl.BlockSpec(shape, lambda i, nd=len(shape): (0,) * nd))
        out_shape.append(jax.ShapeDtypeStruct(shape, dt))

    def body(*refs):
        i = pl.program_id(0)
        vals = []
        vals = [r[...] for r in refs[:n_in + n_c]]
        res = fn(i, n_tiles, *vals) if with_pid else fn(*vals)
        out_refs = refs[n_in + n_c:]
        for r, v in zip(out_refs[:n_out], res[:n_out]):
            r[...] = v.astype(r.dtype)
        if accs:
            @pl.when(i == 0)
            def _():
                for r in out_refs[n_out:]:
                    r[...] = jnp.zeros_like(r)

            for r, v in zip(out_refs[n_out:], res[n_out:]):
                r[...] += v.astype(r.dtype)

    res = pl.pallas_call(
        body, name=name, grid=(n_tiles,), in_specs=in_specs, out_specs=out_specs, out_shape=out_shape,
        compiler_params=pltpu.CompilerParams(dimension_semantics=("arbitrary",), vmem_limit_bytes=VMEM_LIMIT),
    )(*args)
    return res


def _rms(x, g):
    return x * lax.rsqrt(jnp.mean(x * x, axis=-1, keepdims=True) + NORM_EPS) * g


def _gelu(x):
    return 0.5 * x * (1.0 + lax.erf(x * 0.7071067811865476))


def _sigmoid(x):
    return 1.0 / (1.0 + jnp.exp(-x))


def _bdot(a, b):
    return jnp.dot(a.astype(BF16), b.astype(BF16), preferred_element_type=F32)


def _to_heads(x):
    return jnp.concatenate([x[:, h * HN:(h + 1) * HN][None] for h in range(NH)], axis=0)


def _from_heads(xh):
    return jnp.concatenate([xh[h] for h in range(NH)], axis=-1)


def _sgu_fn(p, ln_w, ln_b, sw, sbt):
    z = _gelu(p)
    u, v = z[:, :D], z[:, D:]
    mu = jnp.mean(v, axis=-1, keepdims=True)
    var = jnp.mean(jnp.square(v - mu), axis=-1, keepdims=True)
    vn = (v - mu) * lax.rsqrt(var + LN_EPS) * ln_w + ln_b
    ri = lax.broadcasted_iota(jnp.int32, (SGU_C, SGU_C), 0)
    ci = lax.broadcasted_iota(jnp.int32, (SGU_C, SGU_C), 1)
    mask = (ci <= ri).astype(F32)
    dg = D // SGU_G
    parts = []
    for g in range(SGU_G):
        parts.append(_bdot(sw[g] * mask, vn[:, g * dg:(g + 1) * dg]) + sbt[:, g:g + 1])
    return u * jnp.concatenate(parts, axis=-1)


def _pre_fn(qr, qk, qv, qxw, qxa, qxg, wl, w0, al, a0, gl, k_k, k_a):
    w = -jax.nn.softplus(-(w0 + _bdot(jnp.tanh(qxw), wl))) - 0.5
    lw = -jnp.exp(w)
    aa = _sigmoid(a0 + _bdot(qxa, al))
    g = _bdot(_sigmoid(qxg), gl)
    kk = _to_heads(qk * k_k)
    kk = kk / jnp.maximum(jnp.sqrt(jnp.sum(kk * kk, axis=-1, keepdims=True)), 1e-12)
    k2 = qk * (1.0 + (aa - 1.0) * k_a)
    return _to_heads(qr), _to_heads(lw), _to_heads(k2), _to_heads(qv), kk, _to_heads(aa), g


def _post_fn(o, r, k2, v, g, ln_w, ln_b, r_k):
    mu = jnp.mean(o, axis=-1, keepdims=True)
    d = o - mu
    var = jnp.mean(d * d, axis=-1, keepdims=True)
    on = d * lax.rsqrt(var + GN_EPS) * ln_w + ln_b
    bonus = jnp.sum(r * k2 * r_k, axis=-1, keepdims=True) * v
    return _from_heads(on + bonus) * g


def _gate_fn(pg, ya, yb):
    return _sigmoid(pg[:, :D]) * ya + _sigmoid(pg[:, D:]) * yb


def _bmm(x, y, cx, cy, out_path=False):
    return lax.dot_general(x, y, (((cx,), (cy,)), ((0,), (0,))),
                           precision=SCAN_OUT_PRECISION if out_path else SCAN_PRECISION, preferred_element_type=F32)


def _unit_lower_inverse(M):
    C = M.shape[1]
    ti = lax.broadcasted_iota(jnp.int32, (C, C), 0)
    tj = lax.broadcasted_iota(jnp.int32, (C, C), 1)
    eye = (ti == tj).astype(F32)
    same = lambda b: (ti // b == tj // b).astype(F32)
    X = -(M * same(SOLVE_B))
    inv = eye + X
    span = 1
    while 2 * span < SOLVE_B:
        X = _bmm(X, X, 2, 1)
        inv = inv + _bmm(inv, X, 2, 1)
        span *= 2
    b = SOLVE_B
    while b < C:
        low = M * (same(2 * b) - same(b))
        inv = inv - _bmm(_bmm(inv, low, 2, 1), inv, 2, 1)
        b *= 2
    return inv


@jax.custom_vjp
def _unit_lower_solve(inv, M, y):
    return _bmm(inv, y, 2, 1)


def _unit_lower_solve_fwd(inv, M, y):
    u = _bmm(inv, y, 2, 1)
    return u, (inv, u)


def _unit_lower_solve_bwd(res, du):
    inv, u = res
    dy = _bmm(inv, du, 1, 1)
    return jnp.zeros_like(inv), -_bmm(dy, u, 2, 2), dy


_unit_lower_solve.defvjp(_unit_lower_solve_fwd, _unit_lower_solve_bwd)


def _sum_over_time(x, reverse):
    C = x.shape[1]
    ti = lax.broadcasted_iota(jnp.int32, (C, C), 0)
    tj = lax.broadcasted_iota(jnp.int32, (C, C), 1)
    ones = jnp.broadcast_to(((tj >= ti) if reverse else (tj <= ti)).astype(BF16), (x.shape[0], C, C))
    hi = x.astype(BF16)
    r1 = x - hi.astype(F32)
    mid = r1.astype(BF16)
    lo = (r1 - mid.astype(F32)).astype(BF16)
    dn = (((2,), (1,)), ((0,), (0,)))
    return sum(lax.dot_general(ones, p, dn, preferred_element_type=F32) for p in (lo, mid, hi))


@jax.custom_vjp
def _time_cumsum(lw):
    return _sum_over_time(lw, reverse=False)


_time_cumsum.defvjp(lambda lw: (_sum_over_time(lw, reverse=False), None),
                    lambda _, d: (_sum_over_time(d, reverse=True),))


def _chunk_fn(S0, r, lw, k, v, kk, a, inv=None):
    C = SCAN_C
    bmm = _bmm
    ti = lax.broadcasted_iota(jnp.int32, (C, C), 0)
    tj = lax.broadcasted_iota(jnp.int32, (C, C), 1)
    incl = (tj <= ti).astype(F32)
    strict = (tj < ti).astype(F32)
    cum = _time_cumsum(lw)
    g_in, g_ex, g_inv = jnp.exp(cum), jnp.exp(cum - lw), jnp.exp(-cum)
    kkt, rt = kk * g_ex, r * g_in
    bk = jnp.concatenate([kk * a * g_inv, k * g_inv], axis=1)
    A = bmm(kkt, bk, 2, 2)
    M = A[:, :, :C] * strict
    n_mask = jnp.concatenate([jnp.zeros((C, C), F32), strict], axis=1)
    zv = jnp.concatenate([jnp.zeros_like(v), v], axis=1)
    s0_side = bmm(jnp.concatenate([kkt, rt], axis=1), S0, 2, 2, out_path=True)
    if inv is None:
        inv = lax.stop_gradient(_unit_lower_inverse(M))
    y = _unit_lower_solve(inv, M, s0_side[:, :C] + bmm(A * n_mask, zv, 2, 1, out_path=True))
    z = jnp.concatenate([-y, v], axis=1)
    attn = bmm(rt, bk, 2, 2) * jnp.concatenate([incl, incl], axis=1)
    O = s0_side[:, C:] + bmm(attn, z, 2, 1, out_path=True)
    S1 = (S0 + bmm(z, bk, 1, 1, out_path=True)) * g_in[:, C - 1:C, :]
    return O, S1, inv


def _scan_fwd(r, lw, k, v, kk, a, ex=None, tb=256):
    assert SCAN_C == HN
    T = r.shape[1]
    tb = min(tb, T)
    n_chunks = tb // SCAN_C
    nb = T // tb
    nx = ex.nb if ex else 0

    def body(*refs):
        r_ref, lw_ref, k_ref, v_ref, kk_ref, a_ref = refs[:6]
        x_in, (o_ref, s0_ref), x_out = refs[6:6 + nx], refs[6 + nx:8 + nx], refs[8 + nx:8 + 2 * nx]
        s_ref, sems = refs[8 + 2 * nx], refs[9 + 2 * nx:]

        plan = ex.schedule(nb) if ex else []

        @pl.when(pl.program_id(0) == 0)
        def _():
            s_ref[...] = jnp.zeros_like(s_ref)
            for at, action in plan[:1]:
                action(x_in, x_out, sems)

        def step(c, carry):
            sl = pl.ds(pl.multiple_of(c * SCAN_C, SCAN_C), SCAN_C)
            S0 = s_ref[...]
            O, S1, inv = _chunk_fn(S0, r_ref[:, sl, :], lw_ref[:, sl, :], k_ref[:, sl, :], v_ref[:, sl, :],
                                   kk_ref[:, sl, :], a_ref[:, sl, :])
            o_ref[:, sl, :] = O
            s0_ref[c, 0] = S0
            s0_ref[c, 1] = inv
            s_ref[...] = S1
            return carry

        lax.fori_loop(0, n_chunks, step, 0)

        for at, action in plan[1:]:
            pl.when(pl.program_id(0) == at)(functools.partial(action, x_in, x_out, sems))

    hm = pl.BlockSpec((NH, tb, HN), lambda i: (0, i, 0))
    res = pl.pallas_call(
        body, name="rwkv_scan_fwd", grid=(nb,), in_specs=[hm] * 6 + (ex.any_specs if ex else []),
        out_specs=[hm, pl.BlockSpec((n_chunks, 2, NH, HN, HN), lambda i: (i, 0, 0, 0, 0))]
        + (ex.any_specs if ex else []),
        out_shape=[jax.ShapeDtypeStruct((NH, T, HN), F32), jax.ShapeDtypeStruct((T // SCAN_C, 2, NH, HN, HN), F32)]
        + (ex.out_shape if ex else []),
        scratch_shapes=[pltpu.VMEM((NH, HN, HN), F32)] + (ex.sem_shapes if ex else []),
        compiler_params=pltpu.CompilerParams(dimension_semantics=("arbitrary",), vmem_limit_bytes=VMEM_LIMIT),
    )(r, lw, k, v, kk, a, *(ex.bufs if ex else []))
    return res[0], res[1], list(res[2:])


def _scan_bwd(r, lw, k, v, kk, a, s0s, do, ex=None, tb=128):
    T = r.shape[1]
    tb = min(tb, T)
    n_chunks = tb // SCAN_C
    nb = T // tb
    nx = ex.nb if ex else 0

    def body(*refs):
        r_ref, lw_ref, k_ref, v_ref, kk_ref, a_ref, s0_ref, do_ref = refs[:8]
        x_in, (dr, dlw, dk, dv, dkk, da), x_out = refs[8:8 + nx], refs[8 + nx:14 + nx], refs[14 + nx:14 + 2 * nx]
        ds_ref, sems = refs[14 + 2 * nx], refs[15 + 2 * nx:]

        plan = ex.schedule(nb) if ex else []

        @pl.when(pl.program_id(0) == 0)
        def _():
            ds_ref[...] = jnp.zeros_like(ds_ref)
            for at, action in plan[:1]:
                action(x_in, x_out, sems)

        def step(j, carry):
            c = n_chunks - 1 - j
            sl = pl.ds(pl.multiple_of(c * SCAN_C, SCAN_C), SCAN_C)
            inv = s0_ref[c, 1]
            _, vjp = jax.vjp(lambda *t: _chunk_fn(*t, inv=inv)[:2], s0_ref[c, 0], r_ref[:, sl, :], lw_ref[:, sl, :],
                             k_ref[:, sl, :], v_ref[:, sl, :], kk_ref[:, sl, :], a_ref[:, sl, :])
            g = vjp((do_ref[:, sl, :], ds_ref[...]))
            ds_ref[...] = g[0]
            for ref, val in zip((dr, dlw, dk, dv, dkk, da), g[1:]):
                ref[:, sl, :] = val
            return carry

        lax.fori_loop(0, n_chunks, step, 0)

        for at, action in plan[1:]:
            pl.when(pl.program_id(0) == at)(functools.partial(action, x_in, x_out, sems))

    hm = pl.BlockSpec((NH, tb, HN), lambda i: (0, nb - 1 - i, 0))
    res = pl.pallas_call(
        body, name="rwkv_scan_bwd", grid=(nb,),
        in_specs=[hm] * 6 + [pl.BlockSpec((n_chunks, 2, NH, HN, HN), lambda i: (nb - 1 - i, 0, 0, 0, 0)), hm]
        + (ex.any_specs if ex else []),
        out_specs=[hm] * 6 + (ex.any_specs if ex else []),
        out_shape=[jax.ShapeDtypeStruct((NH, T, HN), F32)] * 6 + (ex.out_shape if ex else []),
        scratch_shapes=[pltpu.VMEM((NH, HN, HN), F32)] + (ex.sem_shapes if ex else []),
        compiler_params=pltpu.CompilerParams(dimension_semantics=("arbitrary",), vmem_limit_bytes=VMEM_LIMIT),
    )(r, lw, k, v, kk, a, s0s, do, *(ex.bufs if ex else []))
    return list(res[:6]), list(res[6:])


def _shift_down(i, p, prev8):
    first = jnp.where(i > 0, prev8[7:8, :], 0.0)
    row = lax.broadcasted_iota(jnp.int32, p.shape, 0)
    return jnp.where(row == 0, first, pltpu.roll(p, 1, axis=0))


def _mix_fwd(p, sb, tm=256):
    def fn(i, n, p, prev8, sb):
        return (p * sb[0:1] + _shift_down(i, p, prev8) * sb[1:2],)
    return _rows_call("shift_mix_fwd", fn, [Rows(p), Halo(p, -1)], [sb], [("rows", p.shape[1], F32)], tm=tm,
                      with_pid=True)[0]


def _mix_bwd(dq, p, sb, tm=256):
    def fn(i, n, dq, next8, p, prev8, sb):
        ps = _shift_down(i, p, prev8)
        d1 = dq * sb[1:2]
        last = jnp.where(i < n - 1, next8[0:1, :] * sb[1:2], 0.0)
        row = lax.broadcasted_iota(jnp.int32, dq.shape, 0)
        up = jnp.where(row == dq.shape[0] - 1, last, pltpu.roll(d1, dq.shape[0] - 1, axis=0))
        return (dq * sb[0:1] + up, jnp.sum(dq * p, axis=0, keepdims=True), jnp.sum(dq * ps, axis=0, keepdims=True))
    w = p.shape[1]
    return _rows_call("shift_mix_bwd", fn, [Rows(dq), Halo(dq, +1), Rows(p), Halo(p, -1)], [sb], [("rows", w, F32)],
                      accs=[((1, w), F32), ((1, w), F32)], tm=tm, with_pid=True)


def _local_step(x, target, W, late_weights=None, early_grads=None):
    G = {}
    a = _rows_call("norm_mix_fwd", lambda x, g: (_rms(x, g),), [Rows(x)], [W["g_mix"]], [("rows", D, BF16)])[0]
    p_sgu = _matmul("proj_sgu", a, W["w_sgu_t"], "nt")
    p_rw = _matmul("proj_rwkv", a, W["w_rw_t"], "nt")
    p_gate = _matmul("proj_gate", a, W["w_gate_t"], "nt")

    sgu_consts = [W["sgu_ln_w"], W["sgu_ln_b"], W["sgu_w"], W["sgu_bt"]]
    s = _rows_call("sgu_fwd", lambda *t: (_sgu_fn(*t),), [Rows(p_sgu)], sgu_consts, [("rows", D, BF16)], tm=SGU_C)[0]

    q = _mix_fwd(p_rw, W["sb"])
    q_ins = [Rows(q, D, 0), Rows(q, D, 1), Rows(q, D, 2), Rows(q, 128, 24), Rows(q, 128, 25), Rows(q, 256, 13)]
    pre_consts = [W["w_lora"], W["w0"], W["a_lora"], W["a0"], W["g_lora"], W["k_k"], W["k_a"]]
    r_h, lw_h, k_h, v_h, kk_h, a_h, g_gate = _rows_call(
        "rwkv_pre_fwd", _pre_fn, q_ins, pre_consts, [("heads", F32)] * 6 + [("rows", D, F32)], tm=128)
    o_h, s0s, got = _scan_fwd(r_h, lw_h, k_h, v_h, kk_h, a_h, ex=late_weights[0] if late_weights else None)
    if late_weights:
        W = {**W, **late_weights[1](got)}
    y_a = _matmul("proj_a", s, W["w_proj_a"], "nn")
    post_ins = [Heads(o_h), Heads(r_h), Heads(k_h), Heads(v_h), Rows(g_gate)]
    post_consts = [W[n].reshape(NH, 1, HN) for n in ("ln_x_w", "ln_x_b", "r_k")]
    z_b = _rows_call("rwkv_post_fwd", lambda *t: (_post_fn(*t),), post_ins, post_consts, [("rows", D, BF16)], tm=128)[0]
    y_b, mixed = _matmul("proj_b", z_b, W["w_proj_b"], "nn", extras=[(p_gate, 0), (p_gate, 1), y_a],
                         epilogue=lambda yb, ga, gb, ya: (yb, _sigmoid(ga) * ya + _sigmoid(gb) * yb),
                         out_dtypes=(F32, BF16))
    gate_ins = [Rows(p_gate), Rows(y_a), Rows(y_b)]

    def res1(mo, x, g):
        h1 = x + mo
        return h1, _rms(h1, g)
    h1, f = _matmul("proj_out", mixed, W["w_out"], "nn", extras=[x, W["g_ffn"]], epilogue=res1,
                    out_dtypes=(F32, BF16))

    def relu_sq(u):
        r = jnp.maximum(u, 0.0)
        return r, r * r
    r1, act = _matmul("ffn_up", f, W["w_ffn1"], "nn", epilogue=relu_sq, out_dtypes=(BF16, BF16))
    ff = _matmul("ffn_down", act, W["w_ffn2"], "nn")

    def head(h1, ff, tgt, g):
        def f_(h1, ff, g):
            y = _rms(h1 + ff, g)
            return 0.5 * jnp.sum(jnp.mean(jnp.square(y - tgt), axis=-1))
        loss, (dh2, _, dg) = jax.value_and_grad(f_, argnums=(0, 1, 2))(h1, ff, g)
        return dh2, jnp.full((8, LANES), loss, F32), dg
    dh2, loss_acc, G["g_final"] = _rows_call("loss_head", head, [Rows(h1), Rows(ff), Rows(target)], [W["g_final"]],
                                             [("rows", D, F32)], accs=[((8, LANES), F32), ((1, D), F32)])

    d_u1 = _matmul("ffn_down_dx", dh2, W["w_ffn2"], "nt", extras=[r1], out_dtypes=(BF16,),
                   epilogue=lambda d_act, r: (d_act * 2.0 * r.astype(F32),))[0]
    G["w_ffn2"] = _matmul("ffn_down_dw", act, dh2, "tn", out_dtype=GRAD_PAYLOAD)
    d_f = _matmul("ffn_up_dx", d_u1, W["w_ffn1"], "nt")
    G["w_ffn1"] = _matmul("ffn_up_dw", f, d_u1, "tn", out_blocks=N_DEV, out_dtype=GRAD_PAYLOAD)

    def res1_bwd(h1, d_f, dh2, g):
        _, vjp = jax.vjp(_rms, h1, g)
        dh, dg = vjp(d_f)
        return dh2 + dh, dg
    dh1, G["g_ffn"] = _rows_call("residual_norm_bwd", res1_bwd, [Rows(h1), Rows(d_f), Rows(dh2)],
                                 [W["g_ffn"]], [("rows", D, F32)], accs=[((1, D), F32)])
    d_mixed = _matmul("proj_out_dx", dh1, W["w_out"], "nt")
    G["w_out"] = _matmul("proj_out_dw", mixed, dh1, "tn", out_dtype=GRAD_PAYLOAD)

    def gate_bwd(pg, ya, yb, dm):
        _, vjp = jax.vjp(_gate_fn, pg, ya, yb)
        return vjp(dm)
    d_gate, d_ya, d_yb = _rows_call("gate_bwd", gate_bwd, gate_ins + [Rows(d_mixed)], [],
                                    [("rows", 2 * D, F32), ("rows", D, BF16), ("rows", D, BF16)])

    d_s = _matmul("proj_a_dx", d_ya, W["w_proj_a"], "nt")
    G["w_proj_a"] = _matmul("proj_a_dw", s, d_ya, "tn", out_dtype=GRAD_PAYLOAD)

    def sgu_bwd(p, ds, *c):
        _, vjp = jax.vjp(_sgu_fn, p, *c)
        return vjp(ds)
    d_p_sgu, G["sgu_ln_w"], G["sgu_ln_b"], G["sgu_w"], G["sgu_bt"] = _rows_call(
        "sgu_bwd", sgu_bwd, [Rows(p_sgu), Rows(d_s)], sgu_consts, [("rows", 2 * D, F32)],
        accs=[((1, D), F32), ((1, D), F32), ((SGU_G, SGU_C, SGU_C), F32), ((SGU_C, SGU_G), F32)], tm=SGU_C)

    d_zb = _matmul("proj_b_dx", d_yb, W["w_proj_b"], "nt")
    G["w_proj_b"] = _matmul("proj_b_dw", z_b, d_yb, "tn", out_dtype=GRAD_PAYLOAD)

    def post_bwd(o, r, k2, v, g, dz, *c):
        _, vjp = jax.vjp(_post_fn, o, r, k2, v, g, *c)
        return vjp(dz)
    do_h, dr1, dk1, dv1, d_g, g_lnw, g_lnb, g_rk = _rows_call(
        "rwkv_post_bwd", post_bwd, post_ins + [Rows(d_zb)], post_consts, [("heads", F32)] * 4 + [("rows", D, F32)],
        accs=[((NH, 1, HN), F32)] * 3, tm=128)
    G["ln_x_w"], G["ln_x_b"], G["r_k"] = (t.reshape(1, D) for t in (g_lnw, g_lnb, g_rk))
    (dr2, dlw, dk2, dv2, dkk, daa), early = _scan_bwd(r_h, lw_h, k_h, v_h, kk_h, a_h, s0s, do_h,
                                                      ex=early_grads(G) if early_grads else None)

    def pre_bwd(qr, qk, qv, qxw, qxa, qxg, dr1, dr2, dlw, dk1, dk2, dv1, dv2, dkk, daa, dg, *c):
        _, vjp = jax.vjp(_pre_fn, qr, qk, qv, qxw, qxa, qxg, *c)
        g = vjp((dr1 + dr2, dlw, dk1 + dk2, dv1 + dv2, dkk, daa, dg))
        dq = jnp.concatenate(g[:6], axis=-1)
        return (dq,) + tuple(g[6:])
    pre_b_ins = q_ins + [Heads(dr1), Heads(dr2), Heads(dlw), Heads(dk1), Heads(dk2), Heads(dv1), Heads(dv2),
                         Heads(dkk), Heads(daa), Rows(d_g)]
    d_q, G["w_lora"], G["w0"], G["a_lora"], G["a0"], G["g_lora"], G["k_k"], G["k_a"] = _rows_call(
        "rwkv_pre_bwd", pre_bwd, pre_b_ins, pre_consts, [("rows", RW_INT, F32)],
        accs=[((128, D), F32), ((1, D), F32), ((128, D), F32), ((1, D), F32), ((256, D), F32), ((1, D), F32),
              ((1, D), F32)], tm=128)
    d_p_rw, dsb0, dsb1 = _mix_bwd(d_q, p_rw, W["sb"])
    G["sb"] = jnp.concatenate([dsb0, dsb1], axis=0)

    G["w_sgu_t"] = _matmul("proj_sgu_dw", d_p_sgu, a, "tn", out_dtype=GRAD_PAYLOAD)
    G["w_rw_t"] = _matmul("proj_rwkv_dw", d_p_rw, a, "tn", out_dtype=GRAD_PAYLOAD)
    G["w_gate_t"] = _matmul("proj_gate_dw", d_gate, a, "tn", out_dtype=GRAD_PAYLOAD)
    da1 = _matmul("proj_sgu_dx", d_p_sgu, W["w_sgu_t"], "nn")
    da2 = _matmul("proj_rwkv_dx", d_p_rw, W["w_rw_t"], "nn")
    da3 = _matmul("proj_gate_dx", d_gate, W["w_gate_t"], "nn")

    def norm1_bwd(x, da1, da2, da3, dh1, g):
        _, vjp = jax.vjp(_rms, x, g)
        dx, dg = vjp(da1 + da2 + da3)
        return dh1 + dx, dg
    dx, G["g_mix"] = _rows_call("norm_mix_bwd", norm1_bwd, [Rows(x), Rows(da1), Rows(da2), Rows(da3), Rows(dh1)],
                                [W["g_mix"]], [("rows", D, F32)], accs=[((1, D), F32)])
    return loss_acc[0, 0], dx, G, early


class Exchange:
    def __init__(self, bufs, gathers):
        self.bufs, self.gathers, self.nb = list(bufs), list(gathers), len(bufs)
        self.any_specs = [pl.BlockSpec(memory_space=pl.ANY)] * self.nb
        self.out_shape = [jax.ShapeDtypeStruct((N_DEV,) + (b.shape if g else b.shape[1:]), b.dtype)
                          for b, g in zip(self.bufs, self.gathers)]
        n = (N_DEV - 1) * self.nb
        self.sem_shapes = [pltpu.SemaphoreType.DMA((n,)), pltpu.SemaphoreType.DMA((n,)),
                           pltpu.SemaphoreType.DMA((self.nb,))]

    def _copies(self, in_refs, out_refs, sems):
        send_sems, recv_sems, local_sems = sems
        x, y, c = lax.axis_index("x"), lax.axis_index("y"), lax.axis_index("c")
        me = 4 * x + 2 * y + c

        def src(b, dest):
            return in_refs[b] if self.gathers[b] else in_refs[b].at[dest]

        local = [pltpu.make_async_copy(src(b, me), out_refs[b].at[me], local_sems.at[b]) for b in range(self.nb)]
        sends, recvs = [], []
        for kbits in range(1, N_DEV):
            px = 1 - x if kbits & 4 else x
            py = 1 - y if kbits & 2 else y
            pc = 1 - c if kbits & 1 else c
            peer = 4 * px + 2 * py + pc
            for b in range(self.nb):
                s = (kbits - 1) * self.nb + b
                sends.append(pltpu.make_async_remote_copy(
                    src_ref=src(b, peer), dst_ref=out_refs[b].at[me], send_sem=send_sems.at[s],
                    recv_sem=recv_sems.at[s], device_id=(px, py, pc), device_id_type=pl.DeviceIdType.MESH))
                recvs.append(pltpu.make_async_remote_copy(
                    src_ref=src(b, peer), dst_ref=out_refs[b].at[peer], send_sem=send_sems.at[s],
                    recv_sem=recv_sems.at[s], device_id=(px, py, pc), device_id_type=pl.DeviceIdType.MESH))
        return local, sends, recvs

    def start(self, in_refs, out_refs, sems):
        local, sends, _ = self._copies(in_refs, out_refs, sems)
        for cp in sends + local:
            cp.start()

    def wait(self, in_refs, out_refs, sems):
        local, sends, recvs = self._copies(in_refs, out_refs, sems)
        for cp in recvs:
            cp.wait_recv()
        for cp in sends:
            cp.wait_send()
        for cp in local:
            cp.wait()

    def schedule(self, n_steps):
        return [(0, self.start), (n_steps - 1, self.wait)]


def _exchange(name, bufs, gather):
    ex = Exchange(bufs, gather if isinstance(gather, (list, tuple)) else [gather] * len(bufs))

    def body(*refs):
        in_refs, out_refs, sems = refs[:ex.nb], refs[ex.nb:2 * ex.nb], refs[2 * ex.nb:]
        ex.start(in_refs, out_refs, sems)
        ex.wait(in_refs, out_refs, sems)

    return pl.pallas_call(body, name=name, in_specs=ex.any_specs, out_specs=ex.any_specs, out_shape=ex.out_shape,
                          scratch_shapes=ex.sem_shapes)(*ex.bufs)


N_CHIP = 4


def _pair_exchange(name, blocks):
    def body(b_ref, got_ref, send_sems, recv_sems):
        x, y, c = lax.axis_index("x"), lax.axis_index("y"), lax.axis_index("c")
        copies = [pltpu.make_async_remote_copy(
            src_ref=b_ref.at[2 * q + 1 - c], dst_ref=got_ref.at[q], send_sem=send_sems.at[q],
            recv_sem=recv_sems.at[q], device_id=(x, y, 1 - c), device_id_type=pl.DeviceIdType.MESH)
            for q in range(N_CHIP)]
        for cp in copies:
            cp.start()
        for cp in copies:
            cp.wait_recv()
        for cp in copies:
            cp.wait_send()

    any_spec = pl.BlockSpec(memory_space=pl.ANY)
    return pl.pallas_call(
        body, name=name, in_specs=[any_spec], out_specs=any_spec,
        out_shape=jax.ShapeDtypeStruct((N_CHIP,) + blocks.shape[1:], blocks.dtype),
        scratch_shapes=[pltpu.SemaphoreType.DMA((N_CHIP,))] * 2,
    )(blocks)


def _pair_sum(name, blocks, got, core):
    _, R, Wd = blocks.shape

    def body(c_ref, a_ref, b_ref, o_ref):
        o_ref[...] = (a_ref[...].astype(F32) + b_ref[...].astype(F32)).astype(o_ref.dtype)

    return pl.pallas_call(
        body, name=name,
        grid_spec=pltpu.PrefetchScalarGridSpec(
            num_scalar_prefetch=1, grid=(N_CHIP,),
            in_specs=[pl.BlockSpec((None, R, Wd), lambda q, c_ref: (2 * q + c_ref[0], 0, 0)),
                      pl.BlockSpec((None, R, Wd), lambda q, c_ref: (q, 0, 0))],
            out_specs=pl.BlockSpec((None, R, Wd), lambda q, c_ref: (q, 0, 0))),
        out_shape=jax.ShapeDtypeStruct(got.shape, blocks.dtype),
        compiler_params=pltpu.CompilerParams(dimension_semantics=("parallel",), vmem_limit_bytes=VMEM_LIMIT),
    )(core, blocks, got)


def _chip_exchange(name, sums, ex):
    def body(*refs):
        s_ref, x_in = refs[0], refs[1:1 + ex.nb]
        slots_ref, x_out = refs[1 + ex.nb], refs[2 + ex.nb:2 + 2 * ex.nb]
        send_sems, recv_sems = refs[2 + 2 * ex.nb:4 + 2 * ex.nb]
        ex_sems = refs[4 + 2 * ex.nb:]
        x, y, c = lax.axis_index("x"), lax.axis_index("y"), lax.axis_index("c")
        my_q = 2 * x + y
        sends, recvs = [], []
        for kbits in range(1, N_CHIP):
            px = 1 - x if kbits & 2 else x
            py = 1 - y if kbits & 1 else y
            peer_q = 2 * px + py
            sends.append(pltpu.make_async_remote_copy(
                src_ref=s_ref.at[peer_q], dst_ref=slots_ref.at[my_q], send_sem=send_sems.at[kbits - 1],
                recv_sem=recv_sems.at[kbits - 1], device_id=(px, py, c), device_id_type=pl.DeviceIdType.MESH))
            recvs.append(pltpu.make_async_remote_copy(
                src_ref=s_ref.at[peer_q], dst_ref=slots_ref.at[peer_q], send_sem=send_sems.at[kbits - 1],
                recv_sem=recv_sems.at[kbits - 1], device_id=(px, py, c), device_id_type=pl.DeviceIdType.MESH))
        for cp in sends:
            cp.start()
        ex.start(x_in, x_out, ex_sems)
        for cp in recvs:
            cp.wait_recv()
        for cp in sends:
            cp.wait_send()
        ex.wait(x_in, x_out, ex_sems)

    any_spec = pl.BlockSpec(memory_space=pl.ANY)
    res = pl.pallas_call(
        body, name=name, in_specs=[any_spec] + ex.any_specs, out_specs=[any_spec] + ex.any_specs,
        out_shape=[jax.ShapeDtypeStruct(sums.shape, sums.dtype)] + ex.out_shape,
        scratch_shapes=[pltpu.SemaphoreType.DMA((N_CHIP - 1,)), pltpu.SemaphoreType.DMA((N_CHIP - 1,))]
        + ex.sem_shapes,
    )(sums, *ex.bufs)
    return res[0], list(res[1:])


def _adamw(name, slots, w, m, v, tr=256):
    R, Wd = w.shape[-2:]
    depth_axis = w.ndim == 3
    if R % tr == 0:
        tc = Wd
    else:
        tr, tc = R, (256 if (Wd % 256 == 0 and R > 256) else Wd)

    def body(s_ref, w_ref, m_ref, v_ref, g_out, d_out, m_out, v_out):
        g = s_ref[0].astype(F32)
        for j in range(1, slots.shape[0]):
            g = g + s_ref[j].astype(F32)
        m_new = ADAM_B1 * m_ref[...] + (1.0 - ADAM_B1) * g
        v_new = ADAM_B2 * v_ref[...] + (1.0 - ADAM_B2) * jnp.square(g)
        m_hat = m_new / (1.0 - ADAM_B1 ** ADAM_STEP)
        v_hat = v_new / (1.0 - ADAM_B2 ** ADAM_STEP)
        g_out[...] = g
        d_out[...] = -ADAM_LR * (m_hat / (jnp.sqrt(v_hat) + ADAM_EPS) + ADAM_WD * w_ref[...])
        m_out[...] = m_new
        v_out[...] = v_new

    if depth_axis:
        row = pl.BlockSpec((None, tr, tc), lambda i, j: (0, i, j))
    else:
        row = pl.BlockSpec((tr, tc), lambda i, j: (i, j))
    return pl.pallas_call(
        body, name=name, grid=(R // tr, Wd // tc),
        in_specs=[pl.BlockSpec((slots.shape[0], tr, tc), lambda i, j: (0, i, j)), row, row, row],
        out_specs=[row] * 4, out_shape=[jax.ShapeDtypeStruct(w.shape, F32)] * 4,
        compiler_params=pltpu.CompilerParams(dimension_semantics=("parallel", "parallel"),
                                             vmem_limit_bytes=VMEM_LIMIT),
    )(slots, w, m, v)


PACK_W = 1024
PACKED = [(n, s) for n, s in REPLICATED if n != "sgu_w"]
_SMALL_SIZES = [int(np.prod(s)) for _, s in PACKED]
_SMALL_ROWS = _round_up(_round_up(sum(_SMALL_SIZES) + PACK_W, PACK_W) // PACK_W, 8)
_LOSS_AT = sum(_SMALL_SIZES)
W_IN_SHARD = P_TOTAL // N_DEV


def _pack_rows(parts, rows, dtype):
    flat = jnp.concatenate([p.reshape(-1).astype(dtype) for p in parts])
    return jnp.pad(flat, (0, rows * PACK_W - flat.shape[0])).reshape(rows, PACK_W)


def _w_in_groups_t(blocks):
    wt = blocks.reshape(P_TOTAL, D)
    o, c = 2 * D, 2 * D + 3 * D
    z = lambda r: jnp.zeros((r, D), wt.dtype)
    rw = jnp.concatenate([wt[o:c], wt[c:c + L_W], z(128 - L_W), wt[c + L_W:c + L_W + L_A], z(128 - L_A),
                          wt[c + L_W + L_A:o + C_B], z(256 - L_G)], axis=0)
    return wt[:o], rw, wt[o + C_B:]


def _w_in_grad_blocks(g_sgu_t, g_rw_t, g_gate_t):
    c = 3 * D
    full = jnp.concatenate([g_sgu_t, g_rw_t[:c], g_rw_t[c:c + L_W], g_rw_t[c + 128:c + 128 + L_A],
                            g_rw_t[c + 256:c + 256 + L_G], g_gate_t], axis=0)
    return full.reshape(N_DEV, W_IN_SHARD, D)


def _mesh_index():
    me = 4 * lax.axis_index("x") + 2 * lax.axis_index("y") + lax.axis_index("c")
    return me.astype(jnp.int32).reshape(1)


def _fill_slot(name, dst, src, idx, src_idx=None):
    R, Wd = dst.shape[1:]
    scalars = [idx] if src_idx is None else [idx, src_idx]
    if src_idx is None:
        src_spec = pl.BlockSpec((R, Wd), lambda i, *s: (0, 0))
    else:
        src_spec = pl.BlockSpec((None, R, Wd), lambda i, *s: (s[1][0], 0, 0))

    def body(*refs):
        src_ref, out_ref = refs[len(scalars) + 1], refs[len(scalars) + 2]
        out_ref[...] = src_ref[...]

    return pl.pallas_call(
        body, name=name,
        grid_spec=pltpu.PrefetchScalarGridSpec(
            num_scalar_prefetch=len(scalars), grid=(1,),
            in_specs=[pl.BlockSpec(memory_space=pl.ANY), src_spec],
            out_specs=pl.BlockSpec((None, R, Wd), lambda i, *s: (s[0][0], 0, 0))),
        out_shape=jax.ShapeDtypeStruct(dst.shape, dst.dtype),
        input_output_aliases={len(scalars): 0},
        compiler_params=pltpu.CompilerParams(vmem_limit_bytes=VMEM_LIMIT),
    )(*scalars, dst, src)


class TwoLevelGather:
    def __init__(self, bufs, skip_own=()):
        self.bufs, self.nb, self.skip_own = list(bufs), len(bufs), tuple(skip_own)
        self.any_specs = [pl.BlockSpec(memory_space=pl.ANY)] * self.nb
        self.out_shape = [jax.ShapeDtypeStruct((N_DEV,) + b.shape, b.dtype) for b in self.bufs]
        self.sem_shapes = [pltpu.SemaphoreType.DMA((7 * self.nb,)), pltpu.SemaphoreType.DMA((7 * self.nb,)),
                           pltpu.SemaphoreType.DMA((self.nb,))]

    def _copies(self, in_refs, out_refs, sems):
        send_sems, recv_sems, local_sems = sems
        nb = self.nb
        x, y, c = lax.axis_index("x"), lax.axis_index("y"), lax.axis_index("c")
        me, sibling = (x, y, c), (x, y, 1 - c)
        chips = [(1 - x, y), (x, 1 - y), (1 - x, 1 - y)]

        def slot(b, dev):
            return out_refs[b].at[4 * dev[0] + 2 * dev[1] + dev[2]]

        def copy(b, k, block, to, own=False):
            return pltpu.make_async_remote_copy(
                src_ref=in_refs[b] if own else slot(b, block), dst_ref=slot(b, block),
                send_sem=send_sems.at[7 * b + k], recv_sem=recv_sems.at[7 * b + k], device_id=to,
                device_id_type=pl.DeviceIdType.MESH)

        cp = {}
        cp["local"] = [pltpu.make_async_copy(in_refs[b], slot(b, me), local_sems.at[b]) for b in range(nb)
                       if b not in self.skip_own]
        cp["first"] = [copy(b, 0, me, sibling, own=True) for b in range(nb)]
        cp["first"] += [copy(b, 1 + j, me, (*chip, c), own=True) for j, chip in enumerate(chips) for b in range(nb)]
        cp["over_ici"] = [copy(b, 1 + j, (*chip, c), me) for j, chip in enumerate(chips) for b in range(nb)]
        cp["passed"] = [copy(b, 4 + j, (*chip, c), sibling) for j, chip in enumerate(chips) for b in range(nb)]
        cp["from_sibling"] = [copy(b, 0, sibling, me) for b in range(nb)]
        cp["from_sibling"] += [copy(b, 4 + j, (*chip, 1 - c), me) for j, chip in enumerate(chips) for b in range(nb)]
        return cp

    def start(self, in_refs, out_refs, sems):
        cp = self._copies(in_refs, out_refs, sems)
        for c in cp["first"] + cp["local"]:
            c.start()

    def forward(self, in_refs, out_refs, sems):
        cp = self._copies(in_refs, out_refs, sems)
        for arrived, onward in zip(cp["over_ici"], cp["passed"]):
            arrived.wait_recv()
            onward.start()

    def finish(self, in_refs, out_refs, sems):
        cp = self._copies(in_refs, out_refs, sems)
        for c in cp["from_sibling"]:
            c.wait_recv()
        for c in cp["first"] + cp["passed"]:
            c.wait_send()
        for c in cp["local"]:
            c.wait()

    def schedule(self, n_steps):
        return [(0, self.start), (max(n_steps - 3, 0), self.forward), (n_steps - 1, self.finish)]


def _all_gather_two_level(name, bufs, skip_own=()):
    ex = TwoLevelGather(bufs, skip_own)

    def body(*refs):
        args = refs[:ex.nb], refs[ex.nb:2 * ex.nb], refs[2 * ex.nb:]
        ex.start(*args)
        ex.forward(*args)
        ex.finish(*args)

    return pl.pallas_call(body, name=name, in_specs=ex.any_specs, out_specs=ex.any_specs, out_shape=ex.out_shape,
                          scratch_shapes=ex.sem_shapes)(*ex.bufs)


def _cols_from_blocks(blk):
    return jnp.transpose(blk, (1, 0, 2)).reshape(blk.shape[1], -1)


def _cols_to_blocks(g):
    r, c = g.shape
    return jnp.transpose(g.reshape(r, N_DEV, c // N_DEV), (1, 0, 2))


FIRST_WEIGHTS = ["w_in", "shift_b", "w_lora_w", "a_lora_w", "g_lora_w"]
LATE_WEIGHTS = ["w_proj_a", "w_proj_b", "w_out", "w_ffn1", "w_ffn2"]


def _late_weights(shards):
    ex = TwoLevelGather([shards[n].astype(BF16) for n in LATE_WEIGHTS])

    def finish(results):
        got = dict(zip(LATE_WEIGHTS, results))
        W = {n: got[n].reshape(-1, D) for n in ("w_proj_a", "w_proj_b", "w_out", "w_ffn2")}
        W["w_ffn1"] = got["w_ffn1"].reshape(N_DEV, D, -1)
        return W
    return ex, finish


def _gather_weights(shards):
    def payload(n):
        if n == "w_in":
            return jnp.transpose(shards[n][0]).astype(BF16)
        return shards[n] if n == "shift_b" else shards[n].astype(BF16)
    payloads = [payload(n) for n in FIRST_WEIGHTS]
    got = dict(zip(FIRST_WEIGHTS, _all_gather_two_level("weight_all_gather", payloads, skip_own=(0,))))
    got["w_in"] = _fill_slot("w_in_own_slot", got["w_in"], payloads[0], _mesh_index())
    W = {}
    W["w_sgu_t"], W["w_rw_t"], W["w_gate_t"] = _w_in_groups_t(got["w_in"])
    z = lambda r, c, dt: jnp.zeros((r, c), dt)
    W["w_lora"] = jnp.concatenate([_cols_from_blocks(got["w_lora_w"][:, 0]).astype(F32), z(128 - L_W, D, F32)], axis=0)
    W["a_lora"] = jnp.concatenate([_cols_from_blocks(got["a_lora_w"][:, 0]).astype(F32), z(128 - L_A, D, F32)], axis=0)
    W["g_lora"] = jnp.concatenate([_cols_from_blocks(got["g_lora_w"][:, 0]).astype(F32), z(256 - L_G, D, F32)], axis=0)
    sb = _cols_from_blocks(got["shift_b"][:, 0])
    W["sb"] = jnp.concatenate([sb[:, :3 * D], sb[:, 3 * D:3 * D + L_W], z(2, 128 - L_W, F32),
                               sb[:, 3 * D + L_W:3 * D + L_W + L_A], z(2, 128 - L_A, F32),
                               sb[:, 3 * D + L_W + L_A:], z(2, 256 - L_G, F32)], axis=1)
    return W


def _replicated_weights(rep):
    W = {n: rep[n] for n in ("g_mix", "sgu_ln_w", "sgu_ln_b", "w0", "a0", "k_k", "k_a", "r_k", "ln_x_w", "ln_x_b",
                             "g_ffn")}
    W["g_final"] = rep["g_final"].reshape(1, D)
    W["sgu_w"] = rep["sgu_w"][0]
    W["sgu_bt"] = jnp.transpose(rep["sgu_b"][0])
    return W


def _late_grad_blocks(G):
    blocks = {n: G[n].reshape(N_DEV, -1, D) for n in ("w_proj_a", "w_proj_b", "w_out", "w_ffn2")}
    blocks["w_ffn1"] = G["w_ffn1"]
    return Exchange([blocks[n] for n in LATE_WEIGHTS] + [G["sgu_w"].reshape(SGU_G * SGU_C, SGU_C)],
                    [False] * len(LATE_WEIGHTS) + [True])


def _first_grad_blocks(G):
    sbg = G["sb"]
    c = 3 * D
    sb = jnp.concatenate([sbg[:, :c], sbg[:, c:c + L_W], sbg[:, c + 128:c + 128 + L_A],
                          sbg[:, c + 256:c + 256 + L_G]], axis=1)
    return {
        "w_in": _w_in_grad_blocks(G["w_sgu_t"], G["w_rw_t"], G["w_gate_t"]),
        "shift_b": _cols_to_blocks(sb),
        "w_lora_w": _cols_to_blocks(G["w_lora"][:L_W]), "a_lora_w": _cols_to_blocks(G["a_lora"][:L_A]),
        "g_lora_w": _cols_to_blocks(G["g_lora"][:L_G]),
    }


def _replicated_grads(G):
    small = {n: G[n] for n in ("g_mix", "sgu_ln_w", "sgu_ln_b", "w0", "a0", "k_k", "k_a", "r_k", "ln_x_w", "ln_x_b",
                               "g_ffn", "g_final")}
    small["sgu_w"] = G["sgu_w"]
    small["sgu_b"] = jnp.transpose(G["sgu_bt"])
    return small


def kernel(x, g_mix, w_in, sgu_ln_w, sgu_ln_b, sgu_w, sgu_b, w_proj_a, shift_b, w_lora_w, w0, a_lora_w, a0, g_lora_w, k_k, k_a, r_k, ln_x_w, ln_x_b, w_proj_b, w_out, g_ffn, w_ffn1, w_ffn2, g_final, loss_target, m_g_mix, m_w_in, m_sgu_ln_w, m_sgu_ln_b, m_sgu_w, m_sgu_b, m_w_proj_a, m_shift_b, m_w_lora_w, m_w0, m_a_lora_w, m_a0, m_g_lora_w, m_k_k, m_k_a, m_r_k, m_ln_x_w, m_ln_x_b, m_w_proj_b, m_w_out, m_g_ffn, m_w_ffn1, m_w_ffn2, m_g_final, v_g_mix, v_w_in, v_sgu_ln_w, v_sgu_ln_b, v_sgu_w, v_sgu_b, v_w_proj_a, v_shift_b, v_w_lora_w, v_w0, v_a_lora_w, v_a0, v_g_lora_w, v_k_k, v_k_a, v_r_k, v_ln_x_w, v_ln_x_b, v_w_proj_b, v_w_out, v_g_ffn, v_w_ffn1, v_w_ffn2, v_g_final):
    env = dict(locals())
    weights = {n: env[n] for n in WEIGHT_ORDER}
    moms = {n: env["m_" + n] for n in WEIGHT_ORDER}
    vars_ = {n: env["v_" + n] for n in WEIGHT_ORDER}

    shards = {n: weights[n] for n, _, _ in SHARDED}
    W = _gather_weights(shards)
    W.update(_replicated_weights({n: weights[n] for n, _ in REPLICATED}))
    loss_part, dx, G, late_slots = _local_step(x[0], loss_target[0], W, late_weights=_late_weights(shards),
                                               early_grads=_late_grad_blocks)

    slots = dict(zip(LATE_WEIGHTS, late_slots))
    blocks = _first_grad_blocks(G)
    small = _replicated_grads(G)
    small_parts = [small[n] for n, _ in PACKED] + [jnp.full((PACK_W,), loss_part, F32)]
    got = _pair_exchange("grad_pair_exchange", blocks["w_in"])
    core = lax.axis_index("c").astype(jnp.int32).reshape(1)
    rest = [n for n in FIRST_WEIGHTS if n != "w_in"]
    ex = Exchange([blocks[n] for n in rest] + [_pack_rows(small_parts, _SMALL_ROWS, F32)], [False] * len(rest) + [True])
    sums = _pair_sum("grad_pair_sum", blocks["w_in"], got, core)
    chip_slots, res = _chip_exchange("grad_exchange", sums, ex)
    my_chip = (2 * lax.axis_index("x") + lax.axis_index("y")).astype(jnp.int32).reshape(1)
    slots["w_in"] = _fill_slot("grad_own_slot", chip_slots, sums, my_chip, src_idx=my_chip)
    slots.update(zip(rest, res[:-1]))
    small_slots = res[-1]

    outs = [dict(), dict(), dict(), dict()]
    for n, _, _ in SHARDED:
        if n == "w_in":
            res = _adamw("adamw_" + n, slots[n], *[jnp.transpose(t[0]) for t in (weights[n], moms[n], vars_[n])])
            res = [jnp.transpose(t)[None] for t in res]
        else:
            res = _adamw("adamw_" + n, slots[n], weights[n], moms[n], vars_[n])
        for k in range(4):
            outs[k][n] = res[k]

    sgu_shape = (SGU_G * SGU_C, SGU_C)
    res = _adamw("adamw_sgu_w", late_slots[len(LATE_WEIGHTS)], *[t.reshape(sgu_shape) for t in
                                                                  (weights["sgu_w"], moms["sgu_w"], vars_["sgu_w"])])
    for k in range(4):
        outs[k]["sgu_w"] = res[k].reshape(weights["sgu_w"].shape)

    def packed(d):
        return _pack_rows([d[n] for n, _ in PACKED], _SMALL_ROWS, F32)
    small_out = _adamw("adamw_replicated", small_slots, packed(weights), packed(moms), packed(vars_))
    for k in range(4):
        flat = small_out[k].reshape(-1)
        off = 0
        for (n, s), size in zip(PACKED, _SMALL_SIZES):
            outs[k][n] = flat[off:off + size].reshape(s)
            off += size
    loss = small_out[0].reshape(-1)[_LOSS_AT]
    return (loss, dx[None], *[outs[0][n] for n in WEIGHT_ORDER], *[outs[1][n] for n in WEIGHT_ORDER],
            *[outs[2][n] for n in WEIGHT_ORDER], *[outs[3][n] for n in WEIGHT_ORDER])
```

```python
import functools
import numpy as np
import jax
import jax.numpy as jnp
from jax import lax
from jax.experimental import pallas as pl
from jax.experimental.pallas import tpu as pltpu

F32 = jnp.float32
BF16 = jnp.bfloat16

D = 1024
NH, HN = 16, 64
SGU_G, SGU_C = 8, 128
L_W, L_A, L_G = 64, 64, 160
C_B = 3 * D + L_W + L_A + L_G
P_TOTAL = 2 * D + C_B + 2 * D
D_FF = 4 * D
RW_INT = 3 * D + 128 + 128 + 256
NORM_EPS, LN_EPS, GN_EPS = 1e-6, 1e-5, 64e-5
N_DEV = 8
LANES = 128
SCAN_C = 64
SOLVE_B = 16
SCAN_PRECISION = lax.Precision.HIGH
SCAN_OUT_PRECISION = lax.Precision.DEFAULT
GRAD_PAYLOAD = BF16
VMEM_LIMIT = 56 * 1024 * 1024
MATMUL_VMEM_BUDGET = 40 * 1024 * 1024
STEP_COST_BYTES = 512 * 1024
HBM_COST_RATIO = 3

ADAM_LR, ADAM_B1, ADAM_B2, ADAM_EPS, ADAM_WD, ADAM_STEP = 0.001, 0.9, 0.999, 1e-08, 0.01, 10

SHARDED = [
    ("w_in", (D, P_TOTAL), 1), ("w_proj_a", (D, D), 0), ("shift_b", (2, C_B), 1), ("w_lora_w", (L_W, D), 1),
    ("a_lora_w", (L_A, D), 1), ("g_lora_w", (L_G, D), 1), ("w_proj_b", (D, D), 0), ("w_out", (D, D), 0),
    ("w_ffn1", (D, D_FF), 1), ("w_ffn2", (D_FF, D), 0),
]
REPLICATED = [
    ("g_mix", (1, D)), ("sgu_ln_w", (1, D)), ("sgu_ln_b", (1, D)), ("sgu_w", (1, SGU_G, SGU_C, SGU_C)),
    ("sgu_b", (1, SGU_G, SGU_C)), ("w0", (1, D)), ("a0", (1, D)), ("k_k", (1, D)), ("k_a", (1, D)), ("r_k", (1, D)),
    ("ln_x_w", (1, D)), ("ln_x_b", (1, D)), ("g_ffn", (1, D)), ("g_final", (D,)),
]
WEIGHT_ORDER = ["g_mix", "w_in", "sgu_ln_w", "sgu_ln_b", "sgu_w", "sgu_b", "w_proj_a", "shift_b", "w_lora_w", "w0",
                "a_lora_w", "a0", "g_lora_w", "k_k", "k_a", "r_k", "ln_x_w", "ln_x_b", "w_proj_b", "w_out", "g_ffn",
                "w_ffn1", "w_ffn2", "g_final"]


def _shard_shape(shape, axis):
    s = list(shape)
    s[axis] //= N_DEV
    return tuple(s)


def _round_up(n, m):
    return (n + m - 1) // m * m


def _pick(n, target):
    if n <= target:
        return n
    best = None
    for t in range(LANES, target + 1, LANES):
        if n % t == 0:
            best = t
    assert best is not None, (n, target)
    return best


def _matmul(name, a, b, mode, out_dtype=F32, tm=2048, tn=1024, tk=2048, out_blocks=None, epilogue=None, extras=(),
            out_dtypes=(), after=None):
    b_blocks = b.shape[0] if b.ndim == 3 else None
    bshape = b.shape if b.ndim == 2 else (b.shape[1], b.shape[0] * b.shape[2])
    if mode == "nn":
        (M, K), (K2, N) = a.shape, bshape
    elif mode == "nt":
        (M, K), (N, K2) = a.shape, bshape
    else:
        (K, M), (K2, N) = a.shape, bshape
    assert K == K2, (name, a.shape, b.shape)
    assert b_blocks is None or mode != "tn"
    assert out_blocks is None or mode == "tn"
    tn = min(tn, N // (out_blocks or 1), bshape[1] // b_blocks if (b_blocks and mode == "nn") else tn)
    tk = min(tk, bshape[1] // b_blocks if (b_blocks and mode == "nt") else tk)
    tm, tn, tk = _pick(M, tm), _pick(N, tn), _pick(K, tk)

    def vmem_bytes(tm, tk):
        tiles = tm * tk * a.dtype.itemsize + tk * tn * b.dtype.itemsize
        for dt in (out_dtypes if epilogue else (out_dtype,)):
            tiles += tm * tn * jnp.dtype(dt).itemsize
        for x in extras:
            arr = x[0] if isinstance(x, tuple) else x
            tiles += (tm if arr.shape[0] > 1 else 1) * tn * arr.dtype.itemsize
        return 2 * tiles + (tm * tn * 4 if K // tk > 1 else 0)

    def cost(tm, tk):
        ni, nj, nk = M // tm, N // tn, K // tk
        steps = ni * nj * nk
        acc_passes = steps * tm * tn * 8 if nk > 1 else 0
        a_reads = M * K * a.dtype.itemsize * (nj if nk > 1 else 1)
        b_reads = K * N * b.dtype.itemsize * (ni if (nj > 1 or nk > 1) else 1)
        return (steps * STEP_COST_BYTES + acc_passes + vmem_bytes(tm, tk) // 2
                + HBM_COST_RATIO * (a_reads + b_reads))

    options = [(m, k) for m in {_pick(M, max(t, LANES)) for t in (tm, tm // 2, tm // 4)}
               for k in {_pick(K, max(t, LANES)) for t in (tk, tk // 2, tk // 4)}
               if vmem_bytes(m, k) <= MATMUL_VMEM_BUDGET]
    tm, tk = min(options, key=lambda o: cost(*o))
    nk = K // tk
    dims = {"nn": (((1,), (0,)), ((), ())), "nt": (((1,), (1,)), ((), ())), "tn": (((0,), (0,)), ((), ()))}[mode]

    n_x, n_o = len(extras), len(out_dtypes) if epilogue else 1
    n_after = 0 if after is None else 1

    def body(a_ref, b_ref, *rest):
        x_refs, o_refs, acc = rest[:n_x], rest[n_x + n_after:n_x + n_after + n_o], rest[n_x + n_after + n_o:]
        part = lax.dot_general(a_ref[...].astype(BF16), b_ref[...].astype(BF16), dims, preferred_element_type=F32)

        def finish(res):
            outs = epilogue(res, *[r[...] for r in x_refs]) if epilogue else (res,)
            for r, v in zip(o_refs, outs):
                r[...] = v.astype(r.dtype)

        if nk == 1:
            finish(part)
            return
        acc_ref, k = acc[0], pl.program_id(2)

        @pl.when(k == 0)
        def _():
            acc_ref[...] = part

        @pl.when(k > 0)
        def _():
            acc_ref[...] += part

        @pl.when(k == nk - 1)
        def _():
            finish(acc_ref[...])

    a_spec = {"nn": pl.BlockSpec((tm, tk), lambda i, j, k: (i, k)), "nt": pl.BlockSpec((tm, tk), lambda i, j, k: (i, k)),
              "tn": pl.BlockSpec((tk, tm), lambda i, j, k: (k, i))}[mode]
    b_spec = {"nn": pl.BlockSpec((tk, tn), lambda i, j, k: (k, j)), "nt": pl.BlockSpec((tn, tk), lambda i, j, k: (j, k)),
              "tn": pl.BlockSpec((tk, tn), lambda i, j, k: (k, j))}[mode]
    if b_blocks and mode == "nn":
        per = b.shape[2] // tn
        b_spec = pl.BlockSpec((None, tk, tn), lambda i, j, k: (j // per, k, j % per))
    elif b_blocks:
        per = b.shape[2] // tk
        b_spec = pl.BlockSpec((None, tn, tk), lambda i, j, k: (k // per, j, k % per))
    out_spec = pl.BlockSpec((tm, tn), lambda i, j, k: (i, j))
    out_shape = jax.ShapeDtypeStruct((M, N), out_dtype)
    if out_blocks:
        per_o = N // out_blocks // tn
        out_spec = pl.BlockSpec((None, tm, tn), lambda i, j, k: (j // per_o, i, j % per_o))
        out_shape = jax.ShapeDtypeStruct((out_blocks, M, N // out_blocks), out_dtype)
    x_specs, x_args = [], []
    for x in extras:
        arr, off = x if isinstance(x, tuple) else (x, 0)
        if arr.shape[0] == 1:
            x_specs.append(pl.BlockSpec((1, tn), lambda i, j, k: (0, j)))
        else:
            x_specs.append(pl.BlockSpec((tm, tn), lambda i, j, k, off=off: (i, j + off)))
        x_args.append(arr)
    res = pl.pallas_call(
        body, name=name, grid=(M // tm, N // tn, nk),
        in_specs=[a_spec, b_spec] + x_specs + [pl.BlockSpec(memory_space=pl.ANY)] * n_after,
        out_specs=[out_spec] * n_o if epilogue else out_spec,
        out_shape=[jax.ShapeDtypeStruct((M, N), dt) for dt in out_dtypes] if epilogue else out_shape,
        scratch_shapes=[pltpu.VMEM((tm, tn), F32)] if nk > 1 else [],
        compiler_params=pltpu.CompilerParams(dimension_semantics=("parallel", "parallel", "arbitrary"),
                                             vmem_limit_bytes=VMEM_LIMIT),
    )(a, b, *x_args, *([after] if n_after else []))
    return res


class Rows:
    def __init__(self, arr, width=None, cb=0):
        self.arr, self.width, self.cb = arr, (arr.shape[1] if width is None else width), cb


class Heads:
    def __init__(self, arr):
        self.arr = arr


class Halo:
    def __init__(self, arr, side):
        self.arr, self.side = arr, side


def _rows_call(name, fn, ins, consts, outs, accs=(), tm=256, with_pid=False):
    T = next(o.arr.shape[1] if isinstance(o, Heads) else o.arr.shape[0] for o in ins if not isinstance(o, Halo))
    tm = min(tm, T)
    n_tiles = T // tm
    n_in, n_c, n_out = len(ins), len(consts), len(outs)
    in_specs, args = [], []
    for o in ins:
        if isinstance(o, Rows):
            in_specs.append(pl.BlockSpec((tm, o.width), lambda i, cb=o.cb: (i, cb)))
        elif isinstance(o, Heads):
            in_specs.append(pl.BlockSpec((NH, tm, HN), lambda i: (0, i, 0)))
        else:
            w = o.arr.shape[1]
            if o.side < 0:
                in_specs.append(pl.BlockSpec((8, w), lambda i: (jnp.maximum(i * (tm // 8) - 1, 0), 0)))
            else:
                in_specs.append(pl.BlockSpec((8, w), lambda i: (jnp.minimum((i + 1) * (tm // 8), T // 8 - 1), 0)))
        args.append(o.arr)
    for c in consts:
        in_specs.append(pl.BlockSpec(c.shape, lambda i, nd=c.ndim: (0,) * nd))
        args.append(c)
    out_specs, out_shape = [], []
    for o in outs:
        if o[0] == "rows":
            out_specs.append(pl.BlockSpec((tm, o[1]), lambda i: (i, 0)))
            out_shape.append(jax.ShapeDtypeStruct((T, o[1]), o[2]))
        else:
            out_specs.append(pl.BlockSpec((NH, tm, HN), lambda i: (0, i, 0)))
            out_shape.append(jax.ShapeDtypeStruct((NH, T, HN), o[1]))
    for shape, dt in accs:
        out_specs.append(pl.BlockSpec(shape, lambda i, nd=len(shape): (0,) * nd))
        out_shape.append(jax.ShapeDtypeStruct(shape, dt))

    def body(*refs):
        i = pl.program_id(0)
        vals = []
        vals = [r[...] for r in refs[:n_in + n_c]]
        res = fn(i, n_tiles, *vals) if with_pid else fn(*vals)
        out_refs = refs[n_in + n_c:]
        for r, v in zip(out_refs[:n_out], res[:n_out]):
            r[...] = v.astype(r.dtype)
        if accs:
            @pl.when(i == 0)
            def _():
                for r in out_refs[n_out:]:
                    r[...] = jnp.zeros_like(r)

            for r, v in zip(out_refs[n_out:], res[n_out:]):
                r[...] += v.astype(r.dtype)

    res = pl.pallas_call(
        body, name=name, grid=(n_tiles,), in_specs=in_specs, out_specs=out_specs, out_shape=out_shape,
        compiler_params=pltpu.CompilerParams(dimension_semantics=("arbitrary",), vmem_limit_bytes=VMEM_LIMIT),
    )(*args)
    return res


def _rms(x, g):
    return x * lax.rsqrt(jnp.mean(x * x, axis=-1, keepdims=True) + NORM_EPS) * g


def _gelu(x):
    return 0.5 * x * (1.0 + lax.erf(x * 0.7071067811865476))


def _sigmoid(x):
    return 1.0 / (1.0 + jnp.exp(-x))


def _bdot(a, b):
    return jnp.dot(a.astype(BF16), b.astype(BF16), preferred_element_type=F32)


def _to_heads(x):
    return jnp.concatenate([x[:, h * HN:(h + 1) * HN][None] for h in range(NH)], axis=0)


def _from_heads(xh):
    return jnp.concatenate([xh[h] for h in range(NH)], axis=-1)


def _sgu_fn(p, ln_w, ln_b, sw, sbt):
    z = _gelu(p)
    u, v = z[:, :D], z[:, D:]
    mu = jnp.mean(v, axis=-1, keepdims=True)
    var = jnp.mean(jnp.square(v - mu), axis=-1, keepdims=True)
    vn = (v - mu) * lax.rsqrt(var + LN_EPS) * ln_w + ln_b
    ri = lax.broadcasted_iota(jnp.int32, (SGU_C, SGU_C), 0)
    ci = lax.broadcasted_iota(jnp.int32, (SGU_C, SGU_C), 1)
    mask = (ci <= ri).astype(F32)
    dg = D // SGU_G
    parts = []
    for g in range(SGU_G):
        parts.append(_bdot(sw[g] * mask, vn[:, g * dg:(g + 1) * dg]) + sbt[:, g:g + 1])
    return u * jnp.concatenate(parts, axis=-1)


def _pre_fn(qr, qk, qv, qxw, qxa, qxg, wl, w0, al, a0, gl, k_k, k_a):
    w = -jax.nn.softplus(-(w0 + _bdot(jnp.tanh(qxw), wl))) - 0.5
    lw = -jnp.exp(w)
    aa = _sigmoid(a0 + _bdot(qxa, al))
    g = _bdot(_sigmoid(qxg), gl)
    kk = _to_heads(qk * k_k)
    kk = kk / jnp.maximum(jnp.sqrt(jnp.sum(kk * kk, axis=-1, keepdims=True)), 1e-12)
    k2 = qk * (1.0 + (aa - 1.0) * k_a)
    return _to_heads(qr), _to_heads(lw), _to_heads(k2), _to_heads(qv), kk, _to_heads(aa), g


def _post_fn(o, r, k2, v, g, ln_w, ln_b, r_k):
    mu = jnp.mean(o, axis=-1, keepdims=True)
    d = o - mu
    var = jnp.mean(d * d, axis=-1, keepdims=True)
    on = d * lax.rsqrt(var + GN_EPS) * ln_w + ln_b
    bonus = jnp.sum(r * k2 * r_k, axis=-1, keepdims=True) * v
    return _from_heads(on + bonus) * g


def _gate_fn(pg, ya, yb):
    return _sigmoid(pg[:, :D]) * ya + _sigmoid(pg[:, D:]) * yb


def _bmm(x, y, cx, cy, out_path=False):
    return lax.dot_general(x, y, (((cx,), (cy,)), ((0,), (0,))),
                           precision=SCAN_OUT_PRECISION if out_path else SCAN_PRECISION, preferred_element_type=F32)


def _unit_lower_inverse(M):
    C = M.shape[1]
    ti = lax.broadcasted_iota(jnp.int32, (C, C), 0)
    tj = lax.broadcasted_iota(jnp.int32, (C, C), 1)
    eye = (ti == tj).astype(F32)
    same = lambda b: (ti // b == tj // b).astype(F32)
    X = -(M * same(SOLVE_B))
    inv = eye + X
    span = 1
    while 2 * span < SOLVE_B:
        X = _bmm(X, X, 2, 1)
        inv = inv + _bmm(inv, X, 2, 1)
        span *= 2
    b = SOLVE_B
    while b < C:
        low = M * (same(2 * b) - same(b))
        inv = inv - _bmm(_bmm(inv, low, 2, 1), inv, 2, 1)
        b *= 2
    return inv


@jax.custom_vjp
def _unit_lower_solve(inv, M, y):
    return _bmm(inv, y, 2, 1)


def _unit_lower_solve_fwd(inv, M, y):
    u = _bmm(inv, y, 2, 1)
    return u, (inv, u)


def _unit_lower_solve_bwd(res, du):
    inv, u = res
    dy = _bmm(inv, du, 1, 1)
    return jnp.zeros_like(inv), -_bmm(dy, u, 2, 2), dy


_unit_lower_solve.defvjp(_unit_lower_solve_fwd, _unit_lower_solve_bwd)


def _sum_over_time(x, reverse):
    C = x.shape[1]
    ti = lax.broadcasted_iota(jnp.int32, (C, C), 0)
    tj = lax.broadcasted_iota(jnp.int32, (C, C), 1)
    ones = jnp.broadcast_to(((tj >= ti) if reverse else (tj <= ti)).astype(BF16), (x.shape[0], C, C))
    hi = x.astype(BF16)
    r1 = x - hi.astype(F32)
    mid = r1.astype(BF16)
    lo = (r1 - mid.astype(F32)).astype(BF16)
    dn = (((2,), (1,)), ((0,), (0,)))
    return sum(lax.dot_general(ones, p, dn, preferred_element_type=F32) for p in (lo, mid, hi))


@jax.custom_vjp
def _time_cumsum(lw):
    return _sum_over_time(lw, reverse=False)


_time_cumsum.defvjp(lambda lw: (_sum_over_time(lw, reverse=False), None),
                    lambda _, d: (_sum_over_time(d, reverse=True),))


def _chunk_fn(S0, r, lw, k, v, kk, a, inv=None):
    C = SCAN_C
    bmm = _bmm
    ti = lax.broadcasted_iota(jnp.int32, (C, C), 0)
    tj = lax.broadcasted_iota(jnp.int32, (C, C), 1)
    incl = (tj <= ti).astype(F32)
    strict = (tj < ti).astype(F32)
    cum = _time_cumsum(lw)
    g_in, g_ex, g_inv = jnp.exp(cum), jnp.exp(cum - lw), jnp.exp(-cum)
    kkt, rt = kk * g_ex, r * g_in
    bk = jnp.concatenate([kk * a * g_inv, k * g_inv], axis=1)
    A = bmm(kkt, bk, 2, 2)
    M = A[:, :, :C] * strict
    n_mask = jnp.concatenate([jnp.zeros((C, C), F32), strict], axis=1)
    zv = jnp.concatenate([jnp.zeros_like(v), v], axis=1)
    s0_side = bmm(jnp.concatenate([kkt, rt], axis=1), S0, 2, 2, out_path=True)
    if inv is None:
        inv = lax.stop_gradient(_unit_lower_inverse(M))
    y = _unit_lower_solve(inv, M, s0_side[:, :C] + bmm(A * n_mask, zv, 2, 1, out_path=True))
    z = jnp.concatenate([-y, v], axis=1)
    attn = bmm(rt, bk, 2, 2) * jnp.concatenate([incl, incl], axis=1)
    O = s0_side[:, C:] + bmm(attn, z, 2, 1, out_path=True)
    S1 = (S0 + bmm(z, bk, 1, 1, out_path=True)) * g_in[:, C - 1:C, :]
    return O, S1, inv


def _scan_fwd(r, lw, k, v, kk, a, ex=None, tb=256):
    assert SCAN_C == HN
    T = r.shape[1]
    tb = min(tb, T)
    n_chunks = tb // SCAN_C
    nb = T // tb
    nx = ex.nb if ex else 0

    def body(*refs):
        r_ref, lw_ref, k_ref, v_ref, kk_ref, a_ref = refs[:6]
        x_in, (o_ref, s0_ref), x_out = refs[6:6 + nx], refs[6 + nx:8 + nx], refs[8 + nx:8 + 2 * nx]
        s_ref, sems = refs[8 + 2 * nx], refs[9 + 2 * nx:]

        plan = ex.schedule(nb) if ex else []

        @pl.when(pl.program_id(0) == 0)
        def _():
            s_ref[...] = jnp.zeros_like(s_ref)
            for at, action in plan[:1]:
                action(x_in, x_out, sems)

        def step(c, carry):
            sl = pl.ds(pl.multiple_of(c * SCAN_C, SCAN_C), SCAN_C)
            S0 = s_ref[...]
            O, S1, inv = _chunk_fn(S0, r_ref[:, sl, :], lw_ref[:, sl, :], k_ref[:, sl, :], v_ref[:, sl, :],
                                   kk_ref[:, sl, :], a_ref[:, sl, :])
            o_ref[:, sl, :] = O
            s0_ref[c, 0] = S0
            s0_ref[c, 1] = inv
            s_ref[...] = S1
            return carry

        lax.fori_loop(0, n_chunks, step, 0)

        for at, action in plan[1:]:
            pl.when(pl.program_id(0) == at)(functools.partial(action, x_in, x_out, sems))

    hm = pl.BlockSpec((NH, tb, HN), lambda i: (0, i, 0))
    res = pl.pallas_call(
        body, name="rwkv_scan_fwd", grid=(nb,), in_specs=[hm] * 6 + (ex.any_specs if ex else []),
        out_specs=[hm, pl.BlockSpec((n_chunks, 2, NH, HN, HN), lambda i: (i, 0, 0, 0, 0))]
        + (ex.any_specs if ex else []),
        out_shape=[jax.ShapeDtypeStruct((NH, T, HN), F32), jax.ShapeDtypeStruct((T // SCAN_C, 2, NH, HN, HN), F32)]
        + (ex.out_shape if ex else []),
        scratch_shapes=[pltpu.VMEM((NH, HN, HN), F32)] + (ex.sem_shapes if ex else []),
        compiler_params=pltpu.CompilerParams(dimension_semantics=("arbitrary",), vmem_limit_bytes=VMEM_LIMIT),
    )(r, lw, k, v, kk, a, *(ex.bufs if ex else []))
    return res[0], res[1], list(res[2:])


def _scan_bwd(r, lw, k, v, kk, a, s0s, do, ex=None, tb=128):
    T = r.shape[1]
    tb = min(tb, T)
    n_chunks = tb // SCAN_C
    nb = T // tb
    nx = ex.nb if ex else 0

    def body(*refs):
        r_ref, lw_ref, k_ref, v_ref, kk_ref, a_ref, s0_ref, do_ref = refs[:8]
        x_in, (dr, dlw, dk, dv, dkk, da), x_out = refs[8:8 + nx], refs[8 + nx:14 + nx], refs[14 + nx:14 + 2 * nx]
        ds_ref, sems = refs[14 + 2 * nx], refs[15 + 2 * nx:]

        plan = ex.schedule(nb) if ex else []

        @pl.when(pl.program_id(0) == 0)
        def _():
            ds_ref[...] = jnp.zeros_like(ds_ref)
            for at, action in plan[:1]:
                action(x_in, x_out, sems)

        def step(j, carry):
            c = n_chunks - 1 - j
            sl = pl.ds(pl.multiple_of(c * SCAN_C, SCAN_C), SCAN_C)
            inv = s0_ref[c, 1]
            _, vjp = jax.vjp(lambda *t: _chunk_fn(*t, inv=inv)[:2], s0_ref[c, 0], r_ref[:, sl, :], lw_ref[:, sl, :],
                             k_ref[:, sl, :], v_ref[:, sl, :], kk_ref[:, sl, :], a_ref[:, sl, :])
            g = vjp((do_ref[:, sl, :], ds_ref[...]))
            ds_ref[...] = g[0]
            for ref, val in zip((dr, dlw, dk, dv, dkk, da), g[1:]):
                ref[:, sl, :] = val
            return carry

        lax.fori_loop(0, n_chunks, step, 0)

        for at, action in plan[1:]:
            pl.when(pl.program_id(0) == at)(functools.partial(action, x_in, x_out, sems))

    hm = pl.BlockSpec((NH, tb, HN), lambda i: (0, nb - 1 - i, 0))
    res = pl.pallas_call(
        body, name="rwkv_scan_bwd", grid=(nb,),
        in_specs=[hm] * 6 + [pl.BlockSpec((n_chunks, 2, NH, HN, HN), lambda i: (nb - 1 - i, 0, 0, 0, 0)), hm]
        + (ex.any_specs if ex else []),
        out_specs=[hm] * 6 + (ex.any_specs if ex else []),
        out_shape=[jax.ShapeDtypeStruct((NH, T, HN), F32)] * 6 + (ex.out_shape if ex else []),
        scratch_shapes=[pltpu.VMEM((NH, HN, HN), F32)] + (ex.sem_shapes if ex else []),
        compiler_params=pltpu.CompilerParams(dimension_semantics=("arbitrary",), vmem_limit_bytes=VMEM_LIMIT),
    )(r, lw, k, v, kk, a, s0s, do, *(ex.bufs if ex else []))
    return list(res[:6]), list(res[6:])


def _shift_down(i, p, prev8):
    first = jnp.where(i > 0, prev8[7:8, :], 0.0)
    row = lax.broadcasted_iota(jnp.int32, p.shape, 0)
    return jnp.where(row == 0, first, pltpu.roll(p, 1, axis=0))


def _mix_fwd(p, sb, tm=256):
    def fn(i, n, p, prev8, sb):
        return (p * sb[0:1] + _shift_down(i, p, prev8) * sb[1:2],)
    return _rows_call("shift_mix_fwd", fn, [Rows(p), Halo(p, -1)], [sb], [("rows", p.shape[1], F32)], tm=tm,
                      with_pid=True)[0]


def _mix_bwd(dq, p, sb, tm=256):
    def fn(i, n, dq, next8, p, prev8, sb):
        ps = _shift_down(i, p, prev8)
        d1 = dq * sb[1:2]
        last = jnp.where(i < n - 1, next8[0:1, :] * sb[1:2], 0.0)
        row = lax.broadcasted_iota(jnp.int32, dq.shape, 0)
        up = jnp.where(row == dq.shape[0] - 1, last, pltpu.roll(d1, dq.shape[0] - 1, axis=0))
        return (dq * sb[0:1] + up, jnp.sum(dq * p, axis=0, keepdims=True), jnp.sum(dq * ps, axis=0, keepdims=True))
    w = p.shape[1]
    return _rows_call("shift_mix_bwd", fn, [Rows(dq), Halo(dq, +1), Rows(p), Halo(p, -1)], [sb], [("rows", w, F32)],
                      accs=[((1, w), F32), ((1, w), F32)], tm=tm, with_pid=True)


def _local_step(x, target, W, late_weights=None, early_grads=None, w_in_grads_ready=None):
    G = {}
    a = _rows_call("norm_mix_fwd", lambda x, g: (_rms(x, g),), [Rows(x)], [W["g_mix"]], [("rows", D, BF16)])[0]
    p_sgu = _matmul("proj_sgu", a, W["w_sgu_t"], "nt")
    p_rw = _matmul("proj_rwkv", a, W["w_rw_t"], "nt")
    p_gate = _matmul("proj_gate", a, W["w_gate_t"], "nt")

    sgu_consts = [W["sgu_ln_w"], W["sgu_ln_b"], W["sgu_w"], W["sgu_bt"]]
    s = _rows_call("sgu_fwd", lambda *t: (_sgu_fn(*t),), [Rows(p_sgu)], sgu_consts, [("rows", D, BF16)], tm=SGU_C)[0]

    q = _mix_fwd(p_rw, W["sb"])
    q_ins = [Rows(q, D, 0), Rows(q, D, 1), Rows(q, D, 2), Rows(q, 128, 24), Rows(q, 128, 25), Rows(q, 256, 13)]
    pre_consts = [W["w_lora"], W["w0"], W["a_lora"], W["a0"], W["g_lora"], W["k_k"], W["k_a"]]
    r_h, lw_h, k_h, v_h, kk_h, a_h, g_gate = _rows_call(
        "rwkv_pre_fwd", _pre_fn, q_ins, pre_consts, [("heads", F32)] * 6 + [("rows", D, F32)], tm=128)
    o_h, s0s, got = _scan_fwd(r_h, lw_h, k_h, v_h, kk_h, a_h, ex=late_weights[0] if late_weights else None)
    if late_weights:
        W = {**W, **late_weights[1](got)}
    y_a = _matmul("proj_a", s, W["w_proj_a"], "nn")
    post_ins = [Heads(o_h), Heads(r_h), Heads(k_h), Heads(v_h), Rows(g_gate)]
    post_consts = [W[n].reshape(NH, 1, HN) for n in ("ln_x_w", "ln_x_b", "r_k")]
    z_b = _rows_call("rwkv_post_fwd", lambda *t: (_post_fn(*t),), post_ins, post_consts, [("rows", D, BF16)], tm=128)[0]
    y_b, mixed = _matmul("proj_b", z_b, W["w_proj_b"], "nn", extras=[(p_gate, 0), (p_gate, 1), y_a],
                         epilogue=lambda yb, ga, gb, ya: (yb, _sigmoid(ga) * ya + _sigmoid(gb) * yb),
                         out_dtypes=(F32, BF16))
    gate_ins = [Rows(p_gate), Rows(y_a), Rows(y_b)]

    def res1(mo, x, g):
        h1 = x + mo
        return h1, _rms(h1, g)
    h1, f = _matmul("proj_out", mixed, W["w_out"], "nn", extras=[x, W["g_ffn"]], epilogue=res1,
                    out_dtypes=(F32, BF16))

    def relu_sq(u):
        r = jnp.maximum(u, 0.0)
        return r, r * r
    r1, act = _matmul("ffn_up", f, W["w_ffn1"], "nn", epilogue=relu_sq, out_dtypes=(BF16, BF16))
    ff = _matmul("ffn_down", act, W["w_ffn2"], "nn")

    def head(h1, ff, tgt, g):
        def f_(h1, ff, g):
            y = _rms(h1 + ff, g)
            return 0.5 * jnp.sum(jnp.mean(jnp.square(y - tgt), axis=-1))
        loss, (dh2, _, dg) = jax.value_and_grad(f_, argnums=(0, 1, 2))(h1, ff, g)
        return dh2, jnp.full((8, LANES), loss, F32), dg
    dh2, loss_acc, G["g_final"] = _rows_call("loss_head", head, [Rows(h1), Rows(ff), Rows(target)], [W["g_final"]],
                                             [("rows", D, F32)], accs=[((8, LANES), F32), ((1, D), F32)])

    d_u1 = _matmul("ffn_down_dx", dh2, W["w_ffn2"], "nt", extras=[r1], out_dtypes=(BF16,),
                   epilogue=lambda d_act, r: (d_act * 2.0 * r.astype(F32),))[0]
    G["w_ffn2"] = _matmul("ffn_down_dw", act, dh2, "tn", out_dtype=GRAD_PAYLOAD)
    d_f = _matmul("ffn_up_dx", d_u1, W["w_ffn1"], "nt")
    G["w_ffn1"] = _matmul("ffn_up_dw", f, d_u1, "tn", out_blocks=N_DEV, out_dtype=GRAD_PAYLOAD)

    def res1_bwd(h1, d_f, dh2, g):
        _, vjp = jax.vjp(_rms, h1, g)
        dh, dg = vjp(d_f)
        return dh2 + dh, dg
    dh1, G["g_ffn"] = _rows_call("residual_norm_bwd", res1_bwd, [Rows(h1), Rows(d_f), Rows(dh2)],
                                 [W["g_ffn"]], [("rows", D, F32)], accs=[((1, D), F32)])
    d_mixed = _matmul("proj_out_dx", dh1, W["w_out"], "nt")
    G["w_out"] = _matmul("proj_out_dw", mixed, dh1, "tn", out_dtype=GRAD_PAYLOAD)

    def gate_bwd(pg, ya, yb, dm):
        _, vjp = jax.vjp(_gate_fn, pg, ya, yb)
        return vjp(dm)
    d_gate, d_ya, d_yb = _rows_call("gate_bwd", gate_bwd, gate_ins + [Rows(d_mixed)], [],
                                    [("rows", 2 * D, F32), ("rows", D, BF16), ("rows", D, BF16)])

    d_s = _matmul("proj_a_dx", d_ya, W["w_proj_a"], "nt")
    G["w_proj_a"] = _matmul("proj_a_dw", s, d_ya, "tn", out_dtype=GRAD_PAYLOAD)

    def sgu_bwd(p, ds, *c):
        _, vjp = jax.vjp(_sgu_fn, p, *c)
        return vjp(ds)
    d_p_sgu, G["sgu_ln_w"], G["sgu_ln_b"], G["sgu_w"], G["sgu_bt"] = _rows_call(
        "sgu_bwd", sgu_bwd, [Rows(p_sgu), Rows(d_s)], sgu_consts, [("rows", 2 * D, F32)],
        accs=[((1, D), F32), ((1, D), F32), ((SGU_G, SGU_C, SGU_C), F32), ((SGU_C, SGU_G), F32)], tm=SGU_C)

    d_zb = _matmul("proj_b_dx", d_yb, W["w_proj_b"], "nt")
    G["w_proj_b"] = _matmul("proj_b_dw", z_b, d_yb, "tn", out_dtype=GRAD_PAYLOAD)

    def post_bwd(o, r, k2, v, g, dz, *c):
        _, vjp = jax.vjp(_post_fn, o, r, k2, v, g, *c)
        return vjp(dz)
    do_h, dr1, dk1, dv1, d_g, g_lnw, g_lnb, g_rk = _rows_call(
        "rwkv_post_bwd", post_bwd, post_ins + [Rows(d_zb)], post_consts, [("heads", F32)] * 4 + [("rows", D, F32)],
        accs=[((NH, 1, HN), F32)] * 3, tm=128)
    G["ln_x_w"], G["ln_x_b"], G["r_k"] = (t.reshape(1, D) for t in (g_lnw, g_lnb, g_rk))
    (dr2, dlw, dk2, dv2, dkk, daa), early = _scan_bwd(r_h, lw_h, k_h, v_h, kk_h, a_h, s0s, do_h,
                                                      ex=early_grads(G) if early_grads else None)

    def pre_bwd(qr, qk, qv, qxw, qxa, qxg, dr1, dr2, dlw, dk1, dk2, dv1, dv2, dkk, daa, dg, *c):
        _, vjp = jax.vjp(_pre_fn, qr, qk, qv, qxw, qxa, qxg, *c)
        g = vjp((dr1 + dr2, dlw, dk1 + dk2, dv1 + dv2, dkk, daa, dg))
        dq = jnp.concatenate(g[:6], axis=-1)
        return (dq,) + tuple(g[6:])
    pre_b_ins = q_ins + [Heads(dr1), Heads(dr2), Heads(dlw), Heads(dk1), Heads(dk2), Heads(dv1), Heads(dv2),
                         Heads(dkk), Heads(daa), Rows(d_g)]
    d_q, G["w_lora"], G["w0"], G["a_lora"], G["a0"], G["g_lora"], G["k_k"], G["k_a"] = _rows_call(
        "rwkv_pre_bwd", pre_bwd, pre_b_ins, pre_consts, [("rows", RW_INT, F32)],
        accs=[((128, D), F32), ((1, D), F32), ((128, D), F32), ((1, D), F32), ((256, D), F32), ((1, D), F32),
              ((1, D), F32)], tm=128)
    d_p_rw, dsb0, dsb1 = _mix_bwd(d_q, p_rw, W["sb"])
    G["sb"] = jnp.concatenate([dsb0, dsb1], axis=0)

    G["w_sgu_t"] = _matmul("proj_sgu_dw", d_p_sgu, a, "tn", out_dtype=GRAD_PAYLOAD)
    G["w_rw_t"] = _matmul("proj_rwkv_dw", d_p_rw, a, "tn", out_dtype=GRAD_PAYLOAD)
    G["w_gate_t"] = _matmul("proj_gate_dw", d_gate, a, "tn", out_dtype=GRAD_PAYLOAD)
    token = w_in_grads_ready(G) if w_in_grads_ready else None
    da1 = _matmul("proj_sgu_dx", d_p_sgu, W["w_sgu_t"], "nn", after=token)
    da2 = _matmul("proj_rwkv_dx", d_p_rw, W["w_rw_t"], "nn", after=token)
    da3 = _matmul("proj_gate_dx", d_gate, W["w_gate_t"], "nn", after=token)

    def norm1_bwd(x, da1, da2, da3, dh1, g):
        _, vjp = jax.vjp(_rms, x, g)
        dx, dg = vjp(da1 + da2 + da3)
        return dh1 + dx, dg
    dx, G["g_mix"] = _rows_call("norm_mix_bwd", norm1_bwd, [Rows(x), Rows(da1), Rows(da2), Rows(da3), Rows(dh1)],
                                [W["g_mix"]], [("rows", D, F32)], accs=[((1, D), F32)])
    return loss_acc[0, 0], dx, G, early


class Exchange:
    def __init__(self, bufs, gathers):
        self.bufs, self.gathers, self.nb = list(bufs), list(gathers), len(bufs)
        self.any_specs = [pl.BlockSpec(memory_space=pl.ANY)] * self.nb
        self.out_shape = [jax.ShapeDtypeStruct((N_DEV,) + (b.shape if g else b.shape[1:]), b.dtype)
                          for b, g in zip(self.bufs, self.gathers)]
        n = (N_DEV - 1) * self.nb
        self.sem_shapes = [pltpu.SemaphoreType.DMA((n,)), pltpu.SemaphoreType.DMA((n,)),
                           pltpu.SemaphoreType.DMA((self.nb,))]

    def _copies(self, in_refs, out_refs, sems):
        send_sems, recv_sems, local_sems = sems
        x, y, c = lax.axis_index("x"), lax.axis_index("y"), lax.axis_index("c")
        me = 4 * x + 2 * y + c

        def src(b, dest):
            return in_refs[b] if self.gathers[b] else in_refs[b].at[dest]

        local = [pltpu.make_async_copy(src(b, me), out_refs[b].at[me], local_sems.at[b]) for b in range(self.nb)]
        sends, recvs = [], []
        for kbits in range(1, N_DEV):
            px = 1 - x if kbits & 4 else x
            py = 1 - y if kbits & 2 else y
            pc = 1 - c if kbits & 1 else c
            peer = 4 * px + 2 * py + pc
            for b in range(self.nb):
                s = (kbits - 1) * self.nb + b
                sends.append(pltpu.make_async_remote_copy(
                    src_ref=src(b, peer), dst_ref=out_refs[b].at[me], send_sem=send_sems.at[s],
                    recv_sem=recv_sems.at[s], device_id=(px, py, pc), device_id_type=pl.DeviceIdType.MESH))
                recvs.append(pltpu.make_async_remote_copy(
                    src_ref=src(b, peer), dst_ref=out_refs[b].at[peer], send_sem=send_sems.at[s],
                    recv_sem=recv_sems.at[s], device_id=(px, py, pc), device_id_type=pl.DeviceIdType.MESH))
        return local, sends, recvs

    def start(self, in_refs, out_refs, sems):
        local, sends, _ = self._copies(in_refs, out_refs, sems)
        for cp in sends + local:
            cp.start()

    def wait(self, in_refs, out_refs, sems):
        local, sends, recvs = self._copies(in_refs, out_refs, sems)
        for cp in recvs:
            cp.wait_recv()
        for cp in sends:
            cp.wait_send()
        for cp in local:
            cp.wait()

    def schedule(self, n_steps):
        return [(0, self.start), (n_steps - 1, self.wait)]


def _exchange(name, bufs, gather):
    ex = Exchange(bufs, gather if isinstance(gather, (list, tuple)) else [gather] * len(bufs))

    def body(*refs):
        in_refs, out_refs, sems = refs[:ex.nb], refs[ex.nb:2 * ex.nb], refs[2 * ex.nb:]
        ex.start(in_refs, out_refs, sems)
        ex.wait(in_refs, out_refs, sems)

    return pl.pallas_call(body, name=name, in_specs=ex.any_specs, out_specs=ex.any_specs, out_shape=ex.out_shape,
                          scratch_shapes=ex.sem_shapes)(*ex.bufs)


N_CHIP = 4


def _pair_exchange(name, blocks):
    def body(b_ref, got_ref, send_sems, recv_sems):
        x, y, c = lax.axis_index("x"), lax.axis_index("y"), lax.axis_index("c")
        copies = [pltpu.make_async_remote_copy(
            src_ref=b_ref.at[2 * q + 1 - c], dst_ref=got_ref.at[q], send_sem=send_sems.at[q],
            recv_sem=recv_sems.at[q], device_id=(x, y, 1 - c), device_id_type=pl.DeviceIdType.MESH)
            for q in range(N_CHIP)]
        for cp in copies:
            cp.start()
        for cp in copies:
            cp.wait_recv()
        for cp in copies:
            cp.wait_send()

    any_spec = pl.BlockSpec(memory_space=pl.ANY)
    return pl.pallas_call(
        body, name=name, in_specs=[any_spec], out_specs=any_spec,
        out_shape=jax.ShapeDtypeStruct((N_CHIP,) + blocks.shape[1:], blocks.dtype),
        scratch_shapes=[pltpu.SemaphoreType.DMA((N_CHIP,))] * 2,
    )(blocks)


def _pair_sum(name, blocks, got, core):
    _, R, Wd = blocks.shape

    def body(c_ref, a_ref, b_ref, o_ref):
        o_ref[...] = (a_ref[...].astype(F32) + b_ref[...].astype(F32)).astype(o_ref.dtype)

    return pl.pallas_call(
        body, name=name,
        grid_spec=pltpu.PrefetchScalarGridSpec(
            num_scalar_prefetch=1, grid=(N_CHIP,),
            in_specs=[pl.BlockSpec((None, R, Wd), lambda q, c_ref: (2 * q + c_ref[0], 0, 0)),
                      pl.BlockSpec((None, R, Wd), lambda q, c_ref: (q, 0, 0))],
            out_specs=pl.BlockSpec((None, R, Wd), lambda q, c_ref: (q, 0, 0))),
        out_shape=jax.ShapeDtypeStruct(got.shape, blocks.dtype),
        compiler_params=pltpu.CompilerParams(dimension_semantics=("parallel",), vmem_limit_bytes=VMEM_LIMIT),
    )(core, blocks, got)


def _chip_copies(s_ref, land_ref, send_sems, recv_sems):
    x, y, c = lax.axis_index("x"), lax.axis_index("y"), lax.axis_index("c")
    my_q = 2 * x + y
    sends, recvs = [], []
    for kbits in range(1, N_CHIP):
        px = 1 - x if kbits & 2 else x
        py = 1 - y if kbits & 1 else y
        peer_q = 2 * px + py
        sends.append(pltpu.make_async_remote_copy(
            src_ref=s_ref.at[peer_q], dst_ref=land_ref.at[my_q], send_sem=send_sems[kbits - 1],
            recv_sem=recv_sems[kbits - 1], device_id=(px, py, c), device_id_type=pl.DeviceIdType.MESH))
        recvs.append(pltpu.make_async_remote_copy(
            src_ref=s_ref.at[peer_q], dst_ref=land_ref.at[peer_q], send_sem=send_sems[kbits - 1],
            recv_sem=recv_sems[kbits - 1], device_id=(px, py, c), device_id_type=pl.DeviceIdType.MESH))
    return sends, recvs


_HBM = pl.BlockSpec(memory_space=pltpu.HBM)
_SEM = pl.BlockSpec(memory_space=pltpu.SEMAPHORE)
N_CHIP_SEMS = 2 * (N_CHIP - 1)


def _chip_exchange_start(name, sums):
    def body(s_ref, land_ref, *outs):
        sems, token = outs[:N_CHIP_SEMS], outs[N_CHIP_SEMS + 2]
        sends, _ = _chip_copies(s_ref, land_ref, sems[:N_CHIP - 1], sems[N_CHIP - 1:])
        for cp in sends:
            cp.start()
        token[...] = jnp.zeros_like(token)

    res = pl.pallas_call(
        body, name=name, in_specs=(_HBM, _HBM),
        out_specs=(_SEM,) * N_CHIP_SEMS + (_HBM, _HBM, pl.BlockSpec(memory_space=pltpu.VMEM)),
        out_shape=(pltpu.SemaphoreType.DMA(()),) * N_CHIP_SEMS
        + (pltpu.HBM(sums.shape, sums.dtype), pltpu.HBM(sums.shape, sums.dtype), jax.ShapeDtypeStruct((8, LANES), F32)),
        input_output_aliases={0: N_CHIP_SEMS, 1: N_CHIP_SEMS + 1},
        compiler_params=pltpu.CompilerParams(has_side_effects=pltpu.SideEffectType.DATAFLOW_SIDE_EFFECTING),
    )(pltpu.with_memory_space_constraint(sums, pltpu.HBM),
      pltpu.with_memory_space_constraint(lax.empty(sums.shape, sums.dtype), pltpu.HBM))
    return res[:N_CHIP_SEMS], res[N_CHIP_SEMS], res[N_CHIP_SEMS + 1], res[N_CHIP_SEMS + 2]


def _chip_exchange_wait(name, sems, sums_thru, land_thru, after):
    def body(s_ref, land_ref, *rest):
        sems = rest[:N_CHIP_SEMS]
        sends, recvs = _chip_copies(s_ref, land_ref, sems[:N_CHIP - 1], sems[N_CHIP - 1:])
        for cp in sends:
            cp.wait_send()
        for cp in recvs:
            cp.wait_recv()

    return pl.pallas_call(
        body, name=name, in_specs=(_HBM, _HBM) + (_SEM,) * N_CHIP_SEMS + (pl.BlockSpec(memory_space=pl.ANY),),
        out_specs=(_HBM, _HBM),
        out_shape=(pltpu.HBM(sums_thru.shape, sums_thru.dtype), pltpu.HBM(sums_thru.shape, sums_thru.dtype)),
        input_output_aliases={0: 0, 1: 1},
        compiler_params=pltpu.CompilerParams(has_side_effects=pltpu.SideEffectType.DATAFLOW_SIDE_EFFECTING),
    )(sums_thru, land_thru, *sems, after)


def _adamw(name, slots, w, m, v, tr=256):
    R, Wd = w.shape[-2:]
    depth_axis = w.ndim == 3
    if R % tr == 0:
        tc = Wd
    else:
        tr, tc = R, (256 if (Wd % 256 == 0 and R > 256) else Wd)

    def body(s_ref, w_ref, m_ref, v_ref, g_out, d_out, m_out, v_out):
        g = s_ref[0].astype(F32)
        for j in range(1, slots.shape[0]):
            g = g + s_ref[j].astype(F32)
        m_new = ADAM_B1 * m_ref[...] + (1.0 - ADAM_B1) * g
        v_new = ADAM_B2 * v_ref[...] + (1.0 - ADAM_B2) * jnp.square(g)
        m_hat = m_new / (1.0 - ADAM_B1 ** ADAM_STEP)
        v_hat = v_new / (1.0 - ADAM_B2 ** ADAM_STEP)
        g_out[...] = g
        d_out[...] = -ADAM_LR * (m_hat / (jnp.sqrt(v_hat) + ADAM_EPS) + ADAM_WD * w_ref[...])
        m_out[...] = m_new
        v_out[...] = v_new

    if depth_axis:
        row = pl.BlockSpec((None, tr, tc), lambda i, j: (0, i, j))
    else:
        row = pl.BlockSpec((tr, tc), lambda i, j: (i, j))
    return pl.pallas_call(
        body, name=name, grid=(R // tr, Wd // tc),
        in_specs=[pl.BlockSpec((slots.shape[0], tr, tc), lambda i, j: (0, i, j)), row, row, row],
        out_specs=[row] * 4, out_shape=[jax.ShapeDtypeStruct(w.shape, F32)] * 4,
        compiler_params=pltpu.CompilerParams(dimension_semantics=("parallel", "parallel"),
                                             vmem_limit_bytes=VMEM_LIMIT),
    )(slots, w, m, v)


PACK_W = 1024
PACKED = [(n, s) for n, s in REPLICATED if n != "sgu_w"]
_SMALL_SIZES = [int(np.prod(s)) for _, s in PACKED]
_SMALL_ROWS = _round_up(_round_up(sum(_SMALL_SIZES) + PACK_W, PACK_W) // PACK_W, 8)
_LOSS_AT = sum(_SMALL_SIZES)
W_IN_SHARD = P_TOTAL // N_DEV


def _pack_rows(parts, rows, dtype):
    flat = jnp.concatenate([p.reshape(-1).astype(dtype) for p in parts])
    return jnp.pad(flat, (0, rows * PACK_W - flat.shape[0])).reshape(rows, PACK_W)


def _w_in_groups_t(blocks):
    wt = blocks.reshape(P_TOTAL, D)
    o, c = 2 * D, 2 * D + 3 * D
    z = lambda r: jnp.zeros((r, D), wt.dtype)
    rw = jnp.concatenate([wt[o:c], wt[c:c + L_W], z(128 - L_W), wt[c + L_W:c + L_W + L_A], z(128 - L_A),
                          wt[c + L_W + L_A:o + C_B], z(256 - L_G)], axis=0)
    return wt[:o], rw, wt[o + C_B:]


def _w_in_grad_blocks(g_sgu_t, g_rw_t, g_gate_t):
    c = 3 * D
    full = jnp.concatenate([g_sgu_t, g_rw_t[:c], g_rw_t[c:c + L_W], g_rw_t[c + 128:c + 128 + L_A],
                            g_rw_t[c + 256:c + 256 + L_G], g_gate_t], axis=0)
    return full.reshape(N_DEV, W_IN_SHARD, D)


def _mesh_index():
    me = 4 * lax.axis_index("x") + 2 * lax.axis_index("y") + lax.axis_index("c")
    return me.astype(jnp.int32).reshape(1)


def _fill_slot(name, dst, src, idx, src_idx=None):
    R, Wd = dst.shape[1:]
    scalars = [idx] if src_idx is None else [idx, src_idx]
    if src_idx is None:
        src_spec = pl.BlockSpec((R, Wd), lambda i, *s: (0, 0))
    else:
        src_spec = pl.BlockSpec((None, R, Wd), lambda i, *s: (s[1][0], 0, 0))

    def body(*refs):
        src_ref, out_ref = refs[len(scalars) + 1], refs[len(scalars) + 2]
        out_ref[...] = src_ref[...]

    return pl.pallas_call(
        body, name=name,
        grid_spec=pltpu.PrefetchScalarGridSpec(
            num_scalar_prefetch=len(scalars), grid=(1,),
            in_specs=[pl.BlockSpec(memory_space=pl.ANY), src_spec],
            out_specs=pl.BlockSpec((None, R, Wd), lambda i, *s: (s[0][0], 0, 0))),
        out_shape=jax.ShapeDtypeStruct(dst.shape, dst.dtype),
        input_output_aliases={len(scalars): 0},
        compiler_params=pltpu.CompilerParams(vmem_limit_bytes=VMEM_LIMIT),
    )(*scalars, dst, src)


class TwoLevelGather:
    def __init__(self, bufs, skip_own=()):
        self.bufs, self.nb, self.skip_own = list(bufs), len(bufs), tuple(skip_own)
        self.any_specs = [pl.BlockSpec(memory_space=pl.ANY)] * self.nb
        self.out_shape = [jax.ShapeDtypeStruct((N_DEV,) + b.shape, b.dtype) for b in self.bufs]
        self.sem_shapes = [pltpu.SemaphoreType.DMA((7 * self.nb,)), pltpu.SemaphoreType.DMA((7 * self.nb,)),
                           pltpu.SemaphoreType.DMA((self.nb,))]

    def _copies(self, in_refs, out_refs, sems):
        send_sems, recv_sems, local_sems = sems
        nb = self.nb
        x, y, c = lax.axis_index("x"), lax.axis_index("y"), lax.axis_index("c")
        me, sibling = (x, y, c), (x, y, 1 - c)
        chips = [(1 - x, y), (x, 1 - y), (1 - x, 1 - y)]

        def slot(b, dev):
            return out_refs[b].at[4 * dev[0] + 2 * dev[1] + dev[2]]

        def copy(b, k, block, to, own=False):
            return pltpu.make_async_remote_copy(
                src_ref=in_refs[b] if own else slot(b, block), dst_ref=slot(b, block),
                send_sem=send_sems.at[7 * b + k], recv_sem=recv_sems.at[7 * b + k], device_id=to,
                device_id_type=pl.DeviceIdType.MESH)

        cp = {}
        cp["local"] = [pltpu.make_async_copy(in_refs[b], slot(b, me), local_sems.at[b]) for b in range(nb)
                       if b not in self.skip_own]
        cp["first"] = [copy(b, 0, me, sibling, own=True) for b in range(nb)]
        cp["first"] += [copy(b, 1 + j, me, (*chip, c), own=True) for j, chip in enumerate(chips) for b in range(nb)]
        cp["over_ici"] = [copy(b, 1 + j, (*chip, c), me) for j, chip in enumerate(chips) for b in range(nb)]
        cp["passed"] = [copy(b, 4 + j, (*chip, c), sibling) for j, chip in enumerate(chips) for b in range(nb)]
        cp["from_sibling"] = [copy(b, 0, sibling, me) for b in range(nb)]
        cp["from_sibling"] += [copy(b, 4 + j, (*chip, 1 - c), me) for j, chip in enumerate(chips) for b in range(nb)]
        return cp

    def start(self, in_refs, out_refs, sems):
        cp = self._copies(in_refs, out_refs, sems)
        for c in cp["first"] + cp["local"]:
            c.start()

    def forward(self, in_refs, out_refs, sems):
        cp = self._copies(in_refs, out_refs, sems)
        for arrived, onward in zip(cp["over_ici"], cp["passed"]):
            arrived.wait_recv()
            onward.start()

    def finish(self, in_refs, out_refs, sems):
        cp = self._copies(in_refs, out_refs, sems)
        for c in cp["from_sibling"]:
            c.wait_recv()
        for c in cp["first"] + cp["passed"]:
            c.wait_send()
        for c in cp["local"]:
            c.wait()

    def schedule(self, n_steps):
        return [(0, self.start), (max(n_steps - 3, 0), self.forward), (n_steps - 1, self.finish)]


def _all_gather_two_level(name, bufs, skip_own=()):
    ex = TwoLevelGather(bufs, skip_own)

    def body(*refs):
        args = refs[:ex.nb], refs[ex.nb:2 * ex.nb], refs[2 * ex.nb:]
        ex.start(*args)
        ex.forward(*args)
        ex.finish(*args)

    return pl.pallas_call(body, name=name, in_specs=ex.any_specs, out_specs=ex.any_specs, out_shape=ex.out_shape,
                          scratch_shapes=ex.sem_shapes)(*ex.bufs)


def _cols_from_blocks(blk):
    return jnp.transpose(blk, (1, 0, 2)).reshape(blk.shape[1], -1)


def _cols_to_blocks(g):
    r, c = g.shape
    return jnp.transpose(g.reshape(r, N_DEV, c // N_DEV), (1, 0, 2))


FIRST_WEIGHTS = ["w_in", "shift_b", "w_lora_w", "a_lora_w", "g_lora_w"]
LATE_WEIGHTS = ["w_proj_a", "w_proj_b", "w_out", "w_ffn1", "w_ffn2"]


def _late_weights(shards):
    ex = TwoLevelGather([shards[n].astype(BF16) for n in LATE_WEIGHTS])

    def finish(results):
        got = dict(zip(LATE_WEIGHTS, results))
        W = {n: got[n].reshape(-1, D) for n in ("w_proj_a", "w_proj_b", "w_out", "w_ffn2")}
        W["w_ffn1"] = got["w_ffn1"].reshape(N_DEV, D, -1)
        return W
    return ex, finish


def _gather_weights(shards):
    def payload(n):
        if n == "w_in":
            return jnp.transpose(shards[n][0]).astype(BF16)
        return shards[n] if n == "shift_b" else shards[n].astype(BF16)
    payloads = [payload(n) for n in FIRST_WEIGHTS]
    got = dict(zip(FIRST_WEIGHTS, _all_gather_two_level("weight_all_gather", payloads, skip_own=(0,))))
    got["w_in"] = _fill_slot("w_in_own_slot", got["w_in"], payloads[0], _mesh_index())
    W = {}
    W["w_sgu_t"], W["w_rw_t"], W["w_gate_t"] = _w_in_groups_t(got["w_in"])
    z = lambda r, c, dt: jnp.zeros((r, c), dt)
    W["w_lora"] = jnp.concatenate([_cols_from_blocks(got["w_lora_w"][:, 0]).astype(F32), z(128 - L_W, D, F32)], axis=0)
    W["a_lora"] = jnp.concatenate([_cols_from_blocks(got["a_lora_w"][:, 0]).astype(F32), z(128 - L_A, D, F32)], axis=0)
    W["g_lora"] = jnp.concatenate([_cols_from_blocks(got["g_lora_w"][:, 0]).astype(F32), z(256 - L_G, D, F32)], axis=0)
    sb = _cols_from_blocks(got["shift_b"][:, 0])
    W["sb"] = jnp.concatenate([sb[:, :3 * D], sb[:, 3 * D:3 * D + L_W], z(2, 128 - L_W, F32),
                               sb[:, 3 * D + L_W:3 * D + L_W + L_A], z(2, 128 - L_A, F32),
                               sb[:, 3 * D + L_W + L_A:], z(2, 256 - L_G, F32)], axis=1)
    return W


def _replicated_weights(rep):
    W = {n: rep[n] for n in ("g_mix", "sgu_ln_w", "sgu_ln_b", "w0", "a0", "k_k", "k_a", "r_k", "ln_x_w", "ln_x_b",
                             "g_ffn")}
    W["g_final"] = rep["g_final"].reshape(1, D)
    W["sgu_w"] = rep["sgu_w"][0]
    W["sgu_bt"] = jnp.transpose(rep["sgu_b"][0])
    return W


def _late_grad_blocks(G):
    blocks = {n: G[n].reshape(N_DEV, -1, D) for n in ("w_proj_a", "w_proj_b", "w_out", "w_ffn2")}
    blocks["w_ffn1"] = G["w_ffn1"]
    return Exchange([blocks[n] for n in LATE_WEIGHTS] + [G["sgu_w"].reshape(SGU_G * SGU_C, SGU_C).astype(GRAD_PAYLOAD)],
                    [False] * len(LATE_WEIGHTS) + [True])


def _first_grad_blocks(G):
    sbg = G["sb"]
    c = 3 * D
    sb = jnp.concatenate([sbg[:, :c], sbg[:, c:c + L_W], sbg[:, c + 128:c + 128 + L_A],
                          sbg[:, c + 256:c + 256 + L_G]], axis=1)
    return {
        "shift_b": _cols_to_blocks(sb),
        "w_lora_w": _cols_to_blocks(G["w_lora"][:L_W]), "a_lora_w": _cols_to_blocks(G["a_lora"][:L_A]),
        "g_lora_w": _cols_to_blocks(G["g_lora"][:L_G]),
    }


def _replicated_grads(G):
    small = {n: G[n] for n in ("g_mix", "sgu_ln_w", "sgu_ln_b", "w0", "a0", "k_k", "k_a", "r_k", "ln_x_w", "ln_x_b",
                               "g_ffn", "g_final")}
    small["sgu_w"] = G["sgu_w"]
    small["sgu_b"] = jnp.transpose(G["sgu_bt"])
    return small


def kernel(x, g_mix, w_in, sgu_ln_w, sgu_ln_b, sgu_w, sgu_b, w_proj_a, shift_b, w_lora_w, w0, a_lora_w, a0, g_lora_w, k_k, k_a, r_k, ln_x_w, ln_x_b, w_proj_b, w_out, g_ffn, w_ffn1, w_ffn2, g_final, loss_target, m_g_mix, m_w_in, m_sgu_ln_w, m_sgu_ln_b, m_sgu_w, m_sgu_b, m_w_proj_a, m_shift_b, m_w_lora_w, m_w0, m_a_lora_w, m_a0, m_g_lora_w, m_k_k, m_k_a, m_r_k, m_ln_x_w, m_ln_x_b, m_w_proj_b, m_w_out, m_g_ffn, m_w_ffn1, m_w_ffn2, m_g_final, v_g_mix, v_w_in, v_sgu_ln_w, v_sgu_ln_b, v_sgu_w, v_sgu_b, v_w_proj_a, v_shift_b, v_w_lora_w, v_w0, v_a_lora_w, v_a0, v_g_lora_w, v_k_k, v_k_a, v_r_k, v_ln_x_w, v_ln_x_b, v_w_proj_b, v_w_out, v_g_ffn, v_w_ffn1, v_w_ffn2, v_g_final):
    env = dict(locals())
    weights = {n: env[n] for n in WEIGHT_ORDER}
    moms = {n: env["m_" + n] for n in WEIGHT_ORDER}
    vars_ = {n: env["v_" + n] for n in WEIGHT_ORDER}

    shards = {n: weights[n] for n, _, _ in SHARDED}
    W = _gather_weights(shards)
    W.update(_replicated_weights({n: weights[n] for n, _ in REPLICATED}))
    in_flight = {}

    def send_w_in_grads(G):
        blocks = _w_in_grad_blocks(G["w_sgu_t"], G["w_rw_t"], G["w_gate_t"])
        got = _pair_exchange("grad_pair_exchange", blocks)
        core = lax.axis_index("c").astype(jnp.int32).reshape(1)
        sums = _pair_sum("grad_pair_sum", blocks, got, core)
        in_flight["sems"], in_flight["sums"], in_flight["land"], token = _chip_exchange_start("grad_chip_start", sums)
        return token

    loss_part, dx, G, late_slots = _local_step(x[0], loss_target[0], W, late_weights=_late_weights(shards),
                                               early_grads=_late_grad_blocks, w_in_grads_ready=send_w_in_grads)

    slots = dict(zip(LATE_WEIGHTS, late_slots))
    blocks = _first_grad_blocks(G)
    small = _replicated_grads(G)
    small_parts = [small[n] for n, _ in PACKED] + [jnp.full((PACK_W,), loss_part, F32)]
    rest = [n for n in FIRST_WEIGHTS if n != "w_in"]
    res = _exchange("grad_exchange", [blocks[n] for n in rest] + [_pack_rows(small_parts, _SMALL_ROWS, F32)],
                    [False] * len(rest) + [True])
    slots.update(zip(rest, res[:-1]))
    small_slots = res[-1]
    sums, chip_slots = _chip_exchange_wait("grad_chip_wait", in_flight["sems"], in_flight["sums"], in_flight["land"],
                                           after=small_slots)
    my_chip = (2 * lax.axis_index("x") + lax.axis_index("y")).astype(jnp.int32).reshape(1)
    slots["w_in"] = _fill_slot("grad_own_slot", chip_slots, sums, my_chip, src_idx=my_chip)

    outs = [dict(), dict(), dict(), dict()]
    for n, _, _ in SHARDED:
        if n == "w_in":
            res = _adamw("adamw_" + n, slots[n], *[jnp.transpose(t[0]) for t in (weights[n], moms[n], vars_[n])])
            res = [jnp.transpose(t)[None] for t in res]
        else:
            res = _adamw("adamw_" + n, slots[n], weights[n], moms[n], vars_[n])
        for k in range(4):
            outs[k][n] = res[k]

    sgu_shape = (SGU_G * SGU_C, SGU_C)
    res = _adamw("adamw_sgu_w", late_slots[len(LATE_WEIGHTS)], *[t.reshape(sgu_shape) for t in
                                                                  (weights["sgu_w"], moms["sgu_w"], vars_["sgu_w"])])
    for k in range(4):
        outs[k]["sgu_w"] = res[k].reshape(weights["sgu_w"].shape)

    def packed(d):
        return _pack_rows([d[n] for n, _ in PACKED], _SMALL_ROWS, F32)
    small_out = _adamw("adamw_replicated", small_slots, packed(weights), packed(moms), packed(vars_))
    for k in range(4):
        flat = small_out[k].reshape(-1)
        off = 0
        for (n, s), size in zip(PACKED, _SMALL_SIZES):
            outs[k][n] = flat[off:off + size].reshape(s)
            off += size
    loss = small_out[0].reshape(-1)[_LOSS_AT]
    return (loss, dx[None], *[outs[0][n] for n in WEIGHT_ORDER], *[outs[1][n] for n in WEIGHT_ORDER],
            *[outs[2][n] for n in WEIGHT_ORDER], *[outs[3][n] for n in WEIGHT_ORDER])
```

```python
import functools
import numpy as np
import jax
import jax.numpy as jnp
from jax import lax
from jax.experimental import pallas as pl
from jax.experimental.pallas import tpu as pltpu

F32 = jnp.float32
BF16 = jnp.bfloat16

D = 1024
NH, HN = 16, 64
SGU_G, SGU_C = 8, 128
L_W, L_A, L_G = 64, 64, 160
C_B = 3 * D + L_W + L_A + L_G
P_TOTAL = 2 * D + C_B + 2 * D
D_FF = 4 * D
RW_INT = 3 * D + 128 + 128 + 256
NORM_EPS, LN_EPS, GN_EPS = 1e-6, 1e-5, 64e-5
N_DEV = 8
LANES = 128
SCAN_C = 64
SOLVE_B = 16
SCAN_PRECISION = lax.Precision.HIGH
SCAN_OUT_PRECISION = lax.Precision.DEFAULT
GRAD_PAYLOAD = BF16
VMEM_LIMIT = 56 * 1024 * 1024
MATMUL_VMEM_BUDGET = 40 * 1024 * 1024
STEP_COST_BYTES = 512 * 1024
HBM_COST_RATIO = 3

ADAM_LR, ADAM_B1, ADAM_B2, ADAM_EPS, ADAM_WD, ADAM_STEP = 0.001, 0.9, 0.999, 1e-08, 0.01, 10

SHARDED = [
    ("w_in", (D, P_TOTAL), 1), ("w_proj_a", (D, D), 0), ("shift_b", (2, C_B), 1), ("w_lora_w", (L_W, D), 1),
    ("a_lora_w", (L_A, D), 1), ("g_lora_w", (L_G, D), 1), ("w_proj_b", (D, D), 0), ("w_out", (D, D), 0),
    ("w_ffn1", (D, D_FF), 1), ("w_ffn2", (D_FF, D), 0),
]
REPLICATED = [
    ("g_mix", (1, D)), ("sgu_ln_w", (1, D)), ("sgu_ln_b", (1, D)), ("sgu_w", (1, SGU_G, SGU_C, SGU_C)),
    ("sgu_b", (1, SGU_G, SGU_C)), ("w0", (1, D)), ("a0", (1, D)), ("k_k", (1, D)), ("k_a", (1, D)), ("r_k", (1, D)),
    ("ln_x_w", (1, D)), ("ln_x_b", (1, D)), ("g_ffn", (1, D)), ("g_final", (D,)),
]
WEIGHT_ORDER = ["g_mix", "w_in", "sgu_ln_w", "sgu_ln_b", "sgu_w", "sgu_b", "w_proj_a", "shift_b", "w_lora_w", "w0",
                "a_lora_w", "a0", "g_lora_w", "k_k", "k_a", "r_k", "ln_x_w", "ln_x_b", "w_proj_b", "w_out", "g_ffn",
                "w_ffn1", "w_ffn2", "g_final"]


def _shard_shape(shape, axis):
    s = list(shape)
    s[axis] //= N_DEV
    return tuple(s)


def _round_up(n, m):
    return (n + m - 1) // m * m


def _pick(n, target):
    if n <= target:
        return n
    best = None
    for t in range(LANES, target + 1, LANES):
        if n % t == 0:
            best = t
    assert best is not None, (n, target)
    return best


def _matmul(name, a, b, mode, out_dtype=F32, tm=2048, tn=1024, tk=2048, out_blocks=None, epilogue=None, extras=(),
            out_dtypes=(), after=None):
    b_blocks = b.shape[0] if b.ndim == 3 else None
    bshape = b.shape if b.ndim == 2 else (b.shape[1], b.shape[0] * b.shape[2])
    if mode == "nn":
        (M, K), (K2, N) = a.shape, bshape
    elif mode == "nt":
        (M, K), (N, K2) = a.shape, bshape
    else:
        (K, M), (K2, N) = a.shape, bshape
    assert K == K2, (name, a.shape, b.shape)
    assert b_blocks is None or mode != "tn"
    assert out_blocks is None or mode == "tn"
    tn = min(tn, N // (out_blocks or 1), bshape[1] // b_blocks if (b_blocks and mode == "nn") else tn)
    tk = min(tk, bshape[1] // b_blocks if (b_blocks and mode == "nt") else tk)
    tm, tn, tk = _pick(M, tm), _pick(N, tn), _pick(K, tk)

    def vmem_bytes(tm, tk):
        tiles = tm * tk * a.dtype.itemsize + tk * tn * b.dtype.itemsize
        for dt in (out_dtypes if epilogue else (out_dtype,)):
            tiles += tm * tn * jnp.dtype(dt).itemsize
        for x in extras:
            arr = x[0] if isinstance(x, tuple) else x
            tiles += (tm if arr.shape[0] > 1 else 1) * tn * arr.dtype.itemsize
        return 2 * tiles + (tm * tn * 4 if K // tk > 1 else 0)

    def cost(tm, tk):
        ni, nj, nk = M // tm, N // tn, K // tk
        steps = ni * nj * nk
        acc_passes = steps * tm * tn * 8 if nk > 1 else 0
        a_reads = M * K * a.dtype.itemsize * (nj if nk > 1 else 1)
        b_reads = K * N * b.dtype.itemsize * (ni if (nj > 1 or nk > 1) else 1)
        return (steps * STEP_COST_BYTES + acc_passes + vmem_bytes(tm, tk) // 2
                + HBM_COST_RATIO * (a_reads + b_reads))

    options = [(m, k) for m in {_pick(M, max(t, LANES)) for t in (tm, tm // 2, tm // 4)}
               for k in {_pick(K, max(t, LANES)) for t in (tk, tk // 2, tk // 4)}
               if vmem_bytes(m, k) <= MATMUL_VMEM_BUDGET]
    tm, tk = min(options, key=lambda o: cost(*o))
    nk = K // tk
    dims = {"nn": (((1,), (0,)), ((), ())), "nt": (((1,), (1,)), ((), ())), "tn": (((0,), (0,)), ((), ()))}[mode]

    n_x, n_o = len(extras), len(out_dtypes) if epilogue else 1
    n_after = 0 if after is None else 1

    def body(a_ref, b_ref, *rest):
        x_refs, o_refs, acc = rest[:n_x], rest[n_x + n_after:n_x + n_after + n_o], rest[n_x + n_after + n_o:]
        part = lax.dot_general(a_ref[...].astype(BF16), b_ref[...].astype(BF16), dims, preferred_element_type=F32)

        def finish(res):
            outs = epilogue(res, *[r[...] for r in x_refs]) if epilogue else (res,)
            for r, v in zip(o_refs, outs):
                r[...] = v.astype(r.dtype)

        if nk == 1:
            finish(part)
            return
        acc_ref, k = acc[0], pl.program_id(2)

        @pl.when(k == 0)
        def _():
            acc_ref[...] = part

        @pl.when(k > 0)
        def _():
            acc_ref[...] += part

        @pl.when(k == nk - 1)
        def _():
            finish(acc_ref[...])

    a_spec = {"nn": pl.BlockSpec((tm, tk), lambda i, j, k: (i, k)), "nt": pl.BlockSpec((tm, tk), lambda i, j, k: (i, k)),
              "tn": pl.BlockSpec((tk, tm), lambda i, j, k: (k, i))}[mode]
    b_spec = {"nn": pl.BlockSpec((tk, tn), lambda i, j, k: (k, j)), "nt": pl.BlockSpec((tn, tk), lambda i, j, k: (j, k)),
              "tn": pl.BlockSpec((tk, tn), lambda i, j, k: (k, j))}[mode]
    if b_blocks and mode == "nn":
        per = b.shape[2] // tn
        b_spec = pl.BlockSpec((None, tk, tn), lambda i, j, k: (j // per, k, j % per))
    elif b_blocks:
        per = b.shape[2] // tk
        b_spec = pl.BlockSpec((None, tn, tk), lambda i, j, k: (k // per, j, k % per))
    out_spec = pl.BlockSpec((tm, tn), lambda i, j, k: (i, j))
    out_shape = jax.ShapeDtypeStruct((M, N), out_dtype)
    if out_blocks:
        per_o = N // out_blocks // tn
        out_spec = pl.BlockSpec((None, tm, tn), lambda i, j, k: (j // per_o, i, j % per_o))
        out_shape = jax.ShapeDtypeStruct((out_blocks, M, N // out_blocks), out_dtype)
    x_specs, x_args = [], []
    for x in extras:
        arr, off = x if isinstance(x, tuple) else (x, 0)
        if arr.shape[0] == 1:
            x_specs.append(pl.BlockSpec((1, tn), lambda i, j, k: (0, j)))
        else:
            x_specs.append(pl.BlockSpec((tm, tn), lambda i, j, k, off=off: (i, j + off)))
        x_args.append(arr)
    res = pl.pallas_call(
        body, name=name, grid=(M // tm, N // tn, nk),
        in_specs=[a_spec, b_spec] + x_specs + [pl.BlockSpec(memory_space=pl.ANY)] * n_after,
        out_specs=[out_spec] * n_o if epilogue else out_spec,
        out_shape=[jax.ShapeDtypeStruct((M, N), dt) for dt in out_dtypes] if epilogue else out_shape,
        scratch_shapes=[pltpu.VMEM((tm, tn), F32)] if nk > 1 else [],
        compiler_params=pltpu.CompilerParams(dimension_semantics=("parallel", "parallel", "arbitrary"),
                                             vmem_limit_bytes=VMEM_LIMIT),
    )(a, b, *x_args, *([after] if n_after else []))
    return res


class Rows:
    def __init__(self, arr, width=None, cb=0):
        self.arr, self.width, self.cb = arr, (arr.shape[1] if width is None else width), cb


class Heads:
    def __init__(self, arr):
        self.arr = arr


class Halo:
    def __init__(self, arr, side):
        self.arr, self.side = arr, side


def _rows_call(name, fn, ins, consts, outs, accs=(), tm=256, with_pid=False):
    T = next(o.arr.shape[1] if isinstance(o, Heads) else o.arr.shape[0] for o in ins if not isinstance(o, Halo))
    tm = min(tm, T)
    n_tiles = T // tm
    n_in, n_c, n_out = len(ins), len(consts), len(outs)
    in_specs, args = [], []
    for o in ins:
        if isinstance(o, Rows):
            in_specs.append(pl.BlockSpec((tm, o.width), lambda i, cb=o.cb: (i, cb)))
        elif isinstance(o, Heads):
            in_specs.append(pl.BlockSpec((NH, tm, HN), lambda i: (0, i, 0)))
        else:
            w = o.arr.shape[1]
            if o.side < 0:
                in_specs.append(pl.BlockSpec((8, w), lambda i: (jnp.maximum(i * (tm // 8) - 1, 0), 0)))
            else:
                in_specs.append(pl.BlockSpec((8, w), lambda i: (jnp.minimum((i + 1) * (tm // 8), T // 8 - 1), 0)))
        args.append(o.arr)
    for c in consts:
        in_specs.append(pl.BlockSpec(c.shape, lambda i, nd=c.ndim: (0,) * nd))
        args.append(c)
    out_specs, out_shape = [], []
    for o in outs:
        if o[0] == "rows":
            out_specs.append(pl.BlockSpec((tm, o[1]), lambda i: (i, 0)))
            out_shape.append(jax.ShapeDtypeStruct((T, o[1]), o[2]))
        else:
            out_specs.append(pl.BlockSpec((NH, tm, HN), lambda i: (0, i, 0)))
            out_shape.append(jax.ShapeDtypeStruct((NH, T, HN), o[1]))
    for shape, dt in accs:
        out_specs.append(pl.BlockSpec(shape, lambda i, nd=len(shape): (0,) * nd))
        out_shape.append(jax.ShapeDtypeStruct(shape, dt))

    def body(*refs):
        i = pl.program_id(0)
        vals = []
        vals = [r[...] for r in refs[:n_in + n_c]]
        res = fn(i, n_tiles, *vals) if with_pid else fn(*vals)
        out_refs = refs[n_in + n_c:]
        for r, v in zip(out_refs[:n_out], res[:n_out]):
            r[...] = v.astype(r.dtype)
        if accs:
            @pl.when(i == 0)
            def _():
                for r in out_refs[n_out:]:
                    r[...] = jnp.zeros_like(r)

            for r, v in zip(out_refs[n_out:], res[n_out:]):
                r[...] += v.astype(r.dtype)

    res = pl.pallas_call(
        body, name=name, grid=(n_tiles,), in_specs=in_specs, out_specs=out_specs, out_shape=out_shape,
        compiler_params=pltpu.CompilerParams(dimension_semantics=("arbitrary",), vmem_limit_bytes=VMEM_LIMIT),
    )(*args)
    return res


def _rms(x, g):
    return x * lax.rsqrt(jnp.mean(x * x, axis=-1, keepdims=True) + NORM_EPS) * g


def _gelu(x):
    return 0.5 * x * (1.0 + lax.erf(x * 0.7071067811865476))


def _sigmoid(x):
    return 1.0 / (1.0 + jnp.exp(-x))


def _bdot(a, b):
    return jnp.dot(a.astype(BF16), b.astype(BF16), preferred_element_type=F32)


def _to_heads(x):
    return jnp.concatenate([x[:, h * HN:(h + 1) * HN][None] for h in range(NH)], axis=0)


def _from_heads(xh):
    return jnp.concatenate([xh[h] for h in range(NH)], axis=-1)


def _sgu_fn(p, ln_w, ln_b, sw, sbt):
    z = _gelu(p)
    u, v = z[:, :D], z[:, D:]
    mu = jnp.mean(v, axis=-1, keepdims=True)
    var = jnp.mean(jnp.square(v - mu), axis=-1, keepdims=True)
    vn = (v - mu) * lax.rsqrt(var + LN_EPS) * ln_w + ln_b
    ri = lax.broadcasted_iota(jnp.int32, (SGU_C, SGU_C), 0)
    ci = lax.broadcasted_iota(jnp.int32, (SGU_C, SGU_C), 1)
    mask = (ci <= ri).astype(F32)
    dg = D // SGU_G
    parts = []
    for g in range(SGU_G):
        parts.append(_bdot(sw[g] * mask, vn[:, g * dg:(g + 1) * dg]) + sbt[:, g:g + 1])
    return u * jnp.concatenate(parts, axis=-1)


def _pre_fn(qr, qk, qv, qxw, qxa, qxg, wl, w0, al, a0, gl, k_k, k_a):
    w = -jax.nn.softplus(-(w0 + _bdot(jnp.tanh(qxw), wl))) - 0.5
    lw = -jnp.exp(w)
    aa = _sigmoid(a0 + _bdot(qxa, al))
    g = _bdot(_sigmoid(qxg), gl)
    kk = _to_heads(qk * k_k)
    kk = kk / jnp.maximum(jnp.sqrt(jnp.sum(kk * kk, axis=-1, keepdims=True)), 1e-12)
    k2 = qk * (1.0 + (aa - 1.0) * k_a)
    return _to_heads(qr), _to_heads(lw), _to_heads(k2), _to_heads(qv), kk, _to_heads(aa), g


def _post_fn(o, r, k2, v, g, ln_w, ln_b, r_k):
    mu = jnp.mean(o, axis=-1, keepdims=True)
    d = o - mu
    var = jnp.mean(d * d, axis=-1, keepdims=True)
    on = d * lax.rsqrt(var + GN_EPS) * ln_w + ln_b
    bonus = jnp.sum(r * k2 * r_k, axis=-1, keepdims=True) * v
    return _from_heads(on + bonus) * g


def _gate_fn(pg, ya, yb):
    return _sigmoid(pg[:, :D]) * ya + _sigmoid(pg[:, D:]) * yb


def _bmm(x, y, cx, cy, out_path=False):
    return lax.dot_general(x, y, (((cx,), (cy,)), ((0,), (0,))),
                           precision=SCAN_OUT_PRECISION if out_path else SCAN_PRECISION, preferred_element_type=F32)


def _unit_lower_inverse(M):
    C = M.shape[1]
    ti = lax.broadcasted_iota(jnp.int32, (C, C), 0)
    tj = lax.broadcasted_iota(jnp.int32, (C, C), 1)
    eye = (ti == tj).astype(F32)
    same = lambda b: (ti // b == tj // b).astype(F32)
    X = -(M * same(SOLVE_B))
    inv = eye + X
    span = 1
    while 2 * span < SOLVE_B:
        X = _bmm(X, X, 2, 1)
        inv = inv + _bmm(inv, X, 2, 1)
        span *= 2
    b = SOLVE_B
    while b < C:
        low = M * (same(2 * b) - same(b))
        inv = inv - _bmm(_bmm(inv, low, 2, 1), inv, 2, 1)
        b *= 2
    return inv


@jax.custom_vjp
def _unit_lower_solve(inv, M, y):
    return _bmm(inv, y, 2, 1)


def _unit_lower_solve_fwd(inv, M, y):
    u = _bmm(inv, y, 2, 1)
    return u, (inv, u)


def _unit_lower_solve_bwd(res, du):
    inv, u = res
    dy = _bmm(inv, du, 1, 1)
    return jnp.zeros_like(inv), -_bmm(dy, u, 2, 2), dy


_unit_lower_solve.defvjp(_unit_lower_solve_fwd, _unit_lower_solve_bwd)


def _sum_over_time(x, reverse):
    C = x.shape[1]
    ti = lax.broadcasted_iota(jnp.int32, (C, C), 0)
    tj = lax.broadcasted_iota(jnp.int32, (C, C), 1)
    ones = jnp.broadcast_to(((tj >= ti) if reverse else (tj <= ti)).astype(BF16), (x.shape[0], C, C))
    hi = x.astype(BF16)
    r1 = x - hi.astype(F32)
    mid = r1.astype(BF16)
    lo = (r1 - mid.astype(F32)).astype(BF16)
    dn = (((2,), (1,)), ((0,), (0,)))
    return sum(lax.dot_general(ones, p, dn, preferred_element_type=F32) for p in (lo, mid, hi))


@jax.custom_vjp
def _time_cumsum(lw):
    return _sum_over_time(lw, reverse=False)


_time_cumsum.defvjp(lambda lw: (_sum_over_time(lw, reverse=False), None),
                    lambda _, d: (_sum_over_time(d, reverse=True),))


def _chunk_fn(S0, r, lw, k, v, kk, a, inv=None):
    C = SCAN_C
    bmm = _bmm
    ti = lax.broadcasted_iota(jnp.int32, (C, C), 0)
    tj = lax.broadcasted_iota(jnp.int32, (C, C), 1)
    incl = (tj <= ti).astype(F32)
    strict = (tj < ti).astype(F32)
    cum = _time_cumsum(lw)
    g_in, g_ex, g_inv = jnp.exp(cum), jnp.exp(cum - lw), jnp.exp(-cum)
    kkt, rt = kk * g_ex, r * g_in
    bk = jnp.concatenate([kk * a * g_inv, k * g_inv], axis=1)
    A = bmm(kkt, bk, 2, 2)
    M = A[:, :, :C] * strict
    n_mask = jnp.concatenate([jnp.zeros((C, C), F32), strict], axis=1)
    zv = jnp.concatenate([jnp.zeros_like(v), v], axis=1)
    s0_side = bmm(jnp.concatenate([kkt, rt], axis=1), S0, 2, 2, out_path=True)
    if inv is None:
        inv = lax.stop_gradient(_unit_lower_inverse(M))
    y = _unit_lower_solve(inv, M, s0_side[:, :C] + bmm(A * n_mask, zv, 2, 1, out_path=True))
    z = jnp.concatenate([-y, v], axis=1)
    attn = bmm(rt, bk, 2, 2) * jnp.concatenate([incl, incl], axis=1)
    O = s0_side[:, C:] + bmm(attn, z, 2, 1, out_path=True)
    S1 = (S0 + bmm(z, bk, 1, 1, out_path=True)) * g_in[:, C - 1:C, :]
    return O, S1, inv


def _scan_fwd(r, lw, k, v, kk, a, ex=None, tb=256):
    assert SCAN_C == HN
    T = r.shape[1]
    tb = min(tb, T)
    n_chunks = tb // SCAN_C
    nb = T // tb
    nx = ex.nb if ex else 0

    def body(*refs):
        r_ref, lw_ref, k_ref, v_ref, kk_ref, a_ref = refs[:6]
        x_in, (o_ref, s0_ref), x_out = refs[6:6 + nx], refs[6 + nx:8 + nx], refs[8 + nx:8 + 2 * nx]
        s_ref, sems = refs[8 + 2 * nx], refs[9 + 2 * nx:]

        plan = ex.schedule(nb) if ex else []

        @pl.when(pl.program_id(0) == 0)
        def _():
            s_ref[...] = jnp.zeros_like(s_ref)
            for at, action in plan[:1]:
                action(x_in, x_out, sems)

        def step(c, carry):
            sl = pl.ds(pl.multiple_of(c * SCAN_C, SCAN_C), SCAN_C)
            S0 = s_ref[...]
            O, S1, inv = _chunk_fn(S0, r_ref[:, sl, :], lw_ref[:, sl, :], k_ref[:, sl, :], v_ref[:, sl, :],
                                   kk_ref[:, sl, :], a_ref[:, sl, :])
            o_ref[:, sl, :] = O
            s0_ref[c, 0] = S0
            s0_ref[c, 1] = inv
            s_ref[...] = S1
            return carry

        lax.fori_loop(0, n_chunks, step, 0)

        for at, action in plan[1:]:
            pl.when(pl.program_id(0) == at)(functools.partial(action, x_in, x_out, sems))

    hm = pl.BlockSpec((NH, tb, HN), lambda i: (0, i, 0))
    res = pl.pallas_call(
        body, name="rwkv_scan_fwd", grid=(nb,), in_specs=[hm] * 6 + (ex.any_specs if ex else []),
        out_specs=[hm, pl.BlockSpec((n_chunks, 2, NH, HN, HN), lambda i: (i, 0, 0, 0, 0))]
        + (ex.any_specs if ex else []),
        out_shape=[jax.ShapeDtypeStruct((NH, T, HN), F32), jax.ShapeDtypeStruct((T // SCAN_C, 2, NH, HN, HN), F32)]
        + (ex.out_shape if ex else []),
        scratch_shapes=[pltpu.VMEM((NH, HN, HN), F32)] + (ex.sem_shapes if ex else []),
        compiler_params=pltpu.CompilerParams(dimension_semantics=("arbitrary",), vmem_limit_bytes=VMEM_LIMIT),
    )(r, lw, k, v, kk, a, *(ex.bufs if ex else []))
    return res[0], res[1], list(res[2:])


def _scan_bwd(r, lw, k, v, kk, a, s0s, do, ex=None, tb=128):
    T = r.shape[1]
    tb = min(tb, T)
    n_chunks = tb // SCAN_C
    nb = T // tb
    nx = ex.nb if ex else 0

    def body(*refs):
        r_ref, lw_ref, k_ref, v_ref, kk_ref, a_ref, s0_ref, do_ref = refs[:8]
        x_in, (dr, dlw, dk, dv, dkk, da), x_out = refs[8:8 + nx], refs[8 + nx:14 + nx], refs[14 + nx:14 + 2 * nx]
        ds_ref, sems = refs[14 + 2 * nx], refs[15 + 2 * nx:]

        plan = ex.schedule(nb) if ex else []

        @pl.when(pl.program_id(0) == 0)
        def _():
            ds_ref[...] = jnp.zeros_like(ds_ref)
            for at, action in plan[:1]:
                action(x_in, x_out, sems)

        def step(j, carry):
            c = n_chunks - 1 - j
            sl = pl.ds(pl.multiple_of(c * SCAN_C, SCAN_C), SCAN_C)
            inv = s0_ref[c, 1]
            _, vjp = jax.vjp(lambda *t: _chunk_fn(*t, inv=inv)[:2], s0_ref[c, 0], r_ref[:, sl, :], lw_ref[:, sl, :],
                             k_ref[:, sl, :], v_ref[:, sl, :], kk_ref[:, sl, :], a_ref[:, sl, :])
            g = vjp((do_ref[:, sl, :], ds_ref[...]))
            ds_ref[...] = g[0]
            for ref, val in zip((dr, dlw, dk, dv, dkk, da), g[1:]):
                ref[:, sl, :] = val
            return carry

        lax.fori_loop(0, n_chunks, step, 0)

        for at, action in plan[1:]:
            pl.when(pl.program_id(0) == at)(functools.partial(action, x_in, x_out, sems))

    hm = pl.BlockSpec((NH, tb, HN), lambda i: (0, nb - 1 - i, 0))
    res = pl.pallas_call(
        body, name="rwkv_scan_bwd", grid=(nb,),
        in_specs=[hm] * 6 + [pl.BlockSpec((n_chunks, 2, NH, HN, HN), lambda i: (nb - 1 - i, 0, 0, 0, 0)), hm]
        + (ex.any_specs if ex else []),
        out_specs=[hm] * 6 + (ex.any_specs if ex else []),
        out_shape=[jax.ShapeDtypeStruct((NH, T, HN), F32)] * 6 + (ex.out_shape if ex else []),
        scratch_shapes=[pltpu.VMEM((NH, HN, HN), F32)] + (ex.sem_shapes if ex else []),
        compiler_params=pltpu.CompilerParams(dimension_semantics=("arbitrary",), vmem_limit_bytes=VMEM_LIMIT),
    )(r, lw, k, v, kk, a, s0s, do, *(ex.bufs if ex else []))
    return list(res[:6]), list(res[6:])


def _shift_down(i, p, prev8):
    first = jnp.where(i > 0, prev8[7:8, :], 0.0)
    row = lax.broadcasted_iota(jnp.int32, p.shape, 0)
    return jnp.where(row == 0, first, pltpu.roll(p, 1, axis=0))


def _mix_fwd(p, sb, tm=256):
    def fn(i, n, p, prev8, sb):
        return (p * sb[0:1] + _shift_down(i, p, prev8) * sb[1:2],)
    return _rows_call("shift_mix_fwd", fn, [Rows(p), Halo(p, -1)], [sb], [("rows", p.shape[1], F32)], tm=tm,
                      with_pid=True)[0]


def _mix_bwd(dq, p, sb, tm=256):
    def fn(i, n, dq, next8, p, prev8, sb):
        ps = _shift_down(i, p, prev8)
        d1 = dq * sb[1:2]
        last = jnp.where(i < n - 1, next8[0:1, :] * sb[1:2], 0.0)
        row = lax.broadcasted_iota(jnp.int32, dq.shape, 0)
        up = jnp.where(row == dq.shape[0] - 1, last, pltpu.roll(d1, dq.shape[0] - 1, axis=0))
        return (dq * sb[0:1] + up, jnp.sum(dq * p, axis=0, keepdims=True), jnp.sum(dq * ps, axis=0, keepdims=True))
    w = p.shape[1]
    return _rows_call("shift_mix_bwd", fn, [Rows(dq), Halo(dq, +1), Rows(p), Halo(p, -1)], [sb], [("rows", w, F32)],
                      accs=[((1, w), F32), ((1, w), F32)], tm=tm, with_pid=True)


def _local_step(x, target, W, late_weights=None, early_grads=None, w_in_grads_ready=None):
    G = {}
    a = _rows_call("norm_mix_fwd", lambda x, g: (_rms(x, g),), [Rows(x)], [W["g_mix"]], [("rows", D, BF16)])[0]
    p_sgu = _matmul("proj_sgu", a, W["w_sgu_t"], "nt")
    p_rw = _matmul("proj_rwkv", a, W["w_rw_t"], "nt")
    p_gate = _matmul("proj_gate", a, W["w_gate_t"], "nt")

    sgu_consts = [W["sgu_ln_w"], W["sgu_ln_b"], W["sgu_w"], W["sgu_bt"]]
    s = _rows_call("sgu_fwd", lambda *t: (_sgu_fn(*t),), [Rows(p_sgu)], sgu_consts, [("rows", D, BF16)], tm=SGU_C)[0]

    q = _mix_fwd(p_rw, W["sb"])
    q_ins = [Rows(q, D, 0), Rows(q, D, 1), Rows(q, D, 2), Rows(q, 128, 24), Rows(q, 128, 25), Rows(q, 256, 13)]
    pre_consts = [W["w_lora"], W["w0"], W["a_lora"], W["a0"], W["g_lora"], W["k_k"], W["k_a"]]
    r_h, lw_h, k_h, v_h, kk_h, a_h, g_gate = _rows_call(
        "rwkv_pre_fwd", _pre_fn, q_ins, pre_consts, [("heads", F32)] * 6 + [("rows", D, F32)], tm=128)
    o_h, s0s, got = _scan_fwd(r_h, lw_h, k_h, v_h, kk_h, a_h, ex=late_weights[0] if late_weights else None)
    if late_weights:
        W = {**W, **late_weights[1](got)}
    y_a = _matmul("proj_a", s, W["w_proj_a"], "nn")
    post_ins = [Heads(o_h), Heads(r_h), Heads(k_h), Heads(v_h), Rows(g_gate)]
    post_consts = [W[n].reshape(NH, 1, HN) for n in ("ln_x_w", "ln_x_b", "r_k")]
    z_b = _rows_call("rwkv_post_fwd", lambda *t: (_post_fn(*t),), post_ins, post_consts, [("rows", D, BF16)], tm=128)[0]
    y_b, mixed = _matmul("proj_b", z_b, W["w_proj_b"], "nn", extras=[(p_gate, 0), (p_gate, 1), y_a],
                         epilogue=lambda yb, ga, gb, ya: (yb, _sigmoid(ga) * ya + _sigmoid(gb) * yb),
                         out_dtypes=(F32, BF16))
    gate_ins = [Rows(p_gate), Rows(y_a), Rows(y_b)]

    def res1(mo, x, g):
        h1 = x + mo
        return h1, _rms(h1, g)
    h1, f = _matmul("proj_out", mixed, W["w_out"], "nn", extras=[x, W["g_ffn"]], epilogue=res1,
                    out_dtypes=(F32, BF16))

    def relu_sq(u):
        r = jnp.maximum(u, 0.0)
        return r, r * r
    r1, act = _matmul("ffn_up", f, W["w_ffn1"], "nn", epilogue=relu_sq, out_dtypes=(BF16, BF16))
    ff = _matmul("ffn_down", act, W["w_ffn2"], "nn")

    def head(h1, ff, tgt, g):
        def f_(h1, ff, g):
            y = _rms(h1 + ff, g)
            return 0.5 * jnp.sum(jnp.mean(jnp.square(y - tgt), axis=-1))
        loss, (dh2, _, dg) = jax.value_and_grad(f_, argnums=(0, 1, 2))(h1, ff, g)
        return dh2, jnp.full((8, LANES), loss, F32), dg
    dh2, loss_acc, G["g_final"] = _rows_call("loss_head", head, [Rows(h1), Rows(ff), Rows(target)], [W["g_final"]],
                                             [("rows", D, F32)], accs=[((8, LANES), F32), ((1, D), F32)])

    d_u1 = _matmul("ffn_down_dx", dh2, W["w_ffn2"], "nt", extras=[r1], out_dtypes=(BF16,),
                   epilogue=lambda d_act, r: (d_act * 2.0 * r.astype(F32),))[0]
    G["w_ffn2"] = _matmul("ffn_down_dw", act, dh2, "tn", out_dtype=GRAD_PAYLOAD)
    d_f = _matmul("ffn_up_dx", d_u1, W["w_ffn1"], "nt")
    G["w_ffn1"] = _matmul("ffn_up_dw", f, d_u1, "tn", out_blocks=N_DEV, out_dtype=GRAD_PAYLOAD)

    def res1_bwd(h1, d_f, dh2, g):
        _, vjp = jax.vjp(_rms, h1, g)
        dh, dg = vjp(d_f)
        return dh2 + dh, dg
    dh1, G["g_ffn"] = _rows_call("residual_norm_bwd", res1_bwd, [Rows(h1), Rows(d_f), Rows(dh2)],
                                 [W["g_ffn"]], [("rows", D, F32)], accs=[((1, D), F32)])
    d_mixed = _matmul("proj_out_dx", dh1, W["w_out"], "nt")
    G["w_out"] = _matmul("proj_out_dw", mixed, dh1, "tn", out_dtype=GRAD_PAYLOAD)

    def gate_bwd(pg, ya, yb, dm):
        _, vjp = jax.vjp(_gate_fn, pg, ya, yb)
        return vjp(dm)
    d_gate, d_ya, d_yb = _rows_call("gate_bwd", gate_bwd, gate_ins + [Rows(d_mixed)], [],
                                    [("rows", 2 * D, F32), ("rows", D, BF16), ("rows", D, BF16)])

    d_s = _matmul("proj_a_dx", d_ya, W["w_proj_a"], "nt")
    G["w_proj_a"] = _matmul("proj_a_dw", s, d_ya, "tn", out_dtype=GRAD_PAYLOAD)

    def sgu_bwd(p, ds, *c):
        _, vjp = jax.vjp(_sgu_fn, p, *c)
        return vjp(ds)
    d_p_sgu, G["sgu_ln_w"], G["sgu_ln_b"], G["sgu_w"], G["sgu_bt"] = _rows_call(
        "sgu_bwd", sgu_bwd, [Rows(p_sgu), Rows(d_s)], sgu_consts, [("rows", 2 * D, F32)],
        accs=[((1, D), F32), ((1, D), F32), ((SGU_G, SGU_C, SGU_C), F32), ((SGU_C, SGU_G), F32)], tm=SGU_C)

    d_zb = _matmul("proj_b_dx", d_yb, W["w_proj_b"], "nt")
    G["w_proj_b"] = _matmul("proj_b_dw", z_b, d_yb, "tn", out_dtype=GRAD_PAYLOAD)

    def post_bwd(o, r, k2, v, g, dz, *c):
        _, vjp = jax.vjp(_post_fn, o, r, k2, v, g, *c)
        return vjp(dz)
    do_h, dr1, dk1, dv1, d_g, g_lnw, g_lnb, g_rk = _rows_call(
        "rwkv_post_bwd", post_bwd, post_ins + [Rows(d_zb)], post_consts, [("heads", F32)] * 4 + [("rows", D, F32)],
        accs=[((NH, 1, HN), F32)] * 3, tm=128)
    G["ln_x_w"], G["ln_x_b"], G["r_k"] = (t.reshape(1, D) for t in (g_lnw, g_lnb, g_rk))
    (dr2, dlw, dk2, dv2, dkk, daa), early = _scan_bwd(r_h, lw_h, k_h, v_h, kk_h, a_h, s0s, do_h,
                                                      ex=early_grads(G) if early_grads else None)

    def pre_bwd(qr, qk, qv, qxw, qxa, qxg, dr1, dr2, dlw, dk1, dk2, dv1, dv2, dkk, daa, dg, *c):
        _, vjp = jax.vjp(_pre_fn, qr, qk, qv, qxw, qxa, qxg, *c)
        g = vjp((dr1 + dr2, dlw, dk1 + dk2, dv1 + dv2, dkk, daa, dg))
        dq = jnp.concatenate(g[:6], axis=-1)
        return (dq,) + tuple(g[6:])
    pre_b_ins = q_ins + [Heads(dr1), Heads(dr2), Heads(dlw), Heads(dk1), Heads(dk2), Heads(dv1), Heads(dv2),
                         Heads(dkk), Heads(daa), Rows(d_g)]
    d_q, G["w_lora"], G["w0"], G["a_lora"], G["a0"], G["g_lora"], G["k_k"], G["k_a"] = _rows_call(
        "rwkv_pre_bwd", pre_bwd, pre_b_ins, pre_consts, [("rows", RW_INT, F32)],
        accs=[((128, D), F32), ((1, D), F32), ((128, D), F32), ((1, D), F32), ((256, D), F32), ((1, D), F32),
              ((1, D), F32)], tm=128)
    d_p_rw, dsb0, dsb1 = _mix_bwd(d_q, p_rw, W["sb"])
    G["sb"] = jnp.concatenate([dsb0, dsb1], axis=0)

    G["w_sgu_t"] = _matmul("proj_sgu_dw", d_p_sgu, a, "tn", out_dtype=GRAD_PAYLOAD)
    G["w_rw_t"] = _matmul("proj_rwkv_dw", d_p_rw, a, "tn", out_dtype=GRAD_PAYLOAD)
    G["w_gate_t"] = _matmul("proj_gate_dw", d_gate, a, "tn", out_dtype=GRAD_PAYLOAD)
    token = w_in_grads_ready(G) if w_in_grads_ready else None
    da1 = _matmul("proj_sgu_dx", d_p_sgu, W["w_sgu_t"], "nn", after=token)
    da2 = _matmul("proj_rwkv_dx", d_p_rw, W["w_rw_t"], "nn", after=token)
    da3 = _matmul("proj_gate_dx", d_gate, W["w_gate_t"], "nn", after=token)

    def norm1_bwd(x, da1, da2, da3, dh1, g):
        _, vjp = jax.vjp(_rms, x, g)
        dx, dg = vjp(da1 + da2 + da3)
        return dh1 + dx, dg
    dx, G["g_mix"] = _rows_call("norm_mix_bwd", norm1_bwd, [Rows(x), Rows(da1), Rows(da2), Rows(da3), Rows(dh1)],
                                [W["g_mix"]], [("rows", D, F32)], accs=[((1, D), F32)])
    return loss_acc[0, 0], dx, G, early


class Exchange:
    def __init__(self, bufs, gathers):
        self.bufs, self.gathers, self.nb = list(bufs), list(gathers), len(bufs)
        self.any_specs = [pl.BlockSpec(memory_space=pl.ANY)] * self.nb
        self.out_shape = [jax.ShapeDtypeStruct((N_DEV,) + (b.shape if g else b.shape[1:]), b.dtype)
                          for b, g in zip(self.bufs, self.gathers)]
        n = (N_DEV - 1) * self.nb
        self.sem_shapes = [pltpu.SemaphoreType.DMA((n,)), pltpu.SemaphoreType.DMA((n,)),
                           pltpu.SemaphoreType.DMA((self.nb,))]

    def _copies(self, in_refs, out_refs, sems):
        send_sems, recv_sems, local_sems = sems
        x, y, c = lax.axis_index("x"), lax.axis_index("y"), lax.axis_index("c")
        me = 4 * x + 2 * y + c

        def src(b, dest):
            return in_refs[b] if self.gathers[b] else in_refs[b].at[dest]

        local = [pltpu.make_async_copy(src(b, me), out_refs[b].at[me], local_sems.at[b]) for b in range(self.nb)]
        sends, recvs = [], []
        for kbits in range(1, N_DEV):
            px = 1 - x if kbits & 4 else x
            py = 1 - y if kbits & 2 else y
            pc = 1 - c if kbits & 1 else c
            peer = 4 * px + 2 * py + pc
            for b in range(self.nb):
                s = (kbits - 1) * self.nb + b
                sends.append(pltpu.make_async_remote_copy(
                    src_ref=src(b, peer), dst_ref=out_refs[b].at[me], send_sem=send_sems.at[s],
                    recv_sem=recv_sems.at[s], device_id=(px, py, pc), device_id_type=pl.DeviceIdType.MESH))
                recvs.append(pltpu.make_async_remote_copy(
                    src_ref=src(b, peer), dst_ref=out_refs[b].at[peer], send_sem=send_sems.at[s],
                    recv_sem=recv_sems.at[s], device_id=(px, py, pc), device_id_type=pl.DeviceIdType.MESH))
        return local, sends, recvs

    def start(self, in_refs, out_refs, sems):
        local, sends, _ = self._copies(in_refs, out_refs, sems)
        for cp in sends + local:
            cp.start()

    def wait(self, in_refs, out_refs, sems):
        local, sends, recvs = self._copies(in_refs, out_refs, sems)
        for cp in recvs:
            cp.wait_recv()
        for cp in sends:
            cp.wait_send()
        for cp in local:
            cp.wait()

    def schedule(self, n_steps):
        return [(0, self.start), (n_steps - 1, self.wait)]


def _exchange(name, bufs, gather):
    ex = Exchange(bufs, gather if isinstance(gather, (list, tuple)) else [gather] * len(bufs))

    def body(*refs):
        in_refs, out_refs, sems = refs[:ex.nb], refs[ex.nb:2 * ex.nb], refs[2 * ex.nb:]
        ex.start(in_refs, out_refs, sems)
        ex.wait(in_refs, out_refs, sems)

    return pl.pallas_call(body, name=name, in_specs=ex.any_specs, out_specs=ex.any_specs, out_shape=ex.out_shape,
                          scratch_shapes=ex.sem_shapes)(*ex.bufs)


N_CHIP = 4


def _pair_exchange(name, blocks):
    def body(b_ref, got_ref, send_sems, recv_sems):
        x, y, c = lax.axis_index("x"), lax.axis_index("y"), lax.axis_index("c")
        copies = [pltpu.make_async_remote_copy(
            src_ref=b_ref.at[2 * q + 1 - c], dst_ref=got_ref.at[q], send_sem=send_sems.at[q],
            recv_sem=recv_sems.at[q], device_id=(x, y, 1 - c), device_id_type=pl.DeviceIdType.MESH)
            for q in range(N_CHIP)]
        for cp in copies:
            cp.start()
        for cp in copies:
            cp.wait_recv()
        for cp in copies:
            cp.wait_send()

    any_spec = pl.BlockSpec(memory_space=pl.ANY)
    return pl.pallas_call(
        body, name=name, in_specs=[any_spec], out_specs=any_spec,
        out_shape=jax.ShapeDtypeStruct((N_CHIP,) + blocks.shape[1:], blocks.dtype),
        scratch_shapes=[pltpu.SemaphoreType.DMA((N_CHIP,))] * 2,
    )(blocks)


def _pair_sum(name, blocks, got, core):
    _, R, Wd = blocks.shape

    def body(c_ref, a_ref, b_ref, o_ref):
        o_ref[...] = (a_ref[...].astype(F32) + b_ref[...].astype(F32)).astype(o_ref.dtype)

    return pl.pallas_call(
        body, name=name,
        grid_spec=pltpu.PrefetchScalarGridSpec(
            num_scalar_prefetch=1, grid=(N_CHIP,),
            in_specs=[pl.BlockSpec((None, R, Wd), lambda q, c_ref: (2 * q + c_ref[0], 0, 0)),
                      pl.BlockSpec((None, R, Wd), lambda q, c_ref: (q, 0, 0))],
            out_specs=pl.BlockSpec((None, R, Wd), lambda q, c_ref: (q, 0, 0))),
        out_shape=jax.ShapeDtypeStruct(got.shape, blocks.dtype),
        compiler_params=pltpu.CompilerParams(dimension_semantics=("parallel",), vmem_limit_bytes=VMEM_LIMIT),
    )(core, blocks, got)


def _chip_copies(s_ref, land_ref, send_sems, recv_sems):
    x, y, c = lax.axis_index("x"), lax.axis_index("y"), lax.axis_index("c")
    my_q = 2 * x + y
    sends, recvs = [], []
    for kbits in range(1, N_CHIP):
        px = 1 - x if kbits & 2 else x
        py = 1 - y if kbits & 1 else y
        peer_q = 2 * px + py
        sends.append(pltpu.make_async_remote_copy(
            src_ref=s_ref.at[peer_q], dst_ref=land_ref.at[my_q], send_sem=send_sems[kbits - 1],
            recv_sem=recv_sems[kbits - 1], device_id=(px, py, c), device_id_type=pl.DeviceIdType.MESH))
        recvs.append(pltpu.make_async_remote_copy(
            src_ref=s_ref.at[peer_q], dst_ref=land_ref.at[peer_q], send_sem=send_sems[kbits - 1],
            recv_sem=recv_sems[kbits - 1], device_id=(px, py, c), device_id_type=pl.DeviceIdType.MESH))
    return sends, recvs


_HBM = pl.BlockSpec(memory_space=pltpu.HBM)
_SEM = pl.BlockSpec(memory_space=pltpu.SEMAPHORE)
N_CHIP_SEMS = 2 * (N_CHIP - 1)


def _chip_exchange_start(name, sums):
    def body(s_ref, land_ref, *outs):
        sems, token = outs[:N_CHIP_SEMS], outs[N_CHIP_SEMS + 2]
        sends, _ = _chip_copies(s_ref, land_ref, sems[:N_CHIP - 1], sems[N_CHIP - 1:])
        for cp in sends:
            cp.start()
        token[...] = jnp.zeros_like(token)

    res = pl.pallas_call(
        body, name=name, in_specs=(_HBM, _HBM),
        out_specs=(_SEM,) * N_CHIP_SEMS + (_HBM, _HBM, pl.BlockSpec(memory_space=pltpu.VMEM)),
        out_shape=(pltpu.SemaphoreType.DMA(()),) * N_CHIP_SEMS
        + (pltpu.HBM(sums.shape, sums.dtype), pltpu.HBM(sums.shape, sums.dtype), jax.ShapeDtypeStruct((8, LANES), F32)),
        input_output_aliases={0: N_CHIP_SEMS, 1: N_CHIP_SEMS + 1},
        compiler_params=pltpu.CompilerParams(has_side_effects=pltpu.SideEffectType.DATAFLOW_SIDE_EFFECTING),
    )(pltpu.with_memory_space_constraint(sums, pltpu.HBM),
      pltpu.with_memory_space_constraint(lax.empty(sums.shape, sums.dtype), pltpu.HBM))
    return res[:N_CHIP_SEMS], res[N_CHIP_SEMS], res[N_CHIP_SEMS + 1], res[N_CHIP_SEMS + 2]


def _chip_exchange_wait(name, sems, sums_thru, land_thru, after):
    def body(s_ref, land_ref, *rest):
        sems = rest[:N_CHIP_SEMS]
        sends, recvs = _chip_copies(s_ref, land_ref, sems[:N_CHIP - 1], sems[N_CHIP - 1:])
        for cp in sends:
            cp.wait_send()
        for cp in recvs:
            cp.wait_recv()

    return pl.pallas_call(
        body, name=name, in_specs=(_HBM, _HBM) + (_SEM,) * N_CHIP_SEMS + (pl.BlockSpec(memory_space=pl.ANY),),
        out_specs=(_HBM, _HBM),
        out_shape=(pltpu.HBM(sums_thru.shape, sums_thru.dtype), pltpu.HBM(sums_thru.shape, sums_thru.dtype)),
        input_output_aliases={0: 0, 1: 1},
        compiler_params=pltpu.CompilerParams(has_side_effects=pltpu.SideEffectType.DATAFLOW_SIDE_EFFECTING),
    )(sums_thru, land_thru, *sems, after)


def _adamw(name, slots, w, m, v, tr=256):
    unit_mid = w.ndim == 3 and w.shape[1] == 1 and w.shape[0] > 1
    R, Wd = (w.shape[0], w.shape[2]) if unit_mid else w.shape[-2:]
    depth_axis = w.ndim == 3 and not unit_mid
    if R % tr == 0:
        tc = Wd
    else:
        tr, tc = R, (256 if (Wd % 256 == 0 and R > 256) else Wd)
    at = (slice(None), 0, slice(None)) if unit_mid else Ellipsis

    def body(s_ref, w_ref, m_ref, v_ref, g_out, d_out, m_out, v_out):
        g = s_ref[0].astype(F32)
        for j in range(1, slots.shape[0]):
            g = g + s_ref[j].astype(F32)
        m_new = ADAM_B1 * m_ref[at] + (1.0 - ADAM_B1) * g
        v_new = ADAM_B2 * v_ref[at] + (1.0 - ADAM_B2) * jnp.square(g)
        m_hat = m_new / (1.0 - ADAM_B1 ** ADAM_STEP)
        v_hat = v_new / (1.0 - ADAM_B2 ** ADAM_STEP)
        g_out[at] = g
        d_out[at] = -ADAM_LR * (m_hat / (jnp.sqrt(v_hat) + ADAM_EPS) + ADAM_WD * w_ref[at])
        m_out[at] = m_new
        v_out[at] = v_new

    if unit_mid:
        row = pl.BlockSpec((tr, 1, tc), lambda i, j: (i, 0, j))
    elif depth_axis:
        row = pl.BlockSpec((None, tr, tc), lambda i, j: (0, i, j))
    else:
        row = pl.BlockSpec((tr, tc), lambda i, j: (i, j))
    return pl.pallas_call(
        body, name=name, grid=(R // tr, Wd // tc),
        in_specs=[pl.BlockSpec((slots.shape[0], tr, tc), lambda i, j: (0, i, j)), row, row, row],
        out_specs=[row] * 4, out_shape=[jax.ShapeDtypeStruct(w.shape, F32)] * 4,
        compiler_params=pltpu.CompilerParams(dimension_semantics=("parallel", "parallel"),
                                             vmem_limit_bytes=VMEM_LIMIT),
    )(slots, w, m, v)


PACK_W = 1024
PACKED = [(n, s) for n, s in REPLICATED if n != "sgu_w"]
_SMALL_SIZES = [int(np.prod(s)) for _, s in PACKED]
_SMALL_ROWS = _round_up(_round_up(sum(_SMALL_SIZES) + PACK_W, PACK_W) // PACK_W, 8)
_LOSS_AT = sum(_SMALL_SIZES)
W_IN_SHARD = P_TOTAL // N_DEV


def _pack_rows(parts, rows, dtype):
    flat = jnp.concatenate([p.reshape(-1).astype(dtype) for p in parts])
    return jnp.pad(flat, (0, rows * PACK_W - flat.shape[0])).reshape(rows, PACK_W)


def _w_in_groups_t(blocks):
    wt = blocks.reshape(P_TOTAL, D)
    o, c = 2 * D, 2 * D + 3 * D
    z = lambda r: jnp.zeros((r, D), wt.dtype)
    rw = jnp.concatenate([wt[o:c], wt[c:c + L_W], z(128 - L_W), wt[c + L_W:c + L_W + L_A], z(128 - L_A),
                          wt[c + L_W + L_A:o + C_B], z(256 - L_G)], axis=0)
    return wt[:o], rw, wt[o + C_B:]


def _w_in_grad_blocks(g_sgu_t, g_rw_t, g_gate_t):
    c = 3 * D
    full = jnp.concatenate([g_sgu_t, g_rw_t[:c], g_rw_t[c:c + L_W], g_rw_t[c + 128:c + 128 + L_A],
                            g_rw_t[c + 256:c + 256 + L_G], g_gate_t], axis=0)
    return full.reshape(N_DEV, W_IN_SHARD, D)


def _mesh_index():
    me = 4 * lax.axis_index("x") + 2 * lax.axis_index("y") + lax.axis_index("c")
    return me.astype(jnp.int32).reshape(1)


def _fill_slot(name, dst, src, idx, src_idx=None):
    R, Wd = dst.shape[1:]
    scalars = [idx] if src_idx is None else [idx, src_idx]
    if src_idx is None:
        src_spec = pl.BlockSpec((R, Wd), lambda i, *s: (0, 0))
    else:
        src_spec = pl.BlockSpec((None, R, Wd), lambda i, *s: (s[1][0], 0, 0))

    def body(*refs):
        src_ref, out_ref = refs[len(scalars) + 1], refs[len(scalars) + 2]
        out_ref[...] = src_ref[...]

    return pl.pallas_call(
        body, name=name,
        grid_spec=pltpu.PrefetchScalarGridSpec(
            num_scalar_prefetch=len(scalars), grid=(1,),
            in_specs=[pl.BlockSpec(memory_space=pl.ANY), src_spec],
            out_specs=pl.BlockSpec((None, R, Wd), lambda i, *s: (s[0][0], 0, 0))),
        out_shape=jax.ShapeDtypeStruct(dst.shape, dst.dtype),
        input_output_aliases={len(scalars): 0},
        compiler_params=pltpu.CompilerParams(vmem_limit_bytes=VMEM_LIMIT),
    )(*scalars, dst, src)


class TwoLevelGather:
    def __init__(self, bufs, skip_own=()):
        self.bufs, self.nb, self.skip_own = list(bufs), len(bufs), tuple(skip_own)
        self.any_specs = [pl.BlockSpec(memory_space=pl.ANY)] * self.nb
        self.out_shape = [jax.ShapeDtypeStruct((N_DEV,) + b.shape, b.dtype) for b in self.bufs]
        self.sem_shapes = [pltpu.SemaphoreType.DMA((7 * self.nb,)), pltpu.SemaphoreType.DMA((7 * self.nb,)),
                           pltpu.SemaphoreType.DMA((self.nb,))]

    def _copies(self, in_refs, out_refs, sems):
        send_sems, recv_sems, local_sems = sems
        nb = self.nb
        x, y, c = lax.axis_index("x"), lax.axis_index("y"), lax.axis_index("c")
        me, sibling = (x, y, c), (x, y, 1 - c)
        chips = [(1 - x, y), (x, 1 - y), (1 - x, 1 - y)]

        def slot(b, dev):
            return out_refs[b].at[4 * dev[0] + 2 * dev[1] + dev[2]]

        def copy(b, k, block, to, own=False):
            return pltpu.make_async_remote_copy(
                src_ref=in_refs[b] if own else slot(b, block), dst_ref=slot(b, block),
                send_sem=send_sems.at[7 * b + k], recv_sem=recv_sems.at[7 * b + k], device_id=to,
                device_id_type=pl.DeviceIdType.MESH)

        cp = {}
        cp["local"] = [pltpu.make_async_copy(in_refs[b], slot(b, me), local_sems.at[b]) for b in range(nb)
                       if b not in self.skip_own]
        cp["first"] = [copy(b, 0, me, sibling, own=True) for b in range(nb)]
        cp["first"] += [copy(b, 1 + j, me, (*chip, c), own=True) for j, chip in enumerate(chips) for b in range(nb)]
        cp["over_ici"] = [copy(b, 1 + j, (*chip, c), me) for j, chip in enumerate(chips) for b in range(nb)]
        cp["passed"] = [copy(b, 4 + j, (*chip, c), sibling) for j, chip in enumerate(chips) for b in range(nb)]
        cp["from_sibling"] = [copy(b, 0, sibling, me) for b in range(nb)]
        cp["from_sibling"] += [copy(b, 4 + j, (*chip, 1 - c), me) for j, chip in enumerate(chips) for b in range(nb)]
        return cp

    def start(self, in_refs, out_refs, sems):
        cp = self._copies(in_refs, out_refs, sems)
        for c in cp["first"] + cp["local"]:
            c.start()

    def forward(self, in_refs, out_refs, sems):
        cp = self._copies(in_refs, out_refs, sems)
        for arrived, onward in zip(cp["over_ici"], cp["passed"]):
            arrived.wait_recv()
            onward.start()

    def finish(self, in_refs, out_refs, sems):
        cp = self._copies(in_refs, out_refs, sems)
        for c in cp["from_sibling"]:
            c.wait_recv()
        for c in cp["first"] + cp["passed"]:
            c.wait_send()
        for c in cp["local"]:
            c.wait()

    def schedule(self, n_steps):
        return [(0, self.start), (max(n_steps - 3, 0), self.forward), (n_steps - 1, self.finish)]


def _all_gather_two_level(name, bufs, skip_own=()):
    ex = TwoLevelGather(bufs, skip_own)

    def body(*refs):
        args = refs[:ex.nb], refs[ex.nb:2 * ex.nb], refs[2 * ex.nb:]
        ex.start(*args)
        ex.forward(*args)
        ex.finish(*args)

    return pl.pallas_call(body, name=name, in_specs=ex.any_specs, out_specs=ex.any_specs, out_shape=ex.out_shape,
                          scratch_shapes=ex.sem_shapes)(*ex.bufs)


def _cols_from_blocks(blk):
    return jnp.transpose(blk, (1, 0, 2)).reshape(blk.shape[1], -1)


def _cols_to_blocks(g):
    r, c = g.shape
    return jnp.transpose(g.reshape(r, N_DEV, c // N_DEV), (1, 0, 2))


FIRST_WEIGHTS = ["w_in", "shift_b", "w_lora_w", "a_lora_w", "g_lora_w"]
LATE_WEIGHTS = ["w_proj_a", "w_proj_b", "w_out", "w_ffn1", "w_ffn2"]


def _late_weights(shards):
    ex = TwoLevelGather([shards[n].astype(BF16) for n in LATE_WEIGHTS])

    def finish(results):
        got = dict(zip(LATE_WEIGHTS, results))
        W = {n: got[n].reshape(-1, D) for n in ("w_proj_a", "w_proj_b", "w_out", "w_ffn2")}
        W["w_ffn1"] = got["w_ffn1"].reshape(N_DEV, D, -1)
        return W
    return ex, finish


def _gather_weights(shards):
    def payload(n):
        if n == "w_in":
            return jnp.transpose(shards[n][0]).astype(BF16)
        return shards[n] if n == "shift_b" else shards[n].astype(BF16)
    payloads = [payload(n) for n in FIRST_WEIGHTS]
    got = dict(zip(FIRST_WEIGHTS, _all_gather_two_level("weight_all_gather", payloads, skip_own=(0,))))
    got["w_in"] = _fill_slot("w_in_own_slot", got["w_in"], payloads[0], _mesh_index())
    W = {}
    W["w_sgu_t"], W["w_rw_t"], W["w_gate_t"] = _w_in_groups_t(got["w_in"])
    z = lambda r, c, dt: jnp.zeros((r, c), dt)
    W["w_lora"] = jnp.concatenate([_cols_from_blocks(got["w_lora_w"][:, 0]).astype(F32), z(128 - L_W, D, F32)], axis=0)
    W["a_lora"] = jnp.concatenate([_cols_from_blocks(got["a_lora_w"][:, 0]).astype(F32), z(128 - L_A, D, F32)], axis=0)
    W["g_lora"] = jnp.concatenate([_cols_from_blocks(got["g_lora_w"][:, 0]).astype(F32), z(256 - L_G, D, F32)], axis=0)
    sb = _cols_from_blocks(got["shift_b"][:, 0])
    W["sb"] = jnp.concatenate([sb[:, :3 * D], sb[:, 3 * D:3 * D + L_W], z(2, 128 - L_W, F32),
                               sb[:, 3 * D + L_W:3 * D + L_W + L_A], z(2, 128 - L_A, F32),
                               sb[:, 3 * D + L_W + L_A:], z(2, 256 - L_G, F32)], axis=1)
    return W


def _replicated_weights(rep):
    W = {n: rep[n] for n in ("g_mix", "sgu_ln_w", "sgu_ln_b", "w0", "a0", "k_k", "k_a", "r_k", "ln_x_w", "ln_x_b",
                             "g_ffn")}
    W["g_final"] = rep["g_final"].reshape(1, D)
    W["sgu_w"] = rep["sgu_w"][0]
    W["sgu_bt"] = jnp.transpose(rep["sgu_b"][0])
    return W


def _late_grad_blocks(G):
    blocks = {n: G[n].reshape(N_DEV, -1, D) for n in ("w_proj_a", "w_proj_b", "w_out", "w_ffn2")}
    blocks["w_ffn1"] = G["w_ffn1"]
    return Exchange([blocks[n] for n in LATE_WEIGHTS] + [G["sgu_w"].reshape(SGU_G * SGU_C, SGU_C).astype(GRAD_PAYLOAD)],
                    [False] * len(LATE_WEIGHTS) + [True])


def _first_grad_blocks(G):
    sbg = G["sb"]
    c = 3 * D
    sb = jnp.concatenate([sbg[:, :c], sbg[:, c:c + L_W], sbg[:, c + 128:c + 128 + L_A],
                          sbg[:, c + 256:c + 256 + L_G]], axis=1)
    return {
        "shift_b": _cols_to_blocks(sb),
        "w_lora_w": _cols_to_blocks(G["w_lora"][:L_W]), "a_lora_w": _cols_to_blocks(G["a_lora"][:L_A]),
        "g_lora_w": _cols_to_blocks(G["g_lora"][:L_G]),
    }


def _replicated_grads(G):
    small = {n: G[n] for n in ("g_mix", "sgu_ln_w", "sgu_ln_b", "w0", "a0", "k_k", "k_a", "r_k", "ln_x_w", "ln_x_b",
                               "g_ffn", "g_final")}
    small["sgu_w"] = G["sgu_w"]
    small["sgu_b"] = jnp.transpose(G["sgu_bt"])
    return small


def kernel(x, g_mix, w_in, sgu_ln_w, sgu_ln_b, sgu_w, sgu_b, w_proj_a, shift_b, w_lora_w, w0, a_lora_w, a0, g_lora_w, k_k, k_a, r_k, ln_x_w, ln_x_b, w_proj_b, w_out, g_ffn, w_ffn1, w_ffn2, g_final, loss_target, m_g_mix, m_w_in, m_sgu_ln_w, m_sgu_ln_b, m_sgu_w, m_sgu_b, m_w_proj_a, m_shift_b, m_w_lora_w, m_w0, m_a_lora_w, m_a0, m_g_lora_w, m_k_k, m_k_a, m_r_k, m_ln_x_w, m_ln_x_b, m_w_proj_b, m_w_out, m_g_ffn, m_w_ffn1, m_w_ffn2, m_g_final, v_g_mix, v_w_in, v_sgu_ln_w, v_sgu_ln_b, v_sgu_w, v_sgu_b, v_w_proj_a, v_shift_b, v_w_lora_w, v_w0, v_a_lora_w, v_a0, v_g_lora_w, v_k_k, v_k_a, v_r_k, v_ln_x_w, v_ln_x_b, v_w_proj_b, v_w_out, v_g_ffn, v_w_ffn1, v_w_ffn2, v_g_final):
    env = dict(locals())
    weights = {n: env[n] for n in WEIGHT_ORDER}
    moms = {n: env["m_" + n] for n in WEIGHT_ORDER}
    vars_ = {n: env["v_" + n] for n in WEIGHT_ORDER}

    shards = {n: weights[n] for n, _, _ in SHARDED}
    W = _gather_weights(shards)
    W.update(_replicated_weights({n: weights[n] for n, _ in REPLICATED}))
    in_flight = {}

    def send_w_in_grads(G):
        blocks = _w_in_grad_blocks(G["w_sgu_t"], G["w_rw_t"], G["w_gate_t"])
        got = _pair_exchange("grad_pair_exchange", blocks)
        core = lax.axis_index("c").astype(jnp.int32).reshape(1)
        sums = _pair_sum("grad_pair_sum", blocks, got, core)
        in_flight["sems"], in_flight["sums"], in_flight["land"], token = _chip_exchange_start("grad_chip_start", sums)
        return token

    loss_part, dx, G, late_slots = _local_step(x[0], loss_target[0], W, late_weights=_late_weights(shards),
                                               early_grads=_late_grad_blocks, w_in_grads_ready=send_w_in_grads)

    slots = dict(zip(LATE_WEIGHTS, late_slots))
    blocks = _first_grad_blocks(G)
    small = _replicated_grads(G)
    small_parts = [small[n] for n, _ in PACKED] + [jnp.full((PACK_W,), loss_part, F32)]
    rest = [n for n in FIRST_WEIGHTS if n != "w_in"]
    res = _exchange("grad_exchange", [blocks[n] for n in rest] + [_pack_rows(small_parts, _SMALL_ROWS, F32)],
                    [False] * len(rest) + [True])
    slots.update(zip(rest, res[:-1]))
    small_slots = res[-1]
    sums, chip_slots = _chip_exchange_wait("grad_chip_wait", in_flight["sems"], in_flight["sums"], in_flight["land"],
                                           after=small_slots)
    my_chip = (2 * lax.axis_index("x") + lax.axis_index("y")).astype(jnp.int32).reshape(1)
    slots["w_in"] = _fill_slot("grad_own_slot", chip_slots, sums, my_chip, src_idx=my_chip)

    outs = [dict(), dict(), dict(), dict()]
    for n, _, _ in SHARDED:
        if n == "w_in":
            res = _adamw("adamw_" + n, slots[n], *[jnp.transpose(t, (2, 0, 1)) for t in (weights[n], moms[n], vars_[n])])
            res = [jnp.transpose(t, (1, 2, 0)) for t in res]
        else:
            res = _adamw("adamw_" + n, slots[n], weights[n], moms[n], vars_[n])
        for k in range(4):
            outs[k][n] = res[k]

    sgu_shape = (SGU_G * SGU_C, SGU_C)
    res = _adamw("adamw_sgu_w", late_slots[len(LATE_WEIGHTS)], *[t.reshape(sgu_shape) for t in
                                                                  (weights["sgu_w"], moms["sgu_w"], vars_["sgu_w"])])
    for k in range(4):
        outs[k]["sgu_w"] = res[k].reshape(weights["sgu_w"].shape)

    def packed(d):
        return _pack_rows([d[n] for n, _ in PACKED], _SMALL_ROWS, F32)
    small_out = _adamw("adamw_replicated", small_slots, packed(weights), packed(moms), packed(vars_))
    for k in range(4):
        flat = small_out[k].reshape(-1)
        off = 0
        for (n, s), size in zip(PACKED, _SMALL_SIZES):
            outs[k][n] = flat[off:off + size].reshape(s)
            off += size
    loss = small_out[0].reshape(-1)[_LOSS_AT]
    return (loss, dx[None], *[outs[0][n] for n in WEIGHT_ORDER], *[outs[1][n] for n in WEIGHT_ORDER],
            *[outs[2][n] for n in WEIGHT_ORDER], *[outs[3][n] for n in WEIGHT_ORDER])
```

```python
import functools
import numpy as np
import jax
import jax.numpy as jnp
from jax import lax
from jax.experimental import pallas as pl
from jax.experimental.pallas import tpu as pltpu

F32 = jnp.float32
BF16 = jnp.bfloat16

D = 1024
NH, HN = 16, 64
SGU_G, SGU_C = 8, 128
L_W, L_A, L_G = 64, 64, 160
C_B = 3 * D + L_W + L_A + L_G
P_TOTAL = 2 * D + C_B + 2 * D
D_FF = 4 * D
RW_INT = 3 * D + 128 + 128 + 256
NORM_EPS, LN_EPS, GN_EPS = 1e-6, 1e-5, 64e-5
N_DEV = 8
LANES = 128
SCAN_C = 64
SOLVE_B = 16
SCAN_PRECISION = lax.Precision.HIGH
SCAN_OUT_PRECISION = lax.Precision.DEFAULT
GRAD_PAYLOAD = BF16
VMEM_LIMIT = 56 * 1024 * 1024
MATMUL_VMEM_BUDGET = 40 * 1024 * 1024
STEP_COST_BYTES = 512 * 1024
HBM_COST_RATIO = 3

ADAM_LR, ADAM_B1, ADAM_B2, ADAM_EPS, ADAM_WD, ADAM_STEP = 0.001, 0.9, 0.999, 1e-08, 0.01, 10

SHARDED = [
    ("w_in", (D, P_TOTAL), 1), ("w_proj_a", (D, D), 0), ("shift_b", (2, C_B), 1), ("w_lora_w", (L_W, D), 1),
    ("a_lora_w", (L_A, D), 1), ("g_lora_w", (L_G, D), 1), ("w_proj_b", (D, D), 0), ("w_out", (D, D), 0),
    ("w_ffn1", (D, D_FF), 1), ("w_ffn2", (D_FF, D), 0),
]
REPLICATED = [
    ("g_mix", (1, D)), ("sgu_ln_w", (1, D)), ("sgu_ln_b", (1, D)), ("sgu_w", (1, SGU_G, SGU_C, SGU_C)),
    ("sgu_b", (1, SGU_G, SGU_C)), ("w0", (1, D)), ("a0", (1, D)), ("k_k", (1, D)), ("k_a", (1, D)), ("r_k", (1, D)),
    ("ln_x_w", (1, D)), ("ln_x_b", (1, D)), ("g_ffn", (1, D)), ("g_final", (D,)),
]
WEIGHT_ORDER = ["g_mix", "w_in", "sgu_ln_w", "sgu_ln_b", "sgu_w", "sgu_b", "w_proj_a", "shift_b", "w_lora_w", "w0",
                "a_lora_w", "a0", "g_lora_w", "k_k", "k_a", "r_k", "ln_x_w", "ln_x_b", "w_proj_b", "w_out", "g_ffn",
                "w_ffn1", "w_ffn2", "g_final"]


def _shard_shape(shape, axis):
    s = list(shape)
    s[axis] //= N_DEV
    return tuple(s)


def _round_up(n, m):
    return (n + m - 1) // m * m


def _pick(n, target):
    if n <= target:
        return n
    best = None
    for t in range(LANES, target + 1, LANES):
        if n % t == 0:
            best = t
    assert best is not None, (n, target)
    return best


def _matmul(name, a, b, mode, out_dtype=F32, tm=2048, tn=1024, tk=2048, out_blocks=None, epilogue=None, extras=(),
            out_dtypes=(), after=None):
    b_blocks = b.shape[0] if b.ndim == 3 else None
    bshape = b.shape if b.ndim == 2 else (b.shape[1], b.shape[0] * b.shape[2])
    if mode == "nn":
        (M, K), (K2, N) = a.shape, bshape
    elif mode == "nt":
        (M, K), (N, K2) = a.shape, bshape
    else:
        (K, M), (K2, N) = a.shape, bshape
    assert K == K2, (name, a.shape, b.shape)
    assert b_blocks is None or mode != "tn"
    assert out_blocks is None or mode == "tn"
    tn = min(tn, N // (out_blocks or 1), bshape[1] // b_blocks if (b_blocks and mode == "nn") else tn)
    tk = min(tk, bshape[1] // b_blocks if (b_blocks and mode == "nt") else tk)
    tm, tn, tk = _pick(M, tm), _pick(N, tn), _pick(K, tk)

    def vmem_bytes(tm, tk):
        tiles = tm * tk * a.dtype.itemsize + tk * tn * b.dtype.itemsize
        for dt in (out_dtypes if epilogue else (out_dtype,)):
            tiles += tm * tn * jnp.dtype(dt).itemsize
        for x in extras:
            arr = x[0] if isinstance(x, tuple) else x
            tiles += (tm if arr.shape[0] > 1 else 1) * tn * arr.dtype.itemsize
        return 2 * tiles + (tm * tn * 4 if K // tk > 1 else 0)

    def cost(tm, tk):
        ni, nj, nk = M // tm, N // tn, K // tk
        steps = ni * nj * nk
        acc_passes = steps * tm * tn * 8 if nk > 1 else 0
        a_reads = M * K * a.dtype.itemsize * (nj if nk > 1 else 1)
        b_reads = K * N * b.dtype.itemsize * (ni if (nj > 1 or nk > 1) else 1)
        return (steps * STEP_COST_BYTES + acc_passes + vmem_bytes(tm, tk) // 2
                + HBM_COST_RATIO * (a_reads + b_reads))

    options = [(m, k) for m in {_pick(M, max(t, LANES)) for t in (tm, tm // 2, tm // 4)}
               for k in {_pick(K, max(t, LANES)) for t in (tk, tk // 2, tk // 4)}
               if vmem_bytes(m, k) <= MATMUL_VMEM_BUDGET]
    tm, tk = min(options, key=lambda o: cost(*o))
    nk = K // tk
    dims = {"nn": (((1,), (0,)), ((), ())), "nt": (((1,), (1,)), ((), ())), "tn": (((0,), (0,)), ((), ()))}[mode]

    n_x, n_o = len(extras), len(out_dtypes) if epilogue else 1
    n_after = 0 if after is None else 1

    def body(a_ref, b_ref, *rest):
        x_refs, o_refs, acc = rest[:n_x], rest[n_x + n_after:n_x + n_after + n_o], rest[n_x + n_after + n_o:]
        part = lax.dot_general(a_ref[...].astype(BF16), b_ref[...].astype(BF16), dims, preferred_element_type=F32)

        def finish(res):
            outs = epilogue(res, *[r[...] for r in x_refs]) if epilogue else (res,)
            for r, v in zip(o_refs, outs):
                r[...] = v.astype(r.dtype)

        if nk == 1:
            finish(part)
            return
        acc_ref, k = acc[0], pl.program_id(2)

        @pl.when(k == 0)
        def _():
            acc_ref[...] = part

        @pl.when(k > 0)
        def _():
            acc_ref[...] += part

        @pl.when(k == nk - 1)
        def _():
            finish(acc_ref[...])

    a_spec = {"nn": pl.BlockSpec((tm, tk), lambda i, j, k: (i, k)), "nt": pl.BlockSpec((tm, tk), lambda i, j, k: (i, k)),
              "tn": pl.BlockSpec((tk, tm), lambda i, j, k: (k, i))}[mode]
    b_spec = {"nn": pl.BlockSpec((tk, tn), lambda i, j, k: (k, j)), "nt": pl.BlockSpec((tn, tk), lambda i, j, k: (j, k)),
              "tn": pl.BlockSpec((tk, tn), lambda i, j, k: (k, j))}[mode]
    if b_blocks and mode == "nn":
        per = b.shape[2] // tn
        b_spec = pl.BlockSpec((None, tk, tn), lambda i, j, k: (j // per, k, j % per))
    elif b_blocks:
        per = b.shape[2] // tk
        b_spec = pl.BlockSpec((None, tn, tk), lambda i, j, k: (k // per, j, k % per))
    out_spec = pl.BlockSpec((tm, tn), lambda i, j, k: (i, j))
    out_shape = jax.ShapeDtypeStruct((M, N), out_dtype)
    if out_blocks:
        per_o = N // out_blocks // tn
        out_spec = pl.BlockSpec((None, tm, tn), lambda i, j, k: (j // per_o, i, j % per_o))
        out_shape = jax.ShapeDtypeStruct((out_blocks, M, N // out_blocks), out_dtype)
    x_specs, x_args = [], []
    for x in extras:
        arr, off = x if isinstance(x, tuple) else (x, 0)
        if arr.shape[0] == 1:
            x_specs.append(pl.BlockSpec((1, tn), lambda i, j, k: (0, j)))
        else:
            x_specs.append(pl.BlockSpec((tm, tn), lambda i, j, k, off=off: (i, j + off)))
        x_args.append(arr)
    res = pl.pallas_call(
        body, name=name, grid=(M // tm, N // tn, nk),
        in_specs=[a_spec, b_spec] + x_specs + [pl.BlockSpec(memory_space=pl.ANY)] * n_after,
        out_specs=[out_spec] * n_o if epilogue else out_spec,
        out_shape=[jax.ShapeDtypeStruct((M, N), dt) for dt in out_dtypes] if epilogue else out_shape,
        scratch_shapes=[pltpu.VMEM((tm, tn), F32)] if nk > 1 else [],
        compiler_params=pltpu.CompilerParams(dimension_semantics=("parallel", "parallel", "arbitrary"),
                                             vmem_limit_bytes=VMEM_LIMIT),
    )(a, b, *x_args, *([after] if n_after else []))
    return res


class Rows:
    def __init__(self, arr, width=None, cb=0):
        self.arr, self.width, self.cb = arr, (arr.shape[1] if width is None else width), cb


class Heads:
    def __init__(self, arr):
        self.arr = arr


class Halo:
    def __init__(self, arr, side):
        self.arr, self.side = arr, side


def _rows_call(name, fn, ins, consts, outs, accs=(), tm=256, with_pid=False):
    T = next(o.arr.shape[1] if isinstance(o, Heads) else o.arr.shape[0] for o in ins if not isinstance(o, Halo))
    tm = min(tm, T)
    n_tiles = T // tm
    n_in, n_c, n_out = len(ins), len(consts), len(outs)
    in_specs, args = [], []
    for o in ins:
        if isinstance(o, Rows):
            in_specs.append(pl.BlockSpec((tm, o.width), lambda i, cb=o.cb: (i, cb)))
        elif isinstance(o, Heads):
            in_specs.append(pl.BlockSpec((NH, tm, HN), lambda i: (0, i, 0)))
        else:
            w = o.arr.shape[1]
            if o.side < 0:
                in_specs.append(pl.BlockSpec((8, w), lambda i: (jnp.maximum(i * (tm // 8) - 1, 0), 0)))
            else:
                in_specs.append(pl.BlockSpec((8, w), lambda i: (jnp.minimum((i + 1) * (tm // 8), T // 8 - 1), 0)))
        args.append(o.arr)
    for c in consts:
        in_specs.append(pl.BlockSpec(c.shape, lambda i, nd=c.ndim: (0,) * nd))
        args.append(c)
    out_specs, out_shape = [], []
    for o in outs:
        if o[0] == "rows":
            out_specs.append(pl.BlockSpec((tm, o[1]), lambda i: (i, 0)))
            out_shape.append(jax.ShapeDtypeStruct((T, o[1]), o[2]))
        else:
            out_specs.append(pl.BlockSpec((NH, tm, HN), lambda i: (0, i, 0)))
            out_shape.append(jax.ShapeDtypeStruct((NH, T, HN), o[1]))
    for shape, dt in accs:
        out_specs.append(pl.BlockSpec(shape, lambda i, nd=len(shape): (0,) * nd))
        out_shape.append(jax.ShapeDtypeStruct(shape, dt))

    def body(*refs):
        i = pl.program_id(0)
        vals = []
        vals = [r[...] for r in refs[:n_in + n_c]]
        res = fn(i, n_tiles, *vals) if with_pid else fn(*vals)
        out_refs = refs[n_in + n_c:]
        for r, v in zip(out_refs[:n_out], res[:n_out]):
            r[...] = v.astype(r.dtype)
        if accs:
            @pl.when(i == 0)
            def _():
                for r in out_refs[n_out:]:
                    r[...] = jnp.zeros_like(r)

            for r, v in zip(out_refs[n_out:], res[n_out:]):
                r[...] += v.astype(r.dtype)

    res = pl.pallas_call(
        body, name=name, grid=(n_tiles,), in_specs=in_specs, out_specs=out_specs, out_shape=out_shape,
        compiler_params=pltpu.CompilerParams(dimension_semantics=("arbitrary",), vmem_limit_bytes=VMEM_LIMIT),
    )(*args)
    return res


def _rms(x, g):
    return x * lax.rsqrt(jnp.mean(x * x, axis=-1, keepdims=True) + NORM_EPS) * g


def _gelu(x):
    return 0.5 * x * (1.0 + lax.erf(x * 0.7071067811865476))


def _sigmoid(x):
    return 1.0 / (1.0 + jnp.exp(-x))


def _bdot(a, b):
    return jnp.dot(a.astype(BF16), b.astype(BF16), preferred_element_type=F32)


def _to_heads(x):
    return jnp.concatenate([x[:, h * HN:(h + 1) * HN][None] for h in range(NH)], axis=0)


def _from_heads(xh):
    return jnp.concatenate([xh[h] for h in range(NH)], axis=-1)


def _sgu_fn(p, ln_w, ln_b, sw, sbt):
    z = _gelu(p)
    u, v = z[:, :D], z[:, D:]
    mu = jnp.mean(v, axis=-1, keepdims=True)
    var = jnp.mean(jnp.square(v - mu), axis=-1, keepdims=True)
    vn = (v - mu) * lax.rsqrt(var + LN_EPS) * ln_w + ln_b
    ri = lax.broadcasted_iota(jnp.int32, (SGU_C, SGU_C), 0)
    ci = lax.broadcasted_iota(jnp.int32, (SGU_C, SGU_C), 1)
    mask = (ci <= ri).astype(F32)
    dg = D // SGU_G
    parts = []
    for g in range(SGU_G):
        parts.append(_bdot(sw[g] * mask, vn[:, g * dg:(g + 1) * dg]) + sbt[:, g:g + 1])
    return u * jnp.concatenate(parts, axis=-1)


def _pre_fn(qr, qk, qv, qxw, qxa, qxg, wl, w0, al, a0, gl, k_k, k_a):
    w = -jax.nn.softplus(-(w0 + _bdot(jnp.tanh(qxw), wl))) - 0.5
    lw = -jnp.exp(w)
    aa = _sigmoid(a0 + _bdot(qxa, al))
    g = _bdot(_sigmoid(qxg), gl)
    kk = _to_heads(qk * k_k)
    kk = kk / jnp.maximum(jnp.sqrt(jnp.sum(kk * kk, axis=-1, keepdims=True)), 1e-12)
    k2 = qk * (1.0 + (aa - 1.0) * k_a)
    return _to_heads(qr), _to_heads(lw), _to_heads(k2), _to_heads(qv), kk, _to_heads(aa), g


def _post_fn(o, r, k2, v, g, ln_w, ln_b, r_k):
    mu = jnp.mean(o, axis=-1, keepdims=True)
    d = o - mu
    var = jnp.mean(d * d, axis=-1, keepdims=True)
    on = d * lax.rsqrt(var + GN_EPS) * ln_w + ln_b
    bonus = jnp.sum(r * k2 * r_k, axis=-1, keepdims=True) * v
    return _from_heads(on + bonus) * g


def _gate_fn(pg, ya, yb):
    return _sigmoid(pg[:, :D]) * ya + _sigmoid(pg[:, D:]) * yb


def _bmm(x, y, cx, cy, out_path=False):
    return lax.dot_general(x, y, (((cx,), (cy,)), ((0,), (0,))),
                           precision=SCAN_OUT_PRECISION if out_path else SCAN_PRECISION, preferred_element_type=F32)


def _unit_lower_inverse(M):
    C = M.shape[1]
    ti = lax.broadcasted_iota(jnp.int32, (C, C), 0)
    tj = lax.broadcasted_iota(jnp.int32, (C, C), 1)
    eye = (ti == tj).astype(F32)
    same = lambda b: (ti // b == tj // b).astype(F32)
    X = -(M * same(SOLVE_B))
    inv = eye + X
    span = 1
    while 2 * span < SOLVE_B:
        X = _bmm(X, X, 2, 1)
        inv = inv + _bmm(inv, X, 2, 1)
        span *= 2
    b = SOLVE_B
    while b < C:
        low = M * (same(2 * b) - same(b))
        inv = inv - _bmm(_bmm(inv, low, 2, 1), inv, 2, 1)
        b *= 2
    return inv


@jax.custom_vjp
def _unit_lower_solve(inv, M, y):
    return _bmm(inv, y, 2, 1)


def _unit_lower_solve_fwd(inv, M, y):
    u = _bmm(inv, y, 2, 1)
    return u, (inv, u)


def _unit_lower_solve_bwd(res, du):
    inv, u = res
    dy = _bmm(inv, du, 1, 1)
    return jnp.zeros_like(inv), -_bmm(dy, u, 2, 2), dy


_unit_lower_solve.defvjp(_unit_lower_solve_fwd, _unit_lower_solve_bwd)


def _sum_over_time(x, reverse):
    C = x.shape[1]
    ti = lax.broadcasted_iota(jnp.int32, (C, C), 0)
    tj = lax.broadcasted_iota(jnp.int32, (C, C), 1)
    ones = jnp.broadcast_to(((tj >= ti) if reverse else (tj <= ti)).astype(BF16), (x.shape[0], C, C))
    hi = x.astype(BF16)
    r1 = x - hi.astype(F32)
    mid = r1.astype(BF16)
    lo = (r1 - mid.astype(F32)).astype(BF16)
    dn = (((2,), (1,)), ((0,), (0,)))
    return sum(lax.dot_general(ones, p, dn, preferred_element_type=F32) for p in (lo, mid, hi))


@jax.custom_vjp
def _time_cumsum(lw):
    return _sum_over_time(lw, reverse=False)


_time_cumsum.defvjp(lambda lw: (_sum_over_time(lw, reverse=False), None),
                    lambda _, d: (_sum_over_time(d, reverse=True),))


def _chunk_fn(S0, r, lw, k, v, kk, a, inv=None):
    C = SCAN_C
    bmm = _bmm
    ti = lax.broadcasted_iota(jnp.int32, (C, C), 0)
    tj = lax.broadcasted_iota(jnp.int32, (C, C), 1)
    incl = (tj <= ti).astype(F32)
    strict = (tj < ti).astype(F32)
    cum = _time_cumsum(lw)
    g_in, g_ex, g_inv = jnp.exp(cum), jnp.exp(cum - lw), jnp.exp(-cum)
    kkt, rt = kk * g_ex, r * g_in
    bk = jnp.concatenate([kk * a * g_inv, k * g_inv], axis=1)
    A = bmm(kkt, bk, 2, 2)
    M = A[:, :, :C] * strict
    n_mask = jnp.concatenate([jnp.zeros((C, C), F32), strict], axis=1)
    zv = jnp.concatenate([jnp.zeros_like(v), v], axis=1)
    s0_side = bmm(jnp.concatenate([kkt, rt], axis=1), S0, 2, 2, out_path=True)
    if inv is None:
        inv = lax.stop_gradient(_unit_lower_inverse(M))
    y = _unit_lower_solve(inv, M, s0_side[:, :C] + bmm(A * n_mask, zv, 2, 1, out_path=True))
    z = jnp.concatenate([-y, v], axis=1)
    attn = bmm(rt, bk, 2, 2) * jnp.concatenate([incl, incl], axis=1)
    O = s0_side[:, C:] + bmm(attn, z, 2, 1, out_path=True)
    S1 = (S0 + bmm(z, bk, 1, 1, out_path=True)) * g_in[:, C - 1:C, :]
    return O, S1, inv


def _scan_fwd(r, lw, k, v, kk, a, ex=None, tb=256):
    assert SCAN_C == HN
    T = r.shape[1]
    tb = min(tb, T)
    n_chunks = tb // SCAN_C
    nb = T // tb
    nx = ex.nb if ex else 0

    def body(*refs):
        r_ref, lw_ref, k_ref, v_ref, kk_ref, a_ref = refs[:6]
        x_in, (o_ref, s0_ref), x_out = refs[6:6 + nx], refs[6 + nx:8 + nx], refs[8 + nx:8 + 2 * nx]
        s_ref, sems = refs[8 + 2 * nx], refs[9 + 2 * nx:]

        plan = ex.schedule(nb) if ex else []

        @pl.when(pl.program_id(0) == 0)
        def _():
            s_ref[...] = jnp.zeros_like(s_ref)
            for at, action in plan[:1]:
                action(x_in, x_out, sems)

        def step(c, carry):
            sl = pl.ds(pl.multiple_of(c * SCAN_C, SCAN_C), SCAN_C)
            S0 = s_ref[...]
            O, S1, inv = _chunk_fn(S0, r_ref[:, sl, :], lw_ref[:, sl, :], k_ref[:, sl, :], v_ref[:, sl, :],
                                   kk_ref[:, sl, :], a_ref[:, sl, :])
            o_ref[:, sl, :] = O
            s0_ref[c, 0] = S0
            s0_ref[c, 1] = inv
            s_ref[...] = S1
            return carry

        lax.fori_loop(0, n_chunks, step, 0)

        for at, action in plan[1:]:
            pl.when(pl.program_id(0) == at)(functools.partial(action, x_in, x_out, sems))

    hm = pl.BlockSpec((NH, tb, HN), lambda i: (0, i, 0))
    res = pl.pallas_call(
        body, name="rwkv_scan_fwd", grid=(nb,), in_specs=[hm] * 6 + (ex.any_specs if ex else []),
        out_specs=[hm, pl.BlockSpec((n_chunks, 2, NH, HN, HN), lambda i: (i, 0, 0, 0, 0))]
        + (ex.any_specs if ex else []),
        out_shape=[jax.ShapeDtypeStruct((NH, T, HN), F32), jax.ShapeDtypeStruct((T // SCAN_C, 2, NH, HN, HN), F32)]
        + (ex.out_shape if ex else []),
        scratch_shapes=[pltpu.VMEM((NH, HN, HN), F32)] + (ex.sem_shapes if ex else []),
        compiler_params=pltpu.CompilerParams(dimension_semantics=("arbitrary",), vmem_limit_bytes=VMEM_LIMIT),
    )(r, lw, k, v, kk, a, *(ex.bufs if ex else []))
    return res[0], res[1], list(res[2:])


def _scan_bwd(r, lw, k, v, kk, a, s0s, do, ex=None, tb=128):
    T = r.shape[1]
    tb = min(tb, T)
    n_chunks = tb // SCAN_C
    nb = T // tb
    nx = ex.nb if ex else 0

    def body(*refs):
        r_ref, lw_ref, k_ref, v_ref, kk_ref, a_ref, s0_ref, do_ref = refs[:8]
        x_in, (dr, dlw, dk, dv, dkk, da), x_out = refs[8:8 + nx], refs[8 + nx:14 + nx], refs[14 + nx:14 + 2 * nx]
        ds_ref, sems = refs[14 + 2 * nx], refs[15 + 2 * nx:]

        plan = ex.schedule(nb) if ex else []

        @pl.when(pl.program_id(0) == 0)
        def _():
            ds_ref[...] = jnp.zeros_like(ds_ref)
            for at, action in plan[:1]:
                action(x_in, x_out, sems)

        def step(j, carry):
            c = n_chunks - 1 - j
            sl = pl.ds(pl.multiple_of(c * SCAN_C, SCAN_C), SCAN_C)
            inv = s0_ref[c, 1]
            _, vjp = jax.vjp(lambda *t: _chunk_fn(*t, inv=inv)[:2], s0_ref[c, 0], r_ref[:, sl, :], lw_ref[:, sl, :],
                             k_ref[:, sl, :], v_ref[:, sl, :], kk_ref[:, sl, :], a_ref[:, sl, :])
            g = vjp((do_ref[:, sl, :], ds_ref[...]))
            ds_ref[...] = g[0]
            for ref, val in zip((dr, dlw, dk, dv, dkk, da), g[1:]):
                ref[:, sl, :] = val
            return carry

        lax.fori_loop(0, n_chunks, step, 0)

        for at, action in plan[1:]:
            pl.when(pl.program_id(0) == at)(functools.partial(action, x_in, x_out, sems))

    hm = pl.BlockSpec((NH, tb, HN), lambda i: (0, nb - 1 - i, 0))
    res = pl.pallas_call(
        body, name="rwkv_scan_bwd", grid=(nb,),
        in_specs=[hm] * 6 + [pl.BlockSpec((n_chunks, 2, NH, HN, HN), lambda i: (nb - 1 - i, 0, 0, 0, 0)), hm]
        + (ex.any_specs if ex else []),
        out_specs=[hm] * 6 + (ex.any_specs if ex else []),
        out_shape=[jax.ShapeDtypeStruct((NH, T, HN), F32)] * 6 + (ex.out_shape if ex else []),
        scratch_shapes=[pltpu.VMEM((NH, HN, HN), F32)] + (ex.sem_shapes if ex else []),
        compiler_params=pltpu.CompilerParams(dimension_semantics=("arbitrary",), vmem_limit_bytes=VMEM_LIMIT),
    )(r, lw, k, v, kk, a, s0s, do, *(ex.bufs if ex else []))
    return list(res[:6]), list(res[6:])


def _shift_down(i, p, prev8):
    first = jnp.where(i > 0, prev8[7:8, :], 0.0)
    row = lax.broadcasted_iota(jnp.int32, p.shape, 0)
    return jnp.where(row == 0, first, pltpu.roll(p, 1, axis=0))


def _mix_fwd(p, sb, tm=256):
    def fn(i, n, p, prev8, sb):
        return (p * sb[0:1] + _shift_down(i, p, prev8) * sb[1:2],)
    return _rows_call("shift_mix_fwd", fn, [Rows(p), Halo(p, -1)], [sb], [("rows", p.shape[1], F32)], tm=tm,
                      with_pid=True)[0]


def _mix_bwd(dq, p, sb, tm=256):
    def fn(i, n, dq, next8, p, prev8, sb):
        ps = _shift_down(i, p, prev8)
        d1 = dq * sb[1:2]
        last = jnp.where(i < n - 1, next8[0:1, :] * sb[1:2], 0.0)
        row = lax.broadcasted_iota(jnp.int32, dq.shape, 0)
        up = jnp.where(row == dq.shape[0] - 1, last, pltpu.roll(d1, dq.shape[0] - 1, axis=0))
        return (dq * sb[0:1] + up, jnp.sum(dq * p, axis=0, keepdims=True), jnp.sum(dq * ps, axis=0, keepdims=True))
    w = p.shape[1]
    return _rows_call("shift_mix_bwd", fn, [Rows(dq), Halo(dq, +1), Rows(p), Halo(p, -1)], [sb], [("rows", w, F32)],
                      accs=[((1, w), F32), ((1, w), F32)], tm=tm, with_pid=True)


def _local_step(x, target, W, late_weights=None, early_grads=None, w_in_grads_ready=None, ffn_grads_ready=None):
    G = {}
    a = _rows_call("norm_mix_fwd", lambda x, g: (_rms(x, g),), [Rows(x)], [W["g_mix"]], [("rows", D, BF16)])[0]
    p_sgu = _matmul("proj_sgu", a, W["w_sgu_t"], "nt")
    p_rw = _matmul("proj_rwkv", a, W["w_rw_t"], "nt")
    p_gate = _matmul("proj_gate", a, W["w_gate_t"], "nt")

    sgu_consts = [W["sgu_ln_w"], W["sgu_ln_b"], W["sgu_w"], W["sgu_bt"]]
    s = _rows_call("sgu_fwd", lambda *t: (_sgu_fn(*t),), [Rows(p_sgu)], sgu_consts, [("rows", D, BF16)], tm=SGU_C)[0]

    q = _mix_fwd(p_rw, W["sb"])
    q_ins = [Rows(q, D, 0), Rows(q, D, 1), Rows(q, D, 2), Rows(q, 128, 24), Rows(q, 128, 25), Rows(q, 256, 13)]
    pre_consts = [W["w_lora"], W["w0"], W["a_lora"], W["a0"], W["g_lora"], W["k_k"], W["k_a"]]
    r_h, lw_h, k_h, v_h, kk_h, a_h, g_gate = _rows_call(
        "rwkv_pre_fwd", _pre_fn, q_ins, pre_consts, [("heads", F32)] * 6 + [("rows", D, F32)], tm=128)
    o_h, s0s, got = _scan_fwd(r_h, lw_h, k_h, v_h, kk_h, a_h, ex=late_weights[0] if late_weights else None)
    if late_weights:
        W = {**W, **late_weights[1](got)}
    y_a = _matmul("proj_a", s, W["w_proj_a"], "nn")
    post_ins = [Heads(o_h), Heads(r_h), Heads(k_h), Heads(v_h), Rows(g_gate)]
    post_consts = [W[n].reshape(NH, 1, HN) for n in ("ln_x_w", "ln_x_b", "r_k")]
    z_b = _rows_call("rwkv_post_fwd", lambda *t: (_post_fn(*t),), post_ins, post_consts, [("rows", D, BF16)], tm=128)[0]
    y_b, mixed = _matmul("proj_b", z_b, W["w_proj_b"], "nn", extras=[(p_gate, 0), (p_gate, 1), y_a],
                         epilogue=lambda yb, ga, gb, ya: (yb, _sigmoid(ga) * ya + _sigmoid(gb) * yb),
                         out_dtypes=(F32, BF16))
    gate_ins = [Rows(p_gate), Rows(y_a), Rows(y_b)]

    def res1(mo, x, g):
        h1 = x + mo
        return h1, _rms(h1, g)
    h1, f = _matmul("proj_out", mixed, W["w_out"], "nn", extras=[x, W["g_ffn"]], epilogue=res1,
                    out_dtypes=(F32, BF16))

    def relu_sq(u):
        r = jnp.maximum(u, 0.0)
        return r, r * r
    r1, act = _matmul("ffn_up", f, W["w_ffn1"], "nn", epilogue=relu_sq, out_dtypes=(BF16, BF16))
    ff = _matmul("ffn_down", act, W["w_ffn2"], "nn")

    def head(h1, ff, tgt, g):
        def f_(h1, ff, g):
            y = _rms(h1 + ff, g)
            return 0.5 * jnp.sum(jnp.mean(jnp.square(y - tgt), axis=-1))
        loss, (dh2, _, dg) = jax.value_and_grad(f_, argnums=(0, 1, 2))(h1, ff, g)
        return dh2, jnp.full((8, LANES), loss, F32), dg
    dh2, loss_acc, G["g_final"] = _rows_call("loss_head", head, [Rows(h1), Rows(ff), Rows(target)], [W["g_final"]],
                                             [("rows", D, F32)], accs=[((8, LANES), F32), ((1, D), F32)])

    d_u1 = _matmul("ffn_down_dx", dh2, W["w_ffn2"], "nt", extras=[r1], out_dtypes=(BF16,),
                   epilogue=lambda d_act, r: (d_act * 2.0 * r.astype(F32),))[0]
    G["w_ffn2"] = _matmul("ffn_down_dw", act, dh2, "tn", out_dtype=GRAD_PAYLOAD)
    d_f = _matmul("ffn_up_dx", d_u1, W["w_ffn1"], "nt")
    G["w_ffn1"] = _matmul("ffn_up_dw", f, d_u1, "tn", out_blocks=N_DEV, out_dtype=GRAD_PAYLOAD)

    def res1_bwd(h1, d_f, dh2, g):
        _, vjp = jax.vjp(_rms, h1, g)
        dh, dg = vjp(d_f)
        return dh2 + dh, dg
    dh1, G["g_ffn"] = _rows_call("residual_norm_bwd", res1_bwd, [Rows(h1), Rows(d_f), Rows(dh2)],
                                 [W["g_ffn"]], [("rows", D, F32)], accs=[((1, D), F32)])
    ffn_token = ffn_grads_ready(G) if ffn_grads_ready else None
    d_mixed = _matmul("proj_out_dx", dh1, W["w_out"], "nt", after=ffn_token)
    G["w_out"] = _matmul("proj_out_dw", mixed, dh1, "tn", out_dtype=GRAD_PAYLOAD)

    def gate_bwd(pg, ya, yb, dm):
        _, vjp = jax.vjp(_gate_fn, pg, ya, yb)
        return vjp(dm)
    d_gate, d_ya, d_yb = _rows_call("gate_bwd", gate_bwd, gate_ins + [Rows(d_mixed)], [],
                                    [("rows", 2 * D, F32), ("rows", D, BF16), ("rows", D, BF16)])

    d_s = _matmul("proj_a_dx", d_ya, W["w_proj_a"], "nt")
    G["w_proj_a"] = _matmul("proj_a_dw", s, d_ya, "tn", out_dtype=GRAD_PAYLOAD)

    def sgu_bwd(p, ds, *c):
        _, vjp = jax.vjp(_sgu_fn, p, *c)
        return vjp(ds)
    d_p_sgu, G["sgu_ln_w"], G["sgu_ln_b"], G["sgu_w"], G["sgu_bt"] = _rows_call(
        "sgu_bwd", sgu_bwd, [Rows(p_sgu), Rows(d_s)], sgu_consts, [("rows", 2 * D, F32)],
        accs=[((1, D), F32), ((1, D), F32), ((SGU_G, SGU_C, SGU_C), F32), ((SGU_C, SGU_G), F32)], tm=SGU_C)

    d_zb = _matmul("proj_b_dx", d_yb, W["w_proj_b"], "nt")
    G["w_proj_b"] = _matmul("proj_b_dw", z_b, d_yb, "tn", out_dtype=GRAD_PAYLOAD)

    def post_bwd(o, r, k2, v, g, dz, *c):
        _, vjp = jax.vjp(_post_fn, o, r, k2, v, g, *c)
        return vjp(dz)
    do_h, dr1, dk1, dv1, d_g, g_lnw, g_lnb, g_rk = _rows_call(
        "rwkv_post_bwd", post_bwd, post_ins + [Rows(d_zb)], post_consts, [("heads", F32)] * 4 + [("rows", D, F32)],
        accs=[((NH, 1, HN), F32)] * 3, tm=128)
    G["ln_x_w"], G["ln_x_b"], G["r_k"] = (t.reshape(1, D) for t in (g_lnw, g_lnb, g_rk))
    (dr2, dlw, dk2, dv2, dkk, daa), early = _scan_bwd(r_h, lw_h, k_h, v_h, kk_h, a_h, s0s, do_h,
                                                      ex=early_grads(G) if early_grads else None)

    def pre_bwd(qr, qk, qv, qxw, qxa, qxg, dr1, dr2, dlw, dk1, dk2, dv1, dv2, dkk, daa, dg, *c):
        _, vjp = jax.vjp(_pre_fn, qr, qk, qv, qxw, qxa, qxg, *c)
        g = vjp((dr1 + dr2, dlw, dk1 + dk2, dv1 + dv2, dkk, daa, dg))
        dq = jnp.concatenate(g[:6], axis=-1)
        return (dq,) + tuple(g[6:])
    pre_b_ins = q_ins + [Heads(dr1), Heads(dr2), Heads(dlw), Heads(dk1), Heads(dk2), Heads(dv1), Heads(dv2),
                         Heads(dkk), Heads(daa), Rows(d_g)]
    d_q, G["w_lora"], G["w0"], G["a_lora"], G["a0"], G["g_lora"], G["k_k"], G["k_a"] = _rows_call(
        "rwkv_pre_bwd", pre_bwd, pre_b_ins, pre_consts, [("rows", RW_INT, F32)],
        accs=[((128, D), F32), ((1, D), F32), ((128, D), F32), ((1, D), F32), ((256, D), F32), ((1, D), F32),
              ((1, D), F32)], tm=128)
    d_p_rw, dsb0, dsb1 = _mix_bwd(d_q, p_rw, W["sb"])
    G["sb"] = jnp.concatenate([dsb0, dsb1], axis=0)

    G["w_sgu_t"] = _matmul("proj_sgu_dw", d_p_sgu, a, "tn", out_dtype=GRAD_PAYLOAD)
    G["w_rw_t"] = _matmul("proj_rwkv_dw", d_p_rw, a, "tn", out_dtype=GRAD_PAYLOAD)
    G["w_gate_t"] = _matmul("proj_gate_dw", d_gate, a, "tn", out_dtype=GRAD_PAYLOAD)
    token = w_in_grads_ready(G) if w_in_grads_ready else None
    da1 = _matmul("proj_sgu_dx", d_p_sgu, W["w_sgu_t"], "nn", after=token)
    da2 = _matmul("proj_rwkv_dx", d_p_rw, W["w_rw_t"], "nn", after=token)
    da3 = _matmul("proj_gate_dx", d_gate, W["w_gate_t"], "nn", after=token)

    def norm1_bwd(x, da1, da2, da3, dh1, g):
        _, vjp = jax.vjp(_rms, x, g)
        dx, dg = vjp(da1 + da2 + da3)
        return dh1 + dx, dg
    dx, G["g_mix"] = _rows_call("norm_mix_bwd", norm1_bwd, [Rows(x), Rows(da1), Rows(da2), Rows(da3), Rows(dh1)],
                                [W["g_mix"]], [("rows", D, F32)], accs=[((1, D), F32)])
    return loss_acc[0, 0], dx, G, early


class Exchange:
    def __init__(self, bufs, gathers):
        self.bufs, self.gathers, self.nb = list(bufs), list(gathers), len(bufs)
        self.any_specs = [pl.BlockSpec(memory_space=pl.ANY)] * self.nb
        self.out_shape = [jax.ShapeDtypeStruct((N_DEV,) + (b.shape if g else b.shape[1:]), b.dtype)
                          for b, g in zip(self.bufs, self.gathers)]
        n = (N_DEV - 1) * self.nb
        self.sem_shapes = [pltpu.SemaphoreType.DMA((n,)), pltpu.SemaphoreType.DMA((n,)),
                           pltpu.SemaphoreType.DMA((self.nb,))]

    def _copies(self, in_refs, out_refs, sems):
        send_sems, recv_sems, local_sems = sems
        x, y, c = lax.axis_index("x"), lax.axis_index("y"), lax.axis_index("c")
        me = 4 * x + 2 * y + c

        def src(b, dest):
            return in_refs[b] if self.gathers[b] else in_refs[b].at[dest]

        local = [pltpu.make_async_copy(src(b, me), out_refs[b].at[me], local_sems.at[b]) for b in range(self.nb)]
        sends, recvs = [], []
        for kbits in range(1, N_DEV):
            px = 1 - x if kbits & 4 else x
            py = 1 - y if kbits & 2 else y
            pc = 1 - c if kbits & 1 else c
            peer = 4 * px + 2 * py + pc
            for b in range(self.nb):
                s = (kbits - 1) * self.nb + b
                sends.append(pltpu.make_async_remote_copy(
                    src_ref=src(b, peer), dst_ref=out_refs[b].at[me], send_sem=send_sems.at[s],
                    recv_sem=recv_sems.at[s], device_id=(px, py, pc), device_id_type=pl.DeviceIdType.MESH))
                recvs.append(pltpu.make_async_remote_copy(
                    src_ref=src(b, peer), dst_ref=out_refs[b].at[peer], send_sem=send_sems.at[s],
                    recv_sem=recv_sems.at[s], device_id=(px, py, pc), device_id_type=pl.DeviceIdType.MESH))
        return local, sends, recvs

    def start(self, in_refs, out_refs, sems):
        local, sends, _ = self._copies(in_refs, out_refs, sems)
        for cp in sends + local:
            cp.start()

    def wait(self, in_refs, out_refs, sems):
        local, sends, recvs = self._copies(in_refs, out_refs, sems)
        for cp in recvs:
            cp.wait_recv()
        for cp in sends:
            cp.wait_send()
        for cp in local:
            cp.wait()

    def schedule(self, n_steps):
        return [(0, self.start), (n_steps - 1, self.wait)]


def _exchange(name, bufs, gather):
    ex = Exchange(bufs, gather if isinstance(gather, (list, tuple)) else [gather] * len(bufs))

    def body(*refs):
        in_refs, out_refs, sems = refs[:ex.nb], refs[ex.nb:2 * ex.nb], refs[2 * ex.nb:]
        ex.start(in_refs, out_refs, sems)
        ex.wait(in_refs, out_refs, sems)

    return pl.pallas_call(body, name=name, in_specs=ex.any_specs, out_specs=ex.any_specs, out_shape=ex.out_shape,
                          scratch_shapes=ex.sem_shapes)(*ex.bufs)


N_CHIP = 4


def _pair_exchange(name, blocks):
    def body(b_ref, got_ref, send_sems, recv_sems):
        x, y, c = lax.axis_index("x"), lax.axis_index("y"), lax.axis_index("c")
        copies = [pltpu.make_async_remote_copy(
            src_ref=b_ref.at[2 * q + 1 - c], dst_ref=got_ref.at[q], send_sem=send_sems.at[q],
            recv_sem=recv_sems.at[q], device_id=(x, y, 1 - c), device_id_type=pl.DeviceIdType.MESH)
            for q in range(N_CHIP)]
        for cp in copies:
            cp.start()
        for cp in copies:
            cp.wait_recv()
        for cp in copies:
            cp.wait_send()

    any_spec = pl.BlockSpec(memory_space=pl.ANY)
    return pl.pallas_call(
        body, name=name, in_specs=[any_spec], out_specs=any_spec,
        out_shape=jax.ShapeDtypeStruct((N_CHIP,) + blocks.shape[1:], blocks.dtype),
        scratch_shapes=[pltpu.SemaphoreType.DMA((N_CHIP,))] * 2,
    )(blocks)


def _pair_sum(name, blocks, got, core):
    _, R, Wd = blocks.shape

    def body(c_ref, a_ref, b_ref, o_ref):
        o_ref[...] = (a_ref[...].astype(F32) + b_ref[...].astype(F32)).astype(o_ref.dtype)

    return pl.pallas_call(
        body, name=name,
        grid_spec=pltpu.PrefetchScalarGridSpec(
            num_scalar_prefetch=1, grid=(N_CHIP,),
            in_specs=[pl.BlockSpec((None, R, Wd), lambda q, c_ref: (2 * q + c_ref[0], 0, 0)),
                      pl.BlockSpec((None, R, Wd), lambda q, c_ref: (q, 0, 0))],
            out_specs=pl.BlockSpec((None, R, Wd), lambda q, c_ref: (q, 0, 0))),
        out_shape=jax.ShapeDtypeStruct(got.shape, blocks.dtype),
        compiler_params=pltpu.CompilerParams(dimension_semantics=("parallel",), vmem_limit_bytes=VMEM_LIMIT),
    )(core, blocks, got)


def _chip_copies(s_ref, land_ref, send_sems, recv_sems):
    x, y, c = lax.axis_index("x"), lax.axis_index("y"), lax.axis_index("c")
    my_q = 2 * x + y
    sends, recvs = [], []
    for kbits in range(1, N_CHIP):
        px = 1 - x if kbits & 2 else x
        py = 1 - y if kbits & 1 else y
        peer_q = 2 * px + py
        sends.append(pltpu.make_async_remote_copy(
            src_ref=s_ref.at[peer_q], dst_ref=land_ref.at[my_q], send_sem=send_sems[kbits - 1],
            recv_sem=recv_sems[kbits - 1], device_id=(px, py, c), device_id_type=pl.DeviceIdType.MESH))
        recvs.append(pltpu.make_async_remote_copy(
            src_ref=s_ref.at[peer_q], dst_ref=land_ref.at[peer_q], send_sem=send_sems[kbits - 1],
            recv_sem=recv_sems[kbits - 1], device_id=(px, py, c), device_id_type=pl.DeviceIdType.MESH))
    return sends, recvs


_HBM = pl.BlockSpec(memory_space=pltpu.HBM)
_SEM = pl.BlockSpec(memory_space=pltpu.SEMAPHORE)
N_CHIP_SEMS = 2 * (N_CHIP - 1)


def _scatter_copies(b_refs, land_refs, send_sems, recv_sems):
    x, y, c = lax.axis_index("x"), lax.axis_index("y"), lax.axis_index("c")
    me = 4 * x + 2 * y + c
    sends, recvs = [], []
    for kbits in range(1, N_DEV):
        px = 1 - x if kbits & 4 else x
        py = 1 - y if kbits & 2 else y
        pc = 1 - c if kbits & 1 else c
        peer = 4 * px + 2 * py + pc
        for b in range(len(b_refs)):
            s = (kbits - 1) * len(b_refs) + b
            sends.append(pltpu.make_async_remote_copy(
                src_ref=b_refs[b].at[peer], dst_ref=land_refs[b].at[me], send_sem=send_sems[s],
                recv_sem=recv_sems[s], device_id=(px, py, pc), device_id_type=pl.DeviceIdType.MESH))
            recvs.append(pltpu.make_async_remote_copy(
                src_ref=b_refs[b].at[peer], dst_ref=land_refs[b].at[peer], send_sem=send_sems[s],
                recv_sem=recv_sems[s], device_id=(px, py, pc), device_id_type=pl.DeviceIdType.MESH))
    return sends, recvs


def _scatter_start(name, bufs):
    nb = len(bufs)
    n = (N_DEV - 1) * nb

    def body(*refs):
        b_refs, land_refs, outs = refs[:nb], refs[nb:2 * nb], refs[2 * nb:]
        sends, _ = _scatter_copies(b_refs, land_refs, outs[:n], outs[n:2 * n])
        for cp in sends:
            cp.start()
        token = outs[2 * n + 2 * nb]
        token[...] = jnp.zeros_like(token)

    thru = tuple(pltpu.HBM(b.shape, b.dtype) for b in bufs)
    res = pl.pallas_call(
        body, name=name, in_specs=(_HBM,) * (2 * nb),
        out_specs=(_SEM,) * (2 * n) + (_HBM,) * (2 * nb) + (pl.BlockSpec(memory_space=pltpu.VMEM),),
        out_shape=(pltpu.SemaphoreType.DMA(()),) * (2 * n) + thru + thru + (jax.ShapeDtypeStruct((8, LANES), F32),),
        input_output_aliases={i: 2 * n + i for i in range(2 * nb)},
        compiler_params=pltpu.CompilerParams(has_side_effects=pltpu.SideEffectType.DATAFLOW_SIDE_EFFECTING),
    )(*[pltpu.with_memory_space_constraint(b, pltpu.HBM) for b in bufs],
      *[pltpu.with_memory_space_constraint(lax.empty(b.shape, b.dtype), pltpu.HBM) for b in bufs])
    return res[:2 * n], list(res[2 * n:2 * n + nb]), list(res[2 * n + nb:2 * n + 2 * nb]), res[2 * n + 2 * nb]


def _scatter_wait(name, sems, bufs_thru, lands_thru, after):
    nb = len(bufs_thru)
    n = (N_DEV - 1) * nb

    def body(*refs):
        b_refs, land_refs, sem_refs = refs[:nb], refs[nb:2 * nb], refs[2 * nb:2 * nb + 2 * n]
        sends, recvs = _scatter_copies(b_refs, land_refs, sem_refs[:n], sem_refs[n:])
        for cp in sends:
            cp.wait_send()
        for cp in recvs:
            cp.wait_recv()

    thru = tuple(pltpu.HBM(b.shape, b.dtype) for b in bufs_thru)
    res = pl.pallas_call(
        body, name=name, in_specs=(_HBM,) * (2 * nb) + (_SEM,) * (2 * n) + (pl.BlockSpec(memory_space=pl.ANY),),
        out_specs=(_HBM,) * (2 * nb), out_shape=thru + thru,
        input_output_aliases={i: i for i in range(2 * nb)},
        compiler_params=pltpu.CompilerParams(has_side_effects=pltpu.SideEffectType.DATAFLOW_SIDE_EFFECTING),
    )(*bufs_thru, *lands_thru, *sems, after)
    return list(res[:nb]), list(res[nb:])


def _chip_exchange_start(name, sums):
    def body(s_ref, land_ref, *outs):
        sems, token = outs[:N_CHIP_SEMS], outs[N_CHIP_SEMS + 2]
        sends, _ = _chip_copies(s_ref, land_ref, sems[:N_CHIP - 1], sems[N_CHIP - 1:])
        for cp in sends:
            cp.start()
        token[...] = jnp.zeros_like(token)

    res = pl.pallas_call(
        body, name=name, in_specs=(_HBM, _HBM),
        out_specs=(_SEM,) * N_CHIP_SEMS + (_HBM, _HBM, pl.BlockSpec(memory_space=pltpu.VMEM)),
        out_shape=(pltpu.SemaphoreType.DMA(()),) * N_CHIP_SEMS
        + (pltpu.HBM(sums.shape, sums.dtype), pltpu.HBM(sums.shape, sums.dtype), jax.ShapeDtypeStruct((8, LANES), F32)),
        input_output_aliases={0: N_CHIP_SEMS, 1: N_CHIP_SEMS + 1},
        compiler_params=pltpu.CompilerParams(has_side_effects=pltpu.SideEffectType.DATAFLOW_SIDE_EFFECTING),
    )(pltpu.with_memory_space_constraint(sums, pltpu.HBM),
      pltpu.with_memory_space_constraint(lax.empty(sums.shape, sums.dtype), pltpu.HBM))
    return res[:N_CHIP_SEMS], res[N_CHIP_SEMS], res[N_CHIP_SEMS + 1], res[N_CHIP_SEMS + 2]


def _chip_exchange_wait(name, sems, sums_thru, land_thru, after):
    def body(s_ref, land_ref, *rest):
        sems = rest[:N_CHIP_SEMS]
        sends, recvs = _chip_copies(s_ref, land_ref, sems[:N_CHIP - 1], sems[N_CHIP - 1:])
        for cp in sends:
            cp.wait_send()
        for cp in recvs:
            cp.wait_recv()

    return pl.pallas_call(
        body, name=name, in_specs=(_HBM, _HBM) + (_SEM,) * N_CHIP_SEMS + (pl.BlockSpec(memory_space=pl.ANY),),
        out_specs=(_HBM, _HBM),
        out_shape=(pltpu.HBM(sums_thru.shape, sums_thru.dtype), pltpu.HBM(sums_thru.shape, sums_thru.dtype)),
        input_output_aliases={0: 0, 1: 1},
        compiler_params=pltpu.CompilerParams(has_side_effects=pltpu.SideEffectType.DATAFLOW_SIDE_EFFECTING),
    )(sums_thru, land_thru, *sems, after)


def _adamw(name, slots, w, m, v, tr=256):
    unit_mid = w.ndim == 3 and w.shape[1] == 1 and w.shape[0] > 1
    R, Wd = (w.shape[0], w.shape[2]) if unit_mid else w.shape[-2:]
    depth_axis = w.ndim == 3 and not unit_mid
    if R % tr == 0:
        tc = Wd
    else:
        tr, tc = R, (256 if (Wd % 256 == 0 and R > 256) else Wd)
    at = (slice(None), 0, slice(None)) if unit_mid else Ellipsis

    def body(s_ref, w_ref, m_ref, v_ref, g_out, d_out, m_out, v_out):
        g = s_ref[0].astype(F32)
        for j in range(1, slots.shape[0]):
            g = g + s_ref[j].astype(F32)
        m_new = ADAM_B1 * m_ref[at] + (1.0 - ADAM_B1) * g
        v_new = ADAM_B2 * v_ref[at] + (1.0 - ADAM_B2) * jnp.square(g)
        m_hat = m_new / (1.0 - ADAM_B1 ** ADAM_STEP)
        v_hat = v_new / (1.0 - ADAM_B2 ** ADAM_STEP)
        g_out[at] = g
        d_out[at] = -ADAM_LR * (m_hat / (jnp.sqrt(v_hat) + ADAM_EPS) + ADAM_WD * w_ref[at])
        m_out[at] = m_new
        v_out[at] = v_new

    if unit_mid:
        row = pl.BlockSpec((tr, 1, tc), lambda i, j: (i, 0, j))
    elif depth_axis:
        row = pl.BlockSpec((None, tr, tc), lambda i, j: (0, i, j))
    else:
        row = pl.BlockSpec((tr, tc), lambda i, j: (i, j))
    return pl.pallas_call(
        body, name=name, grid=(R // tr, Wd // tc),
        in_specs=[pl.BlockSpec((slots.shape[0], tr, tc), lambda i, j: (0, i, j)), row, row, row],
        out_specs=[row] * 4, out_shape=[jax.ShapeDtypeStruct(w.shape, F32)] * 4,
        compiler_params=pltpu.CompilerParams(dimension_semantics=("parallel", "parallel"),
                                             vmem_limit_bytes=VMEM_LIMIT),
    )(slots, w, m, v)


PACK_W = 1024
PACKED = [(n, s) for n, s in REPLICATED if n != "sgu_w"]
_SMALL_SIZES = [int(np.prod(s)) for _, s in PACKED]
_SMALL_ROWS = _round_up(_round_up(sum(_SMALL_SIZES) + PACK_W, PACK_W) // PACK_W, 8)
_LOSS_AT = sum(_SMALL_SIZES)
W_IN_SHARD = P_TOTAL // N_DEV


def _pack_rows(parts, rows, dtype):
    flat = jnp.concatenate([p.reshape(-1).astype(dtype) for p in parts])
    return jnp.pad(flat, (0, rows * PACK_W - flat.shape[0])).reshape(rows, PACK_W)


def _w_in_groups_t(blocks):
    wt = blocks.reshape(P_TOTAL, D)
    o, c = 2 * D, 2 * D + 3 * D
    z = lambda r: jnp.zeros((r, D), wt.dtype)
    rw = jnp.concatenate([wt[o:c], wt[c:c + L_W], z(128 - L_W), wt[c + L_W:c + L_W + L_A], z(128 - L_A),
                          wt[c + L_W + L_A:o + C_B], z(256 - L_G)], axis=0)
    return wt[:o], rw, wt[o + C_B:]


def _w_in_grad_blocks(g_sgu_t, g_rw_t, g_gate_t):
    c = 3 * D
    full = jnp.concatenate([g_sgu_t, g_rw_t[:c], g_rw_t[c:c + L_W], g_rw_t[c + 128:c + 128 + L_A],
                            g_rw_t[c + 256:c + 256 + L_G], g_gate_t], axis=0)
    return full.reshape(N_DEV, W_IN_SHARD, D)


def _mesh_index():
    me = 4 * lax.axis_index("x") + 2 * lax.axis_index("y") + lax.axis_index("c")
    return me.astype(jnp.int32).reshape(1)


def _fill_slot(name, dst, src, idx, src_idx=None):
    R, Wd = dst.shape[1:]
    scalars = [idx] if src_idx is None else [idx, src_idx]
    if src_idx is None:
        src_spec = pl.BlockSpec((R, Wd), lambda i, *s: (0, 0))
    else:
        src_spec = pl.BlockSpec((None, R, Wd), lambda i, *s: (s[1][0], 0, 0))

    def body(*refs):
        src_ref, out_ref = refs[len(scalars) + 1], refs[len(scalars) + 2]
        out_ref[...] = src_ref[...]

    return pl.pallas_call(
        body, name=name,
        grid_spec=pltpu.PrefetchScalarGridSpec(
            num_scalar_prefetch=len(scalars), grid=(1,),
            in_specs=[pl.BlockSpec(memory_space=pl.ANY), src_spec],
            out_specs=pl.BlockSpec((None, R, Wd), lambda i, *s: (s[0][0], 0, 0))),
        out_shape=jax.ShapeDtypeStruct(dst.shape, dst.dtype),
        input_output_aliases={len(scalars): 0},
        compiler_params=pltpu.CompilerParams(vmem_limit_bytes=VMEM_LIMIT),
    )(*scalars, dst, src)


class TwoLevelGather:
    def __init__(self, bufs, skip_own=()):
        self.bufs, self.nb, self.skip_own = list(bufs), len(bufs), tuple(skip_own)
        self.any_specs = [pl.BlockSpec(memory_space=pl.ANY)] * self.nb
        self.out_shape = [jax.ShapeDtypeStruct((N_DEV,) + b.shape, b.dtype) for b in self.bufs]
        self.sem_shapes = [pltpu.SemaphoreType.DMA((7 * self.nb,)), pltpu.SemaphoreType.DMA((7 * self.nb,)),
                           pltpu.SemaphoreType.DMA((self.nb,))]

    def _copies(self, in_refs, out_refs, sems):
        send_sems, recv_sems, local_sems = sems
        nb = self.nb
        x, y, c = lax.axis_index("x"), lax.axis_index("y"), lax.axis_index("c")
        me, sibling = (x, y, c), (x, y, 1 - c)
        chips = [(1 - x, y), (x, 1 - y), (1 - x, 1 - y)]

        def slot(b, dev):
            return out_refs[b].at[4 * dev[0] + 2 * dev[1] + dev[2]]

        def copy(b, k, block, to, own=False):
            return pltpu.make_async_remote_copy(
                src_ref=in_refs[b] if own else slot(b, block), dst_ref=slot(b, block),
                send_sem=send_sems.at[7 * b + k], recv_sem=recv_sems.at[7 * b + k], device_id=to,
                device_id_type=pl.DeviceIdType.MESH)

        cp = {}
        cp["local"] = [pltpu.make_async_copy(in_refs[b], slot(b, me), local_sems.at[b]) for b in range(nb)
                       if b not in self.skip_own]
        cp["first"] = [copy(b, 0, me, sibling, own=True) for b in range(nb)]
        cp["first"] += [copy(b, 1 + j, me, (*chip, c), own=True) for j, chip in enumerate(chips) for b in range(nb)]
        cp["over_ici"] = [copy(b, 1 + j, (*chip, c), me) for j, chip in enumerate(chips) for b in range(nb)]
        cp["passed"] = [copy(b, 4 + j, (*chip, c), sibling) for j, chip in enumerate(chips) for b in range(nb)]
        cp["from_sibling"] = [copy(b, 0, sibling, me) for b in range(nb)]
        cp["from_sibling"] += [copy(b, 4 + j, (*chip, 1 - c), me) for j, chip in enumerate(chips) for b in range(nb)]
        return cp

    def start(self, in_refs, out_refs, sems):
        cp = self._copies(in_refs, out_refs, sems)
        for c in cp["first"] + cp["local"]:
            c.start()

    def forward(self, in_refs, out_refs, sems):
        cp = self._copies(in_refs, out_refs, sems)
        for arrived, onward in zip(cp["over_ici"], cp["passed"]):
            arrived.wait_recv()
            onward.start()

    def finish(self, in_refs, out_refs, sems):
        cp = self._copies(in_refs, out_refs, sems)
        for c in cp["from_sibling"]:
            c.wait_recv()
        for c in cp["first"] + cp["passed"]:
            c.wait_send()
        for c in cp["local"]:
            c.wait()

    def schedule(self, n_steps):
        return [(0, self.start), (max(n_steps - 3, 0), self.forward), (n_steps - 1, self.finish)]


def _all_gather_two_level(name, bufs, skip_own=()):
    ex = TwoLevelGather(bufs, skip_own)

    def body(*refs):
        args = refs[:ex.nb], refs[ex.nb:2 * ex.nb], refs[2 * ex.nb:]
        ex.start(*args)
        ex.forward(*args)
        ex.finish(*args)

    return pl.pallas_call(body, name=name, in_specs=ex.any_specs, out_specs=ex.any_specs, out_shape=ex.out_shape,
                          scratch_shapes=ex.sem_shapes)(*ex.bufs)


def _cols_from_blocks(blk):
    return jnp.transpose(blk, (1, 0, 2)).reshape(blk.shape[1], -1)


def _cols_to_blocks(g):
    r, c = g.shape
    return jnp.transpose(g.reshape(r, N_DEV, c // N_DEV), (1, 0, 2))


FIRST_WEIGHTS = ["w_in", "shift_b", "w_lora_w", "a_lora_w", "g_lora_w"]
LATE_WEIGHTS = ["w_proj_a", "w_proj_b", "w_out", "w_ffn1", "w_ffn2"]
SCAN_CARRIED = ["w_proj_a", "w_proj_b", "w_out"]
FFN_WEIGHTS = ["w_ffn1", "w_ffn2"]


def _late_weights(shards):
    ex = TwoLevelGather([shards[n].astype(BF16) for n in LATE_WEIGHTS])

    def finish(results):
        got = dict(zip(LATE_WEIGHTS, results))
        W = {n: got[n].reshape(-1, D) for n in ("w_proj_a", "w_proj_b", "w_out", "w_ffn2")}
        W["w_ffn1"] = got["w_ffn1"].reshape(N_DEV, D, -1)
        return W
    return ex, finish


def _gather_weights(shards):
    def payload(n):
        if n == "w_in":
            return jnp.transpose(shards[n][0]).astype(BF16)
        return shards[n] if n == "shift_b" else shards[n].astype(BF16)
    payloads = [payload(n) for n in FIRST_WEIGHTS]
    got = dict(zip(FIRST_WEIGHTS, _all_gather_two_level("weight_all_gather", payloads, skip_own=(0,))))
    got["w_in"] = _fill_slot("w_in_own_slot", got["w_in"], payloads[0], _mesh_index())
    W = {}
    W["w_sgu_t"], W["w_rw_t"], W["w_gate_t"] = _w_in_groups_t(got["w_in"])
    z = lambda r, c, dt: jnp.zeros((r, c), dt)
    W["w_lora"] = jnp.concatenate([_cols_from_blocks(got["w_lora_w"][:, 0]).astype(F32), z(128 - L_W, D, F32)], axis=0)
    W["a_lora"] = jnp.concatenate([_cols_from_blocks(got["a_lora_w"][:, 0]).astype(F32), z(128 - L_A, D, F32)], axis=0)
    W["g_lora"] = jnp.concatenate([_cols_from_blocks(got["g_lora_w"][:, 0]).astype(F32), z(256 - L_G, D, F32)], axis=0)
    sb = _cols_from_blocks(got["shift_b"][:, 0])
    W["sb"] = jnp.concatenate([sb[:, :3 * D], sb[:, 3 * D:3 * D + L_W], z(2, 128 - L_W, F32),
                               sb[:, 3 * D + L_W:3 * D + L_W + L_A], z(2, 128 - L_A, F32),
                               sb[:, 3 * D + L_W + L_A:], z(2, 256 - L_G, F32)], axis=1)
    return W


def _replicated_weights(rep):
    W = {n: rep[n] for n in ("g_mix", "sgu_ln_w", "sgu_ln_b", "w0", "a0", "k_k", "k_a", "r_k", "ln_x_w", "ln_x_b",
                             "g_ffn")}
    W["g_final"] = rep["g_final"].reshape(1, D)
    W["sgu_w"] = rep["sgu_w"][0]
    W["sgu_bt"] = jnp.transpose(rep["sgu_b"][0])
    return W


def _late_grad_blocks(G):
    return Exchange([G[n].reshape(N_DEV, -1, D) for n in SCAN_CARRIED]
                    + [G["sgu_w"].reshape(SGU_G * SGU_C, SGU_C).astype(GRAD_PAYLOAD)],
                    [False] * len(SCAN_CARRIED) + [True])


def _first_grad_blocks(G):
    sbg = G["sb"]
    c = 3 * D
    sb = jnp.concatenate([sbg[:, :c], sbg[:, c:c + L_W], sbg[:, c + 128:c + 128 + L_A],
                          sbg[:, c + 256:c + 256 + L_G]], axis=1)
    return {
        "shift_b": _cols_to_blocks(sb),
        "w_lora_w": _cols_to_blocks(G["w_lora"][:L_W]), "a_lora_w": _cols_to_blocks(G["a_lora"][:L_A]),
        "g_lora_w": _cols_to_blocks(G["g_lora"][:L_G]),
    }


def _replicated_grads(G):
    small = {n: G[n] for n in ("g_mix", "sgu_ln_w", "sgu_ln_b", "w0", "a0", "k_k", "k_a", "r_k", "ln_x_w", "ln_x_b",
                               "g_ffn", "g_final")}
    small["sgu_w"] = G["sgu_w"]
    small["sgu_b"] = jnp.transpose(G["sgu_bt"])
    return small


def kernel(x, g_mix, w_in, sgu_ln_w, sgu_ln_b, sgu_w, sgu_b, w_proj_a, shift_b, w_lora_w, w0, a_lora_w, a0, g_lora_w, k_k, k_a, r_k, ln_x_w, ln_x_b, w_proj_b, w_out, g_ffn, w_ffn1, w_ffn2, g_final, loss_target, m_g_mix, m_w_in, m_sgu_ln_w, m_sgu_ln_b, m_sgu_w, m_sgu_b, m_w_proj_a, m_shift_b, m_w_lora_w, m_w0, m_a_lora_w, m_a0, m_g_lora_w, m_k_k, m_k_a, m_r_k, m_ln_x_w, m_ln_x_b, m_w_proj_b, m_w_out, m_g_ffn, m_w_ffn1, m_w_ffn2, m_g_final, v_g_mix, v_w_in, v_sgu_ln_w, v_sgu_ln_b, v_sgu_w, v_sgu_b, v_w_proj_a, v_shift_b, v_w_lora_w, v_w0, v_a_lora_w, v_a0, v_g_lora_w, v_k_k, v_k_a, v_r_k, v_ln_x_w, v_ln_x_b, v_w_proj_b, v_w_out, v_g_ffn, v_w_ffn1, v_w_ffn2, v_g_final):
    env = dict(locals())
    weights = {n: env[n] for n in WEIGHT_ORDER}
    moms = {n: env["m_" + n] for n in WEIGHT_ORDER}
    vars_ = {n: env["v_" + n] for n in WEIGHT_ORDER}

    shards = {n: weights[n] for n, _, _ in SHARDED}
    W = _gather_weights(shards)
    W.update(_replicated_weights({n: weights[n] for n, _ in REPLICATED}))
    in_flight = {}

    def send_w_in_grads(G):
        blocks = _w_in_grad_blocks(G["w_sgu_t"], G["w_rw_t"], G["w_gate_t"])
        got = _pair_exchange("grad_pair_exchange", blocks)
        core = lax.axis_index("c").astype(jnp.int32).reshape(1)
        sums = _pair_sum("grad_pair_sum", blocks, got, core)
        in_flight["sems"], in_flight["sums"], in_flight["land"], token = _chip_exchange_start("grad_chip_start", sums)
        return token

    def send_ffn_grads(G):
        in_flight["ffn"] = _scatter_start("grad_ffn_start", [G["w_ffn1"], G["w_ffn2"].reshape(N_DEV, -1, D)])
        return in_flight["ffn"][3]

    loss_part, dx, G, late_slots = _local_step(x[0], loss_target[0], W, late_weights=_late_weights(shards),
                                               early_grads=_late_grad_blocks, w_in_grads_ready=send_w_in_grads,
                                               ffn_grads_ready=send_ffn_grads)

    slots = dict(zip(SCAN_CARRIED, late_slots))
    me = _mesh_index()
    sent, landed = _scatter_wait("grad_ffn_wait", *in_flight["ffn"][:3], after=dx)
    for i, n in enumerate(FFN_WEIGHTS):
        slots[n] = _fill_slot("grad_own_slot_" + n, landed[i], sent[i], me, src_idx=me)
    blocks = _first_grad_blocks(G)
    small = _replicated_grads(G)
    small_parts = [small[n] for n, _ in PACKED] + [jnp.full((PACK_W,), loss_part, F32)]
    rest = [n for n in FIRST_WEIGHTS if n != "w_in"]
    res = _exchange("grad_exchange", [blocks[n] for n in rest] + [_pack_rows(small_parts, _SMALL_ROWS, F32)],
                    [False] * len(rest) + [True])
    slots.update(zip(rest, res[:-1]))
    small_slots = res[-1]
    sums, chip_slots = _chip_exchange_wait("grad_chip_wait", in_flight["sems"], in_flight["sums"], in_flight["land"],
                                           after=small_slots)
    my_chip = (2 * lax.axis_index("x") + lax.axis_index("y")).astype(jnp.int32).reshape(1)
    slots["w_in"] = _fill_slot("grad_own_slot", chip_slots, sums, my_chip, src_idx=my_chip)

    outs = [dict(), dict(), dict(), dict()]
    for n, _, _ in SHARDED:
        if n == "w_in":
            res = _adamw("adamw_" + n, slots[n], *[jnp.transpose(t, (2, 0, 1)) for t in (weights[n], moms[n], vars_[n])])
            res = [jnp.transpose(t, (1, 2, 0)) for t in res]
        else:
            res = _adamw("adamw_" + n, slots[n], weights[n], moms[n], vars_[n])
        for k in range(4):
            outs[k][n] = res[k]

    sgu_shape = (SGU_G * SGU_C, SGU_C)
    res = _adamw("adamw_sgu_w", late_slots[len(SCAN_CARRIED)], *[t.reshape(sgu_shape) for t in
                                                                  (weights["sgu_w"], moms["sgu_w"], vars_["sgu_w"])])
    for k in range(4):
        outs[k]["sgu_w"] = res[k].reshape(weights["sgu_w"].shape)

    def packed(d):
        return _pack_rows([d[n] for n, _ in PACKED], _SMALL_ROWS, F32)
    small_out = _adamw("adamw_replicated", small_slots, packed(weights), packed(moms), packed(vars_))
    for k in range(4):
        flat = small_out[k].reshape(-1)
        off = 0
        for (n, s), size in zip(PACKED, _SMALL_SIZES):
            outs[k][n] = flat[off:off + size].reshape(s)
            off += size
    loss = small_out[0].reshape(-1)[_LOSS_AT]
    return (loss, dx[None], *[outs[0][n] for n in WEIGHT_ORDER], *[outs[1][n] for n in WEIGHT_ORDER],
            *[outs[2][n] for n in WEIGHT_ORDER], *[outs[3][n] for n in WEIGHT_ORDER])
```

```python
import functools
import numpy as np
import jax
import jax.numpy as jnp
from jax import lax
from jax.experimental import pallas as pl
from jax.experimental.pallas import tpu as pltpu

F32 = jnp.float32
BF16 = jnp.bfloat16

D = 1024
NH, HN = 16, 64
NP, PW = NH // 2, 2 * HN
SGU_G, SGU_C = 8, 128
L_W, L_A, L_G = 64, 64, 160
C_B = 3 * D + L_W + L_A + L_G
P_TOTAL = 2 * D + C_B + 2 * D
D_FF = 4 * D
RW_INT = 3 * D + 128 + 128 + 256
NORM_EPS, LN_EPS, GN_EPS = 1e-6, 1e-5, 64e-5
N_DEV = 8
LANES = 128
SCAN_C = 64
SOLVE_B = 16
SCAN_PRECISION = lax.Precision.HIGH
SCAN_OUT_PRECISION = lax.Precision.DEFAULT
GRAD_PAYLOAD = BF16
VMEM_LIMIT = 56 * 1024 * 1024
MATMUL_VMEM_BUDGET = 40 * 1024 * 1024
STEP_COST_BYTES = 512 * 1024
HBM_COST_RATIO = 3

ADAM_LR, ADAM_B1, ADAM_B2, ADAM_EPS, ADAM_WD, ADAM_STEP = 0.001, 0.9, 0.999, 1e-08, 0.01, 10

SHARDED = [
    ("w_in", (D, P_TOTAL), 1), ("w_proj_a", (D, D), 0), ("shift_b", (2, C_B), 1), ("w_lora_w", (L_W, D), 1),
    ("a_lora_w", (L_A, D), 1), ("g_lora_w", (L_G, D), 1), ("w_proj_b", (D, D), 0), ("w_out", (D, D), 0),
    ("w_ffn1", (D, D_FF), 1), ("w_ffn2", (D_FF, D), 0),
]
REPLICATED = [
    ("g_mix", (1, D)), ("sgu_ln_w", (1, D)), ("sgu_ln_b", (1, D)), ("sgu_w", (1, SGU_G, SGU_C, SGU_C)),
    ("sgu_b", (1, SGU_G, SGU_C)), ("w0", (1, D)), ("a0", (1, D)), ("k_k", (1, D)), ("k_a", (1, D)), ("r_k", (1, D)),
    ("ln_x_w", (1, D)), ("ln_x_b", (1, D)), ("g_ffn", (1, D)), ("g_final", (D,)),
]
WEIGHT_ORDER = ["g_mix", "w_in", "sgu_ln_w", "sgu_ln_b", "sgu_w", "sgu_b", "w_proj_a", "shift_b", "w_lora_w", "w0",
                "a_lora_w", "a0", "g_lora_w", "k_k", "k_a", "r_k", "ln_x_w", "ln_x_b", "w_proj_b", "w_out", "g_ffn",
                "w_ffn1", "w_ffn2", "g_final"]


def _shard_shape(shape, axis):
    s = list(shape)
    s[axis] //= N_DEV
    return tuple(s)


def _round_up(n, m):
    return (n + m - 1) // m * m


def _pick(n, target):
    if n <= target:
        return n
    best = None
    for t in range(LANES, target + 1, LANES):
        if n % t == 0:
            best = t
    assert best is not None, (n, target)
    return best


def _matmul(name, a, b, mode, out_dtype=F32, tm=2048, tn=1024, tk=2048, out_blocks=None, epilogue=None, extras=(),
            out_dtypes=(), after=None):
    b_blocks = b.shape[0] if b.ndim == 3 else None
    bshape = b.shape if b.ndim == 2 else (b.shape[1], b.shape[0] * b.shape[2])
    if mode == "nn":
        (M, K), (K2, N) = a.shape, bshape
    elif mode == "nt":
        (M, K), (N, K2) = a.shape, bshape
    else:
        (K, M), (K2, N) = a.shape, bshape
    assert K == K2, (name, a.shape, b.shape)
    assert b_blocks is None or mode != "tn"
    assert out_blocks is None or mode == "tn"
    tn = min(tn, N // (out_blocks or 1), bshape[1] // b_blocks if (b_blocks and mode == "nn") else tn)
    tk = min(tk, bshape[1] // b_blocks if (b_blocks and mode == "nt") else tk)
    tm, tn, tk = _pick(M, tm), _pick(N, tn), _pick(K, tk)

    def vmem_bytes(tm, tk):
        tiles = tm * tk * a.dtype.itemsize + tk * tn * b.dtype.itemsize
        for dt in (out_dtypes if epilogue else (out_dtype,)):
            tiles += tm * tn * jnp.dtype(dt).itemsize
        for x in extras:
            arr = x[0] if isinstance(x, tuple) else x
            tiles += (tm if arr.shape[0] > 1 else 1) * tn * arr.dtype.itemsize
        return 2 * tiles + (tm * tn * 4 if K // tk > 1 else 0)

    def cost(tm, tk):
        ni, nj, nk = M // tm, N // tn, K // tk
        steps = ni * nj * nk
        acc_passes = steps * tm * tn * 8 if nk > 1 else 0
        a_reads = M * K * a.dtype.itemsize * (nj if nk > 1 else 1)
        b_reads = K * N * b.dtype.itemsize * (ni if (nj > 1 or nk > 1) else 1)
        return (steps * STEP_COST_BYTES + acc_passes + vmem_bytes(tm, tk) // 2
                + HBM_COST_RATIO * (a_reads + b_reads))

    options = [(m, k) for m in {_pick(M, max(t, LANES)) for t in (tm, tm // 2, tm // 4)}
               for k in {_pick(K, max(t, LANES)) for t in (tk, tk // 2, tk // 4)}
               if vmem_bytes(m, k) <= MATMUL_VMEM_BUDGET]
    tm, tk = min(options, key=lambda o: cost(*o))
    nk = K // tk
    dims = {"nn": (((1,), (0,)), ((), ())), "nt": (((1,), (1,)), ((), ())), "tn": (((0,), (0,)), ((), ()))}[mode]

    n_x, n_o = len(extras), len(out_dtypes) if epilogue else 1
    n_after = 0 if after is None else 1

    def body(a_ref, b_ref, *rest):
        x_refs, o_refs, acc = rest[:n_x], rest[n_x + n_after:n_x + n_after + n_o], rest[n_x + n_after + n_o:]
        part = lax.dot_general(a_ref[...].astype(BF16), b_ref[...].astype(BF16), dims, preferred_element_type=F32)

        def finish(res):
            outs = epilogue(res, *[r[...] for r in x_refs]) if epilogue else (res,)
            for r, v in zip(o_refs, outs):
                r[...] = v.astype(r.dtype)

        if nk == 1:
            finish(part)
            return
        acc_ref, k = acc[0], pl.program_id(2)

        @pl.when(k == 0)
        def _():
            acc_ref[...] = part

        @pl.when(k > 0)
        def _():
            acc_ref[...] += part

        @pl.when(k == nk - 1)
        def _():
            finish(acc_ref[...])

    a_spec = {"nn": pl.BlockSpec((tm, tk), lambda i, j, k: (i, k)), "nt": pl.BlockSpec((tm, tk), lambda i, j, k: (i, k)),
              "tn": pl.BlockSpec((tk, tm), lambda i, j, k: (k, i))}[mode]
    b_spec = {"nn": pl.BlockSpec((tk, tn), lambda i, j, k: (k, j)), "nt": pl.BlockSpec((tn, tk), lambda i, j, k: (j, k)),
              "tn": pl.BlockSpec((tk, tn), lambda i, j, k: (k, j))}[mode]
    if b_blocks and mode == "nn":
        per = b.shape[2] // tn
        b_spec = pl.BlockSpec((None, tk, tn), lambda i, j, k: (j // per, k, j % per))
    elif b_blocks:
        per = b.shape[2] // tk
        b_spec = pl.BlockSpec((None, tn, tk), lambda i, j, k: (k // per, j, k % per))
    out_spec = pl.BlockSpec((tm, tn), lambda i, j, k: (i, j))
    out_shape = jax.ShapeDtypeStruct((M, N), out_dtype)
    if out_blocks:
        per_o = N // out_blocks // tn
        out_spec = pl.BlockSpec((None, tm, tn), lambda i, j, k: (j // per_o, i, j % per_o))
        out_shape = jax.ShapeDtypeStruct((out_blocks, M, N // out_blocks), out_dtype)
    x_specs, x_args = [], []
    for x in extras:
        arr, off = x if isinstance(x, tuple) else (x, 0)
        if arr.shape[0] == 1:
            x_specs.append(pl.BlockSpec((1, tn), lambda i, j, k: (0, j)))
        else:
            x_specs.append(pl.BlockSpec((tm, tn), lambda i, j, k, off=off: (i, j + off)))
        x_args.append(arr)
    res = pl.pallas_call(
        body, name=name, grid=(M // tm, N // tn, nk),
        in_specs=[a_spec, b_spec] + x_specs + [pl.BlockSpec(memory_space=pl.ANY)] * n_after,
        out_specs=[out_spec] * n_o if epilogue else out_spec,
        out_shape=[jax.ShapeDtypeStruct((M, N), dt) for dt in out_dtypes] if epilogue else out_shape,
        scratch_shapes=[pltpu.VMEM((tm, tn), F32)] if nk > 1 else [],
        compiler_params=pltpu.CompilerParams(dimension_semantics=("parallel", "parallel", "arbitrary"),
                                             vmem_limit_bytes=VMEM_LIMIT),
    )(a, b, *x_args, *([after] if n_after else []))
    return res


class Rows:
    def __init__(self, arr, width=None, cb=0):
        self.arr, self.width, self.cb = arr, (arr.shape[1] if width is None else width), cb


class Heads:
    def __init__(self, arr):
        self.arr = arr


class Halo:
    def __init__(self, arr, side):
        self.arr, self.side = arr, side


def _rows_call(name, fn, ins, consts, outs, accs=(), tm=256, with_pid=False):
    T = next(o.arr.shape[1] if isinstance(o, Heads) else o.arr.shape[0] for o in ins if not isinstance(o, Halo))
    tm = min(tm, T)
    n_tiles = T // tm
    n_in, n_c, n_out = len(ins), len(consts), len(outs)
    in_specs, args = [], []
    for o in ins:
        if isinstance(o, Rows):
            in_specs.append(pl.BlockSpec((tm, o.width), lambda i, cb=o.cb: (i, cb)))
        elif isinstance(o, Heads):
            in_specs.append(pl.BlockSpec((NP, tm, PW), lambda i: (0, i, 0)))
        else:
            w = o.arr.shape[1]
            if o.side < 0:
                in_specs.append(pl.BlockSpec((8, w), lambda i: (jnp.maximum(i * (tm // 8) - 1, 0), 0)))
            else:
                in_specs.append(pl.BlockSpec((8, w), lambda i: (jnp.minimum((i + 1) * (tm // 8), T // 8 - 1), 0)))
        args.append(o.arr)
    for c in consts:
        in_specs.append(pl.BlockSpec(c.shape, lambda i, nd=c.ndim: (0,) * nd))
        args.append(c)
    out_specs, out_shape = [], []
    for o in outs:
        if o[0] == "rows":
            out_specs.append(pl.BlockSpec((tm, o[1]), lambda i: (i, 0)))
            out_shape.append(jax.ShapeDtypeStruct((T, o[1]), o[2]))
        else:
            out_specs.append(pl.BlockSpec((NP, tm, PW), lambda i: (0, i, 0)))
            out_shape.append(jax.ShapeDtypeStruct((NP, T, PW), o[1]))
    for shape, dt in accs:
        out_specs.append(pl.BlockSpec(shape, lambda i, nd=len(shape): (0,) * nd))
        out_shape.append(jax.ShapeDtypeStruct(shape, dt))

    def body(*refs):
        i = pl.program_id(0)
        vals = []
        vals = [r[...] for r in refs[:n_in + n_c]]
        res = fn(i, n_tiles, *vals) if with_pid else fn(*vals)
        out_refs = refs[n_in + n_c:]
        for r, v in zip(out_refs[:n_out], res[:n_out]):
            r[...] = v.astype(r.dtype)
        if accs:
            @pl.when(i == 0)
            def _():
                for r in out_refs[n_out:]:
                    r[...] = jnp.zeros_like(r)

            for r, v in zip(out_refs[n_out:], res[n_out:]):
                r[...] += v.astype(r.dtype)

    res = pl.pallas_call(
        body, name=name, grid=(n_tiles,), in_specs=in_specs, out_specs=out_specs, out_shape=out_shape,
        compiler_params=pltpu.CompilerParams(dimension_semantics=("arbitrary",), vmem_limit_bytes=VMEM_LIMIT),
    )(*args)
    return res


def _rms(x, g):
    return x * lax.rsqrt(jnp.mean(x * x, axis=-1, keepdims=True) + NORM_EPS) * g


def _gelu(x):
    return 0.5 * x * (1.0 + lax.erf(x * 0.7071067811865476))


def _sigmoid(x):
    return 1.0 / (1.0 + jnp.exp(-x))


def _bdot(a, b):
    return jnp.dot(a.astype(BF16), b.astype(BF16), preferred_element_type=F32)


def _to_heads(x):
    return jnp.concatenate([x[:, p * PW:(p + 1) * PW][None] for p in range(NP)], axis=0)


def _from_heads(xp):
    return jnp.concatenate([xp[p] for p in range(NP)], axis=-1)


def _head_sum(xp):
    low = lax.broadcasted_iota(jnp.int32, xp.shape, xp.ndim - 1) < HN
    both = jnp.sum(xp, axis=-1, keepdims=True)
    first = jnp.sum(jnp.where(low, xp, 0.0), axis=-1, keepdims=True)
    return jnp.where(low, first, both - first)


def _split_pairs(xp):
    return jnp.concatenate([xp[:, :, :HN], xp[:, :, HN:]], axis=0)


def _join_pairs(xh):
    return jnp.concatenate([xh[:NP], xh[NP:]], axis=-1)


def _sgu_fn(p, ln_w, ln_b, sw, sbt):
    z = _gelu(p)
    u, v = z[:, :D], z[:, D:]
    mu = jnp.mean(v, axis=-1, keepdims=True)
    var = jnp.mean(jnp.square(v - mu), axis=-1, keepdims=True)
    vn = (v - mu) * lax.rsqrt(var + LN_EPS) * ln_w + ln_b
    ri = lax.broadcasted_iota(jnp.int32, (SGU_C, SGU_C), 0)
    ci = lax.broadcasted_iota(jnp.int32, (SGU_C, SGU_C), 1)
    mask = (ci <= ri).astype(F32)
    dg = D // SGU_G
    parts = []
    for g in range(SGU_G):
        parts.append(_bdot(sw[g] * mask, vn[:, g * dg:(g + 1) * dg]) + sbt[:, g:g + 1])
    return u * jnp.concatenate(parts, axis=-1)


def _pre_fn(qr, qk, qv, qxw, qxa, qxg, wl, w0, al, a0, gl, k_k, k_a):
    w = -jax.nn.softplus(-(w0 + _bdot(jnp.tanh(qxw), wl))) - 0.5
    lw = -jnp.exp(w)
    aa = _sigmoid(a0 + _bdot(qxa, al))
    g = _bdot(_sigmoid(qxg), gl)
    kk = _to_heads(qk * k_k)
    kk = kk / jnp.maximum(jnp.sqrt(_head_sum(kk * kk)), 1e-12)
    k2 = qk * (1.0 + (aa - 1.0) * k_a)
    return _to_heads(qr), _to_heads(lw), _to_heads(k2), _to_heads(qv), kk, _to_heads(aa), g


def _post_fn(o, r, k2, v, g, ln_w, ln_b, r_k):
    mu = _head_sum(o) * (1.0 / HN)
    d = o - mu
    var = _head_sum(d * d) * (1.0 / HN)
    on = d * lax.rsqrt(var + GN_EPS) * ln_w + ln_b
    bonus = _head_sum(r * k2 * r_k) * v
    return _from_heads(on + bonus) * g


def _gate_fn(pg, ya, yb):
    return _sigmoid(pg[:, :D]) * ya + _sigmoid(pg[:, D:]) * yb


def _bmm(x, y, cx, cy, out_path=False):
    return lax.dot_general(x, y, (((cx,), (cy,)), ((0,), (0,))),
                           precision=SCAN_OUT_PRECISION if out_path else SCAN_PRECISION, preferred_element_type=F32)


def _unit_lower_inverse(M):
    C = M.shape[1]
    ti = lax.broadcasted_iota(jnp.int32, (C, C), 0)
    tj = lax.broadcasted_iota(jnp.int32, (C, C), 1)
    eye = (ti == tj).astype(F32)
    same = lambda b: (ti // b == tj // b).astype(F32)
    X = -(M * same(SOLVE_B))
    inv = eye + X
    span = 1
    while 2 * span < SOLVE_B:
        X = _bmm(X, X, 2, 1)
        inv = inv + _bmm(inv, X, 2, 1)
        span *= 2
    b = SOLVE_B
    while b < C:
        low = M * (same(2 * b) - same(b))
        inv = inv - _bmm(_bmm(inv, low, 2, 1), inv, 2, 1)
        b *= 2
    return inv


@jax.custom_vjp
def _unit_lower_solve(inv, M, y):
    return _bmm(inv, y, 2, 1)


def _unit_lower_solve_fwd(inv, M, y):
    u = _bmm(inv, y, 2, 1)
    return u, (inv, u)


def _unit_lower_solve_bwd(res, du):
    inv, u = res
    dy = _bmm(inv, du, 1, 1)
    return jnp.zeros_like(inv), -_bmm(dy, u, 2, 2), dy


_unit_lower_solve.defvjp(_unit_lower_solve_fwd, _unit_lower_solve_bwd)


def _sum_over_time(x, reverse):
    C = x.shape[1]
    ti = lax.broadcasted_iota(jnp.int32, (C, C), 0)
    tj = lax.broadcasted_iota(jnp.int32, (C, C), 1)
    ones = jnp.broadcast_to(((tj >= ti) if reverse else (tj <= ti)).astype(BF16), (x.shape[0], C, C))
    hi = x.astype(BF16)
    r1 = x - hi.astype(F32)
    mid = r1.astype(BF16)
    lo = (r1 - mid.astype(F32)).astype(BF16)
    dn = (((2,), (1,)), ((0,), (0,)))
    return sum(lax.dot_general(ones, p, dn, preferred_element_type=F32) for p in (lo, mid, hi))


@jax.custom_vjp
def _time_cumsum(lw):
    return _sum_over_time(lw, reverse=False)


_time_cumsum.defvjp(lambda lw: (_sum_over_time(lw, reverse=False), None),
                    lambda _, d: (_sum_over_time(d, reverse=True),))


def _chunk_fn(S0, r, lw, k, v, kk, a, inv=None):
    C = SCAN_C
    bmm = _bmm
    ti = lax.broadcasted_iota(jnp.int32, (C, C), 0)
    tj = lax.broadcasted_iota(jnp.int32, (C, C), 1)
    incl = (tj <= ti).astype(F32)
    strict = (tj < ti).astype(F32)
    cum = _time_cumsum(lw)
    g_in, g_ex, g_inv = jnp.exp(cum), jnp.exp(cum - lw), jnp.exp(-cum)
    kkt, rt = kk * g_ex, r * g_in
    bk = jnp.concatenate([kk * a * g_inv, k * g_inv], axis=1)
    A = bmm(kkt, bk, 2, 2)
    M = A[:, :, :C] * strict
    n_mask = jnp.concatenate([jnp.zeros((C, C), F32), strict], axis=1)
    zv = jnp.concatenate([jnp.zeros_like(v), v], axis=1)
    s0_side = bmm(jnp.concatenate([kkt, rt], axis=1), S0, 2, 2, out_path=True)
    if inv is None:
        inv = lax.stop_gradient(_unit_lower_inverse(M))
    y = _unit_lower_solve(inv, M, s0_side[:, :C] + bmm(A * n_mask, zv, 2, 1, out_path=True))
    z = jnp.concatenate([-y, v], axis=1)
    attn = bmm(rt, bk, 2, 2) * jnp.concatenate([incl, incl], axis=1)
    O = s0_side[:, C:] + bmm(attn, z, 2, 1, out_path=True)
    S1 = (S0 + bmm(z, bk, 1, 1, out_path=True)) * g_in[:, C - 1:C, :]
    return O, S1, inv


def _scan_fwd(r, lw, k, v, kk, a, ex=None, tb=256):
    assert SCAN_C == HN
    T = r.shape[1]
    tb = min(tb, T)
    n_chunks = tb // SCAN_C
    nb = T // tb
    nx = ex.nb if ex else 0

    def body(*refs):
        r_ref, lw_ref, k_ref, v_ref, kk_ref, a_ref = refs[:6]
        x_in, (o_ref, s0_ref), x_out = refs[6:6 + nx], refs[6 + nx:8 + nx], refs[8 + nx:8 + 2 * nx]
        s_ref, sems = refs[8 + 2 * nx], refs[9 + 2 * nx:]

        plan = ex.schedule(nb) if ex else []

        @pl.when(pl.program_id(0) == 0)
        def _():
            s_ref[...] = jnp.zeros_like(s_ref)
            for at, action in plan[:1]:
                action(x_in, x_out, sems)

        def step(c, carry):
            sl = pl.ds(pl.multiple_of(c * SCAN_C, SCAN_C), SCAN_C)
            S0 = s_ref[...]
            O, S1, inv = _chunk_fn(S0, *[_split_pairs(ref[:, sl, :])
                                         for ref in (r_ref, lw_ref, k_ref, v_ref, kk_ref, a_ref)])
            o_ref[:, sl, :] = _join_pairs(O)
            s0_ref[c, 0] = S0
            s0_ref[c, 1] = inv
            s_ref[...] = S1
            return carry

        lax.fori_loop(0, n_chunks, step, 0)

        for at, action in plan[1:]:
            pl.when(pl.program_id(0) == at)(functools.partial(action, x_in, x_out, sems))

    hm = pl.BlockSpec((NP, tb, PW), lambda i: (0, i, 0))
    res = pl.pallas_call(
        body, name="rwkv_scan_fwd", grid=(nb,), in_specs=[hm] * 6 + (ex.any_specs if ex else []),
        out_specs=[hm, pl.BlockSpec((n_chunks, 2, NH, HN, HN), lambda i: (i, 0, 0, 0, 0))]
        + (ex.any_specs if ex else []),
        out_shape=[jax.ShapeDtypeStruct((NP, T, PW), F32), jax.ShapeDtypeStruct((T // SCAN_C, 2, NH, HN, HN), F32)]
        + (ex.out_shape if ex else []),
        scratch_shapes=[pltpu.VMEM((NH, HN, HN), F32)] + (ex.sem_shapes if ex else []),
        compiler_params=pltpu.CompilerParams(dimension_semantics=("arbitrary",), vmem_limit_bytes=VMEM_LIMIT),
    )(r, lw, k, v, kk, a, *(ex.bufs if ex else []))
    return res[0], res[1], list(res[2:])


def _scan_bwd(r, lw, k, v, kk, a, s0s, do, ex=None, tb=128):
    T = r.shape[1]
    tb = min(tb, T)
    n_chunks = tb // SCAN_C
    nb = T // tb
    nx = ex.nb if ex else 0

    def body(*refs):
        r_ref, lw_ref, k_ref, v_ref, kk_ref, a_ref, s0_ref, do_ref = refs[:8]
        x_in, (dr, dlw, dk, dv, dkk, da), x_out = refs[8:8 + nx], refs[8 + nx:14 + nx], refs[14 + nx:14 + 2 * nx]
        ds_ref, sems = refs[14 + 2 * nx], refs[15 + 2 * nx:]

        plan = ex.schedule(nb) if ex else []

        @pl.when(pl.program_id(0) == 0)
        def _():
            ds_ref[...] = jnp.zeros_like(ds_ref)
            for at, action in plan[:1]:
                action(x_in, x_out, sems)

        def step(j, carry):
            c = n_chunks - 1 - j
            sl = pl.ds(pl.multiple_of(c * SCAN_C, SCAN_C), SCAN_C)
            inv = s0_ref[c, 1]
            _, vjp = jax.vjp(lambda *t: _chunk_fn(*t, inv=inv)[:2], s0_ref[c, 0],
                             *[_split_pairs(ref[:, sl, :]) for ref in (r_ref, lw_ref, k_ref, v_ref, kk_ref, a_ref)])
            g = vjp((_split_pairs(do_ref[:, sl, :]), ds_ref[...]))
            ds_ref[...] = g[0]
            for ref, val in zip((dr, dlw, dk, dv, dkk, da), g[1:]):
                ref[:, sl, :] = _join_pairs(val)
            return carry

        lax.fori_loop(0, n_chunks, step, 0)

        for at, action in plan[1:]:
            pl.when(pl.program_id(0) == at)(functools.partial(action, x_in, x_out, sems))

    hm = pl.BlockSpec((NP, tb, PW), lambda i: (0, nb - 1 - i, 0))
    res = pl.pallas_call(
        body, name="rwkv_scan_bwd", grid=(nb,),
        in_specs=[hm] * 6 + [pl.BlockSpec((n_chunks, 2, NH, HN, HN), lambda i: (nb - 1 - i, 0, 0, 0, 0)), hm]
        + (ex.any_specs if ex else []),
        out_specs=[hm] * 6 + (ex.any_specs if ex else []),
        out_shape=[jax.ShapeDtypeStruct((NP, T, PW), F32)] * 6 + (ex.out_shape if ex else []),
        scratch_shapes=[pltpu.VMEM((NH, HN, HN), F32)] + (ex.sem_shapes if ex else []),
        compiler_params=pltpu.CompilerParams(dimension_semantics=("arbitrary",), vmem_limit_bytes=VMEM_LIMIT),
    )(r, lw, k, v, kk, a, s0s, do, *(ex.bufs if ex else []))
    return list(res[:6]), list(res[6:])


def _shift_down(i, p, prev8):
    first = jnp.where(i > 0, prev8[7:8, :], 0.0)
    row = lax.broadcasted_iota(jnp.int32, p.shape, 0)
    return jnp.where(row == 0, first, pltpu.roll(p, 1, axis=0))


def _mix_fwd(p, sb, tm=256):
    def fn(i, n, p, prev8, sb):
        return (p * sb[0:1] + _shift_down(i, p, prev8) * sb[1:2],)
    return _rows_call("shift_mix_fwd", fn, [Rows(p), Halo(p, -1)], [sb], [("rows", p.shape[1], F32)], tm=tm,
                      with_pid=True)[0]


def _mix_bwd(dq, p, sb, tm=256):
    def fn(i, n, dq, next8, p, prev8, sb):
        ps = _shift_down(i, p, prev8)
        d1 = dq * sb[1:2]
        last = jnp.where(i < n - 1, next8[0:1, :] * sb[1:2], 0.0)
        row = lax.broadcasted_iota(jnp.int32, dq.shape, 0)
        up = jnp.where(row == dq.shape[0] - 1, last, pltpu.roll(d1, dq.shape[0] - 1, axis=0))
        return (dq * sb[0:1] + up, jnp.sum(dq * p, axis=0, keepdims=True), jnp.sum(dq * ps, axis=0, keepdims=True))
    w = p.shape[1]
    return _rows_call("shift_mix_bwd", fn, [Rows(dq), Halo(dq, +1), Rows(p), Halo(p, -1)], [sb], [("rows", w, F32)],
                      accs=[((1, w), F32), ((1, w), F32)], tm=tm, with_pid=True)


def _local_step(x, target, W, late_weights=None, early_grads=None, w_in_grads_ready=None):
    G = {}
    a = _rows_call("norm_mix_fwd", lambda x, g: (_rms(x, g),), [Rows(x)], [W["g_mix"]], [("rows", D, BF16)])[0]
    p_sgu = _matmul("proj_sgu", a, W["w_sgu_t"], "nt")
    p_rw = _matmul("proj_rwkv", a, W["w_rw_t"], "nt")
    p_gate = _matmul("proj_gate", a, W["w_gate_t"], "nt")

    sgu_consts = [W["sgu_ln_w"], W["sgu_ln_b"], W["sgu_w"], W["sgu_bt"]]
    s = _rows_call("sgu_fwd", lambda *t: (_sgu_fn(*t),), [Rows(p_sgu)], sgu_consts, [("rows", D, BF16)], tm=SGU_C)[0]

    q = _mix_fwd(p_rw, W["sb"])
    q_ins = [Rows(q, D, 0), Rows(q, D, 1), Rows(q, D, 2), Rows(q, 128, 24), Rows(q, 128, 25), Rows(q, 256, 13)]
    pre_consts = [W["w_lora"], W["w0"], W["a_lora"], W["a0"], W["g_lora"], W["k_k"], W["k_a"]]
    r_h, lw_h, k_h, v_h, kk_h, a_h, g_gate = _rows_call(
        "rwkv_pre_fwd", _pre_fn, q_ins, pre_consts, [("heads", F32)] * 6 + [("rows", D, F32)], tm=128)
    o_h, s0s, got = _scan_fwd(r_h, lw_h, k_h, v_h, kk_h, a_h, ex=late_weights[0] if late_weights else None)
    if late_weights:
        W = {**W, **late_weights[1](got)}
    y_a = _matmul("proj_a", s, W["w_proj_a"], "nn")
    post_ins = [Heads(o_h), Heads(r_h), Heads(k_h), Heads(v_h), Rows(g_gate)]
    post_consts = [W[n].reshape(NP, 1, PW) for n in ("ln_x_w", "ln_x_b", "r_k")]
    z_b = _rows_call("rwkv_post_fwd", lambda *t: (_post_fn(*t),), post_ins, post_consts, [("rows", D, BF16)], tm=128)[0]
    y_b, mixed = _matmul("proj_b", z_b, W["w_proj_b"], "nn", extras=[(p_gate, 0), (p_gate, 1), y_a],
                         epilogue=lambda yb, ga, gb, ya: (yb, _sigmoid(ga) * ya + _sigmoid(gb) * yb),
                         out_dtypes=(F32, BF16))
    gate_ins = [Rows(p_gate), Rows(y_a), Rows(y_b)]

    def res1(mo, x, g):
        h1 = x + mo
        return h1, _rms(h1, g)
    h1, f = _matmul("proj_out", mixed, W["w_out"], "nn", extras=[x, W["g_ffn"]], epilogue=res1,
                    out_dtypes=(F32, BF16))

    def relu_sq(u):
        r = jnp.maximum(u, 0.0)
        return r, r * r
    r1, act = _matmul("ffn_up", f, W["w_ffn1"], "nn", epilogue=relu_sq, out_dtypes=(BF16, BF16))
    ff = _matmul("ffn_down", act, W["w_ffn2"], "nn")

    def head(h1, ff, tgt, g):
        def f_(h1, ff, g):
            y = _rms(h1 + ff, g)
            return 0.5 * jnp.sum(jnp.mean(jnp.square(y - tgt), axis=-1))
        loss, (dh2, _, dg) = jax.value_and_grad(f_, argnums=(0, 1, 2))(h1, ff, g)
        return dh2, jnp.full((8, LANES), loss, F32), dg
    dh2, loss_acc, G["g_final"] = _rows_call("loss_head", head, [Rows(h1), Rows(ff), Rows(target)], [W["g_final"]],
                                             [("rows", D, F32)], accs=[((8, LANES), F32), ((1, D), F32)])

    d_u1 = _matmul("ffn_down_dx", dh2, W["w_ffn2"], "nt", extras=[r1], out_dtypes=(BF16,),
                   epilogue=lambda d_act, r: (d_act * 2.0 * r.astype(F32),))[0]
    G["w_ffn2"] = _matmul("ffn_down_dw", act, dh2, "tn", out_dtype=GRAD_PAYLOAD)
    d_f = _matmul("ffn_up_dx", d_u1, W["w_ffn1"], "nt")
    G["w_ffn1"] = _matmul("ffn_up_dw", f, d_u1, "tn", out_blocks=N_DEV, out_dtype=GRAD_PAYLOAD)

    def res1_bwd(h1, d_f, dh2, g):
        _, vjp = jax.vjp(_rms, h1, g)
        dh, dg = vjp(d_f)
        return dh2 + dh, dg
    dh1, G["g_ffn"] = _rows_call("residual_norm_bwd", res1_bwd, [Rows(h1), Rows(d_f), Rows(dh2)],
                                 [W["g_ffn"]], [("rows", D, F32)], accs=[((1, D), F32)])
    d_mixed = _matmul("proj_out_dx", dh1, W["w_out"], "nt")
    G["w_out"] = _matmul("proj_out_dw", mixed, dh1, "tn", out_dtype=GRAD_PAYLOAD)

    def gate_bwd(pg, ya, yb, dm):
        _, vjp = jax.vjp(_gate_fn, pg, ya, yb)
        return vjp(dm)
    d_gate, d_ya, d_yb = _rows_call("gate_bwd", gate_bwd, gate_ins + [Rows(d_mixed)], [],
                                    [("rows", 2 * D, F32), ("rows", D, BF16), ("rows", D, BF16)])

    d_s = _matmul("proj_a_dx", d_ya, W["w_proj_a"], "nt")
    G["w_proj_a"] = _matmul("proj_a_dw", s, d_ya, "tn", out_dtype=GRAD_PAYLOAD)

    def sgu_bwd(p, ds, *c):
        _, vjp = jax.vjp(_sgu_fn, p, *c)
        return vjp(ds)
    d_p_sgu, G["sgu_ln_w"], G["sgu_ln_b"], G["sgu_w"], G["sgu_bt"] = _rows_call(
        "sgu_bwd", sgu_bwd, [Rows(p_sgu), Rows(d_s)], sgu_consts, [("rows", 2 * D, F32)],
        accs=[((1, D), F32), ((1, D), F32), ((SGU_G, SGU_C, SGU_C), F32), ((SGU_C, SGU_G), F32)], tm=SGU_C)

    d_zb = _matmul("proj_b_dx", d_yb, W["w_proj_b"], "nt")
    G["w_proj_b"] = _matmul("proj_b_dw", z_b, d_yb, "tn", out_dtype=GRAD_PAYLOAD)

    def post_bwd(o, r, k2, v, g, dz, *c):
        _, vjp = jax.vjp(_post_fn, o, r, k2, v, g, *c)
        return vjp(dz)
    do_h, dr1, dk1, dv1, d_g, g_lnw, g_lnb, g_rk = _rows_call(
        "rwkv_post_bwd", post_bwd, post_ins + [Rows(d_zb)], post_consts, [("heads", F32)] * 4 + [("rows", D, F32)],
        accs=[((NP, 1, PW), F32)] * 3, tm=128)
    G["ln_x_w"], G["ln_x_b"], G["r_k"] = (t.reshape(1, D) for t in (g_lnw, g_lnb, g_rk))
    (dr2, dlw, dk2, dv2, dkk, daa), early = _scan_bwd(r_h, lw_h, k_h, v_h, kk_h, a_h, s0s, do_h,
                                                      ex=early_grads(G) if early_grads else None)

    def pre_bwd(qr, qk, qv, qxw, qxa, qxg, dr1, dr2, dlw, dk1, dk2, dv1, dv2, dkk, daa, dg, *c):
        _, vjp = jax.vjp(_pre_fn, qr, qk, qv, qxw, qxa, qxg, *c)
        g = vjp((dr1 + dr2, dlw, dk1 + dk2, dv1 + dv2, dkk, daa, dg))
        dq = jnp.concatenate(g[:6], axis=-1)
        return (dq,) + tuple(g[6:])
    pre_b_ins = q_ins + [Heads(dr1), Heads(dr2), Heads(dlw), Heads(dk1), Heads(dk2), Heads(dv1), Heads(dv2),
                         Heads(dkk), Heads(daa), Rows(d_g)]
    d_q, G["w_lora"], G["w0"], G["a_lora"], G["a0"], G["g_lora"], G["k_k"], G["k_a"] = _rows_call(
        "rwkv_pre_bwd", pre_bwd, pre_b_ins, pre_consts, [("rows", RW_INT, F32)],
        accs=[((128, D), F32), ((1, D), F32), ((128, D), F32), ((1, D), F32), ((256, D), F32), ((1, D), F32),
              ((1, D), F32)], tm=128)
    d_p_rw, dsb0, dsb1 = _mix_bwd(d_q, p_rw, W["sb"])
    G["sb"] = jnp.concatenate([dsb0, dsb1], axis=0)

    G["w_sgu_t"] = _matmul("proj_sgu_dw", d_p_sgu, a, "tn", out_dtype=GRAD_PAYLOAD)
    G["w_rw_t"] = _matmul("proj_rwkv_dw", d_p_rw, a, "tn", out_dtype=GRAD_PAYLOAD)
    G["w_gate_t"] = _matmul("proj_gate_dw", d_gate, a, "tn", out_dtype=GRAD_PAYLOAD)
    token = w_in_grads_ready(G) if w_in_grads_ready else None
    da1 = _matmul("proj_sgu_dx", d_p_sgu, W["w_sgu_t"], "nn", after=token)
    da2 = _matmul("proj_rwkv_dx", d_p_rw, W["w_rw_t"], "nn", after=token)
    da3 = _matmul("proj_gate_dx", d_gate, W["w_gate_t"], "nn", after=token)

    def norm1_bwd(x, da1, da2, da3, dh1, g):
        _, vjp = jax.vjp(_rms, x, g)
        dx, dg = vjp(da1 + da2 + da3)
        return dh1 + dx, dg
    dx, G["g_mix"] = _rows_call("norm_mix_bwd", norm1_bwd, [Rows(x), Rows(da1), Rows(da2), Rows(da3), Rows(dh1)],
                                [W["g_mix"]], [("rows", D, F32)], accs=[((1, D), F32)])
    return loss_acc[0, 0], dx, G, early


class Exchange:
    def __init__(self, bufs, gathers):
        self.bufs, self.gathers, self.nb = list(bufs), list(gathers), len(bufs)
        self.any_specs = [pl.BlockSpec(memory_space=pl.ANY)] * self.nb
        self.out_shape = [jax.ShapeDtypeStruct((N_DEV,) + (b.shape if g else b.shape[1:]), b.dtype)
                          for b, g in zip(self.bufs, self.gathers)]
        n = (N_DEV - 1) * self.nb
        self.sem_shapes = [pltpu.SemaphoreType.DMA((n,)), pltpu.SemaphoreType.DMA((n,)),
                           pltpu.SemaphoreType.DMA((self.nb,))]

    def _copies(self, in_refs, out_refs, sems):
        send_sems, recv_sems, local_sems = sems
        x, y, c = lax.axis_index("x"), lax.axis_index("y"), lax.axis_index("c")
        me = 4 * x + 2 * y + c

        def src(b, dest):
            return in_refs[b] if self.gathers[b] else in_refs[b].at[dest]

        local = [pltpu.make_async_copy(src(b, me), out_refs[b].at[me], local_sems.at[b]) for b in range(self.nb)]
        sends, recvs = [], []
        for kbits in range(1, N_DEV):
            px = 1 - x if kbits & 4 else x
            py = 1 - y if kbits & 2 else y
            pc = 1 - c if kbits & 1 else c
            peer = 4 * px + 2 * py + pc
            for b in range(self.nb):
                s = (kbits - 1) * self.nb + b
                sends.append(pltpu.make_async_remote_copy(
                    src_ref=src(b, peer), dst_ref=out_refs[b].at[me], send_sem=send_sems.at[s],
                    recv_sem=recv_sems.at[s], device_id=(px, py, pc), device_id_type=pl.DeviceIdType.MESH))
                recvs.append(pltpu.make_async_remote_copy(
                    src_ref=src(b, peer), dst_ref=out_refs[b].at[peer], send_sem=send_sems.at[s],
                    recv_sem=recv_sems.at[s], device_id=(px, py, pc), device_id_type=pl.DeviceIdType.MESH))
        return local, sends, recvs

    def start(self, in_refs, out_refs, sems):
        local, sends, _ = self._copies(in_refs, out_refs, sems)
        for cp in sends + local:
            cp.start()

    def wait(self, in_refs, out_refs, sems):
        local, sends, recvs = self._copies(in_refs, out_refs, sems)
        for cp in recvs:
            cp.wait_recv()
        for cp in sends:
            cp.wait_send()
        for cp in local:
            cp.wait()

    def schedule(self, n_steps):
        return [(0, self.start), (n_steps - 1, self.wait)]


def _exchange(name, bufs, gather):
    ex = Exchange(bufs, gather if isinstance(gather, (list, tuple)) else [gather] * len(bufs))

    def body(*refs):
        in_refs, out_refs, sems = refs[:ex.nb], refs[ex.nb:2 * ex.nb], refs[2 * ex.nb:]
        ex.start(in_refs, out_refs, sems)
        ex.wait(in_refs, out_refs, sems)

    return pl.pallas_call(body, name=name, in_specs=ex.any_specs, out_specs=ex.any_specs, out_shape=ex.out_shape,
                          scratch_shapes=ex.sem_shapes)(*ex.bufs)


N_CHIP = 4


def _pair_exchange(name, blocks):
    def body(b_ref, got_ref, send_sems, recv_sems):
        x, y, c = lax.axis_index("x"), lax.axis_index("y"), lax.axis_index("c")
        copies = [pltpu.make_async_remote_copy(
            src_ref=b_ref.at[2 * q + 1 - c], dst_ref=got_ref.at[q], send_sem=send_sems.at[q],
            recv_sem=recv_sems.at[q], device_id=(x, y, 1 - c), device_id_type=pl.DeviceIdType.MESH)
            for q in range(N_CHIP)]
        for cp in copies:
            cp.start()
        for cp in copies:
            cp.wait_recv()
        for cp in copies:
            cp.wait_send()

    any_spec = pl.BlockSpec(memory_space=pl.ANY)
    return pl.pallas_call(
        body, name=name, in_specs=[any_spec], out_specs=any_spec,
        out_shape=jax.ShapeDtypeStruct((N_CHIP,) + blocks.shape[1:], blocks.dtype),
        scratch_shapes=[pltpu.SemaphoreType.DMA((N_CHIP,))] * 2,
    )(blocks)


def _pair_sum(name, blocks, got, core):
    _, R, Wd = blocks.shape

    def body(c_ref, a_ref, b_ref, o_ref):
        o_ref[...] = (a_ref[...].astype(F32) + b_ref[...].astype(F32)).astype(o_ref.dtype)

    return pl.pallas_call(
        body, name=name,
        grid_spec=pltpu.PrefetchScalarGridSpec(
            num_scalar_prefetch=1, grid=(N_CHIP,),
            in_specs=[pl.BlockSpec((None, R, Wd), lambda q, c_ref: (2 * q + c_ref[0], 0, 0)),
                      pl.BlockSpec((None, R, Wd), lambda q, c_ref: (q, 0, 0))],
            out_specs=pl.BlockSpec((None, R, Wd), lambda q, c_ref: (q, 0, 0))),
        out_shape=jax.ShapeDtypeStruct(got.shape, blocks.dtype),
        compiler_params=pltpu.CompilerParams(dimension_semantics=("parallel",), vmem_limit_bytes=VMEM_LIMIT),
    )(core, blocks, got)


def _chip_copies(s_ref, land_ref, send_sems, recv_sems):
    x, y, c = lax.axis_index("x"), lax.axis_index("y"), lax.axis_index("c")
    my_q = 2 * x + y
    sends, recvs = [], []
    for kbits in range(1, N_CHIP):
        px = 1 - x if kbits & 2 else x
        py = 1 - y if kbits & 1 else y
        peer_q = 2 * px + py
        sends.append(pltpu.make_async_remote_copy(
            src_ref=s_ref.at[peer_q], dst_ref=land_ref.at[my_q], send_sem=send_sems[kbits - 1],
            recv_sem=recv_sems[kbits - 1], device_id=(px, py, c), device_id_type=pl.DeviceIdType.MESH))
        recvs.append(pltpu.make_async_remote_copy(
            src_ref=s_ref.at[peer_q], dst_ref=land_ref.at[peer_q], send_sem=send_sems[kbits - 1],
            recv_sem=recv_sems[kbits - 1], device_id=(px, py, c), device_id_type=pl.DeviceIdType.MESH))
    return sends, recvs


_HBM = pl.BlockSpec(memory_space=pltpu.HBM)
_SEM = pl.BlockSpec(memory_space=pltpu.SEMAPHORE)
N_CHIP_SEMS = 2 * (N_CHIP - 1)


def _chip_exchange_start(name, sums):
    def body(s_ref, land_ref, *outs):
        sems, token = outs[:N_CHIP_SEMS], outs[N_CHIP_SEMS + 2]
        sends, _ = _chip_copies(s_ref, land_ref, sems[:N_CHIP - 1], sems[N_CHIP - 1:])
        for cp in sends:
            cp.start()
        token[...] = jnp.zeros_like(token)

    res = pl.pallas_call(
        body, name=name, in_specs=(_HBM, _HBM),
        out_specs=(_SEM,) * N_CHIP_SEMS + (_HBM, _HBM, pl.BlockSpec(memory_space=pltpu.VMEM)),
        out_shape=(pltpu.SemaphoreType.DMA(()),) * N_CHIP_SEMS
        + (pltpu.HBM(sums.shape, sums.dtype), pltpu.HBM(sums.shape, sums.dtype), jax.ShapeDtypeStruct((8, LANES), F32)),
        input_output_aliases={0: N_CHIP_SEMS, 1: N_CHIP_SEMS + 1},
        compiler_params=pltpu.CompilerParams(has_side_effects=pltpu.SideEffectType.DATAFLOW_SIDE_EFFECTING),
    )(pltpu.with_memory_space_constraint(sums, pltpu.HBM),
      pltpu.with_memory_space_constraint(lax.empty(sums.shape, sums.dtype), pltpu.HBM))
    return res[:N_CHIP_SEMS], res[N_CHIP_SEMS], res[N_CHIP_SEMS + 1], res[N_CHIP_SEMS + 2]


def _chip_exchange_wait(name, sems, sums_thru, land_thru, after):
    def body(s_ref, land_ref, *rest):
        sems = rest[:N_CHIP_SEMS]
        sends, recvs = _chip_copies(s_ref, land_ref, sems[:N_CHIP - 1], sems[N_CHIP - 1:])
        for cp in sends:
            cp.wait_send()
        for cp in recvs:
            cp.wait_recv()

    return pl.pallas_call(
        body, name=name, in_specs=(_HBM, _HBM) + (_SEM,) * N_CHIP_SEMS + (pl.BlockSpec(memory_space=pl.ANY),),
        out_specs=(_HBM, _HBM),
        out_shape=(pltpu.HBM(sums_thru.shape, sums_thru.dtype), pltpu.HBM(sums_thru.shape, sums_thru.dtype)),
        input_output_aliases={0: 0, 1: 1},
        compiler_params=pltpu.CompilerParams(has_side_effects=pltpu.SideEffectType.DATAFLOW_SIDE_EFFECTING),
    )(sums_thru, land_thru, *sems, after)


def _adamw(name, slots, w, m, v, tr=256):
    unit_mid = w.ndim == 3 and w.shape[1] == 1 and w.shape[0] > 1
    R, Wd = (w.shape[0], w.shape[2]) if unit_mid else w.shape[-2:]
    depth_axis = w.ndim == 3 and not unit_mid
    if R % tr == 0:
        tc = Wd
    else:
        tr, tc = R, (256 if (Wd % 256 == 0 and R > 256) else Wd)
    at = (slice(None), 0, slice(None)) if unit_mid else Ellipsis

    def body(s_ref, w_ref, m_ref, v_ref, g_out, d_out, m_out, v_out):
        g = s_ref[0].astype(F32)
        for j in range(1, slots.shape[0]):
            g = g + s_ref[j].astype(F32)
        m_new = ADAM_B1 * m_ref[at] + (1.0 - ADAM_B1) * g
        v_new = ADAM_B2 * v_ref[at] + (1.0 - ADAM_B2) * jnp.square(g)
        m_hat = m_new / (1.0 - ADAM_B1 ** ADAM_STEP)
        v_hat = v_new / (1.0 - ADAM_B2 ** ADAM_STEP)
        g_out[at] = g
        d_out[at] = -ADAM_LR * (m_hat / (jnp.sqrt(v_hat) + ADAM_EPS) + ADAM_WD * w_ref[at])
        m_out[at] = m_new
        v_out[at] = v_new

    if unit_mid:
        row = pl.BlockSpec((tr, 1, tc), lambda i, j: (i, 0, j))
    elif depth_axis:
        row = pl.BlockSpec((None, tr, tc), lambda i, j: (0, i, j))
    else:
        row = pl.BlockSpec((tr, tc), lambda i, j: (i, j))
    return pl.pallas_call(
        body, name=name, grid=(R // tr, Wd // tc),
        in_specs=[pl.BlockSpec((slots.shape[0], tr, tc), lambda i, j: (0, i, j)), row, row, row],
        out_specs=[row] * 4, out_shape=[jax.ShapeDtypeStruct(w.shape, F32)] * 4,
        compiler_params=pltpu.CompilerParams(dimension_semantics=("parallel", "parallel"),
                                             vmem_limit_bytes=VMEM_LIMIT),
    )(slots, w, m, v)


PACK_W = 1024
PACKED = [(n, s) for n, s in REPLICATED if n != "sgu_w"]
_SMALL_SIZES = [int(np.prod(s)) for _, s in PACKED]
_SMALL_ROWS = _round_up(_round_up(sum(_SMALL_SIZES) + PACK_W, PACK_W) // PACK_W, 8)
_LOSS_AT = sum(_SMALL_SIZES)
W_IN_SHARD = P_TOTAL // N_DEV


def _pack_rows(parts, rows, dtype):
    flat = jnp.concatenate([p.reshape(-1).astype(dtype) for p in parts])
    return jnp.pad(flat, (0, rows * PACK_W - flat.shape[0])).reshape(rows, PACK_W)


def _w_in_groups_t(blocks):
    wt = blocks.reshape(P_TOTAL, D)
    o, c = 2 * D, 2 * D + 3 * D
    z = lambda r: jnp.zeros((r, D), wt.dtype)
    rw = jnp.concatenate([wt[o:c], wt[c:c + L_W], z(128 - L_W), wt[c + L_W:c + L_W + L_A], z(128 - L_A),
                          wt[c + L_W + L_A:o + C_B], z(256 - L_G)], axis=0)
    return wt[:o], rw, wt[o + C_B:]


def _w_in_grad_blocks(g_sgu_t, g_rw_t, g_gate_t):
    c = 3 * D
    full = jnp.concatenate([g_sgu_t, g_rw_t[:c], g_rw_t[c:c + L_W], g_rw_t[c + 128:c + 128 + L_A],
                            g_rw_t[c + 256:c + 256 + L_G], g_gate_t], axis=0)
    return full.reshape(N_DEV, W_IN_SHARD, D)


def _mesh_index():
    me = 4 * lax.axis_index("x") + 2 * lax.axis_index("y") + lax.axis_index("c")
    return me.astype(jnp.int32).reshape(1)


def _fill_slot(name, dst, src, idx, src_idx=None):
    R, Wd = dst.shape[1:]
    scalars = [idx] if src_idx is None else [idx, src_idx]
    if src_idx is None:
        src_spec = pl.BlockSpec((R, Wd), lambda i, *s: (0, 0))
    else:
        src_spec = pl.BlockSpec((None, R, Wd), lambda i, *s: (s[1][0], 0, 0))

    def body(*refs):
        src_ref, out_ref = refs[len(scalars) + 1], refs[len(scalars) + 2]
        out_ref[...] = src_ref[...]

    return pl.pallas_call(
        body, name=name,
        grid_spec=pltpu.PrefetchScalarGridSpec(
            num_scalar_prefetch=len(scalars), grid=(1,),
            in_specs=[pl.BlockSpec(memory_space=pl.ANY), src_spec],
            out_specs=pl.BlockSpec((None, R, Wd), lambda i, *s: (s[0][0], 0, 0))),
        out_shape=jax.ShapeDtypeStruct(dst.shape, dst.dtype),
        input_output_aliases={len(scalars): 0},
        compiler_params=pltpu.CompilerParams(vmem_limit_bytes=VMEM_LIMIT),
    )(*scalars, dst, src)


class TwoLevelGather:
    def __init__(self, bufs, skip_own=()):
        self.bufs, self.nb, self.skip_own = list(bufs), len(bufs), tuple(skip_own)
        self.any_specs = [pl.BlockSpec(memory_space=pl.ANY)] * self.nb
        self.out_shape = [jax.ShapeDtypeStruct((N_DEV,) + b.shape, b.dtype) for b in self.bufs]
        self.sem_shapes = [pltpu.SemaphoreType.DMA((7 * self.nb,)), pltpu.SemaphoreType.DMA((7 * self.nb,)),
                           pltpu.SemaphoreType.DMA((self.nb,))]

    def _copies(self, in_refs, out_refs, sems):
        send_sems, recv_sems, local_sems = sems
        nb = self.nb
        x, y, c = lax.axis_index("x"), lax.axis_index("y"), lax.axis_index("c")
        me, sibling = (x, y, c), (x, y, 1 - c)
        chips = [(1 - x, y), (x, 1 - y), (1 - x, 1 - y)]

        def slot(b, dev):
            return out_refs[b].at[4 * dev[0] + 2 * dev[1] + dev[2]]

        def copy(b, k, block, to, own=False):
            return pltpu.make_async_remote_copy(
                src_ref=in_refs[b] if own else slot(b, block), dst_ref=slot(b, block),
                send_sem=send_sems.at[7 * b + k], recv_sem=recv_sems.at[7 * b + k], device_id=to,
                device_id_type=pl.DeviceIdType.MESH)

        cp = {}
        cp["local"] = [pltpu.make_async_copy(in_refs[b], slot(b, me), local_sems.at[b]) for b in range(nb)
                       if b not in self.skip_own]
        cp["first"] = [copy(b, 0, me, sibling, own=True) for b in range(nb)]
        cp["first"] += [copy(b, 1 + j, me, (*chip, c), own=True) for j, chip in enumerate(chips) for b in range(nb)]
        cp["over_ici"] = [copy(b, 1 + j, (*chip, c), me) for j, chip in enumerate(chips) for b in range(nb)]
        cp["passed"] = [copy(b, 4 + j, (*chip, c), sibling) for j, chip in enumerate(chips) for b in range(nb)]
        cp["from_sibling"] = [copy(b, 0, sibling, me) for b in range(nb)]
        cp["from_sibling"] += [copy(b, 4 + j, (*chip, 1 - c), me) for j, chip in enumerate(chips) for b in range(nb)]
        return cp

    def start(self, in_refs, out_refs, sems):
        cp = self._copies(in_refs, out_refs, sems)
        for c in cp["first"] + cp["local"]:
            c.start()

    def forward(self, in_refs, out_refs, sems):
        cp = self._copies(in_refs, out_refs, sems)
        for arrived, onward in zip(cp["over_ici"], cp["passed"]):
            arrived.wait_recv()
            onward.start()

    def finish(self, in_refs, out_refs, sems):
        cp = self._copies(in_refs, out_refs, sems)
        for c in cp["from_sibling"]:
            c.wait_recv()
        for c in cp["first"] + cp["passed"]:
            c.wait_send()
        for c in cp["local"]:
            c.wait()

    def schedule(self, n_steps):
        return [(0, self.start), (max(n_steps - 3, 0), self.forward), (n_steps - 1, self.finish)]


def _all_gather_two_level(name, bufs, skip_own=()):
    ex = TwoLevelGather(bufs, skip_own)

    def body(*refs):
        args = refs[:ex.nb], refs[ex.nb:2 * ex.nb], refs[2 * ex.nb:]
        ex.start(*args)
        ex.forward(*args)
        ex.finish(*args)

    return pl.pallas_call(body, name=name, in_specs=ex.any_specs, out_specs=ex.any_specs, out_shape=ex.out_shape,
                          scratch_shapes=ex.sem_shapes)(*ex.bufs)


def _cols_from_blocks(blk):
    return jnp.transpose(blk, (1, 0, 2)).reshape(blk.shape[1], -1)


def _cols_to_blocks(g):
    r, c = g.shape
    return jnp.transpose(g.reshape(r, N_DEV, c // N_DEV), (1, 0, 2))


FIRST_WEIGHTS = ["w_in", "shift_b", "w_lora_w", "a_lora_w", "g_lora_w"]
LATE_WEIGHTS = ["w_proj_a", "w_proj_b", "w_out", "w_ffn1", "w_ffn2"]


def _late_weights(shards):
    ex = TwoLevelGather([shards[n].astype(BF16) for n in LATE_WEIGHTS])

    def finish(results):
        got = dict(zip(LATE_WEIGHTS, results))
        W = {n: got[n].reshape(-1, D) for n in ("w_proj_a", "w_proj_b", "w_out", "w_ffn2")}
        W["w_ffn1"] = got["w_ffn1"].reshape(N_DEV, D, -1)
        return W
    return ex, finish


def _gather_weights(shards):
    def payload(n):
        if n == "w_in":
            return jnp.transpose(shards[n][0]).astype(BF16)
        return shards[n] if n == "shift_b" else shards[n].astype(BF16)
    payloads = [payload(n) for n in FIRST_WEIGHTS]
    got = dict(zip(FIRST_WEIGHTS, _all_gather_two_level("weight_all_gather", payloads, skip_own=(0,))))
    got["w_in"] = _fill_slot("w_in_own_slot", got["w_in"], payloads[0], _mesh_index())
    W = {}
    W["w_sgu_t"], W["w_rw_t"], W["w_gate_t"] = _w_in_groups_t(got["w_in"])
    z = lambda r, c, dt: jnp.zeros((r, c), dt)
    W["w_lora"] = jnp.concatenate([_cols_from_blocks(got["w_lora_w"][:, 0]).astype(F32), z(128 - L_W, D, F32)], axis=0)
    W["a_lora"] = jnp.concatenate([_cols_from_blocks(got["a_lora_w"][:, 0]).astype(F32), z(128 - L_A, D, F32)], axis=0)
    W["g_lora"] = jnp.concatenate([_cols_from_blocks(got["g_lora_w"][:, 0]).astype(F32), z(256 - L_G, D, F32)], axis=0)
    sb = _cols_from_blocks(got["shift_b"][:, 0])
    W["sb"] = jnp.concatenate([sb[:, :3 * D], sb[:, 3 * D:3 * D + L_W], z(2, 128 - L_W, F32),
                               sb[:, 3 * D + L_W:3 * D + L_W + L_A], z(2, 128 - L_A, F32),
                               sb[:, 3 * D + L_W + L_A:], z(2, 256 - L_G, F32)], axis=1)
    return W


def _replicated_weights(rep):
    W = {n: rep[n] for n in ("g_mix", "sgu_ln_w", "sgu_ln_b", "w0", "a0", "k_k", "k_a", "r_k", "ln_x_w", "ln_x_b",
                             "g_ffn")}
    W["g_final"] = rep["g_final"].reshape(1, D)
    W["sgu_w"] = rep["sgu_w"][0]
    W["sgu_bt"] = jnp.transpose(rep["sgu_b"][0])
    return W


def _late_grad_blocks(G):
    blocks = {n: G[n].reshape(N_DEV, -1, D) for n in ("w_proj_a", "w_proj_b", "w_out", "w_ffn2")}
    blocks["w_ffn1"] = G["w_ffn1"]
    return Exchange([blocks[n] for n in LATE_WEIGHTS] + [G["sgu_w"].reshape(SGU_G * SGU_C, SGU_C).astype(GRAD_PAYLOAD)],
                    [False] * len(LATE_WEIGHTS) + [True])


def _first_grad_blocks(G):
    sbg = G["sb"]
    c = 3 * D
    sb = jnp.concatenate([sbg[:, :c], sbg[:, c:c + L_W], sbg[:, c + 128:c + 128 + L_A],
                          sbg[:, c + 256:c + 256 + L_G]], axis=1)
    return {
        "shift_b": _cols_to_blocks(sb),
        "w_lora_w": _cols_to_blocks(G["w_lora"][:L_W]), "a_lora_w": _cols_to_blocks(G["a_lora"][:L_A]),
        "g_lora_w": _cols_to_blocks(G["g_lora"][:L_G]),
    }


def _replicated_grads(G):
    small = {n: G[n] for n in ("g_mix", "sgu_ln_w", "sgu_ln_b", "w0", "a0", "k_k", "k_a", "r_k", "ln_x_w", "ln_x_b",
                               "g_ffn", "g_final")}
    small["sgu_w"] = G["sgu_w"]
    small["sgu_b"] = jnp.transpose(G["sgu_bt"])
    return small


def kernel(x, g_mix, w_in, sgu_ln_w, sgu_ln_b, sgu_w, sgu_b, w_proj_a, shift_b, w_lora_w, w0, a_lora_w, a0, g_lora_w, k_k, k_a, r_k, ln_x_w, ln_x_b, w_proj_b, w_out, g_ffn, w_ffn1, w_ffn2, g_final, loss_target, m_g_mix, m_w_in, m_sgu_ln_w, m_sgu_ln_b, m_sgu_w, m_sgu_b, m_w_proj_a, m_shift_b, m_w_lora_w, m_w0, m_a_lora_w, m_a0, m_g_lora_w, m_k_k, m_k_a, m_r_k, m_ln_x_w, m_ln_x_b, m_w_proj_b, m_w_out, m_g_ffn, m_w_ffn1, m_w_ffn2, m_g_final, v_g_mix, v_w_in, v_sgu_ln_w, v_sgu_ln_b, v_sgu_w, v_sgu_b, v_w_proj_a, v_shift_b, v_w_lora_w, v_w0, v_a_lora_w, v_a0, v_g_lora_w, v_k_k, v_k_a, v_r_k, v_ln_x_w, v_ln_x_b, v_w_proj_b, v_w_out, v_g_ffn, v_w_ffn1, v_w_ffn2, v_g_final):
    env = dict(locals())
    weights = {n: env[n] for n in WEIGHT_ORDER}
    moms = {n: env["m_" + n] for n in WEIGHT_ORDER}
    vars_ = {n: env["v_" + n] for n in WEIGHT_ORDER}

    shards = {n: weights[n] for n, _, _ in SHARDED}
    W = _gather_weights(shards)
    W.update(_replicated_weights({n: weights[n] for n, _ in REPLICATED}))
    in_flight = {}

    def send_w_in_grads(G):
        blocks = _w_in_grad_blocks(G["w_sgu_t"], G["w_rw_t"], G["w_gate_t"])
        got = _pair_exchange("grad_pair_exchange", blocks)
        core = lax.axis_index("c").astype(jnp.int32).reshape(1)
        sums = _pair_sum("grad_pair_sum", blocks, got, core)
        in_flight["sems"], in_flight["sums"], in_flight["land"], token = _chip_exchange_start("grad_chip_start", sums)
        return token

    loss_part, dx, G, late_slots = _local_step(x[0], loss_target[0], W, late_weights=_late_weights(shards),
                                               early_grads=_late_grad_blocks, w_in_grads_ready=send_w_in_grads)

    slots = dict(zip(LATE_WEIGHTS, late_slots))
    blocks = _first_grad_blocks(G)
    small = _replicated_grads(G)
    small_parts = [small[n] for n, _ in PACKED] + [jnp.full((PACK_W,), loss_part, F32)]
    rest = [n for n in FIRST_WEIGHTS if n != "w_in"]
    res = _exchange("grad_exchange", [blocks[n] for n in rest] + [_pack_rows(small_parts, _SMALL_ROWS, F32)],
                    [False] * len(rest) + [True])
    slots.update(zip(rest, res[:-1]))
    small_slots = res[-1]
    sums, chip_slots = _chip_exchange_wait("grad_chip_wait", in_flight["sems"], in_flight["sums"], in_flight["land"],
                                           after=small_slots)
    my_chip = (2 * lax.axis_index("x") + lax.axis_index("y")).astype(jnp.int32).reshape(1)
    slots["w_in"] = _fill_slot("grad_own_slot", chip_slots, sums, my_chip, src_idx=my_chip)

    outs = [dict(), dict(), dict(), dict()]
    for n, _, _ in SHARDED:
        if n == "w_in":
            res = _adamw("adamw_" + n, slots[n], *[jnp.transpose(t, (2, 0, 1)) for t in (weights[n], moms[n], vars_[n])])
            res = [jnp.transpose(t, (1, 2, 0)) for t in res]
        else:
            res = _adamw("adamw_" + n, slots[n], weights[n], moms[n], vars_[n])
        for k in range(4):
            outs[k][n] = res[k]

    sgu_shape = (SGU_G * SGU_C, SGU_C)
    res = _adamw("adamw_sgu_w", late_slots[len(LATE_WEIGHTS)], *[t.reshape(sgu_shape) for t in
                                                                  (weights["sgu_w"], moms["sgu_w"], vars_["sgu_w"])])
    for k in range(4):
        outs[k]["sgu_w"] = res[k].reshape(weights["sgu_w"].shape)

    def packed(d):
        return _pack_rows([d[n] for n, _ in PACKED], _SMALL_ROWS, F32)
    small_out = _adamw("adamw_replicated", small_slots, packed(weights), packed(moms), packed(vars_))
    for k in range(4):
        flat = small_out[k].reshape(-1)
        off = 0
        for (n, s), size in zip(PACKED, _SMALL_SIZES):
            outs[k][n] = flat[off:off + size].reshape(s)
            off += size
    loss = small_out[0].reshape(-1)[_LOSS_AT]
    return (loss, dx[None], *[outs[0][n] for n in WEIGHT_ORDER], *[outs[1][n] for n in WEIGHT_ORDER],
            *[outs[2][n] for n in WEIGHT_ORDER], *[outs[3][n] for n in WEIGHT_ORDER])
```

```python
import functools
import numpy as np
import jax
import jax.numpy as jnp
from jax import lax
from jax.experimental import pallas as pl
from jax.experimental.pallas import tpu as pltpu

F32 = jnp.float32
BF16 = jnp.bfloat16

D = 1024
NH, HN = 16, 64
NP, PW = NH // 2, 2 * HN
SGU_G, SGU_C = 8, 128
L_W, L_A, L_G = 64, 64, 160
C_B = 3 * D + L_W + L_A + L_G
P_TOTAL = 2 * D + C_B + 2 * D
D_FF = 4 * D
RW_INT = 3 * D + 128 + 128 + 256
NORM_EPS, LN_EPS, GN_EPS = 1e-6, 1e-5, 64e-5
N_DEV = 8
LANES = 128
SCAN_C = 64
SOLVE_B = 16
SCAN_PRECISION = lax.Precision.HIGH
SCAN_OUT_PRECISION = lax.Precision.DEFAULT
GRAD_PAYLOAD = BF16
VMEM_LIMIT = 56 * 1024 * 1024
MATMUL_VMEM_BUDGET = 40 * 1024 * 1024
STEP_COST_BYTES = 512 * 1024
HBM_COST_RATIO = 3

ADAM_LR, ADAM_B1, ADAM_B2, ADAM_EPS, ADAM_WD, ADAM_STEP = 0.001, 0.9, 0.999, 1e-08, 0.01, 10

SHARDED = [
    ("w_in", (D, P_TOTAL), 1), ("w_proj_a", (D, D), 0), ("shift_b", (2, C_B), 1), ("w_lora_w", (L_W, D), 1),
    ("a_lora_w", (L_A, D), 1), ("g_lora_w", (L_G, D), 1), ("w_proj_b", (D, D), 0), ("w_out", (D, D), 0),
    ("w_ffn1", (D, D_FF), 1), ("w_ffn2", (D_FF, D), 0),
]
REPLICATED = [
    ("g_mix", (1, D)), ("sgu_ln_w", (1, D)), ("sgu_ln_b", (1, D)), ("sgu_w", (1, SGU_G, SGU_C, SGU_C)),
    ("sgu_b", (1, SGU_G, SGU_C)), ("w0", (1, D)), ("a0", (1, D)), ("k_k", (1, D)), ("k_a", (1, D)), ("r_k", (1, D)),
    ("ln_x_w", (1, D)), ("ln_x_b", (1, D)), ("g_ffn", (1, D)), ("g_final", (D,)),
]
WEIGHT_ORDER = ["g_mix", "w_in", "sgu_ln_w", "sgu_ln_b", "sgu_w", "sgu_b", "w_proj_a", "shift_b", "w_lora_w", "w0",
                "a_lora_w", "a0", "g_lora_w", "k_k", "k_a", "r_k", "ln_x_w", "ln_x_b", "w_proj_b", "w_out", "g_ffn",
                "w_ffn1", "w_ffn2", "g_final"]


def _shard_shape(shape, axis):
    s = list(shape)
    s[axis] //= N_DEV
    return tuple(s)


def _round_up(n, m):
    return (n + m - 1) // m * m


def _pick(n, target):
    if n <= target:
        return n
    best = None
    for t in range(LANES, target + 1, LANES):
        if n % t == 0:
            best = t
    assert best is not None, (n, target)
    return best


def _matmul(name, a, b, mode, out_dtype=F32, tm=2048, tn=1024, tk=2048, out_blocks=None, epilogue=None, extras=(),
            out_dtypes=(), after=None, whole_rows=False):
    b_blocks = b.shape[0] if b.ndim == 3 else None
    bshape = b.shape if b.ndim == 2 else (b.shape[1], b.shape[0] * b.shape[2])
    if mode == "nn":
        (M, K), (K2, N) = a.shape, bshape
    elif mode == "nt":
        (M, K), (N, K2) = a.shape, bshape
    else:
        (K, M), (K2, N) = a.shape, bshape
    assert K == K2, (name, a.shape, b.shape)
    assert b_blocks is None or mode != "tn"
    assert out_blocks is None or mode == "tn"
    tn = min(tn, N // (out_blocks or 1), bshape[1] // b_blocks if (b_blocks and mode == "nn") else tn)
    tk = min(tk, bshape[1] // b_blocks if (b_blocks and mode == "nt") else tk)
    tm, tn, tk = _pick(M, tm), _pick(N, tn), _pick(K, tk)

    def vmem_bytes(tm, tk):
        tiles = tm * tk * a.dtype.itemsize + tk * tn * b.dtype.itemsize
        for dt in (out_dtypes if epilogue else (out_dtype,)):
            tiles += tm * tn * jnp.dtype(dt).itemsize
        for x in extras:
            arr = x[0] if isinstance(x, tuple) else x
            tiles += (tm if arr.shape[0] > 1 else 1) * tn * arr.dtype.itemsize
        return 2 * tiles + (tm * tn * 4 if K // tk > 1 else 0)

    def cost(tm, tk):
        ni, nj, nk = M // tm, N // tn, K // tk
        steps = ni * nj * nk
        acc_passes = steps * tm * tn * 8 if nk > 1 else 0
        a_reads = M * K * a.dtype.itemsize * (nj if nk > 1 else 1)
        b_reads = K * N * b.dtype.itemsize * (ni if (nj > 1 or nk > 1) else 1)
        return (steps * STEP_COST_BYTES + acc_passes + vmem_bytes(tm, tk) // 2
                + HBM_COST_RATIO * (a_reads + b_reads))

    options = [(m, k) for m in ({M} if whole_rows else {_pick(M, max(t, LANES)) for t in (tm, tm // 2, tm // 4)})
               for k in {_pick(K, max(t, LANES)) for t in (tk, tk // 2, tk // 4)}
               if vmem_bytes(m, k) <= MATMUL_VMEM_BUDGET]
    tm, tk = min(options, key=lambda o: cost(*o))
    nk = K // tk
    dims = {"nn": (((1,), (0,)), ((), ())), "nt": (((1,), (1,)), ((), ())), "tn": (((0,), (0,)), ((), ()))}[mode]

    n_x, n_o = len(extras), len(out_dtypes) if epilogue else 1
    n_after = 0 if after is None else 1

    def body(a_ref, b_ref, *rest):
        x_refs, o_refs, acc = rest[:n_x], rest[n_x + n_after:n_x + n_after + n_o], rest[n_x + n_after + n_o:]
        part = lax.dot_general(a_ref[...].astype(BF16), b_ref[...].astype(BF16), dims, preferred_element_type=F32)

        def finish(res):
            outs = epilogue(res, *[r[...] for r in x_refs]) if epilogue else (res,)
            for r, v in zip(o_refs, outs):
                r[...] = v.astype(r.dtype)

        if nk == 1:
            finish(part)
            return
        acc_ref, k = acc[0], pl.program_id(2)

        @pl.when(k == 0)
        def _():
            acc_ref[...] = part

        @pl.when(k > 0)
        def _():
            acc_ref[...] += part

        @pl.when(k == nk - 1)
        def _():
            finish(acc_ref[...])

    a_spec = {"nn": pl.BlockSpec((tm, tk), lambda i, j, k: (i, k)), "nt": pl.BlockSpec((tm, tk), lambda i, j, k: (i, k)),
              "tn": pl.BlockSpec((tk, tm), lambda i, j, k: (k, i))}[mode]
    b_spec = {"nn": pl.BlockSpec((tk, tn), lambda i, j, k: (k, j)), "nt": pl.BlockSpec((tn, tk), lambda i, j, k: (j, k)),
              "tn": pl.BlockSpec((tk, tn), lambda i, j, k: (k, j))}[mode]
    if b_blocks and mode == "nn":
        per = b.shape[2] // tn
        b_spec = pl.BlockSpec((None, tk, tn), lambda i, j, k: (j // per, k, j % per))
    elif b_blocks:
        per = b.shape[2] // tk
        b_spec = pl.BlockSpec((None, tn, tk), lambda i, j, k: (k // per, j, k % per))
    out_spec = pl.BlockSpec((tm, tn), lambda i, j, k: (i, j))
    out_shape = jax.ShapeDtypeStruct((M, N), out_dtype)
    if out_blocks:
        per_o = N // out_blocks // tn
        out_spec = pl.BlockSpec((None, tm, tn), lambda i, j, k: (j // per_o, i, j % per_o))
        out_shape = jax.ShapeDtypeStruct((out_blocks, M, N // out_blocks), out_dtype)
    x_specs, x_args = [], []
    for x in extras:
        arr, off = x if isinstance(x, tuple) else (x, 0)
        if arr.shape[0] == 1:
            x_specs.append(pl.BlockSpec((1, tn), lambda i, j, k: (0, j)))
        else:
            x_specs.append(pl.BlockSpec((tm, tn), lambda i, j, k, off=off: (i, j + off)))
        x_args.append(arr)
    res = pl.pallas_call(
        body, name=name, grid=(M // tm, N // tn, nk),
        in_specs=[a_spec, b_spec] + x_specs + [pl.BlockSpec(memory_space=pl.ANY)] * n_after,
        out_specs=[out_spec] * n_o if epilogue else out_spec,
        out_shape=[jax.ShapeDtypeStruct((M, N), dt) for dt in out_dtypes] if epilogue else out_shape,
        scratch_shapes=[pltpu.VMEM((tm, tn), F32)] if nk > 1 else [],
        compiler_params=pltpu.CompilerParams(dimension_semantics=("parallel", "parallel", "arbitrary"),
                                             vmem_limit_bytes=VMEM_LIMIT),
    )(a, b, *x_args, *([after] if n_after else []))
    return res


class Rows:
    def __init__(self, arr, width=None, cb=0):
        self.arr, self.width, self.cb = arr, (arr.shape[1] if width is None else width), cb


class Heads:
    def __init__(self, arr):
        self.arr = arr


class Halo:
    def __init__(self, arr, side):
        self.arr, self.side = arr, side


def _rows_call(name, fn, ins, consts, outs, accs=(), tm=256, with_pid=False):
    T = next(o.arr.shape[1] if isinstance(o, Heads) else o.arr.shape[0] for o in ins if not isinstance(o, Halo))
    tm = min(tm, T)
    n_tiles = T // tm
    n_in, n_c, n_out = len(ins), len(consts), len(outs)
    in_specs, args = [], []
    for o in ins:
        if isinstance(o, Rows):
            in_specs.append(pl.BlockSpec((tm, o.width), lambda i, cb=o.cb: (i, cb)))
        elif isinstance(o, Heads):
            in_specs.append(pl.BlockSpec((NP, tm, PW), lambda i: (0, i, 0)))
        else:
            w = o.arr.shape[1]
            if o.side < 0:
                in_specs.append(pl.BlockSpec((8, w), lambda i: (jnp.maximum(i * (tm // 8) - 1, 0), 0)))
            else:
                in_specs.append(pl.BlockSpec((8, w), lambda i: (jnp.minimum((i + 1) * (tm // 8), T // 8 - 1), 0)))
        args.append(o.arr)
    for c in consts:
        in_specs.append(pl.BlockSpec(c.shape, lambda i, nd=c.ndim: (0,) * nd))
        args.append(c)
    out_specs, out_shape = [], []
    for o in outs:
        if o[0] == "rows":
            out_specs.append(pl.BlockSpec((tm, o[1]), lambda i: (i, 0)))
            out_shape.append(jax.ShapeDtypeStruct((T, o[1]), o[2]))
        else:
            out_specs.append(pl.BlockSpec((NP, tm, PW), lambda i: (0, i, 0)))
            out_shape.append(jax.ShapeDtypeStruct((NP, T, PW), o[1]))
    for shape, dt in accs:
        out_specs.append(pl.BlockSpec(shape, lambda i, nd=len(shape): (0,) * nd))
        out_shape.append(jax.ShapeDtypeStruct(shape, dt))

    def body(*refs):
        i = pl.program_id(0)
        vals = []
        vals = [r[...] for r in refs[:n_in + n_c]]
        res = fn(i, n_tiles, *vals) if with_pid else fn(*vals)
        out_refs = refs[n_in + n_c:]
        for r, v in zip(out_refs[:n_out], res[:n_out]):
            r[...] = v.astype(r.dtype)
        if accs:
            @pl.when(i == 0)
            def _():
                for r in out_refs[n_out:]:
                    r[...] = jnp.zeros_like(r)

            for r, v in zip(out_refs[n_out:], res[n_out:]):
                r[...] += v.astype(r.dtype)

    res = pl.pallas_call(
        body, name=name, grid=(n_tiles,), in_specs=in_specs, out_specs=out_specs, out_shape=out_shape,
        compiler_params=pltpu.CompilerParams(dimension_semantics=("arbitrary",), vmem_limit_bytes=VMEM_LIMIT),
    )(*args)
    return res


def _rms(x, g):
    return x * lax.rsqrt(jnp.mean(x * x, axis=-1, keepdims=True) + NORM_EPS) * g


def _gelu(x):
    return 0.5 * x * (1.0 + lax.erf(x * 0.7071067811865476))


def _sigmoid(x):
    return 1.0 / (1.0 + jnp.exp(-x))


def _bdot(a, b):
    return jnp.dot(a.astype(BF16), b.astype(BF16), preferred_element_type=F32)


def _to_heads(x):
    return jnp.concatenate([x[:, p * PW:(p + 1) * PW][None] for p in range(NP)], axis=0)


def _from_heads(xp):
    return jnp.concatenate([xp[p] for p in range(NP)], axis=-1)


def _head_sum(xp):
    low = lax.broadcasted_iota(jnp.int32, xp.shape, xp.ndim - 1) < HN
    both = jnp.sum(xp, axis=-1, keepdims=True)
    first = jnp.sum(jnp.where(low, xp, 0.0), axis=-1, keepdims=True)
    return jnp.where(low, first, both - first)


def _split_pairs(xp):
    return jnp.concatenate([xp[:, :, :HN], xp[:, :, HN:]], axis=0)


def _join_pairs(xh):
    return jnp.concatenate([xh[:NP], xh[NP:]], axis=-1)


def _sgu_fn(p, ln_w, ln_b, sw, sbt):
    z = _gelu(p)
    u, v = z[:, :D], z[:, D:]
    mu = jnp.mean(v, axis=-1, keepdims=True)
    var = jnp.mean(jnp.square(v - mu), axis=-1, keepdims=True)
    vn = (v - mu) * lax.rsqrt(var + LN_EPS) * ln_w + ln_b
    ri = lax.broadcasted_iota(jnp.int32, (SGU_C, SGU_C), 0)
    ci = lax.broadcasted_iota(jnp.int32, (SGU_C, SGU_C), 1)
    mask = (ci <= ri).astype(F32)
    dg = D // SGU_G
    parts = []
    for g in range(SGU_G):
        parts.append(_bdot(sw[g] * mask, vn[:, g * dg:(g + 1) * dg]) + sbt[:, g:g + 1])
    return u * jnp.concatenate(parts, axis=-1)


def _pre_fn(qr, qk, qv, qxw, qxa, qxg, wl, w0, al, a0, gl, k_k, k_a):
    w = -jax.nn.softplus(-(w0 + _bdot(jnp.tanh(qxw), wl))) - 0.5
    lw = -jnp.exp(w)
    aa = _sigmoid(a0 + _bdot(qxa, al))
    g = _bdot(_sigmoid(qxg), gl)
    kk = _to_heads(qk * k_k)
    kk = kk / jnp.maximum(jnp.sqrt(_head_sum(kk * kk)), 1e-12)
    k2 = qk * (1.0 + (aa - 1.0) * k_a)
    return _to_heads(qr), _to_heads(lw), _to_heads(k2), _to_heads(qv), kk, _to_heads(aa), g


def _post_fn(o, r, k2, v, g, ln_w, ln_b, r_k):
    mu = _head_sum(o) * (1.0 / HN)
    d = o - mu
    var = _head_sum(d * d) * (1.0 / HN)
    on = d * lax.rsqrt(var + GN_EPS) * ln_w + ln_b
    bonus = _head_sum(r * k2 * r_k) * v
    return _from_heads(on + bonus) * g


def _gate_fn(pg, ya, yb):
    return _sigmoid(pg[:, :D]) * ya + _sigmoid(pg[:, D:]) * yb


def _bmm(x, y, cx, cy, out_path=False):
    return lax.dot_general(x, y, (((cx,), (cy,)), ((0,), (0,))),
                           precision=SCAN_OUT_PRECISION if out_path else SCAN_PRECISION, preferred_element_type=F32)


def _unit_lower_inverse(M):
    C = M.shape[1]
    ti = lax.broadcasted_iota(jnp.int32, (C, C), 0)
    tj = lax.broadcasted_iota(jnp.int32, (C, C), 1)
    eye = (ti == tj).astype(F32)
    same = lambda b: (ti // b == tj // b).astype(F32)
    X = -(M * same(SOLVE_B))
    inv = eye + X
    span = 1
    while 2 * span < SOLVE_B:
        X = _bmm(X, X, 2, 1)
        inv = inv + _bmm(inv, X, 2, 1)
        span *= 2
    b = SOLVE_B
    while b < C:
        low = M * (same(2 * b) - same(b))
        inv = inv - _bmm(_bmm(inv, low, 2, 1), inv, 2, 1)
        b *= 2
    return inv


@jax.custom_vjp
def _unit_lower_solve(inv, M, y):
    return _bmm(inv, y, 2, 1)


def _unit_lower_solve_fwd(inv, M, y):
    u = _bmm(inv, y, 2, 1)
    return u, (inv, u)


def _unit_lower_solve_bwd(res, du):
    inv, u = res
    dy = _bmm(inv, du, 1, 1)
    return jnp.zeros_like(inv), -_bmm(dy, u, 2, 2), dy


_unit_lower_solve.defvjp(_unit_lower_solve_fwd, _unit_lower_solve_bwd)


def _sum_over_time(x, reverse):
    C = x.shape[1]
    ti = lax.broadcasted_iota(jnp.int32, (C, C), 0)
    tj = lax.broadcasted_iota(jnp.int32, (C, C), 1)
    ones = jnp.broadcast_to(((tj >= ti) if reverse else (tj <= ti)).astype(BF16), (x.shape[0], C, C))
    hi = x.astype(BF16)
    r1 = x - hi.astype(F32)
    mid = r1.astype(BF16)
    lo = (r1 - mid.astype(F32)).astype(BF16)
    dn = (((2,), (1,)), ((0,), (0,)))
    return sum(lax.dot_general(ones, p, dn, preferred_element_type=F32) for p in (lo, mid, hi))


@jax.custom_vjp
def _time_cumsum(lw):
    return _sum_over_time(lw, reverse=False)


_time_cumsum.defvjp(lambda lw: (_sum_over_time(lw, reverse=False), None),
                    lambda _, d: (_sum_over_time(d, reverse=True),))


def _chunk_fn(S0, r, lw, k, v, kk, a, inv=None):
    C = SCAN_C
    bmm = _bmm
    ti = lax.broadcasted_iota(jnp.int32, (C, C), 0)
    tj = lax.broadcasted_iota(jnp.int32, (C, C), 1)
    incl = (tj <= ti).astype(F32)
    strict = (tj < ti).astype(F32)
    cum = _time_cumsum(lw)
    g_in, g_ex, g_inv = jnp.exp(cum), jnp.exp(cum - lw), jnp.exp(-cum)
    kkt, rt = kk * g_ex, r * g_in
    bk = jnp.concatenate([kk * a * g_inv, k * g_inv], axis=1)
    A = bmm(kkt, bk, 2, 2)
    M = A[:, :, :C] * strict
    n_mask = jnp.concatenate([jnp.zeros((C, C), F32), strict], axis=1)
    zv = jnp.concatenate([jnp.zeros_like(v), v], axis=1)
    s0_side = bmm(jnp.concatenate([kkt, rt], axis=1), S0, 2, 2, out_path=True)
    if inv is None:
        inv = lax.stop_gradient(_unit_lower_inverse(M))
    y = _unit_lower_solve(inv, M, s0_side[:, :C] + bmm(A * n_mask, zv, 2, 1, out_path=True))
    z = jnp.concatenate([-y, v], axis=1)
    attn = bmm(rt, bk, 2, 2) * jnp.concatenate([incl, incl], axis=1)
    O = s0_side[:, C:] + bmm(attn, z, 2, 1, out_path=True)
    S1 = (S0 + bmm(z, bk, 1, 1, out_path=True)) * g_in[:, C - 1:C, :]
    return O, S1, inv


def _scan_fwd(r, lw, k, v, kk, a, ex=None, tb=256):
    assert SCAN_C == HN
    T = r.shape[1]
    tb = min(tb, T)
    n_chunks = tb // SCAN_C
    nb = T // tb
    nx = ex.nb if ex else 0

    def body(*refs):
        r_ref, lw_ref, k_ref, v_ref, kk_ref, a_ref = refs[:6]
        x_in, (o_ref, s0_ref), x_out = refs[6:6 + nx], refs[6 + nx:8 + nx], refs[8 + nx:8 + 2 * nx]
        s_ref, sems = refs[8 + 2 * nx], refs[9 + 2 * nx:]

        plan = ex.schedule(nb) if ex else []

        @pl.when(pl.program_id(0) == 0)
        def _():
            s_ref[...] = jnp.zeros_like(s_ref)
            for at, action in plan[:1]:
                action(x_in, x_out, sems)

        def step(c, carry):
            sl = pl.ds(pl.multiple_of(c * SCAN_C, SCAN_C), SCAN_C)
            S0 = s_ref[...]
            O, S1, inv = _chunk_fn(S0, *[_split_pairs(ref[:, sl, :])
                                         for ref in (r_ref, lw_ref, k_ref, v_ref, kk_ref, a_ref)])
            o_ref[:, sl, :] = _join_pairs(O)
            s0_ref[c, 0] = S0
            s0_ref[c, 1] = inv
            s_ref[...] = S1
            return carry

        lax.fori_loop(0, n_chunks, step, 0)

        for at, action in plan[1:]:
            pl.when(pl.program_id(0) == at)(functools.partial(action, x_in, x_out, sems))

    hm = pl.BlockSpec((NP, tb, PW), lambda i: (0, i, 0))
    res = pl.pallas_call(
        body, name="rwkv_scan_fwd", grid=(nb,), in_specs=[hm] * 6 + (ex.any_specs if ex else []),
        out_specs=[hm, pl.BlockSpec((n_chunks, 2, NH, HN, HN), lambda i: (i, 0, 0, 0, 0))]
        + (ex.any_specs if ex else []),
        out_shape=[jax.ShapeDtypeStruct((NP, T, PW), F32), jax.ShapeDtypeStruct((T // SCAN_C, 2, NH, HN, HN), F32)]
        + (ex.out_shape if ex else []),
        scratch_shapes=[pltpu.VMEM((NH, HN, HN), F32)] + (ex.sem_shapes if ex else []),
        compiler_params=pltpu.CompilerParams(dimension_semantics=("arbitrary",), vmem_limit_bytes=VMEM_LIMIT),
    )(r, lw, k, v, kk, a, *(ex.bufs if ex else []))
    return res[0], res[1], list(res[2:])


def _scan_bwd(r, lw, k, v, kk, a, s0s, do, ex=None, tb=128):
    T = r.shape[1]
    tb = min(tb, T)
    n_chunks = tb // SCAN_C
    nb = T // tb
    nx = ex.nb if ex else 0

    def body(*refs):
        r_ref, lw_ref, k_ref, v_ref, kk_ref, a_ref, s0_ref, do_ref = refs[:8]
        x_in, (dr, dlw, dk, dv, dkk, da), x_out = refs[8:8 + nx], refs[8 + nx:14 + nx], refs[14 + nx:14 + 2 * nx]
        ds_ref, sems = refs[14 + 2 * nx], refs[15 + 2 * nx:]

        plan = ex.schedule(nb) if ex else []

        @pl.when(pl.program_id(0) == 0)
        def _():
            ds_ref[...] = jnp.zeros_like(ds_ref)
            for at, action in plan[:1]:
                action(x_in, x_out, sems)

        def step(j, carry):
            c = n_chunks - 1 - j
            sl = pl.ds(pl.multiple_of(c * SCAN_C, SCAN_C), SCAN_C)
            inv = s0_ref[c, 1]
            _, vjp = jax.vjp(lambda *t: _chunk_fn(*t, inv=inv)[:2], s0_ref[c, 0],
                             *[_split_pairs(ref[:, sl, :]) for ref in (r_ref, lw_ref, k_ref, v_ref, kk_ref, a_ref)])
            g = vjp((_split_pairs(do_ref[:, sl, :]), ds_ref[...]))
            ds_ref[...] = g[0]
            for ref, val in zip((dr, dlw, dk, dv, dkk, da), g[1:]):
                ref[:, sl, :] = _join_pairs(val)
            return carry

        lax.fori_loop(0, n_chunks, step, 0)

        for at, action in plan[1:]:
            pl.when(pl.program_id(0) == at)(functools.partial(action, x_in, x_out, sems))

    hm = pl.BlockSpec((NP, tb, PW), lambda i: (0, nb - 1 - i, 0))
    res = pl.pallas_call(
        body, name="rwkv_scan_bwd", grid=(nb,),
        in_specs=[hm] * 6 + [pl.BlockSpec((n_chunks, 2, NH, HN, HN), lambda i: (nb - 1 - i, 0, 0, 0, 0)), hm]
        + (ex.any_specs if ex else []),
        out_specs=[hm] * 6 + (ex.any_specs if ex else []),
        out_shape=[jax.ShapeDtypeStruct((NP, T, PW), F32)] * 6 + (ex.out_shape if ex else []),
        scratch_shapes=[pltpu.VMEM((NH, HN, HN), F32)] + (ex.sem_shapes if ex else []),
        compiler_params=pltpu.CompilerParams(dimension_semantics=("arbitrary",), vmem_limit_bytes=VMEM_LIMIT),
    )(r, lw, k, v, kk, a, s0s, do, *(ex.bufs if ex else []))
    return list(res[:6]), list(res[6:])


def _shift_down(i, p, prev8):
    first = jnp.where(i > 0, prev8[7:8, :], 0.0)
    row = lax.broadcasted_iota(jnp.int32, p.shape, 0)
    return jnp.where(row == 0, first, pltpu.roll(p, 1, axis=0))


def _mix_bwd(dq, p, sb, tm=256):
    def fn(i, n, dq, next8, p, prev8, sb):
        ps = _shift_down(i, p, prev8)
        d1 = dq * sb[1:2]
        last = jnp.where(i < n - 1, next8[0:1, :] * sb[1:2], 0.0)
        row = lax.broadcasted_iota(jnp.int32, dq.shape, 0)
        up = jnp.where(row == dq.shape[0] - 1, last, pltpu.roll(d1, dq.shape[0] - 1, axis=0))
        return (dq * sb[0:1] + up, jnp.sum(dq * p, axis=0, keepdims=True), jnp.sum(dq * ps, axis=0, keepdims=True))
    w = p.shape[1]
    return _rows_call("shift_mix_bwd", fn, [Rows(dq), Halo(dq, +1), Rows(p), Halo(p, -1)], [sb], [("rows", w, BF16)],
                      accs=[((1, w), F32), ((1, w), F32)], tm=tm, with_pid=True)


def _local_step(x, target, W, late_weights=None, early_grads=None, w_in_grads_ready=None):
    G = {}
    a = _rows_call("norm_mix_fwd", lambda x, g: (_rms(x, g),), [Rows(x)], [W["g_mix"]], [("rows", D, BF16)])[0]
    p_sgu = _matmul("proj_sgu", a, W["w_sgu_t"], "nt")
    def token_shift(p, sb0, sb1):
        row = lax.broadcasted_iota(jnp.int32, p.shape, 0)
        return p, p * sb0 + jnp.where(row == 0, 0.0, pltpu.roll(p, 1, axis=0)) * sb1
    p_rw, q = _matmul("proj_rwkv", a, W["w_rw_t"], "nt", tn=512, whole_rows=True, epilogue=token_shift,
                      extras=[W["sb"][0:1], W["sb"][1:2]], out_dtypes=(F32, F32))
    p_gate = _matmul("proj_gate", a, W["w_gate_t"], "nt")

    sgu_consts = [W["sgu_ln_w"], W["sgu_ln_b"], W["sgu_w"], W["sgu_bt"]]
    s = _rows_call("sgu_fwd", lambda *t: (_sgu_fn(*t),), [Rows(p_sgu)], sgu_consts, [("rows", D, BF16)], tm=SGU_C)[0]

    q_ins = [Rows(q, D, 0), Rows(q, D, 1), Rows(q, D, 2), Rows(q, 128, 24), Rows(q, 128, 25), Rows(q, 256, 13)]
    pre_consts = [W["w_lora"], W["w0"], W["a_lora"], W["a0"], W["g_lora"], W["k_k"], W["k_a"]]
    r_h, lw_h, k_h, v_h, kk_h, a_h, g_gate = _rows_call(
        "rwkv_pre_fwd", _pre_fn, q_ins, pre_consts, [("heads", F32)] * 6 + [("rows", D, F32)], tm=128)
    o_h, s0s, got = _scan_fwd(r_h, lw_h, k_h, v_h, kk_h, a_h, ex=late_weights[0] if late_weights else None)
    if late_weights:
        W = {**W, **late_weights[1](got)}
    y_a = _matmul("proj_a", s, W["w_proj_a"], "nn")
    post_ins = [Heads(o_h), Heads(r_h), Heads(k_h), Heads(v_h), Rows(g_gate)]
    post_consts = [W[n].reshape(NP, 1, PW) for n in ("ln_x_w", "ln_x_b", "r_k")]
    z_b = _rows_call("rwkv_post_fwd", lambda *t: (_post_fn(*t),), post_ins, post_consts, [("rows", D, BF16)], tm=128)[0]
    y_b, mixed = _matmul("proj_b", z_b, W["w_proj_b"], "nn", extras=[(p_gate, 0), (p_gate, 1), y_a],
                         epilogue=lambda yb, ga, gb, ya: (yb, _sigmoid(ga) * ya + _sigmoid(gb) * yb),
                         out_dtypes=(F32, BF16))
    gate_ins = [Rows(p_gate), Rows(y_a), Rows(y_b)]

    def res1(mo, x, g):
        h1 = x + mo
        return h1, _rms(h1, g)
    h1, f = _matmul("proj_out", mixed, W["w_out"], "nn", extras=[x, W["g_ffn"]], epilogue=res1,
                    out_dtypes=(F32, BF16))

    def relu_sq(u):
        r = jnp.maximum(u, 0.0)
        return r, r * r
    r1, act = _matmul("ffn_up", f, W["w_ffn1"], "nn", epilogue=relu_sq, out_dtypes=(BF16, BF16))
    ff = _matmul("ffn_down", act, W["w_ffn2"], "nn")

    def head(h1, ff, tgt, g):
        def f_(h1, ff, g):
            y = _rms(h1 + ff, g)
            return 0.5 * jnp.sum(jnp.mean(jnp.square(y - tgt), axis=-1))
        loss, (dh2, _, dg) = jax.value_and_grad(f_, argnums=(0, 1, 2))(h1, ff, g)
        return dh2, jnp.full((8, LANES), loss, F32), dg
    dh2, loss_acc, G["g_final"] = _rows_call("loss_head", head, [Rows(h1), Rows(ff), Rows(target)], [W["g_final"]],
                                             [("rows", D, F32)], accs=[((8, LANES), F32), ((1, D), F32)])

    d_u1 = _matmul("ffn_down_dx", dh2, W["w_ffn2"], "nt", extras=[r1], out_dtypes=(BF16,),
                   epilogue=lambda d_act, r: (d_act * 2.0 * r.astype(F32),))[0]
    G["w_ffn2"] = _matmul("ffn_down_dw", act, dh2, "tn", out_dtype=GRAD_PAYLOAD)
    d_f = _matmul("ffn_up_dx", d_u1, W["w_ffn1"], "nt")
    G["w_ffn1"] = _matmul("ffn_up_dw", f, d_u1, "tn", out_blocks=N_DEV, out_dtype=GRAD_PAYLOAD)

    def res1_bwd(h1, d_f, dh2, g):
        _, vjp = jax.vjp(_rms, h1, g)
        dh, dg = vjp(d_f)
        return dh2 + dh, dg
    dh1, G["g_ffn"] = _rows_call("residual_norm_bwd", res1_bwd, [Rows(h1), Rows(d_f), Rows(dh2)],
                                 [W["g_ffn"]], [("rows", D, F32)], accs=[((1, D), F32)])
    d_mixed = _matmul("proj_out_dx", dh1, W["w_out"], "nt")
    G["w_out"] = _matmul("proj_out_dw", mixed, dh1, "tn", out_dtype=GRAD_PAYLOAD)

    def gate_bwd(pg, ya, yb, dm):
        _, vjp = jax.vjp(_gate_fn, pg, ya, yb)
        return vjp(dm)
    d_gate, d_ya, d_yb = _rows_call("gate_bwd", gate_bwd, gate_ins + [Rows(d_mixed)], [],
                                    [("rows", 2 * D, BF16), ("rows", D, BF16), ("rows", D, BF16)])

    d_s = _matmul("proj_a_dx", d_ya, W["w_proj_a"], "nt")
    G["w_proj_a"] = _matmul("proj_a_dw", s, d_ya, "tn", out_dtype=GRAD_PAYLOAD)

    def sgu_bwd(p, ds, *c):
        _, vjp = jax.vjp(_sgu_fn, p, *c)
        return vjp(ds)
    d_p_sgu, G["sgu_ln_w"], G["sgu_ln_b"], G["sgu_w"], G["sgu_bt"] = _rows_call(
        "sgu_bwd", sgu_bwd, [Rows(p_sgu), Rows(d_s)], sgu_consts, [("rows", 2 * D, BF16)],
        accs=[((1, D), F32), ((1, D), F32), ((SGU_G, SGU_C, SGU_C), F32), ((SGU_C, SGU_G), F32)], tm=SGU_C)

    d_zb = _matmul("proj_b_dx", d_yb, W["w_proj_b"], "nt")
    G["w_proj_b"] = _matmul("proj_b_dw", z_b, d_yb, "tn", out_dtype=GRAD_PAYLOAD)

    def post_bwd(o, r, k2, v, g, dz, *c):
        _, vjp = jax.vjp(_post_fn, o, r, k2, v, g, *c)
        return vjp(dz)
    do_h, dr1, dk1, dv1, d_g, g_lnw, g_lnb, g_rk = _rows_call(
        "rwkv_post_bwd", post_bwd, post_ins + [Rows(d_zb)], post_consts, [("heads", F32)] * 4 + [("rows", D, F32)],
        accs=[((NP, 1, PW), F32)] * 3, tm=128)
    G["ln_x_w"], G["ln_x_b"], G["r_k"] = (t.reshape(1, D) for t in (g_lnw, g_lnb, g_rk))
    (dr2, dlw, dk2, dv2, dkk, daa), early = _scan_bwd(r_h, lw_h, k_h, v_h, kk_h, a_h, s0s, do_h,
                                                      ex=early_grads(G) if early_grads else None)

    def pre_bwd(qr, qk, qv, qxw, qxa, qxg, dr1, dr2, dlw, dk1, dk2, dv1, dv2, dkk, daa, dg, *c):
        _, vjp = jax.vjp(_pre_fn, qr, qk, qv, qxw, qxa, qxg, *c)
        g = vjp((dr1 + dr2, dlw, dk1 + dk2, dv1 + dv2, dkk, daa, dg))
        dq = jnp.concatenate(g[:6], axis=-1)
        return (dq,) + tuple(g[6:])
    pre_b_ins = q_ins + [Heads(dr1), Heads(dr2), Heads(dlw), Heads(dk1), Heads(dk2), Heads(dv1), Heads(dv2),
                         Heads(dkk), Heads(daa), Rows(d_g)]
    d_q, G["w_lora"], G["w0"], G["a_lora"], G["a0"], G["g_lora"], G["k_k"], G["k_a"] = _rows_call(
        "rwkv_pre_bwd", pre_bwd, pre_b_ins, pre_consts, [("rows", RW_INT, F32)],
        accs=[((128, D), F32), ((1, D), F32), ((128, D), F32), ((1, D), F32), ((256, D), F32), ((1, D), F32),
              ((1, D), F32)], tm=128)
    d_p_rw, dsb0, dsb1 = _mix_bwd(d_q, p_rw, W["sb"])
    G["sb"] = jnp.concatenate([dsb0, dsb1], axis=0)

    G["w_sgu_t"] = _matmul("proj_sgu_dw", d_p_sgu, a, "tn", out_dtype=GRAD_PAYLOAD)
    G["w_rw_t"] = _matmul("proj_rwkv_dw", d_p_rw, a, "tn", out_dtype=GRAD_PAYLOAD)
    G["w_gate_t"] = _matmul("proj_gate_dw", d_gate, a, "tn", out_dtype=GRAD_PAYLOAD)
    token = w_in_grads_ready(G) if w_in_grads_ready else None
    da1 = _matmul("proj_sgu_dx", d_p_sgu, W["w_sgu_t"], "nn", after=token)
    da2 = _matmul("proj_rwkv_dx", d_p_rw, W["w_rw_t"], "nn", after=token)
    da3 = _matmul("proj_gate_dx", d_gate, W["w_gate_t"], "nn", after=token)

    def norm1_bwd(x, da1, da2, da3, dh1, g):
        _, vjp = jax.vjp(_rms, x, g)
        dx, dg = vjp(da1 + da2 + da3)
        return dh1 + dx, dg
    dx, G["g_mix"] = _rows_call("norm_mix_bwd", norm1_bwd, [Rows(x), Rows(da1), Rows(da2), Rows(da3), Rows(dh1)],
                                [W["g_mix"]], [("rows", D, F32)], accs=[((1, D), F32)])
    return loss_acc[0, 0], dx, G, early


class Exchange:
    def __init__(self, bufs, gathers):
        self.bufs, self.gathers, self.nb = list(bufs), list(gathers), len(bufs)
        self.any_specs = [pl.BlockSpec(memory_space=pl.ANY)] * self.nb
        self.out_shape = [jax.ShapeDtypeStruct((N_DEV,) + (b.shape if g else b.shape[1:]), b.dtype)
                          for b, g in zip(self.bufs, self.gathers)]
        n = (N_DEV - 1) * self.nb
        self.sem_shapes = [pltpu.SemaphoreType.DMA((n,)), pltpu.SemaphoreType.DMA((n,)),
                           pltpu.SemaphoreType.DMA((self.nb,))]

    def _copies(self, in_refs, out_refs, sems):
        send_sems, recv_sems, local_sems = sems
        x, y, c = lax.axis_index("x"), lax.axis_index("y"), lax.axis_index("c")
        me = 4 * x + 2 * y + c

        def src(b, dest):
            return in_refs[b] if self.gathers[b] else in_refs[b].at[dest]

        local = [pltpu.make_async_copy(src(b, me), out_refs[b].at[me], local_sems.at[b]) for b in range(self.nb)]
        sends, recvs = [], []
        for kbits in range(1, N_DEV):
            px = 1 - x if kbits & 4 else x
            py = 1 - y if kbits & 2 else y
            pc = 1 - c if kbits & 1 else c
            peer = 4 * px + 2 * py + pc
            for b in range(self.nb):
                s = (kbits - 1) * self.nb + b
                sends.append(pltpu.make_async_remote_copy(
                    src_ref=src(b, peer), dst_ref=out_refs[b].at[me], send_sem=send_sems.at[s],
                    recv_sem=recv_sems.at[s], device_id=(px, py, pc), device_id_type=pl.DeviceIdType.MESH))
                recvs.append(pltpu.make_async_remote_copy(
                    src_ref=src(b, peer), dst_ref=out_refs[b].at[peer], send_sem=send_sems.at[s],
                    recv_sem=recv_sems.at[s], device_id=(px, py, pc), device_id_type=pl.DeviceIdType.MESH))
        return local, sends, recvs

    def start(self, in_refs, out_refs, sems):
        local, sends, _ = self._copies(in_refs, out_refs, sems)
        for cp in sends + local:
            cp.start()

    def wait(self, in_refs, out_refs, sems):
        local, sends, recvs = self._copies(in_refs, out_refs, sems)
        for cp in recvs:
            cp.wait_recv()
        for cp in sends:
            cp.wait_send()
        for cp in local:
            cp.wait()

    def schedule(self, n_steps):
        return [(0, self.start), (n_steps - 1, self.wait)]


def _exchange(name, bufs, gather):
    ex = Exchange(bufs, gather if isinstance(gather, (list, tuple)) else [gather] * len(bufs))

    def body(*refs):
        in_refs, out_refs, sems = refs[:ex.nb], refs[ex.nb:2 * ex.nb], refs[2 * ex.nb:]
        ex.start(in_refs, out_refs, sems)
        ex.wait(in_refs, out_refs, sems)

    return pl.pallas_call(body, name=name, in_specs=ex.any_specs, out_specs=ex.any_specs, out_shape=ex.out_shape,
                          scratch_shapes=ex.sem_shapes)(*ex.bufs)


N_CHIP = 4


def _pair_exchange(name, blocks):
    def body(b_ref, got_ref, send_sems, recv_sems):
        x, y, c = lax.axis_index("x"), lax.axis_index("y"), lax.axis_index("c")
        copies = [pltpu.make_async_remote_copy(
            src_ref=b_ref.at[2 * q + 1 - c], dst_ref=got_ref.at[q], send_sem=send_sems.at[q],
            recv_sem=recv_sems.at[q], device_id=(x, y, 1 - c), device_id_type=pl.DeviceIdType.MESH)
            for q in range(N_CHIP)]
        for cp in copies:
            cp.start()
        for cp in copies:
            cp.wait_recv()
        for cp in copies:
            cp.wait_send()

    any_spec = pl.BlockSpec(memory_space=pl.ANY)
    return pl.pallas_call(
        body, name=name, in_specs=[any_spec], out_specs=any_spec,
        out_shape=jax.ShapeDtypeStruct((N_CHIP,) + blocks.shape[1:], blocks.dtype),
        scratch_shapes=[pltpu.SemaphoreType.DMA((N_CHIP,))] * 2,
    )(blocks)


def _pair_sum(name, blocks, got, core):
    _, R, Wd = blocks.shape

    def body(c_ref, a_ref, b_ref, o_ref):
        o_ref[...] = (a_ref[...].astype(F32) + b_ref[...].astype(F32)).astype(o_ref.dtype)

    return pl.pallas_call(
        body, name=name,
        grid_spec=pltpu.PrefetchScalarGridSpec(
            num_scalar_prefetch=1, grid=(N_CHIP,),
            in_specs=[pl.BlockSpec((None, R, Wd), lambda q, c_ref: (2 * q + c_ref[0], 0, 0)),
                      pl.BlockSpec((None, R, Wd), lambda q, c_ref: (q, 0, 0))],
            out_specs=pl.BlockSpec((None, R, Wd), lambda q, c_ref: (q, 0, 0))),
        out_shape=jax.ShapeDtypeStruct(got.shape, blocks.dtype),
        compiler_params=pltpu.CompilerParams(dimension_semantics=("parallel",), vmem_limit_bytes=VMEM_LIMIT),
    )(core, blocks, got)


def _chip_copies(s_ref, land_ref, send_sems, recv_sems):
    x, y, c = lax.axis_index("x"), lax.axis_index("y"), lax.axis_index("c")
    my_q = 2 * x + y
    sends, recvs = [], []
    for kbits in range(1, N_CHIP):
        px = 1 - x if kbits & 2 else x
        py = 1 - y if kbits & 1 else y
        peer_q = 2 * px + py
        sends.append(pltpu.make_async_remote_copy(
            src_ref=s_ref.at[peer_q], dst_ref=land_ref.at[my_q], send_sem=send_sems[kbits - 1],
            recv_sem=recv_sems[kbits - 1], device_id=(px, py, c), device_id_type=pl.DeviceIdType.MESH))
        recvs.append(pltpu.make_async_remote_copy(
            src_ref=s_ref.at[peer_q], dst_ref=land_ref.at[peer_q], send_sem=send_sems[kbits - 1],
            recv_sem=recv_sems[kbits - 1], device_id=(px, py, c), device_id_type=pl.DeviceIdType.MESH))
    return sends, recvs


_HBM = pl.BlockSpec(memory_space=pltpu.HBM)
_SEM = pl.BlockSpec(memory_space=pltpu.SEMAPHORE)
N_CHIP_SEMS = 2 * (N_CHIP - 1)


def _chip_exchange_start(name, sums):
    def body(s_ref, land_ref, *outs):
        sems, token = outs[:N_CHIP_SEMS], outs[N_CHIP_SEMS + 2]
        sends, _ = _chip_copies(s_ref, land_ref, sems[:N_CHIP - 1], sems[N_CHIP - 1:])
        for cp in sends:
            cp.start()
        token[...] = jnp.zeros_like(token)

    res = pl.pallas_call(
        body, name=name, in_specs=(_HBM, _HBM),
        out_specs=(_SEM,) * N_CHIP_SEMS + (_HBM, _HBM, pl.BlockSpec(memory_space=pltpu.VMEM)),
        out_shape=(pltpu.SemaphoreType.DMA(()),) * N_CHIP_SEMS
        + (pltpu.HBM(sums.shape, sums.dtype), pltpu.HBM(sums.shape, sums.dtype), jax.ShapeDtypeStruct((8, LANES), F32)),
        input_output_aliases={0: N_CHIP_SEMS, 1: N_CHIP_SEMS + 1},
        compiler_params=pltpu.CompilerParams(has_side_effects=pltpu.SideEffectType.DATAFLOW_SIDE_EFFECTING),
    )(pltpu.with_memory_space_constraint(sums, pltpu.HBM),
      pltpu.with_memory_space_constraint(lax.empty(sums.shape, sums.dtype), pltpu.HBM))
    return res[:N_CHIP_SEMS], res[N_CHIP_SEMS], res[N_CHIP_SEMS + 1], res[N_CHIP_SEMS + 2]


def _chip_exchange_wait(name, sems, sums_thru, land_thru, after):
    def body(s_ref, land_ref, *rest):
        sems = rest[:N_CHIP_SEMS]
        sends, recvs = _chip_copies(s_ref, land_ref, sems[:N_CHIP - 1], sems[N_CHIP - 1:])
        for cp in sends:
            cp.wait_send()
        for cp in recvs:
            cp.wait_recv()

    return pl.pallas_call(
        body, name=name, in_specs=(_HBM, _HBM) + (_SEM,) * N_CHIP_SEMS + (pl.BlockSpec(memory_space=pl.ANY),),
        out_specs=(_HBM, _HBM),
        out_shape=(pltpu.HBM(sums_thru.shape, sums_thru.dtype), pltpu.HBM(sums_thru.shape, sums_thru.dtype)),
        input_output_aliases={0: 0, 1: 1},
        compiler_params=pltpu.CompilerParams(has_side_effects=pltpu.SideEffectType.DATAFLOW_SIDE_EFFECTING),
    )(sums_thru, land_thru, *sems, after)


def _adamw(name, slots, w, m, v, tr=256):
    unit_mid = w.ndim == 3 and w.shape[1] == 1 and w.shape[0] > 1
    R, Wd = (w.shape[0], w.shape[2]) if unit_mid else w.shape[-2:]
    depth_axis = w.ndim == 3 and not unit_mid
    if R % tr == 0:
        tc = Wd
    else:
        tr, tc = R, (256 if (Wd % 256 == 0 and R > 256) else Wd)
    at = (slice(None), 0, slice(None)) if unit_mid else Ellipsis

    def body(s_ref, w_ref, m_ref, v_ref, g_out, d_out, m_out, v_out):
        g = s_ref[0].astype(F32)
        for j in range(1, slots.shape[0]):
            g = g + s_ref[j].astype(F32)
        m_new = ADAM_B1 * m_ref[at] + (1.0 - ADAM_B1) * g
        v_new = ADAM_B2 * v_ref[at] + (1.0 - ADAM_B2) * jnp.square(g)
        m_hat = m_new / (1.0 - ADAM_B1 ** ADAM_STEP)
        v_hat = v_new / (1.0 - ADAM_B2 ** ADAM_STEP)
        g_out[at] = g
        d_out[at] = -ADAM_LR * (m_hat / (jnp.sqrt(v_hat) + ADAM_EPS) + ADAM_WD * w_ref[at])
        m_out[at] = m_new
        v_out[at] = v_new

    if unit_mid:
        row = pl.BlockSpec((tr, 1, tc), lambda i, j: (i, 0, j))
    elif depth_axis:
        row = pl.BlockSpec((None, tr, tc), lambda i, j: (0, i, j))
    else:
        row = pl.BlockSpec((tr, tc), lambda i, j: (i, j))
    return pl.pallas_call(
        body, name=name, grid=(R // tr, Wd // tc),
        in_specs=[pl.BlockSpec((slots.shape[0], tr, tc), lambda i, j: (0, i, j)), row, row, row],
        out_specs=[row] * 4, out_shape=[jax.ShapeDtypeStruct(w.shape, F32)] * 4,
        compiler_params=pltpu.CompilerParams(dimension_semantics=("parallel", "parallel"),
                                             vmem_limit_bytes=VMEM_LIMIT),
    )(slots, w, m, v)


PACK_W = 1024
PACKED = [(n, s) for n, s in REPLICATED if n != "sgu_w"]
_SMALL_SIZES = [int(np.prod(s)) for _, s in PACKED]
_SMALL_ROWS = _round_up(_round_up(sum(_SMALL_SIZES) + PACK_W, PACK_W) // PACK_W, 8)
_LOSS_AT = sum(_SMALL_SIZES)
W_IN_SHARD = P_TOTAL // N_DEV


def _pack_rows(parts, rows, dtype):
    flat = jnp.concatenate([p.reshape(-1).astype(dtype) for p in parts])
    return jnp.pad(flat, (0, rows * PACK_W - flat.shape[0])).reshape(rows, PACK_W)


def _w_in_groups_t(blocks):
    wt = blocks.reshape(P_TOTAL, D)
    o, c = 2 * D, 2 * D + 3 * D
    z = lambda r: jnp.zeros((r, D), wt.dtype)
    rw = jnp.concatenate([wt[o:c], wt[c:c + L_W], z(128 - L_W), wt[c + L_W:c + L_W + L_A], z(128 - L_A),
                          wt[c + L_W + L_A:o + C_B], z(256 - L_G)], axis=0)
    return wt[:o], rw, wt[o + C_B:]


def _w_in_grad_blocks(g_sgu_t, g_rw_t, g_gate_t):
    c = 3 * D
    full = jnp.concatenate([g_sgu_t, g_rw_t[:c], g_rw_t[c:c + L_W], g_rw_t[c + 128:c + 128 + L_A],
                            g_rw_t[c + 256:c + 256 + L_G], g_gate_t], axis=0)
    return full.reshape(N_DEV, W_IN_SHARD, D)


def _mesh_index():
    me = 4 * lax.axis_index("x") + 2 * lax.axis_index("y") + lax.axis_index("c")
    return me.astype(jnp.int32).reshape(1)


def _fill_slot(name, dst, src, idx, src_idx=None):
    R, Wd = dst.shape[1:]
    scalars = [idx] if src_idx is None else [idx, src_idx]
    if src_idx is None:
        src_spec = pl.BlockSpec((R, Wd), lambda i, *s: (0, 0))
    else:
        src_spec = pl.BlockSpec((None, R, Wd), lambda i, *s: (s[1][0], 0, 0))

    def body(*refs):
        src_ref, out_ref = refs[len(scalars) + 1], refs[len(scalars) + 2]
        out_ref[...] = src_ref[...]

    return pl.pallas_call(
        body, name=name,
        grid_spec=pltpu.PrefetchScalarGridSpec(
            num_scalar_prefetch=len(scalars), grid=(1,),
            in_specs=[pl.BlockSpec(memory_space=pl.ANY), src_spec],
            out_specs=pl.BlockSpec((None, R, Wd), lambda i, *s: (s[0][0], 0, 0))),
        out_shape=jax.ShapeDtypeStruct(dst.shape, dst.dtype),
        input_output_aliases={len(scalars): 0},
        compiler_params=pltpu.CompilerParams(vmem_limit_bytes=VMEM_LIMIT),
    )(*scalars, dst, src)


class TwoLevelGather:
    def __init__(self, bufs, skip_own=()):
        self.bufs, self.nb, self.skip_own = list(bufs), len(bufs), tuple(skip_own)
        self.any_specs = [pl.BlockSpec(memory_space=pl.ANY)] * self.nb
        self.out_shape = [jax.ShapeDtypeStruct((N_DEV,) + b.shape, b.dtype) for b in self.bufs]
        self.sem_shapes = [pltpu.SemaphoreType.DMA((7 * self.nb,)), pltpu.SemaphoreType.DMA((7 * self.nb,)),
                           pltpu.SemaphoreType.DMA((self.nb,))]

    def _copies(self, in_refs, out_refs, sems):
        send_sems, recv_sems, local_sems = sems
        nb = self.nb
        x, y, c = lax.axis_index("x"), lax.axis_index("y"), lax.axis_index("c")
        me, sibling = (x, y, c), (x, y, 1 - c)
        chips = [(1 - x, y), (x, 1 - y), (1 - x, 1 - y)]

        def slot(b, dev):
            return out_refs[b].at[4 * dev[0] + 2 * dev[1] + dev[2]]

        def copy(b, k, block, to, own=False):
            return pltpu.make_async_remote_copy(
                src_ref=in_refs[b] if own else slot(b, block), dst_ref=slot(b, block),
                send_sem=send_sems.at[7 * b + k], recv_sem=recv_sems.at[7 * b + k], device_id=to,
                device_id_type=pl.DeviceIdType.MESH)

        cp = {}
        cp["local"] = [pltpu.make_async_copy(in_refs[b], slot(b, me), local_sems.at[b]) for b in range(nb)
                       if b not in self.skip_own]
        cp["first"] = [copy(b, 0, me, sibling, own=True) for b in range(nb)]
        cp["first"] += [copy(b, 1 + j, me, (*chip, c), own=True) for j, chip in enumerate(chips) for b in range(nb)]
        cp["over_ici"] = [copy(b, 1 + j, (*chip, c), me) for j, chip in enumerate(chips) for b in range(nb)]
        cp["passed"] = [copy(b, 4 + j, (*chip, c), sibling) for j, chip in enumerate(chips) for b in range(nb)]
        cp["from_sibling"] = [copy(b, 0, sibling, me) for b in range(nb)]
        cp["from_sibling"] += [copy(b, 4 + j, (*chip, 1 - c), me) for j, chip in enumerate(chips) for b in range(nb)]
        return cp

    def start(self, in_refs, out_refs, sems):
        cp = self._copies(in_refs, out_refs, sems)
        for c in cp["first"] + cp["local"]:
            c.start()

    def forward(self, in_refs, out_refs, sems):
        cp = self._copies(in_refs, out_refs, sems)
        for arrived, onward in zip(cp["over_ici"], cp["passed"]):
            arrived.wait_recv()
            onward.start()

    def finish(self, in_refs, out_refs, sems):
        cp = self._copies(in_refs, out_refs, sems)
        for c in cp["from_sibling"]:
            c.wait_recv()
        for c in cp["first"] + cp["passed"]:
            c.wait_send()
        for c in cp["local"]:
            c.wait()

    def schedule(self, n_steps):
        return [(0, self.start), (max(n_steps - 3, 0), self.forward), (n_steps - 1, self.finish)]


def _all_gather_two_level(name, bufs, skip_own=()):
    ex = TwoLevelGather(bufs, skip_own)

    def body(*refs):
        args = refs[:ex.nb], refs[ex.nb:2 * ex.nb], refs[2 * ex.nb:]
        ex.start(*args)
        ex.forward(*args)
        ex.finish(*args)

    return pl.pallas_call(body, name=name, in_specs=ex.any_specs, out_specs=ex.any_specs, out_shape=ex.out_shape,
                          scratch_shapes=ex.sem_shapes)(*ex.bufs)


def _cols_from_blocks(blk):
    return jnp.transpose(blk, (1, 0, 2)).reshape(blk.shape[1], -1)


def _cols_to_blocks(g):
    r, c = g.shape
    return jnp.transpose(g.reshape(r, N_DEV, c // N_DEV), (1, 0, 2))


FIRST_WEIGHTS = ["w_in", "shift_b", "w_lora_w", "a_lora_w", "g_lora_w"]
LATE_WEIGHTS = ["w_proj_a", "w_proj_b", "w_out", "w_ffn1", "w_ffn2"]


def _late_weights(shards):
    ex = TwoLevelGather([shards[n].astype(BF16) for n in LATE_WEIGHTS])

    def finish(results):
        got = dict(zip(LATE_WEIGHTS, results))
        W = {n: got[n].reshape(-1, D) for n in ("w_proj_a", "w_proj_b", "w_out", "w_ffn2")}
        W["w_ffn1"] = got["w_ffn1"].reshape(N_DEV, D, -1)
        return W
    return ex, finish


def _gather_weights(shards):
    def payload(n):
        if n == "w_in":
            return jnp.transpose(shards[n][0]).astype(BF16)
        return shards[n] if n == "shift_b" else shards[n].astype(BF16)
    payloads = [payload(n) for n in FIRST_WEIGHTS]
    got = dict(zip(FIRST_WEIGHTS, _all_gather_two_level("weight_all_gather", payloads, skip_own=(0,))))
    got["w_in"] = _fill_slot("w_in_own_slot", got["w_in"], payloads[0], _mesh_index())
    W = {}
    W["w_sgu_t"], W["w_rw_t"], W["w_gate_t"] = _w_in_groups_t(got["w_in"])
    z = lambda r, c, dt: jnp.zeros((r, c), dt)
    W["w_lora"] = jnp.concatenate([_cols_from_blocks(got["w_lora_w"][:, 0]).astype(F32), z(128 - L_W, D, F32)], axis=0)
    W["a_lora"] = jnp.concatenate([_cols_from_blocks(got["a_lora_w"][:, 0]).astype(F32), z(128 - L_A, D, F32)], axis=0)
    W["g_lora"] = jnp.concatenate([_cols_from_blocks(got["g_lora_w"][:, 0]).astype(F32), z(256 - L_G, D, F32)], axis=0)
    sb = _cols_from_blocks(got["shift_b"][:, 0])
    W["sb"] = jnp.concatenate([sb[:, :3 * D], sb[:, 3 * D:3 * D + L_W], z(2, 128 - L_W, F32),
                               sb[:, 3 * D + L_W:3 * D + L_W + L_A], z(2, 128 - L_A, F32),
                               sb[:, 3 * D + L_W + L_A:], z(2, 256 - L_G, F32)], axis=1)
    return W


def _replicated_weights(rep):
    W = {n: rep[n] for n in ("g_mix", "sgu_ln_w", "sgu_ln_b", "w0", "a0", "k_k", "k_a", "r_k", "ln_x_w", "ln_x_b",
                             "g_ffn")}
    W["g_final"] = rep["g_final"].reshape(1, D)
    W["sgu_w"] = rep["sgu_w"][0]
    W["sgu_bt"] = jnp.transpose(rep["sgu_b"][0])
    return W


def _late_grad_blocks(G):
    blocks = {n: G[n].reshape(N_DEV, -1, D) for n in ("w_proj_a", "w_proj_b", "w_out", "w_ffn2")}
    blocks["w_ffn1"] = G["w_ffn1"]
    return Exchange([blocks[n] for n in LATE_WEIGHTS] + [G["sgu_w"].reshape(SGU_G * SGU_C, SGU_C).astype(GRAD_PAYLOAD)],
                    [False] * len(LATE_WEIGHTS) + [True])


def _first_grad_blocks(G):
    sbg = G["sb"]
    c = 3 * D
    sb = jnp.concatenate([sbg[:, :c], sbg[:, c:c + L_W], sbg[:, c + 128:c + 128 + L_A],
                          sbg[:, c + 256:c + 256 + L_G]], axis=1)
    return {
        "shift_b": _cols_to_blocks(sb),
        "w_lora_w": _cols_to_blocks(G["w_lora"][:L_W]), "a_lora_w": _cols_to_blocks(G["a_lora"][:L_A]),
        "g_lora_w": _cols_to_blocks(G["g_lora"][:L_G]),
    }


def _replicated_grads(G):
    small = {n: G[n] for n in ("g_mix", "sgu_ln_w", "sgu_ln_b", "w0", "a0", "k_k", "k_a", "r_k", "ln_x_w", "ln_x_b",
                               "g_ffn", "g_final")}
    small["sgu_w"] = G["sgu_w"]
    small["sgu_b"] = jnp.transpose(G["sgu_bt"])
    return small


def kernel(x, g_mix, w_in, sgu_ln_w, sgu_ln_b, sgu_w, sgu_b, w_proj_a, shift_b, w_lora_w, w0, a_lora_w, a0, g_lora_w, k_k, k_a, r_k, ln_x_w, ln_x_b, w_proj_b, w_out, g_ffn, w_ffn1, w_ffn2, g_final, loss_target, m_g_mix, m_w_in, m_sgu_ln_w, m_sgu_ln_b, m_sgu_w, m_sgu_b, m_w_proj_a, m_shift_b, m_w_lora_w, m_w0, m_a_lora_w, m_a0, m_g_lora_w, m_k_k, m_k_a, m_r_k, m_ln_x_w, m_ln_x_b, m_w_proj_b, m_w_out, m_g_ffn, m_w_ffn1, m_w_ffn2, m_g_final, v_g_mix, v_w_in, v_sgu_ln_w, v_sgu_ln_b, v_sgu_w, v_sgu_b, v_w_proj_a, v_shift_b, v_w_lora_w, v_w0, v_a_lora_w, v_a0, v_g_lora_w, v_k_k, v_k_a, v_r_k, v_ln_x_w, v_ln_x_b, v_w_proj_b, v_w_out, v_g_ffn, v_w_ffn1, v_w_ffn2, v_g_final):
    env = dict(locals())
    weights = {n: env[n] for n in WEIGHT_ORDER}
    moms = {n: env["m_" + n] for n in WEIGHT_ORDER}
    vars_ = {n: env["v_" + n] for n in WEIGHT_ORDER}

    shards = {n: weights[n] for n, _, _ in SHARDED}
    W = _gather_weights(shards)
    W.update(_replicated_weights({n: weights[n] for n, _ in REPLICATED}))
    in_flight = {}

    def send_w_in_grads(G):
        blocks = _w_in_grad_blocks(G["w_sgu_t"], G["w_rw_t"], G["w_gate_t"])
        got = _pair_exchange("grad_pair_exchange", blocks)
        core = lax.axis_index("c").astype(jnp.int32).reshape(1)
        sums = _pair_sum("grad_pair_sum", blocks, got, core)
        in_flight["sems"], in_flight["sums"], in_flight["land"], token = _chip_exchange_start("grad_chip_start", sums)
        return token

    loss_part, dx, G, late_slots = _local_step(x[0], loss_target[0], W, late_weights=_late_weights(shards),
                                               early_grads=_late_grad_blocks, w_in_grads_ready=send_w_in_grads)

    slots = dict(zip(LATE_WEIGHTS, late_slots))
    blocks = _first_grad_blocks(G)
    small = _replicated_grads(G)
    small_parts = [small[n] for n, _ in PACKED] + [jnp.full((PACK_W,), loss_part, F32)]
    rest = [n for n in FIRST_WEIGHTS if n != "w_in"]
    res = _exchange("grad_exchange", [blocks[n] for n in rest] + [_pack_rows(small_parts, _SMALL_ROWS, F32)],
                    [False] * len(rest) + [True])
    slots.update(zip(rest, res[:-1]))
    small_slots = res[-1]
    sums, chip_slots = _chip_exchange_wait("grad_chip_wait", in_flight["sems"], in_flight["sums"], in_flight["land"],
                                           after=small_slots)
    my_chip = (2 * lax.axis_index("x") + lax.axis_index("y")).astype(jnp.int32).reshape(1)
    slots["w_in"] = _fill_slot("grad_own_slot", chip_slots, sums, my_chip, src_idx=my_chip)

    outs = [dict(), dict(), dict(), dict()]
    for n, _, _ in SHARDED:
        if n == "w_in":
            res = _adamw("adamw_" + n, slots[n], *[jnp.transpose(t, (2, 0, 1)) for t in (weights[n], moms[n], vars_[n])])
            res = [jnp.transpose(t, (1, 2, 0)) for t in res]
        else:
            res = _adamw("adamw_" + n, slots[n], weights[n], moms[n], vars_[n])
        for k in range(4):
            outs[k][n] = res[k]

    sgu_shape = (SGU_G * SGU_C, SGU_C)
    res = _adamw("adamw_sgu_w", late_slots[len(LATE_WEIGHTS)], *[t.reshape(sgu_shape) for t in
                                                                  (weights["sgu_w"], moms["sgu_w"], vars_["sgu_w"])])
    for k in range(4):
        outs[k]["sgu_w"] = res[k].reshape(weights["sgu_w"].shape)

    def packed(d):
        return _pack_rows([d[n] for n, _ in PACKED], _SMALL_ROWS, F32)
    small_out = _adamw("adamw_replicated", small_slots, packed(weights), packed(moms), packed(vars_))
    for k in range(4):
        flat = small_out[k].reshape(-1)
        off = 0
        for (n, s), size in zip(PACKED, _SMALL_SIZES):
            outs[k][n] = flat[off:off + size].reshape(s)
            off += size
    loss = small_out[0].reshape(-1)[_LOSS_AT]
    return (loss, dx[None], *[outs[0][n] for n in WEIGHT_ORDER], *[outs[1][n] for n in WEIGHT_ORDER],
            *[outs[2][n] for n in WEIGHT_ORDER], *[outs[3][n] for n in WEIGHT_ORDER])
```

```python
import functools
import numpy as np
import jax
import jax.numpy as jnp
from jax import lax
from jax.experimental import pallas as pl
from jax.experimental.pallas import tpu as pltpu

F32 = jnp.float32
BF16 = jnp.bfloat16

D = 1024
NH, HN = 16, 64
NP, PW = NH // 2, 2 * HN
SGU_G, SGU_C = 8, 128
L_W, L_A, L_G = 64, 64, 160
C_B = 3 * D + L_W + L_A + L_G
P_TOTAL = 2 * D + C_B + 2 * D
D_FF = 4 * D
RW_INT = 3 * D + 128 + 128 + 256
NORM_EPS, LN_EPS, GN_EPS = 1e-6, 1e-5, 64e-5
N_DEV = 8
LANES = 128
SCAN_C = 64
SOLVE_B = 16
SCAN_PRECISION = lax.Precision.HIGH
SCAN_OUT_PRECISION = lax.Precision.DEFAULT
GRAD_PAYLOAD = BF16
VMEM_LIMIT = 56 * 1024 * 1024
MATMUL_VMEM_BUDGET = 40 * 1024 * 1024
STEP_COST_BYTES = 512 * 1024
HBM_COST_RATIO = 3

ADAM_LR, ADAM_B1, ADAM_B2, ADAM_EPS, ADAM_WD, ADAM_STEP = 0.001, 0.9, 0.999, 1e-08, 0.01, 10

SHARDED = [
    ("w_in", (D, P_TOTAL), 1), ("w_proj_a", (D, D), 0), ("shift_b", (2, C_B), 1), ("w_lora_w", (L_W, D), 1),
    ("a_lora_w", (L_A, D), 1), ("g_lora_w", (L_G, D), 1), ("w_proj_b", (D, D), 0), ("w_out", (D, D), 0),
    ("w_ffn1", (D, D_FF), 1), ("w_ffn2", (D_FF, D), 0),
]
REPLICATED = [
    ("g_mix", (1, D)), ("sgu_ln_w", (1, D)), ("sgu_ln_b", (1, D)), ("sgu_w", (1, SGU_G, SGU_C, SGU_C)),
    ("sgu_b", (1, SGU_G, SGU_C)), ("w0", (1, D)), ("a0", (1, D)), ("k_k", (1, D)), ("k_a", (1, D)), ("r_k", (1, D)),
    ("ln_x_w", (1, D)), ("ln_x_b", (1, D)), ("g_ffn", (1, D)), ("g_final", (D,)),
]
WEIGHT_ORDER = ["g_mix", "w_in", "sgu_ln_w", "sgu_ln_b", "sgu_w", "sgu_b", "w_proj_a", "shift_b", "w_lora_w", "w0",
                "a_lora_w", "a0", "g_lora_w", "k_k", "k_a", "r_k", "ln_x_w", "ln_x_b", "w_proj_b", "w_out", "g_ffn",
                "w_ffn1", "w_ffn2", "g_final"]


def _shard_shape(shape, axis):
    s = list(shape)
    s[axis] //= N_DEV
    return tuple(s)


def _round_up(n, m):
    return (n + m - 1) // m * m


def _pick(n, target):
    if n <= target:
        return n
    best = None
    for t in range(LANES, target + 1, LANES):
        if n % t == 0:
            best = t
    assert best is not None, (n, target)
    return best


def _matmul(name, a, b, mode, out_dtype=F32, tm=2048, tn=1024, tk=2048, out_blocks=None, epilogue=None, extras=(),
            out_dtypes=(), after=None, whole_rows=False):
    b_blocks = b.shape[0] if b.ndim == 3 else None
    bshape = b.shape if b.ndim == 2 else (b.shape[1], b.shape[0] * b.shape[2])
    if mode == "nn":
        (M, K), (K2, N) = a.shape, bshape
    elif mode == "nt":
        (M, K), (N, K2) = a.shape, bshape
    else:
        (K, M), (K2, N) = a.shape, bshape
    assert K == K2, (name, a.shape, b.shape)
    assert b_blocks is None or mode != "tn"
    assert out_blocks is None or mode == "tn"
    tn = min(tn, N // (out_blocks or 1), bshape[1] // b_blocks if (b_blocks and mode == "nn") else tn)
    tk = min(tk, bshape[1] // b_blocks if (b_blocks and mode == "nt") else tk)
    tm, tn, tk = _pick(M, tm), _pick(N, tn), _pick(K, tk)

    def vmem_bytes(tm, tk):
        tiles = tm * tk * a.dtype.itemsize + tk * tn * b.dtype.itemsize
        for dt in (out_dtypes if epilogue else (out_dtype,)):
            tiles += tm * tn * jnp.dtype(dt).itemsize
        for x in extras:
            arr = x[0] if isinstance(x, tuple) else x
            tiles += (tm if arr.shape[0] > 1 else 1) * tn * arr.dtype.itemsize
        return 2 * tiles + (tm * tn * 4 if K // tk > 1 else 0)

    def cost(tm, tk):
        ni, nj, nk = M // tm, N // tn, K // tk
        steps = ni * nj * nk
        acc_passes = steps * tm * tn * 8 if nk > 1 else 0
        a_reads = M * K * a.dtype.itemsize * (nj if nk > 1 else 1)
        b_reads = K * N * b.dtype.itemsize * (ni if (nj > 1 or nk > 1) else 1)
        return (steps * STEP_COST_BYTES + acc_passes + vmem_bytes(tm, tk) // 2
                + HBM_COST_RATIO * (a_reads + b_reads))

    options = [(m, k) for m in ({M} if whole_rows else {_pick(M, max(t, LANES)) for t in (tm, tm // 2, tm // 4)})
               for k in {_pick(K, max(t, LANES)) for t in (tk, tk // 2, tk // 4)}
               if vmem_bytes(m, k) <= MATMUL_VMEM_BUDGET]
    tm, tk = min(options, key=lambda o: cost(*o))
    nk = K // tk
    dims = {"nn": (((1,), (0,)), ((), ())), "nt": (((1,), (1,)), ((), ())), "tn": (((0,), (0,)), ((), ()))}[mode]

    n_x, n_o = len(extras), len(out_dtypes) if epilogue else 1
    n_after = 0 if after is None else 1

    def body(a_ref, b_ref, *rest):
        x_refs, o_refs, acc = rest[:n_x], rest[n_x + n_after:n_x + n_after + n_o], rest[n_x + n_after + n_o:]
        part = lax.dot_general(a_ref[...].astype(BF16), b_ref[...].astype(BF16), dims, preferred_element_type=F32)

        def finish(res):
            outs = epilogue(res, *[r[...] for r in x_refs]) if epilogue else (res,)
            for r, v in zip(o_refs, outs):
                r[...] = v.astype(r.dtype)

        if nk == 1:
            finish(part)
            return
        acc_ref, k = acc[0], pl.program_id(2)

        @pl.when(k == 0)
        def _():
            acc_ref[...] = part

        @pl.when(k > 0)
        def _():
            acc_ref[...] += part

        @pl.when(k == nk - 1)
        def _():
            finish(acc_ref[...])

    a_spec = {"nn": pl.BlockSpec((tm, tk), lambda i, j, k: (i, k)), "nt": pl.BlockSpec((tm, tk), lambda i, j, k: (i, k)),
              "tn": pl.BlockSpec((tk, tm), lambda i, j, k: (k, i))}[mode]
    b_spec = {"nn": pl.BlockSpec((tk, tn), lambda i, j, k: (k, j)), "nt": pl.BlockSpec((tn, tk), lambda i, j, k: (j, k)),
              "tn": pl.BlockSpec((tk, tn), lambda i, j, k: (k, j))}[mode]
    if b_blocks and mode == "nn":
        per = b.shape[2] // tn
        b_spec = pl.BlockSpec((None, tk, tn), lambda i, j, k: (j // per, k, j % per))
    elif b_blocks:
        per = b.shape[2] // tk
        b_spec = pl.BlockSpec((None, tn, tk), lambda i, j, k: (k // per, j, k % per))
    out_spec = pl.BlockSpec((tm, tn), lambda i, j, k: (i, j))
    out_shape = jax.ShapeDtypeStruct((M, N), out_dtype)
    if out_blocks:
        per_o = N // out_blocks // tn
        out_spec = pl.BlockSpec((None, tm, tn), lambda i, j, k: (j // per_o, i, j % per_o))
        out_shape = jax.ShapeDtypeStruct((out_blocks, M, N // out_blocks), out_dtype)
    x_specs, x_args = [], []
    for x in extras:
        arr, off = x if isinstance(x, tuple) else (x, 0)
        if arr.shape[0] == 1:
            x_specs.append(pl.BlockSpec((1, tn), lambda i, j, k: (0, j)))
        else:
            x_specs.append(pl.BlockSpec((tm, tn), lambda i, j, k, off=off: (i, j + off)))
        x_args.append(arr)
    res = pl.pallas_call(
        body, name=name, grid=(M // tm, N // tn, nk),
        in_specs=[a_spec, b_spec] + x_specs + [pl.BlockSpec(memory_space=pl.ANY)] * n_after,
        out_specs=[out_spec] * n_o if epilogue else out_spec,
        out_shape=[jax.ShapeDtypeStruct((M, N), dt) for dt in out_dtypes] if epilogue else out_shape,
        scratch_shapes=[pltpu.VMEM((tm, tn), F32)] if nk > 1 else [],
        compiler_params=pltpu.CompilerParams(dimension_semantics=("parallel", "parallel", "arbitrary"),
                                             vmem_limit_bytes=VMEM_LIMIT),
    )(a, b, *x_args, *([after] if n_after else []))
    return res


class Rows:
    def __init__(self, arr, width=None, cb=0):
        self.arr, self.width, self.cb = arr, (arr.shape[1] if width is None else width), cb


class Heads:
    def __init__(self, arr):
        self.arr = arr


class Halo:
    def __init__(self, arr, side):
        self.arr, self.side = arr, side


def _rows_call(name, fn, ins, consts, outs, accs=(), tm=256, with_pid=False):
    T = next(o.arr.shape[1] if isinstance(o, Heads) else o.arr.shape[0] for o in ins if not isinstance(o, Halo))
    tm = min(tm, T)
    n_tiles = T // tm
    n_in, n_c, n_out = len(ins), len(consts), len(outs)
    in_specs, args = [], []
    for o in ins:
        if isinstance(o, Rows):
            in_specs.append(pl.BlockSpec((tm, o.width), lambda i, cb=o.cb: (i, cb)))
        elif isinstance(o, Heads):
            in_specs.append(pl.BlockSpec((NP, tm, PW), lambda i: (0, i, 0)))
        else:
            w = o.arr.shape[1]
            if o.side < 0:
                in_specs.append(pl.BlockSpec((8, w), lambda i: (jnp.maximum(i * (tm // 8) - 1, 0), 0)))
            else:
                in_specs.append(pl.BlockSpec((8, w), lambda i: (jnp.minimum((i + 1) * (tm // 8), T // 8 - 1), 0)))
        args.append(o.arr)
    for c in consts:
        in_specs.append(pl.BlockSpec(c.shape, lambda i, nd=c.ndim: (0,) * nd))
        args.append(c)
    out_specs, out_shape = [], []
    for o in outs:
        if o[0] == "rows":
            out_specs.append(pl.BlockSpec((tm, o[1]), lambda i: (i, 0)))
            out_shape.append(jax.ShapeDtypeStruct((T, o[1]), o[2]))
        else:
            out_specs.append(pl.BlockSpec((NP, tm, PW), lambda i: (0, i, 0)))
            out_shape.append(jax.ShapeDtypeStruct((NP, T, PW), o[1]))
    for shape, dt in accs:
        out_specs.append(pl.BlockSpec(shape, lambda i, nd=len(shape): (0,) * nd))
        out_shape.append(jax.ShapeDtypeStruct(shape, dt))

    def body(*refs):
        i = pl.program_id(0)
        vals = []
        vals = [r[...] for r in refs[:n_in + n_c]]
        res = fn(i, n_tiles, *vals) if with_pid else fn(*vals)
        out_refs = refs[n_in + n_c:]
        for r, v in zip(out_refs[:n_out], res[:n_out]):
            r[...] = v.astype(r.dtype)
        if accs:
            @pl.when(i == 0)
            def _():
                for r in out_refs[n_out:]:
                    r[...] = jnp.zeros_like(r)

            for r, v in zip(out_refs[n_out:], res[n_out:]):
                r[...] += v.astype(r.dtype)

    res = pl.pallas_call(
        body, name=name, grid=(n_tiles,), in_specs=in_specs, out_specs=out_specs, out_shape=out_shape,
        compiler_params=pltpu.CompilerParams(dimension_semantics=("arbitrary",), vmem_limit_bytes=VMEM_LIMIT),
    )(*args)
    return res


def _rms(x, g):
    return x * lax.rsqrt(jnp.mean(x * x, axis=-1, keepdims=True) + NORM_EPS) * g


def _gelu(x):
    return 0.5 * x * (1.0 + lax.erf(x * 0.7071067811865476))


def _sigmoid(x):
    return 1.0 / (1.0 + jnp.exp(-x))


def _bdot(a, b):
    return jnp.dot(a.astype(BF16), b.astype(BF16), preferred_element_type=F32)


def _to_heads(x):
    return jnp.concatenate([x[:, p * PW:(p + 1) * PW][None] for p in range(NP)], axis=0)


def _from_heads(xp):
    return jnp.concatenate([xp[p] for p in range(NP)], axis=-1)


def _head_sum(xp):
    low = lax.broadcasted_iota(jnp.int32, xp.shape, xp.ndim - 1) < HN
    both = jnp.sum(xp, axis=-1, keepdims=True)
    first = jnp.sum(jnp.where(low, xp, 0.0), axis=-1, keepdims=True)
    return jnp.where(low, first, both - first)


def _split_pairs(xp):
    return jnp.concatenate([xp[:, :, :HN], xp[:, :, HN:]], axis=0)


def _join_pairs(xh):
    return jnp.concatenate([xh[:NP], xh[NP:]], axis=-1)


def _sgu_fn(p, ln_w, ln_b, sw, sbt):
    z = _gelu(p)
    u, v = z[:, :D], z[:, D:]
    mu = jnp.mean(v, axis=-1, keepdims=True)
    var = jnp.mean(jnp.square(v - mu), axis=-1, keepdims=True)
    vn = (v - mu) * lax.rsqrt(var + LN_EPS) * ln_w + ln_b
    ri = lax.broadcasted_iota(jnp.int32, (SGU_C, SGU_C), 0)
    ci = lax.broadcasted_iota(jnp.int32, (SGU_C, SGU_C), 1)
    mask = (ci <= ri).astype(F32)
    dg = D // SGU_G
    parts = []
    for g in range(SGU_G):
        parts.append(_bdot(sw[g] * mask, vn[:, g * dg:(g + 1) * dg]) + sbt[:, g:g + 1])
    return u * jnp.concatenate(parts, axis=-1)


def _pre_fn(qr, qk, qv, qxw, qxa, qxg, wl, w0, al, a0, gl, k_k, k_a):
    w = -jax.nn.softplus(-(w0 + _bdot(jnp.tanh(qxw), wl))) - 0.5
    lw = -jnp.exp(w)
    aa = _sigmoid(a0 + _bdot(qxa, al))
    g = _bdot(_sigmoid(qxg), gl)
    kk = _to_heads(qk * k_k)
    kk = kk / jnp.maximum(jnp.sqrt(_head_sum(kk * kk)), 1e-12)
    k2 = qk * (1.0 + (aa - 1.0) * k_a)
    return _to_heads(qr), _to_heads(lw), _to_heads(k2), _to_heads(qv), kk, _to_heads(aa), g


def _post_fn(o, r, k2, v, g, ln_w, ln_b, r_k):
    mu = _head_sum(o) * (1.0 / HN)
    d = o - mu
    var = _head_sum(d * d) * (1.0 / HN)
    on = d * lax.rsqrt(var + GN_EPS) * ln_w + ln_b
    bonus = _head_sum(r * k2 * r_k) * v
    return _from_heads(on + bonus) * g


def _gate_fn(pg, ya, yb):
    return _sigmoid(pg[:, :D]) * ya + _sigmoid(pg[:, D:]) * yb


def _bmm(x, y, cx, cy, out_path=False):
    return lax.dot_general(x, y, (((cx,), (cy,)), ((0,), (0,))),
                           precision=SCAN_OUT_PRECISION if out_path else SCAN_PRECISION, preferred_element_type=F32)


def _unit_lower_inverse(M):
    C = M.shape[1]
    ti = lax.broadcasted_iota(jnp.int32, (C, C), 0)
    tj = lax.broadcasted_iota(jnp.int32, (C, C), 1)
    eye = (ti == tj).astype(F32)
    same = lambda b: (ti // b == tj // b).astype(F32)
    X = -(M * same(SOLVE_B))
    inv = eye + X
    span = 1
    while 2 * span < SOLVE_B:
        X = _bmm(X, X, 2, 1)
        inv = inv + _bmm(inv, X, 2, 1)
        span *= 2
    b = SOLVE_B
    while b < C:
        low = M * (same(2 * b) - same(b))
        inv = inv - _bmm(_bmm(inv, low, 2, 1, out_path=True), inv, 2, 1, out_path=True)
        b *= 2
    return inv


@jax.custom_vjp
def _unit_lower_solve(inv, M, y):
    return _bmm(inv, y, 2, 1)


def _unit_lower_solve_fwd(inv, M, y):
    u = _bmm(inv, y, 2, 1)
    return u, (inv, u)


def _unit_lower_solve_bwd(res, du):
    inv, u = res
    dy = _bmm(inv, du, 1, 1)
    return jnp.zeros_like(inv), -_bmm(dy, u, 2, 2), dy


_unit_lower_solve.defvjp(_unit_lower_solve_fwd, _unit_lower_solve_bwd)


def _sum_over_time(x, reverse):
    C = x.shape[1]
    ti = lax.broadcasted_iota(jnp.int32, (C, C), 0)
    tj = lax.broadcasted_iota(jnp.int32, (C, C), 1)
    ones = jnp.broadcast_to(((tj >= ti) if reverse else (tj <= ti)).astype(BF16), (x.shape[0], C, C))
    hi = x.astype(BF16)
    r1 = x - hi.astype(F32)
    mid = r1.astype(BF16)
    lo = (r1 - mid.astype(F32)).astype(BF16)
    dn = (((2,), (1,)), ((0,), (0,)))
    return sum(lax.dot_general(ones, p, dn, preferred_element_type=F32) for p in (lo, mid, hi))


@jax.custom_vjp
def _time_cumsum(lw):
    return _sum_over_time(lw, reverse=False)


_time_cumsum.defvjp(lambda lw: (_sum_over_time(lw, reverse=False), None),
                    lambda _, d: (_sum_over_time(d, reverse=True),))


def _chunk_fn(S0, r, lw, k, v, kk, a, inv=None):
    C = SCAN_C
    bmm = _bmm
    ti = lax.broadcasted_iota(jnp.int32, (C, C), 0)
    tj = lax.broadcasted_iota(jnp.int32, (C, C), 1)
    incl = (tj <= ti).astype(F32)
    strict = (tj < ti).astype(F32)
    cum = _time_cumsum(lw)
    g_in, g_ex, g_inv = jnp.exp(cum), jnp.exp(cum - lw), jnp.exp(-cum)
    kkt, rt = kk * g_ex, r * g_in
    bk = jnp.concatenate([kk * a * g_inv, k * g_inv], axis=1)
    A = bmm(kkt, bk, 2, 2)
    M = A[:, :, :C] * strict
    n_mask = jnp.concatenate([jnp.zeros((C, C), F32), strict], axis=1)
    zv = jnp.concatenate([jnp.zeros_like(v), v], axis=1)
    s0_side = bmm(jnp.concatenate([kkt, rt], axis=1), S0, 2, 2, out_path=True)
    if inv is None:
        inv = lax.stop_gradient(_unit_lower_inverse(M))
    y = _unit_lower_solve(inv, M, s0_side[:, :C] + bmm(A * n_mask, zv, 2, 1, out_path=True))
    z = jnp.concatenate([-y, v], axis=1)
    attn = bmm(rt, bk, 2, 2) * jnp.concatenate([incl, incl], axis=1)
    O = s0_side[:, C:] + bmm(attn, z, 2, 1, out_path=True)
    S1 = (S0 + bmm(z, bk, 1, 1, out_path=True)) * g_in[:, C - 1:C, :]
    return O, S1, inv


def _scan_fwd(r, lw, k, v, kk, a, ex=None, tb=256):
    assert SCAN_C == HN
    T = r.shape[1]
    tb = min(tb, T)
    n_chunks = tb // SCAN_C
    nb = T // tb
    nx = ex.nb if ex else 0

    def body(*refs):
        r_ref, lw_ref, k_ref, v_ref, kk_ref, a_ref = refs[:6]
        x_in, (o_ref, s0_ref), x_out = refs[6:6 + nx], refs[6 + nx:8 + nx], refs[8 + nx:8 + 2 * nx]
        s_ref, sems = refs[8 + 2 * nx], refs[9 + 2 * nx:]

        plan = ex.schedule(nb) if ex else []

        @pl.when(pl.program_id(0) == 0)
        def _():
            s_ref[...] = jnp.zeros_like(s_ref)
            for at, action in plan[:1]:
                action(x_in, x_out, sems)

        def step(c, carry):
            sl = pl.ds(pl.multiple_of(c * SCAN_C, SCAN_C), SCAN_C)
            S0 = s_ref[...]
            O, S1, inv = _chunk_fn(S0, *[_split_pairs(ref[:, sl, :])
                                         for ref in (r_ref, lw_ref, k_ref, v_ref, kk_ref, a_ref)])
            o_ref[:, sl, :] = _join_pairs(O)
            s0_ref[c, 0] = S0
            s0_ref[c, 1] = inv
            s_ref[...] = S1
            return carry

        lax.fori_loop(0, n_chunks, step, 0)

        for at, action in plan[1:]:
            pl.when(pl.program_id(0) == at)(functools.partial(action, x_in, x_out, sems))

    hm = pl.BlockSpec((NP, tb, PW), lambda i: (0, i, 0))
    res = pl.pallas_call(
        body, name="rwkv_scan_fwd", grid=(nb,), in_specs=[hm] * 6 + (ex.any_specs if ex else []),
        out_specs=[hm, pl.BlockSpec((n_chunks, 2, NH, HN, HN), lambda i: (i, 0, 0, 0, 0))]
        + (ex.any_specs if ex else []),
        out_shape=[jax.ShapeDtypeStruct((NP, T, PW), F32), jax.ShapeDtypeStruct((T // SCAN_C, 2, NH, HN, HN), F32)]
        + (ex.out_shape if ex else []),
        scratch_shapes=[pltpu.VMEM((NH, HN, HN), F32)] + (ex.sem_shapes if ex else []),
        compiler_params=pltpu.CompilerParams(dimension_semantics=("arbitrary",), vmem_limit_bytes=VMEM_LIMIT),
    )(r, lw, k, v, kk, a, *(ex.bufs if ex else []))
    return res[0], res[1], list(res[2:])


def _scan_bwd(r, lw, k, v, kk, a, s0s, do, ex=None, tb=128):
    T = r.shape[1]
    tb = min(tb, T)
    n_chunks = tb // SCAN_C
    nb = T // tb
    nx = ex.nb if ex else 0

    def body(*refs):
        r_ref, lw_ref, k_ref, v_ref, kk_ref, a_ref, s0_ref, do_ref = refs[:8]
        x_in, (dr, dlw, dk, dv, dkk, da), x_out = refs[8:8 + nx], refs[8 + nx:14 + nx], refs[14 + nx:14 + 2 * nx]
        ds_ref, sems = refs[14 + 2 * nx], refs[15 + 2 * nx:]

        plan = ex.schedule(nb) if ex else []

        @pl.when(pl.program_id(0) == 0)
        def _():
            ds_ref[...] = jnp.zeros_like(ds_ref)
            for at, action in plan[:1]:
                action(x_in, x_out, sems)

        def step(j, carry):
            c = n_chunks - 1 - j
            sl = pl.ds(pl.multiple_of(c * SCAN_C, SCAN_C), SCAN_C)
            inv = s0_ref[c, 1]
            _, vjp = jax.vjp(lambda *t: _chunk_fn(*t, inv=inv)[:2], s0_ref[c, 0],
                             *[_split_pairs(ref[:, sl, :]) for ref in (r_ref, lw_ref, k_ref, v_ref, kk_ref, a_ref)])
            g = vjp((_split_pairs(do_ref[:, sl, :]), ds_ref[...]))
            ds_ref[...] = g[0]
            for ref, val in zip((dr, dlw, dk, dv, dkk, da), g[1:]):
                ref[:, sl, :] = _join_pairs(val)
            return carry

        lax.fori_loop(0, n_chunks, step, 0)

        for at, action in plan[1:]:
            pl.when(pl.program_id(0) == at)(functools.partial(action, x_in, x_out, sems))

    hm = pl.BlockSpec((NP, tb, PW), lambda i: (0, nb - 1 - i, 0))
    res = pl.pallas_call(
        body, name="rwkv_scan_bwd", grid=(nb,),
        in_specs=[hm] * 6 + [pl.BlockSpec((n_chunks, 2, NH, HN, HN), lambda i: (nb - 1 - i, 0, 0, 0, 0)), hm]
        + (ex.any_specs if ex else []),
        out_specs=[hm] * 6 + (ex.any_specs if ex else []),
        out_shape=[jax.ShapeDtypeStruct((NP, T, PW), F32)] * 6 + (ex.out_shape if ex else []),
        scratch_shapes=[pltpu.VMEM((NH, HN, HN), F32)] + (ex.sem_shapes if ex else []),
        compiler_params=pltpu.CompilerParams(dimension_semantics=("arbitrary",), vmem_limit_bytes=VMEM_LIMIT),
    )(r, lw, k, v, kk, a, s0s, do, *(ex.bufs if ex else []))
    return list(res[:6]), list(res[6:])


def _shift_down(i, p, prev8):
    first = jnp.where(i > 0, prev8[7:8, :], 0.0)
    row = lax.broadcasted_iota(jnp.int32, p.shape, 0)
    return jnp.where(row == 0, first, pltpu.roll(p, 1, axis=0))


def _mix_bwd(dq, p, sb, tm=256):
    def fn(i, n, dq, next8, p, prev8, sb):
        ps = _shift_down(i, p, prev8)
        d1 = dq * sb[1:2]
        last = jnp.where(i < n - 1, next8[0:1, :] * sb[1:2], 0.0)
        row = lax.broadcasted_iota(jnp.int32, dq.shape, 0)
        up = jnp.where(row == dq.shape[0] - 1, last, pltpu.roll(d1, dq.shape[0] - 1, axis=0))
        return (dq * sb[0:1] + up, jnp.sum(dq * p, axis=0, keepdims=True), jnp.sum(dq * ps, axis=0, keepdims=True))
    w = p.shape[1]
    return _rows_call("shift_mix_bwd", fn, [Rows(dq), Halo(dq, +1), Rows(p), Halo(p, -1)], [sb], [("rows", w, BF16)],
                      accs=[((1, w), F32), ((1, w), F32)], tm=tm, with_pid=True)


def _local_step(x, target, W, late_weights=None, early_grads=None, w_in_grads_ready=None):
    G = {}
    a = _rows_call("norm_mix_fwd", lambda x, g: (_rms(x, g),), [Rows(x)], [W["g_mix"]], [("rows", D, BF16)])[0]
    p_sgu = _matmul("proj_sgu", a, W["w_sgu_t"], "nt")
    def token_shift(p, sb0, sb1):
        row = lax.broadcasted_iota(jnp.int32, p.shape, 0)
        return p, p * sb0 + jnp.where(row == 0, 0.0, pltpu.roll(p, 1, axis=0)) * sb1
    p_rw, q = _matmul("proj_rwkv", a, W["w_rw_t"], "nt", tn=512, whole_rows=True, epilogue=token_shift,
                      extras=[W["sb"][0:1], W["sb"][1:2]], out_dtypes=(F32, F32))
    p_gate = _matmul("proj_gate", a, W["w_gate_t"], "nt")

    sgu_consts = [W["sgu_ln_w"], W["sgu_ln_b"], W["sgu_w"], W["sgu_bt"]]
    s = _rows_call("sgu_fwd", lambda *t: (_sgu_fn(*t),), [Rows(p_sgu)], sgu_consts, [("rows", D, BF16)], tm=SGU_C)[0]

    q_ins = [Rows(q, D, 0), Rows(q, D, 1), Rows(q, D, 2), Rows(q, 128, 24), Rows(q, 128, 25), Rows(q, 256, 13)]
    pre_consts = [W["w_lora"], W["w0"], W["a_lora"], W["a0"], W["g_lora"], W["k_k"], W["k_a"]]
    r_h, lw_h, k_h, v_h, kk_h, a_h, g_gate = _rows_call(
        "rwkv_pre_fwd", _pre_fn, q_ins, pre_consts, [("heads", F32)] * 6 + [("rows", D, F32)], tm=128)
    o_h, s0s, got = _scan_fwd(r_h, lw_h, k_h, v_h, kk_h, a_h, ex=late_weights[0] if late_weights else None)
    if late_weights:
        W = {**W, **late_weights[1](got)}
    y_a = _matmul("proj_a", s, W["w_proj_a"], "nn")
    post_ins = [Heads(o_h), Heads(r_h), Heads(k_h), Heads(v_h), Rows(g_gate)]
    post_consts = [W[n].reshape(NP, 1, PW) for n in ("ln_x_w", "ln_x_b", "r_k")]
    z_b = _rows_call("rwkv_post_fwd", lambda *t: (_post_fn(*t),), post_ins, post_consts, [("rows", D, BF16)], tm=128)[0]
    y_b, mixed = _matmul("proj_b", z_b, W["w_proj_b"], "nn", extras=[(p_gate, 0), (p_gate, 1), y_a],
                         epilogue=lambda yb, ga, gb, ya: (yb, _sigmoid(ga) * ya + _sigmoid(gb) * yb),
                         out_dtypes=(F32, BF16))
    gate_ins = [Rows(p_gate), Rows(y_a), Rows(y_b)]

    def res1(mo, x, g):
        h1 = x + mo
        return h1, _rms(h1, g)
    h1, f = _matmul("proj_out", mixed, W["w_out"], "nn", extras=[x, W["g_ffn"]], epilogue=res1,
                    out_dtypes=(F32, BF16))

    def relu_sq(u):
        r = jnp.maximum(u, 0.0)
        return r, r * r
    r1, act = _matmul("ffn_up", f, W["w_ffn1"], "nn", epilogue=relu_sq, out_dtypes=(BF16, BF16))
    ff = _matmul("ffn_down", act, W["w_ffn2"], "nn")

    def head(h1, ff, tgt, g):
        def f_(h1, ff, g):
            y = _rms(h1 + ff, g)
            return 0.5 * jnp.sum(jnp.mean(jnp.square(y - tgt), axis=-1))
        loss, (dh2, _, dg) = jax.value_and_grad(f_, argnums=(0, 1, 2))(h1, ff, g)
        return dh2, jnp.full((8, LANES), loss, F32), dg
    dh2, loss_acc, G["g_final"] = _rows_call("loss_head", head, [Rows(h1), Rows(ff), Rows(target)], [W["g_final"]],
                                             [("rows", D, F32)], accs=[((8, LANES), F32), ((1, D), F32)])

    d_u1 = _matmul("ffn_down_dx", dh2, W["w_ffn2"], "nt", extras=[r1], out_dtypes=(BF16,),
                   epilogue=lambda d_act, r: (d_act * 2.0 * r.astype(F32),))[0]
    G["w_ffn2"] = _matmul("ffn_down_dw", act, dh2, "tn", out_dtype=GRAD_PAYLOAD)
    d_f = _matmul("ffn_up_dx", d_u1, W["w_ffn1"], "nt")
    G["w_ffn1"] = _matmul("ffn_up_dw", f, d_u1, "tn", out_blocks=N_DEV, out_dtype=GRAD_PAYLOAD)

    def res1_bwd(h1, d_f, dh2, g):
        _, vjp = jax.vjp(_rms, h1, g)
        dh, dg = vjp(d_f)
        return dh2 + dh, dg
    dh1, G["g_ffn"] = _rows_call("residual_norm_bwd", res1_bwd, [Rows(h1), Rows(d_f), Rows(dh2)],
                                 [W["g_ffn"]], [("rows", D, F32)], accs=[((1, D), F32)])
    d_mixed = _matmul("proj_out_dx", dh1, W["w_out"], "nt")
    G["w_out"] = _matmul("proj_out_dw", mixed, dh1, "tn", out_dtype=GRAD_PAYLOAD)

    def gate_bwd(pg, ya, yb, dm):
        _, vjp = jax.vjp(_gate_fn, pg, ya, yb)
        return vjp(dm)
    d_gate, d_ya, d_yb = _rows_call("gate_bwd", gate_bwd, gate_ins + [Rows(d_mixed)], [],
                                    [("rows", 2 * D, BF16), ("rows", D, BF16), ("rows", D, BF16)])

    d_s = _matmul("proj_a_dx", d_ya, W["w_proj_a"], "nt")
    G["w_proj_a"] = _matmul("proj_a_dw", s, d_ya, "tn", out_dtype=GRAD_PAYLOAD)

    def sgu_bwd(p, ds, *c):
        _, vjp = jax.vjp(_sgu_fn, p, *c)
        return vjp(ds)
    d_p_sgu, G["sgu_ln_w"], G["sgu_ln_b"], G["sgu_w"], G["sgu_bt"] = _rows_call(
        "sgu_bwd", sgu_bwd, [Rows(p_sgu), Rows(d_s)], sgu_consts, [("rows", 2 * D, BF16)],
        accs=[((1, D), F32), ((1, D), F32), ((SGU_G, SGU_C, SGU_C), F32), ((SGU_C, SGU_G), F32)], tm=SGU_C)

    d_zb = _matmul("proj_b_dx", d_yb, W["w_proj_b"], "nt")
    G["w_proj_b"] = _matmul("proj_b_dw", z_b, d_yb, "tn", out_dtype=GRAD_PAYLOAD)

    def post_bwd(o, r, k2, v, g, dz, *c):
        _, vjp = jax.vjp(_post_fn, o, r, k2, v, g, *c)
        return vjp(dz)
    do_h, dr1, dk1, dv1, d_g, g_lnw, g_lnb, g_rk = _rows_call(
        "rwkv_post_bwd", post_bwd, post_ins + [Rows(d_zb)], post_consts, [("heads", F32)] * 4 + [("rows", D, F32)],
        accs=[((NP, 1, PW), F32)] * 3, tm=128)
    G["ln_x_w"], G["ln_x_b"], G["r_k"] = (t.reshape(1, D) for t in (g_lnw, g_lnb, g_rk))
    (dr2, dlw, dk2, dv2, dkk, daa), early = _scan_bwd(r_h, lw_h, k_h, v_h, kk_h, a_h, s0s, do_h,
                                                      ex=early_grads(G) if early_grads else None)

    def pre_bwd(qr, qk, qv, qxw, qxa, qxg, dr1, dr2, dlw, dk1, dk2, dv1, dv2, dkk, daa, dg, *c):
        _, vjp = jax.vjp(_pre_fn, qr, qk, qv, qxw, qxa, qxg, *c)
        g = vjp((dr1 + dr2, dlw, dk1 + dk2, dv1 + dv2, dkk, daa, dg))
        dq = jnp.concatenate(g[:6], axis=-1)
        return (dq,) + tuple(g[6:])
    pre_b_ins = q_ins + [Heads(dr1), Heads(dr2), Heads(dlw), Heads(dk1), Heads(dk2), Heads(dv1), Heads(dv2),
                         Heads(dkk), Heads(daa), Rows(d_g)]
    d_q, G["w_lora"], G["w0"], G["a_lora"], G["a0"], G["g_lora"], G["k_k"], G["k_a"] = _rows_call(
        "rwkv_pre_bwd", pre_bwd, pre_b_ins, pre_consts, [("rows", RW_INT, F32)],
        accs=[((128, D), F32), ((1, D), F32), ((128, D), F32), ((1, D), F32), ((256, D), F32), ((1, D), F32),
              ((1, D), F32)], tm=128)
    d_p_rw, dsb0, dsb1 = _mix_bwd(d_q, p_rw, W["sb"])
    G["sb"] = jnp.concatenate([dsb0, dsb1], axis=0)

    G["w_sgu_t"] = _matmul("proj_sgu_dw", d_p_sgu, a, "tn", out_dtype=GRAD_PAYLOAD)
    G["w_rw_t"] = _matmul("proj_rwkv_dw", d_p_rw, a, "tn", out_dtype=GRAD_PAYLOAD)
    G["w_gate_t"] = _matmul("proj_gate_dw", d_gate, a, "tn", out_dtype=GRAD_PAYLOAD)
    token = w_in_grads_ready(G) if w_in_grads_ready else None
    da1 = _matmul("proj_sgu_dx", d_p_sgu, W["w_sgu_t"], "nn", after=token)
    da2 = _matmul("proj_rwkv_dx", d_p_rw, W["w_rw_t"], "nn", after=token)
    da3 = _matmul("proj_gate_dx", d_gate, W["w_gate_t"], "nn", after=token)

    def norm1_bwd(x, da1, da2, da3, dh1, g):
        _, vjp = jax.vjp(_rms, x, g)
        dx, dg = vjp(da1 + da2 + da3)
        return dh1 + dx, dg
    dx, G["g_mix"] = _rows_call("norm_mix_bwd", norm1_bwd, [Rows(x), Rows(da1), Rows(da2), Rows(da3), Rows(dh1)],
                                [W["g_mix"]], [("rows", D, F32)], accs=[((1, D), F32)])
    return loss_acc[0, 0], dx, G, early


class Exchange:
    def __init__(self, bufs, gathers):
        self.bufs, self.gathers, self.nb = list(bufs), list(gathers), len(bufs)
        self.any_specs = [pl.BlockSpec(memory_space=pl.ANY)] * self.nb
        self.out_shape = [jax.ShapeDtypeStruct((N_DEV,) + (b.shape if g else b.shape[1:]), b.dtype)
                          for b, g in zip(self.bufs, self.gathers)]
        n = (N_DEV - 1) * self.nb
        self.sem_shapes = [pltpu.SemaphoreType.DMA((n,)), pltpu.SemaphoreType.DMA((n,)),
                           pltpu.SemaphoreType.DMA((self.nb,))]

    def _copies(self, in_refs, out_refs, sems):
        send_sems, recv_sems, local_sems = sems
        x, y, c = lax.axis_index("x"), lax.axis_index("y"), lax.axis_index("c")
        me = 4 * x + 2 * y + c

        def src(b, dest):
            return in_refs[b] if self.gathers[b] else in_refs[b].at[dest]

        local = [pltpu.make_async_copy(src(b, me), out_refs[b].at[me], local_sems.at[b]) for b in range(self.nb)]
        sends, recvs = [], []
        for kbits in range(1, N_DEV):
            px = 1 - x if kbits & 4 else x
            py = 1 - y if kbits & 2 else y
            pc = 1 - c if kbits & 1 else c
            peer = 4 * px + 2 * py + pc
            for b in range(self.nb):
                s = (kbits - 1) * self.nb + b
                sends.append(pltpu.make_async_remote_copy(
                    src_ref=src(b, peer), dst_ref=out_refs[b].at[me], send_sem=send_sems.at[s],
                    recv_sem=recv_sems.at[s], device_id=(px, py, pc), device_id_type=pl.DeviceIdType.MESH))
                recvs.append(pltpu.make_async_remote_copy(
                    src_ref=src(b, peer), dst_ref=out_refs[b].at[peer], send_sem=send_sems.at[s],
                    recv_sem=recv_sems.at[s], device_id=(px, py, pc), device_id_type=pl.DeviceIdType.MESH))
        return local, sends, recvs

    def start(self, in_refs, out_refs, sems):
        local, sends, _ = self._copies(in_refs, out_refs, sems)
        for cp in sends + local:
            cp.start()

    def wait(self, in_refs, out_refs, sems):
        local, sends, recvs = self._copies(in_refs, out_refs, sems)
        for cp in recvs:
            cp.wait_recv()
        for cp in sends:
            cp.wait_send()
        for cp in local:
            cp.wait()

    def schedule(self, n_steps):
        return [(0, self.start), (n_steps - 1, self.wait)]


def _exchange(name, bufs, gather):
    ex = Exchange(bufs, gather if isinstance(gather, (list, tuple)) else [gather] * len(bufs))

    def body(*refs):
        in_refs, out_refs, sems = refs[:ex.nb], refs[ex.nb:2 * ex.nb], refs[2 * ex.nb:]
        ex.start(in_refs, out_refs, sems)
        ex.wait(in_refs, out_refs, sems)

    return pl.pallas_call(body, name=name, in_specs=ex.any_specs, out_specs=ex.any_specs, out_shape=ex.out_shape,
                          scratch_shapes=ex.sem_shapes)(*ex.bufs)


N_CHIP = 4


def _pair_exchange(name, blocks):
    def body(b_ref, got_ref, send_sems, recv_sems):
        x, y, c = lax.axis_index("x"), lax.axis_index("y"), lax.axis_index("c")
        copies = [pltpu.make_async_remote_copy(
            src_ref=b_ref.at[2 * q + 1 - c], dst_ref=got_ref.at[q], send_sem=send_sems.at[q],
            recv_sem=recv_sems.at[q], device_id=(x, y, 1 - c), device_id_type=pl.DeviceIdType.MESH)
            for q in range(N_CHIP)]
        for cp in copies:
            cp.start()
        for cp in copies:
            cp.wait_recv()
        for cp in copies:
            cp.wait_send()

    any_spec = pl.BlockSpec(memory_space=pl.ANY)
    return pl.pallas_call(
        body, name=name, in_specs=[any_spec], out_specs=any_spec,
        out_shape=jax.ShapeDtypeStruct((N_CHIP,) + blocks.shape[1:], blocks.dtype),
        scratch_shapes=[pltpu.SemaphoreType.DMA((N_CHIP,))] * 2,
    )(blocks)


def _pair_sum(name, blocks, got, core):
    _, R, Wd = blocks.shape

    def body(c_ref, a_ref, b_ref, o_ref):
        o_ref[...] = (a_ref[...].astype(F32) + b_ref[...].astype(F32)).astype(o_ref.dtype)

    return pl.pallas_call(
        body, name=name,
        grid_spec=pltpu.PrefetchScalarGridSpec(
            num_scalar_prefetch=1, grid=(N_CHIP,),
            in_specs=[pl.BlockSpec((None, R, Wd), lambda q, c_ref: (2 * q + c_ref[0], 0, 0)),
                      pl.BlockSpec((None, R, Wd), lambda q, c_ref: (q, 0, 0))],
            out_specs=pl.BlockSpec((None, R, Wd), lambda q, c_ref: (q, 0, 0))),
        out_shape=jax.ShapeDtypeStruct(got.shape, blocks.dtype),
        compiler_params=pltpu.CompilerParams(dimension_semantics=("parallel",), vmem_limit_bytes=VMEM_LIMIT),
    )(core, blocks, got)


def _chip_copies(s_ref, land_ref, send_sems, recv_sems):
    x, y, c = lax.axis_index("x"), lax.axis_index("y"), lax.axis_index("c")
    my_q = 2 * x + y
    sends, recvs = [], []
    for kbits in range(1, N_CHIP):
        px = 1 - x if kbits & 2 else x
        py = 1 - y if kbits & 1 else y
        peer_q = 2 * px + py
        sends.append(pltpu.make_async_remote_copy(
            src_ref=s_ref.at[peer_q], dst_ref=land_ref.at[my_q], send_sem=send_sems[kbits - 1],
            recv_sem=recv_sems[kbits - 1], device_id=(px, py, c), device_id_type=pl.DeviceIdType.MESH))
        recvs.append(pltpu.make_async_remote_copy(
            src_ref=s_ref.at[peer_q], dst_ref=land_ref.at[peer_q], send_sem=send_sems[kbits - 1],
            recv_sem=recv_sems[kbits - 1], device_id=(px, py, c), device_id_type=pl.DeviceIdType.MESH))
    return sends, recvs


_HBM = pl.BlockSpec(memory_space=pltpu.HBM)
_SEM = pl.BlockSpec(memory_space=pltpu.SEMAPHORE)
N_CHIP_SEMS = 2 * (N_CHIP - 1)


def _chip_exchange_start(name, sums):
    def body(s_ref, land_ref, *outs):
        sems, token = outs[:N_CHIP_SEMS], outs[N_CHIP_SEMS + 2]
        sends, _ = _chip_copies(s_ref, land_ref, sems[:N_CHIP - 1], sems[N_CHIP - 1:])
        for cp in sends:
            cp.start()
        token[...] = jnp.zeros_like(token)

    res = pl.pallas_call(
        body, name=name, in_specs=(_HBM, _HBM),
        out_specs=(_SEM,) * N_CHIP_SEMS + (_HBM, _HBM, pl.BlockSpec(memory_space=pltpu.VMEM)),
        out_shape=(pltpu.SemaphoreType.DMA(()),) * N_CHIP_SEMS
        + (pltpu.HBM(sums.shape, sums.dtype), pltpu.HBM(sums.shape, sums.dtype), jax.ShapeDtypeStruct((8, LANES), F32)),
        input_output_aliases={0: N_CHIP_SEMS, 1: N_CHIP_SEMS + 1},
        compiler_params=pltpu.CompilerParams(has_side_effects=pltpu.SideEffectType.DATAFLOW_SIDE_EFFECTING),
    )(pltpu.with_memory_space_constraint(sums, pltpu.HBM),
      pltpu.with_memory_space_constraint(lax.empty(sums.shape, sums.dtype), pltpu.HBM))
    return res[:N_CHIP_SEMS], res[N_CHIP_SEMS], res[N_CHIP_SEMS + 1], res[N_CHIP_SEMS + 2]


def _chip_exchange_wait(name, sems, sums_thru, land_thru, after):
    def body(s_ref, land_ref, *rest):
        sems = rest[:N_CHIP_SEMS]
        sends, recvs = _chip_copies(s_ref, land_ref, sems[:N_CHIP - 1], sems[N_CHIP - 1:])
        for cp in sends:
            cp.wait_send()
        for cp in recvs:
            cp.wait_recv()

    return pl.pallas_call(
        body, name=name, in_specs=(_HBM, _HBM) + (_SEM,) * N_CHIP_SEMS + (pl.BlockSpec(memory_space=pl.ANY),),
        out_specs=(_HBM, _HBM),
        out_shape=(pltpu.HBM(sums_thru.shape, sums_thru.dtype), pltpu.HBM(sums_thru.shape, sums_thru.dtype)),
        input_output_aliases={0: 0, 1: 1},
        compiler_params=pltpu.CompilerParams(has_side_effects=pltpu.SideEffectType.DATAFLOW_SIDE_EFFECTING),
    )(sums_thru, land_thru, *sems, after)


def _adamw(name, slots, w, m, v, tr=256):
    unit_mid = w.ndim == 3 and w.shape[1] == 1 and w.shape[0] > 1
    R, Wd = (w.shape[0], w.shape[2]) if unit_mid else w.shape[-2:]
    depth_axis = w.ndim == 3 and not unit_mid
    if R % tr == 0:
        tc = Wd
    else:
        tr, tc = R, (256 if (Wd % 256 == 0 and R > 256) else Wd)
    at = (slice(None), 0, slice(None)) if unit_mid else Ellipsis

    def body(s_ref, w_ref, m_ref, v_ref, g_out, d_out, m_out, v_out):
        g = s_ref[0].astype(F32)
        for j in range(1, slots.shape[0]):
            g = g + s_ref[j].astype(F32)
        m_new = ADAM_B1 * m_ref[at] + (1.0 - ADAM_B1) * g
        v_new = ADAM_B2 * v_ref[at] + (1.0 - ADAM_B2) * jnp.square(g)
        m_hat = m_new / (1.0 - ADAM_B1 ** ADAM_STEP)
        v_hat = v_new / (1.0 - ADAM_B2 ** ADAM_STEP)
        g_out[at] = g
        d_out[at] = -ADAM_LR * (m_hat / (jnp.sqrt(v_hat) + ADAM_EPS) + ADAM_WD * w_ref[at])
        m_out[at] = m_new
        v_out[at] = v_new

    if unit_mid:
        row = pl.BlockSpec((tr, 1, tc), lambda i, j: (i, 0, j))
    elif depth_axis:
        row = pl.BlockSpec((None, tr, tc), lambda i, j: (0, i, j))
    else:
        row = pl.BlockSpec((tr, tc), lambda i, j: (i, j))
    return pl.pallas_call(
        body, name=name, grid=(R // tr, Wd // tc),
        in_specs=[pl.BlockSpec((slots.shape[0], tr, tc), lambda i, j: (0, i, j)), row, row, row],
        out_specs=[row] * 4, out_shape=[jax.ShapeDtypeStruct(w.shape, F32)] * 4,
        compiler_params=pltpu.CompilerParams(dimension_semantics=("parallel", "parallel"),
                                             vmem_limit_bytes=VMEM_LIMIT),
    )(slots, w, m, v)


PACK_W = 1024
PACKED = [(n, s) for n, s in REPLICATED if n != "sgu_w"]
_SMALL_SIZES = [int(np.prod(s)) for _, s in PACKED]
_SMALL_ROWS = _round_up(_round_up(sum(_SMALL_SIZES) + PACK_W, PACK_W) // PACK_W, 8)
_LOSS_AT = sum(_SMALL_SIZES)
W_IN_SHARD = P_TOTAL // N_DEV


def _pack_rows(parts, rows, dtype):
    flat = jnp.concatenate([p.reshape(-1).astype(dtype) for p in parts])
    return jnp.pad(flat, (0, rows * PACK_W - flat.shape[0])).reshape(rows, PACK_W)


def _w_in_groups_t(blocks):
    wt = blocks.reshape(P_TOTAL, D)
    o, c = 2 * D, 2 * D + 3 * D
    z = lambda r: jnp.zeros((r, D), wt.dtype)
    rw = jnp.concatenate([wt[o:c], wt[c:c + L_W], z(128 - L_W), wt[c + L_W:c + L_W + L_A], z(128 - L_A),
                          wt[c + L_W + L_A:o + C_B], z(256 - L_G)], axis=0)
    return wt[:o], rw, wt[o + C_B:]


def _w_in_grad_blocks(g_sgu_t, g_rw_t, g_gate_t):
    c = 3 * D
    full = jnp.concatenate([g_sgu_t, g_rw_t[:c], g_rw_t[c:c + L_W], g_rw_t[c + 128:c + 128 + L_A],
                            g_rw_t[c + 256:c + 256 + L_G], g_gate_t], axis=0)
    return full.reshape(N_DEV, W_IN_SHARD, D)


def _mesh_index():
    me = 4 * lax.axis_index("x") + 2 * lax.axis_index("y") + lax.axis_index("c")
    return me.astype(jnp.int32).reshape(1)


def _fill_slot(name, dst, src, idx, src_idx=None):
    R, Wd = dst.shape[1:]
    scalars = [idx] if src_idx is None else [idx, src_idx]
    if src_idx is None:
        src_spec = pl.BlockSpec((R, Wd), lambda i, *s: (0, 0))
    else:
        src_spec = pl.BlockSpec((None, R, Wd), lambda i, *s: (s[1][0], 0, 0))

    def body(*refs):
        src_ref, out_ref = refs[len(scalars) + 1], refs[len(scalars) + 2]
        out_ref[...] = src_ref[...]

    return pl.pallas_call(
        body, name=name,
        grid_spec=pltpu.PrefetchScalarGridSpec(
            num_scalar_prefetch=len(scalars), grid=(1,),
            in_specs=[pl.BlockSpec(memory_space=pl.ANY), src_spec],
            out_specs=pl.BlockSpec((None, R, Wd), lambda i, *s: (s[0][0], 0, 0))),
        out_shape=jax.ShapeDtypeStruct(dst.shape, dst.dtype),
        input_output_aliases={len(scalars): 0},
        compiler_params=pltpu.CompilerParams(vmem_limit_bytes=VMEM_LIMIT),
    )(*scalars, dst, src)


class TwoLevelGather:
    def __init__(self, bufs, skip_own=()):
        self.bufs, self.nb, self.skip_own = list(bufs), len(bufs), tuple(skip_own)
        self.any_specs = [pl.BlockSpec(memory_space=pl.ANY)] * self.nb
        self.out_shape = [jax.ShapeDtypeStruct((N_DEV,) + b.shape, b.dtype) for b in self.bufs]
        self.sem_shapes = [pltpu.SemaphoreType.DMA((7 * self.nb,)), pltpu.SemaphoreType.DMA((7 * self.nb,)),
                           pltpu.SemaphoreType.DMA((self.nb,))]

    def _copies(self, in_refs, out_refs, sems):
        send_sems, recv_sems, local_sems = sems
        nb = self.nb
        x, y, c = lax.axis_index("x"), lax.axis_index("y"), lax.axis_index("c")
        me, sibling = (x, y, c), (x, y, 1 - c)
        chips = [(1 - x, y), (x, 1 - y), (1 - x, 1 - y)]

        def slot(b, dev):
            return out_refs[b].at[4 * dev[0] + 2 * dev[1] + dev[2]]

        def copy(b, k, block, to, own=False):
            return pltpu.make_async_remote_copy(
                src_ref=in_refs[b] if own else slot(b, block), dst_ref=slot(b, block),
                send_sem=send_sems.at[7 * b + k], recv_sem=recv_sems.at[7 * b + k], device_id=to,
                device_id_type=pl.DeviceIdType.MESH)

        cp = {}
        cp["local"] = [pltpu.make_async_copy(in_refs[b], slot(b, me), local_sems.at[b]) for b in range(nb)
                       if b not in self.skip_own]
        cp["first"] = [copy(b, 0, me, sibling, own=True) for b in range(nb)]
        cp["first"] += [copy(b, 1 + j, me, (*chip, c), own=True) for j, chip in enumerate(chips) for b in range(nb)]
        cp["over_ici"] = [copy(b, 1 + j, (*chip, c), me) for j, chip in enumerate(chips) for b in range(nb)]
        cp["passed"] = [copy(b, 4 + j, (*chip, c), sibling) for j, chip in enumerate(chips) for b in range(nb)]
        cp["from_sibling"] = [copy(b, 0, sibling, me) for b in range(nb)]
        cp["from_sibling"] += [copy(b, 4 + j, (*chip, 1 - c), me) for j, chip in enumerate(chips) for b in range(nb)]
        return cp

    def start(self, in_refs, out_refs, sems):
        cp = self._copies(in_refs, out_refs, sems)
        for c in cp["first"] + cp["local"]:
            c.start()

    def forward(self, in_refs, out_refs, sems):
        cp = self._copies(in_refs, out_refs, sems)
        for arrived, onward in zip(cp["over_ici"], cp["passed"]):
            arrived.wait_recv()
            onward.start()

    def finish(self, in_refs, out_refs, sems):
        cp = self._copies(in_refs, out_refs, sems)
        for c in cp["from_sibling"]:
            c.wait_recv()
        for c in cp["first"] + cp["passed"]:
            c.wait_send()
        for c in cp["local"]:
            c.wait()

    def schedule(self, n_steps):
        return [(0, self.start), (max(n_steps - 3, 0), self.forward), (n_steps - 1, self.finish)]


def _all_gather_two_level(name, bufs, skip_own=()):
    ex = TwoLevelGather(bufs, skip_own)

    def body(*refs):
        args = refs[:ex.nb], refs[ex.nb:2 * ex.nb], refs[2 * ex.nb:]
        ex.start(*args)
        ex.forward(*args)
        ex.finish(*args)

    return pl.pallas_call(body, name=name, in_specs=ex.any_specs, out_specs=ex.any_specs, out_shape=ex.out_shape,
                          scratch_shapes=ex.sem_shapes)(*ex.bufs)


def _cols_from_blocks(blk):
    return jnp.transpose(blk, (1, 0, 2)).reshape(blk.shape[1], -1)


def _cols_to_blocks(g):
    r, c = g.shape
    return jnp.transpose(g.reshape(r, N_DEV, c // N_DEV), (1, 0, 2))


FIRST_WEIGHTS = ["w_in", "shift_b", "w_lora_w", "a_lora_w", "g_lora_w"]
LATE_WEIGHTS = ["w_proj_a", "w_proj_b", "w_out", "w_ffn1", "w_ffn2"]


def _late_weights(shards):
    ex = TwoLevelGather([shards[n].astype(BF16) for n in LATE_WEIGHTS])

    def finish(results):
        got = dict(zip(LATE_WEIGHTS, results))
        W = {n: got[n].reshape(-1, D) for n in ("w_proj_a", "w_proj_b", "w_out", "w_ffn2")}
        W["w_ffn1"] = got["w_ffn1"].reshape(N_DEV, D, -1)
        return W
    return ex, finish


def _gather_weights(shards):
    def payload(n):
        if n == "w_in":
            return jnp.transpose(shards[n][0]).astype(BF16)
        return shards[n] if n == "shift_b" else shards[n].astype(BF16)
    payloads = [payload(n) for n in FIRST_WEIGHTS]
    got = dict(zip(FIRST_WEIGHTS, _all_gather_two_level("weight_all_gather", payloads, skip_own=(0,))))
    got["w_in"] = _fill_slot("w_in_own_slot", got["w_in"], payloads[0], _mesh_index())
    W = {}
    W["w_sgu_t"], W["w_rw_t"], W["w_gate_t"] = _w_in_groups_t(got["w_in"])
    z = lambda r, c, dt: jnp.zeros((r, c), dt)
    W["w_lora"] = jnp.concatenate([_cols_from_blocks(got["w_lora_w"][:, 0]).astype(F32), z(128 - L_W, D, F32)], axis=0)
    W["a_lora"] = jnp.concatenate([_cols_from_blocks(got["a_lora_w"][:, 0]).astype(F32), z(128 - L_A, D, F32)], axis=0)
    W["g_lora"] = jnp.concatenate([_cols_from_blocks(got["g_lora_w"][:, 0]).astype(F32), z(256 - L_G, D, F32)], axis=0)
    sb = _cols_from_blocks(got["shift_b"][:, 0])
    W["sb"] = jnp.concatenate([sb[:, :3 * D], sb[:, 3 * D:3 * D + L_W], z(2, 128 - L_W, F32),
                               sb[:, 3 * D + L_W:3 * D + L_W + L_A], z(2, 128 - L_A, F32),
                               sb[:, 3 * D + L_W + L_A:], z(2, 256 - L_G, F32)], axis=1)
    return W


def _replicated_weights(rep):
    W = {n: rep[n] for n in ("g_mix", "sgu_ln_w", "sgu_ln_b", "w0", "a0", "k_k", "k_a", "r_k", "ln_x_w", "ln_x_b",
                             "g_ffn")}
    W["g_final"] = rep["g_final"].reshape(1, D)
    W["sgu_w"] = rep["sgu_w"][0]
    W["sgu_bt"] = jnp.transpose(rep["sgu_b"][0])
    return W


def _late_grad_blocks(G):
    blocks = {n: G[n].reshape(N_DEV, -1, D) for n in ("w_proj_a", "w_proj_b", "w_out", "w_ffn2")}
    blocks["w_ffn1"] = G["w_ffn1"]
    return Exchange([blocks[n] for n in LATE_WEIGHTS] + [G["sgu_w"].reshape(SGU_G * SGU_C, SGU_C).astype(GRAD_PAYLOAD)],
                    [False] * len(LATE_WEIGHTS) + [True])


def _first_grad_blocks(G):
    sbg = G["sb"]
    c = 3 * D
    sb = jnp.concatenate([sbg[:, :c], sbg[:, c:c + L_W], sbg[:, c + 128:c + 128 + L_A],
                          sbg[:, c + 256:c + 256 + L_G]], axis=1)
    return {
        "shift_b": _cols_to_blocks(sb),
        "w_lora_w": _cols_to_blocks(G["w_lora"][:L_W]), "a_lora_w": _cols_to_blocks(G["a_lora"][:L_A]),
        "g_lora_w": _cols_to_blocks(G["g_lora"][:L_G]),
    }


def _replicated_grads(G):
    small = {n: G[n] for n in ("g_mix", "sgu_ln_w", "sgu_ln_b", "w0", "a0", "k_k", "k_a", "r_k", "ln_x_w", "ln_x_b",
                               "g_ffn", "g_final")}
    small["sgu_w"] = G["sgu_w"]
    small["sgu_b"] = jnp.transpose(G["sgu_bt"])
    return small


def kernel(x, g_mix, w_in, sgu_ln_w, sgu_ln_b, sgu_w, sgu_b, w_proj_a, shift_b, w_lora_w, w0, a_lora_w, a0, g_lora_w, k_k, k_a, r_k, ln_x_w, ln_x_b, w_proj_b, w_out, g_ffn, w_ffn1, w_ffn2, g_final, loss_target, m_g_mix, m_w_in, m_sgu_ln_w, m_sgu_ln_b, m_sgu_w, m_sgu_b, m_w_proj_a, m_shift_b, m_w_lora_w, m_w0, m_a_lora_w, m_a0, m_g_lora_w, m_k_k, m_k_a, m_r_k, m_ln_x_w, m_ln_x_b, m_w_proj_b, m_w_out, m_g_ffn, m_w_ffn1, m_w_ffn2, m_g_final, v_g_mix, v_w_in, v_sgu_ln_w, v_sgu_ln_b, v_sgu_w, v_sgu_b, v_w_proj_a, v_shift_b, v_w_lora_w, v_w0, v_a_lora_w, v_a0, v_g_lora_w, v_k_k, v_k_a, v_r_k, v_ln_x_w, v_ln_x_b, v_w_proj_b, v_w_out, v_g_ffn, v_w_ffn1, v_w_ffn2, v_g_final):
    env = dict(locals())
    weights = {n: env[n] for n in WEIGHT_ORDER}
    moms = {n: env["m_" + n] for n in WEIGHT_ORDER}
    vars_ = {n: env["v_" + n] for n in WEIGHT_ORDER}

    shards = {n: weights[n] for n, _, _ in SHARDED}
    W = _gather_weights(shards)
    W.update(_replicated_weights({n: weights[n] for n, _ in REPLICATED}))
    in_flight = {}

    def send_w_in_grads(G):
        blocks = _w_in_grad_blocks(G["w_sgu_t"], G["w_rw_t"], G["w_gate_t"])
        got = _pair_exchange("grad_pair_exchange", blocks)
        core = lax.axis_index("c").astype(jnp.int32).reshape(1)
        sums = _pair_sum("grad_pair_sum", blocks, got, core)
        in_flight["sems"], in_flight["sums"], in_flight["land"], token = _chip_exchange_start("grad_chip_start", sums)
        return token

    loss_part, dx, G, late_slots = _local_step(x[0], loss_target[0], W, late_weights=_late_weights(shards),
                                               early_grads=_late_grad_blocks, w_in_grads_ready=send_w_in_grads)

    slots = dict(zip(LATE_WEIGHTS, late_slots))
    blocks = _first_grad_blocks(G)
    small = _replicated_grads(G)
    small_parts = [small[n] for n, _ in PACKED] + [jnp.full((PACK_W,), loss_part, F32)]
    rest = [n for n in FIRST_WEIGHTS if n != "w_in"]
    res = _exchange("grad_exchange", [blocks[n] for n in rest] + [_pack_rows(small_parts, _SMALL_ROWS, F32)],
                    [False] * len(rest) + [True])
    slots.update(zip(rest, res[:-1]))
    small_slots = res[-1]
    sums, chip_slots = _chip_exchange_wait("grad_chip_wait", in_flight["sems"], in_flight["sums"], in_flight["land"],
                                           after=small_slots)
    my_chip = (2 * lax.axis_index("x") + lax.axis_index("y")).astype(jnp.int32).reshape(1)
    slots["w_in"] = _fill_slot("grad_own_slot", chip_slots, sums, my_chip, src_idx=my_chip)

    outs = [dict(), dict(), dict(), dict()]
    for n, _, _ in SHARDED:
        if n == "w_in":
            res = _adamw("adamw_" + n, slots[n], *[jnp.transpose(t, (2, 0, 1)) for t in (weights[n], moms[n], vars_[n])])
            res = [jnp.transpose(t, (1, 2, 0)) for t in res]
        else:
            res = _adamw("adamw_" + n, slots[n], weights[n], moms[n], vars_[n])
        for k in range(4):
            outs[k][n] = res[k]

    sgu_shape = (SGU_G * SGU_C, SGU_C)
    res = _adamw("adamw_sgu_w", late_slots[len(LATE_WEIGHTS)], *[t.reshape(sgu_shape) for t in
                                                                  (weights["sgu_w"], moms["sgu_w"], vars_["sgu_w"])])
    for k in range(4):
        outs[k]["sgu_w"] = res[k].reshape(weights["sgu_w"].shape)

    def packed(d):
        return _pack_rows([d[n] for n, _ in PACKED], _SMALL_ROWS, F32)
    small_out = _adamw("adamw_replicated", small_slots, packed(weights), packed(moms), packed(vars_))
    for k in range(4):
        flat = small_out[k].reshape(-1)
        off = 0
        for (n, s), size in zip(PACKED, _SMALL_SIZES):
            outs[k][n] = flat[off:off + size].reshape(s)
            off += size
    loss = small_out[0].reshape(-1)[_LOSS_AT]
    return (loss, dx[None], *[outs[0][n] for n in WEIGHT_ORDER], *[outs[1][n] for n in WEIGHT_ORDER],
            *[outs[2][n] for n in WEIGHT_ORDER], *[outs[3][n] for n in WEIGHT_ORDER])
```

```python
import functools
import numpy as np
import jax
import jax.numpy as jnp
from jax import lax
from jax.experimental import pallas as pl
from jax.experimental.pallas import tpu as pltpu

F32 = jnp.float32
BF16 = jnp.bfloat16

D = 1024
NH, HN = 16, 64
NP, PW = NH // 2, 2 * HN
SGU_G, SGU_C = 8, 128
L_W, L_A, L_G = 64, 64, 160
C_B = 3 * D + L_W + L_A + L_G
P_TOTAL = 2 * D + C_B + 2 * D
D_FF = 4 * D
RW_INT = 3 * D + 128 + 128 + 256
NORM_EPS, LN_EPS, GN_EPS = 1e-6, 1e-5, 64e-5
N_DEV = 8
LANES = 128
SCAN_C = 64
N_KEPT = 4
SOLVE_B = 16
SCAN_PRECISION = lax.Precision.HIGH
SCAN_OUT_PRECISION = lax.Precision.DEFAULT
GRAD_PAYLOAD = BF16
VMEM_LIMIT = 56 * 1024 * 1024
MATMUL_VMEM_BUDGET = 40 * 1024 * 1024
STEP_COST_BYTES = 512 * 1024
HBM_COST_RATIO = 3

ADAM_LR, ADAM_B1, ADAM_B2, ADAM_EPS, ADAM_WD, ADAM_STEP = 0.001, 0.9, 0.999, 1e-08, 0.01, 10

SHARDED = [
    ("w_in", (D, P_TOTAL), 1), ("w_proj_a", (D, D), 0), ("shift_b", (2, C_B), 1), ("w_lora_w", (L_W, D), 1),
    ("a_lora_w", (L_A, D), 1), ("g_lora_w", (L_G, D), 1), ("w_proj_b", (D, D), 0), ("w_out", (D, D), 0),
    ("w_ffn1", (D, D_FF), 1), ("w_ffn2", (D_FF, D), 0),
]
REPLICATED = [
    ("g_mix", (1, D)), ("sgu_ln_w", (1, D)), ("sgu_ln_b", (1, D)), ("sgu_w", (1, SGU_G, SGU_C, SGU_C)),
    ("sgu_b", (1, SGU_G, SGU_C)), ("w0", (1, D)), ("a0", (1, D)), ("k_k", (1, D)), ("k_a", (1, D)), ("r_k", (1, D)),
    ("ln_x_w", (1, D)), ("ln_x_b", (1, D)), ("g_ffn", (1, D)), ("g_final", (D,)),
]
WEIGHT_ORDER = ["g_mix", "w_in", "sgu_ln_w", "sgu_ln_b", "sgu_w", "sgu_b", "w_proj_a", "shift_b", "w_lora_w", "w0",
                "a_lora_w", "a0", "g_lora_w", "k_k", "k_a", "r_k", "ln_x_w", "ln_x_b", "w_proj_b", "w_out", "g_ffn",
                "w_ffn1", "w_ffn2", "g_final"]


def _shard_shape(shape, axis):
    s = list(shape)
    s[axis] //= N_DEV
    return tuple(s)


def _round_up(n, m):
    return (n + m - 1) // m * m


def _pick(n, target):
    if n <= target:
        return n
    best = None
    for t in range(LANES, target + 1, LANES):
        if n % t == 0:
            best = t
    assert best is not None, (n, target)
    return best


def _matmul(name, a, b, mode, out_dtype=F32, tm=2048, tn=1024, tk=2048, out_blocks=None, epilogue=None, extras=(),
            out_dtypes=(), after=None, whole_rows=False):
    b_blocks = b.shape[0] if b.ndim == 3 else None
    bshape = b.shape if b.ndim == 2 else (b.shape[1], b.shape[0] * b.shape[2])
    if mode == "nn":
        (M, K), (K2, N) = a.shape, bshape
    elif mode == "nt":
        (M, K), (N, K2) = a.shape, bshape
    else:
        (K, M), (K2, N) = a.shape, bshape
    assert K == K2, (name, a.shape, b.shape)
    assert b_blocks is None or mode != "tn"
    assert out_blocks is None or mode == "tn"
    tn = min(tn, N // (out_blocks or 1), bshape[1] // b_blocks if (b_blocks and mode == "nn") else tn)
    tk = min(tk, bshape[1] // b_blocks if (b_blocks and mode == "nt") else tk)
    tm, tn, tk = _pick(M, tm), _pick(N, tn), _pick(K, tk)

    def vmem_bytes(tm, tk):
        tiles = tm * tk * a.dtype.itemsize + tk * tn * b.dtype.itemsize
        for dt in (out_dtypes if epilogue else (out_dtype,)):
            tiles += tm * tn * jnp.dtype(dt).itemsize
        for x in extras:
            arr = x[0] if isinstance(x, tuple) else x
            tiles += (tm if arr.shape[0] > 1 else 1) * tn * arr.dtype.itemsize
        return 2 * tiles + (tm * tn * 4 if K // tk > 1 else 0)

    def cost(tm, tk):
        ni, nj, nk = M // tm, N // tn, K // tk
        steps = ni * nj * nk
        acc_passes = steps * tm * tn * 8 if nk > 1 else 0
        a_reads = M * K * a.dtype.itemsize * (nj if nk > 1 else 1)
        b_reads = K * N * b.dtype.itemsize * (ni if (nj > 1 or nk > 1) else 1)
        return (steps * STEP_COST_BYTES + acc_passes + vmem_bytes(tm, tk) // 2
                + HBM_COST_RATIO * (a_reads + b_reads))

    options = [(m, k) for m in ({M} if whole_rows else {_pick(M, max(t, LANES)) for t in (tm, tm // 2, tm // 4)})
               for k in {_pick(K, max(t, LANES)) for t in (tk, tk // 2, tk // 4)}
               if vmem_bytes(m, k) <= MATMUL_VMEM_BUDGET]
    tm, tk = min(options, key=lambda o: cost(*o))
    nk = K // tk
    dims = {"nn": (((1,), (0,)), ((), ())), "nt": (((1,), (1,)), ((), ())), "tn": (((0,), (0,)), ((), ()))}[mode]

    n_x, n_o = len(extras), len(out_dtypes) if epilogue else 1
    n_after = 0 if after is None else 1

    def body(a_ref, b_ref, *rest):
        x_refs, o_refs, acc = rest[:n_x], rest[n_x + n_after:n_x + n_after + n_o], rest[n_x + n_after + n_o:]
        part = lax.dot_general(a_ref[...].astype(BF16), b_ref[...].astype(BF16), dims, preferred_element_type=F32)

        def finish(res):
            outs = epilogue(res, *[r[...] for r in x_refs]) if epilogue else (res,)
            for r, v in zip(o_refs, outs):
                r[...] = v.astype(r.dtype)

        if nk == 1:
            finish(part)
            return
        acc_ref, k = acc[0], pl.program_id(2)

        @pl.when(k == 0)
        def _():
            acc_ref[...] = part

        @pl.when(k > 0)
        def _():
            acc_ref[...] += part

        @pl.when(k == nk - 1)
        def _():
            finish(acc_ref[...])

    a_spec = {"nn": pl.BlockSpec((tm, tk), lambda i, j, k: (i, k)), "nt": pl.BlockSpec((tm, tk), lambda i, j, k: (i, k)),
              "tn": pl.BlockSpec((tk, tm), lambda i, j, k: (k, i))}[mode]
    b_spec = {"nn": pl.BlockSpec((tk, tn), lambda i, j, k: (k, j)), "nt": pl.BlockSpec((tn, tk), lambda i, j, k: (j, k)),
              "tn": pl.BlockSpec((tk, tn), lambda i, j, k: (k, j))}[mode]
    if b_blocks and mode == "nn":
        per = b.shape[2] // tn
        b_spec = pl.BlockSpec((None, tk, tn), lambda i, j, k: (j // per, k, j % per))
    elif b_blocks:
        per = b.shape[2] // tk
        b_spec = pl.BlockSpec((None, tn, tk), lambda i, j, k: (k // per, j, k % per))
    out_spec = pl.BlockSpec((tm, tn), lambda i, j, k: (i, j))
    out_shape = jax.ShapeDtypeStruct((M, N), out_dtype)
    if out_blocks:
        per_o = N // out_blocks // tn
        out_spec = pl.BlockSpec((None, tm, tn), lambda i, j, k: (j // per_o, i, j % per_o))
        out_shape = jax.ShapeDtypeStruct((out_blocks, M, N // out_blocks), out_dtype)
    x_specs, x_args = [], []
    for x in extras:
        arr, off = x if isinstance(x, tuple) else (x, 0)
        if arr.shape[0] == 1:
            x_specs.append(pl.BlockSpec((1, tn), lambda i, j, k: (0, j)))
        else:
            x_specs.append(pl.BlockSpec((tm, tn), lambda i, j, k, off=off: (i, j + off)))
        x_args.append(arr)
    res = pl.pallas_call(
        body, name=name, grid=(M // tm, N // tn, nk),
        in_specs=[a_spec, b_spec] + x_specs + [pl.BlockSpec(memory_space=pl.ANY)] * n_after,
        out_specs=[out_spec] * n_o if epilogue else out_spec,
        out_shape=[jax.ShapeDtypeStruct((M, N), dt) for dt in out_dtypes] if epilogue else out_shape,
        scratch_shapes=[pltpu.VMEM((tm, tn), F32)] if nk > 1 else [],
        compiler_params=pltpu.CompilerParams(dimension_semantics=("parallel", "parallel", "arbitrary"),
                                             vmem_limit_bytes=VMEM_LIMIT),
    )(a, b, *x_args, *([after] if n_after else []))
    return res


class Rows:
    def __init__(self, arr, width=None, cb=0):
        self.arr, self.width, self.cb = arr, (arr.shape[1] if width is None else width), cb


class Heads:
    def __init__(self, arr):
        self.arr = arr


class Halo:
    def __init__(self, arr, side):
        self.arr, self.side = arr, side


def _rows_call(name, fn, ins, consts, outs, accs=(), tm=256, with_pid=False):
    T = next(o.arr.shape[1] if isinstance(o, Heads) else o.arr.shape[0] for o in ins if not isinstance(o, Halo))
    tm = min(tm, T)
    n_tiles = T // tm
    n_in, n_c, n_out = len(ins), len(consts), len(outs)
    in_specs, args = [], []
    for o in ins:
        if isinstance(o, Rows):
            in_specs.append(pl.BlockSpec((tm, o.width), lambda i, cb=o.cb: (i, cb)))
        elif isinstance(o, Heads):
            in_specs.append(pl.BlockSpec((NP, tm, PW), lambda i: (0, i, 0)))
        else:
            w = o.arr.shape[1]
            if o.side < 0:
                in_specs.append(pl.BlockSpec((8, w), lambda i: (jnp.maximum(i * (tm // 8) - 1, 0), 0)))
            else:
                in_specs.append(pl.BlockSpec((8, w), lambda i: (jnp.minimum((i + 1) * (tm // 8), T // 8 - 1), 0)))
        args.append(o.arr)
    for c in consts:
        in_specs.append(pl.BlockSpec(c.shape, lambda i, nd=c.ndim: (0,) * nd))
        args.append(c)
    out_specs, out_shape = [], []
    for o in outs:
        if o[0] == "rows":
            out_specs.append(pl.BlockSpec((tm, o[1]), lambda i: (i, 0)))
            out_shape.append(jax.ShapeDtypeStruct((T, o[1]), o[2]))
        else:
            out_specs.append(pl.BlockSpec((NP, tm, PW), lambda i: (0, i, 0)))
            out_shape.append(jax.ShapeDtypeStruct((NP, T, PW), o[1]))
    for shape, dt in accs:
        out_specs.append(pl.BlockSpec(shape, lambda i, nd=len(shape): (0,) * nd))
        out_shape.append(jax.ShapeDtypeStruct(shape, dt))

    def body(*refs):
        i = pl.program_id(0)
        vals = []
        vals = [r[...] for r in refs[:n_in + n_c]]
        res = fn(i, n_tiles, *vals) if with_pid else fn(*vals)
        out_refs = refs[n_in + n_c:]
        for r, v in zip(out_refs[:n_out], res[:n_out]):
            r[...] = v.astype(r.dtype)
        if accs:
            @pl.when(i == 0)
            def _():
                for r in out_refs[n_out:]:
                    r[...] = jnp.zeros_like(r)

            for r, v in zip(out_refs[n_out:], res[n_out:]):
                r[...] += v.astype(r.dtype)

    res = pl.pallas_call(
        body, name=name, grid=(n_tiles,), in_specs=in_specs, out_specs=out_specs, out_shape=out_shape,
        compiler_params=pltpu.CompilerParams(dimension_semantics=("arbitrary",), vmem_limit_bytes=VMEM_LIMIT),
    )(*args)
    return res


def _rms(x, g):
    return x * lax.rsqrt(jnp.mean(x * x, axis=-1, keepdims=True) + NORM_EPS) * g


def _gelu(x):
    return 0.5 * x * (1.0 + lax.erf(x * 0.7071067811865476))


def _sigmoid(x):
    return 1.0 / (1.0 + jnp.exp(-x))


def _bdot(a, b):
    return jnp.dot(a.astype(BF16), b.astype(BF16), preferred_element_type=F32)


def _to_heads(x):
    return jnp.concatenate([x[:, p * PW:(p + 1) * PW][None] for p in range(NP)], axis=0)


def _from_heads(xp):
    return jnp.concatenate([xp[p] for p in range(NP)], axis=-1)


def _head_sum(xp):
    low = lax.broadcasted_iota(jnp.int32, xp.shape, xp.ndim - 1) < HN
    both = jnp.sum(xp, axis=-1, keepdims=True)
    first = jnp.sum(jnp.where(low, xp, 0.0), axis=-1, keepdims=True)
    return jnp.where(low, first, both - first)


def _split_pairs(xp):
    return jnp.concatenate([xp[:, :, :HN], xp[:, :, HN:]], axis=0)


def _join_pairs(xh):
    return jnp.concatenate([xh[:NP], xh[NP:]], axis=-1)


def _sgu_fn(p, ln_w, ln_b, sw, sbt):
    z = _gelu(p)
    u, v = z[:, :D], z[:, D:]
    mu = jnp.mean(v, axis=-1, keepdims=True)
    var = jnp.mean(jnp.square(v - mu), axis=-1, keepdims=True)
    vn = (v - mu) * lax.rsqrt(var + LN_EPS) * ln_w + ln_b
    ri = lax.broadcasted_iota(jnp.int32, (SGU_C, SGU_C), 0)
    ci = lax.broadcasted_iota(jnp.int32, (SGU_C, SGU_C), 1)
    mask = (ci <= ri).astype(F32)
    dg = D // SGU_G
    parts = []
    for g in range(SGU_G):
        parts.append(_bdot(sw[g] * mask, vn[:, g * dg:(g + 1) * dg]) + sbt[:, g:g + 1])
    return u * jnp.concatenate(parts, axis=-1)


def _pre_fn(qr, qk, qv, qxw, qxa, qxg, wl, w0, al, a0, gl, k_k, k_a):
    w = -jax.nn.softplus(-(w0 + _bdot(jnp.tanh(qxw), wl))) - 0.5
    lw = -jnp.exp(w)
    aa = _sigmoid(a0 + _bdot(qxa, al))
    g = _bdot(_sigmoid(qxg), gl)
    kk = _to_heads(qk * k_k)
    kk = kk / jnp.maximum(jnp.sqrt(_head_sum(kk * kk)), 1e-12)
    k2 = qk * (1.0 + (aa - 1.0) * k_a)
    return _to_heads(qr), _to_heads(lw), _to_heads(k2), _to_heads(qv), kk, _to_heads(aa), g


def _post_fn(o, r, k2, v, g, ln_w, ln_b, r_k):
    mu = _head_sum(o) * (1.0 / HN)
    d = o - mu
    var = _head_sum(d * d) * (1.0 / HN)
    on = d * lax.rsqrt(var + GN_EPS) * ln_w + ln_b
    bonus = _head_sum(r * k2 * r_k) * v
    return _from_heads(on + bonus) * g


def _gate_fn(pg, ya, yb):
    return _sigmoid(pg[:, :D]) * ya + _sigmoid(pg[:, D:]) * yb


def _bmm(x, y, cx, cy, out_path=False):
    return lax.dot_general(x, y, (((cx,), (cy,)), ((0,), (0,))),
                           precision=SCAN_OUT_PRECISION if out_path else SCAN_PRECISION, preferred_element_type=F32)


def _unit_lower_inverse(M):
    C = M.shape[1]
    ti = lax.broadcasted_iota(jnp.int32, (C, C), 0)
    tj = lax.broadcasted_iota(jnp.int32, (C, C), 1)
    eye = (ti == tj).astype(F32)
    same = lambda b: (ti // b == tj // b).astype(F32)
    X = -(M * same(SOLVE_B))
    inv = eye + X
    span = 1
    while 2 * span < SOLVE_B:
        X = _bmm(X, X, 2, 1)
        inv = inv + _bmm(inv, X, 2, 1)
        span *= 2
    b = SOLVE_B
    while b < C:
        low = M * (same(2 * b) - same(b))
        inv = inv - _bmm(_bmm(inv, low, 2, 1, out_path=True), inv, 2, 1, out_path=True)
        b *= 2
    return inv


@jax.custom_vjp
def _unit_lower_solve(inv, M, y):
    return _bmm(inv, y, 2, 1)


def _unit_lower_solve_fwd(inv, M, y):
    u = _bmm(inv, y, 2, 1)
    return u, (inv, u)


def _unit_lower_solve_bwd(res, du):
    inv, u = res
    dy = _bmm(inv, du, 1, 1)
    return jnp.zeros_like(inv), -_bmm(dy, u, 2, 2), dy


_unit_lower_solve.defvjp(_unit_lower_solve_fwd, _unit_lower_solve_bwd)


@jax.custom_vjp
def _unit_lower_solved(inv, u, M, y):
    return u


_unit_lower_solved.defvjp(lambda inv, u, M, y: (u, (inv, u)),
                          lambda res, du: (jnp.zeros_like(res[0]), jnp.zeros_like(res[1]))
                          + _unit_lower_solve_bwd(res, du)[1:])


@functools.partial(jax.custom_vjp, nondiff_argnums=(0,))
def _kept(fn, value, *args):
    return value


def _kept_fwd(fn, value, *args):
    return value, args


def _kept_bwd(fn, args, d):
    _, vjp = jax.vjp(fn, *args)
    return (jnp.zeros_like(d),) + tuple(vjp(d))


_kept.defvjp(_kept_fwd, _kept_bwd)


def _sum_over_time(x, reverse):
    C = x.shape[1]
    ti = lax.broadcasted_iota(jnp.int32, (C, C), 0)
    tj = lax.broadcasted_iota(jnp.int32, (C, C), 1)
    ones = jnp.broadcast_to(((tj >= ti) if reverse else (tj <= ti)).astype(BF16), (x.shape[0], C, C))
    hi = x.astype(BF16)
    r1 = x - hi.astype(F32)
    mid = r1.astype(BF16)
    lo = (r1 - mid.astype(F32)).astype(BF16)
    dn = (((2,), (1,)), ((0,), (0,)))
    return sum(lax.dot_general(ones, p, dn, preferred_element_type=F32) for p in (lo, mid, hi))


@jax.custom_vjp
def _time_cumsum(lw):
    return _sum_over_time(lw, reverse=False)


_time_cumsum.defvjp(lambda lw: (_sum_over_time(lw, reverse=False), None),
                    lambda _, d: (_sum_over_time(d, reverse=True),))


def _chunk_fn(S0, r, lw, k, v, kk, a, kept=None):
    C = SCAN_C
    bmm = _bmm
    ti = lax.broadcasted_iota(jnp.int32, (C, C), 0)
    tj = lax.broadcasted_iota(jnp.int32, (C, C), 1)
    incl2 = jnp.concatenate([(tj <= ti).astype(F32)] * 2, axis=1)
    strict = (tj < ti).astype(F32)
    n_mask = jnp.concatenate([jnp.zeros((C, C), F32), strict], axis=1)

    def known(name, fn, *args):
        return fn(*args) if kept is None else _kept(fn, kept[name], *args)

    cum = known("cum", _time_cumsum, lw)
    g_in, g_ex, g_inv = jnp.exp(cum), jnp.exp(cum - lw), jnp.exp(-cum)
    kkt, rt = kk * g_ex, r * g_in
    bk = jnp.concatenate([kk * a * g_inv, k * g_inv], axis=1)
    A = known("A", lambda x, y: bmm(x, y, 2, 2), kkt, bk)
    M = A[:, :, :C] * strict
    zv = jnp.concatenate([jnp.zeros_like(v), v], axis=1)
    s0_side = bmm(jnp.concatenate([kkt, rt], axis=1), S0, 2, 2, out_path=True)
    rhs = s0_side[:, :C] + bmm(A * n_mask, zv, 2, 1, out_path=True)
    if kept is None:
        inv = lax.stop_gradient(_unit_lower_inverse(M))
        y = _unit_lower_solve(inv, M, rhs)
    else:
        inv = kept["inv"]
        y = _unit_lower_solved(inv, kept["y"], M, rhs)
    z = jnp.concatenate([-y, v], axis=1)
    attn = known("attn", lambda x, y: bmm(x, y, 2, 2) * incl2, rt, bk)
    O = s0_side[:, C:] + bmm(attn, z, 2, 1, out_path=True)
    g_end = g_in[:, C - 1:C, :]
    S1 = S0 * g_end + bmm(z, bk * g_end, 1, 1, out_path=True)
    return O, S1, dict(cum=cum, A=A, attn=attn, y=y, inv=inv)


def _scan_fwd(r, lw, k, v, kk, a, ex=None, tb=256):
    assert SCAN_C == HN and 2 * SCAN_C == PW
    T = r.shape[1]
    tb = min(tb, T)
    n_chunks = tb // SCAN_C
    nb = T // tb
    nx = ex.nb if ex else 0

    def body(*refs):
        r_ref, lw_ref, k_ref, v_ref, kk_ref, a_ref = refs[:6]
        x_in, (o_ref, s0_ref), x_out = refs[6:6 + nx], refs[6 + nx:8 + nx], refs[8 + nx:8 + 2 * nx]
        s_ref, sems = refs[8 + 2 * nx], refs[9 + 2 * nx:]

        plan = ex.schedule(nb) if ex else []

        @pl.when(pl.program_id(0) == 0)
        def _():
            s_ref[...] = jnp.zeros_like(s_ref)
            for at, action in plan[:1]:
                action(x_in, x_out, sems)

        def step(c, carry):
            sl = pl.ds(pl.multiple_of(c * SCAN_C, SCAN_C), SCAN_C)
            S0 = s_ref[...]
            O, S1, keep = _chunk_fn(S0, *[_split_pairs(ref[:, sl, :])
                                          for ref in (r_ref, lw_ref, k_ref, v_ref, kk_ref, a_ref)])
            o_ref[:, sl, :] = _join_pairs(O)
            s0_ref[c, 0] = jnp.concatenate([S0, keep["inv"]], axis=-1)
            s0_ref[c, 1] = jnp.concatenate([keep["cum"], keep["y"]], axis=-1)
            s0_ref[c, 2] = keep["A"]
            s0_ref[c, 3] = keep["attn"]
            s_ref[...] = S1
            return carry

        lax.fori_loop(0, n_chunks, step, 0)

        for at, action in plan[1:]:
            pl.when(pl.program_id(0) == at)(functools.partial(action, x_in, x_out, sems))

    hm = pl.BlockSpec((NP, tb, PW), lambda i: (0, i, 0))
    res = pl.pallas_call(
        body, name="rwkv_scan_fwd", grid=(nb,), in_specs=[hm] * 6 + (ex.any_specs if ex else []),
        out_specs=[hm, pl.BlockSpec((n_chunks, N_KEPT, NH, HN, PW), lambda i: (i, 0, 0, 0, 0))]
        + (ex.any_specs if ex else []),
        out_shape=[jax.ShapeDtypeStruct((NP, T, PW), F32),
                   jax.ShapeDtypeStruct((T // SCAN_C, N_KEPT, NH, HN, PW), F32)]
        + (ex.out_shape if ex else []),
        scratch_shapes=[pltpu.VMEM((NH, HN, HN), F32)] + (ex.sem_shapes if ex else []),
        compiler_params=pltpu.CompilerParams(dimension_semantics=("arbitrary",), vmem_limit_bytes=VMEM_LIMIT),
    )(r, lw, k, v, kk, a, *(ex.bufs if ex else []))
    return res[0], res[1], list(res[2:])


def _scan_bwd(r, lw, k, v, kk, a, s0s, do, ex=None, tb=128):
    T = r.shape[1]
    tb = min(tb, T)
    n_chunks = tb // SCAN_C
    nb = T // tb
    nx = ex.nb if ex else 0

    def body(*refs):
        r_ref, lw_ref, k_ref, v_ref, kk_ref, a_ref, s0_ref, do_ref = refs[:8]
        x_in, (dr, dlw, dk, dv, dkk, da), x_out = refs[8:8 + nx], refs[8 + nx:14 + nx], refs[14 + nx:14 + 2 * nx]
        ds_ref, sems = refs[14 + 2 * nx], refs[15 + 2 * nx:]

        plan = ex.schedule(nb) if ex else []

        @pl.when(pl.program_id(0) == 0)
        def _():
            ds_ref[...] = jnp.zeros_like(ds_ref)
            for at, action in plan[:1]:
                action(x_in, x_out, sems)

        def step(j, carry):
            c = n_chunks - 1 - j
            sl = pl.ds(pl.multiple_of(c * SCAN_C, SCAN_C), SCAN_C)
            s0_inv, cum_y = s0_ref[c, 0], s0_ref[c, 1]
            kept = dict(inv=s0_inv[:, :, HN:], cum=cum_y[:, :, :HN], y=cum_y[:, :, HN:], A=s0_ref[c, 2],
                        attn=s0_ref[c, 3])
            _, vjp = jax.vjp(lambda *t: _chunk_fn(*t, kept=kept)[:2], s0_inv[:, :, :HN],
                             *[_split_pairs(ref[:, sl, :]) for ref in (r_ref, lw_ref, k_ref, v_ref, kk_ref, a_ref)])
            g = vjp((_split_pairs(do_ref[:, sl, :]), ds_ref[...]))
            ds_ref[...] = g[0]
            for ref, val in zip((dr, dlw, dk, dv, dkk, da), g[1:]):
                ref[:, sl, :] = _join_pairs(val)
            return carry

        lax.fori_loop(0, n_chunks, step, 0)

        for at, action in plan[1:]:
            pl.when(pl.program_id(0) == at)(functools.partial(action, x_in, x_out, sems))

    hm = pl.BlockSpec((NP, tb, PW), lambda i: (0, nb - 1 - i, 0))
    res = pl.pallas_call(
        body, name="rwkv_scan_bwd", grid=(nb,),
        in_specs=[hm] * 6 + [pl.BlockSpec((n_chunks, N_KEPT, NH, HN, PW), lambda i: (nb - 1 - i, 0, 0, 0, 0)), hm]
        + (ex.any_specs if ex else []),
        out_specs=[hm] * 6 + (ex.any_specs if ex else []),
        out_shape=[jax.ShapeDtypeStruct((NP, T, PW), F32)] * 6 + (ex.out_shape if ex else []),
        scratch_shapes=[pltpu.VMEM((NH, HN, HN), F32)] + (ex.sem_shapes if ex else []),
        compiler_params=pltpu.CompilerParams(dimension_semantics=("arbitrary",), vmem_limit_bytes=VMEM_LIMIT),
    )(r, lw, k, v, kk, a, s0s, do, *(ex.bufs if ex else []))
    return list(res[:6]), list(res[6:])


def _shift_down(i, p, prev8):
    first = jnp.where(i > 0, prev8[7:8, :], 0.0)
    row = lax.broadcasted_iota(jnp.int32, p.shape, 0)
    return jnp.where(row == 0, first, pltpu.roll(p, 1, axis=0))


def _mix_bwd(dq, p, sb, tm=256):
    def fn(i, n, dq, next8, p, prev8, sb):
        ps = _shift_down(i, p, prev8)
        d1 = dq * sb[1:2]
        last = jnp.where(i < n - 1, next8[0:1, :] * sb[1:2], 0.0)
        row = lax.broadcasted_iota(jnp.int32, dq.shape, 0)
        up = jnp.where(row == dq.shape[0] - 1, last, pltpu.roll(d1, dq.shape[0] - 1, axis=0))
        return (dq * sb[0:1] + up, jnp.sum(dq * p, axis=0, keepdims=True), jnp.sum(dq * ps, axis=0, keepdims=True))
    w = p.shape[1]
    return _rows_call("shift_mix_bwd", fn, [Rows(dq), Halo(dq, +1), Rows(p), Halo(p, -1)], [sb], [("rows", w, BF16)],
                      accs=[((1, w), F32), ((1, w), F32)], tm=tm, with_pid=True)


def _local_step(x, target, W, late_weights=None, early_grads=None, w_in_grads_ready=None):
    G = {}
    a = _rows_call("norm_mix_fwd", lambda x, g: (_rms(x, g),), [Rows(x)], [W["g_mix"]], [("rows", D, BF16)])[0]
    p_sgu = _matmul("proj_sgu", a, W["w_sgu_t"], "nt")
    def token_shift(p, sb0, sb1):
        row = lax.broadcasted_iota(jnp.int32, p.shape, 0)
        return p, p * sb0 + jnp.where(row == 0, 0.0, pltpu.roll(p, 1, axis=0)) * sb1
    p_rw, q = _matmul("proj_rwkv", a, W["w_rw_t"], "nt", tn=512, whole_rows=True, epilogue=token_shift,
                      extras=[W["sb"][0:1], W["sb"][1:2]], out_dtypes=(F32, F32))
    p_gate = _matmul("proj_gate", a, W["w_gate_t"], "nt")

    sgu_consts = [W["sgu_ln_w"], W["sgu_ln_b"], W["sgu_w"], W["sgu_bt"]]
    s = _rows_call("sgu_fwd", lambda *t: (_sgu_fn(*t),), [Rows(p_sgu)], sgu_consts, [("rows", D, BF16)], tm=SGU_C)[0]

    q_ins = [Rows(q, D, 0), Rows(q, D, 1), Rows(q, D, 2), Rows(q, 128, 24), Rows(q, 128, 25), Rows(q, 256, 13)]
    pre_consts = [W["w_lora"], W["w0"], W["a_lora"], W["a0"], W["g_lora"], W["k_k"], W["k_a"]]
    r_h, lw_h, k_h, v_h, kk_h, a_h, g_gate = _rows_call(
        "rwkv_pre_fwd", _pre_fn, q_ins, pre_consts, [("heads", F32)] * 6 + [("rows", D, F32)], tm=128)
    o_h, s0s, got = _scan_fwd(r_h, lw_h, k_h, v_h, kk_h, a_h, ex=late_weights[0] if late_weights else None)
    if late_weights:
        W = {**W, **late_weights[1](got)}
    y_a = _matmul("proj_a", s, W["w_proj_a"], "nn")
    post_ins = [Heads(o_h), Heads(r_h), Heads(k_h), Heads(v_h), Rows(g_gate)]
    post_consts = [W[n].reshape(NP, 1, PW) for n in ("ln_x_w", "ln_x_b", "r_k")]
    z_b = _rows_call("rwkv_post_fwd", lambda *t: (_post_fn(*t),), post_ins, post_consts, [("rows", D, BF16)], tm=128)[0]
    y_b, mixed = _matmul("proj_b", z_b, W["w_proj_b"], "nn", extras=[(p_gate, 0), (p_gate, 1), y_a],
                         epilogue=lambda yb, ga, gb, ya: (yb, _sigmoid(ga) * ya + _sigmoid(gb) * yb),
                         out_dtypes=(F32, BF16))
    gate_ins = [Rows(p_gate), Rows(y_a), Rows(y_b)]

    def res1(mo, x, g):
        h1 = x + mo
        return h1, _rms(h1, g)
    h1, f = _matmul("proj_out", mixed, W["w_out"], "nn", extras=[x, W["g_ffn"]], epilogue=res1,
                    out_dtypes=(F32, BF16))

    def relu_sq(u):
        r = jnp.maximum(u, 0.0)
        return r, r * r
    r1, act = _matmul("ffn_up", f, W["w_ffn1"], "nn", epilogue=relu_sq, out_dtypes=(BF16, BF16))
    ff = _matmul("ffn_down", act, W["w_ffn2"], "nn")

    def head(h1, ff, tgt, g):
        def f_(h1, ff, g):
            y = _rms(h1 + ff, g)
            return 0.5 * jnp.sum(jnp.mean(jnp.square(y - tgt), axis=-1))
        loss, (dh2, _, dg) = jax.value_and_grad(f_, argnums=(0, 1, 2))(h1, ff, g)
        return dh2, jnp.full((8, LANES), loss, F32), dg
    dh2, loss_acc, G["g_final"] = _rows_call("loss_head", head, [Rows(h1), Rows(ff), Rows(target)], [W["g_final"]],
                                             [("rows", D, F32)], accs=[((8, LANES), F32), ((1, D), F32)])

    d_u1 = _matmul("ffn_down_dx", dh2, W["w_ffn2"], "nt", extras=[r1], out_dtypes=(BF16,),
                   epilogue=lambda d_act, r: (d_act * 2.0 * r.astype(F32),))[0]
    G["w_ffn2"] = _matmul("ffn_down_dw", act, dh2, "tn", out_dtype=GRAD_PAYLOAD)
    d_f = _matmul("ffn_up_dx", d_u1, W["w_ffn1"], "nt")
    G["w_ffn1"] = _matmul("ffn_up_dw", f, d_u1, "tn", out_blocks=N_DEV, out_dtype=GRAD_PAYLOAD)

    def res1_bwd(h1, d_f, dh2, g):
        _, vjp = jax.vjp(_rms, h1, g)
        dh, dg = vjp(d_f)
        return dh2 + dh, dg
    dh1, G["g_ffn"] = _rows_call("residual_norm_bwd", res1_bwd, [Rows(h1), Rows(d_f), Rows(dh2)],
                                 [W["g_ffn"]], [("rows", D, F32)], accs=[((1, D), F32)])
    d_mixed = _matmul("proj_out_dx", dh1, W["w_out"], "nt")
    G["w_out"] = _matmul("proj_out_dw", mixed, dh1, "tn", out_dtype=GRAD_PAYLOAD)

    def gate_bwd(pg, ya, yb, dm):
        _, vjp = jax.vjp(_gate_fn, pg, ya, yb)
        return vjp(dm)
    d_gate, d_ya, d_yb = _rows_call("gate_bwd", gate_bwd, gate_ins + [Rows(d_mixed)], [],
                                    [("rows", 2 * D, BF16), ("rows", D, BF16), ("rows", D, BF16)])

    d_s = _matmul("proj_a_dx", d_ya, W["w_proj_a"], "nt")
    G["w_proj_a"] = _matmul("proj_a_dw", s, d_ya, "tn", out_dtype=GRAD_PAYLOAD)

    def sgu_bwd(p, ds, *c):
        _, vjp = jax.vjp(_sgu_fn, p, *c)
        return vjp(ds)
    d_p_sgu, G["sgu_ln_w"], G["sgu_ln_b"], G["sgu_w"], G["sgu_bt"] = _rows_call(
        "sgu_bwd", sgu_bwd, [Rows(p_sgu), Rows(d_s)], sgu_consts, [("rows", 2 * D, BF16)],
        accs=[((1, D), F32), ((1, D), F32), ((SGU_G, SGU_C, SGU_C), F32), ((SGU_C, SGU_G), F32)], tm=SGU_C)

    d_zb = _matmul("proj_b_dx", d_yb, W["w_proj_b"], "nt")
    G["w_proj_b"] = _matmul("proj_b_dw", z_b, d_yb, "tn", out_dtype=GRAD_PAYLOAD)

    def post_bwd(o, r, k2, v, g, dz, *c):
        _, vjp = jax.vjp(_post_fn, o, r, k2, v, g, *c)
        return vjp(dz)
    do_h, dr1, dk1, dv1, d_g, g_lnw, g_lnb, g_rk = _rows_call(
        "rwkv_post_bwd", post_bwd, post_ins + [Rows(d_zb)], post_consts, [("heads", F32)] * 4 + [("rows", D, F32)],
        accs=[((NP, 1, PW), F32)] * 3, tm=128)
    G["ln_x_w"], G["ln_x_b"], G["r_k"] = (t.reshape(1, D) for t in (g_lnw, g_lnb, g_rk))
    (dr2, dlw, dk2, dv2, dkk, daa), early = _scan_bwd(r_h, lw_h, k_h, v_h, kk_h, a_h, s0s, do_h,
                                                      ex=early_grads(G) if early_grads else None)

    def pre_bwd(qr, qk, qv, qxw, qxa, qxg, dr1, dr2, dlw, dk1, dk2, dv1, dv2, dkk, daa, dg, *c):
        _, vjp = jax.vjp(_pre_fn, qr, qk, qv, qxw, qxa, qxg, *c)
        g = vjp((dr1 + dr2, dlw, dk1 + dk2, dv1 + dv2, dkk, daa, dg))
        dq = jnp.concatenate(g[:6], axis=-1)
        return (dq,) + tuple(g[6:])
    pre_b_ins = q_ins + [Heads(dr1), Heads(dr2), Heads(dlw), Heads(dk1), Heads(dk2), Heads(dv1), Heads(dv2),
                         Heads(dkk), Heads(daa), Rows(d_g)]
    d_q, G["w_lora"], G["w0"], G["a_lora"], G["a0"], G["g_lora"], G["k_k"], G["k_a"] = _rows_call(
        "rwkv_pre_bwd", pre_bwd, pre_b_ins, pre_consts, [("rows", RW_INT, F32)],
        accs=[((128, D), F32), ((1, D), F32), ((128, D), F32), ((1, D), F32), ((256, D), F32), ((1, D), F32),
              ((1, D), F32)], tm=128)
    d_p_rw, dsb0, dsb1 = _mix_bwd(d_q, p_rw, W["sb"])
    G["sb"] = jnp.concatenate([dsb0, dsb1], axis=0)

    G["w_sgu_t"] = _matmul("proj_sgu_dw", d_p_sgu, a, "tn", out_dtype=GRAD_PAYLOAD)
    G["w_rw_t"] = _matmul("proj_rwkv_dw", d_p_rw, a, "tn", out_dtype=GRAD_PAYLOAD)
    G["w_gate_t"] = _matmul("proj_gate_dw", d_gate, a, "tn", out_dtype=GRAD_PAYLOAD)
    token = w_in_grads_ready(G) if w_in_grads_ready else None
    da1 = _matmul("proj_sgu_dx", d_p_sgu, W["w_sgu_t"], "nn", after=token)
    da2 = _matmul("proj_rwkv_dx", d_p_rw, W["w_rw_t"], "nn", after=token)
    da3 = _matmul("proj_gate_dx", d_gate, W["w_gate_t"], "nn", after=token)

    def norm1_bwd(x, da1, da2, da3, dh1, g):
        _, vjp = jax.vjp(_rms, x, g)
        dx, dg = vjp(da1 + da2 + da3)
        return dh1 + dx, dg
    dx, G["g_mix"] = _rows_call("norm_mix_bwd", norm1_bwd, [Rows(x), Rows(da1), Rows(da2), Rows(da3), Rows(dh1)],
                                [W["g_mix"]], [("rows", D, F32)], accs=[((1, D), F32)])
    return loss_acc[0, 0], dx, G, early


class Exchange:
    def __init__(self, bufs, gathers):
        self.bufs, self.gathers, self.nb = list(bufs), list(gathers), len(bufs)
        self.any_specs = [pl.BlockSpec(memory_space=pl.ANY)] * self.nb
        self.out_shape = [jax.ShapeDtypeStruct((N_DEV,) + (b.shape if g else b.shape[1:]), b.dtype)
                          for b, g in zip(self.bufs, self.gathers)]
        n = (N_DEV - 1) * self.nb
        self.sem_shapes = [pltpu.SemaphoreType.DMA((n,)), pltpu.SemaphoreType.DMA((n,)),
                           pltpu.SemaphoreType.DMA((self.nb,))]

    def _copies(self, in_refs, out_refs, sems):
        send_sems, recv_sems, local_sems = sems
        x, y, c = lax.axis_index("x"), lax.axis_index("y"), lax.axis_index("c")
        me = 4 * x + 2 * y + c

        def src(b, dest):
            return in_refs[b] if self.gathers[b] else in_refs[b].at[dest]

        local = [pltpu.make_async_copy(src(b, me), out_refs[b].at[me], local_sems.at[b]) for b in range(self.nb)]
        sends, recvs = [], []
        for kbits in range(1, N_DEV):
            px = 1 - x if kbits & 4 else x
            py = 1 - y if kbits & 2 else y
            pc = 1 - c if kbits & 1 else c
            peer = 4 * px + 2 * py + pc
            for b in range(self.nb):
                s = (kbits - 1) * self.nb + b
                sends.append(pltpu.make_async_remote_copy(
                    src_ref=src(b, peer), dst_ref=out_refs[b].at[me], send_sem=send_sems.at[s],
                    recv_sem=recv_sems.at[s], device_id=(px, py, pc), device_id_type=pl.DeviceIdType.MESH))
                recvs.append(pltpu.make_async_remote_copy(
                    src_ref=src(b, peer), dst_ref=out_refs[b].at[peer], send_sem=send_sems.at[s],
                    recv_sem=recv_sems.at[s], device_id=(px, py, pc), device_id_type=pl.DeviceIdType.MESH))
        return local, sends, recvs

    def start(self, in_refs, out_refs, sems):
        local, sends, _ = self._copies(in_refs, out_refs, sems)
        for cp in sends + local:
            cp.start()

    def wait(self, in_refs, out_refs, sems):
        local, sends, recvs = self._copies(in_refs, out_refs, sems)
        for cp in recvs:
            cp.wait_recv()
        for cp in sends:
            cp.wait_send()
        for cp in local:
            cp.wait()

    def schedule(self, n_steps):
        return [(0, self.start), (n_steps - 1, self.wait)]


def _exchange(name, bufs, gather):
    ex = Exchange(bufs, gather if isinstance(gather, (list, tuple)) else [gather] * len(bufs))

    def body(*refs):
        in_refs, out_refs, sems = refs[:ex.nb], refs[ex.nb:2 * ex.nb], refs[2 * ex.nb:]
        ex.start(in_refs, out_refs, sems)
        ex.wait(in_refs, out_refs, sems)

    return pl.pallas_call(body, name=name, in_specs=ex.any_specs, out_specs=ex.any_specs, out_shape=ex.out_shape,
                          scratch_shapes=ex.sem_shapes)(*ex.bufs)


N_CHIP = 4


def _pair_exchange(name, blocks):
    def body(b_ref, got_ref, send_sems, recv_sems):
        x, y, c = lax.axis_index("x"), lax.axis_index("y"), lax.axis_index("c")
        copies = [pltpu.make_async_remote_copy(
            src_ref=b_ref.at[2 * q + 1 - c], dst_ref=got_ref.at[q], send_sem=send_sems.at[q],
            recv_sem=recv_sems.at[q], device_id=(x, y, 1 - c), device_id_type=pl.DeviceIdType.MESH)
            for q in range(N_CHIP)]
        for cp in copies:
            cp.start()
        for cp in copies:
            cp.wait_recv()
        for cp in copies:
            cp.wait_send()

    any_spec = pl.BlockSpec(memory_space=pl.ANY)
    return pl.pallas_call(
        body, name=name, in_specs=[any_spec], out_specs=any_spec,
        out_shape=jax.ShapeDtypeStruct((N_CHIP,) + blocks.shape[1:], blocks.dtype),
        scratch_shapes=[pltpu.SemaphoreType.DMA((N_CHIP,))] * 2,
    )(blocks)


def _pair_sum(name, blocks, got, core):
    _, R, Wd = blocks.shape

    def body(c_ref, a_ref, b_ref, o_ref):
        o_ref[...] = (a_ref[...].astype(F32) + b_ref[...].astype(F32)).astype(o_ref.dtype)

    return pl.pallas_call(
        body, name=name,
        grid_spec=pltpu.PrefetchScalarGridSpec(
            num_scalar_prefetch=1, grid=(N_CHIP,),
            in_specs=[pl.BlockSpec((None, R, Wd), lambda q, c_ref: (2 * q + c_ref[0], 0, 0)),
                      pl.BlockSpec((None, R, Wd), lambda q, c_ref: (q, 0, 0))],
            out_specs=pl.BlockSpec((None, R, Wd), lambda q, c_ref: (q, 0, 0))),
        out_shape=jax.ShapeDtypeStruct(got.shape, blocks.dtype),
        compiler_params=pltpu.CompilerParams(dimension_semantics=("parallel",), vmem_limit_bytes=VMEM_LIMIT),
    )(core, blocks, got)


def _chip_copies(s_ref, land_ref, send_sems, recv_sems):
    x, y, c = lax.axis_index("x"), lax.axis_index("y"), lax.axis_index("c")
    my_q = 2 * x + y
    sends, recvs = [], []
    for kbits in range(1, N_CHIP):
        px = 1 - x if kbits & 2 else x
        py = 1 - y if kbits & 1 else y
        peer_q = 2 * px + py
        sends.append(pltpu.make_async_remote_copy(
            src_ref=s_ref.at[peer_q], dst_ref=land_ref.at[my_q], send_sem=send_sems[kbits - 1],
            recv_sem=recv_sems[kbits - 1], device_id=(px, py, c), device_id_type=pl.DeviceIdType.MESH))
        recvs.append(pltpu.make_async_remote_copy(
            src_ref=s_ref.at[peer_q], dst_ref=land_ref.at[peer_q], send_sem=send_sems[kbits - 1],
            recv_sem=recv_sems[kbits - 1], device_id=(px, py, c), device_id_type=pl.DeviceIdType.MESH))
    return sends, recvs


_HBM = pl.BlockSpec(memory_space=pltpu.HBM)
_SEM = pl.BlockSpec(memory_space=pltpu.SEMAPHORE)
N_CHIP_SEMS = 2 * (N_CHIP - 1)


def _chip_exchange_start(name, sums):
    def body(s_ref, land_ref, *outs):
        sems, token = outs[:N_CHIP_SEMS], outs[N_CHIP_SEMS + 2]
        sends, _ = _chip_copies(s_ref, land_ref, sems[:N_CHIP - 1], sems[N_CHIP - 1:])
        for cp in sends:
            cp.start()
        token[...] = jnp.zeros_like(token)

    res = pl.pallas_call(
        body, name=name, in_specs=(_HBM, _HBM),
        out_specs=(_SEM,) * N_CHIP_SEMS + (_HBM, _HBM, pl.BlockSpec(memory_space=pltpu.VMEM)),
        out_shape=(pltpu.SemaphoreType.DMA(()),) * N_CHIP_SEMS
        + (pltpu.HBM(sums.shape, sums.dtype), pltpu.HBM(sums.shape, sums.dtype), jax.ShapeDtypeStruct((8, LANES), F32)),
        input_output_aliases={0: N_CHIP_SEMS, 1: N_CHIP_SEMS + 1},
        compiler_params=pltpu.CompilerParams(has_side_effects=pltpu.SideEffectType.DATAFLOW_SIDE_EFFECTING),
    )(pltpu.with_memory_space_constraint(sums, pltpu.HBM),
      pltpu.with_memory_space_constraint(lax.empty(sums.shape, sums.dtype), pltpu.HBM))
    return res[:N_CHIP_SEMS], res[N_CHIP_SEMS], res[N_CHIP_SEMS + 1], res[N_CHIP_SEMS + 2]


def _chip_exchange_wait(name, sems, sums_thru, land_thru, after):
    def body(s_ref, land_ref, *rest):
        sems = rest[:N_CHIP_SEMS]
        sends, recvs = _chip_copies(s_ref, land_ref, sems[:N_CHIP - 1], sems[N_CHIP - 1:])
        for cp in sends:
            cp.wait_send()
        for cp in recvs:
            cp.wait_recv()

    return pl.pallas_call(
        body, name=name, in_specs=(_HBM, _HBM) + (_SEM,) * N_CHIP_SEMS + (pl.BlockSpec(memory_space=pl.ANY),),
        out_specs=(_HBM, _HBM),
        out_shape=(pltpu.HBM(sums_thru.shape, sums_thru.dtype), pltpu.HBM(sums_thru.shape, sums_thru.dtype)),
        input_output_aliases={0: 0, 1: 1},
        compiler_params=pltpu.CompilerParams(has_side_effects=pltpu.SideEffectType.DATAFLOW_SIDE_EFFECTING),
    )(sums_thru, land_thru, *sems, after)


def _adamw(name, slots, w, m, v, tr=256):
    unit_mid = w.ndim == 3 and w.shape[1] == 1 and w.shape[0] > 1
    R, Wd = (w.shape[0], w.shape[2]) if unit_mid else w.shape[-2:]
    depth_axis = w.ndim == 3 and not unit_mid
    if R % tr == 0:
        tc = Wd
    else:
        tr, tc = R, (256 if (Wd % 256 == 0 and R > 256) else Wd)
    at = (slice(None), 0, slice(None)) if unit_mid else Ellipsis

    def body(s_ref, w_ref, m_ref, v_ref, g_out, d_out, m_out, v_out):
        g = s_ref[0].astype(F32)
        for j in range(1, slots.shape[0]):
            g = g + s_ref[j].astype(F32)
        m_new = ADAM_B1 * m_ref[at] + (1.0 - ADAM_B1) * g
        v_new = ADAM_B2 * v_ref[at] + (1.0 - ADAM_B2) * jnp.square(g)
        m_hat = m_new / (1.0 - ADAM_B1 ** ADAM_STEP)
        v_hat = v_new / (1.0 - ADAM_B2 ** ADAM_STEP)
        g_out[at] = g
        d_out[at] = -ADAM_LR * (m_hat / (jnp.sqrt(v_hat) + ADAM_EPS) + ADAM_WD * w_ref[at])
        m_out[at] = m_new
        v_out[at] = v_new

    if unit_mid:
        row = pl.BlockSpec((tr, 1, tc), lambda i, j: (i, 0, j))
    elif depth_axis:
        row = pl.BlockSpec((None, tr, tc), lambda i, j: (0, i, j))
    else:
        row = pl.BlockSpec((tr, tc), lambda i, j: (i, j))
    return pl.pallas_call(
        body, name=name, grid=(R // tr, Wd // tc),
        in_specs=[pl.BlockSpec((slots.shape[0], tr, tc), lambda i, j: (0, i, j)), row, row, row],
        out_specs=[row] * 4, out_shape=[jax.ShapeDtypeStruct(w.shape, F32)] * 4,
        compiler_params=pltpu.CompilerParams(dimension_semantics=("parallel", "parallel"),
                                             vmem_limit_bytes=VMEM_LIMIT),
    )(slots, w, m, v)


PACK_W = 1024
PACKED = [(n, s) for n, s in REPLICATED if n != "sgu_w"]
_SMALL_SIZES = [int(np.prod(s)) for _, s in PACKED]
_SMALL_ROWS = _round_up(_round_up(sum(_SMALL_SIZES) + PACK_W, PACK_W) // PACK_W, 8)
_LOSS_AT = sum(_SMALL_SIZES)
W_IN_SHARD = P_TOTAL // N_DEV


def _pack_rows(parts, rows, dtype):
    flat = jnp.concatenate([p.reshape(-1).astype(dtype) for p in parts])
    return jnp.pad(flat, (0, rows * PACK_W - flat.shape[0])).reshape(rows, PACK_W)


def _w_in_groups_t(blocks):
    wt = blocks.reshape(P_TOTAL, D)
    o, c = 2 * D, 2 * D + 3 * D
    z = lambda r: jnp.zeros((r, D), wt.dtype)
    rw = jnp.concatenate([wt[o:c], wt[c:c + L_W], z(128 - L_W), wt[c + L_W:c + L_W + L_A], z(128 - L_A),
                          wt[c + L_W + L_A:o + C_B], z(256 - L_G)], axis=0)
    return wt[:o], rw, wt[o + C_B:]


def _w_in_grad_blocks(g_sgu_t, g_rw_t, g_gate_t):
    c = 3 * D
    full = jnp.concatenate([g_sgu_t, g_rw_t[:c], g_rw_t[c:c + L_W], g_rw_t[c + 128:c + 128 + L_A],
                            g_rw_t[c + 256:c + 256 + L_G], g_gate_t], axis=0)
    return full.reshape(N_DEV, W_IN_SHARD, D)


def _mesh_index():
    me = 4 * lax.axis_index("x") + 2 * lax.axis_index("y") + lax.axis_index("c")
    return me.astype(jnp.int32).reshape(1)


def _fill_slot(name, dst, src, idx, src_idx=None):
    R, Wd = dst.shape[1:]
    scalars = [idx] if src_idx is None else [idx, src_idx]
    if src_idx is None:
        src_spec = pl.BlockSpec((R, Wd), lambda i, *s: (0, 0))
    else:
        src_spec = pl.BlockSpec((None, R, Wd), lambda i, *s: (s[1][0], 0, 0))

    def body(*refs):
        src_ref, out_ref = refs[len(scalars) + 1], refs[len(scalars) + 2]
        out_ref[...] = src_ref[...]

    return pl.pallas_call(
        body, name=name,
        grid_spec=pltpu.PrefetchScalarGridSpec(
            num_scalar_prefetch=len(scalars), grid=(1,),
            in_specs=[pl.BlockSpec(memory_space=pl.ANY), src_spec],
            out_specs=pl.BlockSpec((None, R, Wd), lambda i, *s: (s[0][0], 0, 0))),
        out_shape=jax.ShapeDtypeStruct(dst.shape, dst.dtype),
        input_output_aliases={len(scalars): 0},
        compiler_params=pltpu.CompilerParams(vmem_limit_bytes=VMEM_LIMIT),
    )(*scalars, dst, src)


class TwoLevelGather:
    def __init__(self, bufs, skip_own=()):
        self.bufs, self.nb, self.skip_own = list(bufs), len(bufs), tuple(skip_own)
        self.any_specs = [pl.BlockSpec(memory_space=pl.ANY)] * self.nb
        self.out_shape = [jax.ShapeDtypeStruct((N_DEV,) + b.shape, b.dtype) for b in self.bufs]
        self.sem_shapes = [pltpu.SemaphoreType.DMA((7 * self.nb,)), pltpu.SemaphoreType.DMA((7 * self.nb,)),
                           pltpu.SemaphoreType.DMA((self.nb,))]

    def _copies(self, in_refs, out_refs, sems):
        send_sems, recv_sems, local_sems = sems
        nb = self.nb
        x, y, c = lax.axis_index("x"), lax.axis_index("y"), lax.axis_index("c")
        me, sibling = (x, y, c), (x, y, 1 - c)
        chips = [(1 - x, y), (x, 1 - y), (1 - x, 1 - y)]

        def slot(b, dev):
            return out_refs[b].at[4 * dev[0] + 2 * dev[1] + dev[2]]

        def copy(b, k, block, to, own=False):
            return pltpu.make_async_remote_copy(
                src_ref=in_refs[b] if own else slot(b, block), dst_ref=slot(b, block),
                send_sem=send_sems.at[7 * b + k], recv_sem=recv_sems.at[7 * b + k], device_id=to,
                device_id_type=pl.DeviceIdType.MESH)

        cp = {}
        cp["local"] = [pltpu.make_async_copy(in_refs[b], slot(b, me), local_sems.at[b]) for b in range(nb)
                       if b not in self.skip_own]
        cp["first"] = [copy(b, 0, me, sibling, own=True) for b in range(nb)]
        cp["first"] += [copy(b, 1 + j, me, (*chip, c), own=True) for j, chip in enumerate(chips) for b in range(nb)]
        cp["over_ici"] = [copy(b, 1 + j, (*chip, c), me) for j, chip in enumerate(chips) for b in range(nb)]
        cp["passed"] = [copy(b, 4 + j, (*chip, c), sibling) for j, chip in enumerate(chips) for b in range(nb)]
        cp["from_sibling"] = [copy(b, 0, sibling, me) for b in range(nb)]
        cp["from_sibling"] += [copy(b, 4 + j, (*chip, 1 - c), me) for j, chip in enumerate(chips) for b in range(nb)]
        return cp

    def start(self, in_refs, out_refs, sems):
        cp = self._copies(in_refs, out_refs, sems)
        for c in cp["first"] + cp["local"]:
            c.start()

    def forward(self, in_refs, out_refs, sems):
        cp = self._copies(in_refs, out_refs, sems)
        for arrived, onward in zip(cp["over_ici"], cp["passed"]):
            arrived.wait_recv()
            onward.start()

    def finish(self, in_refs, out_refs, sems):
        cp = self._copies(in_refs, out_refs, sems)
        for c in cp["from_sibling"]:
            c.wait_recv()
        for c in cp["first"] + cp["passed"]:
            c.wait_send()
        for c in cp["local"]:
            c.wait()

    def schedule(self, n_steps):
        return [(0, self.start), (max(n_steps - 3, 0), self.forward), (n_steps - 1, self.finish)]


def _all_gather_two_level(name, bufs, skip_own=()):
    ex = TwoLevelGather(bufs, skip_own)

    def body(*refs):
        args = refs[:ex.nb], refs[ex.nb:2 * ex.nb], refs[2 * ex.nb:]
        ex.start(*args)
        ex.forward(*args)
        ex.finish(*args)

    return pl.pallas_call(body, name=name, in_specs=ex.any_specs, out_specs=ex.any_specs, out_shape=ex.out_shape,
                          scratch_shapes=ex.sem_shapes)(*ex.bufs)


def _cols_from_blocks(blk):
    return jnp.transpose(blk, (1, 0, 2)).reshape(blk.shape[1], -1)


def _cols_to_blocks(g):
    r, c = g.shape
    return jnp.transpose(g.reshape(r, N_DEV, c // N_DEV), (1, 0, 2))


FIRST_WEIGHTS = ["w_in", "shift_b", "w_lora_w", "a_lora_w", "g_lora_w"]
LATE_WEIGHTS = ["w_proj_a", "w_proj_b", "w_out", "w_ffn1", "w_ffn2"]


def _late_weights(shards):
    ex = TwoLevelGather([shards[n].astype(BF16) for n in LATE_WEIGHTS])

    def finish(results):
        got = dict(zip(LATE_WEIGHTS, results))
        W = {n: got[n].reshape(-1, D) for n in ("w_proj_a", "w_proj_b", "w_out", "w_ffn2")}
        W["w_ffn1"] = got["w_ffn1"].reshape(N_DEV, D, -1)
        return W
    return ex, finish


def _gather_weights(shards):
    def payload(n):
        if n == "w_in":
            return jnp.transpose(shards[n][0]).astype(BF16)
        return shards[n] if n == "shift_b" else shards[n].astype(BF16)
    payloads = [payload(n) for n in FIRST_WEIGHTS]
    got = dict(zip(FIRST_WEIGHTS, _all_gather_two_level("weight_all_gather", payloads, skip_own=(0,))))
    got["w_in"] = _fill_slot("w_in_own_slot", got["w_in"], payloads[0], _mesh_index())
    W = {}
    W["w_sgu_t"], W["w_rw_t"], W["w_gate_t"] = _w_in_groups_t(got["w_in"])
    z = lambda r, c, dt: jnp.zeros((r, c), dt)
    W["w_lora"] = jnp.concatenate([_cols_from_blocks(got["w_lora_w"][:, 0]).astype(F32), z(128 - L_W, D, F32)], axis=0)
    W["a_lora"] = jnp.concatenate([_cols_from_blocks(got["a_lora_w"][:, 0]).astype(F32), z(128 - L_A, D, F32)], axis=0)
    W["g_lora"] = jnp.concatenate([_cols_from_blocks(got["g_lora_w"][:, 0]).astype(F32), z(256 - L_G, D, F32)], axis=0)
    sb = _cols_from_blocks(got["shift_b"][:, 0])
    W["sb"] = jnp.concatenate([sb[:, :3 * D], sb[:, 3 * D:3 * D + L_W], z(2, 128 - L_W, F32),
                               sb[:, 3 * D + L_W:3 * D + L_W + L_A], z(2, 128 - L_A, F32),
                               sb[:, 3 * D + L_W + L_A:], z(2, 256 - L_G, F32)], axis=1)
    return W


def _replicated_weights(rep):
    W = {n: rep[n] for n in ("g_mix", "sgu_ln_w", "sgu_ln_b", "w0", "a0", "k_k", "k_a", "r_k", "ln_x_w", "ln_x_b",
                             "g_ffn")}
    W["g_final"] = rep["g_final"].reshape(1, D)
    W["sgu_w"] = rep["sgu_w"][0]
    W["sgu_bt"] = jnp.transpose(rep["sgu_b"][0])
    return W


def _late_grad_blocks(G):
    blocks = {n: G[n].reshape(N_DEV, -1, D) for n in ("w_proj_a", "w_proj_b", "w_out", "w_ffn2")}
    blocks["w_ffn1"] = G["w_ffn1"]
    return Exchange([blocks[n] for n in LATE_WEIGHTS] + [G["sgu_w"].reshape(SGU_G * SGU_C, SGU_C).astype(GRAD_PAYLOAD)],
                    [False] * len(LATE_WEIGHTS) + [True])


def _first_grad_blocks(G):
    sbg = G["sb"]
    c = 3 * D
    sb = jnp.concatenate([sbg[:, :c], sbg[:, c:c + L_W], sbg[:, c + 128:c + 128 + L_A],
                          sbg[:, c + 256:c + 256 + L_G]], axis=1)
    return {
        "shift_b": _cols_to_blocks(sb),
        "w_lora_w": _cols_to_blocks(G["w_lora"][:L_W]), "a_lora_w": _cols_to_blocks(G["a_lora"][:L_A]),
        "g_lora_w": _cols_to_blocks(G["g_lora"][:L_G]),
    }


def _replicated_grads(G):
    small = {n: G[n] for n in ("g_mix", "sgu_ln_w", "sgu_ln_b", "w0", "a0", "k_k", "k_a", "r_k", "ln_x_w", "ln_x_b",
                               "g_ffn", "g_final")}
    small["sgu_w"] = G["sgu_w"]
    small["sgu_b"] = jnp.transpose(G["sgu_bt"])
    return small


def kernel(x, g_mix, w_in, sgu_ln_w, sgu_ln_b, sgu_w, sgu_b, w_proj_a, shift_b, w_lora_w, w0, a_lora_w, a0, g_lora_w, k_k, k_a, r_k, ln_x_w, ln_x_b, w_proj_b, w_out, g_ffn, w_ffn1, w_ffn2, g_final, loss_target, m_g_mix, m_w_in, m_sgu_ln_w, m_sgu_ln_b, m_sgu_w, m_sgu_b, m_w_proj_a, m_shift_b, m_w_lora_w, m_w0, m_a_lora_w, m_a0, m_g_lora_w, m_k_k, m_k_a, m_r_k, m_ln_x_w, m_ln_x_b, m_w_proj_b, m_w_out, m_g_ffn, m_w_ffn1, m_w_ffn2, m_g_final, v_g_mix, v_w_in, v_sgu_ln_w, v_sgu_ln_b, v_sgu_w, v_sgu_b, v_w_proj_a, v_shift_b, v_w_lora_w, v_w0, v_a_lora_w, v_a0, v_g_lora_w, v_k_k, v_k_a, v_r_k, v_ln_x_w, v_ln_x_b, v_w_proj_b, v_w_out, v_g_ffn, v_w_ffn1, v_w_ffn2, v_g_final):
    env = dict(locals())
    weights = {n: env[n] for n in WEIGHT_ORDER}
    moms = {n: env["m_" + n] for n in WEIGHT_ORDER}
    vars_ = {n: env["v_" + n] for n in WEIGHT_ORDER}

    shards = {n: weights[n] for n, _, _ in SHARDED}
    W = _gather_weights(shards)
    W.update(_replicated_weights({n: weights[n] for n, _ in REPLICATED}))
    in_flight = {}

    def send_w_in_grads(G):
        blocks = _w_in_grad_blocks(G["w_sgu_t"], G["w_rw_t"], G["w_gate_t"])
        got = _pair_exchange("grad_pair_exchange", blocks)
        core = lax.axis_index("c").astype(jnp.int32).reshape(1)
        sums = _pair_sum("grad_pair_sum", blocks, got, core)
        in_flight["sems"], in_flight["sums"], in_flight["land"], token = _chip_exchange_start("grad_chip_start", sums)
        return token

    loss_part, dx, G, late_slots = _local_step(x[0], loss_target[0], W, late_weights=_late_weights(shards),
                                               early_grads=_late_grad_blocks, w_in_grads_ready=send_w_in_grads)

    slots = dict(zip(LATE_WEIGHTS, late_slots))
    blocks = _first_grad_blocks(G)
    small = _replicated_grads(G)
    small_parts = [small[n] for n, _ in PACKED] + [jnp.full((PACK_W,), loss_part, F32)]
    rest = [n for n in FIRST_WEIGHTS if n != "w_in"]
    res = _exchange("grad_exchange", [blocks[n] for n in rest] + [_pack_rows(small_parts, _SMALL_ROWS, F32)],
                    [False] * len(rest) + [True])
    slots.update(zip(rest, res[:-1]))
    small_slots = res[-1]
    sums, chip_slots = _chip_exchange_wait("grad_chip_wait", in_flight["sems"], in_flight["sums"], in_flight["land"],
                                           after=small_slots)
    my_chip = (2 * lax.axis_index("x") + lax.axis_index("y")).astype(jnp.int32).reshape(1)
    slots["w_in"] = _fill_slot("grad_own_slot", chip_slots, sums, my_chip, src_idx=my_chip)

    outs = [dict(), dict(), dict(), dict()]
    for n, _, _ in SHARDED:
        if n == "w_in":
            res = _adamw("adamw_" + n, slots[n], *[jnp.transpose(t, (2, 0, 1)) for t in (weights[n], moms[n], vars_[n])])
            res = [jnp.transpose(t, (1, 2, 0)) for t in res]
        else:
            res = _adamw("adamw_" + n, slots[n], weights[n], moms[n], vars_[n])
        for k in range(4):
            outs[k][n] = res[k]

    sgu_shape = (SGU_G * SGU_C, SGU_C)
    res = _adamw("adamw_sgu_w", late_slots[len(LATE_WEIGHTS)], *[t.reshape(sgu_shape) for t in
                                                                  (weights["sgu_w"], moms["sgu_w"], vars_["sgu_w"])])
    for k in range(4):
        outs[k]["sgu_w"] = res[k].reshape(weights["sgu_w"].shape)

    def packed(d):
        return _pack_rows([d[n] for n, _ in PACKED], _SMALL_ROWS, F32)
    small_out = _adamw("adamw_replicated", small_slots, packed(weights), packed(moms), packed(vars_))
    for k in range(4):
        flat = small_out[k].reshape(-1)
        off = 0
        for (n, s), size in zip(PACKED, _SMALL_SIZES):
            outs[k][n] = flat[off:off + size].reshape(s)
            off += size
    loss = small_out[0].reshape(-1)[_LOSS_AT]
    return (loss, dx[None], *[outs[0][n] for n in WEIGHT_ORDER], *[outs[1][n] for n in WEIGHT_ORDER],
            *[outs[2][n] for n in WEIGHT_ORDER], *[outs[3][n] for n in WEIGHT_ORDER])
```

```python
import functools
import numpy as np
import jax
import jax.numpy as jnp
from jax import lax
from jax.experimental import pallas as pl
from jax.experimental.pallas import tpu as pltpu

F32 = jnp.float32
BF16 = jnp.bfloat16

D = 1024
NH, HN = 16, 64
NP, PW = NH // 2, 2 * HN
SGU_G, SGU_C = 8, 128
L_W, L_A, L_G = 64, 64, 160
C_B = 3 * D + L_W + L_A + L_G
P_TOTAL = 2 * D + C_B + 2 * D
D_FF = 4 * D
RW_INT = 3 * D + 128 + 128 + 256
NORM_EPS, LN_EPS, GN_EPS = 1e-6, 1e-5, 64e-5
N_DEV = 8
LANES = 128
SCAN_C = 64
N_KEPT = 4
SOLVE_B = 16
SCAN_PRECISION = lax.Precision.HIGH
SCAN_OUT_PRECISION = lax.Precision.DEFAULT
GRAD_PAYLOAD = BF16
VMEM_LIMIT = 56 * 1024 * 1024
MATMUL_VMEM_BUDGET = 40 * 1024 * 1024
STEP_COST_BYTES = 512 * 1024
HBM_COST_RATIO = 3

ADAM_LR, ADAM_B1, ADAM_B2, ADAM_EPS, ADAM_WD, ADAM_STEP = 0.001, 0.9, 0.999, 1e-08, 0.01, 10

SHARDED = [
    ("w_in", (D, P_TOTAL), 1), ("w_proj_a", (D, D), 0), ("shift_b", (2, C_B), 1), ("w_lora_w", (L_W, D), 1),
    ("a_lora_w", (L_A, D), 1), ("g_lora_w", (L_G, D), 1), ("w_proj_b", (D, D), 0), ("w_out", (D, D), 0),
    ("w_ffn1", (D, D_FF), 1), ("w_ffn2", (D_FF, D), 0),
]
REPLICATED = [
    ("g_mix", (1, D)), ("sgu_ln_w", (1, D)), ("sgu_ln_b", (1, D)), ("sgu_w", (1, SGU_G, SGU_C, SGU_C)),
    ("sgu_b", (1, SGU_G, SGU_C)), ("w0", (1, D)), ("a0", (1, D)), ("k_k", (1, D)), ("k_a", (1, D)), ("r_k", (1, D)),
    ("ln_x_w", (1, D)), ("ln_x_b", (1, D)), ("g_ffn", (1, D)), ("g_final", (D,)),
]
WEIGHT_ORDER = ["g_mix", "w_in", "sgu_ln_w", "sgu_ln_b", "sgu_w", "sgu_b", "w_proj_a", "shift_b", "w_lora_w", "w0",
                "a_lora_w", "a0", "g_lora_w", "k_k", "k_a", "r_k", "ln_x_w", "ln_x_b", "w_proj_b", "w_out", "g_ffn",
                "w_ffn1", "w_ffn2", "g_final"]


def _shard_shape(shape, axis):
    s = list(shape)
    s[axis] //= N_DEV
    return tuple(s)


def _round_up(n, m):
    return (n + m - 1) // m * m


def _pick(n, target):
    if n <= target:
        return n
    best = None
    for t in range(LANES, target + 1, LANES):
        if n % t == 0:
            best = t
    assert best is not None, (n, target)
    return best


def _matmul(name, a, b, mode, out_dtype=F32, tm=2048, tn=1024, tk=2048, out_blocks=None, epilogue=None, extras=(),
            out_dtypes=(), after=None, whole_rows=False):
    b_blocks = b.shape[0] if b.ndim == 3 else None
    bshape = b.shape if b.ndim == 2 else (b.shape[1], b.shape[0] * b.shape[2])
    if mode == "nn":
        (M, K), (K2, N) = a.shape, bshape
    elif mode == "nt":
        (M, K), (N, K2) = a.shape, bshape
    else:
        (K, M), (K2, N) = a.shape, bshape
    assert K == K2, (name, a.shape, b.shape)
    assert b_blocks is None or mode != "tn"
    assert out_blocks is None or mode == "tn"
    tn = min(tn, N // (out_blocks or 1), bshape[1] // b_blocks if (b_blocks and mode == "nn") else tn)
    tk = min(tk, bshape[1] // b_blocks if (b_blocks and mode == "nt") else tk)
    tm, tn, tk = _pick(M, tm), _pick(N, tn), _pick(K, tk)

    def vmem_bytes(tm, tk):
        tiles = tm * tk * a.dtype.itemsize + tk * tn * b.dtype.itemsize
        for dt in (out_dtypes if epilogue else (out_dtype,)):
            tiles += tm * tn * jnp.dtype(dt).itemsize
        for x in extras:
            arr = x[0] if isinstance(x, tuple) else x
            tiles += (tm if arr.shape[0] > 1 else 1) * tn * arr.dtype.itemsize
        return 2 * tiles + (tm * tn * 4 if K // tk > 1 else 0)

    def cost(tm, tk):
        ni, nj, nk = M // tm, N // tn, K // tk
        steps = ni * nj * nk
        acc_passes = steps * tm * tn * 8 if nk > 1 else 0
        a_reads = M * K * a.dtype.itemsize * (nj if nk > 1 else 1)
        b_reads = K * N * b.dtype.itemsize * (ni if (nj > 1 or nk > 1) else 1)
        return (steps * STEP_COST_BYTES + acc_passes + vmem_bytes(tm, tk) // 2
                + HBM_COST_RATIO * (a_reads + b_reads))

    options = [(m, k) for m in ({M} if whole_rows else {_pick(M, max(t, LANES)) for t in (tm, tm // 2, tm // 4)})
               for k in {_pick(K, max(t, LANES)) for t in (tk, tk // 2, tk // 4)}
               if vmem_bytes(m, k) <= MATMUL_VMEM_BUDGET]
    tm, tk = min(options, key=lambda o: cost(*o))
    nk = K // tk
    dims = {"nn": (((1,), (0,)), ((), ())), "nt": (((1,), (1,)), ((), ())), "tn": (((0,), (0,)), ((), ()))}[mode]

    n_x, n_o = len(extras), len(out_dtypes) if epilogue else 1
    n_after = 0 if after is None else 1

    def body(a_ref, b_ref, *rest):
        x_refs, o_refs, acc = rest[:n_x], rest[n_x + n_after:n_x + n_after + n_o], rest[n_x + n_after + n_o:]
        part = lax.dot_general(a_ref[...].astype(BF16), b_ref[...].astype(BF16), dims, preferred_element_type=F32)

        def finish(res):
            outs = epilogue(res, *[r[...] for r in x_refs]) if epilogue else (res,)
            for r, v in zip(o_refs, outs):
                r[...] = v.astype(r.dtype)

        if nk == 1:
            finish(part)
            return
        acc_ref, k = acc[0], pl.program_id(2)

        @pl.when(k == 0)
        def _():
            acc_ref[...] = part

        @pl.when(k > 0)
        def _():
            acc_ref[...] += part

        @pl.when(k == nk - 1)
        def _():
            finish(acc_ref[...])

    a_spec = {"nn": pl.BlockSpec((tm, tk), lambda i, j, k: (i, k)), "nt": pl.BlockSpec((tm, tk), lambda i, j, k: (i, k)),
              "tn": pl.BlockSpec((tk, tm), lambda i, j, k: (k, i))}[mode]
    b_spec = {"nn": pl.BlockSpec((tk, tn), lambda i, j, k: (k, j)), "nt": pl.BlockSpec((tn, tk), lambda i, j, k: (j, k)),
              "tn": pl.BlockSpec((tk, tn), lambda i, j, k: (k, j))}[mode]
    if b_blocks and mode == "nn":
        per = b.shape[2] // tn
        b_spec = pl.BlockSpec((None, tk, tn), lambda i, j, k: (j // per, k, j % per))
    elif b_blocks:
        per = b.shape[2] // tk
        b_spec = pl.BlockSpec((None, tn, tk), lambda i, j, k: (k // per, j, k % per))
    out_spec = pl.BlockSpec((tm, tn), lambda i, j, k: (i, j))
    out_shape = jax.ShapeDtypeStruct((M, N), out_dtype)
    if out_blocks:
        per_o = N // out_blocks // tn
        out_spec = pl.BlockSpec((None, tm, tn), lambda i, j, k: (j // per_o, i, j % per_o))
        out_shape = jax.ShapeDtypeStruct((out_blocks, M, N // out_blocks), out_dtype)
    x_specs, x_args = [], []
    for x in extras:
        arr, off = x if isinstance(x, tuple) else (x, 0)
        if arr.shape[0] == 1:
            x_specs.append(pl.BlockSpec((1, tn), lambda i, j, k: (0, j)))
        else:
            x_specs.append(pl.BlockSpec((tm, tn), lambda i, j, k, off=off: (i, j + off)))
        x_args.append(arr)
    res = pl.pallas_call(
        body, name=name, grid=(M // tm, N // tn, nk),
        in_specs=[a_spec, b_spec] + x_specs + [pl.BlockSpec(memory_space=pl.ANY)] * n_after,
        out_specs=[out_spec] * n_o if epilogue else out_spec,
        out_shape=[jax.ShapeDtypeStruct((M, N), dt) for dt in out_dtypes] if epilogue else out_shape,
        scratch_shapes=[pltpu.VMEM((tm, tn), F32)] if nk > 1 else [],
        compiler_params=pltpu.CompilerParams(dimension_semantics=("parallel", "parallel", "arbitrary"),
                                             vmem_limit_bytes=VMEM_LIMIT),
    )(a, b, *x_args, *([after] if n_after else []))
    return res


class Rows:
    def __init__(self, arr, width=None, cb=0):
        self.arr, self.width, self.cb = arr, (arr.shape[1] if width is None else width), cb


class Heads:
    def __init__(self, arr):
        self.arr = arr


class Halo:
    def __init__(self, arr, side):
        self.arr, self.side = arr, side


def _rows_call(name, fn, ins, consts, outs, accs=(), tm=256, with_pid=False):
    T = next(o.arr.shape[1] if isinstance(o, Heads) else o.arr.shape[0] for o in ins if not isinstance(o, Halo))
    tm = min(tm, T)
    n_tiles = T // tm
    n_in, n_c, n_out = len(ins), len(consts), len(outs)
    in_specs, args = [], []
    for o in ins:
        if isinstance(o, Rows):
            in_specs.append(pl.BlockSpec((tm, o.width), lambda i, cb=o.cb: (i, cb)))
        elif isinstance(o, Heads):
            in_specs.append(pl.BlockSpec((NP, tm, PW), lambda i: (0, i, 0)))
        else:
            w = o.arr.shape[1]
            if o.side < 0:
                in_specs.append(pl.BlockSpec((8, w), lambda i: (jnp.maximum(i * (tm // 8) - 1, 0), 0)))
            else:
                in_specs.append(pl.BlockSpec((8, w), lambda i: (jnp.minimum((i + 1) * (tm // 8), T // 8 - 1), 0)))
        args.append(o.arr)
    for c in consts:
        in_specs.append(pl.BlockSpec(c.shape, lambda i, nd=c.ndim: (0,) * nd))
        args.append(c)
    out_specs, out_shape = [], []
    for o in outs:
        if o[0] == "rows":
            out_specs.append(pl.BlockSpec((tm, o[1]), lambda i: (i, 0)))
            out_shape.append(jax.ShapeDtypeStruct((T, o[1]), o[2]))
        else:
            out_specs.append(pl.BlockSpec((NP, tm, PW), lambda i: (0, i, 0)))
            out_shape.append(jax.ShapeDtypeStruct((NP, T, PW), o[1]))
    for shape, dt in accs:
        out_specs.append(pl.BlockSpec(shape, lambda i, nd=len(shape): (0,) * nd))
        out_shape.append(jax.ShapeDtypeStruct(shape, dt))

    def body(*refs):
        i = pl.program_id(0)
        vals = []
        vals = [r[...] for r in refs[:n_in + n_c]]
        res = fn(i, n_tiles, *vals) if with_pid else fn(*vals)
        out_refs = refs[n_in + n_c:]
        for r, v in zip(out_refs[:n_out], res[:n_out]):
            r[...] = v.astype(r.dtype)
        if accs:
            @pl.when(i == 0)
            def _():
                for r in out_refs[n_out:]:
                    r[...] = jnp.zeros_like(r)

            for r, v in zip(out_refs[n_out:], res[n_out:]):
                r[...] += v.astype(r.dtype)

    res = pl.pallas_call(
        body, name=name, grid=(n_tiles,), in_specs=in_specs, out_specs=out_specs, out_shape=out_shape,
        compiler_params=pltpu.CompilerParams(dimension_semantics=("arbitrary",), vmem_limit_bytes=VMEM_LIMIT),
    )(*args)
    return res


def _rms(x, g):
    return x * lax.rsqrt(jnp.mean(x * x, axis=-1, keepdims=True) + NORM_EPS) * g


def _gelu(x):
    return 0.5 * x * (1.0 + lax.erf(x * 0.7071067811865476))


def _sigmoid(x):
    return 1.0 / (1.0 + jnp.exp(-x))


def _bdot(a, b):
    return jnp.dot(a.astype(BF16), b.astype(BF16), preferred_element_type=F32)


def _to_heads(x):
    return jnp.concatenate([x[:, p * PW:(p + 1) * PW][None] for p in range(NP)], axis=0)


def _from_heads(xp):
    return jnp.concatenate([xp[p] for p in range(NP)], axis=-1)


def _head_sum(xp):
    low = lax.broadcasted_iota(jnp.int32, xp.shape, xp.ndim - 1) < HN
    both = jnp.sum(xp, axis=-1, keepdims=True)
    first = jnp.sum(jnp.where(low, xp, 0.0), axis=-1, keepdims=True)
    return jnp.where(low, first, both - first)


def _split_pairs(xp):
    return jnp.concatenate([xp[:, :, :HN], xp[:, :, HN:]], axis=0)


def _join_pairs(xh):
    return jnp.concatenate([xh[:NP], xh[NP:]], axis=-1)


def _sgu_fn(p, ln_w, ln_b, sw, sbt):
    z = _gelu(p)
    u, v = z[:, :D], z[:, D:]
    mu = jnp.mean(v, axis=-1, keepdims=True)
    var = jnp.mean(jnp.square(v - mu), axis=-1, keepdims=True)
    vn = (v - mu) * lax.rsqrt(var + LN_EPS) * ln_w + ln_b
    ri = lax.broadcasted_iota(jnp.int32, (SGU_C, SGU_C), 0)
    ci = lax.broadcasted_iota(jnp.int32, (SGU_C, SGU_C), 1)
    mask = (ci <= ri).astype(F32)
    dg = D // SGU_G
    parts = []
    for g in range(SGU_G):
        parts.append(_bdot(sw[g] * mask, vn[:, g * dg:(g + 1) * dg]) + sbt[:, g:g + 1])
    return u * jnp.concatenate(parts, axis=-1)


def _pre_fn(qr, qk, qv, qxw, qxa, qxg, wl, w0, al, a0, gl, k_k, k_a):
    w = -jax.nn.softplus(-(w0 + _bdot(jnp.tanh(qxw), wl))) - 0.5
    lw = -jnp.exp(w)
    aa = _sigmoid(a0 + _bdot(qxa, al))
    g = _bdot(_sigmoid(qxg), gl)
    kk = _to_heads(qk * k_k)
    kk = kk / jnp.maximum(jnp.sqrt(_head_sum(kk * kk)), 1e-12)
    k2 = qk * (1.0 + (aa - 1.0) * k_a)
    return _to_heads(qr), _to_heads(lw), _to_heads(k2), _to_heads(qv), kk, _to_heads(aa), g


def _post_fn(o, r, k2, v, g, ln_w, ln_b, r_k):
    mu = _head_sum(o) * (1.0 / HN)
    d = o - mu
    var = _head_sum(d * d) * (1.0 / HN)
    on = d * lax.rsqrt(var + GN_EPS) * ln_w + ln_b
    bonus = _head_sum(r * k2 * r_k) * v
    return _from_heads(on + bonus) * g


def _gate_fn(pg, ya, yb):
    return _sigmoid(pg[:, :D]) * ya + _sigmoid(pg[:, D:]) * yb


def _bmm(x, y, cx, cy, out_path=False):
    return lax.dot_general(x, y, (((cx,), (cy,)), ((0,), (0,))),
                           precision=SCAN_OUT_PRECISION if out_path else SCAN_PRECISION, preferred_element_type=F32)


def _unit_lower_inverse(M):
    C = M.shape[1]
    ti = lax.broadcasted_iota(jnp.int32, (C, C), 0)
    tj = lax.broadcasted_iota(jnp.int32, (C, C), 1)
    eye = (ti == tj).astype(F32)
    same = lambda b: (ti // b == tj // b).astype(F32)
    X = -(M * same(SOLVE_B))
    inv = eye + X
    span = 1
    while 2 * span < SOLVE_B:
        X = _bmm(X, X, 2, 1)
        inv = inv + _bmm(inv, X, 2, 1)
        span *= 2
    b = SOLVE_B
    while b < C:
        low = M * (same(2 * b) - same(b))
        inv = inv - _bmm(_bmm(inv, low, 2, 1, out_path=True), inv, 2, 1, out_path=True)
        b *= 2
    return inv


@jax.custom_vjp
def _unit_lower_solve(inv, M, y):
    return _bmm(inv, y, 2, 1)


def _unit_lower_solve_fwd(inv, M, y):
    u = _bmm(inv, y, 2, 1)
    return u, (inv, u)


def _unit_lower_solve_bwd(res, du):
    inv, u = res
    dy = _bmm(inv, du, 1, 1)
    return jnp.zeros_like(inv), -_bmm(dy, u, 2, 2), dy


_unit_lower_solve.defvjp(_unit_lower_solve_fwd, _unit_lower_solve_bwd)


@jax.custom_vjp
def _unit_lower_solved(inv, u, M, y):
    return u


_unit_lower_solved.defvjp(lambda inv, u, M, y: (u, (inv, u)),
                          lambda res, du: (jnp.zeros_like(res[0]), jnp.zeros_like(res[1]))
                          + _unit_lower_solve_bwd(res, du)[1:])


@functools.partial(jax.custom_vjp, nondiff_argnums=(0,))
def _kept(fn, value, *args):
    return value


def _kept_fwd(fn, value, *args):
    return value, args


def _kept_bwd(fn, args, d):
    _, vjp = jax.vjp(fn, *args)
    return (jnp.zeros_like(d),) + tuple(vjp(d))


_kept.defvjp(_kept_fwd, _kept_bwd)


def _sum_over_time(x, reverse):
    C = x.shape[1]
    ti = lax.broadcasted_iota(jnp.int32, (C, C), 0)
    tj = lax.broadcasted_iota(jnp.int32, (C, C), 1)
    ones = jnp.broadcast_to(((tj >= ti) if reverse else (tj <= ti)).astype(BF16), (x.shape[0], C, C))
    hi = x.astype(BF16)
    r1 = x - hi.astype(F32)
    mid = r1.astype(BF16)
    lo = (r1 - mid.astype(F32)).astype(BF16)
    dn = (((2,), (1,)), ((0,), (0,)))
    return sum(lax.dot_general(ones, p, dn, preferred_element_type=F32) for p in (lo, mid, hi))


@jax.custom_vjp
def _time_cumsum(lw):
    return _sum_over_time(lw, reverse=False)


_time_cumsum.defvjp(lambda lw: (_sum_over_time(lw, reverse=False), None),
                    lambda _, d: (_sum_over_time(d, reverse=True),))


def _chunk_fn(S0, r, lw, k, v, kk, a, kept=None):
    C = SCAN_C
    bmm = _bmm
    ti = lax.broadcasted_iota(jnp.int32, (C, C), 0)
    tj = lax.broadcasted_iota(jnp.int32, (C, C), 1)
    incl2 = jnp.concatenate([(tj <= ti).astype(F32)] * 2, axis=1)
    strict = (tj < ti).astype(F32)
    n_mask = jnp.concatenate([jnp.zeros((C, C), F32), strict], axis=1)

    def known(name, fn, *args):
        return fn(*args) if kept is None else _kept(fn, kept[name], *args)

    cum = known("cum", _time_cumsum, lw)
    g_in, g_ex, g_inv = jnp.exp(cum), jnp.exp(cum - lw), jnp.exp(-cum)
    kkt, rt = kk * g_ex, r * g_in
    bk = jnp.concatenate([kk * a * g_inv, k * g_inv], axis=1)
    A = known("A", lambda x, y: bmm(x, y, 2, 2), kkt, bk)
    M = A[:, :, :C] * strict
    zv = jnp.concatenate([jnp.zeros_like(v), v], axis=1)
    s0_side = bmm(jnp.concatenate([kkt, rt], axis=1), S0, 2, 2, out_path=True)
    rhs = s0_side[:, :C] + bmm(A * n_mask, zv, 2, 1, out_path=True)
    if kept is None:
        inv = lax.stop_gradient(_unit_lower_inverse(M))
        y = _unit_lower_solve(inv, M, rhs)
    else:
        inv = kept["inv"]
        y = _unit_lower_solved(inv, kept["y"], M, rhs)
    z = jnp.concatenate([-y, v], axis=1)
    attn = known("attn", lambda x, y: bmm(x, y, 2, 2) * incl2, rt, bk)
    O = s0_side[:, C:] + bmm(attn, z, 2, 1, out_path=True)
    g_end = g_in[:, C - 1:C, :]
    S1 = S0 * g_end + bmm(z, bk * g_end, 1, 1, out_path=True)
    return O, S1, dict(cum=cum, A=A, attn=attn, y=y, inv=inv)


def _scan_fwd(r, lw, k, v, kk, a, ex=None, tb=256):
    assert SCAN_C == HN and 2 * SCAN_C == PW
    T = r.shape[1]
    tb = min(tb, T)
    n_chunks = tb // SCAN_C
    nb = T // tb
    nx = ex.nb if ex else 0

    def body(*refs):
        r_ref, lw_ref, k_ref, v_ref, kk_ref, a_ref = refs[:6]
        x_in, (o_ref, s0_ref), x_out = refs[6:6 + nx], refs[6 + nx:8 + nx], refs[8 + nx:8 + 2 * nx]
        s_ref, sems = refs[8 + 2 * nx], refs[9 + 2 * nx:]

        plan = ex.schedule(nb) if ex else []

        @pl.when(pl.program_id(0) == 0)
        def _():
            s_ref[...] = jnp.zeros_like(s_ref)
            for at, action in plan[:1]:
                action(x_in, x_out, sems)

        def step(c, carry):
            sl = pl.ds(pl.multiple_of(c * SCAN_C, SCAN_C), SCAN_C)
            S0 = s_ref[...]
            O, S1, keep = _chunk_fn(S0, *[_split_pairs(ref[:, sl, :])
                                          for ref in (r_ref, lw_ref, k_ref, v_ref, kk_ref, a_ref)])
            o_ref[:, sl, :] = _join_pairs(O)
            s0_ref[c, 0] = jnp.concatenate([S0, keep["inv"]], axis=-1)
            s0_ref[c, 1] = jnp.concatenate([keep["cum"], keep["y"]], axis=-1)
            s0_ref[c, 2] = keep["A"]
            s0_ref[c, 3] = keep["attn"]
            s_ref[...] = S1
            return carry

        lax.fori_loop(0, n_chunks, step, 0)

        for at, action in plan[1:]:
            pl.when(pl.program_id(0) == at)(functools.partial(action, x_in, x_out, sems))

    hm = pl.BlockSpec((NP, tb, PW), lambda i: (0, i, 0))
    res = pl.pallas_call(
        body, name="rwkv_scan_fwd", grid=(nb,), in_specs=[hm] * 6 + (ex.any_specs if ex else []),
        out_specs=[hm, pl.BlockSpec((n_chunks, N_KEPT, NH, HN, PW), lambda i: (i, 0, 0, 0, 0))]
        + (ex.any_specs if ex else []),
        out_shape=[jax.ShapeDtypeStruct((NP, T, PW), F32),
                   jax.ShapeDtypeStruct((T // SCAN_C, N_KEPT, NH, HN, PW), F32)]
        + (ex.out_shape if ex else []),
        scratch_shapes=[pltpu.VMEM((NH, HN, HN), F32)] + (ex.sem_shapes if ex else []),
        compiler_params=pltpu.CompilerParams(dimension_semantics=("arbitrary",), vmem_limit_bytes=VMEM_LIMIT),
    )(r, lw, k, v, kk, a, *(ex.bufs if ex else []))
    return res[0], res[1], list(res[2:])


def _scan_bwd(r, lw, k, v, kk, a, s0s, do, ex=None, tb=128):
    T = r.shape[1]
    tb = min(tb, T)
    n_chunks = tb // SCAN_C
    nb = T // tb
    nx = ex.nb if ex else 0

    def body(*refs):
        r_ref, lw_ref, k_ref, v_ref, kk_ref, a_ref, s0_ref, do_ref = refs[:8]
        x_in, (dr, dlw, dk, dv, dkk, da), x_out = refs[8:8 + nx], refs[8 + nx:14 + nx], refs[14 + nx:14 + 2 * nx]
        ds_ref, sems = refs[14 + 2 * nx], refs[15 + 2 * nx:]

        plan = ex.schedule(nb) if ex else []

        @pl.when(pl.program_id(0) == 0)
        def _():
            ds_ref[...] = jnp.zeros_like(ds_ref)
            for at, action in plan[:1]:
                action(x_in, x_out, sems)

        def step(j, carry):
            c = n_chunks - 1 - j
            sl = pl.ds(pl.multiple_of(c * SCAN_C, SCAN_C), SCAN_C)
            s0_inv, cum_y = s0_ref[c, 0], s0_ref[c, 1]
            kept = dict(inv=s0_inv[:, :, HN:], cum=cum_y[:, :, :HN], y=cum_y[:, :, HN:], A=s0_ref[c, 2],
                        attn=s0_ref[c, 3])
            _, vjp = jax.vjp(lambda *t: _chunk_fn(*t, kept=kept)[:2], s0_inv[:, :, :HN],
                             *[_split_pairs(ref[:, sl, :]) for ref in (r_ref, lw_ref, k_ref, v_ref, kk_ref, a_ref)])
            g = vjp((_split_pairs(do_ref[:, sl, :]), ds_ref[...]))
            ds_ref[...] = g[0]
            for ref, val in zip((dr, dlw, dk, dv, dkk, da), g[1:]):
                ref[:, sl, :] = _join_pairs(val)
            return carry

        lax.fori_loop(0, n_chunks, step, 0)

        for at, action in plan[1:]:
            pl.when(pl.program_id(0) == at)(functools.partial(action, x_in, x_out, sems))

    hm = pl.BlockSpec((NP, tb, PW), lambda i: (0, nb - 1 - i, 0))
    res = pl.pallas_call(
        body, name="rwkv_scan_bwd", grid=(nb,),
        in_specs=[hm] * 6 + [pl.BlockSpec((n_chunks, N_KEPT, NH, HN, PW), lambda i: (nb - 1 - i, 0, 0, 0, 0)), hm]
        + (ex.any_specs if ex else []),
        out_specs=[hm] * 6 + (ex.any_specs if ex else []),
        out_shape=[jax.ShapeDtypeStruct((NP, T, PW), F32)] * 6 + (ex.out_shape if ex else []),
        scratch_shapes=[pltpu.VMEM((NH, HN, HN), F32)] + (ex.sem_shapes if ex else []),
        compiler_params=pltpu.CompilerParams(dimension_semantics=("arbitrary",), vmem_limit_bytes=VMEM_LIMIT),
    )(r, lw, k, v, kk, a, s0s, do, *(ex.bufs if ex else []))
    return list(res[:6]), list(res[6:])


def _shift_down(i, p, prev8):
    first = jnp.where(i > 0, prev8[7:8, :], 0.0)
    row = lax.broadcasted_iota(jnp.int32, p.shape, 0)
    return jnp.where(row == 0, first, pltpu.roll(p, 1, axis=0))


def _mix_bwd(dq, p, sb, tm=256):
    def fn(i, n, dq, next8, p, prev8, sb):
        ps = _shift_down(i, p, prev8)
        d1 = dq * sb[1:2]
        last = jnp.where(i < n - 1, next8[0:1, :] * sb[1:2], 0.0)
        row = lax.broadcasted_iota(jnp.int32, dq.shape, 0)
        up = jnp.where(row == dq.shape[0] - 1, last, pltpu.roll(d1, dq.shape[0] - 1, axis=0))
        return (dq * sb[0:1] + up, jnp.sum(dq * p, axis=0, keepdims=True), jnp.sum(dq * ps, axis=0, keepdims=True))
    w = p.shape[1]
    return _rows_call("shift_mix_bwd", fn, [Rows(dq), Halo(dq, +1), Rows(p), Halo(p, -1)], [sb], [("rows", w, BF16)],
                      accs=[((1, w), F32), ((1, w), F32)], tm=tm, with_pid=True)


def _local_step(x, target, W, late_weights=None, early_grads=None, w_in_grads_ready=None, ffn_grads_ready=None):
    G = {}
    a = _rows_call("norm_mix_fwd", lambda x, g: (_rms(x, g),), [Rows(x)], [W["g_mix"]], [("rows", D, BF16)])[0]
    p_sgu = _matmul("proj_sgu", a, W["w_sgu_t"], "nt")
    def token_shift(p, sb0, sb1):
        row = lax.broadcasted_iota(jnp.int32, p.shape, 0)
        return p, p * sb0 + jnp.where(row == 0, 0.0, pltpu.roll(p, 1, axis=0)) * sb1
    p_rw, q = _matmul("proj_rwkv", a, W["w_rw_t"], "nt", tn=512, whole_rows=True, epilogue=token_shift,
                      extras=[W["sb"][0:1], W["sb"][1:2]], out_dtypes=(F32, F32))
    p_gate = _matmul("proj_gate", a, W["w_gate_t"], "nt")

    sgu_consts = [W["sgu_ln_w"], W["sgu_ln_b"], W["sgu_w"], W["sgu_bt"]]
    s = _rows_call("sgu_fwd", lambda *t: (_sgu_fn(*t),), [Rows(p_sgu)], sgu_consts, [("rows", D, BF16)], tm=SGU_C)[0]

    q_ins = [Rows(q, D, 0), Rows(q, D, 1), Rows(q, D, 2), Rows(q, 128, 24), Rows(q, 128, 25), Rows(q, 256, 13)]
    pre_consts = [W["w_lora"], W["w0"], W["a_lora"], W["a0"], W["g_lora"], W["k_k"], W["k_a"]]
    r_h, lw_h, k_h, v_h, kk_h, a_h, g_gate = _rows_call(
        "rwkv_pre_fwd", _pre_fn, q_ins, pre_consts, [("heads", F32)] * 6 + [("rows", D, F32)], tm=128)
    o_h, s0s, got = _scan_fwd(r_h, lw_h, k_h, v_h, kk_h, a_h, ex=late_weights[0] if late_weights else None)
    if late_weights:
        W = {**W, **late_weights[1](got)}
    y_a = _matmul("proj_a", s, W["w_proj_a"], "nn")
    post_ins = [Heads(o_h), Heads(r_h), Heads(k_h), Heads(v_h), Rows(g_gate)]
    post_consts = [W[n].reshape(NP, 1, PW) for n in ("ln_x_w", "ln_x_b", "r_k")]
    z_b = _rows_call("rwkv_post_fwd", lambda *t: (_post_fn(*t),), post_ins, post_consts, [("rows", D, BF16)], tm=128)[0]
    y_b, mixed = _matmul("proj_b", z_b, W["w_proj_b"], "nn", extras=[(p_gate, 0), (p_gate, 1), y_a],
                         epilogue=lambda yb, ga, gb, ya: (yb, _sigmoid(ga) * ya + _sigmoid(gb) * yb),
                         out_dtypes=(F32, BF16))
    gate_ins = [Rows(p_gate), Rows(y_a), Rows(y_b)]

    def res1(mo, x, g):
        h1 = x + mo
        return h1, _rms(h1, g)
    h1, f = _matmul("proj_out", mixed, W["w_out"], "nn", extras=[x, W["g_ffn"]], epilogue=res1,
                    out_dtypes=(F32, BF16))

    def relu_sq(u):
        r = jnp.maximum(u, 0.0)
        return r, r * r
    r1, act = _matmul("ffn_up", f, W["w_ffn1"], "nn", epilogue=relu_sq, out_dtypes=(BF16, BF16))
    ff = _matmul("ffn_down", act, W["w_ffn2"], "nn")

    def head(h1, ff, tgt, g):
        def f_(h1, ff, g):
            y = _rms(h1 + ff, g)
            return 0.5 * jnp.sum(jnp.mean(jnp.square(y - tgt), axis=-1))
        loss, (dh2, _, dg) = jax.value_and_grad(f_, argnums=(0, 1, 2))(h1, ff, g)
        return dh2, jnp.full((8, LANES), loss, F32), dg
    dh2, loss_acc, G["g_final"] = _rows_call("loss_head", head, [Rows(h1), Rows(ff), Rows(target)], [W["g_final"]],
                                             [("rows", D, F32)], accs=[((8, LANES), F32), ((1, D), F32)])

    d_u1 = _matmul("ffn_down_dx", dh2, W["w_ffn2"], "nt", extras=[r1], out_dtypes=(BF16,),
                   epilogue=lambda d_act, r: (d_act * 2.0 * r.astype(F32),))[0]
    G["w_ffn2"] = _matmul("ffn_down_dw", act, dh2, "tn", out_dtype=GRAD_PAYLOAD)
    d_f = _matmul("ffn_up_dx", d_u1, W["w_ffn1"], "nt")
    G["w_ffn1"] = _matmul("ffn_up_dw", f, d_u1, "tn", out_blocks=N_DEV, out_dtype=GRAD_PAYLOAD)

    def res1_bwd(h1, d_f, dh2, g):
        _, vjp = jax.vjp(_rms, h1, g)
        dh, dg = vjp(d_f)
        return dh2 + dh, dg
    dh1, G["g_ffn"] = _rows_call("residual_norm_bwd", res1_bwd, [Rows(h1), Rows(d_f), Rows(dh2)],
                                 [W["g_ffn"]], [("rows", D, F32)], accs=[((1, D), F32)])
    ffn_token = ffn_grads_ready(G) if ffn_grads_ready else None
    d_mixed = _matmul("proj_out_dx", dh1, W["w_out"], "nt", after=ffn_token)
    G["w_out"] = _matmul("proj_out_dw", mixed, dh1, "tn", out_dtype=GRAD_PAYLOAD)

    def gate_bwd(pg, ya, yb, dm):
        _, vjp = jax.vjp(_gate_fn, pg, ya, yb)
        return vjp(dm)
    d_gate, d_ya, d_yb = _rows_call("gate_bwd", gate_bwd, gate_ins + [Rows(d_mixed)], [],
                                    [("rows", 2 * D, BF16), ("rows", D, BF16), ("rows", D, BF16)])

    d_s = _matmul("proj_a_dx", d_ya, W["w_proj_a"], "nt")
    G["w_proj_a"] = _matmul("proj_a_dw", s, d_ya, "tn", out_dtype=GRAD_PAYLOAD)

    def sgu_bwd(p, ds, *c):
        _, vjp = jax.vjp(_sgu_fn, p, *c)
        return vjp(ds)
    d_p_sgu, G["sgu_ln_w"], G["sgu_ln_b"], G["sgu_w"], G["sgu_bt"] = _rows_call(
        "sgu_bwd", sgu_bwd, [Rows(p_sgu), Rows(d_s)], sgu_consts, [("rows", 2 * D, BF16)],
        accs=[((1, D), F32), ((1, D), F32), ((SGU_G, SGU_C, SGU_C), F32), ((SGU_C, SGU_G), F32)], tm=SGU_C)

    d_zb = _matmul("proj_b_dx", d_yb, W["w_proj_b"], "nt")
    G["w_proj_b"] = _matmul("proj_b_dw", z_b, d_yb, "tn", out_dtype=GRAD_PAYLOAD)

    def post_bwd(o, r, k2, v, g, dz, *c):
        _, vjp = jax.vjp(_post_fn, o, r, k2, v, g, *c)
        return vjp(dz)
    do_h, dr1, dk1, dv1, d_g, g_lnw, g_lnb, g_rk = _rows_call(
        "rwkv_post_bwd", post_bwd, post_ins + [Rows(d_zb)], post_consts, [("heads", F32)] * 4 + [("rows", D, F32)],
        accs=[((NP, 1, PW), F32)] * 3, tm=128)
    G["ln_x_w"], G["ln_x_b"], G["r_k"] = (t.reshape(1, D) for t in (g_lnw, g_lnb, g_rk))
    (dr2, dlw, dk2, dv2, dkk, daa), early = _scan_bwd(r_h, lw_h, k_h, v_h, kk_h, a_h, s0s, do_h,
                                                      ex=early_grads(G) if early_grads else None)

    def pre_bwd(qr, qk, qv, qxw, qxa, qxg, dr1, dr2, dlw, dk1, dk2, dv1, dv2, dkk, daa, dg, *c):
        _, vjp = jax.vjp(_pre_fn, qr, qk, qv, qxw, qxa, qxg, *c)
        g = vjp((dr1 + dr2, dlw, dk1 + dk2, dv1 + dv2, dkk, daa, dg))
        dq = jnp.concatenate(g[:6], axis=-1)
        return (dq,) + tuple(g[6:])
    pre_b_ins = q_ins + [Heads(dr1), Heads(dr2), Heads(dlw), Heads(dk1), Heads(dk2), Heads(dv1), Heads(dv2),
                         Heads(dkk), Heads(daa), Rows(d_g)]
    d_q, G["w_lora"], G["w0"], G["a_lora"], G["a0"], G["g_lora"], G["k_k"], G["k_a"] = _rows_call(
        "rwkv_pre_bwd", pre_bwd, pre_b_ins, pre_consts, [("rows", RW_INT, F32)],
        accs=[((128, D), F32), ((1, D), F32), ((128, D), F32), ((1, D), F32), ((256, D), F32), ((1, D), F32),
              ((1, D), F32)], tm=128)
    d_p_rw, dsb0, dsb1 = _mix_bwd(d_q, p_rw, W["sb"])
    G["sb"] = jnp.concatenate([dsb0, dsb1], axis=0)

    G["w_sgu_t"] = _matmul("proj_sgu_dw", d_p_sgu, a, "tn", out_dtype=GRAD_PAYLOAD)
    G["w_rw_t"] = _matmul("proj_rwkv_dw", d_p_rw, a, "tn", out_dtype=GRAD_PAYLOAD)
    G["w_gate_t"] = _matmul("proj_gate_dw", d_gate, a, "tn", out_dtype=GRAD_PAYLOAD)
    token = w_in_grads_ready(G) if w_in_grads_ready else None
    da1 = _matmul("proj_sgu_dx", d_p_sgu, W["w_sgu_t"], "nn", after=token)
    da2 = _matmul("proj_rwkv_dx", d_p_rw, W["w_rw_t"], "nn", after=token)
    da3 = _matmul("proj_gate_dx", d_gate, W["w_gate_t"], "nn", after=token)

    def norm1_bwd(x, da1, da2, da3, dh1, g):
        _, vjp = jax.vjp(_rms, x, g)
        dx, dg = vjp(da1 + da2 + da3)
        return dh1 + dx, dg
    dx, G["g_mix"] = _rows_call("norm_mix_bwd", norm1_bwd, [Rows(x), Rows(da1), Rows(da2), Rows(da3), Rows(dh1)],
                                [W["g_mix"]], [("rows", D, F32)], accs=[((1, D), F32)])
    return loss_acc[0, 0], dx, G, early


class Exchange:
    def __init__(self, bufs, gathers):
        self.bufs, self.gathers, self.nb = list(bufs), list(gathers), len(bufs)
        self.any_specs = [pl.BlockSpec(memory_space=pl.ANY)] * self.nb
        self.out_shape = [jax.ShapeDtypeStruct((N_DEV,) + (b.shape if g else b.shape[1:]), b.dtype)
                          for b, g in zip(self.bufs, self.gathers)]
        n = (N_DEV - 1) * self.nb
        self.sem_shapes = [pltpu.SemaphoreType.DMA((n,)), pltpu.SemaphoreType.DMA((n,)),
                           pltpu.SemaphoreType.DMA((self.nb,))]

    def _copies(self, in_refs, out_refs, sems):
        send_sems, recv_sems, local_sems = sems
        x, y, c = lax.axis_index("x"), lax.axis_index("y"), lax.axis_index("c")
        me = 4 * x + 2 * y + c

        def src(b, dest):
            return in_refs[b] if self.gathers[b] else in_refs[b].at[dest]

        local = [pltpu.make_async_copy(src(b, me), out_refs[b].at[me], local_sems.at[b]) for b in range(self.nb)]
        sends, recvs = [], []
        for kbits in range(1, N_DEV):
            px = 1 - x if kbits & 4 else x
            py = 1 - y if kbits & 2 else y
            pc = 1 - c if kbits & 1 else c
            peer = 4 * px + 2 * py + pc
            for b in range(self.nb):
                s = (kbits - 1) * self.nb + b
                sends.append(pltpu.make_async_remote_copy(
                    src_ref=src(b, peer), dst_ref=out_refs[b].at[me], send_sem=send_sems.at[s],
                    recv_sem=recv_sems.at[s], device_id=(px, py, pc), device_id_type=pl.DeviceIdType.MESH))
                recvs.append(pltpu.make_async_remote_copy(
                    src_ref=src(b, peer), dst_ref=out_refs[b].at[peer], send_sem=send_sems.at[s],
                    recv_sem=recv_sems.at[s], device_id=(px, py, pc), device_id_type=pl.DeviceIdType.MESH))
        return local, sends, recvs

    def start(self, in_refs, out_refs, sems):
        local, sends, _ = self._copies(in_refs, out_refs, sems)
        for cp in sends + local:
            cp.start()

    def wait(self, in_refs, out_refs, sems):
        local, sends, recvs = self._copies(in_refs, out_refs, sems)
        for cp in recvs:
            cp.wait_recv()
        for cp in sends:
            cp.wait_send()
        for cp in local:
            cp.wait()

    def schedule(self, n_steps):
        return [(0, self.start), (n_steps - 1, self.wait)]


def _exchange(name, bufs, gather):
    ex = Exchange(bufs, gather if isinstance(gather, (list, tuple)) else [gather] * len(bufs))

    def body(*refs):
        in_refs, out_refs, sems = refs[:ex.nb], refs[ex.nb:2 * ex.nb], refs[2 * ex.nb:]
        ex.start(in_refs, out_refs, sems)
        ex.wait(in_refs, out_refs, sems)

    return pl.pallas_call(body, name=name, in_specs=ex.any_specs, out_specs=ex.any_specs, out_shape=ex.out_shape,
                          scratch_shapes=ex.sem_shapes)(*ex.bufs)


N_CHIP = 4


def _pair_exchange(name, blocks):
    def body(b_ref, got_ref, send_sems, recv_sems):
        x, y, c = lax.axis_index("x"), lax.axis_index("y"), lax.axis_index("c")
        copies = [pltpu.make_async_remote_copy(
            src_ref=b_ref.at[2 * q + 1 - c], dst_ref=got_ref.at[q], send_sem=send_sems.at[q],
            recv_sem=recv_sems.at[q], device_id=(x, y, 1 - c), device_id_type=pl.DeviceIdType.MESH)
            for q in range(N_CHIP)]
        for cp in copies:
            cp.start()
        for cp in copies:
            cp.wait_recv()
        for cp in copies:
            cp.wait_send()

    any_spec = pl.BlockSpec(memory_space=pl.ANY)
    return pl.pallas_call(
        body, name=name, in_specs=[any_spec], out_specs=any_spec,
        out_shape=jax.ShapeDtypeStruct((N_CHIP,) + blocks.shape[1:], blocks.dtype),
        scratch_shapes=[pltpu.SemaphoreType.DMA((N_CHIP,))] * 2,
    )(blocks)


def _pair_sum(name, blocks, got, core):
    _, R, Wd = blocks.shape

    def body(c_ref, a_ref, b_ref, o_ref):
        o_ref[...] = (a_ref[...].astype(F32) + b_ref[...].astype(F32)).astype(o_ref.dtype)

    return pl.pallas_call(
        body, name=name,
        grid_spec=pltpu.PrefetchScalarGridSpec(
            num_scalar_prefetch=1, grid=(N_CHIP,),
            in_specs=[pl.BlockSpec((None, R, Wd), lambda q, c_ref: (2 * q + c_ref[0], 0, 0)),
                      pl.BlockSpec((None, R, Wd), lambda q, c_ref: (q, 0, 0))],
            out_specs=pl.BlockSpec((None, R, Wd), lambda q, c_ref: (q, 0, 0))),
        out_shape=jax.ShapeDtypeStruct(got.shape, blocks.dtype),
        compiler_params=pltpu.CompilerParams(dimension_semantics=("parallel",), vmem_limit_bytes=VMEM_LIMIT),
    )(core, blocks, got)


def _chip_copies(s_ref, land_ref, send_sems, recv_sems):
    x, y, c = lax.axis_index("x"), lax.axis_index("y"), lax.axis_index("c")
    my_q = 2 * x + y
    sends, recvs = [], []
    for kbits in range(1, N_CHIP):
        px = 1 - x if kbits & 2 else x
        py = 1 - y if kbits & 1 else y
        peer_q = 2 * px + py
        sends.append(pltpu.make_async_remote_copy(
            src_ref=s_ref.at[peer_q], dst_ref=land_ref.at[my_q], send_sem=send_sems[kbits - 1],
            recv_sem=recv_sems[kbits - 1], device_id=(px, py, c), device_id_type=pl.DeviceIdType.MESH))
        recvs.append(pltpu.make_async_remote_copy(
            src_ref=s_ref.at[peer_q], dst_ref=land_ref.at[peer_q], send_sem=send_sems[kbits - 1],
            recv_sem=recv_sems[kbits - 1], device_id=(px, py, c), device_id_type=pl.DeviceIdType.MESH))
    return sends, recvs


_HBM = pl.BlockSpec(memory_space=pltpu.HBM)
_SEM = pl.BlockSpec(memory_space=pltpu.SEMAPHORE)
N_CHIP_SEMS = 2 * (N_CHIP - 1)


def _scatter_copies(b_refs, land_refs, send_sems, recv_sems):
    x, y, c = lax.axis_index("x"), lax.axis_index("y"), lax.axis_index("c")
    me = 4 * x + 2 * y + c
    sends, recvs = [], []
    for kbits in range(1, N_DEV):
        px = 1 - x if kbits & 4 else x
        py = 1 - y if kbits & 2 else y
        pc = 1 - c if kbits & 1 else c
        peer = 4 * px + 2 * py + pc
        for b in range(len(b_refs)):
            s = (kbits - 1) * len(b_refs) + b
            sends.append(pltpu.make_async_remote_copy(
                src_ref=b_refs[b].at[peer], dst_ref=land_refs[b].at[me], send_sem=send_sems[s],
                recv_sem=recv_sems[s], device_id=(px, py, pc), device_id_type=pl.DeviceIdType.MESH))
            recvs.append(pltpu.make_async_remote_copy(
                src_ref=b_refs[b].at[peer], dst_ref=land_refs[b].at[peer], send_sem=send_sems[s],
                recv_sem=recv_sems[s], device_id=(px, py, pc), device_id_type=pl.DeviceIdType.MESH))
    return sends, recvs


def _scatter_start(name, bufs):
    nb = len(bufs)
    n = (N_DEV - 1) * nb

    def body(*refs):
        b_refs, land_refs, outs = refs[:nb], refs[nb:2 * nb], refs[2 * nb:]
        sends, _ = _scatter_copies(b_refs, land_refs, outs[:n], outs[n:2 * n])
        for cp in sends:
            cp.start()
        token = outs[2 * n + 2 * nb]
        token[...] = jnp.zeros_like(token)

    thru = tuple(pltpu.HBM(b.shape, b.dtype) for b in bufs)
    res = pl.pallas_call(
        body, name=name, in_specs=(_HBM,) * (2 * nb),
        out_specs=(_SEM,) * (2 * n) + (_HBM,) * (2 * nb) + (pl.BlockSpec(memory_space=pltpu.VMEM),),
        out_shape=(pltpu.SemaphoreType.DMA(()),) * (2 * n) + thru + thru + (jax.ShapeDtypeStruct((8, LANES), F32),),
        input_output_aliases={i: 2 * n + i for i in range(2 * nb)},
        compiler_params=pltpu.CompilerParams(has_side_effects=pltpu.SideEffectType.DATAFLOW_SIDE_EFFECTING),
    )(*[pltpu.with_memory_space_constraint(b, pltpu.HBM) for b in bufs],
      *[pltpu.with_memory_space_constraint(lax.empty(b.shape, b.dtype), pltpu.HBM) for b in bufs])
    return res[:2 * n], list(res[2 * n:2 * n + nb]), list(res[2 * n + nb:2 * n + 2 * nb]), res[2 * n + 2 * nb]


def _scatter_wait(name, sems, bufs_thru, lands_thru, after):
    nb = len(bufs_thru)
    n = (N_DEV - 1) * nb

    def body(*refs):
        b_refs, land_refs, sem_refs = refs[:nb], refs[nb:2 * nb], refs[2 * nb:2 * nb + 2 * n]
        sends, recvs = _scatter_copies(b_refs, land_refs, sem_refs[:n], sem_refs[n:])
        for cp in sends:
            cp.wait_send()
        for cp in recvs:
            cp.wait_recv()

    thru = tuple(pltpu.HBM(b.shape, b.dtype) for b in bufs_thru)
    res = pl.pallas_call(
        body, name=name, in_specs=(_HBM,) * (2 * nb) + (_SEM,) * (2 * n) + (pl.BlockSpec(memory_space=pl.ANY),),
        out_specs=(_HBM,) * (2 * nb), out_shape=thru + thru,
        input_output_aliases={i: i for i in range(2 * nb)},
        compiler_params=pltpu.CompilerParams(has_side_effects=pltpu.SideEffectType.DATAFLOW_SIDE_EFFECTING),
    )(*bufs_thru, *lands_thru, *sems, after)
    return list(res[:nb]), list(res[nb:])


def _chip_exchange_start(name, sums):
    def body(s_ref, land_ref, *outs):
        sems, token = outs[:N_CHIP_SEMS], outs[N_CHIP_SEMS + 2]
        sends, _ = _chip_copies(s_ref, land_ref, sems[:N_CHIP - 1], sems[N_CHIP - 1:])
        for cp in sends:
            cp.start()
        token[...] = jnp.zeros_like(token)

    res = pl.pallas_call(
        body, name=name, in_specs=(_HBM, _HBM),
        out_specs=(_SEM,) * N_CHIP_SEMS + (_HBM, _HBM, pl.BlockSpec(memory_space=pltpu.VMEM)),
        out_shape=(pltpu.SemaphoreType.DMA(()),) * N_CHIP_SEMS
        + (pltpu.HBM(sums.shape, sums.dtype), pltpu.HBM(sums.shape, sums.dtype), jax.ShapeDtypeStruct((8, LANES), F32)),
        input_output_aliases={0: N_CHIP_SEMS, 1: N_CHIP_SEMS + 1},
        compiler_params=pltpu.CompilerParams(has_side_effects=pltpu.SideEffectType.DATAFLOW_SIDE_EFFECTING),
    )(pltpu.with_memory_space_constraint(sums, pltpu.HBM),
      pltpu.with_memory_space_constraint(lax.empty(sums.shape, sums.dtype), pltpu.HBM))
    return res[:N_CHIP_SEMS], res[N_CHIP_SEMS], res[N_CHIP_SEMS + 1], res[N_CHIP_SEMS + 2]


def _chip_exchange_wait(name, sems, sums_thru, land_thru, after):
    def body(s_ref, land_ref, *rest):
        sems = rest[:N_CHIP_SEMS]
        sends, recvs = _chip_copies(s_ref, land_ref, sems[:N_CHIP - 1], sems[N_CHIP - 1:])
        for cp in sends:
            cp.wait_send()
        for cp in recvs:
            cp.wait_recv()

    return pl.pallas_call(
        body, name=name, in_specs=(_HBM, _HBM) + (_SEM,) * N_CHIP_SEMS + (pl.BlockSpec(memory_space=pl.ANY),),
        out_specs=(_HBM, _HBM),
        out_shape=(pltpu.HBM(sums_thru.shape, sums_thru.dtype), pltpu.HBM(sums_thru.shape, sums_thru.dtype)),
        input_output_aliases={0: 0, 1: 1},
        compiler_params=pltpu.CompilerParams(has_side_effects=pltpu.SideEffectType.DATAFLOW_SIDE_EFFECTING),
    )(sums_thru, land_thru, *sems, after)


def _adamw(name, slots, w, m, v, tr=256):
    unit_mid = w.ndim == 3 and w.shape[1] == 1 and w.shape[0] > 1
    R, Wd = (w.shape[0], w.shape[2]) if unit_mid else w.shape[-2:]
    depth_axis = w.ndim == 3 and not unit_mid
    if R % tr == 0:
        tc = Wd
    else:
        tr, tc = R, (256 if (Wd % 256 == 0 and R > 256) else Wd)
    at = (slice(None), 0, slice(None)) if unit_mid else Ellipsis

    def body(s_ref, w_ref, m_ref, v_ref, g_out, d_out, m_out, v_out):
        g = s_ref[0].astype(F32)
        for j in range(1, slots.shape[0]):
            g = g + s_ref[j].astype(F32)
        m_new = ADAM_B1 * m_ref[at] + (1.0 - ADAM_B1) * g
        v_new = ADAM_B2 * v_ref[at] + (1.0 - ADAM_B2) * jnp.square(g)
        m_hat = m_new / (1.0 - ADAM_B1 ** ADAM_STEP)
        v_hat = v_new / (1.0 - ADAM_B2 ** ADAM_STEP)
        g_out[at] = g
        d_out[at] = -ADAM_LR * (m_hat / (jnp.sqrt(v_hat) + ADAM_EPS) + ADAM_WD * w_ref[at])
        m_out[at] = m_new
        v_out[at] = v_new

    if unit_mid:
        row = pl.BlockSpec((tr, 1, tc), lambda i, j: (i, 0, j))
    elif depth_axis:
        row = pl.BlockSpec((None, tr, tc), lambda i, j: (0, i, j))
    else:
        row = pl.BlockSpec((tr, tc), lambda i, j: (i, j))
    return pl.pallas_call(
        body, name=name, grid=(R // tr, Wd // tc),
        in_specs=[pl.BlockSpec((slots.shape[0], tr, tc), lambda i, j: (0, i, j)), row, row, row],
        out_specs=[row] * 4, out_shape=[jax.ShapeDtypeStruct(w.shape, F32)] * 4,
        compiler_params=pltpu.CompilerParams(dimension_semantics=("parallel", "parallel"),
                                             vmem_limit_bytes=VMEM_LIMIT),
    )(slots, w, m, v)


PACK_W = 1024
PACKED = [(n, s) for n, s in REPLICATED if n != "sgu_w"]
_SMALL_SIZES = [int(np.prod(s)) for _, s in PACKED]
_SMALL_ROWS = _round_up(_round_up(sum(_SMALL_SIZES) + PACK_W, PACK_W) // PACK_W, 8)
_LOSS_AT = sum(_SMALL_SIZES)
W_IN_SHARD = P_TOTAL // N_DEV


def _pack_rows(parts, rows, dtype):
    flat = jnp.concatenate([p.reshape(-1).astype(dtype) for p in parts])
    return jnp.pad(flat, (0, rows * PACK_W - flat.shape[0])).reshape(rows, PACK_W)


def _w_in_groups_t(blocks):
    wt = blocks.reshape(P_TOTAL, D)
    o, c = 2 * D, 2 * D + 3 * D
    z = lambda r: jnp.zeros((r, D), wt.dtype)
    rw = jnp.concatenate([wt[o:c], wt[c:c + L_W], z(128 - L_W), wt[c + L_W:c + L_W + L_A], z(128 - L_A),
                          wt[c + L_W + L_A:o + C_B], z(256 - L_G)], axis=0)
    return wt[:o], rw, wt[o + C_B:]


def _w_in_grad_blocks(g_sgu_t, g_rw_t, g_gate_t):
    c = 3 * D
    full = jnp.concatenate([g_sgu_t, g_rw_t[:c], g_rw_t[c:c + L_W], g_rw_t[c + 128:c + 128 + L_A],
                            g_rw_t[c + 256:c + 256 + L_G], g_gate_t], axis=0)
    return full.reshape(N_DEV, W_IN_SHARD, D)


def _mesh_index():
    me = 4 * lax.axis_index("x") + 2 * lax.axis_index("y") + lax.axis_index("c")
    return me.astype(jnp.int32).reshape(1)


def _fill_slot(name, dst, src, idx, src_idx=None):
    R, Wd = dst.shape[1:]
    scalars = [idx] if src_idx is None else [idx, src_idx]
    if src_idx is None:
        src_spec = pl.BlockSpec((R, Wd), lambda i, *s: (0, 0))
    else:
        src_spec = pl.BlockSpec((None, R, Wd), lambda i, *s: (s[1][0], 0, 0))

    def body(*refs):
        src_ref, out_ref = refs[len(scalars) + 1], refs[len(scalars) + 2]
        out_ref[...] = src_ref[...]

    return pl.pallas_call(
        body, name=name,
        grid_spec=pltpu.PrefetchScalarGridSpec(
            num_scalar_prefetch=len(scalars), grid=(1,),
            in_specs=[pl.BlockSpec(memory_space=pl.ANY), src_spec],
            out_specs=pl.BlockSpec((None, R, Wd), lambda i, *s: (s[0][0], 0, 0))),
        out_shape=jax.ShapeDtypeStruct(dst.shape, dst.dtype),
        input_output_aliases={len(scalars): 0},
        compiler_params=pltpu.CompilerParams(vmem_limit_bytes=VMEM_LIMIT),
    )(*scalars, dst, src)


class TwoLevelGather:
    def __init__(self, bufs, skip_own=()):
        self.bufs, self.nb, self.skip_own = list(bufs), len(bufs), tuple(skip_own)
        self.any_specs = [pl.BlockSpec(memory_space=pl.ANY)] * self.nb
        self.out_shape = [jax.ShapeDtypeStruct((N_DEV,) + b.shape, b.dtype) for b in self.bufs]
        self.sem_shapes = [pltpu.SemaphoreType.DMA((7 * self.nb,)), pltpu.SemaphoreType.DMA((7 * self.nb,)),
                           pltpu.SemaphoreType.DMA((self.nb,))]

    def _copies(self, in_refs, out_refs, sems):
        send_sems, recv_sems, local_sems = sems
        nb = self.nb
        x, y, c = lax.axis_index("x"), lax.axis_index("y"), lax.axis_index("c")
        me, sibling = (x, y, c), (x, y, 1 - c)
        chips = [(1 - x, y), (x, 1 - y), (1 - x, 1 - y)]

        def slot(b, dev):
            return out_refs[b].at[4 * dev[0] + 2 * dev[1] + dev[2]]

        def copy(b, k, block, to, own=False):
            return pltpu.make_async_remote_copy(
                src_ref=in_refs[b] if own else slot(b, block), dst_ref=slot(b, block),
                send_sem=send_sems.at[7 * b + k], recv_sem=recv_sems.at[7 * b + k], device_id=to,
                device_id_type=pl.DeviceIdType.MESH)

        cp = {}
        cp["local"] = [pltpu.make_async_copy(in_refs[b], slot(b, me), local_sems.at[b]) for b in range(nb)
                       if b not in self.skip_own]
        cp["first"] = [copy(b, 0, me, sibling, own=True) for b in range(nb)]
        cp["first"] += [copy(b, 1 + j, me, (*chip, c), own=True) for j, chip in enumerate(chips) for b in range(nb)]
        cp["over_ici"] = [copy(b, 1 + j, (*chip, c), me) for j, chip in enumerate(chips) for b in range(nb)]
        cp["passed"] = [copy(b, 4 + j, (*chip, c), sibling) for j, chip in enumerate(chips) for b in range(nb)]
        cp["from_sibling"] = [copy(b, 0, sibling, me) for b in range(nb)]
        cp["from_sibling"] += [copy(b, 4 + j, (*chip, 1 - c), me) for j, chip in enumerate(chips) for b in range(nb)]
        return cp

    def start(self, in_refs, out_refs, sems):
        cp = self._copies(in_refs, out_refs, sems)
        for c in cp["first"] + cp["local"]:
            c.start()

    def forward(self, in_refs, out_refs, sems):
        cp = self._copies(in_refs, out_refs, sems)
        for arrived, onward in zip(cp["over_ici"], cp["passed"]):
            arrived.wait_recv()
            onward.start()

    def finish(self, in_refs, out_refs, sems):
        cp = self._copies(in_refs, out_refs, sems)
        for c in cp["from_sibling"]:
            c.wait_recv()
        for c in cp["first"] + cp["passed"]:
            c.wait_send()
        for c in cp["local"]:
            c.wait()

    def schedule(self, n_steps):
        return [(0, self.start), (max(n_steps - 3, 0), self.forward), (n_steps - 1, self.finish)]


def _all_gather_two_level(name, bufs, skip_own=()):
    ex = TwoLevelGather(bufs, skip_own)

    def body(*refs):
        args = refs[:ex.nb], refs[ex.nb:2 * ex.nb], refs[2 * ex.nb:]
        ex.start(*args)
        ex.forward(*args)
        ex.finish(*args)

    return pl.pallas_call(body, name=name, in_specs=ex.any_specs, out_specs=ex.any_specs, out_shape=ex.out_shape,
                          scratch_shapes=ex.sem_shapes)(*ex.bufs)


def _cols_from_blocks(blk):
    return jnp.transpose(blk, (1, 0, 2)).reshape(blk.shape[1], -1)


def _cols_to_blocks(g):
    r, c = g.shape
    return jnp.transpose(g.reshape(r, N_DEV, c // N_DEV), (1, 0, 2))


FIRST_WEIGHTS = ["w_in", "shift_b", "w_lora_w", "a_lora_w", "g_lora_w"]
LATE_WEIGHTS = ["w_proj_a", "w_proj_b", "w_out", "w_ffn1", "w_ffn2"]
SCAN_CARRIED = ["w_proj_a", "w_proj_b", "w_out"]
FFN_WEIGHTS = ["w_ffn1", "w_ffn2"]


def _late_weights(shards):
    ex = TwoLevelGather([shards[n].astype(BF16) for n in LATE_WEIGHTS])

    def finish(results):
        got = dict(zip(LATE_WEIGHTS, results))
        W = {n: got[n].reshape(-1, D) for n in ("w_proj_a", "w_proj_b", "w_out", "w_ffn2")}
        W["w_ffn1"] = got["w_ffn1"].reshape(N_DEV, D, -1)
        return W
    return ex, finish


def _gather_weights(shards):
    def payload(n):
        if n == "w_in":
            return jnp.transpose(shards[n][0]).astype(BF16)
        return shards[n] if n == "shift_b" else shards[n].astype(BF16)
    payloads = [payload(n) for n in FIRST_WEIGHTS]
    got = dict(zip(FIRST_WEIGHTS, _all_gather_two_level("weight_all_gather", payloads, skip_own=(0,))))
    got["w_in"] = _fill_slot("w_in_own_slot", got["w_in"], payloads[0], _mesh_index())
    W = {}
    W["w_sgu_t"], W["w_rw_t"], W["w_gate_t"] = _w_in_groups_t(got["w_in"])
    z = lambda r, c, dt: jnp.zeros((r, c), dt)
    W["w_lora"] = jnp.concatenate([_cols_from_blocks(got["w_lora_w"][:, 0]).astype(F32), z(128 - L_W, D, F32)], axis=0)
    W["a_lora"] = jnp.concatenate([_cols_from_blocks(got["a_lora_w"][:, 0]).astype(F32), z(128 - L_A, D, F32)], axis=0)
    W["g_lora"] = jnp.concatenate([_cols_from_blocks(got["g_lora_w"][:, 0]).astype(F32), z(256 - L_G, D, F32)], axis=0)
    sb = _cols_from_blocks(got["shift_b"][:, 0])
    W["sb"] = jnp.concatenate([sb[:, :3 * D], sb[:, 3 * D:3 * D + L_W], z(2, 128 - L_W, F32),
                               sb[:, 3 * D + L_W:3 * D + L_W + L_A], z(2, 128 - L_A, F32),
                               sb[:, 3 * D + L_W + L_A:], z(2, 256 - L_G, F32)], axis=1)
    return W


def _replicated_weights(rep):
    W = {n: rep[n] for n in ("g_mix", "sgu_ln_w", "sgu_ln_b", "w0", "a0", "k_k", "k_a", "r_k", "ln_x_w", "ln_x_b",
                             "g_ffn")}
    W["g_final"] = rep["g_final"].reshape(1, D)
    W["sgu_w"] = rep["sgu_w"][0]
    W["sgu_bt"] = jnp.transpose(rep["sgu_b"][0])
    return W


def _late_grad_blocks(G):
    return Exchange([G[n].reshape(N_DEV, -1, D) for n in SCAN_CARRIED]
                    + [G["sgu_w"].reshape(SGU_G * SGU_C, SGU_C).astype(GRAD_PAYLOAD)],
                    [False] * len(SCAN_CARRIED) + [True])


def _first_grad_blocks(G):
    sbg = G["sb"]
    c = 3 * D
    sb = jnp.concatenate([sbg[:, :c], sbg[:, c:c + L_W], sbg[:, c + 128:c + 128 + L_A],
                          sbg[:, c + 256:c + 256 + L_G]], axis=1)
    return {
        "shift_b": _cols_to_blocks(sb),
        "w_lora_w": _cols_to_blocks(G["w_lora"][:L_W]), "a_lora_w": _cols_to_blocks(G["a_lora"][:L_A]),
        "g_lora_w": _cols_to_blocks(G["g_lora"][:L_G]),
    }


def _replicated_grads(G):
    small = {n: G[n] for n in ("g_mix", "sgu_ln_w", "sgu_ln_b", "w0", "a0", "k_k", "k_a", "r_k", "ln_x_w", "ln_x_b",
                               "g_ffn", "g_final")}
    small["sgu_w"] = G["sgu_w"]
    small["sgu_b"] = jnp.transpose(G["sgu_bt"])
    return small


def kernel(x, g_mix, w_in, sgu_ln_w, sgu_ln_b, sgu_w, sgu_b, w_proj_a, shift_b, w_lora_w, w0, a_lora_w, a0, g_lora_w, k_k, k_a, r_k, ln_x_w, ln_x_b, w_proj_b, w_out, g_ffn, w_ffn1, w_ffn2, g_final, loss_target, m_g_mix, m_w_in, m_sgu_ln_w, m_sgu_ln_b, m_sgu_w, m_sgu_b, m_w_proj_a, m_shift_b, m_w_lora_w, m_w0, m_a_lora_w, m_a0, m_g_lora_w, m_k_k, m_k_a, m_r_k, m_ln_x_w, m_ln_x_b, m_w_proj_b, m_w_out, m_g_ffn, m_w_ffn1, m_w_ffn2, m_g_final, v_g_mix, v_w_in, v_sgu_ln_w, v_sgu_ln_b, v_sgu_w, v_sgu_b, v_w_proj_a, v_shift_b, v_w_lora_w, v_w0, v_a_lora_w, v_a0, v_g_lora_w, v_k_k, v_k_a, v_r_k, v_ln_x_w, v_ln_x_b, v_w_proj_b, v_w_out, v_g_ffn, v_w_ffn1, v_w_ffn2, v_g_final):
    env = dict(locals())
    weights = {n: env[n] for n in WEIGHT_ORDER}
    moms = {n: env["m_" + n] for n in WEIGHT_ORDER}
    vars_ = {n: env["v_" + n] for n in WEIGHT_ORDER}

    shards = {n: weights[n] for n, _, _ in SHARDED}
    W = _gather_weights(shards)
    W.update(_replicated_weights({n: weights[n] for n, _ in REPLICATED}))
    in_flight = {}

    def send_w_in_grads(G):
        blocks = _w_in_grad_blocks(G["w_sgu_t"], G["w_rw_t"], G["w_gate_t"])
        got = _pair_exchange("grad_pair_exchange", blocks)
        core = lax.axis_index("c").astype(jnp.int32).reshape(1)
        sums = _pair_sum("grad_pair_sum", blocks, got, core)
        in_flight["sems"], in_flight["sums"], in_flight["land"], token = _chip_exchange_start("grad_chip_start", sums)
        return token

    def send_ffn_grads(G):
        in_flight["ffn"] = _scatter_start("grad_ffn_start", [G["w_ffn1"], G["w_ffn2"].reshape(N_DEV, -1, D)])
        return in_flight["ffn"][3]

    loss_part, dx, G, late_slots = _local_step(x[0], loss_target[0], W, late_weights=_late_weights(shards),
                                               early_grads=_late_grad_blocks, w_in_grads_ready=send_w_in_grads,
                                               ffn_grads_ready=send_ffn_grads)

    slots = dict(zip(SCAN_CARRIED, late_slots))
    me = _mesh_index()
    sent, landed = _scatter_wait("grad_ffn_wait", *in_flight["ffn"][:3], after=dx)
    for i, n in enumerate(FFN_WEIGHTS):
        slots[n] = _fill_slot("grad_own_slot_" + n, landed[i], sent[i], me, src_idx=me)
    blocks = _first_grad_blocks(G)
    small = _replicated_grads(G)
    small_parts = [small[n] for n, _ in PACKED] + [jnp.full((PACK_W,), loss_part, F32)]
    rest = [n for n in FIRST_WEIGHTS if n != "w_in"]
    res = _exchange("grad_exchange", [blocks[n] for n in rest] + [_pack_rows(small_parts, _SMALL_ROWS, F32)],
                    [False] * len(rest) + [True])
    slots.update(zip(rest, res[:-1]))
    small_slots = res[-1]
    sums, chip_slots = _chip_exchange_wait("grad_chip_wait", in_flight["sems"], in_flight["sums"], in_flight["land"],
                                           after=small_slots)
    my_chip = (2 * lax.axis_index("x") + lax.axis_index("y")).astype(jnp.int32).reshape(1)
    slots["w_in"] = _fill_slot("grad_own_slot", chip_slots, sums, my_chip, src_idx=my_chip)

    outs = [dict(), dict(), dict(), dict()]
    for n, _, _ in SHARDED:
        if n == "w_in":
            res = _adamw("adamw_" + n, slots[n], *[jnp.transpose(t, (2, 0, 1)) for t in (weights[n], moms[n], vars_[n])])
            res = [jnp.transpose(t, (1, 2, 0)) for t in res]
        else:
            res = _adamw("adamw_" + n, slots[n], weights[n], moms[n], vars_[n])
        for k in range(4):
            outs[k][n] = res[k]

    sgu_shape = (SGU_G * SGU_C, SGU_C)
    res = _adamw("adamw_sgu_w", late_slots[len(SCAN_CARRIED)], *[t.reshape(sgu_shape) for t in
                                                                  (weights["sgu_w"], moms["sgu_w"], vars_["sgu_w"])])
    for k in range(4):
        outs[k]["sgu_w"] = res[k].reshape(weights["sgu_w"].shape)

    def packed(d):
        return _pack_rows([d[n] for n, _ in PACKED], _SMALL_ROWS, F32)
    small_out = _adamw("adamw_replicated", small_slots, packed(weights), packed(moms), packed(vars_))
    for k in range(4):
        flat = small_out[k].reshape(-1)
        off = 0
        for (n, s), size in zip(PACKED, _SMALL_SIZES):
            outs[k][n] = flat[off:off + size].reshape(s)
            off += size
    loss = small_out[0].reshape(-1)[_LOSS_AT]
    return (loss, dx[None], *[outs[0][n] for n in WEIGHT_ORDER], *[outs[1][n] for n in WEIGHT_ORDER],
            *[outs[2][n] for n in WEIGHT_ORDER], *[outs[3][n] for n in WEIGHT_ORDER])
```

```python
import functools
import numpy as np
import jax
import jax.numpy as jnp
from jax import lax
from jax.experimental import pallas as pl
from jax.experimental.pallas import tpu as pltpu

F32 = jnp.float32
BF16 = jnp.bfloat16

D = 1024
NH, HN = 16, 64
NP, PW = NH // 2, 2 * HN
SGU_G, SGU_C = 8, 128
L_W, L_A, L_G = 64, 64, 160
C_B = 3 * D + L_W + L_A + L_G
P_TOTAL = 2 * D + C_B + 2 * D
D_FF = 4 * D
RW_INT = 3 * D + 128 + 128 + 256
NORM_EPS, LN_EPS, GN_EPS = 1e-6, 1e-5, 64e-5
N_DEV = 8
LANES = 128
SCAN_C = 64
N_KEPT = 4
SOLVE_B = 16
SCAN_PRECISION = lax.Precision.HIGH
SCAN_OUT_PRECISION = lax.Precision.DEFAULT
GRAD_PAYLOAD = BF16
VMEM_LIMIT = 56 * 1024 * 1024
MATMUL_VMEM_BUDGET = 40 * 1024 * 1024
STEP_COST_BYTES = 512 * 1024
HBM_COST_RATIO = 3

ADAM_LR, ADAM_B1, ADAM_B2, ADAM_EPS, ADAM_WD, ADAM_STEP = 0.001, 0.9, 0.999, 1e-08, 0.01, 10

SHARDED = [
    ("w_in", (D, P_TOTAL), 1), ("w_proj_a", (D, D), 0), ("shift_b", (2, C_B), 1), ("w_lora_w", (L_W, D), 1),
    ("a_lora_w", (L_A, D), 1), ("g_lora_w", (L_G, D), 1), ("w_proj_b", (D, D), 0), ("w_out", (D, D), 0),
    ("w_ffn1", (D, D_FF), 1), ("w_ffn2", (D_FF, D), 0),
]
REPLICATED = [
    ("g_mix", (1, D)), ("sgu_ln_w", (1, D)), ("sgu_ln_b", (1, D)), ("sgu_w", (1, SGU_G, SGU_C, SGU_C)),
    ("sgu_b", (1, SGU_G, SGU_C)), ("w0", (1, D)), ("a0", (1, D)), ("k_k", (1, D)), ("k_a", (1, D)), ("r_k", (1, D)),
    ("ln_x_w", (1, D)), ("ln_x_b", (1, D)), ("g_ffn", (1, D)), ("g_final", (D,)),
]
WEIGHT_ORDER = ["g_mix", "w_in", "sgu_ln_w", "sgu_ln_b", "sgu_w", "sgu_b", "w_proj_a", "shift_b", "w_lora_w", "w0",
                "a_lora_w", "a0", "g_lora_w", "k_k", "k_a", "r_k", "ln_x_w", "ln_x_b", "w_proj_b", "w_out", "g_ffn",
                "w_ffn1", "w_ffn2", "g_final"]


def _shard_shape(shape, axis):
    s = list(shape)
    s[axis] //= N_DEV
    return tuple(s)


def _round_up(n, m):
    return (n + m - 1) // m * m


def _pick(n, target):
    if n <= target:
        return n
    best = None
    for t in range(LANES, target + 1, LANES):
        if n % t == 0:
            best = t
    assert best is not None, (n, target)
    return best


def _matmul(name, a, b, mode, out_dtype=F32, tm=2048, tn=1024, tk=2048, out_blocks=None, epilogue=None, extras=(),
            out_dtypes=(), after=None, whole_rows=False, out_widths=None):
    b_blocks = b.shape[0] if b.ndim == 3 else None
    bshape = b.shape if b.ndim == 2 else (b.shape[1], b.shape[0] * b.shape[2])
    if mode == "nn":
        (M, K), (K2, N) = a.shape, bshape
    elif mode == "nt":
        (M, K), (N, K2) = a.shape, bshape
    else:
        (K, M), (K2, N) = a.shape, bshape
    assert K == K2, (name, a.shape, b.shape)
    assert b_blocks is None or mode != "tn"
    assert out_blocks is None or mode == "tn"
    tn = min(tn, N // (out_blocks or 1), bshape[1] // b_blocks if (b_blocks and mode == "nn") else tn)
    tk = min(tk, bshape[1] // b_blocks if (b_blocks and mode == "nt") else tk)
    tm, tn, tk = _pick(M, tm), _pick(N, tn), _pick(K, tk)

    def vmem_bytes(tm, tk):
        tiles = tm * tk * a.dtype.itemsize + tk * tn * b.dtype.itemsize
        for i, dt in enumerate(out_dtypes if epilogue else (out_dtype,)):
            tiles += tm * (out_widths[i] if out_widths else tn) * jnp.dtype(dt).itemsize
        for x in extras:
            arr = x[0] if isinstance(x, tuple) else x
            tiles += (tm if arr.shape[0] > 1 else 1) * tn * arr.dtype.itemsize
        return 2 * tiles + (tm * tn * 4 if K // tk > 1 else 0)

    def cost(tm, tk):
        ni, nj, nk = M // tm, N // tn, K // tk
        steps = ni * nj * nk
        acc_passes = steps * tm * tn * 8 if nk > 1 else 0
        a_reads = M * K * a.dtype.itemsize * (nj if nk > 1 else 1)
        b_reads = K * N * b.dtype.itemsize * (ni if (nj > 1 or nk > 1) else 1)
        return (steps * STEP_COST_BYTES + acc_passes + vmem_bytes(tm, tk) // 2
                + HBM_COST_RATIO * (a_reads + b_reads))

    options = [(m, k) for m in ({M} if whole_rows else {_pick(M, max(t, LANES)) for t in (tm, tm // 2, tm // 4)})
               for k in {_pick(K, max(t, LANES)) for t in (tk, tk // 2, tk // 4)}
               if vmem_bytes(m, k) <= MATMUL_VMEM_BUDGET]
    tm, tk = min(options, key=lambda o: cost(*o))
    nk = K // tk
    dims = {"nn": (((1,), (0,)), ((), ())), "nt": (((1,), (1,)), ((), ())), "tn": (((0,), (0,)), ((), ()))}[mode]

    n_x, n_o = len(extras), len(out_dtypes) if epilogue else 1
    n_after = 0 if after is None else 1

    def body(a_ref, b_ref, *rest):
        x_refs, o_refs, acc = rest[:n_x], rest[n_x + n_after:n_x + n_after + n_o], rest[n_x + n_after + n_o:]
        part = lax.dot_general(a_ref[...].astype(BF16), b_ref[...].astype(BF16), dims, preferred_element_type=F32)

        def finish(res):
            outs = epilogue(res, *[r[...] for r in x_refs]) if epilogue else (res,)
            for r, v in zip(o_refs, outs):
                r[...] = v.astype(r.dtype)

        if nk == 1:
            finish(part)
            return
        acc_ref, k = acc[0], pl.program_id(2)

        @pl.when(k == 0)
        def _():
            acc_ref[...] = part

        @pl.when(k > 0)
        def _():
            acc_ref[...] += part

        @pl.when(k == nk - 1)
        def _():
            finish(acc_ref[...])

    a_spec = {"nn": pl.BlockSpec((tm, tk), lambda i, j, k: (i, k)), "nt": pl.BlockSpec((tm, tk), lambda i, j, k: (i, k)),
              "tn": pl.BlockSpec((tk, tm), lambda i, j, k: (k, i))}[mode]
    b_spec = {"nn": pl.BlockSpec((tk, tn), lambda i, j, k: (k, j)), "nt": pl.BlockSpec((tn, tk), lambda i, j, k: (j, k)),
              "tn": pl.BlockSpec((tk, tn), lambda i, j, k: (k, j))}[mode]
    if b_blocks and mode == "nn":
        per = b.shape[2] // tn
        b_spec = pl.BlockSpec((None, tk, tn), lambda i, j, k: (j // per, k, j % per))
    elif b_blocks:
        per = b.shape[2] // tk
        b_spec = pl.BlockSpec((None, tn, tk), lambda i, j, k: (k // per, j, k % per))
    out_spec = pl.BlockSpec((tm, tn), lambda i, j, k: (i, j))
    out_shape = jax.ShapeDtypeStruct((M, N), out_dtype)
    if out_blocks:
        per_o = N // out_blocks // tn
        out_spec = pl.BlockSpec((None, tm, tn), lambda i, j, k: (j // per_o, i, j % per_o))
        out_shape = jax.ShapeDtypeStruct((out_blocks, M, N // out_blocks), out_dtype)
    epi_widths = list(out_widths) if out_widths else [N] * n_o
    assert all(w == N for w in epi_widths) or N == tn
    epi_specs = [pl.BlockSpec((tm, tn if w == N else w), lambda i, j, k: (i, j)) for w in epi_widths]
    x_specs, x_args = [], []
    for x in extras:
        arr, off = x if isinstance(x, tuple) else (x, 0)
        if arr.shape[0] == 1:
            x_specs.append(pl.BlockSpec((1, tn), lambda i, j, k: (0, j)))
        else:
            x_specs.append(pl.BlockSpec((tm, tn), lambda i, j, k, off=off: (i, j + off)))
        x_args.append(arr)
    res = pl.pallas_call(
        body, name=name, grid=(M // tm, N // tn, nk),
        in_specs=[a_spec, b_spec] + x_specs + [pl.BlockSpec(memory_space=pl.ANY)] * n_after,
        out_specs=epi_specs if epilogue else out_spec,
        out_shape=[jax.ShapeDtypeStruct((M, w), dt) for w, dt in zip(epi_widths, out_dtypes)] if epilogue else out_shape,
        scratch_shapes=[pltpu.VMEM((tm, tn), F32)] if nk > 1 else [],
        compiler_params=pltpu.CompilerParams(dimension_semantics=("parallel", "parallel", "arbitrary"),
                                             vmem_limit_bytes=VMEM_LIMIT),
    )(a, b, *x_args, *([after] if n_after else []))
    return res


class Rows:
    def __init__(self, arr, width=None, cb=0):
        self.arr, self.width, self.cb = arr, (arr.shape[1] if width is None else width), cb


class Heads:
    def __init__(self, arr):
        self.arr = arr


class Halo:
    def __init__(self, arr, side):
        self.arr, self.side = arr, side


def _rows_call(name, fn, ins, consts, outs, accs=(), tm=256, with_pid=False):
    T = next(o.arr.shape[1] if isinstance(o, Heads) else o.arr.shape[0] for o in ins if not isinstance(o, Halo))
    tm = min(tm, T)
    n_tiles = T // tm
    n_in, n_c, n_out = len(ins), len(consts), len(outs)
    in_specs, args = [], []
    for o in ins:
        if isinstance(o, Rows):
            in_specs.append(pl.BlockSpec((tm, o.width), lambda i, cb=o.cb: (i, cb)))
        elif isinstance(o, Heads):
            in_specs.append(pl.BlockSpec((NP, tm, PW), lambda i: (0, i, 0)))
        else:
            w = o.arr.shape[1]
            if o.side < 0:
                in_specs.append(pl.BlockSpec((8, w), lambda i: (jnp.maximum(i * (tm // 8) - 1, 0), 0)))
            else:
                in_specs.append(pl.BlockSpec((8, w), lambda i: (jnp.minimum((i + 1) * (tm // 8), T // 8 - 1), 0)))
        args.append(o.arr)
    for c in consts:
        in_specs.append(pl.BlockSpec(c.shape, lambda i, nd=c.ndim: (0,) * nd))
        args.append(c)
    out_specs, out_shape = [], []
    for o in outs:
        if o[0] == "rows":
            out_specs.append(pl.BlockSpec((tm, o[1]), lambda i: (i, 0)))
            out_shape.append(jax.ShapeDtypeStruct((T, o[1]), o[2]))
        else:
            out_specs.append(pl.BlockSpec((NP, tm, PW), lambda i: (0, i, 0)))
            out_shape.append(jax.ShapeDtypeStruct((NP, T, PW), o[1]))
    for shape, dt in accs:
        out_specs.append(pl.BlockSpec(shape, lambda i, nd=len(shape): (0,) * nd))
        out_shape.append(jax.ShapeDtypeStruct(shape, dt))

    def body(*refs):
        i = pl.program_id(0)
        vals = []
        vals = [r[...] for r in refs[:n_in + n_c]]
        res = fn(i, n_tiles, *vals) if with_pid else fn(*vals)
        out_refs = refs[n_in + n_c:]
        for r, v in zip(out_refs[:n_out], res[:n_out]):
            r[...] = v.astype(r.dtype)
        if accs:
            @pl.when(i == 0)
            def _():
                for r in out_refs[n_out:]:
                    r[...] = jnp.zeros_like(r)

            for r, v in zip(out_refs[n_out:], res[n_out:]):
                r[...] += v.astype(r.dtype)

    res = pl.pallas_call(
        body, name=name, grid=(n_tiles,), in_specs=in_specs, out_specs=out_specs, out_shape=out_shape,
        compiler_params=pltpu.CompilerParams(dimension_semantics=("arbitrary",), vmem_limit_bytes=VMEM_LIMIT),
    )(*args)
    return res


def _rms(x, g):
    return x * lax.rsqrt(jnp.mean(x * x, axis=-1, keepdims=True) + NORM_EPS) * g


def _gelu(x):
    return 0.5 * x * (1.0 + lax.erf(x * 0.7071067811865476))


def _sigmoid(x):
    return 1.0 / (1.0 + jnp.exp(-x))


def _bdot(a, b):
    return jnp.dot(a.astype(BF16), b.astype(BF16), preferred_element_type=F32)


def _to_heads(x):
    return jnp.concatenate([x[:, p * PW:(p + 1) * PW][None] for p in range(NP)], axis=0)


def _from_heads(xp):
    return jnp.concatenate([xp[p] for p in range(NP)], axis=-1)


def _head_sum(xp):
    low = lax.broadcasted_iota(jnp.int32, xp.shape, xp.ndim - 1) < HN
    both = jnp.sum(xp, axis=-1, keepdims=True)
    first = jnp.sum(jnp.where(low, xp, 0.0), axis=-1, keepdims=True)
    return jnp.where(low, first, both - first)


def _split_pairs(xp):
    return jnp.concatenate([xp[:, :, :HN], xp[:, :, HN:]], axis=0)


def _join_pairs(xh):
    return jnp.concatenate([xh[:NP], xh[NP:]], axis=-1)


def _sgu_fn(p, ln_w, ln_b, sw, sbt):
    z = _gelu(p)
    u, v = z[:, :D], z[:, D:]
    mu = jnp.mean(v, axis=-1, keepdims=True)
    var = jnp.mean(jnp.square(v - mu), axis=-1, keepdims=True)
    vn = (v - mu) * lax.rsqrt(var + LN_EPS) * ln_w + ln_b
    ri = lax.broadcasted_iota(jnp.int32, (SGU_C, SGU_C), 0)
    ci = lax.broadcasted_iota(jnp.int32, (SGU_C, SGU_C), 1)
    mask = (ci <= ri).astype(F32)
    dg = D // SGU_G
    parts = []
    for g in range(SGU_G):
        parts.append(_bdot(sw[g] * mask, vn[:, g * dg:(g + 1) * dg]) + sbt[:, g:g + 1])
    return u * jnp.concatenate(parts, axis=-1)


def _pre_fn(qr, qk, qv, qxw, qxa, qxg, wl, w0, al, a0, gl, k_k, k_a):
    w = -jax.nn.softplus(-(w0 + _bdot(jnp.tanh(qxw), wl))) - 0.5
    lw = -jnp.exp(w)
    aa = _sigmoid(a0 + _bdot(qxa, al))
    g = _bdot(_sigmoid(qxg), gl)
    kk = _to_heads(qk * k_k)
    kk = kk / jnp.maximum(jnp.sqrt(_head_sum(kk * kk)), 1e-12)
    k2 = qk * (1.0 + (aa - 1.0) * k_a)
    return _to_heads(qr), _to_heads(lw), _to_heads(k2), _to_heads(qv), kk, _to_heads(aa), g


def _post_fn(o, r, k2, v, g, ln_w, ln_b, r_k):
    mu = _head_sum(o) * (1.0 / HN)
    d = o - mu
    var = _head_sum(d * d) * (1.0 / HN)
    on = d * lax.rsqrt(var + GN_EPS) * ln_w + ln_b
    bonus = _head_sum(r * k2 * r_k) * v
    return _from_heads(on + bonus) * g


def _gate_fn(pg, ya, yb):
    return _sigmoid(pg[:, :D]) * ya + _sigmoid(pg[:, D:]) * yb


def _bmm(x, y, cx, cy, out_path=False):
    return lax.dot_general(x, y, (((cx,), (cy,)), ((0,), (0,))),
                           precision=SCAN_OUT_PRECISION if out_path else SCAN_PRECISION, preferred_element_type=F32)


def _unit_lower_inverse(M):
    C = M.shape[1]
    ti = lax.broadcasted_iota(jnp.int32, (C, C), 0)
    tj = lax.broadcasted_iota(jnp.int32, (C, C), 1)
    eye = (ti == tj).astype(F32)
    same = lambda b: (ti // b == tj // b).astype(F32)
    X = -(M * same(SOLVE_B))
    inv = eye + X
    span = 1
    while 2 * span < SOLVE_B:
        X = _bmm(X, X, 2, 1)
        inv = inv + _bmm(inv, X, 2, 1)
        span *= 2
    b = SOLVE_B
    while b < C:
        low = M * (same(2 * b) - same(b))
        inv = inv - _bmm(_bmm(inv, low, 2, 1, out_path=True), inv, 2, 1, out_path=True)
        b *= 2
    return inv


@jax.custom_vjp
def _unit_lower_solve(inv, M, y):
    return _bmm(inv, y, 2, 1)


def _unit_lower_solve_fwd(inv, M, y):
    u = _bmm(inv, y, 2, 1)
    return u, (inv, u)


def _unit_lower_solve_bwd(res, du):
    inv, u = res
    dy = _bmm(inv, du, 1, 1)
    return jnp.zeros_like(inv), -_bmm(dy, u, 2, 2), dy


_unit_lower_solve.defvjp(_unit_lower_solve_fwd, _unit_lower_solve_bwd)


@jax.custom_vjp
def _unit_lower_solved(inv, u, M, y):
    return u


_unit_lower_solved.defvjp(lambda inv, u, M, y: (u, (inv, u)),
                          lambda res, du: (jnp.zeros_like(res[0]), jnp.zeros_like(res[1]))
                          + _unit_lower_solve_bwd(res, du)[1:])


@functools.partial(jax.custom_vjp, nondiff_argnums=(0,))
def _kept(fn, value, *args):
    return value


def _kept_fwd(fn, value, *args):
    return value, args


def _kept_bwd(fn, args, d):
    _, vjp = jax.vjp(fn, *args)
    return (jnp.zeros_like(d),) + tuple(vjp(d))


_kept.defvjp(_kept_fwd, _kept_bwd)


def _sum_over_time(x, reverse):
    C = x.shape[1]
    ti = lax.broadcasted_iota(jnp.int32, (C, C), 0)
    tj = lax.broadcasted_iota(jnp.int32, (C, C), 1)
    ones = jnp.broadcast_to(((tj >= ti) if reverse else (tj <= ti)).astype(BF16), (x.shape[0], C, C))
    hi = x.astype(BF16)
    r1 = x - hi.astype(F32)
    mid = r1.astype(BF16)
    lo = (r1 - mid.astype(F32)).astype(BF16)
    dn = (((2,), (1,)), ((0,), (0,)))
    return sum(lax.dot_general(ones, p, dn, preferred_element_type=F32) for p in (lo, mid, hi))


@jax.custom_vjp
def _time_cumsum(lw):
    return _sum_over_time(lw, reverse=False)


_time_cumsum.defvjp(lambda lw: (_sum_over_time(lw, reverse=False), None),
                    lambda _, d: (_sum_over_time(d, reverse=True),))


def _chunk_fn(S0, r, lw, k, v, kk, a, kept=None):
    C = SCAN_C
    bmm = _bmm
    ti = lax.broadcasted_iota(jnp.int32, (C, C), 0)
    tj = lax.broadcasted_iota(jnp.int32, (C, C), 1)
    incl2 = jnp.concatenate([(tj <= ti).astype(F32)] * 2, axis=1)
    strict = (tj < ti).astype(F32)
    n_mask = jnp.concatenate([jnp.zeros((C, C), F32), strict], axis=1)

    def known(name, fn, *args):
        return fn(*args) if kept is None else _kept(fn, kept[name], *args)

    cum = known("cum", _time_cumsum, lw)
    g_in, g_ex, g_inv = jnp.exp(cum), jnp.exp(cum - lw), jnp.exp(-cum)
    kkt, rt = kk * g_ex, r * g_in
    bk = jnp.concatenate([kk * a * g_inv, k * g_inv], axis=1)
    A = known("A", lambda x, y: bmm(x, y, 2, 2), kkt, bk)
    M = A[:, :, :C] * strict
    zv = jnp.concatenate([jnp.zeros_like(v), v], axis=1)
    s0_side = bmm(jnp.concatenate([kkt, rt], axis=1), S0, 2, 2, out_path=True)
    rhs = s0_side[:, :C] + bmm(A * n_mask, zv, 2, 1, out_path=True)
    if kept is None:
        inv = lax.stop_gradient(_unit_lower_inverse(M))
        y = _unit_lower_solve(inv, M, rhs)
    else:
        inv = kept["inv"]
        y = _unit_lower_solved(inv, kept["y"], M, rhs)
    z = jnp.concatenate([-y, v], axis=1)
    attn = known("attn", lambda x, y: bmm(x, y, 2, 2) * incl2, rt, bk)
    O = s0_side[:, C:] + bmm(attn, z, 2, 1, out_path=True)
    g_end = g_in[:, C - 1:C, :]
    S1 = S0 * g_end + bmm(z, bk * g_end, 1, 1, out_path=True)
    return O, S1, dict(cum=cum, A=A, attn=attn, y=y, inv=inv)


def _scan_fwd(r, lw, k, v, kk, a, ex=None, tb=256):
    assert SCAN_C == HN and 2 * SCAN_C == PW
    T = r.shape[1]
    tb = min(tb, T)
    n_chunks = tb // SCAN_C
    nb = T // tb
    nx = ex.nb if ex else 0

    def body(*refs):
        r_ref, lw_ref, k_ref, v_ref, kk_ref, a_ref = refs[:6]
        x_in, (o_ref, s0_ref), x_out = refs[6:6 + nx], refs[6 + nx:8 + nx], refs[8 + nx:8 + 2 * nx]
        s_ref, sems = refs[8 + 2 * nx], refs[9 + 2 * nx:]

        plan = ex.schedule(nb) if ex else []

        @pl.when(pl.program_id(0) == 0)
        def _():
            s_ref[...] = jnp.zeros_like(s_ref)
            for at, action in plan[:1]:
                action(x_in, x_out, sems)

        def step(c, carry):
            sl = pl.ds(pl.multiple_of(c * SCAN_C, SCAN_C), SCAN_C)
            S0 = s_ref[...]
            O, S1, keep = _chunk_fn(S0, *[_split_pairs(ref[:, sl, :])
                                          for ref in (r_ref, lw_ref, k_ref, v_ref, kk_ref, a_ref)])
            o_ref[:, sl, :] = _join_pairs(O)
            s0_ref[c, 0] = jnp.concatenate([S0, keep["inv"]], axis=-1)
            s0_ref[c, 1] = jnp.concatenate([keep["cum"], keep["y"]], axis=-1)
            s0_ref[c, 2] = keep["A"]
            s0_ref[c, 3] = keep["attn"]
            s_ref[...] = S1
            return carry

        lax.fori_loop(0, n_chunks, step, 0)

        for at, action in plan[1:]:
            pl.when(pl.program_id(0) == at)(functools.partial(action, x_in, x_out, sems))

    hm = pl.BlockSpec((NP, tb, PW), lambda i: (0, i, 0))
    res = pl.pallas_call(
        body, name="rwkv_scan_fwd", grid=(nb,), in_specs=[hm] * 6 + (ex.any_specs if ex else []),
        out_specs=[hm, pl.BlockSpec((n_chunks, N_KEPT, NH, HN, PW), lambda i: (i, 0, 0, 0, 0))]
        + (ex.any_specs if ex else []),
        out_shape=[jax.ShapeDtypeStruct((NP, T, PW), F32),
                   jax.ShapeDtypeStruct((T // SCAN_C, N_KEPT, NH, HN, PW), F32)]
        + (ex.out_shape if ex else []),
        scratch_shapes=[pltpu.VMEM((NH, HN, HN), F32)] + (ex.sem_shapes if ex else []),
        compiler_params=pltpu.CompilerParams(dimension_semantics=("arbitrary",), vmem_limit_bytes=VMEM_LIMIT),
    )(r, lw, k, v, kk, a, *(ex.bufs if ex else []))
    return res[0], res[1], list(res[2:])


def _scan_bwd(r, lw, k, v, kk, a, s0s, do, ex=None, tb=128):
    T = r.shape[1]
    tb = min(tb, T)
    n_chunks = tb // SCAN_C
    nb = T // tb
    nx = ex.nb if ex else 0

    def body(*refs):
        r_ref, lw_ref, k_ref, v_ref, kk_ref, a_ref, s0_ref, do_ref = refs[:8]
        x_in, (dr, dlw, dk, dv, dkk, da), x_out = refs[8:8 + nx], refs[8 + nx:14 + nx], refs[14 + nx:14 + 2 * nx]
        ds_ref, sems = refs[14 + 2 * nx], refs[15 + 2 * nx:]

        plan = ex.schedule(nb) if ex else []

        @pl.when(pl.program_id(0) == 0)
        def _():
            ds_ref[...] = jnp.zeros_like(ds_ref)
            for at, action in plan[:1]:
                action(x_in, x_out, sems)

        def step(j, carry):
            c = n_chunks - 1 - j
            sl = pl.ds(pl.multiple_of(c * SCAN_C, SCAN_C), SCAN_C)
            s0_inv, cum_y = s0_ref[c, 0], s0_ref[c, 1]
            kept = dict(inv=s0_inv[:, :, HN:], cum=cum_y[:, :, :HN], y=cum_y[:, :, HN:], A=s0_ref[c, 2],
                        attn=s0_ref[c, 3])
            _, vjp = jax.vjp(lambda *t: _chunk_fn(*t, kept=kept)[:2], s0_inv[:, :, :HN],
                             *[_split_pairs(ref[:, sl, :]) for ref in (r_ref, lw_ref, k_ref, v_ref, kk_ref, a_ref)])
            g = vjp((_split_pairs(do_ref[:, sl, :]), ds_ref[...]))
            ds_ref[...] = g[0]
            for ref, val in zip((dr, dlw, dk, dv, dkk, da), g[1:]):
                ref[:, sl, :] = _join_pairs(val)
            return carry

        lax.fori_loop(0, n_chunks, step, 0)

        for at, action in plan[1:]:
            pl.when(pl.program_id(0) == at)(functools.partial(action, x_in, x_out, sems))

    hm = pl.BlockSpec((NP, tb, PW), lambda i: (0, nb - 1 - i, 0))
    res = pl.pallas_call(
        body, name="rwkv_scan_bwd", grid=(nb,),
        in_specs=[hm] * 6 + [pl.BlockSpec((n_chunks, N_KEPT, NH, HN, PW), lambda i: (nb - 1 - i, 0, 0, 0, 0)), hm]
        + (ex.any_specs if ex else []),
        out_specs=[hm] * 6 + (ex.any_specs if ex else []),
        out_shape=[jax.ShapeDtypeStruct((NP, T, PW), F32)] * 6 + (ex.out_shape if ex else []),
        scratch_shapes=[pltpu.VMEM((NH, HN, HN), F32)] + (ex.sem_shapes if ex else []),
        compiler_params=pltpu.CompilerParams(dimension_semantics=("arbitrary",), vmem_limit_bytes=VMEM_LIMIT),
    )(r, lw, k, v, kk, a, s0s, do, *(ex.bufs if ex else []))
    return list(res[:6]), list(res[6:])


def _shift_down(i, p, prev8):
    first = jnp.where(i > 0, prev8[7:8, :], 0.0)
    row = lax.broadcasted_iota(jnp.int32, p.shape, 0)
    return jnp.where(row == 0, first, pltpu.roll(p, 1, axis=0))


def _mix_bwd(dq, p, sb, tm=256):
    def fn(i, n, dq, next8, p, prev8, sb):
        ps = _shift_down(i, p, prev8)
        d1 = dq * sb[1:2]
        last = jnp.where(i < n - 1, next8[0:1, :] * sb[1:2], 0.0)
        row = lax.broadcasted_iota(jnp.int32, dq.shape, 0)
        up = jnp.where(row == dq.shape[0] - 1, last, pltpu.roll(d1, dq.shape[0] - 1, axis=0))
        return (dq * sb[0:1] + up, jnp.sum(dq * p, axis=0, keepdims=True), jnp.sum(dq * ps, axis=0, keepdims=True))
    w = p.shape[1]
    return _rows_call("shift_mix_bwd", fn, [Rows(dq), Halo(dq, +1), Rows(p), Halo(p, -1)], [sb], [("rows", w, BF16)],
                      accs=[((1, w), F32), ((1, w), F32)], tm=tm, with_pid=True)


def _local_step(x, target, W, late_weights=None, early_grads=None, w_in_grads_ready=None, ffn_grads_ready=None):
    G = {}
    a = _rows_call("norm_mix_fwd", lambda x, g: (_rms(x, g),), [Rows(x)], [W["g_mix"]], [("rows", D, BF16)])[0]
    p_sgu = _matmul("proj_sgu", a, W["w_sgu_t"], "nt")
    def token_shift(p, sb0, sb1):
        row = lax.broadcasted_iota(jnp.int32, p.shape, 0)
        return p, p * sb0 + jnp.where(row == 0, 0.0, pltpu.roll(p, 1, axis=0)) * sb1
    p_rw, q = _matmul("proj_rwkv", a, W["w_rw_t"], "nt", tn=512, whole_rows=True, epilogue=token_shift,
                      extras=[W["sb"][0:1], W["sb"][1:2]], out_dtypes=(F32, F32))
    p_gate = _matmul("proj_gate", a, W["w_gate_t"], "nt")

    sgu_consts = [W["sgu_ln_w"], W["sgu_ln_b"], W["sgu_w"], W["sgu_bt"]]
    s = _rows_call("sgu_fwd", lambda *t: (_sgu_fn(*t),), [Rows(p_sgu)], sgu_consts, [("rows", D, BF16)], tm=SGU_C)[0]

    q_ins = [Rows(q, D, 0), Rows(q, D, 1), Rows(q, D, 2), Rows(q, 128, 24), Rows(q, 128, 25), Rows(q, 256, 13)]
    pre_consts = [W["w_lora"], W["w0"], W["a_lora"], W["a0"], W["g_lora"], W["k_k"], W["k_a"]]
    r_h, lw_h, k_h, v_h, kk_h, a_h, g_gate = _rows_call(
        "rwkv_pre_fwd", _pre_fn, q_ins, pre_consts, [("heads", F32)] * 6 + [("rows", D, F32)], tm=128)
    o_h, s0s, got = _scan_fwd(r_h, lw_h, k_h, v_h, kk_h, a_h, ex=late_weights[0] if late_weights else None)
    if late_weights:
        W = {**W, **late_weights[1](got)}
    y_a = _matmul("proj_a", s, W["w_proj_a"], "nn")
    post_ins = [Heads(o_h), Heads(r_h), Heads(k_h), Heads(v_h), Rows(g_gate)]
    post_consts = [W[n].reshape(NP, 1, PW) for n in ("ln_x_w", "ln_x_b", "r_k")]
    z_b = _rows_call("rwkv_post_fwd", lambda *t: (_post_fn(*t),), post_ins, post_consts, [("rows", D, BF16)], tm=128)[0]
    y_b, mixed = _matmul("proj_b", z_b, W["w_proj_b"], "nn", extras=[(p_gate, 0), (p_gate, 1), y_a],
                         epilogue=lambda yb, ga, gb, ya: (yb, _sigmoid(ga) * ya + _sigmoid(gb) * yb),
                         out_dtypes=(F32, BF16))

    def res1(mo, x, g):
        h1 = x + mo
        return h1, _rms(h1, g)
    h1, f = _matmul("proj_out", mixed, W["w_out"], "nn", extras=[x, W["g_ffn"]], epilogue=res1,
                    out_dtypes=(F32, BF16))

    def relu_sq(u):
        r = jnp.maximum(u, 0.0)
        return r, r * r
    r1, act = _matmul("ffn_up", f, W["w_ffn1"], "nn", epilogue=relu_sq, out_dtypes=(BF16, BF16))
    ff = _matmul("ffn_down", act, W["w_ffn2"], "nn")

    def head(h1, ff, tgt, g):
        def f_(h1, ff, g):
            y = _rms(h1 + ff, g)
            return 0.5 * jnp.sum(jnp.mean(jnp.square(y - tgt), axis=-1))
        loss, (dh2, _, dg) = jax.value_and_grad(f_, argnums=(0, 1, 2))(h1, ff, g)
        return dh2, jnp.full((8, LANES), loss, F32), dg
    dh2, loss_acc, G["g_final"] = _rows_call("loss_head", head, [Rows(h1), Rows(ff), Rows(target)], [W["g_final"]],
                                             [("rows", D, F32)], accs=[((8, LANES), F32), ((1, D), F32)])

    d_u1 = _matmul("ffn_down_dx", dh2, W["w_ffn2"], "nt", extras=[r1], out_dtypes=(BF16,),
                   epilogue=lambda d_act, r: (d_act * 2.0 * r.astype(F32),))[0]
    G["w_ffn2"] = _matmul("ffn_down_dw", act, dh2, "tn", out_dtype=GRAD_PAYLOAD)
    d_f = _matmul("ffn_up_dx", d_u1, W["w_ffn1"], "nt")
    G["w_ffn1"] = _matmul("ffn_up_dw", f, d_u1, "tn", out_blocks=N_DEV, out_dtype=GRAD_PAYLOAD)

    def res1_bwd(h1, d_f, dh2, g):
        _, vjp = jax.vjp(_rms, h1, g)
        dh, dg = vjp(d_f)
        return dh2 + dh, dg
    dh1, G["g_ffn"] = _rows_call("residual_norm_bwd", res1_bwd, [Rows(h1), Rows(d_f), Rows(dh2)],
                                 [W["g_ffn"]], [("rows", D, F32)], accs=[((1, D), F32)])
    ffn_token = ffn_grads_ready(G) if ffn_grads_ready else None
    def gate_bwd(d_mixed, ga, gb, ya, yb):
        _, vjp = jax.vjp(_gate_fn, jnp.concatenate([ga, gb], axis=-1), ya, yb)
        return vjp(d_mixed)
    d_gate, d_ya, d_yb = _matmul("proj_out_dx", dh1, W["w_out"], "nt", after=ffn_token, epilogue=gate_bwd,
                                 extras=[(p_gate, 0), (p_gate, 1), y_a, y_b], out_dtypes=(BF16, BF16, BF16),
                                 out_widths=(2 * D, D, D))
    G["w_out"] = _matmul("proj_out_dw", mixed, dh1, "tn", out_dtype=GRAD_PAYLOAD)

    d_s = _matmul("proj_a_dx", d_ya, W["w_proj_a"], "nt")
    G["w_proj_a"] = _matmul("proj_a_dw", s, d_ya, "tn", out_dtype=GRAD_PAYLOAD)

    def sgu_bwd(p, ds, *c):
        _, vjp = jax.vjp(_sgu_fn, p, *c)
        return vjp(ds)
    d_p_sgu, G["sgu_ln_w"], G["sgu_ln_b"], G["sgu_w"], G["sgu_bt"] = _rows_call(
        "sgu_bwd", sgu_bwd, [Rows(p_sgu), Rows(d_s)], sgu_consts, [("rows", 2 * D, BF16)],
        accs=[((1, D), F32), ((1, D), F32), ((SGU_G, SGU_C, SGU_C), F32), ((SGU_C, SGU_G), F32)], tm=SGU_C)

    d_zb = _matmul("proj_b_dx", d_yb, W["w_proj_b"], "nt")
    G["w_proj_b"] = _matmul("proj_b_dw", z_b, d_yb, "tn", out_dtype=GRAD_PAYLOAD)

    def post_bwd(o, r, k2, v, g, dz, *c):
        _, vjp = jax.vjp(_post_fn, o, r, k2, v, g, *c)
        return vjp(dz)
    do_h, dr1, dk1, dv1, d_g, g_lnw, g_lnb, g_rk = _rows_call(
        "rwkv_post_bwd", post_bwd, post_ins + [Rows(d_zb)], post_consts, [("heads", F32)] * 4 + [("rows", D, F32)],
        accs=[((NP, 1, PW), F32)] * 3, tm=128)
    G["ln_x_w"], G["ln_x_b"], G["r_k"] = (t.reshape(1, D) for t in (g_lnw, g_lnb, g_rk))
    (dr2, dlw, dk2, dv2, dkk, daa), early = _scan_bwd(r_h, lw_h, k_h, v_h, kk_h, a_h, s0s, do_h,
                                                      ex=early_grads(G) if early_grads else None)

    def pre_bwd(qr, qk, qv, qxw, qxa, qxg, dr1, dr2, dlw, dk1, dk2, dv1, dv2, dkk, daa, dg, *c):
        _, vjp = jax.vjp(_pre_fn, qr, qk, qv, qxw, qxa, qxg, *c)
        g = vjp((dr1 + dr2, dlw, dk1 + dk2, dv1 + dv2, dkk, daa, dg))
        dq = jnp.concatenate(g[:6], axis=-1)
        return (dq,) + tuple(g[6:])
    pre_b_ins = q_ins + [Heads(dr1), Heads(dr2), Heads(dlw), Heads(dk1), Heads(dk2), Heads(dv1), Heads(dv2),
                         Heads(dkk), Heads(daa), Rows(d_g)]
    d_q, G["w_lora"], G["w0"], G["a_lora"], G["a0"], G["g_lora"], G["k_k"], G["k_a"] = _rows_call(
        "rwkv_pre_bwd", pre_bwd, pre_b_ins, pre_consts, [("rows", RW_INT, F32)],
        accs=[((128, D), F32), ((1, D), F32), ((128, D), F32), ((1, D), F32), ((256, D), F32), ((1, D), F32),
              ((1, D), F32)], tm=128)
    d_p_rw, dsb0, dsb1 = _mix_bwd(d_q, p_rw, W["sb"])
    G["sb"] = jnp.concatenate([dsb0, dsb1], axis=0)

    G["w_sgu_t"] = _matmul("proj_sgu_dw", d_p_sgu, a, "tn", out_dtype=GRAD_PAYLOAD)
    G["w_rw_t"] = _matmul("proj_rwkv_dw", d_p_rw, a, "tn", out_dtype=GRAD_PAYLOAD)
    G["w_gate_t"] = _matmul("proj_gate_dw", d_gate, a, "tn", out_dtype=GRAD_PAYLOAD)
    token = w_in_grads_ready(G) if w_in_grads_ready else None
    da1 = _matmul("proj_sgu_dx", d_p_sgu, W["w_sgu_t"], "nn", after=token)
    da2 = _matmul("proj_rwkv_dx", d_p_rw, W["w_rw_t"], "nn", after=token)
    da3 = _matmul("proj_gate_dx", d_gate, W["w_gate_t"], "nn", after=token)

    def norm1_bwd(x, da1, da2, da3, dh1, g):
        _, vjp = jax.vjp(_rms, x, g)
        dx, dg = vjp(da1 + da2 + da3)
        return dh1 + dx, dg
    dx, G["g_mix"] = _rows_call("norm_mix_bwd", norm1_bwd, [Rows(x), Rows(da1), Rows(da2), Rows(da3), Rows(dh1)],
                                [W["g_mix"]], [("rows", D, F32)], accs=[((1, D), F32)])
    return loss_acc[0, 0], dx, G, early


class Exchange:
    def __init__(self, bufs, gathers):
        self.bufs, self.gathers, self.nb = list(bufs), list(gathers), len(bufs)
        self.any_specs = [pl.BlockSpec(memory_space=pl.ANY)] * self.nb
        self.out_shape = [jax.ShapeDtypeStruct((N_DEV,) + (b.shape if g else b.shape[1:]), b.dtype)
                          for b, g in zip(self.bufs, self.gathers)]
        n = (N_DEV - 1) * self.nb
        self.sem_shapes = [pltpu.SemaphoreType.DMA((n,)), pltpu.SemaphoreType.DMA((n,)),
                           pltpu.SemaphoreType.DMA((self.nb,))]

    def _copies(self, in_refs, out_refs, sems):
        send_sems, recv_sems, local_sems = sems
        x, y, c = lax.axis_index("x"), lax.axis_index("y"), lax.axis_index("c")
        me = 4 * x + 2 * y + c

        def src(b, dest):
            return in_refs[b] if self.gathers[b] else in_refs[b].at[dest]

        local = [pltpu.make_async_copy(src(b, me), out_refs[b].at[me], local_sems.at[b]) for b in range(self.nb)]
        sends, recvs = [], []
        for kbits in range(1, N_DEV):
            px = 1 - x if kbits & 4 else x
            py = 1 - y if kbits & 2 else y
            pc = 1 - c if kbits & 1 else c
            peer = 4 * px + 2 * py + pc
            for b in range(self.nb):
                s = (kbits - 1) * self.nb + b
                sends.append(pltpu.make_async_remote_copy(
                    src_ref=src(b, peer), dst_ref=out_refs[b].at[me], send_sem=send_sems.at[s],
                    recv_sem=recv_sems.at[s], device_id=(px, py, pc), device_id_type=pl.DeviceIdType.MESH))
                recvs.append(pltpu.make_async_remote_copy(
                    src_ref=src(b, peer), dst_ref=out_refs[b].at[peer], send_sem=send_sems.at[s],
                    recv_sem=recv_sems.at[s], device_id=(px, py, pc), device_id_type=pl.DeviceIdType.MESH))
        return local, sends, recvs

    def start(self, in_refs, out_refs, sems):
        local, sends, _ = self._copies(in_refs, out_refs, sems)
        for cp in sends + local:
            cp.start()

    def wait(self, in_refs, out_refs, sems):
        local, sends, recvs = self._copies(in_refs, out_refs, sems)
        for cp in recvs:
            cp.wait_recv()
        for cp in sends:
            cp.wait_send()
        for cp in local:
            cp.wait()

    def schedule(self, n_steps):
        return [(0, self.start), (n_steps - 1, self.wait)]


def _exchange(name, bufs, gather):
    ex = Exchange(bufs, gather if isinstance(gather, (list, tuple)) else [gather] * len(bufs))

    def body(*refs):
        in_refs, out_refs, sems = refs[:ex.nb], refs[ex.nb:2 * ex.nb], refs[2 * ex.nb:]
        ex.start(in_refs, out_refs, sems)
        ex.wait(in_refs, out_refs, sems)

    return pl.pallas_call(body, name=name, in_specs=ex.any_specs, out_specs=ex.any_specs, out_shape=ex.out_shape,
                          scratch_shapes=ex.sem_shapes)(*ex.bufs)


N_CHIP = 4


def _pair_exchange(name, blocks):
    def body(b_ref, got_ref, send_sems, recv_sems):
        x, y, c = lax.axis_index("x"), lax.axis_index("y"), lax.axis_index("c")
        copies = [pltpu.make_async_remote_copy(
            src_ref=b_ref.at[2 * q + 1 - c], dst_ref=got_ref.at[q], send_sem=send_sems.at[q],
            recv_sem=recv_sems.at[q], device_id=(x, y, 1 - c), device_id_type=pl.DeviceIdType.MESH)
            for q in range(N_CHIP)]
        for cp in copies:
            cp.start()
        for cp in copies:
            cp.wait_recv()
        for cp in copies:
            cp.wait_send()

    any_spec = pl.BlockSpec(memory_space=pl.ANY)
    return pl.pallas_call(
        body, name=name, in_specs=[any_spec], out_specs=any_spec,
        out_shape=jax.ShapeDtypeStruct((N_CHIP,) + blocks.shape[1:], blocks.dtype),
        scratch_shapes=[pltpu.SemaphoreType.DMA((N_CHIP,))] * 2,
    )(blocks)


def _pair_sum(name, blocks, got, core):
    _, R, Wd = blocks.shape

    def body(c_ref, a_ref, b_ref, o_ref):
        o_ref[...] = (a_ref[...].astype(F32) + b_ref[...].astype(F32)).astype(o_ref.dtype)

    return pl.pallas_call(
        body, name=name,
        grid_spec=pltpu.PrefetchScalarGridSpec(
            num_scalar_prefetch=1, grid=(N_CHIP,),
            in_specs=[pl.BlockSpec((None, R, Wd), lambda q, c_ref: (2 * q + c_ref[0], 0, 0)),
                      pl.BlockSpec((None, R, Wd), lambda q, c_ref: (q, 0, 0))],
            out_specs=pl.BlockSpec((None, R, Wd), lambda q, c_ref: (q, 0, 0))),
        out_shape=jax.ShapeDtypeStruct(got.shape, blocks.dtype),
        compiler_params=pltpu.CompilerParams(dimension_semantics=("parallel",), vmem_limit_bytes=VMEM_LIMIT),
    )(core, blocks, got)


def _chip_copies(s_ref, land_ref, send_sems, recv_sems):
    x, y, c = lax.axis_index("x"), lax.axis_index("y"), lax.axis_index("c")
    my_q = 2 * x + y
    sends, recvs = [], []
    for kbits in range(1, N_CHIP):
        px = 1 - x if kbits & 2 else x
        py = 1 - y if kbits & 1 else y
        peer_q = 2 * px + py
        sends.append(pltpu.make_async_remote_copy(
            src_ref=s_ref.at[peer_q], dst_ref=land_ref.at[my_q], send_sem=send_sems[kbits - 1],
            recv_sem=recv_sems[kbits - 1], device_id=(px, py, c), device_id_type=pl.DeviceIdType.MESH))
        recvs.append(pltpu.make_async_remote_copy(
            src_ref=s_ref.at[peer_q], dst_ref=land_ref.at[peer_q], send_sem=send_sems[kbits - 1],
            recv_sem=recv_sems[kbits - 1], device_id=(px, py, c), device_id_type=pl.DeviceIdType.MESH))
    return sends, recvs


_HBM = pl.BlockSpec(memory_space=pltpu.HBM)
_SEM = pl.BlockSpec(memory_space=pltpu.SEMAPHORE)
N_CHIP_SEMS = 2 * (N_CHIP - 1)


def _scatter_copies(b_refs, land_refs, send_sems, recv_sems):
    x, y, c = lax.axis_index("x"), lax.axis_index("y"), lax.axis_index("c")
    me = 4 * x + 2 * y + c
    sends, recvs = [], []
    for kbits in range(1, N_DEV):
        px = 1 - x if kbits & 4 else x
        py = 1 - y if kbits & 2 else y
        pc = 1 - c if kbits & 1 else c
        peer = 4 * px + 2 * py + pc
        for b in range(len(b_refs)):
            s = (kbits - 1) * len(b_refs) + b
            sends.append(pltpu.make_async_remote_copy(
                src_ref=b_refs[b].at[peer], dst_ref=land_refs[b].at[me], send_sem=send_sems[s],
                recv_sem=recv_sems[s], device_id=(px, py, pc), device_id_type=pl.DeviceIdType.MESH))
            recvs.append(pltpu.make_async_remote_copy(
                src_ref=b_refs[b].at[peer], dst_ref=land_refs[b].at[peer], send_sem=send_sems[s],
                recv_sem=recv_sems[s], device_id=(px, py, pc), device_id_type=pl.DeviceIdType.MESH))
    return sends, recvs


def _scatter_start(name, bufs):
    nb = len(bufs)
    n = (N_DEV - 1) * nb

    def body(*refs):
        b_refs, land_refs, outs = refs[:nb], refs[nb:2 * nb], refs[2 * nb:]
        sends, _ = _scatter_copies(b_refs, land_refs, outs[:n], outs[n:2 * n])
        for cp in sends:
            cp.start()
        token = outs[2 * n + 2 * nb]
        token[...] = jnp.zeros_like(token)

    thru = tuple(pltpu.HBM(b.shape, b.dtype) for b in bufs)
    res = pl.pallas_call(
        body, name=name, in_specs=(_HBM,) * (2 * nb),
        out_specs=(_SEM,) * (2 * n) + (_HBM,) * (2 * nb) + (pl.BlockSpec(memory_space=pltpu.VMEM),),
        out_shape=(pltpu.SemaphoreType.DMA(()),) * (2 * n) + thru + thru + (jax.ShapeDtypeStruct((8, LANES), F32),),
        input_output_aliases={i: 2 * n + i for i in range(2 * nb)},
        compiler_params=pltpu.CompilerParams(has_side_effects=pltpu.SideEffectType.DATAFLOW_SIDE_EFFECTING),
    )(*[pltpu.with_memory_space_constraint(b, pltpu.HBM) for b in bufs],
      *[pltpu.with_memory_space_constraint(lax.empty(b.shape, b.dtype), pltpu.HBM) for b in bufs])
    return res[:2 * n], list(res[2 * n:2 * n + nb]), list(res[2 * n + nb:2 * n + 2 * nb]), res[2 * n + 2 * nb]


def _scatter_wait(name, sems, bufs_thru, lands_thru, after):
    nb = len(bufs_thru)
    n = (N_DEV - 1) * nb

    def body(*refs):
        b_refs, land_refs, sem_refs = refs[:nb], refs[nb:2 * nb], refs[2 * nb:2 * nb + 2 * n]
        sends, recvs = _scatter_copies(b_refs, land_refs, sem_refs[:n], sem_refs[n:])
        for cp in sends:
            cp.wait_send()
        for cp in recvs:
            cp.wait_recv()

    thru = tuple(pltpu.HBM(b.shape, b.dtype) for b in bufs_thru)
    res = pl.pallas_call(
        body, name=name, in_specs=(_HBM,) * (2 * nb) + (_SEM,) * (2 * n) + (pl.BlockSpec(memory_space=pl.ANY),),
        out_specs=(_HBM,) * (2 * nb), out_shape=thru + thru,
        input_output_aliases={i: i for i in range(2 * nb)},
        compiler_params=pltpu.CompilerParams(has_side_effects=pltpu.SideEffectType.DATAFLOW_SIDE_EFFECTING),
    )(*bufs_thru, *lands_thru, *sems, after)
    return list(res[:nb]), list(res[nb:])


def _chip_exchange_start(name, sums):
    def body(s_ref, land_ref, *outs):
        sems, token = outs[:N_CHIP_SEMS], outs[N_CHIP_SEMS + 2]
        sends, _ = _chip_copies(s_ref, land_ref, sems[:N_CHIP - 1], sems[N_CHIP - 1:])
        for cp in sends:
            cp.start()
        token[...] = jnp.zeros_like(token)

    res = pl.pallas_call(
        body, name=name, in_specs=(_HBM, _HBM),
        out_specs=(_SEM,) * N_CHIP_SEMS + (_HBM, _HBM, pl.BlockSpec(memory_space=pltpu.VMEM)),
        out_shape=(pltpu.SemaphoreType.DMA(()),) * N_CHIP_SEMS
        + (pltpu.HBM(sums.shape, sums.dtype), pltpu.HBM(sums.shape, sums.dtype), jax.ShapeDtypeStruct((8, LANES), F32)),
        input_output_aliases={0: N_CHIP_SEMS, 1: N_CHIP_SEMS + 1},
        compiler_params=pltpu.CompilerParams(has_side_effects=pltpu.SideEffectType.DATAFLOW_SIDE_EFFECTING),
    )(pltpu.with_memory_space_constraint(sums, pltpu.HBM),
      pltpu.with_memory_space_constraint(lax.empty(sums.shape, sums.dtype), pltpu.HBM))
    return res[:N_CHIP_SEMS], res[N_CHIP_SEMS], res[N_CHIP_SEMS + 1], res[N_CHIP_SEMS + 2]


def _chip_exchange_wait(name, sems, sums_thru, land_thru, after):
    def body(s_ref, land_ref, *rest):
        sems = rest[:N_CHIP_SEMS]
        sends, recvs = _chip_copies(s_ref, land_ref, sems[:N_CHIP - 1], sems[N_CHIP - 1:])
        for cp in sends:
            cp.wait_send()
        for cp in recvs:
            cp.wait_recv()

    return pl.pallas_call(
        body, name=name, in_specs=(_HBM, _HBM) + (_SEM,) * N_CHIP_SEMS + (pl.BlockSpec(memory_space=pl.ANY),),
        out_specs=(_HBM, _HBM),
        out_shape=(pltpu.HBM(sums_thru.shape, sums_thru.dtype), pltpu.HBM(sums_thru.shape, sums_thru.dtype)),
        input_output_aliases={0: 0, 1: 1},
        compiler_params=pltpu.CompilerParams(has_side_effects=pltpu.SideEffectType.DATAFLOW_SIDE_EFFECTING),
    )(sums_thru, land_thru, *sems, after)


def _adamw(name, slots, w, m, v, tr=256):
    unit_mid = w.ndim == 3 and w.shape[1] == 1 and w.shape[0] > 1
    R, Wd = (w.shape[0], w.shape[2]) if unit_mid else w.shape[-2:]
    depth_axis = w.ndim == 3 and not unit_mid
    if R % tr == 0:
        tc = Wd
    else:
        tr, tc = R, (256 if (Wd % 256 == 0 and R > 256) else Wd)
    at = (slice(None), 0, slice(None)) if unit_mid else Ellipsis

    def body(s_ref, w_ref, m_ref, v_ref, g_out, d_out, m_out, v_out):
        g = s_ref[0].astype(F32)
        for j in range(1, slots.shape[0]):
            g = g + s_ref[j].astype(F32)
        m_new = ADAM_B1 * m_ref[at] + (1.0 - ADAM_B1) * g
        v_new = ADAM_B2 * v_ref[at] + (1.0 - ADAM_B2) * jnp.square(g)
        m_hat = m_new / (1.0 - ADAM_B1 ** ADAM_STEP)
        v_hat = v_new / (1.0 - ADAM_B2 ** ADAM_STEP)
        g_out[at] = g
        d_out[at] = -ADAM_LR * (m_hat / (jnp.sqrt(v_hat) + ADAM_EPS) + ADAM_WD * w_ref[at])
        m_out[at] = m_new
        v_out[at] = v_new

    if unit_mid:
        row = pl.BlockSpec((tr, 1, tc), lambda i, j: (i, 0, j))
    elif depth_axis:
        row = pl.BlockSpec((None, tr, tc), lambda i, j: (0, i, j))
    else:
        row = pl.BlockSpec((tr, tc), lambda i, j: (i, j))
    return pl.pallas_call(
        body, name=name, grid=(R // tr, Wd // tc),
        in_specs=[pl.BlockSpec((slots.shape[0], tr, tc), lambda i, j: (0, i, j)), row, row, row],
        out_specs=[row] * 4, out_shape=[jax.ShapeDtypeStruct(w.shape, F32)] * 4,
        compiler_params=pltpu.CompilerParams(dimension_semantics=("parallel", "parallel"),
                                             vmem_limit_bytes=VMEM_LIMIT),
    )(slots, w, m, v)


PACK_W = 1024
PACKED = [(n, s) for n, s in REPLICATED if n != "sgu_w"]
_SMALL_SIZES = [int(np.prod(s)) for _, s in PACKED]
_SMALL_ROWS = _round_up(_round_up(sum(_SMALL_SIZES) + PACK_W, PACK_W) // PACK_W, 8)
_LOSS_AT = sum(_SMALL_SIZES)
W_IN_SHARD = P_TOTAL // N_DEV


def _pack_rows(parts, rows, dtype):
    flat = jnp.concatenate([p.reshape(-1).astype(dtype) for p in parts])
    return jnp.pad(flat, (0, rows * PACK_W - flat.shape[0])).reshape(rows, PACK_W)


def _w_in_groups_t(blocks):
    wt = blocks.reshape(P_TOTAL, D)
    o, c = 2 * D, 2 * D + 3 * D
    z = lambda r: jnp.zeros((r, D), wt.dtype)
    rw = jnp.concatenate([wt[o:c], wt[c:c + L_W], z(128 - L_W), wt[c + L_W:c + L_W + L_A], z(128 - L_A),
                          wt[c + L_W + L_A:o + C_B], z(256 - L_G)], axis=0)
    return wt[:o], rw, wt[o + C_B:]


def _w_in_grad_blocks(g_sgu_t, g_rw_t, g_gate_t):
    c = 3 * D
    full = jnp.concatenate([g_sgu_t, g_rw_t[:c], g_rw_t[c:c + L_W], g_rw_t[c + 128:c + 128 + L_A],
                            g_rw_t[c + 256:c + 256 + L_G], g_gate_t], axis=0)
    return full.reshape(N_DEV, W_IN_SHARD, D)


def _mesh_index():
    me = 4 * lax.axis_index("x") + 2 * lax.axis_index("y") + lax.axis_index("c")
    return me.astype(jnp.int32).reshape(1)


def _fill_slot(name, dst, src, idx, src_idx=None):
    R, Wd = dst.shape[1:]
    scalars = [idx] if src_idx is None else [idx, src_idx]
    if src_idx is None:
        src_spec = pl.BlockSpec((R, Wd), lambda i, *s: (0, 0))
    else:
        src_spec = pl.BlockSpec((None, R, Wd), lambda i, *s: (s[1][0], 0, 0))

    def body(*refs):
        src_ref, out_ref = refs[len(scalars) + 1], refs[len(scalars) + 2]
        out_ref[...] = src_ref[...]

    return pl.pallas_call(
        body, name=name,
        grid_spec=pltpu.PrefetchScalarGridSpec(
            num_scalar_prefetch=len(scalars), grid=(1,),
            in_specs=[pl.BlockSpec(memory_space=pl.ANY), src_spec],
            out_specs=pl.BlockSpec((None, R, Wd), lambda i, *s: (s[0][0], 0, 0))),
        out_shape=jax.ShapeDtypeStruct(dst.shape, dst.dtype),
        input_output_aliases={len(scalars): 0},
        compiler_params=pltpu.CompilerParams(vmem_limit_bytes=VMEM_LIMIT),
    )(*scalars, dst, src)


class TwoLevelGather:
    def __init__(self, bufs, skip_own=()):
        self.bufs, self.nb, self.skip_own = list(bufs), len(bufs), tuple(skip_own)
        self.any_specs = [pl.BlockSpec(memory_space=pl.ANY)] * self.nb
        self.out_shape = [jax.ShapeDtypeStruct((N_DEV,) + b.shape, b.dtype) for b in self.bufs]
        self.sem_shapes = [pltpu.SemaphoreType.DMA((7 * self.nb,)), pltpu.SemaphoreType.DMA((7 * self.nb,)),
                           pltpu.SemaphoreType.DMA((self.nb,))]

    def _copies(self, in_refs, out_refs, sems):
        send_sems, recv_sems, local_sems = sems
        nb = self.nb
        x, y, c = lax.axis_index("x"), lax.axis_index("y"), lax.axis_index("c")
        me, sibling = (x, y, c), (x, y, 1 - c)
        chips = [(1 - x, y), (x, 1 - y), (1 - x, 1 - y)]

        def slot(b, dev):
            return out_refs[b].at[4 * dev[0] + 2 * dev[1] + dev[2]]

        def copy(b, k, block, to, own=False):
            return pltpu.make_async_remote_copy(
                src_ref=in_refs[b] if own else slot(b, block), dst_ref=slot(b, block),
                send_sem=send_sems.at[7 * b + k], recv_sem=recv_sems.at[7 * b + k], device_id=to,
                device_id_type=pl.DeviceIdType.MESH)

        cp = {}
        cp["local"] = [pltpu.make_async_copy(in_refs[b], slot(b, me), local_sems.at[b]) for b in range(nb)
                       if b not in self.skip_own]
        cp["first"] = [copy(b, 0, me, sibling, own=True) for b in range(nb)]
        cp["first"] += [copy(b, 1 + j, me, (*chip, c), own=True) for j, chip in enumerate(chips) for b in range(nb)]
        cp["over_ici"] = [copy(b, 1 + j, (*chip, c), me) for j, chip in enumerate(chips) for b in range(nb)]
        cp["passed"] = [copy(b, 4 + j, (*chip, c), sibling) for j, chip in enumerate(chips) for b in range(nb)]
        cp["from_sibling"] = [copy(b, 0, sibling, me) for b in range(nb)]
        cp["from_sibling"] += [copy(b, 4 + j, (*chip, 1 - c), me) for j, chip in enumerate(chips) for b in range(nb)]
        return cp

    def start(self, in_refs, out_refs, sems):
        cp = self._copies(in_refs, out_refs, sems)
        for c in cp["first"] + cp["local"]:
            c.start()

    def forward(self, in_refs, out_refs, sems):
        cp = self._copies(in_refs, out_refs, sems)
        for arrived, onward in zip(cp["over_ici"], cp["passed"]):
            arrived.wait_recv()
            onward.start()

    def finish(self, in_refs, out_refs, sems):
        cp = self._copies(in_refs, out_refs, sems)
        for c in cp["from_sibling"]:
            c.wait_recv()
        for c in cp["first"] + cp["passed"]:
            c.wait_send()
        for c in cp["local"]:
            c.wait()

    def schedule(self, n_steps):
        return [(0, self.start), (max(n_steps - 3, 0), self.forward), (n_steps - 1, self.finish)]


def _all_gather_two_level(name, bufs, skip_own=()):
    ex = TwoLevelGather(bufs, skip_own)

    def body(*refs):
        args = refs[:ex.nb], refs[ex.nb:2 * ex.nb], refs[2 * ex.nb:]
        ex.start(*args)
        ex.forward(*args)
        ex.finish(*args)

    return pl.pallas_call(body, name=name, in_specs=ex.any_specs, out_specs=ex.any_specs, out_shape=ex.out_shape,
                          scratch_shapes=ex.sem_shapes)(*ex.bufs)


def _cols_from_blocks(blk):
    return jnp.transpose(blk, (1, 0, 2)).reshape(blk.shape[1], -1)


def _cols_to_blocks(g):
    r, c = g.shape
    return jnp.transpose(g.reshape(r, N_DEV, c // N_DEV), (1, 0, 2))


FIRST_WEIGHTS = ["w_in", "shift_b", "w_lora_w", "a_lora_w", "g_lora_w"]
LATE_WEIGHTS = ["w_proj_a", "w_proj_b", "w_out", "w_ffn1", "w_ffn2"]
SCAN_CARRIED = ["w_proj_a", "w_proj_b", "w_out"]
FFN_WEIGHTS = ["w_ffn1", "w_ffn2"]


def _late_weights(shards):
    ex = TwoLevelGather([shards[n].astype(BF16) for n in LATE_WEIGHTS])

    def finish(results):
        got = dict(zip(LATE_WEIGHTS, results))
        W = {n: got[n].reshape(-1, D) for n in ("w_proj_a", "w_proj_b", "w_out", "w_ffn2")}
        W["w_ffn1"] = got["w_ffn1"].reshape(N_DEV, D, -1)
        return W
    return ex, finish


def _gather_weights(shards):
    def payload(n):
        if n == "w_in":
            return jnp.transpose(shards[n][0]).astype(BF16)
        return shards[n] if n == "shift_b" else shards[n].astype(BF16)
    payloads = [payload(n) for n in FIRST_WEIGHTS]
    got = dict(zip(FIRST_WEIGHTS, _all_gather_two_level("weight_all_gather", payloads, skip_own=(0,))))
    got["w_in"] = _fill_slot("w_in_own_slot", got["w_in"], payloads[0], _mesh_index())
    W = {}
    W["w_sgu_t"], W["w_rw_t"], W["w_gate_t"] = _w_in_groups_t(got["w_in"])
    z = lambda r, c, dt: jnp.zeros((r, c), dt)
    W["w_lora"] = jnp.concatenate([_cols_from_blocks(got["w_lora_w"][:, 0]).astype(F32), z(128 - L_W, D, F32)], axis=0)
    W["a_lora"] = jnp.concatenate([_cols_from_blocks(got["a_lora_w"][:, 0]).astype(F32), z(128 - L_A, D, F32)], axis=0)
    W["g_lora"] = jnp.concatenate([_cols_from_blocks(got["g_lora_w"][:, 0]).astype(F32), z(256 - L_G, D, F32)], axis=0)
    sb = _cols_from_blocks(got["shift_b"][:, 0])
    W["sb"] = jnp.concatenate([sb[:, :3 * D], sb[:, 3 * D:3 * D + L_W], z(2, 128 - L_W, F32),
                               sb[:, 3 * D + L_W:3 * D + L_W + L_A], z(2, 128 - L_A, F32),
                               sb[:, 3 * D + L_W + L_A:], z(2, 256 - L_G, F32)], axis=1)
    return W


def _replicated_weights(rep):
    W = {n: rep[n] for n in ("g_mix", "sgu_ln_w", "sgu_ln_b", "w0", "a0", "k_k", "k_a", "r_k", "ln_x_w", "ln_x_b",
                             "g_ffn")}
    W["g_final"] = rep["g_final"].reshape(1, D)
    W["sgu_w"] = rep["sgu_w"][0]
    W["sgu_bt"] = jnp.transpose(rep["sgu_b"][0])
    return W


def _late_grad_blocks(G):
    return Exchange([G[n].reshape(N_DEV, -1, D) for n in SCAN_CARRIED]
                    + [G["sgu_w"].reshape(SGU_G * SGU_C, SGU_C).astype(GRAD_PAYLOAD)],
                    [False] * len(SCAN_CARRIED) + [True])


def _first_grad_blocks(G):
    sbg = G["sb"]
    c = 3 * D
    sb = jnp.concatenate([sbg[:, :c], sbg[:, c:c + L_W], sbg[:, c + 128:c + 128 + L_A],
                          sbg[:, c + 256:c + 256 + L_G]], axis=1)
    return {
        "shift_b": _cols_to_blocks(sb),
        "w_lora_w": _cols_to_blocks(G["w_lora"][:L_W]), "a_lora_w": _cols_to_blocks(G["a_lora"][:L_A]),
        "g_lora_w": _cols_to_blocks(G["g_lora"][:L_G]),
    }


def _replicated_grads(G):
    small = {n: G[n] for n in ("g_mix", "sgu_ln_w", "sgu_ln_b", "w0", "a0", "k_k", "k_a", "r_k", "ln_x_w", "ln_x_b",
                               "g_ffn", "g_final")}
    small["sgu_w"] = G["sgu_w"]
    small["sgu_b"] = jnp.transpose(G["sgu_bt"])
    return small


def kernel(x, g_mix, w_in, sgu_ln_w, sgu_ln_b, sgu_w, sgu_b, w_proj_a, shift_b, w_lora_w, w0, a_lora_w, a0, g_lora_w, k_k, k_a, r_k, ln_x_w, ln_x_b, w_proj_b, w_out, g_ffn, w_ffn1, w_ffn2, g_final, loss_target, m_g_mix, m_w_in, m_sgu_ln_w, m_sgu_ln_b, m_sgu_w, m_sgu_b, m_w_proj_a, m_shift_b, m_w_lora_w, m_w0, m_a_lora_w, m_a0, m_g_lora_w, m_k_k, m_k_a, m_r_k, m_ln_x_w, m_ln_x_b, m_w_proj_b, m_w_out, m_g_ffn, m_w_ffn1, m_w_ffn2, m_g_final, v_g_mix, v_w_in, v_sgu_ln_w, v_sgu_ln_b, v_sgu_w, v_sgu_b, v_w_proj_a, v_shift_b, v_w_lora_w, v_w0, v_a_lora_w, v_a0, v_g_lora_w, v_k_k, v_k_a, v_r_k, v_ln_x_w, v_ln_x_b, v_w_proj_b, v_w_out, v_g_ffn, v_w_ffn1, v_w_ffn2, v_g_final):
    env = dict(locals())
    weights = {n: env[n] for n in WEIGHT_ORDER}
    moms = {n: env["m_" + n] for n in WEIGHT_ORDER}
    vars_ = {n: env["v_" + n] for n in WEIGHT_ORDER}

    shards = {n: weights[n] for n, _, _ in SHARDED}
    W = _gather_weights(shards)
    W.update(_replicated_weights({n: weights[n] for n, _ in REPLICATED}))
    in_flight = {}

    def send_w_in_grads(G):
        blocks = _w_in_grad_blocks(G["w_sgu_t"], G["w_rw_t"], G["w_gate_t"])
        got = _pair_exchange("grad_pair_exchange", blocks)
        core = lax.axis_index("c").astype(jnp.int32).reshape(1)
        sums = _pair_sum("grad_pair_sum", blocks, got, core)
        in_flight["sems"], in_flight["sums"], in_flight["land"], token = _chip_exchange_start("grad_chip_start", sums)
        return token

    def send_ffn_grads(G):
        in_flight["ffn"] = _scatter_start("grad_ffn_start", [G["w_ffn1"], G["w_ffn2"].reshape(N_DEV, -1, D)])
        return in_flight["ffn"][3]

    loss_part, dx, G, late_slots = _local_step(x[0], loss_target[0], W, late_weights=_late_weights(shards),
                                               early_grads=_late_grad_blocks, w_in_grads_ready=send_w_in_grads,
                                               ffn_grads_ready=send_ffn_grads)

    slots = dict(zip(SCAN_CARRIED, late_slots))
    me = _mesh_index()
    sent, landed = _scatter_wait("grad_ffn_wait", *in_flight["ffn"][:3], after=dx)
    for i, n in enumerate(FFN_WEIGHTS):
        slots[n] = _fill_slot("grad_own_slot_" + n, landed[i], sent[i], me, src_idx=me)
    blocks = _first_grad_blocks(G)
    small = _replicated_grads(G)
    small_parts = [small[n] for n, _ in PACKED] + [jnp.full((PACK_W,), loss_part, F32)]
    rest = [n for n in FIRST_WEIGHTS if n != "w_in"]
    res = _exchange("grad_exchange", [blocks[n] for n in rest] + [_pack_rows(small_parts, _SMALL_ROWS, F32)],
                    [False] * len(rest) + [True])
    slots.update(zip(rest, res[:-1]))
    small_slots = res[-1]
    sums, chip_slots = _chip_exchange_wait("grad_chip_wait", in_flight["sems"], in_flight["sums"], in_flight["land"],
                                           after=small_slots)
    my_chip = (2 * lax.axis_index("x") + lax.axis_index("y")).astype(jnp.int32).reshape(1)
    slots["w_in"] = _fill_slot("grad_own_slot", chip_slots, sums, my_chip, src_idx=my_chip)

    outs = [dict(), dict(), dict(), dict()]
    for n, _, _ in SHARDED:
        if n == "w_in":
            res = _adamw("adamw_" + n, slots[n], *[jnp.transpose(t, (2, 0, 1)) for t in (weights[n], moms[n], vars_[n])])
            res = [jnp.transpose(t, (1, 2, 0)) for t in res]
        else:
            res = _adamw("adamw_" + n, slots[n], weights[n], moms[n], vars_[n])
        for k in range(4):
            outs[k][n] = res[k]

    sgu_shape = (SGU_G * SGU_C, SGU_C)
    res = _adamw("adamw_sgu_w", late_slots[len(SCAN_CARRIED)], *[t.reshape(sgu_shape) for t in
                                                                  (weights["sgu_w"], moms["sgu_w"], vars_["sgu_w"])])
    for k in range(4):
        outs[k]["sgu_w"] = res[k].reshape(weights["sgu_w"].shape)

    def packed(d):
        return _pack_rows([d[n] for n, _ in PACKED], _SMALL_ROWS, F32)
    small_out = _adamw("adamw_replicated", small_slots, packed(weights), packed(moms), packed(vars_))
    for k in range(4):
        flat = small_out[k].reshape(-1)
        off = 0
        for (n, s), size in zip(PACKED, _SMALL_SIZES):
            outs[k][n] = flat[off:off + size].reshape(s)
            off += size
    loss = small_out[0].reshape(-1)[_LOSS_AT]
    return (loss, dx[None], *[outs[0][n] for n in WEIGHT_ORDER], *[outs[1][n] for n in WEIGHT_ORDER],
            *[outs[2][n] for n in WEIGHT_ORDER], *[outs[3][n] for n in WEIGHT_ORDER])
```

```python
import functools
import numpy as np
import jax
import jax.numpy as jnp
from jax import lax
from jax.experimental import pallas as pl
from jax.experimental.pallas import tpu as pltpu

F32 = jnp.float32
BF16 = jnp.bfloat16

D = 1024
NH, HN = 16, 64
NP, PW = NH // 2, 2 * HN
SGU_G, SGU_C = 8, 128
L_W, L_A, L_G = 64, 64, 160
C_B = 3 * D + L_W + L_A + L_G
P_TOTAL = 2 * D + C_B + 2 * D
D_FF = 4 * D
RW_INT = 3 * D + 128 + 128 + 256
NORM_EPS, LN_EPS, GN_EPS = 1e-6, 1e-5, 64e-5
N_DEV = 8
LANES = 128
SCAN_C = 64
N_KEPT = 4
SOLVE_B = 16
SCAN_PRECISION = lax.Precision.HIGH
SCAN_OUT_PRECISION = lax.Precision.DEFAULT
GRAD_PAYLOAD = BF16
VMEM_LIMIT = 56 * 1024 * 1024
MATMUL_VMEM_BUDGET = 40 * 1024 * 1024
STEP_COST_BYTES = 512 * 1024
HBM_COST_RATIO = 3

ADAM_LR, ADAM_B1, ADAM_B2, ADAM_EPS, ADAM_WD, ADAM_STEP = 0.001, 0.9, 0.999, 1e-08, 0.01, 10

SHARDED = [
    ("w_in", (D, P_TOTAL), 1), ("w_proj_a", (D, D), 0), ("shift_b", (2, C_B), 1), ("w_lora_w", (L_W, D), 1),
    ("a_lora_w", (L_A, D), 1), ("g_lora_w", (L_G, D), 1), ("w_proj_b", (D, D), 0), ("w_out", (D, D), 0),
    ("w_ffn1", (D, D_FF), 1), ("w_ffn2", (D_FF, D), 0),
]
REPLICATED = [
    ("g_mix", (1, D)), ("sgu_ln_w", (1, D)), ("sgu_ln_b", (1, D)), ("sgu_w", (1, SGU_G, SGU_C, SGU_C)),
    ("sgu_b", (1, SGU_G, SGU_C)), ("w0", (1, D)), ("a0", (1, D)), ("k_k", (1, D)), ("k_a", (1, D)), ("r_k", (1, D)),
    ("ln_x_w", (1, D)), ("ln_x_b", (1, D)), ("g_ffn", (1, D)), ("g_final", (D,)),
]
WEIGHT_ORDER = ["g_mix", "w_in", "sgu_ln_w", "sgu_ln_b", "sgu_w", "sgu_b", "w_proj_a", "shift_b", "w_lora_w", "w0",
                "a_lora_w", "a0", "g_lora_w", "k_k", "k_a", "r_k", "ln_x_w", "ln_x_b", "w_proj_b", "w_out", "g_ffn",
                "w_ffn1", "w_ffn2", "g_final"]


def _round_up(n, m):
    return (n + m - 1) // m * m


def _pick(n, target):
    if n <= target:
        return n
    best = None
    for t in range(LANES, target + 1, LANES):
        if n % t == 0:
            best = t
    assert best is not None, (n, target)
    return best


def _matmul(name, a, b, mode, out_dtype=F32, tm=2048, tn=1024, tk=2048, out_blocks=None, epilogue=None, extras=(),
            out_dtypes=(), after=None, whole_rows=False, out_widths=None):
    b_blocks = b.shape[0] if b.ndim == 3 else None
    bshape = b.shape if b.ndim == 2 else (b.shape[1], b.shape[0] * b.shape[2])
    if mode == "nn":
        (M, K), (K2, N) = a.shape, bshape
    elif mode == "nt":
        (M, K), (N, K2) = a.shape, bshape
    else:
        (K, M), (K2, N) = a.shape, bshape
    assert K == K2, (name, a.shape, b.shape)
    assert b_blocks is None or mode != "tn"
    assert out_blocks is None or mode == "tn"
    tn = min(tn, N // (out_blocks or 1), bshape[1] // b_blocks if (b_blocks and mode == "nn") else tn)
    tk = min(tk, bshape[1] // b_blocks if (b_blocks and mode == "nt") else tk)
    tm, tn, tk = _pick(M, tm), _pick(N, tn), _pick(K, tk)

    def vmem_bytes(tm, tk):
        tiles = tm * tk * a.dtype.itemsize + tk * tn * b.dtype.itemsize
        for i, dt in enumerate(out_dtypes if epilogue else (out_dtype,)):
            tiles += tm * (out_widths[i] if out_widths else tn) * jnp.dtype(dt).itemsize
        for x in extras:
            arr = x[0] if isinstance(x, tuple) else x
            tiles += (tm if arr.shape[0] > 1 else 1) * tn * arr.dtype.itemsize
        return 2 * tiles + (tm * tn * 4 if K // tk > 1 else 0)

    def cost(tm, tk):
        ni, nj, nk = M // tm, N // tn, K // tk
        steps = ni * nj * nk
        acc_passes = steps * tm * tn * 8 if nk > 1 else 0
        a_reads = M * K * a.dtype.itemsize * (nj if nk > 1 else 1)
        b_reads = K * N * b.dtype.itemsize * (ni if (nj > 1 or nk > 1) else 1)
        return (steps * STEP_COST_BYTES + acc_passes + vmem_bytes(tm, tk) // 2
                + HBM_COST_RATIO * (a_reads + b_reads))

    options = [(m, k) for m in ({M} if whole_rows else {_pick(M, max(t, LANES)) for t in (tm, tm // 2, tm // 4)})
               for k in {_pick(K, max(t, LANES)) for t in (tk, tk // 2, tk // 4)}
               if vmem_bytes(m, k) <= MATMUL_VMEM_BUDGET]
    tm, tk = min(options, key=lambda o: cost(*o))
    nk = K // tk
    dims = {"nn": (((1,), (0,)), ((), ())), "nt": (((1,), (1,)), ((), ())), "tn": (((0,), (0,)), ((), ()))}[mode]

    n_x, n_o = len(extras), len(out_dtypes) if epilogue else 1
    n_after = 0 if after is None else 1

    def body(a_ref, b_ref, *rest):
        x_refs, o_refs, acc = rest[:n_x], rest[n_x + n_after:n_x + n_after + n_o], rest[n_x + n_after + n_o:]
        part = lax.dot_general(a_ref[...].astype(BF16), b_ref[...].astype(BF16), dims, preferred_element_type=F32)

        def finish(res):
            outs = epilogue(res, *[r[...] for r in x_refs]) if epilogue else (res,)
            for r, v in zip(o_refs, outs):
                r[...] = v.astype(r.dtype)

        if nk == 1:
            finish(part)
            return
        acc_ref, k = acc[0], pl.program_id(2)

        @pl.when(k == 0)
        def _():
            acc_ref[...] = part

        @pl.when(k > 0)
        def _():
            acc_ref[...] += part

        @pl.when(k == nk - 1)
        def _():
            finish(acc_ref[...])

    a_spec = {"nn": pl.BlockSpec((tm, tk), lambda i, j, k: (i, k)), "nt": pl.BlockSpec((tm, tk), lambda i, j, k: (i, k)),
              "tn": pl.BlockSpec((tk, tm), lambda i, j, k: (k, i))}[mode]
    b_spec = {"nn": pl.BlockSpec((tk, tn), lambda i, j, k: (k, j)), "nt": pl.BlockSpec((tn, tk), lambda i, j, k: (j, k)),
              "tn": pl.BlockSpec((tk, tn), lambda i, j, k: (k, j))}[mode]
    if b_blocks and mode == "nn":
        per = b.shape[2] // tn
        b_spec = pl.BlockSpec((None, tk, tn), lambda i, j, k: (j // per, k, j % per))
    elif b_blocks:
        per = b.shape[2] // tk
        b_spec = pl.BlockSpec((None, tn, tk), lambda i, j, k: (k // per, j, k % per))
    out_spec = pl.BlockSpec((tm, tn), lambda i, j, k: (i, j))
    out_shape = jax.ShapeDtypeStruct((M, N), out_dtype)
    if out_blocks:
        per_o = N // out_blocks // tn
        out_spec = pl.BlockSpec((None, tm, tn), lambda i, j, k: (j // per_o, i, j % per_o))
        out_shape = jax.ShapeDtypeStruct((out_blocks, M, N // out_blocks), out_dtype)
    epi_widths = list(out_widths) if out_widths else [N] * n_o
    assert all(w == N for w in epi_widths) or N == tn
    epi_specs = [pl.BlockSpec((tm, tn if w == N else w), lambda i, j, k: (i, j)) for w in epi_widths]
    x_specs, x_args = [], []
    for x in extras:
        arr, off = x if isinstance(x, tuple) else (x, 0)
        if arr.shape[0] == 1:
            x_specs.append(pl.BlockSpec((1, tn), lambda i, j, k: (0, j)))
        else:
            x_specs.append(pl.BlockSpec((tm, tn), lambda i, j, k, off=off: (i, j + off)))
        x_args.append(arr)
    res = pl.pallas_call(
        body, name=name, grid=(M // tm, N // tn, nk),
        in_specs=[a_spec, b_spec] + x_specs + [pl.BlockSpec(memory_space=pl.ANY)] * n_after,
        out_specs=epi_specs if epilogue else out_spec,
        out_shape=[jax.ShapeDtypeStruct((M, w), dt) for w, dt in zip(epi_widths, out_dtypes)] if epilogue else out_shape,
        scratch_shapes=[pltpu.VMEM((tm, tn), F32)] if nk > 1 else [],
        compiler_params=pltpu.CompilerParams(dimension_semantics=("parallel", "parallel", "arbitrary"),
                                             vmem_limit_bytes=VMEM_LIMIT),
    )(a, b, *x_args, *([after] if n_after else []))
    return res


class Rows:
    def __init__(self, arr, width=None, cb=0):
        self.arr, self.width, self.cb = arr, (arr.shape[1] if width is None else width), cb


class Heads:
    def __init__(self, arr):
        self.arr = arr


class Halo:
    def __init__(self, arr, side):
        self.arr, self.side = arr, side


def _rows_call(name, fn, ins, consts, outs, accs=(), tm=512, with_pid=False):
    T = next(o.arr.shape[1] if isinstance(o, Heads) else o.arr.shape[0] for o in ins if not isinstance(o, Halo))
    tm = min(tm, T)
    n_tiles = T // tm
    n_in, n_c, n_out = len(ins), len(consts), len(outs)
    in_specs, args = [], []
    for o in ins:
        if isinstance(o, Rows):
            in_specs.append(pl.BlockSpec((tm, o.width), lambda i, cb=o.cb: (i, cb)))
        elif isinstance(o, Heads):
            in_specs.append(pl.BlockSpec((NP, tm, PW), lambda i: (0, i, 0)))
        else:
            w = o.arr.shape[1]
            if o.side < 0:
                in_specs.append(pl.BlockSpec((8, w), lambda i: (jnp.maximum(i * (tm // 8) - 1, 0), 0)))
            else:
                in_specs.append(pl.BlockSpec((8, w), lambda i: (jnp.minimum((i + 1) * (tm // 8), T // 8 - 1), 0)))
        args.append(o.arr)
    for c in consts:
        in_specs.append(pl.BlockSpec(c.shape, lambda i, nd=c.ndim: (0,) * nd))
        args.append(c)
    out_specs, out_shape = [], []
    for o in outs:
        if o[0] == "rows":
            out_specs.append(pl.BlockSpec((tm, o[1]), lambda i: (i, 0)))
            out_shape.append(jax.ShapeDtypeStruct((T, o[1]), o[2]))
        else:
            out_specs.append(pl.BlockSpec((NP, tm, PW), lambda i: (0, i, 0)))
            out_shape.append(jax.ShapeDtypeStruct((NP, T, PW), o[1]))
    for shape, dt in accs:
        out_specs.append(pl.BlockSpec(shape, lambda i, nd=len(shape): (0,) * nd))
        out_shape.append(jax.ShapeDtypeStruct(shape, dt))

    def body(*refs):
        i = pl.program_id(0)
        vals = []
        vals = [r[...] for r in refs[:n_in + n_c]]
        res = fn(i, n_tiles, *vals) if with_pid else fn(*vals)
        out_refs = refs[n_in + n_c:]
        for r, v in zip(out_refs[:n_out], res[:n_out]):
            r[...] = v.astype(r.dtype)
        if accs:
            @pl.when(i == 0)
            def _():
                for r in out_refs[n_out:]:
                    r[...] = jnp.zeros_like(r)

            for r, v in zip(out_refs[n_out:], res[n_out:]):
                r[...] += v.astype(r.dtype)

    res = pl.pallas_call(
        body, name=name, grid=(n_tiles,), in_specs=in_specs, out_specs=out_specs, out_shape=out_shape,
        compiler_params=pltpu.CompilerParams(dimension_semantics=("arbitrary",), vmem_limit_bytes=VMEM_LIMIT),
    )(*args)
    return res


def _rms(x, g):
    return x * lax.rsqrt(jnp.mean(x * x, axis=-1, keepdims=True) + NORM_EPS) * g


def _gelu(x):
    return 0.5 * x * (1.0 + lax.erf(x * 0.7071067811865476))


def _sigmoid(x):
    return 1.0 / (1.0 + jnp.exp(-x))


def _bdot(a, b):
    return jnp.dot(a.astype(BF16), b.astype(BF16), preferred_element_type=F32)


def _to_heads(x):
    return jnp.concatenate([x[:, p * PW:(p + 1) * PW][None] for p in range(NP)], axis=0)


def _from_heads(xp):
    return jnp.concatenate([xp[p] for p in range(NP)], axis=-1)


def _head_sum(xp):
    low = lax.broadcasted_iota(jnp.int32, xp.shape, xp.ndim - 1) < HN
    both = jnp.sum(xp, axis=-1, keepdims=True)
    first = jnp.sum(jnp.where(low, xp, 0.0), axis=-1, keepdims=True)
    return jnp.where(low, first, both - first)


def _split_pairs(xp):
    return jnp.concatenate([xp[:, :, :HN], xp[:, :, HN:]], axis=0)


def _join_pairs(xh):
    return jnp.concatenate([xh[:NP], xh[NP:]], axis=-1)


def _sgu_fn(p, ln_w, ln_b, sw, sbt):
    z = _gelu(p)
    u, v = z[:, :D], z[:, D:]
    mu = jnp.mean(v, axis=-1, keepdims=True)
    var = jnp.mean(jnp.square(v - mu), axis=-1, keepdims=True)
    vn = (v - mu) * lax.rsqrt(var + LN_EPS) * ln_w + ln_b
    ri = lax.broadcasted_iota(jnp.int32, (SGU_C, SGU_C), 0)
    ci = lax.broadcasted_iota(jnp.int32, (SGU_C, SGU_C), 1)
    mask = (ci <= ri).astype(F32)
    dg = D // SGU_G
    parts = []
    for g in range(SGU_G):
        parts.append(_bdot(sw[g] * mask, vn[:, g * dg:(g + 1) * dg]) + sbt[:, g:g + 1])
    return u * jnp.concatenate(parts, axis=-1)


def _pre_fn(qr, qk, qv, qxw, qxa, qxg, wl, w0, al, a0, gl, k_k, k_a):
    w = -jax.nn.softplus(-(w0 + _bdot(jnp.tanh(qxw), wl))) - 0.5
    lw = -jnp.exp(w)
    aa = _sigmoid(a0 + _bdot(qxa, al))
    g = _bdot(_sigmoid(qxg), gl)
    kk = _to_heads(qk * k_k)
    kk = kk / jnp.maximum(jnp.sqrt(_head_sum(kk * kk)), 1e-12)
    k2 = qk * (1.0 + (aa - 1.0) * k_a)
    return _to_heads(qr), _to_heads(lw), _to_heads(k2), _to_heads(qv), kk, _to_heads(aa), g


def _post_fn(o, r, k2, v, g, ln_w, ln_b, r_k):
    mu = _head_sum(o) * (1.0 / HN)
    d = o - mu
    var = _head_sum(d * d) * (1.0 / HN)
    on = d * lax.rsqrt(var + GN_EPS) * ln_w + ln_b
    bonus = _head_sum(r * k2 * r_k) * v
    return _from_heads(on + bonus) * g


def _gate_fn(pg, ya, yb):
    return _sigmoid(pg[:, :D]) * ya + _sigmoid(pg[:, D:]) * yb


def _bmm(x, y, cx, cy, out_path=False):
    return lax.dot_general(x, y, (((cx,), (cy,)), ((0,), (0,))),
                           precision=SCAN_OUT_PRECISION if out_path else SCAN_PRECISION, preferred_element_type=F32)


def _unit_lower_inverse(M):
    C = M.shape[1]
    ti = lax.broadcasted_iota(jnp.int32, (C, C), 0)
    tj = lax.broadcasted_iota(jnp.int32, (C, C), 1)
    eye = (ti == tj).astype(F32)
    same = lambda b: (ti // b == tj // b).astype(F32)
    X = -(M * same(SOLVE_B))
    inv = eye + X
    span = 1
    while 2 * span < SOLVE_B:
        X = _bmm(X, X, 2, 1)
        inv = inv + _bmm(inv, X, 2, 1)
        span *= 2
    b = SOLVE_B
    while b < C:
        low = M * (same(2 * b) - same(b))
        inv = inv - _bmm(_bmm(inv, low, 2, 1, out_path=True), inv, 2, 1, out_path=True)
        b *= 2
    return inv


@jax.custom_vjp
def _unit_lower_solve(inv, M, y):
    return _bmm(inv, y, 2, 1)


def _unit_lower_solve_fwd(inv, M, y):
    u = _bmm(inv, y, 2, 1)
    return u, (inv, u)


def _unit_lower_solve_bwd(res, du):
    inv, u = res
    dy = _bmm(inv, du, 1, 1)
    return jnp.zeros_like(inv), -_bmm(dy, u, 2, 2), dy


_unit_lower_solve.defvjp(_unit_lower_solve_fwd, _unit_lower_solve_bwd)


@jax.custom_vjp
def _unit_lower_solved(inv, u, M, y):
    return u


_unit_lower_solved.defvjp(lambda inv, u, M, y: (u, (inv, u)),
                          lambda res, du: (jnp.zeros_like(res[0]), jnp.zeros_like(res[1]))
                          + _unit_lower_solve_bwd(res, du)[1:])


@functools.partial(jax.custom_vjp, nondiff_argnums=(0,))
def _kept(fn, value, *args):
    return value


def _kept_fwd(fn, value, *args):
    return value, args


def _kept_bwd(fn, args, d):
    _, vjp = jax.vjp(fn, *args)
    return (jnp.zeros_like(d),) + tuple(vjp(d))


_kept.defvjp(_kept_fwd, _kept_bwd)


def _sum_over_time(x, reverse):
    C = x.shape[1]
    ti = lax.broadcasted_iota(jnp.int32, (C, C), 0)
    tj = lax.broadcasted_iota(jnp.int32, (C, C), 1)
    ones = jnp.broadcast_to(((tj >= ti) if reverse else (tj <= ti)).astype(BF16), (x.shape[0], C, C))
    hi = x.astype(BF16)
    r1 = x - hi.astype(F32)
    mid = r1.astype(BF16)
    lo = (r1 - mid.astype(F32)).astype(BF16)
    dn = (((2,), (1,)), ((0,), (0,)))
    return sum(lax.dot_general(ones, p, dn, preferred_element_type=F32) for p in (lo, mid, hi))


@jax.custom_vjp
def _time_cumsum(lw):
    return _sum_over_time(lw, reverse=False)


_time_cumsum.defvjp(lambda lw: (_sum_over_time(lw, reverse=False), None),
                    lambda _, d: (_sum_over_time(d, reverse=True),))


def _chunk_fn(S0, r, lw, k, v, kk, a, kept=None):
    C = SCAN_C
    bmm = _bmm
    ti = lax.broadcasted_iota(jnp.int32, (C, C), 0)
    tj = lax.broadcasted_iota(jnp.int32, (C, C), 1)
    incl2 = jnp.concatenate([(tj <= ti).astype(F32)] * 2, axis=1)
    strict = (tj < ti).astype(F32)
    n_mask = jnp.concatenate([jnp.zeros((C, C), F32), strict], axis=1)

    def known(name, fn, *args):
        return fn(*args) if kept is None else _kept(fn, kept[name], *args)

    cum = known("cum", _time_cumsum, lw)
    g_in, g_ex, g_inv = jnp.exp(cum), jnp.exp(cum - lw), jnp.exp(-cum)
    kkt, rt = kk * g_ex, r * g_in
    bk = jnp.concatenate([kk * a * g_inv, k * g_inv], axis=1)
    A = known("A", lambda x, y: bmm(x, y, 2, 2), kkt, bk)
    M = A[:, :, :C] * strict
    zv = jnp.concatenate([jnp.zeros_like(v), v], axis=1)
    s0_side = bmm(jnp.concatenate([kkt, rt], axis=1), S0, 2, 2, out_path=True)
    rhs = s0_side[:, :C] + bmm(A * n_mask, zv, 2, 1, out_path=True)
    if kept is None:
        inv = lax.stop_gradient(_unit_lower_inverse(M))
        y = _unit_lower_solve(inv, M, rhs)
    else:
        inv = kept["inv"]
        y = _unit_lower_solved(inv, kept["y"], M, rhs)
    z = jnp.concatenate([-y, v], axis=1)
    attn = known("attn", lambda x, y: bmm(x, y, 2, 2) * incl2, rt, bk)
    O = s0_side[:, C:] + bmm(attn, z, 2, 1, out_path=True)
    g_end = g_in[:, C - 1:C, :]
    S1 = S0 * g_end + bmm(z, bk * g_end, 1, 1, out_path=True)
    return O, S1, dict(cum=cum, A=A, attn=attn, y=y, inv=inv)


def _scan_fwd(r, lw, k, v, kk, a, ex=None, tb=256):
    assert SCAN_C == HN and 2 * SCAN_C == PW
    T = r.shape[1]
    tb = min(tb, T)
    n_chunks = tb // SCAN_C
    nb = T // tb
    nx = ex.nb if ex else 0

    def body(*refs):
        r_ref, lw_ref, k_ref, v_ref, kk_ref, a_ref = refs[:6]
        x_in, (o_ref, s0_ref), x_out = refs[6:6 + nx], refs[6 + nx:8 + nx], refs[8 + nx:8 + 2 * nx]
        s_ref, sems = refs[8 + 2 * nx], refs[9 + 2 * nx:]

        plan = ex.schedule(nb) if ex else []

        @pl.when(pl.program_id(0) == 0)
        def _():
            s_ref[...] = jnp.zeros_like(s_ref)
            for at, action in plan[:1]:
                action(x_in, x_out, sems)

        def step(c, carry):
            sl = pl.ds(pl.multiple_of(c * SCAN_C, SCAN_C), SCAN_C)
            S0 = s_ref[...]
            O, S1, keep = _chunk_fn(S0, *[_split_pairs(ref[:, sl, :])
                                          for ref in (r_ref, lw_ref, k_ref, v_ref, kk_ref, a_ref)])
            o_ref[:, sl, :] = _join_pairs(O)
            s0_ref[c, 0] = jnp.concatenate([S0, keep["inv"]], axis=-1)
            s0_ref[c, 1] = jnp.concatenate([keep["cum"], keep["y"]], axis=-1)
            s0_ref[c, 2] = keep["A"]
            s0_ref[c, 3] = keep["attn"]
            s_ref[...] = S1
            return carry

        lax.fori_loop(0, n_chunks, step, 0)

        for at, action in plan[1:]:
            pl.when(pl.program_id(0) == at)(functools.partial(action, x_in, x_out, sems))

    hm = pl.BlockSpec((NP, tb, PW), lambda i: (0, i, 0))
    res = pl.pallas_call(
        body, name="rwkv_scan_fwd", grid=(nb,), in_specs=[hm] * 6 + (ex.any_specs if ex else []),
        out_specs=[hm, pl.BlockSpec((n_chunks, N_KEPT, NH, HN, PW), lambda i: (i, 0, 0, 0, 0))]
        + (ex.any_specs if ex else []),
        out_shape=[jax.ShapeDtypeStruct((NP, T, PW), F32),
                   jax.ShapeDtypeStruct((T // SCAN_C, N_KEPT, NH, HN, PW), F32)]
        + (ex.out_shape if ex else []),
        scratch_shapes=[pltpu.VMEM((NH, HN, HN), F32)] + (ex.sem_shapes if ex else []),
        compiler_params=pltpu.CompilerParams(dimension_semantics=("arbitrary",), vmem_limit_bytes=VMEM_LIMIT),
    )(r, lw, k, v, kk, a, *(ex.bufs if ex else []))
    return res[0], res[1], list(res[2:])


def _scan_bwd(r, lw, k, v, kk, a, s0s, do, ex=None, tb=128):
    T = r.shape[1]
    tb = min(tb, T)
    n_chunks = tb // SCAN_C
    nb = T // tb
    nx = ex.nb if ex else 0

    def body(*refs):
        r_ref, lw_ref, k_ref, v_ref, kk_ref, a_ref, s0_ref, do_ref = refs[:8]
        x_in, (dr, dlw, dk, dv, dkk, da), x_out = refs[8:8 + nx], refs[8 + nx:14 + nx], refs[14 + nx:14 + 2 * nx]
        ds_ref, sems = refs[14 + 2 * nx], refs[15 + 2 * nx:]

        plan = ex.schedule(nb) if ex else []

        @pl.when(pl.program_id(0) == 0)
        def _():
            ds_ref[...] = jnp.zeros_like(ds_ref)
            for at, action in plan[:1]:
                action(x_in, x_out, sems)

        def step(j, carry):
            c = n_chunks - 1 - j
            sl = pl.ds(pl.multiple_of(c * SCAN_C, SCAN_C), SCAN_C)
            s0_inv, cum_y = s0_ref[c, 0], s0_ref[c, 1]
            kept = dict(inv=s0_inv[:, :, HN:], cum=cum_y[:, :, :HN], y=cum_y[:, :, HN:], A=s0_ref[c, 2],
                        attn=s0_ref[c, 3])
            _, vjp = jax.vjp(lambda *t: _chunk_fn(*t, kept=kept)[:2], s0_inv[:, :, :HN],
                             *[_split_pairs(ref[:, sl, :]) for ref in (r_ref, lw_ref, k_ref, v_ref, kk_ref, a_ref)])
            g = vjp((_split_pairs(do_ref[:, sl, :]), ds_ref[...]))
            ds_ref[...] = g[0]
            for ref, val in zip((dr, dlw, dk, dv, dkk, da), g[1:]):
                ref[:, sl, :] = _join_pairs(val)
            return carry

        lax.fori_loop(0, n_chunks, step, 0)

        for at, action in plan[1:]:
            pl.when(pl.program_id(0) == at)(functools.partial(action, x_in, x_out, sems))

    hm = pl.BlockSpec((NP, tb, PW), lambda i: (0, nb - 1 - i, 0))
    res = pl.pallas_call(
        body, name="rwkv_scan_bwd", grid=(nb,),
        in_specs=[hm] * 6 + [pl.BlockSpec((n_chunks, N_KEPT, NH, HN, PW), lambda i: (nb - 1 - i, 0, 0, 0, 0)), hm]
        + (ex.any_specs if ex else []),
        out_specs=[hm] * 6 + (ex.any_specs if ex else []),
        out_shape=[jax.ShapeDtypeStruct((NP, T, PW), F32)] * 6 + (ex.out_shape if ex else []),
        scratch_shapes=[pltpu.VMEM((NH, HN, HN), F32)] + (ex.sem_shapes if ex else []),
        compiler_params=pltpu.CompilerParams(dimension_semantics=("arbitrary",), vmem_limit_bytes=VMEM_LIMIT),
    )(r, lw, k, v, kk, a, s0s, do, *(ex.bufs if ex else []))
    return list(res[:6]), list(res[6:])


def _shift_down(i, p, prev8):
    first = jnp.where(i > 0, prev8[7:8, :], 0.0)
    row = lax.broadcasted_iota(jnp.int32, p.shape, 0)
    return jnp.where(row == 0, first, pltpu.roll(p, 1, axis=0))


def _mix_bwd(dq, p, sb, tm=256):
    def fn(i, n, dq, next8, p, prev8, sb):
        ps = _shift_down(i, p, prev8)
        d1 = dq * sb[1:2]
        last = jnp.where(i < n - 1, next8[0:1, :] * sb[1:2], 0.0)
        row = lax.broadcasted_iota(jnp.int32, dq.shape, 0)
        up = jnp.where(row == dq.shape[0] - 1, last, pltpu.roll(d1, dq.shape[0] - 1, axis=0))
        return (dq * sb[0:1] + up, jnp.sum(dq * p, axis=0, keepdims=True), jnp.sum(dq * ps, axis=0, keepdims=True))
    w = p.shape[1]
    return _rows_call("shift_mix_bwd", fn, [Rows(dq), Halo(dq, +1), Rows(p), Halo(p, -1)], [sb], [("rows", w, BF16)],
                      accs=[((1, w), F32), ((1, w), F32)], tm=tm, with_pid=True)


def _local_step(x, target, W, late_weights=None, early_grads=None, w_in_grads_ready=None, ffn_grads_ready=None):
    G = {}
    a = _rows_call("norm_mix_fwd", lambda x, g: (_rms(x, g),), [Rows(x)], [W["g_mix"]], [("rows", D, BF16)])[0]
    p_sgu = _matmul("proj_sgu", a, W["w_sgu_t"], "nt")
    def token_shift(p, sb0, sb1):
        row = lax.broadcasted_iota(jnp.int32, p.shape, 0)
        return p, p * sb0 + jnp.where(row == 0, 0.0, pltpu.roll(p, 1, axis=0)) * sb1
    p_rw, q = _matmul("proj_rwkv", a, W["w_rw_t"], "nt", tn=512, whole_rows=True, epilogue=token_shift,
                      extras=[W["sb"][0:1], W["sb"][1:2]], out_dtypes=(F32, F32))
    p_gate = _matmul("proj_gate", a, W["w_gate_t"], "nt")

    sgu_consts = [W["sgu_ln_w"], W["sgu_ln_b"], W["sgu_w"], W["sgu_bt"]]
    s = _rows_call("sgu_fwd", lambda *t: (_sgu_fn(*t),), [Rows(p_sgu)], sgu_consts, [("rows", D, BF16)], tm=SGU_C)[0]

    q_ins = [Rows(q, D, 0), Rows(q, D, 1), Rows(q, D, 2), Rows(q, 128, 24), Rows(q, 128, 25), Rows(q, 256, 13)]
    pre_consts = [W["w_lora"], W["w0"], W["a_lora"], W["a0"], W["g_lora"], W["k_k"], W["k_a"]]
    r_h, lw_h, k_h, v_h, kk_h, a_h, g_gate = _rows_call(
        "rwkv_pre_fwd", _pre_fn, q_ins, pre_consts, [("heads", F32)] * 6 + [("rows", D, F32)], tm=128)
    o_h, s0s, got = _scan_fwd(r_h, lw_h, k_h, v_h, kk_h, a_h, ex=late_weights[0] if late_weights else None)
    if late_weights:
        W = {**W, **late_weights[1](got)}
    y_a = _matmul("proj_a", s, W["w_proj_a"], "nn")
    post_ins = [Heads(o_h), Heads(r_h), Heads(k_h), Heads(v_h), Rows(g_gate)]
    post_consts = [W[n].reshape(NP, 1, PW) for n in ("ln_x_w", "ln_x_b", "r_k")]
    z_b = _rows_call("rwkv_post_fwd", lambda *t: (_post_fn(*t),), post_ins, post_consts, [("rows", D, BF16)], tm=128)[0]
    y_b, mixed = _matmul("proj_b", z_b, W["w_proj_b"], "nn", extras=[(p_gate, 0), (p_gate, 1), y_a],
                         epilogue=lambda yb, ga, gb, ya: (yb, _sigmoid(ga) * ya + _sigmoid(gb) * yb),
                         out_dtypes=(F32, BF16))

    def res1(mo, x, g):
        h1 = x + mo
        return h1, _rms(h1, g)
    h1, f = _matmul("proj_out", mixed, W["w_out"], "nn", extras=[x, W["g_ffn"]], epilogue=res1,
                    out_dtypes=(F32, BF16))

    def relu_sq(u):
        r = jnp.maximum(u, 0.0)
        return r, r * r
    r1, act = _matmul("ffn_up", f, W["w_ffn1"], "nn", epilogue=relu_sq, out_dtypes=(BF16, BF16))
    ff = _matmul("ffn_down", act, W["w_ffn2"], "nn")

    def head(h1, ff, tgt, g):
        def f_(h1, ff, g):
            y = _rms(h1 + ff, g)
            return 0.5 * jnp.sum(jnp.mean(jnp.square(y - tgt), axis=-1))
        loss, (dh2, _, dg) = jax.value_and_grad(f_, argnums=(0, 1, 2))(h1, ff, g)
        return dh2, jnp.full((8, LANES), loss, F32), dg
    dh2, loss_acc, G["g_final"] = _rows_call("loss_head", head, [Rows(h1), Rows(ff), Rows(target)], [W["g_final"]],
                                             [("rows", D, F32)], accs=[((8, LANES), F32), ((1, D), F32)])

    d_u1 = _matmul("ffn_down_dx", dh2, W["w_ffn2"], "nt", extras=[r1], out_dtypes=(BF16,),
                   epilogue=lambda d_act, r: (d_act * 2.0 * r.astype(F32),))[0]
    G["w_ffn2"] = _matmul("ffn_down_dw", act, dh2, "tn", out_dtype=GRAD_PAYLOAD)
    d_f = _matmul("ffn_up_dx", d_u1, W["w_ffn1"], "nt")
    G["w_ffn1"] = _matmul("ffn_up_dw", f, d_u1, "tn", out_blocks=N_DEV, out_dtype=GRAD_PAYLOAD)

    def res1_bwd(h1, d_f, dh2, g):
        _, vjp = jax.vjp(_rms, h1, g)
        dh, dg = vjp(d_f)
        return dh2 + dh, dg
    dh1, G["g_ffn"] = _rows_call("residual_norm_bwd", res1_bwd, [Rows(h1), Rows(d_f), Rows(dh2)],
                                 [W["g_ffn"]], [("rows", D, F32)], accs=[((1, D), F32)])
    ffn_token = ffn_grads_ready(G) if ffn_grads_ready else None
    def gate_bwd(d_mixed, ga, gb, ya, yb):
        _, vjp = jax.vjp(_gate_fn, jnp.concatenate([ga, gb], axis=-1), ya, yb)
        return vjp(d_mixed)
    d_gate, d_ya, d_yb = _matmul("proj_out_dx", dh1, W["w_out"], "nt", after=ffn_token, epilogue=gate_bwd,
                                 extras=[(p_gate, 0), (p_gate, 1), y_a, y_b], out_dtypes=(BF16, BF16, BF16),
                                 out_widths=(2 * D, D, D))
    G["w_out"] = _matmul("proj_out_dw", mixed, dh1, "tn", out_dtype=GRAD_PAYLOAD)

    d_s = _matmul("proj_a_dx", d_ya, W["w_proj_a"], "nt")
    G["w_proj_a"] = _matmul("proj_a_dw", s, d_ya, "tn", out_dtype=GRAD_PAYLOAD)

    def sgu_bwd(p, ds, *c):
        _, vjp = jax.vjp(_sgu_fn, p, *c)
        return vjp(ds)
    d_p_sgu, G["sgu_ln_w"], G["sgu_ln_b"], G["sgu_w"], G["sgu_bt"] = _rows_call(
        "sgu_bwd", sgu_bwd, [Rows(p_sgu), Rows(d_s)], sgu_consts, [("rows", 2 * D, BF16)],
        accs=[((1, D), F32), ((1, D), F32), ((SGU_G, SGU_C, SGU_C), F32), ((SGU_C, SGU_G), F32)], tm=SGU_C)

    d_zb = _matmul("proj_b_dx", d_yb, W["w_proj_b"], "nt")
    G["w_proj_b"] = _matmul("proj_b_dw", z_b, d_yb, "tn", out_dtype=GRAD_PAYLOAD)

    def post_bwd(o, r, k2, v, g, dz, *c):
        _, vjp = jax.vjp(_post_fn, o, r, k2, v, g, *c)
        return vjp(dz)
    do_h, dr1, dk1, dv1, d_g, g_lnw, g_lnb, g_rk = _rows_call(
        "rwkv_post_bwd", post_bwd, post_ins + [Rows(d_zb)], post_consts, [("heads", F32)] * 4 + [("rows", D, F32)],
        accs=[((NP, 1, PW), F32)] * 3, tm=128)
    G["ln_x_w"], G["ln_x_b"], G["r_k"] = (t.reshape(1, D) for t in (g_lnw, g_lnb, g_rk))
    (dr2, dlw, dk2, dv2, dkk, daa), early = _scan_bwd(r_h, lw_h, k_h, v_h, kk_h, a_h, s0s, do_h,
                                                      ex=early_grads(G) if early_grads else None)

    def pre_bwd(qr, qk, qv, qxw, qxa, qxg, dr1, dr2, dlw, dk1, dk2, dv1, dv2, dkk, daa, dg, *c):
        _, vjp = jax.vjp(_pre_fn, qr, qk, qv, qxw, qxa, qxg, *c)
        g = vjp((dr1 + dr2, dlw, dk1 + dk2, dv1 + dv2, dkk, daa, dg))
        dq = jnp.concatenate(g[:6], axis=-1)
        return (dq,) + tuple(g[6:])
    pre_b_ins = q_ins + [Heads(dr1), Heads(dr2), Heads(dlw), Heads(dk1), Heads(dk2), Heads(dv1), Heads(dv2),
                         Heads(dkk), Heads(daa), Rows(d_g)]
    d_q, G["w_lora"], G["w0"], G["a_lora"], G["a0"], G["g_lora"], G["k_k"], G["k_a"] = _rows_call(
        "rwkv_pre_bwd", pre_bwd, pre_b_ins, pre_consts, [("rows", RW_INT, F32)],
        accs=[((128, D), F32), ((1, D), F32), ((128, D), F32), ((1, D), F32), ((256, D), F32), ((1, D), F32),
              ((1, D), F32)], tm=128)
    d_p_rw, dsb0, dsb1 = _mix_bwd(d_q, p_rw, W["sb"])
    G["sb"] = jnp.concatenate([dsb0, dsb1], axis=0)

    G["w_sgu_t"] = _matmul("proj_sgu_dw", d_p_sgu, a, "tn", out_dtype=GRAD_PAYLOAD)
    G["w_rw_t"] = _matmul("proj_rwkv_dw", d_p_rw, a, "tn", out_dtype=GRAD_PAYLOAD)
    G["w_gate_t"] = _matmul("proj_gate_dw", d_gate, a, "tn", out_dtype=GRAD_PAYLOAD)
    token = w_in_grads_ready(G) if w_in_grads_ready else None
    da1 = _matmul("proj_sgu_dx", d_p_sgu, W["w_sgu_t"], "nn", after=token)
    da2 = _matmul("proj_rwkv_dx", d_p_rw, W["w_rw_t"], "nn", after=token)
    da3 = _matmul("proj_gate_dx", d_gate, W["w_gate_t"], "nn", after=token)

    def norm1_bwd(x, da1, da2, da3, dh1, g):
        _, vjp = jax.vjp(_rms, x, g)
        dx, dg = vjp(da1 + da2 + da3)
        return dh1 + dx, dg
    dx, G["g_mix"] = _rows_call("norm_mix_bwd", norm1_bwd, [Rows(x), Rows(da1), Rows(da2), Rows(da3), Rows(dh1)],
                                [W["g_mix"]], [("rows", D, F32)], accs=[((1, D), F32)])
    return loss_acc[0, 0], dx, G, early


class Exchange:
    def __init__(self, bufs, gathers):
        self.bufs, self.gathers, self.nb = list(bufs), list(gathers), len(bufs)
        self.any_specs = [pl.BlockSpec(memory_space=pl.ANY)] * self.nb
        self.out_shape = [jax.ShapeDtypeStruct((N_DEV,) + (b.shape if g else b.shape[1:]), b.dtype)
                          for b, g in zip(self.bufs, self.gathers)]
        n = (N_DEV - 1) * self.nb
        self.sem_shapes = [pltpu.SemaphoreType.DMA((n,)), pltpu.SemaphoreType.DMA((n,)),
                           pltpu.SemaphoreType.DMA((self.nb,))]

    def _copies(self, in_refs, out_refs, sems):
        send_sems, recv_sems, local_sems = sems
        x, y, c = lax.axis_index("x"), lax.axis_index("y"), lax.axis_index("c")
        me = 4 * x + 2 * y + c

        def src(b, dest):
            return in_refs[b] if self.gathers[b] else in_refs[b].at[dest]

        local = [pltpu.make_async_copy(src(b, me), out_refs[b].at[me], local_sems.at[b]) for b in range(self.nb)]
        sends, recvs = [], []
        for kbits in range(1, N_DEV):
            px = 1 - x if kbits & 4 else x
            py = 1 - y if kbits & 2 else y
            pc = 1 - c if kbits & 1 else c
            peer = 4 * px + 2 * py + pc
            for b in range(self.nb):
                s = (kbits - 1) * self.nb + b
                sends.append(pltpu.make_async_remote_copy(
                    src_ref=src(b, peer), dst_ref=out_refs[b].at[me], send_sem=send_sems.at[s],
                    recv_sem=recv_sems.at[s], device_id=(px, py, pc), device_id_type=pl.DeviceIdType.MESH))
                recvs.append(pltpu.make_async_remote_copy(
                    src_ref=src(b, peer), dst_ref=out_refs[b].at[peer], send_sem=send_sems.at[s],
                    recv_sem=recv_sems.at[s], device_id=(px, py, pc), device_id_type=pl.DeviceIdType.MESH))
        return local, sends, recvs

    def start(self, in_refs, out_refs, sems):
        local, sends, _ = self._copies(in_refs, out_refs, sems)
        for cp in sends + local:
            cp.start()

    def wait(self, in_refs, out_refs, sems):
        local, sends, recvs = self._copies(in_refs, out_refs, sems)
        for cp in recvs:
            cp.wait_recv()
        for cp in sends:
            cp.wait_send()
        for cp in local:
            cp.wait()

    def schedule(self, n_steps):
        return [(0, self.start), (n_steps - 1, self.wait)]


def _exchange(name, bufs, gather):
    ex = Exchange(bufs, gather if isinstance(gather, (list, tuple)) else [gather] * len(bufs))

    def body(*refs):
        in_refs, out_refs, sems = refs[:ex.nb], refs[ex.nb:2 * ex.nb], refs[2 * ex.nb:]
        ex.start(in_refs, out_refs, sems)
        ex.wait(in_refs, out_refs, sems)

    return pl.pallas_call(body, name=name, in_specs=ex.any_specs, out_specs=ex.any_specs, out_shape=ex.out_shape,
                          scratch_shapes=ex.sem_shapes)(*ex.bufs)


N_CHIP = 4


def _pair_exchange(name, blocks):
    def body(b_ref, got_ref, send_sems, recv_sems):
        x, y, c = lax.axis_index("x"), lax.axis_index("y"), lax.axis_index("c")
        copies = [pltpu.make_async_remote_copy(
            src_ref=b_ref.at[2 * q + 1 - c], dst_ref=got_ref.at[q], send_sem=send_sems.at[q],
            recv_sem=recv_sems.at[q], device_id=(x, y, 1 - c), device_id_type=pl.DeviceIdType.MESH)
            for q in range(N_CHIP)]
        for cp in copies:
            cp.start()
        for cp in copies:
            cp.wait_recv()
        for cp in copies:
            cp.wait_send()

    any_spec = pl.BlockSpec(memory_space=pl.ANY)
    return pl.pallas_call(
        body, name=name, in_specs=[any_spec], out_specs=any_spec,
        out_shape=jax.ShapeDtypeStruct((N_CHIP,) + blocks.shape[1:], blocks.dtype),
        scratch_shapes=[pltpu.SemaphoreType.DMA((N_CHIP,))] * 2,
    )(blocks)


def _pair_sum(name, blocks, got, core):
    _, R, Wd = blocks.shape

    def body(c_ref, a_ref, b_ref, o_ref):
        o_ref[...] = (a_ref[...].astype(F32) + b_ref[...].astype(F32)).astype(o_ref.dtype)

    return pl.pallas_call(
        body, name=name,
        grid_spec=pltpu.PrefetchScalarGridSpec(
            num_scalar_prefetch=1, grid=(N_CHIP,),
            in_specs=[pl.BlockSpec((None, R, Wd), lambda q, c_ref: (2 * q + c_ref[0], 0, 0)),
                      pl.BlockSpec((None, R, Wd), lambda q, c_ref: (q, 0, 0))],
            out_specs=pl.BlockSpec((None, R, Wd), lambda q, c_ref: (q, 0, 0))),
        out_shape=jax.ShapeDtypeStruct(got.shape, blocks.dtype),
        compiler_params=pltpu.CompilerParams(dimension_semantics=("parallel",), vmem_limit_bytes=VMEM_LIMIT),
    )(core, blocks, got)


def _chip_copies(s_ref, land_ref, send_sems, recv_sems):
    x, y, c = lax.axis_index("x"), lax.axis_index("y"), lax.axis_index("c")
    my_q = 2 * x + y
    sends, recvs = [], []
    for kbits in range(1, N_CHIP):
        px = 1 - x if kbits & 2 else x
        py = 1 - y if kbits & 1 else y
        peer_q = 2 * px + py
        sends.append(pltpu.make_async_remote_copy(
            src_ref=s_ref.at[peer_q], dst_ref=land_ref.at[my_q], send_sem=send_sems[kbits - 1],
            recv_sem=recv_sems[kbits - 1], device_id=(px, py, c), device_id_type=pl.DeviceIdType.MESH))
        recvs.append(pltpu.make_async_remote_copy(
            src_ref=s_ref.at[peer_q], dst_ref=land_ref.at[peer_q], send_sem=send_sems[kbits - 1],
            recv_sem=recv_sems[kbits - 1], device_id=(px, py, c), device_id_type=pl.DeviceIdType.MESH))
    return sends, recvs


_HBM = pl.BlockSpec(memory_space=pltpu.HBM)
_SEM = pl.BlockSpec(memory_space=pltpu.SEMAPHORE)
N_CHIP_SEMS = 2 * (N_CHIP - 1)


def _scatter_copies(b_refs, land_refs, send_sems, recv_sems):
    x, y, c = lax.axis_index("x"), lax.axis_index("y"), lax.axis_index("c")
    me = 4 * x + 2 * y + c
    sends, recvs = [], []
    for kbits in range(1, N_DEV):
        px = 1 - x if kbits & 4 else x
        py = 1 - y if kbits & 2 else y
        pc = 1 - c if kbits & 1 else c
        peer = 4 * px + 2 * py + pc
        for b in range(len(b_refs)):
            s = (kbits - 1) * len(b_refs) + b
            sends.append(pltpu.make_async_remote_copy(
                src_ref=b_refs[b].at[peer], dst_ref=land_refs[b].at[me], send_sem=send_sems[s],
                recv_sem=recv_sems[s], device_id=(px, py, pc), device_id_type=pl.DeviceIdType.MESH))
            recvs.append(pltpu.make_async_remote_copy(
                src_ref=b_refs[b].at[peer], dst_ref=land_refs[b].at[peer], send_sem=send_sems[s],
                recv_sem=recv_sems[s], device_id=(px, py, pc), device_id_type=pl.DeviceIdType.MESH))
    return sends, recvs


def _scatter_start(name, bufs):
    nb = len(bufs)
    n = (N_DEV - 1) * nb

    def body(*refs):
        b_refs, land_refs, outs = refs[:nb], refs[nb:2 * nb], refs[2 * nb:]
        sends, _ = _scatter_copies(b_refs, land_refs, outs[:n], outs[n:2 * n])
        for cp in sends:
            cp.start()
        token = outs[2 * n + 2 * nb]
        token[...] = jnp.zeros_like(token)

    thru = tuple(pltpu.HBM(b.shape, b.dtype) for b in bufs)
    res = pl.pallas_call(
        body, name=name, in_specs=(_HBM,) * (2 * nb),
        out_specs=(_SEM,) * (2 * n) + (_HBM,) * (2 * nb) + (pl.BlockSpec(memory_space=pltpu.VMEM),),
        out_shape=(pltpu.SemaphoreType.DMA(()),) * (2 * n) + thru + thru + (jax.ShapeDtypeStruct((8, LANES), F32),),
        input_output_aliases={i: 2 * n + i for i in range(2 * nb)},
        compiler_params=pltpu.CompilerParams(has_side_effects=pltpu.SideEffectType.DATAFLOW_SIDE_EFFECTING),
    )(*[pltpu.with_memory_space_constraint(b, pltpu.HBM) for b in bufs],
      *[pltpu.with_memory_space_constraint(lax.empty(b.shape, b.dtype), pltpu.HBM) for b in bufs])
    return res[:2 * n], list(res[2 * n:2 * n + nb]), list(res[2 * n + nb:2 * n + 2 * nb]), res[2 * n + 2 * nb]


def _scatter_wait(name, sems, bufs_thru, lands_thru, after):
    nb = len(bufs_thru)
    n = (N_DEV - 1) * nb

    def body(*refs):
        b_refs, land_refs, sem_refs = refs[:nb], refs[nb:2 * nb], refs[2 * nb:2 * nb + 2 * n]
        sends, recvs = _scatter_copies(b_refs, land_refs, sem_refs[:n], sem_refs[n:])
        for cp in sends:
            cp.wait_send()
        for cp in recvs:
            cp.wait_recv()

    thru = tuple(pltpu.HBM(b.shape, b.dtype) for b in bufs_thru)
    res = pl.pallas_call(
        body, name=name, in_specs=(_HBM,) * (2 * nb) + (_SEM,) * (2 * n) + (pl.BlockSpec(memory_space=pl.ANY),),
        out_specs=(_HBM,) * (2 * nb), out_shape=thru + thru,
        input_output_aliases={i: i for i in range(2 * nb)},
        compiler_params=pltpu.CompilerParams(has_side_effects=pltpu.SideEffectType.DATAFLOW_SIDE_EFFECTING),
    )(*bufs_thru, *lands_thru, *sems, after)
    return list(res[:nb]), list(res[nb:])


def _chip_exchange_start(name, sums):
    def body(s_ref, land_ref, *outs):
        sems, token = outs[:N_CHIP_SEMS], outs[N_CHIP_SEMS + 2]
        sends, _ = _chip_copies(s_ref, land_ref, sems[:N_CHIP - 1], sems[N_CHIP - 1:])
        for cp in sends:
            cp.start()
        token[...] = jnp.zeros_like(token)

    res = pl.pallas_call(
        body, name=name, in_specs=(_HBM, _HBM),
        out_specs=(_SEM,) * N_CHIP_SEMS + (_HBM, _HBM, pl.BlockSpec(memory_space=pltpu.VMEM)),
        out_shape=(pltpu.SemaphoreType.DMA(()),) * N_CHIP_SEMS
        + (pltpu.HBM(sums.shape, sums.dtype), pltpu.HBM(sums.shape, sums.dtype), jax.ShapeDtypeStruct((8, LANES), F32)),
        input_output_aliases={0: N_CHIP_SEMS, 1: N_CHIP_SEMS + 1},
        compiler_params=pltpu.CompilerParams(has_side_effects=pltpu.SideEffectType.DATAFLOW_SIDE_EFFECTING),
    )(pltpu.with_memory_space_constraint(sums, pltpu.HBM),
      pltpu.with_memory_space_constraint(lax.empty(sums.shape, sums.dtype), pltpu.HBM))
    return res[:N_CHIP_SEMS], res[N_CHIP_SEMS], res[N_CHIP_SEMS + 1], res[N_CHIP_SEMS + 2]


def _chip_exchange_wait(name, sems, sums_thru, land_thru, after):
    def body(s_ref, land_ref, *rest):
        sems = rest[:N_CHIP_SEMS]
        sends, recvs = _chip_copies(s_ref, land_ref, sems[:N_CHIP - 1], sems[N_CHIP - 1:])
        for cp in sends:
            cp.wait_send()
        for cp in recvs:
            cp.wait_recv()

    return pl.pallas_call(
        body, name=name, in_specs=(_HBM, _HBM) + (_SEM,) * N_CHIP_SEMS + (pl.BlockSpec(memory_space=pl.ANY),),
        out_specs=(_HBM, _HBM),
        out_shape=(pltpu.HBM(sums_thru.shape, sums_thru.dtype), pltpu.HBM(sums_thru.shape, sums_thru.dtype)),
        input_output_aliases={0: 0, 1: 1},
        compiler_params=pltpu.CompilerParams(has_side_effects=pltpu.SideEffectType.DATAFLOW_SIDE_EFFECTING),
    )(sums_thru, land_thru, *sems, after)


def _adamw(name, slots, w, m, v, tr=256):
    unit_mid = w.ndim == 3 and w.shape[1] == 1 and w.shape[0] > 1
    R, Wd = (w.shape[0], w.shape[2]) if unit_mid else w.shape[-2:]
    depth_axis = w.ndim == 3 and not unit_mid
    if R % tr == 0:
        tc = Wd
    else:
        tr, tc = R, (256 if (Wd % 256 == 0 and R > 256) else Wd)
    at = (slice(None), 0, slice(None)) if unit_mid else Ellipsis

    def body(s_ref, w_ref, m_ref, v_ref, g_out, d_out, m_out, v_out):
        g = s_ref[0].astype(F32)
        for j in range(1, slots.shape[0]):
            g = g + s_ref[j].astype(F32)
        m_new = ADAM_B1 * m_ref[at] + (1.0 - ADAM_B1) * g
        v_new = ADAM_B2 * v_ref[at] + (1.0 - ADAM_B2) * jnp.square(g)
        m_hat = m_new / (1.0 - ADAM_B1 ** ADAM_STEP)
        v_hat = v_new / (1.0 - ADAM_B2 ** ADAM_STEP)
        g_out[at] = g
        d_out[at] = -ADAM_LR * (m_hat / (jnp.sqrt(v_hat) + ADAM_EPS) + ADAM_WD * w_ref[at])
        m_out[at] = m_new
        v_out[at] = v_new

    if unit_mid:
        row = pl.BlockSpec((tr, 1, tc), lambda i, j: (i, 0, j))
    elif depth_axis:
        row = pl.BlockSpec((None, tr, tc), lambda i, j: (0, i, j))
    else:
        row = pl.BlockSpec((tr, tc), lambda i, j: (i, j))
    return pl.pallas_call(
        body, name=name, grid=(R // tr, Wd // tc),
        in_specs=[pl.BlockSpec((slots.shape[0], tr, tc), lambda i, j: (0, i, j)), row, row, row],
        out_specs=[row] * 4, out_shape=[jax.ShapeDtypeStruct(w.shape, F32)] * 4,
        compiler_params=pltpu.CompilerParams(dimension_semantics=("parallel", "parallel"),
                                             vmem_limit_bytes=VMEM_LIMIT),
    )(slots, w, m, v)


PACK_W = 1024
PACKED = [(n, s) for n, s in REPLICATED if n != "sgu_w"]
_SMALL_SIZES = [int(np.prod(s)) for _, s in PACKED]
_SMALL_ROWS = _round_up(_round_up(sum(_SMALL_SIZES) + PACK_W, PACK_W) // PACK_W, 8)
_LOSS_AT = sum(_SMALL_SIZES)
W_IN_SHARD = P_TOTAL // N_DEV


def _pack_rows(parts, rows, dtype):
    flat = jnp.concatenate([p.reshape(-1).astype(dtype) for p in parts])
    return jnp.pad(flat, (0, rows * PACK_W - flat.shape[0])).reshape(rows, PACK_W)


def _w_in_groups_t(blocks):
    wt = blocks.reshape(P_TOTAL, D)
    o, c = 2 * D, 2 * D + 3 * D
    z = lambda r: jnp.zeros((r, D), wt.dtype)
    rw = jnp.concatenate([wt[o:c], wt[c:c + L_W], z(128 - L_W), wt[c + L_W:c + L_W + L_A], z(128 - L_A),
                          wt[c + L_W + L_A:o + C_B], z(256 - L_G)], axis=0)
    return wt[:o], rw, wt[o + C_B:]


def _w_in_grad_blocks(g_sgu_t, g_rw_t, g_gate_t):
    c = 3 * D
    full = jnp.concatenate([g_sgu_t, g_rw_t[:c], g_rw_t[c:c + L_W], g_rw_t[c + 128:c + 128 + L_A],
                            g_rw_t[c + 256:c + 256 + L_G], g_gate_t], axis=0)
    return full.reshape(N_DEV, W_IN_SHARD, D)


def _mesh_index():
    me = 4 * lax.axis_index("x") + 2 * lax.axis_index("y") + lax.axis_index("c")
    return me.astype(jnp.int32).reshape(1)


def _fill_slot(name, dst, src, idx, src_idx=None):
    R, Wd = dst.shape[1:]
    scalars = [idx] if src_idx is None else [idx, src_idx]
    if src_idx is None:
        src_spec = pl.BlockSpec((R, Wd), lambda i, *s: (0, 0))
    else:
        src_spec = pl.BlockSpec((None, R, Wd), lambda i, *s: (s[1][0], 0, 0))

    def body(*refs):
        src_ref, out_ref = refs[len(scalars) + 1], refs[len(scalars) + 2]
        out_ref[...] = src_ref[...]

    return pl.pallas_call(
        body, name=name,
        grid_spec=pltpu.PrefetchScalarGridSpec(
            num_scalar_prefetch=len(scalars), grid=(1,),
            in_specs=[pl.BlockSpec(memory_space=pl.ANY), src_spec],
            out_specs=pl.BlockSpec((None, R, Wd), lambda i, *s: (s[0][0], 0, 0))),
        out_shape=jax.ShapeDtypeStruct(dst.shape, dst.dtype),
        input_output_aliases={len(scalars): 0},
        compiler_params=pltpu.CompilerParams(vmem_limit_bytes=VMEM_LIMIT),
    )(*scalars, dst, src)


class TwoLevelGather:
    def __init__(self, bufs, skip_own=()):
        self.bufs, self.nb, self.skip_own = list(bufs), len(bufs), tuple(skip_own)
        self.any_specs = [pl.BlockSpec(memory_space=pl.ANY)] * self.nb
        self.out_shape = [jax.ShapeDtypeStruct((N_DEV,) + b.shape, b.dtype) for b in self.bufs]
        self.sem_shapes = [pltpu.SemaphoreType.DMA((7 * self.nb,)), pltpu.SemaphoreType.DMA((7 * self.nb,)),
                           pltpu.SemaphoreType.DMA((self.nb,))]

    def _copies(self, in_refs, out_refs, sems):
        send_sems, recv_sems, local_sems = sems
        nb = self.nb
        x, y, c = lax.axis_index("x"), lax.axis_index("y"), lax.axis_index("c")
        me, sibling = (x, y, c), (x, y, 1 - c)
        chips = [(1 - x, y), (x, 1 - y), (1 - x, 1 - y)]

        def slot(b, dev):
            return out_refs[b].at[4 * dev[0] + 2 * dev[1] + dev[2]]

        def copy(b, k, block, to, own=False):
            return pltpu.make_async_remote_copy(
                src_ref=in_refs[b] if own else slot(b, block), dst_ref=slot(b, block),
                send_sem=send_sems.at[7 * b + k], recv_sem=recv_sems.at[7 * b + k], device_id=to,
                device_id_type=pl.DeviceIdType.MESH)

        cp = {}
        cp["local"] = [pltpu.make_async_copy(in_refs[b], slot(b, me), local_sems.at[b]) for b in range(nb)
                       if b not in self.skip_own]
        cp["first"] = [copy(b, 0, me, sibling, own=True) for b in range(nb)]
        cp["first"] += [copy(b, 1 + j, me, (*chip, c), own=True) for j, chip in enumerate(chips) for b in range(nb)]
        cp["over_ici"] = [copy(b, 1 + j, (*chip, c), me) for j, chip in enumerate(chips) for b in range(nb)]
        cp["passed"] = [copy(b, 4 + j, (*chip, c), sibling) for j, chip in enumerate(chips) for b in range(nb)]
        cp["from_sibling"] = [copy(b, 0, sibling, me) for b in range(nb)]
        cp["from_sibling"] += [copy(b, 4 + j, (*chip, 1 - c), me) for j, chip in enumerate(chips) for b in range(nb)]
        return cp

    def start(self, in_refs, out_refs, sems):
        cp = self._copies(in_refs, out_refs, sems)
        for c in cp["first"] + cp["local"]:
            c.start()

    def forward(self, in_refs, out_refs, sems):
        cp = self._copies(in_refs, out_refs, sems)
        for arrived, onward in zip(cp["over_ici"], cp["passed"]):
            arrived.wait_recv()
            onward.start()

    def finish(self, in_refs, out_refs, sems):
        cp = self._copies(in_refs, out_refs, sems)
        for c in cp["from_sibling"]:
            c.wait_recv()
        for c in cp["first"] + cp["passed"]:
            c.wait_send()
        for c in cp["local"]:
            c.wait()

    def schedule(self, n_steps):
        return [(0, self.start), (max(n_steps - 3, 0), self.forward), (n_steps - 1, self.finish)]


def _all_gather_two_level(name, bufs, skip_own=()):
    ex = TwoLevelGather(bufs, skip_own)

    def body(*refs):
        args = refs[:ex.nb], refs[ex.nb:2 * ex.nb], refs[2 * ex.nb:]
        ex.start(*args)
        ex.forward(*args)
        ex.finish(*args)

    return pl.pallas_call(body, name=name, in_specs=ex.any_specs, out_specs=ex.any_specs, out_shape=ex.out_shape,
                          scratch_shapes=ex.sem_shapes)(*ex.bufs)


def _cols_from_blocks(blk):
    return jnp.transpose(blk, (1, 0, 2)).reshape(blk.shape[1], -1)


def _cols_to_blocks(g):
    r, c = g.shape
    return jnp.transpose(g.reshape(r, N_DEV, c // N_DEV), (1, 0, 2))


FIRST_WEIGHTS = ["w_in", "shift_b", "w_lora_w", "a_lora_w", "g_lora_w"]
LATE_WEIGHTS = ["w_proj_a", "w_proj_b", "w_out", "w_ffn1", "w_ffn2"]
SCAN_CARRIED = ["w_proj_a", "w_proj_b", "w_out"]
FFN_WEIGHTS = ["w_ffn1", "w_ffn2"]


def _late_weights(shards):
    ex = TwoLevelGather([shards[n].astype(BF16) for n in LATE_WEIGHTS])

    def finish(results):
        got = dict(zip(LATE_WEIGHTS, results))
        W = {n: got[n].reshape(-1, D) for n in ("w_proj_a", "w_proj_b", "w_out", "w_ffn2")}
        W["w_ffn1"] = got["w_ffn1"].reshape(N_DEV, D, -1)
        return W
    return ex, finish


def _gather_weights(shards):
    def payload(n):
        if n == "w_in":
            return jnp.transpose(shards[n][0]).astype(BF16)
        return shards[n] if n == "shift_b" else shards[n].astype(BF16)
    payloads = [payload(n) for n in FIRST_WEIGHTS]
    got = dict(zip(FIRST_WEIGHTS, _all_gather_two_level("weight_all_gather", payloads, skip_own=(0,))))
    got["w_in"] = _fill_slot("w_in_own_slot", got["w_in"], payloads[0], _mesh_index())
    W = {}
    W["w_sgu_t"], W["w_rw_t"], W["w_gate_t"] = _w_in_groups_t(got["w_in"])
    z = lambda r, c, dt: jnp.zeros((r, c), dt)
    W["w_lora"] = jnp.concatenate([_cols_from_blocks(got["w_lora_w"][:, 0]).astype(F32), z(128 - L_W, D, F32)], axis=0)
    W["a_lora"] = jnp.concatenate([_cols_from_blocks(got["a_lora_w"][:, 0]).astype(F32), z(128 - L_A, D, F32)], axis=0)
    W["g_lora"] = jnp.concatenate([_cols_from_blocks(got["g_lora_w"][:, 0]).astype(F32), z(256 - L_G, D, F32)], axis=0)
    sb = _cols_from_blocks(got["shift_b"][:, 0])
    W["sb"] = jnp.concatenate([sb[:, :3 * D], sb[:, 3 * D:3 * D + L_W], z(2, 128 - L_W, F32),
                               sb[:, 3 * D + L_W:3 * D + L_W + L_A], z(2, 128 - L_A, F32),
                               sb[:, 3 * D + L_W + L_A:], z(2, 256 - L_G, F32)], axis=1)
    return W


def _replicated_weights(rep):
    W = {n: rep[n] for n in ("g_mix", "sgu_ln_w", "sgu_ln_b", "w0", "a0", "k_k", "k_a", "r_k", "ln_x_w", "ln_x_b",
                             "g_ffn")}
    W["g_final"] = rep["g_final"].reshape(1, D)
    W["sgu_w"] = rep["sgu_w"][0]
    W["sgu_bt"] = jnp.transpose(rep["sgu_b"][0])
    return W


def _late_grad_blocks(G):
    return Exchange([G[n].reshape(N_DEV, -1, D) for n in SCAN_CARRIED]
                    + [G["sgu_w"].reshape(SGU_G * SGU_C, SGU_C).astype(GRAD_PAYLOAD)],
                    [False] * len(SCAN_CARRIED) + [True])


def _first_grad_blocks(G):
    sbg = G["sb"]
    c = 3 * D
    sb = jnp.concatenate([sbg[:, :c], sbg[:, c:c + L_W], sbg[:, c + 128:c + 128 + L_A],
                          sbg[:, c + 256:c + 256 + L_G]], axis=1)
    return {
        "shift_b": _cols_to_blocks(sb),
        "w_lora_w": _cols_to_blocks(G["w_lora"][:L_W]), "a_lora_w": _cols_to_blocks(G["a_lora"][:L_A]),
        "g_lora_w": _cols_to_blocks(G["g_lora"][:L_G]),
    }


def _replicated_grads(G):
    small = {n: G[n] for n in ("g_mix", "sgu_ln_w", "sgu_ln_b", "w0", "a0", "k_k", "k_a", "r_k", "ln_x_w", "ln_x_b",
                               "g_ffn", "g_final")}
    small["sgu_w"] = G["sgu_w"]
    small["sgu_b"] = jnp.transpose(G["sgu_bt"])
    return small


def kernel(x, g_mix, w_in, sgu_ln_w, sgu_ln_b, sgu_w, sgu_b, w_proj_a, shift_b, w_lora_w, w0, a_lora_w, a0, g_lora_w, k_k, k_a, r_k, ln_x_w, ln_x_b, w_proj_b, w_out, g_ffn, w_ffn1, w_ffn2, g_final, loss_target, m_g_mix, m_w_in, m_sgu_ln_w, m_sgu_ln_b, m_sgu_w, m_sgu_b, m_w_proj_a, m_shift_b, m_w_lora_w, m_w0, m_a_lora_w, m_a0, m_g_lora_w, m_k_k, m_k_a, m_r_k, m_ln_x_w, m_ln_x_b, m_w_proj_b, m_w_out, m_g_ffn, m_w_ffn1, m_w_ffn2, m_g_final, v_g_mix, v_w_in, v_sgu_ln_w, v_sgu_ln_b, v_sgu_w, v_sgu_b, v_w_proj_a, v_shift_b, v_w_lora_w, v_w0, v_a_lora_w, v_a0, v_g_lora_w, v_k_k, v_k_a, v_r_k, v_ln_x_w, v_ln_x_b, v_w_proj_b, v_w_out, v_g_ffn, v_w_ffn1, v_w_ffn2, v_g_final):
    env = dict(locals())
    weights = {n: env[n] for n in WEIGHT_ORDER}
    moms = {n: env["m_" + n] for n in WEIGHT_ORDER}
    vars_ = {n: env["v_" + n] for n in WEIGHT_ORDER}

    shards = {n: weights[n] for n, _, _ in SHARDED}
    W = _gather_weights(shards)
    W.update(_replicated_weights({n: weights[n] for n, _ in REPLICATED}))
    in_flight = {}

    def send_w_in_grads(G):
        blocks = _w_in_grad_blocks(G["w_sgu_t"], G["w_rw_t"], G["w_gate_t"])
        got = _pair_exchange("grad_pair_exchange", blocks)
        core = lax.axis_index("c").astype(jnp.int32).reshape(1)
        sums = _pair_sum("grad_pair_sum", blocks, got, core)
        in_flight["sems"], in_flight["sums"], in_flight["land"], token = _chip_exchange_start("grad_chip_start", sums)
        return token

    def send_ffn_grads(G):
        in_flight["ffn"] = _scatter_start("grad_ffn_start", [G["w_ffn1"], G["w_ffn2"].reshape(N_DEV, -1, D)])
        return in_flight["ffn"][3]

    loss_part, dx, G, late_slots = _local_step(x[0], loss_target[0], W, late_weights=_late_weights(shards),
                                               early_grads=_late_grad_blocks, w_in_grads_ready=send_w_in_grads,
                                               ffn_grads_ready=send_ffn_grads)

    slots = dict(zip(SCAN_CARRIED, late_slots))
    me = _mesh_index()
    sent, landed = _scatter_wait("grad_ffn_wait", *in_flight["ffn"][:3], after=dx)
    for i, n in enumerate(FFN_WEIGHTS):
        slots[n] = _fill_slot("grad_own_slot_" + n, landed[i], sent[i], me, src_idx=me)
    blocks = _first_grad_blocks(G)
    small = _replicated_grads(G)
    small_parts = [small[n] for n, _ in PACKED] + [jnp.full((PACK_W,), loss_part, F32)]
    rest = [n for n in FIRST_WEIGHTS if n != "w_in"]
    res = _exchange("grad_exchange", [blocks[n] for n in rest] + [_pack_rows(small_parts, _SMALL_ROWS, F32)],
                    [False] * len(rest) + [True])
    slots.update(zip(rest, res[:-1]))
    small_slots = res[-1]
    sums, chip_slots = _chip_exchange_wait("grad_chip_wait", in_flight["sems"], in_flight["sums"], in_flight["land"],
                                           after=small_slots)
    my_chip = (2 * lax.axis_index("x") + lax.axis_index("y")).astype(jnp.int32).reshape(1)
    slots["w_in"] = _fill_slot("grad_own_slot", chip_slots, sums, my_chip, src_idx=my_chip)

    outs = [dict(), dict(), dict(), dict()]
    for n, _, _ in SHARDED:
        if n == "w_in":
            res = _adamw("adamw_" + n, slots[n], *[jnp.transpose(t, (2, 0, 1)) for t in (weights[n], moms[n], vars_[n])])
            res = [jnp.transpose(t, (1, 2, 0)) for t in res]
        else:
            res = _adamw("adamw_" + n, slots[n], weights[n], moms[n], vars_[n])
        for k in range(4):
            outs[k][n] = res[k]

    sgu_shape = (SGU_G * SGU_C, SGU_C)
    res = _adamw("adamw_sgu_w", late_slots[len(SCAN_CARRIED)], *[t.reshape(sgu_shape) for t in
                                                                  (weights["sgu_w"], moms["sgu_w"], vars_["sgu_w"])])
    for k in range(4):
        outs[k]["sgu_w"] = res[k].reshape(weights["sgu_w"].shape)

    def packed(d):
        return _pack_rows([d[n] for n, _ in PACKED], _SMALL_ROWS, F32)
    small_out = _adamw("adamw_replicated", small_slots, packed(weights), packed(moms), packed(vars_))
    for k in range(4):
        flat = small_out[k].reshape(-1)
        off = 0
        for (n, s), size in zip(PACKED, _SMALL_SIZES):
            outs[k][n] = flat[off:off + size].reshape(s)
            off += size
    loss = small_out[0].reshape(-1)[_LOSS_AT]
    return (loss, dx[None], *[outs[0][n] for n in WEIGHT_ORDER], *[outs[1][n] for n in WEIGHT_ORDER],
            *[outs[2][n] for n in WEIGHT_ORDER], *[outs[3][n] for n in WEIGHT_ORDER])
```

```python
import functools
import numpy as np
import jax
import jax.numpy as jnp
from jax import lax
from jax.experimental import pallas as pl
from jax.experimental.pallas import tpu as pltpu

F32 = jnp.float32
BF16 = jnp.bfloat16

D = 1024
NH, HN = 16, 64
NP, PW = NH // 2, 2 * HN
SGU_G, SGU_C = 8, 128
L_W, L_A, L_G = 64, 64, 160
C_B = 3 * D + L_W + L_A + L_G
P_TOTAL = 2 * D + C_B + 2 * D
D_FF = 4 * D
RW_INT = 3 * D + 128 + 128 + 256
NORM_EPS, LN_EPS, GN_EPS = 1e-6, 1e-5, 64e-5
N_DEV = 8
LANES = 128
SCAN_C = 64
N_KEPT = 4
SOLVE_B = 16
SCAN_PRECISION = lax.Precision.HIGH
SCAN_OUT_PRECISION = lax.Precision.DEFAULT
GRAD_PAYLOAD = BF16
VMEM_LIMIT = 56 * 1024 * 1024
MATMUL_VMEM_BUDGET = 40 * 1024 * 1024
STEP_COST_BYTES = 512 * 1024
HBM_COST_RATIO = 3
ACC_PASS_WEIGHT = 4

ADAM_LR, ADAM_B1, ADAM_B2, ADAM_EPS, ADAM_WD, ADAM_STEP = 0.001, 0.9, 0.999, 1e-08, 0.01, 10

SHARDED = [
    ("w_in", (D, P_TOTAL), 1), ("w_proj_a", (D, D), 0), ("shift_b", (2, C_B), 1), ("w_lora_w", (L_W, D), 1),
    ("a_lora_w", (L_A, D), 1), ("g_lora_w", (L_G, D), 1), ("w_proj_b", (D, D), 0), ("w_out", (D, D), 0),
    ("w_ffn1", (D, D_FF), 1), ("w_ffn2", (D_FF, D), 0),
]
REPLICATED = [
    ("g_mix", (1, D)), ("sgu_ln_w", (1, D)), ("sgu_ln_b", (1, D)), ("sgu_w", (1, SGU_G, SGU_C, SGU_C)),
    ("sgu_b", (1, SGU_G, SGU_C)), ("w0", (1, D)), ("a0", (1, D)), ("k_k", (1, D)), ("k_a", (1, D)), ("r_k", (1, D)),
    ("ln_x_w", (1, D)), ("ln_x_b", (1, D)), ("g_ffn", (1, D)), ("g_final", (D,)),
]
WEIGHT_ORDER = ["g_mix", "w_in", "sgu_ln_w", "sgu_ln_b", "sgu_w", "sgu_b", "w_proj_a", "shift_b", "w_lora_w", "w0",
                "a_lora_w", "a0", "g_lora_w", "k_k", "k_a", "r_k", "ln_x_w", "ln_x_b", "w_proj_b", "w_out", "g_ffn",
                "w_ffn1", "w_ffn2", "g_final"]


def _round_up(n, m):
    return (n + m - 1) // m * m


def _pick(n, target):
    if n <= target:
        return n
    best = None
    for t in range(LANES, target + 1, LANES):
        if n % t == 0:
            best = t
    assert best is not None, (n, target)
    return best


def _matmul(name, a, b, mode, out_dtype=F32, tm=2048, tn=1024, tk=4096, out_blocks=None, epilogue=None, extras=(),
            out_dtypes=(), after=None, whole_rows=False, out_widths=None):
    b_blocks = b.shape[0] if b.ndim == 3 else None
    bshape = b.shape if b.ndim == 2 else (b.shape[1], b.shape[0] * b.shape[2])
    if mode == "nn":
        (M, K), (K2, N) = a.shape, bshape
    elif mode == "nt":
        (M, K), (N, K2) = a.shape, bshape
    else:
        (K, M), (K2, N) = a.shape, bshape
    assert K == K2, (name, a.shape, b.shape)
    assert b_blocks is None or mode != "tn"
    assert out_blocks is None or mode == "tn"
    tn = min(tn, N // (out_blocks or 1), bshape[1] // b_blocks if (b_blocks and mode == "nn") else tn)
    blocked_k = bool(b_blocks) and mode == "nt"
    tm, tn, tk = _pick(M, tm), _pick(N, tn), (K if blocked_k else _pick(K, tk))

    def vmem_bytes(tm, tk):
        tiles = tm * tk * a.dtype.itemsize + tk * tn * b.dtype.itemsize
        for i, dt in enumerate(out_dtypes if epilogue else (out_dtype,)):
            tiles += tm * (out_widths[i] if out_widths else tn) * jnp.dtype(dt).itemsize
        for x in extras:
            arr = x[0] if isinstance(x, tuple) else x
            tiles += (tm if arr.shape[0] > 1 else 1) * tn * arr.dtype.itemsize
        return 2 * tiles + (tm * tn * 4 if K // tk > 1 else 0)

    def cost(tm, tk):
        ni, nj, nk = M // tm, N // tn, K // tk
        steps = ni * nj * nk
        acc_passes = steps * tm * tn * 8 * ACC_PASS_WEIGHT if nk > 1 else 0
        a_reads = M * K * a.dtype.itemsize * (nj if nk > 1 else 1)
        b_reads = K * N * b.dtype.itemsize * (ni if (nj > 1 or nk > 1) else 1)
        return (steps * STEP_COST_BYTES + acc_passes + vmem_bytes(tm, tk) // 2
                + HBM_COST_RATIO * (a_reads + b_reads))

    options = [(m, k) for m in ({M} if whole_rows else {_pick(M, max(t, LANES)) for t in (tm, tm // 2, tm // 4)})
               for k in ({K} if blocked_k else {_pick(K, max(t, LANES)) for t in (tk, tk // 2, tk // 4)})
               if vmem_bytes(m, k) <= MATMUL_VMEM_BUDGET]
    tm, tk = min(options, key=lambda o: cost(*o))
    nk = K // tk
    dims = {"nn": (((1,), (0,)), ((), ())), "nt": (((1,), (1,)), ((), ())), "tn": (((0,), (0,)), ((), ()))}[mode]

    n_x, n_o = len(extras), len(out_dtypes) if epilogue else 1
    n_after = 0 if after is None else 1

    def body(a_ref, b_ref, *rest):
        x_refs, o_refs, acc = rest[:n_x], rest[n_x + n_after:n_x + n_after + n_o], rest[n_x + n_after + n_o:]
        if blocked_k:
            bw = b.shape[2]
            part = sum(lax.dot_general(a_ref[:, blk * bw:(blk + 1) * bw].astype(BF16), b_ref[blk].astype(BF16), dims,
                                       preferred_element_type=F32) for blk in range(b_blocks))
        else:
            part = lax.dot_general(a_ref[...].astype(BF16), b_ref[...].astype(BF16), dims, preferred_element_type=F32)

        def finish(res):
            outs = epilogue(res, *[r[...] for r in x_refs]) if epilogue else (res,)
            for r, v in zip(o_refs, outs):
                r[...] = v.astype(r.dtype)

        if nk == 1:
            finish(part)
            return
        acc_ref, k = acc[0], pl.program_id(2)

        @pl.when(k == 0)
        def _():
            acc_ref[...] = part

        @pl.when(k > 0)
        def _():
            acc_ref[...] += part

        @pl.when(k == nk - 1)
        def _():
            finish(acc_ref[...])

    a_spec = {"nn": pl.BlockSpec((tm, tk), lambda i, j, k: (i, k)), "nt": pl.BlockSpec((tm, tk), lambda i, j, k: (i, k)),
              "tn": pl.BlockSpec((tk, tm), lambda i, j, k: (k, i))}[mode]
    b_spec = {"nn": pl.BlockSpec((tk, tn), lambda i, j, k: (k, j)), "nt": pl.BlockSpec((tn, tk), lambda i, j, k: (j, k)),
              "tn": pl.BlockSpec((tk, tn), lambda i, j, k: (k, j))}[mode]
    if b_blocks and mode == "nn":
        per = b.shape[2] // tn
        b_spec = pl.BlockSpec((None, tk, tn), lambda i, j, k: (j // per, k, j % per))
    elif b_blocks:
        b_spec = pl.BlockSpec((b_blocks, tn, b.shape[2]), lambda i, j, k: (0, j, 0))
    out_spec = pl.BlockSpec((tm, tn), lambda i, j, k: (i, j))
    out_shape = jax.ShapeDtypeStruct((M, N), out_dtype)
    if out_blocks:
        per_o = N // out_blocks // tn
        out_spec = pl.BlockSpec((None, tm, tn), lambda i, j, k: (j // per_o, i, j % per_o))
        out_shape = jax.ShapeDtypeStruct((out_blocks, M, N // out_blocks), out_dtype)
    epi_widths = list(out_widths) if out_widths else [N] * n_o
    assert all(w == N for w in epi_widths) or N == tn
    epi_specs = [pl.BlockSpec((tm, tn if w == N else w), lambda i, j, k: (i, j)) for w in epi_widths]
    x_specs, x_args = [], []
    for x in extras:
        arr, off = x if isinstance(x, tuple) else (x, 0)
        if arr.shape[0] == 1:
            x_specs.append(pl.BlockSpec((1, tn), lambda i, j, k: (0, j)))
        else:
            x_specs.append(pl.BlockSpec((tm, tn), lambda i, j, k, off=off: (i, j + off)))
        x_args.append(arr)
    res = pl.pallas_call(
        body, name=name, grid=(M // tm, N // tn, nk),
        in_specs=[a_spec, b_spec] + x_specs + [pl.BlockSpec(memory_space=pl.ANY)] * n_after,
        out_specs=epi_specs if epilogue else out_spec,
        out_shape=[jax.ShapeDtypeStruct((M, w), dt) for w, dt in zip(epi_widths, out_dtypes)] if epilogue else out_shape,
        scratch_shapes=[pltpu.VMEM((tm, tn), F32)] if nk > 1 else [],
        compiler_params=pltpu.CompilerParams(dimension_semantics=("parallel", "parallel", "arbitrary"),
                                             vmem_limit_bytes=VMEM_LIMIT),
    )(a, b, *x_args, *([after] if n_after else []))
    return res


class Rows:
    def __init__(self, arr, width=None, cb=0):
        self.arr, self.width, self.cb = arr, (arr.shape[1] if width is None else width), cb


class Heads:
    def __init__(self, arr):
        self.arr = arr


class Halo:
    def __init__(self, arr, side):
        self.arr, self.side = arr, side


def _rows_call(name, fn, ins, consts, outs, accs=(), tm=512, with_pid=False):
    T = next(o.arr.shape[1] if isinstance(o, Heads) else o.arr.shape[0] for o in ins if not isinstance(o, Halo))
    tm = min(tm, T)
    n_tiles = T // tm
    n_in, n_c, n_out = len(ins), len(consts), len(outs)
    in_specs, args = [], []
    for o in ins:
        if isinstance(o, Rows):
            in_specs.append(pl.BlockSpec((tm, o.width), lambda i, cb=o.cb: (i, cb)))
        elif isinstance(o, Heads):
            in_specs.append(pl.BlockSpec((NP, tm, PW), lambda i: (0, i, 0)))
        else:
            w = o.arr.shape[1]
            if o.side < 0:
                in_specs.append(pl.BlockSpec((8, w), lambda i: (jnp.maximum(i * (tm // 8) - 1, 0), 0)))
            else:
                in_specs.append(pl.BlockSpec((8, w), lambda i: (jnp.minimum((i + 1) * (tm // 8), T // 8 - 1), 0)))
        args.append(o.arr)
    for c in consts:
        in_specs.append(pl.BlockSpec(c.shape, lambda i, nd=c.ndim: (0,) * nd))
        args.append(c)
    out_specs, out_shape = [], []
    for o in outs:
        if o[0] == "rows":
            out_specs.append(pl.BlockSpec((tm, o[1]), lambda i: (i, 0)))
            out_shape.append(jax.ShapeDtypeStruct((T, o[1]), o[2]))
        else:
            out_specs.append(pl.BlockSpec((NP, tm, PW), lambda i: (0, i, 0)))
            out_shape.append(jax.ShapeDtypeStruct((NP, T, PW), o[1]))
    for shape, dt in accs:
        out_specs.append(pl.BlockSpec(shape, lambda i, nd=len(shape): (0,) * nd))
        out_shape.append(jax.ShapeDtypeStruct(shape, dt))

    def body(*refs):
        i = pl.program_id(0)
        vals = []
        vals = [r[...] for r in refs[:n_in + n_c]]
        res = fn(i, n_tiles, *vals) if with_pid else fn(*vals)
        out_refs = refs[n_in + n_c:]
        for r, v in zip(out_refs[:n_out], res[:n_out]):
            r[...] = v.astype(r.dtype)
        if accs:
            @pl.when(i == 0)
            def _():
                for r in out_refs[n_out:]:
                    r[...] = jnp.zeros_like(r)

            for r, v in zip(out_refs[n_out:], res[n_out:]):
                r[...] += v.astype(r.dtype)

    res = pl.pallas_call(
        body, name=name, grid=(n_tiles,), in_specs=in_specs, out_specs=out_specs, out_shape=out_shape,
        compiler_params=pltpu.CompilerParams(dimension_semantics=("arbitrary",), vmem_limit_bytes=VMEM_LIMIT),
    )(*args)
    return res


def _rms(x, g):
    return x * lax.rsqrt(jnp.mean(x * x, axis=-1, keepdims=True) + NORM_EPS) * g


def _gelu(x):
    return 0.5 * x * (1.0 + lax.erf(x * 0.7071067811865476))


def _sigmoid(x):
    return 1.0 / (1.0 + jnp.exp(-x))


def _bdot(a, b):
    return jnp.dot(a.astype(BF16), b.astype(BF16), preferred_element_type=F32)


def _to_heads(x):
    return jnp.concatenate([x[:, p * PW:(p + 1) * PW][None] for p in range(NP)], axis=0)


def _from_heads(xp):
    return jnp.concatenate([xp[p] for p in range(NP)], axis=-1)


def _head_sum(xp):
    low = lax.broadcasted_iota(jnp.int32, xp.shape, xp.ndim - 1) < HN
    both = jnp.sum(xp, axis=-1, keepdims=True)
    first = jnp.sum(jnp.where(low, xp, 0.0), axis=-1, keepdims=True)
    return jnp.where(low, first, both - first)


def _split_pairs(xp):
    return jnp.concatenate([xp[:, :, :HN], xp[:, :, HN:]], axis=0)


def _join_pairs(xh):
    return jnp.concatenate([xh[:NP], xh[NP:]], axis=-1)


def _sgu_fn(p, ln_w, ln_b, sw, sbt):
    z = _gelu(p)
    u, v = z[:, :D], z[:, D:]
    mu = jnp.mean(v, axis=-1, keepdims=True)
    var = jnp.mean(jnp.square(v - mu), axis=-1, keepdims=True)
    vn = (v - mu) * lax.rsqrt(var + LN_EPS) * ln_w + ln_b
    ri = lax.broadcasted_iota(jnp.int32, (SGU_C, SGU_C), 0)
    ci = lax.broadcasted_iota(jnp.int32, (SGU_C, SGU_C), 1)
    mask = (ci <= ri).astype(F32)
    dg = D // SGU_G
    parts = []
    for g in range(SGU_G):
        parts.append(_bdot(sw[g] * mask, vn[:, g * dg:(g + 1) * dg]) + sbt[:, g:g + 1])
    return u * jnp.concatenate(parts, axis=-1)


def _pre_fn(qr, qk, qv, qxw, qxa, qxg, wl, w0, al, a0, gl, k_k, k_a):
    w = -jax.nn.softplus(-(w0 + _bdot(jnp.tanh(qxw), wl))) - 0.5
    lw = -jnp.exp(w)
    aa = _sigmoid(a0 + _bdot(qxa, al))
    g = _bdot(_sigmoid(qxg), gl)
    kk = _to_heads(qk * k_k)
    kk = kk / jnp.maximum(jnp.sqrt(_head_sum(kk * kk)), 1e-12)
    k2 = qk * (1.0 + (aa - 1.0) * k_a)
    return _to_heads(qr), _to_heads(lw), _to_heads(k2), _to_heads(qv), kk, _to_heads(aa), g


def _post_fn(o, r, k2, v, g, ln_w, ln_b, r_k):
    mu = _head_sum(o) * (1.0 / HN)
    d = o - mu
    var = _head_sum(d * d) * (1.0 / HN)
    on = d * lax.rsqrt(var + GN_EPS) * ln_w + ln_b
    bonus = _head_sum(r * k2 * r_k) * v
    return _from_heads(on + bonus) * g


def _gate_fn(pg, ya, yb):
    return _sigmoid(pg[:, :D]) * ya + _sigmoid(pg[:, D:]) * yb


def _bmm(x, y, cx, cy, out_path=False):
    return lax.dot_general(x, y, (((cx,), (cy,)), ((0,), (0,))),
                           precision=SCAN_OUT_PRECISION if out_path else SCAN_PRECISION, preferred_element_type=F32)


def _unit_lower_inverse(M):
    C = M.shape[1]
    ti = lax.broadcasted_iota(jnp.int32, (C, C), 0)
    tj = lax.broadcasted_iota(jnp.int32, (C, C), 1)
    eye = (ti == tj).astype(F32)
    same = lambda b: (ti // b == tj // b).astype(F32)
    X = -(M * same(SOLVE_B))
    inv = eye + X
    span = 1
    while 2 * span < SOLVE_B:
        X = _bmm(X, X, 2, 1)
        inv = inv + _bmm(inv, X, 2, 1)
        span *= 2
    b = SOLVE_B
    while b < C:
        low = M * (same(2 * b) - same(b))
        inv = inv - _bmm(_bmm(inv, low, 2, 1, out_path=True), inv, 2, 1, out_path=True)
        b *= 2
    return inv


@jax.custom_vjp
def _unit_lower_solve(inv, M, y):
    return _bmm(inv, y, 2, 1)


def _unit_lower_solve_fwd(inv, M, y):
    u = _bmm(inv, y, 2, 1)
    return u, (inv, u)


def _unit_lower_solve_bwd(res, du):
    inv, u = res
    dy = _bmm(inv, du, 1, 1)
    return jnp.zeros_like(inv), -_bmm(dy, u, 2, 2), dy


_unit_lower_solve.defvjp(_unit_lower_solve_fwd, _unit_lower_solve_bwd)


@jax.custom_vjp
def _unit_lower_solved(inv, u, M, y):
    return u


_unit_lower_solved.defvjp(lambda inv, u, M, y: (u, (inv, u)),
                          lambda res, du: (jnp.zeros_like(res[0]), jnp.zeros_like(res[1]))
                          + _unit_lower_solve_bwd(res, du)[1:])


@functools.partial(jax.custom_vjp, nondiff_argnums=(0,))
def _kept(fn, value, *args):
    return value


def _kept_fwd(fn, value, *args):
    return value, args


def _kept_bwd(fn, args, d):
    _, vjp = jax.vjp(fn, *args)
    return (jnp.zeros_like(d),) + tuple(vjp(d))


_kept.defvjp(_kept_fwd, _kept_bwd)


def _sum_over_time(x, reverse):
    C = x.shape[1]
    ti = lax.broadcasted_iota(jnp.int32, (C, C), 0)
    tj = lax.broadcasted_iota(jnp.int32, (C, C), 1)
    ones = jnp.broadcast_to(((tj >= ti) if reverse else (tj <= ti)).astype(BF16), (x.shape[0], C, C))
    hi = x.astype(BF16)
    r1 = x - hi.astype(F32)
    mid = r1.astype(BF16)
    lo = (r1 - mid.astype(F32)).astype(BF16)
    dn = (((2,), (1,)), ((0,), (0,)))
    return sum(lax.dot_general(ones, p, dn, preferred_element_type=F32) for p in (lo, mid, hi))


@jax.custom_vjp
def _time_cumsum(lw):
    return _sum_over_time(lw, reverse=False)


_time_cumsum.defvjp(lambda lw: (_sum_over_time(lw, reverse=False), None),
                    lambda _, d: (_sum_over_time(d, reverse=True),))


def _chunk_fn(S0, r, lw, k, v, kk, a, kept=None):
    C = SCAN_C
    bmm = _bmm
    ti = lax.broadcasted_iota(jnp.int32, (C, C), 0)
    tj = lax.broadcasted_iota(jnp.int32, (C, C), 1)
    incl2 = jnp.concatenate([(tj <= ti).astype(F32)] * 2, axis=1)
    strict = (tj < ti).astype(F32)
    n_mask = jnp.concatenate([jnp.zeros((C, C), F32), strict], axis=1)

    def known(name, fn, *args):
        return fn(*args) if kept is None else _kept(fn, kept[name], *args)

    cum = known("cum", _time_cumsum, lw)
    g_in, g_ex, g_inv = jnp.exp(cum), jnp.exp(cum - lw), jnp.exp(-cum)
    kkt, rt = kk * g_ex, r * g_in
    bk = jnp.concatenate([kk * a * g_inv, k * g_inv], axis=1)
    A = known("A", lambda x, y: bmm(x, y, 2, 2), kkt, bk)
    M = A[:, :, :C] * strict
    zv = jnp.concatenate([jnp.zeros_like(v), v], axis=1)
    s0_side = bmm(jnp.concatenate([kkt, rt], axis=1), S0, 2, 2, out_path=True)
    rhs = s0_side[:, :C] + bmm(A * n_mask, zv, 2, 1, out_path=True)
    if kept is None:
        inv = lax.stop_gradient(_unit_lower_inverse(M))
        y = _unit_lower_solve(inv, M, rhs)
    else:
        inv = kept["inv"]
        y = _unit_lower_solved(inv, kept["y"], M, rhs)
    z = jnp.concatenate([-y, v], axis=1)
    attn = known("attn", lambda x, y: bmm(x, y, 2, 2) * incl2, rt, bk)
    O = s0_side[:, C:] + bmm(attn, z, 2, 1, out_path=True)
    g_end = g_in[:, C - 1:C, :]
    S1 = S0 * g_end + bmm(z, bk * g_end, 1, 1, out_path=True)
    return O, S1, dict(cum=cum, A=A, attn=attn, y=y, inv=inv)


def _scan_fwd(r, lw, k, v, kk, a, ex=None, tb=256):
    assert SCAN_C == HN and 2 * SCAN_C == PW
    T = r.shape[1]
    tb = min(tb, T)
    n_chunks = tb // SCAN_C
    nb = T // tb
    nx = ex.nb if ex else 0

    def body(*refs):
        r_ref, lw_ref, k_ref, v_ref, kk_ref, a_ref = refs[:6]
        x_in, (o_ref, s0_ref), x_out = refs[6:6 + nx], refs[6 + nx:8 + nx], refs[8 + nx:8 + 2 * nx]
        s_ref, sems = refs[8 + 2 * nx], refs[9 + 2 * nx:]

        plan = ex.schedule(nb) if ex else []

        @pl.when(pl.program_id(0) == 0)
        def _():
            s_ref[...] = jnp.zeros_like(s_ref)
            for at, action in plan[:1]:
                action(x_in, x_out, sems)

        def step(c, carry):
            sl = pl.ds(pl.multiple_of(c * SCAN_C, SCAN_C), SCAN_C)
            S0 = s_ref[...]
            O, S1, keep = _chunk_fn(S0, *[_split_pairs(ref[:, sl, :])
                                          for ref in (r_ref, lw_ref, k_ref, v_ref, kk_ref, a_ref)])
            o_ref[:, sl, :] = _join_pairs(O)
            s0_ref[c, 0] = jnp.concatenate([S0, keep["inv"]], axis=-1)
            s0_ref[c, 1] = jnp.concatenate([keep["cum"], keep["y"]], axis=-1)
            s0_ref[c, 2] = keep["A"]
            s0_ref[c, 3] = keep["attn"]
            s_ref[...] = S1
            return carry

        lax.fori_loop(0, n_chunks, step, 0)

        for at, action in plan[1:]:
            pl.when(pl.program_id(0) == at)(functools.partial(action, x_in, x_out, sems))

    hm = pl.BlockSpec((NP, tb, PW), lambda i: (0, i, 0))
    res = pl.pallas_call(
        body, name="rwkv_scan_fwd", grid=(nb,), in_specs=[hm] * 6 + (ex.any_specs if ex else []),
        out_specs=[hm, pl.BlockSpec((n_chunks, N_KEPT, NH, HN, PW), lambda i: (i, 0, 0, 0, 0))]
        + (ex.any_specs if ex else []),
        out_shape=[jax.ShapeDtypeStruct((NP, T, PW), F32),
                   jax.ShapeDtypeStruct((T // SCAN_C, N_KEPT, NH, HN, PW), F32)]
        + (ex.out_shape if ex else []),
        scratch_shapes=[pltpu.VMEM((NH, HN, HN), F32)] + (ex.sem_shapes if ex else []),
        compiler_params=pltpu.CompilerParams(dimension_semantics=("arbitrary",), vmem_limit_bytes=VMEM_LIMIT),
    )(r, lw, k, v, kk, a, *(ex.bufs if ex else []))
    return res[0], res[1], list(res[2:])


def _scan_bwd(r, lw, k, v, kk, a, s0s, do, ex=None, tb=128):
    T = r.shape[1]
    tb = min(tb, T)
    n_chunks = tb // SCAN_C
    nb = T // tb
    nx = ex.nb if ex else 0

    def body(*refs):
        r_ref, lw_ref, k_ref, v_ref, kk_ref, a_ref, s0_ref, do_ref = refs[:8]
        x_in, (dr, dlw, dk, dv, dkk, da), x_out = refs[8:8 + nx], refs[8 + nx:14 + nx], refs[14 + nx:14 + 2 * nx]
        ds_ref, sems = refs[14 + 2 * nx], refs[15 + 2 * nx:]

        plan = ex.schedule(nb) if ex else []

        @pl.when(pl.program_id(0) == 0)
        def _():
            ds_ref[...] = jnp.zeros_like(ds_ref)
            for at, action in plan[:1]:
                action(x_in, x_out, sems)

        def step(j, carry):
            c = n_chunks - 1 - j
            sl = pl.ds(pl.multiple_of(c * SCAN_C, SCAN_C), SCAN_C)
            s0_inv, cum_y = s0_ref[c, 0], s0_ref[c, 1]
            kept = dict(inv=s0_inv[:, :, HN:], cum=cum_y[:, :, :HN], y=cum_y[:, :, HN:], A=s0_ref[c, 2],
                        attn=s0_ref[c, 3])
            _, vjp = jax.vjp(lambda *t: _chunk_fn(*t, kept=kept)[:2], s0_inv[:, :, :HN],
                             *[_split_pairs(ref[:, sl, :]) for ref in (r_ref, lw_ref, k_ref, v_ref, kk_ref, a_ref)])
            g = vjp((_split_pairs(do_ref[:, sl, :]), ds_ref[...]))
            ds_ref[...] = g[0]
            for ref, val in zip((dr, dlw, dk, dv, dkk, da), g[1:]):
                ref[:, sl, :] = _join_pairs(val)
            return carry

        lax.fori_loop(0, n_chunks, step, 0)

        for at, action in plan[1:]:
            pl.when(pl.program_id(0) == at)(functools.partial(action, x_in, x_out, sems))

    hm = pl.BlockSpec((NP, tb, PW), lambda i: (0, nb - 1 - i, 0))
    res = pl.pallas_call(
        body, name="rwkv_scan_bwd", grid=(nb,),
        in_specs=[hm] * 6 + [pl.BlockSpec((n_chunks, N_KEPT, NH, HN, PW), lambda i: (nb - 1 - i, 0, 0, 0, 0)), hm]
        + (ex.any_specs if ex else []),
        out_specs=[hm] * 6 + (ex.any_specs if ex else []),
        out_shape=[jax.ShapeDtypeStruct((NP, T, PW), F32)] * 6 + (ex.out_shape if ex else []),
        scratch_shapes=[pltpu.VMEM((NH, HN, HN), F32)] + (ex.sem_shapes if ex else []),
        compiler_params=pltpu.CompilerParams(dimension_semantics=("arbitrary",), vmem_limit_bytes=VMEM_LIMIT),
    )(r, lw, k, v, kk, a, s0s, do, *(ex.bufs if ex else []))
    return list(res[:6]), list(res[6:])


def _shift_down(i, p, prev8):
    first = jnp.where(i > 0, prev8[7:8, :], 0.0)
    row = lax.broadcasted_iota(jnp.int32, p.shape, 0)
    return jnp.where(row == 0, first, pltpu.roll(p, 1, axis=0))


def _mix_bwd(dq, p, sb, tm=256):
    def fn(i, n, dq, next8, p, prev8, sb):
        ps = _shift_down(i, p, prev8)
        d1 = dq * sb[1:2]
        last = jnp.where(i < n - 1, next8[0:1, :] * sb[1:2], 0.0)
        row = lax.broadcasted_iota(jnp.int32, dq.shape, 0)
        up = jnp.where(row == dq.shape[0] - 1, last, pltpu.roll(d1, dq.shape[0] - 1, axis=0))
        return (dq * sb[0:1] + up, jnp.sum(dq * p, axis=0, keepdims=True), jnp.sum(dq * ps, axis=0, keepdims=True))
    w = p.shape[1]
    return _rows_call("shift_mix_bwd", fn, [Rows(dq), Halo(dq, +1), Rows(p), Halo(p, -1)], [sb], [("rows", w, BF16)],
                      accs=[((1, w), F32), ((1, w), F32)], tm=tm, with_pid=True)


def _local_step(x, target, W, late_weights=None, early_grads=None, w_in_grads_ready=None, ffn_grads_ready=None):
    G = {}
    a = _rows_call("norm_mix_fwd", lambda x, g: (_rms(x, g),), [Rows(x)], [W["g_mix"]], [("rows", D, BF16)])[0]
    p_sgu = _matmul("proj_sgu", a, W["w_sgu_t"], "nt")
    def token_shift(p, sb0, sb1):
        row = lax.broadcasted_iota(jnp.int32, p.shape, 0)
        return p, p * sb0 + jnp.where(row == 0, 0.0, pltpu.roll(p, 1, axis=0)) * sb1
    p_rw, q = _matmul("proj_rwkv", a, W["w_rw_t"], "nt", tn=512, whole_rows=True, epilogue=token_shift,
                      extras=[W["sb"][0:1], W["sb"][1:2]], out_dtypes=(F32, F32))
    p_gate = _matmul("proj_gate", a, W["w_gate_t"], "nt")

    sgu_consts = [W["sgu_ln_w"], W["sgu_ln_b"], W["sgu_w"], W["sgu_bt"]]
    s = _rows_call("sgu_fwd", lambda *t: (_sgu_fn(*t),), [Rows(p_sgu)], sgu_consts, [("rows", D, BF16)], tm=SGU_C)[0]

    q_ins = [Rows(q, D, 0), Rows(q, D, 1), Rows(q, D, 2), Rows(q, 128, 24), Rows(q, 128, 25), Rows(q, 256, 13)]
    pre_consts = [W["w_lora"], W["w0"], W["a_lora"], W["a0"], W["g_lora"], W["k_k"], W["k_a"]]
    r_h, lw_h, k_h, v_h, kk_h, a_h, g_gate = _rows_call(
        "rwkv_pre_fwd", _pre_fn, q_ins, pre_consts, [("heads", F32)] * 6 + [("rows", D, F32)], tm=128)
    o_h, s0s, got = _scan_fwd(r_h, lw_h, k_h, v_h, kk_h, a_h, ex=late_weights[0] if late_weights else None)
    if late_weights:
        W = {**W, **late_weights[1](got)}
    y_a = _matmul("proj_a", s, W["w_proj_a"], "nn")
    post_ins = [Heads(o_h), Heads(r_h), Heads(k_h), Heads(v_h), Rows(g_gate)]
    post_consts = [W[n].reshape(NP, 1, PW) for n in ("ln_x_w", "ln_x_b", "r_k")]
    z_b = _rows_call("rwkv_post_fwd", lambda *t: (_post_fn(*t),), post_ins, post_consts, [("rows", D, BF16)], tm=128)[0]
    y_b, mixed = _matmul("proj_b", z_b, W["w_proj_b"], "nn", extras=[(p_gate, 0), (p_gate, 1), y_a],
                         epilogue=lambda yb, ga, gb, ya: (yb, _sigmoid(ga) * ya + _sigmoid(gb) * yb),
                         out_dtypes=(F32, BF16))

    def res1(mo, x, g):
        h1 = x + mo
        return h1, _rms(h1, g)
    h1, f = _matmul("proj_out", mixed, W["w_out"], "nn", extras=[x, W["g_ffn"]], epilogue=res1,
                    out_dtypes=(F32, BF16))

    def relu_sq(u):
        r = jnp.maximum(u, 0.0)
        return r, r * r
    r1, act = _matmul("ffn_up", f, W["w_ffn1"], "nn", epilogue=relu_sq, out_dtypes=(BF16, BF16))
    ff = _matmul("ffn_down", act, W["w_ffn2"], "nn")

    def head(h1, ff, tgt, g):
        def f_(h1, ff, g):
            y = _rms(h1 + ff, g)
            return 0.5 * jnp.sum(jnp.mean(jnp.square(y - tgt), axis=-1))
        loss, (dh2, _, dg) = jax.value_and_grad(f_, argnums=(0, 1, 2))(h1, ff, g)
        return dh2, jnp.full((8, LANES), loss, F32), dg
    dh2, loss_acc, G["g_final"] = _rows_call("loss_head", head, [Rows(h1), Rows(ff), Rows(target)], [W["g_final"]],
                                             [("rows", D, F32)], accs=[((8, LANES), F32), ((1, D), F32)])

    d_u1 = _matmul("ffn_down_dx", dh2, W["w_ffn2"], "nt", extras=[r1], out_dtypes=(BF16,),
                   epilogue=lambda d_act, r: (d_act * 2.0 * r.astype(F32),))[0]
    G["w_ffn2"] = _matmul("ffn_down_dw", act, dh2, "tn", out_dtype=GRAD_PAYLOAD)
    d_f = _matmul("ffn_up_dx", d_u1, W["w_ffn1"], "nt")
    G["w_ffn1"] = _matmul("ffn_up_dw", f, d_u1, "tn", out_blocks=N_DEV, out_dtype=GRAD_PAYLOAD)

    def res1_bwd(h1, d_f, dh2, g):
        _, vjp = jax.vjp(_rms, h1, g)
        dh, dg = vjp(d_f)
        return dh2 + dh, dg
    dh1, G["g_ffn"] = _rows_call("residual_norm_bwd", res1_bwd, [Rows(h1), Rows(d_f), Rows(dh2)],
                                 [W["g_ffn"]], [("rows", D, F32)], accs=[((1, D), F32)])
    ffn_token = ffn_grads_ready(G) if ffn_grads_ready else None
    def gate_bwd(d_mixed, ga, gb, ya, yb):
        _, vjp = jax.vjp(_gate_fn, jnp.concatenate([ga, gb], axis=-1), ya, yb)
        return vjp(d_mixed)
    d_gate, d_ya, d_yb = _matmul("proj_out_dx", dh1, W["w_out"], "nt", after=ffn_token, epilogue=gate_bwd,
                                 extras=[(p_gate, 0), (p_gate, 1), y_a, y_b], out_dtypes=(BF16, BF16, BF16),
                                 out_widths=(2 * D, D, D))
    G["w_out"] = _matmul("proj_out_dw", mixed, dh1, "tn", out_dtype=GRAD_PAYLOAD)

    d_s = _matmul("proj_a_dx", d_ya, W["w_proj_a"], "nt")
    G["w_proj_a"] = _matmul("proj_a_dw", s, d_ya, "tn", out_dtype=GRAD_PAYLOAD)

    def sgu_bwd(p, ds, *c):
        _, vjp = jax.vjp(_sgu_fn, p, *c)
        return vjp(ds)
    d_p_sgu, G["sgu_ln_w"], G["sgu_ln_b"], G["sgu_w"], G["sgu_bt"] = _rows_call(
        "sgu_bwd", sgu_bwd, [Rows(p_sgu), Rows(d_s)], sgu_consts, [("rows", 2 * D, BF16)],
        accs=[((1, D), F32), ((1, D), F32), ((SGU_G, SGU_C, SGU_C), F32), ((SGU_C, SGU_G), F32)], tm=SGU_C)

    d_zb = _matmul("proj_b_dx", d_yb, W["w_proj_b"], "nt")
    G["w_proj_b"] = _matmul("proj_b_dw", z_b, d_yb, "tn", out_dtype=GRAD_PAYLOAD)

    def post_bwd(o, r, k2, v, g, dz, *c):
        _, vjp = jax.vjp(_post_fn, o, r, k2, v, g, *c)
        return vjp(dz)
    do_h, dr1, dk1, dv1, d_g, g_lnw, g_lnb, g_rk = _rows_call(
        "rwkv_post_bwd", post_bwd, post_ins + [Rows(d_zb)], post_consts, [("heads", F32)] * 4 + [("rows", D, F32)],
        accs=[((NP, 1, PW), F32)] * 3, tm=128)
    G["ln_x_w"], G["ln_x_b"], G["r_k"] = (t.reshape(1, D) for t in (g_lnw, g_lnb, g_rk))
    (dr2, dlw, dk2, dv2, dkk, daa), early = _scan_bwd(r_h, lw_h, k_h, v_h, kk_h, a_h, s0s, do_h,
                                                      ex=early_grads(G) if early_grads else None)

    def pre_bwd(qr, qk, qv, qxw, qxa, qxg, dr1, dr2, dlw, dk1, dk2, dv1, dv2, dkk, daa, dg, *c):
        _, vjp = jax.vjp(_pre_fn, qr, qk, qv, qxw, qxa, qxg, *c)
        g = vjp((dr1 + dr2, dlw, dk1 + dk2, dv1 + dv2, dkk, daa, dg))
        dq = jnp.concatenate(g[:6], axis=-1)
        return (dq,) + tuple(g[6:])
    pre_b_ins = q_ins + [Heads(dr1), Heads(dr2), Heads(dlw), Heads(dk1), Heads(dk2), Heads(dv1), Heads(dv2),
                         Heads(dkk), Heads(daa), Rows(d_g)]
    d_q, G["w_lora"], G["w0"], G["a_lora"], G["a0"], G["g_lora"], G["k_k"], G["k_a"] = _rows_call(
        "rwkv_pre_bwd", pre_bwd, pre_b_ins, pre_consts, [("rows", RW_INT, F32)],
        accs=[((128, D), F32), ((1, D), F32), ((128, D), F32), ((1, D), F32), ((256, D), F32), ((1, D), F32),
              ((1, D), F32)], tm=128)
    d_p_rw, dsb0, dsb1 = _mix_bwd(d_q, p_rw, W["sb"])
    G["sb"] = jnp.concatenate([dsb0, dsb1], axis=0)

    G["w_sgu_t"] = _matmul("proj_sgu_dw", d_p_sgu, a, "tn", out_dtype=GRAD_PAYLOAD)
    G["w_rw_t"] = _matmul("proj_rwkv_dw", d_p_rw, a, "tn", out_dtype=GRAD_PAYLOAD)
    G["w_gate_t"] = _matmul("proj_gate_dw", d_gate, a, "tn", out_dtype=GRAD_PAYLOAD)
    token = w_in_grads_ready(G) if w_in_grads_ready else None
    da1 = _matmul("proj_sgu_dx", d_p_sgu, W["w_sgu_t"], "nn", after=token)
    da2 = _matmul("proj_rwkv_dx", d_p_rw, W["w_rw_t"], "nn", after=token)
    da3 = _matmul("proj_gate_dx", d_gate, W["w_gate_t"], "nn", after=token)

    def norm1_bwd(x, da1, da2, da3, dh1, g):
        _, vjp = jax.vjp(_rms, x, g)
        dx, dg = vjp(da1 + da2 + da3)
        return dh1 + dx, dg
    dx, G["g_mix"] = _rows_call("norm_mix_bwd", norm1_bwd, [Rows(x), Rows(da1), Rows(da2), Rows(da3), Rows(dh1)],
                                [W["g_mix"]], [("rows", D, F32)], accs=[((1, D), F32)])
    return loss_acc[0, 0], dx, G, early


class Exchange:
    def __init__(self, bufs, gathers):
        self.bufs, self.gathers, self.nb = list(bufs), list(gathers), len(bufs)
        self.any_specs = [pl.BlockSpec(memory_space=pl.ANY)] * self.nb
        self.out_shape = [jax.ShapeDtypeStruct((N_DEV,) + (b.shape if g else b.shape[1:]), b.dtype)
                          for b, g in zip(self.bufs, self.gathers)]
        n = (N_DEV - 1) * self.nb
        self.sem_shapes = [pltpu.SemaphoreType.DMA((n,)), pltpu.SemaphoreType.DMA((n,)),
                           pltpu.SemaphoreType.DMA((self.nb,))]

    def _copies(self, in_refs, out_refs, sems):
        send_sems, recv_sems, local_sems = sems
        x, y, c = lax.axis_index("x"), lax.axis_index("y"), lax.axis_index("c")
        me = 4 * x + 2 * y + c

        def src(b, dest):
            return in_refs[b] if self.gathers[b] else in_refs[b].at[dest]

        local = [pltpu.make_async_copy(src(b, me), out_refs[b].at[me], local_sems.at[b]) for b in range(self.nb)]
        sends, recvs = [], []
        for kbits in range(1, N_DEV):
            px = 1 - x if kbits & 4 else x
            py = 1 - y if kbits & 2 else y
            pc = 1 - c if kbits & 1 else c
            peer = 4 * px + 2 * py + pc
            for b in range(self.nb):
                s = (kbits - 1) * self.nb + b
                sends.append(pltpu.make_async_remote_copy(
                    src_ref=src(b, peer), dst_ref=out_refs[b].at[me], send_sem=send_sems.at[s],
                    recv_sem=recv_sems.at[s], device_id=(px, py, pc), device_id_type=pl.DeviceIdType.MESH))
                recvs.append(pltpu.make_async_remote_copy(
                    src_ref=src(b, peer), dst_ref=out_refs[b].at[peer], send_sem=send_sems.at[s],
                    recv_sem=recv_sems.at[s], device_id=(px, py, pc), device_id_type=pl.DeviceIdType.MESH))
        return local, sends, recvs

    def start(self, in_refs, out_refs, sems):
        local, sends, _ = self._copies(in_refs, out_refs, sems)
        for cp in sends + local:
            cp.start()

    def wait(self, in_refs, out_refs, sems):
        local, sends, recvs = self._copies(in_refs, out_refs, sems)
        for cp in recvs:
            cp.wait_recv()
        for cp in sends:
            cp.wait_send()
        for cp in local:
            cp.wait()

    def schedule(self, n_steps):
        return [(0, self.start), (n_steps - 1, self.wait)]


def _exchange(name, bufs, gather):
    ex = Exchange(bufs, gather if isinstance(gather, (list, tuple)) else [gather] * len(bufs))

    def body(*refs):
        in_refs, out_refs, sems = refs[:ex.nb], refs[ex.nb:2 * ex.nb], refs[2 * ex.nb:]
        ex.start(in_refs, out_refs, sems)
        ex.wait(in_refs, out_refs, sems)

    return pl.pallas_call(body, name=name, in_specs=ex.any_specs, out_specs=ex.any_specs, out_shape=ex.out_shape,
                          scratch_shapes=ex.sem_shapes)(*ex.bufs)


N_CHIP = 4


def _pair_exchange(name, blocks):
    def body(b_ref, got_ref, send_sems, recv_sems):
        x, y, c = lax.axis_index("x"), lax.axis_index("y"), lax.axis_index("c")
        copies = [pltpu.make_async_remote_copy(
            src_ref=b_ref.at[2 * q + 1 - c], dst_ref=got_ref.at[q], send_sem=send_sems.at[q],
            recv_sem=recv_sems.at[q], device_id=(x, y, 1 - c), device_id_type=pl.DeviceIdType.MESH)
            for q in range(N_CHIP)]
        for cp in copies:
            cp.start()
        for cp in copies:
            cp.wait_recv()
        for cp in copies:
            cp.wait_send()

    any_spec = pl.BlockSpec(memory_space=pl.ANY)
    return pl.pallas_call(
        body, name=name, in_specs=[any_spec], out_specs=any_spec,
        out_shape=jax.ShapeDtypeStruct((N_CHIP,) + blocks.shape[1:], blocks.dtype),
        scratch_shapes=[pltpu.SemaphoreType.DMA((N_CHIP,))] * 2,
    )(blocks)


def _pair_sum(name, blocks, got, core):
    _, R, Wd = blocks.shape

    def body(c_ref, a_ref, b_ref, o_ref):
        o_ref[...] = (a_ref[...].astype(F32) + b_ref[...].astype(F32)).astype(o_ref.dtype)

    return pl.pallas_call(
        body, name=name,
        grid_spec=pltpu.PrefetchScalarGridSpec(
            num_scalar_prefetch=1, grid=(N_CHIP,),
            in_specs=[pl.BlockSpec((None, R, Wd), lambda q, c_ref: (2 * q + c_ref[0], 0, 0)),
                      pl.BlockSpec((None, R, Wd), lambda q, c_ref: (q, 0, 0))],
            out_specs=pl.BlockSpec((None, R, Wd), lambda q, c_ref: (q, 0, 0))),
        out_shape=jax.ShapeDtypeStruct(got.shape, blocks.dtype),
        compiler_params=pltpu.CompilerParams(dimension_semantics=("parallel",), vmem_limit_bytes=VMEM_LIMIT),
    )(core, blocks, got)


def _chip_copies(s_ref, land_ref, send_sems, recv_sems):
    x, y, c = lax.axis_index("x"), lax.axis_index("y"), lax.axis_index("c")
    my_q = 2 * x + y
    sends, recvs = [], []
    for kbits in range(1, N_CHIP):
        px = 1 - x if kbits & 2 else x
        py = 1 - y if kbits & 1 else y
        peer_q = 2 * px + py
        sends.append(pltpu.make_async_remote_copy(
            src_ref=s_ref.at[peer_q], dst_ref=land_ref.at[my_q], send_sem=send_sems[kbits - 1],
            recv_sem=recv_sems[kbits - 1], device_id=(px, py, c), device_id_type=pl.DeviceIdType.MESH))
        recvs.append(pltpu.make_async_remote_copy(
            src_ref=s_ref.at[peer_q], dst_ref=land_ref.at[peer_q], send_sem=send_sems[kbits - 1],
            recv_sem=recv_sems[kbits - 1], device_id=(px, py, c), device_id_type=pl.DeviceIdType.MESH))
    return sends, recvs


_HBM = pl.BlockSpec(memory_space=pltpu.HBM)
_SEM = pl.BlockSpec(memory_space=pltpu.SEMAPHORE)
N_CHIP_SEMS = 2 * (N_CHIP - 1)


def _scatter_copies(b_refs, land_refs, send_sems, recv_sems):
    x, y, c = lax.axis_index("x"), lax.axis_index("y"), lax.axis_index("c")
    me = 4 * x + 2 * y + c
    sends, recvs = [], []
    for kbits in range(1, N_DEV):
        px = 1 - x if kbits & 4 else x
        py = 1 - y if kbits & 2 else y
        pc = 1 - c if kbits & 1 else c
        peer = 4 * px + 2 * py + pc
        for b in range(len(b_refs)):
            s = (kbits - 1) * len(b_refs) + b
            sends.append(pltpu.make_async_remote_copy(
                src_ref=b_refs[b].at[peer], dst_ref=land_refs[b].at[me], send_sem=send_sems[s],
                recv_sem=recv_sems[s], device_id=(px, py, pc), device_id_type=pl.DeviceIdType.MESH))
            recvs.append(pltpu.make_async_remote_copy(
                src_ref=b_refs[b].at[peer], dst_ref=land_refs[b].at[peer], send_sem=send_sems[s],
                recv_sem=recv_sems[s], device_id=(px, py, pc), device_id_type=pl.DeviceIdType.MESH))
    return sends, recvs


def _scatter_start(name, bufs):
    nb = len(bufs)
    n = (N_DEV - 1) * nb

    def body(*refs):
        b_refs, land_refs, outs = refs[:nb], refs[nb:2 * nb], refs[2 * nb:]
        sends, _ = _scatter_copies(b_refs, land_refs, outs[:n], outs[n:2 * n])
        for cp in sends:
            cp.start()
        token = outs[2 * n + 2 * nb]
        token[...] = jnp.zeros_like(token)

    thru = tuple(pltpu.HBM(b.shape, b.dtype) for b in bufs)
    res = pl.pallas_call(
        body, name=name, in_specs=(_HBM,) * (2 * nb),
        out_specs=(_SEM,) * (2 * n) + (_HBM,) * (2 * nb) + (pl.BlockSpec(memory_space=pltpu.VMEM),),
        out_shape=(pltpu.SemaphoreType.DMA(()),) * (2 * n) + thru + thru + (jax.ShapeDtypeStruct((8, LANES), F32),),
        input_output_aliases={i: 2 * n + i for i in range(2 * nb)},
        compiler_params=pltpu.CompilerParams(has_side_effects=pltpu.SideEffectType.DATAFLOW_SIDE_EFFECTING),
    )(*[pltpu.with_memory_space_constraint(b, pltpu.HBM) for b in bufs],
      *[pltpu.with_memory_space_constraint(lax.empty(b.shape, b.dtype), pltpu.HBM) for b in bufs])
    return res[:2 * n], list(res[2 * n:2 * n + nb]), list(res[2 * n + nb:2 * n + 2 * nb]), res[2 * n + 2 * nb]


def _scatter_wait(name, sems, bufs_thru, lands_thru, after):
    nb = len(bufs_thru)
    n = (N_DEV - 1) * nb

    def body(*refs):
        b_refs, land_refs, sem_refs = refs[:nb], refs[nb:2 * nb], refs[2 * nb:2 * nb + 2 * n]
        sends, recvs = _scatter_copies(b_refs, land_refs, sem_refs[:n], sem_refs[n:])
        for cp in sends:
            cp.wait_send()
        for cp in recvs:
            cp.wait_recv()

    thru = tuple(pltpu.HBM(b.shape, b.dtype) for b in bufs_thru)
    res = pl.pallas_call(
        body, name=name, in_specs=(_HBM,) * (2 * nb) + (_SEM,) * (2 * n) + (pl.BlockSpec(memory_space=pl.ANY),),
        out_specs=(_HBM,) * (2 * nb), out_shape=thru + thru,
        input_output_aliases={i: i for i in range(2 * nb)},
        compiler_params=pltpu.CompilerParams(has_side_effects=pltpu.SideEffectType.DATAFLOW_SIDE_EFFECTING),
    )(*bufs_thru, *lands_thru, *sems, after)
    return list(res[:nb]), list(res[nb:])


def _chip_exchange_start(name, sums):
    def body(s_ref, land_ref, *outs):
        sems, token = outs[:N_CHIP_SEMS], outs[N_CHIP_SEMS + 2]
        sends, _ = _chip_copies(s_ref, land_ref, sems[:N_CHIP - 1], sems[N_CHIP - 1:])
        for cp in sends:
            cp.start()
        token[...] = jnp.zeros_like(token)

    res = pl.pallas_call(
        body, name=name, in_specs=(_HBM, _HBM),
        out_specs=(_SEM,) * N_CHIP_SEMS + (_HBM, _HBM, pl.BlockSpec(memory_space=pltpu.VMEM)),
        out_shape=(pltpu.SemaphoreType.DMA(()),) * N_CHIP_SEMS
        + (pltpu.HBM(sums.shape, sums.dtype), pltpu.HBM(sums.shape, sums.dtype), jax.ShapeDtypeStruct((8, LANES), F32)),
        input_output_aliases={0: N_CHIP_SEMS, 1: N_CHIP_SEMS + 1},
        compiler_params=pltpu.CompilerParams(has_side_effects=pltpu.SideEffectType.DATAFLOW_SIDE_EFFECTING),
    )(pltpu.with_memory_space_constraint(sums, pltpu.HBM),
      pltpu.with_memory_space_constraint(lax.empty(sums.shape, sums.dtype), pltpu.HBM))
    return res[:N_CHIP_SEMS], res[N_CHIP_SEMS], res[N_CHIP_SEMS + 1], res[N_CHIP_SEMS + 2]


def _chip_exchange_wait(name, sems, sums_thru, land_thru, after):
    def body(s_ref, land_ref, *rest):
        sems = rest[:N_CHIP_SEMS]
        sends, recvs = _chip_copies(s_ref, land_ref, sems[:N_CHIP - 1], sems[N_CHIP - 1:])
        for cp in sends:
            cp.wait_send()
        for cp in recvs:
            cp.wait_recv()

    return pl.pallas_call(
        body, name=name, in_specs=(_HBM, _HBM) + (_SEM,) * N_CHIP_SEMS + (pl.BlockSpec(memory_space=pl.ANY),),
        out_specs=(_HBM, _HBM),
        out_shape=(pltpu.HBM(sums_thru.shape, sums_thru.dtype), pltpu.HBM(sums_thru.shape, sums_thru.dtype)),
        input_output_aliases={0: 0, 1: 1},
        compiler_params=pltpu.CompilerParams(has_side_effects=pltpu.SideEffectType.DATAFLOW_SIDE_EFFECTING),
    )(sums_thru, land_thru, *sems, after)


def _adamw(name, slots, w, m, v, tr=256):
    unit_mid = w.ndim == 3 and w.shape[1] == 1 and w.shape[0] > 1
    R, Wd = (w.shape[0], w.shape[2]) if unit_mid else w.shape[-2:]
    depth_axis = w.ndim == 3 and not unit_mid
    if R % tr == 0:
        tc = Wd
    else:
        tr, tc = R, (256 if (Wd % 256 == 0 and R > 256) else Wd)
    at = (slice(None), 0, slice(None)) if unit_mid else Ellipsis

    def body(s_ref, w_ref, m_ref, v_ref, g_out, d_out, m_out, v_out):
        g = s_ref[0].astype(F32)
        for j in range(1, slots.shape[0]):
            g = g + s_ref[j].astype(F32)
        m_new = ADAM_B1 * m_ref[at] + (1.0 - ADAM_B1) * g
        v_new = ADAM_B2 * v_ref[at] + (1.0 - ADAM_B2) * jnp.square(g)
        m_hat = m_new / (1.0 - ADAM_B1 ** ADAM_STEP)
        v_hat = v_new / (1.0 - ADAM_B2 ** ADAM_STEP)
        g_out[at] = g
        d_out[at] = -ADAM_LR * (m_hat / (jnp.sqrt(v_hat) + ADAM_EPS) + ADAM_WD * w_ref[at])
        m_out[at] = m_new
        v_out[at] = v_new

    if unit_mid:
        row = pl.BlockSpec((tr, 1, tc), lambda i, j: (i, 0, j))
    elif depth_axis:
        row = pl.BlockSpec((None, tr, tc), lambda i, j: (0, i, j))
    else:
        row = pl.BlockSpec((tr, tc), lambda i, j: (i, j))
    return pl.pallas_call(
        body, name=name, grid=(R // tr, Wd // tc),
        in_specs=[pl.BlockSpec((slots.shape[0], tr, tc), lambda i, j: (0, i, j)), row, row, row],
        out_specs=[row] * 4, out_shape=[jax.ShapeDtypeStruct(w.shape, F32)] * 4,
        compiler_params=pltpu.CompilerParams(dimension_semantics=("parallel", "parallel"),
                                             vmem_limit_bytes=VMEM_LIMIT),
    )(slots, w, m, v)


PACK_W = 1024
PACKED = [(n, s) for n, s in REPLICATED if n != "sgu_w"]
_SMALL_SIZES = [int(np.prod(s)) for _, s in PACKED]
_SMALL_ROWS = _round_up(_round_up(sum(_SMALL_SIZES) + PACK_W, PACK_W) // PACK_W, 8)
_LOSS_AT = sum(_SMALL_SIZES)
W_IN_SHARD = P_TOTAL // N_DEV


def _pack_rows(parts, rows, dtype):
    flat = jnp.concatenate([p.reshape(-1).astype(dtype) for p in parts])
    return jnp.pad(flat, (0, rows * PACK_W - flat.shape[0])).reshape(rows, PACK_W)


def _w_in_groups_t(blocks):
    wt = blocks.reshape(P_TOTAL, D)
    o, c = 2 * D, 2 * D + 3 * D
    z = lambda r: jnp.zeros((r, D), wt.dtype)
    rw = jnp.concatenate([wt[o:c], wt[c:c + L_W], z(128 - L_W), wt[c + L_W:c + L_W + L_A], z(128 - L_A),
                          wt[c + L_W + L_A:o + C_B], z(256 - L_G)], axis=0)
    return wt[:o], rw, wt[o + C_B:]


def _w_in_grad_blocks(g_sgu_t, g_rw_t, g_gate_t):
    c = 3 * D
    full = jnp.concatenate([g_sgu_t, g_rw_t[:c], g_rw_t[c:c + L_W], g_rw_t[c + 128:c + 128 + L_A],
                            g_rw_t[c + 256:c + 256 + L_G], g_gate_t], axis=0)
    return full.reshape(N_DEV, W_IN_SHARD, D)


def _mesh_index():
    me = 4 * lax.axis_index("x") + 2 * lax.axis_index("y") + lax.axis_index("c")
    return me.astype(jnp.int32).reshape(1)


def _fill_slot(name, dst, src, idx, src_idx=None):
    R, Wd = dst.shape[1:]
    scalars = [idx] if src_idx is None else [idx, src_idx]
    if src_idx is None:
        src_spec = pl.BlockSpec((R, Wd), lambda i, *s: (0, 0))
    else:
        src_spec = pl.BlockSpec((None, R, Wd), lambda i, *s: (s[1][0], 0, 0))

    def body(*refs):
        src_ref, out_ref = refs[len(scalars) + 1], refs[len(scalars) + 2]
        out_ref[...] = src_ref[...]

    return pl.pallas_call(
        body, name=name,
        grid_spec=pltpu.PrefetchScalarGridSpec(
            num_scalar_prefetch=len(scalars), grid=(1,),
            in_specs=[pl.BlockSpec(memory_space=pl.ANY), src_spec],
            out_specs=pl.BlockSpec((None, R, Wd), lambda i, *s: (s[0][0], 0, 0))),
        out_shape=jax.ShapeDtypeStruct(dst.shape, dst.dtype),
        input_output_aliases={len(scalars): 0},
        compiler_params=pltpu.CompilerParams(vmem_limit_bytes=VMEM_LIMIT),
    )(*scalars, dst, src)


class TwoLevelGather:
    def __init__(self, bufs, skip_own=()):
        self.bufs, self.nb, self.skip_own = list(bufs), len(bufs), tuple(skip_own)
        self.any_specs = [pl.BlockSpec(memory_space=pl.ANY)] * self.nb
        self.out_shape = [jax.ShapeDtypeStruct((N_DEV,) + b.shape, b.dtype) for b in self.bufs]
        self.sem_shapes = [pltpu.SemaphoreType.DMA((7 * self.nb,)), pltpu.SemaphoreType.DMA((7 * self.nb,)),
                           pltpu.SemaphoreType.DMA((self.nb,))]

    def _copies(self, in_refs, out_refs, sems):
        send_sems, recv_sems, local_sems = sems
        nb = self.nb
        x, y, c = lax.axis_index("x"), lax.axis_index("y"), lax.axis_index("c")
        me, sibling = (x, y, c), (x, y, 1 - c)
        chips = [(1 - x, y), (x, 1 - y), (1 - x, 1 - y)]

        def slot(b, dev):
            return out_refs[b].at[4 * dev[0] + 2 * dev[1] + dev[2]]

        def copy(b, k, block, to, own=False):
            return pltpu.make_async_remote_copy(
                src_ref=in_refs[b] if own else slot(b, block), dst_ref=slot(b, block),
                send_sem=send_sems.at[7 * b + k], recv_sem=recv_sems.at[7 * b + k], device_id=to,
                device_id_type=pl.DeviceIdType.MESH)

        cp = {}
        cp["local"] = [pltpu.make_async_copy(in_refs[b], slot(b, me), local_sems.at[b]) for b in range(nb)
                       if b not in self.skip_own]
        cp["first"] = [copy(b, 0, me, sibling, own=True) for b in range(nb)]
        cp["first"] += [copy(b, 1 + j, me, (*chip, c), own=True) for j, chip in enumerate(chips) for b in range(nb)]
        cp["over_ici"] = [copy(b, 1 + j, (*chip, c), me) for j, chip in enumerate(chips) for b in range(nb)]
        cp["passed"] = [copy(b, 4 + j, (*chip, c), sibling) for j, chip in enumerate(chips) for b in range(nb)]
        cp["from_sibling"] = [copy(b, 0, sibling, me) for b in range(nb)]
        cp["from_sibling"] += [copy(b, 4 + j, (*chip, 1 - c), me) for j, chip in enumerate(chips) for b in range(nb)]
        return cp

    def start(self, in_refs, out_refs, sems):
        cp = self._copies(in_refs, out_refs, sems)
        for c in cp["first"] + cp["local"]:
            c.start()

    def forward(self, in_refs, out_refs, sems):
        cp = self._copies(in_refs, out_refs, sems)
        for arrived, onward in zip(cp["over_ici"], cp["passed"]):
            arrived.wait_recv()
            onward.start()

    def finish(self, in_refs, out_refs, sems):
        cp = self._copies(in_refs, out_refs, sems)
        for c in cp["from_sibling"]:
            c.wait_recv()
        for c in cp["first"] + cp["passed"]:
            c.wait_send()
        for c in cp["local"]:
            c.wait()

    def schedule(self, n_steps):
        return [(0, self.start), (max(n_steps - 3, 0), self.forward), (n_steps - 1, self.finish)]


def _all_gather_two_level(name, bufs, skip_own=()):
    ex = TwoLevelGather(bufs, skip_own)

    def body(*refs):
        args = refs[:ex.nb], refs[ex.nb:2 * ex.nb], refs[2 * ex.nb:]
        ex.start(*args)
        ex.forward(*args)
        ex.finish(*args)

    return pl.pallas_call(body, name=name, in_specs=ex.any_specs, out_specs=ex.any_specs, out_shape=ex.out_shape,
                          scratch_shapes=ex.sem_shapes)(*ex.bufs)


def _cols_from_blocks(blk):
    return jnp.transpose(blk, (1, 0, 2)).reshape(blk.shape[1], -1)


def _cols_to_blocks(g):
    r, c = g.shape
    return jnp.transpose(g.reshape(r, N_DEV, c // N_DEV), (1, 0, 2))


FIRST_WEIGHTS = ["w_in", "shift_b", "w_lora_w", "a_lora_w", "g_lora_w"]
LATE_WEIGHTS = ["w_proj_a", "w_proj_b", "w_out", "w_ffn1", "w_ffn2"]
SCAN_CARRIED = ["w_proj_a", "w_proj_b", "w_out"]
FFN_WEIGHTS = ["w_ffn1", "w_ffn2"]


def _late_weights(shards):
    ex = TwoLevelGather([shards[n].astype(BF16) for n in LATE_WEIGHTS])

    def finish(results):
        got = dict(zip(LATE_WEIGHTS, results))
        W = {n: got[n].reshape(-1, D) for n in ("w_proj_a", "w_proj_b", "w_out", "w_ffn2")}
        W["w_ffn1"] = got["w_ffn1"].reshape(N_DEV, D, -1)
        return W
    return ex, finish


def _gather_weights(shards):
    def payload(n):
        if n == "w_in":
            return jnp.transpose(shards[n][0]).astype(BF16)
        return shards[n] if n == "shift_b" else shards[n].astype(BF16)
    payloads = [payload(n) for n in FIRST_WEIGHTS]
    got = dict(zip(FIRST_WEIGHTS, _all_gather_two_level("weight_all_gather", payloads, skip_own=(0,))))
    got["w_in"] = _fill_slot("w_in_own_slot", got["w_in"], payloads[0], _mesh_index())
    W = {}
    W["w_sgu_t"], W["w_rw_t"], W["w_gate_t"] = _w_in_groups_t(got["w_in"])
    z = lambda r, c, dt: jnp.zeros((r, c), dt)
    W["w_lora"] = jnp.concatenate([_cols_from_blocks(got["w_lora_w"][:, 0]).astype(F32), z(128 - L_W, D, F32)], axis=0)
    W["a_lora"] = jnp.concatenate([_cols_from_blocks(got["a_lora_w"][:, 0]).astype(F32), z(128 - L_A, D, F32)], axis=0)
    W["g_lora"] = jnp.concatenate([_cols_from_blocks(got["g_lora_w"][:, 0]).astype(F32), z(256 - L_G, D, F32)], axis=0)
    sb = _cols_from_blocks(got["shift_b"][:, 0])
    W["sb"] = jnp.concatenate([sb[:, :3 * D], sb[:, 3 * D:3 * D + L_W], z(2, 128 - L_W, F32),
                               sb[:, 3 * D + L_W:3 * D + L_W + L_A], z(2, 128 - L_A, F32),
                               sb[:, 3 * D + L_W + L_A:], z(2, 256 - L_G, F32)], axis=1)
    return W


def _replicated_weights(rep):
    W = {n: rep[n] for n in ("g_mix", "sgu_ln_w", "sgu_ln_b", "w0", "a0", "k_k", "k_a", "r_k", "ln_x_w", "ln_x_b",
                             "g_ffn")}
    W["g_final"] = rep["g_final"].reshape(1, D)
    W["sgu_w"] = rep["sgu_w"][0]
    W["sgu_bt"] = jnp.transpose(rep["sgu_b"][0])
    return W


def _late_grad_blocks(G):
    return Exchange([G[n].reshape(N_DEV, -1, D) for n in SCAN_CARRIED]
                    + [G["sgu_w"].reshape(SGU_G * SGU_C, SGU_C).astype(GRAD_PAYLOAD)],
                    [False] * len(SCAN_CARRIED) + [True])


def _first_grad_blocks(G):
    sbg = G["sb"]
    c = 3 * D
    sb = jnp.concatenate([sbg[:, :c], sbg[:, c:c + L_W], sbg[:, c + 128:c + 128 + L_A],
                          sbg[:, c + 256:c + 256 + L_G]], axis=1)
    return {
        "shift_b": _cols_to_blocks(sb),
        "w_lora_w": _cols_to_blocks(G["w_lora"][:L_W]), "a_lora_w": _cols_to_blocks(G["a_lora"][:L_A]),
        "g_lora_w": _cols_to_blocks(G["g_lora"][:L_G]),
    }


def _replicated_grads(G):
    small = {n: G[n] for n in ("g_mix", "sgu_ln_w", "sgu_ln_b", "w0", "a0", "k_k", "k_a", "r_k", "ln_x_w", "ln_x_b",
                               "g_ffn", "g_final")}
    small["sgu_w"] = G["sgu_w"]
    small["sgu_b"] = jnp.transpose(G["sgu_bt"])
    return small


def kernel(x, g_mix, w_in, sgu_ln_w, sgu_ln_b, sgu_w, sgu_b, w_proj_a, shift_b, w_lora_w, w0, a_lora_w, a0, g_lora_w, k_k, k_a, r_k, ln_x_w, ln_x_b, w_proj_b, w_out, g_ffn, w_ffn1, w_ffn2, g_final, loss_target, m_g_mix, m_w_in, m_sgu_ln_w, m_sgu_ln_b, m_sgu_w, m_sgu_b, m_w_proj_a, m_shift_b, m_w_lora_w, m_w0, m_a_lora_w, m_a0, m_g_lora_w, m_k_k, m_k_a, m_r_k, m_ln_x_w, m_ln_x_b, m_w_proj_b, m_w_out, m_g_ffn, m_w_ffn1, m_w_ffn2, m_g_final, v_g_mix, v_w_in, v_sgu_ln_w, v_sgu_ln_b, v_sgu_w, v_sgu_b, v_w_proj_a, v_shift_b, v_w_lora_w, v_w0, v_a_lora_w, v_a0, v_g_lora_w, v_k_k, v_k_a, v_r_k, v_ln_x_w, v_ln_x_b, v_w_proj_b, v_w_out, v_g_ffn, v_w_ffn1, v_w_ffn2, v_g_final):
    env = dict(locals())
    weights = {n: env[n] for n in WEIGHT_ORDER}
    moms = {n: env["m_" + n] for n in WEIGHT_ORDER}
    vars_ = {n: env["v_" + n] for n in WEIGHT_ORDER}

    shards = {n: weights[n] for n, _, _ in SHARDED}
    W = _gather_weights(shards)
    W.update(_replicated_weights({n: weights[n] for n, _ in REPLICATED}))
    in_flight = {}

    def send_w_in_grads(G):
        blocks = _w_in_grad_blocks(G["w_sgu_t"], G["w_rw_t"], G["w_gate_t"])
        got = _pair_exchange("grad_pair_exchange", blocks)
        core = lax.axis_index("c").astype(jnp.int32).reshape(1)
        sums = _pair_sum("grad_pair_sum", blocks, got, core)
        in_flight["sems"], in_flight["sums"], in_flight["land"], token = _chip_exchange_start("grad_chip_start", sums)
        return token

    def send_ffn_grads(G):
        in_flight["ffn"] = _scatter_start("grad_ffn_start", [G["w_ffn1"], G["w_ffn2"].reshape(N_DEV, -1, D)])
        return in_flight["ffn"][3]

    loss_part, dx, G, late_slots = _local_step(x[0], loss_target[0], W, late_weights=_late_weights(shards),
                                               early_grads=_late_grad_blocks, w_in_grads_ready=send_w_in_grads,
                                               ffn_grads_ready=send_ffn_grads)

    slots = dict(zip(SCAN_CARRIED, late_slots))
    me = _mesh_index()
    sent, landed = _scatter_wait("grad_ffn_wait", *in_flight["ffn"][:3], after=dx)
    for i, n in enumerate(FFN_WEIGHTS):
        slots[n] = _fill_slot("grad_own_slot_" + n, landed[i], sent[i], me, src_idx=me)
    blocks = _first_grad_blocks(G)
    small = _replicated_grads(G)
    small_parts = [small[n] for n, _ in PACKED] + [jnp.full((PACK_W,), loss_part, F32)]
    rest = [n for n in FIRST_WEIGHTS if n != "w_in"]
    res = _exchange("grad_exchange", [blocks[n] for n in rest] + [_pack_rows(small_parts, _SMALL_ROWS, F32)],
                    [False] * len(rest) + [True])
    slots.update(zip(rest, res[:-1]))
    small_slots = res[-1]
    sums, chip_slots = _chip_exchange_wait("grad_chip_wait", in_flight["sems"], in_flight["sums"], in_flight["land"],
                                           after=small_slots)
    my_chip = (2 * lax.axis_index("x") + lax.axis_index("y")).astype(jnp.int32).reshape(1)
    slots["w_in"] = _fill_slot("grad_own_slot", chip_slots, sums, my_chip, src_idx=my_chip)

    outs = [dict(), dict(), dict(), dict()]
    for n, _, _ in SHARDED:
        if n == "w_in":
            res = _adamw("adamw_" + n, slots[n], *[jnp.transpose(t, (2, 0, 1)) for t in (weights[n], moms[n], vars_[n])])
            res = [jnp.transpose(t, (1, 2, 0)) for t in res]
        else:
            res = _adamw("adamw_" + n, slots[n], weights[n], moms[n], vars_[n])
        for k in range(4):
            outs[k][n] = res[k]

    sgu_shape = (SGU_G * SGU_C, SGU_C)
    res = _adamw("adamw_sgu_w", late_slots[len(SCAN_CARRIED)], *[t.reshape(sgu_shape) for t in
                                                                  (weights["sgu_w"], moms["sgu_w"], vars_["sgu_w"])])
    for k in range(4):
        outs[k]["sgu_w"] = res[k].reshape(weights["sgu_w"].shape)

    def packed(d):
        return _pack_rows([d[n] for n, _ in PACKED], _SMALL_ROWS, F32)
    small_out = _adamw("adamw_replicated", small_slots, packed(weights), packed(moms), packed(vars_))
    for k in range(4):
        flat = small_out[k].reshape(-1)
        off = 0
        for (n, s), size in zip(PACKED, _SMALL_SIZES):
            outs[k][n] = flat[off:off + size].reshape(s)
            off += size
    loss = small_out[0].reshape(-1)[_LOSS_AT]
    return (loss, dx[None], *[outs[0][n] for n in WEIGHT_ORDER], *[outs[1][n] for n in WEIGHT_ORDER],
            *[outs[2][n] for n in WEIGHT_ORDER], *[outs[3][n] for n in WEIGHT_ORDER])
```

```python
import functools
import numpy as np
import jax
import jax.numpy as jnp
from jax import lax
from jax.experimental import pallas as pl
from jax.experimental.pallas import tpu as pltpu

F32 = jnp.float32
BF16 = jnp.bfloat16

D = 1024
NH, HN = 16, 64
NP, PW = NH // 2, 2 * HN
SGU_G, SGU_C = 8, 128
L_W, L_A, L_G = 64, 64, 160
C_B = 3 * D + L_W + L_A + L_G
P_TOTAL = 2 * D + C_B + 2 * D
D_FF = 4 * D
RW_INT = 3 * D + 128 + 128 + 256
NORM_EPS, LN_EPS, GN_EPS = 1e-6, 1e-5, 64e-5
N_DEV = 8
LANES = 128
SCAN_C = 64
N_KEPT = 4
SOLVE_B = 16
SCAN_PRECISION = lax.Precision.HIGH
SCAN_OUT_PRECISION = lax.Precision.DEFAULT
GRAD_PAYLOAD = BF16
VMEM_LIMIT = 56 * 1024 * 1024
MATMUL_VMEM_BUDGET = 40 * 1024 * 1024
STEP_COST_BYTES = 512 * 1024
HBM_COST_RATIO = 3
ACC_PASS_WEIGHT = 4

ADAM_LR, ADAM_B1, ADAM_B2, ADAM_EPS, ADAM_WD, ADAM_STEP = 0.001, 0.9, 0.999, 1e-08, 0.01, 10

SHARDED = [
    ("w_in", (D, P_TOTAL), 1), ("w_proj_a", (D, D), 0), ("shift_b", (2, C_B), 1), ("w_lora_w", (L_W, D), 1),
    ("a_lora_w", (L_A, D), 1), ("g_lora_w", (L_G, D), 1), ("w_proj_b", (D, D), 0), ("w_out", (D, D), 0),
    ("w_ffn1", (D, D_FF), 1), ("w_ffn2", (D_FF, D), 0),
]
REPLICATED = [
    ("g_mix", (1, D)), ("sgu_ln_w", (1, D)), ("sgu_ln_b", (1, D)), ("sgu_w", (1, SGU_G, SGU_C, SGU_C)),
    ("sgu_b", (1, SGU_G, SGU_C)), ("w0", (1, D)), ("a0", (1, D)), ("k_k", (1, D)), ("k_a", (1, D)), ("r_k", (1, D)),
    ("ln_x_w", (1, D)), ("ln_x_b", (1, D)), ("g_ffn", (1, D)), ("g_final", (D,)),
]
WEIGHT_ORDER = ["g_mix", "w_in", "sgu_ln_w", "sgu_ln_b", "sgu_w", "sgu_b", "w_proj_a", "shift_b", "w_lora_w", "w0",
                "a_lora_w", "a0", "g_lora_w", "k_k", "k_a", "r_k", "ln_x_w", "ln_x_b", "w_proj_b", "w_out", "g_ffn",
                "w_ffn1", "w_ffn2", "g_final"]


def _round_up(n, m):
    return (n + m - 1) // m * m


def _pick(n, target):
    if n <= target:
        return n
    best = None
    for t in range(LANES, target + 1, LANES):
        if n % t == 0:
            best = t
    assert best is not None, (n, target)
    return best


def _matmul(name, a, b, mode, out_dtype=F32, tm=2048, tn=1024, tk=4096, out_blocks=None, epilogue=None, extras=(),
            out_dtypes=(), after=None, whole_rows=False, out_widths=None):
    b_blocks = b.shape[0] if b.ndim == 3 else None
    bshape = b.shape if b.ndim == 2 else (b.shape[1], b.shape[0] * b.shape[2])
    if mode == "nn":
        (M, K), (K2, N) = a.shape, bshape
    elif mode == "nt":
        (M, K), (N, K2) = a.shape, bshape
    else:
        (K, M), (K2, N) = a.shape, bshape
    assert K == K2, (name, a.shape, b.shape)
    assert b_blocks is None or mode != "tn"
    assert out_blocks is None or mode == "tn"
    tn = min(tn, N // (out_blocks or 1), bshape[1] // b_blocks if (b_blocks and mode == "nn") else tn)
    blocked_k = bool(b_blocks) and mode == "nt"
    tm, tn, tk = _pick(M, tm), _pick(N, tn), (K if blocked_k else _pick(K, tk))

    def vmem_bytes(tm, tk):
        tiles = tm * tk * a.dtype.itemsize + tk * tn * b.dtype.itemsize
        for i, dt in enumerate(out_dtypes if epilogue else (out_dtype,)):
            tiles += tm * (out_widths[i] if out_widths else tn) * jnp.dtype(dt).itemsize
        for x in extras:
            arr = x[0] if isinstance(x, tuple) else x
            tiles += (tm if arr.shape[0] > 1 else 1) * tn * arr.dtype.itemsize
        return 2 * tiles + (tm * tn * 4 if K // tk > 1 else 0)

    def cost(tm, tk):
        ni, nj, nk = M // tm, N // tn, K // tk
        steps = ni * nj * nk
        acc_passes = steps * tm * tn * 8 * ACC_PASS_WEIGHT if nk > 1 else 0
        a_reads = M * K * a.dtype.itemsize * (nj if nk > 1 else 1)
        b_reads = K * N * b.dtype.itemsize * (ni if (nj > 1 or nk > 1) else 1)
        return (steps * STEP_COST_BYTES + acc_passes + vmem_bytes(tm, tk) // 2
                + HBM_COST_RATIO * (a_reads + b_reads))

    options = [(m, k) for m in ({M} if whole_rows else {_pick(M, max(t, LANES)) for t in (tm, tm // 2, tm // 4)})
               for k in ({K} if blocked_k else {_pick(K, max(t, LANES)) for t in (tk, tk // 2, tk // 4)})
               if vmem_bytes(m, k) <= MATMUL_VMEM_BUDGET]
    tm, tk = min(options, key=lambda o: cost(*o))
    nk = K // tk
    dims = {"nn": (((1,), (0,)), ((), ())), "nt": (((1,), (1,)), ((), ())), "tn": (((0,), (0,)), ((), ()))}[mode]

    n_x, n_o = len(extras), len(out_dtypes) if epilogue else 1
    n_after = 0 if after is None else 1

    def body(a_ref, b_ref, *rest):
        x_refs, o_refs, acc = rest[:n_x], rest[n_x + n_after:n_x + n_after + n_o], rest[n_x + n_after + n_o:]
        if blocked_k:
            bw = b.shape[2]
            part = sum(lax.dot_general(a_ref[:, blk * bw:(blk + 1) * bw].astype(BF16), b_ref[blk].astype(BF16), dims,
                                       preferred_element_type=F32) for blk in range(b_blocks))
        else:
            part = lax.dot_general(a_ref[...].astype(BF16), b_ref[...].astype(BF16), dims, preferred_element_type=F32)

        def finish(res):
            outs = epilogue(res, *[r[...] for r in x_refs]) if epilogue else (res,)
            for r, v in zip(o_refs, outs):
                r[...] = v.astype(r.dtype)

        if nk == 1:
            finish(part)
            return
        acc_ref, k = acc[0], pl.program_id(2)

        @pl.when(k == 0)
        def _():
            acc_ref[...] = part

        @pl.when(k > 0)
        def _():
            acc_ref[...] += part

        @pl.when(k == nk - 1)
        def _():
            finish(acc_ref[...])

    a_spec = {"nn": pl.BlockSpec((tm, tk), lambda i, j, k: (i, k)), "nt": pl.BlockSpec((tm, tk), lambda i, j, k: (i, k)),
              "tn": pl.BlockSpec((tk, tm), lambda i, j, k: (k, i))}[mode]
    b_spec = {"nn": pl.BlockSpec((tk, tn), lambda i, j, k: (k, j)), "nt": pl.BlockSpec((tn, tk), lambda i, j, k: (j, k)),
              "tn": pl.BlockSpec((tk, tn), lambda i, j, k: (k, j))}[mode]
    if b_blocks and mode == "nn":
        per = b.shape[2] // tn
        b_spec = pl.BlockSpec((None, tk, tn), lambda i, j, k: (j // per, k, j % per))
    elif b_blocks:
        b_spec = pl.BlockSpec((b_blocks, tn, b.shape[2]), lambda i, j, k: (0, j, 0))
    out_spec = pl.BlockSpec((tm, tn), lambda i, j, k: (i, j))
    out_shape = jax.ShapeDtypeStruct((M, N), out_dtype)
    if out_blocks:
        per_o = N // out_blocks // tn
        out_spec = pl.BlockSpec((None, tm, tn), lambda i, j, k: (j // per_o, i, j % per_o))
        out_shape = jax.ShapeDtypeStruct((out_blocks, M, N // out_blocks), out_dtype)
    epi_widths = list(out_widths) if out_widths else [N] * n_o
    assert all(w == N for w in epi_widths) or N == tn
    epi_specs = [pl.BlockSpec((tm, tn if w == N else w), lambda i, j, k: (i, j)) for w in epi_widths]
    x_specs, x_args = [], []
    for x in extras:
        arr, off = x if isinstance(x, tuple) else (x, 0)
        if arr.shape[0] == 1:
            x_specs.append(pl.BlockSpec((1, tn), lambda i, j, k: (0, j)))
        else:
            x_specs.append(pl.BlockSpec((tm, tn), lambda i, j, k, off=off: (i, j + off)))
        x_args.append(arr)
    res = pl.pallas_call(
        body, name=name, grid=(M // tm, N // tn, nk),
        in_specs=[a_spec, b_spec] + x_specs + [pl.BlockSpec(memory_space=pl.ANY)] * n_after,
        out_specs=epi_specs if epilogue else out_spec,
        out_shape=[jax.ShapeDtypeStruct((M, w), dt) for w, dt in zip(epi_widths, out_dtypes)] if epilogue else out_shape,
        scratch_shapes=[pltpu.VMEM((tm, tn), F32)] if nk > 1 else [],
        compiler_params=pltpu.CompilerParams(dimension_semantics=("parallel", "parallel", "arbitrary"),
                                             vmem_limit_bytes=VMEM_LIMIT),
    )(a, b, *x_args, *([after] if n_after else []))
    return res


class Rows:
    def __init__(self, arr, width=None, cb=0):
        self.arr, self.width, self.cb = arr, (arr.shape[1] if width is None else width), cb


class Heads:
    def __init__(self, arr):
        self.arr = arr


class Halo:
    def __init__(self, arr, side):
        self.arr, self.side = arr, side


def _rows_call(name, fn, ins, consts, outs, accs=(), tm=512, with_pid=False):
    T = next(o.arr.shape[1] if isinstance(o, Heads) else o.arr.shape[0] for o in ins if not isinstance(o, Halo))
    tm = min(tm, T)
    n_tiles = T // tm
    n_in, n_c, n_out = len(ins), len(consts), len(outs)
    in_specs, args = [], []
    for o in ins:
        if isinstance(o, Rows):
            in_specs.append(pl.BlockSpec((tm, o.width), lambda i, cb=o.cb: (i, cb)))
        elif isinstance(o, Heads):
            in_specs.append(pl.BlockSpec((NP, tm, PW), lambda i: (0, i, 0)))
        else:
            w = o.arr.shape[1]
            if o.side < 0:
                in_specs.append(pl.BlockSpec((8, w), lambda i: (jnp.maximum(i * (tm // 8) - 1, 0), 0)))
            else:
                in_specs.append(pl.BlockSpec((8, w), lambda i: (jnp.minimum((i + 1) * (tm // 8), T // 8 - 1), 0)))
        args.append(o.arr)
    for c in consts:
        in_specs.append(pl.BlockSpec(c.shape, lambda i, nd=c.ndim: (0,) * nd))
        args.append(c)
    out_specs, out_shape = [], []
    for o in outs:
        if o[0] == "rows":
            out_specs.append(pl.BlockSpec((tm, o[1]), lambda i: (i, 0)))
            out_shape.append(jax.ShapeDtypeStruct((T, o[1]), o[2]))
        else:
            out_specs.append(pl.BlockSpec((NP, tm, PW), lambda i: (0, i, 0)))
            out_shape.append(jax.ShapeDtypeStruct((NP, T, PW), o[1]))
    for shape, dt in accs:
        out_specs.append(pl.BlockSpec(shape, lambda i, nd=len(shape): (0,) * nd))
        out_shape.append(jax.ShapeDtypeStruct(shape, dt))

    def body(*refs):
        i = pl.program_id(0)
        vals = []
        vals = [r[...] for r in refs[:n_in + n_c]]
        res = fn(i, n_tiles, *vals) if with_pid else fn(*vals)
        out_refs = refs[n_in + n_c:]
        for r, v in zip(out_refs[:n_out], res[:n_out]):
            r[...] = v.astype(r.dtype)
        if accs:
            @pl.when(i == 0)
            def _():
                for r in out_refs[n_out:]:
                    r[...] = jnp.zeros_like(r)

            for r, v in zip(out_refs[n_out:], res[n_out:]):
                r[...] += v.astype(r.dtype)

    res = pl.pallas_call(
        body, name=name, grid=(n_tiles,), in_specs=in_specs, out_specs=out_specs, out_shape=out_shape,
        compiler_params=pltpu.CompilerParams(dimension_semantics=("arbitrary",), vmem_limit_bytes=VMEM_LIMIT),
    )(*args)
    return res


def _rms(x, g):
    return x * lax.rsqrt(jnp.mean(x * x, axis=-1, keepdims=True) + NORM_EPS) * g


def _gelu(x):
    return 0.5 * x * (1.0 + lax.erf(x * 0.7071067811865476))


def _sigmoid(x):
    return 1.0 / (1.0 + jnp.exp(-x))


def _bdot(a, b):
    return jnp.dot(a.astype(BF16), b.astype(BF16), preferred_element_type=F32)


def _to_heads(x):
    return jnp.concatenate([x[:, p * PW:(p + 1) * PW][None] for p in range(NP)], axis=0)


def _from_heads(xp):
    return jnp.concatenate([xp[p] for p in range(NP)], axis=-1)


def _head_sum(xp):
    low = lax.broadcasted_iota(jnp.int32, xp.shape, xp.ndim - 1) < HN
    both = jnp.sum(xp, axis=-1, keepdims=True)
    first = jnp.sum(jnp.where(low, xp, 0.0), axis=-1, keepdims=True)
    return jnp.where(low, first, both - first)


def _split_pairs(xp):
    return jnp.concatenate([xp[:, :, :HN], xp[:, :, HN:]], axis=0)


def _join_pairs(xh):
    return jnp.concatenate([xh[:NP], xh[NP:]], axis=-1)


def _sgu_fn(p, ln_w, ln_b, sw, sbt):
    z = _gelu(p)
    u, v = z[:, :D], z[:, D:]
    mu = jnp.mean(v, axis=-1, keepdims=True)
    var = jnp.mean(jnp.square(v - mu), axis=-1, keepdims=True)
    vn = (v - mu) * lax.rsqrt(var + LN_EPS) * ln_w + ln_b
    ri = lax.broadcasted_iota(jnp.int32, (SGU_C, SGU_C), 0)
    ci = lax.broadcasted_iota(jnp.int32, (SGU_C, SGU_C), 1)
    mask = (ci <= ri).astype(F32)
    dg = D // SGU_G
    parts = []
    for g in range(SGU_G):
        parts.append(_bdot(sw[g] * mask, vn[:, g * dg:(g + 1) * dg]) + sbt[:, g:g + 1])
    return u * jnp.concatenate(parts, axis=-1)


def _pre_fn(qr, qk, qv, qxw, qxa, qxg, wl, w0, al, a0, gl, k_k, k_a):
    w = -jax.nn.softplus(-(w0 + _bdot(jnp.tanh(qxw), wl))) - 0.5
    lw = -jnp.exp(w)
    aa = _sigmoid(a0 + _bdot(qxa, al))
    g = _bdot(_sigmoid(qxg), gl)
    kk = _to_heads(qk * k_k)
    kk = kk / jnp.maximum(jnp.sqrt(_head_sum(kk * kk)), 1e-12)
    k2 = qk * (1.0 + (aa - 1.0) * k_a)
    return _to_heads(qr), _to_heads(lw), _to_heads(k2), _to_heads(qv), kk, _to_heads(aa), g


def _post_fn(o, r, k2, v, g, ln_w, ln_b, r_k):
    mu = _head_sum(o) * (1.0 / HN)
    d = o - mu
    var = _head_sum(d * d) * (1.0 / HN)
    on = d * lax.rsqrt(var + GN_EPS) * ln_w + ln_b
    bonus = _head_sum(r * k2 * r_k) * v
    return _from_heads(on + bonus) * g


def _gate_fn(pg, ya, yb):
    return _sigmoid(pg[:, :D]) * ya + _sigmoid(pg[:, D:]) * yb


def _bmm(x, y, cx, cy, out_path=False):
    return lax.dot_general(x, y, (((cx,), (cy,)), ((0,), (0,))),
                           precision=SCAN_OUT_PRECISION if out_path else SCAN_PRECISION, preferred_element_type=F32)


def _unit_lower_inverse(M):
    C = M.shape[1]
    ti = lax.broadcasted_iota(jnp.int32, (C, C), 0)
    tj = lax.broadcasted_iota(jnp.int32, (C, C), 1)
    eye = (ti == tj).astype(F32)
    same = lambda b: (ti // b == tj // b).astype(F32)
    X = -(M * same(SOLVE_B))
    inv = eye + X
    span = 1
    while 2 * span < SOLVE_B:
        X = _bmm(X, X, 2, 1)
        inv = inv + _bmm(inv, X, 2, 1)
        span *= 2
    b = SOLVE_B
    while b < C:
        low = M * (same(2 * b) - same(b))
        inv = inv - _bmm(_bmm(inv, low, 2, 1, out_path=True), inv, 2, 1, out_path=True)
        b *= 2
    return inv


@jax.custom_vjp
def _unit_lower_solve(inv, M, y):
    return _bmm(inv, y, 2, 1)


def _unit_lower_solve_fwd(inv, M, y):
    u = _bmm(inv, y, 2, 1)
    return u, (inv, u)


def _unit_lower_solve_bwd(res, du):
    inv, u = res
    dy = _bmm(inv, du, 1, 1)
    return jnp.zeros_like(inv), -_bmm(dy, u, 2, 2), dy


_unit_lower_solve.defvjp(_unit_lower_solve_fwd, _unit_lower_solve_bwd)


@jax.custom_vjp
def _unit_lower_solved(inv, u, M, y):
    return u


_unit_lower_solved.defvjp(lambda inv, u, M, y: (u, (inv, u)),
                          lambda res, du: (jnp.zeros_like(res[0]), jnp.zeros_like(res[1]))
                          + _unit_lower_solve_bwd(res, du)[1:])


@functools.partial(jax.custom_vjp, nondiff_argnums=(0,))
def _kept(fn, value, *args):
    return value


def _kept_fwd(fn, value, *args):
    return value, args


def _kept_bwd(fn, args, d):
    _, vjp = jax.vjp(fn, *args)
    return (jnp.zeros_like(d),) + tuple(vjp(d))


_kept.defvjp(_kept_fwd, _kept_bwd)


def _sum_over_time(x, reverse):
    C = x.shape[1]
    ti = lax.broadcasted_iota(jnp.int32, (C, C), 0)
    tj = lax.broadcasted_iota(jnp.int32, (C, C), 1)
    ones = jnp.broadcast_to(((tj >= ti) if reverse else (tj <= ti)).astype(BF16), (x.shape[0], C, C))
    hi = x.astype(BF16)
    r1 = x - hi.astype(F32)
    mid = r1.astype(BF16)
    lo = (r1 - mid.astype(F32)).astype(BF16)
    dn = (((2,), (1,)), ((0,), (0,)))
    return sum(lax.dot_general(ones, p, dn, preferred_element_type=F32) for p in (lo, mid, hi))


@jax.custom_vjp
def _time_cumsum(lw):
    return _sum_over_time(lw, reverse=False)


_time_cumsum.defvjp(lambda lw: (_sum_over_time(lw, reverse=False), None),
                    lambda _, d: (_sum_over_time(d, reverse=True),))


def _chunk_fn(S0, r, lw, k, v, kk, a, kept=None):
    C = SCAN_C
    bmm = _bmm
    ti = lax.broadcasted_iota(jnp.int32, (C, C), 0)
    tj = lax.broadcasted_iota(jnp.int32, (C, C), 1)
    incl2 = jnp.concatenate([(tj <= ti).astype(F32)] * 2, axis=1)
    strict = (tj < ti).astype(F32)
    n_mask = jnp.concatenate([jnp.zeros((C, C), F32), strict], axis=1)

    def known(name, fn, *args):
        return fn(*args) if kept is None else _kept(fn, kept[name], *args)

    cum = known("cum", _time_cumsum, lw)
    g_in, g_ex, g_inv = jnp.exp(cum), jnp.exp(cum - lw), jnp.exp(-cum)
    kkt, rt = kk * g_ex, r * g_in
    bk = jnp.concatenate([kk * a * g_inv, k * g_inv], axis=1)
    kr = jnp.concatenate([kkt, rt], axis=1)
    ratios = known("ratios", lambda x, y: bmm(x, y, 2, 2), kr, bk)
    A = ratios[:, :C]
    M = A[:, :, :C] * strict
    zv = jnp.concatenate([jnp.zeros_like(v), v], axis=1)
    s0_side = bmm(kr, S0, 2, 2, out_path=True)
    rhs = s0_side[:, :C] + bmm(A * n_mask, zv, 2, 1, out_path=True)
    if kept is None:
        inv = lax.stop_gradient(_unit_lower_inverse(M))
        y = _unit_lower_solve(inv, M, rhs)
    else:
        inv = kept["inv"]
        y = _unit_lower_solved(inv, kept["y"], M, rhs)
    z = jnp.concatenate([-y, v], axis=1)
    O = s0_side[:, C:] + bmm(ratios[:, C:] * incl2, z, 2, 1, out_path=True)
    g_end = g_in[:, C - 1:C, :]
    S1 = S0 * g_end + bmm(z, bk * g_end, 1, 1, out_path=True)
    return O, S1, dict(cum=cum, ratios=ratios, y=y, inv=inv)


def _scan_fwd(r, lw, k, v, kk, a, ex=None, tb=256):
    assert SCAN_C == HN and 2 * SCAN_C == PW
    T = r.shape[1]
    tb = min(tb, T)
    n_chunks = tb // SCAN_C
    nb = T // tb
    nx = ex.nb if ex else 0

    def body(*refs):
        r_ref, lw_ref, k_ref, v_ref, kk_ref, a_ref = refs[:6]
        x_in, (o_ref, s0_ref), x_out = refs[6:6 + nx], refs[6 + nx:8 + nx], refs[8 + nx:8 + 2 * nx]
        s_ref, sems = refs[8 + 2 * nx], refs[9 + 2 * nx:]

        plan = ex.schedule(nb) if ex else []

        @pl.when(pl.program_id(0) == 0)
        def _():
            s_ref[...] = jnp.zeros_like(s_ref)
            for at, action in plan[:1]:
                action(x_in, x_out, sems)

        def step(c, carry):
            sl = pl.ds(pl.multiple_of(c * SCAN_C, SCAN_C), SCAN_C)
            S0 = s_ref[...]
            O, S1, keep = _chunk_fn(S0, *[_split_pairs(ref[:, sl, :])
                                          for ref in (r_ref, lw_ref, k_ref, v_ref, kk_ref, a_ref)])
            o_ref[:, sl, :] = _join_pairs(O)
            s0_ref[c, 0] = jnp.concatenate([S0, keep["inv"]], axis=-1)
            s0_ref[c, 1] = jnp.concatenate([keep["cum"], keep["y"]], axis=-1)
            s0_ref[c, 2] = keep["ratios"][:, :SCAN_C]
            s0_ref[c, 3] = keep["ratios"][:, SCAN_C:]
            s_ref[...] = S1
            return carry

        lax.fori_loop(0, n_chunks, step, 0)

        for at, action in plan[1:]:
            pl.when(pl.program_id(0) == at)(functools.partial(action, x_in, x_out, sems))

    hm = pl.BlockSpec((NP, tb, PW), lambda i: (0, i, 0))
    res = pl.pallas_call(
        body, name="rwkv_scan_fwd", grid=(nb,), in_specs=[hm] * 6 + (ex.any_specs if ex else []),
        out_specs=[hm, pl.BlockSpec((n_chunks, N_KEPT, NH, HN, PW), lambda i: (i, 0, 0, 0, 0))]
        + (ex.any_specs if ex else []),
        out_shape=[jax.ShapeDtypeStruct((NP, T, PW), F32),
                   jax.ShapeDtypeStruct((T // SCAN_C, N_KEPT, NH, HN, PW), F32)]
        + (ex.out_shape if ex else []),
        scratch_shapes=[pltpu.VMEM((NH, HN, HN), F32)] + (ex.sem_shapes if ex else []),
        compiler_params=pltpu.CompilerParams(dimension_semantics=("arbitrary",), vmem_limit_bytes=VMEM_LIMIT),
    )(r, lw, k, v, kk, a, *(ex.bufs if ex else []))
    return res[0], res[1], list(res[2:])


def _scan_bwd(r, lw, k, v, kk, a, s0s, do, ex=None, tb=128):
    T = r.shape[1]
    tb = min(tb, T)
    n_chunks = tb // SCAN_C
    nb = T // tb
    nx = ex.nb if ex else 0

    def body(*refs):
        r_ref, lw_ref, k_ref, v_ref, kk_ref, a_ref, s0_ref, do_ref = refs[:8]
        x_in, (dr, dlw, dk, dv, dkk, da), x_out = refs[8:8 + nx], refs[8 + nx:14 + nx], refs[14 + nx:14 + 2 * nx]
        ds_ref, sems = refs[14 + 2 * nx], refs[15 + 2 * nx:]

        plan = ex.schedule(nb) if ex else []

        @pl.when(pl.program_id(0) == 0)
        def _():
            ds_ref[...] = jnp.zeros_like(ds_ref)
            for at, action in plan[:1]:
                action(x_in, x_out, sems)

        def step(j, carry):
            c = n_chunks - 1 - j
            sl = pl.ds(pl.multiple_of(c * SCAN_C, SCAN_C), SCAN_C)
            s0_inv, cum_y = s0_ref[c, 0], s0_ref[c, 1]
            kept = dict(inv=s0_inv[:, :, HN:], cum=cum_y[:, :, :HN], y=cum_y[:, :, HN:],
                        ratios=jnp.concatenate([s0_ref[c, 2], s0_ref[c, 3]], axis=1))
            _, vjp = jax.vjp(lambda *t: _chunk_fn(*t, kept=kept)[:2], s0_inv[:, :, :HN],
                             *[_split_pairs(ref[:, sl, :]) for ref in (r_ref, lw_ref, k_ref, v_ref, kk_ref, a_ref)])
            g = vjp((_split_pairs(do_ref[:, sl, :]), ds_ref[...]))
            ds_ref[...] = g[0]
            for ref, val in zip((dr, dlw, dk, dv, dkk, da), g[1:]):
                ref[:, sl, :] = _join_pairs(val)
            return carry

        lax.fori_loop(0, n_chunks, step, 0)

        for at, action in plan[1:]:
            pl.when(pl.program_id(0) == at)(functools.partial(action, x_in, x_out, sems))

    hm = pl.BlockSpec((NP, tb, PW), lambda i: (0, nb - 1 - i, 0))
    res = pl.pallas_call(
        body, name="rwkv_scan_bwd", grid=(nb,),
        in_specs=[hm] * 6 + [pl.BlockSpec((n_chunks, N_KEPT, NH, HN, PW), lambda i: (nb - 1 - i, 0, 0, 0, 0)), hm]
        + (ex.any_specs if ex else []),
        out_specs=[hm] * 6 + (ex.any_specs if ex else []),
        out_shape=[jax.ShapeDtypeStruct((NP, T, PW), F32)] * 6 + (ex.out_shape if ex else []),
        scratch_shapes=[pltpu.VMEM((NH, HN, HN), F32)] + (ex.sem_shapes if ex else []),
        compiler_params=pltpu.CompilerParams(dimension_semantics=("arbitrary",), vmem_limit_bytes=VMEM_LIMIT),
    )(r, lw, k, v, kk, a, s0s, do, *(ex.bufs if ex else []))
    return list(res[:6]), list(res[6:])


def _shift_down(i, p, prev8):
    first = jnp.where(i > 0, prev8[7:8, :], 0.0)
    row = lax.broadcasted_iota(jnp.int32, p.shape, 0)
    return jnp.where(row == 0, first, pltpu.roll(p, 1, axis=0))


def _mix_bwd(dq, p, sb, tm=256):
    def fn(i, n, dq, next8, p, prev8, sb):
        ps = _shift_down(i, p, prev8)
        d1 = dq * sb[1:2]
        last = jnp.where(i < n - 1, next8[0:1, :] * sb[1:2], 0.0)
        row = lax.broadcasted_iota(jnp.int32, dq.shape, 0)
        up = jnp.where(row == dq.shape[0] - 1, last, pltpu.roll(d1, dq.shape[0] - 1, axis=0))
        return (dq * sb[0:1] + up, jnp.sum(dq * p, axis=0, keepdims=True), jnp.sum(dq * ps, axis=0, keepdims=True))
    w = p.shape[1]
    return _rows_call("shift_mix_bwd", fn, [Rows(dq), Halo(dq, +1), Rows(p), Halo(p, -1)], [sb], [("rows", w, BF16)],
                      accs=[((1, w), F32), ((1, w), F32)], tm=tm, with_pid=True)


def _local_step(x, target, W, late_weights=None, early_grads=None, w_in_grads_ready=None, ffn_grads_ready=None):
    G = {}
    a = _rows_call("norm_mix_fwd", lambda x, g: (_rms(x, g),), [Rows(x)], [W["g_mix"]], [("rows", D, BF16)])[0]
    p_sgu = _matmul("proj_sgu", a, W["w_sgu_t"], "nt")
    def token_shift(p, sb0, sb1):
        row = lax.broadcasted_iota(jnp.int32, p.shape, 0)
        return p, p * sb0 + jnp.where(row == 0, 0.0, pltpu.roll(p, 1, axis=0)) * sb1
    p_rw, q = _matmul("proj_rwkv", a, W["w_rw_t"], "nt", tn=512, whole_rows=True, epilogue=token_shift,
                      extras=[W["sb"][0:1], W["sb"][1:2]], out_dtypes=(F32, F32))
    p_gate = _matmul("proj_gate", a, W["w_gate_t"], "nt")

    sgu_consts = [W["sgu_ln_w"], W["sgu_ln_b"], W["sgu_w"], W["sgu_bt"]]
    s = _rows_call("sgu_fwd", lambda *t: (_sgu_fn(*t),), [Rows(p_sgu)], sgu_consts, [("rows", D, BF16)], tm=SGU_C)[0]

    q_ins = [Rows(q, D, 0), Rows(q, D, 1), Rows(q, D, 2), Rows(q, 128, 24), Rows(q, 128, 25), Rows(q, 256, 13)]
    pre_consts = [W["w_lora"], W["w0"], W["a_lora"], W["a0"], W["g_lora"], W["k_k"], W["k_a"]]
    r_h, lw_h, k_h, v_h, kk_h, a_h, g_gate = _rows_call(
        "rwkv_pre_fwd", _pre_fn, q_ins, pre_consts, [("heads", F32)] * 6 + [("rows", D, F32)], tm=128)
    o_h, s0s, got = _scan_fwd(r_h, lw_h, k_h, v_h, kk_h, a_h, ex=late_weights[0] if late_weights else None)
    if late_weights:
        W = {**W, **late_weights[1](got)}
    y_a = _matmul("proj_a", s, W["w_proj_a"], "nn")
    post_ins = [Heads(o_h), Heads(r_h), Heads(k_h), Heads(v_h), Rows(g_gate)]
    post_consts = [W[n].reshape(NP, 1, PW) for n in ("ln_x_w", "ln_x_b", "r_k")]
    z_b = _rows_call("rwkv_post_fwd", lambda *t: (_post_fn(*t),), post_ins, post_consts, [("rows", D, BF16)], tm=128)[0]
    y_b, mixed = _matmul("proj_b", z_b, W["w_proj_b"], "nn", extras=[(p_gate, 0), (p_gate, 1), y_a],
                         epilogue=lambda yb, ga, gb, ya: (yb, _sigmoid(ga) * ya + _sigmoid(gb) * yb),
                         out_dtypes=(F32, BF16))

    def res1(mo, x, g):
        h1 = x + mo
        return h1, _rms(h1, g)
    h1, f = _matmul("proj_out", mixed, W["w_out"], "nn", extras=[x, W["g_ffn"]], epilogue=res1,
                    out_dtypes=(F32, BF16))

    def relu_sq(u):
        r = jnp.maximum(u, 0.0)
        return r, r * r
    r1, act = _matmul("ffn_up", f, W["w_ffn1"], "nn", epilogue=relu_sq, out_dtypes=(BF16, BF16))
    ff = _matmul("ffn_down", act, W["w_ffn2"], "nn")

    def head(h1, ff, tgt, g):
        def f_(h1, ff, g):
            y = _rms(h1 + ff, g)
            return 0.5 * jnp.sum(jnp.mean(jnp.square(y - tgt), axis=-1))
        loss, (dh2, _, dg) = jax.value_and_grad(f_, argnums=(0, 1, 2))(h1, ff, g)
        return dh2, jnp.full((8, LANES), loss, F32), dg
    dh2, loss_acc, G["g_final"] = _rows_call("loss_head", head, [Rows(h1), Rows(ff), Rows(target)], [W["g_final"]],
                                             [("rows", D, F32)], accs=[((8, LANES), F32), ((1, D), F32)])

    d_u1 = _matmul("ffn_down_dx", dh2, W["w_ffn2"], "nt", extras=[r1], out_dtypes=(BF16,),
                   epilogue=lambda d_act, r: (d_act * 2.0 * r.astype(F32),))[0]
    G["w_ffn2"] = _matmul("ffn_down_dw", act, dh2, "tn", out_dtype=GRAD_PAYLOAD)
    d_f = _matmul("ffn_up_dx", d_u1, W["w_ffn1"], "nt")
    G["w_ffn1"] = _matmul("ffn_up_dw", f, d_u1, "tn", out_blocks=N_DEV, out_dtype=GRAD_PAYLOAD)

    def res1_bwd(h1, d_f, dh2, g):
        _, vjp = jax.vjp(_rms, h1, g)
        dh, dg = vjp(d_f)
        return dh2 + dh, dg
    dh1, G["g_ffn"] = _rows_call("residual_norm_bwd", res1_bwd, [Rows(h1), Rows(d_f), Rows(dh2)],
                                 [W["g_ffn"]], [("rows", D, F32)], accs=[((1, D), F32)])
    ffn_token = ffn_grads_ready(G) if ffn_grads_ready else None
    def gate_bwd(d_mixed, ga, gb, ya, yb):
        _, vjp = jax.vjp(_gate_fn, jnp.concatenate([ga, gb], axis=-1), ya, yb)
        return vjp(d_mixed)
    d_gate, d_ya, d_yb = _matmul("proj_out_dx", dh1, W["w_out"], "nt", after=ffn_token, epilogue=gate_bwd,
                                 extras=[(p_gate, 0), (p_gate, 1), y_a, y_b], out_dtypes=(BF16, BF16, BF16),
                                 out_widths=(2 * D, D, D))
    G["w_out"] = _matmul("proj_out_dw", mixed, dh1, "tn", out_dtype=GRAD_PAYLOAD)

    d_s = _matmul("proj_a_dx", d_ya, W["w_proj_a"], "nt")
    G["w_proj_a"] = _matmul("proj_a_dw", s, d_ya, "tn", out_dtype=GRAD_PAYLOAD)

    def sgu_bwd(p, ds, *c):
        _, vjp = jax.vjp(_sgu_fn, p, *c)
        return vjp(ds)
    d_p_sgu, G["sgu_ln_w"], G["sgu_ln_b"], G["sgu_w"], G["sgu_bt"] = _rows_call(
        "sgu_bwd", sgu_bwd, [Rows(p_sgu), Rows(d_s)], sgu_consts, [("rows", 2 * D, BF16)],
        accs=[((1, D), F32), ((1, D), F32), ((SGU_G, SGU_C, SGU_C), F32), ((SGU_C, SGU_G), F32)], tm=SGU_C)

    d_zb = _matmul("proj_b_dx", d_yb, W["w_proj_b"], "nt")
    G["w_proj_b"] = _matmul("proj_b_dw", z_b, d_yb, "tn", out_dtype=GRAD_PAYLOAD)

    def post_bwd(o, r, k2, v, g, dz, *c):
        _, vjp = jax.vjp(_post_fn, o, r, k2, v, g, *c)
        return vjp(dz)
    do_h, dr1, dk1, dv1, d_g, g_lnw, g_lnb, g_rk = _rows_call(
        "rwkv_post_bwd", post_bwd, post_ins + [Rows(d_zb)], post_consts, [("heads", F32)] * 4 + [("rows", D, F32)],
        accs=[((NP, 1, PW), F32)] * 3, tm=128)
    G["ln_x_w"], G["ln_x_b"], G["r_k"] = (t.reshape(1, D) for t in (g_lnw, g_lnb, g_rk))
    (dr2, dlw, dk2, dv2, dkk, daa), early = _scan_bwd(r_h, lw_h, k_h, v_h, kk_h, a_h, s0s, do_h,
                                                      ex=early_grads(G) if early_grads else None)

    def pre_bwd(qr, qk, qv, qxw, qxa, qxg, dr1, dr2, dlw, dk1, dk2, dv1, dv2, dkk, daa, dg, *c):
        _, vjp = jax.vjp(_pre_fn, qr, qk, qv, qxw, qxa, qxg, *c)
        g = vjp((dr1 + dr2, dlw, dk1 + dk2, dv1 + dv2, dkk, daa, dg))
        dq = jnp.concatenate(g[:6], axis=-1)
        return (dq,) + tuple(g[6:])
    pre_b_ins = q_ins + [Heads(dr1), Heads(dr2), Heads(dlw), Heads(dk1), Heads(dk2), Heads(dv1), Heads(dv2),
                         Heads(dkk), Heads(daa), Rows(d_g)]
    d_q, G["w_lora"], G["w0"], G["a_lora"], G["a0"], G["g_lora"], G["k_k"], G["k_a"] = _rows_call(
        "rwkv_pre_bwd", pre_bwd, pre_b_ins, pre_consts, [("rows", RW_INT, F32)],
        accs=[((128, D), F32), ((1, D), F32), ((128, D), F32), ((1, D), F32), ((256, D), F32), ((1, D), F32),
              ((1, D), F32)], tm=128)
    d_p_rw, dsb0, dsb1 = _mix_bwd(d_q, p_rw, W["sb"])
    G["sb"] = jnp.concatenate([dsb0, dsb1], axis=0)

    G["w_sgu_t"] = _matmul("proj_sgu_dw", d_p_sgu, a, "tn", out_dtype=GRAD_PAYLOAD)
    G["w_rw_t"] = _matmul("proj_rwkv_dw", d_p_rw, a, "tn", out_dtype=GRAD_PAYLOAD)
    G["w_gate_t"] = _matmul("proj_gate_dw", d_gate, a, "tn", out_dtype=GRAD_PAYLOAD)
    token = w_in_grads_ready(G) if w_in_grads_ready else None
    da1 = _matmul("proj_sgu_dx", d_p_sgu, W["w_sgu_t"], "nn", after=token)
    da2 = _matmul("proj_rwkv_dx", d_p_rw, W["w_rw_t"], "nn", after=token)
    da3 = _matmul("proj_gate_dx", d_gate, W["w_gate_t"], "nn", after=token)

    def norm1_bwd(x, da1, da2, da3, dh1, g):
        _, vjp = jax.vjp(_rms, x, g)
        dx, dg = vjp(da1 + da2 + da3)
        return dh1 + dx, dg
    dx, G["g_mix"] = _rows_call("norm_mix_bwd", norm1_bwd, [Rows(x), Rows(da1), Rows(da2), Rows(da3), Rows(dh1)],
                                [W["g_mix"]], [("rows", D, F32)], accs=[((1, D), F32)])
    return loss_acc[0, 0], dx, G, early


class Exchange:
    def __init__(self, bufs, gathers):
        self.bufs, self.gathers, self.nb = list(bufs), list(gathers), len(bufs)
        self.any_specs = [pl.BlockSpec(memory_space=pl.ANY)] * self.nb
        self.out_shape = [jax.ShapeDtypeStruct((N_DEV,) + (b.shape if g else b.shape[1:]), b.dtype)
                          for b, g in zip(self.bufs, self.gathers)]
        n = (N_DEV - 1) * self.nb
        self.sem_shapes = [pltpu.SemaphoreType.DMA((n,)), pltpu.SemaphoreType.DMA((n,)),
                           pltpu.SemaphoreType.DMA((self.nb,))]

    def _copies(self, in_refs, out_refs, sems):
        send_sems, recv_sems, local_sems = sems
        x, y, c = lax.axis_index("x"), lax.axis_index("y"), lax.axis_index("c")
        me = 4 * x + 2 * y + c

        def src(b, dest):
            return in_refs[b] if self.gathers[b] else in_refs[b].at[dest]

        local = [pltpu.make_async_copy(src(b, me), out_refs[b].at[me], local_sems.at[b]) for b in range(self.nb)]
        sends, recvs = [], []
        for kbits in range(1, N_DEV):
            px = 1 - x if kbits & 4 else x
            py = 1 - y if kbits & 2 else y
            pc = 1 - c if kbits & 1 else c
            peer = 4 * px + 2 * py + pc
            for b in range(self.nb):
                s = (kbits - 1) * self.nb + b
                sends.append(pltpu.make_async_remote_copy(
                    src_ref=src(b, peer), dst_ref=out_refs[b].at[me], send_sem=send_sems.at[s],
                    recv_sem=recv_sems.at[s], device_id=(px, py, pc), device_id_type=pl.DeviceIdType.MESH))
                recvs.append(pltpu.make_async_remote_copy(
                    src_ref=src(b, peer), dst_ref=out_refs[b].at[peer], send_sem=send_sems.at[s],
                    recv_sem=recv_sems.at[s], device_id=(px, py, pc), device_id_type=pl.DeviceIdType.MESH))
        return local, sends, recvs

    def start(self, in_refs, out_refs, sems):
        local, sends, _ = self._copies(in_refs, out_refs, sems)
        for cp in sends + local:
            cp.start()

    def wait(self, in_refs, out_refs, sems):
        local, sends, recvs = self._copies(in_refs, out_refs, sems)
        for cp in recvs:
            cp.wait_recv()
        for cp in sends:
            cp.wait_send()
        for cp in local:
            cp.wait()

    def schedule(self, n_steps):
        return [(0, self.start), (n_steps - 1, self.wait)]


def _exchange(name, bufs, gather):
    ex = Exchange(bufs, gather if isinstance(gather, (list, tuple)) else [gather] * len(bufs))

    def body(*refs):
        in_refs, out_refs, sems = refs[:ex.nb], refs[ex.nb:2 * ex.nb], refs[2 * ex.nb:]
        ex.start(in_refs, out_refs, sems)
        ex.wait(in_refs, out_refs, sems)

    return pl.pallas_call(body, name=name, in_specs=ex.any_specs, out_specs=ex.any_specs, out_shape=ex.out_shape,
                          scratch_shapes=ex.sem_shapes)(*ex.bufs)


N_CHIP = 4


def _pair_exchange(name, blocks):
    def body(b_ref, got_ref, send_sems, recv_sems):
        x, y, c = lax.axis_index("x"), lax.axis_index("y"), lax.axis_index("c")
        copies = [pltpu.make_async_remote_copy(
            src_ref=b_ref.at[2 * q + 1 - c], dst_ref=got_ref.at[q], send_sem=send_sems.at[q],
            recv_sem=recv_sems.at[q], device_id=(x, y, 1 - c), device_id_type=pl.DeviceIdType.MESH)
            for q in range(N_CHIP)]
        for cp in copies:
            cp.start()
        for cp in copies:
            cp.wait_recv()
        for cp in copies:
            cp.wait_send()

    any_spec = pl.BlockSpec(memory_space=pl.ANY)
    return pl.pallas_call(
        body, name=name, in_specs=[any_spec], out_specs=any_spec,
        out_shape=jax.ShapeDtypeStruct((N_CHIP,) + blocks.shape[1:], blocks.dtype),
        scratch_shapes=[pltpu.SemaphoreType.DMA((N_CHIP,))] * 2,
    )(blocks)


def _pair_sum(name, blocks, got, core):
    _, R, Wd = blocks.shape

    def body(c_ref, a_ref, b_ref, o_ref):
        o_ref[...] = (a_ref[...].astype(F32) + b_ref[...].astype(F32)).astype(o_ref.dtype)

    return pl.pallas_call(
        body, name=name,
        grid_spec=pltpu.PrefetchScalarGridSpec(
            num_scalar_prefetch=1, grid=(N_CHIP,),
            in_specs=[pl.BlockSpec((None, R, Wd), lambda q, c_ref: (2 * q + c_ref[0], 0, 0)),
                      pl.BlockSpec((None, R, Wd), lambda q, c_ref: (q, 0, 0))],
            out_specs=pl.BlockSpec((None, R, Wd), lambda q, c_ref: (q, 0, 0))),
        out_shape=jax.ShapeDtypeStruct(got.shape, blocks.dtype),
        compiler_params=pltpu.CompilerParams(dimension_semantics=("parallel",), vmem_limit_bytes=VMEM_LIMIT),
    )(core, blocks, got)


def _chip_copies(s_ref, land_ref, send_sems, recv_sems):
    x, y, c = lax.axis_index("x"), lax.axis_index("y"), lax.axis_index("c")
    my_q = 2 * x + y
    sends, recvs = [], []
    for kbits in range(1, N_CHIP):
        px = 1 - x if kbits & 2 else x
        py = 1 - y if kbits & 1 else y
        peer_q = 2 * px + py
        sends.append(pltpu.make_async_remote_copy(
            src_ref=s_ref.at[peer_q], dst_ref=land_ref.at[my_q], send_sem=send_sems[kbits - 1],
            recv_sem=recv_sems[kbits - 1], device_id=(px, py, c), device_id_type=pl.DeviceIdType.MESH))
        recvs.append(pltpu.make_async_remote_copy(
            src_ref=s_ref.at[peer_q], dst_ref=land_ref.at[peer_q], send_sem=send_sems[kbits - 1],
            recv_sem=recv_sems[kbits - 1], device_id=(px, py, c), device_id_type=pl.DeviceIdType.MESH))
    return sends, recvs


_HBM = pl.BlockSpec(memory_space=pltpu.HBM)
_SEM = pl.BlockSpec(memory_space=pltpu.SEMAPHORE)
N_CHIP_SEMS = 2 * (N_CHIP - 1)


def _scatter_copies(b_refs, land_refs, send_sems, recv_sems):
    x, y, c = lax.axis_index("x"), lax.axis_index("y"), lax.axis_index("c")
    me = 4 * x + 2 * y + c
    sends, recvs = [], []
    for kbits in range(1, N_DEV):
        px = 1 - x if kbits & 4 else x
        py = 1 - y if kbits & 2 else y
        pc = 1 - c if kbits & 1 else c
        peer = 4 * px + 2 * py + pc
        for b in range(len(b_refs)):
            s = (kbits - 1) * len(b_refs) + b
            sends.append(pltpu.make_async_remote_copy(
                src_ref=b_refs[b].at[peer], dst_ref=land_refs[b].at[me], send_sem=send_sems[s],
                recv_sem=recv_sems[s], device_id=(px, py, pc), device_id_type=pl.DeviceIdType.MESH))
            recvs.append(pltpu.make_async_remote_copy(
                src_ref=b_refs[b].at[peer], dst_ref=land_refs[b].at[peer], send_sem=send_sems[s],
                recv_sem=recv_sems[s], device_id=(px, py, pc), device_id_type=pl.DeviceIdType.MESH))
    return sends, recvs


def _scatter_start(name, bufs):
    nb = len(bufs)
    n = (N_DEV - 1) * nb

    def body(*refs):
        b_refs, land_refs, outs = refs[:nb], refs[nb:2 * nb], refs[2 * nb:]
        sends, _ = _scatter_copies(b_refs, land_refs, outs[:n], outs[n:2 * n])
        for cp in sends:
            cp.start()
        token = outs[2 * n + 2 * nb]
        token[...] = jnp.zeros_like(token)

    thru = tuple(pltpu.HBM(b.shape, b.dtype) for b in bufs)
    res = pl.pallas_call(
        body, name=name, in_specs=(_HBM,) * (2 * nb),
        out_specs=(_SEM,) * (2 * n) + (_HBM,) * (2 * nb) + (pl.BlockSpec(memory_space=pltpu.VMEM),),
        out_shape=(pltpu.SemaphoreType.DMA(()),) * (2 * n) + thru + thru + (jax.ShapeDtypeStruct((8, LANES), F32),),
        input_output_aliases={i: 2 * n + i for i in range(2 * nb)},
        compiler_params=pltpu.CompilerParams(has_side_effects=pltpu.SideEffectType.DATAFLOW_SIDE_EFFECTING),
    )(*[pltpu.with_memory_space_constraint(b, pltpu.HBM) for b in bufs],
      *[pltpu.with_memory_space_constraint(lax.empty(b.shape, b.dtype), pltpu.HBM) for b in bufs])
    return res[:2 * n], list(res[2 * n:2 * n + nb]), list(res[2 * n + nb:2 * n + 2 * nb]), res[2 * n + 2 * nb]


def _scatter_wait(name, sems, bufs_thru, lands_thru, after):
    nb = len(bufs_thru)
    n = (N_DEV - 1) * nb

    def body(*refs):
        b_refs, land_refs, sem_refs = refs[:nb], refs[nb:2 * nb], refs[2 * nb:2 * nb + 2 * n]
        sends, recvs = _scatter_copies(b_refs, land_refs, sem_refs[:n], sem_refs[n:])
        for cp in sends:
            cp.wait_send()
        for cp in recvs:
            cp.wait_recv()

    thru = tuple(pltpu.HBM(b.shape, b.dtype) for b in bufs_thru)
    res = pl.pallas_call(
        body, name=name, in_specs=(_HBM,) * (2 * nb) + (_SEM,) * (2 * n) + (pl.BlockSpec(memory_space=pl.ANY),),
        out_specs=(_HBM,) * (2 * nb), out_shape=thru + thru,
        input_output_aliases={i: i for i in range(2 * nb)},
        compiler_params=pltpu.CompilerParams(has_side_effects=pltpu.SideEffectType.DATAFLOW_SIDE_EFFECTING),
    )(*bufs_thru, *lands_thru, *sems, after)
    return list(res[:nb]), list(res[nb:])


def _chip_exchange_start(name, sums):
    def body(s_ref, land_ref, *outs):
        sems, token = outs[:N_CHIP_SEMS], outs[N_CHIP_SEMS + 2]
        sends, _ = _chip_copies(s_ref, land_ref, sems[:N_CHIP - 1], sems[N_CHIP - 1:])
        for cp in sends:
            cp.start()
        token[...] = jnp.zeros_like(token)

    res = pl.pallas_call(
        body, name=name, in_specs=(_HBM, _HBM),
        out_specs=(_SEM,) * N_CHIP_SEMS + (_HBM, _HBM, pl.BlockSpec(memory_space=pltpu.VMEM)),
        out_shape=(pltpu.SemaphoreType.DMA(()),) * N_CHIP_SEMS
        + (pltpu.HBM(sums.shape, sums.dtype), pltpu.HBM(sums.shape, sums.dtype), jax.ShapeDtypeStruct((8, LANES), F32)),
        input_output_aliases={0: N_CHIP_SEMS, 1: N_CHIP_SEMS + 1},
        compiler_params=pltpu.CompilerParams(has_side_effects=pltpu.SideEffectType.DATAFLOW_SIDE_EFFECTING),
    )(pltpu.with_memory_space_constraint(sums, pltpu.HBM),
      pltpu.with_memory_space_constraint(lax.empty(sums.shape, sums.dtype), pltpu.HBM))
    return res[:N_CHIP_SEMS], res[N_CHIP_SEMS], res[N_CHIP_SEMS + 1], res[N_CHIP_SEMS + 2]


def _chip_exchange_wait(name, sems, sums_thru, land_thru, after):
    def body(s_ref, land_ref, *rest):
        sems = rest[:N_CHIP_SEMS]
        sends, recvs = _chip_copies(s_ref, land_ref, sems[:N_CHIP - 1], sems[N_CHIP - 1:])
        for cp in sends:
            cp.wait_send()
        for cp in recvs:
            cp.wait_recv()

    return pl.pallas_call(
        body, name=name, in_specs=(_HBM, _HBM) + (_SEM,) * N_CHIP_SEMS + (pl.BlockSpec(memory_space=pl.ANY),),
        out_specs=(_HBM, _HBM),
        out_shape=(pltpu.HBM(sums_thru.shape, sums_thru.dtype), pltpu.HBM(sums_thru.shape, sums_thru.dtype)),
        input_output_aliases={0: 0, 1: 1},
        compiler_params=pltpu.CompilerParams(has_side_effects=pltpu.SideEffectType.DATAFLOW_SIDE_EFFECTING),
    )(sums_thru, land_thru, *sems, after)


def _adamw(name, slots, w, m, v, tr=256):
    unit_mid = w.ndim == 3 and w.shape[1] == 1 and w.shape[0] > 1
    R, Wd = (w.shape[0], w.shape[2]) if unit_mid else w.shape[-2:]
    depth_axis = w.ndim == 3 and not unit_mid
    if R % tr == 0:
        tc = Wd
    else:
        tr, tc = R, (256 if (Wd % 256 == 0 and R > 256) else Wd)
    at = (slice(None), 0, slice(None)) if unit_mid else Ellipsis

    def body(s_ref, w_ref, m_ref, v_ref, g_out, d_out, m_out, v_out):
        g = s_ref[0].astype(F32)
        for j in range(1, slots.shape[0]):
            g = g + s_ref[j].astype(F32)
        m_new = ADAM_B1 * m_ref[at] + (1.0 - ADAM_B1) * g
        v_new = ADAM_B2 * v_ref[at] + (1.0 - ADAM_B2) * jnp.square(g)
        m_hat = m_new / (1.0 - ADAM_B1 ** ADAM_STEP)
        v_hat = v_new / (1.0 - ADAM_B2 ** ADAM_STEP)
        g_out[at] = g
        d_out[at] = -ADAM_LR * (m_hat / (jnp.sqrt(v_hat) + ADAM_EPS) + ADAM_WD * w_ref[at])
        m_out[at] = m_new
        v_out[at] = v_new

    if unit_mid:
        row = pl.BlockSpec((tr, 1, tc), lambda i, j: (i, 0, j))
    elif depth_axis:
        row = pl.BlockSpec((None, tr, tc), lambda i, j: (0, i, j))
    else:
        row = pl.BlockSpec((tr, tc), lambda i, j: (i, j))
    return pl.pallas_call(
        body, name=name, grid=(R // tr, Wd // tc),
        in_specs=[pl.BlockSpec((slots.shape[0], tr, tc), lambda i, j: (0, i, j)), row, row, row],
        out_specs=[row] * 4, out_shape=[jax.ShapeDtypeStruct(w.shape, F32)] * 4,
        compiler_params=pltpu.CompilerParams(dimension_semantics=("parallel", "parallel"),
                                             vmem_limit_bytes=VMEM_LIMIT),
    )(slots, w, m, v)


PACK_W = 1024
PACKED = [(n, s) for n, s in REPLICATED if n != "sgu_w"]
_SMALL_SIZES = [int(np.prod(s)) for _, s in PACKED]
_SMALL_ROWS = _round_up(_round_up(sum(_SMALL_SIZES) + PACK_W, PACK_W) // PACK_W, 8)
_LOSS_AT = sum(_SMALL_SIZES)
W_IN_SHARD = P_TOTAL // N_DEV


def _pack_rows(parts, rows, dtype):
    flat = jnp.concatenate([p.reshape(-1).astype(dtype) for p in parts])
    return jnp.pad(flat, (0, rows * PACK_W - flat.shape[0])).reshape(rows, PACK_W)


def _w_in_groups_t(blocks):
    wt = blocks.reshape(P_TOTAL, D)
    o, c = 2 * D, 2 * D + 3 * D
    z = lambda r: jnp.zeros((r, D), wt.dtype)
    rw = jnp.concatenate([wt[o:c], wt[c:c + L_W], z(128 - L_W), wt[c + L_W:c + L_W + L_A], z(128 - L_A),
                          wt[c + L_W + L_A:o + C_B], z(256 - L_G)], axis=0)
    return wt[:o], rw, wt[o + C_B:]


def _w_in_grad_blocks(g_sgu_t, g_rw_t, g_gate_t):
    c = 3 * D
    full = jnp.concatenate([g_sgu_t, g_rw_t[:c], g_rw_t[c:c + L_W], g_rw_t[c + 128:c + 128 + L_A],
                            g_rw_t[c + 256:c + 256 + L_G], g_gate_t], axis=0)
    return full.reshape(N_DEV, W_IN_SHARD, D)


def _mesh_index():
    me = 4 * lax.axis_index("x") + 2 * lax.axis_index("y") + lax.axis_index("c")
    return me.astype(jnp.int32).reshape(1)


def _fill_slot(name, dst, src, idx, src_idx=None):
    R, Wd = dst.shape[1:]
    scalars = [idx] if src_idx is None else [idx, src_idx]
    if src_idx is None:
        src_spec = pl.BlockSpec((R, Wd), lambda i, *s: (0, 0))
    else:
        src_spec = pl.BlockSpec((None, R, Wd), lambda i, *s: (s[1][0], 0, 0))

    def body(*refs):
        src_ref, out_ref = refs[len(scalars) + 1], refs[len(scalars) + 2]
        out_ref[...] = src_ref[...]

    return pl.pallas_call(
        body, name=name,
        grid_spec=pltpu.PrefetchScalarGridSpec(
            num_scalar_prefetch=len(scalars), grid=(1,),
            in_specs=[pl.BlockSpec(memory_space=pl.ANY), src_spec],
            out_specs=pl.BlockSpec((None, R, Wd), lambda i, *s: (s[0][0], 0, 0))),
        out_shape=jax.ShapeDtypeStruct(dst.shape, dst.dtype),
        input_output_aliases={len(scalars): 0},
        compiler_params=pltpu.CompilerParams(vmem_limit_bytes=VMEM_LIMIT),
    )(*scalars, dst, src)


class TwoLevelGather:
    def __init__(self, bufs, skip_own=()):
        self.bufs, self.nb, self.skip_own = list(bufs), len(bufs), tuple(skip_own)
        self.any_specs = [pl.BlockSpec(memory_space=pl.ANY)] * self.nb
        self.out_shape = [jax.ShapeDtypeStruct((N_DEV,) + b.shape, b.dtype) for b in self.bufs]
        self.sem_shapes = [pltpu.SemaphoreType.DMA((7 * self.nb,)), pltpu.SemaphoreType.DMA((7 * self.nb,)),
                           pltpu.SemaphoreType.DMA((self.nb,))]

    def _copies(self, in_refs, out_refs, sems):
        send_sems, recv_sems, local_sems = sems
        nb = self.nb
        x, y, c = lax.axis_index("x"), lax.axis_index("y"), lax.axis_index("c")
        me, sibling = (x, y, c), (x, y, 1 - c)
        chips = [(1 - x, y), (x, 1 - y), (1 - x, 1 - y)]

        def slot(b, dev):
            return out_refs[b].at[4 * dev[0] + 2 * dev[1] + dev[2]]

        def copy(b, k, block, to, own=False):
            return pltpu.make_async_remote_copy(
                src_ref=in_refs[b] if own else slot(b, block), dst_ref=slot(b, block),
                send_sem=send_sems.at[7 * b + k], recv_sem=recv_sems.at[7 * b + k], device_id=to,
                device_id_type=pl.DeviceIdType.MESH)

        cp = {}
        cp["local"] = [pltpu.make_async_copy(in_refs[b], slot(b, me), local_sems.at[b]) for b in range(nb)
                       if b not in self.skip_own]
        cp["first"] = [copy(b, 0, me, sibling, own=True) for b in range(nb)]
        cp["first"] += [copy(b, 1 + j, me, (*chip, c), own=True) for j, chip in enumerate(chips) for b in range(nb)]
        cp["over_ici"] = [copy(b, 1 + j, (*chip, c), me) for j, chip in enumerate(chips) for b in range(nb)]
        cp["passed"] = [copy(b, 4 + j, (*chip, c), sibling) for j, chip in enumerate(chips) for b in range(nb)]
        cp["from_sibling"] = [copy(b, 0, sibling, me) for b in range(nb)]
        cp["from_sibling"] += [copy(b, 4 + j, (*chip, 1 - c), me) for j, chip in enumerate(chips) for b in range(nb)]
        return cp

    def start(self, in_refs, out_refs, sems):
        cp = self._copies(in_refs, out_refs, sems)
        for c in cp["first"] + cp["local"]:
            c.start()

    def forward(self, in_refs, out_refs, sems):
        cp = self._copies(in_refs, out_refs, sems)
        for arrived, onward in zip(cp["over_ici"], cp["passed"]):
            arrived.wait_recv()
            onward.start()

    def finish(self, in_refs, out_refs, sems):
        cp = self._copies(in_refs, out_refs, sems)
        for c in cp["from_sibling"]:
            c.wait_recv()
        for c in cp["first"] + cp["passed"]:
            c.wait_send()
        for c in cp["local"]:
            c.wait()

    def schedule(self, n_steps):
        return [(0, self.start), (max(n_steps - 3, 0), self.forward), (n_steps - 1, self.finish)]


def _all_gather_two_level(name, bufs, skip_own=()):
    ex = TwoLevelGather(bufs, skip_own)

    def body(*refs):
        args = refs[:ex.nb], refs[ex.nb:2 * ex.nb], refs[2 * ex.nb:]
        ex.start(*args)
        ex.forward(*args)
        ex.finish(*args)

    return pl.pallas_call(body, name=name, in_specs=ex.any_specs, out_specs=ex.any_specs, out_shape=ex.out_shape,
                          scratch_shapes=ex.sem_shapes)(*ex.bufs)


def _cols_from_blocks(blk):
    return jnp.transpose(blk, (1, 0, 2)).reshape(blk.shape[1], -1)


def _cols_to_blocks(g):
    r, c = g.shape
    return jnp.transpose(g.reshape(r, N_DEV, c // N_DEV), (1, 0, 2))


FIRST_WEIGHTS = ["w_in", "shift_b", "w_lora_w", "a_lora_w", "g_lora_w"]
LATE_WEIGHTS = ["w_proj_a", "w_proj_b", "w_out", "w_ffn1", "w_ffn2"]
SCAN_CARRIED = ["w_proj_a", "w_proj_b", "w_out"]
FFN_WEIGHTS = ["w_ffn1", "w_ffn2"]


def _late_weights(shards):
    ex = TwoLevelGather([shards[n].astype(BF16) for n in LATE_WEIGHTS])

    def finish(results):
        got = dict(zip(LATE_WEIGHTS, results))
        W = {n: got[n].reshape(-1, D) for n in ("w_proj_a", "w_proj_b", "w_out", "w_ffn2")}
        W["w_ffn1"] = got["w_ffn1"].reshape(N_DEV, D, -1)
        return W
    return ex, finish


def _gather_weights(shards):
    def payload(n):
        if n == "w_in":
            return jnp.transpose(shards[n][0]).astype(BF16)
        return shards[n] if n == "shift_b" else shards[n].astype(BF16)
    payloads = [payload(n) for n in FIRST_WEIGHTS]
    got = dict(zip(FIRST_WEIGHTS, _all_gather_two_level("weight_all_gather", payloads, skip_own=(0,))))
    got["w_in"] = _fill_slot("w_in_own_slot", got["w_in"], payloads[0], _mesh_index())
    W = {}
    W["w_sgu_t"], W["w_rw_t"], W["w_gate_t"] = _w_in_groups_t(got["w_in"])
    z = lambda r, c, dt: jnp.zeros((r, c), dt)
    W["w_lora"] = jnp.concatenate([_cols_from_blocks(got["w_lora_w"][:, 0]).astype(F32), z(128 - L_W, D, F32)], axis=0)
    W["a_lora"] = jnp.concatenate([_cols_from_blocks(got["a_lora_w"][:, 0]).astype(F32), z(128 - L_A, D, F32)], axis=0)
    W["g_lora"] = jnp.concatenate([_cols_from_blocks(got["g_lora_w"][:, 0]).astype(F32), z(256 - L_G, D, F32)], axis=0)
    sb = _cols_from_blocks(got["shift_b"][:, 0])
    W["sb"] = jnp.concatenate([sb[:, :3 * D], sb[:, 3 * D:3 * D + L_W], z(2, 128 - L_W, F32),
                               sb[:, 3 * D + L_W:3 * D + L_W + L_A], z(2, 128 - L_A, F32),
                               sb[:, 3 * D + L_W + L_A:], z(2, 256 - L_G, F32)], axis=1)
    return W


def _replicated_weights(rep):
    W = {n: rep[n] for n in ("g_mix", "sgu_ln_w", "sgu_ln_b", "w0", "a0", "k_k", "k_a", "r_k", "ln_x_w", "ln_x_b",
                             "g_ffn")}
    W["g_final"] = rep["g_final"].reshape(1, D)
    W["sgu_w"] = rep["sgu_w"][0]
    W["sgu_bt"] = jnp.transpose(rep["sgu_b"][0])
    return W


def _late_grad_blocks(G):
    return Exchange([G[n].reshape(N_DEV, -1, D) for n in SCAN_CARRIED]
                    + [G["sgu_w"].reshape(SGU_G * SGU_C, SGU_C).astype(GRAD_PAYLOAD)],
                    [False] * len(SCAN_CARRIED) + [True])


def _first_grad_blocks(G):
    sbg = G["sb"]
    c = 3 * D
    sb = jnp.concatenate([sbg[:, :c], sbg[:, c:c + L_W], sbg[:, c + 128:c + 128 + L_A],
                          sbg[:, c + 256:c + 256 + L_G]], axis=1)
    return {
        "shift_b": _cols_to_blocks(sb),
        "w_lora_w": _cols_to_blocks(G["w_lora"][:L_W]), "a_lora_w": _cols_to_blocks(G["a_lora"][:L_A]),
        "g_lora_w": _cols_to_blocks(G["g_lora"][:L_G]),
    }


def _replicated_grads(G):
    small = {n: G[n] for n in ("g_mix", "sgu_ln_w", "sgu_ln_b", "w0", "a0", "k_k", "k_a", "r_k", "ln_x_w", "ln_x_b",
                               "g_ffn", "g_final")}
    small["sgu_w"] = G["sgu_w"]
    small["sgu_b"] = jnp.transpose(G["sgu_bt"])
    return small


def kernel(x, g_mix, w_in, sgu_ln_w, sgu_ln_b, sgu_w, sgu_b, w_proj_a, shift_b, w_lora_w, w0, a_lora_w, a0, g_lora_w, k_k, k_a, r_k, ln_x_w, ln_x_b, w_proj_b, w_out, g_ffn, w_ffn1, w_ffn2, g_final, loss_target, m_g_mix, m_w_in, m_sgu_ln_w, m_sgu_ln_b, m_sgu_w, m_sgu_b, m_w_proj_a, m_shift_b, m_w_lora_w, m_w0, m_a_lora_w, m_a0, m_g_lora_w, m_k_k, m_k_a, m_r_k, m_ln_x_w, m_ln_x_b, m_w_proj_b, m_w_out, m_g_ffn, m_w_ffn1, m_w_ffn2, m_g_final, v_g_mix, v_w_in, v_sgu_ln_w, v_sgu_ln_b, v_sgu_w, v_sgu_b, v_w_proj_a, v_shift_b, v_w_lora_w, v_w0, v_a_lora_w, v_a0, v_g_lora_w, v_k_k, v_k_a, v_r_k, v_ln_x_w, v_ln_x_b, v_w_proj_b, v_w_out, v_g_ffn, v_w_ffn1, v_w_ffn2, v_g_final):
    env = dict(locals())
    weights = {n: env[n] for n in WEIGHT_ORDER}
    moms = {n: env["m_" + n] for n in WEIGHT_ORDER}
    vars_ = {n: env["v_" + n] for n in WEIGHT_ORDER}

    shards = {n: weights[n] for n, _, _ in SHARDED}
    W = _gather_weights(shards)
    W.update(_replicated_weights({n: weights[n] for n, _ in REPLICATED}))
    in_flight = {}

    def send_w_in_grads(G):
        blocks = _w_in_grad_blocks(G["w_sgu_t"], G["w_rw_t"], G["w_gate_t"])
        got = _pair_exchange("grad_pair_exchange", blocks)
        core = lax.axis_index("c").astype(jnp.int32).reshape(1)
        sums = _pair_sum("grad_pair_sum", blocks, got, core)
        in_flight["sems"], in_flight["sums"], in_flight["land"], token = _chip_exchange_start("grad_chip_start", sums)
        return token

    def send_ffn_grads(G):
        in_flight["ffn"] = _scatter_start("grad_ffn_start", [G["w_ffn1"], G["w_ffn2"].reshape(N_DEV, -1, D)])
        return in_flight["ffn"][3]

    loss_part, dx, G, late_slots = _local_step(x[0], loss_target[0], W, late_weights=_late_weights(shards),
                                               early_grads=_late_grad_blocks, w_in_grads_ready=send_w_in_grads,
                                               ffn_grads_ready=send_ffn_grads)

    slots = dict(zip(SCAN_CARRIED, late_slots))
    me = _mesh_index()
    sent, landed = _scatter_wait("grad_ffn_wait", *in_flight["ffn"][:3], after=dx)
    for i, n in enumerate(FFN_WEIGHTS):
        slots[n] = _fill_slot("grad_own_slot_" + n, landed[i], sent[i], me, src_idx=me)
    blocks = _first_grad_blocks(G)
    small = _replicated_grads(G)
    small_parts = [small[n] for n, _ in PACKED] + [jnp.full((PACK_W,), loss_part, F32)]
    rest = [n for n in FIRST_WEIGHTS if n != "w_in"]
    res = _exchange("grad_exchange", [blocks[n] for n in rest] + [_pack_rows(small_parts, _SMALL_ROWS, F32)],
                    [False] * len(rest) + [True])
    slots.update(zip(rest, res[:-1]))
    small_slots = res[-1]
    sums, chip_slots = _chip_exchange_wait("grad_chip_wait", in_flight["sems"], in_flight["sums"], in_flight["land"],
                                           after=small_slots)
    my_chip = (2 * lax.axis_index("x") + lax.axis_index("y")).astype(jnp.int32).reshape(1)
    slots["w_in"] = _fill_slot("grad_own_slot", chip_slots, sums, my_chip, src_idx=my_chip)

    outs = [dict(), dict(), dict(), dict()]
    for n, _, _ in SHARDED:
        if n == "w_in":
            res = _adamw("adamw_" + n, slots[n], *[jnp.transpose(t, (2, 0, 1)) for t in (weights[n], moms[n], vars_[n])])
            res = [jnp.transpose(t, (1, 2, 0)) for t in res]
        else:
            res = _adamw("adamw_" + n, slots[n], weights[n], moms[n], vars_[n])
        for k in range(4):
            outs[k][n] = res[k]

    sgu_shape = (SGU_G * SGU_C, SGU_C)
    res = _adamw("adamw_sgu_w", late_slots[len(SCAN_CARRIED)], *[t.reshape(sgu_shape) for t in
                                                                  (weights["sgu_w"], moms["sgu_w"], vars_["sgu_w"])])
    for k in range(4):
        outs[k]["sgu_w"] = res[k].reshape(weights["sgu_w"].shape)

    def packed(d):
        return _pack_rows([d[n] for n, _ in PACKED], _SMALL_ROWS, F32)
    small_out = _adamw("adamw_replicated", small_slots, packed(weights), packed(moms), packed(vars_))
    for k in range(4):
        flat = small_out[k].reshape(-1)
        off = 0
        for (n, s), size in zip(PACKED, _SMALL_SIZES):
            outs[k][n] = flat[off:off + size].reshape(s)
            off += size
    loss = small_out[0].reshape(-1)[_LOSS_AT]
    return (loss, dx[None], *[outs[0][n] for n in WEIGHT_ORDER], *[outs[1][n] for n in WEIGHT_ORDER],
            *[outs[2][n] for n in WEIGHT_ORDER], *[outs[3][n] for n in WEIGHT_ORDER])
```

```python
import functools
import numpy as np
import jax
import jax.numpy as jnp
from jax import lax
from jax.experimental import pallas as pl
from jax.experimental.pallas import tpu as pltpu

F32 = jnp.float32
BF16 = jnp.bfloat16

D = 1024
NH, HN = 16, 64
NP, PW = NH // 2, 2 * HN
SGU_G, SGU_C = 8, 128
L_W, L_A, L_G = 64, 64, 160
C_B = 3 * D + L_W + L_A + L_G
P_TOTAL = 2 * D + C_B + 2 * D
D_FF = 4 * D
RW_INT = 3 * D + 128 + 128 + 256
NORM_EPS, LN_EPS, GN_EPS = 1e-6, 1e-5, 64e-5
N_DEV = 8
LANES = 128
SCAN_C = 64
N_KEPT = 4
SOLVE_B = 16
SCAN_PRECISION = lax.Precision.HIGH
SCAN_OUT_PRECISION = lax.Precision.DEFAULT
GRAD_PAYLOAD = BF16
VMEM_LIMIT = 56 * 1024 * 1024
MATMUL_VMEM_BUDGET = 40 * 1024 * 1024
STEP_COST_BYTES = 512 * 1024
HBM_COST_RATIO = 3
ACC_PASS_WEIGHT = 4

ADAM_LR, ADAM_B1, ADAM_B2, ADAM_EPS, ADAM_WD, ADAM_STEP = 0.001, 0.9, 0.999, 1e-08, 0.01, 10

SHARDED = [
    ("w_in", (D, P_TOTAL), 1), ("w_proj_a", (D, D), 0), ("shift_b", (2, C_B), 1), ("w_lora_w", (L_W, D), 1),
    ("a_lora_w", (L_A, D), 1), ("g_lora_w", (L_G, D), 1), ("w_proj_b", (D, D), 0), ("w_out", (D, D), 0),
    ("w_ffn1", (D, D_FF), 1), ("w_ffn2", (D_FF, D), 0),
]
REPLICATED = [
    ("g_mix", (1, D)), ("sgu_ln_w", (1, D)), ("sgu_ln_b", (1, D)), ("sgu_w", (1, SGU_G, SGU_C, SGU_C)),
    ("sgu_b", (1, SGU_G, SGU_C)), ("w0", (1, D)), ("a0", (1, D)), ("k_k", (1, D)), ("k_a", (1, D)), ("r_k", (1, D)),
    ("ln_x_w", (1, D)), ("ln_x_b", (1, D)), ("g_ffn", (1, D)), ("g_final", (D,)),
]
WEIGHT_ORDER = ["g_mix", "w_in", "sgu_ln_w", "sgu_ln_b", "sgu_w", "sgu_b", "w_proj_a", "shift_b", "w_lora_w", "w0",
                "a_lora_w", "a0", "g_lora_w", "k_k", "k_a", "r_k", "ln_x_w", "ln_x_b", "w_proj_b", "w_out", "g_ffn",
                "w_ffn1", "w_ffn2", "g_final"]


def _round_up(n, m):
    return (n + m - 1) // m * m


def _pick(n, target):
    if n <= target:
        return n
    best = None
    for t in range(LANES, target + 1, LANES):
        if n % t == 0:
            best = t
    assert best is not None, (n, target)
    return best


def _matmul(name, a, b, mode, out_dtype=F32, tm=2048, tn=1024, tk=4096, out_blocks=None, epilogue=None, extras=(),
            out_dtypes=(), after=None, whole_rows=False, out_widths=None):
    b_blocks = b.shape[0] if b.ndim == 3 else None
    bshape = b.shape if b.ndim == 2 else (b.shape[1], b.shape[0] * b.shape[2])
    if mode == "nn":
        (M, K), (K2, N) = a.shape, bshape
    elif mode == "nt":
        (M, K), (N, K2) = a.shape, bshape
    else:
        (K, M), (K2, N) = a.shape, bshape
    assert K == K2, (name, a.shape, b.shape)
    assert b_blocks is None or mode != "tn"
    assert out_blocks is None or mode == "tn"
    tn = min(tn, N // (out_blocks or 1), bshape[1] // b_blocks if (b_blocks and mode == "nn") else tn)
    blocked_k = bool(b_blocks) and mode == "nt"
    tm, tn, tk = _pick(M, tm), _pick(N, tn), (K if blocked_k else _pick(K, tk))

    def vmem_bytes(tm, tk):
        tiles = tm * tk * a.dtype.itemsize + tk * tn * b.dtype.itemsize
        for i, dt in enumerate(out_dtypes if epilogue else (out_dtype,)):
            tiles += tm * (out_widths[i] if out_widths else tn) * jnp.dtype(dt).itemsize
        for x in extras:
            arr = x[0] if isinstance(x, tuple) else x
            tiles += (tm if arr.shape[0] > 1 else 1) * tn * arr.dtype.itemsize
        return 2 * tiles + (tm * tn * 4 if K // tk > 1 else 0)

    def cost(tm, tk):
        ni, nj, nk = M // tm, N // tn, K // tk
        steps = ni * nj * nk
        acc_passes = steps * tm * tn * 8 * ACC_PASS_WEIGHT if nk > 1 else 0
        a_reads = M * K * a.dtype.itemsize * (nj if nk > 1 else 1)
        b_reads = K * N * b.dtype.itemsize * (ni if (nj > 1 or nk > 1) else 1)
        return (steps * STEP_COST_BYTES + acc_passes + vmem_bytes(tm, tk) // 2
                + HBM_COST_RATIO * (a_reads + b_reads))

    options = [(m, k) for m in ({M} if whole_rows else {_pick(M, max(t, LANES)) for t in (tm, tm // 2, tm // 4)})
               for k in ({K} if blocked_k else {_pick(K, max(t, LANES)) for t in (tk, tk // 2, tk // 4)})
               if vmem_bytes(m, k) <= MATMUL_VMEM_BUDGET]
    tm, tk = min(options, key=lambda o: cost(*o))
    nk = K // tk
    dims = {"nn": (((1,), (0,)), ((), ())), "nt": (((1,), (1,)), ((), ())), "tn": (((0,), (0,)), ((), ()))}[mode]

    n_x, n_o = len(extras), len(out_dtypes) if epilogue else 1
    n_after = 0 if after is None else 1

    def body(a_ref, b_ref, *rest):
        x_refs, o_refs, acc = rest[:n_x], rest[n_x + n_after:n_x + n_after + n_o], rest[n_x + n_after + n_o:]
        if blocked_k:
            bw = b.shape[2]
            part = sum(lax.dot_general(a_ref[:, blk * bw:(blk + 1) * bw].astype(BF16), b_ref[blk].astype(BF16), dims,
                                       preferred_element_type=F32) for blk in range(b_blocks))
        else:
            part = lax.dot_general(a_ref[...].astype(BF16), b_ref[...].astype(BF16), dims, preferred_element_type=F32)

        def finish(res):
            outs = epilogue(res, *[r[...] for r in x_refs]) if epilogue else (res,)
            for r, v in zip(o_refs, outs):
                r[...] = v.astype(r.dtype)

        if nk == 1:
            finish(part)
            return
        acc_ref, k = acc[0], pl.program_id(2)

        @pl.when(k == 0)
        def _():
            acc_ref[...] = part

        @pl.when(k > 0)
        def _():
            acc_ref[...] += part

        @pl.when(k == nk - 1)
        def _():
            finish(acc_ref[...])

    a_spec = {"nn": pl.BlockSpec((tm, tk), lambda i, j, k: (i, k)), "nt": pl.BlockSpec((tm, tk), lambda i, j, k: (i, k)),
              "tn": pl.BlockSpec((tk, tm), lambda i, j, k: (k, i))}[mode]
    b_spec = {"nn": pl.BlockSpec((tk, tn), lambda i, j, k: (k, j)), "nt": pl.BlockSpec((tn, tk), lambda i, j, k: (j, k)),
              "tn": pl.BlockSpec((tk, tn), lambda i, j, k: (k, j))}[mode]
    if b_blocks and mode == "nn":
        per = b.shape[2] // tn
        b_spec = pl.BlockSpec((None, tk, tn), lambda i, j, k: (j // per, k, j % per))
    elif b_blocks:
        b_spec = pl.BlockSpec((b_blocks, tn, b.shape[2]), lambda i, j, k: (0, j, 0))
    out_spec = pl.BlockSpec((tm, tn), lambda i, j, k: (i, j))
    out_shape = jax.ShapeDtypeStruct((M, N), out_dtype)
    if out_blocks:
        per_o = N // out_blocks // tn
        out_spec = pl.BlockSpec((None, tm, tn), lambda i, j, k: (j // per_o, i, j % per_o))
        out_shape = jax.ShapeDtypeStruct((out_blocks, M, N // out_blocks), out_dtype)
    epi_widths = list(out_widths) if out_widths else [N] * n_o
    assert all(w == N for w in epi_widths) or N == tn
    epi_specs = [pl.BlockSpec((tm, tn if w == N else w), lambda i, j, k: (i, j)) for w in epi_widths]
    x_specs, x_args = [], []
    for x in extras:
        arr, off = x if isinstance(x, tuple) else (x, 0)
        if arr.shape[0] == 1:
            x_specs.append(pl.BlockSpec((1, tn), lambda i, j, k: (0, j)))
        else:
            x_specs.append(pl.BlockSpec((tm, tn), lambda i, j, k, off=off: (i, j + off)))
        x_args.append(arr)
    res = pl.pallas_call(
        body, name=name, grid=(M // tm, N // tn, nk),
        in_specs=[a_spec, b_spec] + x_specs + [pl.BlockSpec(memory_space=pl.ANY)] * n_after,
        out_specs=epi_specs if epilogue else out_spec,
        out_shape=[jax.ShapeDtypeStruct((M, w), dt) for w, dt in zip(epi_widths, out_dtypes)] if epilogue else out_shape,
        scratch_shapes=[pltpu.VMEM((tm, tn), F32)] if nk > 1 else [],
        compiler_params=pltpu.CompilerParams(dimension_semantics=("parallel", "parallel", "arbitrary"),
                                             vmem_limit_bytes=VMEM_LIMIT),
    )(a, b, *x_args, *([after] if n_after else []))
    return res


class Rows:
    def __init__(self, arr, width=None, cb=0):
        self.arr, self.width, self.cb = arr, (arr.shape[1] if width is None else width), cb


class Heads:
    def __init__(self, arr):
        self.arr = arr


class Halo:
    def __init__(self, arr, side):
        self.arr, self.side = arr, side


def _rows_call(name, fn, ins, consts, outs, accs=(), tm=512, with_pid=False):
    T = next(o.arr.shape[1] if isinstance(o, Heads) else o.arr.shape[0] for o in ins if not isinstance(o, Halo))
    tm = min(tm, T)
    n_tiles = T // tm
    n_in, n_c, n_out = len(ins), len(consts), len(outs)
    in_specs, args = [], []
    for o in ins:
        if isinstance(o, Rows):
            in_specs.append(pl.BlockSpec((tm, o.width), lambda i, cb=o.cb: (i, cb)))
        elif isinstance(o, Heads):
            in_specs.append(pl.BlockSpec((NP, tm, PW), lambda i: (0, i, 0)))
        else:
            w = o.arr.shape[1]
            if o.side < 0:
                in_specs.append(pl.BlockSpec((8, w), lambda i: (jnp.maximum(i * (tm // 8) - 1, 0), 0)))
            else:
                in_specs.append(pl.BlockSpec((8, w), lambda i: (jnp.minimum((i + 1) * (tm // 8), T // 8 - 1), 0)))
        args.append(o.arr)
    for c in consts:
        in_specs.append(pl.BlockSpec(c.shape, lambda i, nd=c.ndim: (0,) * nd))
        args.append(c)
    out_specs, out_shape = [], []
    for o in outs:
        if o[0] == "rows":
            out_specs.append(pl.BlockSpec((tm, o[1]), lambda i: (i, 0)))
            out_shape.append(jax.ShapeDtypeStruct((T, o[1]), o[2]))
        else:
            out_specs.append(pl.BlockSpec((NP, tm, PW), lambda i: (0, i, 0)))
            out_shape.append(jax.ShapeDtypeStruct((NP, T, PW), o[1]))
    for shape, dt in accs:
        out_specs.append(pl.BlockSpec(shape, lambda i, nd=len(shape): (0,) * nd))
        out_shape.append(jax.ShapeDtypeStruct(shape, dt))

    def body(*refs):
        i = pl.program_id(0)
        vals = []
        vals = [r[...] for r in refs[:n_in + n_c]]
        res = fn(i, n_tiles, *vals) if with_pid else fn(*vals)
        out_refs = refs[n_in + n_c:]
        for r, v in zip(out_refs[:n_out], res[:n_out]):
            r[...] = v.astype(r.dtype)
        if accs:
            @pl.when(i == 0)
            def _():
                for r in out_refs[n_out:]:
                    r[...] = jnp.zeros_like(r)

            for r, v in zip(out_refs[n_out:], res[n_out:]):
                r[...] += v.astype(r.dtype)

    res = pl.pallas_call(
        body, name=name, grid=(n_tiles,), in_specs=in_specs, out_specs=out_specs, out_shape=out_shape,
        compiler_params=pltpu.CompilerParams(dimension_semantics=("arbitrary",), vmem_limit_bytes=VMEM_LIMIT),
    )(*args)
    return res


def _rms(x, g):
    return x * lax.rsqrt(jnp.mean(x * x, axis=-1, keepdims=True) + NORM_EPS) * g


def _gelu(x):
    return 0.5 * x * (1.0 + lax.erf(x * 0.7071067811865476))


def _sigmoid(x):
    return 1.0 / (1.0 + jnp.exp(-x))


def _bdot(a, b):
    return jnp.dot(a.astype(BF16), b.astype(BF16), preferred_element_type=F32)


def _to_heads(x):
    return jnp.concatenate([x[:, p * PW:(p + 1) * PW][None] for p in range(NP)], axis=0)


def _from_heads(xp):
    return jnp.concatenate([xp[p] for p in range(NP)], axis=-1)


def _head_sum(xp):
    low = lax.broadcasted_iota(jnp.int32, xp.shape, xp.ndim - 1) < HN
    both = jnp.sum(xp, axis=-1, keepdims=True)
    first = jnp.sum(jnp.where(low, xp, 0.0), axis=-1, keepdims=True)
    return jnp.where(low, first, both - first)


def _split_pairs(xp):
    return jnp.concatenate([xp[:, :, :HN], xp[:, :, HN:]], axis=0)


def _join_pairs(xh):
    return jnp.concatenate([xh[:NP], xh[NP:]], axis=-1)


def _sgu_fn(p, ln_w, ln_b, sw, sbt):
    z = _gelu(p)
    u, v = z[:, :D], z[:, D:]
    mu = jnp.mean(v, axis=-1, keepdims=True)
    var = jnp.mean(jnp.square(v - mu), axis=-1, keepdims=True)
    vn = (v - mu) * lax.rsqrt(var + LN_EPS) * ln_w + ln_b
    ri = lax.broadcasted_iota(jnp.int32, (SGU_C, SGU_C), 0)
    ci = lax.broadcasted_iota(jnp.int32, (SGU_C, SGU_C), 1)
    mask = (ci <= ri).astype(F32)
    dg = D // SGU_G
    parts = []
    for g in range(SGU_G):
        parts.append(_bdot(sw[g] * mask, vn[:, g * dg:(g + 1) * dg]) + sbt[:, g:g + 1])
    return u * jnp.concatenate(parts, axis=-1)


def _pre_fn(qr, qk, qv, qxw, qxa, qxg, wl, w0, al, a0, gl, k_k, k_a):
    w = -jax.nn.softplus(-(w0 + _bdot(jnp.tanh(qxw), wl))) - 0.5
    lw = -jnp.exp(w)
    aa = _sigmoid(a0 + _bdot(qxa, al))
    g = _bdot(_sigmoid(qxg), gl)
    kk = _to_heads(qk * k_k)
    kk = kk / jnp.maximum(jnp.sqrt(_head_sum(kk * kk)), 1e-12)
    k2 = qk * (1.0 + (aa - 1.0) * k_a)
    return _to_heads(qr), _to_heads(lw), _to_heads(k2), _to_heads(qv), kk, _to_heads(aa), g


def _post_fn(o, r, k2, v, g, ln_w, ln_b, r_k):
    mu = _head_sum(o) * (1.0 / HN)
    d = o - mu
    var = _head_sum(d * d) * (1.0 / HN)
    on = d * lax.rsqrt(var + GN_EPS) * ln_w + ln_b
    bonus = _head_sum(r * k2 * r_k) * v
    return _from_heads(on + bonus) * g


def _gate_fn(pg, ya, yb):
    return _sigmoid(pg[:, :D]) * ya + _sigmoid(pg[:, D:]) * yb


def _bmm(x, y, cx, cy, out_path=False):
    return lax.dot_general(x, y, (((cx,), (cy,)), ((0,), (0,))),
                           precision=SCAN_OUT_PRECISION if out_path else SCAN_PRECISION, preferred_element_type=F32)


def _unit_lower_inverse(M):
    C = M.shape[1]
    ti = lax.broadcasted_iota(jnp.int32, (C, C), 0)
    tj = lax.broadcasted_iota(jnp.int32, (C, C), 1)
    eye = (ti == tj).astype(F32)
    same = lambda b: (ti // b == tj // b).astype(F32)
    X = -(M * same(SOLVE_B))
    inv = eye + X
    span = 1
    while 2 * span < SOLVE_B:
        X = _bmm(X, X, 2, 1)
        inv = inv + _bmm(inv, X, 2, 1)
        span *= 2
    b = SOLVE_B
    while b < C:
        low = M * (same(2 * b) - same(b))
        inv = inv - _bmm(_bmm(inv, low, 2, 1, out_path=True), inv, 2, 1, out_path=True)
        b *= 2
    return inv


@jax.custom_vjp
def _unit_lower_solve(inv, M, y):
    return _bmm(inv, y, 2, 1)


def _unit_lower_solve_fwd(inv, M, y):
    u = _bmm(inv, y, 2, 1)
    return u, (inv, u)


def _unit_lower_solve_bwd(res, du):
    inv, u = res
    dy = _bmm(inv, du, 1, 1)
    return jnp.zeros_like(inv), -_bmm(dy, u, 2, 2), dy


_unit_lower_solve.defvjp(_unit_lower_solve_fwd, _unit_lower_solve_bwd)


@jax.custom_vjp
def _unit_lower_solved(inv, u, M, y):
    return u


_unit_lower_solved.defvjp(lambda inv, u, M, y: (u, (inv, u)),
                          lambda res, du: (jnp.zeros_like(res[0]), jnp.zeros_like(res[1]))
                          + _unit_lower_solve_bwd(res, du)[1:])


@functools.partial(jax.custom_vjp, nondiff_argnums=(0,))
def _kept(fn, value, *args):
    return value


def _kept_fwd(fn, value, *args):
    return value, args


def _kept_bwd(fn, args, d):
    _, vjp = jax.vjp(fn, *args)
    return (jnp.zeros_like(d),) + tuple(vjp(d))


_kept.defvjp(_kept_fwd, _kept_bwd)


def _sum_over_time(x, reverse):
    C = x.shape[1]
    ti = lax.broadcasted_iota(jnp.int32, (C, C), 0)
    tj = lax.broadcasted_iota(jnp.int32, (C, C), 1)
    ones = jnp.broadcast_to(((tj >= ti) if reverse else (tj <= ti)).astype(BF16), (x.shape[0], C, C))
    hi = x.astype(BF16)
    r1 = x - hi.astype(F32)
    mid = r1.astype(BF16)
    lo = (r1 - mid.astype(F32)).astype(BF16)
    dn = (((2,), (1,)), ((0,), (0,)))
    return sum(lax.dot_general(ones, p, dn, preferred_element_type=F32) for p in (lo, mid, hi))


@jax.custom_vjp
def _time_cumsum(lw):
    return _sum_over_time(lw, reverse=False)


_time_cumsum.defvjp(lambda lw: (_sum_over_time(lw, reverse=False), None),
                    lambda _, d: (_sum_over_time(d, reverse=True),))


def _chunk_fn(S0, r, lw, k, v, kk, a, kept=None):
    C = SCAN_C
    bmm = _bmm
    ti = lax.broadcasted_iota(jnp.int32, (C, C), 0)
    tj = lax.broadcasted_iota(jnp.int32, (C, C), 1)
    incl2 = jnp.concatenate([(tj <= ti).astype(F32)] * 2, axis=1)
    strict = (tj < ti).astype(F32)
    n_mask = jnp.concatenate([jnp.zeros((C, C), F32), strict], axis=1)

    def known(name, fn, *args):
        return fn(*args) if kept is None else _kept(fn, kept[name], *args)

    cum = known("cum", _time_cumsum, lw)
    g_in, g_ex, g_inv = jnp.exp(cum), jnp.exp(cum - lw), jnp.exp(-cum)
    kkt, rt = kk * g_ex, r * g_in
    bk = jnp.concatenate([kk * a * g_inv, k * g_inv], axis=1)
    kr = jnp.concatenate([kkt, rt], axis=1)
    ratios = known("ratios", lambda x, y: bmm(x, y, 2, 2), kr, bk)
    A = ratios[:, :C]
    M = A[:, :, :C] * strict
    zv = jnp.concatenate([jnp.zeros_like(v), v], axis=1)
    s0_side = bmm(kr, S0, 2, 2, out_path=True)
    rhs = s0_side[:, :C] + bmm(A * n_mask, zv, 2, 1, out_path=True)
    if kept is None:
        inv = lax.stop_gradient(_unit_lower_inverse(M))
        y = _unit_lower_solve(inv, M, rhs)
    else:
        inv = kept["inv"]
        y = _unit_lower_solved(inv, kept["y"], M, rhs)
    z = jnp.concatenate([-y, v], axis=1)
    O = s0_side[:, C:] + bmm(ratios[:, C:] * incl2, z, 2, 1, out_path=True)
    g_end = g_in[:, C - 1:C, :]
    S1 = S0 * g_end + bmm(z, bk * g_end, 1, 1, out_path=True)
    return O, S1, dict(cum=cum, ratios=ratios, y=y, inv=inv)


def _scan_fwd(r, lw, k, v, kk, a, ex=None, tb=256):
    assert SCAN_C == HN and 2 * SCAN_C == PW
    T = r.shape[1]
    tb = min(tb, T)
    n_chunks = tb // SCAN_C
    nb = T // tb
    nx = ex.nb if ex else 0

    def body(*refs):
        r_ref, lw_ref, k_ref, v_ref, kk_ref, a_ref = refs[:6]
        x_in, (o_ref, s0_ref), x_out = refs[6:6 + nx], refs[6 + nx:8 + nx], refs[8 + nx:8 + 2 * nx]
        s_ref, sems = refs[8 + 2 * nx], refs[9 + 2 * nx:]

        plan = ex.schedule(nb) if ex else []

        @pl.when(pl.program_id(0) == 0)
        def _():
            s_ref[...] = jnp.zeros_like(s_ref)
            for at, action in plan[:1]:
                action(x_in, x_out, sems)

        def step(c, carry):
            sl = pl.ds(pl.multiple_of(c * SCAN_C, SCAN_C), SCAN_C)
            S0 = s_ref[...]
            O, S1, keep = _chunk_fn(S0, *[_split_pairs(ref[:, sl, :])
                                          for ref in (r_ref, lw_ref, k_ref, v_ref, kk_ref, a_ref)])
            o_ref[:, sl, :] = _join_pairs(O)
            s0_ref[c, 0] = jnp.concatenate([S0, keep["inv"]], axis=-1)
            s0_ref[c, 1] = jnp.concatenate([keep["cum"], keep["y"]], axis=-1)
            s0_ref[c, 2] = keep["ratios"][:, :SCAN_C]
            s0_ref[c, 3] = keep["ratios"][:, SCAN_C:]
            s_ref[...] = S1
            return carry

        lax.fori_loop(0, n_chunks, step, 0)

        for at, action in plan[1:]:
            pl.when(pl.program_id(0) == at)(functools.partial(action, x_in, x_out, sems))

    hm = pl.BlockSpec((NP, tb, PW), lambda i: (0, i, 0))
    res = pl.pallas_call(
        body, name="rwkv_scan_fwd", grid=(nb,), in_specs=[hm] * 6 + (ex.any_specs if ex else []),
        out_specs=[hm, pl.BlockSpec((n_chunks, N_KEPT, NH, HN, PW), lambda i: (i, 0, 0, 0, 0))]
        + (ex.any_specs if ex else []),
        out_shape=[jax.ShapeDtypeStruct((NP, T, PW), F32),
                   jax.ShapeDtypeStruct((T // SCAN_C, N_KEPT, NH, HN, PW), F32)]
        + (ex.out_shape if ex else []),
        scratch_shapes=[pltpu.VMEM((NH, HN, HN), F32)] + (ex.sem_shapes if ex else []),
        compiler_params=pltpu.CompilerParams(dimension_semantics=("arbitrary",), vmem_limit_bytes=VMEM_LIMIT),
    )(r, lw, k, v, kk, a, *(ex.bufs if ex else []))
    return res[0], res[1], list(res[2:])


def _scan_bwd(r, lw, k, v, kk, a, s0s, do, ex=None, tb=128):
    T = r.shape[1]
    tb = min(tb, T)
    n_chunks = tb // SCAN_C
    nb = T // tb
    nx = ex.nb if ex else 0

    def body(*refs):
        r_ref, lw_ref, k_ref, v_ref, kk_ref, a_ref, s0_ref, do_ref = refs[:8]
        x_in, (dr, dlw, dk, dv, dkk, da), x_out = refs[8:8 + nx], refs[8 + nx:14 + nx], refs[14 + nx:14 + 2 * nx]
        ds_ref, sems = refs[14 + 2 * nx], refs[15 + 2 * nx:]

        plan = ex.schedule(nb) if ex else []

        @pl.when(pl.program_id(0) == 0)
        def _():
            ds_ref[...] = jnp.zeros_like(ds_ref)
            for at, action in plan[:1]:
                action(x_in, x_out, sems)

        def step(j, carry):
            c = n_chunks - 1 - j
            sl = pl.ds(pl.multiple_of(c * SCAN_C, SCAN_C), SCAN_C)
            s0_inv, cum_y = s0_ref[c, 0], s0_ref[c, 1]
            kept = dict(inv=s0_inv[:, :, HN:], cum=cum_y[:, :, :HN], y=cum_y[:, :, HN:],
                        ratios=jnp.concatenate([s0_ref[c, 2], s0_ref[c, 3]], axis=1))
            _, vjp = jax.vjp(lambda *t: _chunk_fn(*t, kept=kept)[:2], s0_inv[:, :, :HN],
                             *[_split_pairs(ref[:, sl, :]) for ref in (r_ref, lw_ref, k_ref, v_ref, kk_ref, a_ref)])
            g = vjp((_split_pairs(do_ref[:, sl, :]), ds_ref[...]))
            ds_ref[...] = g[0]
            for ref, val in zip((dr, dlw, dk, dv, dkk, da), g[1:]):
                ref[:, sl, :] = _join_pairs(val)
            return carry

        lax.fori_loop(0, n_chunks, step, 0)

        for at, action in plan[1:]:
            pl.when(pl.program_id(0) == at)(functools.partial(action, x_in, x_out, sems))

    hm = pl.BlockSpec((NP, tb, PW), lambda i: (0, nb - 1 - i, 0))
    res = pl.pallas_call(
        body, name="rwkv_scan_bwd", grid=(nb,),
        in_specs=[hm] * 6 + [pl.BlockSpec((n_chunks, N_KEPT, NH, HN, PW), lambda i: (nb - 1 - i, 0, 0, 0, 0)), hm]
        + (ex.any_specs if ex else []),
        out_specs=[hm] * 6 + (ex.any_specs if ex else []),
        out_shape=[jax.ShapeDtypeStruct((NP, T, PW), F32)] * 6 + (ex.out_shape if ex else []),
        scratch_shapes=[pltpu.VMEM((NH, HN, HN), F32)] + (ex.sem_shapes if ex else []),
        compiler_params=pltpu.CompilerParams(dimension_semantics=("arbitrary",), vmem_limit_bytes=VMEM_LIMIT),
    )(r, lw, k, v, kk, a, s0s, do, *(ex.bufs if ex else []))
    return list(res[:6]), list(res[6:])


def _shift_down(i, p, prev8):
    first = jnp.where(i > 0, prev8[7:8, :], 0.0)
    row = lax.broadcasted_iota(jnp.int32, p.shape, 0)
    return jnp.where(row == 0, first, pltpu.roll(p, 1, axis=0))


def _mix_bwd(dq, p, sb, tm=256):
    def fn(i, n, dq, next8, p, prev8, sb):
        ps = _shift_down(i, p, prev8)
        d1 = dq * sb[1:2]
        last = jnp.where(i < n - 1, next8[0:1, :] * sb[1:2], 0.0)
        row = lax.broadcasted_iota(jnp.int32, dq.shape, 0)
        up = jnp.where(row == dq.shape[0] - 1, last, pltpu.roll(d1, dq.shape[0] - 1, axis=0))
        return (dq * sb[0:1] + up, jnp.sum(dq * p, axis=0, keepdims=True), jnp.sum(dq * ps, axis=0, keepdims=True))
    w = p.shape[1]
    return _rows_call("shift_mix_bwd", fn, [Rows(dq), Halo(dq, +1), Rows(p), Halo(p, -1)], [sb], [("rows", w, BF16)],
                      accs=[((1, w), F32), ((1, w), F32)], tm=tm, with_pid=True)


def _local_step(x, target, W, late_weights=None, early_grads=None, w_in_grads_ready=None, ffn_grads_ready=None):
    G = {}
    a = _rows_call("norm_mix_fwd", lambda x, g: (_rms(x, g),), [Rows(x)], [W["g_mix"]], [("rows", D, BF16)])[0]
    p_sgu = _matmul("proj_sgu", a, W["w_sgu_t"], "nt", after=W.get("ffn_weights_sent"))
    def token_shift(p, sb0, sb1):
        row = lax.broadcasted_iota(jnp.int32, p.shape, 0)
        return p, p * sb0 + jnp.where(row == 0, 0.0, pltpu.roll(p, 1, axis=0)) * sb1
    p_rw, q = _matmul("proj_rwkv", a, W["w_rw_t"], "nt", tn=512, whole_rows=True, epilogue=token_shift,
                      extras=[W["sb"][0:1], W["sb"][1:2]], out_dtypes=(F32, F32))
    p_gate = _matmul("proj_gate", a, W["w_gate_t"], "nt")

    sgu_consts = [W["sgu_ln_w"], W["sgu_ln_b"], W["sgu_w"], W["sgu_bt"]]
    s = _rows_call("sgu_fwd", lambda *t: (_sgu_fn(*t),), [Rows(p_sgu)], sgu_consts, [("rows", D, BF16)], tm=SGU_C)[0]

    q_ins = [Rows(q, D, 0), Rows(q, D, 1), Rows(q, D, 2), Rows(q, 128, 24), Rows(q, 128, 25), Rows(q, 256, 13)]
    pre_consts = [W["w_lora"], W["w0"], W["a_lora"], W["a0"], W["g_lora"], W["k_k"], W["k_a"]]
    r_h, lw_h, k_h, v_h, kk_h, a_h, g_gate = _rows_call(
        "rwkv_pre_fwd", _pre_fn, q_ins, pre_consts, [("heads", F32)] * 6 + [("rows", D, F32)], tm=128)
    o_h, s0s, got = _scan_fwd(r_h, lw_h, k_h, v_h, kk_h, a_h, ex=late_weights[0] if late_weights else None)
    if late_weights:
        W = {**W, **late_weights[1](got, o_h)}
    y_a = _matmul("proj_a", s, W["w_proj_a"], "nn")
    post_ins = [Heads(o_h), Heads(r_h), Heads(k_h), Heads(v_h), Rows(g_gate)]
    post_consts = [W[n].reshape(NP, 1, PW) for n in ("ln_x_w", "ln_x_b", "r_k")]
    z_b = _rows_call("rwkv_post_fwd", lambda *t: (_post_fn(*t),), post_ins, post_consts, [("rows", D, BF16)], tm=128)[0]
    y_b, mixed = _matmul("proj_b", z_b, W["w_proj_b"], "nn", extras=[(p_gate, 0), (p_gate, 1), y_a],
                         epilogue=lambda yb, ga, gb, ya: (yb, _sigmoid(ga) * ya + _sigmoid(gb) * yb),
                         out_dtypes=(F32, BF16))

    def res1(mo, x, g):
        h1 = x + mo
        return h1, _rms(h1, g)
    h1, f = _matmul("proj_out", mixed, W["w_out"], "nn", extras=[x, W["g_ffn"]], epilogue=res1,
                    out_dtypes=(F32, BF16))

    def relu_sq(u):
        r = jnp.maximum(u, 0.0)
        return r, r * r
    r1, act = _matmul("ffn_up", f, W["w_ffn1"], "nn", epilogue=relu_sq, out_dtypes=(BF16, BF16))
    ff = _matmul("ffn_down", act, W["w_ffn2"], "nn")

    def head(h1, ff, tgt, g):
        def f_(h1, ff, g):
            y = _rms(h1 + ff, g)
            return 0.5 * jnp.sum(jnp.mean(jnp.square(y - tgt), axis=-1))
        loss, (dh2, _, dg) = jax.value_and_grad(f_, argnums=(0, 1, 2))(h1, ff, g)
        return dh2, jnp.full((8, LANES), loss, F32), dg
    dh2, loss_acc, G["g_final"] = _rows_call("loss_head", head, [Rows(h1), Rows(ff), Rows(target)], [W["g_final"]],
                                             [("rows", D, F32)], accs=[((8, LANES), F32), ((1, D), F32)])

    d_u1 = _matmul("ffn_down_dx", dh2, W["w_ffn2"], "nt", extras=[r1], out_dtypes=(BF16,),
                   epilogue=lambda d_act, r: (d_act * 2.0 * r.astype(F32),))[0]
    G["w_ffn2"] = _matmul("ffn_down_dw", act, dh2, "tn", out_dtype=GRAD_PAYLOAD)
    d_f = _matmul("ffn_up_dx", d_u1, W["w_ffn1"], "nt")
    G["w_ffn1"] = _matmul("ffn_up_dw", f, d_u1, "tn", out_blocks=N_DEV, out_dtype=GRAD_PAYLOAD)

    def res1_bwd(h1, d_f, dh2, g):
        _, vjp = jax.vjp(_rms, h1, g)
        dh, dg = vjp(d_f)
        return dh2 + dh, dg
    dh1, G["g_ffn"] = _rows_call("residual_norm_bwd", res1_bwd, [Rows(h1), Rows(d_f), Rows(dh2)],
                                 [W["g_ffn"]], [("rows", D, F32)], accs=[((1, D), F32)])
    ffn_token = ffn_grads_ready(G) if ffn_grads_ready else None
    def gate_bwd(d_mixed, ga, gb, ya, yb):
        _, vjp = jax.vjp(_gate_fn, jnp.concatenate([ga, gb], axis=-1), ya, yb)
        return vjp(d_mixed)
    d_gate, d_ya, d_yb = _matmul("proj_out_dx", dh1, W["w_out"], "nt", after=ffn_token, epilogue=gate_bwd,
                                 extras=[(p_gate, 0), (p_gate, 1), y_a, y_b], out_dtypes=(BF16, BF16, BF16),
                                 out_widths=(2 * D, D, D))
    G["w_out"] = _matmul("proj_out_dw", mixed, dh1, "tn", out_dtype=GRAD_PAYLOAD)

    d_s = _matmul("proj_a_dx", d_ya, W["w_proj_a"], "nt")
    G["w_proj_a"] = _matmul("proj_a_dw", s, d_ya, "tn", out_dtype=GRAD_PAYLOAD)

    def sgu_bwd(p, ds, *c):
        _, vjp = jax.vjp(_sgu_fn, p, *c)
        return vjp(ds)
    d_p_sgu, G["sgu_ln_w"], G["sgu_ln_b"], G["sgu_w"], G["sgu_bt"] = _rows_call(
        "sgu_bwd", sgu_bwd, [Rows(p_sgu), Rows(d_s)], sgu_consts, [("rows", 2 * D, BF16)],
        accs=[((1, D), F32), ((1, D), F32), ((SGU_G, SGU_C, SGU_C), F32), ((SGU_C, SGU_G), F32)], tm=SGU_C)

    d_zb = _matmul("proj_b_dx", d_yb, W["w_proj_b"], "nt")
    G["w_proj_b"] = _matmul("proj_b_dw", z_b, d_yb, "tn", out_dtype=GRAD_PAYLOAD)

    def post_bwd(o, r, k2, v, g, dz, *c):
        _, vjp = jax.vjp(_post_fn, o, r, k2, v, g, *c)
        return vjp(dz)
    do_h, dr1, dk1, dv1, d_g, g_lnw, g_lnb, g_rk = _rows_call(
        "rwkv_post_bwd", post_bwd, post_ins + [Rows(d_zb)], post_consts, [("heads", F32)] * 4 + [("rows", D, F32)],
        accs=[((NP, 1, PW), F32)] * 3, tm=128)
    G["ln_x_w"], G["ln_x_b"], G["r_k"] = (t.reshape(1, D) for t in (g_lnw, g_lnb, g_rk))
    (dr2, dlw, dk2, dv2, dkk, daa), early = _scan_bwd(r_h, lw_h, k_h, v_h, kk_h, a_h, s0s, do_h,
                                                      ex=early_grads(G) if early_grads else None)

    def pre_bwd(qr, qk, qv, qxw, qxa, qxg, dr1, dr2, dlw, dk1, dk2, dv1, dv2, dkk, daa, dg, *c):
        _, vjp = jax.vjp(_pre_fn, qr, qk, qv, qxw, qxa, qxg, *c)
        g = vjp((dr1 + dr2, dlw, dk1 + dk2, dv1 + dv2, dkk, daa, dg))
        dq = jnp.concatenate(g[:6], axis=-1)
        return (dq,) + tuple(g[6:])
    pre_b_ins = q_ins + [Heads(dr1), Heads(dr2), Heads(dlw), Heads(dk1), Heads(dk2), Heads(dv1), Heads(dv2),
                         Heads(dkk), Heads(daa), Rows(d_g)]
    d_q, G["w_lora"], G["w0"], G["a_lora"], G["a0"], G["g_lora"], G["k_k"], G["k_a"] = _rows_call(
        "rwkv_pre_bwd", pre_bwd, pre_b_ins, pre_consts, [("rows", RW_INT, F32)],
        accs=[((128, D), F32), ((1, D), F32), ((128, D), F32), ((1, D), F32), ((256, D), F32), ((1, D), F32),
              ((1, D), F32)], tm=128)
    d_p_rw, dsb0, dsb1 = _mix_bwd(d_q, p_rw, W["sb"])
    G["sb"] = jnp.concatenate([dsb0, dsb1], axis=0)

    G["w_sgu_t"] = _matmul("proj_sgu_dw", d_p_sgu, a, "tn", out_dtype=GRAD_PAYLOAD)
    G["w_rw_t"] = _matmul("proj_rwkv_dw", d_p_rw, a, "tn", out_dtype=GRAD_PAYLOAD)
    G["w_gate_t"] = _matmul("proj_gate_dw", d_gate, a, "tn", out_dtype=GRAD_PAYLOAD)
    token = w_in_grads_ready(G) if w_in_grads_ready else None
    da1 = _matmul("proj_sgu_dx", d_p_sgu, W["w_sgu_t"], "nn", after=token)
    da2 = _matmul("proj_rwkv_dx", d_p_rw, W["w_rw_t"], "nn", after=token)
    da3 = _matmul("proj_gate_dx", d_gate, W["w_gate_t"], "nn", after=token)

    def norm1_bwd(x, da1, da2, da3, dh1, g):
        _, vjp = jax.vjp(_rms, x, g)
        dx, dg = vjp(da1 + da2 + da3)
        return dh1 + dx, dg
    dx, G["g_mix"] = _rows_call("norm_mix_bwd", norm1_bwd, [Rows(x), Rows(da1), Rows(da2), Rows(da3), Rows(dh1)],
                                [W["g_mix"]], [("rows", D, F32)], accs=[((1, D), F32)])
    return loss_acc[0, 0], dx, G, early


class Exchange:
    def __init__(self, bufs, gathers):
        self.bufs, self.gathers, self.nb = list(bufs), list(gathers), len(bufs)
        self.any_specs = [pl.BlockSpec(memory_space=pl.ANY)] * self.nb
        self.out_shape = [jax.ShapeDtypeStruct((N_DEV,) + (b.shape if g else b.shape[1:]), b.dtype)
                          for b, g in zip(self.bufs, self.gathers)]
        n = (N_DEV - 1) * self.nb
        self.sem_shapes = [pltpu.SemaphoreType.DMA((n,)), pltpu.SemaphoreType.DMA((n,)),
                           pltpu.SemaphoreType.DMA((self.nb,))]

    def _copies(self, in_refs, out_refs, sems):
        send_sems, recv_sems, local_sems = sems
        x, y, c = lax.axis_index("x"), lax.axis_index("y"), lax.axis_index("c")
        me = 4 * x + 2 * y + c

        def src(b, dest):
            return in_refs[b] if self.gathers[b] else in_refs[b].at[dest]

        local = [pltpu.make_async_copy(src(b, me), out_refs[b].at[me], local_sems.at[b]) for b in range(self.nb)]
        sends, recvs = [], []
        for kbits in range(1, N_DEV):
            px = 1 - x if kbits & 4 else x
            py = 1 - y if kbits & 2 else y
            pc = 1 - c if kbits & 1 else c
            peer = 4 * px + 2 * py + pc
            for b in range(self.nb):
                s = (kbits - 1) * self.nb + b
                sends.append(pltpu.make_async_remote_copy(
                    src_ref=src(b, peer), dst_ref=out_refs[b].at[me], send_sem=send_sems.at[s],
                    recv_sem=recv_sems.at[s], device_id=(px, py, pc), device_id_type=pl.DeviceIdType.MESH))
                recvs.append(pltpu.make_async_remote_copy(
                    src_ref=src(b, peer), dst_ref=out_refs[b].at[peer], send_sem=send_sems.at[s],
                    recv_sem=recv_sems.at[s], device_id=(px, py, pc), device_id_type=pl.DeviceIdType.MESH))
        return local, sends, recvs

    def start(self, in_refs, out_refs, sems):
        local, sends, _ = self._copies(in_refs, out_refs, sems)
        for cp in sends + local:
            cp.start()

    def wait(self, in_refs, out_refs, sems):
        local, sends, recvs = self._copies(in_refs, out_refs, sems)
        for cp in recvs:
            cp.wait_recv()
        for cp in sends:
            cp.wait_send()
        for cp in local:
            cp.wait()

    def schedule(self, n_steps):
        return [(0, self.start), (n_steps - 1, self.wait)]


def _exchange(name, bufs, gather):
    ex = Exchange(bufs, gather if isinstance(gather, (list, tuple)) else [gather] * len(bufs))

    def body(*refs):
        in_refs, out_refs, sems = refs[:ex.nb], refs[ex.nb:2 * ex.nb], refs[2 * ex.nb:]
        ex.start(in_refs, out_refs, sems)
        ex.wait(in_refs, out_refs, sems)

    return pl.pallas_call(body, name=name, in_specs=ex.any_specs, out_specs=ex.any_specs, out_shape=ex.out_shape,
                          scratch_shapes=ex.sem_shapes)(*ex.bufs)


N_CHIP = 4


def _pair_exchange(name, blocks):
    def body(b_ref, got_ref, send_sems, recv_sems):
        x, y, c = lax.axis_index("x"), lax.axis_index("y"), lax.axis_index("c")
        copies = [pltpu.make_async_remote_copy(
            src_ref=b_ref.at[2 * q + 1 - c], dst_ref=got_ref.at[q], send_sem=send_sems.at[q],
            recv_sem=recv_sems.at[q], device_id=(x, y, 1 - c), device_id_type=pl.DeviceIdType.MESH)
            for q in range(N_CHIP)]
        for cp in copies:
            cp.start()
        for cp in copies:
            cp.wait_recv()
        for cp in copies:
            cp.wait_send()

    any_spec = pl.BlockSpec(memory_space=pl.ANY)
    return pl.pallas_call(
        body, name=name, in_specs=[any_spec], out_specs=any_spec,
        out_shape=jax.ShapeDtypeStruct((N_CHIP,) + blocks.shape[1:], blocks.dtype),
        scratch_shapes=[pltpu.SemaphoreType.DMA((N_CHIP,))] * 2,
    )(blocks)


def _pair_sum(name, blocks, got, core):
    _, R, Wd = blocks.shape

    def body(c_ref, a_ref, b_ref, o_ref):
        o_ref[...] = (a_ref[...].astype(F32) + b_ref[...].astype(F32)).astype(o_ref.dtype)

    return pl.pallas_call(
        body, name=name,
        grid_spec=pltpu.PrefetchScalarGridSpec(
            num_scalar_prefetch=1, grid=(N_CHIP,),
            in_specs=[pl.BlockSpec((None, R, Wd), lambda q, c_ref: (2 * q + c_ref[0], 0, 0)),
                      pl.BlockSpec((None, R, Wd), lambda q, c_ref: (q, 0, 0))],
            out_specs=pl.BlockSpec((None, R, Wd), lambda q, c_ref: (q, 0, 0))),
        out_shape=jax.ShapeDtypeStruct(got.shape, blocks.dtype),
        compiler_params=pltpu.CompilerParams(dimension_semantics=("parallel",), vmem_limit_bytes=VMEM_LIMIT),
    )(core, blocks, got)


def _chip_copies(s_ref, land_ref, send_sems, recv_sems):
    x, y, c = lax.axis_index("x"), lax.axis_index("y"), lax.axis_index("c")
    my_q = 2 * x + y
    sends, recvs = [], []
    for kbits in range(1, N_CHIP):
        px = 1 - x if kbits & 2 else x
        py = 1 - y if kbits & 1 else y
        peer_q = 2 * px + py
        sends.append(pltpu.make_async_remote_copy(
            src_ref=s_ref.at[peer_q], dst_ref=land_ref.at[my_q], send_sem=send_sems[kbits - 1],
            recv_sem=recv_sems[kbits - 1], device_id=(px, py, c), device_id_type=pl.DeviceIdType.MESH))
        recvs.append(pltpu.make_async_remote_copy(
            src_ref=s_ref.at[peer_q], dst_ref=land_ref.at[peer_q], send_sem=send_sems[kbits - 1],
            recv_sem=recv_sems[kbits - 1], device_id=(px, py, c), device_id_type=pl.DeviceIdType.MESH))
    return sends, recvs


_HBM = pl.BlockSpec(memory_space=pltpu.HBM)
_SEM = pl.BlockSpec(memory_space=pltpu.SEMAPHORE)
N_CHIP_SEMS = 2 * (N_CHIP - 1)


def _scatter_copies(b_refs, land_refs, send_sems, recv_sems, gather=False):
    x, y, c = lax.axis_index("x"), lax.axis_index("y"), lax.axis_index("c")
    me = 4 * x + 2 * y + c
    sends, recvs = [], []
    for kbits in range(1, N_DEV):
        px = 1 - x if kbits & 4 else x
        py = 1 - y if kbits & 2 else y
        pc = 1 - c if kbits & 1 else c
        peer = 4 * px + 2 * py + pc
        for b in range(len(b_refs)):
            s = (kbits - 1) * len(b_refs) + b
            src = b_refs[b] if gather else b_refs[b].at[peer]
            sends.append(pltpu.make_async_remote_copy(
                src_ref=src, dst_ref=land_refs[b].at[me], send_sem=send_sems[s],
                recv_sem=recv_sems[s], device_id=(px, py, pc), device_id_type=pl.DeviceIdType.MESH))
            recvs.append(pltpu.make_async_remote_copy(
                src_ref=src, dst_ref=land_refs[b].at[peer], send_sem=send_sems[s],
                recv_sem=recv_sems[s], device_id=(px, py, pc), device_id_type=pl.DeviceIdType.MESH))
    return sends, recvs


def _scatter_start(name, bufs, gather=False, after=None):
    nb = len(bufs)
    n = (N_DEV - 1) * nb
    n_after = 0 if after is None else 1

    def body(*refs):
        b_refs, land_refs, outs = refs[:nb], refs[nb:2 * nb], refs[2 * nb + n_after:]
        sends, _ = _scatter_copies(b_refs, land_refs, outs[:n], outs[n:2 * n], gather)
        for cp in sends:
            cp.start()
        token = outs[2 * n + 2 * nb]
        token[...] = jnp.zeros_like(token)

    lands = [lax.empty(((N_DEV,) + b.shape) if gather else b.shape, b.dtype) for b in bufs]
    thru = tuple(pltpu.HBM(b.shape, b.dtype) for b in list(bufs) + lands)
    res = pl.pallas_call(
        body, name=name, in_specs=(_HBM,) * (2 * nb) + (pl.BlockSpec(memory_space=pl.ANY),) * n_after,
        out_specs=(_SEM,) * (2 * n) + (_HBM,) * (2 * nb) + (pl.BlockSpec(memory_space=pltpu.VMEM),),
        out_shape=(pltpu.SemaphoreType.DMA(()),) * (2 * n) + thru + (jax.ShapeDtypeStruct((8, LANES), F32),),
        input_output_aliases={i: 2 * n + i for i in range(2 * nb)},
        compiler_params=pltpu.CompilerParams(has_side_effects=pltpu.SideEffectType.DATAFLOW_SIDE_EFFECTING),
    )(*[pltpu.with_memory_space_constraint(b, pltpu.HBM) for b in list(bufs) + lands], *([after] if n_after else []))
    return res[:2 * n], list(res[2 * n:2 * n + nb]), list(res[2 * n + nb:2 * n + 2 * nb]), res[2 * n + 2 * nb]


def _scatter_wait(name, sems, bufs_thru, lands_thru, after, gather=False):
    nb = len(bufs_thru)
    n = (N_DEV - 1) * nb

    def body(*refs):
        b_refs, land_refs, sem_refs = refs[:nb], refs[nb:2 * nb], refs[2 * nb:2 * nb + 2 * n]
        sends, recvs = _scatter_copies(b_refs, land_refs, sem_refs[:n], sem_refs[n:], gather)
        for cp in sends:
            cp.wait_send()
        for cp in recvs:
            cp.wait_recv()

    thru = tuple(pltpu.HBM(b.shape, b.dtype) for b in list(bufs_thru) + list(lands_thru))
    res = pl.pallas_call(
        body, name=name, in_specs=(_HBM,) * (2 * nb) + (_SEM,) * (2 * n) + (pl.BlockSpec(memory_space=pl.ANY),),
        out_specs=(_HBM,) * (2 * nb), out_shape=thru,
        input_output_aliases={i: i for i in range(2 * nb)},
        compiler_params=pltpu.CompilerParams(has_side_effects=pltpu.SideEffectType.DATAFLOW_SIDE_EFFECTING),
    )(*bufs_thru, *lands_thru, *sems, after)
    return list(res[:nb]), list(res[nb:])


def _chip_exchange_start(name, sums):
    def body(s_ref, land_ref, *outs):
        sems, token = outs[:N_CHIP_SEMS], outs[N_CHIP_SEMS + 2]
        sends, _ = _chip_copies(s_ref, land_ref, sems[:N_CHIP - 1], sems[N_CHIP - 1:])
        for cp in sends:
            cp.start()
        token[...] = jnp.zeros_like(token)

    res = pl.pallas_call(
        body, name=name, in_specs=(_HBM, _HBM),
        out_specs=(_SEM,) * N_CHIP_SEMS + (_HBM, _HBM, pl.BlockSpec(memory_space=pltpu.VMEM)),
        out_shape=(pltpu.SemaphoreType.DMA(()),) * N_CHIP_SEMS
        + (pltpu.HBM(sums.shape, sums.dtype), pltpu.HBM(sums.shape, sums.dtype), jax.ShapeDtypeStruct((8, LANES), F32)),
        input_output_aliases={0: N_CHIP_SEMS, 1: N_CHIP_SEMS + 1},
        compiler_params=pltpu.CompilerParams(has_side_effects=pltpu.SideEffectType.DATAFLOW_SIDE_EFFECTING),
    )(pltpu.with_memory_space_constraint(sums, pltpu.HBM),
      pltpu.with_memory_space_constraint(lax.empty(sums.shape, sums.dtype), pltpu.HBM))
    return res[:N_CHIP_SEMS], res[N_CHIP_SEMS], res[N_CHIP_SEMS + 1], res[N_CHIP_SEMS + 2]


def _chip_exchange_wait(name, sems, sums_thru, land_thru, after):
    def body(s_ref, land_ref, *rest):
        sems = rest[:N_CHIP_SEMS]
        sends, recvs = _chip_copies(s_ref, land_ref, sems[:N_CHIP - 1], sems[N_CHIP - 1:])
        for cp in sends:
            cp.wait_send()
        for cp in recvs:
            cp.wait_recv()

    return pl.pallas_call(
        body, name=name, in_specs=(_HBM, _HBM) + (_SEM,) * N_CHIP_SEMS + (pl.BlockSpec(memory_space=pl.ANY),),
        out_specs=(_HBM, _HBM),
        out_shape=(pltpu.HBM(sums_thru.shape, sums_thru.dtype), pltpu.HBM(sums_thru.shape, sums_thru.dtype)),
        input_output_aliases={0: 0, 1: 1},
        compiler_params=pltpu.CompilerParams(has_side_effects=pltpu.SideEffectType.DATAFLOW_SIDE_EFFECTING),
    )(sums_thru, land_thru, *sems, after)


def _adamw(name, slots, w, m, v, tr=256):
    unit_mid = w.ndim == 3 and w.shape[1] == 1 and w.shape[0] > 1
    R, Wd = (w.shape[0], w.shape[2]) if unit_mid else w.shape[-2:]
    depth_axis = w.ndim == 3 and not unit_mid
    if R % tr == 0:
        tc = Wd
    else:
        tr, tc = R, (256 if (Wd % 256 == 0 and R > 256) else Wd)
    at = (slice(None), 0, slice(None)) if unit_mid else Ellipsis

    def body(s_ref, w_ref, m_ref, v_ref, g_out, d_out, m_out, v_out):
        g = s_ref[0].astype(F32)
        for j in range(1, slots.shape[0]):
            g = g + s_ref[j].astype(F32)
        m_new = ADAM_B1 * m_ref[at] + (1.0 - ADAM_B1) * g
        v_new = ADAM_B2 * v_ref[at] + (1.0 - ADAM_B2) * jnp.square(g)
        m_hat = m_new / (1.0 - ADAM_B1 ** ADAM_STEP)
        v_hat = v_new / (1.0 - ADAM_B2 ** ADAM_STEP)
        g_out[at] = g
        d_out[at] = -ADAM_LR * (m_hat / (jnp.sqrt(v_hat) + ADAM_EPS) + ADAM_WD * w_ref[at])
        m_out[at] = m_new
        v_out[at] = v_new

    if unit_mid:
        row = pl.BlockSpec((tr, 1, tc), lambda i, j: (i, 0, j))
    elif depth_axis:
        row = pl.BlockSpec((None, tr, tc), lambda i, j: (0, i, j))
    else:
        row = pl.BlockSpec((tr, tc), lambda i, j: (i, j))
    return pl.pallas_call(
        body, name=name, grid=(R // tr, Wd // tc),
        in_specs=[pl.BlockSpec((slots.shape[0], tr, tc), lambda i, j: (0, i, j)), row, row, row],
        out_specs=[row] * 4, out_shape=[jax.ShapeDtypeStruct(w.shape, F32)] * 4,
        compiler_params=pltpu.CompilerParams(dimension_semantics=("parallel", "parallel"),
                                             vmem_limit_bytes=VMEM_LIMIT),
    )(slots, w, m, v)


PACK_W = 1024
PACKED = [(n, s) for n, s in REPLICATED if n != "sgu_w"]
_SMALL_SIZES = [int(np.prod(s)) for _, s in PACKED]
_SMALL_ROWS = _round_up(_round_up(sum(_SMALL_SIZES) + PACK_W, PACK_W) // PACK_W, 8)
_LOSS_AT = sum(_SMALL_SIZES)
W_IN_SHARD = P_TOTAL // N_DEV


def _pack_rows(parts, rows, dtype):
    flat = jnp.concatenate([p.reshape(-1).astype(dtype) for p in parts])
    return jnp.pad(flat, (0, rows * PACK_W - flat.shape[0])).reshape(rows, PACK_W)


def _w_in_groups_t(blocks):
    wt = blocks.reshape(P_TOTAL, D)
    o, c = 2 * D, 2 * D + 3 * D
    z = lambda r: jnp.zeros((r, D), wt.dtype)
    rw = jnp.concatenate([wt[o:c], wt[c:c + L_W], z(128 - L_W), wt[c + L_W:c + L_W + L_A], z(128 - L_A),
                          wt[c + L_W + L_A:o + C_B], z(256 - L_G)], axis=0)
    return wt[:o], rw, wt[o + C_B:]


def _w_in_grad_blocks(g_sgu_t, g_rw_t, g_gate_t):
    c = 3 * D
    full = jnp.concatenate([g_sgu_t, g_rw_t[:c], g_rw_t[c:c + L_W], g_rw_t[c + 128:c + 128 + L_A],
                            g_rw_t[c + 256:c + 256 + L_G], g_gate_t], axis=0)
    return full.reshape(N_DEV, W_IN_SHARD, D)


def _mesh_index():
    me = 4 * lax.axis_index("x") + 2 * lax.axis_index("y") + lax.axis_index("c")
    return me.astype(jnp.int32).reshape(1)


def _fill_slot(name, dst, src, idx, src_idx=None):
    R, Wd = dst.shape[1:]
    scalars = [idx] if src_idx is None else [idx, src_idx]
    if src_idx is None:
        src_spec = pl.BlockSpec((R, Wd), lambda i, *s: (0, 0))
    else:
        src_spec = pl.BlockSpec((None, R, Wd), lambda i, *s: (s[1][0], 0, 0))

    def body(*refs):
        src_ref, out_ref = refs[len(scalars) + 1], refs[len(scalars) + 2]
        out_ref[...] = src_ref[...]

    return pl.pallas_call(
        body, name=name,
        grid_spec=pltpu.PrefetchScalarGridSpec(
            num_scalar_prefetch=len(scalars), grid=(1,),
            in_specs=[pl.BlockSpec(memory_space=pl.ANY), src_spec],
            out_specs=pl.BlockSpec((None, R, Wd), lambda i, *s: (s[0][0], 0, 0))),
        out_shape=jax.ShapeDtypeStruct(dst.shape, dst.dtype),
        input_output_aliases={len(scalars): 0},
        compiler_params=pltpu.CompilerParams(vmem_limit_bytes=VMEM_LIMIT),
    )(*scalars, dst, src)


class TwoLevelGather:
    def __init__(self, bufs, skip_own=()):
        self.bufs, self.nb, self.skip_own = list(bufs), len(bufs), tuple(skip_own)
        self.any_specs = [pl.BlockSpec(memory_space=pl.ANY)] * self.nb
        self.out_shape = [jax.ShapeDtypeStruct((N_DEV,) + b.shape, b.dtype) for b in self.bufs]
        self.sem_shapes = [pltpu.SemaphoreType.DMA((7 * self.nb,)), pltpu.SemaphoreType.DMA((7 * self.nb,)),
                           pltpu.SemaphoreType.DMA((self.nb,))]

    def _copies(self, in_refs, out_refs, sems):
        send_sems, recv_sems, local_sems = sems
        nb = self.nb
        x, y, c = lax.axis_index("x"), lax.axis_index("y"), lax.axis_index("c")
        me, sibling = (x, y, c), (x, y, 1 - c)
        chips = [(1 - x, y), (x, 1 - y), (1 - x, 1 - y)]

        def slot(b, dev):
            return out_refs[b].at[4 * dev[0] + 2 * dev[1] + dev[2]]

        def copy(b, k, block, to, own=False):
            return pltpu.make_async_remote_copy(
                src_ref=in_refs[b] if own else slot(b, block), dst_ref=slot(b, block),
                send_sem=send_sems.at[7 * b + k], recv_sem=recv_sems.at[7 * b + k], device_id=to,
                device_id_type=pl.DeviceIdType.MESH)

        cp = {}
        cp["local"] = [pltpu.make_async_copy(in_refs[b], slot(b, me), local_sems.at[b]) for b in range(nb)
                       if b not in self.skip_own]
        cp["first"] = [copy(b, 0, me, sibling, own=True) for b in range(nb)]
        cp["first"] += [copy(b, 1 + j, me, (*chip, c), own=True) for j, chip in enumerate(chips) for b in range(nb)]
        cp["over_ici"] = [copy(b, 1 + j, (*chip, c), me) for j, chip in enumerate(chips) for b in range(nb)]
        cp["passed"] = [copy(b, 4 + j, (*chip, c), sibling) for j, chip in enumerate(chips) for b in range(nb)]
        cp["from_sibling"] = [copy(b, 0, sibling, me) for b in range(nb)]
        cp["from_sibling"] += [copy(b, 4 + j, (*chip, 1 - c), me) for j, chip in enumerate(chips) for b in range(nb)]
        return cp

    def start(self, in_refs, out_refs, sems):
        cp = self._copies(in_refs, out_refs, sems)
        for c in cp["first"] + cp["local"]:
            c.start()

    def forward(self, in_refs, out_refs, sems):
        cp = self._copies(in_refs, out_refs, sems)
        for arrived, onward in zip(cp["over_ici"], cp["passed"]):
            arrived.wait_recv()
            onward.start()

    def finish(self, in_refs, out_refs, sems):
        cp = self._copies(in_refs, out_refs, sems)
        for c in cp["from_sibling"]:
            c.wait_recv()
        for c in cp["first"] + cp["passed"]:
            c.wait_send()
        for c in cp["local"]:
            c.wait()

    def schedule(self, n_steps):
        return [(0, self.start), (max(n_steps - 3, 0), self.forward), (n_steps - 1, self.finish)]


def _all_gather_two_level(name, bufs, skip_own=()):
    ex = TwoLevelGather(bufs, skip_own)

    def body(*refs):
        args = refs[:ex.nb], refs[ex.nb:2 * ex.nb], refs[2 * ex.nb:]
        ex.start(*args)
        ex.forward(*args)
        ex.finish(*args)

    return pl.pallas_call(body, name=name, in_specs=ex.any_specs, out_specs=ex.any_specs, out_shape=ex.out_shape,
                          scratch_shapes=ex.sem_shapes)(*ex.bufs)


def _cols_from_blocks(blk):
    return jnp.transpose(blk, (1, 0, 2)).reshape(blk.shape[1], -1)


def _cols_to_blocks(g):
    r, c = g.shape
    return jnp.transpose(g.reshape(r, N_DEV, c // N_DEV), (1, 0, 2))


FIRST_WEIGHTS = ["w_in", "shift_b", "w_lora_w", "a_lora_w", "g_lora_w"]
SCAN_CARRIED = ["w_proj_a", "w_proj_b", "w_out"]
FFN_WEIGHTS = ["w_ffn1", "w_ffn2"]


def _late_weights(shards, after):
    ex = TwoLevelGather([shards[n].astype(BF16) for n in SCAN_CARRIED])
    sems, sent, landed, token = _scatter_start("weight_ffn_start", [shards[n][0].astype(BF16) for n in FFN_WEIGHTS],
                                               gather=True, after=after)

    def finish(results, scan_out):
        W = {n: r.reshape(-1, D) for n, r in zip(SCAN_CARRIED, results)}
        own, got = _scatter_wait("weight_ffn_wait", sems, sent, landed, after=scan_out, gather=True)
        me = _mesh_index()
        got = [_fill_slot("weight_own_slot_" + n, got[i], own[i], me) for i, n in enumerate(FFN_WEIGHTS)]
        W["w_ffn1"] = got[0]
        W["w_ffn2"] = got[1].reshape(-1, D)
        return W
    return token, (ex, finish)


def _gather_weights(shards):
    def payload(n):
        if n == "w_in":
            return jnp.transpose(shards[n][0]).astype(BF16)
        return shards[n] if n == "shift_b" else shards[n].astype(BF16)
    payloads = [payload(n) for n in FIRST_WEIGHTS]
    got = dict(zip(FIRST_WEIGHTS, _all_gather_two_level("weight_all_gather", payloads, skip_own=(0,))))
    got["w_in"] = _fill_slot("w_in_own_slot", got["w_in"], payloads[0], _mesh_index())
    W = {}
    W["w_sgu_t"], W["w_rw_t"], W["w_gate_t"] = _w_in_groups_t(got["w_in"])
    z = lambda r, c, dt: jnp.zeros((r, c), dt)
    W["w_lora"] = jnp.concatenate([_cols_from_blocks(got["w_lora_w"][:, 0]).astype(F32), z(128 - L_W, D, F32)], axis=0)
    W["a_lora"] = jnp.concatenate([_cols_from_blocks(got["a_lora_w"][:, 0]).astype(F32), z(128 - L_A, D, F32)], axis=0)
    W["g_lora"] = jnp.concatenate([_cols_from_blocks(got["g_lora_w"][:, 0]).astype(F32), z(256 - L_G, D, F32)], axis=0)
    sb = _cols_from_blocks(got["shift_b"][:, 0])
    W["sb"] = jnp.concatenate([sb[:, :3 * D], sb[:, 3 * D:3 * D + L_W], z(2, 128 - L_W, F32),
                               sb[:, 3 * D + L_W:3 * D + L_W + L_A], z(2, 128 - L_A, F32),
                               sb[:, 3 * D + L_W + L_A:], z(2, 256 - L_G, F32)], axis=1)
    return W, got["shift_b"]


def _replicated_weights(rep):
    W = {n: rep[n] for n in ("g_mix", "sgu_ln_w", "sgu_ln_b", "w0", "a0", "k_k", "k_a", "r_k", "ln_x_w", "ln_x_b",
                             "g_ffn")}
    W["g_final"] = rep["g_final"].reshape(1, D)
    W["sgu_w"] = rep["sgu_w"][0]
    W["sgu_bt"] = jnp.transpose(rep["sgu_b"][0])
    return W


def _late_grad_blocks(G):
    return Exchange([G[n].reshape(N_DEV, -1, D) for n in SCAN_CARRIED]
                    + [G["sgu_w"].reshape(SGU_G * SGU_C, SGU_C).astype(GRAD_PAYLOAD)],
                    [False] * len(SCAN_CARRIED) + [True])


def _first_grad_blocks(G):
    sbg = G["sb"]
    c = 3 * D
    sb = jnp.concatenate([sbg[:, :c], sbg[:, c:c + L_W], sbg[:, c + 128:c + 128 + L_A],
                          sbg[:, c + 256:c + 256 + L_G]], axis=1)
    return {
        "shift_b": _cols_to_blocks(sb),
        "w_lora_w": _cols_to_blocks(G["w_lora"][:L_W]), "a_lora_w": _cols_to_blocks(G["a_lora"][:L_A]),
        "g_lora_w": _cols_to_blocks(G["g_lora"][:L_G]),
    }


def _replicated_grads(G):
    small = {n: G[n] for n in ("g_mix", "sgu_ln_w", "sgu_ln_b", "w0", "a0", "k_k", "k_a", "r_k", "ln_x_w", "ln_x_b",
                               "g_ffn", "g_final")}
    small["sgu_w"] = G["sgu_w"]
    small["sgu_b"] = jnp.transpose(G["sgu_bt"])
    return small


def kernel(x, g_mix, w_in, sgu_ln_w, sgu_ln_b, sgu_w, sgu_b, w_proj_a, shift_b, w_lora_w, w0, a_lora_w, a0, g_lora_w, k_k, k_a, r_k, ln_x_w, ln_x_b, w_proj_b, w_out, g_ffn, w_ffn1, w_ffn2, g_final, loss_target, m_g_mix, m_w_in, m_sgu_ln_w, m_sgu_ln_b, m_sgu_w, m_sgu_b, m_w_proj_a, m_shift_b, m_w_lora_w, m_w0, m_a_lora_w, m_a0, m_g_lora_w, m_k_k, m_k_a, m_r_k, m_ln_x_w, m_ln_x_b, m_w_proj_b, m_w_out, m_g_ffn, m_w_ffn1, m_w_ffn2, m_g_final, v_g_mix, v_w_in, v_sgu_ln_w, v_sgu_ln_b, v_sgu_w, v_sgu_b, v_w_proj_a, v_shift_b, v_w_lora_w, v_w0, v_a_lora_w, v_a0, v_g_lora_w, v_k_k, v_k_a, v_r_k, v_ln_x_w, v_ln_x_b, v_w_proj_b, v_w_out, v_g_ffn, v_w_ffn1, v_w_ffn2, v_g_final):
    env = dict(locals())
    weights = {n: env[n] for n in WEIGHT_ORDER}
    moms = {n: env["m_" + n] for n in WEIGHT_ORDER}
    vars_ = {n: env["v_" + n] for n in WEIGHT_ORDER}

    shards = {n: weights[n] for n, _, _ in SHARDED}
    W, gathered = _gather_weights(shards)
    W.update(_replicated_weights({n: weights[n] for n, _ in REPLICATED}))
    W["ffn_weights_sent"], late_weights = _late_weights(shards, after=gathered)
    in_flight = {}

    def send_w_in_grads(G):
        blocks = _w_in_grad_blocks(G["w_sgu_t"], G["w_rw_t"], G["w_gate_t"])
        got = _pair_exchange("grad_pair_exchange", blocks)
        core = lax.axis_index("c").astype(jnp.int32).reshape(1)
        sums = _pair_sum("grad_pair_sum", blocks, got, core)
        in_flight["sems"], in_flight["sums"], in_flight["land"], token = _chip_exchange_start("grad_chip_start", sums)
        return token

    def send_ffn_grads(G):
        in_flight["ffn"] = _scatter_start("grad_ffn_start", [G["w_ffn1"], G["w_ffn2"].reshape(N_DEV, -1, D)])
        return in_flight["ffn"][3]

    loss_part, dx, G, late_slots = _local_step(x[0], loss_target[0], W, late_weights=late_weights,
                                               early_grads=_late_grad_blocks, w_in_grads_ready=send_w_in_grads,
                                               ffn_grads_ready=send_ffn_grads)

    slots = dict(zip(SCAN_CARRIED, late_slots))
    me = _mesh_index()
    sent, landed = _scatter_wait("grad_ffn_wait", *in_flight["ffn"][:3], after=dx)
    for i, n in enumerate(FFN_WEIGHTS):
        slots[n] = _fill_slot("grad_own_slot_" + n, landed[i], sent[i], me, src_idx=me)
    blocks = _first_grad_blocks(G)
    small = _replicated_grads(G)
    small_parts = [small[n] for n, _ in PACKED] + [jnp.full((PACK_W,), loss_part, F32)]
    rest = [n for n in FIRST_WEIGHTS if n != "w_in"]
    res = _exchange("grad_exchange", [blocks[n] for n in rest] + [_pack_rows(small_parts, _SMALL_ROWS, F32)],
                    [False] * len(rest) + [True])
    slots.update(zip(rest, res[:-1]))
    small_slots = res[-1]
    sums, chip_slots = _chip_exchange_wait("grad_chip_wait", in_flight["sems"], in_flight["sums"], in_flight["land"],
                                           after=small_slots)
    my_chip = (2 * lax.axis_index("x") + lax.axis_index("y")).astype(jnp.int32).reshape(1)
    slots["w_in"] = _fill_slot("grad_own_slot", chip_slots, sums, my_chip, src_idx=my_chip)

    outs = [dict(), dict(), dict(), dict()]
    for n, _, _ in SHARDED:
        if n == "w_in":
            res = _adamw("adamw_" + n, slots[n], *[jnp.transpose(t, (2, 0, 1)) for t in (weights[n], moms[n], vars_[n])])
            res = [jnp.transpose(t, (1, 2, 0)) for t in res]
        else:
            res = _adamw("adamw_" + n, slots[n], weights[n], moms[n], vars_[n])
        for k in range(4):
            outs[k][n] = res[k]

    sgu_shape = (SGU_G * SGU_C, SGU_C)
    res = _adamw("adamw_sgu_w", late_slots[len(SCAN_CARRIED)], *[t.reshape(sgu_shape) for t in
                                                                  (weights["sgu_w"], moms["sgu_w"], vars_["sgu_w"])])
    for k in range(4):
        outs[k]["sgu_w"] = res[k].reshape(weights["sgu_w"].shape)

    def packed(d):
        return _pack_rows([d[n] for n, _ in PACKED], _SMALL_ROWS, F32)
    small_out = _adamw("adamw_replicated", small_slots, packed(weights), packed(moms), packed(vars_))
    for k in range(4):
        flat = small_out[k].reshape(-1)
        off = 0
        for (n, s), size in zip(PACKED, _SMALL_SIZES):
            outs[k][n] = flat[off:off + size].reshape(s)
            off += size
    loss = small_out[0].reshape(-1)[_LOSS_AT]
    return (loss, dx[None], *[outs[0][n] for n in WEIGHT_ORDER], *[outs[1][n] for n in WEIGHT_ORDER],
            *[outs[2][n] for n in WEIGHT_ORDER], *[outs[3][n] for n in WEIGHT_ORDER])
```

```python
import functools
import numpy as np
import jax
import jax.numpy as jnp
from jax import lax
from jax.experimental import pallas as pl
from jax.experimental.pallas import tpu as pltpu

F32 = jnp.float32
BF16 = jnp.bfloat16

D = 1024
NH, HN = 16, 64
NP, PW = NH // 2, 2 * HN
SGU_G, SGU_C = 8, 128
L_W, L_A, L_G = 64, 64, 160
C_B = 3 * D + L_W + L_A + L_G
P_TOTAL = 2 * D + C_B + 2 * D
D_FF = 4 * D
RW_INT = 3 * D + 128 + 128 + 256
NORM_EPS, LN_EPS, GN_EPS = 1e-6, 1e-5, 64e-5
N_DEV = 8
LANES = 128
SCAN_C = 64
N_KEPT = 4
SOLVE_B = 16
SCAN_PRECISION = lax.Precision.HIGH
SCAN_OUT_PRECISION = lax.Precision.DEFAULT
GRAD_PAYLOAD = BF16
VMEM_LIMIT = 56 * 1024 * 1024
MATMUL_VMEM_BUDGET = 40 * 1024 * 1024
STEP_COST_BYTES = 512 * 1024
HBM_COST_RATIO = 3
ACC_PASS_WEIGHT = 4

ADAM_LR, ADAM_B1, ADAM_B2, ADAM_EPS, ADAM_WD, ADAM_STEP = 0.001, 0.9, 0.999, 1e-08, 0.01, 10

SHARDED = [
    ("w_in", (D, P_TOTAL), 1), ("w_proj_a", (D, D), 0), ("shift_b", (2, C_B), 1), ("w_lora_w", (L_W, D), 1),
    ("a_lora_w", (L_A, D), 1), ("g_lora_w", (L_G, D), 1), ("w_proj_b", (D, D), 0), ("w_out", (D, D), 0),
    ("w_ffn1", (D, D_FF), 1), ("w_ffn2", (D_FF, D), 0),
]
REPLICATED = [
    ("g_mix", (1, D)), ("sgu_ln_w", (1, D)), ("sgu_ln_b", (1, D)), ("sgu_w", (1, SGU_G, SGU_C, SGU_C)),
    ("sgu_b", (1, SGU_G, SGU_C)), ("w0", (1, D)), ("a0", (1, D)), ("k_k", (1, D)), ("k_a", (1, D)), ("r_k", (1, D)),
    ("ln_x_w", (1, D)), ("ln_x_b", (1, D)), ("g_ffn", (1, D)), ("g_final", (D,)),
]
WEIGHT_ORDER = ["g_mix", "w_in", "sgu_ln_w", "sgu_ln_b", "sgu_w", "sgu_b", "w_proj_a", "shift_b", "w_lora_w", "w0",
                "a_lora_w", "a0", "g_lora_w", "k_k", "k_a", "r_k", "ln_x_w", "ln_x_b", "w_proj_b", "w_out", "g_ffn",
                "w_ffn1", "w_ffn2", "g_final"]


def _round_up(n, m):
    return (n + m - 1) // m * m


def _pick(n, target):
    if n <= target:
        return n
    best = None
    for t in range(LANES, target + 1, LANES):
        if n % t == 0:
            best = t
    assert best is not None, (n, target)
    return best


def _matmul(name, a, b, mode, out_dtype=F32, tm=2048, tn=1024, tk=4096, out_blocks=None, epilogue=None, extras=(),
            out_dtypes=(), after=None, whole_rows=False, out_widths=None):
    b_blocks = b.shape[0] if b.ndim == 3 else None
    bshape = b.shape if b.ndim == 2 else (b.shape[1], b.shape[0] * b.shape[2])
    if mode == "nn":
        (M, K), (K2, N) = a.shape, bshape
    elif mode == "nt":
        (M, K), (N, K2) = a.shape, bshape
    else:
        (K, M), (K2, N) = a.shape, bshape
    assert K == K2, (name, a.shape, b.shape)
    assert b_blocks is None or mode != "tn"
    assert out_blocks is None or mode == "tn"
    tn = min(tn, N // (out_blocks or 1), bshape[1] // b_blocks if (b_blocks and mode == "nn") else tn)
    blocked_k = bool(b_blocks) and mode == "nt"
    tm, tn, tk = _pick(M, tm), _pick(N, tn), (K if blocked_k else _pick(K, tk))

    def vmem_bytes(tm, tk):
        tiles = tm * tk * a.dtype.itemsize + tk * tn * b.dtype.itemsize
        for i, dt in enumerate(out_dtypes if epilogue else (out_dtype,)):
            tiles += tm * (out_widths[i] if out_widths else tn) * jnp.dtype(dt).itemsize
        for x in extras:
            arr = x[0] if isinstance(x, tuple) else x
            tiles += (tm if arr.shape[0] > 1 else 1) * tn * arr.dtype.itemsize
        return 2 * tiles + (tm * tn * 4 if K // tk > 1 else 0)

    def cost(tm, tk):
        ni, nj, nk = M // tm, N // tn, K // tk
        steps = ni * nj * nk
        acc_passes = steps * tm * tn * 8 * ACC_PASS_WEIGHT if nk > 1 else 0
        a_reads = M * K * a.dtype.itemsize * (nj if nk > 1 else 1)
        b_reads = K * N * b.dtype.itemsize * (ni if (nj > 1 or nk > 1) else 1)
        return (steps * STEP_COST_BYTES + acc_passes + vmem_bytes(tm, tk) // 2
                + HBM_COST_RATIO * (a_reads + b_reads))

    options = [(m, k) for m in ({M} if whole_rows else {_pick(M, max(t, LANES)) for t in (tm, tm // 2, tm // 4)})
               for k in ({K} if blocked_k else {_pick(K, max(t, LANES)) for t in (tk, tk // 2, tk // 4)})
               if vmem_bytes(m, k) <= MATMUL_VMEM_BUDGET]
    tm, tk = min(options, key=lambda o: cost(*o))
    nk = K // tk
    dims = {"nn": (((1,), (0,)), ((), ())), "nt": (((1,), (1,)), ((), ())), "tn": (((0,), (0,)), ((), ()))}[mode]

    n_x, n_o = len(extras), len(out_dtypes) if epilogue else 1
    n_after = 0 if after is None else 1

    def body(a_ref, b_ref, *rest):
        x_refs, o_refs, acc = rest[:n_x], rest[n_x + n_after:n_x + n_after + n_o], rest[n_x + n_after + n_o:]
        if blocked_k:
            bw = b.shape[2]
            part = sum(lax.dot_general(a_ref[:, blk * bw:(blk + 1) * bw].astype(BF16), b_ref[blk].astype(BF16), dims,
                                       preferred_element_type=F32) for blk in range(b_blocks))
        else:
            part = lax.dot_general(a_ref[...].astype(BF16), b_ref[...].astype(BF16), dims, preferred_element_type=F32)

        def finish(res):
            outs = epilogue(res, *[r[...] for r in x_refs]) if epilogue else (res,)
            for r, v in zip(o_refs, outs):
                r[...] = v.astype(r.dtype)

        if nk == 1:
            finish(part)
            return
        acc_ref, k = acc[0], pl.program_id(2)

        @pl.when(k == 0)
        def _():
            acc_ref[...] = part

        @pl.when(k > 0)
        def _():
            acc_ref[...] += part

        @pl.when(k == nk - 1)
        def _():
            finish(acc_ref[...])

    a_spec = {"nn": pl.BlockSpec((tm, tk), lambda i, j, k: (i, k)), "nt": pl.BlockSpec((tm, tk), lambda i, j, k: (i, k)),
              "tn": pl.BlockSpec((tk, tm), lambda i, j, k: (k, i))}[mode]
    b_spec = {"nn": pl.BlockSpec((tk, tn), lambda i, j, k: (k, j)), "nt": pl.BlockSpec((tn, tk), lambda i, j, k: (j, k)),
              "tn": pl.BlockSpec((tk, tn), lambda i, j, k: (k, j))}[mode]
    if b_blocks and mode == "nn":
        per = b.shape[2] // tn
        b_spec = pl.BlockSpec((None, tk, tn), lambda i, j, k: (j // per, k, j % per))
    elif b_blocks:
        b_spec = pl.BlockSpec((b_blocks, tn, b.shape[2]), lambda i, j, k: (0, j, 0))
    out_spec = pl.BlockSpec((tm, tn), lambda i, j, k: (i, j))
    out_shape = jax.ShapeDtypeStruct((M, N), out_dtype)
    if out_blocks:
        per_o = N // out_blocks // tn
        out_spec = pl.BlockSpec((None, tm, tn), lambda i, j, k: (j // per_o, i, j % per_o))
        out_shape = jax.ShapeDtypeStruct((out_blocks, M, N // out_blocks), out_dtype)
    epi_widths = list(out_widths) if out_widths else [N] * n_o
    assert all(w == N for w in epi_widths) or N == tn
    epi_specs = [pl.BlockSpec((tm, tn if w == N else w), lambda i, j, k: (i, j)) for w in epi_widths]
    x_specs, x_args = [], []
    for x in extras:
        arr, off = x if isinstance(x, tuple) else (x, 0)
        if arr.shape[0] == 1:
            x_specs.append(pl.BlockSpec((1, tn), lambda i, j, k: (0, j)))
        else:
            x_specs.append(pl.BlockSpec((tm, tn), lambda i, j, k, off=off: (i, j + off)))
        x_args.append(arr)
    res = pl.pallas_call(
        body, name=name, grid=(M // tm, N // tn, nk),
        in_specs=[a_spec, b_spec] + x_specs + [pl.BlockSpec(memory_space=pl.ANY)] * n_after,
        out_specs=epi_specs if epilogue else out_spec,
        out_shape=[jax.ShapeDtypeStruct((M, w), dt) for w, dt in zip(epi_widths, out_dtypes)] if epilogue else out_shape,
        scratch_shapes=[pltpu.VMEM((tm, tn), F32)] if nk > 1 else [],
        compiler_params=pltpu.CompilerParams(dimension_semantics=("parallel", "parallel", "arbitrary"),
                                             vmem_limit_bytes=VMEM_LIMIT),
    )(a, b, *x_args, *([after] if n_after else []))
    return res


class Rows:
    def __init__(self, arr, width=None, cb=0):
        self.arr, self.width, self.cb = arr, (arr.shape[1] if width is None else width), cb


class Heads:
    def __init__(self, arr):
        self.arr = arr


class Halo:
    def __init__(self, arr, side):
        self.arr, self.side = arr, side


def _rows_call(name, fn, ins, consts, outs, accs=(), tm=512, with_pid=False):
    T = next(o.arr.shape[1] if isinstance(o, Heads) else o.arr.shape[0] for o in ins if not isinstance(o, Halo))
    tm = min(tm, T)
    n_tiles = T // tm
    n_in, n_c, n_out = len(ins), len(consts), len(outs)
    in_specs, args = [], []
    for o in ins:
        if isinstance(o, Rows):
            in_specs.append(pl.BlockSpec((tm, o.width), lambda i, cb=o.cb: (i, cb)))
        elif isinstance(o, Heads):
            in_specs.append(pl.BlockSpec((NP, tm, PW), lambda i: (0, i, 0)))
        else:
            w = o.arr.shape[1]
            if o.side < 0:
                in_specs.append(pl.BlockSpec((8, w), lambda i: (jnp.maximum(i * (tm // 8) - 1, 0), 0)))
            else:
                in_specs.append(pl.BlockSpec((8, w), lambda i: (jnp.minimum((i + 1) * (tm // 8), T // 8 - 1), 0)))
        args.append(o.arr)
    for c in consts:
        in_specs.append(pl.BlockSpec(c.shape, lambda i, nd=c.ndim: (0,) * nd))
        args.append(c)
    out_specs, out_shape = [], []
    for o in outs:
        if o[0] == "rows":
            out_specs.append(pl.BlockSpec((tm, o[1]), lambda i: (i, 0)))
            out_shape.append(jax.ShapeDtypeStruct((T, o[1]), o[2]))
        else:
            out_specs.append(pl.BlockSpec((NP, tm, PW), lambda i: (0, i, 0)))
            out_shape.append(jax.ShapeDtypeStruct((NP, T, PW), o[1]))
    for shape, dt in accs:
        out_specs.append(pl.BlockSpec(shape, lambda i, nd=len(shape): (0,) * nd))
        out_shape.append(jax.ShapeDtypeStruct(shape, dt))

    def body(*refs):
        i = pl.program_id(0)
        vals = []
        vals = [r[...] for r in refs[:n_in + n_c]]
        res = fn(i, n_tiles, *vals) if with_pid else fn(*vals)
        out_refs = refs[n_in + n_c:]
        for r, v in zip(out_refs[:n_out], res[:n_out]):
            r[...] = v.astype(r.dtype)
        if accs:
            @pl.when(i == 0)
            def _():
                for r in out_refs[n_out:]:
                    r[...] = jnp.zeros_like(r)

            for r, v in zip(out_refs[n_out:], res[n_out:]):
                r[...] += v.astype(r.dtype)

    res = pl.pallas_call(
        body, name=name, grid=(n_tiles,), in_specs=in_specs, out_specs=out_specs, out_shape=out_shape,
        compiler_params=pltpu.CompilerParams(dimension_semantics=("arbitrary",), vmem_limit_bytes=VMEM_LIMIT),
    )(*args)
    return res


def _rms(x, g):
    return x * lax.rsqrt(jnp.mean(x * x, axis=-1, keepdims=True) + NORM_EPS) * g


def _gelu(x):
    return 0.5 * x * (1.0 + lax.erf(x * 0.7071067811865476))


def _sigmoid(x):
    return 1.0 / (1.0 + jnp.exp(-x))


def _bdot(a, b):
    return jnp.dot(a.astype(BF16), b.astype(BF16), preferred_element_type=F32)


def _to_heads(x):
    return jnp.concatenate([x[:, p * PW:(p + 1) * PW][None] for p in range(NP)], axis=0)


def _from_heads(xp):
    return jnp.concatenate([xp[p] for p in range(NP)], axis=-1)


def _head_sum(xp):
    low = lax.broadcasted_iota(jnp.int32, xp.shape, xp.ndim - 1) < HN
    both = jnp.sum(xp, axis=-1, keepdims=True)
    first = jnp.sum(jnp.where(low, xp, 0.0), axis=-1, keepdims=True)
    return jnp.where(low, first, both - first)


def _split_pairs(xp):
    return jnp.concatenate([xp[:, :, :HN], xp[:, :, HN:]], axis=0)


def _join_pairs(xh):
    return jnp.concatenate([xh[:NP], xh[NP:]], axis=-1)


def _sgu_fn(p, ln_w, ln_b, sw, sbt):
    z = _gelu(p)
    u, v = z[:, :D], z[:, D:]
    mu = jnp.mean(v, axis=-1, keepdims=True)
    var = jnp.mean(jnp.square(v - mu), axis=-1, keepdims=True)
    vn = (v - mu) * lax.rsqrt(var + LN_EPS) * ln_w + ln_b
    ri = lax.broadcasted_iota(jnp.int32, (SGU_C, SGU_C), 0)
    ci = lax.broadcasted_iota(jnp.int32, (SGU_C, SGU_C), 1)
    mask = (ci <= ri).astype(F32)
    dg = D // SGU_G
    parts = []
    for g in range(SGU_G):
        parts.append(_bdot(sw[g] * mask, vn[:, g * dg:(g + 1) * dg]) + sbt[:, g:g + 1])
    return u * jnp.concatenate(parts, axis=-1)


def _pre_fn(qr, qk, qv, qxw, qxa, qxg, wl, w0, al, a0, gl, k_k, k_a):
    w = -jax.nn.softplus(-(w0 + _bdot(jnp.tanh(qxw), wl))) - 0.5
    lw = -jnp.exp(w)
    aa = _sigmoid(a0 + _bdot(qxa, al))
    g = _bdot(_sigmoid(qxg), gl)
    kk = _to_heads(qk * k_k)
    kk = kk / jnp.maximum(jnp.sqrt(_head_sum(kk * kk)), 1e-12)
    k2 = qk * (1.0 + (aa - 1.0) * k_a)
    return _to_heads(qr), _to_heads(lw), _to_heads(k2), _to_heads(qv), kk, _to_heads(aa), g


def _post_fn(o, r, k2, v, g, ln_w, ln_b, r_k):
    mu = _head_sum(o) * (1.0 / HN)
    d = o - mu
    var = _head_sum(d * d) * (1.0 / HN)
    on = d * lax.rsqrt(var + GN_EPS) * ln_w + ln_b
    bonus = _head_sum(r * k2 * r_k) * v
    return _from_heads(on + bonus) * g


def _gate_fn(pg, ya, yb):
    return _sigmoid(pg[:, :D]) * ya + _sigmoid(pg[:, D:]) * yb


def _bmm(x, y, cx, cy, out_path=False):
    return lax.dot_general(x, y, (((cx,), (cy,)), ((0,), (0,))),
                           precision=SCAN_OUT_PRECISION if out_path else SCAN_PRECISION, preferred_element_type=F32)


def _unit_lower_inverse(M):
    C = M.shape[1]
    ti = lax.broadcasted_iota(jnp.int32, (C, C), 0)
    tj = lax.broadcasted_iota(jnp.int32, (C, C), 1)
    eye = (ti == tj).astype(F32)
    same = lambda b: (ti // b == tj // b).astype(F32)
    X = -(M * same(SOLVE_B))
    inv = eye + X
    span = 1
    while 2 * span < SOLVE_B:
        X = _bmm(X, X, 2, 1)
        inv = inv + _bmm(inv, X, 2, 1)
        span *= 2
    b = SOLVE_B
    while b < C:
        low = M * (same(2 * b) - same(b))
        inv = inv - _bmm(_bmm(inv, low, 2, 1, out_path=True), inv, 2, 1, out_path=True)
        b *= 2
    return inv


@jax.custom_vjp
def _unit_lower_solve(inv, M, y):
    return _bmm(inv, y, 2, 1)


def _unit_lower_solve_fwd(inv, M, y):
    u = _bmm(inv, y, 2, 1)
    return u, (inv, u)


def _unit_lower_solve_bwd(res, du):
    inv, u = res
    dy = _bmm(inv, du, 1, 1)
    return jnp.zeros_like(inv), -_bmm(dy, u, 2, 2), dy


_unit_lower_solve.defvjp(_unit_lower_solve_fwd, _unit_lower_solve_bwd)


@jax.custom_vjp
def _unit_lower_solved(inv, u, M, y):
    return u


_unit_lower_solved.defvjp(lambda inv, u, M, y: (u, (inv, u)),
                          lambda res, du: (jnp.zeros_like(res[0]), jnp.zeros_like(res[1]))
                          + _unit_lower_solve_bwd(res, du)[1:])


@functools.partial(jax.custom_vjp, nondiff_argnums=(0,))
def _kept(fn, value, *args):
    return value


def _kept_fwd(fn, value, *args):
    return value, args


def _kept_bwd(fn, args, d):
    _, vjp = jax.vjp(fn, *args)
    return (jnp.zeros_like(d),) + tuple(vjp(d))


_kept.defvjp(_kept_fwd, _kept_bwd)


def _sum_over_time(x, reverse):
    C = x.shape[1]
    ti = lax.broadcasted_iota(jnp.int32, (C, C), 0)
    tj = lax.broadcasted_iota(jnp.int32, (C, C), 1)
    ones = jnp.broadcast_to(((tj >= ti) if reverse else (tj <= ti)).astype(BF16), (x.shape[0], C, C))
    hi = x.astype(BF16)
    r1 = x - hi.astype(F32)
    mid = r1.astype(BF16)
    lo = (r1 - mid.astype(F32)).astype(BF16)
    dn = (((2,), (1,)), ((0,), (0,)))
    return sum(lax.dot_general(ones, p, dn, preferred_element_type=F32) for p in (lo, mid, hi))


@jax.custom_vjp
def _time_cumsum(lw):
    return _sum_over_time(lw, reverse=False)


_time_cumsum.defvjp(lambda lw: (_sum_over_time(lw, reverse=False), None),
                    lambda _, d: (_sum_over_time(d, reverse=True),))


def _chunk_fn(S0, r, lw, k, v, kk, a, kept=None):
    C = SCAN_C
    bmm = _bmm
    ti = lax.broadcasted_iota(jnp.int32, (C, C), 0)
    tj = lax.broadcasted_iota(jnp.int32, (C, C), 1)
    incl2 = jnp.concatenate([(tj <= ti).astype(F32)] * 2, axis=1)
    strict = (tj < ti).astype(F32)
    n_mask = jnp.concatenate([jnp.zeros((C, C), F32), strict], axis=1)

    def known(name, fn, *args):
        return fn(*args) if kept is None else _kept(fn, kept[name], *args)

    cum = known("cum", _time_cumsum, lw)
    g_in, g_ex, g_inv = jnp.exp(cum), jnp.exp(cum - lw), jnp.exp(-cum)
    kkt, rt = kk * g_ex, r * g_in
    bk = jnp.concatenate([kk * a * g_inv, k * g_inv], axis=1)
    kr = jnp.concatenate([kkt, rt], axis=1)
    ratios = known("ratios", lambda x, y: bmm(x, y, 2, 2), kr, bk)
    A = ratios[:, :C]
    M = A[:, :, :C] * strict
    zv = jnp.concatenate([jnp.zeros_like(v), v], axis=1)
    s0_side = bmm(kr, S0, 2, 2, out_path=True)
    rhs = s0_side[:, :C] + bmm(A * n_mask, zv, 2, 1, out_path=True)
    if kept is None:
        inv = lax.stop_gradient(_unit_lower_inverse(M))
        y = _unit_lower_solve(inv, M, rhs)
    else:
        inv = kept["inv"]
        y = _unit_lower_solved(inv, kept["y"], M, rhs)
    z = jnp.concatenate([-y, v], axis=1)
    O = s0_side[:, C:] + bmm(ratios[:, C:] * incl2, z, 2, 1, out_path=True)
    g_end = g_in[:, C - 1:C, :]
    S1 = S0 * g_end + bmm(z, bk * g_end, 1, 1, out_path=True)
    return O, S1, dict(cum=cum, ratios=ratios, y=y, inv=inv)


def _scan_fwd(r, lw, k, v, kk, a, ex=None, after=None, tb=256):
    assert SCAN_C == HN and 2 * SCAN_C == PW
    T = r.shape[1]
    tb = min(tb, T)
    n_chunks = tb // SCAN_C
    nb = T // tb
    nx = ex.nb if ex else 0
    unread = [] if after is None else [after]

    def body(*refs):
        r_ref, lw_ref, k_ref, v_ref, kk_ref, a_ref = refs[:6]
        x_in, refs = refs[6:6 + nx], refs[6 + nx + len(unread):]
        (o_ref, s0_ref), x_out, s_ref, sems = refs[:2], refs[2:2 + nx], refs[2 + nx], refs[3 + nx:]

        plan = ex.schedule(nb) if ex else []

        @pl.when(pl.program_id(0) == 0)
        def _():
            s_ref[...] = jnp.zeros_like(s_ref)
            for at, action in plan[:1]:
                action(x_in, x_out, sems)

        def step(c, carry):
            sl = pl.ds(pl.multiple_of(c * SCAN_C, SCAN_C), SCAN_C)
            S0 = s_ref[...]
            O, S1, keep = _chunk_fn(S0, *[_split_pairs(ref[:, sl, :])
                                          for ref in (r_ref, lw_ref, k_ref, v_ref, kk_ref, a_ref)])
            o_ref[:, sl, :] = _join_pairs(O)
            s0_ref[c, 0] = jnp.concatenate([S0, keep["inv"]], axis=-1)
            s0_ref[c, 1] = jnp.concatenate([keep["cum"], keep["y"]], axis=-1)
            s0_ref[c, 2] = keep["ratios"][:, :SCAN_C]
            s0_ref[c, 3] = keep["ratios"][:, SCAN_C:]
            s_ref[...] = S1
            return carry

        lax.fori_loop(0, n_chunks, step, 0)

        for at, action in plan[1:]:
            pl.when(pl.program_id(0) == at)(functools.partial(action, x_in, x_out, sems))

    hm = pl.BlockSpec((NP, tb, PW), lambda i: (0, i, 0))
    res = pl.pallas_call(
        body, name="rwkv_scan_fwd", grid=(nb,),
        in_specs=[hm] * 6 + (ex.any_specs if ex else []) + [pl.BlockSpec(memory_space=pl.ANY)] * len(unread),
        out_specs=[hm, pl.BlockSpec((n_chunks, N_KEPT, NH, HN, PW), lambda i: (i, 0, 0, 0, 0))]
        + (ex.any_specs if ex else []),
        out_shape=[jax.ShapeDtypeStruct((NP, T, PW), F32),
                   jax.ShapeDtypeStruct((T // SCAN_C, N_KEPT, NH, HN, PW), F32)]
        + (ex.out_shape if ex else []),
        scratch_shapes=[pltpu.VMEM((NH, HN, HN), F32)] + (ex.sem_shapes if ex else []),
        compiler_params=pltpu.CompilerParams(dimension_semantics=("arbitrary",), vmem_limit_bytes=VMEM_LIMIT),
    )(r, lw, k, v, kk, a, *(ex.bufs if ex else []), *unread)
    return res[0], res[1], list(res[2:])


def _scan_bwd(r, lw, k, v, kk, a, s0s, do, ex=None, tb=128):
    T = r.shape[1]
    tb = min(tb, T)
    n_chunks = tb // SCAN_C
    nb = T // tb
    nx = ex.nb if ex else 0

    def body(*refs):
        r_ref, lw_ref, k_ref, v_ref, kk_ref, a_ref, s0_ref, do_ref = refs[:8]
        x_in, (dr, dlw, dk, dv, dkk, da), x_out = refs[8:8 + nx], refs[8 + nx:14 + nx], refs[14 + nx:14 + 2 * nx]
        ds_ref, sems = refs[14 + 2 * nx], refs[15 + 2 * nx:]

        plan = ex.schedule(nb) if ex else []

        @pl.when(pl.program_id(0) == 0)
        def _():
            ds_ref[...] = jnp.zeros_like(ds_ref)
            for at, action in plan[:1]:
                action(x_in, x_out, sems)

        def step(j, carry):
            c = n_chunks - 1 - j
            sl = pl.ds(pl.multiple_of(c * SCAN_C, SCAN_C), SCAN_C)
            s0_inv, cum_y = s0_ref[c, 0], s0_ref[c, 1]
            kept = dict(inv=s0_inv[:, :, HN:], cum=cum_y[:, :, :HN], y=cum_y[:, :, HN:],
                        ratios=jnp.concatenate([s0_ref[c, 2], s0_ref[c, 3]], axis=1))
            _, vjp = jax.vjp(lambda *t: _chunk_fn(*t, kept=kept)[:2], s0_inv[:, :, :HN],
                             *[_split_pairs(ref[:, sl, :]) for ref in (r_ref, lw_ref, k_ref, v_ref, kk_ref, a_ref)])
            g = vjp((_split_pairs(do_ref[:, sl, :]), ds_ref[...]))
            ds_ref[...] = g[0]
            for ref, val in zip((dr, dlw, dk, dv, dkk, da), g[1:]):
                ref[:, sl, :] = _join_pairs(val)
            return carry

        lax.fori_loop(0, n_chunks, step, 0)

        for at, action in plan[1:]:
            pl.when(pl.program_id(0) == at)(functools.partial(action, x_in, x_out, sems))

    hm = pl.BlockSpec((NP, tb, PW), lambda i: (0, nb - 1 - i, 0))
    res = pl.pallas_call(
        body, name="rwkv_scan_bwd", grid=(nb,),
        in_specs=[hm] * 6 + [pl.BlockSpec((n_chunks, N_KEPT, NH, HN, PW), lambda i: (nb - 1 - i, 0, 0, 0, 0)), hm]
        + (ex.any_specs if ex else []),
        out_specs=[hm] * 6 + (ex.any_specs if ex else []),
        out_shape=[jax.ShapeDtypeStruct((NP, T, PW), F32)] * 6 + (ex.out_shape if ex else []),
        scratch_shapes=[pltpu.VMEM((NH, HN, HN), F32)] + (ex.sem_shapes if ex else []),
        compiler_params=pltpu.CompilerParams(dimension_semantics=("arbitrary",), vmem_limit_bytes=VMEM_LIMIT),
    )(r, lw, k, v, kk, a, s0s, do, *(ex.bufs if ex else []))
    return list(res[:6]), list(res[6:])


def _shift_down(i, p, prev8):
    first = jnp.where(i > 0, prev8[7:8, :], 0.0)
    row = lax.broadcasted_iota(jnp.int32, p.shape, 0)
    return jnp.where(row == 0, first, pltpu.roll(p, 1, axis=0))


def _mix_bwd(dq, p, sb, tm=256):
    def fn(i, n, dq, next8, p, prev8, sb):
        ps = _shift_down(i, p, prev8)
        d1 = dq * sb[1:2]
        last = jnp.where(i < n - 1, next8[0:1, :] * sb[1:2], 0.0)
        row = lax.broadcasted_iota(jnp.int32, dq.shape, 0)
        up = jnp.where(row == dq.shape[0] - 1, last, pltpu.roll(d1, dq.shape[0] - 1, axis=0))
        return (dq * sb[0:1] + up, jnp.sum(dq * p, axis=0, keepdims=True), jnp.sum(dq * ps, axis=0, keepdims=True))
    w = p.shape[1]
    return _rows_call("shift_mix_bwd", fn, [Rows(dq), Halo(dq, +1), Rows(p), Halo(p, -1)], [sb], [("rows", w, BF16)],
                      accs=[((1, w), F32), ((1, w), F32)], tm=tm, with_pid=True)


def _local_step(x, target, W, late_weights=None, early_grads=None, w_in_grads_ready=None, ffn_grads_ready=None):
    G = {}
    a = _rows_call("norm_mix_fwd", lambda x, g: (_rms(x, g),), [Rows(x)], [W["g_mix"]], [("rows", D, BF16)])[0]
    p_sgu = _matmul("proj_sgu", a, W["w_sgu_t"], "nt", after=W.get("ffn_weights_sent"))
    def token_shift(p, sb0, sb1):
        row = lax.broadcasted_iota(jnp.int32, p.shape, 0)
        return p, p * sb0 + jnp.where(row == 0, 0.0, pltpu.roll(p, 1, axis=0)) * sb1
    p_rw, q = _matmul("proj_rwkv", a, W["w_rw_t"], "nt", tn=512, whole_rows=True, epilogue=token_shift,
                      extras=[W["sb"][0:1], W["sb"][1:2]], out_dtypes=(F32, F32))
    p_gate = _matmul("proj_gate", a, W["w_gate_t"], "nt")

    sgu_consts = [W["sgu_ln_w"], W["sgu_ln_b"], W["sgu_w"], W["sgu_bt"]]
    s = _rows_call("sgu_fwd", lambda *t: (_sgu_fn(*t),), [Rows(p_sgu)], sgu_consts, [("rows", D, BF16)], tm=SGU_C)[0]

    q_ins = [Rows(q, D, 0), Rows(q, D, 1), Rows(q, D, 2), Rows(q, 128, 24), Rows(q, 128, 25), Rows(q, 256, 13)]
    pre_consts = [W["w_lora"], W["w0"], W["a_lora"], W["a0"], W["g_lora"], W["k_k"], W["k_a"]]
    r_h, lw_h, k_h, v_h, kk_h, a_h, g_gate = _rows_call(
        "rwkv_pre_fwd", _pre_fn, q_ins, pre_consts, [("heads", F32)] * 6 + [("rows", D, F32)], tm=128)
    if late_weights:
        ex, before_scan, finish = late_weights
        o_h, s0s, got = _scan_fwd(r_h, lw_h, k_h, v_h, kk_h, a_h, ex=ex, after=before_scan(r_h))
        W = {**W, **finish(got, o_h)}
    else:
        o_h, s0s, _ = _scan_fwd(r_h, lw_h, k_h, v_h, kk_h, a_h)
    y_a = _matmul("proj_a", s, W["w_proj_a"], "nn")
    post_ins = [Heads(o_h), Heads(r_h), Heads(k_h), Heads(v_h), Rows(g_gate)]
    post_consts = [W[n].reshape(NP, 1, PW) for n in ("ln_x_w", "ln_x_b", "r_k")]
    z_b = _rows_call("rwkv_post_fwd", lambda *t: (_post_fn(*t),), post_ins, post_consts, [("rows", D, BF16)], tm=128)[0]
    y_b, mixed = _matmul("proj_b", z_b, W["w_proj_b"], "nn", extras=[(p_gate, 0), (p_gate, 1), y_a],
                         epilogue=lambda yb, ga, gb, ya: (yb, _sigmoid(ga) * ya + _sigmoid(gb) * yb),
                         out_dtypes=(F32, BF16))

    def res1(mo, x, g):
        h1 = x + mo
        return h1, _rms(h1, g)
    h1, f = _matmul("proj_out", mixed, W["w_out"], "nn", extras=[x, W["g_ffn"]], epilogue=res1,
                    out_dtypes=(F32, BF16))

    def relu_sq(u):
        r = jnp.maximum(u, 0.0)
        return r, r * r
    r1, act = _matmul("ffn_up", f, W["w_ffn1"], "nn", epilogue=relu_sq, out_dtypes=(BF16, BF16))
    ff = _matmul("ffn_down", act, W["w_ffn2"], "nn")

    def head(h1, ff, tgt, g):
        def f_(h1, ff, g):
            y = _rms(h1 + ff, g)
            return 0.5 * jnp.sum(jnp.mean(jnp.square(y - tgt), axis=-1))
        loss, (dh2, _, dg) = jax.value_and_grad(f_, argnums=(0, 1, 2))(h1, ff, g)
        return dh2, jnp.full((8, LANES), loss, F32), dg
    dh2, loss_acc, G["g_final"] = _rows_call("loss_head", head, [Rows(h1), Rows(ff), Rows(target)], [W["g_final"]],
                                             [("rows", D, F32)], accs=[((8, LANES), F32), ((1, D), F32)])

    d_u1 = _matmul("ffn_down_dx", dh2, W["w_ffn2"], "nt", extras=[r1], out_dtypes=(BF16,),
                   epilogue=lambda d_act, r: (d_act * 2.0 * r.astype(F32),))[0]
    G["w_ffn2"] = _matmul("ffn_down_dw", act, dh2, "tn", out_dtype=GRAD_PAYLOAD)
    d_f = _matmul("ffn_up_dx", d_u1, W["w_ffn1"], "nt")
    G["w_ffn1"] = _matmul("ffn_up_dw", f, d_u1, "tn", out_blocks=N_DEV, out_dtype=GRAD_PAYLOAD)

    def res1_bwd(h1, d_f, dh2, g):
        _, vjp = jax.vjp(_rms, h1, g)
        dh, dg = vjp(d_f)
        return dh2 + dh, dg
    dh1, G["g_ffn"] = _rows_call("residual_norm_bwd", res1_bwd, [Rows(h1), Rows(d_f), Rows(dh2)],
                                 [W["g_ffn"]], [("rows", D, F32)], accs=[((1, D), F32)])
    ffn_token = ffn_grads_ready(G) if ffn_grads_ready else None
    def gate_bwd(d_mixed, ga, gb, ya, yb):
        _, vjp = jax.vjp(_gate_fn, jnp.concatenate([ga, gb], axis=-1), ya, yb)
        return vjp(d_mixed)
    d_gate, d_ya, d_yb = _matmul("proj_out_dx", dh1, W["w_out"], "nt", after=ffn_token, epilogue=gate_bwd,
                                 extras=[(p_gate, 0), (p_gate, 1), y_a, y_b], out_dtypes=(BF16, BF16, BF16),
                                 out_widths=(2 * D, D, D))
    G["w_out"] = _matmul("proj_out_dw", mixed, dh1, "tn", out_dtype=GRAD_PAYLOAD)

    d_s = _matmul("proj_a_dx", d_ya, W["w_proj_a"], "nt")
    G["w_proj_a"] = _matmul("proj_a_dw", s, d_ya, "tn", out_dtype=GRAD_PAYLOAD)

    def sgu_bwd(p, ds, *c):
        _, vjp = jax.vjp(_sgu_fn, p, *c)
        return vjp(ds)
    d_p_sgu, G["sgu_ln_w"], G["sgu_ln_b"], G["sgu_w"], G["sgu_bt"] = _rows_call(
        "sgu_bwd", sgu_bwd, [Rows(p_sgu), Rows(d_s)], sgu_consts, [("rows", 2 * D, BF16)],
        accs=[((1, D), F32), ((1, D), F32), ((SGU_G, SGU_C, SGU_C), F32), ((SGU_C, SGU_G), F32)], tm=SGU_C)

    d_zb = _matmul("proj_b_dx", d_yb, W["w_proj_b"], "nt")
    G["w_proj_b"] = _matmul("proj_b_dw", z_b, d_yb, "tn", out_dtype=GRAD_PAYLOAD)

    def post_bwd(o, r, k2, v, g, dz, *c):
        _, vjp = jax.vjp(_post_fn, o, r, k2, v, g, *c)
        return vjp(dz)
    do_h, dr1, dk1, dv1, d_g, g_lnw, g_lnb, g_rk = _rows_call(
        "rwkv_post_bwd", post_bwd, post_ins + [Rows(d_zb)], post_consts, [("heads", F32)] * 4 + [("rows", D, F32)],
        accs=[((NP, 1, PW), F32)] * 3, tm=128)
    G["ln_x_w"], G["ln_x_b"], G["r_k"] = (t.reshape(1, D) for t in (g_lnw, g_lnb, g_rk))
    (dr2, dlw, dk2, dv2, dkk, daa), early = _scan_bwd(r_h, lw_h, k_h, v_h, kk_h, a_h, s0s, do_h,
                                                      ex=early_grads(G) if early_grads else None)

    def pre_bwd(qr, qk, qv, qxw, qxa, qxg, dr1, dr2, dlw, dk1, dk2, dv1, dv2, dkk, daa, dg, *c):
        _, vjp = jax.vjp(_pre_fn, qr, qk, qv, qxw, qxa, qxg, *c)
        g = vjp((dr1 + dr2, dlw, dk1 + dk2, dv1 + dv2, dkk, daa, dg))
        dq = jnp.concatenate(g[:6], axis=-1)
        return (dq,) + tuple(g[6:])
    pre_b_ins = q_ins + [Heads(dr1), Heads(dr2), Heads(dlw), Heads(dk1), Heads(dk2), Heads(dv1), Heads(dv2),
                         Heads(dkk), Heads(daa), Rows(d_g)]
    d_q, G["w_lora"], G["w0"], G["a_lora"], G["a0"], G["g_lora"], G["k_k"], G["k_a"] = _rows_call(
        "rwkv_pre_bwd", pre_bwd, pre_b_ins, pre_consts, [("rows", RW_INT, F32)],
        accs=[((128, D), F32), ((1, D), F32), ((128, D), F32), ((1, D), F32), ((256, D), F32), ((1, D), F32),
              ((1, D), F32)], tm=128)
    d_p_rw, dsb0, dsb1 = _mix_bwd(d_q, p_rw, W["sb"])
    G["sb"] = jnp.concatenate([dsb0, dsb1], axis=0)

    G["w_sgu_t"] = _matmul("proj_sgu_dw", d_p_sgu, a, "tn", out_dtype=GRAD_PAYLOAD)
    G["w_rw_t"] = _matmul("proj_rwkv_dw", d_p_rw, a, "tn", out_dtype=GRAD_PAYLOAD)
    G["w_gate_t"] = _matmul("proj_gate_dw", d_gate, a, "tn", out_dtype=GRAD_PAYLOAD)
    token = w_in_grads_ready(G) if w_in_grads_ready else None
    da1 = _matmul("proj_sgu_dx", d_p_sgu, W["w_sgu_t"], "nn", after=token)
    da2 = _matmul("proj_rwkv_dx", d_p_rw, W["w_rw_t"], "nn", after=token)
    da3 = _matmul("proj_gate_dx", d_gate, W["w_gate_t"], "nn", after=token)

    def norm1_bwd(x, da1, da2, da3, dh1, g):
        _, vjp = jax.vjp(_rms, x, g)
        dx, dg = vjp(da1 + da2 + da3)
        return dh1 + dx, dg
    dx, G["g_mix"] = _rows_call("norm_mix_bwd", norm1_bwd, [Rows(x), Rows(da1), Rows(da2), Rows(da3), Rows(dh1)],
                                [W["g_mix"]], [("rows", D, F32)], accs=[((1, D), F32)])
    return loss_acc[0, 0], dx, G, early


class Exchange:
    def __init__(self, bufs, gathers):
        self.bufs, self.gathers, self.nb = list(bufs), list(gathers), len(bufs)
        self.any_specs = [pl.BlockSpec(memory_space=pl.ANY)] * self.nb
        self.out_shape = [jax.ShapeDtypeStruct((N_DEV,) + (b.shape if g else b.shape[1:]), b.dtype)
                          for b, g in zip(self.bufs, self.gathers)]
        n = (N_DEV - 1) * self.nb
        self.sem_shapes = [pltpu.SemaphoreType.DMA((n,)), pltpu.SemaphoreType.DMA((n,)),
                           pltpu.SemaphoreType.DMA((self.nb,))]

    def _copies(self, in_refs, out_refs, sems):
        send_sems, recv_sems, local_sems = sems
        x, y, c = lax.axis_index("x"), lax.axis_index("y"), lax.axis_index("c")
        me = 4 * x + 2 * y + c

        def src(b, dest):
            return in_refs[b] if self.gathers[b] else in_refs[b].at[dest]

        local = [pltpu.make_async_copy(src(b, me), out_refs[b].at[me], local_sems.at[b]) for b in range(self.nb)]
        sends, recvs = [], []
        for kbits in range(1, N_DEV):
            px = 1 - x if kbits & 4 else x
            py = 1 - y if kbits & 2 else y
            pc = 1 - c if kbits & 1 else c
            peer = 4 * px + 2 * py + pc
            for b in range(self.nb):
                s = (kbits - 1) * self.nb + b
                sends.append(pltpu.make_async_remote_copy(
                    src_ref=src(b, peer), dst_ref=out_refs[b].at[me], send_sem=send_sems.at[s],
                    recv_sem=recv_sems.at[s], device_id=(px, py, pc), device_id_type=pl.DeviceIdType.MESH))
                recvs.append(pltpu.make_async_remote_copy(
                    src_ref=src(b, peer), dst_ref=out_refs[b].at[peer], send_sem=send_sems.at[s],
                    recv_sem=recv_sems.at[s], device_id=(px, py, pc), device_id_type=pl.DeviceIdType.MESH))
        return local, sends, recvs

    def start(self, in_refs, out_refs, sems):
        local, sends, _ = self._copies(in_refs, out_refs, sems)
        for cp in sends + local:
            cp.start()

    def wait(self, in_refs, out_refs, sems):
        local, sends, recvs = self._copies(in_refs, out_refs, sems)
        for cp in recvs:
            cp.wait_recv()
        for cp in sends:
            cp.wait_send()
        for cp in local:
            cp.wait()

    def schedule(self, n_steps):
        return [(0, self.start), (n_steps - 1, self.wait)]


def _exchange(name, bufs, gather):
    ex = Exchange(bufs, gather if isinstance(gather, (list, tuple)) else [gather] * len(bufs))

    def body(*refs):
        in_refs, out_refs, sems = refs[:ex.nb], refs[ex.nb:2 * ex.nb], refs[2 * ex.nb:]
        ex.start(in_refs, out_refs, sems)
        ex.wait(in_refs, out_refs, sems)

    return pl.pallas_call(body, name=name, in_specs=ex.any_specs, out_specs=ex.any_specs, out_shape=ex.out_shape,
                          scratch_shapes=ex.sem_shapes)(*ex.bufs)


N_CHIP = 4


def _pair_exchange(name, blocks):
    def body(b_ref, got_ref, send_sems, recv_sems):
        x, y, c = lax.axis_index("x"), lax.axis_index("y"), lax.axis_index("c")
        copies = [pltpu.make_async_remote_copy(
            src_ref=b_ref.at[2 * q + 1 - c], dst_ref=got_ref.at[q], send_sem=send_sems.at[q],
            recv_sem=recv_sems.at[q], device_id=(x, y, 1 - c), device_id_type=pl.DeviceIdType.MESH)
            for q in range(N_CHIP)]
        for cp in copies:
            cp.start()
        for cp in copies:
            cp.wait_recv()
        for cp in copies:
            cp.wait_send()

    any_spec = pl.BlockSpec(memory_space=pl.ANY)
    return pl.pallas_call(
        body, name=name, in_specs=[any_spec], out_specs=any_spec,
        out_shape=jax.ShapeDtypeStruct((N_CHIP,) + blocks.shape[1:], blocks.dtype),
        scratch_shapes=[pltpu.SemaphoreType.DMA((N_CHIP,))] * 2,
    )(blocks)


def _pair_sum(name, blocks, got, core):
    _, R, Wd = blocks.shape

    def body(c_ref, a_ref, b_ref, o_ref):
        o_ref[...] = (a_ref[...].astype(F32) + b_ref[...].astype(F32)).astype(o_ref.dtype)

    return pl.pallas_call(
        body, name=name,
        grid_spec=pltpu.PrefetchScalarGridSpec(
            num_scalar_prefetch=1, grid=(N_CHIP,),
            in_specs=[pl.BlockSpec((None, R, Wd), lambda q, c_ref: (2 * q + c_ref[0], 0, 0)),
                      pl.BlockSpec((None, R, Wd), lambda q, c_ref: (q, 0, 0))],
            out_specs=pl.BlockSpec((None, R, Wd), lambda q, c_ref: (q, 0, 0))),
        out_shape=jax.ShapeDtypeStruct(got.shape, blocks.dtype),
        compiler_params=pltpu.CompilerParams(dimension_semantics=("parallel",), vmem_limit_bytes=VMEM_LIMIT),
    )(core, blocks, got)


def _chip_copies(s_ref, land_ref, send_sems, recv_sems):
    x, y, c = lax.axis_index("x"), lax.axis_index("y"), lax.axis_index("c")
    my_q = 2 * x + y
    sends, recvs = [], []
    for kbits in range(1, N_CHIP):
        px = 1 - x if kbits & 2 else x
        py = 1 - y if kbits & 1 else y
        peer_q = 2 * px + py
        sends.append(pltpu.make_async_remote_copy(
            src_ref=s_ref.at[peer_q], dst_ref=land_ref.at[my_q], send_sem=send_sems[kbits - 1],
            recv_sem=recv_sems[kbits - 1], device_id=(px, py, c), device_id_type=pl.DeviceIdType.MESH))
        recvs.append(pltpu.make_async_remote_copy(
            src_ref=s_ref.at[peer_q], dst_ref=land_ref.at[peer_q], send_sem=send_sems[kbits - 1],
            recv_sem=recv_sems[kbits - 1], device_id=(px, py, c), device_id_type=pl.DeviceIdType.MESH))
    return sends, recvs


_HBM = pl.BlockSpec(memory_space=pltpu.HBM)
_SEM = pl.BlockSpec(memory_space=pltpu.SEMAPHORE)
N_CHIP_SEMS = 2 * (N_CHIP - 1)


def _scatter_copies(b_refs, land_refs, send_sems, recv_sems):
    x, y, c = lax.axis_index("x"), lax.axis_index("y"), lax.axis_index("c")
    me = 4 * x + 2 * y + c
    sends, recvs = [], []
    for kbits in range(1, N_DEV):
        px = 1 - x if kbits & 4 else x
        py = 1 - y if kbits & 2 else y
        pc = 1 - c if kbits & 1 else c
        peer = 4 * px + 2 * py + pc
        for b in range(len(b_refs)):
            s = (kbits - 1) * len(b_refs) + b
            sends.append(pltpu.make_async_remote_copy(
                src_ref=b_refs[b].at[peer], dst_ref=land_refs[b].at[me], send_sem=send_sems[s],
                recv_sem=recv_sems[s], device_id=(px, py, pc), device_id_type=pl.DeviceIdType.MESH))
            recvs.append(pltpu.make_async_remote_copy(
                src_ref=b_refs[b].at[peer], dst_ref=land_refs[b].at[peer], send_sem=send_sems[s],
                recv_sem=recv_sems[s], device_id=(px, py, pc), device_id_type=pl.DeviceIdType.MESH))
    return sends, recvs


def _scatter_start(name, bufs):
    nb = len(bufs)
    n = (N_DEV - 1) * nb

    def body(*refs):
        b_refs, land_refs, outs = refs[:nb], refs[nb:2 * nb], refs[2 * nb:]
        sends, _ = _scatter_copies(b_refs, land_refs, outs[:n], outs[n:2 * n])
        for cp in sends:
            cp.start()
        token = outs[2 * n + 2 * nb]
        token[...] = jnp.zeros_like(token)

    thru = tuple(pltpu.HBM(b.shape, b.dtype) for b in bufs)
    res = pl.pallas_call(
        body, name=name, in_specs=(_HBM,) * (2 * nb),
        out_specs=(_SEM,) * (2 * n) + (_HBM,) * (2 * nb) + (pl.BlockSpec(memory_space=pltpu.VMEM),),
        out_shape=(pltpu.SemaphoreType.DMA(()),) * (2 * n) + thru + thru + (jax.ShapeDtypeStruct((8, LANES), F32),),
        input_output_aliases={i: 2 * n + i for i in range(2 * nb)},
        compiler_params=pltpu.CompilerParams(has_side_effects=pltpu.SideEffectType.DATAFLOW_SIDE_EFFECTING),
    )(*[pltpu.with_memory_space_constraint(b, pltpu.HBM) for b in bufs],
      *[pltpu.with_memory_space_constraint(lax.empty(b.shape, b.dtype), pltpu.HBM) for b in bufs])
    return res[:2 * n], list(res[2 * n:2 * n + nb]), list(res[2 * n + nb:2 * n + 2 * nb]), res[2 * n + 2 * nb]


def _scatter_wait(name, sems, bufs_thru, lands_thru, after):
    nb = len(bufs_thru)
    n = (N_DEV - 1) * nb

    def body(*refs):
        b_refs, land_refs, sem_refs = refs[:nb], refs[nb:2 * nb], refs[2 * nb:2 * nb + 2 * n]
        sends, recvs = _scatter_copies(b_refs, land_refs, sem_refs[:n], sem_refs[n:])
        for cp in sends:
            cp.wait_send()
        for cp in recvs:
            cp.wait_recv()

    thru = tuple(pltpu.HBM(b.shape, b.dtype) for b in bufs_thru)
    res = pl.pallas_call(
        body, name=name, in_specs=(_HBM,) * (2 * nb) + (_SEM,) * (2 * n) + (pl.BlockSpec(memory_space=pl.ANY),),
        out_specs=(_HBM,) * (2 * nb), out_shape=thru + thru,
        input_output_aliases={i: i for i in range(2 * nb)},
        compiler_params=pltpu.CompilerParams(has_side_effects=pltpu.SideEffectType.DATAFLOW_SIDE_EFFECTING),
    )(*bufs_thru, *lands_thru, *sems, after)
    return list(res[:nb]), list(res[nb:])


def _chip_exchange_start(name, sums):
    def body(s_ref, land_ref, *outs):
        sems, token = outs[:N_CHIP_SEMS], outs[N_CHIP_SEMS + 2]
        sends, _ = _chip_copies(s_ref, land_ref, sems[:N_CHIP - 1], sems[N_CHIP - 1:])
        for cp in sends:
            cp.start()
        token[...] = jnp.zeros_like(token)

    res = pl.pallas_call(
        body, name=name, in_specs=(_HBM, _HBM),
        out_specs=(_SEM,) * N_CHIP_SEMS + (_HBM, _HBM, pl.BlockSpec(memory_space=pltpu.VMEM)),
        out_shape=(pltpu.SemaphoreType.DMA(()),) * N_CHIP_SEMS
        + (pltpu.HBM(sums.shape, sums.dtype), pltpu.HBM(sums.shape, sums.dtype), jax.ShapeDtypeStruct((8, LANES), F32)),
        input_output_aliases={0: N_CHIP_SEMS, 1: N_CHIP_SEMS + 1},
        compiler_params=pltpu.CompilerParams(has_side_effects=pltpu.SideEffectType.DATAFLOW_SIDE_EFFECTING),
    )(pltpu.with_memory_space_constraint(sums, pltpu.HBM),
      pltpu.with_memory_space_constraint(lax.empty(sums.shape, sums.dtype), pltpu.HBM))
    return res[:N_CHIP_SEMS], res[N_CHIP_SEMS], res[N_CHIP_SEMS + 1], res[N_CHIP_SEMS + 2]


def _chip_exchange_wait(name, sems, sums_thru, land_thru, after):
    def body(s_ref, land_ref, *rest):
        sems = rest[:N_CHIP_SEMS]
        sends, recvs = _chip_copies(s_ref, land_ref, sems[:N_CHIP - 1], sems[N_CHIP - 1:])
        for cp in sends:
            cp.wait_send()
        for cp in recvs:
            cp.wait_recv()

    return pl.pallas_call(
        body, name=name, in_specs=(_HBM, _HBM) + (_SEM,) * N_CHIP_SEMS + (pl.BlockSpec(memory_space=pl.ANY),),
        out_specs=(_HBM, _HBM),
        out_shape=(pltpu.HBM(sums_thru.shape, sums_thru.dtype), pltpu.HBM(sums_thru.shape, sums_thru.dtype)),
        input_output_aliases={0: 0, 1: 1},
        compiler_params=pltpu.CompilerParams(has_side_effects=pltpu.SideEffectType.DATAFLOW_SIDE_EFFECTING),
    )(sums_thru, land_thru, *sems, after)


def _adamw(name, slots, w, m, v, tr=256):
    unit_mid = w.ndim == 3 and w.shape[1] == 1 and w.shape[0] > 1
    R, Wd = (w.shape[0], w.shape[2]) if unit_mid else w.shape[-2:]
    depth_axis = w.ndim == 3 and not unit_mid
    if R % tr == 0:
        tc = Wd
    else:
        tr, tc = R, (256 if (Wd % 256 == 0 and R > 256) else Wd)
    at = (slice(None), 0, slice(None)) if unit_mid else Ellipsis

    def body(s_ref, w_ref, m_ref, v_ref, g_out, d_out, m_out, v_out):
        g = s_ref[0].astype(F32)
        for j in range(1, slots.shape[0]):
            g = g + s_ref[j].astype(F32)
        m_new = ADAM_B1 * m_ref[at] + (1.0 - ADAM_B1) * g
        v_new = ADAM_B2 * v_ref[at] + (1.0 - ADAM_B2) * jnp.square(g)
        m_hat = m_new / (1.0 - ADAM_B1 ** ADAM_STEP)
        v_hat = v_new / (1.0 - ADAM_B2 ** ADAM_STEP)
        g_out[at] = g
        d_out[at] = -ADAM_LR * (m_hat / (jnp.sqrt(v_hat) + ADAM_EPS) + ADAM_WD * w_ref[at])
        m_out[at] = m_new
        v_out[at] = v_new

    if unit_mid:
        row = pl.BlockSpec((tr, 1, tc), lambda i, j: (i, 0, j))
    elif depth_axis:
        row = pl.BlockSpec((None, tr, tc), lambda i, j: (0, i, j))
    else:
        row = pl.BlockSpec((tr, tc), lambda i, j: (i, j))
    return pl.pallas_call(
        body, name=name, grid=(R // tr, Wd // tc),
        in_specs=[pl.BlockSpec((slots.shape[0], tr, tc), lambda i, j: (0, i, j)), row, row, row],
        out_specs=[row] * 4, out_shape=[jax.ShapeDtypeStruct(w.shape, F32)] * 4,
        compiler_params=pltpu.CompilerParams(dimension_semantics=("parallel", "parallel"),
                                             vmem_limit_bytes=VMEM_LIMIT),
    )(slots, w, m, v)


PACK_W = 1024
PACKED = [(n, s) for n, s in REPLICATED if n != "sgu_w"]
_SMALL_SIZES = [int(np.prod(s)) for _, s in PACKED]
_SMALL_ROWS = _round_up(_round_up(sum(_SMALL_SIZES) + PACK_W, PACK_W) // PACK_W, 8)
_LOSS_AT = sum(_SMALL_SIZES)
W_IN_SHARD = P_TOTAL // N_DEV


def _pack_rows(parts, rows, dtype):
    flat = jnp.concatenate([p.reshape(-1).astype(dtype) for p in parts])
    return jnp.pad(flat, (0, rows * PACK_W - flat.shape[0])).reshape(rows, PACK_W)


def _w_in_groups_t(blocks):
    wt = blocks.reshape(P_TOTAL, D)
    o, c = 2 * D, 2 * D + 3 * D
    z = lambda r: jnp.zeros((r, D), wt.dtype)
    rw = jnp.concatenate([wt[o:c], wt[c:c + L_W], z(128 - L_W), wt[c + L_W:c + L_W + L_A], z(128 - L_A),
                          wt[c + L_W + L_A:o + C_B], z(256 - L_G)], axis=0)
    return wt[:o], rw, wt[o + C_B:]


def _w_in_grad_blocks(g_sgu_t, g_rw_t, g_gate_t):
    c = 3 * D
    full = jnp.concatenate([g_sgu_t, g_rw_t[:c], g_rw_t[c:c + L_W], g_rw_t[c + 128:c + 128 + L_A],
                            g_rw_t[c + 256:c + 256 + L_G], g_gate_t], axis=0)
    return full.reshape(N_DEV, W_IN_SHARD, D)


def _mesh_index():
    me = 4 * lax.axis_index("x") + 2 * lax.axis_index("y") + lax.axis_index("c")
    return me.astype(jnp.int32).reshape(1)


def _fill_slot(name, dst, src, idx, src_idx=None):
    R, Wd = dst.shape[1:]
    scalars = [idx] if src_idx is None else [idx, src_idx]
    if src_idx is None:
        src_spec = pl.BlockSpec((R, Wd), lambda i, *s: (0, 0))
    else:
        src_spec = pl.BlockSpec((None, R, Wd), lambda i, *s: (s[1][0], 0, 0))

    def body(*refs):
        src_ref, out_ref = refs[len(scalars) + 1], refs[len(scalars) + 2]
        out_ref[...] = src_ref[...]

    return pl.pallas_call(
        body, name=name,
        grid_spec=pltpu.PrefetchScalarGridSpec(
            num_scalar_prefetch=len(scalars), grid=(1,),
            in_specs=[pl.BlockSpec(memory_space=pl.ANY), src_spec],
            out_specs=pl.BlockSpec((None, R, Wd), lambda i, *s: (s[0][0], 0, 0))),
        out_shape=jax.ShapeDtypeStruct(dst.shape, dst.dtype),
        input_output_aliases={len(scalars): 0},
        compiler_params=pltpu.CompilerParams(vmem_limit_bytes=VMEM_LIMIT),
    )(*scalars, dst, src)


class TwoLevelGather:
    def __init__(self, bufs, skip_own=()):
        self.bufs, self.nb, self.skip_own = list(bufs), len(bufs), tuple(skip_own)
        self.any_specs = [pl.BlockSpec(memory_space=pl.ANY)] * self.nb
        self.out_shape = [jax.ShapeDtypeStruct((N_DEV,) + b.shape, b.dtype) for b in self.bufs]
        self.sem_shapes = [pltpu.SemaphoreType.DMA((7 * self.nb,)), pltpu.SemaphoreType.DMA((7 * self.nb,)),
                           pltpu.SemaphoreType.DMA((self.nb,))]

    def _copies(self, in_refs, out_refs, sems):
        send_sems, recv_sems, local_sems = sems
        nb = self.nb
        x, y, c = lax.axis_index("x"), lax.axis_index("y"), lax.axis_index("c")
        me, sibling = (x, y, c), (x, y, 1 - c)
        chips = [(1 - x, y), (x, 1 - y), (1 - x, 1 - y)]

        def slot(b, dev):
            return out_refs[b].at[4 * dev[0] + 2 * dev[1] + dev[2]]

        def copy(b, k, block, to, own=False):
            return pltpu.make_async_remote_copy(
                src_ref=in_refs[b] if own else slot(b, block), dst_ref=slot(b, block),
                send_sem=send_sems.at[7 * b + k], recv_sem=recv_sems.at[7 * b + k], device_id=to,
                device_id_type=pl.DeviceIdType.MESH)

        cp = {}
        cp["local"] = [pltpu.make_async_copy(in_refs[b], slot(b, me), local_sems.at[b]) for b in range(nb)
                       if b not in self.skip_own]
        cp["first"] = [copy(b, 0, me, sibling, own=True) for b in range(nb)]
        cp["first"] += [copy(b, 1 + j, me, (*chip, c), own=True) for j, chip in enumerate(chips) for b in range(nb)]
        cp["over_ici"] = [copy(b, 1 + j, (*chip, c), me) for j, chip in enumerate(chips) for b in range(nb)]
        cp["passed"] = [copy(b, 4 + j, (*chip, c), sibling) for j, chip in enumerate(chips) for b in range(nb)]
        cp["from_sibling"] = [copy(b, 0, sibling, me) for b in range(nb)]
        cp["from_sibling"] += [copy(b, 4 + j, (*chip, 1 - c), me) for j, chip in enumerate(chips) for b in range(nb)]
        return cp

    def start(self, in_refs, out_refs, sems):
        cp = self._copies(in_refs, out_refs, sems)
        for c in cp["first"] + cp["local"]:
            c.start()

    def forward(self, in_refs, out_refs, sems):
        cp = self._copies(in_refs, out_refs, sems)
        for arrived, onward in zip(cp["over_ici"], cp["passed"]):
            arrived.wait_recv()
            onward.start()

    def finish(self, in_refs, out_refs, sems):
        cp = self._copies(in_refs, out_refs, sems)
        for c in cp["from_sibling"]:
            c.wait_recv()
        for c in cp["first"] + cp["passed"]:
            c.wait_send()
        for c in cp["local"]:
            c.wait()

    def schedule(self, n_steps):
        return [(0, self.start), (max(n_steps - 3, 0), self.forward), (n_steps - 1, self.finish)]


def _all_gather_two_level(name, bufs, skip_own=()):
    ex = TwoLevelGather(bufs, skip_own)

    def body(*refs):
        args = refs[:ex.nb], refs[ex.nb:2 * ex.nb], refs[2 * ex.nb:]
        ex.start(*args)
        ex.forward(*args)
        ex.finish(*args)

    return pl.pallas_call(body, name=name, in_specs=ex.any_specs, out_specs=ex.any_specs, out_shape=ex.out_shape,
                          scratch_shapes=ex.sem_shapes)(*ex.bufs)


class _SemRefs:
    def __init__(self, refs):
        self.at = list(refs)


def _gather_step(name, ex, step, carried, after, scratch_shapes=()):
    nb, n = ex.nb, 7 * ex.nb
    first = carried is None
    if first:
        sems, bufs, lands = (), ex.bufs, [lax.empty(s.shape, s.dtype) for s in ex.out_shape]
        bufs, lands = ([pltpu.with_memory_space_constraint(b, pltpu.HBM) for b in group] for group in (bufs, lands))
    else:
        sems, bufs, lands = carried
    n_in = 2 * nb + len(sems) + 1
    n_sem_out = 2 * n if first else 0

    def body(*refs):
        b_refs, land_refs, outs = refs[:nb], refs[nb:2 * nb], refs[n_in:]
        sem_refs = outs[:2 * n] if first else refs[2 * nb:2 * nb + 2 * n]
        token, scratch = outs[n_sem_out + 2 * nb], outs[n_sem_out + 2 * nb + 1:]
        step(b_refs, land_refs, (_SemRefs(sem_refs[:n]), _SemRefs(sem_refs[n:]), None), *scratch)
        token[...] = jnp.zeros_like(token)

    thru = tuple(pltpu.HBM(b.shape, b.dtype) for b in list(bufs) + list(lands))
    res = pl.pallas_call(
        body, name=name, in_specs=(_HBM,) * (2 * nb) + (_SEM,) * len(sems) + (pl.BlockSpec(memory_space=pl.ANY),),
        out_specs=(_SEM,) * n_sem_out + (_HBM,) * (2 * nb) + (pl.BlockSpec(memory_space=pltpu.VMEM),),
        out_shape=(pltpu.SemaphoreType.DMA(()),) * n_sem_out + thru + (jax.ShapeDtypeStruct((8, LANES), F32),),
        input_output_aliases={i: n_sem_out + i for i in range(2 * nb)}, scratch_shapes=list(scratch_shapes),
        compiler_params=pltpu.CompilerParams(has_side_effects=pltpu.SideEffectType.DATAFLOW_SIDE_EFFECTING,
                                             vmem_limit_bytes=VMEM_LIMIT),
    )(*bufs, *lands, *sems, after)
    sems, res = (res[:n_sem_out] if first else sems), res[n_sem_out:]
    return (sems, list(res[:nb]), list(res[nb:2 * nb])), res[2 * nb]


def _cols_from_blocks(blk):
    return jnp.transpose(blk, (1, 0, 2)).reshape(blk.shape[1], -1)


def _cols_to_blocks(g):
    r, c = g.shape
    return jnp.transpose(g.reshape(r, N_DEV, c // N_DEV), (1, 0, 2))


FIRST_WEIGHTS = ["w_in", "shift_b", "w_lora_w", "a_lora_w", "g_lora_w"]
SCAN_CARRIED = ["w_proj_a", "w_proj_b", "w_out"]
FFN_WEIGHTS = ["w_ffn1", "w_ffn2"]


def _late_weights(shards, after):
    ex = TwoLevelGather([shards[n].astype(BF16) for n in SCAN_CARRIED])
    ffn = TwoLevelGather([shards[n][0].astype(BF16) for n in FFN_WEIGHTS], skip_own=range(len(FFN_WEIGHTS)))

    def start(b_refs, land_refs, sems, *scratch):
        ffn.start(b_refs, land_refs, sems)
        stage, local_sems = scratch[:ffn.nb], scratch[ffn.nb]
        me = 4 * lax.axis_index("x") + 2 * lax.axis_index("y") + lax.axis_index("c")
        for src, dst in ((b_refs, stage), (stage, [ref.at[me] for ref in land_refs])):
            copies = [pltpu.make_async_copy(src[b], dst[b], local_sems.at[b]) for b in range(ffn.nb)]
            for cp in copies:
                cp.start()
            for cp in copies:
                cp.wait()

    state = {}
    state["carried"], sent = _gather_step(
        "weight_ffn_start", ffn, start, None, after,
        scratch_shapes=[pltpu.VMEM(b.shape, b.dtype) for b in ffn.bufs] + [pltpu.SemaphoreType.DMA((ffn.nb,))])

    def before_scan(after):
        state["carried"], passed_on = _gather_step("weight_ffn_forward", ffn, ffn.forward, state["carried"], after)
        return passed_on

    def finish(results, scan_out):
        W = {n: r.reshape(-1, D) for n, r in zip(SCAN_CARRIED, results)}
        (_, _, got), _ = _gather_step("weight_ffn_finish", ffn, ffn.finish, state["carried"], scan_out)
        W["w_ffn1"] = got[0]
        W["w_ffn2"] = got[1].reshape(-1, D)
        return W
    return sent, (ex, before_scan, finish)


def _gather_weights(shards):
    def payload(n):
        if n == "w_in":
            return jnp.transpose(shards[n][0]).astype(BF16)
        return shards[n] if n == "shift_b" else shards[n].astype(BF16)
    payloads = [payload(n) for n in FIRST_WEIGHTS]
    got = dict(zip(FIRST_WEIGHTS, _all_gather_two_level("weight_all_gather", payloads, skip_own=(0,))))
    got["w_in"] = _fill_slot("w_in_own_slot", got["w_in"], payloads[0], _mesh_index())
    W = {}
    W["w_sgu_t"], W["w_rw_t"], W["w_gate_t"] = _w_in_groups_t(got["w_in"])
    z = lambda r, c, dt: jnp.zeros((r, c), dt)
    W["w_lora"] = jnp.concatenate([_cols_from_blocks(got["w_lora_w"][:, 0]).astype(F32), z(128 - L_W, D, F32)], axis=0)
    W["a_lora"] = jnp.concatenate([_cols_from_blocks(got["a_lora_w"][:, 0]).astype(F32), z(128 - L_A, D, F32)], axis=0)
    W["g_lora"] = jnp.concatenate([_cols_from_blocks(got["g_lora_w"][:, 0]).astype(F32), z(256 - L_G, D, F32)], axis=0)
    sb = _cols_from_blocks(got["shift_b"][:, 0])
    W["sb"] = jnp.concatenate([sb[:, :3 * D], sb[:, 3 * D:3 * D + L_W], z(2, 128 - L_W, F32),
                               sb[:, 3 * D + L_W:3 * D + L_W + L_A], z(2, 128 - L_A, F32),
                               sb[:, 3 * D + L_W + L_A:], z(2, 256 - L_G, F32)], axis=1)
    return W, got["shift_b"]


def _replicated_weights(rep):
    W = {n: rep[n] for n in ("g_mix", "sgu_ln_w", "sgu_ln_b", "w0", "a0", "k_k", "k_a", "r_k", "ln_x_w", "ln_x_b",
                             "g_ffn")}
    W["g_final"] = rep["g_final"].reshape(1, D)
    W["sgu_w"] = rep["sgu_w"][0]
    W["sgu_bt"] = jnp.transpose(rep["sgu_b"][0])
    return W


def _late_grad_blocks(G):
    return Exchange([G[n].reshape(N_DEV, -1, D) for n in SCAN_CARRIED]
                    + [G["sgu_w"].reshape(SGU_G * SGU_C, SGU_C).astype(GRAD_PAYLOAD)],
                    [False] * len(SCAN_CARRIED) + [True])


def _first_grad_blocks(G):
    sbg = G["sb"]
    c = 3 * D
    sb = jnp.concatenate([sbg[:, :c], sbg[:, c:c + L_W], sbg[:, c + 128:c + 128 + L_A],
                          sbg[:, c + 256:c + 256 + L_G]], axis=1)
    return {
        "shift_b": _cols_to_blocks(sb),
        "w_lora_w": _cols_to_blocks(G["w_lora"][:L_W]), "a_lora_w": _cols_to_blocks(G["a_lora"][:L_A]),
        "g_lora_w": _cols_to_blocks(G["g_lora"][:L_G]),
    }


def _replicated_grads(G):
    small = {n: G[n] for n in ("g_mix", "sgu_ln_w", "sgu_ln_b", "w0", "a0", "k_k", "k_a", "r_k", "ln_x_w", "ln_x_b",
                               "g_ffn", "g_final")}
    small["sgu_w"] = G["sgu_w"]
    small["sgu_b"] = jnp.transpose(G["sgu_bt"])
    return small


def kernel(x, g_mix, w_in, sgu_ln_w, sgu_ln_b, sgu_w, sgu_b, w_proj_a, shift_b, w_lora_w, w0, a_lora_w, a0, g_lora_w, k_k, k_a, r_k, ln_x_w, ln_x_b, w_proj_b, w_out, g_ffn, w_ffn1, w_ffn2, g_final, loss_target, m_g_mix, m_w_in, m_sgu_ln_w, m_sgu_ln_b, m_sgu_w, m_sgu_b, m_w_proj_a, m_shift_b, m_w_lora_w, m_w0, m_a_lora_w, m_a0, m_g_lora_w, m_k_k, m_k_a, m_r_k, m_ln_x_w, m_ln_x_b, m_w_proj_b, m_w_out, m_g_ffn, m_w_ffn1, m_w_ffn2, m_g_final, v_g_mix, v_w_in, v_sgu_ln_w, v_sgu_ln_b, v_sgu_w, v_sgu_b, v_w_proj_a, v_shift_b, v_w_lora_w, v_w0, v_a_lora_w, v_a0, v_g_lora_w, v_k_k, v_k_a, v_r_k, v_ln_x_w, v_ln_x_b, v_w_proj_b, v_w_out, v_g_ffn, v_w_ffn1, v_w_ffn2, v_g_final):
    env = dict(locals())
    weights = {n: env[n] for n in WEIGHT_ORDER}
    moms = {n: env["m_" + n] for n in WEIGHT_ORDER}
    vars_ = {n: env["v_" + n] for n in WEIGHT_ORDER}

    shards = {n: weights[n] for n, _, _ in SHARDED}
    W, gathered = _gather_weights(shards)
    W.update(_replicated_weights({n: weights[n] for n, _ in REPLICATED}))
    W["ffn_weights_sent"], late_weights = _late_weights(shards, after=gathered)
    in_flight = {}

    def send_w_in_grads(G):
        blocks = _w_in_grad_blocks(G["w_sgu_t"], G["w_rw_t"], G["w_gate_t"])
        got = _pair_exchange("grad_pair_exchange", blocks)
        core = lax.axis_index("c").astype(jnp.int32).reshape(1)
        sums = _pair_sum("grad_pair_sum", blocks, got, core)
        in_flight["sems"], in_flight["sums"], in_flight["land"], token = _chip_exchange_start("grad_chip_start", sums)
        return token

    def send_ffn_grads(G):
        in_flight["ffn"] = _scatter_start("grad_ffn_start", [G["w_ffn1"], G["w_ffn2"].reshape(N_DEV, -1, D)])
        return in_flight["ffn"][3]

    loss_part, dx, G, late_slots = _local_step(x[0], loss_target[0], W, late_weights=late_weights,
                                               early_grads=_late_grad_blocks, w_in_grads_ready=send_w_in_grads,
                                               ffn_grads_ready=send_ffn_grads)

    slots = dict(zip(SCAN_CARRIED, late_slots))
    me = _mesh_index()
    sent, landed = _scatter_wait("grad_ffn_wait", *in_flight["ffn"][:3], after=dx)
    for i, n in enumerate(FFN_WEIGHTS):
        slots[n] = _fill_slot("grad_own_slot_" + n, landed[i], sent[i], me, src_idx=me)
    blocks = _first_grad_blocks(G)
    small = _replicated_grads(G)
    small_parts = [small[n] for n, _ in PACKED] + [jnp.full((PACK_W,), loss_part, F32)]
    rest = [n for n in FIRST_WEIGHTS if n != "w_in"]
    res = _exchange("grad_exchange", [blocks[n] for n in rest] + [_pack_rows(small_parts, _SMALL_ROWS, F32)],
                    [False] * len(rest) + [True])
    slots.update(zip(rest, res[:-1]))
    small_slots = res[-1]
    sums, chip_slots = _chip_exchange_wait("grad_chip_wait", in_flight["sems"], in_flight["sums"], in_flight["land"],
                                           after=small_slots)
    my_chip = (2 * lax.axis_index("x") + lax.axis_index("y")).astype(jnp.int32).reshape(1)
    slots["w_in"] = _fill_slot("grad_own_slot", chip_slots, sums, my_chip, src_idx=my_chip)

    outs = [dict(), dict(), dict(), dict()]
    for n, _, _ in SHARDED:
        if n == "w_in":
            res = _adamw("adamw_" + n, slots[n], *[jnp.transpose(t, (2, 0, 1)) for t in (weights[n], moms[n], vars_[n])])
            res = [jnp.transpose(t, (1, 2, 0)) for t in res]
        else:
            res = _adamw("adamw_" + n, slots[n], weights[n], moms[n], vars_[n])
        for k in range(4):
            outs[k][n] = res[k]

    sgu_shape = (SGU_G * SGU_C, SGU_C)
    res = _adamw("adamw_sgu_w", late_slots[len(SCAN_CARRIED)], *[t.reshape(sgu_shape) for t in
                                                                  (weights["sgu_w"], moms["sgu_w"], vars_["sgu_w"])])
    for k in range(4):
        outs[k]["sgu_w"] = res[k].reshape(weights["sgu_w"].shape)

    def packed(d):
        return _pack_rows([d[n] for n, _ in PACKED], _SMALL_ROWS, F32)
    small_out = _adamw("adamw_replicated", small_slots, packed(weights), packed(moms), packed(vars_))
    for k in range(4):
        flat = small_out[k].reshape(-1)
        off = 0
        for (n, s), size in zip(PACKED, _SMALL_SIZES):
            outs[k][n] = flat[off:off + size].reshape(s)
            off += size
    loss = small_out[0].reshape(-1)[_LOSS_AT]
    return (loss, dx[None], *[outs[0][n] for n in WEIGHT_ORDER], *[outs[1][n] for n in WEIGHT_ORDER],
            *[outs[2][n] for n in WEIGHT_ORDER], *[outs[3][n] for n in WEIGHT_ORDER])
```

```python
import functools
import numpy as np
import jax
import jax.numpy as jnp
from jax import lax
from jax.experimental import pallas as pl
from jax.experimental.pallas import tpu as pltpu

F32 = jnp.float32
BF16 = jnp.bfloat16

D = 1024
NH, HN = 16, 64
NP, PW = NH // 2, 2 * HN
SGU_G, SGU_C = 8, 128
L_W, L_A, L_G = 64, 64, 160
C_B = 3 * D + L_W + L_A + L_G
P_TOTAL = 2 * D + C_B + 2 * D
D_FF = 4 * D
RW_INT = 3 * D + 128 + 128 + 256
NORM_EPS, LN_EPS, GN_EPS = 1e-6, 1e-5, 64e-5
N_DEV = 8
LANES = 128
SCAN_C = 64
N_KEPT = 4
SOLVE_B = 16
SCAN_PRECISION = lax.Precision.HIGH
SCAN_OUT_PRECISION = lax.Precision.DEFAULT
GRAD_PAYLOAD = BF16
VMEM_LIMIT = 56 * 1024 * 1024
MATMUL_VMEM_BUDGET = 40 * 1024 * 1024
STEP_COST_BYTES = 512 * 1024
HBM_COST_RATIO = 3
ACC_PASS_WEIGHT = 4

ADAM_LR, ADAM_B1, ADAM_B2, ADAM_EPS, ADAM_WD, ADAM_STEP = 0.001, 0.9, 0.999, 1e-08, 0.01, 10

SHARDED = [
    ("w_in", (D, P_TOTAL), 1), ("w_proj_a", (D, D), 0), ("shift_b", (2, C_B), 1), ("w_lora_w", (L_W, D), 1),
    ("a_lora_w", (L_A, D), 1), ("g_lora_w", (L_G, D), 1), ("w_proj_b", (D, D), 0), ("w_out", (D, D), 0),
    ("w_ffn1", (D, D_FF), 1), ("w_ffn2", (D_FF, D), 0),
]
REPLICATED = [
    ("g_mix", (1, D)), ("sgu_ln_w", (1, D)), ("sgu_ln_b", (1, D)), ("sgu_w", (1, SGU_G, SGU_C, SGU_C)),
    ("sgu_b", (1, SGU_G, SGU_C)), ("w0", (1, D)), ("a0", (1, D)), ("k_k", (1, D)), ("k_a", (1, D)), ("r_k", (1, D)),
    ("ln_x_w", (1, D)), ("ln_x_b", (1, D)), ("g_ffn", (1, D)), ("g_final", (D,)),
]
WEIGHT_ORDER = ["g_mix", "w_in", "sgu_ln_w", "sgu_ln_b", "sgu_w", "sgu_b", "w_proj_a", "shift_b", "w_lora_w", "w0",
                "a_lora_w", "a0", "g_lora_w", "k_k", "k_a", "r_k", "ln_x_w", "ln_x_b", "w_proj_b", "w_out", "g_ffn",
                "w_ffn1", "w_ffn2", "g_final"]


def _round_up(n, m):
    return (n + m - 1) // m * m


def _pick(n, target):
    if n <= target:
        return n
    best = None
    for t in range(LANES, target + 1, LANES):
        if n % t == 0:
            best = t
    assert best is not None, (n, target)
    return best


def _matmul(name, a, b, mode, out_dtype=F32, tm=2048, tn=1024, tk=4096, out_blocks=None, epilogue=None, extras=(),
            out_dtypes=(), after=None, whole_rows=False, out_widths=None):
    b_blocks = b.shape[0] if b.ndim == 3 else None
    bshape = b.shape if b.ndim == 2 else (b.shape[1], b.shape[0] * b.shape[2])
    if mode == "nn":
        (M, K), (K2, N) = a.shape, bshape
    elif mode == "nt":
        (M, K), (N, K2) = a.shape, bshape
    else:
        (K, M), (K2, N) = a.shape, bshape
    assert K == K2, (name, a.shape, b.shape)
    assert b_blocks is None or mode != "tn"
    assert out_blocks is None or mode == "tn"
    tn = min(tn, N // (out_blocks or 1), bshape[1] // b_blocks if (b_blocks and mode == "nn") else tn)
    blocked_k = bool(b_blocks) and mode == "nt"
    tm, tn, tk = _pick(M, tm), _pick(N, tn), (K if blocked_k else _pick(K, tk))

    def vmem_bytes(tm, tk):
        tiles = tm * tk * a.dtype.itemsize + tk * tn * b.dtype.itemsize
        for i, dt in enumerate(out_dtypes if epilogue else (out_dtype,)):
            tiles += tm * (out_widths[i] if out_widths else tn) * jnp.dtype(dt).itemsize
        for x in extras:
            arr = x[0] if isinstance(x, tuple) else x
            tiles += (tm if arr.shape[0] > 1 else 1) * tn * arr.dtype.itemsize
        return 2 * tiles + (tm * tn * 4 if K // tk > 1 else 0)

    def cost(tm, tk):
        ni, nj, nk = M // tm, N // tn, K // tk
        steps = ni * nj * nk
        acc_passes = steps * tm * tn * 8 * ACC_PASS_WEIGHT if nk > 1 else 0
        a_reads = M * K * a.dtype.itemsize * (nj if nk > 1 else 1)
        b_reads = K * N * b.dtype.itemsize * (ni if (nj > 1 or nk > 1) else 1)
        return (steps * STEP_COST_BYTES + acc_passes + vmem_bytes(tm, tk) // 2
                + HBM_COST_RATIO * (a_reads + b_reads))

    options = [(m, k) for m in ({M} if whole_rows else {_pick(M, max(t, LANES)) for t in (tm, tm // 2, tm // 4)})
               for k in ({K} if blocked_k else {_pick(K, max(t, LANES)) for t in (tk, tk // 2, tk // 4)})
               if vmem_bytes(m, k) <= MATMUL_VMEM_BUDGET]
    tm, tk = min(options, key=lambda o: cost(*o))
    nk = K // tk
    dims = {"nn": (((1,), (0,)), ((), ())), "nt": (((1,), (1,)), ((), ())), "tn": (((0,), (0,)), ((), ()))}[mode]

    n_x, n_o = len(extras), len(out_dtypes) if epilogue else 1
    n_after = 0 if after is None else 1

    def body(a_ref, b_ref, *rest):
        x_refs, o_refs, acc = rest[:n_x], rest[n_x + n_after:n_x + n_after + n_o], rest[n_x + n_after + n_o:]
        if blocked_k:
            bw = b.shape[2]
            part = sum(lax.dot_general(a_ref[:, blk * bw:(blk + 1) * bw].astype(BF16), b_ref[blk].astype(BF16), dims,
                                       preferred_element_type=F32) for blk in range(b_blocks))
        else:
            part = lax.dot_general(a_ref[...].astype(BF16), b_ref[...].astype(BF16), dims, preferred_element_type=F32)

        def finish(res):
            outs = epilogue(res, *[r[...] for r in x_refs]) if epilogue else (res,)
            for r, v in zip(o_refs, outs):
                r[...] = v.astype(r.dtype)

        if nk == 1:
            finish(part)
            return
        acc_ref, k = acc[0], pl.program_id(2)

        @pl.when(k == 0)
        def _():
            acc_ref[...] = part

        @pl.when(k > 0)
        def _():
            acc_ref[...] += part

        @pl.when(k == nk - 1)
        def _():
            finish(acc_ref[...])

    a_spec = {"nn": pl.BlockSpec((tm, tk), lambda i, j, k: (i, k)), "nt": pl.BlockSpec((tm, tk), lambda i, j, k: (i, k)),
              "tn": pl.BlockSpec((tk, tm), lambda i, j, k: (k, i))}[mode]
    b_spec = {"nn": pl.BlockSpec((tk, tn), lambda i, j, k: (k, j)), "nt": pl.BlockSpec((tn, tk), lambda i, j, k: (j, k)),
              "tn": pl.BlockSpec((tk, tn), lambda i, j, k: (k, j))}[mode]
    if b_blocks and mode == "nn":
        per = b.shape[2] // tn
        b_spec = pl.BlockSpec((None, tk, tn), lambda i, j, k: (j // per, k, j % per))
    elif b_blocks:
        b_spec = pl.BlockSpec((b_blocks, tn, b.shape[2]), lambda i, j, k: (0, j, 0))
    out_spec = pl.BlockSpec((tm, tn), lambda i, j, k: (i, j))
    out_shape = jax.ShapeDtypeStruct((M, N), out_dtype)
    if out_blocks:
        per_o = N // out_blocks // tn
        out_spec = pl.BlockSpec((None, tm, tn), lambda i, j, k: (j // per_o, i, j % per_o))
        out_shape = jax.ShapeDtypeStruct((out_blocks, M, N // out_blocks), out_dtype)
    epi_widths = list(out_widths) if out_widths else [N] * n_o
    assert all(w == N for w in epi_widths) or N == tn
    epi_specs = [pl.BlockSpec((tm, tn if w == N else w), lambda i, j, k: (i, j)) for w in epi_widths]
    x_specs, x_args = [], []
    for x in extras:
        arr, off = x if isinstance(x, tuple) else (x, 0)
        if arr.shape[0] == 1:
            x_specs.append(pl.BlockSpec((1, tn), lambda i, j, k: (0, j)))
        else:
            x_specs.append(pl.BlockSpec((tm, tn), lambda i, j, k, off=off: (i, j + off)))
        x_args.append(arr)
    res = pl.pallas_call(
        body, name=name, grid=(M // tm, N // tn, nk),
        in_specs=[a_spec, b_spec] + x_specs + [pl.BlockSpec(memory_space=pl.ANY)] * n_after,
        out_specs=epi_specs if epilogue else out_spec,
        out_shape=[jax.ShapeDtypeStruct((M, w), dt) for w, dt in zip(epi_widths, out_dtypes)] if epilogue else out_shape,
        scratch_shapes=[pltpu.VMEM((tm, tn), F32)] if nk > 1 else [],
        compiler_params=pltpu.CompilerParams(dimension_semantics=("parallel", "parallel", "arbitrary"),
                                             vmem_limit_bytes=VMEM_LIMIT),
    )(a, b, *x_args, *([after] if n_after else []))
    return res


class Rows:
    def __init__(self, arr, width=None, cb=0):
        self.arr, self.width, self.cb = arr, (arr.shape[1] if width is None else width), cb


class Heads:
    def __init__(self, arr):
        self.arr = arr


class Halo:
    def __init__(self, arr, side):
        self.arr, self.side = arr, side


def _rows_call(name, fn, ins, consts, outs, accs=(), tm=512, with_pid=False):
    T = next(o.arr.shape[1] if isinstance(o, Heads) else o.arr.shape[0] for o in ins if not isinstance(o, Halo))
    tm = min(tm, T)
    n_tiles = T // tm
    n_in, n_c, n_out = len(ins), len(consts), len(outs)
    in_specs, args = [], []
    for o in ins:
        if isinstance(o, Rows):
            in_specs.append(pl.BlockSpec((tm, o.width), lambda i, cb=o.cb: (i, cb)))
        elif isinstance(o, Heads):
            in_specs.append(pl.BlockSpec((NP, tm, PW), lambda i: (0, i, 0)))
        else:
            w = o.arr.shape[1]
            if o.side < 0:
                in_specs.append(pl.BlockSpec((8, w), lambda i: (jnp.maximum(i * (tm // 8) - 1, 0), 0)))
            else:
                in_specs.append(pl.BlockSpec((8, w), lambda i: (jnp.minimum((i + 1) * (tm // 8), T // 8 - 1), 0)))
        args.append(o.arr)
    for c in consts:
        in_specs.append(pl.BlockSpec(c.shape, lambda i, nd=c.ndim: (0,) * nd))
        args.append(c)
    out_specs, out_shape = [], []
    for o in outs:
        if o[0] == "rows":
            out_specs.append(pl.BlockSpec((tm, o[1]), lambda i: (i, 0)))
            out_shape.append(jax.ShapeDtypeStruct((T, o[1]), o[2]))
        else:
            out_specs.append(pl.BlockSpec((NP, tm, PW), lambda i: (0, i, 0)))
            out_shape.append(jax.ShapeDtypeStruct((NP, T, PW), o[1]))
    for shape, dt in accs:
        out_specs.append(pl.BlockSpec(shape, lambda i, nd=len(shape): (0,) * nd))
        out_shape.append(jax.ShapeDtypeStruct(shape, dt))

    def body(*refs):
        i = pl.program_id(0)
        vals = []
        vals = [r[...] for r in refs[:n_in + n_c]]
        res = fn(i, n_tiles, *vals) if with_pid else fn(*vals)
        out_refs = refs[n_in + n_c:]
        for r, v in zip(out_refs[:n_out], res[:n_out]):
            r[...] = v.astype(r.dtype)
        if accs:
            @pl.when(i == 0)
            def _():
                for r in out_refs[n_out:]:
                    r[...] = jnp.zeros_like(r)

            for r, v in zip(out_refs[n_out:], res[n_out:]):
                r[...] += v.astype(r.dtype)

    res = pl.pallas_call(
        body, name=name, grid=(n_tiles,), in_specs=in_specs, out_specs=out_specs, out_shape=out_shape,
        compiler_params=pltpu.CompilerParams(dimension_semantics=("arbitrary",), vmem_limit_bytes=VMEM_LIMIT),
    )(*args)
    return res


def _rms(x, g):
    return x * lax.rsqrt(jnp.mean(x * x, axis=-1, keepdims=True) + NORM_EPS) * g


def _gelu(x):
    return 0.5 * x * (1.0 + lax.erf(x * 0.7071067811865476))


def _sigmoid(x):
    return 1.0 / (1.0 + jnp.exp(-x))


def _bdot(a, b):
    return jnp.dot(a.astype(BF16), b.astype(BF16), preferred_element_type=F32)


def _to_heads(x):
    return jnp.concatenate([x[:, p * PW:(p + 1) * PW][None] for p in range(NP)], axis=0)


def _from_heads(xp):
    return jnp.concatenate([xp[p] for p in range(NP)], axis=-1)


def _head_sum(xp):
    low = lax.broadcasted_iota(jnp.int32, xp.shape, xp.ndim - 1) < HN
    both = jnp.sum(xp, axis=-1, keepdims=True)
    first = jnp.sum(jnp.where(low, xp, 0.0), axis=-1, keepdims=True)
    return jnp.where(low, first, both - first)


def _split_pairs(xp):
    return jnp.concatenate([xp[:, :, :HN], xp[:, :, HN:]], axis=0)


def _join_pairs(xh):
    return jnp.concatenate([xh[:NP], xh[NP:]], axis=-1)


def _sgu_fn(p, ln_w, ln_b, sw, sbt):
    z = _gelu(p)
    u, v = z[:, :D], z[:, D:]
    mu = jnp.mean(v, axis=-1, keepdims=True)
    var = jnp.mean(jnp.square(v - mu), axis=-1, keepdims=True)
    vn = (v - mu) * lax.rsqrt(var + LN_EPS) * ln_w + ln_b
    ri = lax.broadcasted_iota(jnp.int32, (SGU_C, SGU_C), 0)
    ci = lax.broadcasted_iota(jnp.int32, (SGU_C, SGU_C), 1)
    mask = (ci <= ri).astype(F32)
    dg = D // SGU_G
    parts = []
    for g in range(SGU_G):
        parts.append(_bdot(sw[g] * mask, vn[:, g * dg:(g + 1) * dg]) + sbt[:, g:g + 1])
    return u * jnp.concatenate(parts, axis=-1)


def _pre_fn(qr, qk, qv, qxw, qxa, qxg, wl, w0, al, a0, gl, k_k, k_a):
    w = -jax.nn.softplus(-(w0 + _bdot(jnp.tanh(qxw), wl))) - 0.5
    lw = -jnp.exp(w)
    aa = _sigmoid(a0 + _bdot(qxa, al))
    g = _bdot(_sigmoid(qxg), gl)
    kk = _to_heads(qk * k_k)
    kk = kk / jnp.maximum(jnp.sqrt(_head_sum(kk * kk)), 1e-12)
    k2 = qk * (1.0 + (aa - 1.0) * k_a)
    return _to_heads(qr), _to_heads(lw), _to_heads(k2), _to_heads(qv), kk, _to_heads(aa), g


def _post_fn(o, r, k2, v, g, ln_w, ln_b, r_k):
    mu = _head_sum(o) * (1.0 / HN)
    d = o - mu
    var = _head_sum(d * d) * (1.0 / HN)
    on = d * lax.rsqrt(var + GN_EPS) * ln_w + ln_b
    bonus = _head_sum(r * k2 * r_k) * v
    return _from_heads(on + bonus) * g


def _gate_fn(pg, ya, yb):
    return _sigmoid(pg[:, :D]) * ya + _sigmoid(pg[:, D:]) * yb


def _bmm(x, y, cx, cy, out_path=False):
    return lax.dot_general(x, y, (((cx,), (cy,)), ((0,), (0,))),
                           precision=SCAN_OUT_PRECISION if out_path else SCAN_PRECISION, preferred_element_type=F32)


def _unit_lower_inverse(M):
    C = M.shape[1]
    ti = lax.broadcasted_iota(jnp.int32, (C, C), 0)
    tj = lax.broadcasted_iota(jnp.int32, (C, C), 1)
    eye = (ti == tj).astype(F32)
    same = lambda b: (ti // b == tj // b).astype(F32)
    X = -(M * same(SOLVE_B))
    inv = eye + X
    span = 1
    while 2 * span < SOLVE_B:
        X = _bmm(X, X, 2, 1)
        inv = inv + _bmm(inv, X, 2, 1)
        span *= 2
    b = SOLVE_B
    while b < C:
        low = M * (same(2 * b) - same(b))
        inv = inv - _bmm(_bmm(inv, low, 2, 1, out_path=True), inv, 2, 1, out_path=True)
        b *= 2
    return inv


@jax.custom_vjp
def _unit_lower_solve(inv, M, y):
    return _bmm(inv, y, 2, 1)


def _unit_lower_solve_fwd(inv, M, y):
    u = _bmm(inv, y, 2, 1)
    return u, (inv, u)


def _unit_lower_solve_bwd(res, du):
    inv, u = res
    dy = _bmm(inv, du, 1, 1)
    return jnp.zeros_like(inv), -_bmm(dy, u, 2, 2), dy


_unit_lower_solve.defvjp(_unit_lower_solve_fwd, _unit_lower_solve_bwd)


@jax.custom_vjp
def _unit_lower_solved(inv, u, M, y):
    return u


_unit_lower_solved.defvjp(lambda inv, u, M, y: (u, (inv, u)),
                          lambda res, du: (jnp.zeros_like(res[0]), jnp.zeros_like(res[1]))
                          + _unit_lower_solve_bwd(res, du)[1:])


@functools.partial(jax.custom_vjp, nondiff_argnums=(0,))
def _kept(fn, value, *args):
    return value


def _kept_fwd(fn, value, *args):
    return value, args


def _kept_bwd(fn, args, d):
    _, vjp = jax.vjp(fn, *args)
    return (jnp.zeros_like(d),) + tuple(vjp(d))


_kept.defvjp(_kept_fwd, _kept_bwd)


def _sum_over_time(x, reverse):
    C = x.shape[1]
    ti = lax.broadcasted_iota(jnp.int32, (C, C), 0)
    tj = lax.broadcasted_iota(jnp.int32, (C, C), 1)
    ones = jnp.broadcast_to(((tj >= ti) if reverse else (tj <= ti)).astype(BF16), (x.shape[0], C, C))
    hi = x.astype(BF16)
    r1 = x - hi.astype(F32)
    mid = r1.astype(BF16)
    lo = (r1 - mid.astype(F32)).astype(BF16)
    dn = (((2,), (1,)), ((0,), (0,)))
    return sum(lax.dot_general(ones, p, dn, preferred_element_type=F32) for p in (lo, mid, hi))


@jax.custom_vjp
def _time_cumsum(lw):
    return _sum_over_time(lw, reverse=False)


_time_cumsum.defvjp(lambda lw: (_sum_over_time(lw, reverse=False), None),
                    lambda _, d: (_sum_over_time(d, reverse=True),))


def _chunk_fn(S0, r, lw, k, v, kk, a, kept=None):
    C = SCAN_C
    bmm = _bmm
    ti = lax.broadcasted_iota(jnp.int32, (C, C), 0)
    tj = lax.broadcasted_iota(jnp.int32, (C, C), 1)
    incl2 = jnp.concatenate([(tj <= ti).astype(F32)] * 2, axis=1)
    strict = (tj < ti).astype(F32)
    n_mask = jnp.concatenate([jnp.zeros((C, C), F32), strict], axis=1)

    def known(name, fn, *args):
        return fn(*args) if kept is None else _kept(fn, kept[name], *args)

    cum = known("cum", _time_cumsum, lw)
    g_in, g_ex, g_inv = jnp.exp(cum), jnp.exp(cum - lw), jnp.exp(-cum)
    kkt, rt = kk * g_ex, r * g_in
    bk = jnp.concatenate([kk * a * g_inv, k * g_inv], axis=1)
    kr = jnp.concatenate([kkt, rt], axis=1)
    ratios = known("ratios", lambda x, y: bmm(x, y, 2, 2), kr, bk)
    A = ratios[:, :C]
    M = A[:, :, :C] * strict
    zv = jnp.concatenate([jnp.zeros_like(v), v], axis=1)
    s0_side = bmm(kr, S0, 2, 2, out_path=True)
    rhs = s0_side[:, :C] + bmm(A * n_mask, zv, 2, 1, out_path=True)
    if kept is None:
        inv = lax.stop_gradient(_unit_lower_inverse(M))
        y = _unit_lower_solve(inv, M, rhs)
    else:
        inv = kept["inv"]
        y = _unit_lower_solved(inv, kept["y"], M, rhs)
    z = jnp.concatenate([-y, v], axis=1)
    O = s0_side[:, C:] + bmm(ratios[:, C:] * incl2, z, 2, 1, out_path=True)
    g_end = g_in[:, C - 1:C, :]
    S1 = S0 * g_end + bmm(z, bk * g_end, 1, 1, out_path=True)
    return O, S1, dict(cum=cum, ratios=ratios, y=y, inv=inv)


def _scan_fwd(r, lw, k, v, kk, a, ex=None, tb=256):
    assert SCAN_C == HN and 2 * SCAN_C == PW
    T = r.shape[1]
    tb = min(tb, T)
    n_chunks = tb // SCAN_C
    nb = T // tb
    nx = ex.nb if ex else 0

    def body(*refs):
        r_ref, lw_ref, k_ref, v_ref, kk_ref, a_ref = refs[:6]
        x_in, (o_ref, s0_ref), x_out = refs[6:6 + nx], refs[6 + nx:8 + nx], refs[8 + nx:8 + 2 * nx]
        s_ref, sems = refs[8 + 2 * nx], refs[9 + 2 * nx:]

        plan = ex.schedule(nb) if ex else []

        @pl.when(pl.program_id(0) == 0)
        def _():
            s_ref[...] = jnp.zeros_like(s_ref)
            for at, action in plan[:1]:
                action(x_in, x_out, sems)

        def step(c, carry):
            sl = pl.ds(pl.multiple_of(c * SCAN_C, SCAN_C), SCAN_C)
            S0 = s_ref[...]
            O, S1, keep = _chunk_fn(S0, *[_split_pairs(ref[:, sl, :])
                                          for ref in (r_ref, lw_ref, k_ref, v_ref, kk_ref, a_ref)])
            o_ref[:, sl, :] = _join_pairs(O)
            s0_ref[c, 0] = jnp.concatenate([S0, keep["inv"]], axis=-1)
            s0_ref[c, 1] = jnp.concatenate([keep["cum"], keep["y"]], axis=-1)
            s0_ref[c, 2] = keep["ratios"][:, :SCAN_C]
            s0_ref[c, 3] = keep["ratios"][:, SCAN_C:]
            s_ref[...] = S1
            return carry

        lax.fori_loop(0, n_chunks, step, 0)

        for at, action in plan[1:]:
            pl.when(pl.program_id(0) == at)(functools.partial(action, x_in, x_out, sems))

    hm = pl.BlockSpec((NP, tb, PW), lambda i: (0, i, 0))
    res = pl.pallas_call(
        body, name="rwkv_scan_fwd", grid=(nb,), in_specs=[hm] * 6 + (ex.any_specs if ex else []),
        out_specs=[hm, pl.BlockSpec((n_chunks, N_KEPT, NH, HN, PW), lambda i: (i, 0, 0, 0, 0))]
        + (ex.any_specs if ex else []),
        out_shape=[jax.ShapeDtypeStruct((NP, T, PW), F32),
                   jax.ShapeDtypeStruct((T // SCAN_C, N_KEPT, NH, HN, PW), F32)]
        + (ex.out_shape if ex else []),
        scratch_shapes=[pltpu.VMEM((NH, HN, HN), F32)] + (ex.sem_shapes if ex else []),
        compiler_params=pltpu.CompilerParams(dimension_semantics=("arbitrary",), vmem_limit_bytes=VMEM_LIMIT),
    )(r, lw, k, v, kk, a, *(ex.bufs if ex else []))
    return res[0], res[1], list(res[2:])


def _scan_bwd(r, lw, k, v, kk, a, s0s, do, ex=None, tb=128):
    T = r.shape[1]
    tb = min(tb, T)
    n_chunks = tb // SCAN_C
    nb = T // tb
    nx = ex.nb if ex else 0

    def body(*refs):
        r_ref, lw_ref, k_ref, v_ref, kk_ref, a_ref, s0_ref, do_ref = refs[:8]
        x_in, (dr, dlw, dk, dv, dkk, da), x_out = refs[8:8 + nx], refs[8 + nx:14 + nx], refs[14 + nx:14 + 2 * nx]
        ds_ref, sems = refs[14 + 2 * nx], refs[15 + 2 * nx:]

        plan = ex.schedule(nb) if ex else []

        @pl.when(pl.program_id(0) == 0)
        def _():
            ds_ref[...] = jnp.zeros_like(ds_ref)
            for at, action in plan[:1]:
                action(x_in, x_out, sems)

        def step(j, carry):
            c = n_chunks - 1 - j
            sl = pl.ds(pl.multiple_of(c * SCAN_C, SCAN_C), SCAN_C)
            s0_inv, cum_y = s0_ref[c, 0], s0_ref[c, 1]
            kept = dict(inv=s0_inv[:, :, HN:], cum=cum_y[:, :, :HN], y=cum_y[:, :, HN:],
                        ratios=jnp.concatenate([s0_ref[c, 2], s0_ref[c, 3]], axis=1))
            _, vjp = jax.vjp(lambda *t: _chunk_fn(*t, kept=kept)[:2], s0_inv[:, :, :HN],
                             *[_split_pairs(ref[:, sl, :]) for ref in (r_ref, lw_ref, k_ref, v_ref, kk_ref, a_ref)])
            g = vjp((_split_pairs(do_ref[:, sl, :]), ds_ref[...]))
            ds_ref[...] = g[0]
            for ref, val in zip((dr, dlw, dk, dv, dkk, da), g[1:]):
                ref[:, sl, :] = _join_pairs(val)
            return carry

        lax.fori_loop(0, n_chunks, step, 0)

        for at, action in plan[1:]:
            pl.when(pl.program_id(0) == at)(functools.partial(action, x_in, x_out, sems))

    hm = pl.BlockSpec((NP, tb, PW), lambda i: (0, nb - 1 - i, 0))
    res = pl.pallas_call(
        body, name="rwkv_scan_bwd", grid=(nb,),
        in_specs=[hm] * 6 + [pl.BlockSpec((n_chunks, N_KEPT, NH, HN, PW), lambda i: (nb - 1 - i, 0, 0, 0, 0)), hm]
        + (ex.any_specs if ex else []),
        out_specs=[hm] * 6 + (ex.any_specs if ex else []),
        out_shape=[jax.ShapeDtypeStruct((NP, T, PW), F32)] * 6 + (ex.out_shape if ex else []),
        scratch_shapes=[pltpu.VMEM((NH, HN, HN), F32)] + (ex.sem_shapes if ex else []),
        compiler_params=pltpu.CompilerParams(dimension_semantics=("arbitrary",), vmem_limit_bytes=VMEM_LIMIT),
    )(r, lw, k, v, kk, a, s0s, do, *(ex.bufs if ex else []))
    return list(res[:6]), list(res[6:])


def _shift_down(i, p, prev8):
    first = jnp.where(i > 0, prev8[7:8, :], 0.0)
    row = lax.broadcasted_iota(jnp.int32, p.shape, 0)
    return jnp.where(row == 0, first, pltpu.roll(p, 1, axis=0))


def _mix_bwd(dq, p, sb, tm=256):
    def fn(i, n, dq, next8, p, prev8, sb):
        ps = _shift_down(i, p, prev8)
        d1 = dq * sb[1:2]
        last = jnp.where(i < n - 1, next8[0:1, :] * sb[1:2], 0.0)
        row = lax.broadcasted_iota(jnp.int32, dq.shape, 0)
        up = jnp.where(row == dq.shape[0] - 1, last, pltpu.roll(d1, dq.shape[0] - 1, axis=0))
        return (dq * sb[0:1] + up, jnp.sum(dq * p, axis=0, keepdims=True), jnp.sum(dq * ps, axis=0, keepdims=True))
    w = p.shape[1]
    return _rows_call("shift_mix_bwd", fn, [Rows(dq), Halo(dq, +1), Rows(p), Halo(p, -1)], [sb], [("rows", w, BF16)],
                      accs=[((1, w), F32), ((1, w), F32)], tm=tm, with_pid=True)


def _local_step(x, target, W, late_weights=None, early_grads=None, w_in_grads_ready=None, ffn_grads_ready=None):
    G = {}
    a = _rows_call("norm_mix_fwd", lambda x, g: (_rms(x, g),), [Rows(x)], [W["g_mix"]], [("rows", D, BF16)])[0]
    p_sgu = _matmul("proj_sgu", a, W["w_sgu_t"], "nt")
    def token_shift(p, sb0, sb1):
        row = lax.broadcasted_iota(jnp.int32, p.shape, 0)
        return p, p * sb0 + jnp.where(row == 0, 0.0, pltpu.roll(p, 1, axis=0)) * sb1
    p_rw, q = _matmul("proj_rwkv", a, W["w_rw_t"], "nt", tn=512, whole_rows=True, epilogue=token_shift,
                      extras=[W["sb"][0:1], W["sb"][1:2]], out_dtypes=(F32, F32))
    p_gate = _matmul("proj_gate", a, W["w_gate_t"], "nt")

    sgu_consts = [W["sgu_ln_w"], W["sgu_ln_b"], W["sgu_w"], W["sgu_bt"]]
    s = _rows_call("sgu_fwd", lambda *t: (_sgu_fn(*t),), [Rows(p_sgu)], sgu_consts, [("rows", D, BF16)], tm=SGU_C)[0]

    q_ins = [Rows(q, D, 0), Rows(q, D, 1), Rows(q, D, 2), Rows(q, 128, 24), Rows(q, 128, 25), Rows(q, 256, 13)]
    pre_consts = [W["w_lora"], W["w0"], W["a_lora"], W["a0"], W["g_lora"], W["k_k"], W["k_a"]]
    r_h, lw_h, k_h, v_h, kk_h, a_h, g_gate = _rows_call(
        "rwkv_pre_fwd", _pre_fn, q_ins, pre_consts, [("heads", F32)] * 6 + [("rows", D, F32)], tm=128)
    o_h, s0s, got = _scan_fwd(r_h, lw_h, k_h, v_h, kk_h, a_h, ex=late_weights[0] if late_weights else None)
    if late_weights:
        W = {**W, **late_weights[1](got)}
    y_a = _matmul("proj_a", s, W["w_proj_a"], "nn")
    post_ins = [Heads(o_h), Heads(r_h), Heads(k_h), Heads(v_h), Rows(g_gate)]
    post_consts = [W[n].reshape(NP, 1, PW) for n in ("ln_x_w", "ln_x_b", "r_k")]
    z_b = _rows_call("rwkv_post_fwd", lambda *t: (_post_fn(*t),), post_ins, post_consts, [("rows", D, BF16)], tm=128)[0]
    y_b, mixed = _matmul("proj_b", z_b, W["w_proj_b"], "nn", extras=[(p_gate, 0), (p_gate, 1), y_a],
                         epilogue=lambda yb, ga, gb, ya: (yb, _sigmoid(ga) * ya + _sigmoid(gb) * yb),
                         out_dtypes=(F32, BF16))

    def res1(mo, x, g):
        h1 = x + mo
        return h1, _rms(h1, g)
    h1, f = _matmul("proj_out", mixed, W["w_out"], "nn", extras=[x, W["g_ffn"]], epilogue=res1,
                    out_dtypes=(F32, BF16))

    def relu_sq(u):
        r = jnp.maximum(u, 0.0)
        return r, r * r
    r1, act = _matmul("ffn_up", f, W["w_ffn1"], "nn", epilogue=relu_sq, out_dtypes=(BF16, BF16))
    ff = _matmul("ffn_down", act, W["w_ffn2"], "nn")

    def head(h1, ff, tgt, g):
        def f_(h1, ff, g):
            y = _rms(h1 + ff, g)
            return 0.5 * jnp.sum(jnp.mean(jnp.square(y - tgt), axis=-1))
        loss, (dh2, _, dg) = jax.value_and_grad(f_, argnums=(0, 1, 2))(h1, ff, g)
        return dh2, jnp.full((8, LANES), loss, F32), dg
    dh2, loss_acc, G["g_final"] = _rows_call("loss_head", head, [Rows(h1), Rows(ff), Rows(target)], [W["g_final"]],
                                             [("rows", D, F32)], accs=[((8, LANES), F32), ((1, D), F32)])

    d_u1 = _matmul("ffn_down_dx", dh2, W["w_ffn2"], "nt", extras=[r1], out_dtypes=(BF16,),
                   epilogue=lambda d_act, r: (d_act * 2.0 * r.astype(F32),))[0]
    G["w_ffn2"] = _matmul("ffn_down_dw", act, dh2, "tn", out_dtype=GRAD_PAYLOAD)
    d_f = _matmul("ffn_up_dx", d_u1, W["w_ffn1"], "nt")
    G["w_ffn1"] = _matmul("ffn_up_dw", f, d_u1, "tn", out_blocks=N_DEV, out_dtype=GRAD_PAYLOAD)

    def res1_bwd(h1, d_f, dh2, g):
        _, vjp = jax.vjp(_rms, h1, g)
        dh, dg = vjp(d_f)
        return dh2 + dh, dg
    dh1, G["g_ffn"] = _rows_call("residual_norm_bwd", res1_bwd, [Rows(h1), Rows(d_f), Rows(dh2)],
                                 [W["g_ffn"]], [("rows", D, F32)], accs=[((1, D), F32)])
    ffn_token = ffn_grads_ready(G) if ffn_grads_ready else None
    def gate_bwd(d_mixed, ga, gb, ya, yb):
        _, vjp = jax.vjp(_gate_fn, jnp.concatenate([ga, gb], axis=-1), ya, yb)
        return vjp(d_mixed)
    d_gate, d_ya, d_yb = _matmul("proj_out_dx", dh1, W["w_out"], "nt", after=ffn_token, epilogue=gate_bwd,
                                 extras=[(p_gate, 0), (p_gate, 1), y_a, y_b], out_dtypes=(BF16, BF16, BF16),
                                 out_widths=(2 * D, D, D))
    G["w_out"] = _matmul("proj_out_dw", mixed, dh1, "tn", out_dtype=GRAD_PAYLOAD)

    d_s = _matmul("proj_a_dx", d_ya, W["w_proj_a"], "nt")
    G["w_proj_a"] = _matmul("proj_a_dw", s, d_ya, "tn", out_dtype=GRAD_PAYLOAD)

    def sgu_bwd(p, ds, *c):
        _, vjp = jax.vjp(_sgu_fn, p, *c)
        return vjp(ds)
    d_p_sgu, G["sgu_ln_w"], G["sgu_ln_b"], G["sgu_w"], G["sgu_bt"] = _rows_call(
        "sgu_bwd", sgu_bwd, [Rows(p_sgu), Rows(d_s)], sgu_consts, [("rows", 2 * D, BF16)],
        accs=[((1, D), F32), ((1, D), F32), ((SGU_G, SGU_C, SGU_C), F32), ((SGU_C, SGU_G), F32)], tm=SGU_C)

    d_zb = _matmul("proj_b_dx", d_yb, W["w_proj_b"], "nt")
    G["w_proj_b"] = _matmul("proj_b_dw", z_b, d_yb, "tn", out_dtype=GRAD_PAYLOAD)

    def post_bwd(o, r, k2, v, g, dz, *c):
        _, vjp = jax.vjp(_post_fn, o, r, k2, v, g, *c)
        return vjp(dz)
    do_h, dr1, dk1, dv1, d_g, g_lnw, g_lnb, g_rk = _rows_call(
        "rwkv_post_bwd", post_bwd, post_ins + [Rows(d_zb)], post_consts, [("heads", F32)] * 4 + [("rows", D, F32)],
        accs=[((NP, 1, PW), F32)] * 3, tm=128)
    G["ln_x_w"], G["ln_x_b"], G["r_k"] = (t.reshape(1, D) for t in (g_lnw, g_lnb, g_rk))
    (dr2, dlw, dk2, dv2, dkk, daa), early = _scan_bwd(r_h, lw_h, k_h, v_h, kk_h, a_h, s0s, do_h,
                                                      ex=early_grads(G) if early_grads else None)

    def pre_bwd(qr, qk, qv, qxw, qxa, qxg, dr1, dr2, dlw, dk1, dk2, dv1, dv2, dkk, daa, dg, *c):
        _, vjp = jax.vjp(_pre_fn, qr, qk, qv, qxw, qxa, qxg, *c)
        g = vjp((dr1 + dr2, dlw, dk1 + dk2, dv1 + dv2, dkk, daa, dg))
        dq = jnp.concatenate(g[:6], axis=-1)
        return (dq,) + tuple(g[6:])
    pre_b_ins = q_ins + [Heads(dr1), Heads(dr2), Heads(dlw), Heads(dk1), Heads(dk2), Heads(dv1), Heads(dv2),
                         Heads(dkk), Heads(daa), Rows(d_g)]
    d_q, G["w_lora"], G["w0"], G["a_lora"], G["a0"], G["g_lora"], G["k_k"], G["k_a"] = _rows_call(
        "rwkv_pre_bwd", pre_bwd, pre_b_ins, pre_consts, [("rows", RW_INT, F32)],
        accs=[((128, D), F32), ((1, D), F32), ((128, D), F32), ((1, D), F32), ((256, D), F32), ((1, D), F32),
              ((1, D), F32)], tm=128)
    d_p_rw, dsb0, dsb1 = _mix_bwd(d_q, p_rw, W["sb"])
    G["sb"] = jnp.concatenate([dsb0, dsb1], axis=0)

    G["w_sgu_t"] = _matmul("proj_sgu_dw", d_p_sgu, a, "tn", out_dtype=GRAD_PAYLOAD)
    G["w_rw_t"] = _matmul("proj_rwkv_dw", d_p_rw, a, "tn", out_dtype=GRAD_PAYLOAD)
    G["w_gate_t"] = _matmul("proj_gate_dw", d_gate, a, "tn", out_dtype=GRAD_PAYLOAD)
    token = w_in_grads_ready(G) if w_in_grads_ready else None
    da1 = _matmul("proj_sgu_dx", d_p_sgu, W["w_sgu_t"], "nn", after=token)
    da2 = _matmul("proj_rwkv_dx", d_p_rw, W["w_rw_t"], "nn", after=token)
    da3 = _matmul("proj_gate_dx", d_gate, W["w_gate_t"], "nn", after=token)

    def norm1_bwd(x, da1, da2, da3, dh1, g):
        _, vjp = jax.vjp(_rms, x, g)
        dx, dg = vjp(da1 + da2 + da3)
        return dh1 + dx, dg
    dx, G["g_mix"] = _rows_call("norm_mix_bwd", norm1_bwd, [Rows(x), Rows(da1), Rows(da2), Rows(da3), Rows(dh1)],
                                [W["g_mix"]], [("rows", D, F32)], accs=[((1, D), F32)])
    return loss_acc[0, 0], dx, G, early


class Exchange:
    def __init__(self, bufs, gathers):
        self.bufs, self.gathers, self.nb = list(bufs), list(gathers), len(bufs)
        self.any_specs = [pl.BlockSpec(memory_space=pl.ANY)] * self.nb
        self.out_shape = [jax.ShapeDtypeStruct((N_DEV,) + (b.shape if g else b.shape[1:]), b.dtype)
                          for b, g in zip(self.bufs, self.gathers)]
        n = (N_DEV - 1) * self.nb
        self.sem_shapes = [pltpu.SemaphoreType.DMA((n,)), pltpu.SemaphoreType.DMA((n,)),
                           pltpu.SemaphoreType.DMA((self.nb,))]

    def _copies(self, in_refs, out_refs, sems):
        send_sems, recv_sems, local_sems = sems
        x, y, c = lax.axis_index("x"), lax.axis_index("y"), lax.axis_index("c")
        me = 4 * x + 2 * y + c

        def src(b, dest):
            return in_refs[b] if self.gathers[b] else in_refs[b].at[dest]

        local = [pltpu.make_async_copy(src(b, me), out_refs[b].at[me], local_sems.at[b]) for b in range(self.nb)]
        sends, recvs = [], []
        for kbits in range(1, N_DEV):
            px = 1 - x if kbits & 4 else x
            py = 1 - y if kbits & 2 else y
            pc = 1 - c if kbits & 1 else c
            peer = 4 * px + 2 * py + pc
            for b in range(self.nb):
                s = (kbits - 1) * self.nb + b
                sends.append(pltpu.make_async_remote_copy(
                    src_ref=src(b, peer), dst_ref=out_refs[b].at[me], send_sem=send_sems.at[s],
                    recv_sem=recv_sems.at[s], device_id=(px, py, pc), device_id_type=pl.DeviceIdType.MESH))
                recvs.append(pltpu.make_async_remote_copy(
                    src_ref=src(b, peer), dst_ref=out_refs[b].at[peer], send_sem=send_sems.at[s],
                    recv_sem=recv_sems.at[s], device_id=(px, py, pc), device_id_type=pl.DeviceIdType.MESH))
        return local, sends, recvs

    def start(self, in_refs, out_refs, sems):
        local, sends, _ = self._copies(in_refs, out_refs, sems)
        for cp in sends + local:
            cp.start()

    def wait(self, in_refs, out_refs, sems):
        local, sends, recvs = self._copies(in_refs, out_refs, sems)
        for cp in recvs:
            cp.wait_recv()
        for cp in sends:
            cp.wait_send()
        for cp in local:
            cp.wait()

    def schedule(self, n_steps):
        return [(0, self.start), (n_steps - 1, self.wait)]


def _exchange(name, bufs, gather):
    ex = Exchange(bufs, gather if isinstance(gather, (list, tuple)) else [gather] * len(bufs))

    def body(*refs):
        in_refs, out_refs, sems = refs[:ex.nb], refs[ex.nb:2 * ex.nb], refs[2 * ex.nb:]
        ex.start(in_refs, out_refs, sems)
        ex.wait(in_refs, out_refs, sems)

    return pl.pallas_call(body, name=name, in_specs=ex.any_specs, out_specs=ex.any_specs, out_shape=ex.out_shape,
                          scratch_shapes=ex.sem_shapes)(*ex.bufs)


N_CHIP = 4


def _pair_exchange(name, blocks):
    def body(b_ref, got_ref, send_sems, recv_sems):
        x, y, c = lax.axis_index("x"), lax.axis_index("y"), lax.axis_index("c")
        copies = [pltpu.make_async_remote_copy(
            src_ref=b_ref.at[2 * q + 1 - c], dst_ref=got_ref.at[q], send_sem=send_sems.at[q],
            recv_sem=recv_sems.at[q], device_id=(x, y, 1 - c), device_id_type=pl.DeviceIdType.MESH)
            for q in range(N_CHIP)]
        for cp in copies:
            cp.start()
        for cp in copies:
            cp.wait_recv()
        for cp in copies:
            cp.wait_send()

    any_spec = pl.BlockSpec(memory_space=pl.ANY)
    return pl.pallas_call(
        body, name=name, in_specs=[any_spec], out_specs=any_spec,
        out_shape=jax.ShapeDtypeStruct((N_CHIP,) + blocks.shape[1:], blocks.dtype),
        scratch_shapes=[pltpu.SemaphoreType.DMA((N_CHIP,))] * 2,
    )(blocks)


def _pair_sum(name, blocks, got, core):
    _, R, Wd = blocks.shape

    def body(c_ref, a_ref, b_ref, o_ref):
        o_ref[...] = (a_ref[...].astype(F32) + b_ref[...].astype(F32)).astype(o_ref.dtype)

    return pl.pallas_call(
        body, name=name,
        grid_spec=pltpu.PrefetchScalarGridSpec(
            num_scalar_prefetch=1, grid=(N_CHIP,),
            in_specs=[pl.BlockSpec((None, R, Wd), lambda q, c_ref: (2 * q + c_ref[0], 0, 0)),
                      pl.BlockSpec((None, R, Wd), lambda q, c_ref: (q, 0, 0))],
            out_specs=pl.BlockSpec((None, R, Wd), lambda q, c_ref: (q, 0, 0))),
        out_shape=jax.ShapeDtypeStruct(got.shape, blocks.dtype),
        compiler_params=pltpu.CompilerParams(dimension_semantics=("parallel",), vmem_limit_bytes=VMEM_LIMIT),
    )(core, blocks, got)


def _chip_copies(s_ref, land_ref, send_sems, recv_sems):
    x, y, c = lax.axis_index("x"), lax.axis_index("y"), lax.axis_index("c")
    my_q = 2 * x + y
    sends, recvs = [], []
    for kbits in range(1, N_CHIP):
        px = 1 - x if kbits & 2 else x
        py = 1 - y if kbits & 1 else y
        peer_q = 2 * px + py
        sends.append(pltpu.make_async_remote_copy(
            src_ref=s_ref.at[peer_q], dst_ref=land_ref.at[my_q], send_sem=send_sems[kbits - 1],
            recv_sem=recv_sems[kbits - 1], device_id=(px, py, c), device_id_type=pl.DeviceIdType.MESH))
        recvs.append(pltpu.make_async_remote_copy(
            src_ref=s_ref.at[peer_q], dst_ref=land_ref.at[peer_q], send_sem=send_sems[kbits - 1],
            recv_sem=recv_sems[kbits - 1], device_id=(px, py, c), device_id_type=pl.DeviceIdType.MESH))
    return sends, recvs


_HBM = pl.BlockSpec(memory_space=pltpu.HBM)
_SEM = pl.BlockSpec(memory_space=pltpu.SEMAPHORE)
N_CHIP_SEMS = 2 * (N_CHIP - 1)


def _scatter_copies(b_refs, land_refs, send_sems, recv_sems):
    x, y, c = lax.axis_index("x"), lax.axis_index("y"), lax.axis_index("c")
    me = 4 * x + 2 * y + c
    sends, recvs = [], []
    for kbits in range(1, N_DEV):
        px = 1 - x if kbits & 4 else x
        py = 1 - y if kbits & 2 else y
        pc = 1 - c if kbits & 1 else c
        peer = 4 * px + 2 * py + pc
        for b in range(len(b_refs)):
            s = (kbits - 1) * len(b_refs) + b
            sends.append(pltpu.make_async_remote_copy(
                src_ref=b_refs[b].at[peer], dst_ref=land_refs[b].at[me], send_sem=send_sems[s],
                recv_sem=recv_sems[s], device_id=(px, py, pc), device_id_type=pl.DeviceIdType.MESH))
            recvs.append(pltpu.make_async_remote_copy(
                src_ref=b_refs[b].at[peer], dst_ref=land_refs[b].at[peer], send_sem=send_sems[s],
                recv_sem=recv_sems[s], device_id=(px, py, pc), device_id_type=pl.DeviceIdType.MESH))
    return sends, recvs


def _scatter_start(name, bufs):
    nb = len(bufs)
    n = (N_DEV - 1) * nb

    def body(*refs):
        b_refs, land_refs, outs = refs[:nb], refs[nb:2 * nb], refs[2 * nb:]
        sends, _ = _scatter_copies(b_refs, land_refs, outs[:n], outs[n:2 * n])
        for cp in sends:
            cp.start()
        token = outs[2 * n + 2 * nb]
        token[...] = jnp.zeros_like(token)

    thru = tuple(pltpu.HBM(b.shape, b.dtype) for b in bufs)
    res = pl.pallas_call(
        body, name=name, in_specs=(_HBM,) * (2 * nb),
        out_specs=(_SEM,) * (2 * n) + (_HBM,) * (2 * nb) + (pl.BlockSpec(memory_space=pltpu.VMEM),),
        out_shape=(pltpu.SemaphoreType.DMA(()),) * (2 * n) + thru + thru + (jax.ShapeDtypeStruct((8, LANES), F32),),
        input_output_aliases={i: 2 * n + i for i in range(2 * nb)},
        compiler_params=pltpu.CompilerParams(has_side_effects=pltpu.SideEffectType.DATAFLOW_SIDE_EFFECTING),
    )(*[pltpu.with_memory_space_constraint(b, pltpu.HBM) for b in bufs],
      *[pltpu.with_memory_space_constraint(lax.empty(b.shape, b.dtype), pltpu.HBM) for b in bufs])
    return res[:2 * n], list(res[2 * n:2 * n + nb]), list(res[2 * n + nb:2 * n + 2 * nb]), res[2 * n + 2 * nb]


def _scatter_wait(name, sems, bufs_thru, lands_thru, after):
    nb = len(bufs_thru)
    n = (N_DEV - 1) * nb

    def body(*refs):
        b_refs, land_refs, sem_refs = refs[:nb], refs[nb:2 * nb], refs[2 * nb:2 * nb + 2 * n]
        sends, recvs = _scatter_copies(b_refs, land_refs, sem_refs[:n], sem_refs[n:])
        for cp in sends:
            cp.wait_send()
        for cp in recvs:
            cp.wait_recv()

    thru = tuple(pltpu.HBM(b.shape, b.dtype) for b in bufs_thru)
    res = pl.pallas_call(
        body, name=name, in_specs=(_HBM,) * (2 * nb) + (_SEM,) * (2 * n) + (pl.BlockSpec(memory_space=pl.ANY),),
        out_specs=(_HBM,) * (2 * nb), out_shape=thru + thru,
        input_output_aliases={i: i for i in range(2 * nb)},
        compiler_params=pltpu.CompilerParams(has_side_effects=pltpu.SideEffectType.DATAFLOW_SIDE_EFFECTING),
    )(*bufs_thru, *lands_thru, *sems, after)
    return list(res[:nb]), list(res[nb:])


def _chip_exchange_start(name, sums):
    def body(s_ref, land_ref, *outs):
        sems, token = outs[:N_CHIP_SEMS], outs[N_CHIP_SEMS + 2]
        sends, _ = _chip_copies(s_ref, land_ref, sems[:N_CHIP - 1], sems[N_CHIP - 1:])
        for cp in sends:
            cp.start()
        token[...] = jnp.zeros_like(token)

    res = pl.pallas_call(
        body, name=name, in_specs=(_HBM, _HBM),
        out_specs=(_SEM,) * N_CHIP_SEMS + (_HBM, _HBM, pl.BlockSpec(memory_space=pltpu.VMEM)),
        out_shape=(pltpu.SemaphoreType.DMA(()),) * N_CHIP_SEMS
        + (pltpu.HBM(sums.shape, sums.dtype), pltpu.HBM(sums.shape, sums.dtype), jax.ShapeDtypeStruct((8, LANES), F32)),
        input_output_aliases={0: N_CHIP_SEMS, 1: N_CHIP_SEMS + 1},
        compiler_params=pltpu.CompilerParams(has_side_effects=pltpu.SideEffectType.DATAFLOW_SIDE_EFFECTING),
    )(pltpu.with_memory_space_constraint(sums, pltpu.HBM),
      pltpu.with_memory_space_constraint(lax.empty(sums.shape, sums.dtype), pltpu.HBM))
    return res[:N_CHIP_SEMS], res[N_CHIP_SEMS], res[N_CHIP_SEMS + 1], res[N_CHIP_SEMS + 2]


def _chip_exchange_wait(name, sems, sums_thru, land_thru, after):
    def body(s_ref, land_ref, *rest):
        sems = rest[:N_CHIP_SEMS]
        sends, recvs = _chip_copies(s_ref, land_ref, sems[:N_CHIP - 1], sems[N_CHIP - 1:])
        for cp in sends:
            cp.wait_send()
        for cp in recvs:
            cp.wait_recv()

    return pl.pallas_call(
        body, name=name, in_specs=(_HBM, _HBM) + (_SEM,) * N_CHIP_SEMS + (pl.BlockSpec(memory_space=pl.ANY),),
        out_specs=(_HBM, _HBM),
        out_shape=(pltpu.HBM(sums_thru.shape, sums_thru.dtype), pltpu.HBM(sums_thru.shape, sums_thru.dtype)),
        input_output_aliases={0: 0, 1: 1},
        compiler_params=pltpu.CompilerParams(has_side_effects=pltpu.SideEffectType.DATAFLOW_SIDE_EFFECTING),
    )(sums_thru, land_thru, *sems, after)


def _adamw(name, slots, w, m, v, tr=256):
    unit_mid = w.ndim == 3 and w.shape[1] == 1 and w.shape[0] > 1
    R, Wd = (w.shape[0], w.shape[2]) if unit_mid else w.shape[-2:]
    depth_axis = w.ndim == 3 and not unit_mid
    if R % tr == 0:
        tc = Wd
    else:
        tr, tc = R, (256 if (Wd % 256 == 0 and R > 256) else Wd)
    at = (slice(None), 0, slice(None)) if unit_mid else Ellipsis

    def body(s_ref, w_ref, m_ref, v_ref, g_out, d_out, m_out, v_out):
        g = s_ref[0].astype(F32)
        for j in range(1, slots.shape[0]):
            g = g + s_ref[j].astype(F32)
        m_new = ADAM_B1 * m_ref[at] + (1.0 - ADAM_B1) * g
        v_new = ADAM_B2 * v_ref[at] + (1.0 - ADAM_B2) * jnp.square(g)
        m_hat = m_new / (1.0 - ADAM_B1 ** ADAM_STEP)
        v_hat = v_new / (1.0 - ADAM_B2 ** ADAM_STEP)
        g_out[at] = g
        d_out[at] = -ADAM_LR * (m_hat / (jnp.sqrt(v_hat) + ADAM_EPS) + ADAM_WD * w_ref[at])
        m_out[at] = m_new
        v_out[at] = v_new

    if unit_mid:
        row = pl.BlockSpec((tr, 1, tc), lambda i, j: (i, 0, j))
    elif depth_axis:
        row = pl.BlockSpec((None, tr, tc), lambda i, j: (0, i, j))
    else:
        row = pl.BlockSpec((tr, tc), lambda i, j: (i, j))
    return pl.pallas_call(
        body, name=name, grid=(R // tr, Wd // tc),
        in_specs=[pl.BlockSpec((slots.shape[0], tr, tc), lambda i, j: (0, i, j)), row, row, row],
        out_specs=[row] * 4, out_shape=[jax.ShapeDtypeStruct(w.shape, F32)] * 4,
        compiler_params=pltpu.CompilerParams(dimension_semantics=("parallel", "parallel"),
                                             vmem_limit_bytes=VMEM_LIMIT),
    )(slots, w, m, v)


PACK_W = 1024
PACKED = [(n, s) for n, s in REPLICATED if n != "sgu_w"]
_SMALL_SIZES = [int(np.prod(s)) for _, s in PACKED]
_SMALL_ROWS = _round_up(_round_up(sum(_SMALL_SIZES) + PACK_W, PACK_W) // PACK_W, 8)
_LOSS_AT = sum(_SMALL_SIZES)
W_IN_SHARD = P_TOTAL // N_DEV


def _pack_rows(parts, rows, dtype):
    flat = jnp.concatenate([p.reshape(-1).astype(dtype) for p in parts])
    return jnp.pad(flat, (0, rows * PACK_W - flat.shape[0])).reshape(rows, PACK_W)


def _w_in_groups_t(blocks):
    wt = blocks.reshape(P_TOTAL, D)
    o, c = 2 * D, 2 * D + 3 * D
    z = lambda r: jnp.zeros((r, D), wt.dtype)
    rw = jnp.concatenate([wt[o:c], wt[c:c + L_W], z(128 - L_W), wt[c + L_W:c + L_W + L_A], z(128 - L_A),
                          wt[c + L_W + L_A:o + C_B], z(256 - L_G)], axis=0)
    return wt[:o], rw, wt[o + C_B:]


def _w_in_grad_blocks(g_sgu_t, g_rw_t, g_gate_t):
    c = 3 * D
    full = jnp.concatenate([g_sgu_t, g_rw_t[:c], g_rw_t[c:c + L_W], g_rw_t[c + 128:c + 128 + L_A],
                            g_rw_t[c + 256:c + 256 + L_G], g_gate_t], axis=0)
    return full.reshape(N_DEV, W_IN_SHARD, D)


def _mesh_index():
    me = 4 * lax.axis_index("x") + 2 * lax.axis_index("y") + lax.axis_index("c")
    return me.astype(jnp.int32).reshape(1)


def _fill_slot(name, dst, src, idx, src_idx=None):
    R, Wd = dst.shape[1:]
    scalars = [idx] if src_idx is None else [idx, src_idx]
    if src_idx is None:
        src_spec = pl.BlockSpec((R, Wd), lambda i, *s: (0, 0))
    else:
        src_spec = pl.BlockSpec((None, R, Wd), lambda i, *s: (s[1][0], 0, 0))

    def body(*refs):
        src_ref, out_ref = refs[len(scalars) + 1], refs[len(scalars) + 2]
        out_ref[...] = src_ref[...]

    return pl.pallas_call(
        body, name=name,
        grid_spec=pltpu.PrefetchScalarGridSpec(
            num_scalar_prefetch=len(scalars), grid=(1,),
            in_specs=[pl.BlockSpec(memory_space=pl.ANY), src_spec],
            out_specs=pl.BlockSpec((None, R, Wd), lambda i, *s: (s[0][0], 0, 0))),
        out_shape=jax.ShapeDtypeStruct(dst.shape, dst.dtype),
        input_output_aliases={len(scalars): 0},
        compiler_params=pltpu.CompilerParams(vmem_limit_bytes=VMEM_LIMIT),
    )(*scalars, dst, src)


class TwoLevelGather:
    def __init__(self, bufs, skip_own=()):
        self.bufs, self.nb, self.skip_own = list(bufs), len(bufs), tuple(skip_own)
        self.any_specs = [pl.BlockSpec(memory_space=pl.ANY)] * self.nb
        self.out_shape = [jax.ShapeDtypeStruct((N_DEV,) + b.shape, b.dtype) for b in self.bufs]
        self.sem_shapes = [pltpu.SemaphoreType.DMA((7 * self.nb,)), pltpu.SemaphoreType.DMA((7 * self.nb,)),
                           pltpu.SemaphoreType.DMA((self.nb,))]

    def _copies(self, in_refs, out_refs, sems):
        send_sems, recv_sems, local_sems = sems
        nb = self.nb
        x, y, c = lax.axis_index("x"), lax.axis_index("y"), lax.axis_index("c")
        me, sibling = (x, y, c), (x, y, 1 - c)
        chips = [(1 - x, y), (x, 1 - y), (1 - x, 1 - y)]

        def slot(b, dev):
            return out_refs[b].at[4 * dev[0] + 2 * dev[1] + dev[2]]

        def copy(b, k, block, to, own=False):
            return pltpu.make_async_remote_copy(
                src_ref=in_refs[b] if own else slot(b, block), dst_ref=slot(b, block),
                send_sem=send_sems.at[7 * b + k], recv_sem=recv_sems.at[7 * b + k], device_id=to,
                device_id_type=pl.DeviceIdType.MESH)

        cp = {}
        cp["local"] = [pltpu.make_async_copy(in_refs[b], slot(b, me), local_sems.at[b]) for b in range(nb)
                       if b not in self.skip_own]
        cp["first"] = [copy(b, 0, me, sibling, own=True) for b in range(nb)]
        cp["first"] += [copy(b, 1 + j, me, (*chip, c), own=True) for j, chip in enumerate(chips) for b in range(nb)]
        cp["over_ici"] = [copy(b, 1 + j, (*chip, c), me) for j, chip in enumerate(chips) for b in range(nb)]
        cp["passed"] = [copy(b, 4 + j, (*chip, c), sibling) for j, chip in enumerate(chips) for b in range(nb)]
        cp["from_sibling"] = [copy(b, 0, sibling, me) for b in range(nb)]
        cp["from_sibling"] += [copy(b, 4 + j, (*chip, 1 - c), me) for j, chip in enumerate(chips) for b in range(nb)]
        return cp

    def start(self, in_refs, out_refs, sems):
        cp = self._copies(in_refs, out_refs, sems)
        for c in cp["first"] + cp["local"]:
            c.start()

    def forward(self, in_refs, out_refs, sems):
        cp = self._copies(in_refs, out_refs, sems)
        for arrived, onward in zip(cp["over_ici"], cp["passed"]):
            arrived.wait_recv()
            onward.start()

    def finish(self, in_refs, out_refs, sems):
        cp = self._copies(in_refs, out_refs, sems)
        for c in cp["from_sibling"]:
            c.wait_recv()
        for c in cp["first"] + cp["passed"]:
            c.wait_send()
        for c in cp["local"]:
            c.wait()

    def schedule(self, n_steps):
        return [(0, self.start), (max(n_steps - 2, 0), self.forward), (n_steps - 1, self.finish)]


def _all_gather_two_level(name, bufs, skip_own=()):
    ex = TwoLevelGather(bufs, skip_own)

    def body(*refs):
        args = refs[:ex.nb], refs[ex.nb:2 * ex.nb], refs[2 * ex.nb:]
        ex.start(*args)
        ex.forward(*args)
        ex.finish(*args)

    return pl.pallas_call(body, name=name, in_specs=ex.any_specs, out_specs=ex.any_specs, out_shape=ex.out_shape,
                          scratch_shapes=ex.sem_shapes)(*ex.bufs)


def _cols_from_blocks(blk):
    return jnp.transpose(blk, (1, 0, 2)).reshape(blk.shape[1], -1)


def _cols_to_blocks(g):
    r, c = g.shape
    return jnp.transpose(g.reshape(r, N_DEV, c // N_DEV), (1, 0, 2))


FIRST_WEIGHTS = ["w_in", "shift_b", "w_lora_w", "a_lora_w", "g_lora_w"]
LATE_WEIGHTS = ["w_proj_a", "w_proj_b", "w_out", "w_ffn1", "w_ffn2"]
SCAN_CARRIED = ["w_proj_a", "w_proj_b", "w_out"]
FFN_WEIGHTS = ["w_ffn1", "w_ffn2"]


def _late_weights(shards):
    ex = TwoLevelGather([shards[n].astype(BF16) for n in LATE_WEIGHTS])

    def finish(results):
        got = dict(zip(LATE_WEIGHTS, results))
        W = {n: got[n].reshape(-1, D) for n in ("w_proj_a", "w_proj_b", "w_out", "w_ffn2")}
        W["w_ffn1"] = got["w_ffn1"].reshape(N_DEV, D, -1)
        return W
    return ex, finish


def _gather_weights(shards):
    def payload(n):
        if n == "w_in":
            return jnp.transpose(shards[n][0]).astype(BF16)
        return shards[n] if n == "shift_b" else shards[n].astype(BF16)
    payloads = [payload(n) for n in FIRST_WEIGHTS]
    got = dict(zip(FIRST_WEIGHTS, _all_gather_two_level("weight_all_gather", payloads, skip_own=(0,))))
    got["w_in"] = _fill_slot("w_in_own_slot", got["w_in"], payloads[0], _mesh_index())
    W = {}
    W["w_sgu_t"], W["w_rw_t"], W["w_gate_t"] = _w_in_groups_t(got["w_in"])
    z = lambda r, c, dt: jnp.zeros((r, c), dt)
    W["w_lora"] = jnp.concatenate([_cols_from_blocks(got["w_lora_w"][:, 0]).astype(F32), z(128 - L_W, D, F32)], axis=0)
    W["a_lora"] = jnp.concatenate([_cols_from_blocks(got["a_lora_w"][:, 0]).astype(F32), z(128 - L_A, D, F32)], axis=0)
    W["g_lora"] = jnp.concatenate([_cols_from_blocks(got["g_lora_w"][:, 0]).astype(F32), z(256 - L_G, D, F32)], axis=0)
    sb = _cols_from_blocks(got["shift_b"][:, 0])
    W["sb"] = jnp.concatenate([sb[:, :3 * D], sb[:, 3 * D:3 * D + L_W], z(2, 128 - L_W, F32),
                               sb[:, 3 * D + L_W:3 * D + L_W + L_A], z(2, 128 - L_A, F32),
                               sb[:, 3 * D + L_W + L_A:], z(2, 256 - L_G, F32)], axis=1)
    return W


def _replicated_weights(rep):
    W = {n: rep[n] for n in ("g_mix", "sgu_ln_w", "sgu_ln_b", "w0", "a0", "k_k", "k_a", "r_k", "ln_x_w", "ln_x_b",
                             "g_ffn")}
    W["g_final"] = rep["g_final"].reshape(1, D)
    W["sgu_w"] = rep["sgu_w"][0]
    W["sgu_bt"] = jnp.transpose(rep["sgu_b"][0])
    return W


def _late_grad_blocks(G):
    return Exchange([G[n].reshape(N_DEV, -1, D) for n in SCAN_CARRIED]
                    + [G["sgu_w"].reshape(SGU_G * SGU_C, SGU_C).astype(GRAD_PAYLOAD)],
                    [False] * len(SCAN_CARRIED) + [True])


def _first_grad_blocks(G):
    sbg = G["sb"]
    c = 3 * D
    sb = jnp.concatenate([sbg[:, :c], sbg[:, c:c + L_W], sbg[:, c + 128:c + 128 + L_A],
                          sbg[:, c + 256:c + 256 + L_G]], axis=1)
    return {
        "shift_b": _cols_to_blocks(sb),
        "w_lora_w": _cols_to_blocks(G["w_lora"][:L_W]), "a_lora_w": _cols_to_blocks(G["a_lora"][:L_A]),
        "g_lora_w": _cols_to_blocks(G["g_lora"][:L_G]),
    }


def _replicated_grads(G):
    small = {n: G[n] for n in ("g_mix", "sgu_ln_w", "sgu_ln_b", "w0", "a0", "k_k", "k_a", "r_k", "ln_x_w", "ln_x_b",
                               "g_ffn", "g_final")}
    small["sgu_w"] = G["sgu_w"]
    small["sgu_b"] = jnp.transpose(G["sgu_bt"])
    return small


def kernel(x, g_mix, w_in, sgu_ln_w, sgu_ln_b, sgu_w, sgu_b, w_proj_a, shift_b, w_lora_w, w0, a_lora_w, a0, g_lora_w, k_k, k_a, r_k, ln_x_w, ln_x_b, w_proj_b, w_out, g_ffn, w_ffn1, w_ffn2, g_final, loss_target, m_g_mix, m_w_in, m_sgu_ln_w, m_sgu_ln_b, m_sgu_w, m_sgu_b, m_w_proj_a, m_shift_b, m_w_lora_w, m_w0, m_a_lora_w, m_a0, m_g_lora_w, m_k_k, m_k_a, m_r_k, m_ln_x_w, m_ln_x_b, m_w_proj_b, m_w_out, m_g_ffn, m_w_ffn1, m_w_ffn2, m_g_final, v_g_mix, v_w_in, v_sgu_ln_w, v_sgu_ln_b, v_sgu_w, v_sgu_b, v_w_proj_a, v_shift_b, v_w_lora_w, v_w0, v_a_lora_w, v_a0, v_g_lora_w, v_k_k, v_k_a, v_r_k, v_ln_x_w, v_ln_x_b, v_w_proj_b, v_w_out, v_g_ffn, v_w_ffn1, v_w_ffn2, v_g_final):
    env = dict(locals())
    weights = {n: env[n] for n in WEIGHT_ORDER}
    moms = {n: env["m_" + n] for n in WEIGHT_ORDER}
    vars_ = {n: env["v_" + n] for n in WEIGHT_ORDER}

    shards = {n: weights[n] for n, _, _ in SHARDED}
    W = _gather_weights(shards)
    W.update(_replicated_weights({n: weights[n] for n, _ in REPLICATED}))
    in_flight = {}

    def send_w_in_grads(G):
        blocks = _w_in_grad_blocks(G["w_sgu_t"], G["w_rw_t"], G["w_gate_t"])
        got = _pair_exchange("grad_pair_exchange", blocks)
        core = lax.axis_index("c").astype(jnp.int32).reshape(1)
        sums = _pair_sum("grad_pair_sum", blocks, got, core)
        in_flight["sems"], in_flight["sums"], in_flight["land"], token = _chip_exchange_start("grad_chip_start", sums)
        return token

    def send_ffn_grads(G):
        in_flight["ffn"] = _scatter_start("grad_ffn_start", [G["w_ffn1"], G["w_ffn2"].reshape(N_DEV, -1, D)])
        return in_flight["ffn"][3]

    loss_part, dx, G, late_slots = _local_step(x[0], loss_target[0], W, late_weights=_late_weights(shards),
                                               early_grads=_late_grad_blocks, w_in_grads_ready=send_w_in_grads,
                                               ffn_grads_ready=send_ffn_grads)

    slots = dict(zip(SCAN_CARRIED, late_slots))
    me = _mesh_index()
    sent, landed = _scatter_wait("grad_ffn_wait", *in_flight["ffn"][:3], after=dx)
    for i, n in enumerate(FFN_WEIGHTS):
        slots[n] = _fill_slot("grad_own_slot_" + n, landed[i], sent[i], me, src_idx=me)
    blocks = _first_grad_blocks(G)
    small = _replicated_grads(G)
    small_parts = [small[n] for n, _ in PACKED] + [jnp.full((PACK_W,), loss_part, F32)]
    rest = [n for n in FIRST_WEIGHTS if n != "w_in"]
    res = _exchange("grad_exchange", [blocks[n] for n in rest] + [_pack_rows(small_parts, _SMALL_ROWS, F32)],
                    [False] * len(rest) + [True])
    slots.update(zip(rest, res[:-1]))
    small_slots = res[-1]
    sums, chip_slots = _chip_exchange_wait("grad_chip_wait", in_flight["sems"], in_flight["sums"], in_flight["land"],
                                           after=small_slots)
    my_chip = (2 * lax.axis_index("x") + lax.axis_index("y")).astype(jnp.int32).reshape(1)
    slots["w_in"] = _fill_slot("grad_own_slot", chip_slots, sums, my_chip, src_idx=my_chip)

    outs = [dict(), dict(), dict(), dict()]
    for n, _, _ in SHARDED:
        if n == "w_in":
            res = _adamw("adamw_" + n, slots[n], *[jnp.transpose(t, (2, 0, 1)) for t in (weights[n], moms[n], vars_[n])])
            res = [jnp.transpose(t, (1, 2, 0)) for t in res]
        else:
            res = _adamw("adamw_" + n, slots[n], weights[n], moms[n], vars_[n])
        for k in range(4):
            outs[k][n] = res[k]

    sgu_shape = (SGU_G * SGU_C, SGU_C)
    res = _adamw("adamw_sgu_w", late_slots[len(SCAN_CARRIED)], *[t.reshape(sgu_shape) for t in
                                                                  (weights["sgu_w"], moms["sgu_w"], vars_["sgu_w"])])
    for k in range(4):
        outs[k]["sgu_w"] = res[k].reshape(weights["sgu_w"].shape)

    def packed(d):
        return _pack_rows([d[n] for n, _ in PACKED], _SMALL_ROWS, F32)
    small_out = _adamw("adamw_replicated", small_slots, packed(weights), packed(moms), packed(vars_))
    for k in range(4):
        flat = small_out[k].reshape(-1)
        off = 0
        for (n, s), size in zip(PACKED, _SMALL_SIZES):
            outs[k][n] = flat[off:off + size].reshape(s)
            off += size
    loss = small_out[0].reshape(-1)[_LOSS_AT]
    return (loss, dx[None], *[outs[0][n] for n in WEIGHT_ORDER], *[outs[1][n] for n in WEIGHT_ORDER],
            *[outs[2][n] for n in WEIGHT_ORDER], *[outs[3][n] for n in WEIGHT_ORDER])
```

```python
import functools
import numpy as np
import jax
import jax.numpy as jnp
from jax import lax
from jax.experimental import pallas as pl
from jax.experimental.pallas import tpu as pltpu

F32 = jnp.float32
BF16 = jnp.bfloat16

D = 1024
NH, HN = 16, 64
NP, PW = NH // 2, 2 * HN
SGU_G, SGU_C = 8, 128
L_W, L_A, L_G = 64, 64, 160
C_B = 3 * D + L_W + L_A + L_G
P_TOTAL = 2 * D + C_B + 2 * D
D_FF = 4 * D
RW_INT = 3 * D + 128 + 128 + 256
NORM_EPS, LN_EPS, GN_EPS = 1e-6, 1e-5, 64e-5
N_DEV = 8
LANES = 128
SCAN_C = 64
N_KEPT = 4
SOLVE_B = 16
SCAN_PRECISION = lax.Precision.HIGH
SCAN_OUT_PRECISION = lax.Precision.DEFAULT
GRAD_PAYLOAD = BF16
VMEM_LIMIT = 56 * 1024 * 1024
MATMUL_VMEM_BUDGET = 40 * 1024 * 1024
STEP_COST_BYTES = 512 * 1024
HBM_COST_RATIO = 3
ACC_PASS_WEIGHT = 4

ADAM_LR, ADAM_B1, ADAM_B2, ADAM_EPS, ADAM_WD, ADAM_STEP = 0.001, 0.9, 0.999, 1e-08, 0.01, 10

SHARDED = [
    ("w_in", (D, P_TOTAL), 1), ("w_proj_a", (D, D), 0), ("shift_b", (2, C_B), 1), ("w_lora_w", (L_W, D), 1),
    ("a_lora_w", (L_A, D), 1), ("g_lora_w", (L_G, D), 1), ("w_proj_b", (D, D), 0), ("w_out", (D, D), 0),
    ("w_ffn1", (D, D_FF), 1), ("w_ffn2", (D_FF, D), 0),
]
REPLICATED = [
    ("g_mix", (1, D)), ("sgu_ln_w", (1, D)), ("sgu_ln_b", (1, D)), ("sgu_w", (1, SGU_G, SGU_C, SGU_C)),
    ("sgu_b", (1, SGU_G, SGU_C)), ("w0", (1, D)), ("a0", (1, D)), ("k_k", (1, D)), ("k_a", (1, D)), ("r_k", (1, D)),
    ("ln_x_w", (1, D)), ("ln_x_b", (1, D)), ("g_ffn", (1, D)), ("g_final", (D,)),
]
WEIGHT_ORDER = ["g_mix", "w_in", "sgu_ln_w", "sgu_ln_b", "sgu_w", "sgu_b", "w_proj_a", "shift_b", "w_lora_w", "w0",
                "a_lora_w", "a0", "g_lora_w", "k_k", "k_a", "r_k", "ln_x_w", "ln_x_b", "w_proj_b", "w_out", "g_ffn",
                "w_ffn1", "w_ffn2", "g_final"]


def _round_up(n, m):
    return (n + m - 1) // m * m


def _pick(n, target):
    if n <= target:
        return n
    best = None
    for t in range(LANES, target + 1, LANES):
        if n % t == 0:
            best = t
    assert best is not None, (n, target)
    return best


def _matmul(name, a, b, mode, out_dtype=F32, tm=2048, tn=1024, tk=4096, out_blocks=None, epilogue=None, extras=(),
            out_dtypes=(), after=None, whole_rows=False, out_widths=None):
    b_blocks = b.shape[0] if b.ndim == 3 else None
    bshape = b.shape if b.ndim == 2 else (b.shape[1], b.shape[0] * b.shape[2])
    if mode == "nn":
        (M, K), (K2, N) = a.shape, bshape
    elif mode == "nt":
        (M, K), (N, K2) = a.shape, bshape
    else:
        (K, M), (K2, N) = a.shape, bshape
    assert K == K2, (name, a.shape, b.shape)
    assert b_blocks is None or mode != "tn"
    assert out_blocks is None or mode == "tn"
    tn = min(tn, N // (out_blocks or 1), bshape[1] // b_blocks if (b_blocks and mode == "nn") else tn)
    blocked_k = bool(b_blocks) and mode == "nt"
    tm, tn, tk = _pick(M, tm), _pick(N, tn), (K if blocked_k else _pick(K, tk))

    def vmem_bytes(tm, tk):
        tiles = tm * tk * a.dtype.itemsize + tk * tn * b.dtype.itemsize
        for i, dt in enumerate(out_dtypes if epilogue else (out_dtype,)):
            tiles += tm * (out_widths[i] if out_widths else tn) * jnp.dtype(dt).itemsize
        for x in extras:
            arr = x[0] if isinstance(x, tuple) else x
            tiles += (tm if arr.shape[0] > 1 else 1) * tn * arr.dtype.itemsize
        return 2 * tiles + (tm * tn * 4 if K // tk > 1 else 0)

    def cost(tm, tk):
        ni, nj, nk = M // tm, N // tn, K // tk
        steps = ni * nj * nk
        acc_passes = steps * tm * tn * 8 * ACC_PASS_WEIGHT if nk > 1 else 0
        a_reads = M * K * a.dtype.itemsize * (nj if nk > 1 else 1)
        b_reads = K * N * b.dtype.itemsize * (ni if (nj > 1 or nk > 1) else 1)
        return (steps * STEP_COST_BYTES + acc_passes + vmem_bytes(tm, tk) // 2
                + HBM_COST_RATIO * (a_reads + b_reads))

    options = [(m, k) for m in ({M} if whole_rows else {_pick(M, max(t, LANES)) for t in (tm, tm // 2, tm // 4)})
               for k in ({K} if blocked_k else {_pick(K, max(t, LANES)) for t in (tk, tk // 2, tk // 4)})
               if vmem_bytes(m, k) <= MATMUL_VMEM_BUDGET]
    tm, tk = min(options, key=lambda o: cost(*o))
    nk = K // tk
    dims = {"nn": (((1,), (0,)), ((), ())), "nt": (((1,), (1,)), ((), ())), "tn": (((0,), (0,)), ((), ()))}[mode]

    n_x, n_o = len(extras), len(out_dtypes) if epilogue else 1
    n_after = 0 if after is None else 1

    def body(a_ref, b_ref, *rest):
        x_refs, o_refs, acc = rest[:n_x], rest[n_x + n_after:n_x + n_after + n_o], rest[n_x + n_after + n_o:]
        if blocked_k:
            bw = b.shape[2]
            part = sum(lax.dot_general(a_ref[:, blk * bw:(blk + 1) * bw].astype(BF16), b_ref[blk].astype(BF16), dims,
                                       preferred_element_type=F32) for blk in range(b_blocks))
        else:
            part = lax.dot_general(a_ref[...].astype(BF16), b_ref[...].astype(BF16), dims, preferred_element_type=F32)

        def finish(res):
            outs = epilogue(res, *[r[...] for r in x_refs]) if epilogue else (res,)
            for r, v in zip(o_refs, outs):
                r[...] = v.astype(r.dtype)

        if nk == 1:
            finish(part)
            return
        acc_ref, k = acc[0], pl.program_id(2)

        @pl.when(k == 0)
        def _():
            acc_ref[...] = part

        @pl.when(k > 0)
        def _():
            acc_ref[...] += part

        @pl.when(k == nk - 1)
        def _():
            finish(acc_ref[...])

    a_spec = {"nn": pl.BlockSpec((tm, tk), lambda i, j, k: (i, k)), "nt": pl.BlockSpec((tm, tk), lambda i, j, k: (i, k)),
              "tn": pl.BlockSpec((tk, tm), lambda i, j, k: (k, i))}[mode]
    b_spec = {"nn": pl.BlockSpec((tk, tn), lambda i, j, k: (k, j)), "nt": pl.BlockSpec((tn, tk), lambda i, j, k: (j, k)),
              "tn": pl.BlockSpec((tk, tn), lambda i, j, k: (k, j))}[mode]
    if b_blocks and mode == "nn":
        per = b.shape[2] // tn
        b_spec = pl.BlockSpec((None, tk, tn), lambda i, j, k: (j // per, k, j % per))
    elif b_blocks:
        b_spec = pl.BlockSpec((b_blocks, tn, b.shape[2]), lambda i, j, k: (0, j, 0))
    out_spec = pl.BlockSpec((tm, tn), lambda i, j, k: (i, j))
    out_shape = jax.ShapeDtypeStruct((M, N), out_dtype)
    if out_blocks:
        per_o = N // out_blocks // tn
        out_spec = pl.BlockSpec((None, tm, tn), lambda i, j, k: (j // per_o, i, j % per_o))
        out_shape = jax.ShapeDtypeStruct((out_blocks, M, N // out_blocks), out_dtype)
    epi_widths = list(out_widths) if out_widths else [N] * n_o
    assert all(w == N for w in epi_widths) or N == tn
    epi_specs = [pl.BlockSpec((tm, tn if w == N else w), lambda i, j, k: (i, j)) for w in epi_widths]
    x_specs, x_args = [], []
    for x in extras:
        arr, off = x if isinstance(x, tuple) else (x, 0)
        if arr.shape[0] == 1:
            x_specs.append(pl.BlockSpec((1, tn), lambda i, j, k: (0, j)))
        else:
            x_specs.append(pl.BlockSpec((tm, tn), lambda i, j, k, off=off: (i, j + off)))
        x_args.append(arr)
    res = pl.pallas_call(
        body, name=name, grid=(M // tm, N // tn, nk),
        in_specs=[a_spec, b_spec] + x_specs + [pl.BlockSpec(memory_space=pl.ANY)] * n_after,
        out_specs=epi_specs if epilogue else out_spec,
        out_shape=[jax.ShapeDtypeStruct((M, w), dt) for w, dt in zip(epi_widths, out_dtypes)] if epilogue else out_shape,
        scratch_shapes=[pltpu.VMEM((tm, tn), F32)] if nk > 1 else [],
        compiler_params=pltpu.CompilerParams(dimension_semantics=("parallel", "parallel", "arbitrary"),
                                             vmem_limit_bytes=VMEM_LIMIT),
    )(a, b, *x_args, *([after] if n_after else []))
    return res


class Rows:
    def __init__(self, arr, width=None, cb=0):
        self.arr, self.width, self.cb = arr, (arr.shape[1] if width is None else width), cb


class Heads:
    def __init__(self, arr):
        self.arr = arr


class Halo:
    def __init__(self, arr, side):
        self.arr, self.side = arr, side


def _rows_call(name, fn, ins, consts, outs, accs=(), tm=512, with_pid=False):
    T = next(o.arr.shape[1] if isinstance(o, Heads) else o.arr.shape[0] for o in ins if not isinstance(o, Halo))
    tm = min(tm, T)
    n_tiles = T // tm
    n_in, n_c, n_out = len(ins), len(consts), len(outs)
    in_specs, args = [], []
    for o in ins:
        if isinstance(o, Rows):
            in_specs.append(pl.BlockSpec((tm, o.width), lambda i, cb=o.cb: (i, cb)))
        elif isinstance(o, Heads):
            in_specs.append(pl.BlockSpec((NP, tm, PW), lambda i: (0, i, 0)))
        else:
            w = o.arr.shape[1]
            if o.side < 0:
                in_specs.append(pl.BlockSpec((8, w), lambda i: (jnp.maximum(i * (tm // 8) - 1, 0), 0)))
            else:
                in_specs.append(pl.BlockSpec((8, w), lambda i: (jnp.minimum((i + 1) * (tm // 8), T // 8 - 1), 0)))
        args.append(o.arr)
    for c in consts:
        in_specs.append(pl.BlockSpec(c.shape, lambda i, nd=c.ndim: (0,) * nd))
        args.append(c)
    out_specs, out_shape = [], []
    for o in outs:
        if o[0] == "rows":
            out_specs.append(pl.BlockSpec((tm, o[1]), lambda i: (i, 0)))
            out_shape.append(jax.ShapeDtypeStruct((T, o[1]), o[2]))
        else:
            out_specs.append(pl.BlockSpec((NP, tm, PW), lambda i: (0, i, 0)))
            out_shape.append(jax.ShapeDtypeStruct((NP, T, PW), o[1]))
    for shape, dt in accs:
        out_specs.append(pl.BlockSpec(shape, lambda i, nd=len(shape): (0,) * nd))
        out_shape.append(jax.ShapeDtypeStruct(shape, dt))

    def body(*refs):
        i = pl.program_id(0)
        vals = []
        vals = [r[...] for r in refs[:n_in + n_c]]
        res = fn(i, n_tiles, *vals) if with_pid else fn(*vals)
        out_refs = refs[n_in + n_c:]
        for r, v in zip(out_refs[:n_out], res[:n_out]):
            r[...] = v.astype(r.dtype)
        if accs:
            @pl.when(i == 0)
            def _():
                for r in out_refs[n_out:]:
                    r[...] = jnp.zeros_like(r)

            for r, v in zip(out_refs[n_out:], res[n_out:]):
                r[...] += v.astype(r.dtype)

    res = pl.pallas_call(
        body, name=name, grid=(n_tiles,), in_specs=in_specs, out_specs=out_specs, out_shape=out_shape,
        compiler_params=pltpu.CompilerParams(dimension_semantics=("arbitrary",), vmem_limit_bytes=VMEM_LIMIT),
    )(*args)
    return res


def _rms(x, g):
    return x * lax.rsqrt(jnp.mean(x * x, axis=-1, keepdims=True) + NORM_EPS) * g


def _gelu(x):
    return 0.5 * x * (1.0 + lax.erf(x * 0.7071067811865476))


def _sigmoid(x):
    return 1.0 / (1.0 + jnp.exp(-x))


def _bdot(a, b):
    return jnp.dot(a.astype(BF16), b.astype(BF16), preferred_element_type=F32)


def _to_heads(x):
    return jnp.concatenate([x[:, p * PW:(p + 1) * PW][None] for p in range(NP)], axis=0)


def _from_heads(xp):
    return jnp.concatenate([xp[p] for p in range(NP)], axis=-1)


def _head_sum(xp):
    low = lax.broadcasted_iota(jnp.int32, xp.shape, xp.ndim - 1) < HN
    both = jnp.sum(xp, axis=-1, keepdims=True)
    first = jnp.sum(jnp.where(low, xp, 0.0), axis=-1, keepdims=True)
    return jnp.where(low, first, both - first)


def _split_pairs(xp):
    return jnp.concatenate([xp[:, :, :HN], xp[:, :, HN:]], axis=0)


def _join_pairs(xh):
    return jnp.concatenate([xh[:NP], xh[NP:]], axis=-1)


def _sgu_fn(p, ln_w, ln_b, sw, sbt):
    z = _gelu(p)
    u, v = z[:, :D], z[:, D:]
    mu = jnp.mean(v, axis=-1, keepdims=True)
    var = jnp.mean(jnp.square(v - mu), axis=-1, keepdims=True)
    vn = (v - mu) * lax.rsqrt(var + LN_EPS) * ln_w + ln_b
    ri = lax.broadcasted_iota(jnp.int32, (SGU_C, SGU_C), 0)
    ci = lax.broadcasted_iota(jnp.int32, (SGU_C, SGU_C), 1)
    mask = (ci <= ri).astype(F32)
    dg = D // SGU_G
    parts = []
    for g in range(SGU_G):
        parts.append(_bdot(sw[g] * mask, vn[:, g * dg:(g + 1) * dg]) + sbt[:, g:g + 1])
    return u * jnp.concatenate(parts, axis=-1)


def _pre_fn(qr, qk, qv, qxw, qxa, qxg, wl, w0, al, a0, gl, k_k, k_a):
    w = -jax.nn.softplus(-(w0 + _bdot(jnp.tanh(qxw), wl))) - 0.5
    lw = -jnp.exp(w)
    aa = _sigmoid(a0 + _bdot(qxa, al))
    g = _bdot(_sigmoid(qxg), gl)
    kk = _to_heads(qk * k_k)
    kk = kk / jnp.maximum(jnp.sqrt(_head_sum(kk * kk)), 1e-12)
    k2 = qk * (1.0 + (aa - 1.0) * k_a)
    return _to_heads(qr), _to_heads(lw), _to_heads(k2), _to_heads(qv), kk, _to_heads(aa), g


def _post_fn(o, r, k2, v, g, ln_w, ln_b, r_k):
    mu = _head_sum(o) * (1.0 / HN)
    d = o - mu
    var = _head_sum(d * d) * (1.0 / HN)
    on = d * lax.rsqrt(var + GN_EPS) * ln_w + ln_b
    bonus = _head_sum(r * k2 * r_k) * v
    return _from_heads(on + bonus) * g


def _gate_fn(pg, ya, yb):
    return _sigmoid(pg[:, :D]) * ya + _sigmoid(pg[:, D:]) * yb


def _bmm(x, y, cx, cy, out_path=False):
    return lax.dot_general(x, y, (((cx,), (cy,)), ((0,), (0,))),
                           precision=SCAN_OUT_PRECISION if out_path else SCAN_PRECISION, preferred_element_type=F32)


def _unit_lower_inverse(M):
    C = M.shape[1]
    ti = lax.broadcasted_iota(jnp.int32, (C, C), 0)
    tj = lax.broadcasted_iota(jnp.int32, (C, C), 1)
    eye = (ti == tj).astype(F32)
    same = lambda b: (ti // b == tj // b).astype(F32)
    X = -(M * same(SOLVE_B))
    inv = eye + X
    span = 1
    while 2 * span < SOLVE_B:
        X = _bmm(X, X, 2, 1)
        inv = inv + _bmm(inv, X, 2, 1)
        span *= 2
    b = SOLVE_B
    while b < C:
        low = M * (same(2 * b) - same(b))
        inv = inv - _bmm(_bmm(inv, low, 2, 1, out_path=True), inv, 2, 1, out_path=True)
        b *= 2
    return inv


@jax.custom_vjp
def _unit_lower_solve(inv, M, y):
    return _bmm(inv, y, 2, 1)


def _unit_lower_solve_fwd(inv, M, y):
    u = _bmm(inv, y, 2, 1)
    return u, (inv, u)


def _unit_lower_solve_bwd(res, du):
    inv, u = res
    dy = _bmm(inv, du, 1, 1)
    return jnp.zeros_like(inv), -_bmm(dy, u, 2, 2), dy


_unit_lower_solve.defvjp(_unit_lower_solve_fwd, _unit_lower_solve_bwd)


@jax.custom_vjp
def _unit_lower_solved(inv, u, M, y):
    return u


_unit_lower_solved.defvjp(lambda inv, u, M, y: (u, (inv, u)),
                          lambda res, du: (jnp.zeros_like(res[0]), jnp.zeros_like(res[1]))
                          + _unit_lower_solve_bwd(res, du)[1:])


@functools.partial(jax.custom_vjp, nondiff_argnums=(0,))
def _kept(fn, value, *args):
    return value


def _kept_fwd(fn, value, *args):
    return value, args


def _kept_bwd(fn, args, d):
    _, vjp = jax.vjp(fn, *args)
    return (jnp.zeros_like(d),) + tuple(vjp(d))


_kept.defvjp(_kept_fwd, _kept_bwd)


def _sum_over_time(x, reverse):
    C = x.shape[1]
    ti = lax.broadcasted_iota(jnp.int32, (C, C), 0)
    tj = lax.broadcasted_iota(jnp.int32, (C, C), 1)
    ones = jnp.broadcast_to(((tj >= ti) if reverse else (tj <= ti)).astype(BF16), (x.shape[0], C, C))
    hi = x.astype(BF16)
    r1 = x - hi.astype(F32)
    mid = r1.astype(BF16)
    lo = (r1 - mid.astype(F32)).astype(BF16)
    dn = (((2,), (1,)), ((0,), (0,)))
    return sum(lax.dot_general(ones, p, dn, preferred_element_type=F32) for p in (lo, mid, hi))


@jax.custom_vjp
def _time_cumsum(lw):
    return _sum_over_time(lw, reverse=False)


_time_cumsum.defvjp(lambda lw: (_sum_over_time(lw, reverse=False), None),
                    lambda _, d: (_sum_over_time(d, reverse=True),))


def _chunk_fn(S0, r, lw, k, v, kk, a, kept=None):
    C = SCAN_C
    bmm = _bmm
    ti = lax.broadcasted_iota(jnp.int32, (C, C), 0)
    tj = lax.broadcasted_iota(jnp.int32, (C, C), 1)
    incl2 = jnp.concatenate([(tj <= ti).astype(F32)] * 2, axis=1)
    strict = (tj < ti).astype(F32)
    n_mask = jnp.concatenate([jnp.zeros((C, C), F32), strict], axis=1)

    def known(name, fn, *args):
        return fn(*args) if kept is None else _kept(fn, kept[name], *args)

    cum = known("cum", _time_cumsum, lw)
    g_in, g_ex, g_inv = jnp.exp(cum), jnp.exp(cum - lw), jnp.exp(-cum)
    kkt, rt = kk * g_ex, r * g_in
    bk = jnp.concatenate([kk * a * g_inv, k * g_inv], axis=1)
    kr = jnp.concatenate([kkt, rt], axis=1)
    ratios = known("ratios", lambda x, y: bmm(x, y, 2, 2), kr, bk)
    A = ratios[:, :C]
    M = A[:, :, :C] * strict
    zv = jnp.concatenate([jnp.zeros_like(v), v], axis=1)
    s0_side = bmm(kr, S0, 2, 2, out_path=True)
    rhs = s0_side[:, :C] + bmm(A * n_mask, zv, 2, 1, out_path=True)
    if kept is None:
        inv = lax.stop_gradient(_unit_lower_inverse(M))
        y = _unit_lower_solve(inv, M, rhs)
    else:
        inv = kept["inv"]
        y = _unit_lower_solved(inv, kept["y"], M, rhs)
    z = jnp.concatenate([-y, v], axis=1)
    O = s0_side[:, C:] + bmm(ratios[:, C:] * incl2, z, 2, 1, out_path=True)
    g_end = g_in[:, C - 1:C, :]
    S1 = S0 * g_end + bmm(z, bk * g_end, 1, 1, out_path=True)
    return O, S1, dict(cum=cum, ratios=ratios, y=y, inv=inv)


def _scan_fwd(r, lw, k, v, kk, a, ex=None, tb=256):
    assert SCAN_C == HN and 2 * SCAN_C == PW
    T = r.shape[1]
    tb = min(tb, T)
    n_chunks = tb // SCAN_C
    nb = T // tb
    nx = ex.nb if ex else 0

    def body(*refs):
        r_ref, lw_ref, k_ref, v_ref, kk_ref, a_ref = refs[:6]
        x_in, (o_ref, s0_ref), x_out = refs[6:6 + nx], refs[6 + nx:8 + nx], refs[8 + nx:8 + 2 * nx]
        s_ref, sems = refs[8 + 2 * nx], refs[9 + 2 * nx:]

        plan = ex.schedule(nb) if ex else []

        @pl.when(pl.program_id(0) == 0)
        def _():
            s_ref[...] = jnp.zeros_like(s_ref)
            for at, action in plan[:1]:
                action(x_in, x_out, sems)

        def step(c, carry):
            sl = pl.ds(pl.multiple_of(c * SCAN_C, SCAN_C), SCAN_C)
            S0 = s_ref[...]
            O, S1, keep = _chunk_fn(S0, *[_split_pairs(ref[:, sl, :])
                                          for ref in (r_ref, lw_ref, k_ref, v_ref, kk_ref, a_ref)])
            o_ref[:, sl, :] = _join_pairs(O)
            s0_ref[c, 0] = jnp.concatenate([S0, keep["inv"]], axis=-1)
            s0_ref[c, 1] = jnp.concatenate([keep["cum"], keep["y"]], axis=-1)
            s0_ref[c, 2] = keep["ratios"][:, :SCAN_C]
            s0_ref[c, 3] = keep["ratios"][:, SCAN_C:]
            s_ref[...] = S1
            return carry

        lax.fori_loop(0, n_chunks, step, 0)

        for at, action in plan[1:]:
            pl.when(pl.program_id(0) == at)(functools.partial(action, x_in, x_out, sems))

    hm = pl.BlockSpec((NP, tb, PW), lambda i: (0, i, 0))
    res = pl.pallas_call(
        body, name="rwkv_scan_fwd", grid=(nb,), in_specs=[hm] * 6 + (ex.any_specs if ex else []),
        out_specs=[hm, pl.BlockSpec((n_chunks, N_KEPT, NH, HN, PW), lambda i: (i, 0, 0, 0, 0))]
        + (ex.any_specs if ex else []),
        out_shape=[jax.ShapeDtypeStruct((NP, T, PW), F32),
                   jax.ShapeDtypeStruct((T // SCAN_C, N_KEPT, NH, HN, PW), F32)]
        + (ex.out_shape if ex else []),
        scratch_shapes=[pltpu.VMEM((NH, HN, HN), F32)] + (ex.sem_shapes if ex else []),
        compiler_params=pltpu.CompilerParams(dimension_semantics=("arbitrary",), vmem_limit_bytes=VMEM_LIMIT),
    )(r, lw, k, v, kk, a, *(ex.bufs if ex else []))
    return res[0], res[1], list(res[2:])


def _scan_bwd(r, lw, k, v, kk, a, s0s, do, ex=None, tb=128):
    T = r.shape[1]
    tb = min(tb, T)
    n_chunks = tb // SCAN_C
    nb = T // tb
    nx = ex.nb if ex else 0

    def body(*refs):
        r_ref, lw_ref, k_ref, v_ref, kk_ref, a_ref, s0_ref, do_ref = refs[:8]
        x_in, (dr, dlw, dk, dv, dkk, da), x_out = refs[8:8 + nx], refs[8 + nx:14 + nx], refs[14 + nx:14 + 2 * nx]
        ds_ref, sems = refs[14 + 2 * nx], refs[15 + 2 * nx:]

        plan = ex.schedule(nb) if ex else []

        @pl.when(pl.program_id(0) == 0)
        def _():
            ds_ref[...] = jnp.zeros_like(ds_ref)
            for at, action in plan[:1]:
                action(x_in, x_out, sems)

        def step(j, carry):
            c = n_chunks - 1 - j
            sl = pl.ds(pl.multiple_of(c * SCAN_C, SCAN_C), SCAN_C)
            s0_inv, cum_y = s0_ref[c, 0], s0_ref[c, 1]
            kept = dict(inv=s0_inv[:, :, HN:], cum=cum_y[:, :, :HN], y=cum_y[:, :, HN:],
                        ratios=jnp.concatenate([s0_ref[c, 2], s0_ref[c, 3]], axis=1))
            _, vjp = jax.vjp(lambda *t: _chunk_fn(*t, kept=kept)[:2], s0_inv[:, :, :HN],
                             *[_split_pairs(ref[:, sl, :]) for ref in (r_ref, lw_ref, k_ref, v_ref, kk_ref, a_ref)])
            g = vjp((_split_pairs(do_ref[:, sl, :]), ds_ref[...]))
            ds_ref[...] = g[0]
            for ref, val in zip((dr, dlw, dk, dv, dkk, da), g[1:]):
                ref[:, sl, :] = _join_pairs(val)
            return carry

        lax.fori_loop(0, n_chunks, step, 0)

        for at, action in plan[1:]:
            pl.when(pl.program_id(0) == at)(functools.partial(action, x_in, x_out, sems))

    hm = pl.BlockSpec((NP, tb, PW), lambda i: (0, nb - 1 - i, 0))
    res = pl.pallas_call(
        body, name="rwkv_scan_bwd", grid=(nb,),
        in_specs=[hm] * 6 + [pl.BlockSpec((n_chunks, N_KEPT, NH, HN, PW), lambda i: (nb - 1 - i, 0, 0, 0, 0)), hm]
        + (ex.any_specs if ex else []),
        out_specs=[hm] * 6 + (ex.any_specs if ex else []),
        out_shape=[jax.ShapeDtypeStruct((NP, T, PW), F32)] * 6 + (ex.out_shape if ex else []),
        scratch_shapes=[pltpu.VMEM((NH, HN, HN), F32)] + (ex.sem_shapes if ex else []),
        compiler_params=pltpu.CompilerParams(dimension_semantics=("arbitrary",), vmem_limit_bytes=VMEM_LIMIT),
    )(r, lw, k, v, kk, a, s0s, do, *(ex.bufs if ex else []))
    return list(res[:6]), list(res[6:])


def _shift_down(i, p, prev8):
    first = jnp.where(i > 0, prev8[7:8, :], 0.0)
    row = lax.broadcasted_iota(jnp.int32, p.shape, 0)
    return jnp.where(row == 0, first, pltpu.roll(p, 1, axis=0))


def _mix_bwd(dq, p, sb, tm=256):
    def fn(i, n, dq, next8, p, prev8, sb):
        ps = _shift_down(i, p, prev8)
        d1 = dq * sb[1:2]
        last = jnp.where(i < n - 1, next8[0:1, :] * sb[1:2], 0.0)
        row = lax.broadcasted_iota(jnp.int32, dq.shape, 0)
        up = jnp.where(row == dq.shape[0] - 1, last, pltpu.roll(d1, dq.shape[0] - 1, axis=0))
        return (dq * sb[0:1] + up, jnp.sum(dq * p, axis=0, keepdims=True), jnp.sum(dq * ps, axis=0, keepdims=True))
    w = p.shape[1]
    return _rows_call("shift_mix_bwd", fn, [Rows(dq), Halo(dq, +1), Rows(p), Halo(p, -1)], [sb], [("rows", w, BF16)],
                      accs=[((1, w), F32), ((1, w), F32)], tm=tm, with_pid=True)


def _local_step(x, target, W, late_weights=None, early_grads=None, w_in_grads_ready=None, ffn_grads_ready=None):
    G = {}
    a = _rows_call("norm_mix_fwd", lambda x, g: (_rms(x, g),), [Rows(x)], [W["g_mix"]], [("rows", D, BF16)])[0]
    p_sgu = _matmul("proj_sgu", a, W["w_sgu_t"], "nt")
    def token_shift(p, sb0, sb1):
        row = lax.broadcasted_iota(jnp.int32, p.shape, 0)
        return p, p * sb0 + jnp.where(row == 0, 0.0, pltpu.roll(p, 1, axis=0)) * sb1
    p_rw, q = _matmul("proj_rwkv", a, W["w_rw_t"], "nt", tn=512, whole_rows=True, epilogue=token_shift,
                      extras=[W["sb"][0:1], W["sb"][1:2]], out_dtypes=(F32, F32))
    p_gate = _matmul("proj_gate", a, W["w_gate_t"], "nt")

    sgu_consts = [W["sgu_ln_w"], W["sgu_ln_b"], W["sgu_w"], W["sgu_bt"]]
    s = _rows_call("sgu_fwd", lambda *t: (_sgu_fn(*t),), [Rows(p_sgu)], sgu_consts, [("rows", D, BF16)], tm=SGU_C)[0]

    q_ins = [Rows(q, D, 0), Rows(q, D, 1), Rows(q, D, 2), Rows(q, 128, 24), Rows(q, 128, 25), Rows(q, 256, 13)]
    pre_consts = [W["w_lora"], W["w0"], W["a_lora"], W["a0"], W["g_lora"], W["k_k"], W["k_a"]]
    r_h, lw_h, k_h, v_h, kk_h, a_h, g_gate = _rows_call(
        "rwkv_pre_fwd", _pre_fn, q_ins, pre_consts, [("heads", F32)] * 6 + [("rows", D, F32)], tm=128)
    o_h, s0s, got = _scan_fwd(r_h, lw_h, k_h, v_h, kk_h, a_h, ex=late_weights[0] if late_weights else None)
    if late_weights:
        W = {**W, **late_weights[1](got)}
    y_a = _matmul("proj_a", s, W["w_proj_a"], "nn")
    post_ins = [Heads(o_h), Heads(r_h), Heads(k_h), Heads(v_h), Rows(g_gate)]
    post_consts = [W[n].reshape(NP, 1, PW) for n in ("ln_x_w", "ln_x_b", "r_k")]
    z_b = _rows_call("rwkv_post_fwd", lambda *t: (_post_fn(*t),), post_ins, post_consts, [("rows", D, BF16)], tm=128)[0]
    y_b, mixed = _matmul("proj_b", z_b, W["w_proj_b"], "nn", extras=[(p_gate, 0), (p_gate, 1), y_a],
                         epilogue=lambda yb, ga, gb, ya: (yb, _sigmoid(ga) * ya + _sigmoid(gb) * yb),
                         out_dtypes=(F32, BF16))

    def res1(mo, x, g):
        h1 = x + mo
        return h1, _rms(h1, g)
    h1, f = _matmul("proj_out", mixed, W["w_out"], "nn", extras=[x, W["g_ffn"]], epilogue=res1,
                    out_dtypes=(F32, BF16))

    def relu_sq(u):
        r = jnp.maximum(u, 0.0)
        return r, r * r
    r1, act = _matmul("ffn_up", f, W["w_ffn1"], "nn", epilogue=relu_sq, out_dtypes=(BF16, BF16))
    ff = _matmul("ffn_down", act, W["w_ffn2"], "nn")

    def head(h1, ff, tgt, g):
        def f_(h1, ff, g):
            y = _rms(h1 + ff, g)
            return 0.5 * jnp.sum(jnp.mean(jnp.square(y - tgt), axis=-1))
        loss, (dh2, _, dg) = jax.value_and_grad(f_, argnums=(0, 1, 2))(h1, ff, g)
        return dh2, jnp.full((8, LANES), loss, F32), dg
    dh2, loss_acc, G["g_final"] = _rows_call("loss_head", head, [Rows(h1), Rows(ff), Rows(target)], [W["g_final"]],
                                             [("rows", D, F32)], accs=[((8, LANES), F32), ((1, D), F32)])

    d_u1 = _matmul("ffn_down_dx", dh2, W["w_ffn2"], "nt", extras=[r1], out_dtypes=(BF16,),
                   epilogue=lambda d_act, r: (d_act * 2.0 * r.astype(F32),))[0]
    G["w_ffn2"] = _matmul("ffn_down_dw", act, dh2, "tn", out_dtype=GRAD_PAYLOAD)
    d_f = _matmul("ffn_up_dx", d_u1, W["w_ffn1"], "nt")
    G["w_ffn1"] = _matmul("ffn_up_dw", f, d_u1, "tn", out_blocks=N_DEV, out_dtype=GRAD_PAYLOAD)

    def res1_bwd(h1, d_f, dh2, g):
        _, vjp = jax.vjp(_rms, h1, g)
        dh, dg = vjp(d_f)
        return dh2 + dh, dg
    dh1, G["g_ffn"] = _rows_call("residual_norm_bwd", res1_bwd, [Rows(h1), Rows(d_f), Rows(dh2)],
                                 [W["g_ffn"]], [("rows", D, F32)], accs=[((1, D), F32)])
    ffn_token = ffn_grads_ready(G) if ffn_grads_ready else None
    def gate_bwd(d_mixed, ga, gb, ya, yb):
        _, vjp = jax.vjp(_gate_fn, jnp.concatenate([ga, gb], axis=-1), ya, yb)
        return vjp(d_mixed)
    d_gate, d_ya, d_yb = _matmul("proj_out_dx", dh1, W["w_out"], "nt", after=ffn_token, epilogue=gate_bwd,
                                 extras=[(p_gate, 0), (p_gate, 1), y_a, y_b], out_dtypes=(BF16, BF16, BF16),
                                 out_widths=(2 * D, D, D))
    G["w_out"] = _matmul("proj_out_dw", mixed, dh1, "tn", out_dtype=GRAD_PAYLOAD)

    d_s = _matmul("proj_a_dx", d_ya, W["w_proj_a"], "nt")
    G["w_proj_a"] = _matmul("proj_a_dw", s, d_ya, "tn", out_dtype=GRAD_PAYLOAD)

    def sgu_bwd(p, ds, *c):
        _, vjp = jax.vjp(_sgu_fn, p, *c)
        return vjp(ds)
    d_p_sgu, G["sgu_ln_w"], G["sgu_ln_b"], G["sgu_w"], G["sgu_bt"] = _rows_call(
        "sgu_bwd", sgu_bwd, [Rows(p_sgu), Rows(d_s)], sgu_consts, [("rows", 2 * D, BF16)],
        accs=[((1, D), F32), ((1, D), F32), ((SGU_G, SGU_C, SGU_C), F32), ((SGU_C, SGU_G), F32)], tm=SGU_C)

    d_zb = _matmul("proj_b_dx", d_yb, W["w_proj_b"], "nt")
    G["w_proj_b"] = _matmul("proj_b_dw", z_b, d_yb, "tn", out_dtype=GRAD_PAYLOAD)

    def post_bwd(o, r, k2, v, g, dz, *c):
        _, vjp = jax.vjp(_post_fn, o, r, k2, v, g, *c)
        return vjp(dz)
    do_h, dr1, dk1, dv1, d_g, g_lnw, g_lnb, g_rk = _rows_call(
        "rwkv_post_bwd", post_bwd, post_ins + [Rows(d_zb)], post_consts, [("heads", F32)] * 4 + [("rows", D, F32)],
        accs=[((NP, 1, PW), F32)] * 3, tm=128)
    G["ln_x_w"], G["ln_x_b"], G["r_k"] = (t.reshape(1, D) for t in (g_lnw, g_lnb, g_rk))
    (dr2, dlw, dk2, dv2, dkk, daa), early = _scan_bwd(r_h, lw_h, k_h, v_h, kk_h, a_h, s0s, do_h,
                                                      ex=early_grads(G) if early_grads else None)

    def pre_bwd(qr, qk, qv, qxw, qxa, qxg, dr1, dr2, dlw, dk1, dk2, dv1, dv2, dkk, daa, dg, *c):
        _, vjp = jax.vjp(_pre_fn, qr, qk, qv, qxw, qxa, qxg, *c)
        g = vjp((dr1 + dr2, dlw, dk1 + dk2, dv1 + dv2, dkk, daa, dg))
        dq = jnp.concatenate(g[:6], axis=-1)
        return (dq,) + tuple(g[6:])
    pre_b_ins = q_ins + [Heads(dr1), Heads(dr2), Heads(dlw), Heads(dk1), Heads(dk2), Heads(dv1), Heads(dv2),
                         Heads(dkk), Heads(daa), Rows(d_g)]
    d_q, G["w_lora"], G["w0"], G["a_lora"], G["a0"], G["g_lora"], G["k_k"], G["k_a"] = _rows_call(
        "rwkv_pre_bwd", pre_bwd, pre_b_ins, pre_consts, [("rows", RW_INT, F32)],
        accs=[((128, D), F32), ((1, D), F32), ((128, D), F32), ((1, D), F32), ((256, D), F32), ((1, D), F32),
              ((1, D), F32)], tm=128)
    d_p_rw, dsb0, dsb1 = _mix_bwd(d_q, p_rw, W["sb"])
    G["sb"] = jnp.concatenate([dsb0, dsb1], axis=0)

    G["w_sgu_t"] = _matmul("proj_sgu_dw", d_p_sgu, a, "tn", out_dtype=GRAD_PAYLOAD)
    G["w_rw_t"] = _matmul("proj_rwkv_dw", d_p_rw, a, "tn", out_dtype=GRAD_PAYLOAD)
    G["w_gate_t"] = _matmul("proj_gate_dw", d_gate, a, "tn", out_dtype=GRAD_PAYLOAD)
    token = w_in_grads_ready(G) if w_in_grads_ready else None
    da1 = _matmul("proj_sgu_dx", d_p_sgu, W["w_sgu_t"], "nn", after=token)
    da2 = _matmul("proj_rwkv_dx", d_p_rw, W["w_rw_t"], "nn", after=token)
    da3 = _matmul("proj_gate_dx", d_gate, W["w_gate_t"], "nn", after=token)

    def norm1_bwd(x, da1, da2, da3, dh1, g):
        _, vjp = jax.vjp(_rms, x, g)
        dx, dg = vjp(da1 + da2 + da3)
        return dh1 + dx, dg
    dx, G["g_mix"] = _rows_call("norm_mix_bwd", norm1_bwd, [Rows(x), Rows(da1), Rows(da2), Rows(da3), Rows(dh1)],
                                [W["g_mix"]], [("rows", D, F32)], accs=[((1, D), F32)])
    return loss_acc[0, 0], dx, G, early


class Exchange:
    def __init__(self, bufs, gathers):
        self.bufs, self.gathers, self.nb = list(bufs), list(gathers), len(bufs)
        self.any_specs = [pl.BlockSpec(memory_space=pl.ANY)] * self.nb
        self.out_shape = [jax.ShapeDtypeStruct((N_DEV,) + (b.shape if g else b.shape[1:]), b.dtype)
                          for b, g in zip(self.bufs, self.gathers)]
        n = (N_DEV - 1) * self.nb
        self.sem_shapes = [pltpu.SemaphoreType.DMA((n,)), pltpu.SemaphoreType.DMA((n,)),
                           pltpu.SemaphoreType.DMA((self.nb,))]

    def _copies(self, in_refs, out_refs, sems):
        send_sems, recv_sems, local_sems = sems
        x, y, c = lax.axis_index("x"), lax.axis_index("y"), lax.axis_index("c")
        me = 4 * x + 2 * y + c

        def src(b, dest):
            return in_refs[b] if self.gathers[b] else in_refs[b].at[dest]

        local = [pltpu.make_async_copy(src(b, me), out_refs[b].at[me], local_sems.at[b]) for b in range(self.nb)]
        sends, recvs = [], []
        for kbits in range(1, N_DEV):
            px = 1 - x if kbits & 4 else x
            py = 1 - y if kbits & 2 else y
            pc = 1 - c if kbits & 1 else c
            peer = 4 * px + 2 * py + pc
            for b in range(self.nb):
                s = (kbits - 1) * self.nb + b
                sends.append(pltpu.make_async_remote_copy(
                    src_ref=src(b, peer), dst_ref=out_refs[b].at[me], send_sem=send_sems.at[s],
                    recv_sem=recv_sems.at[s], device_id=(px, py, pc), device_id_type=pl.DeviceIdType.MESH))
                recvs.append(pltpu.make_async_remote_copy(
                    src_ref=src(b, peer), dst_ref=out_refs[b].at[peer], send_sem=send_sems.at[s],
                    recv_sem=recv_sems.at[s], device_id=(px, py, pc), device_id_type=pl.DeviceIdType.MESH))
        return local, sends, recvs

    def start(self, in_refs, out_refs, sems):
        local, sends, _ = self._copies(in_refs, out_refs, sems)
        for cp in sends + local:
            cp.start()

    def wait(self, in_refs, out_refs, sems):
        local, sends, recvs = self._copies(in_refs, out_refs, sems)
        for cp in recvs:
            cp.wait_recv()
        for cp in sends:
            cp.wait_send()
        for cp in local:
            cp.wait()

    def schedule(self, n_steps):
        return [(0, self.start), (n_steps - 1, self.wait)]


def _exchange(name, bufs, gather):
    ex = Exchange(bufs, gather if isinstance(gather, (list, tuple)) else [gather] * len(bufs))

    def body(*refs):
        in_refs, out_refs, sems = refs[:ex.nb], refs[ex.nb:2 * ex.nb], refs[2 * ex.nb:]
        ex.start(in_refs, out_refs, sems)
        ex.wait(in_refs, out_refs, sems)

    return pl.pallas_call(body, name=name, in_specs=ex.any_specs, out_specs=ex.any_specs, out_shape=ex.out_shape,
                          scratch_shapes=ex.sem_shapes)(*ex.bufs)


N_CHIP = 4


def _pair_exchange(name, blocks):
    def body(b_ref, got_ref, send_sems, recv_sems):
        x, y, c = lax.axis_index("x"), lax.axis_index("y"), lax.axis_index("c")
        copies = [pltpu.make_async_remote_copy(
            src_ref=b_ref.at[2 * q + 1 - c], dst_ref=got_ref.at[q], send_sem=send_sems.at[q],
            recv_sem=recv_sems.at[q], device_id=(x, y, 1 - c), device_id_type=pl.DeviceIdType.MESH)
            for q in range(N_CHIP)]
        for cp in copies:
            cp.start()
        for cp in copies:
            cp.wait_recv()
        for cp in copies:
            cp.wait_send()

    any_spec = pl.BlockSpec(memory_space=pl.ANY)
    return pl.pallas_call(
        body, name=name, in_specs=[any_spec], out_specs=any_spec,
        out_shape=jax.ShapeDtypeStruct((N_CHIP,) + blocks.shape[1:], blocks.dtype),
        scratch_shapes=[pltpu.SemaphoreType.DMA((N_CHIP,))] * 2,
    )(blocks)


def _pair_sum(name, blocks, got, core):
    _, R, Wd = blocks.shape

    def body(c_ref, a_ref, b_ref, o_ref):
        o_ref[...] = (a_ref[...].astype(F32) + b_ref[...].astype(F32)).astype(o_ref.dtype)

    return pl.pallas_call(
        body, name=name,
        grid_spec=pltpu.PrefetchScalarGridSpec(
            num_scalar_prefetch=1, grid=(N_CHIP,),
            in_specs=[pl.BlockSpec((None, R, Wd), lambda q, c_ref: (2 * q + c_ref[0], 0, 0)),
                      pl.BlockSpec((None, R, Wd), lambda q, c_ref: (q, 0, 0))],
            out_specs=pl.BlockSpec((None, R, Wd), lambda q, c_ref: (q, 0, 0))),
        out_shape=jax.ShapeDtypeStruct(got.shape, blocks.dtype),
        compiler_params=pltpu.CompilerParams(dimension_semantics=("parallel",), vmem_limit_bytes=VMEM_LIMIT),
    )(core, blocks, got)


def _chip_copies(s_ref, land_ref, send_sems, recv_sems):
    x, y, c = lax.axis_index("x"), lax.axis_index("y"), lax.axis_index("c")
    my_q = 2 * x + y
    sends, recvs = [], []
    for kbits in range(1, N_CHIP):
        px = 1 - x if kbits & 2 else x
        py = 1 - y if kbits & 1 else y
        peer_q = 2 * px + py
        sends.append(pltpu.make_async_remote_copy(
            src_ref=s_ref.at[peer_q], dst_ref=land_ref.at[my_q], send_sem=send_sems[kbits - 1],
            recv_sem=recv_sems[kbits - 1], device_id=(px, py, c), device_id_type=pl.DeviceIdType.MESH))
        recvs.append(pltpu.make_async_remote_copy(
            src_ref=s_ref.at[peer_q], dst_ref=land_ref.at[peer_q], send_sem=send_sems[kbits - 1],
            recv_sem=recv_sems[kbits - 1], device_id=(px, py, c), device_id_type=pl.DeviceIdType.MESH))
    return sends, recvs


_HBM = pl.BlockSpec(memory_space=pltpu.HBM)
_SEM = pl.BlockSpec(memory_space=pltpu.SEMAPHORE)
N_CHIP_SEMS = 2 * (N_CHIP - 1)


def _scatter_copies(b_refs, land_refs, send_sems, recv_sems):
    x, y, c = lax.axis_index("x"), lax.axis_index("y"), lax.axis_index("c")
    me = 4 * x + 2 * y + c
    sends, recvs = [], []
    for kbits in range(1, N_DEV):
        px = 1 - x if kbits & 4 else x
        py = 1 - y if kbits & 2 else y
        pc = 1 - c if kbits & 1 else c
        peer = 4 * px + 2 * py + pc
        for b in range(len(b_refs)):
            s = (kbits - 1) * len(b_refs) + b
            sends.append(pltpu.make_async_remote_copy(
                src_ref=b_refs[b].at[peer], dst_ref=land_refs[b].at[me], send_sem=send_sems[s],
                recv_sem=recv_sems[s], device_id=(px, py, pc), device_id_type=pl.DeviceIdType.MESH))
            recvs.append(pltpu.make_async_remote_copy(
                src_ref=b_refs[b].at[peer], dst_ref=land_refs[b].at[peer], send_sem=send_sems[s],
                recv_sem=recv_sems[s], device_id=(px, py, pc), device_id_type=pl.DeviceIdType.MESH))
    return sends, recvs


def _scatter_start(name, bufs):
    nb = len(bufs)
    n = (N_DEV - 1) * nb

    def body(*refs):
        b_refs, land_refs, outs = refs[:nb], refs[nb:2 * nb], refs[2 * nb:]
        sends, _ = _scatter_copies(b_refs, land_refs, outs[:n], outs[n:2 * n])
        for cp in sends:
            cp.start()
        token = outs[2 * n + 2 * nb]
        token[...] = jnp.zeros_like(token)

    thru = tuple(pltpu.HBM(b.shape, b.dtype) for b in bufs)
    res = pl.pallas_call(
        body, name=name, in_specs=(_HBM,) * (2 * nb),
        out_specs=(_SEM,) * (2 * n) + (_HBM,) * (2 * nb) + (pl.BlockSpec(memory_space=pltpu.VMEM),),
        out_shape=(pltpu.SemaphoreType.DMA(()),) * (2 * n) + thru + thru + (jax.ShapeDtypeStruct((8, LANES), F32),),
        input_output_aliases={i: 2 * n + i for i in range(2 * nb)},
        compiler_params=pltpu.CompilerParams(has_side_effects=pltpu.SideEffectType.DATAFLOW_SIDE_EFFECTING),
    )(*[pltpu.with_memory_space_constraint(b, pltpu.HBM) for b in bufs],
      *[pltpu.with_memory_space_constraint(lax.empty(b.shape, b.dtype), pltpu.HBM) for b in bufs])
    return res[:2 * n], list(res[2 * n:2 * n + nb]), list(res[2 * n + nb:2 * n + 2 * nb]), res[2 * n + 2 * nb]


def _scatter_wait(name, sems, bufs_thru, lands_thru, after):
    nb = len(bufs_thru)
    n = (N_DEV - 1) * nb

    def body(*refs):
        b_refs, land_refs, sem_refs = refs[:nb], refs[nb:2 * nb], refs[2 * nb:2 * nb + 2 * n]
        sends, recvs = _scatter_copies(b_refs, land_refs, sem_refs[:n], sem_refs[n:])
        for cp in sends:
            cp.wait_send()
        for cp in recvs:
            cp.wait_recv()

    thru = tuple(pltpu.HBM(b.shape, b.dtype) for b in bufs_thru)
    res = pl.pallas_call(
        body, name=name, in_specs=(_HBM,) * (2 * nb) + (_SEM,) * (2 * n) + (pl.BlockSpec(memory_space=pl.ANY),),
        out_specs=(_HBM,) * (2 * nb), out_shape=thru + thru,
        input_output_aliases={i: i for i in range(2 * nb)},
        compiler_params=pltpu.CompilerParams(has_side_effects=pltpu.SideEffectType.DATAFLOW_SIDE_EFFECTING),
    )(*bufs_thru, *lands_thru, *sems, after)
    return list(res[:nb]), list(res[nb:])


def _chip_exchange_start(name, sums):
    def body(s_ref, land_ref, *outs):
        sems, token = outs[:N_CHIP_SEMS], outs[N_CHIP_SEMS + 2]
        sends, _ = _chip_copies(s_ref, land_ref, sems[:N_CHIP - 1], sems[N_CHIP - 1:])
        for cp in sends:
            cp.start()
        token[...] = jnp.zeros_like(token)

    res = pl.pallas_call(
        body, name=name, in_specs=(_HBM, _HBM),
        out_specs=(_SEM,) * N_CHIP_SEMS + (_HBM, _HBM, pl.BlockSpec(memory_space=pltpu.VMEM)),
        out_shape=(pltpu.SemaphoreType.DMA(()),) * N_CHIP_SEMS
        + (pltpu.HBM(sums.shape, sums.dtype), pltpu.HBM(sums.shape, sums.dtype), jax.ShapeDtypeStruct((8, LANES), F32)),
        input_output_aliases={0: N_CHIP_SEMS, 1: N_CHIP_SEMS + 1},
        compiler_params=pltpu.CompilerParams(has_side_effects=pltpu.SideEffectType.DATAFLOW_SIDE_EFFECTING),
    )(pltpu.with_memory_space_constraint(sums, pltpu.HBM),
      pltpu.with_memory_space_constraint(lax.empty(sums.shape, sums.dtype), pltpu.HBM))
    return res[:N_CHIP_SEMS], res[N_CHIP_SEMS], res[N_CHIP_SEMS + 1], res[N_CHIP_SEMS + 2]


def _chip_exchange_wait(name, sems, sums_thru, land_thru, after):
    def body(s_ref, land_ref, *rest):
        sems = rest[:N_CHIP_SEMS]
        sends, recvs = _chip_copies(s_ref, land_ref, sems[:N_CHIP - 1], sems[N_CHIP - 1:])
        for cp in sends:
            cp.wait_send()
        for cp in recvs:
            cp.wait_recv()

    return pl.pallas_call(
        body, name=name, in_specs=(_HBM, _HBM) + (_SEM,) * N_CHIP_SEMS + (pl.BlockSpec(memory_space=pl.ANY),),
        out_specs=(_HBM, _HBM),
        out_shape=(pltpu.HBM(sums_thru.shape, sums_thru.dtype), pltpu.HBM(sums_thru.shape, sums_thru.dtype)),
        input_output_aliases={0: 0, 1: 1},
        compiler_params=pltpu.CompilerParams(has_side_effects=pltpu.SideEffectType.DATAFLOW_SIDE_EFFECTING),
    )(sums_thru, land_thru, *sems, after)


def _adamw(name, slots, w, m, v, tr=256):
    unit_mid = w.ndim == 3 and w.shape[1] == 1 and w.shape[0] > 1
    R, Wd = (w.shape[0], w.shape[2]) if unit_mid else w.shape[-2:]
    depth_axis = w.ndim == 3 and not unit_mid
    if R % tr == 0:
        tc = Wd
    else:
        tr, tc = R, (256 if (Wd % 256 == 0 and R > 256) else Wd)
    at = (slice(None), 0, slice(None)) if unit_mid else Ellipsis

    def body(s_ref, w_ref, m_ref, v_ref, g_out, d_out, m_out, v_out):
        g = s_ref[0].astype(F32)
        for j in range(1, slots.shape[0]):
            g = g + s_ref[j].astype(F32)
        m_new = ADAM_B1 * m_ref[at] + (1.0 - ADAM_B1) * g
        v_new = ADAM_B2 * v_ref[at] + (1.0 - ADAM_B2) * jnp.square(g)
        m_hat = m_new / (1.0 - ADAM_B1 ** ADAM_STEP)
        v_hat = v_new / (1.0 - ADAM_B2 ** ADAM_STEP)
        g_out[at] = g
        d_out[at] = -ADAM_LR * (m_hat / (jnp.sqrt(v_hat) + ADAM_EPS) + ADAM_WD * w_ref[at])
        m_out[at] = m_new
        v_out[at] = v_new

    if unit_mid:
        row = pl.BlockSpec((tr, 1, tc), lambda i, j: (i, 0, j))
    elif depth_axis:
        row = pl.BlockSpec((None, tr, tc), lambda i, j: (0, i, j))
    else:
        row = pl.BlockSpec((tr, tc), lambda i, j: (i, j))
    return pl.pallas_call(
        body, name=name, grid=(R // tr, Wd // tc),
        in_specs=[pl.BlockSpec((slots.shape[0], tr, tc), lambda i, j: (0, i, j)), row, row, row],
        out_specs=[row] * 4, out_shape=[jax.ShapeDtypeStruct(w.shape, F32)] * 4,
        compiler_params=pltpu.CompilerParams(dimension_semantics=("parallel", "parallel"),
                                             vmem_limit_bytes=VMEM_LIMIT),
    )(slots, w, m, v)


PACK_W = 1024
PACKED = [(n, s) for n, s in REPLICATED if n != "sgu_w"]
_SMALL_SIZES = [int(np.prod(s)) for _, s in PACKED]
_SMALL_ROWS = _round_up(_round_up(sum(_SMALL_SIZES) + PACK_W, PACK_W) // PACK_W, 8)
_LOSS_AT = sum(_SMALL_SIZES)
W_IN_SHARD = P_TOTAL // N_DEV


def _pack_rows(parts, rows, dtype):
    flat = jnp.concatenate([p.reshape(-1).astype(dtype) for p in parts])
    return jnp.pad(flat, (0, rows * PACK_W - flat.shape[0])).reshape(rows, PACK_W)


def _w_in_groups_t(blocks):
    wt = blocks.reshape(P_TOTAL, D)
    o, c = 2 * D, 2 * D + 3 * D
    z = lambda r: jnp.zeros((r, D), wt.dtype)
    rw = jnp.concatenate([wt[o:c], wt[c:c + L_W], z(128 - L_W), wt[c + L_W:c + L_W + L_A], z(128 - L_A),
                          wt[c + L_W + L_A:o + C_B], z(256 - L_G)], axis=0)
    return wt[:o], rw, wt[o + C_B:]


def _w_in_grad_blocks(g_sgu_t, g_rw_t, g_gate_t):
    c = 3 * D
    full = jnp.concatenate([g_sgu_t, g_rw_t[:c], g_rw_t[c:c + L_W], g_rw_t[c + 128:c + 128 + L_A],
                            g_rw_t[c + 256:c + 256 + L_G], g_gate_t], axis=0)
    return full.reshape(N_DEV, W_IN_SHARD, D)


def _mesh_index():
    me = 4 * lax.axis_index("x") + 2 * lax.axis_index("y") + lax.axis_index("c")
    return me.astype(jnp.int32).reshape(1)


def _fill_slot(name, dst, src, idx, src_idx=None):
    R, Wd = dst.shape[1:]
    scalars = [idx] if src_idx is None else [idx, src_idx]
    if src_idx is None:
        src_spec = pl.BlockSpec((R, Wd), lambda i, *s: (0, 0))
    else:
        src_spec = pl.BlockSpec((None, R, Wd), lambda i, *s: (s[1][0], 0, 0))

    def body(*refs):
        src_ref, out_ref = refs[len(scalars) + 1], refs[len(scalars) + 2]
        out_ref[...] = src_ref[...]

    return pl.pallas_call(
        body, name=name,
        grid_spec=pltpu.PrefetchScalarGridSpec(
            num_scalar_prefetch=len(scalars), grid=(1,),
            in_specs=[pl.BlockSpec(memory_space=pl.ANY), src_spec],
            out_specs=pl.BlockSpec((None, R, Wd), lambda i, *s: (s[0][0], 0, 0))),
        out_shape=jax.ShapeDtypeStruct(dst.shape, dst.dtype),
        input_output_aliases={len(scalars): 0},
        compiler_params=pltpu.CompilerParams(vmem_limit_bytes=VMEM_LIMIT),
    )(*scalars, dst, src)


class TwoLevelGather:
    N_COPIES = 8
    PART_ALIGN = 16

    def __init__(self, bufs, skip_own=()):
        self.bufs, self.nb, self.skip_own = list(bufs), len(bufs), tuple(skip_own)
        self.any_specs = [pl.BlockSpec(memory_space=pl.ANY)] * self.nb
        self.out_shape = [jax.ShapeDtypeStruct((N_DEV,) + b.shape, b.dtype) for b in self.bufs]
        n = self.N_COPIES * self.nb
        self.sem_shapes = [pltpu.SemaphoreType.DMA((n,)), pltpu.SemaphoreType.DMA((n,)),
                           pltpu.SemaphoreType.DMA((self.nb,))]

    def _parts(self, b):
        shape = self.bufs[b].shape
        first = shape[0] // 2 // self.PART_ALIGN * self.PART_ALIGN if len(shape) == 2 else 0
        return [(0, first), (first, shape[0] - first)] if first else [None]

    def _copies(self, in_refs, out_refs, sems):
        send_sems, recv_sems, local_sems = sems
        nb = self.nb
        x, y, c = lax.axis_index("x"), lax.axis_index("y"), lax.axis_index("c")
        me, sibling = (x, y, c), (x, y, 1 - c)
        near, far = [(1 - x, y), (x, 1 - y)], (1 - x, 1 - y)

        def slot(b, dev):
            return out_refs[b].at[4 * dev[0] + 2 * dev[1] + dev[2]]

        def copy(b, k, block, to, own=False, rows=None):
            src, dst = in_refs[b] if own else slot(b, block), slot(b, block)
            if rows is not None:
                src, dst = src.at[pl.ds(*rows)], dst.at[pl.ds(*rows)]
            return pltpu.make_async_remote_copy(
                src_ref=src, dst_ref=dst, send_sem=send_sems.at[self.N_COPIES * b + k],
                recv_sem=recv_sems.at[self.N_COPIES * b + k], device_id=to, device_id_type=pl.DeviceIdType.MESH)

        ways = [(j, b) for j in range(2) for b in range(nb) if j < len(self._parts(b))]
        cp = {}
        cp["local"] = [pltpu.make_async_copy(in_refs[b], slot(b, me), local_sems.at[b]) for b in range(nb)
                       if b not in self.skip_own]
        cp["first"] = [copy(b, 0, me, sibling, own=True) for b in range(nb)]
        cp["first"] += [copy(b, 1 + j, me, (*near[j], c), own=True) for j in range(2) for b in range(nb)]
        cp["from_near"] = [copy(b, 1 + j, (*near[j], c), me) for j in range(2) for b in range(nb)]
        cp["near_on"] = [[copy(b, 5 + j, (*near[j], c), sibling)]
                         + ([copy(b, 3 + j, (*near[j], c), (*near[1 - j], c), rows=self._parts(b)[j])]
                            if (j, b) in ways else []) for j in range(2) for b in range(nb)]
        cp["from_far"] = [copy(b, 3 + j, (*far, c), me, rows=self._parts(b)[j]) for j, b in ways]
        cp["far_on"] = [copy(b, 7, (*far, c), sibling) for b in range(nb)]
        cp["from_sibling"] = [copy(b, 0, sibling, me) for b in range(nb)]
        cp["from_sibling"] += [copy(b, 5 + j, (*near[j], 1 - c), me) for j in range(2) for b in range(nb)]
        cp["from_sibling"] += [copy(b, 7, (*far, 1 - c), me) for b in range(nb)]
        return cp

    def start(self, in_refs, out_refs, sems):
        cp = self._copies(in_refs, out_refs, sems)
        for c in cp["first"] + cp["local"]:
            c.start()

    def pass_near(self, in_refs, out_refs, sems):
        cp = self._copies(in_refs, out_refs, sems)
        for arrived, onward in zip(cp["from_near"], cp["near_on"]):
            arrived.wait_recv()
            for c in onward:
                c.start()

    def pass_far(self, in_refs, out_refs, sems):
        cp = self._copies(in_refs, out_refs, sems)
        for c in cp["from_far"]:
            c.wait_recv()
        for c in cp["far_on"]:
            c.start()

    def finish(self, in_refs, out_refs, sems):
        cp = self._copies(in_refs, out_refs, sems)
        for c in cp["from_sibling"]:
            c.wait_recv()
        for c in cp["first"] + sum(cp["near_on"], []) + cp["far_on"]:
            c.wait_send()
        for c in cp["local"]:
            c.wait()

    def schedule(self, n_steps):
        return [(0, self.start), (max(n_steps // 2 - 1, 0), self.pass_near), (max(n_steps - 3, 0), self.pass_far),
                (n_steps - 1, self.finish)]


def _all_gather_two_level(name, bufs, skip_own=()):
    ex = TwoLevelGather(bufs, skip_own)

    def body(*refs):
        args = refs[:ex.nb], refs[ex.nb:2 * ex.nb], refs[2 * ex.nb:]
        for _, action in ex.schedule(1):
            action(*args)

    return pl.pallas_call(body, name=name, in_specs=ex.any_specs, out_specs=ex.any_specs, out_shape=ex.out_shape,
                          scratch_shapes=ex.sem_shapes)(*ex.bufs)


def _cols_from_blocks(blk):
    return jnp.transpose(blk, (1, 0, 2)).reshape(blk.shape[1], -1)


def _cols_to_blocks(g):
    r, c = g.shape
    return jnp.transpose(g.reshape(r, N_DEV, c // N_DEV), (1, 0, 2))


FIRST_WEIGHTS = ["w_in", "shift_b", "w_lora_w", "a_lora_w", "g_lora_w"]
LATE_WEIGHTS = ["w_proj_a", "w_proj_b", "w_out", "w_ffn1", "w_ffn2"]
SCAN_CARRIED = ["w_proj_a", "w_proj_b", "w_out"]
FFN_WEIGHTS = ["w_ffn1", "w_ffn2"]


def _late_weights(shards):
    ex = TwoLevelGather([shards[n][0].astype(BF16) for n in LATE_WEIGHTS])

    def finish(results):
        got = dict(zip(LATE_WEIGHTS, results))
        W = {n: got[n].reshape(-1, D) for n in ("w_proj_a", "w_proj_b", "w_out", "w_ffn2")}
        W["w_ffn1"] = got["w_ffn1"].reshape(N_DEV, D, -1)
        return W
    return ex, finish


def _gather_weights(shards):
    def payload(n):
        if n == "w_in":
            return jnp.transpose(shards[n][0]).astype(BF16)
        return shards[n] if n == "shift_b" else shards[n].astype(BF16)
    payloads = [payload(n) for n in FIRST_WEIGHTS]
    got = dict(zip(FIRST_WEIGHTS, _all_gather_two_level("weight_all_gather", payloads, skip_own=(0,))))
    got["w_in"] = _fill_slot("w_in_own_slot", got["w_in"], payloads[0], _mesh_index())
    W = {}
    W["w_sgu_t"], W["w_rw_t"], W["w_gate_t"] = _w_in_groups_t(got["w_in"])
    z = lambda r, c, dt: jnp.zeros((r, c), dt)
    W["w_lora"] = jnp.concatenate([_cols_from_blocks(got["w_lora_w"][:, 0]).astype(F32), z(128 - L_W, D, F32)], axis=0)
    W["a_lora"] = jnp.concatenate([_cols_from_blocks(got["a_lora_w"][:, 0]).astype(F32), z(128 - L_A, D, F32)], axis=0)
    W["g_lora"] = jnp.concatenate([_cols_from_blocks(got["g_lora_w"][:, 0]).astype(F32), z(256 - L_G, D, F32)], axis=0)
    sb = _cols_from_blocks(got["shift_b"][:, 0])
    W["sb"] = jnp.concatenate([sb[:, :3 * D], sb[:, 3 * D:3 * D + L_W], z(2, 128 - L_W, F32),
                               sb[:, 3 * D + L_W:3 * D + L_W + L_A], z(2, 128 - L_A, F32),
                               sb[:, 3 * D + L_W + L_A:], z(2, 256 - L_G, F32)], axis=1)
    return W


def _replicated_weights(rep):
    W = {n: rep[n] for n in ("g_mix", "sgu_ln_w", "sgu_ln_b", "w0", "a0", "k_k", "k_a", "r_k", "ln_x_w", "ln_x_b",
                             "g_ffn")}
    W["g_final"] = rep["g_final"].reshape(1, D)
    W["sgu_w"] = rep["sgu_w"][0]
    W["sgu_bt"] = jnp.transpose(rep["sgu_b"][0])
    return W


def _late_grad_blocks(G):
    return Exchange([G[n].reshape(N_DEV, -1, D) for n in SCAN_CARRIED]
                    + [G["sgu_w"].reshape(SGU_G * SGU_C, SGU_C).astype(GRAD_PAYLOAD)],
                    [False] * len(SCAN_CARRIED) + [True])


def _first_grad_blocks(G):
    sbg = G["sb"]
    c = 3 * D
    sb = jnp.concatenate([sbg[:, :c], sbg[:, c:c + L_W], sbg[:, c + 128:c + 128 + L_A],
                          sbg[:, c + 256:c + 256 + L_G]], axis=1)
    return {
        "shift_b": _cols_to_blocks(sb),
        "w_lora_w": _cols_to_blocks(G["w_lora"][:L_W]), "a_lora_w": _cols_to_blocks(G["a_lora"][:L_A]),
        "g_lora_w": _cols_to_blocks(G["g_lora"][:L_G]),
    }


def _replicated_grads(G):
    small = {n: G[n] for n in ("g_mix", "sgu_ln_w", "sgu_ln_b", "w0", "a0", "k_k", "k_a", "r_k", "ln_x_w", "ln_x_b",
                               "g_ffn", "g_final")}
    small["sgu_w"] = G["sgu_w"]
    small["sgu_b"] = jnp.transpose(G["sgu_bt"])
    return small


def kernel(x, g_mix, w_in, sgu_ln_w, sgu_ln_b, sgu_w, sgu_b, w_proj_a, shift_b, w_lora_w, w0, a_lora_w, a0, g_lora_w, k_k, k_a, r_k, ln_x_w, ln_x_b, w_proj_b, w_out, g_ffn, w_ffn1, w_ffn2, g_final, loss_target, m_g_mix, m_w_in, m_sgu_ln_w, m_sgu_ln_b, m_sgu_w, m_sgu_b, m_w_proj_a, m_shift_b, m_w_lora_w, m_w0, m_a_lora_w, m_a0, m_g_lora_w, m_k_k, m_k_a, m_r_k, m_ln_x_w, m_ln_x_b, m_w_proj_b, m_w_out, m_g_ffn, m_w_ffn1, m_w_ffn2, m_g_final, v_g_mix, v_w_in, v_sgu_ln_w, v_sgu_ln_b, v_sgu_w, v_sgu_b, v_w_proj_a, v_shift_b, v_w_lora_w, v_w0, v_a_lora_w, v_a0, v_g_lora_w, v_k_k, v_k_a, v_r_k, v_ln_x_w, v_ln_x_b, v_w_proj_b, v_w_out, v_g_ffn, v_w_ffn1, v_w_ffn2, v_g_final):
    env = dict(locals())
    weights = {n: env[n] for n in WEIGHT_ORDER}
    moms = {n: env["m_" + n] for n in WEIGHT_ORDER}
    vars_ = {n: env["v_" + n] for n in WEIGHT_ORDER}

    shards = {n: weights[n] for n, _, _ in SHARDED}
    W = _gather_weights(shards)
    W.update(_replicated_weights({n: weights[n] for n, _ in REPLICATED}))
    in_flight = {}

    def send_w_in_grads(G):
        blocks = _w_in_grad_blocks(G["w_sgu_t"], G["w_rw_t"], G["w_gate_t"])
        got = _pair_exchange("grad_pair_exchange", blocks)
        core = lax.axis_index("c").astype(jnp.int32).reshape(1)
        sums = _pair_sum("grad_pair_sum", blocks, got, core)
        in_flight["sems"], in_flight["sums"], in_flight["land"], token = _chip_exchange_start("grad_chip_start", sums)
        return token

    def send_ffn_grads(G):
        in_flight["ffn"] = _scatter_start("grad_ffn_start", [G["w_ffn1"], G["w_ffn2"].reshape(N_DEV, -1, D)])
        return in_flight["ffn"][3]

    loss_part, dx, G, late_slots = _local_step(x[0], loss_target[0], W, late_weights=_late_weights(shards),
                                               early_grads=_late_grad_blocks, w_in_grads_ready=send_w_in_grads,
                                               ffn_grads_ready=send_ffn_grads)

    slots = dict(zip(SCAN_CARRIED, late_slots))
    me = _mesh_index()
    sent, landed = _scatter_wait("grad_ffn_wait", *in_flight["ffn"][:3], after=dx)
    for i, n in enumerate(FFN_WEIGHTS):
        slots[n] = _fill_slot("grad_own_slot_" + n, landed[i], sent[i], me, src_idx=me)
    blocks = _first_grad_blocks(G)
    small = _replicated_grads(G)
    small_parts = [small[n] for n, _ in PACKED] + [jnp.full((PACK_W,), loss_part, F32)]
    rest = [n for n in FIRST_WEIGHTS if n != "w_in"]
    res = _exchange("grad_exchange", [blocks[n] for n in rest] + [_pack_rows(small_parts, _SMALL_ROWS, F32)],
                    [False] * len(rest) + [True])
    slots.update(zip(rest, res[:-1]))
    small_slots = res[-1]
    sums, chip_slots = _chip_exchange_wait("grad_chip_wait", in_flight["sems"], in_flight["sums"], in_flight["land"],
                                           after=small_slots)
    my_chip = (2 * lax.axis_index("x") + lax.axis_index("y")).astype(jnp.int32).reshape(1)
    slots["w_in"] = _fill_slot("grad_own_slot", chip_slots, sums, my_chip, src_idx=my_chip)

    outs = [dict(), dict(), dict(), dict()]
    for n, _, _ in SHARDED:
        if n == "w_in":
            res = _adamw("adamw_" + n, slots[n], *[jnp.transpose(t, (2, 0, 1)) for t in (weights[n], moms[n], vars_[n])])
            res = [jnp.transpose(t, (1, 2, 0)) for t in res]
        else:
            res = _adamw("adamw_" + n, slots[n], weights[n], moms[n], vars_[n])
        for k in range(4):
            outs[k][n] = res[k]

    sgu_shape = (SGU_G * SGU_C, SGU_C)
    res = _adamw("adamw_sgu_w", late_slots[len(SCAN_CARRIED)], *[t.reshape(sgu_shape) for t in
                                                                  (weights["sgu_w"], moms["sgu_w"], vars_["sgu_w"])])
    for k in range(4):
        outs[k]["sgu_w"] = res[k].reshape(weights["sgu_w"].shape)

    def packed(d):
        return _pack_rows([d[n] for n, _ in PACKED], _SMALL_ROWS, F32)
    small_out = _adamw("adamw_replicated", small_slots, packed(weights), packed(moms), packed(vars_))
    for k in range(4):
        flat = small_out[k].reshape(-1)
        off = 0
        for (n, s), size in zip(PACKED, _SMALL_SIZES):
            outs[k][n] = flat[off:off + size].reshape(s)
            off += size
    loss = small_out[0].reshape(-1)[_LOSS_AT]
    return (loss, dx[None], *[outs[0][n] for n in WEIGHT_ORDER], *[outs[1][n] for n in WEIGHT_ORDER],
            *[outs[2][n] for n in WEIGHT_ORDER], *[outs[3][n] for n in WEIGHT_ORDER])
```

```python
import functools
import numpy as np
import jax
import jax.numpy as jnp
from jax import lax
from jax.experimental import pallas as pl
from jax.experimental.pallas import tpu as pltpu

F32 = jnp.float32
BF16 = jnp.bfloat16

D = 1024
NH, HN = 16, 64
NP, PW = NH // 2, 2 * HN
SGU_G, SGU_C = 8, 128
L_W, L_A, L_G = 64, 64, 160
C_B = 3 * D + L_W + L_A + L_G
P_TOTAL = 2 * D + C_B + 2 * D
D_FF = 4 * D
RW_INT = 3 * D + 128 + 128 + 256
NORM_EPS, LN_EPS, GN_EPS = 1e-6, 1e-5, 64e-5
N_DEV = 8
LANES = 128
SCAN_C = 64
N_KEPT = 4
SOLVE_B = 16
SCAN_PRECISION = lax.Precision.HIGH
SCAN_OUT_PRECISION = lax.Precision.DEFAULT
GRAD_PAYLOAD = BF16
VMEM_LIMIT = 56 * 1024 * 1024
MATMUL_VMEM_BUDGET = 40 * 1024 * 1024
STEP_COST_BYTES = 512 * 1024
HBM_COST_RATIO = 3
ACC_PASS_WEIGHT = 4

ADAM_LR, ADAM_B1, ADAM_B2, ADAM_EPS, ADAM_WD, ADAM_STEP = 0.001, 0.9, 0.999, 1e-08, 0.01, 10

SHARDED = [
    ("w_in", (D, P_TOTAL), 1), ("w_proj_a", (D, D), 0), ("shift_b", (2, C_B), 1), ("w_lora_w", (L_W, D), 1),
    ("a_lora_w", (L_A, D), 1), ("g_lora_w", (L_G, D), 1), ("w_proj_b", (D, D), 0), ("w_out", (D, D), 0),
    ("w_ffn1", (D, D_FF), 1), ("w_ffn2", (D_FF, D), 0),
]
REPLICATED = [
    ("g_mix", (1, D)), ("sgu_ln_w", (1, D)), ("sgu_ln_b", (1, D)), ("sgu_w", (1, SGU_G, SGU_C, SGU_C)),
    ("sgu_b", (1, SGU_G, SGU_C)), ("w0", (1, D)), ("a0", (1, D)), ("k_k", (1, D)), ("k_a", (1, D)), ("r_k", (1, D)),
    ("ln_x_w", (1, D)), ("ln_x_b", (1, D)), ("g_ffn", (1, D)), ("g_final", (D,)),
]
WEIGHT_ORDER = ["g_mix", "w_in", "sgu_ln_w", "sgu_ln_b", "sgu_w", "sgu_b", "w_proj_a", "shift_b", "w_lora_w", "w0",
                "a_lora_w", "a0", "g_lora_w", "k_k", "k_a", "r_k", "ln_x_w", "ln_x_b", "w_proj_b", "w_out", "g_ffn",
                "w_ffn1", "w_ffn2", "g_final"]


def _round_up(n, m):
    return (n + m - 1) // m * m


def _pick(n, target):
    if n <= target:
        return n
    best = None
    for t in range(LANES, target + 1, LANES):
        if n % t == 0:
            best = t
    assert best is not None, (n, target)
    return best


def _matmul(name, a, b, mode, out_dtype=F32, tm=2048, tn=1024, tk=4096, out_blocks=None, epilogue=None, extras=(),
            out_dtypes=(), after=None, whole_rows=False, out_widths=None):
    b_blocks = b.shape[0] if b.ndim == 3 else None
    bshape = b.shape if b.ndim == 2 else (b.shape[1], b.shape[0] * b.shape[2])
    if mode == "nn":
        (M, K), (K2, N) = a.shape, bshape
    elif mode == "nt":
        (M, K), (N, K2) = a.shape, bshape
    else:
        (K, M), (K2, N) = a.shape, bshape
    assert K == K2, (name, a.shape, b.shape)
    assert b_blocks is None or mode != "tn"
    assert out_blocks is None or mode == "tn"
    tn = min(tn, N // (out_blocks or 1), bshape[1] // b_blocks if (b_blocks and mode == "nn") else tn)
    blocked_k = bool(b_blocks) and mode == "nt"
    tm, tn, tk = _pick(M, tm), _pick(N, tn), (K if blocked_k else _pick(K, tk))

    def vmem_bytes(tm, tk):
        tiles = tm * tk * a.dtype.itemsize + tk * tn * b.dtype.itemsize
        for i, dt in enumerate(out_dtypes if epilogue else (out_dtype,)):
            tiles += tm * (out_widths[i] if out_widths else tn) * jnp.dtype(dt).itemsize
        for x in extras:
            arr = x[0] if isinstance(x, tuple) else x
            tiles += (tm if arr.shape[0] > 1 else 1) * tn * arr.dtype.itemsize
        return 2 * tiles + (tm * tn * 4 if K // tk > 1 else 0)

    def cost(tm, tk):
        ni, nj, nk = M // tm, N // tn, K // tk
        steps = ni * nj * nk
        acc_passes = steps * tm * tn * 8 * ACC_PASS_WEIGHT if nk > 1 else 0
        a_reads = M * K * a.dtype.itemsize * (nj if nk > 1 else 1)
        b_reads = K * N * b.dtype.itemsize * (ni if (nj > 1 or nk > 1) else 1)
        return (steps * STEP_COST_BYTES + acc_passes + vmem_bytes(tm, tk) // 2
                + HBM_COST_RATIO * (a_reads + b_reads))

    options = [(m, k) for m in ({M} if whole_rows else {_pick(M, max(t, LANES)) for t in (tm, tm // 2, tm // 4)})
               for k in ({K} if blocked_k else {_pick(K, max(t, LANES)) for t in (tk, tk // 2, tk // 4)})
               if vmem_bytes(m, k) <= MATMUL_VMEM_BUDGET]
    tm, tk = min(options, key=lambda o: cost(*o))
    nk = K // tk
    dims = {"nn": (((1,), (0,)), ((), ())), "nt": (((1,), (1,)), ((), ())), "tn": (((0,), (0,)), ((), ()))}[mode]

    n_x, n_o = len(extras), len(out_dtypes) if epilogue else 1
    n_after = 0 if after is None else 1

    def body(a_ref, b_ref, *rest):
        x_refs, o_refs, acc = rest[:n_x], rest[n_x + n_after:n_x + n_after + n_o], rest[n_x + n_after + n_o:]
        if blocked_k:
            bw = b.shape[2]
            part = sum(lax.dot_general(a_ref[:, blk * bw:(blk + 1) * bw].astype(BF16), b_ref[blk].astype(BF16), dims,
                                       preferred_element_type=F32) for blk in range(b_blocks))
        else:
            part = lax.dot_general(a_ref[...].astype(BF16), b_ref[...].astype(BF16), dims, preferred_element_type=F32)

        def finish(res):
            outs = epilogue(res, *[r[...] for r in x_refs]) if epilogue else (res,)
            for r, v in zip(o_refs, outs):
                r[...] = v.astype(r.dtype)

        if nk == 1:
            finish(part)
            return
        acc_ref, k = acc[0], pl.program_id(2)

        @pl.when(k == 0)
        def _():
            acc_ref[...] = part

        @pl.when(k > 0)
        def _():
            acc_ref[...] += part

        @pl.when(k == nk - 1)
        def _():
            finish(acc_ref[...])

    a_spec = {"nn": pl.BlockSpec((tm, tk), lambda i, j, k: (i, k)), "nt": pl.BlockSpec((tm, tk), lambda i, j, k: (i, k)),
              "tn": pl.BlockSpec((tk, tm), lambda i, j, k: (k, i))}[mode]
    b_spec = {"nn": pl.BlockSpec((tk, tn), lambda i, j, k: (k, j)), "nt": pl.BlockSpec((tn, tk), lambda i, j, k: (j, k)),
              "tn": pl.BlockSpec((tk, tn), lambda i, j, k: (k, j))}[mode]
    if b_blocks and mode == "nn":
        per = b.shape[2] // tn
        b_spec = pl.BlockSpec((None, tk, tn), lambda i, j, k: (j // per, k, j % per))
    elif b_blocks:
        b_spec = pl.BlockSpec((b_blocks, tn, b.shape[2]), lambda i, j, k: (0, j, 0))
    out_spec = pl.BlockSpec((tm, tn), lambda i, j, k: (i, j))
    out_shape = jax.ShapeDtypeStruct((M, N), out_dtype)
    if out_blocks:
        per_o = N // out_blocks // tn
        out_spec = pl.BlockSpec((None, tm, tn), lambda i, j, k: (j // per_o, i, j % per_o))
        out_shape = jax.ShapeDtypeStruct((out_blocks, M, N // out_blocks), out_dtype)
    epi_widths = list(out_widths) if out_widths else [N] * n_o
    assert all(w == N for w in epi_widths) or N == tn
    epi_specs = [pl.BlockSpec((tm, tn if w == N else w), lambda i, j, k: (i, j)) for w in epi_widths]
    x_specs, x_args = [], []
    for x in extras:
        arr, off = x if isinstance(x, tuple) else (x, 0)
        if arr.shape[0] == 1:
            x_specs.append(pl.BlockSpec((1, tn), lambda i, j, k: (0, j)))
        else:
            x_specs.append(pl.BlockSpec((tm, tn), lambda i, j, k, off=off: (i, j + off)))
        x_args.append(arr)
    res = pl.pallas_call(
        body, name=name, grid=(M // tm, N // tn, nk),
        in_specs=[a_spec, b_spec] + x_specs + [pl.BlockSpec(memory_space=pl.ANY)] * n_after,
        out_specs=epi_specs if epilogue else out_spec,
        out_shape=[jax.ShapeDtypeStruct((M, w), dt) for w, dt in zip(epi_widths, out_dtypes)] if epilogue else out_shape,
        scratch_shapes=[pltpu.VMEM((tm, tn), F32)] if nk > 1 else [],
        compiler_params=pltpu.CompilerParams(dimension_semantics=("parallel", "parallel", "arbitrary"),
                                             vmem_limit_bytes=VMEM_LIMIT),
    )(a, b, *x_args, *([after] if n_after else []))
    return res


class Rows:
    def __init__(self, arr, width=None, cb=0):
        self.arr, self.width, self.cb = arr, (arr.shape[1] if width is None else width), cb


class Heads:
    def __init__(self, arr):
        self.arr = arr


class Halo:
    def __init__(self, arr, side):
        self.arr, self.side = arr, side


def _rows_call(name, fn, ins, consts, outs, accs=(), tm=512, with_pid=False):
    T = next(o.arr.shape[1] if isinstance(o, Heads) else o.arr.shape[0] for o in ins if not isinstance(o, Halo))
    tm = min(tm, T)
    n_tiles = T // tm
    n_in, n_c, n_out = len(ins), len(consts), len(outs)
    in_specs, args = [], []
    for o in ins:
        if isinstance(o, Rows):
            in_specs.append(pl.BlockSpec((tm, o.width), lambda i, cb=o.cb: (i, cb)))
        elif isinstance(o, Heads):
            in_specs.append(pl.BlockSpec((NP, tm, PW), lambda i: (0, i, 0)))
        else:
            w = o.arr.shape[1]
            if o.side < 0:
                in_specs.append(pl.BlockSpec((8, w), lambda i: (jnp.maximum(i * (tm // 8) - 1, 0), 0)))
            else:
                in_specs.append(pl.BlockSpec((8, w), lambda i: (jnp.minimum((i + 1) * (tm // 8), T // 8 - 1), 0)))
        args.append(o.arr)
    for c in consts:
        in_specs.append(pl.BlockSpec(c.shape, lambda i, nd=c.ndim: (0,) * nd))
        args.append(c)
    out_specs, out_shape = [], []
    for o in outs:
        if o[0] == "rows":
            out_specs.append(pl.BlockSpec((tm, o[1]), lambda i: (i, 0)))
            out_shape.append(jax.ShapeDtypeStruct((T, o[1]), o[2]))
        else:
            out_specs.append(pl.BlockSpec((NP, tm, PW), lambda i: (0, i, 0)))
            out_shape.append(jax.ShapeDtypeStruct((NP, T, PW), o[1]))
    for shape, dt in accs:
        out_specs.append(pl.BlockSpec(shape, lambda i, nd=len(shape): (0,) * nd))
        out_shape.append(jax.ShapeDtypeStruct(shape, dt))

    def body(*refs):
        i = pl.program_id(0)
        vals = []
        vals = [r[...] for r in refs[:n_in + n_c]]
        res = fn(i, n_tiles, *vals) if with_pid else fn(*vals)
        out_refs = refs[n_in + n_c:]
        for r, v in zip(out_refs[:n_out], res[:n_out]):
            r[...] = v.astype(r.dtype)
        if accs:
            @pl.when(i == 0)
            def _():
                for r in out_refs[n_out:]:
                    r[...] = jnp.zeros_like(r)

            for r, v in zip(out_refs[n_out:], res[n_out:]):
                r[...] += v.astype(r.dtype)

    res = pl.pallas_call(
        body, name=name, grid=(n_tiles,), in_specs=in_specs, out_specs=out_specs, out_shape=out_shape,
        compiler_params=pltpu.CompilerParams(dimension_semantics=("arbitrary",), vmem_limit_bytes=VMEM_LIMIT),
    )(*args)
    return res


def _rms(x, g):
    return x * lax.rsqrt(jnp.mean(x * x, axis=-1, keepdims=True) + NORM_EPS) * g


def _gelu(x):
    return 0.5 * x * (1.0 + lax.erf(x * 0.7071067811865476))


def _sigmoid(x):
    return 1.0 / (1.0 + jnp.exp(-x))


def _bdot(a, b):
    return jnp.dot(a.astype(BF16), b.astype(BF16), preferred_element_type=F32)


def _to_heads(x):
    return jnp.concatenate([x[:, p * PW:(p + 1) * PW][None] for p in range(NP)], axis=0)


def _from_heads(xp):
    return jnp.concatenate([xp[p] for p in range(NP)], axis=-1)


def _head_sum(xp):
    low = lax.broadcasted_iota(jnp.int32, xp.shape, xp.ndim - 1) < HN
    both = jnp.sum(xp, axis=-1, keepdims=True)
    first = jnp.sum(jnp.where(low, xp, 0.0), axis=-1, keepdims=True)
    return jnp.where(low, first, both - first)


def _split_pairs(xp):
    return jnp.concatenate([xp[:, :, :HN], xp[:, :, HN:]], axis=0)


def _join_pairs(xh):
    return jnp.concatenate([xh[:NP], xh[NP:]], axis=-1)


def _sgu_fn(p, ln_w, ln_b, sw, sbt):
    z = _gelu(p)
    u, v = z[:, :D], z[:, D:]
    mu = jnp.mean(v, axis=-1, keepdims=True)
    var = jnp.mean(jnp.square(v - mu), axis=-1, keepdims=True)
    vn = (v - mu) * lax.rsqrt(var + LN_EPS) * ln_w + ln_b
    ri = lax.broadcasted_iota(jnp.int32, (SGU_C, SGU_C), 0)
    ci = lax.broadcasted_iota(jnp.int32, (SGU_C, SGU_C), 1)
    mask = (ci <= ri).astype(F32)
    dg = D // SGU_G
    parts = []
    for g in range(SGU_G):
        parts.append(_bdot(sw[g] * mask, vn[:, g * dg:(g + 1) * dg]) + sbt[:, g:g + 1])
    return u * jnp.concatenate(parts, axis=-1)


def _pre_fn(qr, qk, qv, qxw, qxa, qxg, wl, w0, al, a0, gl, k_k, k_a):
    w = -jax.nn.softplus(-(w0 + _bdot(jnp.tanh(qxw), wl))) - 0.5
    lw = -jnp.exp(w)
    aa = _sigmoid(a0 + _bdot(qxa, al))
    g = _bdot(_sigmoid(qxg), gl)
    kk = _to_heads(qk * k_k)
    kk = kk / jnp.maximum(jnp.sqrt(_head_sum(kk * kk)), 1e-12)
    k2 = qk * (1.0 + (aa - 1.0) * k_a)
    return _to_heads(qr), _to_heads(lw), _to_heads(k2), _to_heads(qv), kk, _to_heads(aa), g


def _post_fn(o, r, k2, v, g, ln_w, ln_b, r_k):
    mu = _head_sum(o) * (1.0 / HN)
    d = o - mu
    var = _head_sum(d * d) * (1.0 / HN)
    on = d * lax.rsqrt(var + GN_EPS) * ln_w + ln_b
    bonus = _head_sum(r * k2 * r_k) * v
    return _from_heads(on + bonus) * g


def _gate_fn(pg, ya, yb):
    return _sigmoid(pg[:, :D]) * ya + _sigmoid(pg[:, D:]) * yb


def _bmm(x, y, cx, cy, out_path=False):
    return lax.dot_general(x, y, (((cx,), (cy,)), ((0,), (0,))),
                           precision=SCAN_OUT_PRECISION if out_path else SCAN_PRECISION, preferred_element_type=F32)


def _unit_lower_inverse(M):
    C = M.shape[1]
    ti = lax.broadcasted_iota(jnp.int32, (C, C), 0)
    tj = lax.broadcasted_iota(jnp.int32, (C, C), 1)
    eye = (ti == tj).astype(F32)
    same = lambda b: (ti // b == tj // b).astype(F32)
    X = -(M * same(SOLVE_B))
    inv = eye + X
    span = 1
    while 2 * span < SOLVE_B:
        X = _bmm(X, X, 2, 1)
        inv = inv + _bmm(inv, X, 2, 1)
        span *= 2
    b = SOLVE_B
    while b < C:
        low = M * (same(2 * b) - same(b))
        inv = inv - _bmm(_bmm(inv, low, 2, 1, out_path=True), inv, 2, 1, out_path=True)
        b *= 2
    return inv


@jax.custom_vjp
def _unit_lower_solve(inv, M, y):
    return _bmm(inv, y, 2, 1)


def _unit_lower_solve_fwd(inv, M, y):
    u = _bmm(inv, y, 2, 1)
    return u, (inv, u)


def _unit_lower_solve_bwd(res, du):
    inv, u = res
    dy = _bmm(inv, du, 1, 1)
    return jnp.zeros_like(inv), -_bmm(dy, u, 2, 2), dy


_unit_lower_solve.defvjp(_unit_lower_solve_fwd, _unit_lower_solve_bwd)


@jax.custom_vjp
def _unit_lower_solved(inv, u, M, y):
    return u


_unit_lower_solved.defvjp(lambda inv, u, M, y: (u, (inv, u)),
                          lambda res, du: (jnp.zeros_like(res[0]), jnp.zeros_like(res[1]))
                          + _unit_lower_solve_bwd(res, du)[1:])


@functools.partial(jax.custom_vjp, nondiff_argnums=(0,))
def _kept(fn, value, *args):
    return value


def _kept_fwd(fn, value, *args):
    return value, args


def _kept_bwd(fn, args, d):
    _, vjp = jax.vjp(fn, *args)
    return (jnp.zeros_like(d),) + tuple(vjp(d))


_kept.defvjp(_kept_fwd, _kept_bwd)


def _sum_over_time(x, reverse):
    C = x.shape[1]
    ti = lax.broadcasted_iota(jnp.int32, (C, C), 0)
    tj = lax.broadcasted_iota(jnp.int32, (C, C), 1)
    ones = jnp.broadcast_to(((tj >= ti) if reverse else (tj <= ti)).astype(BF16), (x.shape[0], C, C))
    hi = x.astype(BF16)
    r1 = x - hi.astype(F32)
    mid = r1.astype(BF16)
    lo = (r1 - mid.astype(F32)).astype(BF16)
    dn = (((2,), (1,)), ((0,), (0,)))
    return sum(lax.dot_general(ones, p, dn, preferred_element_type=F32) for p in (lo, mid, hi))


@jax.custom_vjp
def _time_cumsum(lw):
    return _sum_over_time(lw, reverse=False)


_time_cumsum.defvjp(lambda lw: (_sum_over_time(lw, reverse=False), None),
                    lambda _, d: (_sum_over_time(d, reverse=True),))


def _chunk_fn(S0, r, lw, k, v, kk, a, kept=None):
    C = SCAN_C
    bmm = _bmm
    ti = lax.broadcasted_iota(jnp.int32, (C, C), 0)
    tj = lax.broadcasted_iota(jnp.int32, (C, C), 1)
    incl2 = jnp.concatenate([(tj <= ti).astype(F32)] * 2, axis=1)
    strict = (tj < ti).astype(F32)
    n_mask = jnp.concatenate([jnp.zeros((C, C), F32), strict], axis=1)

    def known(name, fn, *args):
        return fn(*args) if kept is None else _kept(fn, kept[name], *args)

    cum = known("cum", _time_cumsum, lw)
    g_in, g_ex, g_inv = jnp.exp(cum), jnp.exp(cum - lw), jnp.exp(-cum)
    kkt, rt = kk * g_ex, r * g_in
    bk = jnp.concatenate([kk * a * g_inv, k * g_inv], axis=1)
    kr = jnp.concatenate([kkt, rt], axis=1)
    ratios = known("ratios", lambda x, y: bmm(x, y, 2, 2), kr, bk)
    A = ratios[:, :C]
    M = A[:, :, :C] * strict
    zv = jnp.concatenate([jnp.zeros_like(v), v], axis=1)
    s0_side = bmm(kr, S0, 2, 2, out_path=True)
    rhs = s0_side[:, :C] + bmm(A * n_mask, zv, 2, 1, out_path=True)
    if kept is None:
        inv = lax.stop_gradient(_unit_lower_inverse(M))
        y = _unit_lower_solve(inv, M, rhs)
    else:
        inv = kept["inv"]
        y = _unit_lower_solved(inv, kept["y"], M, rhs)
    z = jnp.concatenate([-y, v], axis=1)
    O = s0_side[:, C:] + bmm(ratios[:, C:] * incl2, z, 2, 1, out_path=True)
    g_end = g_in[:, C - 1:C, :]
    S1 = S0 * g_end + bmm(z, bk * g_end, 1, 1, out_path=True)
    return O, S1, dict(cum=cum, ratios=ratios, y=y, inv=inv)


def _scan_fwd(r, lw, k, v, kk, a, ex=None, tb=256):
    assert SCAN_C == HN and 2 * SCAN_C == PW
    T = r.shape[1]
    tb = min(tb, T)
    n_chunks = tb // SCAN_C
    nb = T // tb
    nx = ex.nb if ex else 0

    def body(*refs):
        r_ref, lw_ref, k_ref, v_ref, kk_ref, a_ref = refs[:6]
        x_in, (o_ref, s0_ref), x_out = refs[6:6 + nx], refs[6 + nx:8 + nx], refs[8 + nx:8 + 2 * nx]
        s_ref, sems = refs[8 + 2 * nx], refs[9 + 2 * nx:]

        plan = ex.schedule(nb) if ex else []

        @pl.when(pl.program_id(0) == 0)
        def _():
            s_ref[...] = jnp.zeros_like(s_ref)
            for at, action in plan[:1]:
                action(x_in, x_out, sems)

        def step(c, carry):
            sl = pl.ds(pl.multiple_of(c * SCAN_C, SCAN_C), SCAN_C)
            S0 = s_ref[...]
            O, S1, keep = _chunk_fn(S0, *[_split_pairs(ref[:, sl, :])
                                          for ref in (r_ref, lw_ref, k_ref, v_ref, kk_ref, a_ref)])
            o_ref[:, sl, :] = _join_pairs(O)
            s0_ref[c, 0] = jnp.concatenate([S0, keep["inv"]], axis=-1)
            s0_ref[c, 1] = jnp.concatenate([keep["cum"], keep["y"]], axis=-1)
            s0_ref[c, 2] = keep["ratios"][:, :SCAN_C]
            s0_ref[c, 3] = keep["ratios"][:, SCAN_C:]
            s_ref[...] = S1
            return carry

        lax.fori_loop(0, n_chunks, step, 0)

        for at, action in plan[1:]:
            pl.when(pl.program_id(0) == at)(functools.partial(action, x_in, x_out, sems))

    hm = pl.BlockSpec((NP, tb, PW), lambda i: (0, i, 0))
    res = pl.pallas_call(
        body, name="rwkv_scan_fwd", grid=(nb,), in_specs=[hm] * 6 + (ex.any_specs if ex else []),
        out_specs=[hm, pl.BlockSpec((n_chunks, N_KEPT, NH, HN, PW), lambda i: (i, 0, 0, 0, 0))]
        + (ex.any_specs if ex else []),
        out_shape=[jax.ShapeDtypeStruct((NP, T, PW), F32),
                   jax.ShapeDtypeStruct((T // SCAN_C, N_KEPT, NH, HN, PW), F32)]
        + (ex.out_shape if ex else []),
        scratch_shapes=[pltpu.VMEM((NH, HN, HN), F32)] + (ex.sem_shapes if ex else []),
        compiler_params=pltpu.CompilerParams(dimension_semantics=("arbitrary",), vmem_limit_bytes=VMEM_LIMIT),
    )(r, lw, k, v, kk, a, *(ex.bufs if ex else []))
    return res[0], res[1], list(res[2:])


def _scan_bwd(r, lw, k, v, kk, a, s0s, do, ex=None, tb=128):
    T = r.shape[1]
    tb = min(tb, T)
    n_chunks = tb // SCAN_C
    nb = T // tb
    nx = ex.nb if ex else 0

    def body(*refs):
        r_ref, lw_ref, k_ref, v_ref, kk_ref, a_ref, s0_ref, do_ref = refs[:8]
        x_in, (dr, dlw, dk, dv, dkk, da), x_out = refs[8:8 + nx], refs[8 + nx:14 + nx], refs[14 + nx:14 + 2 * nx]
        ds_ref, sems = refs[14 + 2 * nx], refs[15 + 2 * nx:]

        plan = ex.schedule(nb) if ex else []

        @pl.when(pl.program_id(0) == 0)
        def _():
            ds_ref[...] = jnp.zeros_like(ds_ref)
            for at, action in plan[:1]:
                action(x_in, x_out, sems)

        def step(j, carry):
            c = n_chunks - 1 - j
            sl = pl.ds(pl.multiple_of(c * SCAN_C, SCAN_C), SCAN_C)
            s0_inv, cum_y = s0_ref[c, 0], s0_ref[c, 1]
            kept = dict(inv=s0_inv[:, :, HN:], cum=cum_y[:, :, :HN], y=cum_y[:, :, HN:],
                        ratios=jnp.concatenate([s0_ref[c, 2], s0_ref[c, 3]], axis=1))
            _, vjp = jax.vjp(lambda *t: _chunk_fn(*t, kept=kept)[:2], s0_inv[:, :, :HN],
                             *[_split_pairs(ref[:, sl, :]) for ref in (r_ref, lw_ref, k_ref, v_ref, kk_ref, a_ref)])
            g = vjp((_split_pairs(do_ref[:, sl, :]), ds_ref[...]))
            ds_ref[...] = g[0]
            for ref, val in zip((dr, dlw, dk, dv, dkk, da), g[1:]):
                ref[:, sl, :] = _join_pairs(val)
            return carry

        lax.fori_loop(0, n_chunks, step, 0)

        for at, action in plan[1:]:
            pl.when(pl.program_id(0) == at)(functools.partial(action, x_in, x_out, sems))

    hm = pl.BlockSpec((NP, tb, PW), lambda i: (0, nb - 1 - i, 0))
    res = pl.pallas_call(
        body, name="rwkv_scan_bwd", grid=(nb,),
        in_specs=[hm] * 6 + [pl.BlockSpec((n_chunks, N_KEPT, NH, HN, PW), lambda i: (nb - 1 - i, 0, 0, 0, 0)), hm]
        + (ex.any_specs if ex else []),
        out_specs=[hm] * 6 + (ex.any_specs if ex else []),
        out_shape=[jax.ShapeDtypeStruct((NP, T, PW), F32)] * 6 + (ex.out_shape if ex else []),
        scratch_shapes=[pltpu.VMEM((NH, HN, HN), F32)] + (ex.sem_shapes if ex else []),
        compiler_params=pltpu.CompilerParams(dimension_semantics=("arbitrary",), vmem_limit_bytes=VMEM_LIMIT),
    )(r, lw, k, v, kk, a, s0s, do, *(ex.bufs if ex else []))
    return list(res[:6]), list(res[6:])


def _shift_down(i, p, prev8):
    first = jnp.where(i > 0, prev8[7:8, :], 0.0)
    row = lax.broadcasted_iota(jnp.int32, p.shape, 0)
    return jnp.where(row == 0, first, pltpu.roll(p, 1, axis=0))


def _mix_bwd(dq, p, sb, tm=256):
    def fn(i, n, dq, next8, p, prev8, sb):
        ps = _shift_down(i, p, prev8)
        d1 = dq * sb[1:2]
        last = jnp.where(i < n - 1, next8[0:1, :] * sb[1:2], 0.0)
        row = lax.broadcasted_iota(jnp.int32, dq.shape, 0)
        up = jnp.where(row == dq.shape[0] - 1, last, pltpu.roll(d1, dq.shape[0] - 1, axis=0))
        return (dq * sb[0:1] + up, jnp.sum(dq * p, axis=0, keepdims=True), jnp.sum(dq * ps, axis=0, keepdims=True))
    w = p.shape[1]
    return _rows_call("shift_mix_bwd", fn, [Rows(dq), Halo(dq, +1), Rows(p), Halo(p, -1)], [sb], [("rows", w, BF16)],
                      accs=[((1, w), F32), ((1, w), F32)], tm=tm, with_pid=True)


def _local_step(x, target, W, late_weights=None, early_grads=None, w_in_grads_ready=None, ffn_grads_ready=None):
    G = {}
    a = _rows_call("norm_mix_fwd", lambda x, g: (_rms(x, g),), [Rows(x)], [W["g_mix"]], [("rows", D, BF16)])[0]
    p_sgu = _matmul("proj_sgu", a, W["w_sgu_t"], "nt")
    def token_shift(p, sb0, sb1):
        row = lax.broadcasted_iota(jnp.int32, p.shape, 0)
        return p, p * sb0 + jnp.where(row == 0, 0.0, pltpu.roll(p, 1, axis=0)) * sb1
    p_rw, q = _matmul("proj_rwkv", a, W["w_rw_t"], "nt", tn=512, whole_rows=True, epilogue=token_shift,
                      extras=[W["sb"][0:1], W["sb"][1:2]], out_dtypes=(F32, F32))
    p_gate = _matmul("proj_gate", a, W["w_gate_t"], "nt")

    sgu_consts = [W["sgu_ln_w"], W["sgu_ln_b"], W["sgu_w"], W["sgu_bt"]]
    s = _rows_call("sgu_fwd", lambda *t: (_sgu_fn(*t),), [Rows(p_sgu)], sgu_consts, [("rows", D, BF16)], tm=SGU_C)[0]

    q_ins = [Rows(q, D, 0), Rows(q, D, 1), Rows(q, D, 2), Rows(q, 128, 24), Rows(q, 128, 25), Rows(q, 256, 13)]
    pre_consts = [W["w_lora"], W["w0"], W["a_lora"], W["a0"], W["g_lora"], W["k_k"], W["k_a"]]
    r_h, lw_h, k_h, v_h, kk_h, a_h, g_gate = _rows_call(
        "rwkv_pre_fwd", _pre_fn, q_ins, pre_consts, [("heads", F32)] * 6 + [("rows", D, F32)], tm=128)
    o_h, s0s, got = _scan_fwd(r_h, lw_h, k_h, v_h, kk_h, a_h, ex=late_weights[0] if late_weights else None)
    if late_weights:
        W = {**W, **late_weights[1](got)}
    y_a = _matmul("proj_a", s, W["w_proj_a"], "nn")
    post_ins = [Heads(o_h), Heads(r_h), Heads(k_h), Heads(v_h), Rows(g_gate)]
    post_consts = [W[n].reshape(NP, 1, PW) for n in ("ln_x_w", "ln_x_b", "r_k")]
    z_b = _rows_call("rwkv_post_fwd", lambda *t: (_post_fn(*t),), post_ins, post_consts, [("rows", D, BF16)], tm=128)[0]
    y_b, mixed = _matmul("proj_b", z_b, W["w_proj_b"], "nn", extras=[(p_gate, 0), (p_gate, 1), y_a],
                         epilogue=lambda yb, ga, gb, ya: (yb, _sigmoid(ga) * ya + _sigmoid(gb) * yb),
                         out_dtypes=(F32, BF16))

    def res1(mo, x, g):
        h1 = x + mo
        return h1, _rms(h1, g)
    h1, f = _matmul("proj_out", mixed, W["w_out"], "nn", extras=[x, W["g_ffn"]], epilogue=res1,
                    out_dtypes=(F32, BF16))

    def relu_sq(u):
        r = jnp.maximum(u, 0.0)
        return r, r * r
    r1, act = _matmul("ffn_up", f, W["w_ffn1"], "nn", epilogue=relu_sq, out_dtypes=(BF16, BF16))
    ff = _matmul("ffn_down", act, W["w_ffn2"], "nn")

    def head(h1, ff, tgt, g):
        def f_(h1, ff, g):
            y = _rms(h1 + ff, g)
            return 0.5 * jnp.sum(jnp.mean(jnp.square(y - tgt), axis=-1))
        loss, (dh2, _, dg) = jax.value_and_grad(f_, argnums=(0, 1, 2))(h1, ff, g)
        return dh2, jnp.full((8, LANES), loss, F32), dg
    dh2, loss_acc, G["g_final"] = _rows_call("loss_head", head, [Rows(h1), Rows(ff), Rows(target)], [W["g_final"]],
                                             [("rows", D, F32)], accs=[((8, LANES), F32), ((1, D), F32)])

    d_u1 = _matmul("ffn_down_dx", dh2, W["w_ffn2"], "nt", extras=[r1], out_dtypes=(BF16,),
                   epilogue=lambda d_act, r: (d_act * 2.0 * r.astype(F32),))[0]
    G["w_ffn2"] = _matmul("ffn_down_dw", act, dh2, "tn", out_dtype=GRAD_PAYLOAD)
    d_f = _matmul("ffn_up_dx", d_u1, W["w_ffn1"], "nt")
    G["w_ffn1"] = _matmul("ffn_up_dw", f, d_u1, "tn", out_blocks=N_DEV, out_dtype=GRAD_PAYLOAD)

    def res1_bwd(h1, d_f, dh2, g):
        _, vjp = jax.vjp(_rms, h1, g)
        dh, dg = vjp(d_f)
        return dh2 + dh, dg
    dh1, G["g_ffn"] = _rows_call("residual_norm_bwd", res1_bwd, [Rows(h1), Rows(d_f), Rows(dh2)],
                                 [W["g_ffn"]], [("rows", D, F32)], accs=[((1, D), F32)])
    ffn_token = ffn_grads_ready(G) if ffn_grads_ready else None
    def gate_bwd(d_mixed, ga, gb, ya, yb):
        _, vjp = jax.vjp(_gate_fn, jnp.concatenate([ga, gb], axis=-1), ya, yb)
        return vjp(d_mixed)
    d_gate, d_ya, d_yb = _matmul("proj_out_dx", dh1, W["w_out"], "nt", after=ffn_token, epilogue=gate_bwd,
                                 extras=[(p_gate, 0), (p_gate, 1), y_a, y_b], out_dtypes=(BF16, BF16, BF16),
                                 out_widths=(2 * D, D, D))
    G["w_out"] = _matmul("proj_out_dw", mixed, dh1, "tn", out_dtype=GRAD_PAYLOAD)

    d_s = _matmul("proj_a_dx", d_ya, W["w_proj_a"], "nt")
    G["w_proj_a"] = _matmul("proj_a_dw", s, d_ya, "tn", out_dtype=GRAD_PAYLOAD)

    def sgu_bwd(p, ds, *c):
        _, vjp = jax.vjp(_sgu_fn, p, *c)
        return vjp(ds)
    d_p_sgu, G["sgu_ln_w"], G["sgu_ln_b"], G["sgu_w"], G["sgu_bt"] = _rows_call(
        "sgu_bwd", sgu_bwd, [Rows(p_sgu), Rows(d_s)], sgu_consts, [("rows", 2 * D, BF16)],
        accs=[((1, D), F32), ((1, D), F32), ((SGU_G, SGU_C, SGU_C), F32), ((SGU_C, SGU_G), F32)], tm=SGU_C)

    d_zb = _matmul("proj_b_dx", d_yb, W["w_proj_b"], "nt")
    G["w_proj_b"] = _matmul("proj_b_dw", z_b, d_yb, "tn", out_dtype=GRAD_PAYLOAD)

    def post_bwd(o, r, k2, v, g, dz, *c):
        _, vjp = jax.vjp(_post_fn, o, r, k2, v, g, *c)
        return vjp(dz)
    do_h, dr1, dk1, dv1, d_g, g_lnw, g_lnb, g_rk = _rows_call(
        "rwkv_post_bwd", post_bwd, post_ins + [Rows(d_zb)], post_consts, [("heads", F32)] * 4 + [("rows", D, F32)],
        accs=[((NP, 1, PW), F32)] * 3, tm=128)
    G["ln_x_w"], G["ln_x_b"], G["r_k"] = (t.reshape(1, D) for t in (g_lnw, g_lnb, g_rk))
    (dr2, dlw, dk2, dv2, dkk, daa), early = _scan_bwd(r_h, lw_h, k_h, v_h, kk_h, a_h, s0s, do_h,
                                                      ex=early_grads(G) if early_grads else None)

    def pre_bwd(qr, qk, qv, qxw, qxa, qxg, dr1, dr2, dlw, dk1, dk2, dv1, dv2, dkk, daa, dg, *c):
        _, vjp = jax.vjp(_pre_fn, qr, qk, qv, qxw, qxa, qxg, *c)
        g = vjp((dr1 + dr2, dlw, dk1 + dk2, dv1 + dv2, dkk, daa, dg))
        dq = jnp.concatenate(g[:6], axis=-1)
        return (dq,) + tuple(g[6:])
    pre_b_ins = q_ins + [Heads(dr1), Heads(dr2), Heads(dlw), Heads(dk1), Heads(dk2), Heads(dv1), Heads(dv2),
                         Heads(dkk), Heads(daa), Rows(d_g)]
    d_q, G["w_lora"], G["w0"], G["a_lora"], G["a0"], G["g_lora"], G["k_k"], G["k_a"] = _rows_call(
        "rwkv_pre_bwd", pre_bwd, pre_b_ins, pre_consts, [("rows", RW_INT, F32)],
        accs=[((128, D), F32), ((1, D), F32), ((128, D), F32), ((1, D), F32), ((256, D), F32), ((1, D), F32),
              ((1, D), F32)], tm=128)
    d_p_rw, dsb0, dsb1 = _mix_bwd(d_q, p_rw, W["sb"])
    G["sb"] = jnp.concatenate([dsb0, dsb1], axis=0)

    G["w_sgu_t"] = _matmul("proj_sgu_dw", d_p_sgu, a, "tn", out_dtype=GRAD_PAYLOAD)
    G["w_rw_t"] = _matmul("proj_rwkv_dw", d_p_rw, a, "tn", out_dtype=GRAD_PAYLOAD)
    G["w_gate_t"] = _matmul("proj_gate_dw", d_gate, a, "tn", out_dtype=GRAD_PAYLOAD)
    token = w_in_grads_ready(G) if w_in_grads_ready else None
    da1 = _matmul("proj_sgu_dx", d_p_sgu, W["w_sgu_t"], "nn", after=token)
    da2 = _matmul("proj_rwkv_dx", d_p_rw, W["w_rw_t"], "nn", after=token)
    da3 = _matmul("proj_gate_dx", d_gate, W["w_gate_t"], "nn", after=token)

    def norm1_bwd(x, da1, da2, da3, dh1, g):
        _, vjp = jax.vjp(_rms, x, g)
        dx, dg = vjp(da1 + da2 + da3)
        return dh1 + dx, dg
    dx, G["g_mix"] = _rows_call("norm_mix_bwd", norm1_bwd, [Rows(x), Rows(da1), Rows(da2), Rows(da3), Rows(dh1)],
                                [W["g_mix"]], [("rows", D, F32)], accs=[((1, D), F32)])
    return loss_acc[0, 0], dx, G, early


class Exchange:
    def __init__(self, bufs, gathers):
        self.bufs, self.gathers, self.nb = list(bufs), list(gathers), len(bufs)
        self.any_specs = [pl.BlockSpec(memory_space=pl.ANY)] * self.nb
        self.out_shape = [jax.ShapeDtypeStruct((N_DEV,) + (b.shape if g else b.shape[1:]), b.dtype)
                          for b, g in zip(self.bufs, self.gathers)]
        n = (N_DEV - 1) * self.nb
        self.sem_shapes = [pltpu.SemaphoreType.DMA((n,)), pltpu.SemaphoreType.DMA((n,)),
                           pltpu.SemaphoreType.DMA((self.nb,))]

    def _copies(self, in_refs, out_refs, sems):
        send_sems, recv_sems, local_sems = sems
        x, y, c = lax.axis_index("x"), lax.axis_index("y"), lax.axis_index("c")
        me = 4 * x + 2 * y + c

        def src(b, dest):
            return in_refs[b] if self.gathers[b] else in_refs[b].at[dest]

        local = [pltpu.make_async_copy(src(b, me), out_refs[b].at[me], local_sems.at[b]) for b in range(self.nb)]
        sends, recvs = [], []
        for kbits in range(1, N_DEV):
            px = 1 - x if kbits & 4 else x
            py = 1 - y if kbits & 2 else y
            pc = 1 - c if kbits & 1 else c
            peer = 4 * px + 2 * py + pc
            for b in range(self.nb):
                s = (kbits - 1) * self.nb + b
                sends.append(pltpu.make_async_remote_copy(
                    src_ref=src(b, peer), dst_ref=out_refs[b].at[me], send_sem=send_sems.at[s],
                    recv_sem=recv_sems.at[s], device_id=(px, py, pc), device_id_type=pl.DeviceIdType.MESH))
                recvs.append(pltpu.make_async_remote_copy(
                    src_ref=src(b, peer), dst_ref=out_refs[b].at[peer], send_sem=send_sems.at[s],
                    recv_sem=recv_sems.at[s], device_id=(px, py, pc), device_id_type=pl.DeviceIdType.MESH))
        return local, sends, recvs

    def start(self, in_refs, out_refs, sems):
        local, sends, _ = self._copies(in_refs, out_refs, sems)
        for cp in sends + local:
            cp.start()

    def wait(self, in_refs, out_refs, sems):
        local, sends, recvs = self._copies(in_refs, out_refs, sems)
        for cp in recvs:
            cp.wait_recv()
        for cp in sends:
            cp.wait_send()
        for cp in local:
            cp.wait()

    def schedule(self, n_steps):
        return [(0, self.start), (n_steps - 1, self.wait)]


def _exchange(name, bufs, gather):
    ex = Exchange(bufs, gather if isinstance(gather, (list, tuple)) else [gather] * len(bufs))

    def body(*refs):
        in_refs, out_refs, sems = refs[:ex.nb], refs[ex.nb:2 * ex.nb], refs[2 * ex.nb:]
        ex.start(in_refs, out_refs, sems)
        ex.wait(in_refs, out_refs, sems)

    return pl.pallas_call(body, name=name, in_specs=ex.any_specs, out_specs=ex.any_specs, out_shape=ex.out_shape,
                          scratch_shapes=ex.sem_shapes)(*ex.bufs)


N_CHIP = 4


def _pair_exchange(name, blocks):
    def body(b_ref, got_ref, send_sems, recv_sems):
        x, y, c = lax.axis_index("x"), lax.axis_index("y"), lax.axis_index("c")
        copies = [pltpu.make_async_remote_copy(
            src_ref=b_ref.at[2 * q + 1 - c], dst_ref=got_ref.at[q], send_sem=send_sems.at[q],
            recv_sem=recv_sems.at[q], device_id=(x, y, 1 - c), device_id_type=pl.DeviceIdType.MESH)
            for q in range(N_CHIP)]
        for cp in copies:
            cp.start()
        for cp in copies:
            cp.wait_recv()
        for cp in copies:
            cp.wait_send()

    any_spec = pl.BlockSpec(memory_space=pl.ANY)
    return pl.pallas_call(
        body, name=name, in_specs=[any_spec], out_specs=any_spec,
        out_shape=jax.ShapeDtypeStruct((N_CHIP,) + blocks.shape[1:], blocks.dtype),
        scratch_shapes=[pltpu.SemaphoreType.DMA((N_CHIP,))] * 2,
    )(blocks)


def _pair_sum(name, blocks, got, core):
    _, R, Wd = blocks.shape

    def body(c_ref, a_ref, b_ref, o_ref):
        o_ref[...] = (a_ref[...].astype(F32) + b_ref[...].astype(F32)).astype(o_ref.dtype)

    return pl.pallas_call(
        body, name=name,
        grid_spec=pltpu.PrefetchScalarGridSpec(
            num_scalar_prefetch=1, grid=(N_CHIP,),
            in_specs=[pl.BlockSpec((None, R, Wd), lambda q, c_ref: (2 * q + c_ref[0], 0, 0)),
                      pl.BlockSpec((None, R, Wd), lambda q, c_ref: (q, 0, 0))],
            out_specs=pl.BlockSpec((None, R, Wd), lambda q, c_ref: (q, 0, 0))),
        out_shape=jax.ShapeDtypeStruct(got.shape, blocks.dtype),
        compiler_params=pltpu.CompilerParams(dimension_semantics=("parallel",), vmem_limit_bytes=VMEM_LIMIT),
    )(core, blocks, got)


def _chip_copies(s_ref, land_ref, send_sems, recv_sems):
    x, y, c = lax.axis_index("x"), lax.axis_index("y"), lax.axis_index("c")
    my_q = 2 * x + y
    sends, recvs = [], []
    for kbits in range(1, N_CHIP):
        px = 1 - x if kbits & 2 else x
        py = 1 - y if kbits & 1 else y
        peer_q = 2 * px + py
        sends.append(pltpu.make_async_remote_copy(
            src_ref=s_ref.at[peer_q], dst_ref=land_ref.at[my_q], send_sem=send_sems[kbits - 1],
            recv_sem=recv_sems[kbits - 1], device_id=(px, py, c), device_id_type=pl.DeviceIdType.MESH))
        recvs.append(pltpu.make_async_remote_copy(
            src_ref=s_ref.at[peer_q], dst_ref=land_ref.at[peer_q], send_sem=send_sems[kbits - 1],
            recv_sem=recv_sems[kbits - 1], device_id=(px, py, c), device_id_type=pl.DeviceIdType.MESH))
    return sends, recvs


_HBM = pl.BlockSpec(memory_space=pltpu.HBM)
_SEM = pl.BlockSpec(memory_space=pltpu.SEMAPHORE)
N_CHIP_SEMS = 2 * (N_CHIP - 1)


def _scatter_copies(b_refs, land_refs, send_sems, recv_sems):
    x, y, c = lax.axis_index("x"), lax.axis_index("y"), lax.axis_index("c")
    me = 4 * x + 2 * y + c
    sends, recvs = [], []
    for kbits in range(1, N_DEV):
        px = 1 - x if kbits & 4 else x
        py = 1 - y if kbits & 2 else y
        pc = 1 - c if kbits & 1 else c
        peer = 4 * px + 2 * py + pc
        for b in range(len(b_refs)):
            s = (kbits - 1) * len(b_refs) + b
            sends.append(pltpu.make_async_remote_copy(
                src_ref=b_refs[b].at[peer], dst_ref=land_refs[b].at[me], send_sem=send_sems[s],
                recv_sem=recv_sems[s], device_id=(px, py, pc), device_id_type=pl.DeviceIdType.MESH))
            recvs.append(pltpu.make_async_remote_copy(
                src_ref=b_refs[b].at[peer], dst_ref=land_refs[b].at[peer], send_sem=send_sems[s],
                recv_sem=recv_sems[s], device_id=(px, py, pc), device_id_type=pl.DeviceIdType.MESH))
    return sends, recvs


def _scatter_start(name, bufs):
    nb = len(bufs)
    n = (N_DEV - 1) * nb

    def body(*refs):
        b_refs, land_refs, outs = refs[:nb], refs[nb:2 * nb], refs[2 * nb:]
        sends, _ = _scatter_copies(b_refs, land_refs, outs[:n], outs[n:2 * n])
        for cp in sends:
            cp.start()
        token = outs[2 * n + 2 * nb]
        token[...] = jnp.zeros_like(token)

    thru = tuple(pltpu.HBM(b.shape, b.dtype) for b in bufs)
    res = pl.pallas_call(
        body, name=name, in_specs=(_HBM,) * (2 * nb),
        out_specs=(_SEM,) * (2 * n) + (_HBM,) * (2 * nb) + (pl.BlockSpec(memory_space=pltpu.VMEM),),
        out_shape=(pltpu.SemaphoreType.DMA(()),) * (2 * n) + thru + thru + (jax.ShapeDtypeStruct((8, LANES), F32),),
        input_output_aliases={i: 2 * n + i for i in range(2 * nb)},
        compiler_params=pltpu.CompilerParams(has_side_effects=pltpu.SideEffectType.DATAFLOW_SIDE_EFFECTING),
    )(*[pltpu.with_memory_space_constraint(b, pltpu.HBM) for b in bufs],
      *[pltpu.with_memory_space_constraint(lax.empty(b.shape, b.dtype), pltpu.HBM) for b in bufs])
    return res[:2 * n], list(res[2 * n:2 * n + nb]), list(res[2 * n + nb:2 * n + 2 * nb]), res[2 * n + 2 * nb]


def _scatter_wait(name, sems, bufs_thru, lands_thru, after):
    nb = len(bufs_thru)
    n = (N_DEV - 1) * nb

    def body(*refs):
        b_refs, land_refs, sem_refs = refs[:nb], refs[nb:2 * nb], refs[2 * nb:2 * nb + 2 * n]
        sends, recvs = _scatter_copies(b_refs, land_refs, sem_refs[:n], sem_refs[n:])
        for cp in sends:
            cp.wait_send()
        for cp in recvs:
            cp.wait_recv()

    thru = tuple(pltpu.HBM(b.shape, b.dtype) for b in bufs_thru)
    res = pl.pallas_call(
        body, name=name, in_specs=(_HBM,) * (2 * nb) + (_SEM,) * (2 * n) + (pl.BlockSpec(memory_space=pl.ANY),),
        out_specs=(_HBM,) * (2 * nb), out_shape=thru + thru,
        input_output_aliases={i: i for i in range(2 * nb)},
        compiler_params=pltpu.CompilerParams(has_side_effects=pltpu.SideEffectType.DATAFLOW_SIDE_EFFECTING),
    )(*bufs_thru, *lands_thru, *sems, after)
    return list(res[:nb]), list(res[nb:])


def _chip_exchange_start(name, sums):
    def body(s_ref, land_ref, *outs):
        sems, token = outs[:N_CHIP_SEMS], outs[N_CHIP_SEMS + 2]
        sends, _ = _chip_copies(s_ref, land_ref, sems[:N_CHIP - 1], sems[N_CHIP - 1:])
        for cp in sends:
            cp.start()
        token[...] = jnp.zeros_like(token)

    res = pl.pallas_call(
        body, name=name, in_specs=(_HBM, _HBM),
        out_specs=(_SEM,) * N_CHIP_SEMS + (_HBM, _HBM, pl.BlockSpec(memory_space=pltpu.VMEM)),
        out_shape=(pltpu.SemaphoreType.DMA(()),) * N_CHIP_SEMS
        + (pltpu.HBM(sums.shape, sums.dtype), pltpu.HBM(sums.shape, sums.dtype), jax.ShapeDtypeStruct((8, LANES), F32)),
        input_output_aliases={0: N_CHIP_SEMS, 1: N_CHIP_SEMS + 1},
        compiler_params=pltpu.CompilerParams(has_side_effects=pltpu.SideEffectType.DATAFLOW_SIDE_EFFECTING),
    )(pltpu.with_memory_space_constraint(sums, pltpu.HBM),
      pltpu.with_memory_space_constraint(lax.empty(sums.shape, sums.dtype), pltpu.HBM))
    return res[:N_CHIP_SEMS], res[N_CHIP_SEMS], res[N_CHIP_SEMS + 1], res[N_CHIP_SEMS + 2]


def _chip_exchange_wait(name, sems, sums_thru, land_thru, after):
    def body(s_ref, land_ref, *rest):
        sems = rest[:N_CHIP_SEMS]
        sends, recvs = _chip_copies(s_ref, land_ref, sems[:N_CHIP - 1], sems[N_CHIP - 1:])
        for cp in sends:
            cp.wait_send()
        for cp in recvs:
            cp.wait_recv()

    return pl.pallas_call(
        body, name=name, in_specs=(_HBM, _HBM) + (_SEM,) * N_CHIP_SEMS + (pl.BlockSpec(memory_space=pl.ANY),),
        out_specs=(_HBM, _HBM),
        out_shape=(pltpu.HBM(sums_thru.shape, sums_thru.dtype), pltpu.HBM(sums_thru.shape, sums_thru.dtype)),
        input_output_aliases={0: 0, 1: 1},
        compiler_params=pltpu.CompilerParams(has_side_effects=pltpu.SideEffectType.DATAFLOW_SIDE_EFFECTING),
    )(sums_thru, land_thru, *sems, after)


def _adamw(name, slots, w, m, v, tr=256):
    unit_mid = w.ndim == 3 and w.shape[1] == 1 and w.shape[0] > 1
    R, Wd = (w.shape[0], w.shape[2]) if unit_mid else w.shape[-2:]
    depth_axis = w.ndim == 3 and not unit_mid
    if unit_mid:
        tr, tc = 128, Wd
    elif R % tr == 0:
        tc = Wd
    else:
        tr, tc = R, (256 if (Wd % 256 == 0 and R > 256) else Wd)
    at = (slice(None), 0, slice(None)) if unit_mid else Ellipsis

    def body(s_ref, w_ref, m_ref, v_ref, g_out, d_out, m_out, v_out):
        g = s_ref[0].astype(F32)
        for j in range(1, slots.shape[0]):
            g = g + s_ref[j].astype(F32)
        m_new = ADAM_B1 * m_ref[at] + (1.0 - ADAM_B1) * g
        v_new = ADAM_B2 * v_ref[at] + (1.0 - ADAM_B2) * jnp.square(g)
        m_hat = m_new / (1.0 - ADAM_B1 ** ADAM_STEP)
        v_hat = v_new / (1.0 - ADAM_B2 ** ADAM_STEP)
        g_out[at] = g
        d_out[at] = -ADAM_LR * (m_hat / (jnp.sqrt(v_hat) + ADAM_EPS) + ADAM_WD * w_ref[at])
        m_out[at] = m_new
        v_out[at] = v_new

    if unit_mid:
        row = pl.BlockSpec((tr, 1, tc), lambda i, j: (i, 0, j))
    elif depth_axis:
        row = pl.BlockSpec((None, tr, tc), lambda i, j: (0, i, j))
    else:
        row = pl.BlockSpec((tr, tc), lambda i, j: (i, j))
    return pl.pallas_call(
        body, name=name, grid=(pl.cdiv(R, tr), Wd // tc),
        in_specs=[pl.BlockSpec((slots.shape[0], tr, tc), lambda i, j: (0, i, j)), row, row, row],
        out_specs=[row] * 4, out_shape=[jax.ShapeDtypeStruct(w.shape, F32)] * 4,
        compiler_params=pltpu.CompilerParams(dimension_semantics=("parallel", "parallel"),
                                             vmem_limit_bytes=VMEM_LIMIT),
    )(slots, w, m, v)


PACK_W = 1024
PACKED = [(n, s) for n, s in REPLICATED if n != "sgu_w"]
_SMALL_SIZES = [int(np.prod(s)) for _, s in PACKED]
_SMALL_ROWS = _round_up(_round_up(sum(_SMALL_SIZES) + PACK_W, PACK_W) // PACK_W, 8)
_LOSS_AT = sum(_SMALL_SIZES)
W_IN_SHARD = P_TOTAL // N_DEV


def _pack_rows(parts, rows, dtype):
    flat = jnp.concatenate([p.reshape(-1).astype(dtype) for p in parts])
    return jnp.pad(flat, (0, rows * PACK_W - flat.shape[0])).reshape(rows, PACK_W)


def _w_in_groups_t(blocks):
    wt = blocks.reshape(P_TOTAL, D)
    o, c = 2 * D, 2 * D + 3 * D
    z = lambda r: jnp.zeros((r, D), wt.dtype)
    rw = jnp.concatenate([wt[o:c], wt[c:c + L_W], z(128 - L_W), wt[c + L_W:c + L_W + L_A], z(128 - L_A),
                          wt[c + L_W + L_A:o + C_B], z(256 - L_G)], axis=0)
    return wt[:o], rw, wt[o + C_B:]


def _w_in_grad_blocks(g_sgu_t, g_rw_t, g_gate_t):
    c = 3 * D
    full = jnp.concatenate([g_sgu_t, g_rw_t[:c], g_rw_t[c:c + L_W], g_rw_t[c + 128:c + 128 + L_A],
                            g_rw_t[c + 256:c + 256 + L_G], g_gate_t], axis=0)
    return full.reshape(N_DEV, W_IN_SHARD, D)


def _mesh_index():
    me = 4 * lax.axis_index("x") + 2 * lax.axis_index("y") + lax.axis_index("c")
    return me.astype(jnp.int32).reshape(1)


def _fill_slot(name, dst, src, idx, src_idx=None):
    R, Wd = dst.shape[1:]
    scalars = [idx] if src_idx is None else [idx, src_idx]
    if src_idx is None:
        src_spec = pl.BlockSpec((R, Wd), lambda i, *s: (0, 0))
    else:
        src_spec = pl.BlockSpec((None, R, Wd), lambda i, *s: (s[1][0], 0, 0))

    def body(*refs):
        src_ref, out_ref = refs[len(scalars) + 1], refs[len(scalars) + 2]
        out_ref[...] = src_ref[...]

    return pl.pallas_call(
        body, name=name,
        grid_spec=pltpu.PrefetchScalarGridSpec(
            num_scalar_prefetch=len(scalars), grid=(1,),
            in_specs=[pl.BlockSpec(memory_space=pl.ANY), src_spec],
            out_specs=pl.BlockSpec((None, R, Wd), lambda i, *s: (s[0][0], 0, 0))),
        out_shape=jax.ShapeDtypeStruct(dst.shape, dst.dtype),
        input_output_aliases={len(scalars): 0},
        compiler_params=pltpu.CompilerParams(vmem_limit_bytes=VMEM_LIMIT),
    )(*scalars, dst, src)


class TwoLevelGather:
    N_COPIES = 8
    PART_ALIGN = 16

    def __init__(self, bufs, skip_own=()):
        self.bufs, self.nb, self.skip_own = list(bufs), len(bufs), tuple(skip_own)
        self.any_specs = [pl.BlockSpec(memory_space=pl.ANY)] * self.nb
        self.out_shape = [jax.ShapeDtypeStruct((N_DEV,) + b.shape, b.dtype) for b in self.bufs]
        n = self.N_COPIES * self.nb
        self.sem_shapes = [pltpu.SemaphoreType.DMA((n,)), pltpu.SemaphoreType.DMA((n,)),
                           pltpu.SemaphoreType.DMA((self.nb,))]

    def _parts(self, b):
        shape = self.bufs[b].shape
        first = shape[0] // 2 // self.PART_ALIGN * self.PART_ALIGN if len(shape) == 2 else 0
        return [(0, first), (first, shape[0] - first)] if first else [None]

    def _copies(self, in_refs, out_refs, sems):
        send_sems, recv_sems, local_sems = sems
        nb = self.nb
        x, y, c = lax.axis_index("x"), lax.axis_index("y"), lax.axis_index("c")
        me, sibling = (x, y, c), (x, y, 1 - c)
        near, far = [(1 - x, y), (x, 1 - y)], (1 - x, 1 - y)

        def slot(b, dev):
            return out_refs[b].at[4 * dev[0] + 2 * dev[1] + dev[2]]

        def copy(b, k, block, to, own=False, rows=None):
            src, dst = in_refs[b] if own else slot(b, block), slot(b, block)
            if rows is not None:
                src, dst = src.at[pl.ds(*rows)], dst.at[pl.ds(*rows)]
            return pltpu.make_async_remote_copy(
                src_ref=src, dst_ref=dst, send_sem=send_sems.at[self.N_COPIES * b + k],
                recv_sem=recv_sems.at[self.N_COPIES * b + k], device_id=to, device_id_type=pl.DeviceIdType.MESH)

        ways = [(j, b) for j in range(2) for b in range(nb) if j < len(self._parts(b))]
        cp = {}
        cp["local"] = [pltpu.make_async_copy(in_refs[b], slot(b, me), local_sems.at[b]) for b in range(nb)
                       if b not in self.skip_own]
        cp["first"] = [copy(b, 0, me, sibling, own=True) for b in range(nb)]
        cp["first"] += [copy(b, 1 + j, me, (*near[j], c), own=True) for j in range(2) for b in range(nb)]
        cp["from_near"] = [copy(b, 1 + j, (*near[j], c), me) for j in range(2) for b in range(nb)]
        cp["near_on"] = [[copy(b, 5 + j, (*near[j], c), sibling)]
                         + ([copy(b, 3 + j, (*near[j], c), (*near[1 - j], c), rows=self._parts(b)[j])]
                            if (j, b) in ways else []) for j in range(2) for b in range(nb)]
        cp["from_far"] = [copy(b, 3 + j, (*far, c), me, rows=self._parts(b)[j]) for j, b in ways]
        cp["far_on"] = [copy(b, 7, (*far, c), sibling) for b in range(nb)]
        cp["from_sibling"] = [copy(b, 0, sibling, me) for b in range(nb)]
        cp["from_sibling"] += [copy(b, 5 + j, (*near[j], 1 - c), me) for j in range(2) for b in range(nb)]
        cp["from_sibling"] += [copy(b, 7, (*far, 1 - c), me) for b in range(nb)]
        return cp

    def start(self, in_refs, out_refs, sems):
        cp = self._copies(in_refs, out_refs, sems)
        for c in cp["first"] + cp["local"]:
            c.start()

    def pass_near(self, in_refs, out_refs, sems):
        cp = self._copies(in_refs, out_refs, sems)
        for arrived, onward in zip(cp["from_near"], cp["near_on"]):
            arrived.wait_recv()
            for c in onward:
                c.start()

    def pass_far(self, in_refs, out_refs, sems):
        cp = self._copies(in_refs, out_refs, sems)
        for c in cp["from_far"]:
            c.wait_recv()
        for c in cp["far_on"]:
            c.start()

    def finish(self, in_refs, out_refs, sems):
        cp = self._copies(in_refs, out_refs, sems)
        for c in cp["from_sibling"]:
            c.wait_recv()
        for c in cp["first"] + sum(cp["near_on"], []) + cp["far_on"]:
            c.wait_send()
        for c in cp["local"]:
            c.wait()

    def schedule(self, n_steps):
        return [(0, self.start), (max(n_steps // 2 - 1, 0), self.pass_near), (max(n_steps - 3, 0), self.pass_far),
                (n_steps - 1, self.finish)]


def _all_gather_two_level(name, bufs, skip_own=()):
    ex = TwoLevelGather(bufs, skip_own)

    def body(*refs):
        args = refs[:ex.nb], refs[ex.nb:2 * ex.nb], refs[2 * ex.nb:]
        for _, action in ex.schedule(1):
            action(*args)

    return pl.pallas_call(body, name=name, in_specs=ex.any_specs, out_specs=ex.any_specs, out_shape=ex.out_shape,
                          scratch_shapes=ex.sem_shapes)(*ex.bufs)


def _cols_from_blocks(blk):
    return jnp.transpose(blk, (1, 0, 2)).reshape(blk.shape[1], -1)


def _cols_to_blocks(g):
    r, c = g.shape
    return jnp.transpose(g.reshape(r, N_DEV, c // N_DEV), (1, 0, 2))


FIRST_WEIGHTS = ["w_in", "shift_b", "w_lora_w", "a_lora_w", "g_lora_w"]
LATE_WEIGHTS = ["w_proj_a", "w_proj_b", "w_out", "w_ffn1", "w_ffn2"]
SCAN_CARRIED = ["w_proj_a", "w_proj_b", "w_out"]
FFN_WEIGHTS = ["w_ffn1", "w_ffn2"]


def _late_weights(shards):
    ex = TwoLevelGather([shards[n][0].astype(BF16) for n in LATE_WEIGHTS])

    def finish(results):
        got = dict(zip(LATE_WEIGHTS, results))
        W = {n: got[n].reshape(-1, D) for n in ("w_proj_a", "w_proj_b", "w_out", "w_ffn2")}
        W["w_ffn1"] = got["w_ffn1"].reshape(N_DEV, D, -1)
        return W
    return ex, finish


def _gather_weights(shards):
    def payload(n):
        if n == "w_in":
            return jnp.transpose(shards[n][0]).astype(BF16)
        return shards[n] if n == "shift_b" else shards[n].astype(BF16)
    payloads = [payload(n) for n in FIRST_WEIGHTS]
    got = dict(zip(FIRST_WEIGHTS, _all_gather_two_level("weight_all_gather", payloads, skip_own=(0,))))
    got["w_in"] = _fill_slot("w_in_own_slot", got["w_in"], payloads[0], _mesh_index())
    W = {}
    W["w_sgu_t"], W["w_rw_t"], W["w_gate_t"] = _w_in_groups_t(got["w_in"])
    z = lambda r, c, dt: jnp.zeros((r, c), dt)
    W["w_lora"] = jnp.concatenate([_cols_from_blocks(got["w_lora_w"][:, 0]).astype(F32), z(128 - L_W, D, F32)], axis=0)
    W["a_lora"] = jnp.concatenate([_cols_from_blocks(got["a_lora_w"][:, 0]).astype(F32), z(128 - L_A, D, F32)], axis=0)
    W["g_lora"] = jnp.concatenate([_cols_from_blocks(got["g_lora_w"][:, 0]).astype(F32), z(256 - L_G, D, F32)], axis=0)
    sb = _cols_from_blocks(got["shift_b"][:, 0])
    W["sb"] = jnp.concatenate([sb[:, :3 * D], sb[:, 3 * D:3 * D + L_W], z(2, 128 - L_W, F32),
                               sb[:, 3 * D + L_W:3 * D + L_W + L_A], z(2, 128 - L_A, F32),
                               sb[:, 3 * D + L_W + L_A:], z(2, 256 - L_G, F32)], axis=1)
    return W


def _replicated_weights(rep):
    W = {n: rep[n] for n in ("g_mix", "sgu_ln_w", "sgu_ln_b", "w0", "a0", "k_k", "k_a", "r_k", "ln_x_w", "ln_x_b",
                             "g_ffn")}
    W["g_final"] = rep["g_final"].reshape(1, D)
    W["sgu_w"] = rep["sgu_w"][0]
    W["sgu_bt"] = jnp.transpose(rep["sgu_b"][0])
    return W


def _late_grad_blocks(G):
    return Exchange([G[n].reshape(N_DEV, -1, D) for n in SCAN_CARRIED]
                    + [G["sgu_w"].reshape(SGU_G * SGU_C, SGU_C).astype(GRAD_PAYLOAD)],
                    [False] * len(SCAN_CARRIED) + [True])


def _first_grad_blocks(G):
    sbg = G["sb"]
    c = 3 * D
    sb = jnp.concatenate([sbg[:, :c], sbg[:, c:c + L_W], sbg[:, c + 128:c + 128 + L_A],
                          sbg[:, c + 256:c + 256 + L_G]], axis=1)
    return {
        "shift_b": _cols_to_blocks(sb),
        "w_lora_w": _cols_to_blocks(G["w_lora"][:L_W]), "a_lora_w": _cols_to_blocks(G["a_lora"][:L_A]),
        "g_lora_w": _cols_to_blocks(G["g_lora"][:L_G]),
    }


def _replicated_grads(G):
    small = {n: G[n] for n in ("g_mix", "sgu_ln_w", "sgu_ln_b", "w0", "a0", "k_k", "k_a", "r_k", "ln_x_w", "ln_x_b",
                               "g_ffn", "g_final")}
    small["sgu_w"] = G["sgu_w"]
    small["sgu_b"] = jnp.transpose(G["sgu_bt"])
    return small


def kernel(x, g_mix, w_in, sgu_ln_w, sgu_ln_b, sgu_w, sgu_b, w_proj_a, shift_b, w_lora_w, w0, a_lora_w, a0, g_lora_w, k_k, k_a, r_k, ln_x_w, ln_x_b, w_proj_b, w_out, g_ffn, w_ffn1, w_ffn2, g_final, loss_target, m_g_mix, m_w_in, m_sgu_ln_w, m_sgu_ln_b, m_sgu_w, m_sgu_b, m_w_proj_a, m_shift_b, m_w_lora_w, m_w0, m_a_lora_w, m_a0, m_g_lora_w, m_k_k, m_k_a, m_r_k, m_ln_x_w, m_ln_x_b, m_w_proj_b, m_w_out, m_g_ffn, m_w_ffn1, m_w_ffn2, m_g_final, v_g_mix, v_w_in, v_sgu_ln_w, v_sgu_ln_b, v_sgu_w, v_sgu_b, v_w_proj_a, v_shift_b, v_w_lora_w, v_w0, v_a_lora_w, v_a0, v_g_lora_w, v_k_k, v_k_a, v_r_k, v_ln_x_w, v_ln_x_b, v_w_proj_b, v_w_out, v_g_ffn, v_w_ffn1, v_w_ffn2, v_g_final):
    env = dict(locals())
    weights = {n: env[n] for n in WEIGHT_ORDER}
    moms = {n: env["m_" + n] for n in WEIGHT_ORDER}
    vars_ = {n: env["v_" + n] for n in WEIGHT_ORDER}

    shards = {n: weights[n] for n, _, _ in SHARDED}
    W = _gather_weights(shards)
    W.update(_replicated_weights({n: weights[n] for n, _ in REPLICATED}))
    in_flight = {}

    def send_w_in_grads(G):
        blocks = _w_in_grad_blocks(G["w_sgu_t"], G["w_rw_t"], G["w_gate_t"])
        got = _pair_exchange("grad_pair_exchange", blocks)
        core = lax.axis_index("c").astype(jnp.int32).reshape(1)
        sums = _pair_sum("grad_pair_sum", blocks, got, core)
        in_flight["sems"], in_flight["sums"], in_flight["land"], token = _chip_exchange_start("grad_chip_start", sums)
        return token

    def send_ffn_grads(G):
        in_flight["ffn"] = _scatter_start("grad_ffn_start", [G["w_ffn1"], G["w_ffn2"].reshape(N_DEV, -1, D)])
        return in_flight["ffn"][3]

    loss_part, dx, G, late_slots = _local_step(x[0], loss_target[0], W, late_weights=_late_weights(shards),
                                               early_grads=_late_grad_blocks, w_in_grads_ready=send_w_in_grads,
                                               ffn_grads_ready=send_ffn_grads)

    slots = dict(zip(SCAN_CARRIED, late_slots))
    me = _mesh_index()
    sent, landed = _scatter_wait("grad_ffn_wait", *in_flight["ffn"][:3], after=dx)
    for i, n in enumerate(FFN_WEIGHTS):
        slots[n] = _fill_slot("grad_own_slot_" + n, landed[i], sent[i], me, src_idx=me)
    blocks = _first_grad_blocks(G)
    small = _replicated_grads(G)
    small_parts = [small[n] for n, _ in PACKED] + [jnp.full((PACK_W,), loss_part, F32)]
    rest = [n for n in FIRST_WEIGHTS if n != "w_in"]
    res = _exchange("grad_exchange", [blocks[n] for n in rest] + [_pack_rows(small_parts, _SMALL_ROWS, F32)],
                    [False] * len(rest) + [True])
    slots.update(zip(rest, res[:-1]))
    small_slots = res[-1]
    sums, chip_slots = _chip_exchange_wait("grad_chip_wait", in_flight["sems"], in_flight["sums"], in_flight["land"],
                                           after=small_slots)
    my_chip = (2 * lax.axis_index("x") + lax.axis_index("y")).astype(jnp.int32).reshape(1)
    slots["w_in"] = _fill_slot("grad_own_slot", chip_slots, sums, my_chip, src_idx=my_chip)

    outs = [dict(), dict(), dict(), dict()]
    for n, _, _ in SHARDED:
        if n == "w_in":
            res = _adamw("adamw_" + n, slots[n], *[jnp.transpose(t, (2, 0, 1)) for t in (weights[n], moms[n], vars_[n])])
            res = [jnp.transpose(t, (1, 2, 0)) for t in res]
        else:
            res = _adamw("adamw_" + n, slots[n], weights[n], moms[n], vars_[n])
        for k in range(4):
            outs[k][n] = res[k]

    sgu_shape = (SGU_G * SGU_C, SGU_C)
    res = _adamw("adamw_sgu_w", late_slots[len(SCAN_CARRIED)], *[t.reshape(sgu_shape) for t in
                                                                  (weights["sgu_w"], moms["sgu_w"], vars_["sgu_w"])])
    for k in range(4):
        outs[k]["sgu_w"] = res[k].reshape(weights["sgu_w"].shape)

    def packed(d):
        return _pack_rows([d[n] for n, _ in PACKED], _SMALL_ROWS, F32)
    small_out = _adamw("adamw_replicated", small_slots, packed(weights), packed(moms), packed(vars_))
    for k in range(4):
        flat = small_out[k].reshape(-1)
        off = 0
        for (n, s), size in zip(PACKED, _SMALL_SIZES):
            outs[k][n] = flat[off:off + size].reshape(s)
            off += size
    loss = small_out[0].reshape(-1)[_LOSS_AT]
    return (loss, dx[None], *[outs[0][n] for n in WEIGHT_ORDER], *[outs[1][n] for n in WEIGHT_ORDER],
            *[outs[2][n] for n in WEIGHT_ORDER], *[outs[3][n] for n in WEIGHT_ORDER])
```

```python
import functools
import numpy as np
import jax
import jax.numpy as jnp
from jax import lax
from jax.experimental import pallas as pl
from jax.experimental.pallas import tpu as pltpu

F32 = jnp.float32
BF16 = jnp.bfloat16

D = 1024
NH, HN = 16, 64
NP, PW = NH // 2, 2 * HN
SGU_G, SGU_C = 8, 128
L_W, L_A, L_G = 64, 64, 160
C_B = 3 * D + L_W + L_A + L_G
P_TOTAL = 2 * D + C_B + 2 * D
D_FF = 4 * D
RW_INT = 3 * D + 128 + 128 + 256
NORM_EPS, LN_EPS, GN_EPS = 1e-6, 1e-5, 64e-5
N_DEV = 8
LANES = 128
SCAN_C = 64
N_KEPT = 4
SOLVE_B = 16
SCAN_PRECISION = lax.Precision.HIGH
SCAN_OUT_PRECISION = lax.Precision.DEFAULT
GRAD_PAYLOAD = BF16
VMEM_LIMIT = 56 * 1024 * 1024
MATMUL_VMEM_BUDGET = 40 * 1024 * 1024
STEP_COST_BYTES = 512 * 1024
HBM_COST_RATIO = 3
ACC_PASS_WEIGHT = 4

ADAM_LR, ADAM_B1, ADAM_B2, ADAM_EPS, ADAM_WD, ADAM_STEP = 0.001, 0.9, 0.999, 1e-08, 0.01, 10

SHARDED = [
    ("w_in", (D, P_TOTAL), 1), ("w_proj_a", (D, D), 0), ("shift_b", (2, C_B), 1), ("w_lora_w", (L_W, D), 1),
    ("a_lora_w", (L_A, D), 1), ("g_lora_w", (L_G, D), 1), ("w_proj_b", (D, D), 0), ("w_out", (D, D), 0),
    ("w_ffn1", (D, D_FF), 1), ("w_ffn2", (D_FF, D), 0),
]
REPLICATED = [
    ("g_mix", (1, D)), ("sgu_ln_w", (1, D)), ("sgu_ln_b", (1, D)), ("sgu_w", (1, SGU_G, SGU_C, SGU_C)),
    ("sgu_b", (1, SGU_G, SGU_C)), ("w0", (1, D)), ("a0", (1, D)), ("k_k", (1, D)), ("k_a", (1, D)), ("r_k", (1, D)),
    ("ln_x_w", (1, D)), ("ln_x_b", (1, D)), ("g_ffn", (1, D)), ("g_final", (D,)),
]
WEIGHT_ORDER = ["g_mix", "w_in", "sgu_ln_w", "sgu_ln_b", "sgu_w", "sgu_b", "w_proj_a", "shift_b", "w_lora_w", "w0",
                "a_lora_w", "a0", "g_lora_w", "k_k", "k_a", "r_k", "ln_x_w", "ln_x_b", "w_proj_b", "w_out", "g_ffn",
                "w_ffn1", "w_ffn2", "g_final"]


def _round_up(n, m):
    return (n + m - 1) // m * m


def _pick(n, target):
    if n <= target:
        return n
    best = None
    for t in range(LANES, target + 1, LANES):
        if n % t == 0:
            best = t
    assert best is not None, (n, target)
    return best


def _matmul(name, a, b, mode, out_dtype=F32, tm=2048, tn=1024, tk=4096, out_blocks=None, epilogue=None, extras=(),
            out_dtypes=(), after=None, whole_rows=False, out_widths=None):
    b_blocks = b.shape[0] if b.ndim == 3 else None
    bshape = b.shape if b.ndim == 2 else (b.shape[1], b.shape[0] * b.shape[2])
    if mode == "nn":
        (M, K), (K2, N) = a.shape, bshape
    elif mode == "nt":
        (M, K), (N, K2) = a.shape, bshape
    else:
        (K, M), (K2, N) = a.shape, bshape
    assert K == K2, (name, a.shape, b.shape)
    assert b_blocks is None or mode != "tn"
    assert out_blocks is None or mode == "tn"
    tn = min(tn, N // (out_blocks or 1), bshape[1] // b_blocks if (b_blocks and mode == "nn") else tn)
    blocked_k = bool(b_blocks) and mode == "nt"
    tm, tn, tk = _pick(M, tm), _pick(N, tn), (K if blocked_k else _pick(K, tk))

    def vmem_bytes(tm, tk):
        tiles = tm * tk * a.dtype.itemsize + tk * tn * b.dtype.itemsize
        for i, dt in enumerate(out_dtypes if epilogue else (out_dtype,)):
            tiles += tm * (out_widths[i] if out_widths else tn) * jnp.dtype(dt).itemsize
        for x in extras:
            arr = x[0] if isinstance(x, tuple) else x
            tiles += (tm if arr.shape[0] > 1 else 1) * tn * arr.dtype.itemsize
        return 2 * tiles + (tm * tn * 4 if K // tk > 1 else 0)

    def cost(tm, tk):
        ni, nj, nk = M // tm, N // tn, K // tk
        steps = ni * nj * nk
        acc_passes = steps * tm * tn * 8 * ACC_PASS_WEIGHT if nk > 1 else 0
        a_reads = M * K * a.dtype.itemsize * (nj if nk > 1 else 1)
        b_reads = K * N * b.dtype.itemsize * (ni if (nj > 1 or nk > 1) else 1)
        return (steps * STEP_COST_BYTES + acc_passes + vmem_bytes(tm, tk) // 2
                + HBM_COST_RATIO * (a_reads + b_reads))

    options = [(m, k) for m in ({M} if whole_rows else {_pick(M, max(t, LANES)) for t in (tm, tm // 2, tm // 4)})
               for k in ({K} if blocked_k else {_pick(K, max(t, LANES)) for t in (tk, tk // 2, tk // 4)})
               if vmem_bytes(m, k) <= MATMUL_VMEM_BUDGET]
    tm, tk = min(options, key=lambda o: cost(*o))
    nk = K // tk
    dims = {"nn": (((1,), (0,)), ((), ())), "nt": (((1,), (1,)), ((), ())), "tn": (((0,), (0,)), ((), ()))}[mode]

    n_x, n_o = len(extras), len(out_dtypes) if epilogue else 1
    n_after = 0 if after is None else 1

    def body(a_ref, b_ref, *rest):
        x_refs, o_refs, acc = rest[:n_x], rest[n_x + n_after:n_x + n_after + n_o], rest[n_x + n_after + n_o:]
        if blocked_k:
            bw = b.shape[2]
            part = sum(lax.dot_general(a_ref[:, blk * bw:(blk + 1) * bw].astype(BF16), b_ref[blk].astype(BF16), dims,
                                       preferred_element_type=F32) for blk in range(b_blocks))
        else:
            part = lax.dot_general(a_ref[...].astype(BF16), b_ref[...].astype(BF16), dims, preferred_element_type=F32)

        def finish(res):
            outs = epilogue(res, *[r[...] for r in x_refs]) if epilogue else (res,)
            for r, v in zip(o_refs, outs):
                r[...] = v.astype(r.dtype)

        if nk == 1:
            finish(part)
            return
        acc_ref, k = acc[0], pl.program_id(2)

        @pl.when(k == 0)
        def _():
            acc_ref[...] = part

        @pl.when(k > 0)
        def _():
            acc_ref[...] += part

        @pl.when(k == nk - 1)
        def _():
            finish(acc_ref[...])

    a_spec = {"nn": pl.BlockSpec((tm, tk), lambda i, j, k: (i, k)), "nt": pl.BlockSpec((tm, tk), lambda i, j, k: (i, k)),
              "tn": pl.BlockSpec((tk, tm), lambda i, j, k: (k, i))}[mode]
    b_spec = {"nn": pl.BlockSpec((tk, tn), lambda i, j, k: (k, j)), "nt": pl.BlockSpec((tn, tk), lambda i, j, k: (j, k)),
              "tn": pl.BlockSpec((tk, tn), lambda i, j, k: (k, j))}[mode]
    if b_blocks and mode == "nn":
        per = b.shape[2] // tn
        b_spec = pl.BlockSpec((None, tk, tn), lambda i, j, k: (j // per, k, j % per))
    elif b_blocks:
        b_spec = pl.BlockSpec((b_blocks, tn, b.shape[2]), lambda i, j, k: (0, j, 0))
    out_spec = pl.BlockSpec((tm, tn), lambda i, j, k: (i, j))
    out_shape = jax.ShapeDtypeStruct((M, N), out_dtype)
    if out_blocks:
        per_o = N // out_blocks // tn
        out_spec = pl.BlockSpec((None, tm, tn), lambda i, j, k: (j // per_o, i, j % per_o))
        out_shape = jax.ShapeDtypeStruct((out_blocks, M, N // out_blocks), out_dtype)
    epi_widths = list(out_widths) if out_widths else [N] * n_o
    assert all(w == N for w in epi_widths) or N == tn
    epi_specs = [pl.BlockSpec((tm, tn if w == N else w), lambda i, j, k: (i, j)) for w in epi_widths]
    x_specs, x_args = [], []
    for x in extras:
        arr, off = x if isinstance(x, tuple) else (x, 0)
        if arr.shape[0] == 1:
            x_specs.append(pl.BlockSpec((1, tn), lambda i, j, k: (0, j)))
        else:
            x_specs.append(pl.BlockSpec((tm, tn), lambda i, j, k, off=off: (i, j + off)))
        x_args.append(arr)
    res = pl.pallas_call(
        body, name=name, grid=(M // tm, N // tn, nk),
        in_specs=[a_spec, b_spec] + x_specs + [pl.BlockSpec(memory_space=pl.ANY)] * n_after,
        out_specs=epi_specs if epilogue else out_spec,
        out_shape=[jax.ShapeDtypeStruct((M, w), dt) for w, dt in zip(epi_widths, out_dtypes)] if epilogue else out_shape,
        scratch_shapes=[pltpu.VMEM((tm, tn), F32)] if nk > 1 else [],
        compiler_params=pltpu.CompilerParams(dimension_semantics=("parallel", "parallel", "arbitrary"),
                                             vmem_limit_bytes=VMEM_LIMIT),
    )(a, b, *x_args, *([after] if n_after else []))
    return res


class Rows:
    def __init__(self, arr, width=None, cb=0):
        self.arr, self.width, self.cb = arr, (arr.shape[1] if width is None else width), cb


class Heads:
    def __init__(self, arr):
        self.arr = arr


class Halo:
    def __init__(self, arr, side):
        self.arr, self.side = arr, side


def _rows_call(name, fn, ins, consts, outs, accs=(), tm=512, with_pid=False):
    T = next(o.arr.shape[1] if isinstance(o, Heads) else o.arr.shape[0] for o in ins if not isinstance(o, Halo))
    tm = min(tm, T)
    n_tiles = T // tm
    n_in, n_c, n_out = len(ins), len(consts), len(outs)
    in_specs, args = [], []
    for o in ins:
        if isinstance(o, Rows):
            in_specs.append(pl.BlockSpec((tm, o.width), lambda i, cb=o.cb: (i, cb)))
        elif isinstance(o, Heads):
            in_specs.append(pl.BlockSpec((NP, tm, PW), lambda i: (0, i, 0)))
        else:
            w = o.arr.shape[1]
            if o.side < 0:
                in_specs.append(pl.BlockSpec((8, w), lambda i: (jnp.maximum(i * (tm // 8) - 1, 0), 0)))
            else:
                in_specs.append(pl.BlockSpec((8, w), lambda i: (jnp.minimum((i + 1) * (tm // 8), T // 8 - 1), 0)))
        args.append(o.arr)
    for c in consts:
        in_specs.append(pl.BlockSpec(c.shape, lambda i, nd=c.ndim: (0,) * nd))
        args.append(c)
    out_specs, out_shape = [], []
    for o in outs:
        if o[0] == "rows":
            out_specs.append(pl.BlockSpec((tm, o[1]), lambda i: (i, 0)))
            out_shape.append(jax.ShapeDtypeStruct((T, o[1]), o[2]))
        else:
            out_specs.append(pl.BlockSpec((NP, tm, PW), lambda i: (0, i, 0)))
            out_shape.append(jax.ShapeDtypeStruct((NP, T, PW), o[1]))
    for shape, dt in accs:
        out_specs.append(pl.BlockSpec(shape, lambda i, nd=len(shape): (0,) * nd))
        out_shape.append(jax.ShapeDtypeStruct(shape, dt))

    def body(*refs):
        i = pl.program_id(0)
        vals = []
        vals = [r[...] for r in refs[:n_in + n_c]]
        res = fn(i, n_tiles, *vals) if with_pid else fn(*vals)
        out_refs = refs[n_in + n_c:]
        for r, v in zip(out_refs[:n_out], res[:n_out]):
            r[...] = v.astype(r.dtype)
        if accs:
            @pl.when(i == 0)
            def _():
                for r in out_refs[n_out:]:
                    r[...] = jnp.zeros_like(r)

            for r, v in zip(out_refs[n_out:], res[n_out:]):
                r[...] += v.astype(r.dtype)

    res = pl.pallas_call(
        body, name=name, grid=(n_tiles,), in_specs=in_specs, out_specs=out_specs, out_shape=out_shape,
        compiler_params=pltpu.CompilerParams(dimension_semantics=("arbitrary",), vmem_limit_bytes=VMEM_LIMIT),
    )(*args)
    return res


def _rms(x, g):
    return x * lax.rsqrt(jnp.mean(x * x, axis=-1, keepdims=True) + NORM_EPS) * g


def _gelu(x):
    return 0.5 * x * (1.0 + lax.erf(x * 0.7071067811865476))


def _sigmoid(x):
    return 1.0 / (1.0 + jnp.exp(-x))


def _bdot(a, b):
    return jnp.dot(a.astype(BF16), b.astype(BF16), preferred_element_type=F32)


def _to_heads(x):
    return jnp.concatenate([x[:, p * PW:(p + 1) * PW][None] for p in range(NP)], axis=0)


def _from_heads(xp):
    return jnp.concatenate([xp[p] for p in range(NP)], axis=-1)


def _head_sum(xp):
    low = lax.broadcasted_iota(jnp.int32, xp.shape, xp.ndim - 1) < HN
    both = jnp.sum(xp, axis=-1, keepdims=True)
    first = jnp.sum(jnp.where(low, xp, 0.0), axis=-1, keepdims=True)
    return jnp.where(low, first, both - first)


def _split_pairs(xp):
    return jnp.concatenate([xp[:, :, :HN], xp[:, :, HN:]], axis=0)


def _join_pairs(xh):
    return jnp.concatenate([xh[:NP], xh[NP:]], axis=-1)


def _sgu_fn(p, ln_w, ln_b, sw, sbt):
    z = _gelu(p)
    u, v = z[:, :D], z[:, D:]
    mu = jnp.mean(v, axis=-1, keepdims=True)
    var = jnp.mean(jnp.square(v - mu), axis=-1, keepdims=True)
    vn = (v - mu) * lax.rsqrt(var + LN_EPS) * ln_w + ln_b
    ri = lax.broadcasted_iota(jnp.int32, (SGU_C, SGU_C), 0)
    ci = lax.broadcasted_iota(jnp.int32, (SGU_C, SGU_C), 1)
    mask = (ci <= ri).astype(F32)
    dg = D // SGU_G
    parts = []
    for g in range(SGU_G):
        parts.append(_bdot(sw[g] * mask, vn[:, g * dg:(g + 1) * dg]) + sbt[:, g:g + 1])
    return u * jnp.concatenate(parts, axis=-1)


def _pre_fn(qr, qk, qv, qxw, qxa, qxg, wl, w0, al, a0, gl, k_k, k_a):
    w = -jax.nn.softplus(-(w0 + _bdot(jnp.tanh(qxw), wl))) - 0.5
    lw = -jnp.exp(w)
    aa = _sigmoid(a0 + _bdot(qxa, al))
    g = _bdot(_sigmoid(qxg), gl)
    kk = _to_heads(qk * k_k)
    kk = kk / jnp.maximum(jnp.sqrt(_head_sum(kk * kk)), 1e-12)
    k2 = qk * (1.0 + (aa - 1.0) * k_a)
    return _to_heads(qr), _to_heads(lw), _to_heads(k2), _to_heads(qv), kk, _to_heads(aa), g


def _post_fn(o, r, k2, v, g, ln_w, ln_b, r_k):
    mu = _head_sum(o) * (1.0 / HN)
    d = o - mu
    var = _head_sum(d * d) * (1.0 / HN)
    on = d * lax.rsqrt(var + GN_EPS) * ln_w + ln_b
    bonus = _head_sum(r * k2 * r_k) * v
    return _from_heads(on + bonus) * g


def _gate_fn(pg, ya, yb):
    return _sigmoid(pg[:, :D]) * ya + _sigmoid(pg[:, D:]) * yb


def _bmm(x, y, cx, cy, out_path=False):
    return lax.dot_general(x, y, (((cx,), (cy,)), ((0,), (0,))),
                           precision=SCAN_OUT_PRECISION if out_path else SCAN_PRECISION, preferred_element_type=F32)


def _unit_lower_inverse(M):
    C = M.shape[1]
    ti = lax.broadcasted_iota(jnp.int32, (C, C), 0)
    tj = lax.broadcasted_iota(jnp.int32, (C, C), 1)
    eye = (ti == tj).astype(F32)
    same = lambda b: (ti // b == tj // b).astype(F32)
    X = -(M * same(SOLVE_B))
    inv = eye + X
    span = 1
    while 2 * span < SOLVE_B:
        X = _bmm(X, X, 2, 1)
        inv = inv + _bmm(inv, X, 2, 1)
        span *= 2
    b = SOLVE_B
    while b < C:
        low = M * (same(2 * b) - same(b))
        inv = inv - _bmm(_bmm(inv, low, 2, 1, out_path=True), inv, 2, 1, out_path=True)
        b *= 2
    return inv


@jax.custom_vjp
def _unit_lower_solve(inv, M, y):
    return _bmm(inv, y, 2, 1)


def _unit_lower_solve_fwd(inv, M, y):
    u = _bmm(inv, y, 2, 1)
    return u, (inv, u)


def _unit_lower_solve_bwd(res, du):
    inv, u = res
    dy = _bmm(inv, du, 1, 1)
    return jnp.zeros_like(inv), -_bmm(dy, u, 2, 2), dy


_unit_lower_solve.defvjp(_unit_lower_solve_fwd, _unit_lower_solve_bwd)


@jax.custom_vjp
def _unit_lower_solved(inv, u, M, y):
    return u


_unit_lower_solved.defvjp(lambda inv, u, M, y: (u, (inv, u)),
                          lambda res, du: (jnp.zeros_like(res[0]), jnp.zeros_like(res[1]))
                          + _unit_lower_solve_bwd(res, du)[1:])


@functools.partial(jax.custom_vjp, nondiff_argnums=(0,))
def _kept(fn, value, *args):
    return value


def _kept_fwd(fn, value, *args):
    return value, args


def _kept_bwd(fn, args, d):
    _, vjp = jax.vjp(fn, *args)
    return (jnp.zeros_like(d),) + tuple(vjp(d))


_kept.defvjp(_kept_fwd, _kept_bwd)


def _sum_over_time(x, reverse):
    C = x.shape[1]
    ti = lax.broadcasted_iota(jnp.int32, (C, C), 0)
    tj = lax.broadcasted_iota(jnp.int32, (C, C), 1)
    ones = jnp.broadcast_to(((tj >= ti) if reverse else (tj <= ti)).astype(BF16), (x.shape[0], C, C))
    hi = x.astype(BF16)
    r1 = x - hi.astype(F32)
    mid = r1.astype(BF16)
    lo = (r1 - mid.astype(F32)).astype(BF16)
    dn = (((2,), (1,)), ((0,), (0,)))
    return sum(lax.dot_general(ones, p, dn, preferred_element_type=F32) for p in (lo, mid, hi))


@jax.custom_vjp
def _time_cumsum(lw):
    return _sum_over_time(lw, reverse=False)


_time_cumsum.defvjp(lambda lw: (_sum_over_time(lw, reverse=False), None),
                    lambda _, d: (_sum_over_time(d, reverse=True),))


def _chunk_fn(S0, r, lw, k, v, kk, a, kept=None):
    C = SCAN_C
    bmm = _bmm
    ti = lax.broadcasted_iota(jnp.int32, (C, C), 0)
    tj = lax.broadcasted_iota(jnp.int32, (C, C), 1)
    incl2 = jnp.concatenate([(tj <= ti).astype(F32)] * 2, axis=1)
    strict = (tj < ti).astype(F32)
    n_mask = jnp.concatenate([jnp.zeros((C, C), F32), strict], axis=1)

    def known(name, fn, *args):
        return fn(*args) if kept is None else _kept(fn, kept[name], *args)

    cum = known("cum", _time_cumsum, lw)
    g_in, g_ex, g_inv = jnp.exp(cum), jnp.exp(cum - lw), jnp.exp(-cum)
    kkt, rt = kk * g_ex, r * g_in
    bk = jnp.concatenate([kk * a * g_inv, k * g_inv], axis=1)
    kr = jnp.concatenate([kkt, rt], axis=1)
    ratios = known("ratios", lambda x, y: bmm(x, y, 2, 2), kr, bk)
    A = ratios[:, :C]
    M = A[:, :, :C] * strict
    zv = jnp.concatenate([jnp.zeros_like(v), v], axis=1)
    s0_side = bmm(kr, S0, 2, 2, out_path=True)
    rhs = s0_side[:, :C] + bmm(A * n_mask, zv, 2, 1, out_path=True)
    if kept is None:
        inv = lax.stop_gradient(_unit_lower_inverse(M))
        y = _unit_lower_solve(inv, M, rhs)
    else:
        inv = kept["inv"]
        y = _unit_lower_solved(inv, kept["y"], M, rhs)
    z = jnp.concatenate([-y, v], axis=1)
    O = s0_side[:, C:] + bmm(ratios[:, C:] * incl2, z, 2, 1, out_path=True)
    g_end = g_in[:, C - 1:C, :]
    S1 = S0 * g_end + bmm(z, bk * g_end, 1, 1, out_path=True)
    return O, S1, dict(cum=cum, ratios=ratios, y=y, inv=inv)


def _scan_fwd(r, lw, k, v, kk, a, ex=None, tb=256):
    assert SCAN_C == HN and 2 * SCAN_C == PW
    T = r.shape[1]
    tb = min(tb, T)
    n_chunks = tb // SCAN_C
    nb = T // tb
    nx = ex.nb if ex else 0

    def body(*refs):
        r_ref, lw_ref, k_ref, v_ref, kk_ref, a_ref = refs[:6]
        x_in, (o_ref, s0_ref), x_out = refs[6:6 + nx], refs[6 + nx:8 + nx], refs[8 + nx:8 + 2 * nx]
        s_ref, sems = refs[8 + 2 * nx], refs[9 + 2 * nx:]

        plan = ex.schedule(nb) if ex else []

        @pl.when(pl.program_id(0) == 0)
        def _():
            s_ref[...] = jnp.zeros_like(s_ref)
            for at, action in plan[:1]:
                action(x_in, x_out, sems)

        def step(c, carry):
            sl = pl.ds(pl.multiple_of(c * SCAN_C, SCAN_C), SCAN_C)
            S0 = s_ref[...]
            O, S1, keep = _chunk_fn(S0, *[_split_pairs(ref[:, sl, :])
                                          for ref in (r_ref, lw_ref, k_ref, v_ref, kk_ref, a_ref)])
            o_ref[:, sl, :] = _join_pairs(O)
            s0_ref[c, 0] = jnp.concatenate([S0, keep["inv"]], axis=-1)
            s0_ref[c, 1] = jnp.concatenate([keep["cum"], keep["y"]], axis=-1)
            s0_ref[c, 2] = keep["ratios"][:, :SCAN_C]
            s0_ref[c, 3] = keep["ratios"][:, SCAN_C:]
            s_ref[...] = S1
            return carry

        lax.fori_loop(0, n_chunks, step, 0)

        for at, action in plan[1:]:
            pl.when(pl.program_id(0) == at)(functools.partial(action, x_in, x_out, sems))

    hm = pl.BlockSpec((NP, tb, PW), lambda i: (0, i, 0))
    res = pl.pallas_call(
        body, name="rwkv_scan_fwd", grid=(nb,), in_specs=[hm] * 6 + (ex.any_specs if ex else []),
        out_specs=[hm, pl.BlockSpec((n_chunks, N_KEPT, NH, HN, PW), lambda i: (i, 0, 0, 0, 0))]
        + (ex.any_specs if ex else []),
        out_shape=[jax.ShapeDtypeStruct((NP, T, PW), F32),
                   jax.ShapeDtypeStruct((T // SCAN_C, N_KEPT, NH, HN, PW), F32)]
        + (ex.out_shape if ex else []),
        scratch_shapes=[pltpu.VMEM((NH, HN, HN), F32)] + (ex.sem_shapes if ex else []),
        compiler_params=pltpu.CompilerParams(dimension_semantics=("arbitrary",), vmem_limit_bytes=VMEM_LIMIT),
    )(r, lw, k, v, kk, a, *(ex.bufs if ex else []))
    return res[0], res[1], list(res[2:])


def _scan_bwd(r, lw, k, v, kk, a, s0s, do, ex=None, tb=128):
    T = r.shape[1]
    tb = min(tb, T)
    n_chunks = tb // SCAN_C
    nb = T // tb
    nx = ex.nb if ex else 0

    def body(*refs):
        r_ref, lw_ref, k_ref, v_ref, kk_ref, a_ref, s0_ref, do_ref = refs[:8]
        x_in, (dr, dlw, dk, dv, dkk, da), x_out = refs[8:8 + nx], refs[8 + nx:14 + nx], refs[14 + nx:14 + 2 * nx]
        ds_ref, sems = refs[14 + 2 * nx], refs[15 + 2 * nx:]

        plan = ex.schedule(nb) if ex else []

        @pl.when(pl.program_id(0) == 0)
        def _():
            ds_ref[...] = jnp.zeros_like(ds_ref)
            for at, action in plan[:1]:
                action(x_in, x_out, sems)

        def step(j, carry):
            c = n_chunks - 1 - j
            sl = pl.ds(pl.multiple_of(c * SCAN_C, SCAN_C), SCAN_C)
            s0_inv, cum_y = s0_ref[c, 0], s0_ref[c, 1]
            kept = dict(inv=s0_inv[:, :, HN:], cum=cum_y[:, :, :HN], y=cum_y[:, :, HN:],
                        ratios=jnp.concatenate([s0_ref[c, 2], s0_ref[c, 3]], axis=1))
            _, vjp = jax.vjp(lambda *t: _chunk_fn(*t, kept=kept)[:2], s0_inv[:, :, :HN],
                             *[_split_pairs(ref[:, sl, :]) for ref in (r_ref, lw_ref, k_ref, v_ref, kk_ref, a_ref)])
            g = vjp((_split_pairs(do_ref[:, sl, :]), ds_ref[...]))
            ds_ref[...] = g[0]
            for ref, val in zip((dr, dlw, dk, dv, dkk, da), g[1:]):
                ref[:, sl, :] = _join_pairs(val)
            return carry

        lax.fori_loop(0, n_chunks, step, 0)

        for at, action in plan[1:]:
            pl.when(pl.program_id(0) == at)(functools.partial(action, x_in, x_out, sems))

    hm = pl.BlockSpec((NP, tb, PW), lambda i: (0, nb - 1 - i, 0))
    res = pl.pallas_call(
        body, name="rwkv_scan_bwd", grid=(nb,),
        in_specs=[hm] * 6 + [pl.BlockSpec((n_chunks, N_KEPT, NH, HN, PW), lambda i: (nb - 1 - i, 0, 0, 0, 0)), hm]
        + (ex.any_specs if ex else []),
        out_specs=[hm] * 6 + (ex.any_specs if ex else []),
        out_shape=[jax.ShapeDtypeStruct((NP, T, PW), F32)] * 6 + (ex.out_shape if ex else []),
        scratch_shapes=[pltpu.VMEM((NH, HN, HN), F32)] + (ex.sem_shapes if ex else []),
        compiler_params=pltpu.CompilerParams(dimension_semantics=("arbitrary",), vmem_limit_bytes=VMEM_LIMIT),
    )(r, lw, k, v, kk, a, s0s, do, *(ex.bufs if ex else []))
    return list(res[:6]), list(res[6:])


def _shift_down(i, p, prev8):
    first = jnp.where(i > 0, prev8[7:8, :], 0.0)
    row = lax.broadcasted_iota(jnp.int32, p.shape, 0)
    return jnp.where(row == 0, first, pltpu.roll(p, 1, axis=0))


def _mix_bwd(dq, p, sb, tm=256):
    def fn(i, n, dq, next8, p, prev8, sb):
        ps = _shift_down(i, p, prev8)
        d1 = dq * sb[1:2]
        last = jnp.where(i < n - 1, next8[0:1, :] * sb[1:2], 0.0)
        row = lax.broadcasted_iota(jnp.int32, dq.shape, 0)
        up = jnp.where(row == dq.shape[0] - 1, last, pltpu.roll(d1, dq.shape[0] - 1, axis=0))
        return (dq * sb[0:1] + up, jnp.sum(dq * p, axis=0, keepdims=True), jnp.sum(dq * ps, axis=0, keepdims=True))
    w = p.shape[1]
    return _rows_call("shift_mix_bwd", fn, [Rows(dq), Halo(dq, +1), Rows(p), Halo(p, -1)], [sb], [("rows", w, BF16)],
                      accs=[((1, w), F32), ((1, w), F32)], tm=tm, with_pid=True)


def _local_step(x, target, W, late_weights=None, early_grads=None, w_in_grads_ready=None, ffn_grads_ready=None):
    G = {}
    a = _rows_call("norm_mix_fwd", lambda x, g: (_rms(x, g),), [Rows(x)], [W["g_mix"]], [("rows", D, BF16)])[0]
    p_sgu = _matmul("proj_sgu", a, W["w_sgu_t"], "nt")
    def token_shift(p, sb0, sb1):
        row = lax.broadcasted_iota(jnp.int32, p.shape, 0)
        return p, p * sb0 + jnp.where(row == 0, 0.0, pltpu.roll(p, 1, axis=0)) * sb1
    p_rw, q = _matmul("proj_rwkv", a, W["w_rw_t"], "nt", tn=512, whole_rows=True, epilogue=token_shift,
                      extras=[W["sb"][0:1], W["sb"][1:2]], out_dtypes=(F32, F32))
    p_gate = _matmul("proj_gate", a, W["w_gate_t"], "nt")

    sgu_consts = [W["sgu_ln_w"], W["sgu_ln_b"], W["sgu_w"], W["sgu_bt"]]
    s = _rows_call("sgu_fwd", lambda *t: (_sgu_fn(*t),), [Rows(p_sgu)], sgu_consts, [("rows", D, BF16)], tm=SGU_C)[0]

    q_ins = [Rows(q, D, 0), Rows(q, D, 1), Rows(q, D, 2), Rows(q, 128, 24), Rows(q, 128, 25), Rows(q, 256, 13)]
    pre_consts = [W["w_lora"], W["w0"], W["a_lora"], W["a0"], W["g_lora"], W["k_k"], W["k_a"]]
    r_h, lw_h, k_h, v_h, kk_h, a_h, g_gate = _rows_call(
        "rwkv_pre_fwd", _pre_fn, q_ins, pre_consts, [("heads", F32)] * 6 + [("rows", D, F32)], tm=128)
    o_h, s0s, got = _scan_fwd(r_h, lw_h, k_h, v_h, kk_h, a_h, ex=late_weights[0] if late_weights else None)
    if late_weights:
        W = {**W, **late_weights[1](got)}
    y_a = _matmul("proj_a", s, W["w_proj_a"], "nn")
    post_ins = [Heads(o_h), Heads(r_h), Heads(k_h), Heads(v_h), Rows(g_gate)]
    post_consts = [W[n].reshape(NP, 1, PW) for n in ("ln_x_w", "ln_x_b", "r_k")]
    z_b = _rows_call("rwkv_post_fwd", lambda *t: (_post_fn(*t),), post_ins, post_consts, [("rows", D, BF16)], tm=128)[0]
    y_b, mixed = _matmul("proj_b", z_b, W["w_proj_b"], "nn", extras=[(p_gate, 0), (p_gate, 1), y_a],
                         epilogue=lambda yb, ga, gb, ya: (yb, _sigmoid(ga) * ya + _sigmoid(gb) * yb),
                         out_dtypes=(F32, BF16))

    def res1(mo, x, g):
        h1 = x + mo
        return h1, _rms(h1, g)
    h1, f = _matmul("proj_out", mixed, W["w_out"], "nn", extras=[x, W["g_ffn"]], epilogue=res1,
                    out_dtypes=(F32, BF16))

    def relu_sq(u):
        r = jnp.maximum(u, 0.0)
        return r, r * r
    r1, act = _matmul("ffn_up", f, W["w_ffn1"], "nn", epilogue=relu_sq, out_dtypes=(BF16, BF16))
    ff = _matmul("ffn_down", act, W["w_ffn2"], "nn")

    def head(h1, ff, tgt, g):
        def f_(h1, ff, g):
            y = _rms(h1 + ff, g)
            return 0.5 * jnp.sum(jnp.mean(jnp.square(y - tgt), axis=-1))
        loss, (dh2, _, dg) = jax.value_and_grad(f_, argnums=(0, 1, 2))(h1, ff, g)
        return dh2, jnp.full((8, LANES), loss, F32), dg
    dh2, loss_acc, G["g_final"] = _rows_call("loss_head", head, [Rows(h1), Rows(ff), Rows(target)], [W["g_final"]],
                                             [("rows", D, F32)], accs=[((8, LANES), F32), ((1, D), F32)])

    d_u1 = _matmul("ffn_down_dx", dh2, W["w_ffn2"], "nt", extras=[r1], out_dtypes=(BF16,),
                   epilogue=lambda d_act, r: (d_act * 2.0 * r.astype(F32),))[0]
    G["w_ffn2"] = _matmul("ffn_down_dw", act, dh2, "tn", out_dtype=GRAD_PAYLOAD)
    d_f = _matmul("ffn_up_dx", d_u1, W["w_ffn1"], "nt")
    G["w_ffn1"] = _matmul("ffn_up_dw", f, d_u1, "tn", out_blocks=N_DEV, out_dtype=GRAD_PAYLOAD)

    def res1_bwd(h1, d_f, dh2, g):
        _, vjp = jax.vjp(_rms, h1, g)
        dh, dg = vjp(d_f)
        return dh2 + dh, dg
    dh1, G["g_ffn"] = _rows_call("residual_norm_bwd", res1_bwd, [Rows(h1), Rows(d_f), Rows(dh2)],
                                 [W["g_ffn"]], [("rows", D, F32)], accs=[((1, D), F32)])
    ffn_token = ffn_grads_ready(G) if ffn_grads_ready else None
    def gate_bwd(d_mixed, ga, gb, ya, yb):
        _, vjp = jax.vjp(_gate_fn, jnp.concatenate([ga, gb], axis=-1), ya, yb)
        return vjp(d_mixed)
    d_gate, d_ya, d_yb = _matmul("proj_out_dx", dh1, W["w_out"], "nt", after=ffn_token, epilogue=gate_bwd,
                                 extras=[(p_gate, 0), (p_gate, 1), y_a, y_b], out_dtypes=(BF16, BF16, BF16),
                                 out_widths=(2 * D, D, D))
    G["w_out"] = _matmul("proj_out_dw", mixed, dh1, "tn", out_dtype=GRAD_PAYLOAD)

    d_s = _matmul("proj_a_dx", d_ya, W["w_proj_a"], "nt")
    G["w_proj_a"] = _matmul("proj_a_dw", s, d_ya, "tn", out_dtype=GRAD_PAYLOAD)

    def sgu_bwd(p, ds, *c):
        _, vjp = jax.vjp(_sgu_fn, p, *c)
        return vjp(ds)
    d_p_sgu, G["sgu_ln_w"], G["sgu_ln_b"], G["sgu_w"], G["sgu_bt"] = _rows_call(
        "sgu_bwd", sgu_bwd, [Rows(p_sgu), Rows(d_s)], sgu_consts, [("rows", 2 * D, BF16)],
        accs=[((1, D), F32), ((1, D), F32), ((SGU_G, SGU_C, SGU_C), F32), ((SGU_C, SGU_G), F32)], tm=SGU_C)

    d_zb = _matmul("proj_b_dx", d_yb, W["w_proj_b"], "nt")
    G["w_proj_b"] = _matmul("proj_b_dw", z_b, d_yb, "tn", out_dtype=GRAD_PAYLOAD)

    def post_bwd(o, r, k2, v, g, dz, *c):
        _, vjp = jax.vjp(_post_fn, o, r, k2, v, g, *c)
        return vjp(dz)
    do_h, dr1, dk1, dv1, d_g, g_lnw, g_lnb, g_rk = _rows_call(
        "rwkv_post_bwd", post_bwd, post_ins + [Rows(d_zb)], post_consts, [("heads", F32)] * 4 + [("rows", D, F32)],
        accs=[((NP, 1, PW), F32)] * 3, tm=128)
    G["ln_x_w"], G["ln_x_b"], G["r_k"] = (t.reshape(1, D) for t in (g_lnw, g_lnb, g_rk))
    (dr2, dlw, dk2, dv2, dkk, daa), early = _scan_bwd(r_h, lw_h, k_h, v_h, kk_h, a_h, s0s, do_h,
                                                      ex=early_grads(G) if early_grads else None)

    def pre_bwd(qr, qk, qv, qxw, qxa, qxg, dr1, dr2, dlw, dk1, dk2, dv1, dv2, dkk, daa, dg, *c):
        _, vjp = jax.vjp(_pre_fn, qr, qk, qv, qxw, qxa, qxg, *c)
        g = vjp((dr1 + dr2, dlw, dk1 + dk2, dv1 + dv2, dkk, daa, dg))
        dq = jnp.concatenate(g[:6], axis=-1)
        return (dq,) + tuple(g[6:])
    pre_b_ins = q_ins + [Heads(dr1), Heads(dr2), Heads(dlw), Heads(dk1), Heads(dk2), Heads(dv1), Heads(dv2),
                         Heads(dkk), Heads(daa), Rows(d_g)]
    d_q, G["w_lora"], G["w0"], G["a_lora"], G["a0"], G["g_lora"], G["k_k"], G["k_a"] = _rows_call(
        "rwkv_pre_bwd", pre_bwd, pre_b_ins, pre_consts, [("rows", RW_INT, F32)],
        accs=[((128, D), F32), ((1, D), F32), ((128, D), F32), ((1, D), F32), ((256, D), F32), ((1, D), F32),
              ((1, D), F32)], tm=128)
    d_p_rw, dsb0, dsb1 = _mix_bwd(d_q, p_rw, W["sb"])
    G["sb"] = jnp.concatenate([dsb0, dsb1], axis=0)

    G["w_sgu_t"] = _matmul("proj_sgu_dw", d_p_sgu, a, "tn", out_dtype=GRAD_PAYLOAD)
    G["w_rw_t"] = _matmul("proj_rwkv_dw", d_p_rw, a, "tn", out_dtype=GRAD_PAYLOAD)
    G["w_gate_t"] = _matmul("proj_gate_dw", d_gate, a, "tn", out_dtype=GRAD_PAYLOAD)
    token = w_in_grads_ready(G) if w_in_grads_ready else None
    da1 = _matmul("proj_sgu_dx", d_p_sgu, W["w_sgu_t"], "nn", after=token)
    da2 = _matmul("proj_rwkv_dx", d_p_rw, W["w_rw_t"], "nn", after=token)
    da3 = _matmul("proj_gate_dx", d_gate, W["w_gate_t"], "nn", after=token)

    def norm1_bwd(x, da1, da2, da3, dh1, g):
        _, vjp = jax.vjp(_rms, x, g)
        dx, dg = vjp(da1 + da2 + da3)
        return dh1 + dx, dg
    dx, G["g_mix"] = _rows_call("norm_mix_bwd", norm1_bwd, [Rows(x), Rows(da1), Rows(da2), Rows(da3), Rows(dh1)],
                                [W["g_mix"]], [("rows", D, F32)], accs=[((1, D), F32)])
    return loss_acc[0, 0], dx, G, early


class Exchange:
    def __init__(self, bufs, gathers):
        self.bufs, self.gathers, self.nb = list(bufs), list(gathers), len(bufs)
        self.any_specs = [pl.BlockSpec(memory_space=pl.ANY)] * self.nb
        self.out_shape = [jax.ShapeDtypeStruct((N_DEV,) + (b.shape if g else b.shape[1:]), b.dtype)
                          for b, g in zip(self.bufs, self.gathers)]
        n = (N_DEV - 1) * self.nb
        self.sem_shapes = [pltpu.SemaphoreType.DMA((n,)), pltpu.SemaphoreType.DMA((n,)),
                           pltpu.SemaphoreType.DMA((self.nb,))]

    def _copies(self, in_refs, out_refs, sems):
        send_sems, recv_sems, local_sems = sems
        x, y, c = lax.axis_index("x"), lax.axis_index("y"), lax.axis_index("c")
        me = 4 * x + 2 * y + c

        def src(b, dest):
            return in_refs[b] if self.gathers[b] else in_refs[b].at[dest]

        local = [pltpu.make_async_copy(src(b, me), out_refs[b].at[me], local_sems.at[b]) for b in range(self.nb)]
        sends, recvs = [], []
        for kbits in range(1, N_DEV):
            px = 1 - x if kbits & 4 else x
            py = 1 - y if kbits & 2 else y
            pc = 1 - c if kbits & 1 else c
            peer = 4 * px + 2 * py + pc
            for b in range(self.nb):
                s = (kbits - 1) * self.nb + b
                sends.append(pltpu.make_async_remote_copy(
                    src_ref=src(b, peer), dst_ref=out_refs[b].at[me], send_sem=send_sems.at[s],
                    recv_sem=recv_sems.at[s], device_id=(px, py, pc), device_id_type=pl.DeviceIdType.MESH))
                recvs.append(pltpu.make_async_remote_copy(
                    src_ref=src(b, peer), dst_ref=out_refs[b].at[peer], send_sem=send_sems.at[s],
                    recv_sem=recv_sems.at[s], device_id=(px, py, pc), device_id_type=pl.DeviceIdType.MESH))
        return local, sends, recvs

    def start(self, in_refs, out_refs, sems):
        local, sends, _ = self._copies(in_refs, out_refs, sems)
        for cp in sends + local:
            cp.start()

    def wait(self, in_refs, out_refs, sems):
        local, sends, recvs = self._copies(in_refs, out_refs, sems)
        for cp in recvs:
            cp.wait_recv()
        for cp in sends:
            cp.wait_send()
        for cp in local:
            cp.wait()

    def schedule(self, n_steps):
        return [(0, self.start), (n_steps - 1, self.wait)]


def _exchange(name, bufs, gather):
    ex = Exchange(bufs, gather if isinstance(gather, (list, tuple)) else [gather] * len(bufs))

    def body(*refs):
        in_refs, out_refs, sems = refs[:ex.nb], refs[ex.nb:2 * ex.nb], refs[2 * ex.nb:]
        ex.start(in_refs, out_refs, sems)
        ex.wait(in_refs, out_refs, sems)

    return pl.pallas_call(body, name=name, in_specs=ex.any_specs, out_specs=ex.any_specs, out_shape=ex.out_shape,
                          scratch_shapes=ex.sem_shapes)(*ex.bufs)


N_CHIP = 4


def _pair_exchange(name, blocks):
    def body(b_ref, got_ref, send_sems, recv_sems):
        x, y, c = lax.axis_index("x"), lax.axis_index("y"), lax.axis_index("c")
        copies = [pltpu.make_async_remote_copy(
            src_ref=b_ref.at[2 * q + 1 - c], dst_ref=got_ref.at[q], send_sem=send_sems.at[q],
            recv_sem=recv_sems.at[q], device_id=(x, y, 1 - c), device_id_type=pl.DeviceIdType.MESH)
            for q in range(N_CHIP)]
        for cp in copies:
            cp.start()
        for cp in copies:
            cp.wait_recv()
        for cp in copies:
            cp.wait_send()

    any_spec = pl.BlockSpec(memory_space=pl.ANY)
    return pl.pallas_call(
        body, name=name, in_specs=[any_spec], out_specs=any_spec,
        out_shape=jax.ShapeDtypeStruct((N_CHIP,) + blocks.shape[1:], blocks.dtype),
        scratch_shapes=[pltpu.SemaphoreType.DMA((N_CHIP,))] * 2,
    )(blocks)


def _pair_sum(name, blocks, got, core):
    _, R, Wd = blocks.shape

    def body(c_ref, a_ref, b_ref, o_ref):
        o_ref[...] = (a_ref[...].astype(F32) + b_ref[...].astype(F32)).astype(o_ref.dtype)

    return pl.pallas_call(
        body, name=name,
        grid_spec=pltpu.PrefetchScalarGridSpec(
            num_scalar_prefetch=1, grid=(N_CHIP,),
            in_specs=[pl.BlockSpec((None, R, Wd), lambda q, c_ref: (2 * q + c_ref[0], 0, 0)),
                      pl.BlockSpec((None, R, Wd), lambda q, c_ref: (q, 0, 0))],
            out_specs=pl.BlockSpec((None, R, Wd), lambda q, c_ref: (q, 0, 0))),
        out_shape=jax.ShapeDtypeStruct(got.shape, blocks.dtype),
        compiler_params=pltpu.CompilerParams(dimension_semantics=("parallel",), vmem_limit_bytes=VMEM_LIMIT),
    )(core, blocks, got)


def _chip_copies(s_ref, land_ref, send_sems, recv_sems):
    x, y, c = lax.axis_index("x"), lax.axis_index("y"), lax.axis_index("c")
    my_q = 2 * x + y
    sends, recvs = [], []
    for kbits in range(1, N_CHIP):
        px = 1 - x if kbits & 2 else x
        py = 1 - y if kbits & 1 else y
        peer_q = 2 * px + py
        sends.append(pltpu.make_async_remote_copy(
            src_ref=s_ref.at[peer_q], dst_ref=land_ref.at[my_q], send_sem=send_sems[kbits - 1],
            recv_sem=recv_sems[kbits - 1], device_id=(px, py, c), device_id_type=pl.DeviceIdType.MESH))
        recvs.append(pltpu.make_async_remote_copy(
            src_ref=s_ref.at[peer_q], dst_ref=land_ref.at[peer_q], send_sem=send_sems[kbits - 1],
            recv_sem=recv_sems[kbits - 1], device_id=(px, py, c), device_id_type=pl.DeviceIdType.MESH))
    return sends, recvs


_HBM = pl.BlockSpec(memory_space=pltpu.HBM)
_SEM = pl.BlockSpec(memory_space=pltpu.SEMAPHORE)
N_CHIP_SEMS = 2 * (N_CHIP - 1)


def _scatter_copies(b_refs, land_refs, send_sems, recv_sems):
    x, y, c = lax.axis_index("x"), lax.axis_index("y"), lax.axis_index("c")
    me = 4 * x + 2 * y + c
    sends, recvs = [], []
    for kbits in range(1, N_DEV):
        px = 1 - x if kbits & 4 else x
        py = 1 - y if kbits & 2 else y
        pc = 1 - c if kbits & 1 else c
        peer = 4 * px + 2 * py + pc
        for b in range(len(b_refs)):
            s = (kbits - 1) * len(b_refs) + b
            sends.append(pltpu.make_async_remote_copy(
                src_ref=b_refs[b].at[peer], dst_ref=land_refs[b].at[me], send_sem=send_sems[s],
                recv_sem=recv_sems[s], device_id=(px, py, pc), device_id_type=pl.DeviceIdType.MESH))
            recvs.append(pltpu.make_async_remote_copy(
                src_ref=b_refs[b].at[peer], dst_ref=land_refs[b].at[peer], send_sem=send_sems[s],
                recv_sem=recv_sems[s], device_id=(px, py, pc), device_id_type=pl.DeviceIdType.MESH))
    return sends, recvs


def _scatter_start(name, bufs):
    nb = len(bufs)
    n = (N_DEV - 1) * nb

    def body(*refs):
        b_refs, land_refs, outs = refs[:nb], refs[nb:2 * nb], refs[2 * nb:]
        sends, _ = _scatter_copies(b_refs, land_refs, outs[:n], outs[n:2 * n])
        for cp in sends:
            cp.start()
        token = outs[2 * n + 2 * nb]
        token[...] = jnp.zeros_like(token)

    thru = tuple(pltpu.HBM(b.shape, b.dtype) for b in bufs)
    res = pl.pallas_call(
        body, name=name, in_specs=(_HBM,) * (2 * nb),
        out_specs=(_SEM,) * (2 * n) + (_HBM,) * (2 * nb) + (pl.BlockSpec(memory_space=pltpu.VMEM),),
        out_shape=(pltpu.SemaphoreType.DMA(()),) * (2 * n) + thru + thru + (jax.ShapeDtypeStruct((8, LANES), F32),),
        input_output_aliases={i: 2 * n + i for i in range(2 * nb)},
        compiler_params=pltpu.CompilerParams(has_side_effects=pltpu.SideEffectType.DATAFLOW_SIDE_EFFECTING),
    )(*[pltpu.with_memory_space_constraint(b, pltpu.HBM) for b in bufs],
      *[pltpu.with_memory_space_constraint(lax.empty(b.shape, b.dtype), pltpu.HBM) for b in bufs])
    return res[:2 * n], list(res[2 * n:2 * n + nb]), list(res[2 * n + nb:2 * n + 2 * nb]), res[2 * n + 2 * nb]


def _scatter_wait(name, sems, bufs_thru, lands_thru, after):
    nb = len(bufs_thru)
    n = (N_DEV - 1) * nb

    def body(*refs):
        b_refs, land_refs, sem_refs = refs[:nb], refs[nb:2 * nb], refs[2 * nb:2 * nb + 2 * n]
        sends, recvs = _scatter_copies(b_refs, land_refs, sem_refs[:n], sem_refs[n:])
        for cp in sends:
            cp.wait_send()
        for cp in recvs:
            cp.wait_recv()

    thru = tuple(pltpu.HBM(b.shape, b.dtype) for b in bufs_thru)
    res = pl.pallas_call(
        body, name=name, in_specs=(_HBM,) * (2 * nb) + (_SEM,) * (2 * n) + (pl.BlockSpec(memory_space=pl.ANY),),
        out_specs=(_HBM,) * (2 * nb), out_shape=thru + thru,
        input_output_aliases={i: i for i in range(2 * nb)},
        compiler_params=pltpu.CompilerParams(has_side_effects=pltpu.SideEffectType.DATAFLOW_SIDE_EFFECTING),
    )(*bufs_thru, *lands_thru, *sems, after)
    return list(res[:nb]), list(res[nb:])


def _chip_exchange_start(name, sums):
    def body(s_ref, land_ref, *outs):
        sems, token = outs[:N_CHIP_SEMS], outs[N_CHIP_SEMS + 2]
        sends, _ = _chip_copies(s_ref, land_ref, sems[:N_CHIP - 1], sems[N_CHIP - 1:])
        for cp in sends:
            cp.start()
        token[...] = jnp.zeros_like(token)

    res = pl.pallas_call(
        body, name=name, in_specs=(_HBM, _HBM),
        out_specs=(_SEM,) * N_CHIP_SEMS + (_HBM, _HBM, pl.BlockSpec(memory_space=pltpu.VMEM)),
        out_shape=(pltpu.SemaphoreType.DMA(()),) * N_CHIP_SEMS
        + (pltpu.HBM(sums.shape, sums.dtype), pltpu.HBM(sums.shape, sums.dtype), jax.ShapeDtypeStruct((8, LANES), F32)),
        input_output_aliases={0: N_CHIP_SEMS, 1: N_CHIP_SEMS + 1},
        compiler_params=pltpu.CompilerParams(has_side_effects=pltpu.SideEffectType.DATAFLOW_SIDE_EFFECTING),
    )(pltpu.with_memory_space_constraint(sums, pltpu.HBM),
      pltpu.with_memory_space_constraint(lax.empty(sums.shape, sums.dtype), pltpu.HBM))
    return res[:N_CHIP_SEMS], res[N_CHIP_SEMS], res[N_CHIP_SEMS + 1], res[N_CHIP_SEMS + 2]


def _chip_exchange_wait(name, sems, sums_thru, land_thru, after):
    def body(s_ref, land_ref, *rest):
        sems = rest[:N_CHIP_SEMS]
        sends, recvs = _chip_copies(s_ref, land_ref, sems[:N_CHIP - 1], sems[N_CHIP - 1:])
        for cp in sends:
            cp.wait_send()
        for cp in recvs:
            cp.wait_recv()

    return pl.pallas_call(
        body, name=name, in_specs=(_HBM, _HBM) + (_SEM,) * N_CHIP_SEMS + (pl.BlockSpec(memory_space=pl.ANY),),
        out_specs=(_HBM, _HBM),
        out_shape=(pltpu.HBM(sums_thru.shape, sums_thru.dtype), pltpu.HBM(sums_thru.shape, sums_thru.dtype)),
        input_output_aliases={0: 0, 1: 1},
        compiler_params=pltpu.CompilerParams(has_side_effects=pltpu.SideEffectType.DATAFLOW_SIDE_EFFECTING),
    )(sums_thru, land_thru, *sems, after)


def _adamw(name, slots, w, m, v, tr=256):
    unit_mid = w.ndim == 3 and w.shape[1] == 1 and w.shape[0] > 1
    R, Wd = (w.shape[0], w.shape[2]) if unit_mid else w.shape[-2:]
    depth_axis = w.ndim == 3 and not unit_mid
    if unit_mid:
        tr, tc = 128, Wd
    elif R % tr == 0:
        tc = Wd
    else:
        tr, tc = R, (256 if (Wd % 256 == 0 and R > 256) else Wd)
    at = (slice(None), 0, slice(None)) if unit_mid else Ellipsis

    def body(s_ref, w_ref, m_ref, v_ref, g_out, d_out, m_out, v_out):
        g = s_ref[0].astype(F32)
        for j in range(1, slots.shape[0]):
            g = g + s_ref[j].astype(F32)
        m_new = ADAM_B1 * m_ref[at] + (1.0 - ADAM_B1) * g
        v_new = ADAM_B2 * v_ref[at] + (1.0 - ADAM_B2) * jnp.square(g)
        m_hat = m_new / (1.0 - ADAM_B1 ** ADAM_STEP)
        v_hat = v_new / (1.0 - ADAM_B2 ** ADAM_STEP)
        g_out[at] = g
        d_out[at] = -ADAM_LR * (m_hat / (jnp.sqrt(v_hat) + ADAM_EPS) + ADAM_WD * w_ref[at])
        m_out[at] = m_new
        v_out[at] = v_new

    if unit_mid:
        row = pl.BlockSpec((tr, 1, tc), lambda i, j: (i, 0, j))
    elif depth_axis:
        row = pl.BlockSpec((None, tr, tc), lambda i, j: (0, i, j))
    else:
        row = pl.BlockSpec((tr, tc), lambda i, j: (i, j))
    return pl.pallas_call(
        body, name=name, grid=(pl.cdiv(R, tr), Wd // tc),
        in_specs=[pl.BlockSpec((slots.shape[0], tr, tc), lambda i, j: (0, i, j)), row, row, row],
        out_specs=[row] * 4, out_shape=[jax.ShapeDtypeStruct(w.shape, F32)] * 4,
        compiler_params=pltpu.CompilerParams(dimension_semantics=("parallel", "parallel"),
                                             vmem_limit_bytes=VMEM_LIMIT),
    )(slots, w, m, v)


PACK_W = 1024
PACKED = [(n, s) for n, s in REPLICATED if n != "sgu_w"]
_SMALL_SIZES = [int(np.prod(s)) for _, s in PACKED]
_SMALL_ROWS = _round_up(_round_up(sum(_SMALL_SIZES) + PACK_W, PACK_W) // PACK_W, 8)
_LOSS_AT = sum(_SMALL_SIZES)
W_IN_SHARD = P_TOTAL // N_DEV


def _pack_rows(parts, rows, dtype):
    flat = jnp.concatenate([p.reshape(-1).astype(dtype) for p in parts])
    return jnp.pad(flat, (0, rows * PACK_W - flat.shape[0])).reshape(rows, PACK_W)


_W_IN_SEGMENTS = [(2 * D, 0, 0), (3 * D, 1, 0), (L_W, 1, 3 * D), (L_A, 1, 3 * D + 128), (L_G, 1, 3 * D + 256),
                  (2 * D, 2, 0)]
_W_IN_PADS = [(3 * D + L_W, 128 - L_W), (3 * D + 128 + L_A, 128 - L_A), (3 * D + 256 + L_G, 256 - L_G)]
_W_IN_GROUP_ROWS = [2 * D, 3 * D + 128 + 128 + 256, 2 * D]
assert sum(rows for rows, _, _ in _W_IN_SEGMENTS) == P_TOTAL


def _w_in_pieces(j):
    pieces, first = [], 0
    for rows, group, to in _W_IN_SEGMENTS:
        lo, hi = max(first, W_IN_SHARD * j), min(first + rows, W_IN_SHARD * (j + 1))
        if lo < hi:
            pieces.append((lo - W_IN_SHARD * j, hi - lo, group, to + lo - first))
        first += rows
    return pieces


def _w_in_groups_t(blocks):
    def body(x_ref, *o_refs):
        @pl.when(pl.program_id(0) == 0)
        def _():
            for at, rows in _W_IN_PADS:
                o_refs[1][pl.ds(at, rows), :] = jnp.zeros((rows, D), blocks.dtype)
        for j in range(N_DEV):
            @pl.when(pl.program_id(0) == j)
            def _(j=j):
                for at, rows, group, to in _w_in_pieces(j):
                    o_refs[group][pl.ds(to, rows), :] = x_ref[pl.ds(at, rows), :]

    return pl.pallas_call(
        body, name="w_in_groups", grid=(N_DEV,),
        in_specs=[pl.BlockSpec((None, W_IN_SHARD, D), lambda j: (j, 0, 0))],
        out_specs=[pl.BlockSpec((r, D), lambda j: (0, 0)) for r in _W_IN_GROUP_ROWS],
        out_shape=[jax.ShapeDtypeStruct((r, D), blocks.dtype) for r in _W_IN_GROUP_ROWS],
        compiler_params=pltpu.CompilerParams(dimension_semantics=("arbitrary",), vmem_limit_bytes=VMEM_LIMIT),
    )(blocks)


def _w_in_grad_blocks(g_sgu_t, g_rw_t, g_gate_t):
    def body(*refs):
        g_refs, o_ref = refs[:3], refs[3]
        for j in range(N_DEV):
            @pl.when(pl.program_id(0) == j)
            def _(j=j):
                for at, rows, group, to in _w_in_pieces(j):
                    o_ref[pl.ds(at, rows), :] = g_refs[group][pl.ds(to, rows), :]

    return pl.pallas_call(
        body, name="w_in_grad_blocks", grid=(N_DEV,),
        in_specs=[pl.BlockSpec((r, D), lambda j: (0, 0)) for r in _W_IN_GROUP_ROWS],
        out_specs=pl.BlockSpec((None, W_IN_SHARD, D), lambda j: (j, 0, 0)),
        out_shape=jax.ShapeDtypeStruct((N_DEV, W_IN_SHARD, D), g_sgu_t.dtype),
        compiler_params=pltpu.CompilerParams(dimension_semantics=("arbitrary",), vmem_limit_bytes=VMEM_LIMIT),
    )(g_sgu_t, g_rw_t, g_gate_t)


def _mesh_index():
    me = 4 * lax.axis_index("x") + 2 * lax.axis_index("y") + lax.axis_index("c")
    return me.astype(jnp.int32).reshape(1)


def _fill_slot(name, dst, src, idx, src_idx=None):
    R, Wd = dst.shape[1:]
    scalars = [idx] if src_idx is None else [idx, src_idx]
    if src_idx is None:
        src_spec = pl.BlockSpec((R, Wd), lambda i, *s: (0, 0))
    else:
        src_spec = pl.BlockSpec((None, R, Wd), lambda i, *s: (s[1][0], 0, 0))

    def body(*refs):
        src_ref, out_ref = refs[len(scalars) + 1], refs[len(scalars) + 2]
        out_ref[...] = src_ref[...]

    return pl.pallas_call(
        body, name=name,
        grid_spec=pltpu.PrefetchScalarGridSpec(
            num_scalar_prefetch=len(scalars), grid=(1,),
            in_specs=[pl.BlockSpec(memory_space=pl.ANY), src_spec],
            out_specs=pl.BlockSpec((None, R, Wd), lambda i, *s: (s[0][0], 0, 0))),
        out_shape=jax.ShapeDtypeStruct(dst.shape, dst.dtype),
        input_output_aliases={len(scalars): 0},
        compiler_params=pltpu.CompilerParams(vmem_limit_bytes=VMEM_LIMIT),
    )(*scalars, dst, src)


class TwoLevelGather:
    N_COPIES = 8
    PART_ALIGN = 16

    def __init__(self, bufs, skip_own=()):
        self.bufs, self.nb, self.skip_own = list(bufs), len(bufs), tuple(skip_own)
        self.any_specs = [pl.BlockSpec(memory_space=pl.ANY)] * self.nb
        self.out_shape = [jax.ShapeDtypeStruct((N_DEV,) + b.shape, b.dtype) for b in self.bufs]
        n = self.N_COPIES * self.nb
        self.sem_shapes = [pltpu.SemaphoreType.DMA((n,)), pltpu.SemaphoreType.DMA((n,)),
                           pltpu.SemaphoreType.DMA((self.nb,))]

    def _parts(self, b):
        shape = self.bufs[b].shape
        first = shape[0] // 2 // self.PART_ALIGN * self.PART_ALIGN if len(shape) == 2 else 0
        return [(0, first), (first, shape[0] - first)] if first else [None]

    def _copies(self, in_refs, out_refs, sems):
        send_sems, recv_sems, local_sems = sems
        nb = self.nb
        x, y, c = lax.axis_index("x"), lax.axis_index("y"), lax.axis_index("c")
        me, sibling = (x, y, c), (x, y, 1 - c)
        near, far = [(1 - x, y), (x, 1 - y)], (1 - x, 1 - y)

        def slot(b, dev):
            return out_refs[b].at[4 * dev[0] + 2 * dev[1] + dev[2]]

        def copy(b, k, block, to, own=False, rows=None):
            src, dst = in_refs[b] if own else slot(b, block), slot(b, block)
            if rows is not None:
                src, dst = src.at[pl.ds(*rows)], dst.at[pl.ds(*rows)]
            return pltpu.make_async_remote_copy(
                src_ref=src, dst_ref=dst, send_sem=send_sems.at[self.N_COPIES * b + k],
                recv_sem=recv_sems.at[self.N_COPIES * b + k], device_id=to, device_id_type=pl.DeviceIdType.MESH)

        ways = [(j, b) for j in range(2) for b in range(nb) if j < len(self._parts(b))]
        cp = {}
        cp["local"] = [pltpu.make_async_copy(in_refs[b], slot(b, me), local_sems.at[b]) for b in range(nb)
                       if b not in self.skip_own]
        cp["first"] = [copy(b, 0, me, sibling, own=True) for b in range(nb)]
        cp["first"] += [copy(b, 1 + j, me, (*near[j], c), own=True) for j in range(2) for b in range(nb)]
        cp["from_near"] = [copy(b, 1 + j, (*near[j], c), me) for j in range(2) for b in range(nb)]
        cp["near_on"] = [[copy(b, 5 + j, (*near[j], c), sibling)]
                         + ([copy(b, 3 + j, (*near[j], c), (*near[1 - j], c), rows=self._parts(b)[j])]
                            if (j, b) in ways else []) for j in range(2) for b in range(nb)]
        cp["from_far"] = [copy(b, 3 + j, (*far, c), me, rows=self._parts(b)[j]) for j, b in ways]
        cp["far_on"] = [copy(b, 7, (*far, c), sibling) for b in range(nb)]
        cp["from_sibling"] = [copy(b, 0, sibling, me) for b in range(nb)]
        cp["from_sibling"] += [copy(b, 5 + j, (*near[j], 1 - c), me) for j in range(2) for b in range(nb)]
        cp["from_sibling"] += [copy(b, 7, (*far, 1 - c), me) for b in range(nb)]
        return cp

    def start(self, in_refs, out_refs, sems):
        cp = self._copies(in_refs, out_refs, sems)
        for c in cp["first"] + cp["local"]:
            c.start()

    def pass_near(self, in_refs, out_refs, sems):
        cp = self._copies(in_refs, out_refs, sems)
        for arrived, onward in zip(cp["from_near"], cp["near_on"]):
            arrived.wait_recv()
            for c in onward:
                c.start()

    def pass_far(self, in_refs, out_refs, sems):
        cp = self._copies(in_refs, out_refs, sems)
        for c in cp["from_far"]:
            c.wait_recv()
        for c in cp["far_on"]:
            c.start()

    def finish(self, in_refs, out_refs, sems):
        cp = self._copies(in_refs, out_refs, sems)
        for c in cp["from_sibling"]:
            c.wait_recv()
        for c in cp["first"] + sum(cp["near_on"], []) + cp["far_on"]:
            c.wait_send()
        for c in cp["local"]:
            c.wait()

    def schedule(self, n_steps):
        return [(0, self.start), (max(n_steps // 2 - 1, 0), self.pass_near), (max(n_steps - 3, 0), self.pass_far),
                (n_steps - 1, self.finish)]


def _all_gather_two_level(name, bufs, skip_own=()):
    ex = TwoLevelGather(bufs, skip_own)

    def body(*refs):
        args = refs[:ex.nb], refs[ex.nb:2 * ex.nb], refs[2 * ex.nb:]
        for _, action in ex.schedule(1):
            action(*args)

    return pl.pallas_call(body, name=name, in_specs=ex.any_specs, out_specs=ex.any_specs, out_shape=ex.out_shape,
                          scratch_shapes=ex.sem_shapes)(*ex.bufs)


def _cols_from_blocks(blk):
    return jnp.transpose(blk, (1, 0, 2)).reshape(blk.shape[1], -1)


def _cols_to_blocks(g):
    r, c = g.shape
    return jnp.transpose(g.reshape(r, N_DEV, c // N_DEV), (1, 0, 2))


FIRST_WEIGHTS = ["w_in", "shift_b", "w_lora_w", "a_lora_w", "g_lora_w"]
LATE_WEIGHTS = ["w_proj_a", "w_proj_b", "w_out", "w_ffn1", "w_ffn2"]
SCAN_CARRIED = ["w_proj_a", "w_proj_b", "w_out"]
FFN_WEIGHTS = ["w_ffn1", "w_ffn2"]


def _late_weights(shards):
    ex = TwoLevelGather([shards[n][0].astype(BF16) for n in LATE_WEIGHTS])

    def finish(results):
        got = dict(zip(LATE_WEIGHTS, results))
        W = {n: got[n].reshape(-1, D) for n in ("w_proj_a", "w_proj_b", "w_out", "w_ffn2")}
        W["w_ffn1"] = got["w_ffn1"].reshape(N_DEV, D, -1)
        return W
    return ex, finish


def _gather_weights(shards):
    def payload(n):
        if n == "w_in":
            return jnp.transpose(shards[n][0]).astype(BF16)
        return shards[n] if n == "shift_b" else shards[n].astype(BF16)
    payloads = [payload(n) for n in FIRST_WEIGHTS]
    got = dict(zip(FIRST_WEIGHTS, _all_gather_two_level("weight_all_gather", payloads, skip_own=(0,))))
    got["w_in"] = _fill_slot("w_in_own_slot", got["w_in"], payloads[0], _mesh_index())
    W = {}
    W["w_sgu_t"], W["w_rw_t"], W["w_gate_t"] = _w_in_groups_t(got["w_in"])
    z = lambda r, c, dt: jnp.zeros((r, c), dt)
    W["w_lora"] = jnp.concatenate([_cols_from_blocks(got["w_lora_w"][:, 0]).astype(F32), z(128 - L_W, D, F32)], axis=0)
    W["a_lora"] = jnp.concatenate([_cols_from_blocks(got["a_lora_w"][:, 0]).astype(F32), z(128 - L_A, D, F32)], axis=0)
    W["g_lora"] = jnp.concatenate([_cols_from_blocks(got["g_lora_w"][:, 0]).astype(F32), z(256 - L_G, D, F32)], axis=0)
    sb = _cols_from_blocks(got["shift_b"][:, 0])
    W["sb"] = jnp.concatenate([sb[:, :3 * D], sb[:, 3 * D:3 * D + L_W], z(2, 128 - L_W, F32),
                               sb[:, 3 * D + L_W:3 * D + L_W + L_A], z(2, 128 - L_A, F32),
                               sb[:, 3 * D + L_W + L_A:], z(2, 256 - L_G, F32)], axis=1)
    return W


def _replicated_weights(rep):
    W = {n: rep[n] for n in ("g_mix", "sgu_ln_w", "sgu_ln_b", "w0", "a0", "k_k", "k_a", "r_k", "ln_x_w", "ln_x_b",
                             "g_ffn")}
    W["g_final"] = rep["g_final"].reshape(1, D)
    W["sgu_w"] = rep["sgu_w"][0]
    W["sgu_bt"] = jnp.transpose(rep["sgu_b"][0])
    return W


def _late_grad_blocks(G):
    return Exchange([G[n].reshape(N_DEV, -1, D) for n in SCAN_CARRIED]
                    + [G["sgu_w"].reshape(SGU_G * SGU_C, SGU_C).astype(GRAD_PAYLOAD)],
                    [False] * len(SCAN_CARRIED) + [True])


def _first_grad_blocks(G):
    sbg = G["sb"]
    c = 3 * D
    sb = jnp.concatenate([sbg[:, :c], sbg[:, c:c + L_W], sbg[:, c + 128:c + 128 + L_A],
                          sbg[:, c + 256:c + 256 + L_G]], axis=1)
    return {
        "shift_b": _cols_to_blocks(sb),
        "w_lora_w": _cols_to_blocks(G["w_lora"][:L_W]), "a_lora_w": _cols_to_blocks(G["a_lora"][:L_A]),
        "g_lora_w": _cols_to_blocks(G["g_lora"][:L_G]),
    }


def _replicated_grads(G):
    small = {n: G[n] for n in ("g_mix", "sgu_ln_w", "sgu_ln_b", "w0", "a0", "k_k", "k_a", "r_k", "ln_x_w", "ln_x_b",
                               "g_ffn", "g_final")}
    small["sgu_w"] = G["sgu_w"]
    small["sgu_b"] = jnp.transpose(G["sgu_bt"])
    return small


def kernel(x, g_mix, w_in, sgu_ln_w, sgu_ln_b, sgu_w, sgu_b, w_proj_a, shift_b, w_lora_w, w0, a_lora_w, a0, g_lora_w, k_k, k_a, r_k, ln_x_w, ln_x_b, w_proj_b, w_out, g_ffn, w_ffn1, w_ffn2, g_final, loss_target, m_g_mix, m_w_in, m_sgu_ln_w, m_sgu_ln_b, m_sgu_w, m_sgu_b, m_w_proj_a, m_shift_b, m_w_lora_w, m_w0, m_a_lora_w, m_a0, m_g_lora_w, m_k_k, m_k_a, m_r_k, m_ln_x_w, m_ln_x_b, m_w_proj_b, m_w_out, m_g_ffn, m_w_ffn1, m_w_ffn2, m_g_final, v_g_mix, v_w_in, v_sgu_ln_w, v_sgu_ln_b, v_sgu_w, v_sgu_b, v_w_proj_a, v_shift_b, v_w_lora_w, v_w0, v_a_lora_w, v_a0, v_g_lora_w, v_k_k, v_k_a, v_r_k, v_ln_x_w, v_ln_x_b, v_w_proj_b, v_w_out, v_g_ffn, v_w_ffn1, v_w_ffn2, v_g_final):
    env = dict(locals())
    weights = {n: env[n] for n in WEIGHT_ORDER}
    moms = {n: env["m_" + n] for n in WEIGHT_ORDER}
    vars_ = {n: env["v_" + n] for n in WEIGHT_ORDER}

    shards = {n: weights[n] for n, _, _ in SHARDED}
    W = _gather_weights(shards)
    W.update(_replicated_weights({n: weights[n] for n, _ in REPLICATED}))
    in_flight = {}

    def send_w_in_grads(G):
        blocks = _w_in_grad_blocks(G["w_sgu_t"], G["w_rw_t"], G["w_gate_t"])
        got = _pair_exchange("grad_pair_exchange", blocks)
        core = lax.axis_index("c").astype(jnp.int32).reshape(1)
        sums = _pair_sum("grad_pair_sum", blocks, got, core)
        in_flight["sems"], in_flight["sums"], in_flight["land"], token = _chip_exchange_start("grad_chip_start", sums)
        return token

    def send_ffn_grads(G):
        in_flight["ffn"] = _scatter_start("grad_ffn_start", [G["w_ffn1"], G["w_ffn2"].reshape(N_DEV, -1, D)])
        return in_flight["ffn"][3]

    loss_part, dx, G, late_slots = _local_step(x[0], loss_target[0], W, late_weights=_late_weights(shards),
                                               early_grads=_late_grad_blocks, w_in_grads_ready=send_w_in_grads,
                                               ffn_grads_ready=send_ffn_grads)

    slots = dict(zip(SCAN_CARRIED, late_slots))
    me = _mesh_index()
    sent, landed = _scatter_wait("grad_ffn_wait", *in_flight["ffn"][:3], after=dx)
    for i, n in enumerate(FFN_WEIGHTS):
        slots[n] = _fill_slot("grad_own_slot_" + n, landed[i], sent[i], me, src_idx=me)
    blocks = _first_grad_blocks(G)
    small = _replicated_grads(G)
    small_parts = [small[n] for n, _ in PACKED] + [jnp.full((PACK_W,), loss_part, F32)]
    rest = [n for n in FIRST_WEIGHTS if n != "w_in"]
    res = _exchange("grad_exchange", [blocks[n] for n in rest] + [_pack_rows(small_parts, _SMALL_ROWS, F32)],
                    [False] * len(rest) + [True])
    slots.update(zip(rest, res[:-1]))
    small_slots = res[-1]
    sums, chip_slots = _chip_exchange_wait("grad_chip_wait", in_flight["sems"], in_flight["sums"], in_flight["land"],
                                           after=small_slots)
    my_chip = (2 * lax.axis_index("x") + lax.axis_index("y")).astype(jnp.int32).reshape(1)
    slots["w_in"] = _fill_slot("grad_own_slot", chip_slots, sums, my_chip, src_idx=my_chip)

    outs = [dict(), dict(), dict(), dict()]
    for n, _, _ in SHARDED:
        if n == "w_in":
            res = _adamw("adamw_" + n, slots[n], *[jnp.transpose(t, (2, 0, 1)) for t in (weights[n], moms[n], vars_[n])])
            res = [jnp.transpose(t, (1, 2, 0)) for t in res]
        else:
            res = _adamw("adamw_" + n, slots[n], weights[n], moms[n], vars_[n])
        for k in range(4):
            outs[k][n] = res[k]

    sgu_shape = (SGU_G * SGU_C, SGU_C)
    res = _adamw("adamw_sgu_w", late_slots[len(SCAN_CARRIED)], *[t.reshape(sgu_shape) for t in
                                                                  (weights["sgu_w"], moms["sgu_w"], vars_["sgu_w"])])
    for k in range(4):
        outs[k]["sgu_w"] = res[k].reshape(weights["sgu_w"].shape)

    def packed(d):
        return _pack_rows([d[n] for n, _ in PACKED], _SMALL_ROWS, F32)
    small_out = _adamw("adamw_replicated", small_slots, packed(weights), packed(moms), packed(vars_))
    for k in range(4):
        flat = small_out[k].reshape(-1)
        off = 0
        for (n, s), size in zip(PACKED, _SMALL_SIZES):
            outs[k][n] = flat[off:off + size].reshape(s)
            off += size
    loss = small_out[0].reshape(-1)[_LOSS_AT]
    return (loss, dx[None], *[outs[0][n] for n in WEIGHT_ORDER], *[outs[1][n] for n in WEIGHT_ORDER],
            *[outs[2][n] for n in WEIGHT_ORDER], *[outs[3][n] for n in WEIGHT_ORDER])
```

```python
import functools
import numpy as np
import jax
import jax.numpy as jnp
from jax import lax
from jax.experimental import pallas as pl
from jax.experimental.pallas import tpu as pltpu

F32 = jnp.float32
BF16 = jnp.bfloat16

D = 1024
NH, HN = 16, 64
NP, PW = NH // 2, 2 * HN
SGU_G, SGU_C = 8, 128
L_W, L_A, L_G = 64, 64, 160
C_B = 3 * D + L_W + L_A + L_G
P_TOTAL = 2 * D + C_B + 2 * D
D_FF = 4 * D
RW_INT = 3 * D + 128 + 128 + 256
NORM_EPS, LN_EPS, GN_EPS = 1e-6, 1e-5, 64e-5
N_DEV = 8
LANES = 128
SCAN_C = 64
N_KEPT = 4
SOLVE_B = 16
SCAN_PRECISION = lax.Precision.HIGH
SCAN_OUT_PRECISION = lax.Precision.DEFAULT
GRAD_PAYLOAD = BF16
VMEM_LIMIT = 56 * 1024 * 1024
MATMUL_VMEM_BUDGET = 40 * 1024 * 1024
STEP_COST_BYTES = 512 * 1024
HBM_COST_RATIO = 3
ACC_PASS_WEIGHT = 4

ADAM_LR, ADAM_B1, ADAM_B2, ADAM_EPS, ADAM_WD, ADAM_STEP = 0.001, 0.9, 0.999, 1e-08, 0.01, 10

SHARDED = [
    ("w_in", (D, P_TOTAL), 1), ("w_proj_a", (D, D), 0), ("shift_b", (2, C_B), 1), ("w_lora_w", (L_W, D), 1),
    ("a_lora_w", (L_A, D), 1), ("g_lora_w", (L_G, D), 1), ("w_proj_b", (D, D), 0), ("w_out", (D, D), 0),
    ("w_ffn1", (D, D_FF), 1), ("w_ffn2", (D_FF, D), 0),
]
REPLICATED = [
    ("g_mix", (1, D)), ("sgu_ln_w", (1, D)), ("sgu_ln_b", (1, D)), ("sgu_w", (1, SGU_G, SGU_C, SGU_C)),
    ("sgu_b", (1, SGU_G, SGU_C)), ("w0", (1, D)), ("a0", (1, D)), ("k_k", (1, D)), ("k_a", (1, D)), ("r_k", (1, D)),
    ("ln_x_w", (1, D)), ("ln_x_b", (1, D)), ("g_ffn", (1, D)), ("g_final", (D,)),
]
WEIGHT_ORDER = ["g_mix", "w_in", "sgu_ln_w", "sgu_ln_b", "sgu_w", "sgu_b", "w_proj_a", "shift_b", "w_lora_w", "w0",
                "a_lora_w", "a0", "g_lora_w", "k_k", "k_a", "r_k", "ln_x_w", "ln_x_b", "w_proj_b", "w_out", "g_ffn",
                "w_ffn1", "w_ffn2", "g_final"]


def _round_up(n, m):
    return (n + m - 1) // m * m


def _pick(n, target):
    if n <= target:
        return n
    best = None
    for t in range(LANES, target + 1, LANES):
        if n % t == 0:
            best = t
    assert best is not None, (n, target)
    return best


def _matmul(name, a, b, mode, out_dtype=F32, tm=2048, tn=1024, tk=4096, out_blocks=None, epilogue=None, extras=(),
            out_dtypes=(), after=None, whole_rows=False, out_widths=None):
    b_blocks = b.shape[0] if b.ndim == 3 else None
    bshape = b.shape if b.ndim == 2 else (b.shape[1], b.shape[0] * b.shape[2])
    if mode == "nn":
        (M, K), (K2, N) = a.shape, bshape
    elif mode == "nt":
        (M, K), (N, K2) = a.shape, bshape
    else:
        (K, M), (K2, N) = a.shape, bshape
    assert K == K2, (name, a.shape, b.shape)
    assert b_blocks is None or mode != "tn"
    assert out_blocks is None or mode == "tn"
    tn = min(tn, N // (out_blocks or 1), bshape[1] // b_blocks if (b_blocks and mode == "nn") else tn)
    blocked_k = bool(b_blocks) and mode == "nt"
    tm, tn, tk = _pick(M, tm), _pick(N, tn), (K if blocked_k else _pick(K, tk))

    def vmem_bytes(tm, tk):
        tiles = tm * tk * a.dtype.itemsize + tk * tn * b.dtype.itemsize
        for i, dt in enumerate(out_dtypes if epilogue else (out_dtype,)):
            tiles += tm * (out_widths[i] if out_widths else tn) * jnp.dtype(dt).itemsize
        for x in extras:
            arr = x[0] if isinstance(x, tuple) else x
            tiles += (tm if arr.shape[0] > 1 else 1) * tn * arr.dtype.itemsize
        return 2 * tiles + (tm * tn * 4 if K // tk > 1 else 0)

    def cost(tm, tk):
        ni, nj, nk = M // tm, N // tn, K // tk
        steps = ni * nj * nk
        acc_passes = steps * tm * tn * 8 * ACC_PASS_WEIGHT if nk > 1 else 0
        a_reads = M * K * a.dtype.itemsize * (nj if nk > 1 else 1)
        b_reads = K * N * b.dtype.itemsize * (ni if (nj > 1 or nk > 1) else 1)
        return (steps * STEP_COST_BYTES + acc_passes + vmem_bytes(tm, tk) // 2
                + HBM_COST_RATIO * (a_reads + b_reads))

    options = [(m, k) for m in ({M} if whole_rows else {_pick(M, max(t, LANES)) for t in (tm, tm // 2, tm // 4)})
               for k in ({K} if blocked_k else {_pick(K, max(t, LANES)) for t in (tk, tk // 2, tk // 4)})
               if vmem_bytes(m, k) <= MATMUL_VMEM_BUDGET]
    tm, tk = min(options, key=lambda o: cost(*o))
    nk = K // tk
    dims = {"nn": (((1,), (0,)), ((), ())), "nt": (((1,), (1,)), ((), ())), "tn": (((0,), (0,)), ((), ()))}[mode]

    n_x, n_o = len(extras), len(out_dtypes) if epilogue else 1
    n_after = 0 if after is None else 1

    def body(a_ref, b_ref, *rest):
        x_refs, o_refs, acc = rest[:n_x], rest[n_x + n_after:n_x + n_after + n_o], rest[n_x + n_after + n_o:]
        if blocked_k:
            bw = b.shape[2]
            part = sum(lax.dot_general(a_ref[:, blk * bw:(blk + 1) * bw].astype(BF16), b_ref[blk].astype(BF16), dims,
                                       preferred_element_type=F32) for blk in range(b_blocks))
        else:
            part = lax.dot_general(a_ref[...].astype(BF16), b_ref[...].astype(BF16), dims, preferred_element_type=F32)

        def finish(res):
            outs = epilogue(res, *[r[...] for r in x_refs]) if epilogue else (res,)
            for r, v in zip(o_refs, outs):
                r[...] = v.astype(r.dtype)

        if nk == 1:
            finish(part)
            return
        acc_ref, k = acc[0], pl.program_id(2)

        @pl.when(k == 0)
        def _():
            acc_ref[...] = part

        @pl.when(k > 0)
        def _():
            acc_ref[...] += part

        @pl.when(k == nk - 1)
        def _():
            finish(acc_ref[...])

    a_spec = {"nn": pl.BlockSpec((tm, tk), lambda i, j, k: (i, k)), "nt": pl.BlockSpec((tm, tk), lambda i, j, k: (i, k)),
              "tn": pl.BlockSpec((tk, tm), lambda i, j, k: (k, i))}[mode]
    b_spec = {"nn": pl.BlockSpec((tk, tn), lambda i, j, k: (k, j)), "nt": pl.BlockSpec((tn, tk), lambda i, j, k: (j, k)),
              "tn": pl.BlockSpec((tk, tn), lambda i, j, k: (k, j))}[mode]
    if b_blocks and mode == "nn":
        per = b.shape[2] // tn
        b_spec = pl.BlockSpec((None, tk, tn), lambda i, j, k: (j // per, k, j % per))
    elif b_blocks:
        b_spec = pl.BlockSpec((b_blocks, tn, b.shape[2]), lambda i, j, k: (0, j, 0))
    out_spec = pl.BlockSpec((tm, tn), lambda i, j, k: (i, j))
    out_shape = jax.ShapeDtypeStruct((M, N), out_dtype)
    if out_blocks:
        per_o = N // out_blocks // tn
        out_spec = pl.BlockSpec((None, tm, tn), lambda i, j, k: (j // per_o, i, j % per_o))
        out_shape = jax.ShapeDtypeStruct((out_blocks, M, N // out_blocks), out_dtype)
    epi_widths = list(out_widths) if out_widths else [N] * n_o
    assert all(w == N for w in epi_widths) or N == tn
    epi_specs = [pl.BlockSpec((tm, tn if w == N else w), lambda i, j, k: (i, j)) for w in epi_widths]
    x_specs, x_args = [], []
    for x in extras:
        arr, off = x if isinstance(x, tuple) else (x, 0)
        if arr.shape[0] == 1:
            x_specs.append(pl.BlockSpec((1, tn), lambda i, j, k: (0, j)))
        else:
            x_specs.append(pl.BlockSpec((tm, tn), lambda i, j, k, off=off: (i, j + off)))
        x_args.append(arr)
    res = pl.pallas_call(
        body, name=name, grid=(M // tm, N // tn, nk),
        in_specs=[a_spec, b_spec] + x_specs + [pl.BlockSpec(memory_space=pl.ANY)] * n_after,
        out_specs=epi_specs if epilogue else out_spec,
        out_shape=[jax.ShapeDtypeStruct((M, w), dt) for w, dt in zip(epi_widths, out_dtypes)] if epilogue else out_shape,
        scratch_shapes=[pltpu.VMEM((tm, tn), F32)] if nk > 1 else [],
        compiler_params=pltpu.CompilerParams(dimension_semantics=("parallel", "parallel", "arbitrary"),
                                             vmem_limit_bytes=VMEM_LIMIT),
    )(a, b, *x_args, *([after] if n_after else []))
    return res


class Rows:
    def __init__(self, arr, width=None, cb=0):
        self.arr, self.width, self.cb = arr, (arr.shape[1] if width is None else width), cb


class Heads:
    def __init__(self, arr):
        self.arr = arr


class Halo:
    def __init__(self, arr, side):
        self.arr, self.side = arr, side


def _rows_call(name, fn, ins, consts, outs, accs=(), tm=512, with_pid=False):
    T = next(o.arr.shape[1] if isinstance(o, Heads) else o.arr.shape[0] for o in ins if not isinstance(o, Halo))
    tm = min(tm, T)
    n_tiles = T // tm
    n_in, n_c, n_out = len(ins), len(consts), len(outs)
    in_specs, args = [], []
    for o in ins:
        if isinstance(o, Rows):
            in_specs.append(pl.BlockSpec((tm, o.width), lambda i, cb=o.cb: (i, cb)))
        elif isinstance(o, Heads):
            in_specs.append(pl.BlockSpec((NP, tm, PW), lambda i: (0, i, 0)))
        else:
            w = o.arr.shape[1]
            if o.side < 0:
                in_specs.append(pl.BlockSpec((8, w), lambda i: (jnp.maximum(i * (tm // 8) - 1, 0), 0)))
            else:
                in_specs.append(pl.BlockSpec((8, w), lambda i: (jnp.minimum((i + 1) * (tm // 8), T // 8 - 1), 0)))
        args.append(o.arr)
    for c in consts:
        in_specs.append(pl.BlockSpec(c.shape, lambda i, nd=c.ndim: (0,) * nd))
        args.append(c)
    out_specs, out_shape = [], []
    for o in outs:
        if o[0] == "rows":
            out_specs.append(pl.BlockSpec((tm, o[1]), lambda i: (i, 0)))
            out_shape.append(jax.ShapeDtypeStruct((T, o[1]), o[2]))
        else:
            out_specs.append(pl.BlockSpec((NP, tm, PW), lambda i: (0, i, 0)))
            out_shape.append(jax.ShapeDtypeStruct((NP, T, PW), o[1]))
    for shape, dt in accs:
        out_specs.append(pl.BlockSpec(shape, lambda i, nd=len(shape): (0,) * nd))
        out_shape.append(jax.ShapeDtypeStruct(shape, dt))

    def body(*refs):
        i = pl.program_id(0)
        vals = []
        vals = [r[...] for r in refs[:n_in + n_c]]
        res = fn(i, n_tiles, *vals) if with_pid else fn(*vals)
        out_refs = refs[n_in + n_c:]
        for r, v in zip(out_refs[:n_out], res[:n_out]):
            r[...] = v.astype(r.dtype)
        if accs:
            @pl.when(i == 0)
            def _():
                for r in out_refs[n_out:]:
                    r[...] = jnp.zeros_like(r)

            for r, v in zip(out_refs[n_out:], res[n_out:]):
                r[...] += v.astype(r.dtype)

    res = pl.pallas_call(
        body, name=name, grid=(n_tiles,), in_specs=in_specs, out_specs=out_specs, out_shape=out_shape,
        compiler_params=pltpu.CompilerParams(dimension_semantics=("arbitrary",), vmem_limit_bytes=VMEM_LIMIT),
    )(*args)
    return res


def _rms(x, g):
    return x * lax.rsqrt(jnp.mean(x * x, axis=-1, keepdims=True) + NORM_EPS) * g


def _gelu(x):
    return 0.5 * x * (1.0 + lax.erf(x * 0.7071067811865476))


def _sigmoid(x):
    return 1.0 / (1.0 + jnp.exp(-x))


def _bdot(a, b):
    return jnp.dot(a.astype(BF16), b.astype(BF16), preferred_element_type=F32)


def _to_heads(x):
    return jnp.concatenate([x[:, p * PW:(p + 1) * PW][None] for p in range(NP)], axis=0)


def _from_heads(xp):
    return jnp.concatenate([xp[p] for p in range(NP)], axis=-1)


def _head_sum(xp):
    low = lax.broadcasted_iota(jnp.int32, xp.shape, xp.ndim - 1) < HN
    both = jnp.sum(xp, axis=-1, keepdims=True)
    first = jnp.sum(jnp.where(low, xp, 0.0), axis=-1, keepdims=True)
    return jnp.where(low, first, both - first)


def _split_pairs(xp):
    return jnp.concatenate([xp[:, :, :HN], xp[:, :, HN:]], axis=0)


def _join_pairs(xh):
    return jnp.concatenate([xh[:NP], xh[NP:]], axis=-1)


def _sgu_fn(p, ln_w, ln_b, sw, sbt):
    z = _gelu(p)
    u, v = z[:, :D], z[:, D:]
    mu = jnp.mean(v, axis=-1, keepdims=True)
    var = jnp.mean(jnp.square(v - mu), axis=-1, keepdims=True)
    vn = (v - mu) * lax.rsqrt(var + LN_EPS) * ln_w + ln_b
    ri = lax.broadcasted_iota(jnp.int32, (SGU_C, SGU_C), 0)
    ci = lax.broadcasted_iota(jnp.int32, (SGU_C, SGU_C), 1)
    mask = (ci <= ri).astype(F32)
    dg = D // SGU_G
    parts = []
    for g in range(SGU_G):
        parts.append(_bdot(sw[g] * mask, vn[:, g * dg:(g + 1) * dg]) + sbt[:, g:g + 1])
    return u * jnp.concatenate(parts, axis=-1)


def _pre_fn(qr, qk, qv, qxw, qxa, qxg, wl, w0, al, a0, gl, k_k, k_a):
    w = -jax.nn.softplus(-(w0 + _bdot(jnp.tanh(qxw), wl))) - 0.5
    lw = -jnp.exp(w)
    aa = _sigmoid(a0 + _bdot(qxa, al))
    g = _bdot(_sigmoid(qxg), gl)
    kk = _to_heads(qk * k_k)
    kk = kk / jnp.maximum(jnp.sqrt(_head_sum(kk * kk)), 1e-12)
    k2 = qk * (1.0 + (aa - 1.0) * k_a)
    return _to_heads(qr), _to_heads(lw), _to_heads(k2), _to_heads(qv), kk, _to_heads(aa), g


def _post_fn(o, r, k2, v, g, ln_w, ln_b, r_k):
    mu = _head_sum(o) * (1.0 / HN)
    d = o - mu
    var = _head_sum(d * d) * (1.0 / HN)
    on = d * lax.rsqrt(var + GN_EPS) * ln_w + ln_b
    bonus = _head_sum(r * k2 * r_k) * v
    return _from_heads(on + bonus) * g


def _gate_fn(pg, ya, yb):
    return _sigmoid(pg[:, :D]) * ya + _sigmoid(pg[:, D:]) * yb


def _bmm(x, y, cx, cy, out_path=False):
    return lax.dot_general(x, y, (((cx,), (cy,)), ((0,), (0,))),
                           precision=SCAN_OUT_PRECISION if out_path else SCAN_PRECISION, preferred_element_type=F32)


def _unit_lower_inverse(M):
    C = M.shape[1]
    ti = lax.broadcasted_iota(jnp.int32, (C, C), 0)
    tj = lax.broadcasted_iota(jnp.int32, (C, C), 1)
    eye = (ti == tj).astype(F32)
    same = lambda b: (ti // b == tj // b).astype(F32)
    X = -(M * same(SOLVE_B))
    inv = eye + X
    span = 1
    while 2 * span < SOLVE_B:
        X = _bmm(X, X, 2, 1)
        inv = inv + _bmm(inv, X, 2, 1)
        span *= 2
    b = SOLVE_B
    while b < C:
        low = M * (same(2 * b) - same(b))
        inv = inv - _bmm(_bmm(inv, low, 2, 1, out_path=True), inv, 2, 1, out_path=True)
        b *= 2
    return inv


@jax.custom_vjp
def _unit_lower_solve(inv, M, y):
    return _bmm(inv, y, 2, 1)


def _unit_lower_solve_fwd(inv, M, y):
    u = _bmm(inv, y, 2, 1)
    return u, (inv, u)


def _unit_lower_solve_bwd(res, du):
    inv, u = res
    dy = _bmm(inv, du, 1, 1)
    return jnp.zeros_like(inv), -_bmm(dy, u, 2, 2), dy


_unit_lower_solve.defvjp(_unit_lower_solve_fwd, _unit_lower_solve_bwd)


@jax.custom_vjp
def _unit_lower_solved(inv, u, M, y):
    return u


_unit_lower_solved.defvjp(lambda inv, u, M, y: (u, (inv, u)),
                          lambda res, du: (jnp.zeros_like(res[0]), jnp.zeros_like(res[1]))
                          + _unit_lower_solve_bwd(res, du)[1:])


@functools.partial(jax.custom_vjp, nondiff_argnums=(0,))
def _kept(fn, value, *args):
    return value


def _kept_fwd(fn, value, *args):
    return value, args


def _kept_bwd(fn, args, d):
    _, vjp = jax.vjp(fn, *args)
    return (jnp.zeros_like(d),) + tuple(vjp(d))


_kept.defvjp(_kept_fwd, _kept_bwd)


def _sum_over_time(x, reverse):
    C = x.shape[1]
    ti = lax.broadcasted_iota(jnp.int32, (C, C), 0)
    tj = lax.broadcasted_iota(jnp.int32, (C, C), 1)
    ones = jnp.broadcast_to(((tj >= ti) if reverse else (tj <= ti)).astype(BF16), (x.shape[0], C, C))
    hi = x.astype(BF16)
    r1 = x - hi.astype(F32)
    mid = r1.astype(BF16)
    lo = (r1 - mid.astype(F32)).astype(BF16)
    dn = (((2,), (1,)), ((0,), (0,)))
    return sum(lax.dot_general(ones, p, dn, preferred_element_type=F32) for p in (lo, mid, hi))


@jax.custom_vjp
def _time_cumsum(lw):
    return _sum_over_time(lw, reverse=False)


_time_cumsum.defvjp(lambda lw: (_sum_over_time(lw, reverse=False), None),
                    lambda _, d: (_sum_over_time(d, reverse=True),))


def _chunk_fn(S0, r, lw, k, v, kk, a, kept=None):
    C = SCAN_C
    bmm = _bmm
    ti = lax.broadcasted_iota(jnp.int32, (C, C), 0)
    tj = lax.broadcasted_iota(jnp.int32, (C, C), 1)
    incl2 = jnp.concatenate([(tj <= ti).astype(F32)] * 2, axis=1)
    strict = (tj < ti).astype(F32)
    n_mask = jnp.concatenate([jnp.zeros((C, C), F32), strict], axis=1)

    def known(name, fn, *args):
        return fn(*args) if kept is None else _kept(fn, kept[name], *args)

    cum = known("cum", _time_cumsum, lw)
    g_in, g_ex, g_inv = jnp.exp(cum), jnp.exp(cum - lw), jnp.exp(-cum)
    kkt, rt = kk * g_ex, r * g_in
    bk = jnp.concatenate([kk * a * g_inv, k * g_inv], axis=1)
    kr = jnp.concatenate([kkt, rt], axis=1)
    ratios = known("ratios", lambda x, y: bmm(x, y, 2, 2), kr, bk)
    A = ratios[:, :C]
    M = A[:, :, :C] * strict
    zv = jnp.concatenate([jnp.zeros_like(v), v], axis=1)
    s0_side = bmm(kr, S0, 2, 2, out_path=True)
    rhs = s0_side[:, :C] + bmm(A * n_mask, zv, 2, 1, out_path=True)
    if kept is None:
        inv = lax.stop_gradient(_unit_lower_inverse(M))
        y = _unit_lower_solve(inv, M, rhs)
    else:
        inv = kept["inv"]
        y = _unit_lower_solved(inv, kept["y"], M, rhs)
    z = jnp.concatenate([-y, v], axis=1)
    O = s0_side[:, C:] + bmm(ratios[:, C:] * incl2, z, 2, 1, out_path=True)
    g_end = g_in[:, C - 1:C, :]
    S1 = S0 * g_end + bmm(z, bk * g_end, 1, 1, out_path=True)
    return O, S1, dict(cum=cum, ratios=ratios, y=y, inv=inv)


def _scan_fwd(r, lw, k, v, kk, a, ex=None, tb=256):
    assert SCAN_C == HN and 2 * SCAN_C == PW
    T = r.shape[1]
    tb = min(tb, T)
    n_chunks = tb // SCAN_C
    nb = T // tb
    nx = ex.nb if ex else 0

    def body(*refs):
        r_ref, lw_ref, k_ref, v_ref, kk_ref, a_ref = refs[:6]
        x_in, (o_ref, s0_ref), x_out = refs[6:6 + nx], refs[6 + nx:8 + nx], refs[8 + nx:8 + 2 * nx]
        s_ref, sems = refs[8 + 2 * nx], refs[9 + 2 * nx:]

        plan = ex.schedule(nb) if ex else []

        @pl.when(pl.program_id(0) == 0)
        def _():
            s_ref[...] = jnp.zeros_like(s_ref)
            for at, action in plan[:1]:
                action(x_in, x_out, sems)

        def step(c, carry):
            sl = pl.ds(pl.multiple_of(c * SCAN_C, SCAN_C), SCAN_C)
            S0 = s_ref[...]
            O, S1, keep = _chunk_fn(S0, *[_split_pairs(ref[:, sl, :])
                                          for ref in (r_ref, lw_ref, k_ref, v_ref, kk_ref, a_ref)])
            o_ref[:, sl, :] = _join_pairs(O)
            s0_ref[c, 0] = jnp.concatenate([S0, keep["inv"]], axis=-1)
            s0_ref[c, 1] = jnp.concatenate([keep["cum"], keep["y"]], axis=-1)
            s0_ref[c, 2] = keep["ratios"][:, :SCAN_C]
            s0_ref[c, 3] = keep["ratios"][:, SCAN_C:]
            s_ref[...] = S1
            return carry

        lax.fori_loop(0, n_chunks, step, 0)

        for at, action in plan[1:]:
            pl.when(pl.program_id(0) == at)(functools.partial(action, x_in, x_out, sems))

    hm = pl.BlockSpec((NP, tb, PW), lambda i: (0, i, 0))
    res = pl.pallas_call(
        body, name="rwkv_scan_fwd", grid=(nb,), in_specs=[hm] * 6 + (ex.any_specs if ex else []),
        out_specs=[hm, pl.BlockSpec((n_chunks, N_KEPT, NH, HN, PW), lambda i: (i, 0, 0, 0, 0))]
        + (ex.any_specs if ex else []),
        out_shape=[jax.ShapeDtypeStruct((NP, T, PW), F32),
                   jax.ShapeDtypeStruct((T // SCAN_C, N_KEPT, NH, HN, PW), F32)]
        + (ex.out_shape if ex else []),
        scratch_shapes=[pltpu.VMEM((NH, HN, HN), F32)] + (ex.sem_shapes if ex else []),
        compiler_params=pltpu.CompilerParams(dimension_semantics=("arbitrary",), vmem_limit_bytes=VMEM_LIMIT),
    )(r, lw, k, v, kk, a, *(ex.bufs if ex else []))
    return res[0], res[1], list(res[2:])


def _scan_bwd(r, lw, k, v, kk, a, s0s, do, ex=None, tb=128):
    T = r.shape[1]
    tb = min(tb, T)
    n_chunks = tb // SCAN_C
    nb = T // tb
    nx = ex.nb if ex else 0

    def body(*refs):
        r_ref, lw_ref, k_ref, v_ref, kk_ref, a_ref, s0_ref, do_ref = refs[:8]
        x_in, (dr, dlw, dk, dv, dkk, da), x_out = refs[8:8 + nx], refs[8 + nx:14 + nx], refs[14 + nx:14 + 2 * nx]
        ds_ref, sems = refs[14 + 2 * nx], refs[15 + 2 * nx:]

        plan = ex.schedule(nb) if ex else []

        @pl.when(pl.program_id(0) == 0)
        def _():
            ds_ref[...] = jnp.zeros_like(ds_ref)
            for at, action in plan[:1]:
                action(x_in, x_out, sems)

        def step(j, carry):
            c = n_chunks - 1 - j
            sl = pl.ds(pl.multiple_of(c * SCAN_C, SCAN_C), SCAN_C)
            s0_inv, cum_y = s0_ref[c, 0], s0_ref[c, 1]
            kept = dict(inv=s0_inv[:, :, HN:], cum=cum_y[:, :, :HN], y=cum_y[:, :, HN:],
                        ratios=jnp.concatenate([s0_ref[c, 2], s0_ref[c, 3]], axis=1))
            _, vjp = jax.vjp(lambda *t: _chunk_fn(*t, kept=kept)[:2], s0_inv[:, :, :HN],
                             *[_split_pairs(ref[:, sl, :]) for ref in (r_ref, lw_ref, k_ref, v_ref, kk_ref, a_ref)])
            g = vjp((_split_pairs(do_ref[:, sl, :]), ds_ref[...]))
            ds_ref[...] = g[0]
            for ref, val in zip((dr, dlw, dk, dv, dkk, da), g[1:]):
                ref[:, sl, :] = _join_pairs(val)
            return carry

        lax.fori_loop(0, n_chunks, step, 0)

        for at, action in plan[1:]:
            pl.when(pl.program_id(0) == at)(functools.partial(action, x_in, x_out, sems))

    hm = pl.BlockSpec((NP, tb, PW), lambda i: (0, nb - 1 - i, 0))
    res = pl.pallas_call(
        body, name="rwkv_scan_bwd", grid=(nb,),
        in_specs=[hm] * 6 + [pl.BlockSpec((n_chunks, N_KEPT, NH, HN, PW), lambda i: (nb - 1 - i, 0, 0, 0, 0)), hm]
        + (ex.any_specs if ex else []),
        out_specs=[hm] * 6 + (ex.any_specs if ex else []),
        out_shape=[jax.ShapeDtypeStruct((NP, T, PW), F32)] * 6 + (ex.out_shape if ex else []),
        scratch_shapes=[pltpu.VMEM((NH, HN, HN), F32)] + (ex.sem_shapes if ex else []),
        compiler_params=pltpu.CompilerParams(dimension_semantics=("arbitrary",), vmem_limit_bytes=VMEM_LIMIT),
    )(r, lw, k, v, kk, a, s0s, do, *(ex.bufs if ex else []))
    return list(res[:6]), list(res[6:])


def _shift_down(i, p, prev8):
    first = jnp.where(i > 0, prev8[7:8, :], 0.0)
    row = lax.broadcasted_iota(jnp.int32, p.shape, 0)
    return jnp.where(row == 0, first, pltpu.roll(p, 1, axis=0))


def _mix_bwd(dq, p, sb, tm=256):
    def fn(i, n, dq, next8, p, prev8, sb):
        ps = _shift_down(i, p, prev8)
        d1 = dq * sb[1:2]
        last = jnp.where(i < n - 1, next8[0:1, :] * sb[1:2], 0.0)
        row = lax.broadcasted_iota(jnp.int32, dq.shape, 0)
        up = jnp.where(row == dq.shape[0] - 1, last, pltpu.roll(d1, dq.shape[0] - 1, axis=0))
        return (dq * sb[0:1] + up, jnp.sum(dq * p, axis=0, keepdims=True), jnp.sum(dq * ps, axis=0, keepdims=True))
    w = p.shape[1]
    return _rows_call("shift_mix_bwd", fn, [Rows(dq), Halo(dq, +1), Rows(p), Halo(p, -1)], [sb], [("rows", w, BF16)],
                      accs=[((1, w), F32), ((1, w), F32)], tm=tm, with_pid=True)


def _local_step(x, target, W, late_weights=None, early_grads=None, w_in_grads_ready=None, ffn_grads_ready=None):
    G = {}
    a = _rows_call("norm_mix_fwd", lambda x, g: (_rms(x, g),), [Rows(x)], [W["g_mix"]], [("rows", D, BF16)])[0]
    p_sgu = _matmul("proj_sgu", a, W["w_sgu_t"], "nt")
    def token_shift(p, sb0, sb1):
        row = lax.broadcasted_iota(jnp.int32, p.shape, 0)
        return p, p * sb0 + jnp.where(row == 0, 0.0, pltpu.roll(p, 1, axis=0)) * sb1
    p_rw, q = _matmul("proj_rwkv", a, W["w_rw_t"], "nt", tn=512, whole_rows=True, epilogue=token_shift,
                      extras=[W["sb"][0:1], W["sb"][1:2]], out_dtypes=(F32, F32))
    p_gate = _matmul("proj_gate", a, W["w_gate_t"], "nt")

    sgu_consts = [W["sgu_ln_w"], W["sgu_ln_b"], W["sgu_w"], W["sgu_bt"]]
    s = _rows_call("sgu_fwd", lambda *t: (_sgu_fn(*t),), [Rows(p_sgu)], sgu_consts, [("rows", D, BF16)], tm=SGU_C)[0]

    q_ins = [Rows(q, D, 0), Rows(q, D, 1), Rows(q, D, 2), Rows(q, 128, 24), Rows(q, 128, 25), Rows(q, 256, 13)]
    pre_consts = [W["w_lora"], W["w0"], W["a_lora"], W["a0"], W["g_lora"], W["k_k"], W["k_a"]]
    r_h, lw_h, k_h, v_h, kk_h, a_h, g_gate = _rows_call(
        "rwkv_pre_fwd", _pre_fn, q_ins, pre_consts, [("heads", F32)] * 6 + [("rows", D, F32)], tm=128)
    o_h, s0s, got = _scan_fwd(r_h, lw_h, k_h, v_h, kk_h, a_h, ex=late_weights[0] if late_weights else None)
    if late_weights:
        W = {**W, **late_weights[1](got)}
    y_a = _matmul("proj_a", s, W["w_proj_a"], "nn")
    post_ins = [Heads(o_h), Heads(r_h), Heads(k_h), Heads(v_h), Rows(g_gate)]
    post_consts = [W[n].reshape(NP, 1, PW) for n in ("ln_x_w", "ln_x_b", "r_k")]
    z_b = _rows_call("rwkv_post_fwd", lambda *t: (_post_fn(*t),), post_ins, post_consts, [("rows", D, BF16)], tm=128)[0]
    y_b, mixed = _matmul("proj_b", z_b, W["w_proj_b"], "nn", extras=[(p_gate, 0), (p_gate, 1), y_a],
                         epilogue=lambda yb, ga, gb, ya: (yb, _sigmoid(ga) * ya + _sigmoid(gb) * yb),
                         out_dtypes=(F32, BF16))

    def res1(mo, x, g):
        h1 = x + mo
        return h1, _rms(h1, g)
    h1, f = _matmul("proj_out", mixed, W["w_out"], "nn", extras=[x, W["g_ffn"]], epilogue=res1,
                    out_dtypes=(F32, BF16))

    def relu_sq(u):
        r = jnp.maximum(u, 0.0)
        return r, r * r
    r1, act = _matmul("ffn_up", f, W["w_ffn1"], "nn", epilogue=relu_sq, out_dtypes=(BF16, BF16))
    ff = _matmul("ffn_down", act, W["w_ffn2"], "nn")

    def head(h1, ff, tgt, g):
        def f_(h1, ff, g):
            y = _rms(h1 + ff, g)
            return 0.5 * jnp.sum(jnp.mean(jnp.square(y - tgt), axis=-1))
        loss, (dh2, _, dg) = jax.value_and_grad(f_, argnums=(0, 1, 2))(h1, ff, g)
        return dh2, jnp.full((8, LANES), loss, F32), dg
    dh2, loss_acc, G["g_final"] = _rows_call("loss_head", head, [Rows(h1), Rows(ff), Rows(target)], [W["g_final"]],
                                             [("rows", D, F32)], accs=[((8, LANES), F32), ((1, D), F32)])

    d_u1 = _matmul("ffn_down_dx", dh2, W["w_ffn2"], "nt", extras=[r1], out_dtypes=(BF16,),
                   epilogue=lambda d_act, r: (d_act * 2.0 * r.astype(F32),))[0]
    G["w_ffn2"] = _matmul("ffn_down_dw", act, dh2, "tn", out_dtype=GRAD_PAYLOAD)
    d_f = _matmul("ffn_up_dx", d_u1, W["w_ffn1"], "nt")
    G["w_ffn1"] = _matmul("ffn_up_dw", f, d_u1, "tn", out_blocks=N_DEV, out_dtype=GRAD_PAYLOAD)

    def res1_bwd(h1, d_f, dh2, g):
        _, vjp = jax.vjp(_rms, h1, g)
        dh, dg = vjp(d_f)
        return dh2 + dh, dg
    dh1, G["g_ffn"] = _rows_call("residual_norm_bwd", res1_bwd, [Rows(h1), Rows(d_f), Rows(dh2)],
                                 [W["g_ffn"]], [("rows", D, F32)], accs=[((1, D), F32)])
    ffn_token = ffn_grads_ready(G) if ffn_grads_ready else None
    def gate_bwd(d_mixed, ga, gb, ya, yb):
        _, vjp = jax.vjp(_gate_fn, jnp.concatenate([ga, gb], axis=-1), ya, yb)
        return vjp(d_mixed)
    d_gate, d_ya, d_yb = _matmul("proj_out_dx", dh1, W["w_out"], "nt", after=ffn_token, epilogue=gate_bwd,
                                 extras=[(p_gate, 0), (p_gate, 1), y_a, y_b], out_dtypes=(BF16, BF16, BF16),
                                 out_widths=(2 * D, D, D))
    G["w_out"] = _matmul("proj_out_dw", mixed, dh1, "tn", out_dtype=GRAD_PAYLOAD)

    d_s = _matmul("proj_a_dx", d_ya, W["w_proj_a"], "nt")
    G["w_proj_a"] = _matmul("proj_a_dw", s, d_ya, "tn", out_dtype=GRAD_PAYLOAD)

    def sgu_bwd(p, ds, *c):
        _, vjp = jax.vjp(_sgu_fn, p, *c)
        return vjp(ds)
    d_p_sgu, G["sgu_ln_w"], G["sgu_ln_b"], G["sgu_w"], G["sgu_bt"] = _rows_call(
        "sgu_bwd", sgu_bwd, [Rows(p_sgu), Rows(d_s)], sgu_consts, [("rows", 2 * D, BF16)],
        accs=[((1, D), F32), ((1, D), F32), ((SGU_G, SGU_C, SGU_C), F32), ((SGU_C, SGU_G), F32)], tm=SGU_C)

    d_zb = _matmul("proj_b_dx", d_yb, W["w_proj_b"], "nt")
    G["w_proj_b"] = _matmul("proj_b_dw", z_b, d_yb, "tn", out_dtype=GRAD_PAYLOAD)

    def post_bwd(o, r, k2, v, g, dz, *c):
        _, vjp = jax.vjp(_post_fn, o, r, k2, v, g, *c)
        return vjp(dz)
    do_h, dr1, dk1, dv1, d_g, g_lnw, g_lnb, g_rk = _rows_call(
        "rwkv_post_bwd", post_bwd, post_ins + [Rows(d_zb)], post_consts, [("heads", F32)] * 4 + [("rows", D, F32)],
        accs=[((NP, 1, PW), F32)] * 3, tm=128)
    G["ln_x_w"], G["ln_x_b"], G["r_k"] = (t.reshape(1, D) for t in (g_lnw, g_lnb, g_rk))
    (dr2, dlw, dk2, dv2, dkk, daa), early = _scan_bwd(r_h, lw_h, k_h, v_h, kk_h, a_h, s0s, do_h,
                                                      ex=early_grads(G) if early_grads else None)

    def pre_bwd(qr, qk, qv, qxw, qxa, qxg, dr1, dr2, dlw, dk1, dk2, dv1, dv2, dkk, daa, dg, *c):
        _, vjp = jax.vjp(_pre_fn, qr, qk, qv, qxw, qxa, qxg, *c)
        g = vjp((dr1 + dr2, dlw, dk1 + dk2, dv1 + dv2, dkk, daa, dg))
        dq = jnp.concatenate(g[:6], axis=-1)
        return (dq,) + tuple(g[6:])
    pre_b_ins = q_ins + [Heads(dr1), Heads(dr2), Heads(dlw), Heads(dk1), Heads(dk2), Heads(dv1), Heads(dv2),
                         Heads(dkk), Heads(daa), Rows(d_g)]
    d_q, G["w_lora"], G["w0"], G["a_lora"], G["a0"], G["g_lora"], G["k_k"], G["k_a"] = _rows_call(
        "rwkv_pre_bwd", pre_bwd, pre_b_ins, pre_consts, [("rows", RW_INT, F32)],
        accs=[((128, D), F32), ((1, D), F32), ((128, D), F32), ((1, D), F32), ((256, D), F32), ((1, D), F32),
              ((1, D), F32)], tm=128)
    d_p_rw, dsb0, dsb1 = _mix_bwd(d_q, p_rw, W["sb"])
    G["sb"] = jnp.concatenate([dsb0, dsb1], axis=0)

    G["w_sgu_t"] = _matmul("proj_sgu_dw", d_p_sgu, a, "tn", out_dtype=GRAD_PAYLOAD)
    G["w_rw_t"] = _matmul("proj_rwkv_dw", d_p_rw, a, "tn", out_dtype=GRAD_PAYLOAD)
    G["w_gate_t"] = _matmul("proj_gate_dw", d_gate, a, "tn", out_dtype=GRAD_PAYLOAD)
    token = w_in_grads_ready(G) if w_in_grads_ready else None
    da1 = _matmul("proj_sgu_dx", d_p_sgu, W["w_sgu_t"], "nn", after=token)
    da2 = _matmul("proj_rwkv_dx", d_p_rw, W["w_rw_t"], "nn", after=token)
    da3 = _matmul("proj_gate_dx", d_gate, W["w_gate_t"], "nn", after=token)

    def norm1_bwd(x, da1, da2, da3, dh1, g):
        _, vjp = jax.vjp(_rms, x, g)
        dx, dg = vjp(da1 + da2 + da3)
        return dh1 + dx, dg
    dx, G["g_mix"] = _rows_call("norm_mix_bwd", norm1_bwd, [Rows(x), Rows(da1), Rows(da2), Rows(da3), Rows(dh1)],
                                [W["g_mix"]], [("rows", D, F32)], accs=[((1, D), F32)])
    return loss_acc[0, 0], dx, G, early


class Exchange:
    def __init__(self, bufs, gathers):
        self.bufs, self.gathers, self.nb = list(bufs), list(gathers), len(bufs)
        self.any_specs = [pl.BlockSpec(memory_space=pl.ANY)] * self.nb
        self.out_shape = [jax.ShapeDtypeStruct((N_DEV,) + (b.shape if g else b.shape[1:]), b.dtype)
                          for b, g in zip(self.bufs, self.gathers)]
        n = (N_DEV - 1) * self.nb
        self.sem_shapes = [pltpu.SemaphoreType.DMA((n,)), pltpu.SemaphoreType.DMA((n,)),
                           pltpu.SemaphoreType.DMA((self.nb,))]

    def _copies(self, in_refs, out_refs, sems):
        send_sems, recv_sems, local_sems = sems
        x, y, c = lax.axis_index("x"), lax.axis_index("y"), lax.axis_index("c")
        me = 4 * x + 2 * y + c

        def src(b, dest):
            return in_refs[b] if self.gathers[b] else in_refs[b].at[dest]

        local = [pltpu.make_async_copy(src(b, me), out_refs[b].at[me], local_sems.at[b]) for b in range(self.nb)]
        sends, recvs = [], []
        for kbits in range(1, N_DEV):
            px = 1 - x if kbits & 4 else x
            py = 1 - y if kbits & 2 else y
            pc = 1 - c if kbits & 1 else c
            peer = 4 * px + 2 * py + pc
            for b in range(self.nb):
                s = (kbits - 1) * self.nb + b
                sends.append(pltpu.make_async_remote_copy(
                    src_ref=src(b, peer), dst_ref=out_refs[b].at[me], send_sem=send_sems.at[s],
                    recv_sem=recv_sems.at[s], device_id=(px, py, pc), device_id_type=pl.DeviceIdType.MESH))
                recvs.append(pltpu.make_async_remote_copy(
                    src_ref=src(b, peer), dst_ref=out_refs[b].at[peer], send_sem=send_sems.at[s],
                    recv_sem=recv_sems.at[s], device_id=(px, py, pc), device_id_type=pl.DeviceIdType.MESH))
        return local, sends, recvs

    def start(self, in_refs, out_refs, sems):
        local, sends, _ = self._copies(in_refs, out_refs, sems)
        for cp in sends + local:
            cp.start()

    def wait(self, in_refs, out_refs, sems):
        local, sends, recvs = self._copies(in_refs, out_refs, sems)
        for cp in recvs:
            cp.wait_recv()
        for cp in sends:
            cp.wait_send()
        for cp in local:
            cp.wait()

    def schedule(self, n_steps):
        return [(0, self.start), (n_steps - 1, self.wait)]


def _exchange(name, bufs, gather):
    ex = Exchange(bufs, gather if isinstance(gather, (list, tuple)) else [gather] * len(bufs))

    def body(*refs):
        in_refs, out_refs, sems = refs[:ex.nb], refs[ex.nb:2 * ex.nb], refs[2 * ex.nb:]
        ex.start(in_refs, out_refs, sems)
        ex.wait(in_refs, out_refs, sems)

    return pl.pallas_call(body, name=name, in_specs=ex.any_specs, out_specs=ex.any_specs, out_shape=ex.out_shape,
                          scratch_shapes=ex.sem_shapes)(*ex.bufs)


N_CHIP = 4


def _pair_exchange(name, blocks):
    def body(b_ref, got_ref, send_sems, recv_sems):
        x, y, c = lax.axis_index("x"), lax.axis_index("y"), lax.axis_index("c")
        copies = [pltpu.make_async_remote_copy(
            src_ref=b_ref.at[2 * q + 1 - c], dst_ref=got_ref.at[q], send_sem=send_sems.at[q],
            recv_sem=recv_sems.at[q], device_id=(x, y, 1 - c), device_id_type=pl.DeviceIdType.MESH)
            for q in range(N_CHIP)]
        for cp in copies:
            cp.start()
        for cp in copies:
            cp.wait_recv()
        for cp in copies:
            cp.wait_send()

    any_spec = pl.BlockSpec(memory_space=pl.ANY)
    return pl.pallas_call(
        body, name=name, in_specs=[any_spec], out_specs=any_spec,
        out_shape=jax.ShapeDtypeStruct((N_CHIP,) + blocks.shape[1:], blocks.dtype),
        scratch_shapes=[pltpu.SemaphoreType.DMA((N_CHIP,))] * 2,
    )(blocks)


def _pair_sum(name, blocks, got, core):
    _, R, Wd = blocks.shape

    def body(c_ref, a_ref, b_ref, o_ref):
        o_ref[...] = (a_ref[...].astype(F32) + b_ref[...].astype(F32)).astype(o_ref.dtype)

    return pl.pallas_call(
        body, name=name,
        grid_spec=pltpu.PrefetchScalarGridSpec(
            num_scalar_prefetch=1, grid=(N_CHIP,),
            in_specs=[pl.BlockSpec((None, R, Wd), lambda q, c_ref: (2 * q + c_ref[0], 0, 0)),
                      pl.BlockSpec((None, R, Wd), lambda q, c_ref: (q, 0, 0))],
            out_specs=pl.BlockSpec((None, R, Wd), lambda q, c_ref: (q, 0, 0))),
        out_shape=jax.ShapeDtypeStruct(got.shape, blocks.dtype),
        compiler_params=pltpu.CompilerParams(dimension_semantics=("parallel",), vmem_limit_bytes=VMEM_LIMIT),
    )(core, blocks, got)


def _chip_copies(s_ref, land_ref, send_sems, recv_sems):
    x, y, c = lax.axis_index("x"), lax.axis_index("y"), lax.axis_index("c")
    my_q = 2 * x + y
    sends, recvs = [], []
    for kbits in range(1, N_CHIP):
        px = 1 - x if kbits & 2 else x
        py = 1 - y if kbits & 1 else y
        peer_q = 2 * px + py
        sends.append(pltpu.make_async_remote_copy(
            src_ref=s_ref.at[peer_q], dst_ref=land_ref.at[my_q], send_sem=send_sems[kbits - 1],
            recv_sem=recv_sems[kbits - 1], device_id=(px, py, c), device_id_type=pl.DeviceIdType.MESH))
        recvs.append(pltpu.make_async_remote_copy(
            src_ref=s_ref.at[peer_q], dst_ref=land_ref.at[peer_q], send_sem=send_sems[kbits - 1],
            recv_sem=recv_sems[kbits - 1], device_id=(px, py, c), device_id_type=pl.DeviceIdType.MESH))
    return sends, recvs


_HBM = pl.BlockSpec(memory_space=pltpu.HBM)
_SEM = pl.BlockSpec(memory_space=pltpu.SEMAPHORE)
N_CHIP_SEMS = 2 * (N_CHIP - 1)


def _scatter_copies(b_refs, land_refs, send_sems, recv_sems):
    x, y, c = lax.axis_index("x"), lax.axis_index("y"), lax.axis_index("c")
    me = 4 * x + 2 * y + c
    sends, recvs = [], []
    for kbits in range(1, N_DEV):
        px = 1 - x if kbits & 4 else x
        py = 1 - y if kbits & 2 else y
        pc = 1 - c if kbits & 1 else c
        peer = 4 * px + 2 * py + pc
        for b in range(len(b_refs)):
            s = (kbits - 1) * len(b_refs) + b
            sends.append(pltpu.make_async_remote_copy(
                src_ref=b_refs[b].at[peer], dst_ref=land_refs[b].at[me], send_sem=send_sems[s],
                recv_sem=recv_sems[s], device_id=(px, py, pc), device_id_type=pl.DeviceIdType.MESH))
            recvs.append(pltpu.make_async_remote_copy(
                src_ref=b_refs[b].at[peer], dst_ref=land_refs[b].at[peer], send_sem=send_sems[s],
                recv_sem=recv_sems[s], device_id=(px, py, pc), device_id_type=pl.DeviceIdType.MESH))
    return sends, recvs


def _scatter_start(name, bufs):
    nb = len(bufs)
    n = (N_DEV - 1) * nb

    def body(*refs):
        b_refs, land_refs, outs = refs[:nb], refs[nb:2 * nb], refs[2 * nb:]
        sends, _ = _scatter_copies(b_refs, land_refs, outs[:n], outs[n:2 * n])
        for cp in sends:
            cp.start()
        token = outs[2 * n + 2 * nb]
        token[...] = jnp.zeros_like(token)

    thru = tuple(pltpu.HBM(b.shape, b.dtype) for b in bufs)
    res = pl.pallas_call(
        body, name=name, in_specs=(_HBM,) * (2 * nb),
        out_specs=(_SEM,) * (2 * n) + (_HBM,) * (2 * nb) + (pl.BlockSpec(memory_space=pltpu.VMEM),),
        out_shape=(pltpu.SemaphoreType.DMA(()),) * (2 * n) + thru + thru + (jax.ShapeDtypeStruct((8, LANES), F32),),
        input_output_aliases={i: 2 * n + i for i in range(2 * nb)},
        compiler_params=pltpu.CompilerParams(has_side_effects=pltpu.SideEffectType.DATAFLOW_SIDE_EFFECTING),
    )(*[pltpu.with_memory_space_constraint(b, pltpu.HBM) for b in bufs],
      *[pltpu.with_memory_space_constraint(lax.empty(b.shape, b.dtype), pltpu.HBM) for b in bufs])
    return res[:2 * n], list(res[2 * n:2 * n + nb]), list(res[2 * n + nb:2 * n + 2 * nb]), res[2 * n + 2 * nb]


def _scatter_wait(name, sems, bufs_thru, lands_thru, after):
    nb = len(bufs_thru)
    n = (N_DEV - 1) * nb

    def body(*refs):
        b_refs, land_refs, sem_refs = refs[:nb], refs[nb:2 * nb], refs[2 * nb:2 * nb + 2 * n]
        sends, recvs = _scatter_copies(b_refs, land_refs, sem_refs[:n], sem_refs[n:])
        for cp in sends:
            cp.wait_send()
        for cp in recvs:
            cp.wait_recv()

    thru = tuple(pltpu.HBM(b.shape, b.dtype) for b in bufs_thru)
    res = pl.pallas_call(
        body, name=name, in_specs=(_HBM,) * (2 * nb) + (_SEM,) * (2 * n) + (pl.BlockSpec(memory_space=pl.ANY),),
        out_specs=(_HBM,) * (2 * nb), out_shape=thru + thru,
        input_output_aliases={i: i for i in range(2 * nb)},
        compiler_params=pltpu.CompilerParams(has_side_effects=pltpu.SideEffectType.DATAFLOW_SIDE_EFFECTING),
    )(*bufs_thru, *lands_thru, *sems, after)
    return list(res[:nb]), list(res[nb:])


def _chip_exchange_start(name, sums):
    def body(s_ref, land_ref, *outs):
        sems, token = outs[:N_CHIP_SEMS], outs[N_CHIP_SEMS + 2]
        sends, _ = _chip_copies(s_ref, land_ref, sems[:N_CHIP - 1], sems[N_CHIP - 1:])
        for cp in sends:
            cp.start()
        token[...] = jnp.zeros_like(token)

    res = pl.pallas_call(
        body, name=name, in_specs=(_HBM, _HBM),
        out_specs=(_SEM,) * N_CHIP_SEMS + (_HBM, _HBM, pl.BlockSpec(memory_space=pltpu.VMEM)),
        out_shape=(pltpu.SemaphoreType.DMA(()),) * N_CHIP_SEMS
        + (pltpu.HBM(sums.shape, sums.dtype), pltpu.HBM(sums.shape, sums.dtype), jax.ShapeDtypeStruct((8, LANES), F32)),
        input_output_aliases={0: N_CHIP_SEMS, 1: N_CHIP_SEMS + 1},
        compiler_params=pltpu.CompilerParams(has_side_effects=pltpu.SideEffectType.DATAFLOW_SIDE_EFFECTING),
    )(pltpu.with_memory_space_constraint(sums, pltpu.HBM),
      pltpu.with_memory_space_constraint(lax.empty(sums.shape, sums.dtype), pltpu.HBM))
    return res[:N_CHIP_SEMS], res[N_CHIP_SEMS], res[N_CHIP_SEMS + 1], res[N_CHIP_SEMS + 2]


def _chip_exchange_wait(name, sems, sums_thru, land_thru, after):
    def body(s_ref, land_ref, *rest):
        sems = rest[:N_CHIP_SEMS]
        sends, recvs = _chip_copies(s_ref, land_ref, sems[:N_CHIP - 1], sems[N_CHIP - 1:])
        for cp in sends:
            cp.wait_send()
        for cp in recvs:
            cp.wait_recv()

    return pl.pallas_call(
        body, name=name, in_specs=(_HBM, _HBM) + (_SEM,) * N_CHIP_SEMS + (pl.BlockSpec(memory_space=pl.ANY),),
        out_specs=(_HBM, _HBM),
        out_shape=(pltpu.HBM(sums_thru.shape, sums_thru.dtype), pltpu.HBM(sums_thru.shape, sums_thru.dtype)),
        input_output_aliases={0: 0, 1: 1},
        compiler_params=pltpu.CompilerParams(has_side_effects=pltpu.SideEffectType.DATAFLOW_SIDE_EFFECTING),
    )(sums_thru, land_thru, *sems, after)


def _adam_update(g, w, m, v):
    m_new = ADAM_B1 * m + (1.0 - ADAM_B1) * g
    v_new = ADAM_B2 * v + (1.0 - ADAM_B2) * jnp.square(g)
    m_hat = m_new / (1.0 - ADAM_B1 ** ADAM_STEP)
    v_hat = v_new / (1.0 - ADAM_B2 ** ADAM_STEP)
    return -ADAM_LR * (m_hat / (jnp.sqrt(v_hat) + ADAM_EPS) + ADAM_WD * w), m_new, v_new


def _adamw_rows(name, slots, parts):
    n_parts, rows = len(parts), [p[0].shape[0] for p in parts]
    rest = slots.shape[1] - sum(rows)

    def total(s_ref, at, r):
        g = s_ref[0, pl.ds(at, r), :]
        for j in range(1, slots.shape[0]):
            g = g + s_ref[j, pl.ds(at, r), :]
        return g

    def body(s_ref, *refs):
        ins, outs = refs[:3 * n_parts], refs[3 * n_parts:]
        at = 0
        for i, r in enumerate(rows):
            g = total(s_ref, at, r)
            w_ref, m_ref, v_ref = ins[3 * i:3 * i + 3]
            g_out, d_out, m_out, v_out = outs[4 * i:4 * i + 4]
            g_out[...] = g
            d_out[...], m_out[...], v_out[...] = _adam_update(g, w_ref[...], m_ref[...], v_ref[...])
            at += r
        outs[4 * n_parts][...] = total(s_ref, at, rest)

    shapes = [jax.ShapeDtypeStruct(p[0].shape, F32) for p in parts for _ in range(4)]
    res = pl.pallas_call(
        body, name=name, out_shape=shapes + [jax.ShapeDtypeStruct((rest, slots.shape[2]), F32)],
        compiler_params=pltpu.CompilerParams(vmem_limit_bytes=VMEM_LIMIT),
    )(slots, *[t for p in parts for t in p])
    return [res[4 * i:4 * i + 4] for i in range(n_parts)], res[4 * n_parts]


def _adamw(name, slots, w, m, v, tr=256):
    unit_mid = w.ndim == 3 and w.shape[1] == 1 and w.shape[0] > 1
    R, Wd = (w.shape[0], w.shape[2]) if unit_mid else w.shape[-2:]
    depth_axis = w.ndim == 3 and not unit_mid
    if unit_mid:
        tr, tc = 128, Wd
    elif R % tr == 0:
        tc = Wd
    else:
        tr, tc = R, (256 if (Wd % 256 == 0 and R > 256) else Wd)
    at = (slice(None), 0, slice(None)) if unit_mid else Ellipsis

    def body(s_ref, w_ref, m_ref, v_ref, g_out, d_out, m_out, v_out):
        g = s_ref[0].astype(F32)
        for j in range(1, slots.shape[0]):
            g = g + s_ref[j].astype(F32)
        g_out[at] = g
        d_out[at], m_out[at], v_out[at] = _adam_update(g, w_ref[at], m_ref[at], v_ref[at])

    if unit_mid:
        row = pl.BlockSpec((tr, 1, tc), lambda i, j: (i, 0, j))
    elif depth_axis:
        row = pl.BlockSpec((None, tr, tc), lambda i, j: (0, i, j))
    else:
        row = pl.BlockSpec((tr, tc), lambda i, j: (i, j))
    return pl.pallas_call(
        body, name=name, grid=(pl.cdiv(R, tr), Wd // tc),
        in_specs=[pl.BlockSpec((slots.shape[0], tr, tc), lambda i, j: (0, i, j)), row, row, row],
        out_specs=[row] * 4, out_shape=[jax.ShapeDtypeStruct(w.shape, F32)] * 4,
        compiler_params=pltpu.CompilerParams(dimension_semantics=("parallel", "parallel"),
                                             vmem_limit_bytes=VMEM_LIMIT),
    )(slots, w, m, v)


PACK_W = 1024
PACKED = [(n, s) for n, s in REPLICATED if n != "sgu_w"]
_SMALL_SIZES = [int(np.prod(s)) for _, s in PACKED]
_SMALL_ROWS = _round_up(_round_up(sum(_SMALL_SIZES) + PACK_W, PACK_W) // PACK_W, 8)
assert all(size % PACK_W == 0 for size in _SMALL_SIZES)
W_IN_SHARD = P_TOTAL // N_DEV


def _pack_rows(name, parts, rows):
    parts = [p.reshape(-1, PACK_W) for p in parts]
    assert all(p.dtype == F32 for p in parts)

    def body(*refs):
        at = 0
        for ref in refs[:-1]:
            refs[-1][pl.ds(at, ref.shape[0]), :] = ref[...]
            at += ref.shape[0]
        refs[-1][pl.ds(at, rows - at), :] = jnp.zeros((rows - at, PACK_W), F32)

    return pl.pallas_call(body, name=name, out_shape=jax.ShapeDtypeStruct((rows, PACK_W), F32))(*parts)


_W_IN_SEGMENTS = [(2 * D, 0, 0), (3 * D, 1, 0), (L_W, 1, 3 * D), (L_A, 1, 3 * D + 128), (L_G, 1, 3 * D + 256),
                  (2 * D, 2, 0)]
_W_IN_PADS = [(3 * D + L_W, 128 - L_W), (3 * D + 128 + L_A, 128 - L_A), (3 * D + 256 + L_G, 256 - L_G)]
_W_IN_GROUP_ROWS = [2 * D, 3 * D + 128 + 128 + 256, 2 * D]
assert sum(rows for rows, _, _ in _W_IN_SEGMENTS) == P_TOTAL


def _w_in_pieces(j):
    pieces, first = [], 0
    for rows, group, to in _W_IN_SEGMENTS:
        lo, hi = max(first, W_IN_SHARD * j), min(first + rows, W_IN_SHARD * (j + 1))
        if lo < hi:
            pieces.append((lo - W_IN_SHARD * j, hi - lo, group, to + lo - first))
        first += rows
    return pieces


def _w_in_groups_t(blocks):
    def body(x_ref, *o_refs):
        @pl.when(pl.program_id(0) == 0)
        def _():
            for at, rows in _W_IN_PADS:
                o_refs[1][pl.ds(at, rows), :] = jnp.zeros((rows, D), blocks.dtype)
        for j in range(N_DEV):
            @pl.when(pl.program_id(0) == j)
            def _(j=j):
                for at, rows, group, to in _w_in_pieces(j):
                    o_refs[group][pl.ds(to, rows), :] = x_ref[pl.ds(at, rows), :]

    return pl.pallas_call(
        body, name="w_in_groups", grid=(N_DEV,),
        in_specs=[pl.BlockSpec((None, W_IN_SHARD, D), lambda j: (j, 0, 0))],
        out_specs=[pl.BlockSpec((r, D), lambda j: (0, 0)) for r in _W_IN_GROUP_ROWS],
        out_shape=[jax.ShapeDtypeStruct((r, D), blocks.dtype) for r in _W_IN_GROUP_ROWS],
        compiler_params=pltpu.CompilerParams(dimension_semantics=("arbitrary",), vmem_limit_bytes=VMEM_LIMIT),
    )(blocks)


def _w_in_grad_blocks(g_sgu_t, g_rw_t, g_gate_t):
    def body(*refs):
        g_refs, o_ref = refs[:3], refs[3]
        for j in range(N_DEV):
            @pl.when(pl.program_id(0) == j)
            def _(j=j):
                for at, rows, group, to in _w_in_pieces(j):
                    o_ref[pl.ds(at, rows), :] = g_refs[group][pl.ds(to, rows), :]

    return pl.pallas_call(
        body, name="w_in_grad_blocks", grid=(N_DEV,),
        in_specs=[pl.BlockSpec((r, D), lambda j: (0, 0)) for r in _W_IN_GROUP_ROWS],
        out_specs=pl.BlockSpec((None, W_IN_SHARD, D), lambda j: (j, 0, 0)),
        out_shape=jax.ShapeDtypeStruct((N_DEV, W_IN_SHARD, D), g_sgu_t.dtype),
        compiler_params=pltpu.CompilerParams(dimension_semantics=("arbitrary",), vmem_limit_bytes=VMEM_LIMIT),
    )(g_sgu_t, g_rw_t, g_gate_t)


def _mesh_index():
    me = 4 * lax.axis_index("x") + 2 * lax.axis_index("y") + lax.axis_index("c")
    return me.astype(jnp.int32).reshape(1)


def _fill_slot(name, dst, src, idx, src_idx=None):
    R, Wd = dst.shape[1:]
    scalars = [idx] if src_idx is None else [idx, src_idx]
    if src_idx is None:
        src_spec = pl.BlockSpec((R, Wd), lambda i, *s: (0, 0))
    else:
        src_spec = pl.BlockSpec((None, R, Wd), lambda i, *s: (s[1][0], 0, 0))

    def body(*refs):
        src_ref, out_ref = refs[len(scalars) + 1], refs[len(scalars) + 2]
        out_ref[...] = src_ref[...]

    return pl.pallas_call(
        body, name=name,
        grid_spec=pltpu.PrefetchScalarGridSpec(
            num_scalar_prefetch=len(scalars), grid=(1,),
            in_specs=[pl.BlockSpec(memory_space=pl.ANY), src_spec],
            out_specs=pl.BlockSpec((None, R, Wd), lambda i, *s: (s[0][0], 0, 0))),
        out_shape=jax.ShapeDtypeStruct(dst.shape, dst.dtype),
        input_output_aliases={len(scalars): 0},
        compiler_params=pltpu.CompilerParams(vmem_limit_bytes=VMEM_LIMIT),
    )(*scalars, dst, src)


class TwoLevelGather:
    N_COPIES = 8
    PART_ALIGN = 16

    def __init__(self, bufs, skip_own=()):
        self.bufs, self.nb, self.skip_own = list(bufs), len(bufs), tuple(skip_own)
        self.any_specs = [pl.BlockSpec(memory_space=pl.ANY)] * self.nb
        self.out_shape = [jax.ShapeDtypeStruct((N_DEV,) + b.shape, b.dtype) for b in self.bufs]
        n = self.N_COPIES * self.nb
        self.sem_shapes = [pltpu.SemaphoreType.DMA((n,)), pltpu.SemaphoreType.DMA((n,)),
                           pltpu.SemaphoreType.DMA((self.nb,))]

    def _parts(self, b):
        shape = self.bufs[b].shape
        first = shape[0] // 2 // self.PART_ALIGN * self.PART_ALIGN if len(shape) == 2 else 0
        return [(0, first), (first, shape[0] - first)] if first else [None]

    def _copies(self, in_refs, out_refs, sems):
        send_sems, recv_sems, local_sems = sems
        nb = self.nb
        x, y, c = lax.axis_index("x"), lax.axis_index("y"), lax.axis_index("c")
        me, sibling = (x, y, c), (x, y, 1 - c)
        near, far = [(1 - x, y), (x, 1 - y)], (1 - x, 1 - y)

        def slot(b, dev):
            return out_refs[b].at[4 * dev[0] + 2 * dev[1] + dev[2]]

        def copy(b, k, block, to, own=False, rows=None):
            src, dst = in_refs[b] if own else slot(b, block), slot(b, block)
            if rows is not None:
                src, dst = src.at[pl.ds(*rows)], dst.at[pl.ds(*rows)]
            return pltpu.make_async_remote_copy(
                src_ref=src, dst_ref=dst, send_sem=send_sems.at[self.N_COPIES * b + k],
                recv_sem=recv_sems.at[self.N_COPIES * b + k], device_id=to, device_id_type=pl.DeviceIdType.MESH)

        ways = [(j, b) for j in range(2) for b in range(nb) if j < len(self._parts(b))]
        cp = {}
        cp["local"] = [pltpu.make_async_copy(in_refs[b], slot(b, me), local_sems.at[b]) for b in range(nb)
                       if b not in self.skip_own]
        cp["first"] = [copy(b, 0, me, sibling, own=True) for b in range(nb)]
        cp["first"] += [copy(b, 1 + j, me, (*near[j], c), own=True) for j in range(2) for b in range(nb)]
        cp["from_near"] = [copy(b, 1 + j, (*near[j], c), me) for j in range(2) for b in range(nb)]
        cp["near_on"] = [[copy(b, 5 + j, (*near[j], c), sibling)]
                         + ([copy(b, 3 + j, (*near[j], c), (*near[1 - j], c), rows=self._parts(b)[j])]
                            if (j, b) in ways else []) for j in range(2) for b in range(nb)]
        cp["from_far"] = [copy(b, 3 + j, (*far, c), me, rows=self._parts(b)[j]) for j, b in ways]
        cp["far_on"] = [copy(b, 7, (*far, c), sibling) for b in range(nb)]
        cp["from_sibling"] = [copy(b, 0, sibling, me) for b in range(nb)]
        cp["from_sibling"] += [copy(b, 5 + j, (*near[j], 1 - c), me) for j in range(2) for b in range(nb)]
        cp["from_sibling"] += [copy(b, 7, (*far, 1 - c), me) for b in range(nb)]
        return cp

    def start(self, in_refs, out_refs, sems):
        cp = self._copies(in_refs, out_refs, sems)
        for c in cp["first"] + cp["local"]:
            c.start()

    def pass_near(self, in_refs, out_refs, sems):
        cp = self._copies(in_refs, out_refs, sems)
        for arrived, onward in zip(cp["from_near"], cp["near_on"]):
            arrived.wait_recv()
            for c in onward:
                c.start()

    def pass_far(self, in_refs, out_refs, sems):
        cp = self._copies(in_refs, out_refs, sems)
        for c in cp["from_far"]:
            c.wait_recv()
        for c in cp["far_on"]:
            c.start()

    def finish(self, in_refs, out_refs, sems):
        cp = self._copies(in_refs, out_refs, sems)
        for c in cp["from_sibling"]:
            c.wait_recv()
        for c in cp["first"] + sum(cp["near_on"], []) + cp["far_on"]:
            c.wait_send()
        for c in cp["local"]:
            c.wait()

    def schedule(self, n_steps):
        return [(0, self.start), (max(n_steps // 2 - 1, 0), self.pass_near), (max(n_steps - 3, 0), self.pass_far),
                (n_steps - 1, self.finish)]


def _all_gather_two_level(name, bufs, skip_own=()):
    ex = TwoLevelGather(bufs, skip_own)

    def body(*refs):
        args = refs[:ex.nb], refs[ex.nb:2 * ex.nb], refs[2 * ex.nb:]
        for _, action in ex.schedule(1):
            action(*args)

    return pl.pallas_call(body, name=name, in_specs=ex.any_specs, out_specs=ex.any_specs, out_shape=ex.out_shape,
                          scratch_shapes=ex.sem_shapes)(*ex.bufs)


def _cols_from_blocks(blk):
    return jnp.transpose(blk, (1, 0, 2)).reshape(blk.shape[1], -1)


def _cols_to_blocks(g):
    r, c = g.shape
    return jnp.transpose(g.reshape(r, N_DEV, c // N_DEV), (1, 0, 2))


FIRST_WEIGHTS = ["w_in", "shift_b", "w_lora_w", "a_lora_w", "g_lora_w"]
LATE_WEIGHTS = ["w_proj_a", "w_proj_b", "w_out", "w_ffn1", "w_ffn2"]
SCAN_CARRIED = ["w_proj_a", "w_proj_b", "w_out"]
FFN_WEIGHTS = ["w_ffn1", "w_ffn2"]


def _late_weights(shards):
    ex = TwoLevelGather([shards[n][0].astype(BF16) for n in LATE_WEIGHTS])

    def finish(results):
        got = dict(zip(LATE_WEIGHTS, results))
        W = {n: got[n].reshape(-1, D) for n in ("w_proj_a", "w_proj_b", "w_out", "w_ffn2")}
        W["w_ffn1"] = got["w_ffn1"].reshape(N_DEV, D, -1)
        return W
    return ex, finish


def _gather_weights(shards):
    def payload(n):
        if n == "w_in":
            return jnp.transpose(shards[n][0]).astype(BF16)
        return shards[n] if n == "shift_b" else shards[n].astype(BF16)
    payloads = [payload(n) for n in FIRST_WEIGHTS]
    got = dict(zip(FIRST_WEIGHTS, _all_gather_two_level("weight_all_gather", payloads, skip_own=(0,))))
    got["w_in"] = _fill_slot("w_in_own_slot", got["w_in"], payloads[0], _mesh_index())
    W = {}
    W["w_sgu_t"], W["w_rw_t"], W["w_gate_t"] = _w_in_groups_t(got["w_in"])
    z = lambda r, c, dt: jnp.zeros((r, c), dt)
    W["w_lora"] = jnp.concatenate([_cols_from_blocks(got["w_lora_w"][:, 0]).astype(F32), z(128 - L_W, D, F32)], axis=0)
    W["a_lora"] = jnp.concatenate([_cols_from_blocks(got["a_lora_w"][:, 0]).astype(F32), z(128 - L_A, D, F32)], axis=0)
    W["g_lora"] = jnp.concatenate([_cols_from_blocks(got["g_lora_w"][:, 0]).astype(F32), z(256 - L_G, D, F32)], axis=0)
    sb = _cols_from_blocks(got["shift_b"][:, 0])
    W["sb"] = jnp.concatenate([sb[:, :3 * D], sb[:, 3 * D:3 * D + L_W], z(2, 128 - L_W, F32),
                               sb[:, 3 * D + L_W:3 * D + L_W + L_A], z(2, 128 - L_A, F32),
                               sb[:, 3 * D + L_W + L_A:], z(2, 256 - L_G, F32)], axis=1)
    return W


def _replicated_weights(rep):
    W = {n: rep[n] for n in ("g_mix", "sgu_ln_w", "sgu_ln_b", "w0", "a0", "k_k", "k_a", "r_k", "ln_x_w", "ln_x_b",
                             "g_ffn")}
    W["g_final"] = rep["g_final"].reshape(1, D)
    W["sgu_w"] = rep["sgu_w"][0]
    W["sgu_bt"] = jnp.transpose(rep["sgu_b"][0])
    return W


def _late_grad_blocks(G):
    return Exchange([G[n].reshape(N_DEV, -1, D) for n in SCAN_CARRIED]
                    + [G["sgu_w"].reshape(SGU_G * SGU_C, SGU_C).astype(GRAD_PAYLOAD)],
                    [False] * len(SCAN_CARRIED) + [True])


def _first_grad_blocks(G):
    sbg = G["sb"]
    c = 3 * D
    sb = jnp.concatenate([sbg[:, :c], sbg[:, c:c + L_W], sbg[:, c + 128:c + 128 + L_A],
                          sbg[:, c + 256:c + 256 + L_G]], axis=1)
    return {
        "shift_b": _cols_to_blocks(sb),
        "w_lora_w": _cols_to_blocks(G["w_lora"][:L_W]), "a_lora_w": _cols_to_blocks(G["a_lora"][:L_A]),
        "g_lora_w": _cols_to_blocks(G["g_lora"][:L_G]),
    }


def _replicated_grads(G):
    small = {n: G[n] for n in ("g_mix", "sgu_ln_w", "sgu_ln_b", "w0", "a0", "k_k", "k_a", "r_k", "ln_x_w", "ln_x_b",
                               "g_ffn", "g_final")}
    small["sgu_w"] = G["sgu_w"]
    small["sgu_b"] = jnp.transpose(G["sgu_bt"])
    return small


def kernel(x, g_mix, w_in, sgu_ln_w, sgu_ln_b, sgu_w, sgu_b, w_proj_a, shift_b, w_lora_w, w0, a_lora_w, a0, g_lora_w, k_k, k_a, r_k, ln_x_w, ln_x_b, w_proj_b, w_out, g_ffn, w_ffn1, w_ffn2, g_final, loss_target, m_g_mix, m_w_in, m_sgu_ln_w, m_sgu_ln_b, m_sgu_w, m_sgu_b, m_w_proj_a, m_shift_b, m_w_lora_w, m_w0, m_a_lora_w, m_a0, m_g_lora_w, m_k_k, m_k_a, m_r_k, m_ln_x_w, m_ln_x_b, m_w_proj_b, m_w_out, m_g_ffn, m_w_ffn1, m_w_ffn2, m_g_final, v_g_mix, v_w_in, v_sgu_ln_w, v_sgu_ln_b, v_sgu_w, v_sgu_b, v_w_proj_a, v_shift_b, v_w_lora_w, v_w0, v_a_lora_w, v_a0, v_g_lora_w, v_k_k, v_k_a, v_r_k, v_ln_x_w, v_ln_x_b, v_w_proj_b, v_w_out, v_g_ffn, v_w_ffn1, v_w_ffn2, v_g_final):
    env = dict(locals())
    weights = {n: env[n] for n in WEIGHT_ORDER}
    moms = {n: env["m_" + n] for n in WEIGHT_ORDER}
    vars_ = {n: env["v_" + n] for n in WEIGHT_ORDER}

    shards = {n: weights[n] for n, _, _ in SHARDED}
    W = _gather_weights(shards)
    W.update(_replicated_weights({n: weights[n] for n, _ in REPLICATED}))
    in_flight = {}

    def send_w_in_grads(G):
        blocks = _w_in_grad_blocks(G["w_sgu_t"], G["w_rw_t"], G["w_gate_t"])
        got = _pair_exchange("grad_pair_exchange", blocks)
        core = lax.axis_index("c").astype(jnp.int32).reshape(1)
        sums = _pair_sum("grad_pair_sum", blocks, got, core)
        in_flight["sems"], in_flight["sums"], in_flight["land"], token = _chip_exchange_start("grad_chip_start", sums)
        return token

    def send_ffn_grads(G):
        in_flight["ffn"] = _scatter_start("grad_ffn_start", [G["w_ffn1"], G["w_ffn2"].reshape(N_DEV, -1, D)])
        return in_flight["ffn"][3]

    loss_part, dx, G, late_slots = _local_step(x[0], loss_target[0], W, late_weights=_late_weights(shards),
                                               early_grads=_late_grad_blocks, w_in_grads_ready=send_w_in_grads,
                                               ffn_grads_ready=send_ffn_grads)

    slots = dict(zip(SCAN_CARRIED, late_slots))
    me = _mesh_index()
    sent, landed = _scatter_wait("grad_ffn_wait", *in_flight["ffn"][:3], after=dx)
    for i, n in enumerate(FFN_WEIGHTS):
        slots[n] = _fill_slot("grad_own_slot_" + n, landed[i], sent[i], me, src_idx=me)
    blocks = _first_grad_blocks(G)
    small = _replicated_grads(G)
    small_parts = [small[n] for n, _ in PACKED] + [jnp.full((PACK_W,), loss_part, F32)]
    rest = [n for n in FIRST_WEIGHTS if n != "w_in"]
    res = _exchange("grad_exchange", [blocks[n] for n in rest] + [_pack_rows("grad_pack", small_parts, _SMALL_ROWS)],
                    [False] * len(rest) + [True])
    slots.update(zip(rest, res[:-1]))
    small_slots = res[-1]
    sums, chip_slots = _chip_exchange_wait("grad_chip_wait", in_flight["sems"], in_flight["sums"], in_flight["land"],
                                           after=small_slots)
    my_chip = (2 * lax.axis_index("x") + lax.axis_index("y")).astype(jnp.int32).reshape(1)
    slots["w_in"] = _fill_slot("grad_own_slot", chip_slots, sums, my_chip, src_idx=my_chip)

    outs = [dict(), dict(), dict(), dict()]
    for n, _, _ in SHARDED:
        if n == "w_in":
            res = _adamw("adamw_" + n, slots[n], *[jnp.transpose(t, (2, 0, 1)) for t in (weights[n], moms[n], vars_[n])])
            res = [jnp.transpose(t, (1, 2, 0)) for t in res]
        else:
            res = _adamw("adamw_" + n, slots[n], weights[n], moms[n], vars_[n])
        for k in range(4):
            outs[k][n] = res[k]

    sgu_shape = (SGU_G * SGU_C, SGU_C)
    res = _adamw("adamw_sgu_w", late_slots[len(SCAN_CARRIED)], *[t.reshape(sgu_shape) for t in
                                                                  (weights["sgu_w"], moms["sgu_w"], vars_["sgu_w"])])
    for k in range(4):
        outs[k]["sgu_w"] = res[k].reshape(weights["sgu_w"].shape)

    small_out, after_them = _adamw_rows(
        "adamw_replicated", small_slots,
        [[d[n].reshape(-1, PACK_W) for d in (weights, moms, vars_)] for n, _ in PACKED])
    for (n, s), res in zip(PACKED, small_out):
        for k in range(4):
            outs[k][n] = res[k].reshape(s)
    loss = after_them[0, 0]
    return (loss, dx[None], *[outs[0][n] for n in WEIGHT_ORDER], *[outs[1][n] for n in WEIGHT_ORDER],
            *[outs[2][n] for n in WEIGHT_ORDER], *[outs[3][n] for n in WEIGHT_ORDER])
```

```python
import functools
import numpy as np
import jax
import jax.numpy as jnp
from jax import lax
from jax.experimental import pallas as pl
from jax.experimental.pallas import tpu as pltpu

F32 = jnp.float32
BF16 = jnp.bfloat16

D = 1024
NH, HN = 16, 64
NP, PW = NH // 2, 2 * HN
SGU_G, SGU_C = 8, 128
L_W, L_A, L_G = 64, 64, 160
C_B = 3 * D + L_W + L_A + L_G
P_TOTAL = 2 * D + C_B + 2 * D
D_FF = 4 * D
RW_INT = 3 * D + 128 + 128 + 256
NORM_EPS, LN_EPS, GN_EPS = 1e-6, 1e-5, 64e-5
N_DEV = 8
LANES = 128
SCAN_C = 64
N_KEPT = 4
SOLVE_B = 16
SCAN_PRECISION = lax.Precision.HIGH
SCAN_OUT_PRECISION = lax.Precision.DEFAULT
GRAD_PAYLOAD = BF16
VMEM_LIMIT = 56 * 1024 * 1024
MATMUL_VMEM_BUDGET = 40 * 1024 * 1024
STEP_COST_BYTES = 512 * 1024
HBM_COST_RATIO = 3
ACC_PASS_WEIGHT = 4

ADAM_LR, ADAM_B1, ADAM_B2, ADAM_EPS, ADAM_WD, ADAM_STEP = 0.001, 0.9, 0.999, 1e-08, 0.01, 10

SHARDED = [
    ("w_in", (D, P_TOTAL), 1), ("w_proj_a", (D, D), 0), ("shift_b", (2, C_B), 1), ("w_lora_w", (L_W, D), 1),
    ("a_lora_w", (L_A, D), 1), ("g_lora_w", (L_G, D), 1), ("w_proj_b", (D, D), 0), ("w_out", (D, D), 0),
    ("w_ffn1", (D, D_FF), 1), ("w_ffn2", (D_FF, D), 0),
]
REPLICATED = [
    ("g_mix", (1, D)), ("sgu_ln_w", (1, D)), ("sgu_ln_b", (1, D)), ("sgu_w", (1, SGU_G, SGU_C, SGU_C)),
    ("sgu_b", (1, SGU_G, SGU_C)), ("w0", (1, D)), ("a0", (1, D)), ("k_k", (1, D)), ("k_a", (1, D)), ("r_k", (1, D)),
    ("ln_x_w", (1, D)), ("ln_x_b", (1, D)), ("g_ffn", (1, D)), ("g_final", (D,)),
]
WEIGHT_ORDER = ["g_mix", "w_in", "sgu_ln_w", "sgu_ln_b", "sgu_w", "sgu_b", "w_proj_a", "shift_b", "w_lora_w", "w0",
                "a_lora_w", "a0", "g_lora_w", "k_k", "k_a", "r_k", "ln_x_w", "ln_x_b", "w_proj_b", "w_out", "g_ffn",
                "w_ffn1", "w_ffn2", "g_final"]


def _round_up(n, m):
    return (n + m - 1) // m * m


def _pick(n, target):
    if n <= target:
        return n
    best = None
    for t in range(LANES, target + 1, LANES):
        if n % t == 0:
            best = t
    assert best is not None, (n, target)
    return best


def _matmul(name, a, b, mode, out_dtype=F32, tm=2048, tn=1024, tk=4096, out_blocks=None, epilogue=None, extras=(),
            out_dtypes=(), after=None, whole_rows=False, out_widths=None):
    b_blocks = b.shape[0] if b.ndim == 3 else None
    bshape = b.shape if b.ndim == 2 else (b.shape[1], b.shape[0] * b.shape[2])
    if mode == "nn":
        (M, K), (K2, N) = a.shape, bshape
    elif mode == "nt":
        (M, K), (N, K2) = a.shape, bshape
    else:
        (K, M), (K2, N) = a.shape, bshape
    assert K == K2, (name, a.shape, b.shape)
    assert b_blocks is None or mode != "tn"
    assert out_blocks is None or mode == "tn"
    tn = min(tn, N // (out_blocks or 1), bshape[1] // b_blocks if (b_blocks and mode == "nn") else tn)
    blocked_k = bool(b_blocks) and mode == "nt"
    tm, tn, tk = _pick(M, tm), _pick(N, tn), (K if blocked_k else _pick(K, tk))

    def vmem_bytes(tm, tk):
        tiles = tm * tk * a.dtype.itemsize + tk * tn * b.dtype.itemsize
        for i, dt in enumerate(out_dtypes if epilogue else (out_dtype,)):
            tiles += tm * (out_widths[i] if out_widths else tn) * jnp.dtype(dt).itemsize
        for x in extras:
            arr = x[0] if isinstance(x, tuple) else x
            tiles += (tm if arr.shape[0] > 1 else 1) * tn * arr.dtype.itemsize
        return 2 * tiles + (tm * tn * 4 if K // tk > 1 else 0)

    def cost(tm, tk):
        ni, nj, nk = M // tm, N // tn, K // tk
        steps = ni * nj * nk
        acc_passes = steps * tm * tn * 8 * ACC_PASS_WEIGHT if nk > 1 else 0
        a_reads = M * K * a.dtype.itemsize * (nj if nk > 1 else 1)
        b_reads = K * N * b.dtype.itemsize * (ni if (nj > 1 or nk > 1) else 1)
        return (steps * STEP_COST_BYTES + acc_passes + vmem_bytes(tm, tk) // 2
                + HBM_COST_RATIO * (a_reads + b_reads))

    options = [(m, k) for m in ({M} if whole_rows else {_pick(M, max(t, LANES)) for t in (tm, tm // 2, tm // 4)})
               for k in ({K} if blocked_k else {_pick(K, max(t, LANES)) for t in (tk, tk // 2, tk // 4)})
               if vmem_bytes(m, k) <= MATMUL_VMEM_BUDGET]
    tm, tk = min(options, key=lambda o: cost(*o))
    nk = K // tk
    dims = {"nn": (((1,), (0,)), ((), ())), "nt": (((1,), (1,)), ((), ())), "tn": (((0,), (0,)), ((), ()))}[mode]

    n_x, n_o = len(extras), len(out_dtypes) if epilogue else 1
    n_after = 0 if after is None else 1

    def body(a_ref, b_ref, *rest):
        x_refs, o_refs, acc = rest[:n_x], rest[n_x + n_after:n_x + n_after + n_o], rest[n_x + n_after + n_o:]
        if blocked_k:
            bw = b.shape[2]
            part = sum(lax.dot_general(a_ref[:, blk * bw:(blk + 1) * bw].astype(BF16), b_ref[blk].astype(BF16), dims,
                                       preferred_element_type=F32) for blk in range(b_blocks))
        else:
            part = lax.dot_general(a_ref[...].astype(BF16), b_ref[...].astype(BF16), dims, preferred_element_type=F32)

        def finish(res):
            outs = epilogue(res, *[r[...] for r in x_refs]) if epilogue else (res,)
            for r, v in zip(o_refs, outs):
                r[...] = v.astype(r.dtype)

        if nk == 1:
            finish(part)
            return
        acc_ref, k = acc[0], pl.program_id(2)

        @pl.when(k == 0)
        def _():
            acc_ref[...] = part

        @pl.when(k > 0)
        def _():
            acc_ref[...] += part

        @pl.when(k == nk - 1)
        def _():
            finish(acc_ref[...])

    a_spec = {"nn": pl.BlockSpec((tm, tk), lambda i, j, k: (i, k)), "nt": pl.BlockSpec((tm, tk), lambda i, j, k: (i, k)),
              "tn": pl.BlockSpec((tk, tm), lambda i, j, k: (k, i))}[mode]
    b_spec = {"nn": pl.BlockSpec((tk, tn), lambda i, j, k: (k, j)), "nt": pl.BlockSpec((tn, tk), lambda i, j, k: (j, k)),
              "tn": pl.BlockSpec((tk, tn), lambda i, j, k: (k, j))}[mode]
    if b_blocks and mode == "nn":
        per = b.shape[2] // tn
        b_spec = pl.BlockSpec((None, tk, tn), lambda i, j, k: (j // per, k, j % per))
    elif b_blocks:
        b_spec = pl.BlockSpec((b_blocks, tn, b.shape[2]), lambda i, j, k: (0, j, 0))
    out_spec = pl.BlockSpec((tm, tn), lambda i, j, k: (i, j))
    out_shape = jax.ShapeDtypeStruct((M, N), out_dtype)
    if out_blocks:
        per_o = N // out_blocks // tn
        out_spec = pl.BlockSpec((None, tm, tn), lambda i, j, k: (j // per_o, i, j % per_o))
        out_shape = jax.ShapeDtypeStruct((out_blocks, M, N // out_blocks), out_dtype)
    epi_widths = list(out_widths) if out_widths else [N] * n_o
    assert all(w == N for w in epi_widths) or N == tn
    epi_specs = [pl.BlockSpec((tm, tn if w == N else w), lambda i, j, k: (i, j)) for w in epi_widths]
    x_specs, x_args = [], []
    for x in extras:
        arr, off = x if isinstance(x, tuple) else (x, 0)
        if arr.shape[0] == 1:
            x_specs.append(pl.BlockSpec((1, tn), lambda i, j, k: (0, j)))
        else:
            x_specs.append(pl.BlockSpec((tm, tn), lambda i, j, k, off=off: (i, j + off)))
        x_args.append(arr)
    res = pl.pallas_call(
        body, name=name, grid=(M // tm, N // tn, nk),
        in_specs=[a_spec, b_spec] + x_specs + [pl.BlockSpec(memory_space=pl.ANY)] * n_after,
        out_specs=epi_specs if epilogue else out_spec,
        out_shape=[jax.ShapeDtypeStruct((M, w), dt) for w, dt in zip(epi_widths, out_dtypes)] if epilogue else out_shape,
        scratch_shapes=[pltpu.VMEM((tm, tn), F32)] if nk > 1 else [],
        compiler_params=pltpu.CompilerParams(dimension_semantics=("parallel", "parallel", "arbitrary"),
                                             vmem_limit_bytes=VMEM_LIMIT),
    )(a, b, *x_args, *([after] if n_after else []))
    return res


class Rows:
    def __init__(self, arr, width=None, cb=0):
        self.arr, self.width, self.cb = arr, (arr.shape[1] if width is None else width), cb


class Heads:
    def __init__(self, arr):
        self.arr = arr


class Halo:
    def __init__(self, arr, side):
        self.arr, self.side = arr, side


def _rows_call(name, fn, ins, consts, outs, accs=(), tm=512, with_pid=False):
    T = next(o.arr.shape[1] if isinstance(o, Heads) else o.arr.shape[0] for o in ins if not isinstance(o, Halo))
    tm = min(tm, T)
    n_tiles = T // tm
    n_in, n_c, n_out = len(ins), len(consts), len(outs)
    in_specs, args = [], []
    for o in ins:
        if isinstance(o, Rows):
            in_specs.append(pl.BlockSpec((tm, o.width), lambda i, cb=o.cb: (i, cb)))
        elif isinstance(o, Heads):
            in_specs.append(pl.BlockSpec((NP, tm, PW), lambda i: (0, i, 0)))
        else:
            w = o.arr.shape[1]
            if o.side < 0:
                in_specs.append(pl.BlockSpec((8, w), lambda i: (jnp.maximum(i * (tm // 8) - 1, 0), 0)))
            else:
                in_specs.append(pl.BlockSpec((8, w), lambda i: (jnp.minimum((i + 1) * (tm // 8), T // 8 - 1), 0)))
        args.append(o.arr)
    for c in consts:
        in_specs.append(pl.BlockSpec(c.shape, lambda i, nd=c.ndim: (0,) * nd))
        args.append(c)
    out_specs, out_shape = [], []
    for o in outs:
        if o[0] == "rows":
            out_specs.append(pl.BlockSpec((tm, o[1]), lambda i: (i, 0)))
            out_shape.append(jax.ShapeDtypeStruct((T, o[1]), o[2]))
        else:
            out_specs.append(pl.BlockSpec((NP, tm, PW), lambda i: (0, i, 0)))
            out_shape.append(jax.ShapeDtypeStruct((NP, T, PW), o[1]))
    for shape, dt in accs:
        out_specs.append(pl.BlockSpec(shape, lambda i, nd=len(shape): (0,) * nd))
        out_shape.append(jax.ShapeDtypeStruct(shape, dt))

    def body(*refs):
        i = pl.program_id(0)
        vals = []
        vals = [r[...] for r in refs[:n_in + n_c]]
        res = fn(i, n_tiles, *vals) if with_pid else fn(*vals)
        out_refs = refs[n_in + n_c:]
        for r, v in zip(out_refs[:n_out], res[:n_out]):
            r[...] = v.astype(r.dtype)
        if accs:
            @pl.when(i == 0)
            def _():
                for r in out_refs[n_out:]:
                    r[...] = jnp.zeros_like(r)

            for r, v in zip(out_refs[n_out:], res[n_out:]):
                r[...] += v.astype(r.dtype)

    res = pl.pallas_call(
        body, name=name, grid=(n_tiles,), in_specs=in_specs, out_specs=out_specs, out_shape=out_shape,
        compiler_params=pltpu.CompilerParams(dimension_semantics=("arbitrary",), vmem_limit_bytes=VMEM_LIMIT),
    )(*args)
    return res


def _rms(x, g):
    return x * lax.rsqrt(jnp.mean(x * x, axis=-1, keepdims=True) + NORM_EPS) * g


def _gelu(x):
    return 0.5 * x * (1.0 + lax.erf(x * 0.7071067811865476))


def _sigmoid(x):
    return 1.0 / (1.0 + jnp.exp(-x))


def _bdot(a, b):
    return jnp.dot(a.astype(BF16), b.astype(BF16), preferred_element_type=F32)


def _to_heads(x):
    return jnp.concatenate([x[:, p * PW:(p + 1) * PW][None] for p in range(NP)], axis=0)


def _from_heads(xp):
    return jnp.concatenate([xp[p] for p in range(NP)], axis=-1)


def _head_sum(xp):
    low = lax.broadcasted_iota(jnp.int32, xp.shape, xp.ndim - 1) < HN
    both = jnp.sum(xp, axis=-1, keepdims=True)
    first = jnp.sum(jnp.where(low, xp, 0.0), axis=-1, keepdims=True)
    return jnp.where(low, first, both - first)


def _split_pairs(xp):
    return jnp.concatenate([xp[:, :, :HN], xp[:, :, HN:]], axis=0)


def _join_pairs(xh):
    return jnp.concatenate([xh[:NP], xh[NP:]], axis=-1)


def _sgu_fn(p, ln_w, ln_b, sw, sbt):
    z = _gelu(p)
    u, v = z[:, :D], z[:, D:]
    mu = jnp.mean(v, axis=-1, keepdims=True)
    var = jnp.mean(jnp.square(v - mu), axis=-1, keepdims=True)
    vn = (v - mu) * lax.rsqrt(var + LN_EPS) * ln_w + ln_b
    ri = lax.broadcasted_iota(jnp.int32, (SGU_C, SGU_C), 0)
    ci = lax.broadcasted_iota(jnp.int32, (SGU_C, SGU_C), 1)
    mask = (ci <= ri).astype(F32)
    dg = D // SGU_G
    parts = []
    for g in range(SGU_G):
        parts.append(_bdot(sw[g] * mask, vn[:, g * dg:(g + 1) * dg]) + sbt[:, g:g + 1])
    return u * jnp.concatenate(parts, axis=-1)


def _pre_fn(qr, qk, qv, qxw, qxa, qxg, wl, w0, al, a0, gl, k_k, k_a):
    w = -jax.nn.softplus(-(w0 + _bdot(jnp.tanh(qxw), wl))) - 0.5
    lw = -jnp.exp(w)
    aa = _sigmoid(a0 + _bdot(qxa, al))
    g = _bdot(_sigmoid(qxg), gl)
    kk = _to_heads(qk * k_k)
    kk = kk / jnp.maximum(jnp.sqrt(_head_sum(kk * kk)), 1e-12)
    k2 = qk * (1.0 + (aa - 1.0) * k_a)
    return _to_heads(qr), _to_heads(lw), _to_heads(k2), _to_heads(qv), kk, _to_heads(aa), g


def _post_fn(o, r, k2, v, g, ln_w, ln_b, r_k):
    mu = _head_sum(o) * (1.0 / HN)
    d = o - mu
    var = _head_sum(d * d) * (1.0 / HN)
    on = d * lax.rsqrt(var + GN_EPS) * ln_w + ln_b
    bonus = _head_sum(r * k2 * r_k) * v
    return _from_heads(on + bonus) * g


def _gate_fn(pg, ya, yb):
    return _sigmoid(pg[:, :D]) * ya + _sigmoid(pg[:, D:]) * yb


def _bmm(x, y, cx, cy, out_path=False):
    return lax.dot_general(x, y, (((cx,), (cy,)), ((0,), (0,))),
                           precision=SCAN_OUT_PRECISION if out_path else SCAN_PRECISION, preferred_element_type=F32)


def _unit_lower_inverse(M):
    C = M.shape[1]
    ti = lax.broadcasted_iota(jnp.int32, (C, C), 0)
    tj = lax.broadcasted_iota(jnp.int32, (C, C), 1)
    eye = (ti == tj).astype(F32)
    same = lambda b: (ti // b == tj // b).astype(F32)
    X = -(M * same(SOLVE_B))
    inv = eye + X
    span = 1
    while 2 * span < SOLVE_B:
        X = _bmm(X, X, 2, 1)
        inv = inv + _bmm(inv, X, 2, 1)
        span *= 2
    b = SOLVE_B
    while b < C:
        low = M * (same(2 * b) - same(b))
        inv = inv - _bmm(_bmm(inv, low, 2, 1, out_path=True), inv, 2, 1, out_path=True)
        b *= 2
    return inv


@jax.custom_vjp
def _unit_lower_solve(inv, M, y):
    return _bmm(inv, y, 2, 1)


def _unit_lower_solve_fwd(inv, M, y):
    u = _bmm(inv, y, 2, 1)
    return u, (inv, u)


def _unit_lower_solve_bwd(res, du):
    inv, u = res
    dy = _bmm(inv, du, 1, 1)
    return jnp.zeros_like(inv), -_bmm(dy, u, 2, 2), dy


_unit_lower_solve.defvjp(_unit_lower_solve_fwd, _unit_lower_solve_bwd)


@jax.custom_vjp
def _unit_lower_solved(inv, u, M, y):
    return u


_unit_lower_solved.defvjp(lambda inv, u, M, y: (u, (inv, u)),
                          lambda res, du: (jnp.zeros_like(res[0]), jnp.zeros_like(res[1]))
                          + _unit_lower_solve_bwd(res, du)[1:])


@functools.partial(jax.custom_vjp, nondiff_argnums=(0,))
def _kept(fn, value, *args):
    return value


def _kept_fwd(fn, value, *args):
    return value, args


def _kept_bwd(fn, args, d):
    _, vjp = jax.vjp(fn, *args)
    return (jnp.zeros_like(d),) + tuple(vjp(d))


_kept.defvjp(_kept_fwd, _kept_bwd)


def _sum_over_time(x, reverse):
    C = x.shape[1]
    ti = lax.broadcasted_iota(jnp.int32, (C, C), 0)
    tj = lax.broadcasted_iota(jnp.int32, (C, C), 1)
    ones = jnp.broadcast_to(((tj >= ti) if reverse else (tj <= ti)).astype(BF16), (x.shape[0], C, C))
    hi = x.astype(BF16)
    r1 = x - hi.astype(F32)
    mid = r1.astype(BF16)
    lo = (r1 - mid.astype(F32)).astype(BF16)
    dn = (((2,), (1,)), ((0,), (0,)))
    return sum(lax.dot_general(ones, p, dn, preferred_element_type=F32) for p in (lo, mid, hi))


@jax.custom_vjp
def _time_cumsum(lw):
    return _sum_over_time(lw, reverse=False)


_time_cumsum.defvjp(lambda lw: (_sum_over_time(lw, reverse=False), None),
                    lambda _, d: (_sum_over_time(d, reverse=True),))


def _chunk_fn(S0, r, lw, k, v, kk, a, kept=None):
    C = SCAN_C
    bmm = _bmm
    ti = lax.broadcasted_iota(jnp.int32, (C, C), 0)
    tj = lax.broadcasted_iota(jnp.int32, (C, C), 1)
    incl2 = jnp.concatenate([(tj <= ti).astype(F32)] * 2, axis=1)
    strict = (tj < ti).astype(F32)
    n_mask = jnp.concatenate([jnp.zeros((C, C), F32), strict], axis=1)

    def known(name, fn, *args):
        return fn(*args) if kept is None else _kept(fn, kept[name], *args)

    cum = known("cum", _time_cumsum, lw)
    g_in, g_ex, g_inv = jnp.exp(cum), jnp.exp(cum - lw), jnp.exp(-cum)
    kkt, rt = kk * g_ex, r * g_in
    bk = jnp.concatenate([kk * a * g_inv, k * g_inv], axis=1)
    kr = jnp.concatenate([kkt, rt], axis=1)
    ratios = known("ratios", lambda x, y: bmm(x, y, 2, 2), kr, bk)
    A = ratios[:, :C]
    M = A[:, :, :C] * strict
    zv = jnp.concatenate([jnp.zeros_like(v), v], axis=1)
    s0_side = bmm(kr, S0, 2, 2, out_path=True)
    rhs = s0_side[:, :C] + bmm(A * n_mask, zv, 2, 1, out_path=True)
    if kept is None:
        inv = lax.stop_gradient(_unit_lower_inverse(M))
        y = _unit_lower_solve(inv, M, rhs)
    else:
        inv = kept["inv"]
        y = _unit_lower_solved(inv, kept["y"], M, rhs)
    z = jnp.concatenate([-y, v], axis=1)
    O = s0_side[:, C:] + bmm(ratios[:, C:] * incl2, z, 2, 1, out_path=True)
    g_end = g_in[:, C - 1:C, :]
    S1 = S0 * g_end + bmm(z, bk * g_end, 1, 1, out_path=True)
    return O, S1, dict(cum=cum, ratios=ratios, y=y, inv=inv)


def _scan_fwd(r, lw, k, v, kk, a, ex=None, tb=256):
    assert SCAN_C == HN and 2 * SCAN_C == PW
    T = r.shape[1]
    tb = min(tb, T)
    n_chunks = tb // SCAN_C
    nb = T // tb
    nx = ex.nb if ex else 0

    def body(*refs):
        r_ref, lw_ref, k_ref, v_ref, kk_ref, a_ref = refs[:6]
        x_in, (o_ref, s0_ref), x_out = refs[6:6 + nx], refs[6 + nx:8 + nx], refs[8 + nx:8 + 2 * nx]
        s_ref, sems = refs[8 + 2 * nx], refs[9 + 2 * nx:]

        plan = ex.schedule(nb) if ex else []

        @pl.when(pl.program_id(0) == 0)
        def _():
            s_ref[...] = jnp.zeros_like(s_ref)
            for at, action in plan[:1]:
                action(x_in, x_out, sems)

        def step(c, carry):
            sl = pl.ds(pl.multiple_of(c * SCAN_C, SCAN_C), SCAN_C)
            S0 = s_ref[...]
            O, S1, keep = _chunk_fn(S0, *[_split_pairs(ref[:, sl, :])
                                          for ref in (r_ref, lw_ref, k_ref, v_ref, kk_ref, a_ref)])
            o_ref[:, sl, :] = _join_pairs(O)
            s0_ref[c, 0] = jnp.concatenate([S0, keep["inv"]], axis=-1)
            s0_ref[c, 1] = jnp.concatenate([keep["cum"], keep["y"]], axis=-1)
            s0_ref[c, 2] = keep["ratios"][:, :SCAN_C]
            s0_ref[c, 3] = keep["ratios"][:, SCAN_C:]
            s_ref[...] = S1
            return carry

        lax.fori_loop(0, n_chunks, step, 0)

        for at, action in plan[1:]:
            pl.when(pl.program_id(0) == at)(functools.partial(action, x_in, x_out, sems))

    hm = pl.BlockSpec((NP, tb, PW), lambda i: (0, i, 0))
    res = pl.pallas_call(
        body, name="rwkv_scan_fwd", grid=(nb,), in_specs=[hm] * 6 + (ex.any_specs if ex else []),
        out_specs=[hm, pl.BlockSpec((n_chunks, N_KEPT, NH, HN, PW), lambda i: (i, 0, 0, 0, 0))]
        + (ex.any_specs if ex else []),
        out_shape=[jax.ShapeDtypeStruct((NP, T, PW), F32),
                   jax.ShapeDtypeStruct((T // SCAN_C, N_KEPT, NH, HN, PW), F32)]
        + (ex.out_shape if ex else []),
        scratch_shapes=[pltpu.VMEM((NH, HN, HN), F32)] + (ex.sem_shapes if ex else []),
        compiler_params=pltpu.CompilerParams(dimension_semantics=("arbitrary",), vmem_limit_bytes=VMEM_LIMIT),
    )(r, lw, k, v, kk, a, *(ex.bufs if ex else []))
    return res[0], res[1], list(res[2:])


def _scan_bwd(r, lw, k, v, kk, a, s0s, do, ex=None, tb=128):
    T = r.shape[1]
    tb = min(tb, T)
    n_chunks = tb // SCAN_C
    nb = T // tb
    nx = ex.nb if ex else 0

    def body(*refs):
        r_ref, lw_ref, k_ref, v_ref, kk_ref, a_ref, s0_ref, do_ref = refs[:8]
        x_in, (dr, dlw, dk, dv, dkk, da), x_out = refs[8:8 + nx], refs[8 + nx:14 + nx], refs[14 + nx:14 + 2 * nx]
        ds_ref, sems = refs[14 + 2 * nx], refs[15 + 2 * nx:]

        plan = ex.schedule(nb) if ex else []

        @pl.when(pl.program_id(0) == 0)
        def _():
            ds_ref[...] = jnp.zeros_like(ds_ref)
            for at, action in plan[:1]:
                action(x_in, x_out, sems)

        def step(j, carry):
            c = n_chunks - 1 - j
            sl = pl.ds(pl.multiple_of(c * SCAN_C, SCAN_C), SCAN_C)
            s0_inv, cum_y = s0_ref[c, 0], s0_ref[c, 1]
            kept = dict(inv=s0_inv[:, :, HN:], cum=cum_y[:, :, :HN], y=cum_y[:, :, HN:],
                        ratios=jnp.concatenate([s0_ref[c, 2], s0_ref[c, 3]], axis=1))
            _, vjp = jax.vjp(lambda *t: _chunk_fn(*t, kept=kept)[:2], s0_inv[:, :, :HN],
                             *[_split_pairs(ref[:, sl, :]) for ref in (r_ref, lw_ref, k_ref, v_ref, kk_ref, a_ref)])
            g = vjp((_split_pairs(do_ref[:, sl, :]), ds_ref[...]))
            ds_ref[...] = g[0]
            for ref, val in zip((dr, dlw, dk, dv, dkk, da), g[1:]):
                ref[:, sl, :] = _join_pairs(val)
            return carry

        lax.fori_loop(0, n_chunks, step, 0)

        for at, action in plan[1:]:
            pl.when(pl.program_id(0) == at)(functools.partial(action, x_in, x_out, sems))

    hm = pl.BlockSpec((NP, tb, PW), lambda i: (0, nb - 1 - i, 0))
    res = pl.pallas_call(
        body, name="rwkv_scan_bwd", grid=(nb,),
        in_specs=[hm] * 6 + [pl.BlockSpec((n_chunks, N_KEPT, NH, HN, PW), lambda i: (nb - 1 - i, 0, 0, 0, 0)), hm]
        + (ex.any_specs if ex else []),
        out_specs=[hm] * 6 + (ex.any_specs if ex else []),
        out_shape=[jax.ShapeDtypeStruct((NP, T, PW), F32)] * 6 + (ex.out_shape if ex else []),
        scratch_shapes=[pltpu.VMEM((NH, HN, HN), F32)] + (ex.sem_shapes if ex else []),
        compiler_params=pltpu.CompilerParams(dimension_semantics=("arbitrary",), vmem_limit_bytes=VMEM_LIMIT),
    )(r, lw, k, v, kk, a, s0s, do, *(ex.bufs if ex else []))
    return list(res[:6]), list(res[6:])


def _shift_down(i, p, prev8):
    first = jnp.where(i > 0, prev8[7:8, :], 0.0)
    row = lax.broadcasted_iota(jnp.int32, p.shape, 0)
    return jnp.where(row == 0, first, pltpu.roll(p, 1, axis=0))


def _mix_bwd(dq, p, sb, tm=256):
    def fn(i, n, dq, next8, p, prev8, sb):
        ps = _shift_down(i, p, prev8)
        d1 = dq * sb[1:2]
        last = jnp.where(i < n - 1, next8[0:1, :] * sb[1:2], 0.0)
        row = lax.broadcasted_iota(jnp.int32, dq.shape, 0)
        up = jnp.where(row == dq.shape[0] - 1, last, pltpu.roll(d1, dq.shape[0] - 1, axis=0))
        return (dq * sb[0:1] + up, jnp.sum(dq * p, axis=0, keepdims=True), jnp.sum(dq * ps, axis=0, keepdims=True))
    w = p.shape[1]
    return _rows_call("shift_mix_bwd", fn, [Rows(dq), Halo(dq, +1), Rows(p), Halo(p, -1)], [sb], [("rows", w, BF16)],
                      accs=[((1, w), F32), ((1, w), F32)], tm=tm, with_pid=True)


def _local_step(x, target, W, late_weights=None, early_grads=None, w_in_grads_ready=None, ffn_grads_ready=None):
    G = {}
    a = _rows_call("norm_mix_fwd", lambda x, g: (_rms(x, g),), [Rows(x)], [W["g_mix"]], [("rows", D, BF16)])[0]
    p_sgu = _matmul("proj_sgu", a, W["w_sgu_t"], "nt")
    def token_shift(p, sb0, sb1):
        row = lax.broadcasted_iota(jnp.int32, p.shape, 0)
        return p, p * sb0 + jnp.where(row == 0, 0.0, pltpu.roll(p, 1, axis=0)) * sb1
    p_rw, q = _matmul("proj_rwkv", a, W["w_rw_t"], "nt", tn=512, whole_rows=True, epilogue=token_shift,
                      extras=[W["sb"][0:1], W["sb"][1:2]], out_dtypes=(F32, F32))
    p_gate = _matmul("proj_gate", a, W["w_gate_t"], "nt")

    sgu_consts = [W["sgu_ln_w"], W["sgu_ln_b"], W["sgu_w"], W["sgu_bt"]]
    s = _rows_call("sgu_fwd", lambda *t: (_sgu_fn(*t),), [Rows(p_sgu)], sgu_consts, [("rows", D, BF16)], tm=SGU_C)[0]

    q_ins = [Rows(q, D, 0), Rows(q, D, 1), Rows(q, D, 2), Rows(q, 128, 24), Rows(q, 128, 25), Rows(q, 256, 13)]
    pre_consts = [W["w_lora"], W["w0"], W["a_lora"], W["a0"], W["g_lora"], W["k_k"], W["k_a"]]
    r_h, lw_h, k_h, v_h, kk_h, a_h, g_gate = _rows_call(
        "rwkv_pre_fwd", _pre_fn, q_ins, pre_consts, [("heads", F32)] * 6 + [("rows", D, F32)], tm=128)
    o_h, s0s, got = _scan_fwd(r_h, lw_h, k_h, v_h, kk_h, a_h, ex=late_weights[0] if late_weights else None)
    if late_weights:
        W = {**W, **late_weights[1](got)}
    y_a = _matmul("proj_a", s, W["w_proj_a"], "nn")
    post_ins = [Heads(o_h), Heads(r_h), Heads(k_h), Heads(v_h), Rows(g_gate)]
    post_consts = [W[n].reshape(NP, 1, PW) for n in ("ln_x_w", "ln_x_b", "r_k")]
    z_b = _rows_call("rwkv_post_fwd", lambda *t: (_post_fn(*t),), post_ins, post_consts, [("rows", D, BF16)], tm=128)[0]
    y_b, mixed = _matmul("proj_b", z_b, W["w_proj_b"], "nn", extras=[(p_gate, 0), (p_gate, 1), y_a],
                         epilogue=lambda yb, ga, gb, ya: (yb, _sigmoid(ga) * ya + _sigmoid(gb) * yb),
                         out_dtypes=(F32, BF16))

    def res1(mo, x, g):
        h1 = x + mo
        return h1, _rms(h1, g)
    h1, f = _matmul("proj_out", mixed, W["w_out"], "nn", extras=[x, W["g_ffn"]], epilogue=res1,
                    out_dtypes=(F32, BF16))

    def relu_sq(u):
        r = jnp.maximum(u, 0.0)
        return r, r * r
    r1, act = _matmul("ffn_up", f, W["w_ffn1"], "nn", epilogue=relu_sq, out_dtypes=(BF16, BF16))
    ff = _matmul("ffn_down", act, W["w_ffn2"], "nn")

    def head(h1, ff, tgt, g):
        def f_(h1, ff, g):
            y = _rms(h1 + ff, g)
            return 0.5 * jnp.sum(jnp.mean(jnp.square(y - tgt), axis=-1))
        loss, (dh2, _, dg) = jax.value_and_grad(f_, argnums=(0, 1, 2))(h1, ff, g)
        return dh2, jnp.full((8, LANES), loss, F32), dg
    dh2, loss_acc, G["g_final"] = _rows_call("loss_head", head, [Rows(h1), Rows(ff), Rows(target)], [W["g_final"]],
                                             [("rows", D, F32)], accs=[((8, LANES), F32), ((1, D), F32)])

    d_u1 = _matmul("ffn_down_dx", dh2, W["w_ffn2"], "nt", extras=[r1], out_dtypes=(BF16,),
                   epilogue=lambda d_act, r: (d_act * 2.0 * r.astype(F32),))[0]
    G["w_ffn2"] = _matmul("ffn_down_dw", act, dh2, "tn", out_dtype=GRAD_PAYLOAD)
    d_f = _matmul("ffn_up_dx", d_u1, W["w_ffn1"], "nt")
    G["w_ffn1"] = _matmul("ffn_up_dw", f, d_u1, "tn", out_blocks=N_DEV, out_dtype=GRAD_PAYLOAD)

    def res1_bwd(h1, d_f, dh2, g):
        _, vjp = jax.vjp(_rms, h1, g)
        dh, dg = vjp(d_f)
        return dh2 + dh, dg
    dh1, G["g_ffn"] = _rows_call("residual_norm_bwd", res1_bwd, [Rows(h1), Rows(d_f), Rows(dh2)],
                                 [W["g_ffn"]], [("rows", D, F32)], accs=[((1, D), F32)])
    ffn_token = ffn_grads_ready(G) if ffn_grads_ready else None
    def gate_bwd(d_mixed, ga, gb, ya, yb):
        _, vjp = jax.vjp(_gate_fn, jnp.concatenate([ga, gb], axis=-1), ya, yb)
        return vjp(d_mixed)
    d_gate, d_ya, d_yb = _matmul("proj_out_dx", dh1, W["w_out"], "nt", after=ffn_token, epilogue=gate_bwd,
                                 extras=[(p_gate, 0), (p_gate, 1), y_a, y_b], out_dtypes=(BF16, BF16, BF16),
                                 out_widths=(2 * D, D, D))
    G["w_out"] = _matmul("proj_out_dw", mixed, dh1, "tn", out_dtype=GRAD_PAYLOAD)

    d_s = _matmul("proj_a_dx", d_ya, W["w_proj_a"], "nt")
    G["w_proj_a"] = _matmul("proj_a_dw", s, d_ya, "tn", out_dtype=GRAD_PAYLOAD)

    def sgu_bwd(p, ds, *c):
        _, vjp = jax.vjp(_sgu_fn, p, *c)
        return vjp(ds)
    d_p_sgu, G["sgu_ln_w"], G["sgu_ln_b"], G["sgu_w"], G["sgu_bt"] = _rows_call(
        "sgu_bwd", sgu_bwd, [Rows(p_sgu), Rows(d_s)], sgu_consts, [("rows", 2 * D, BF16)],
        accs=[((1, D), F32), ((1, D), F32), ((SGU_G, SGU_C, SGU_C), F32), ((SGU_C, SGU_G), F32)], tm=SGU_C)

    d_zb = _matmul("proj_b_dx", d_yb, W["w_proj_b"], "nt")
    G["w_proj_b"] = _matmul("proj_b_dw", z_b, d_yb, "tn", out_dtype=GRAD_PAYLOAD)

    def post_bwd(o, r, k2, v, g, dz, *c):
        _, vjp = jax.vjp(_post_fn, o, r, k2, v, g, *c)
        return vjp(dz)
    do_h, dr1, dk1, dv1, d_g, g_lnw, g_lnb, g_rk = _rows_call(
        "rwkv_post_bwd", post_bwd, post_ins + [Rows(d_zb)], post_consts, [("heads", F32)] * 4 + [("rows", D, F32)],
        accs=[((NP, 1, PW), F32)] * 3, tm=128)
    G["ln_x_w"], G["ln_x_b"], G["r_k"] = (t.reshape(1, D) for t in (g_lnw, g_lnb, g_rk))
    (dr2, dlw, dk2, dv2, dkk, daa), early = _scan_bwd(r_h, lw_h, k_h, v_h, kk_h, a_h, s0s, do_h,
                                                      ex=early_grads(G) if early_grads else None)

    def pre_bwd(qr, qk, qv, qxw, qxa, qxg, dr1, dr2, dlw, dk1, dk2, dv1, dv2, dkk, daa, dg, *c):
        _, vjp = jax.vjp(_pre_fn, qr, qk, qv, qxw, qxa, qxg, *c)
        g = vjp((dr1 + dr2, dlw, dk1 + dk2, dv1 + dv2, dkk, daa, dg))
        dq = jnp.concatenate(g[:6], axis=-1)
        return (dq,) + tuple(g[6:])
    pre_b_ins = q_ins + [Heads(dr1), Heads(dr2), Heads(dlw), Heads(dk1), Heads(dk2), Heads(dv1), Heads(dv2),
                         Heads(dkk), Heads(daa), Rows(d_g)]
    d_q, G["w_lora"], G["w0"], G["a_lora"], G["a0"], G["g_lora"], G["k_k"], G["k_a"] = _rows_call(
        "rwkv_pre_bwd", pre_bwd, pre_b_ins, pre_consts, [("rows", RW_INT, F32)],
        accs=[((128, D), F32), ((1, D), F32), ((128, D), F32), ((1, D), F32), ((256, D), F32), ((1, D), F32),
              ((1, D), F32)], tm=128)
    d_p_rw, dsb0, dsb1 = _mix_bwd(d_q, p_rw, W["sb"])
    G["sb"] = jnp.concatenate([dsb0, dsb1], axis=0)

    G["w_sgu_t"] = _matmul("proj_sgu_dw", d_p_sgu, a, "tn", out_dtype=GRAD_PAYLOAD)
    G["w_rw_t"] = _matmul("proj_rwkv_dw", d_p_rw, a, "tn", out_dtype=GRAD_PAYLOAD)
    G["w_gate_t"] = _matmul("proj_gate_dw", d_gate, a, "tn", out_dtype=GRAD_PAYLOAD)
    token = w_in_grads_ready(G) if w_in_grads_ready else None
    da1 = _matmul("proj_sgu_dx", d_p_sgu, W["w_sgu_t"], "nn", after=token)
    da2 = _matmul("proj_rwkv_dx", d_p_rw, W["w_rw_t"], "nn", after=token)
    da3 = _matmul("proj_gate_dx", d_gate, W["w_gate_t"], "nn", after=token)

    def norm1_bwd(x, da1, da2, da3, dh1, g):
        _, vjp = jax.vjp(_rms, x, g)
        dx, dg = vjp(da1 + da2 + da3)
        return dh1 + dx, dg
    dx, G["g_mix"] = _rows_call("norm_mix_bwd", norm1_bwd, [Rows(x), Rows(da1), Rows(da2), Rows(da3), Rows(dh1)],
                                [W["g_mix"]], [("rows", D, F32)], accs=[((1, D), F32)])
    return loss_acc[0, 0], dx, G, early


class Exchange:
    def __init__(self, bufs, gathers):
        self.bufs, self.gathers, self.nb = list(bufs), list(gathers), len(bufs)
        self.any_specs = [pl.BlockSpec(memory_space=pl.ANY)] * self.nb
        self.out_shape = [jax.ShapeDtypeStruct((N_DEV,) + (b.shape if g else b.shape[1:]), b.dtype)
                          for b, g in zip(self.bufs, self.gathers)]
        n = (N_DEV - 1) * self.nb
        self.sem_shapes = [pltpu.SemaphoreType.DMA((n,)), pltpu.SemaphoreType.DMA((n,)),
                           pltpu.SemaphoreType.DMA((self.nb,))]

    def _copies(self, in_refs, out_refs, sems):
        send_sems, recv_sems, local_sems = sems
        x, y, c = lax.axis_index("x"), lax.axis_index("y"), lax.axis_index("c")
        me = 4 * x + 2 * y + c

        def src(b, dest):
            return in_refs[b] if self.gathers[b] else in_refs[b].at[dest]

        local = [pltpu.make_async_copy(src(b, me), out_refs[b].at[me], local_sems.at[b]) for b in range(self.nb)]
        sends, recvs = [], []
        for kbits in range(1, N_DEV):
            px = 1 - x if kbits & 4 else x
            py = 1 - y if kbits & 2 else y
            pc = 1 - c if kbits & 1 else c
            peer = 4 * px + 2 * py + pc
            for b in range(self.nb):
                s = (kbits - 1) * self.nb + b
                sends.append(pltpu.make_async_remote_copy(
                    src_ref=src(b, peer), dst_ref=out_refs[b].at[me], send_sem=send_sems.at[s],
                    recv_sem=recv_sems.at[s], device_id=(px, py, pc), device_id_type=pl.DeviceIdType.MESH))
                recvs.append(pltpu.make_async_remote_copy(
                    src_ref=src(b, peer), dst_ref=out_refs[b].at[peer], send_sem=send_sems.at[s],
                    recv_sem=recv_sems.at[s], device_id=(px, py, pc), device_id_type=pl.DeviceIdType.MESH))
        return local, sends, recvs

    def start(self, in_refs, out_refs, sems):
        local, sends, _ = self._copies(in_refs, out_refs, sems)
        for cp in sends + local:
            cp.start()

    def wait(self, in_refs, out_refs, sems):
        local, sends, recvs = self._copies(in_refs, out_refs, sems)
        for cp in recvs:
            cp.wait_recv()
        for cp in sends:
            cp.wait_send()
        for cp in local:
            cp.wait()

    def schedule(self, n_steps):
        return [(0, self.start), (n_steps - 1, self.wait)]


def _exchange(name, bufs, gather):
    ex = Exchange(bufs, gather if isinstance(gather, (list, tuple)) else [gather] * len(bufs))

    def body(*refs):
        in_refs, out_refs, sems = refs[:ex.nb], refs[ex.nb:2 * ex.nb], refs[2 * ex.nb:]
        ex.start(in_refs, out_refs, sems)
        ex.wait(in_refs, out_refs, sems)

    return pl.pallas_call(body, name=name, in_specs=ex.any_specs, out_specs=ex.any_specs, out_shape=ex.out_shape,
                          scratch_shapes=ex.sem_shapes)(*ex.bufs)


N_CHIP = 4


def _pair_exchange(name, blocks):
    def body(b_ref, got_ref, send_sems, recv_sems):
        x, y, c = lax.axis_index("x"), lax.axis_index("y"), lax.axis_index("c")
        copies = [pltpu.make_async_remote_copy(
            src_ref=b_ref.at[2 * q + 1 - c], dst_ref=got_ref.at[q], send_sem=send_sems.at[q],
            recv_sem=recv_sems.at[q], device_id=(x, y, 1 - c), device_id_type=pl.DeviceIdType.MESH)
            for q in range(N_CHIP)]
        for cp in copies:
            cp.start()
        for cp in copies:
            cp.wait_recv()
        for cp in copies:
            cp.wait_send()

    any_spec = pl.BlockSpec(memory_space=pl.ANY)
    return pl.pallas_call(
        body, name=name, in_specs=[any_spec], out_specs=any_spec,
        out_shape=jax.ShapeDtypeStruct((N_CHIP,) + blocks.shape[1:], blocks.dtype),
        scratch_shapes=[pltpu.SemaphoreType.DMA((N_CHIP,))] * 2,
    )(blocks)


def _pair_sum(name, blocks, got, core):
    _, R, Wd = blocks.shape

    def body(c_ref, a_ref, b_ref, o_ref):
        o_ref[...] = (a_ref[...].astype(F32) + b_ref[...].astype(F32)).astype(o_ref.dtype)

    return pl.pallas_call(
        body, name=name,
        grid_spec=pltpu.PrefetchScalarGridSpec(
            num_scalar_prefetch=1, grid=(N_CHIP,),
            in_specs=[pl.BlockSpec((None, R, Wd), lambda q, c_ref: (2 * q + c_ref[0], 0, 0)),
                      pl.BlockSpec((None, R, Wd), lambda q, c_ref: (q, 0, 0))],
            out_specs=pl.BlockSpec((None, R, Wd), lambda q, c_ref: (q, 0, 0))),
        out_shape=jax.ShapeDtypeStruct(got.shape, blocks.dtype),
        compiler_params=pltpu.CompilerParams(dimension_semantics=("parallel",), vmem_limit_bytes=VMEM_LIMIT),
    )(core, blocks, got)


def _chip_copies(s_ref, land_ref, send_sems, recv_sems):
    x, y, c = lax.axis_index("x"), lax.axis_index("y"), lax.axis_index("c")
    my_q = 2 * x + y
    sends, recvs = [], []
    for kbits in range(1, N_CHIP):
        px = 1 - x if kbits & 2 else x
        py = 1 - y if kbits & 1 else y
        peer_q = 2 * px + py
        sends.append(pltpu.make_async_remote_copy(
            src_ref=s_ref.at[peer_q], dst_ref=land_ref.at[my_q], send_sem=send_sems[kbits - 1],
            recv_sem=recv_sems[kbits - 1], device_id=(px, py, c), device_id_type=pl.DeviceIdType.MESH))
        recvs.append(pltpu.make_async_remote_copy(
            src_ref=s_ref.at[peer_q], dst_ref=land_ref.at[peer_q], send_sem=send_sems[kbits - 1],
            recv_sem=recv_sems[kbits - 1], device_id=(px, py, c), device_id_type=pl.DeviceIdType.MESH))
    return sends, recvs


_HBM = pl.BlockSpec(memory_space=pltpu.HBM)
_SEM = pl.BlockSpec(memory_space=pltpu.SEMAPHORE)
N_CHIP_SEMS = 2 * (N_CHIP - 1)


def _scatter_copies(b_refs, land_refs, send_sems, recv_sems):
    x, y, c = lax.axis_index("x"), lax.axis_index("y"), lax.axis_index("c")
    me = 4 * x + 2 * y + c
    sends, recvs = [], []
    for kbits in range(1, N_DEV):
        px = 1 - x if kbits & 4 else x
        py = 1 - y if kbits & 2 else y
        pc = 1 - c if kbits & 1 else c
        peer = 4 * px + 2 * py + pc
        for b in range(len(b_refs)):
            s = (kbits - 1) * len(b_refs) + b
            sends.append(pltpu.make_async_remote_copy(
                src_ref=b_refs[b].at[peer], dst_ref=land_refs[b].at[me], send_sem=send_sems[s],
                recv_sem=recv_sems[s], device_id=(px, py, pc), device_id_type=pl.DeviceIdType.MESH))
            recvs.append(pltpu.make_async_remote_copy(
                src_ref=b_refs[b].at[peer], dst_ref=land_refs[b].at[peer], send_sem=send_sems[s],
                recv_sem=recv_sems[s], device_id=(px, py, pc), device_id_type=pl.DeviceIdType.MESH))
    return sends, recvs


def _scatter_start(name, bufs):
    nb = len(bufs)
    n = (N_DEV - 1) * nb

    def body(*refs):
        b_refs, land_refs, outs = refs[:nb], refs[nb:2 * nb], refs[2 * nb:]
        sends, _ = _scatter_copies(b_refs, land_refs, outs[:n], outs[n:2 * n])
        for cp in sends:
            cp.start()
        token = outs[2 * n + 2 * nb]
        token[...] = jnp.zeros_like(token)

    thru = tuple(pltpu.HBM(b.shape, b.dtype) for b in bufs)
    res = pl.pallas_call(
        body, name=name, in_specs=(_HBM,) * (2 * nb),
        out_specs=(_SEM,) * (2 * n) + (_HBM,) * (2 * nb) + (pl.BlockSpec(memory_space=pltpu.VMEM),),
        out_shape=(pltpu.SemaphoreType.DMA(()),) * (2 * n) + thru + thru + (jax.ShapeDtypeStruct((8, LANES), F32),),
        input_output_aliases={i: 2 * n + i for i in range(2 * nb)},
        compiler_params=pltpu.CompilerParams(has_side_effects=pltpu.SideEffectType.DATAFLOW_SIDE_EFFECTING),
    )(*[pltpu.with_memory_space_constraint(b, pltpu.HBM) for b in bufs],
      *[pltpu.with_memory_space_constraint(lax.empty(b.shape, b.dtype), pltpu.HBM) for b in bufs])
    return res[:2 * n], list(res[2 * n:2 * n + nb]), list(res[2 * n + nb:2 * n + 2 * nb]), res[2 * n + 2 * nb]


def _scatter_wait(name, sems, bufs_thru, lands_thru, after):
    nb = len(bufs_thru)
    n = (N_DEV - 1) * nb

    def body(*refs):
        b_refs, land_refs, sem_refs = refs[:nb], refs[nb:2 * nb], refs[2 * nb:2 * nb + 2 * n]
        sends, recvs = _scatter_copies(b_refs, land_refs, sem_refs[:n], sem_refs[n:])
        for cp in sends:
            cp.wait_send()
        for cp in recvs:
            cp.wait_recv()

    thru = tuple(pltpu.HBM(b.shape, b.dtype) for b in bufs_thru)
    res = pl.pallas_call(
        body, name=name, in_specs=(_HBM,) * (2 * nb) + (_SEM,) * (2 * n) + (pl.BlockSpec(memory_space=pl.ANY),),
        out_specs=(_HBM,) * (2 * nb), out_shape=thru + thru,
        input_output_aliases={i: i for i in range(2 * nb)},
        compiler_params=pltpu.CompilerParams(has_side_effects=pltpu.SideEffectType.DATAFLOW_SIDE_EFFECTING),
    )(*bufs_thru, *lands_thru, *sems, after)
    return list(res[:nb]), list(res[nb:])


def _chip_exchange_start(name, sums):
    def body(s_ref, land_ref, *outs):
        sems, token = outs[:N_CHIP_SEMS], outs[N_CHIP_SEMS + 2]
        sends, _ = _chip_copies(s_ref, land_ref, sems[:N_CHIP - 1], sems[N_CHIP - 1:])
        for cp in sends:
            cp.start()
        token[...] = jnp.zeros_like(token)

    res = pl.pallas_call(
        body, name=name, in_specs=(_HBM, _HBM),
        out_specs=(_SEM,) * N_CHIP_SEMS + (_HBM, _HBM, pl.BlockSpec(memory_space=pltpu.VMEM)),
        out_shape=(pltpu.SemaphoreType.DMA(()),) * N_CHIP_SEMS
        + (pltpu.HBM(sums.shape, sums.dtype), pltpu.HBM(sums.shape, sums.dtype), jax.ShapeDtypeStruct((8, LANES), F32)),
        input_output_aliases={0: N_CHIP_SEMS, 1: N_CHIP_SEMS + 1},
        compiler_params=pltpu.CompilerParams(has_side_effects=pltpu.SideEffectType.DATAFLOW_SIDE_EFFECTING),
    )(pltpu.with_memory_space_constraint(sums, pltpu.HBM),
      pltpu.with_memory_space_constraint(lax.empty(sums.shape, sums.dtype), pltpu.HBM))
    return res[:N_CHIP_SEMS], res[N_CHIP_SEMS], res[N_CHIP_SEMS + 1], res[N_CHIP_SEMS + 2]


def _chip_exchange_wait(name, sems, sums_thru, land_thru, after):
    def body(s_ref, land_ref, *rest):
        sems = rest[:N_CHIP_SEMS]
        sends, recvs = _chip_copies(s_ref, land_ref, sems[:N_CHIP - 1], sems[N_CHIP - 1:])
        for cp in sends:
            cp.wait_send()
        for cp in recvs:
            cp.wait_recv()

    return pl.pallas_call(
        body, name=name, in_specs=(_HBM, _HBM) + (_SEM,) * N_CHIP_SEMS + (pl.BlockSpec(memory_space=pl.ANY),),
        out_specs=(_HBM, _HBM),
        out_shape=(pltpu.HBM(sums_thru.shape, sums_thru.dtype), pltpu.HBM(sums_thru.shape, sums_thru.dtype)),
        input_output_aliases={0: 0, 1: 1},
        compiler_params=pltpu.CompilerParams(has_side_effects=pltpu.SideEffectType.DATAFLOW_SIDE_EFFECTING),
    )(sums_thru, land_thru, *sems, after)


def _adam_update(g, w, m, v):
    m_new = ADAM_B1 * m + (1.0 - ADAM_B1) * g
    v_new = ADAM_B2 * v + (1.0 - ADAM_B2) * jnp.square(g)
    m_hat = m_new / (1.0 - ADAM_B1 ** ADAM_STEP)
    v_hat = v_new / (1.0 - ADAM_B2 ** ADAM_STEP)
    return -ADAM_LR * (m_hat / (jnp.sqrt(v_hat) + ADAM_EPS) + ADAM_WD * w), m_new, v_new


def _adamw_rows(name, slots, parts):
    n_parts, rows = len(parts), [p[0].shape[0] for p in parts]
    rest = slots.shape[1] - sum(rows)

    def total(s_ref, at, r):
        g = s_ref[0, pl.ds(at, r), :]
        for j in range(1, slots.shape[0]):
            g = g + s_ref[j, pl.ds(at, r), :]
        return g

    def body(s_ref, *refs):
        ins, outs = refs[:3 * n_parts], refs[3 * n_parts:]
        at = 0
        for i, r in enumerate(rows):
            g = total(s_ref, at, r)
            w_ref, m_ref, v_ref = ins[3 * i:3 * i + 3]
            g_out, d_out, m_out, v_out = outs[4 * i:4 * i + 4]
            g_out[...] = g
            d_out[...], m_out[...], v_out[...] = _adam_update(g, w_ref[...], m_ref[...], v_ref[...])
            at += r
        outs[4 * n_parts][...] = total(s_ref, at, rest)

    shapes = [jax.ShapeDtypeStruct(p[0].shape, F32) for p in parts for _ in range(4)]
    res = pl.pallas_call(
        body, name=name, out_shape=shapes + [jax.ShapeDtypeStruct((rest, slots.shape[2]), F32)],
        compiler_params=pltpu.CompilerParams(vmem_limit_bytes=VMEM_LIMIT),
    )(slots, *[t for p in parts for t in p])
    return [res[4 * i:4 * i + 4] for i in range(n_parts)], res[4 * n_parts]


def _adamw_whole(name, items):
    n_items = len(items)

    def body(*refs):
        ins, outs = refs[:4 * n_items], refs[4 * n_items:]
        for i in range(n_items):
            s_ref, w_ref, m_ref, v_ref = ins[4 * i:4 * i + 4]
            g_out, d_out, m_out, v_out = outs[4 * i:4 * i + 4]
            g = s_ref[0].astype(F32)
            for j in range(1, s_ref.shape[0]):
                g = g + s_ref[j].astype(F32)
            g_out[0] = g
            d_out[0], m_out[0], v_out[0] = _adam_update(g, w_ref[0], m_ref[0], v_ref[0])

    res = pl.pallas_call(
        body, name=name, out_shape=[jax.ShapeDtypeStruct(item[1].shape, F32) for item in items for _ in range(4)],
        compiler_params=pltpu.CompilerParams(vmem_limit_bytes=VMEM_LIMIT),
    )(*[t for item in items for t in item])
    return [res[4 * i:4 * i + 4] for i in range(n_items)]


def _adamw(name, slots, w, m, v, tr=256):
    unit_mid = w.ndim == 3 and w.shape[1] == 1 and w.shape[0] > 1
    R, Wd = (w.shape[0], w.shape[2]) if unit_mid else w.shape[-2:]
    depth_axis = w.ndim == 3 and not unit_mid
    if unit_mid:
        tr, tc = 128, Wd
    elif R % tr == 0:
        tc = Wd
    else:
        tr, tc = R, (256 if (Wd % 256 == 0 and R > 256) else Wd)
    at = (slice(None), 0, slice(None)) if unit_mid else Ellipsis

    def body(s_ref, w_ref, m_ref, v_ref, g_out, d_out, m_out, v_out):
        g = s_ref[0].astype(F32)
        for j in range(1, slots.shape[0]):
            g = g + s_ref[j].astype(F32)
        g_out[at] = g
        d_out[at], m_out[at], v_out[at] = _adam_update(g, w_ref[at], m_ref[at], v_ref[at])

    if unit_mid:
        row = pl.BlockSpec((tr, 1, tc), lambda i, j: (i, 0, j))
    elif depth_axis:
        row = pl.BlockSpec((None, tr, tc), lambda i, j: (0, i, j))
    else:
        row = pl.BlockSpec((tr, tc), lambda i, j: (i, j))
    return pl.pallas_call(
        body, name=name, grid=(pl.cdiv(R, tr), Wd // tc),
        in_specs=[pl.BlockSpec((slots.shape[0], tr, tc), lambda i, j: (0, i, j)), row, row, row],
        out_specs=[row] * 4, out_shape=[jax.ShapeDtypeStruct(w.shape, F32)] * 4,
        compiler_params=pltpu.CompilerParams(dimension_semantics=("parallel", "parallel"),
                                             vmem_limit_bytes=VMEM_LIMIT),
    )(slots, w, m, v)


PACK_W = 1024
PACKED = [(n, s) for n, s in REPLICATED if n != "sgu_w"]
_SMALL_SIZES = [int(np.prod(s)) for _, s in PACKED]
_SMALL_ROWS = _round_up(_round_up(sum(_SMALL_SIZES) + PACK_W, PACK_W) // PACK_W, 8)
assert all(size % PACK_W == 0 for size in _SMALL_SIZES)
W_IN_SHARD = P_TOTAL // N_DEV


def _pack_rows(name, parts, rows):
    parts = [p.reshape(-1, PACK_W) for p in parts]
    assert all(p.dtype == F32 for p in parts)

    def body(*refs):
        at = 0
        for ref in refs[:-1]:
            refs[-1][pl.ds(at, ref.shape[0]), :] = ref[...]
            at += ref.shape[0]
        refs[-1][pl.ds(at, rows - at), :] = jnp.zeros((rows - at, PACK_W), F32)

    return pl.pallas_call(body, name=name, out_shape=jax.ShapeDtypeStruct((rows, PACK_W), F32))(*parts)


_W_IN_SEGMENTS = [(2 * D, 0, 0), (3 * D, 1, 0), (L_W, 1, 3 * D), (L_A, 1, 3 * D + 128), (L_G, 1, 3 * D + 256),
                  (2 * D, 2, 0)]
_W_IN_PADS = [(3 * D + L_W, 128 - L_W), (3 * D + 128 + L_A, 128 - L_A), (3 * D + 256 + L_G, 256 - L_G)]
_W_IN_GROUP_ROWS = [2 * D, 3 * D + 128 + 128 + 256, 2 * D]
assert sum(rows for rows, _, _ in _W_IN_SEGMENTS) == P_TOTAL


def _w_in_pieces(j):
    pieces, first = [], 0
    for rows, group, to in _W_IN_SEGMENTS:
        lo, hi = max(first, W_IN_SHARD * j), min(first + rows, W_IN_SHARD * (j + 1))
        if lo < hi:
            pieces.append((lo - W_IN_SHARD * j, hi - lo, group, to + lo - first))
        first += rows
    return pieces


def _w_in_groups_t(blocks):
    def body(x_ref, *o_refs):
        @pl.when(pl.program_id(0) == 0)
        def _():
            for at, rows in _W_IN_PADS:
                o_refs[1][pl.ds(at, rows), :] = jnp.zeros((rows, D), blocks.dtype)
        for j in range(N_DEV):
            @pl.when(pl.program_id(0) == j)
            def _(j=j):
                for at, rows, group, to in _w_in_pieces(j):
                    o_refs[group][pl.ds(to, rows), :] = x_ref[pl.ds(at, rows), :]

    return pl.pallas_call(
        body, name="w_in_groups", grid=(N_DEV,),
        in_specs=[pl.BlockSpec((None, W_IN_SHARD, D), lambda j: (j, 0, 0))],
        out_specs=[pl.BlockSpec((r, D), lambda j: (0, 0)) for r in _W_IN_GROUP_ROWS],
        out_shape=[jax.ShapeDtypeStruct((r, D), blocks.dtype) for r in _W_IN_GROUP_ROWS],
        compiler_params=pltpu.CompilerParams(dimension_semantics=("arbitrary",), vmem_limit_bytes=VMEM_LIMIT),
    )(blocks)


def _w_in_grad_blocks(g_sgu_t, g_rw_t, g_gate_t):
    def body(*refs):
        g_refs, o_ref = refs[:3], refs[3]
        for j in range(N_DEV):
            @pl.when(pl.program_id(0) == j)
            def _(j=j):
                for at, rows, group, to in _w_in_pieces(j):
                    o_ref[pl.ds(at, rows), :] = g_refs[group][pl.ds(to, rows), :]

    return pl.pallas_call(
        body, name="w_in_grad_blocks", grid=(N_DEV,),
        in_specs=[pl.BlockSpec((r, D), lambda j: (0, 0)) for r in _W_IN_GROUP_ROWS],
        out_specs=pl.BlockSpec((None, W_IN_SHARD, D), lambda j: (j, 0, 0)),
        out_shape=jax.ShapeDtypeStruct((N_DEV, W_IN_SHARD, D), g_sgu_t.dtype),
        compiler_params=pltpu.CompilerParams(dimension_semantics=("arbitrary",), vmem_limit_bytes=VMEM_LIMIT),
    )(g_sgu_t, g_rw_t, g_gate_t)


def _mesh_index():
    me = 4 * lax.axis_index("x") + 2 * lax.axis_index("y") + lax.axis_index("c")
    return me.astype(jnp.int32).reshape(1)


def _fill_slot(name, dst, src, idx, src_idx=None):
    R, Wd = dst.shape[1:]
    scalars = [idx] if src_idx is None else [idx, src_idx]
    if src_idx is None:
        src_spec = pl.BlockSpec((R, Wd), lambda i, *s: (0, 0))
    else:
        src_spec = pl.BlockSpec((None, R, Wd), lambda i, *s: (s[1][0], 0, 0))

    def body(*refs):
        src_ref, out_ref = refs[len(scalars) + 1], refs[len(scalars) + 2]
        out_ref[...] = src_ref[...]

    return pl.pallas_call(
        body, name=name,
        grid_spec=pltpu.PrefetchScalarGridSpec(
            num_scalar_prefetch=len(scalars), grid=(1,),
            in_specs=[pl.BlockSpec(memory_space=pl.ANY), src_spec],
            out_specs=pl.BlockSpec((None, R, Wd), lambda i, *s: (s[0][0], 0, 0))),
        out_shape=jax.ShapeDtypeStruct(dst.shape, dst.dtype),
        input_output_aliases={len(scalars): 0},
        compiler_params=pltpu.CompilerParams(vmem_limit_bytes=VMEM_LIMIT),
    )(*scalars, dst, src)


class TwoLevelGather:
    N_COPIES = 8
    PART_ALIGN = 16

    def __init__(self, bufs, skip_own=()):
        self.bufs, self.nb, self.skip_own = list(bufs), len(bufs), tuple(skip_own)
        self.any_specs = [pl.BlockSpec(memory_space=pl.ANY)] * self.nb
        self.out_shape = [jax.ShapeDtypeStruct((N_DEV,) + b.shape, b.dtype) for b in self.bufs]
        n = self.N_COPIES * self.nb
        self.sem_shapes = [pltpu.SemaphoreType.DMA((n,)), pltpu.SemaphoreType.DMA((n,)),
                           pltpu.SemaphoreType.DMA((self.nb,))]

    def _parts(self, b):
        shape = self.bufs[b].shape
        first = shape[0] // 2 // self.PART_ALIGN * self.PART_ALIGN if len(shape) == 2 else 0
        return [(0, first), (first, shape[0] - first)] if first else [None]

    def _copies(self, in_refs, out_refs, sems):
        send_sems, recv_sems, local_sems = sems
        nb = self.nb
        x, y, c = lax.axis_index("x"), lax.axis_index("y"), lax.axis_index("c")
        me, sibling = (x, y, c), (x, y, 1 - c)
        near, far = [(1 - x, y), (x, 1 - y)], (1 - x, 1 - y)

        def slot(b, dev):
            return out_refs[b].at[4 * dev[0] + 2 * dev[1] + dev[2]]

        def copy(b, k, block, to, own=False, rows=None):
            src, dst = in_refs[b] if own else slot(b, block), slot(b, block)
            if rows is not None:
                src, dst = src.at[pl.ds(*rows)], dst.at[pl.ds(*rows)]
            return pltpu.make_async_remote_copy(
                src_ref=src, dst_ref=dst, send_sem=send_sems.at[self.N_COPIES * b + k],
                recv_sem=recv_sems.at[self.N_COPIES * b + k], device_id=to, device_id_type=pl.DeviceIdType.MESH)

        ways = [(j, b) for j in range(2) for b in range(nb) if j < len(self._parts(b))]
        cp = {}
        cp["local"] = [pltpu.make_async_copy(in_refs[b], slot(b, me), local_sems.at[b]) for b in range(nb)
                       if b not in self.skip_own]
        cp["first"] = [copy(b, 0, me, sibling, own=True) for b in range(nb)]
        cp["first"] += [copy(b, 1 + j, me, (*near[j], c), own=True) for j in range(2) for b in range(nb)]
        cp["from_near"] = [copy(b, 1 + j, (*near[j], c), me) for j in range(2) for b in range(nb)]
        cp["near_on"] = [[copy(b, 5 + j, (*near[j], c), sibling)]
                         + ([copy(b, 3 + j, (*near[j], c), (*near[1 - j], c), rows=self._parts(b)[j])]
                            if (j, b) in ways else []) for j in range(2) for b in range(nb)]
        cp["from_far"] = [copy(b, 3 + j, (*far, c), me, rows=self._parts(b)[j]) for j, b in ways]
        cp["far_on"] = [copy(b, 7, (*far, c), sibling) for b in range(nb)]
        cp["from_sibling"] = [copy(b, 0, sibling, me) for b in range(nb)]
        cp["from_sibling"] += [copy(b, 5 + j, (*near[j], 1 - c), me) for j in range(2) for b in range(nb)]
        cp["from_sibling"] += [copy(b, 7, (*far, 1 - c), me) for b in range(nb)]
        return cp

    def start(self, in_refs, out_refs, sems):
        cp = self._copies(in_refs, out_refs, sems)
        for c in cp["first"] + cp["local"]:
            c.start()

    def pass_near(self, in_refs, out_refs, sems):
        cp = self._copies(in_refs, out_refs, sems)
        for arrived, onward in zip(cp["from_near"], cp["near_on"]):
            arrived.wait_recv()
            for c in onward:
                c.start()

    def pass_far(self, in_refs, out_refs, sems):
        cp = self._copies(in_refs, out_refs, sems)
        for c in cp["from_far"]:
            c.wait_recv()
        for c in cp["far_on"]:
            c.start()

    def finish(self, in_refs, out_refs, sems):
        cp = self._copies(in_refs, out_refs, sems)
        for c in cp["from_sibling"]:
            c.wait_recv()
        for c in cp["first"] + sum(cp["near_on"], []) + cp["far_on"]:
            c.wait_send()
        for c in cp["local"]:
            c.wait()

    def schedule(self, n_steps):
        return [(0, self.start), (max(n_steps // 2 - 1, 0), self.pass_near), (max(n_steps - 3, 0), self.pass_far),
                (n_steps - 1, self.finish)]


def _all_gather_two_level(name, bufs, skip_own=()):
    ex = TwoLevelGather(bufs, skip_own)

    def body(*refs):
        args = refs[:ex.nb], refs[ex.nb:2 * ex.nb], refs[2 * ex.nb:]
        for _, action in ex.schedule(1):
            action(*args)

    return pl.pallas_call(body, name=name, in_specs=ex.any_specs, out_specs=ex.any_specs, out_shape=ex.out_shape,
                          scratch_shapes=ex.sem_shapes)(*ex.bufs)


def _cols_from_blocks(blk):
    return jnp.transpose(blk, (1, 0, 2)).reshape(blk.shape[1], -1)


def _cols_to_blocks(g):
    r, c = g.shape
    return jnp.transpose(g.reshape(r, N_DEV, c // N_DEV), (1, 0, 2))


FIRST_WEIGHTS = ["w_in", "shift_b", "w_lora_w", "a_lora_w", "g_lora_w"]
LATE_WEIGHTS = ["w_proj_a", "w_proj_b", "w_out", "w_ffn1", "w_ffn2"]
SCAN_CARRIED = ["w_proj_a", "w_proj_b", "w_out"]
FFN_WEIGHTS = ["w_ffn1", "w_ffn2"]


def _late_weights(shards):
    ex = TwoLevelGather([shards[n][0].astype(BF16) for n in LATE_WEIGHTS])

    def finish(results):
        got = dict(zip(LATE_WEIGHTS, results))
        W = {n: got[n].reshape(-1, D) for n in ("w_proj_a", "w_proj_b", "w_out", "w_ffn2")}
        W["w_ffn1"] = got["w_ffn1"].reshape(N_DEV, D, -1)
        return W
    return ex, finish


def _gather_weights(shards):
    def payload(n):
        if n == "w_in":
            return jnp.transpose(shards[n][0]).astype(BF16)
        return shards[n] if n == "shift_b" else shards[n].astype(BF16)
    payloads = [payload(n) for n in FIRST_WEIGHTS]
    got = dict(zip(FIRST_WEIGHTS, _all_gather_two_level("weight_all_gather", payloads, skip_own=(0,))))
    got["w_in"] = _fill_slot("w_in_own_slot", got["w_in"], payloads[0], _mesh_index())
    W = {}
    W["w_sgu_t"], W["w_rw_t"], W["w_gate_t"] = _w_in_groups_t(got["w_in"])
    z = lambda r, c, dt: jnp.zeros((r, c), dt)
    W["w_lora"] = jnp.concatenate([_cols_from_blocks(got["w_lora_w"][:, 0]).astype(F32), z(128 - L_W, D, F32)], axis=0)
    W["a_lora"] = jnp.concatenate([_cols_from_blocks(got["a_lora_w"][:, 0]).astype(F32), z(128 - L_A, D, F32)], axis=0)
    W["g_lora"] = jnp.concatenate([_cols_from_blocks(got["g_lora_w"][:, 0]).astype(F32), z(256 - L_G, D, F32)], axis=0)
    sb = _cols_from_blocks(got["shift_b"][:, 0])
    W["sb"] = jnp.concatenate([sb[:, :3 * D], sb[:, 3 * D:3 * D + L_W], z(2, 128 - L_W, F32),
                               sb[:, 3 * D + L_W:3 * D + L_W + L_A], z(2, 128 - L_A, F32),
                               sb[:, 3 * D + L_W + L_A:], z(2, 256 - L_G, F32)], axis=1)
    return W


def _replicated_weights(rep):
    W = {n: rep[n] for n in ("g_mix", "sgu_ln_w", "sgu_ln_b", "w0", "a0", "k_k", "k_a", "r_k", "ln_x_w", "ln_x_b",
                             "g_ffn")}
    W["g_final"] = rep["g_final"].reshape(1, D)
    W["sgu_w"] = rep["sgu_w"][0]
    W["sgu_bt"] = jnp.transpose(rep["sgu_b"][0])
    return W


def _late_grad_blocks(G):
    return Exchange([G[n].reshape(N_DEV, -1, D) for n in SCAN_CARRIED]
                    + [G["sgu_w"].reshape(SGU_G * SGU_C, SGU_C).astype(GRAD_PAYLOAD)],
                    [False] * len(SCAN_CARRIED) + [True])


def _first_grad_blocks(G):
    sbg = G["sb"]
    c = 3 * D
    sb = jnp.concatenate([sbg[:, :c], sbg[:, c:c + L_W], sbg[:, c + 128:c + 128 + L_A],
                          sbg[:, c + 256:c + 256 + L_G]], axis=1)
    return {
        "shift_b": _cols_to_blocks(sb),
        "w_lora_w": _cols_to_blocks(G["w_lora"][:L_W]), "a_lora_w": _cols_to_blocks(G["a_lora"][:L_A]),
        "g_lora_w": _cols_to_blocks(G["g_lora"][:L_G]),
    }


def _replicated_grads(G):
    small = {n: G[n] for n in ("g_mix", "sgu_ln_w", "sgu_ln_b", "w0", "a0", "k_k", "k_a", "r_k", "ln_x_w", "ln_x_b",
                               "g_ffn", "g_final")}
    small["sgu_w"] = G["sgu_w"]
    small["sgu_b"] = jnp.transpose(G["sgu_bt"])
    return small


def kernel(x, g_mix, w_in, sgu_ln_w, sgu_ln_b, sgu_w, sgu_b, w_proj_a, shift_b, w_lora_w, w0, a_lora_w, a0, g_lora_w, k_k, k_a, r_k, ln_x_w, ln_x_b, w_proj_b, w_out, g_ffn, w_ffn1, w_ffn2, g_final, loss_target, m_g_mix, m_w_in, m_sgu_ln_w, m_sgu_ln_b, m_sgu_w, m_sgu_b, m_w_proj_a, m_shift_b, m_w_lora_w, m_w0, m_a_lora_w, m_a0, m_g_lora_w, m_k_k, m_k_a, m_r_k, m_ln_x_w, m_ln_x_b, m_w_proj_b, m_w_out, m_g_ffn, m_w_ffn1, m_w_ffn2, m_g_final, v_g_mix, v_w_in, v_sgu_ln_w, v_sgu_ln_b, v_sgu_w, v_sgu_b, v_w_proj_a, v_shift_b, v_w_lora_w, v_w0, v_a_lora_w, v_a0, v_g_lora_w, v_k_k, v_k_a, v_r_k, v_ln_x_w, v_ln_x_b, v_w_proj_b, v_w_out, v_g_ffn, v_w_ffn1, v_w_ffn2, v_g_final):
    env = dict(locals())
    weights = {n: env[n] for n in WEIGHT_ORDER}
    moms = {n: env["m_" + n] for n in WEIGHT_ORDER}
    vars_ = {n: env["v_" + n] for n in WEIGHT_ORDER}

    shards = {n: weights[n] for n, _, _ in SHARDED}
    W = _gather_weights(shards)
    W.update(_replicated_weights({n: weights[n] for n, _ in REPLICATED}))
    in_flight = {}

    def send_w_in_grads(G):
        blocks = _w_in_grad_blocks(G["w_sgu_t"], G["w_rw_t"], G["w_gate_t"])
        got = _pair_exchange("grad_pair_exchange", blocks)
        core = lax.axis_index("c").astype(jnp.int32).reshape(1)
        sums = _pair_sum("grad_pair_sum", blocks, got, core)
        in_flight["sems"], in_flight["sums"], in_flight["land"], token = _chip_exchange_start("grad_chip_start", sums)
        return token

    def send_ffn_grads(G):
        in_flight["ffn"] = _scatter_start("grad_ffn_start", [G["w_ffn1"], G["w_ffn2"].reshape(N_DEV, -1, D)])
        return in_flight["ffn"][3]

    loss_part, dx, G, late_slots = _local_step(x[0], loss_target[0], W, late_weights=_late_weights(shards),
                                               early_grads=_late_grad_blocks, w_in_grads_ready=send_w_in_grads,
                                               ffn_grads_ready=send_ffn_grads)

    slots = dict(zip(SCAN_CARRIED, late_slots))
    me = _mesh_index()
    sent, landed = _scatter_wait("grad_ffn_wait", *in_flight["ffn"][:3], after=dx)
    for i, n in enumerate(FFN_WEIGHTS):
        slots[n] = _fill_slot("grad_own_slot_" + n, landed[i], sent[i], me, src_idx=me)
    blocks = _first_grad_blocks(G)
    small = _replicated_grads(G)
    small_parts = [small[n] for n, _ in PACKED] + [jnp.full((PACK_W,), loss_part, F32)]
    rest = [n for n in FIRST_WEIGHTS if n != "w_in"]
    res = _exchange("grad_exchange", [blocks[n] for n in rest] + [_pack_rows("grad_pack", small_parts, _SMALL_ROWS)],
                    [False] * len(rest) + [True])
    slots.update(zip(rest, res[:-1]))
    small_slots = res[-1]
    sums, chip_slots = _chip_exchange_wait("grad_chip_wait", in_flight["sems"], in_flight["sums"], in_flight["land"],
                                           after=small_slots)
    my_chip = (2 * lax.axis_index("x") + lax.axis_index("y")).astype(jnp.int32).reshape(1)
    slots["w_in"] = _fill_slot("grad_own_slot", chip_slots, sums, my_chip, src_idx=my_chip)

    outs = [dict(), dict(), dict(), dict()]
    for n in ["w_in"] + FFN_WEIGHTS:
        if n == "w_in":
            res = _adamw("adamw_" + n, slots[n], *[jnp.transpose(t, (2, 0, 1)) for t in (weights[n], moms[n], vars_[n])])
            res = [jnp.transpose(t, (1, 2, 0)) for t in res]
        else:
            res = _adamw("adamw_" + n, slots[n], weights[n], moms[n], vars_[n])
        for k in range(4):
            outs[k][n] = res[k]
    for name, group in (("adamw_projections", SCAN_CARRIED), ("adamw_low_rank", rest)):
        group_out = _adamw_whole(name, [(slots[n], weights[n], moms[n], vars_[n]) for n in group])
        for n, res in zip(group, group_out):
            for k in range(4):
                outs[k][n] = res[k]

    sgu_shape = (SGU_G * SGU_C, SGU_C)
    res = _adamw("adamw_sgu_w", late_slots[len(SCAN_CARRIED)], *[t.reshape(sgu_shape) for t in
                                                                  (weights["sgu_w"], moms["sgu_w"], vars_["sgu_w"])])
    for k in range(4):
        outs[k]["sgu_w"] = res[k].reshape(weights["sgu_w"].shape)

    small_out, after_them = _adamw_rows(
        "adamw_replicated", small_slots,
        [[d[n].reshape(-1, PACK_W) for d in (weights, moms, vars_)] for n, _ in PACKED])
    for (n, s), res in zip(PACKED, small_out):
        for k in range(4):
            outs[k][n] = res[k].reshape(s)
    loss = after_them[0, 0]
    return (loss, dx[None], *[outs[0][n] for n in WEIGHT_ORDER], *[outs[1][n] for n in WEIGHT_ORDER],
            *[outs[2][n] for n in WEIGHT_ORDER], *[outs[3][n] for n in WEIGHT_ORDER])
```

```python
import functools
import numpy as np
import jax
import jax.numpy as jnp
from jax import lax
from jax.experimental import pallas as pl
from jax.experimental.pallas import tpu as pltpu

F32 = jnp.float32
BF16 = jnp.bfloat16

D = 1024
NH, HN = 16, 64
NP, PW = NH // 2, 2 * HN
SGU_G, SGU_C = 8, 128
L_W, L_A, L_G = 64, 64, 160
C_B = 3 * D + L_W + L_A + L_G
P_TOTAL = 2 * D + C_B + 2 * D
D_FF = 4 * D
RW_INT = 3 * D + 128 + 128 + 256
NORM_EPS, LN_EPS, GN_EPS = 1e-6, 1e-5, 64e-5
N_DEV = 8
LANES = 128
SCAN_C = 64
N_KEPT = 4
SOLVE_B = 16
SCAN_PRECISION = lax.Precision.HIGH
SCAN_OUT_PRECISION = lax.Precision.DEFAULT
GRAD_PAYLOAD = BF16
VMEM_LIMIT = 56 * 1024 * 1024
MATMUL_VMEM_BUDGET = 40 * 1024 * 1024
STEP_COST_BYTES = 512 * 1024
HBM_COST_RATIO = 3
ACC_PASS_WEIGHT = 4

ADAM_LR, ADAM_B1, ADAM_B2, ADAM_EPS, ADAM_WD, ADAM_STEP = 0.001, 0.9, 0.999, 1e-08, 0.01, 10

SHARDED = [
    ("w_in", (D, P_TOTAL), 1), ("w_proj_a", (D, D), 0), ("shift_b", (2, C_B), 1), ("w_lora_w", (L_W, D), 1),
    ("a_lora_w", (L_A, D), 1), ("g_lora_w", (L_G, D), 1), ("w_proj_b", (D, D), 0), ("w_out", (D, D), 0),
    ("w_ffn1", (D, D_FF), 1), ("w_ffn2", (D_FF, D), 0),
]
REPLICATED = [
    ("g_mix", (1, D)), ("sgu_ln_w", (1, D)), ("sgu_ln_b", (1, D)), ("sgu_w", (1, SGU_G, SGU_C, SGU_C)),
    ("sgu_b", (1, SGU_G, SGU_C)), ("w0", (1, D)), ("a0", (1, D)), ("k_k", (1, D)), ("k_a", (1, D)), ("r_k", (1, D)),
    ("ln_x_w", (1, D)), ("ln_x_b", (1, D)), ("g_ffn", (1, D)), ("g_final", (D,)),
]
WEIGHT_ORDER = ["g_mix", "w_in", "sgu_ln_w", "sgu_ln_b", "sgu_w", "sgu_b", "w_proj_a", "shift_b", "w_lora_w", "w0",
                "a_lora_w", "a0", "g_lora_w", "k_k", "k_a", "r_k", "ln_x_w", "ln_x_b", "w_proj_b", "w_out", "g_ffn",
                "w_ffn1", "w_ffn2", "g_final"]


def _round_up(n, m):
    return (n + m - 1) // m * m


def _pick(n, target):
    if n <= target:
        return n
    best = None
    for t in range(LANES, target + 1, LANES):
        if n % t == 0:
            best = t
    assert best is not None, (n, target)
    return best


def _matmul(name, a, b, mode, out_dtype=F32, tm=2048, tn=1024, tk=4096, out_blocks=None, epilogue=None, extras=(),
            out_dtypes=(), after=None, whole_rows=False, out_widths=None):
    b_blocks = b.shape[0] if b.ndim == 3 else None
    bshape = b.shape if b.ndim == 2 else (b.shape[1], b.shape[0] * b.shape[2])
    if mode == "nn":
        (M, K), (K2, N) = a.shape, bshape
    elif mode == "nt":
        (M, K), (N, K2) = a.shape, bshape
    else:
        (K, M), (K2, N) = a.shape, bshape
    assert K == K2, (name, a.shape, b.shape)
    assert b_blocks is None or mode != "tn"
    assert out_blocks is None or mode == "tn"
    tn = min(tn, N // (out_blocks or 1), bshape[1] // b_blocks if (b_blocks and mode == "nn") else tn)
    blocked_k = bool(b_blocks) and mode == "nt"
    tm, tn, tk = _pick(M, tm), _pick(N, tn), (K if blocked_k else _pick(K, tk))

    def vmem_bytes(tm, tk):
        tiles = tm * tk * a.dtype.itemsize + tk * tn * b.dtype.itemsize
        for i, dt in enumerate(out_dtypes if epilogue else (out_dtype,)):
            tiles += tm * (out_widths[i] if out_widths else tn) * jnp.dtype(dt).itemsize
        for x in extras:
            arr = x[0] if isinstance(x, tuple) else x
            tiles += (tm if arr.shape[0] > 1 else 1) * tn * arr.dtype.itemsize
        return 2 * tiles + (tm * tn * 4 if K // tk > 1 else 0)

    def cost(tm, tk):
        ni, nj, nk = M // tm, N // tn, K // tk
        steps = ni * nj * nk
        acc_passes = steps * tm * tn * 8 * ACC_PASS_WEIGHT if nk > 1 else 0
        a_reads = M * K * a.dtype.itemsize * (nj if nk > 1 else 1)
        b_reads = K * N * b.dtype.itemsize * (ni if (nj > 1 or nk > 1) else 1)
        return (steps * STEP_COST_BYTES + acc_passes + vmem_bytes(tm, tk) // 2
                + HBM_COST_RATIO * (a_reads + b_reads))

    options = [(m, k) for m in ({M} if whole_rows else {_pick(M, max(t, LANES)) for t in (tm, tm // 2, tm // 4)})
               for k in ({K} if blocked_k else {_pick(K, max(t, LANES)) for t in (tk, tk // 2, tk // 4)})
               if vmem_bytes(m, k) <= MATMUL_VMEM_BUDGET]
    tm, tk = min(options, key=lambda o: cost(*o))
    nk = K // tk
    dims = {"nn": (((1,), (0,)), ((), ())), "nt": (((1,), (1,)), ((), ())), "tn": (((0,), (0,)), ((), ()))}[mode]

    n_x, n_o = len(extras), len(out_dtypes) if epilogue else 1
    n_after = 0 if after is None else 1

    def body(a_ref, b_ref, *rest):
        x_refs, o_refs, acc = rest[:n_x], rest[n_x + n_after:n_x + n_after + n_o], rest[n_x + n_after + n_o:]
        if blocked_k:
            bw = b.shape[2]
            part = sum(lax.dot_general(a_ref[:, blk * bw:(blk + 1) * bw].astype(BF16), b_ref[blk].astype(BF16), dims,
                                       preferred_element_type=F32) for blk in range(b_blocks))
        else:
            part = lax.dot_general(a_ref[...].astype(BF16), b_ref[...].astype(BF16), dims, preferred_element_type=F32)

        def finish(res):
            outs = epilogue(res, *[r[...] for r in x_refs]) if epilogue else (res,)
            for r, v in zip(o_refs, outs):
                r[...] = v.astype(r.dtype)

        if nk == 1:
            finish(part)
            return
        acc_ref, k = acc[0], pl.program_id(2)

        @pl.when(k == 0)
        def _():
            acc_ref[...] = part

        @pl.when(k > 0)
        def _():
            acc_ref[...] += part

        @pl.when(k == nk - 1)
        def _():
            finish(acc_ref[...])

    a_spec = {"nn": pl.BlockSpec((tm, tk), lambda i, j, k: (i, k)), "nt": pl.BlockSpec((tm, tk), lambda i, j, k: (i, k)),
              "tn": pl.BlockSpec((tk, tm), lambda i, j, k: (k, i))}[mode]
    b_spec = {"nn": pl.BlockSpec((tk, tn), lambda i, j, k: (k, j)), "nt": pl.BlockSpec((tn, tk), lambda i, j, k: (j, k)),
              "tn": pl.BlockSpec((tk, tn), lambda i, j, k: (k, j))}[mode]
    if b_blocks and mode == "nn":
        per = b.shape[2] // tn
        b_spec = pl.BlockSpec((None, tk, tn), lambda i, j, k: (j // per, k, j % per))
    elif b_blocks:
        b_spec = pl.BlockSpec((b_blocks, tn, b.shape[2]), lambda i, j, k: (0, j, 0))
    out_spec = pl.BlockSpec((tm, tn), lambda i, j, k: (i, j))
    out_shape = jax.ShapeDtypeStruct((M, N), out_dtype)
    if out_blocks:
        per_o = N // out_blocks // tn
        out_spec = pl.BlockSpec((None, tm, tn), lambda i, j, k: (j // per_o, i, j % per_o))
        out_shape = jax.ShapeDtypeStruct((out_blocks, M, N // out_blocks), out_dtype)
    epi_widths = list(out_widths) if out_widths else [N] * n_o
    assert all(w == N for w in epi_widths) or N == tn
    epi_specs = [pl.BlockSpec((tm, tn if w == N else w), lambda i, j, k: (i, j)) for w in epi_widths]
    x_specs, x_args = [], []
    for x in extras:
        arr, off = x if isinstance(x, tuple) else (x, 0)
        if arr.shape[0] == 1:
            x_specs.append(pl.BlockSpec((1, tn), lambda i, j, k: (0, j)))
        else:
            x_specs.append(pl.BlockSpec((tm, tn), lambda i, j, k, off=off: (i, j + off)))
        x_args.append(arr)
    res = pl.pallas_call(
        body, name=name, grid=(M // tm, N // tn, nk),
        in_specs=[a_spec, b_spec] + x_specs + [pl.BlockSpec(memory_space=pl.ANY)] * n_after,
        out_specs=epi_specs if epilogue else out_spec,
        out_shape=[jax.ShapeDtypeStruct((M, w), dt) for w, dt in zip(epi_widths, out_dtypes)] if epilogue else out_shape,
        scratch_shapes=[pltpu.VMEM((tm, tn), F32)] if nk > 1 else [],
        compiler_params=pltpu.CompilerParams(dimension_semantics=("parallel", "parallel", "arbitrary"),
                                             vmem_limit_bytes=VMEM_LIMIT),
    )(a, b, *x_args, *([after] if n_after else []))
    return res


class Rows:
    def __init__(self, arr, width=None, cb=0):
        self.arr, self.width, self.cb = arr, (arr.shape[1] if width is None else width), cb


class Heads:
    def __init__(self, arr):
        self.arr = arr


class Halo:
    def __init__(self, arr, side):
        self.arr, self.side = arr, side


def _rows_call(name, fn, ins, consts, outs, accs=(), tm=512, with_pid=False):
    T = next(o.arr.shape[1] if isinstance(o, Heads) else o.arr.shape[0] for o in ins if not isinstance(o, Halo))
    tm = min(tm, T)
    n_tiles = T // tm
    n_in, n_c, n_out = len(ins), len(consts), len(outs)
    in_specs, args = [], []
    for o in ins:
        if isinstance(o, Rows):
            in_specs.append(pl.BlockSpec((tm, o.width), lambda i, cb=o.cb: (i, cb)))
        elif isinstance(o, Heads):
            in_specs.append(pl.BlockSpec((NP, tm, PW), lambda i: (0, i, 0)))
        else:
            w = o.arr.shape[1]
            if o.side < 0:
                in_specs.append(pl.BlockSpec((8, w), lambda i: (jnp.maximum(i * (tm // 8) - 1, 0), 0)))
            else:
                in_specs.append(pl.BlockSpec((8, w), lambda i: (jnp.minimum((i + 1) * (tm // 8), T // 8 - 1), 0)))
        args.append(o.arr)
    for c in consts:
        in_specs.append(pl.BlockSpec(c.shape, lambda i, nd=c.ndim: (0,) * nd))
        args.append(c)
    out_specs, out_shape = [], []
    for o in outs:
        if o[0] == "rows":
            out_specs.append(pl.BlockSpec((tm, o[1]), lambda i: (i, 0)))
            out_shape.append(jax.ShapeDtypeStruct((T, o[1]), o[2]))
        else:
            out_specs.append(pl.BlockSpec((NP, tm, PW), lambda i: (0, i, 0)))
            out_shape.append(jax.ShapeDtypeStruct((NP, T, PW), o[1]))
    for shape, dt in accs:
        out_specs.append(pl.BlockSpec(shape, lambda i, nd=len(shape): (0,) * nd))
        out_shape.append(jax.ShapeDtypeStruct(shape, dt))

    def body(*refs):
        i = pl.program_id(0)
        vals = []
        vals = [r[...] for r in refs[:n_in + n_c]]
        res = fn(i, n_tiles, *vals) if with_pid else fn(*vals)
        out_refs = refs[n_in + n_c:]
        for r, v in zip(out_refs[:n_out], res[:n_out]):
            r[...] = v.astype(r.dtype)
        if accs:
            @pl.when(i == 0)
            def _():
                for r in out_refs[n_out:]:
                    r[...] = jnp.zeros_like(r)

            for r, v in zip(out_refs[n_out:], res[n_out:]):
                r[...] += v.astype(r.dtype)

    res = pl.pallas_call(
        body, name=name, grid=(n_tiles,), in_specs=in_specs, out_specs=out_specs, out_shape=out_shape,
        compiler_params=pltpu.CompilerParams(dimension_semantics=("arbitrary",), vmem_limit_bytes=VMEM_LIMIT),
    )(*args)
    return res


def _rms(x, g):
    return x * lax.rsqrt(jnp.mean(x * x, axis=-1, keepdims=True) + NORM_EPS) * g


def _gelu(x):
    return 0.5 * x * (1.0 + lax.erf(x * 0.7071067811865476))


def _sigmoid(x):
    return 1.0 / (1.0 + jnp.exp(-x))


def _bdot(a, b):
    return jnp.dot(a.astype(BF16), b.astype(BF16), preferred_element_type=F32)


def _to_heads(x):
    return jnp.concatenate([x[:, p * PW:(p + 1) * PW][None] for p in range(NP)], axis=0)


def _from_heads(xp):
    return jnp.concatenate([xp[p] for p in range(NP)], axis=-1)


def _head_sum(xp):
    low = lax.broadcasted_iota(jnp.int32, xp.shape, xp.ndim - 1) < HN
    both = jnp.sum(xp, axis=-1, keepdims=True)
    first = jnp.sum(jnp.where(low, xp, 0.0), axis=-1, keepdims=True)
    return jnp.where(low, first, both - first)


def _split_pairs(xp):
    return jnp.concatenate([xp[:, :, :HN], xp[:, :, HN:]], axis=0)


def _join_pairs(xh):
    return jnp.concatenate([xh[:NP], xh[NP:]], axis=-1)


def _sgu_fn(p, ln_w, ln_b, sw, sbt):
    z = _gelu(p)
    u, v = z[:, :D], z[:, D:]
    mu = jnp.mean(v, axis=-1, keepdims=True)
    var = jnp.mean(jnp.square(v - mu), axis=-1, keepdims=True)
    vn = (v - mu) * lax.rsqrt(var + LN_EPS) * ln_w + ln_b
    ri = lax.broadcasted_iota(jnp.int32, (SGU_C, SGU_C), 0)
    ci = lax.broadcasted_iota(jnp.int32, (SGU_C, SGU_C), 1)
    mask = (ci <= ri).astype(F32)
    dg = D // SGU_G
    parts = []
    for g in range(SGU_G):
        parts.append(_bdot(sw[g] * mask, vn[:, g * dg:(g + 1) * dg]) + sbt[:, g:g + 1])
    return u * jnp.concatenate(parts, axis=-1)


def _pre_fn(qr, qk, qv, qxw, qxa, qxg, wl, w0, al, a0, gl, k_k, k_a):
    w = -jax.nn.softplus(-(w0 + _bdot(jnp.tanh(qxw), wl))) - 0.5
    lw = -jnp.exp(w)
    aa = _sigmoid(a0 + _bdot(qxa, al))
    g = _bdot(_sigmoid(qxg), gl)
    kk = _to_heads(qk * k_k)
    kk = kk / jnp.maximum(jnp.sqrt(_head_sum(kk * kk)), 1e-12)
    k2 = qk * (1.0 + (aa - 1.0) * k_a)
    return _to_heads(qr), _to_heads(lw), _to_heads(k2), _to_heads(qv), kk, _to_heads(aa), g


def _post_fn(o, r, k2, v, g, ln_w, ln_b, r_k):
    mu = _head_sum(o) * (1.0 / HN)
    d = o - mu
    var = _head_sum(d * d) * (1.0 / HN)
    on = d * lax.rsqrt(var + GN_EPS) * ln_w + ln_b
    bonus = _head_sum(r * k2 * r_k) * v
    return _from_heads(on + bonus) * g


def _gate_fn(pg, ya, yb):
    return _sigmoid(pg[:, :D]) * ya + _sigmoid(pg[:, D:]) * yb


def _bmm(x, y, cx, cy, out_path=False):
    return lax.dot_general(x, y, (((cx,), (cy,)), ((0,), (0,))),
                           precision=SCAN_OUT_PRECISION if out_path else SCAN_PRECISION, preferred_element_type=F32)


def _unit_lower_inverse(M):
    C = M.shape[1]
    ti = lax.broadcasted_iota(jnp.int32, (C, C), 0)
    tj = lax.broadcasted_iota(jnp.int32, (C, C), 1)
    eye = (ti == tj).astype(F32)
    same = lambda b: (ti // b == tj // b).astype(F32)
    X = -(M * same(SOLVE_B))
    inv = eye + X
    span = 1
    while 2 * span < SOLVE_B:
        X = _bmm(X, X, 2, 1)
        inv = inv + _bmm(inv, X, 2, 1)
        span *= 2
    b = SOLVE_B
    while b < C:
        low = M * (same(2 * b) - same(b))
        inv = inv - _bmm(_bmm(inv, low, 2, 1, out_path=True), inv, 2, 1, out_path=True)
        b *= 2
    return inv


@jax.custom_vjp
def _unit_lower_solve(inv, M, y):
    return _bmm(inv, y, 2, 1)


def _unit_lower_solve_fwd(inv, M, y):
    u = _bmm(inv, y, 2, 1)
    return u, (inv, u)


def _unit_lower_solve_bwd(res, du):
    inv, u = res
    dy = _bmm(inv, du, 1, 1)
    return jnp.zeros_like(inv), -_bmm(dy, u, 2, 2), dy


_unit_lower_solve.defvjp(_unit_lower_solve_fwd, _unit_lower_solve_bwd)


@jax.custom_vjp
def _unit_lower_solved(inv, u, M, y):
    return u


_unit_lower_solved.defvjp(lambda inv, u, M, y: (u, (inv, u)),
                          lambda res, du: (jnp.zeros_like(res[0]), jnp.zeros_like(res[1]))
                          + _unit_lower_solve_bwd(res, du)[1:])


@functools.partial(jax.custom_vjp, nondiff_argnums=(0,))
def _kept(fn, value, *args):
    return value


def _kept_fwd(fn, value, *args):
    return value, args


def _kept_bwd(fn, args, d):
    _, vjp = jax.vjp(fn, *args)
    return (jnp.zeros_like(d),) + tuple(vjp(d))


_kept.defvjp(_kept_fwd, _kept_bwd)


def _sum_over_time(x, reverse):
    C = x.shape[1]
    ti = lax.broadcasted_iota(jnp.int32, (C, C), 0)
    tj = lax.broadcasted_iota(jnp.int32, (C, C), 1)
    ones = jnp.broadcast_to(((tj >= ti) if reverse else (tj <= ti)).astype(BF16), (x.shape[0], C, C))
    hi = x.astype(BF16)
    r1 = x - hi.astype(F32)
    mid = r1.astype(BF16)
    lo = (r1 - mid.astype(F32)).astype(BF16)
    dn = (((2,), (1,)), ((0,), (0,)))
    return sum(lax.dot_general(ones, p, dn, preferred_element_type=F32) for p in (lo, mid, hi))


@jax.custom_vjp
def _time_cumsum(lw):
    return _sum_over_time(lw, reverse=False)


_time_cumsum.defvjp(lambda lw: (_sum_over_time(lw, reverse=False), None),
                    lambda _, d: (_sum_over_time(d, reverse=True),))


def _chunk_fn(S0, r, lw, k, v, kk, a, kept=None):
    C = SCAN_C
    bmm = _bmm
    ti = lax.broadcasted_iota(jnp.int32, (C, C), 0)
    tj = lax.broadcasted_iota(jnp.int32, (C, C), 1)
    incl2 = jnp.concatenate([(tj <= ti).astype(F32)] * 2, axis=1)
    strict = (tj < ti).astype(F32)
    n_mask = jnp.concatenate([jnp.zeros((C, C), F32), strict], axis=1)

    def known(name, fn, *args):
        return fn(*args) if kept is None else _kept(fn, kept[name], *args)

    cum = known("cum", _time_cumsum, lw)
    g_in, g_ex, g_inv = jnp.exp(cum), jnp.exp(cum - lw), jnp.exp(-cum)
    kkt, rt = kk * g_ex, r * g_in
    bk = jnp.concatenate([kk * a * g_inv, k * g_inv], axis=1)
    kr = jnp.concatenate([kkt, rt], axis=1)
    ratios = known("ratios", lambda x, y: bmm(x, y, 2, 2), kr, bk)
    A = ratios[:, :C]
    M = A[:, :, :C] * strict
    zv = jnp.concatenate([jnp.zeros_like(v), v], axis=1)
    s0_side = bmm(kr, S0, 2, 2, out_path=True)
    rhs = s0_side[:, :C] + bmm(A * n_mask, zv, 2, 1, out_path=True)
    if kept is None:
        inv = lax.stop_gradient(_unit_lower_inverse(M))
        y = _unit_lower_solve(inv, M, rhs)
    else:
        inv = kept["inv"]
        y = _unit_lower_solved(inv, kept["y"], M, rhs)
    z = jnp.concatenate([-y, v], axis=1)
    O = s0_side[:, C:] + bmm(ratios[:, C:] * incl2, z, 2, 1, out_path=True)
    g_end = g_in[:, C - 1:C, :]
    S1 = S0 * g_end + bmm(z, bk * g_end, 1, 1, out_path=True)
    return O, S1, dict(cum=cum, ratios=ratios, y=y, inv=inv)


def _scan_fwd(r, lw, k, v, kk, a, ex=None, tb=256):
    assert SCAN_C == HN and 2 * SCAN_C == PW
    T = r.shape[1]
    tb = min(tb, T)
    n_chunks = tb // SCAN_C
    nb = T // tb
    nx = ex.nb if ex else 0

    def body(*refs):
        r_ref, lw_ref, k_ref, v_ref, kk_ref, a_ref = refs[:6]
        x_in, (o_ref, s0_ref), x_out = refs[6:6 + nx], refs[6 + nx:8 + nx], refs[8 + nx:8 + 2 * nx]
        s_ref, sems = refs[8 + 2 * nx], refs[9 + 2 * nx:]

        plan = ex.schedule(nb) if ex else []

        @pl.when(pl.program_id(0) == 0)
        def _():
            s_ref[...] = jnp.zeros_like(s_ref)
            for at, action in plan[:1]:
                action(x_in, x_out, sems)

        def step(c, carry):
            sl = pl.ds(pl.multiple_of(c * SCAN_C, SCAN_C), SCAN_C)
            S0 = s_ref[...]
            O, S1, keep = _chunk_fn(S0, *[_split_pairs(ref[:, sl, :])
                                          for ref in (r_ref, lw_ref, k_ref, v_ref, kk_ref, a_ref)])
            o_ref[:, sl, :] = _join_pairs(O)
            s0_ref[c, 0] = jnp.concatenate([S0, keep["inv"]], axis=-1)
            s0_ref[c, 1] = jnp.concatenate([keep["cum"], keep["y"]], axis=-1)
            s0_ref[c, 2] = keep["ratios"][:, :SCAN_C]
            s0_ref[c, 3] = keep["ratios"][:, SCAN_C:]
            s_ref[...] = S1
            return carry

        lax.fori_loop(0, n_chunks, step, 0)

        for at, action in plan[1:]:
            pl.when(pl.program_id(0) == at)(functools.partial(action, x_in, x_out, sems))

    hm = pl.BlockSpec((NP, tb, PW), lambda i: (0, i, 0))
    res = pl.pallas_call(
        body, name="rwkv_scan_fwd", grid=(nb,), in_specs=[hm] * 6 + (ex.any_specs if ex else []),
        out_specs=[hm, pl.BlockSpec((n_chunks, N_KEPT, NH, HN, PW), lambda i: (i, 0, 0, 0, 0))]
        + (ex.any_specs if ex else []),
        out_shape=[jax.ShapeDtypeStruct((NP, T, PW), F32),
                   jax.ShapeDtypeStruct((T // SCAN_C, N_KEPT, NH, HN, PW), F32)]
        + (ex.out_shape if ex else []),
        scratch_shapes=[pltpu.VMEM((NH, HN, HN), F32)] + (ex.sem_shapes if ex else []),
        compiler_params=pltpu.CompilerParams(dimension_semantics=("arbitrary",), vmem_limit_bytes=VMEM_LIMIT),
    )(r, lw, k, v, kk, a, *(ex.bufs if ex else []))
    return res[0], res[1], list(res[2:])


def _scan_bwd(r, lw, k, v, kk, a, s0s, do, ex=None, tb=128):
    T = r.shape[1]
    tb = min(tb, T)
    n_chunks = tb // SCAN_C
    nb = T // tb
    nx = ex.nb if ex else 0

    def body(*refs):
        r_ref, lw_ref, k_ref, v_ref, kk_ref, a_ref, s0_ref, do_ref = refs[:8]
        x_in, (dr, dlw, dk, dv, dkk, da), x_out = refs[8:8 + nx], refs[8 + nx:14 + nx], refs[14 + nx:14 + 2 * nx]
        ds_ref, sems = refs[14 + 2 * nx], refs[15 + 2 * nx:]

        plan = ex.schedule(nb) if ex else []

        @pl.when(pl.program_id(0) == 0)
        def _():
            ds_ref[...] = jnp.zeros_like(ds_ref)
            for at, action in plan[:1]:
                action(x_in, x_out, sems)

        def step(j, carry):
            c = n_chunks - 1 - j
            sl = pl.ds(pl.multiple_of(c * SCAN_C, SCAN_C), SCAN_C)
            s0_inv, cum_y = s0_ref[c, 0], s0_ref[c, 1]
            kept = dict(inv=s0_inv[:, :, HN:], cum=cum_y[:, :, :HN], y=cum_y[:, :, HN:],
                        ratios=jnp.concatenate([s0_ref[c, 2], s0_ref[c, 3]], axis=1))
            _, vjp = jax.vjp(lambda *t: _chunk_fn(*t, kept=kept)[:2], s0_inv[:, :, :HN],
                             *[_split_pairs(ref[:, sl, :]) for ref in (r_ref, lw_ref, k_ref, v_ref, kk_ref, a_ref)])
            g = vjp((_split_pairs(do_ref[:, sl, :]), ds_ref[...]))
            ds_ref[...] = g[0]
            for ref, val in zip((dr, dlw, dk, dv, dkk, da), g[1:]):
                ref[:, sl, :] = _join_pairs(val)
            return carry

        lax.fori_loop(0, n_chunks, step, 0)

        for at, action in plan[1:]:
            pl.when(pl.program_id(0) == at)(functools.partial(action, x_in, x_out, sems))

    hm = pl.BlockSpec((NP, tb, PW), lambda i: (0, nb - 1 - i, 0))
    res = pl.pallas_call(
        body, name="rwkv_scan_bwd", grid=(nb,),
        in_specs=[hm] * 6 + [pl.BlockSpec((n_chunks, N_KEPT, NH, HN, PW), lambda i: (nb - 1 - i, 0, 0, 0, 0)), hm]
        + (ex.any_specs if ex else []),
        out_specs=[hm] * 6 + (ex.any_specs if ex else []),
        out_shape=[jax.ShapeDtypeStruct((NP, T, PW), F32)] * 6 + (ex.out_shape if ex else []),
        scratch_shapes=[pltpu.VMEM((NH, HN, HN), F32)] + (ex.sem_shapes if ex else []),
        compiler_params=pltpu.CompilerParams(dimension_semantics=("arbitrary",), vmem_limit_bytes=VMEM_LIMIT),
    )(r, lw, k, v, kk, a, s0s, do, *(ex.bufs if ex else []))
    return list(res[:6]), list(res[6:])


def _shift_down(i, p, prev8):
    first = jnp.where(i > 0, prev8[7:8, :], 0.0)
    row = lax.broadcasted_iota(jnp.int32, p.shape, 0)
    return jnp.where(row == 0, first, pltpu.roll(p, 1, axis=0))


def _mix_bwd(dq, p, sb, tm=256):
    def fn(i, n, dq, next8, p, prev8, sb):
        ps = _shift_down(i, p, prev8)
        d1 = dq * sb[1:2]
        last = jnp.where(i < n - 1, next8[0:1, :] * sb[1:2], 0.0)
        row = lax.broadcasted_iota(jnp.int32, dq.shape, 0)
        up = jnp.where(row == dq.shape[0] - 1, last, pltpu.roll(d1, dq.shape[0] - 1, axis=0))
        return (dq * sb[0:1] + up, jnp.sum(dq * p, axis=0, keepdims=True), jnp.sum(dq * ps, axis=0, keepdims=True))
    w = p.shape[1]
    return _rows_call("shift_mix_bwd", fn, [Rows(dq), Halo(dq, +1), Rows(p), Halo(p, -1)], [sb], [("rows", w, BF16)],
                      accs=[((1, w), F32), ((1, w), F32)], tm=tm, with_pid=True)


def _local_step(x, target, W, late_weights=None, early_grads=None, w_in_grads_ready=None, ffn_grads_ready=None):
    G = {}
    a = _rows_call("norm_mix_fwd", lambda x, g: (_rms(x, g),), [Rows(x)], [W["g_mix"]], [("rows", D, BF16)])[0]
    p_sgu = _matmul("proj_sgu", a, W["w_sgu_t"], "nt")
    def token_shift(p, sb0, sb1):
        row = lax.broadcasted_iota(jnp.int32, p.shape, 0)
        return p, p * sb0 + jnp.where(row == 0, 0.0, pltpu.roll(p, 1, axis=0)) * sb1
    p_rw, q = _matmul("proj_rwkv", a, W["w_rw_t"], "nt", tn=512, whole_rows=True, epilogue=token_shift,
                      extras=[W["sb"][0:1], W["sb"][1:2]], out_dtypes=(F32, F32))
    p_gate = _matmul("proj_gate", a, W["w_gate_t"], "nt")

    sgu_consts = [W["sgu_ln_w"], W["sgu_ln_b"], W["sgu_w"], W["sgu_bt"]]
    s = _rows_call("sgu_fwd", lambda *t: (_sgu_fn(*t),), [Rows(p_sgu)], sgu_consts, [("rows", D, BF16)], tm=SGU_C)[0]

    q_ins = [Rows(q, D, 0), Rows(q, D, 1), Rows(q, D, 2), Rows(q, 128, 24), Rows(q, 128, 25), Rows(q, 256, 13)]
    pre_consts = [W["w_lora"], W["w0"], W["a_lora"], W["a0"], W["g_lora"], W["k_k"], W["k_a"]]
    r_h, lw_h, k_h, v_h, kk_h, a_h, g_gate = _rows_call(
        "rwkv_pre_fwd", _pre_fn, q_ins, pre_consts, [("heads", F32)] * 6 + [("rows", D, F32)], tm=128)
    o_h, s0s, got = _scan_fwd(r_h, lw_h, k_h, v_h, kk_h, a_h, ex=late_weights[0] if late_weights else None)
    if late_weights:
        W = {**W, **late_weights[1](got)}
    y_a = _matmul("proj_a", s, W["w_proj_a"], "nn")
    post_ins = [Heads(o_h), Heads(r_h), Heads(k_h), Heads(v_h), Rows(g_gate)]
    post_consts = [W[n].reshape(NP, 1, PW) for n in ("ln_x_w", "ln_x_b", "r_k")]
    z_b = _rows_call("rwkv_post_fwd", lambda *t: (_post_fn(*t),), post_ins, post_consts, [("rows", D, BF16)], tm=128)[0]
    y_b, mixed = _matmul("proj_b", z_b, W["w_proj_b"], "nn", extras=[(p_gate, 0), (p_gate, 1), y_a],
                         epilogue=lambda yb, ga, gb, ya: (yb, _sigmoid(ga) * ya + _sigmoid(gb) * yb),
                         out_dtypes=(F32, BF16))

    def res1(mo, x, g):
        h1 = x + mo
        return h1, _rms(h1, g)
    h1, f = _matmul("proj_out", mixed, W["w_out"], "nn", extras=[x, W["g_ffn"]], epilogue=res1,
                    out_dtypes=(F32, BF16))

    def relu_sq(u):
        r = jnp.maximum(u, 0.0)
        return r, r * r
    r1, act = _matmul("ffn_up", f, W["w_ffn1"], "nn", epilogue=relu_sq, out_dtypes=(BF16, BF16))
    ff = _matmul("ffn_down", act, W["w_ffn2"], "nn")

    def head(h1, ff, tgt, g):
        def f_(h1, ff, g):
            y = _rms(h1 + ff, g)
            return 0.5 * jnp.sum(jnp.mean(jnp.square(y - tgt), axis=-1))
        loss, (dh2, _, dg) = jax.value_and_grad(f_, argnums=(0, 1, 2))(h1, ff, g)
        return dh2, jnp.full((8, LANES), loss, F32), dg
    dh2, loss_acc, G["g_final"] = _rows_call("loss_head", head, [Rows(h1), Rows(ff), Rows(target)], [W["g_final"]],
                                             [("rows", D, F32)], accs=[((8, LANES), F32), ((1, D), F32)])

    d_u1 = _matmul("ffn_down_dx", dh2, W["w_ffn2"], "nt", extras=[r1], out_dtypes=(BF16,),
                   epilogue=lambda d_act, r: (d_act * 2.0 * r.astype(F32),))[0]
    G["w_ffn2"] = _matmul("ffn_down_dw", act, dh2, "tn", out_dtype=GRAD_PAYLOAD)
    d_f = _matmul("ffn_up_dx", d_u1, W["w_ffn1"], "nt")
    G["w_ffn1"] = _matmul("ffn_up_dw", f, d_u1, "tn", out_blocks=N_DEV, out_dtype=GRAD_PAYLOAD)

    def res1_bwd(h1, d_f, dh2, g):
        _, vjp = jax.vjp(_rms, h1, g)
        dh, dg = vjp(d_f)
        return dh2 + dh, dg
    dh1, G["g_ffn"] = _rows_call("residual_norm_bwd", res1_bwd, [Rows(h1), Rows(d_f), Rows(dh2)],
                                 [W["g_ffn"]], [("rows", D, F32)], accs=[((1, D), F32)])
    ffn_token = ffn_grads_ready(G) if ffn_grads_ready else None
    def gate_bwd(d_mixed, ga, gb, ya, yb):
        _, vjp = jax.vjp(_gate_fn, jnp.concatenate([ga, gb], axis=-1), ya, yb)
        return vjp(d_mixed)
    d_gate, d_ya, d_yb = _matmul("proj_out_dx", dh1, W["w_out"], "nt", after=ffn_token, epilogue=gate_bwd,
                                 extras=[(p_gate, 0), (p_gate, 1), y_a, y_b], out_dtypes=(BF16, BF16, BF16),
                                 out_widths=(2 * D, D, D))
    G["w_out"] = _matmul("proj_out_dw", mixed, dh1, "tn", out_dtype=GRAD_PAYLOAD)

    d_s = _matmul("proj_a_dx", d_ya, W["w_proj_a"], "nt")
    G["w_proj_a"] = _matmul("proj_a_dw", s, d_ya, "tn", out_dtype=GRAD_PAYLOAD)

    def sgu_bwd(p, ds, *c):
        _, vjp = jax.vjp(_sgu_fn, p, *c)
        return vjp(ds)
    d_p_sgu, G["sgu_ln_w"], G["sgu_ln_b"], G["sgu_w"], G["sgu_bt"] = _rows_call(
        "sgu_bwd", sgu_bwd, [Rows(p_sgu), Rows(d_s)], sgu_consts, [("rows", 2 * D, BF16)],
        accs=[((1, D), F32), ((1, D), F32), ((SGU_G, SGU_C, SGU_C), F32), ((SGU_C, SGU_G), F32)], tm=SGU_C)

    d_zb = _matmul("proj_b_dx", d_yb, W["w_proj_b"], "nt")
    G["w_proj_b"] = _matmul("proj_b_dw", z_b, d_yb, "tn", out_dtype=GRAD_PAYLOAD)

    def post_bwd(o, r, k2, v, g, dz, *c):
        _, vjp = jax.vjp(_post_fn, o, r, k2, v, g, *c)
        return vjp(dz)
    do_h, dr1, dk1, dv1, d_g, g_lnw, g_lnb, g_rk = _rows_call(
        "rwkv_post_bwd", post_bwd, post_ins + [Rows(d_zb)], post_consts, [("heads", F32)] * 4 + [("rows", D, F32)],
        accs=[((NP, 1, PW), F32)] * 3, tm=128)
    G["ln_x_w"], G["ln_x_b"], G["r_k"] = (t.reshape(1, D) for t in (g_lnw, g_lnb, g_rk))
    (dr2, dlw, dk2, dv2, dkk, daa), early = _scan_bwd(r_h, lw_h, k_h, v_h, kk_h, a_h, s0s, do_h,
                                                      ex=early_grads(G) if early_grads else None)

    def pre_bwd(qr, qk, qv, qxw, qxa, qxg, dr1, dr2, dlw, dk1, dk2, dv1, dv2, dkk, daa, dg, *c):
        _, vjp = jax.vjp(_pre_fn, qr, qk, qv, qxw, qxa, qxg, *c)
        g = vjp((dr1 + dr2, dlw, dk1 + dk2, dv1 + dv2, dkk, daa, dg))
        dq = jnp.concatenate(g[:6], axis=-1)
        return (dq,) + tuple(g[6:])
    pre_b_ins = q_ins + [Heads(dr1), Heads(dr2), Heads(dlw), Heads(dk1), Heads(dk2), Heads(dv1), Heads(dv2),
                         Heads(dkk), Heads(daa), Rows(d_g)]
    d_q, G["w_lora"], G["w0"], G["a_lora"], G["a0"], G["g_lora"], G["k_k"], G["k_a"] = _rows_call(
        "rwkv_pre_bwd", pre_bwd, pre_b_ins, pre_consts, [("rows", RW_INT, F32)],
        accs=[((128, D), F32), ((1, D), F32), ((128, D), F32), ((1, D), F32), ((256, D), F32), ((1, D), F32),
              ((1, D), F32)], tm=128)
    d_p_rw, dsb0, dsb1 = _mix_bwd(d_q, p_rw, W["sb"])
    G["sb"] = jnp.concatenate([dsb0, dsb1], axis=0)

    G["w_sgu_t"] = _matmul("proj_sgu_dw", d_p_sgu, a, "tn", out_dtype=GRAD_PAYLOAD)
    G["w_rw_t"] = _matmul("proj_rwkv_dw", d_p_rw, a, "tn", out_dtype=GRAD_PAYLOAD)
    G["w_gate_t"] = _matmul("proj_gate_dw", d_gate, a, "tn", out_dtype=GRAD_PAYLOAD)
    token = w_in_grads_ready(G) if w_in_grads_ready else None
    da1 = _matmul("proj_sgu_dx", d_p_sgu, W["w_sgu_t"], "nn", after=token)
    da2 = _matmul("proj_rwkv_dx", d_p_rw, W["w_rw_t"], "nn", after=token)
    da3 = _matmul("proj_gate_dx", d_gate, W["w_gate_t"], "nn", after=token)

    def norm1_bwd(x, da1, da2, da3, dh1, g):
        _, vjp = jax.vjp(_rms, x, g)
        dx, dg = vjp(da1 + da2 + da3)
        return dh1 + dx, dg
    dx, G["g_mix"] = _rows_call("norm_mix_bwd", norm1_bwd, [Rows(x), Rows(da1), Rows(da2), Rows(da3), Rows(dh1)],
                                [W["g_mix"]], [("rows", D, F32)], accs=[((1, D), F32)])
    return loss_acc[0, 0], dx, G, early


class Exchange:
    def __init__(self, bufs, gathers):
        self.bufs, self.gathers, self.nb = list(bufs), list(gathers), len(bufs)
        self.any_specs = [pl.BlockSpec(memory_space=pl.ANY)] * self.nb
        self.out_shape = [jax.ShapeDtypeStruct((N_DEV,) + (b.shape if g else b.shape[1:]), b.dtype)
                          for b, g in zip(self.bufs, self.gathers)]
        n = (N_DEV - 1) * self.nb
        self.sem_shapes = [pltpu.SemaphoreType.DMA((n,)), pltpu.SemaphoreType.DMA((n,)),
                           pltpu.SemaphoreType.DMA((self.nb,))]

    def _copies(self, in_refs, out_refs, sems):
        send_sems, recv_sems, local_sems = sems
        x, y, c = lax.axis_index("x"), lax.axis_index("y"), lax.axis_index("c")
        me = 4 * x + 2 * y + c

        def src(b, dest):
            return in_refs[b] if self.gathers[b] else in_refs[b].at[dest]

        local = [pltpu.make_async_copy(src(b, me), out_refs[b].at[me], local_sems.at[b]) for b in range(self.nb)]
        sends, recvs = [], []
        for kbits in range(1, N_DEV):
            px = 1 - x if kbits & 4 else x
            py = 1 - y if kbits & 2 else y
            pc = 1 - c if kbits & 1 else c
            peer = 4 * px + 2 * py + pc
            for b in range(self.nb):
                s = (kbits - 1) * self.nb + b
                sends.append(pltpu.make_async_remote_copy(
                    src_ref=src(b, peer), dst_ref=out_refs[b].at[me], send_sem=send_sems.at[s],
                    recv_sem=recv_sems.at[s], device_id=(px, py, pc), device_id_type=pl.DeviceIdType.MESH))
                recvs.append(pltpu.make_async_remote_copy(
                    src_ref=src(b, peer), dst_ref=out_refs[b].at[peer], send_sem=send_sems.at[s],
                    recv_sem=recv_sems.at[s], device_id=(px, py, pc), device_id_type=pl.DeviceIdType.MESH))
        return local, sends, recvs

    def start(self, in_refs, out_refs, sems):
        local, sends, _ = self._copies(in_refs, out_refs, sems)
        for cp in sends + local:
            cp.start()

    def wait(self, in_refs, out_refs, sems):
        local, sends, recvs = self._copies(in_refs, out_refs, sems)
        for cp in recvs:
            cp.wait_recv()
        for cp in sends:
            cp.wait_send()
        for cp in local:
            cp.wait()

    def schedule(self, n_steps):
        return [(0, self.start), (n_steps - 1, self.wait)]


def _exchange(name, bufs, gather):
    ex = Exchange(bufs, gather if isinstance(gather, (list, tuple)) else [gather] * len(bufs))

    def body(*refs):
        in_refs, out_refs, sems = refs[:ex.nb], refs[ex.nb:2 * ex.nb], refs[2 * ex.nb:]
        ex.start(in_refs, out_refs, sems)
        ex.wait(in_refs, out_refs, sems)

    return pl.pallas_call(body, name=name, in_specs=ex.any_specs, out_specs=ex.any_specs, out_shape=ex.out_shape,
                          scratch_shapes=ex.sem_shapes)(*ex.bufs)


N_CHIP = 4


def _pair_exchange(name, blocks):
    def body(b_ref, got_ref, send_sems, recv_sems):
        x, y, c = lax.axis_index("x"), lax.axis_index("y"), lax.axis_index("c")
        copies = [pltpu.make_async_remote_copy(
            src_ref=b_ref.at[2 * q + 1 - c], dst_ref=got_ref.at[q], send_sem=send_sems.at[q],
            recv_sem=recv_sems.at[q], device_id=(x, y, 1 - c), device_id_type=pl.DeviceIdType.MESH)
            for q in range(N_CHIP)]
        for cp in copies:
            cp.start()
        for cp in copies:
            cp.wait_recv()
        for cp in copies:
            cp.wait_send()

    any_spec = pl.BlockSpec(memory_space=pl.ANY)
    return pl.pallas_call(
        body, name=name, in_specs=[any_spec], out_specs=any_spec,
        out_shape=jax.ShapeDtypeStruct((N_CHIP,) + blocks.shape[1:], blocks.dtype),
        scratch_shapes=[pltpu.SemaphoreType.DMA((N_CHIP,))] * 2,
    )(blocks)


def _pair_sum(name, blocks, got, core):
    _, R, Wd = blocks.shape

    def body(c_ref, a_ref, b_ref, o_ref):
        o_ref[...] = (a_ref[...].astype(F32) + b_ref[...].astype(F32)).astype(o_ref.dtype)

    return pl.pallas_call(
        body, name=name,
        grid_spec=pltpu.PrefetchScalarGridSpec(
            num_scalar_prefetch=1, grid=(N_CHIP,),
            in_specs=[pl.BlockSpec((None, R, Wd), lambda q, c_ref: (2 * q + c_ref[0], 0, 0)),
                      pl.BlockSpec((None, R, Wd), lambda q, c_ref: (q, 0, 0))],
            out_specs=pl.BlockSpec((None, R, Wd), lambda q, c_ref: (q, 0, 0))),
        out_shape=jax.ShapeDtypeStruct(got.shape, blocks.dtype),
        compiler_params=pltpu.CompilerParams(dimension_semantics=("parallel",), vmem_limit_bytes=VMEM_LIMIT),
    )(core, blocks, got)


def _chip_copies(s_ref, land_ref, send_sems, recv_sems):
    x, y, c = lax.axis_index("x"), lax.axis_index("y"), lax.axis_index("c")
    my_q = 2 * x + y
    sends, recvs = [], []
    for kbits in range(1, N_CHIP):
        px = 1 - x if kbits & 2 else x
        py = 1 - y if kbits & 1 else y
        peer_q = 2 * px + py
        sends.append(pltpu.make_async_remote_copy(
            src_ref=s_ref.at[peer_q], dst_ref=land_ref.at[my_q], send_sem=send_sems[kbits - 1],
            recv_sem=recv_sems[kbits - 1], device_id=(px, py, c), device_id_type=pl.DeviceIdType.MESH))
        recvs.append(pltpu.make_async_remote_copy(
            src_ref=s_ref.at[peer_q], dst_ref=land_ref.at[peer_q], send_sem=send_sems[kbits - 1],
            recv_sem=recv_sems[kbits - 1], device_id=(px, py, c), device_id_type=pl.DeviceIdType.MESH))
    return sends, recvs


_HBM = pl.BlockSpec(memory_space=pltpu.HBM)
_SEM = pl.BlockSpec(memory_space=pltpu.SEMAPHORE)
N_CHIP_SEMS = 2 * (N_CHIP - 1)


def _scatter_copies(b_refs, land_refs, send_sems, recv_sems):
    x, y, c = lax.axis_index("x"), lax.axis_index("y"), lax.axis_index("c")
    me = 4 * x + 2 * y + c
    sends, recvs = [], []
    for kbits in range(1, N_DEV):
        px = 1 - x if kbits & 4 else x
        py = 1 - y if kbits & 2 else y
        pc = 1 - c if kbits & 1 else c
        peer = 4 * px + 2 * py + pc
        for b in range(len(b_refs)):
            s = (kbits - 1) * len(b_refs) + b
            sends.append(pltpu.make_async_remote_copy(
                src_ref=b_refs[b].at[peer], dst_ref=land_refs[b].at[me], send_sem=send_sems[s],
                recv_sem=recv_sems[s], device_id=(px, py, pc), device_id_type=pl.DeviceIdType.MESH))
            recvs.append(pltpu.make_async_remote_copy(
                src_ref=b_refs[b].at[peer], dst_ref=land_refs[b].at[peer], send_sem=send_sems[s],
                recv_sem=recv_sems[s], device_id=(px, py, pc), device_id_type=pl.DeviceIdType.MESH))
    return sends, recvs


def _scatter_start(name, bufs):
    nb = len(bufs)
    n = (N_DEV - 1) * nb

    def body(*refs):
        b_refs, land_refs, outs = refs[:nb], refs[nb:2 * nb], refs[2 * nb:]
        sends, _ = _scatter_copies(b_refs, land_refs, outs[:n], outs[n:2 * n])
        for cp in sends:
            cp.start()
        token = outs[2 * n + 2 * nb]
        token[...] = jnp.zeros_like(token)

    thru = tuple(pltpu.HBM(b.shape, b.dtype) for b in bufs)
    res = pl.pallas_call(
        body, name=name, in_specs=(_HBM,) * (2 * nb),
        out_specs=(_SEM,) * (2 * n) + (_HBM,) * (2 * nb) + (pl.BlockSpec(memory_space=pltpu.VMEM),),
        out_shape=(pltpu.SemaphoreType.DMA(()),) * (2 * n) + thru + thru + (jax.ShapeDtypeStruct((8, LANES), F32),),
        input_output_aliases={i: 2 * n + i for i in range(2 * nb)},
        compiler_params=pltpu.CompilerParams(has_side_effects=pltpu.SideEffectType.DATAFLOW_SIDE_EFFECTING),
    )(*[pltpu.with_memory_space_constraint(b, pltpu.HBM) for b in bufs],
      *[pltpu.with_memory_space_constraint(lax.empty(b.shape, b.dtype), pltpu.HBM) for b in bufs])
    return res[:2 * n], list(res[2 * n:2 * n + nb]), list(res[2 * n + nb:2 * n + 2 * nb]), res[2 * n + 2 * nb]


def _copy_through_vmem(pairs, stage, sems):
    for into_vmem in (True, False):
        copies = [pltpu.make_async_copy(src if into_vmem else stage[i], stage[i] if into_vmem else dst, sems.at[i])
                  for i, (src, dst) in enumerate(pairs)]
        for cp in copies:
            cp.start()
        for cp in copies:
            cp.wait()


def _scatter_wait(name, sems, bufs_thru, lands_thru, after):
    nb = len(bufs_thru)
    n = (N_DEV - 1) * nb

    def body(*refs):
        b_refs, land_refs, sem_refs = refs[:nb], refs[nb:2 * nb], refs[2 * nb:2 * nb + 2 * n]
        stage, local_sems = refs[-(nb + 1):-1], refs[-1]
        me = 4 * lax.axis_index("x") + 2 * lax.axis_index("y") + lax.axis_index("c")
        _copy_through_vmem([(b_refs[b].at[me], land_refs[b].at[me]) for b in range(nb)], stage, local_sems)
        sends, recvs = _scatter_copies(b_refs, land_refs, sem_refs[:n], sem_refs[n:])
        for cp in sends:
            cp.wait_send()
        for cp in recvs:
            cp.wait_recv()

    thru = tuple(pltpu.HBM(b.shape, b.dtype) for b in bufs_thru)
    res = pl.pallas_call(
        body, name=name, in_specs=(_HBM,) * (2 * nb) + (_SEM,) * (2 * n) + (pl.BlockSpec(memory_space=pl.ANY),),
        out_specs=(_HBM,) * (2 * nb), out_shape=thru + thru,
        input_output_aliases={i: i for i in range(2 * nb)},
        scratch_shapes=[pltpu.VMEM(b.shape[1:], b.dtype) for b in bufs_thru] + [pltpu.SemaphoreType.DMA((nb,))],
        compiler_params=pltpu.CompilerParams(has_side_effects=pltpu.SideEffectType.DATAFLOW_SIDE_EFFECTING,
                                             vmem_limit_bytes=VMEM_LIMIT),
    )(*bufs_thru, *lands_thru, *sems, after)
    return list(res[:nb]), list(res[nb:])


def _chip_exchange_start(name, sums):
    def body(s_ref, land_ref, *outs):
        sems, token = outs[:N_CHIP_SEMS], outs[N_CHIP_SEMS + 2]
        sends, _ = _chip_copies(s_ref, land_ref, sems[:N_CHIP - 1], sems[N_CHIP - 1:])
        for cp in sends:
            cp.start()
        token[...] = jnp.zeros_like(token)

    res = pl.pallas_call(
        body, name=name, in_specs=(_HBM, _HBM),
        out_specs=(_SEM,) * N_CHIP_SEMS + (_HBM, _HBM, pl.BlockSpec(memory_space=pltpu.VMEM)),
        out_shape=(pltpu.SemaphoreType.DMA(()),) * N_CHIP_SEMS
        + (pltpu.HBM(sums.shape, sums.dtype), pltpu.HBM(sums.shape, sums.dtype), jax.ShapeDtypeStruct((8, LANES), F32)),
        input_output_aliases={0: N_CHIP_SEMS, 1: N_CHIP_SEMS + 1},
        compiler_params=pltpu.CompilerParams(has_side_effects=pltpu.SideEffectType.DATAFLOW_SIDE_EFFECTING),
    )(pltpu.with_memory_space_constraint(sums, pltpu.HBM),
      pltpu.with_memory_space_constraint(lax.empty(sums.shape, sums.dtype), pltpu.HBM))
    return res[:N_CHIP_SEMS], res[N_CHIP_SEMS], res[N_CHIP_SEMS + 1], res[N_CHIP_SEMS + 2]


def _chip_exchange_wait(name, sems, sums_thru, land_thru, after):
    def body(s_ref, land_ref, *rest):
        sems, stage, local_sems = rest[:N_CHIP_SEMS], rest[-2], rest[-1]
        my_chip = 2 * lax.axis_index("x") + lax.axis_index("y")
        _copy_through_vmem([(s_ref.at[my_chip], land_ref.at[my_chip])], [stage], local_sems)
        sends, recvs = _chip_copies(s_ref, land_ref, sems[:N_CHIP - 1], sems[N_CHIP - 1:])
        for cp in sends:
            cp.wait_send()
        for cp in recvs:
            cp.wait_recv()

    return pl.pallas_call(
        body, name=name, in_specs=(_HBM, _HBM) + (_SEM,) * N_CHIP_SEMS + (pl.BlockSpec(memory_space=pl.ANY),),
        out_specs=(_HBM, _HBM),
        out_shape=(pltpu.HBM(sums_thru.shape, sums_thru.dtype), pltpu.HBM(sums_thru.shape, sums_thru.dtype)),
        input_output_aliases={0: 0, 1: 1},
        scratch_shapes=[pltpu.VMEM(sums_thru.shape[1:], sums_thru.dtype), pltpu.SemaphoreType.DMA((1,))],
        compiler_params=pltpu.CompilerParams(has_side_effects=pltpu.SideEffectType.DATAFLOW_SIDE_EFFECTING,
                                             vmem_limit_bytes=VMEM_LIMIT),
    )(sums_thru, land_thru, *sems, after)


def _adam_update(g, w, m, v):
    m_new = ADAM_B1 * m + (1.0 - ADAM_B1) * g
    v_new = ADAM_B2 * v + (1.0 - ADAM_B2) * jnp.square(g)
    m_hat = m_new / (1.0 - ADAM_B1 ** ADAM_STEP)
    v_hat = v_new / (1.0 - ADAM_B2 ** ADAM_STEP)
    return -ADAM_LR * (m_hat / (jnp.sqrt(v_hat) + ADAM_EPS) + ADAM_WD * w), m_new, v_new


def _adamw_rows(name, slots, parts):
    n_parts, rows = len(parts), [p[0].shape[0] for p in parts]
    rest = slots.shape[1] - sum(rows)

    def total(s_ref, at, r):
        g = s_ref[0, pl.ds(at, r), :]
        for j in range(1, slots.shape[0]):
            g = g + s_ref[j, pl.ds(at, r), :]
        return g

    def body(s_ref, *refs):
        ins, outs = refs[:3 * n_parts], refs[3 * n_parts:]
        at = 0
        for i, r in enumerate(rows):
            g = total(s_ref, at, r)
            w_ref, m_ref, v_ref = ins[3 * i:3 * i + 3]
            g_out, d_out, m_out, v_out = outs[4 * i:4 * i + 4]
            g_out[...] = g
            d_out[...], m_out[...], v_out[...] = _adam_update(g, w_ref[...], m_ref[...], v_ref[...])
            at += r
        outs[4 * n_parts][...] = total(s_ref, at, rest)

    shapes = [jax.ShapeDtypeStruct(p[0].shape, F32) for p in parts for _ in range(4)]
    res = pl.pallas_call(
        body, name=name, out_shape=shapes + [jax.ShapeDtypeStruct((rest, slots.shape[2]), F32)],
        compiler_params=pltpu.CompilerParams(vmem_limit_bytes=VMEM_LIMIT),
    )(slots, *[t for p in parts for t in p])
    return [res[4 * i:4 * i + 4] for i in range(n_parts)], res[4 * n_parts]


def _adamw_whole(name, items):
    n_items = len(items)

    def body(*refs):
        ins, outs = refs[:4 * n_items], refs[4 * n_items:]
        for i in range(n_items):
            s_ref, w_ref, m_ref, v_ref = ins[4 * i:4 * i + 4]
            g_out, d_out, m_out, v_out = outs[4 * i:4 * i + 4]
            g = s_ref[0].astype(F32)
            for j in range(1, s_ref.shape[0]):
                g = g + s_ref[j].astype(F32)
            g_out[0] = g
            d_out[0], m_out[0], v_out[0] = _adam_update(g, w_ref[0], m_ref[0], v_ref[0])

    res = pl.pallas_call(
        body, name=name, out_shape=[jax.ShapeDtypeStruct(item[1].shape, F32) for item in items for _ in range(4)],
        compiler_params=pltpu.CompilerParams(vmem_limit_bytes=VMEM_LIMIT),
    )(*[t for item in items for t in item])
    return [res[4 * i:4 * i + 4] for i in range(n_items)]


def _adamw(name, slots, w, m, v, tr=256):
    unit_mid = w.ndim == 3 and w.shape[1] == 1 and w.shape[0] > 1
    R, Wd = (w.shape[0], w.shape[2]) if unit_mid else w.shape[-2:]
    depth_axis = w.ndim == 3 and not unit_mid
    if unit_mid:
        tr, tc = 128, Wd
    elif R % tr == 0:
        tc = Wd
    else:
        tr, tc = R, (256 if (Wd % 256 == 0 and R > 256) else Wd)
    at = (slice(None), 0, slice(None)) if unit_mid else Ellipsis

    def body(s_ref, w_ref, m_ref, v_ref, g_out, d_out, m_out, v_out):
        g = s_ref[0].astype(F32)
        for j in range(1, slots.shape[0]):
            g = g + s_ref[j].astype(F32)
        g_out[at] = g
        d_out[at], m_out[at], v_out[at] = _adam_update(g, w_ref[at], m_ref[at], v_ref[at])

    if unit_mid:
        row = pl.BlockSpec((tr, 1, tc), lambda i, j: (i, 0, j))
    elif depth_axis:
        row = pl.BlockSpec((None, tr, tc), lambda i, j: (0, i, j))
    else:
        row = pl.BlockSpec((tr, tc), lambda i, j: (i, j))
    return pl.pallas_call(
        body, name=name, grid=(pl.cdiv(R, tr), Wd // tc),
        in_specs=[pl.BlockSpec((slots.shape[0], tr, tc), lambda i, j: (0, i, j)), row, row, row],
        out_specs=[row] * 4, out_shape=[jax.ShapeDtypeStruct(w.shape, F32)] * 4,
        compiler_params=pltpu.CompilerParams(dimension_semantics=("parallel", "parallel"),
                                             vmem_limit_bytes=VMEM_LIMIT),
    )(slots, w, m, v)


PACK_W = 1024
PACKED = [(n, s) for n, s in REPLICATED if n != "sgu_w"]
_SMALL_SIZES = [int(np.prod(s)) for _, s in PACKED]
_SMALL_ROWS = _round_up(_round_up(sum(_SMALL_SIZES) + PACK_W, PACK_W) // PACK_W, 8)
assert all(size % PACK_W == 0 for size in _SMALL_SIZES)
W_IN_SHARD = P_TOTAL // N_DEV


def _pack_rows(name, parts, rows):
    parts = [p.reshape(-1, PACK_W) for p in parts]
    assert all(p.dtype == F32 for p in parts)

    def body(*refs):
        at = 0
        for ref in refs[:-1]:
            refs[-1][pl.ds(at, ref.shape[0]), :] = ref[...]
            at += ref.shape[0]
        refs[-1][pl.ds(at, rows - at), :] = jnp.zeros((rows - at, PACK_W), F32)

    return pl.pallas_call(body, name=name, out_shape=jax.ShapeDtypeStruct((rows, PACK_W), F32))(*parts)


_W_IN_SEGMENTS = [(2 * D, 0, 0), (3 * D, 1, 0), (L_W, 1, 3 * D), (L_A, 1, 3 * D + 128), (L_G, 1, 3 * D + 256),
                  (2 * D, 2, 0)]
_W_IN_PADS = [(3 * D + L_W, 128 - L_W), (3 * D + 128 + L_A, 128 - L_A), (3 * D + 256 + L_G, 256 - L_G)]
_W_IN_GROUP_ROWS = [2 * D, 3 * D + 128 + 128 + 256, 2 * D]
assert sum(rows for rows, _, _ in _W_IN_SEGMENTS) == P_TOTAL


def _w_in_pieces(j):
    pieces, first = [], 0
    for rows, group, to in _W_IN_SEGMENTS:
        lo, hi = max(first, W_IN_SHARD * j), min(first + rows, W_IN_SHARD * (j + 1))
        if lo < hi:
            pieces.append((lo - W_IN_SHARD * j, hi - lo, group, to + lo - first))
        first += rows
    return pieces


def _w_in_groups_t(blocks):
    def body(x_ref, *o_refs):
        @pl.when(pl.program_id(0) == 0)
        def _():
            for at, rows in _W_IN_PADS:
                o_refs[1][pl.ds(at, rows), :] = jnp.zeros((rows, D), blocks.dtype)
        for j in range(N_DEV):
            @pl.when(pl.program_id(0) == j)
            def _(j=j):
                for at, rows, group, to in _w_in_pieces(j):
                    o_refs[group][pl.ds(to, rows), :] = x_ref[pl.ds(at, rows), :]

    return pl.pallas_call(
        body, name="w_in_groups", grid=(N_DEV,),
        in_specs=[pl.BlockSpec((None, W_IN_SHARD, D), lambda j: (j, 0, 0))],
        out_specs=[pl.BlockSpec((r, D), lambda j: (0, 0)) for r in _W_IN_GROUP_ROWS],
        out_shape=[jax.ShapeDtypeStruct((r, D), blocks.dtype) for r in _W_IN_GROUP_ROWS],
        compiler_params=pltpu.CompilerParams(dimension_semantics=("arbitrary",), vmem_limit_bytes=VMEM_LIMIT),
    )(blocks)


def _w_in_grad_blocks(g_sgu_t, g_rw_t, g_gate_t):
    def body(*refs):
        g_refs, o_ref = refs[:3], refs[3]
        for j in range(N_DEV):
            @pl.when(pl.program_id(0) == j)
            def _(j=j):
                for at, rows, group, to in _w_in_pieces(j):
                    o_ref[pl.ds(at, rows), :] = g_refs[group][pl.ds(to, rows), :]

    return pl.pallas_call(
        body, name="w_in_grad_blocks", grid=(N_DEV,),
        in_specs=[pl.BlockSpec((r, D), lambda j: (0, 0)) for r in _W_IN_GROUP_ROWS],
        out_specs=pl.BlockSpec((None, W_IN_SHARD, D), lambda j: (j, 0, 0)),
        out_shape=jax.ShapeDtypeStruct((N_DEV, W_IN_SHARD, D), g_sgu_t.dtype),
        compiler_params=pltpu.CompilerParams(dimension_semantics=("arbitrary",), vmem_limit_bytes=VMEM_LIMIT),
    )(g_sgu_t, g_rw_t, g_gate_t)


def _mesh_index():
    me = 4 * lax.axis_index("x") + 2 * lax.axis_index("y") + lax.axis_index("c")
    return me.astype(jnp.int32).reshape(1)


def _fill_slot(name, dst, src, idx):
    R, Wd = dst.shape[1:]

    def body(idx_ref, dst_ref, src_ref, out_ref):
        out_ref[...] = src_ref[...]

    return pl.pallas_call(
        body, name=name,
        grid_spec=pltpu.PrefetchScalarGridSpec(
            num_scalar_prefetch=1, grid=(1,),
            in_specs=[pl.BlockSpec(memory_space=pl.ANY), pl.BlockSpec((R, Wd), lambda i, s: (0, 0))],
            out_specs=pl.BlockSpec((None, R, Wd), lambda i, s: (s[0], 0, 0))),
        out_shape=jax.ShapeDtypeStruct(dst.shape, dst.dtype),
        input_output_aliases={1: 0},
        compiler_params=pltpu.CompilerParams(vmem_limit_bytes=VMEM_LIMIT),
    )(idx, dst, src)


class TwoLevelGather:
    N_COPIES = 8
    PART_ALIGN = 16

    def __init__(self, bufs, skip_own=()):
        self.bufs, self.nb, self.skip_own = list(bufs), len(bufs), tuple(skip_own)
        self.any_specs = [pl.BlockSpec(memory_space=pl.ANY)] * self.nb
        self.out_shape = [jax.ShapeDtypeStruct((N_DEV,) + b.shape, b.dtype) for b in self.bufs]
        n = self.N_COPIES * self.nb
        self.sem_shapes = [pltpu.SemaphoreType.DMA((n,)), pltpu.SemaphoreType.DMA((n,)),
                           pltpu.SemaphoreType.DMA((self.nb,))]

    def _parts(self, b):
        shape = self.bufs[b].shape
        first = shape[0] // 2 // self.PART_ALIGN * self.PART_ALIGN if len(shape) == 2 else 0
        return [(0, first), (first, shape[0] - first)] if first else [None]

    def _copies(self, in_refs, out_refs, sems):
        send_sems, recv_sems, local_sems = sems
        nb = self.nb
        x, y, c = lax.axis_index("x"), lax.axis_index("y"), lax.axis_index("c")
        me, sibling = (x, y, c), (x, y, 1 - c)
        near, far = [(1 - x, y), (x, 1 - y)], (1 - x, 1 - y)

        def slot(b, dev):
            return out_refs[b].at[4 * dev[0] + 2 * dev[1] + dev[2]]

        def copy(b, k, block, to, own=False, rows=None):
            src, dst = in_refs[b] if own else slot(b, block), slot(b, block)
            if rows is not None:
                src, dst = src.at[pl.ds(*rows)], dst.at[pl.ds(*rows)]
            return pltpu.make_async_remote_copy(
                src_ref=src, dst_ref=dst, send_sem=send_sems.at[self.N_COPIES * b + k],
                recv_sem=recv_sems.at[self.N_COPIES * b + k], device_id=to, device_id_type=pl.DeviceIdType.MESH)

        ways = [(j, b) for j in range(2) for b in range(nb) if j < len(self._parts(b))]
        cp = {}
        cp["local"] = [pltpu.make_async_copy(in_refs[b], slot(b, me), local_sems.at[b]) for b in range(nb)
                       if b not in self.skip_own]
        cp["first"] = [copy(b, 0, me, sibling, own=True) for b in range(nb)]
        cp["first"] += [copy(b, 1 + j, me, (*near[j], c), own=True) for j in range(2) for b in range(nb)]
        cp["from_near"] = [copy(b, 1 + j, (*near[j], c), me) for j in range(2) for b in range(nb)]
        cp["near_on"] = [[copy(b, 5 + j, (*near[j], c), sibling)]
                         + ([copy(b, 3 + j, (*near[j], c), (*near[1 - j], c), rows=self._parts(b)[j])]
                            if (j, b) in ways else []) for j in range(2) for b in range(nb)]
        cp["from_far"] = [copy(b, 3 + j, (*far, c), me, rows=self._parts(b)[j]) for j, b in ways]
        cp["far_on"] = [copy(b, 7, (*far, c), sibling) for b in range(nb)]
        cp["from_sibling"] = [copy(b, 0, sibling, me) for b in range(nb)]
        cp["from_sibling"] += [copy(b, 5 + j, (*near[j], 1 - c), me) for j in range(2) for b in range(nb)]
        cp["from_sibling"] += [copy(b, 7, (*far, 1 - c), me) for b in range(nb)]
        return cp

    def start(self, in_refs, out_refs, sems):
        cp = self._copies(in_refs, out_refs, sems)
        for c in cp["first"] + cp["local"]:
            c.start()

    def pass_near(self, in_refs, out_refs, sems):
        cp = self._copies(in_refs, out_refs, sems)
        for arrived, onward in zip(cp["from_near"], cp["near_on"]):
            arrived.wait_recv()
            for c in onward:
                c.start()

    def pass_far(self, in_refs, out_refs, sems):
        cp = self._copies(in_refs, out_refs, sems)
        for c in cp["from_far"]:
            c.wait_recv()
        for c in cp["far_on"]:
            c.start()

    def finish(self, in_refs, out_refs, sems):
        cp = self._copies(in_refs, out_refs, sems)
        for c in cp["from_sibling"]:
            c.wait_recv()
        for c in cp["first"] + sum(cp["near_on"], []) + cp["far_on"]:
            c.wait_send()
        for c in cp["local"]:
            c.wait()

    def schedule(self, n_steps):
        return [(0, self.start), (max(n_steps // 2 - 1, 0), self.pass_near), (max(n_steps - 3, 0), self.pass_far),
                (n_steps - 1, self.finish)]


def _all_gather_two_level(name, bufs, skip_own=()):
    ex = TwoLevelGather(bufs, skip_own)

    def body(*refs):
        args = refs[:ex.nb], refs[ex.nb:2 * ex.nb], refs[2 * ex.nb:]
        for _, action in ex.schedule(1):
            action(*args)

    return pl.pallas_call(body, name=name, in_specs=ex.any_specs, out_specs=ex.any_specs, out_shape=ex.out_shape,
                          scratch_shapes=ex.sem_shapes)(*ex.bufs)


def _cols_from_blocks(blk):
    return jnp.transpose(blk, (1, 0, 2)).reshape(blk.shape[1], -1)


def _cols_to_blocks(g):
    r, c = g.shape
    return jnp.transpose(g.reshape(r, N_DEV, c // N_DEV), (1, 0, 2))


FIRST_WEIGHTS = ["w_in", "shift_b", "w_lora_w", "a_lora_w", "g_lora_w"]
LATE_WEIGHTS = ["w_proj_a", "w_proj_b", "w_out", "w_ffn1", "w_ffn2"]
SCAN_CARRIED = ["w_proj_a", "w_proj_b", "w_out"]
FFN_WEIGHTS = ["w_ffn1", "w_ffn2"]


def _late_weights(shards):
    ex = TwoLevelGather([shards[n][0].astype(BF16) for n in LATE_WEIGHTS])

    def finish(results):
        got = dict(zip(LATE_WEIGHTS, results))
        W = {n: got[n].reshape(-1, D) for n in ("w_proj_a", "w_proj_b", "w_out", "w_ffn2")}
        W["w_ffn1"] = got["w_ffn1"].reshape(N_DEV, D, -1)
        return W
    return ex, finish


def _gather_weights(shards):
    def payload(n):
        if n == "w_in":
            return jnp.transpose(shards[n][0]).astype(BF16)
        return shards[n] if n == "shift_b" else shards[n].astype(BF16)
    payloads = [payload(n) for n in FIRST_WEIGHTS]
    got = dict(zip(FIRST_WEIGHTS, _all_gather_two_level("weight_all_gather", payloads, skip_own=(0,))))
    got["w_in"] = _fill_slot("w_in_own_slot", got["w_in"], payloads[0], _mesh_index())
    W = {}
    W["w_sgu_t"], W["w_rw_t"], W["w_gate_t"] = _w_in_groups_t(got["w_in"])
    z = lambda r, c, dt: jnp.zeros((r, c), dt)
    W["w_lora"] = jnp.concatenate([_cols_from_blocks(got["w_lora_w"][:, 0]).astype(F32), z(128 - L_W, D, F32)], axis=0)
    W["a_lora"] = jnp.concatenate([_cols_from_blocks(got["a_lora_w"][:, 0]).astype(F32), z(128 - L_A, D, F32)], axis=0)
    W["g_lora"] = jnp.concatenate([_cols_from_blocks(got["g_lora_w"][:, 0]).astype(F32), z(256 - L_G, D, F32)], axis=0)
    sb = _cols_from_blocks(got["shift_b"][:, 0])
    W["sb"] = jnp.concatenate([sb[:, :3 * D], sb[:, 3 * D:3 * D + L_W], z(2, 128 - L_W, F32),
                               sb[:, 3 * D + L_W:3 * D + L_W + L_A], z(2, 128 - L_A, F32),
                               sb[:, 3 * D + L_W + L_A:], z(2, 256 - L_G, F32)], axis=1)
    return W


def _replicated_weights(rep):
    W = {n: rep[n] for n in ("g_mix", "sgu_ln_w", "sgu_ln_b", "w0", "a0", "k_k", "k_a", "r_k", "ln_x_w", "ln_x_b",
                             "g_ffn")}
    W["g_final"] = rep["g_final"].reshape(1, D)
    W["sgu_w"] = rep["sgu_w"][0]
    W["sgu_bt"] = jnp.transpose(rep["sgu_b"][0])
    return W


def _late_grad_blocks(G):
    return Exchange([G[n].reshape(N_DEV, -1, D) for n in SCAN_CARRIED]
                    + [G["sgu_w"].reshape(SGU_G * SGU_C, SGU_C).astype(GRAD_PAYLOAD)],
                    [False] * len(SCAN_CARRIED) + [True])


def _first_grad_blocks(G):
    sbg = G["sb"]
    c = 3 * D
    sb = jnp.concatenate([sbg[:, :c], sbg[:, c:c + L_W], sbg[:, c + 128:c + 128 + L_A],
                          sbg[:, c + 256:c + 256 + L_G]], axis=1)
    return {
        "shift_b": _cols_to_blocks(sb),
        "w_lora_w": _cols_to_blocks(G["w_lora"][:L_W]), "a_lora_w": _cols_to_blocks(G["a_lora"][:L_A]),
        "g_lora_w": _cols_to_blocks(G["g_lora"][:L_G]),
    }


def _replicated_grads(G):
    small = {n: G[n] for n in ("g_mix", "sgu_ln_w", "sgu_ln_b", "w0", "a0", "k_k", "k_a", "r_k", "ln_x_w", "ln_x_b",
                               "g_ffn", "g_final")}
    small["sgu_w"] = G["sgu_w"]
    small["sgu_b"] = jnp.transpose(G["sgu_bt"])
    return small


def kernel(x, g_mix, w_in, sgu_ln_w, sgu_ln_b, sgu_w, sgu_b, w_proj_a, shift_b, w_lora_w, w0, a_lora_w, a0, g_lora_w, k_k, k_a, r_k, ln_x_w, ln_x_b, w_proj_b, w_out, g_ffn, w_ffn1, w_ffn2, g_final, loss_target, m_g_mix, m_w_in, m_sgu_ln_w, m_sgu_ln_b, m_sgu_w, m_sgu_b, m_w_proj_a, m_shift_b, m_w_lora_w, m_w0, m_a_lora_w, m_a0, m_g_lora_w, m_k_k, m_k_a, m_r_k, m_ln_x_w, m_ln_x_b, m_w_proj_b, m_w_out, m_g_ffn, m_w_ffn1, m_w_ffn2, m_g_final, v_g_mix, v_w_in, v_sgu_ln_w, v_sgu_ln_b, v_sgu_w, v_sgu_b, v_w_proj_a, v_shift_b, v_w_lora_w, v_w0, v_a_lora_w, v_a0, v_g_lora_w, v_k_k, v_k_a, v_r_k, v_ln_x_w, v_ln_x_b, v_w_proj_b, v_w_out, v_g_ffn, v_w_ffn1, v_w_ffn2, v_g_final):
    env = dict(locals())
    weights = {n: env[n] for n in WEIGHT_ORDER}
    moms = {n: env["m_" + n] for n in WEIGHT_ORDER}
    vars_ = {n: env["v_" + n] for n in WEIGHT_ORDER}

    shards = {n: weights[n] for n, _, _ in SHARDED}
    W = _gather_weights(shards)
    W.update(_replicated_weights({n: weights[n] for n, _ in REPLICATED}))
    in_flight = {}

    def send_w_in_grads(G):
        blocks = _w_in_grad_blocks(G["w_sgu_t"], G["w_rw_t"], G["w_gate_t"])
        got = _pair_exchange("grad_pair_exchange", blocks)
        core = lax.axis_index("c").astype(jnp.int32).reshape(1)
        sums = _pair_sum("grad_pair_sum", blocks, got, core)
        in_flight["sems"], in_flight["sums"], in_flight["land"], token = _chip_exchange_start("grad_chip_start", sums)
        return token

    def send_ffn_grads(G):
        in_flight["ffn"] = _scatter_start("grad_ffn_start", [G["w_ffn1"], G["w_ffn2"].reshape(N_DEV, -1, D)])
        return in_flight["ffn"][3]

    loss_part, dx, G, late_slots = _local_step(x[0], loss_target[0], W, late_weights=_late_weights(shards),
                                               early_grads=_late_grad_blocks, w_in_grads_ready=send_w_in_grads,
                                               ffn_grads_ready=send_ffn_grads)

    slots = dict(zip(SCAN_CARRIED, late_slots))
    _, landed = _scatter_wait("grad_ffn_wait", *in_flight["ffn"][:3], after=dx)
    slots.update(zip(FFN_WEIGHTS, landed))
    blocks = _first_grad_blocks(G)
    small = _replicated_grads(G)
    small_parts = [small[n] for n, _ in PACKED] + [jnp.full((PACK_W,), loss_part, F32)]
    rest = [n for n in FIRST_WEIGHTS if n != "w_in"]
    res = _exchange("grad_exchange", [blocks[n] for n in rest] + [_pack_rows("grad_pack", small_parts, _SMALL_ROWS)],
                    [False] * len(rest) + [True])
    slots.update(zip(rest, res[:-1]))
    small_slots = res[-1]
    _, slots["w_in"] = _chip_exchange_wait("grad_chip_wait", in_flight["sems"], in_flight["sums"], in_flight["land"],
                                           after=small_slots)

    outs = [dict(), dict(), dict(), dict()]
    for n in ["w_in"] + FFN_WEIGHTS:
        if n == "w_in":
            res = _adamw("adamw_" + n, slots[n], *[jnp.transpose(t, (2, 0, 1)) for t in (weights[n], moms[n], vars_[n])])
            res = [jnp.transpose(t, (1, 2, 0)) for t in res]
        else:
            res = _adamw("adamw_" + n, slots[n], weights[n], moms[n], vars_[n])
        for k in range(4):
            outs[k][n] = res[k]
    for name, group in (("adamw_projections", SCAN_CARRIED), ("adamw_low_rank", rest)):
        group_out = _adamw_whole(name, [(slots[n], weights[n], moms[n], vars_[n]) for n in group])
        for n, res in zip(group, group_out):
            for k in range(4):
                outs[k][n] = res[k]

    sgu_shape = (SGU_G * SGU_C, SGU_C)
    res = _adamw("adamw_sgu_w", late_slots[len(SCAN_CARRIED)], *[t.reshape(sgu_shape) for t in
                                                                  (weights["sgu_w"], moms["sgu_w"], vars_["sgu_w"])])
    for k in range(4):
        outs[k]["sgu_w"] = res[k].reshape(weights["sgu_w"].shape)

    small_out, after_them = _adamw_rows(
        "adamw_replicated", small_slots,
        [[d[n].reshape(-1, PACK_W) for d in (weights, moms, vars_)] for n, _ in PACKED])
    for (n, s), res in zip(PACKED, small_out):
        for k in range(4):
            outs[k][n] = res[k].reshape(s)
    loss = after_them[0, 0]
    return (loss, dx[None], *[outs[0][n] for n in WEIGHT_ORDER], *[outs[1][n] for n in WEIGHT_ORDER],
            *[outs[2][n] for n in WEIGHT_ORDER], *[outs[3][n] for n in WEIGHT_ORDER])
```

```python
import functools
import numpy as np
import jax
import jax.numpy as jnp
from jax import lax
from jax.experimental import pallas as pl
from jax.experimental.pallas import tpu as pltpu

F32 = jnp.float32
BF16 = jnp.bfloat16

D = 1024
NH, HN = 16, 64
NP, PW = NH // 2, 2 * HN
SGU_G, SGU_C = 8, 128
L_W, L_A, L_G = 64, 64, 160
C_B = 3 * D + L_W + L_A + L_G
P_TOTAL = 2 * D + C_B + 2 * D
D_FF = 4 * D
RW_INT = 3 * D + 128 + 128 + 256
NORM_EPS, LN_EPS, GN_EPS = 1e-6, 1e-5, 64e-5
N_DEV = 8
LANES = 128
SCAN_C = 64
N_KEPT = 4
SOLVE_B = 16
SCAN_PRECISION = lax.Precision.HIGH
SCAN_OUT_PRECISION = lax.Precision.DEFAULT
GRAD_PAYLOAD = BF16
VMEM_LIMIT = 56 * 1024 * 1024
MATMUL_VMEM_BUDGET = 40 * 1024 * 1024
STEP_COST_BYTES = 512 * 1024
HBM_COST_RATIO = 3
ACC_PASS_WEIGHT = 4

ADAM_LR, ADAM_B1, ADAM_B2, ADAM_EPS, ADAM_WD, ADAM_STEP = 0.001, 0.9, 0.999, 1e-08, 0.01, 10

SHARDED = [
    ("w_in", (D, P_TOTAL), 1), ("w_proj_a", (D, D), 0), ("shift_b", (2, C_B), 1), ("w_lora_w", (L_W, D), 1),
    ("a_lora_w", (L_A, D), 1), ("g_lora_w", (L_G, D), 1), ("w_proj_b", (D, D), 0), ("w_out", (D, D), 0),
    ("w_ffn1", (D, D_FF), 1), ("w_ffn2", (D_FF, D), 0),
]
REPLICATED = [
    ("g_mix", (1, D)), ("sgu_ln_w", (1, D)), ("sgu_ln_b", (1, D)), ("sgu_w", (1, SGU_G, SGU_C, SGU_C)),
    ("sgu_b", (1, SGU_G, SGU_C)), ("w0", (1, D)), ("a0", (1, D)), ("k_k", (1, D)), ("k_a", (1, D)), ("r_k", (1, D)),
    ("ln_x_w", (1, D)), ("ln_x_b", (1, D)), ("g_ffn", (1, D)), ("g_final", (D,)),
]
WEIGHT_ORDER = ["g_mix", "w_in", "sgu_ln_w", "sgu_ln_b", "sgu_w", "sgu_b", "w_proj_a", "shift_b", "w_lora_w", "w0",
                "a_lora_w", "a0", "g_lora_w", "k_k", "k_a", "r_k", "ln_x_w", "ln_x_b", "w_proj_b", "w_out", "g_ffn",
                "w_ffn1", "w_ffn2", "g_final"]


def _round_up(n, m):
    return (n + m - 1) // m * m


def _pick(n, target):
    if n <= target:
        return n
    best = None
    for t in range(LANES, target + 1, LANES):
        if n % t == 0:
            best = t
    assert best is not None, (n, target)
    return best


def _matmul(name, a, b, mode, out_dtype=F32, tm=2048, tn=1024, tk=4096, out_blocks=None, epilogue=None, extras=(),
            out_dtypes=(), after=None, whole_rows=False, out_widths=None):
    b_blocks = b.shape[0] if b.ndim == 3 else None
    bshape = b.shape if b.ndim == 2 else (b.shape[1], b.shape[0] * b.shape[2])
    if mode == "nn":
        (M, K), (K2, N) = a.shape, bshape
    elif mode == "nt":
        (M, K), (N, K2) = a.shape, bshape
    else:
        (K, M), (K2, N) = a.shape, bshape
    assert K == K2, (name, a.shape, b.shape)
    assert b_blocks is None or mode != "tn"
    assert out_blocks is None or mode == "tn"
    tn = min(tn, N // (out_blocks or 1), bshape[1] // b_blocks if (b_blocks and mode == "nn") else tn)
    blocked_k = bool(b_blocks) and mode == "nt"
    tm, tn, tk = _pick(M, tm), _pick(N, tn), (K if blocked_k else _pick(K, tk))

    def vmem_bytes(tm, tk):
        tiles = tm * tk * a.dtype.itemsize + tk * tn * b.dtype.itemsize
        for i, dt in enumerate(out_dtypes if epilogue else (out_dtype,)):
            tiles += tm * (out_widths[i] if out_widths else tn) * jnp.dtype(dt).itemsize
        for x in extras:
            arr = x[0] if isinstance(x, tuple) else x
            tiles += (tm if arr.shape[0] > 1 else 1) * tn * arr.dtype.itemsize
        return 2 * tiles + (tm * tn * 4 if K // tk > 1 else 0)

    def cost(tm, tk):
        ni, nj, nk = M // tm, N // tn, K // tk
        steps = ni * nj * nk
        acc_passes = steps * tm * tn * 8 * ACC_PASS_WEIGHT if nk > 1 else 0
        a_reads = M * K * a.dtype.itemsize * (nj if nk > 1 else 1)
        b_reads = K * N * b.dtype.itemsize * (ni if (nj > 1 or nk > 1) else 1)
        return (steps * STEP_COST_BYTES + acc_passes + vmem_bytes(tm, tk) // 2
                + HBM_COST_RATIO * (a_reads + b_reads))

    options = [(m, k) for m in ({M} if whole_rows else {_pick(M, max(t, LANES)) for t in (tm, tm // 2, tm // 4)})
               for k in ({K} if blocked_k else {_pick(K, max(t, LANES)) for t in (tk, tk // 2, tk // 4)})
               if vmem_bytes(m, k) <= MATMUL_VMEM_BUDGET]
    tm, tk = min(options, key=lambda o: cost(*o))
    nk = K // tk
    dims = {"nn": (((1,), (0,)), ((), ())), "nt": (((1,), (1,)), ((), ())), "tn": (((0,), (0,)), ((), ()))}[mode]

    n_x, n_o = len(extras), len(out_dtypes) if epilogue else 1
    n_after = 0 if after is None else 1

    def body(a_ref, b_ref, *rest):
        x_refs, o_refs, acc = rest[:n_x], rest[n_x + n_after:n_x + n_after + n_o], rest[n_x + n_after + n_o:]
        if blocked_k:
            bw = b.shape[2]
            part = sum(lax.dot_general(a_ref[:, blk * bw:(blk + 1) * bw].astype(BF16), b_ref[blk].astype(BF16), dims,
                                       preferred_element_type=F32) for blk in range(b_blocks))
        else:
            part = lax.dot_general(a_ref[...].astype(BF16), b_ref[...].astype(BF16), dims, preferred_element_type=F32)

        def finish(res):
            outs = epilogue(res, *[r[...] for r in x_refs]) if epilogue else (res,)
            for r, v in zip(o_refs, outs):
                r[...] = v.astype(r.dtype)

        if nk == 1:
            finish(part)
            return
        acc_ref, k = acc[0], pl.program_id(2)

        @pl.when(k == 0)
        def _():
            acc_ref[...] = part

        @pl.when(k > 0)
        def _():
            acc_ref[...] += part

        @pl.when(k == nk - 1)
        def _():
            finish(acc_ref[...])

    a_spec = {"nn": pl.BlockSpec((tm, tk), lambda i, j, k: (i, k)), "nt": pl.BlockSpec((tm, tk), lambda i, j, k: (i, k)),
              "tn": pl.BlockSpec((tk, tm), lambda i, j, k: (k, i))}[mode]
    b_spec = {"nn": pl.BlockSpec((tk, tn), lambda i, j, k: (k, j)), "nt": pl.BlockSpec((tn, tk), lambda i, j, k: (j, k)),
              "tn": pl.BlockSpec((tk, tn), lambda i, j, k: (k, j))}[mode]
    if b_blocks and mode == "nn":
        per = b.shape[2] // tn
        b_spec = pl.BlockSpec((None, tk, tn), lambda i, j, k: (j // per, k, j % per))
    elif b_blocks:
        b_spec = pl.BlockSpec((b_blocks, tn, b.shape[2]), lambda i, j, k: (0, j, 0))
    out_spec = pl.BlockSpec((tm, tn), lambda i, j, k: (i, j))
    out_shape = jax.ShapeDtypeStruct((M, N), out_dtype)
    if out_blocks:
        per_o = N // out_blocks // tn
        out_spec = pl.BlockSpec((None, tm, tn), lambda i, j, k: (j // per_o, i, j % per_o))
        out_shape = jax.ShapeDtypeStruct((out_blocks, M, N // out_blocks), out_dtype)
    epi_widths = list(out_widths) if out_widths else [N] * n_o
    assert all(w == N for w in epi_widths) or N == tn
    epi_specs = [pl.BlockSpec((tm, tn if w == N else w), lambda i, j, k: (i, j)) for w in epi_widths]
    x_specs, x_args = [], []
    for x in extras:
        arr, off = x if isinstance(x, tuple) else (x, 0)
        if arr.shape[0] == 1:
            x_specs.append(pl.BlockSpec((1, tn), lambda i, j, k: (0, j)))
        else:
            x_specs.append(pl.BlockSpec((tm, tn), lambda i, j, k, off=off: (i, j + off)))
        x_args.append(arr)
    res = pl.pallas_call(
        body, name=name, grid=(M // tm, N // tn, nk),
        in_specs=[a_spec, b_spec] + x_specs + [pl.BlockSpec(memory_space=pl.ANY)] * n_after,
        out_specs=epi_specs if epilogue else out_spec,
        out_shape=[jax.ShapeDtypeStruct((M, w), dt) for w, dt in zip(epi_widths, out_dtypes)] if epilogue else out_shape,
        scratch_shapes=[pltpu.VMEM((tm, tn), F32)] if nk > 1 else [],
        compiler_params=pltpu.CompilerParams(dimension_semantics=("parallel", "parallel", "arbitrary"),
                                             vmem_limit_bytes=VMEM_LIMIT),
    )(a, b, *x_args, *([after] if n_after else []))
    return res


class Rows:
    def __init__(self, arr, width=None, cb=0):
        self.arr, self.width, self.cb = arr, (arr.shape[1] if width is None else width), cb


class Heads:
    def __init__(self, arr):
        self.arr = arr


class Halo:
    def __init__(self, arr, side):
        self.arr, self.side = arr, side


def _rows_call(name, fn, ins, consts, outs, accs=(), tm=512, with_pid=False):
    T = next(o.arr.shape[1] if isinstance(o, Heads) else o.arr.shape[0] for o in ins if not isinstance(o, Halo))
    tm = min(tm, T)
    n_tiles = T // tm
    n_in, n_c, n_out = len(ins), len(consts), len(outs)
    in_specs, args = [], []
    for o in ins:
        if isinstance(o, Rows):
            in_specs.append(pl.BlockSpec((tm, o.width), lambda i, cb=o.cb: (i, cb)))
        elif isinstance(o, Heads):
            in_specs.append(pl.BlockSpec((NP, tm, PW), lambda i: (0, i, 0)))
        else:
            w = o.arr.shape[1]
            if o.side < 0:
                in_specs.append(pl.BlockSpec((8, w), lambda i: (jnp.maximum(i * (tm // 8) - 1, 0), 0)))
            else:
                in_specs.append(pl.BlockSpec((8, w), lambda i: (jnp.minimum((i + 1) * (tm // 8), T // 8 - 1), 0)))
        args.append(o.arr)
    for c in consts:
        in_specs.append(pl.BlockSpec(c.shape, lambda i, nd=c.ndim: (0,) * nd))
        args.append(c)
    out_specs, out_shape = [], []
    for o in outs:
        if o[0] == "rows":
            out_specs.append(pl.BlockSpec((tm, o[1]), lambda i: (i, 0)))
            out_shape.append(jax.ShapeDtypeStruct((T, o[1]), o[2]))
        else:
            out_specs.append(pl.BlockSpec((NP, tm, PW), lambda i: (0, i, 0)))
            out_shape.append(jax.ShapeDtypeStruct((NP, T, PW), o[1]))
    for shape, dt in accs:
        out_specs.append(pl.BlockSpec(shape, lambda i, nd=len(shape): (0,) * nd))
        out_shape.append(jax.ShapeDtypeStruct(shape, dt))

    def body(*refs):
        i = pl.program_id(0)
        vals = []
        vals = [r[...] for r in refs[:n_in + n_c]]
        res = fn(i, n_tiles, *vals) if with_pid else fn(*vals)
        out_refs = refs[n_in + n_c:]
        for r, v in zip(out_refs[:n_out], res[:n_out]):
            r[...] = v.astype(r.dtype)
        if accs:
            @pl.when(i == 0)
            def _():
                for r in out_refs[n_out:]:
                    r[...] = jnp.zeros_like(r)

            for r, v in zip(out_refs[n_out:], res[n_out:]):
                r[...] += v.astype(r.dtype)

    res = pl.pallas_call(
        body, name=name, grid=(n_tiles,), in_specs=in_specs, out_specs=out_specs, out_shape=out_shape,
        compiler_params=pltpu.CompilerParams(dimension_semantics=("arbitrary",), vmem_limit_bytes=VMEM_LIMIT),
    )(*args)
    return res


def _rms(x, g):
    return x * lax.rsqrt(jnp.mean(x * x, axis=-1, keepdims=True) + NORM_EPS) * g


def _gelu(x):
    return 0.5 * x * (1.0 + lax.erf(x * 0.7071067811865476))


def _sigmoid(x):
    return 1.0 / (1.0 + jnp.exp(-x))


def _bdot(a, b):
    return jnp.dot(a.astype(BF16), b.astype(BF16), preferred_element_type=F32)


def _to_heads(x):
    return jnp.concatenate([x[:, p * PW:(p + 1) * PW][None] for p in range(NP)], axis=0)


def _from_heads(xp):
    return jnp.concatenate([xp[p] for p in range(NP)], axis=-1)


def _head_sum(xp):
    low = lax.broadcasted_iota(jnp.int32, xp.shape, xp.ndim - 1) < HN
    both = jnp.sum(xp, axis=-1, keepdims=True)
    first = jnp.sum(jnp.where(low, xp, 0.0), axis=-1, keepdims=True)
    return jnp.where(low, first, both - first)


def _split_pairs(xp):
    return jnp.concatenate([xp[:, :, :HN], xp[:, :, HN:]], axis=0)


def _join_pairs(xh):
    return jnp.concatenate([xh[:NP], xh[NP:]], axis=-1)


def _sgu_fn(p, ln_w, ln_b, sw, sbt):
    z = _gelu(p)
    u, v = z[:, :D], z[:, D:]
    mu = jnp.mean(v, axis=-1, keepdims=True)
    var = jnp.mean(jnp.square(v - mu), axis=-1, keepdims=True)
    vn = (v - mu) * lax.rsqrt(var + LN_EPS) * ln_w + ln_b
    ri = lax.broadcasted_iota(jnp.int32, (SGU_C, SGU_C), 0)
    ci = lax.broadcasted_iota(jnp.int32, (SGU_C, SGU_C), 1)
    mask = (ci <= ri).astype(F32)
    dg = D // SGU_G
    parts = []
    for g in range(SGU_G):
        parts.append(_bdot(sw[g] * mask, vn[:, g * dg:(g + 1) * dg]) + sbt[:, g:g + 1])
    return u * jnp.concatenate(parts, axis=-1)


def _pre_fn(qr, qk, qv, qxw, qxa, qxg, wl, w0, al, a0, gl, k_k, k_a):
    w = -jax.nn.softplus(-(w0 + _bdot(jnp.tanh(qxw), wl))) - 0.5
    lw = -jnp.exp(w)
    aa = _sigmoid(a0 + _bdot(qxa, al))
    g = _bdot(_sigmoid(qxg), gl)
    kk = _to_heads(qk * k_k)
    kk = kk / jnp.maximum(jnp.sqrt(_head_sum(kk * kk)), 1e-12)
    k2 = qk * (1.0 + (aa - 1.0) * k_a)
    return _to_heads(qr), _to_heads(lw), _to_heads(k2), _to_heads(qv), kk, _to_heads(aa), g


def _post_fn(o, r, k2, v, g, ln_w, ln_b, r_k):
    mu = _head_sum(o) * (1.0 / HN)
    d = o - mu
    var = _head_sum(d * d) * (1.0 / HN)
    on = d * lax.rsqrt(var + GN_EPS) * ln_w + ln_b
    bonus = _head_sum(r * k2 * r_k) * v
    return _from_heads(on + bonus) * g


def _gate_fn(pg, ya, yb):
    return _sigmoid(pg[:, :D]) * ya + _sigmoid(pg[:, D:]) * yb


def _bmm(x, y, cx, cy, out_path=False):
    return lax.dot_general(x, y, (((cx,), (cy,)), ((0,), (0,))),
                           precision=SCAN_OUT_PRECISION if out_path else SCAN_PRECISION, preferred_element_type=F32)


def _unit_lower_inverse(M):
    C = M.shape[1]
    ti = lax.broadcasted_iota(jnp.int32, (C, C), 0)
    tj = lax.broadcasted_iota(jnp.int32, (C, C), 1)
    eye = (ti == tj).astype(F32)
    same = lambda b: (ti // b == tj // b).astype(F32)
    X = -(M * same(SOLVE_B))
    inv = eye + X
    span = 1
    while 2 * span < SOLVE_B:
        X = _bmm(X, X, 2, 1)
        inv = inv + _bmm(inv, X, 2, 1)
        span *= 2
    b = SOLVE_B
    while b < C:
        low = M * (same(2 * b) - same(b))
        inv = inv - _bmm(_bmm(inv, low, 2, 1, out_path=True), inv, 2, 1, out_path=True)
        b *= 2
    return inv


@jax.custom_vjp
def _unit_lower_solve(inv, M, y):
    return _bmm(inv, y, 2, 1)


def _unit_lower_solve_fwd(inv, M, y):
    u = _bmm(inv, y, 2, 1)
    return u, (inv, u)


def _unit_lower_solve_bwd(res, du):
    inv, u = res
    dy = _bmm(inv, du, 1, 1)
    return jnp.zeros_like(inv), -_bmm(dy, u, 2, 2), dy


_unit_lower_solve.defvjp(_unit_lower_solve_fwd, _unit_lower_solve_bwd)


@jax.custom_vjp
def _unit_lower_solved(inv, u, M, y):
    return u


_unit_lower_solved.defvjp(lambda inv, u, M, y: (u, (inv, u)),
                          lambda res, du: (jnp.zeros_like(res[0]), jnp.zeros_like(res[1]))
                          + _unit_lower_solve_bwd(res, du)[1:])


@functools.partial(jax.custom_vjp, nondiff_argnums=(0,))
def _kept(fn, value, *args):
    return value


def _kept_fwd(fn, value, *args):
    return value, args


def _kept_bwd(fn, args, d):
    _, vjp = jax.vjp(fn, *args)
    return (jnp.zeros_like(d),) + tuple(vjp(d))


_kept.defvjp(_kept_fwd, _kept_bwd)


def _sum_over_time(x, reverse):
    C = x.shape[1]
    ti = lax.broadcasted_iota(jnp.int32, (C, C), 0)
    tj = lax.broadcasted_iota(jnp.int32, (C, C), 1)
    ones = jnp.broadcast_to(((tj >= ti) if reverse else (tj <= ti)).astype(BF16), (x.shape[0], C, C))
    hi = x.astype(BF16)
    r1 = x - hi.astype(F32)
    mid = r1.astype(BF16)
    lo = (r1 - mid.astype(F32)).astype(BF16)
    dn = (((2,), (1,)), ((0,), (0,)))
    return sum(lax.dot_general(ones, p, dn, preferred_element_type=F32) for p in (lo, mid, hi))


@jax.custom_vjp
def _time_cumsum(lw):
    return _sum_over_time(lw, reverse=False)


_time_cumsum.defvjp(lambda lw: (_sum_over_time(lw, reverse=False), None),
                    lambda _, d: (_sum_over_time(d, reverse=True),))


def _chunk_fn(S0, r, lw, k, v, kk, a, kept=None):
    C = SCAN_C
    bmm = _bmm
    ti = lax.broadcasted_iota(jnp.int32, (C, C), 0)
    tj = lax.broadcasted_iota(jnp.int32, (C, C), 1)
    incl2 = jnp.concatenate([(tj <= ti).astype(F32)] * 2, axis=1)
    strict = (tj < ti).astype(F32)
    n_mask = jnp.concatenate([jnp.zeros((C, C), F32), strict], axis=1)

    def known(name, fn, *args):
        return fn(*args) if kept is None else _kept(fn, kept[name], *args)

    cum = known("cum", _time_cumsum, lw)
    g_in, g_ex, g_inv = jnp.exp(cum), jnp.exp(cum - lw), jnp.exp(-cum)
    kkt, rt = kk * g_ex, r * g_in
    bk = jnp.concatenate([kk * a * g_inv, k * g_inv], axis=1)
    kr = jnp.concatenate([kkt, rt], axis=1)
    ratios = known("ratios", lambda x, y: bmm(x, y, 2, 2), kr, bk)
    A = ratios[:, :C]
    M = A[:, :, :C] * strict
    zv = jnp.concatenate([jnp.zeros_like(v), v], axis=1)
    s0_side = bmm(kr, S0, 2, 2, out_path=True)
    rhs = s0_side[:, :C] + bmm(A * n_mask, zv, 2, 1, out_path=True)
    if kept is None:
        inv = lax.stop_gradient(_unit_lower_inverse(M))
        y = _unit_lower_solve(inv, M, rhs)
    else:
        inv = kept["inv"]
        y = _unit_lower_solved(inv, kept["y"], M, rhs)
    z = jnp.concatenate([-y, v], axis=1)
    O = s0_side[:, C:] + bmm(ratios[:, C:] * incl2, z, 2, 1, out_path=True)
    g_end = g_in[:, C - 1:C, :]
    S1 = S0 * g_end + bmm(z, bk * g_end, 1, 1, out_path=True)
    return O, S1, dict(cum=cum, ratios=ratios, y=y, inv=inv)


def _scan_fwd(r, lw, k, v, kk, a, ex=None, tb=256):
    assert SCAN_C == HN and 2 * SCAN_C == PW
    T = r.shape[1]
    tb = min(tb, T)
    n_chunks = tb // SCAN_C
    nb = T // tb
    nx = ex.nb if ex else 0

    def body(*refs):
        r_ref, lw_ref, k_ref, v_ref, kk_ref, a_ref = refs[:6]
        x_in, (o_ref, s0_ref), x_out = refs[6:6 + nx], refs[6 + nx:8 + nx], refs[8 + nx:8 + 2 * nx]
        s_ref, sems = refs[8 + 2 * nx], refs[9 + 2 * nx:]

        plan = ex.schedule(nb) if ex else []

        @pl.when(pl.program_id(0) == 0)
        def _():
            s_ref[...] = jnp.zeros_like(s_ref)
            for at, action in plan[:1]:
                action(x_in, x_out, sems)

        def step(c, carry):
            sl = pl.ds(pl.multiple_of(c * SCAN_C, SCAN_C), SCAN_C)
            S0 = s_ref[...]
            O, S1, keep = _chunk_fn(S0, *[_split_pairs(ref[:, sl, :])
                                          for ref in (r_ref, lw_ref, k_ref, v_ref, kk_ref, a_ref)])
            o_ref[:, sl, :] = _join_pairs(O)
            s0_ref[c, 0] = jnp.concatenate([S0, keep["inv"]], axis=-1)
            s0_ref[c, 1] = jnp.concatenate([keep["cum"], keep["y"]], axis=-1)
            s0_ref[c, 2] = keep["ratios"][:, :SCAN_C]
            s0_ref[c, 3] = keep["ratios"][:, SCAN_C:]
            s_ref[...] = S1
            return carry

        lax.fori_loop(0, n_chunks, step, 0)

        for at, action in plan[1:]:
            pl.when(pl.program_id(0) == at)(functools.partial(action, x_in, x_out, sems))

    hm = pl.BlockSpec((NP, tb, PW), lambda i: (0, i, 0))
    res = pl.pallas_call(
        body, name="rwkv_scan_fwd", grid=(nb,), in_specs=[hm] * 6 + (ex.any_specs if ex else []),
        out_specs=[hm, pl.BlockSpec((n_chunks, N_KEPT, NH, HN, PW), lambda i: (i, 0, 0, 0, 0))]
        + (ex.any_specs if ex else []),
        out_shape=[jax.ShapeDtypeStruct((NP, T, PW), F32),
                   jax.ShapeDtypeStruct((T // SCAN_C, N_KEPT, NH, HN, PW), F32)]
        + (ex.out_shape if ex else []),
        scratch_shapes=[pltpu.VMEM((NH, HN, HN), F32)] + (ex.sem_shapes if ex else []),
        compiler_params=pltpu.CompilerParams(dimension_semantics=("arbitrary",), vmem_limit_bytes=VMEM_LIMIT),
    )(r, lw, k, v, kk, a, *(ex.bufs if ex else []))
    return res[0], res[1], list(res[2:])


def _scan_bwd(r, lw, k, v, kk, a, s0s, do, ex=None, tb=128):
    T = r.shape[1]
    tb = min(tb, T)
    n_chunks = tb // SCAN_C
    nb = T // tb
    nx = ex.nb if ex else 0

    def body(*refs):
        r_ref, lw_ref, k_ref, v_ref, kk_ref, a_ref, s0_ref, do_ref = refs[:8]
        x_in, (dr, dlw, dk, dv, dkk, da), x_out = refs[8:8 + nx], refs[8 + nx:14 + nx], refs[14 + nx:14 + 2 * nx]
        ds_ref, sems = refs[14 + 2 * nx], refs[15 + 2 * nx:]

        plan = ex.schedule(nb) if ex else []

        @pl.when(pl.program_id(0) == 0)
        def _():
            ds_ref[...] = jnp.zeros_like(ds_ref)
            for at, action in plan[:1]:
                action(x_in, x_out, sems)

        def step(j, carry):
            c = n_chunks - 1 - j
            sl = pl.ds(pl.multiple_of(c * SCAN_C, SCAN_C), SCAN_C)
            s0_inv, cum_y = s0_ref[c, 0], s0_ref[c, 1]
            kept = dict(inv=s0_inv[:, :, HN:], cum=cum_y[:, :, :HN], y=cum_y[:, :, HN:],
                        ratios=jnp.concatenate([s0_ref[c, 2], s0_ref[c, 3]], axis=1))
            _, vjp = jax.vjp(lambda *t: _chunk_fn(*t, kept=kept)[:2], s0_inv[:, :, :HN],
                             *[_split_pairs(ref[:, sl, :]) for ref in (r_ref, lw_ref, k_ref, v_ref, kk_ref, a_ref)])
            g = vjp((_split_pairs(do_ref[:, sl, :]), ds_ref[...]))
            ds_ref[...] = g[0]
            for ref, val in zip((dr, dlw, dk, dv, dkk, da), g[1:]):
                ref[:, sl, :] = _join_pairs(val)
            return carry

        lax.fori_loop(0, n_chunks, step, 0)

        for at, action in plan[1:]:
            pl.when(pl.program_id(0) == at)(functools.partial(action, x_in, x_out, sems))

    hm = pl.BlockSpec((NP, tb, PW), lambda i: (0, nb - 1 - i, 0))
    res = pl.pallas_call(
        body, name="rwkv_scan_bwd", grid=(nb,),
        in_specs=[hm] * 6 + [pl.BlockSpec((n_chunks, N_KEPT, NH, HN, PW), lambda i: (nb - 1 - i, 0, 0, 0, 0)), hm]
        + (ex.any_specs if ex else []),
        out_specs=[hm] * 6 + (ex.any_specs if ex else []),
        out_shape=[jax.ShapeDtypeStruct((NP, T, PW), F32)] * 6 + (ex.out_shape if ex else []),
        scratch_shapes=[pltpu.VMEM((NH, HN, HN), F32)] + (ex.sem_shapes if ex else []),
        compiler_params=pltpu.CompilerParams(dimension_semantics=("arbitrary",), vmem_limit_bytes=VMEM_LIMIT),
    )(r, lw, k, v, kk, a, s0s, do, *(ex.bufs if ex else []))
    return list(res[:6]), list(res[6:])


def _shift_down(i, p, prev8):
    first = jnp.where(i > 0, prev8[7:8, :], 0.0)
    row = lax.broadcasted_iota(jnp.int32, p.shape, 0)
    return jnp.where(row == 0, first, pltpu.roll(p, 1, axis=0))


def _mix_bwd(dq, p, sb, tm=256):
    def fn(i, n, dq, next8, p, prev8, sb):
        ps = _shift_down(i, p, prev8)
        d1 = dq * sb[1:2]
        last = jnp.where(i < n - 1, next8[0:1, :] * sb[1:2], 0.0)
        row = lax.broadcasted_iota(jnp.int32, dq.shape, 0)
        up = jnp.where(row == dq.shape[0] - 1, last, pltpu.roll(d1, dq.shape[0] - 1, axis=0))
        return (dq * sb[0:1] + up, jnp.sum(dq * p, axis=0, keepdims=True), jnp.sum(dq * ps, axis=0, keepdims=True))
    w = p.shape[1]
    return _rows_call("shift_mix_bwd", fn, [Rows(dq), Halo(dq, +1), Rows(p), Halo(p, -1)], [sb], [("rows", w, BF16)],
                      accs=[((1, w), F32), ((1, w), F32)], tm=tm, with_pid=True)


def _local_step(x, target, W, late_weights=None, early_grads=None, w_in_grads_ready=None, ffn_grads_ready=None):
    G = {}
    a = _rows_call("norm_mix_fwd", lambda x, g: (_rms(x, g),), [Rows(x)], [W["g_mix"]], [("rows", D, BF16)])[0]
    p_sgu = _matmul("proj_sgu", a, W["w_sgu_t"], "nt")
    def token_shift(p, sb0, sb1):
        row = lax.broadcasted_iota(jnp.int32, p.shape, 0)
        return p, p * sb0 + jnp.where(row == 0, 0.0, pltpu.roll(p, 1, axis=0)) * sb1
    p_rw, q = _matmul("proj_rwkv", a, W["w_rw_t"], "nt", tn=512, whole_rows=True, epilogue=token_shift,
                      extras=[W["sb"][0:1], W["sb"][1:2]], out_dtypes=(F32, F32))
    p_gate = _matmul("proj_gate", a, W["w_gate_t"], "nt")

    sgu_consts = [W["sgu_ln_w"], W["sgu_ln_b"], W["sgu_w"], W["sgu_bt"]]
    s = _rows_call("sgu_fwd", lambda *t: (_sgu_fn(*t),), [Rows(p_sgu)], sgu_consts, [("rows", D, BF16)], tm=SGU_C)[0]

    q_ins = [Rows(q, D, 0), Rows(q, D, 1), Rows(q, D, 2), Rows(q, 128, 24), Rows(q, 128, 25), Rows(q, 256, 13)]
    pre_consts = [W["w_lora"], W["w0"], W["a_lora"], W["a0"], W["g_lora"], W["k_k"], W["k_a"]]
    r_h, lw_h, k_h, v_h, kk_h, a_h, g_gate = _rows_call(
        "rwkv_pre_fwd", _pre_fn, q_ins, pre_consts, [("heads", F32)] * 6 + [("rows", D, F32)], tm=128)
    o_h, s0s, got = _scan_fwd(r_h, lw_h, k_h, v_h, kk_h, a_h, ex=late_weights[0] if late_weights else None)
    if late_weights:
        W = {**W, **late_weights[1](got)}
    y_a = _matmul("proj_a", s, W["w_proj_a"], "nn")
    post_ins = [Heads(o_h), Heads(r_h), Heads(k_h), Heads(v_h), Rows(g_gate)]
    post_consts = [W[n].reshape(NP, 1, PW) for n in ("ln_x_w", "ln_x_b", "r_k")]
    z_b = _rows_call("rwkv_post_fwd", lambda *t: (_post_fn(*t),), post_ins, post_consts, [("rows", D, BF16)], tm=128)[0]
    y_b, mixed = _matmul("proj_b", z_b, W["w_proj_b"], "nn", extras=[(p_gate, 0), (p_gate, 1), y_a],
                         epilogue=lambda yb, ga, gb, ya: (yb, _sigmoid(ga) * ya + _sigmoid(gb) * yb),
                         out_dtypes=(F32, BF16))

    def res1(mo, x, g):
        h1 = x + mo
        return h1, _rms(h1, g)
    h1, f = _matmul("proj_out", mixed, W["w_out"], "nn", extras=[x, W["g_ffn"]], epilogue=res1,
                    out_dtypes=(F32, BF16))

    def relu_sq(u):
        r = jnp.maximum(u, 0.0)
        return r, r * r
    r1, act = _matmul("ffn_up", f, W["w_ffn1"], "nn", epilogue=relu_sq, out_dtypes=(BF16, BF16))
    ff = _matmul("ffn_down", act, W["w_ffn2"], "nn")

    def head(h1, ff, tgt, g):
        def f_(h1, ff, g):
            y = _rms(h1 + ff, g)
            return 0.5 * jnp.sum(jnp.mean(jnp.square(y - tgt), axis=-1))
        loss, (dh2, _, dg) = jax.value_and_grad(f_, argnums=(0, 1, 2))(h1, ff, g)
        return dh2, jnp.full((8, LANES), loss, F32), dg
    dh2, loss_acc, G["g_final"] = _rows_call("loss_head", head, [Rows(h1), Rows(ff), Rows(target)], [W["g_final"]],
                                             [("rows", D, F32)], accs=[((8, LANES), F32), ((1, D), F32)])

    d_u1 = _matmul("ffn_down_dx", dh2, W["w_ffn2"], "nt", extras=[r1], out_dtypes=(BF16,),
                   epilogue=lambda d_act, r: (d_act * 2.0 * r.astype(F32),))[0]
    G["w_ffn2"] = _matmul("ffn_down_dw", act, dh2, "tn", out_dtype=GRAD_PAYLOAD)
    d_f = _matmul("ffn_up_dx", d_u1, W["w_ffn1"], "nt")
    G["w_ffn1"] = _matmul("ffn_up_dw", f, d_u1, "tn", out_blocks=N_DEV, out_dtype=GRAD_PAYLOAD)

    def res1_bwd(h1, d_f, dh2, g):
        _, vjp = jax.vjp(_rms, h1, g)
        dh, dg = vjp(d_f)
        return dh2 + dh, dg
    dh1, G["g_ffn"] = _rows_call("residual_norm_bwd", res1_bwd, [Rows(h1), Rows(d_f), Rows(dh2)],
                                 [W["g_ffn"]], [("rows", D, F32)], accs=[((1, D), F32)])
    ffn_token = ffn_grads_ready(G) if ffn_grads_ready else None
    def gate_bwd(d_mixed, ga, gb, ya, yb):
        _, vjp = jax.vjp(_gate_fn, jnp.concatenate([ga, gb], axis=-1), ya, yb)
        return vjp(d_mixed)
    d_gate, d_ya, d_yb = _matmul("proj_out_dx", dh1, W["w_out"], "nt", after=ffn_token, epilogue=gate_bwd,
                                 extras=[(p_gate, 0), (p_gate, 1), y_a, y_b], out_dtypes=(BF16, BF16, BF16),
                                 out_widths=(2 * D, D, D))
    G["w_out"] = _matmul("proj_out_dw", mixed, dh1, "tn", out_dtype=GRAD_PAYLOAD)

    d_s = _matmul("proj_a_dx", d_ya, W["w_proj_a"], "nt")
    G["w_proj_a"] = _matmul("proj_a_dw", s, d_ya, "tn", out_dtype=GRAD_PAYLOAD)

    def sgu_bwd(p, ds, *c):
        _, vjp = jax.vjp(_sgu_fn, p, *c)
        return vjp(ds)
    d_p_sgu, G["sgu_ln_w"], G["sgu_ln_b"], G["sgu_w"], G["sgu_bt"] = _rows_call(
        "sgu_bwd", sgu_bwd, [Rows(p_sgu), Rows(d_s)], sgu_consts, [("rows", 2 * D, BF16)],
        accs=[((1, D), F32), ((1, D), F32), ((SGU_G, SGU_C, SGU_C), F32), ((SGU_C, SGU_G), F32)], tm=SGU_C)

    d_zb = _matmul("proj_b_dx", d_yb, W["w_proj_b"], "nt")
    G["w_proj_b"] = _matmul("proj_b_dw", z_b, d_yb, "tn", out_dtype=GRAD_PAYLOAD)

    def post_bwd(o, r, k2, v, g, dz, *c):
        _, vjp = jax.vjp(_post_fn, o, r, k2, v, g, *c)
        return vjp(dz)
    do_h, dr1, dk1, dv1, d_g, g_lnw, g_lnb, g_rk = _rows_call(
        "rwkv_post_bwd", post_bwd, post_ins + [Rows(d_zb)], post_consts, [("heads", F32)] * 4 + [("rows", D, F32)],
        accs=[((NP, 1, PW), F32)] * 3, tm=128)
    G["ln_x_w"], G["ln_x_b"], G["r_k"] = (t.reshape(1, D) for t in (g_lnw, g_lnb, g_rk))
    (dr2, dlw, dk2, dv2, dkk, daa), early = _scan_bwd(r_h, lw_h, k_h, v_h, kk_h, a_h, s0s, do_h,
                                                      ex=early_grads(G) if early_grads else None)

    def pre_bwd(qr, qk, qv, qxw, qxa, qxg, dr1, dr2, dlw, dk1, dk2, dv1, dv2, dkk, daa, dg, *c):
        _, vjp = jax.vjp(_pre_fn, qr, qk, qv, qxw, qxa, qxg, *c)
        g = vjp((dr1 + dr2, dlw, dk1 + dk2, dv1 + dv2, dkk, daa, dg))
        dq = jnp.concatenate(g[:6], axis=-1)
        return (dq,) + tuple(g[6:])
    pre_b_ins = q_ins + [Heads(dr1), Heads(dr2), Heads(dlw), Heads(dk1), Heads(dk2), Heads(dv1), Heads(dv2),
                         Heads(dkk), Heads(daa), Rows(d_g)]
    d_q, G["w_lora"], G["w0"], G["a_lora"], G["a0"], G["g_lora"], G["k_k"], G["k_a"] = _rows_call(
        "rwkv_pre_bwd", pre_bwd, pre_b_ins, pre_consts, [("rows", RW_INT, F32)],
        accs=[((128, D), F32), ((1, D), F32), ((128, D), F32), ((1, D), F32), ((256, D), F32), ((1, D), F32),
              ((1, D), F32)], tm=128)
    d_p_rw, dsb0, dsb1 = _mix_bwd(d_q, p_rw, W["sb"])
    G["sb"] = jnp.concatenate([dsb0, dsb1], axis=0)

    G["w_sgu_t"] = _matmul("proj_sgu_dw", d_p_sgu, a, "tn", out_dtype=GRAD_PAYLOAD)
    G["w_rw_t"] = _matmul("proj_rwkv_dw", d_p_rw, a, "tn", out_dtype=GRAD_PAYLOAD)
    G["w_gate_t"] = _matmul("proj_gate_dw", d_gate, a, "tn", out_dtype=GRAD_PAYLOAD)
    token = w_in_grads_ready(G) if w_in_grads_ready else None
    da1 = _matmul("proj_sgu_dx", d_p_sgu, W["w_sgu_t"], "nn", after=token)
    da2 = _matmul("proj_rwkv_dx", d_p_rw, W["w_rw_t"], "nn", after=token)
    da3 = _matmul("proj_gate_dx", d_gate, W["w_gate_t"], "nn", after=token)

    def norm1_bwd(x, da1, da2, da3, dh1, g):
        _, vjp = jax.vjp(_rms, x, g)
        dx, dg = vjp(da1 + da2 + da3)
        return dh1 + dx, dg
    dx, G["g_mix"] = _rows_call("norm_mix_bwd", norm1_bwd, [Rows(x), Rows(da1), Rows(da2), Rows(da3), Rows(dh1)],
                                [W["g_mix"]], [("rows", D, F32)], accs=[((1, D), F32)])
    return loss_acc[0, 0], dx, G, early


class Exchange:
    def __init__(self, bufs, gathers):
        self.bufs, self.gathers, self.nb = list(bufs), list(gathers), len(bufs)
        self.any_specs = [pl.BlockSpec(memory_space=pl.ANY)] * self.nb
        self.out_shape = [jax.ShapeDtypeStruct((N_DEV,) + (b.shape if g else b.shape[1:]), b.dtype)
                          for b, g in zip(self.bufs, self.gathers)]
        n = (N_DEV - 1) * self.nb
        self.sem_shapes = [pltpu.SemaphoreType.DMA((n,)), pltpu.SemaphoreType.DMA((n,)),
                           pltpu.SemaphoreType.DMA((self.nb,))]

    def _copies(self, in_refs, out_refs, sems):
        send_sems, recv_sems, local_sems = sems
        x, y, c = lax.axis_index("x"), lax.axis_index("y"), lax.axis_index("c")
        me = 4 * x + 2 * y + c

        def src(b, dest):
            return in_refs[b] if self.gathers[b] else in_refs[b].at[dest]

        local = [pltpu.make_async_copy(src(b, me), out_refs[b].at[me], local_sems.at[b]) for b in range(self.nb)]
        sends, recvs = [], []
        for kbits in range(1, N_DEV):
            px = 1 - x if kbits & 4 else x
            py = 1 - y if kbits & 2 else y
            pc = 1 - c if kbits & 1 else c
            peer = 4 * px + 2 * py + pc
            for b in range(self.nb):
                s = (kbits - 1) * self.nb + b
                sends.append(pltpu.make_async_remote_copy(
                    src_ref=src(b, peer), dst_ref=out_refs[b].at[me], send_sem=send_sems.at[s],
                    recv_sem=recv_sems.at[s], device_id=(px, py, pc), device_id_type=pl.DeviceIdType.MESH))
                recvs.append(pltpu.make_async_remote_copy(
                    src_ref=src(b, peer), dst_ref=out_refs[b].at[peer], send_sem=send_sems.at[s],
                    recv_sem=recv_sems.at[s], device_id=(px, py, pc), device_id_type=pl.DeviceIdType.MESH))
        return local, sends, recvs

    def start(self, in_refs, out_refs, sems):
        local, sends, _ = self._copies(in_refs, out_refs, sems)
        for cp in sends + local:
            cp.start()

    def wait(self, in_refs, out_refs, sems):
        local, sends, recvs = self._copies(in_refs, out_refs, sems)
        for cp in recvs:
            cp.wait_recv()
        for cp in sends:
            cp.wait_send()
        for cp in local:
            cp.wait()

    def schedule(self, n_steps):
        return [(0, self.start), (n_steps - 1, self.wait)]


def _exchange(name, bufs, gather):
    ex = Exchange(bufs, gather if isinstance(gather, (list, tuple)) else [gather] * len(bufs))

    def body(*refs):
        in_refs, out_refs, sems = refs[:ex.nb], refs[ex.nb:2 * ex.nb], refs[2 * ex.nb:]
        ex.start(in_refs, out_refs, sems)
        ex.wait(in_refs, out_refs, sems)

    return pl.pallas_call(body, name=name, in_specs=ex.any_specs, out_specs=ex.any_specs, out_shape=ex.out_shape,
                          scratch_shapes=ex.sem_shapes)(*ex.bufs)


N_CHIP = 4


def _pair_exchange(name, blocks):
    def body(b_ref, got_ref, send_sems, recv_sems):
        x, y, c = lax.axis_index("x"), lax.axis_index("y"), lax.axis_index("c")
        copies = [pltpu.make_async_remote_copy(
            src_ref=b_ref.at[2 * q + 1 - c], dst_ref=got_ref.at[q], send_sem=send_sems.at[q],
            recv_sem=recv_sems.at[q], device_id=(x, y, 1 - c), device_id_type=pl.DeviceIdType.MESH)
            for q in range(N_CHIP)]
        for cp in copies:
            cp.start()
        for cp in copies:
            cp.wait_recv()
        for cp in copies:
            cp.wait_send()

    any_spec = pl.BlockSpec(memory_space=pl.ANY)
    return pl.pallas_call(
        body, name=name, in_specs=[any_spec], out_specs=any_spec,
        out_shape=jax.ShapeDtypeStruct((N_CHIP,) + blocks.shape[1:], blocks.dtype),
        scratch_shapes=[pltpu.SemaphoreType.DMA((N_CHIP,))] * 2,
    )(blocks)


def _pair_sum(name, blocks, got, core):
    _, R, Wd = blocks.shape

    def body(c_ref, a_ref, b_ref, o_ref):
        o_ref[...] = (a_ref[...].astype(F32) + b_ref[...].astype(F32)).astype(o_ref.dtype)

    return pl.pallas_call(
        body, name=name,
        grid_spec=pltpu.PrefetchScalarGridSpec(
            num_scalar_prefetch=1, grid=(N_CHIP,),
            in_specs=[pl.BlockSpec((None, R, Wd), lambda q, c_ref: (2 * q + c_ref[0], 0, 0)),
                      pl.BlockSpec((None, R, Wd), lambda q, c_ref: (q, 0, 0))],
            out_specs=pl.BlockSpec((None, R, Wd), lambda q, c_ref: (q, 0, 0))),
        out_shape=jax.ShapeDtypeStruct(got.shape, blocks.dtype),
        compiler_params=pltpu.CompilerParams(dimension_semantics=("parallel",), vmem_limit_bytes=VMEM_LIMIT),
    )(core, blocks, got)


def _chip_copies(s_ref, land_ref, send_sems, recv_sems):
    x, y, c = lax.axis_index("x"), lax.axis_index("y"), lax.axis_index("c")
    my_q = 2 * x + y
    sends, recvs = [], []
    for kbits in range(1, N_CHIP):
        px = 1 - x if kbits & 2 else x
        py = 1 - y if kbits & 1 else y
        peer_q = 2 * px + py
        sends.append(pltpu.make_async_remote_copy(
            src_ref=s_ref.at[peer_q], dst_ref=land_ref.at[my_q], send_sem=send_sems[kbits - 1],
            recv_sem=recv_sems[kbits - 1], device_id=(px, py, c), device_id_type=pl.DeviceIdType.MESH))
        recvs.append(pltpu.make_async_remote_copy(
            src_ref=s_ref.at[peer_q], dst_ref=land_ref.at[peer_q], send_sem=send_sems[kbits - 1],
            recv_sem=recv_sems[kbits - 1], device_id=(px, py, c), device_id_type=pl.DeviceIdType.MESH))
    return sends, recvs


_HBM = pl.BlockSpec(memory_space=pltpu.HBM)
_SEM = pl.BlockSpec(memory_space=pltpu.SEMAPHORE)
N_CHIP_SEMS = 2 * (N_CHIP - 1)


def _scatter_copies(b_refs, land_refs, send_sems, recv_sems):
    x, y, c = lax.axis_index("x"), lax.axis_index("y"), lax.axis_index("c")
    me = 4 * x + 2 * y + c
    sends, recvs = [], []
    for kbits in range(1, N_DEV):
        px = 1 - x if kbits & 4 else x
        py = 1 - y if kbits & 2 else y
        pc = 1 - c if kbits & 1 else c
        peer = 4 * px + 2 * py + pc
        for b in range(len(b_refs)):
            s = (kbits - 1) * len(b_refs) + b
            sends.append(pltpu.make_async_remote_copy(
                src_ref=b_refs[b].at[peer], dst_ref=land_refs[b].at[me], send_sem=send_sems[s],
                recv_sem=recv_sems[s], device_id=(px, py, pc), device_id_type=pl.DeviceIdType.MESH))
            recvs.append(pltpu.make_async_remote_copy(
                src_ref=b_refs[b].at[peer], dst_ref=land_refs[b].at[peer], send_sem=send_sems[s],
                recv_sem=recv_sems[s], device_id=(px, py, pc), device_id_type=pl.DeviceIdType.MESH))
    return sends, recvs


def _scatter_start(name, bufs):
    nb = len(bufs)
    n = (N_DEV - 1) * nb

    def body(*refs):
        b_refs, land_refs, outs = refs[:nb], refs[nb:2 * nb], refs[2 * nb:]
        sends, _ = _scatter_copies(b_refs, land_refs, outs[:n], outs[n:2 * n])
        for cp in sends:
            cp.start()
        token = outs[2 * n + 2 * nb]
        token[...] = jnp.zeros_like(token)

    thru = tuple(pltpu.HBM(b.shape, b.dtype) for b in bufs)
    res = pl.pallas_call(
        body, name=name, in_specs=(_HBM,) * (2 * nb),
        out_specs=(_SEM,) * (2 * n) + (_HBM,) * (2 * nb) + (pl.BlockSpec(memory_space=pltpu.VMEM),),
        out_shape=(pltpu.SemaphoreType.DMA(()),) * (2 * n) + thru + thru + (jax.ShapeDtypeStruct((8, LANES), F32),),
        input_output_aliases={i: 2 * n + i for i in range(2 * nb)},
        compiler_params=pltpu.CompilerParams(has_side_effects=pltpu.SideEffectType.DATAFLOW_SIDE_EFFECTING),
    )(*[pltpu.with_memory_space_constraint(b, pltpu.HBM) for b in bufs],
      *[pltpu.with_memory_space_constraint(lax.empty(b.shape, b.dtype), pltpu.HBM) for b in bufs])
    return res[:2 * n], list(res[2 * n:2 * n + nb]), list(res[2 * n + nb:2 * n + 2 * nb]), res[2 * n + 2 * nb]


def _copy_through_vmem(pairs, stage, sems):
    for into_vmem in (True, False):
        copies = [pltpu.make_async_copy(src if into_vmem else stage[i], stage[i] if into_vmem else dst, sems.at[i])
                  for i, (src, dst) in enumerate(pairs)]
        for cp in copies:
            cp.start()
        for cp in copies:
            cp.wait()


def _scatter_wait(name, sems, bufs_thru, lands_thru, after):
    nb = len(bufs_thru)
    n = (N_DEV - 1) * nb

    def body(*refs):
        b_refs, land_refs, sem_refs = refs[:nb], refs[nb:2 * nb], refs[2 * nb:2 * nb + 2 * n]
        stage, local_sems = refs[-(nb + 1):-1], refs[-1]
        me = 4 * lax.axis_index("x") + 2 * lax.axis_index("y") + lax.axis_index("c")
        _copy_through_vmem([(b_refs[b].at[me], land_refs[b].at[me]) for b in range(nb)], stage, local_sems)
        sends, recvs = _scatter_copies(b_refs, land_refs, sem_refs[:n], sem_refs[n:])
        for cp in sends:
            cp.wait_send()
        for cp in recvs:
            cp.wait_recv()

    thru = tuple(pltpu.HBM(b.shape, b.dtype) for b in bufs_thru)
    res = pl.pallas_call(
        body, name=name, in_specs=(_HBM,) * (2 * nb) + (_SEM,) * (2 * n) + (pl.BlockSpec(memory_space=pl.ANY),),
        out_specs=(_HBM,) * (2 * nb), out_shape=thru + thru,
        input_output_aliases={i: i for i in range(2 * nb)},
        scratch_shapes=[pltpu.VMEM(b.shape[1:], b.dtype) for b in bufs_thru] + [pltpu.SemaphoreType.DMA((nb,))],
        compiler_params=pltpu.CompilerParams(has_side_effects=pltpu.SideEffectType.DATAFLOW_SIDE_EFFECTING,
                                             vmem_limit_bytes=VMEM_LIMIT),
    )(*bufs_thru, *lands_thru, *sems, after)
    return list(res[:nb]), list(res[nb:])


def _chip_exchange_start(name, sums):
    def body(s_ref, land_ref, *outs):
        sems, token = outs[:N_CHIP_SEMS], outs[N_CHIP_SEMS + 2]
        sends, _ = _chip_copies(s_ref, land_ref, sems[:N_CHIP - 1], sems[N_CHIP - 1:])
        for cp in sends:
            cp.start()
        token[...] = jnp.zeros_like(token)

    res = pl.pallas_call(
        body, name=name, in_specs=(_HBM, _HBM),
        out_specs=(_SEM,) * N_CHIP_SEMS + (_HBM, _HBM, pl.BlockSpec(memory_space=pltpu.VMEM)),
        out_shape=(pltpu.SemaphoreType.DMA(()),) * N_CHIP_SEMS
        + (pltpu.HBM(sums.shape, sums.dtype), pltpu.HBM(sums.shape, sums.dtype), jax.ShapeDtypeStruct((8, LANES), F32)),
        input_output_aliases={0: N_CHIP_SEMS, 1: N_CHIP_SEMS + 1},
        compiler_params=pltpu.CompilerParams(has_side_effects=pltpu.SideEffectType.DATAFLOW_SIDE_EFFECTING),
    )(pltpu.with_memory_space_constraint(sums, pltpu.HBM),
      pltpu.with_memory_space_constraint(lax.empty(sums.shape, sums.dtype), pltpu.HBM))
    return res[:N_CHIP_SEMS], res[N_CHIP_SEMS], res[N_CHIP_SEMS + 1], res[N_CHIP_SEMS + 2]


def _chip_exchange_wait(name, sems, sums_thru, land_thru, after):
    def body(s_ref, land_ref, *rest):
        sems, stage, local_sems = rest[:N_CHIP_SEMS], rest[-2], rest[-1]
        my_chip = 2 * lax.axis_index("x") + lax.axis_index("y")
        _copy_through_vmem([(s_ref.at[my_chip], land_ref.at[my_chip])], [stage], local_sems)
        sends, recvs = _chip_copies(s_ref, land_ref, sems[:N_CHIP - 1], sems[N_CHIP - 1:])
        for cp in sends:
            cp.wait_send()
        for cp in recvs:
            cp.wait_recv()

    return pl.pallas_call(
        body, name=name, in_specs=(_HBM, _HBM) + (_SEM,) * N_CHIP_SEMS + (pl.BlockSpec(memory_space=pl.ANY),),
        out_specs=(_HBM, _HBM),
        out_shape=(pltpu.HBM(sums_thru.shape, sums_thru.dtype), pltpu.HBM(sums_thru.shape, sums_thru.dtype)),
        input_output_aliases={0: 0, 1: 1},
        scratch_shapes=[pltpu.VMEM(sums_thru.shape[1:], sums_thru.dtype), pltpu.SemaphoreType.DMA((1,))],
        compiler_params=pltpu.CompilerParams(has_side_effects=pltpu.SideEffectType.DATAFLOW_SIDE_EFFECTING,
                                             vmem_limit_bytes=VMEM_LIMIT),
    )(sums_thru, land_thru, *sems, after)


def _adam_update(g, w, m, v):
    m_new = ADAM_B1 * m + (1.0 - ADAM_B1) * g
    v_new = ADAM_B2 * v + (1.0 - ADAM_B2) * jnp.square(g)
    m_hat = m_new / (1.0 - ADAM_B1 ** ADAM_STEP)
    v_hat = v_new / (1.0 - ADAM_B2 ** ADAM_STEP)
    return -ADAM_LR * (m_hat / (jnp.sqrt(v_hat) + ADAM_EPS) + ADAM_WD * w), m_new, v_new


def _adamw_rows(name, slots, parts):
    n_parts, rows = len(parts), [p[0].shape[0] for p in parts]
    rest = slots.shape[1] - sum(rows)

    def total(s_ref, at, r):
        g = s_ref[0, pl.ds(at, r), :]
        for j in range(1, slots.shape[0]):
            g = g + s_ref[j, pl.ds(at, r), :]
        return g

    def body(s_ref, *refs):
        ins, outs = refs[:3 * n_parts], refs[3 * n_parts:]
        at = 0
        for i, r in enumerate(rows):
            g = total(s_ref, at, r)
            w_ref, m_ref, v_ref = ins[3 * i:3 * i + 3]
            g_out, d_out, m_out, v_out = outs[4 * i:4 * i + 4]
            g_out[...] = g
            d_out[...], m_out[...], v_out[...] = _adam_update(g, w_ref[...], m_ref[...], v_ref[...])
            at += r
        outs[4 * n_parts][...] = total(s_ref, at, rest)

    shapes = [jax.ShapeDtypeStruct(p[0].shape, F32) for p in parts for _ in range(4)]
    res = pl.pallas_call(
        body, name=name, out_shape=shapes + [jax.ShapeDtypeStruct((rest, slots.shape[2]), F32)],
        compiler_params=pltpu.CompilerParams(vmem_limit_bytes=VMEM_LIMIT),
    )(slots, *[t for p in parts for t in p])
    return [res[4 * i:4 * i + 4] for i in range(n_parts)], res[4 * n_parts]


def _adamw_whole(name, items):
    n_items = len(items)

    def body(*refs):
        ins, outs = refs[:4 * n_items], refs[4 * n_items:]
        for i in range(n_items):
            s_ref, w_ref, m_ref, v_ref = ins[4 * i:4 * i + 4]
            g_out, d_out, m_out, v_out = outs[4 * i:4 * i + 4]
            g = s_ref[0].astype(F32)
            for j in range(1, s_ref.shape[0]):
                g = g + s_ref[j].astype(F32)
            g_out[0] = g
            d_out[0], m_out[0], v_out[0] = _adam_update(g, w_ref[0], m_ref[0], v_ref[0])

    res = pl.pallas_call(
        body, name=name, out_shape=[jax.ShapeDtypeStruct(item[1].shape, F32) for item in items for _ in range(4)],
        compiler_params=pltpu.CompilerParams(vmem_limit_bytes=VMEM_LIMIT),
    )(*[t for item in items for t in item])
    return [res[4 * i:4 * i + 4] for i in range(n_items)]


def _adamw(name, slots, w, m, v, tr=256):
    unit_mid = w.ndim == 3 and w.shape[1] == 1 and w.shape[0] > 1
    R, Wd = (w.shape[0], w.shape[2]) if unit_mid else w.shape[-2:]
    depth_axis = w.ndim == 3 and not unit_mid
    if unit_mid:
        tr, tc = 128, Wd
    elif R % tr == 0:
        tc = Wd
    else:
        tr, tc = R, (256 if (Wd % 256 == 0 and R > 256) else Wd)
    at = (slice(None), 0, slice(None)) if unit_mid else Ellipsis

    def body(s_ref, w_ref, m_ref, v_ref, g_out, d_out, m_out, v_out):
        g = s_ref[0].astype(F32)
        for j in range(1, slots.shape[0]):
            g = g + s_ref[j].astype(F32)
        g_out[at] = g
        d_out[at], m_out[at], v_out[at] = _adam_update(g, w_ref[at], m_ref[at], v_ref[at])

    if unit_mid:
        row = pl.BlockSpec((tr, 1, tc), lambda i, j: (i, 0, j))
    elif depth_axis:
        row = pl.BlockSpec((None, tr, tc), lambda i, j: (0, i, j))
    else:
        row = pl.BlockSpec((tr, tc), lambda i, j: (i, j))
    return pl.pallas_call(
        body, name=name, grid=(pl.cdiv(R, tr), Wd // tc),
        in_specs=[pl.BlockSpec((slots.shape[0], tr, tc), lambda i, j: (0, i, j)), row, row, row],
        out_specs=[row] * 4, out_shape=[jax.ShapeDtypeStruct(w.shape, F32)] * 4,
        compiler_params=pltpu.CompilerParams(dimension_semantics=("parallel", "parallel"),
                                             vmem_limit_bytes=VMEM_LIMIT),
    )(slots, w, m, v)


PACK_W = 1024
PACKED = [(n, s) for n, s in REPLICATED if n != "sgu_w"]
_SMALL_SIZES = [int(np.prod(s)) for _, s in PACKED]
_SMALL_ROWS = _round_up(_round_up(sum(_SMALL_SIZES) + PACK_W, PACK_W) // PACK_W, 8)
assert all(size % PACK_W == 0 for size in _SMALL_SIZES)
W_IN_SHARD = P_TOTAL // N_DEV


def _pack_rows(name, parts, rows, after=()):
    parts = [p.reshape(-1, PACK_W) for p in parts]
    assert all(p.dtype == F32 for p in parts)

    def body(*refs):
        at = 0
        for ref in refs[:len(parts)]:
            refs[-1][pl.ds(at, ref.shape[0]), :] = ref[...]
            at += ref.shape[0]
        refs[-1][pl.ds(at, rows - at), :] = jnp.zeros((rows - at, PACK_W), F32)

    return pl.pallas_call(
        body, name=name, out_shape=jax.ShapeDtypeStruct((rows, PACK_W), F32),
        in_specs=[pl.BlockSpec(memory_space=pltpu.VMEM)] * len(parts) + [pl.BlockSpec(memory_space=pl.ANY)] * len(after),
        out_specs=pl.BlockSpec(memory_space=pltpu.VMEM))(*parts, *after)


_W_IN_SEGMENTS = [(2 * D, 0, 0), (3 * D, 1, 0), (L_W, 1, 3 * D), (L_A, 1, 3 * D + 128), (L_G, 1, 3 * D + 256),
                  (2 * D, 2, 0)]
_W_IN_PADS = [(3 * D + L_W, 128 - L_W), (3 * D + 128 + L_A, 128 - L_A), (3 * D + 256 + L_G, 256 - L_G)]
_W_IN_GROUP_ROWS = [2 * D, 3 * D + 128 + 128 + 256, 2 * D]
assert sum(rows for rows, _, _ in _W_IN_SEGMENTS) == P_TOTAL


def _w_in_pieces(j):
    pieces, first = [], 0
    for rows, group, to in _W_IN_SEGMENTS:
        lo, hi = max(first, W_IN_SHARD * j), min(first + rows, W_IN_SHARD * (j + 1))
        if lo < hi:
            pieces.append((lo - W_IN_SHARD * j, hi - lo, group, to + lo - first))
        first += rows
    return pieces


def _w_in_groups_t(blocks):
    def body(x_ref, *o_refs):
        @pl.when(pl.program_id(0) == 0)
        def _():
            for at, rows in _W_IN_PADS:
                o_refs[1][pl.ds(at, rows), :] = jnp.zeros((rows, D), blocks.dtype)
        for j in range(N_DEV):
            @pl.when(pl.program_id(0) == j)
            def _(j=j):
                for at, rows, group, to in _w_in_pieces(j):
                    o_refs[group][pl.ds(to, rows), :] = x_ref[pl.ds(at, rows), :]

    return pl.pallas_call(
        body, name="w_in_groups", grid=(N_DEV,),
        in_specs=[pl.BlockSpec((None, W_IN_SHARD, D), lambda j: (j, 0, 0))],
        out_specs=[pl.BlockSpec((r, D), lambda j: (0, 0)) for r in _W_IN_GROUP_ROWS],
        out_shape=[jax.ShapeDtypeStruct((r, D), blocks.dtype) for r in _W_IN_GROUP_ROWS],
        compiler_params=pltpu.CompilerParams(dimension_semantics=("arbitrary",), vmem_limit_bytes=VMEM_LIMIT),
    )(blocks)


def _w_in_grad_blocks(g_sgu_t, g_rw_t, g_gate_t):
    def body(*refs):
        g_refs, o_ref = refs[:3], refs[3]
        for j in range(N_DEV):
            @pl.when(pl.program_id(0) == j)
            def _(j=j):
                for at, rows, group, to in _w_in_pieces(j):
                    o_ref[pl.ds(at, rows), :] = g_refs[group][pl.ds(to, rows), :]

    return pl.pallas_call(
        body, name="w_in_grad_blocks", grid=(N_DEV,),
        in_specs=[pl.BlockSpec((r, D), lambda j: (0, 0)) for r in _W_IN_GROUP_ROWS],
        out_specs=pl.BlockSpec((None, W_IN_SHARD, D), lambda j: (j, 0, 0)),
        out_shape=jax.ShapeDtypeStruct((N_DEV, W_IN_SHARD, D), g_sgu_t.dtype),
        compiler_params=pltpu.CompilerParams(dimension_semantics=("arbitrary",), vmem_limit_bytes=VMEM_LIMIT),
    )(g_sgu_t, g_rw_t, g_gate_t)


def _mesh_index():
    me = 4 * lax.axis_index("x") + 2 * lax.axis_index("y") + lax.axis_index("c")
    return me.astype(jnp.int32).reshape(1)


def _fill_slot(name, dst, src, idx):
    R, Wd = dst.shape[1:]

    def body(idx_ref, dst_ref, src_ref, out_ref):
        out_ref[...] = src_ref[...]

    return pl.pallas_call(
        body, name=name,
        grid_spec=pltpu.PrefetchScalarGridSpec(
            num_scalar_prefetch=1, grid=(1,),
            in_specs=[pl.BlockSpec(memory_space=pl.ANY), pl.BlockSpec((R, Wd), lambda i, s: (0, 0))],
            out_specs=pl.BlockSpec((None, R, Wd), lambda i, s: (s[0], 0, 0))),
        out_shape=jax.ShapeDtypeStruct(dst.shape, dst.dtype),
        input_output_aliases={1: 0},
        compiler_params=pltpu.CompilerParams(vmem_limit_bytes=VMEM_LIMIT),
    )(idx, dst, src)


class TwoLevelGather:
    N_COPIES = 8
    PART_ALIGN = 16

    def __init__(self, bufs, skip_own=()):
        self.bufs, self.nb, self.skip_own = list(bufs), len(bufs), tuple(skip_own)
        self.any_specs = [pl.BlockSpec(memory_space=pl.ANY)] * self.nb
        self.out_shape = [jax.ShapeDtypeStruct((N_DEV,) + b.shape, b.dtype) for b in self.bufs]
        n = self.N_COPIES * self.nb
        self.sem_shapes = [pltpu.SemaphoreType.DMA((n,)), pltpu.SemaphoreType.DMA((n,)),
                           pltpu.SemaphoreType.DMA((self.nb,))]

    def _parts(self, b):
        shape = self.bufs[b].shape
        first = shape[0] // 2 // self.PART_ALIGN * self.PART_ALIGN if len(shape) == 2 else 0
        return [(0, first), (first, shape[0] - first)] if first else [None]

    def _copies(self, in_refs, out_refs, sems):
        send_sems, recv_sems, local_sems = sems
        nb = self.nb
        x, y, c = lax.axis_index("x"), lax.axis_index("y"), lax.axis_index("c")
        me, sibling = (x, y, c), (x, y, 1 - c)
        near, far = [(1 - x, y), (x, 1 - y)], (1 - x, 1 - y)

        def slot(b, dev):
            return out_refs[b].at[4 * dev[0] + 2 * dev[1] + dev[2]]

        def copy(b, k, block, to, own=False, rows=None):
            src, dst = in_refs[b] if own else slot(b, block), slot(b, block)
            if rows is not None:
                src, dst = src.at[pl.ds(*rows)], dst.at[pl.ds(*rows)]
            return pltpu.make_async_remote_copy(
                src_ref=src, dst_ref=dst, send_sem=send_sems.at[self.N_COPIES * b + k],
                recv_sem=recv_sems.at[self.N_COPIES * b + k], device_id=to, device_id_type=pl.DeviceIdType.MESH)

        ways = [(j, b) for j in range(2) for b in range(nb) if j < len(self._parts(b))]
        cp = {}
        cp["local"] = [pltpu.make_async_copy(in_refs[b], slot(b, me), local_sems.at[b]) for b in range(nb)
                       if b not in self.skip_own]
        cp["first"] = [copy(b, 0, me, sibling, own=True) for b in range(nb)]
        cp["first"] += [copy(b, 1 + j, me, (*near[j], c), own=True) for j in range(2) for b in range(nb)]
        cp["from_near"] = [copy(b, 1 + j, (*near[j], c), me) for j in range(2) for b in range(nb)]
        cp["near_on"] = [[copy(b, 5 + j, (*near[j], c), sibling)]
                         + ([copy(b, 3 + j, (*near[j], c), (*near[1 - j], c), rows=self._parts(b)[j])]
                            if (j, b) in ways else []) for j in range(2) for b in range(nb)]
        cp["from_far"] = [copy(b, 3 + j, (*far, c), me, rows=self._parts(b)[j]) for j, b in ways]
        cp["far_on"] = [copy(b, 7, (*far, c), sibling) for b in range(nb)]
        cp["from_sibling"] = [copy(b, 0, sibling, me) for b in range(nb)]
        cp["from_sibling"] += [copy(b, 5 + j, (*near[j], 1 - c), me) for j in range(2) for b in range(nb)]
        cp["from_sibling"] += [copy(b, 7, (*far, 1 - c), me) for b in range(nb)]
        return cp

    def start(self, in_refs, out_refs, sems):
        cp = self._copies(in_refs, out_refs, sems)
        for c in cp["first"] + cp["local"]:
            c.start()

    def pass_near(self, in_refs, out_refs, sems):
        cp = self._copies(in_refs, out_refs, sems)
        for arrived, onward in zip(cp["from_near"], cp["near_on"]):
            arrived.wait_recv()
            for c in onward:
                c.start()

    def pass_far(self, in_refs, out_refs, sems):
        cp = self._copies(in_refs, out_refs, sems)
        for c in cp["from_far"]:
            c.wait_recv()
        for c in cp["far_on"]:
            c.start()

    def finish(self, in_refs, out_refs, sems):
        cp = self._copies(in_refs, out_refs, sems)
        for c in cp["from_sibling"]:
            c.wait_recv()
        for c in cp["first"] + sum(cp["near_on"], []) + cp["far_on"]:
            c.wait_send()
        for c in cp["local"]:
            c.wait()

    def schedule(self, n_steps):
        return [(0, self.start), (max(n_steps // 2 - 1, 0), self.pass_near), (max(n_steps - 3, 0), self.pass_far),
                (n_steps - 1, self.finish)]


def _all_gather_two_level(name, bufs, skip_own=()):
    ex = TwoLevelGather(bufs, skip_own)

    def body(*refs):
        args = refs[:ex.nb], refs[ex.nb:2 * ex.nb], refs[2 * ex.nb:]
        for _, action in ex.schedule(1):
            action(*args)

    return pl.pallas_call(body, name=name, in_specs=ex.any_specs, out_specs=ex.any_specs, out_shape=ex.out_shape,
                          scratch_shapes=ex.sem_shapes)(*ex.bufs)


def _cols_from_blocks(blk):
    return jnp.transpose(blk, (1, 0, 2)).reshape(blk.shape[1], -1)


def _cols_to_blocks(g):
    r, c = g.shape
    return jnp.transpose(g.reshape(r, N_DEV, c // N_DEV), (1, 0, 2))


FIRST_WEIGHTS = ["w_in", "shift_b", "w_lora_w", "a_lora_w", "g_lora_w"]
LATE_WEIGHTS = ["w_proj_a", "w_proj_b", "w_out", "w_ffn1", "w_ffn2"]
SCAN_CARRIED = ["w_proj_a", "w_proj_b", "w_out"]
FFN_WEIGHTS = ["w_ffn1", "w_ffn2"]


def _late_weights(shards):
    ex = TwoLevelGather([shards[n][0].astype(BF16) for n in LATE_WEIGHTS])

    def finish(results):
        got = dict(zip(LATE_WEIGHTS, results))
        W = {n: got[n].reshape(-1, D) for n in ("w_proj_a", "w_proj_b", "w_out", "w_ffn2")}
        W["w_ffn1"] = got["w_ffn1"].reshape(N_DEV, D, -1)
        return W
    return ex, finish


def _gather_weights(shards):
    def payload(n):
        if n == "w_in":
            return jnp.transpose(shards[n][0]).astype(BF16)
        return shards[n] if n == "shift_b" else shards[n].astype(BF16)
    payloads = [payload(n) for n in FIRST_WEIGHTS]
    got = dict(zip(FIRST_WEIGHTS, _all_gather_two_level("weight_all_gather", payloads, skip_own=(0,))))
    got["w_in"] = _fill_slot("w_in_own_slot", got["w_in"], payloads[0], _mesh_index())
    W = {}
    W["w_sgu_t"], W["w_rw_t"], W["w_gate_t"] = _w_in_groups_t(got["w_in"])
    z = lambda r, c, dt: jnp.zeros((r, c), dt)
    W["w_lora"] = jnp.concatenate([_cols_from_blocks(got["w_lora_w"][:, 0]).astype(F32), z(128 - L_W, D, F32)], axis=0)
    W["a_lora"] = jnp.concatenate([_cols_from_blocks(got["a_lora_w"][:, 0]).astype(F32), z(128 - L_A, D, F32)], axis=0)
    W["g_lora"] = jnp.concatenate([_cols_from_blocks(got["g_lora_w"][:, 0]).astype(F32), z(256 - L_G, D, F32)], axis=0)
    sb = _cols_from_blocks(got["shift_b"][:, 0])
    W["sb"] = jnp.concatenate([sb[:, :3 * D], sb[:, 3 * D:3 * D + L_W], z(2, 128 - L_W, F32),
                               sb[:, 3 * D + L_W:3 * D + L_W + L_A], z(2, 128 - L_A, F32),
                               sb[:, 3 * D + L_W + L_A:], z(2, 256 - L_G, F32)], axis=1)
    return W


def _replicated_weights(rep):
    W = {n: rep[n] for n in ("g_mix", "sgu_ln_w", "sgu_ln_b", "w0", "a0", "k_k", "k_a", "r_k", "ln_x_w", "ln_x_b",
                             "g_ffn")}
    W["g_final"] = rep["g_final"].reshape(1, D)
    W["sgu_w"] = rep["sgu_w"][0]
    W["sgu_bt"] = jnp.transpose(rep["sgu_b"][0])
    return W


def _late_grad_blocks(G):
    return Exchange([G[n].reshape(N_DEV, -1, D) for n in SCAN_CARRIED]
                    + [G["sgu_w"].reshape(SGU_G * SGU_C, SGU_C).astype(GRAD_PAYLOAD)],
                    [False] * len(SCAN_CARRIED) + [True])


def _first_grad_blocks(G):
    sbg = G["sb"]
    c = 3 * D
    sb = jnp.concatenate([sbg[:, :c], sbg[:, c:c + L_W], sbg[:, c + 128:c + 128 + L_A],
                          sbg[:, c + 256:c + 256 + L_G]], axis=1)
    return {
        "shift_b": _cols_to_blocks(sb),
        "w_lora_w": _cols_to_blocks(G["w_lora"][:L_W]), "a_lora_w": _cols_to_blocks(G["a_lora"][:L_A]),
        "g_lora_w": _cols_to_blocks(G["g_lora"][:L_G]),
    }


def _replicated_grads(G):
    small = {n: G[n] for n in ("g_mix", "sgu_ln_w", "sgu_ln_b", "w0", "a0", "k_k", "k_a", "r_k", "ln_x_w", "ln_x_b",
                               "g_ffn", "g_final")}
    small["sgu_w"] = G["sgu_w"]
    small["sgu_b"] = jnp.transpose(G["sgu_bt"])
    return small


def kernel(x, g_mix, w_in, sgu_ln_w, sgu_ln_b, sgu_w, sgu_b, w_proj_a, shift_b, w_lora_w, w0, a_lora_w, a0, g_lora_w, k_k, k_a, r_k, ln_x_w, ln_x_b, w_proj_b, w_out, g_ffn, w_ffn1, w_ffn2, g_final, loss_target, m_g_mix, m_w_in, m_sgu_ln_w, m_sgu_ln_b, m_sgu_w, m_sgu_b, m_w_proj_a, m_shift_b, m_w_lora_w, m_w0, m_a_lora_w, m_a0, m_g_lora_w, m_k_k, m_k_a, m_r_k, m_ln_x_w, m_ln_x_b, m_w_proj_b, m_w_out, m_g_ffn, m_w_ffn1, m_w_ffn2, m_g_final, v_g_mix, v_w_in, v_sgu_ln_w, v_sgu_ln_b, v_sgu_w, v_sgu_b, v_w_proj_a, v_shift_b, v_w_lora_w, v_w0, v_a_lora_w, v_a0, v_g_lora_w, v_k_k, v_k_a, v_r_k, v_ln_x_w, v_ln_x_b, v_w_proj_b, v_w_out, v_g_ffn, v_w_ffn1, v_w_ffn2, v_g_final):
    env = dict(locals())
    weights = {n: env[n] for n in WEIGHT_ORDER}
    moms = {n: env["m_" + n] for n in WEIGHT_ORDER}
    vars_ = {n: env["v_" + n] for n in WEIGHT_ORDER}

    shards = {n: weights[n] for n, _, _ in SHARDED}
    W = _gather_weights(shards)
    W.update(_replicated_weights({n: weights[n] for n, _ in REPLICATED}))
    in_flight = {}

    def send_w_in_grads(G):
        blocks = _w_in_grad_blocks(G["w_sgu_t"], G["w_rw_t"], G["w_gate_t"])
        got = _pair_exchange("grad_pair_exchange", blocks)
        core = lax.axis_index("c").astype(jnp.int32).reshape(1)
        sums = _pair_sum("grad_pair_sum", blocks, got, core)
        in_flight["sems"], in_flight["sums"], in_flight["land"], token = _chip_exchange_start("grad_chip_start", sums)
        return token

    def send_ffn_grads(G):
        in_flight["ffn"] = _scatter_start("grad_ffn_start", [G["w_ffn1"], G["w_ffn2"].reshape(N_DEV, -1, D)])
        return in_flight["ffn"][3]

    loss_part, dx, G, late_slots = _local_step(x[0], loss_target[0], W, late_weights=_late_weights(shards),
                                               early_grads=_late_grad_blocks, w_in_grads_ready=send_w_in_grads,
                                               ffn_grads_ready=send_ffn_grads)

    slots = dict(zip(SCAN_CARRIED, late_slots))
    _, landed = _scatter_wait("grad_ffn_wait", *in_flight["ffn"][:3], after=dx)
    slots.update(zip(FFN_WEIGHTS, landed))
    outs = [dict(), dict(), dict(), dict()]

    def keep(n, res):
        for k in range(4):
            outs[k][n] = res[k]

    def update_group(name, group):
        group_out = _adamw_whole(name, [(slots[n], weights[n], moms[n], vars_[n]) for n in group])
        for n, res in zip(group, group_out):
            keep(n, res)

    for n in FFN_WEIGHTS:
        keep(n, _adamw("adamw_" + n, slots[n], weights[n], moms[n], vars_[n]))
    update_group("adamw_projections", SCAN_CARRIED)
    sgu_shape = (SGU_G * SGU_C, SGU_C)
    res = _adamw("adamw_sgu_w", late_slots[len(SCAN_CARRIED)], *[t.reshape(sgu_shape) for t in
                                                                  (weights["sgu_w"], moms["sgu_w"], vars_["sgu_w"])])
    keep("sgu_w", [t.reshape(weights["sgu_w"].shape) for t in res])
    updated = [outs[1][n] for n in FFN_WEIGHTS + SCAN_CARRIED + ["sgu_w"]]

    blocks = _first_grad_blocks(G)
    small = _replicated_grads(G)
    small_parts = [small[n] for n, _ in PACKED] + [jnp.full((PACK_W,), loss_part, F32)]
    rest = [n for n in FIRST_WEIGHTS if n != "w_in"]
    packed = _pack_rows("grad_pack", small_parts, _SMALL_ROWS, after=updated)
    res = _exchange("grad_exchange", [blocks[n] for n in rest] + [packed], [False] * len(rest) + [True])
    slots.update(zip(rest, res[:-1]))
    small_slots = res[-1]
    _, slots["w_in"] = _chip_exchange_wait("grad_chip_wait", in_flight["sems"], in_flight["sums"], in_flight["land"],
                                           after=small_slots)

    res = _adamw("adamw_w_in", slots["w_in"], *[jnp.transpose(t, (2, 0, 1)) for t in (weights["w_in"], moms["w_in"], vars_["w_in"])])
    keep("w_in", [jnp.transpose(t, (1, 2, 0)) for t in res])
    update_group("adamw_low_rank", rest)

    small_out, after_them = _adamw_rows(
        "adamw_replicated", small_slots,
        [[d[n].reshape(-1, PACK_W) for d in (weights, moms, vars_)] for n, _ in PACKED])
    for (n, s), res in zip(PACKED, small_out):
        for k in range(4):
            outs[k][n] = res[k].reshape(s)
    loss = after_them[0, 0]
    return (loss, dx[None], *[outs[0][n] for n in WEIGHT_ORDER], *[outs[1][n] for n in WEIGHT_ORDER],
            *[outs[2][n] for n in WEIGHT_ORDER], *[outs[3][n] for n in WEIGHT_ORDER])
```

```python
import functools
import numpy as np
import jax
import jax.numpy as jnp
from jax import lax
from jax.experimental import pallas as pl
from jax.experimental.pallas import tpu as pltpu

F32 = jnp.float32
BF16 = jnp.bfloat16

D = 1024
NH, HN = 16, 64
NP, PW = NH // 2, 2 * HN
SGU_G, SGU_C = 8, 128
L_W, L_A, L_G = 64, 64, 160
C_B = 3 * D + L_W + L_A + L_G
P_TOTAL = 2 * D + C_B + 2 * D
D_FF = 4 * D
RW_INT = 3 * D + 128 + 128 + 256
NORM_EPS, LN_EPS, GN_EPS = 1e-6, 1e-5, 64e-5
N_DEV = 8
LANES = 128
SCAN_C = 64
N_KEPT = 4
SOLVE_B = 16
SCAN_PRECISION = lax.Precision.HIGH
SCAN_OUT_PRECISION = lax.Precision.DEFAULT
GRAD_PAYLOAD = BF16
VMEM_LIMIT = 56 * 1024 * 1024
MATMUL_VMEM_BUDGET = 40 * 1024 * 1024
STEP_COST_BYTES = 512 * 1024
HBM_COST_RATIO = 3
ACC_PASS_WEIGHT = 4

ADAM_LR, ADAM_B1, ADAM_B2, ADAM_EPS, ADAM_WD, ADAM_STEP = 0.001, 0.9, 0.999, 1e-08, 0.01, 10

SHARDED = [
    ("w_in", (D, P_TOTAL), 1), ("w_proj_a", (D, D), 0), ("shift_b", (2, C_B), 1), ("w_lora_w", (L_W, D), 1),
    ("a_lora_w", (L_A, D), 1), ("g_lora_w", (L_G, D), 1), ("w_proj_b", (D, D), 0), ("w_out", (D, D), 0),
    ("w_ffn1", (D, D_FF), 1), ("w_ffn2", (D_FF, D), 0),
]
REPLICATED = [
    ("g_mix", (1, D)), ("sgu_ln_w", (1, D)), ("sgu_ln_b", (1, D)), ("sgu_w", (1, SGU_G, SGU_C, SGU_C)),
    ("sgu_b", (1, SGU_G, SGU_C)), ("w0", (1, D)), ("a0", (1, D)), ("k_k", (1, D)), ("k_a", (1, D)), ("r_k", (1, D)),
    ("ln_x_w", (1, D)), ("ln_x_b", (1, D)), ("g_ffn", (1, D)), ("g_final", (D,)),
]
WEIGHT_ORDER = ["g_mix", "w_in", "sgu_ln_w", "sgu_ln_b", "sgu_w", "sgu_b", "w_proj_a", "shift_b", "w_lora_w", "w0",
                "a_lora_w", "a0", "g_lora_w", "k_k", "k_a", "r_k", "ln_x_w", "ln_x_b", "w_proj_b", "w_out", "g_ffn",
                "w_ffn1", "w_ffn2", "g_final"]


def _round_up(n, m):
    return (n + m - 1) // m * m


def _pick(n, target):
    if n <= target:
        return n
    best = None
    for t in range(LANES, target + 1, LANES):
        if n % t == 0:
            best = t
    assert best is not None, (n, target)
    return best


def _matmul(name, a, b, mode, out_dtype=F32, tm=2048, tn=1024, tk=4096, out_blocks=None, epilogue=None, extras=(),
            out_dtypes=(), after=None, whole_rows=False, out_widths=None):
    b_blocks = b.shape[0] if b.ndim == 3 else None
    bshape = b.shape if b.ndim == 2 else (b.shape[1], b.shape[0] * b.shape[2])
    if mode == "nn":
        (M, K), (K2, N) = a.shape, bshape
    elif mode == "nt":
        (M, K), (N, K2) = a.shape, bshape
    else:
        (K, M), (K2, N) = a.shape, bshape
    assert K == K2, (name, a.shape, b.shape)
    assert b_blocks is None or mode != "tn"
    assert out_blocks is None or mode == "tn"
    tn = min(tn, N // (out_blocks or 1), bshape[1] // b_blocks if (b_blocks and mode == "nn") else tn)
    blocked_k = bool(b_blocks) and mode == "nt"
    tm, tn, tk = _pick(M, tm), _pick(N, tn), (K if blocked_k else _pick(K, tk))

    def vmem_bytes(tm, tk):
        tiles = tm * tk * a.dtype.itemsize + tk * tn * b.dtype.itemsize
        for i, dt in enumerate(out_dtypes if epilogue else (out_dtype,)):
            tiles += tm * (out_widths[i] if out_widths else tn) * jnp.dtype(dt).itemsize
        for x in extras:
            arr = x[0] if isinstance(x, tuple) else x
            tiles += (tm if arr.shape[0] > 1 else 1) * tn * arr.dtype.itemsize
        return 2 * tiles + (tm * tn * 4 if K // tk > 1 else 0)

    def cost(tm, tk):
        ni, nj, nk = M // tm, N // tn, K // tk
        steps = ni * nj * nk
        acc_passes = steps * tm * tn * 8 * ACC_PASS_WEIGHT if nk > 1 else 0
        a_reads = M * K * a.dtype.itemsize * (nj if nk > 1 else 1)
        b_reads = K * N * b.dtype.itemsize * (ni if (nj > 1 or nk > 1) else 1)
        return (steps * STEP_COST_BYTES + acc_passes + vmem_bytes(tm, tk) // 2
                + HBM_COST_RATIO * (a_reads + b_reads))

    options = [(m, k) for m in ({M} if whole_rows else {_pick(M, max(t, LANES)) for t in (tm, tm // 2, tm // 4)})
               for k in ({K} if blocked_k else {_pick(K, max(t, LANES)) for t in (tk, tk // 2, tk // 4)})
               if vmem_bytes(m, k) <= MATMUL_VMEM_BUDGET]
    tm, tk = min(options, key=lambda o: cost(*o))
    nk = K // tk
    dims = {"nn": (((1,), (0,)), ((), ())), "nt": (((1,), (1,)), ((), ())), "tn": (((0,), (0,)), ((), ()))}[mode]

    n_x, n_o = len(extras), len(out_dtypes) if epilogue else 1
    n_after = 0 if after is None else 1

    def body(a_ref, b_ref, *rest):
        x_refs, o_refs, acc = rest[:n_x], rest[n_x + n_after:n_x + n_after + n_o], rest[n_x + n_after + n_o:]
        if blocked_k:
            bw = b.shape[2]
            part = sum(lax.dot_general(a_ref[:, blk * bw:(blk + 1) * bw].astype(BF16), b_ref[blk].astype(BF16), dims,
                                       preferred_element_type=F32) for blk in range(b_blocks))
        else:
            part = lax.dot_general(a_ref[...].astype(BF16), b_ref[...].astype(BF16), dims, preferred_element_type=F32)

        def finish(res):
            outs = epilogue(res, *[r[...] for r in x_refs]) if epilogue else (res,)
            for r, v in zip(o_refs, outs):
                r[...] = v.astype(r.dtype)

        if nk == 1:
            finish(part)
            return
        acc_ref, k = acc[0], pl.program_id(2)

        @pl.when(k == 0)
        def _():
            acc_ref[...] = part

        @pl.when(k > 0)
        def _():
            acc_ref[...] += part

        @pl.when(k == nk - 1)
        def _():
            finish(acc_ref[...])

    a_spec = {"nn": pl.BlockSpec((tm, tk), lambda i, j, k: (i, k)), "nt": pl.BlockSpec((tm, tk), lambda i, j, k: (i, k)),
              "tn": pl.BlockSpec((tk, tm), lambda i, j, k: (k, i))}[mode]
    b_spec = {"nn": pl.BlockSpec((tk, tn), lambda i, j, k: (k, j)), "nt": pl.BlockSpec((tn, tk), lambda i, j, k: (j, k)),
              "tn": pl.BlockSpec((tk, tn), lambda i, j, k: (k, j))}[mode]
    if b_blocks and mode == "nn":
        per = b.shape[2] // tn
        b_spec = pl.BlockSpec((None, tk, tn), lambda i, j, k: (j // per, k, j % per))
    elif b_blocks:
        b_spec = pl.BlockSpec((b_blocks, tn, b.shape[2]), lambda i, j, k: (0, j, 0))
    out_spec = pl.BlockSpec((tm, tn), lambda i, j, k: (i, j))
    out_shape = jax.ShapeDtypeStruct((M, N), out_dtype)
    if out_blocks:
        per_o = N // out_blocks // tn
        out_spec = pl.BlockSpec((None, tm, tn), lambda i, j, k: (j // per_o, i, j % per_o))
        out_shape = jax.ShapeDtypeStruct((out_blocks, M, N // out_blocks), out_dtype)
    epi_widths = list(out_widths) if out_widths else [N] * n_o
    assert all(w == N for w in epi_widths) or N == tn
    epi_specs = [pl.BlockSpec((tm, tn if w == N else w), lambda i, j, k: (i, j)) for w in epi_widths]
    x_specs, x_args = [], []
    for x in extras:
        arr, off = x if isinstance(x, tuple) else (x, 0)
        if arr.shape[0] == 1:
            x_specs.append(pl.BlockSpec((1, tn), lambda i, j, k: (0, j)))
        else:
            x_specs.append(pl.BlockSpec((tm, tn), lambda i, j, k, off=off: (i, j + off)))
        x_args.append(arr)
    res = pl.pallas_call(
        body, name=name, grid=(M // tm, N // tn, nk),
        in_specs=[a_spec, b_spec] + x_specs + [pl.BlockSpec(memory_space=pl.ANY)] * n_after,
        out_specs=epi_specs if epilogue else out_spec,
        out_shape=[jax.ShapeDtypeStruct((M, w), dt) for w, dt in zip(epi_widths, out_dtypes)] if epilogue else out_shape,
        scratch_shapes=[pltpu.VMEM((tm, tn), F32)] if nk > 1 else [],
        compiler_params=pltpu.CompilerParams(dimension_semantics=("parallel", "parallel", "arbitrary"),
                                             vmem_limit_bytes=VMEM_LIMIT),
    )(a, b, *x_args, *([after] if n_after else []))
    return res


class Rows:
    def __init__(self, arr, width=None, cb=0):
        self.arr, self.width, self.cb = arr, (arr.shape[1] if width is None else width), cb


class Heads:
    def __init__(self, arr):
        self.arr = arr


class Halo:
    def __init__(self, arr, side):
        self.arr, self.side = arr, side


def _rows_call(name, fn, ins, consts, outs, accs=(), tm=512, with_pid=False):
    T = next(o.arr.shape[1] if isinstance(o, Heads) else o.arr.shape[0] for o in ins if not isinstance(o, Halo))
    tm = min(tm, T)
    n_tiles = T // tm
    n_in, n_c, n_out = len(ins), len(consts), len(outs)
    in_specs, args = [], []
    for o in ins:
        if isinstance(o, Rows):
            in_specs.append(pl.BlockSpec((tm, o.width), lambda i, cb=o.cb: (i, cb)))
        elif isinstance(o, Heads):
            in_specs.append(pl.BlockSpec((NP, tm, PW), lambda i: (0, i, 0)))
        else:
            w = o.arr.shape[1]
            if o.side < 0:
                in_specs.append(pl.BlockSpec((8, w), lambda i: (jnp.maximum(i * (tm // 8) - 1, 0), 0)))
            else:
                in_specs.append(pl.BlockSpec((8, w), lambda i: (jnp.minimum((i + 1) * (tm // 8), T // 8 - 1), 0)))
        args.append(o.arr)
    for c in consts:
        in_specs.append(pl.BlockSpec(c.shape, lambda i, nd=c.ndim: (0,) * nd))
        args.append(c)
    out_specs, out_shape = [], []
    for o in outs:
        if o[0] == "rows":
            out_specs.append(pl.BlockSpec((tm, o[1]), lambda i: (i, 0)))
            out_shape.append(jax.ShapeDtypeStruct((T, o[1]), o[2]))
        else:
            out_specs.append(pl.BlockSpec((NP, tm, PW), lambda i: (0, i, 0)))
            out_shape.append(jax.ShapeDtypeStruct((NP, T, PW), o[1]))
    for shape, dt in accs:
        out_specs.append(pl.BlockSpec(shape, lambda i, nd=len(shape): (0,) * nd))
        out_shape.append(jax.ShapeDtypeStruct(shape, dt))

    def body(*refs):
        i = pl.program_id(0)
        vals = []
        vals = [r[...] for r in refs[:n_in + n_c]]
        res = fn(i, n_tiles, *vals) if with_pid else fn(*vals)
        out_refs = refs[n_in + n_c:]
        for r, v in zip(out_refs[:n_out], res[:n_out]):
            r[...] = v.astype(r.dtype)
        if accs:
            @pl.when(i == 0)
            def _():
                for r in out_refs[n_out:]:
                    r[...] = jnp.zeros_like(r)

            for r, v in zip(out_refs[n_out:], res[n_out:]):
                r[...] += v.astype(r.dtype)

    res = pl.pallas_call(
        body, name=name, grid=(n_tiles,), in_specs=in_specs, out_specs=out_specs, out_shape=out_shape,
        compiler_params=pltpu.CompilerParams(dimension_semantics=("arbitrary",), vmem_limit_bytes=VMEM_LIMIT),
    )(*args)
    return res


def _rms(x, g):
    return x * lax.rsqrt(jnp.mean(x * x, axis=-1, keepdims=True) + NORM_EPS) * g


def _gelu(x):
    return 0.5 * x * (1.0 + lax.erf(x * 0.7071067811865476))


def _sigmoid(x):
    return 1.0 / (1.0 + jnp.exp(-x))


def _bdot(a, b):
    return jnp.dot(a.astype(BF16), b.astype(BF16), preferred_element_type=F32)


def _to_heads(x):
    return jnp.concatenate([x[:, p * PW:(p + 1) * PW][None] for p in range(NP)], axis=0)


def _from_heads(xp):
    return jnp.concatenate([xp[p] for p in range(NP)], axis=-1)


def _head_sum(xp):
    low = lax.broadcasted_iota(jnp.int32, xp.shape, xp.ndim - 1) < HN
    both = jnp.sum(xp, axis=-1, keepdims=True)
    first = jnp.sum(jnp.where(low, xp, 0.0), axis=-1, keepdims=True)
    return jnp.where(low, first, both - first)


def _split_pairs(xp):
    return jnp.concatenate([xp[:, :, :HN], xp[:, :, HN:]], axis=0)


def _join_pairs(xh):
    return jnp.concatenate([xh[:NP], xh[NP:]], axis=-1)


def _sgu_fn(p, ln_w, ln_b, sw, sbt):
    z = _gelu(p)
    u, v = z[:, :D], z[:, D:]
    mu = jnp.mean(v, axis=-1, keepdims=True)
    var = jnp.mean(jnp.square(v - mu), axis=-1, keepdims=True)
    vn = (v - mu) * lax.rsqrt(var + LN_EPS) * ln_w + ln_b
    ri = lax.broadcasted_iota(jnp.int32, (SGU_C, SGU_C), 0)
    ci = lax.broadcasted_iota(jnp.int32, (SGU_C, SGU_C), 1)
    mask = (ci <= ri).astype(F32)
    dg = D // SGU_G
    parts = []
    for g in range(SGU_G):
        parts.append(_bdot(sw[g] * mask, vn[:, g * dg:(g + 1) * dg]) + sbt[:, g:g + 1])
    return u * jnp.concatenate(parts, axis=-1)


def _pre_fn(qr, qk, qv, qxw, qxa, qxg, wl, w0, al, a0, gl, k_k, k_a):
    w = -jax.nn.softplus(-(w0 + _bdot(jnp.tanh(qxw), wl))) - 0.5
    lw = -jnp.exp(w)
    aa = _sigmoid(a0 + _bdot(qxa, al))
    g = _bdot(_sigmoid(qxg), gl)
    kk = _to_heads(qk * k_k)
    kk = kk / jnp.maximum(jnp.sqrt(_head_sum(kk * kk)), 1e-12)
    k2 = qk * (1.0 + (aa - 1.0) * k_a)
    return _to_heads(qr), _to_heads(lw), _to_heads(k2), _to_heads(qv), kk, _to_heads(aa), g


def _post_fn(o, r, k2, v, g, ln_w, ln_b, r_k):
    mu = _head_sum(o) * (1.0 / HN)
    d = o - mu
    var = _head_sum(d * d) * (1.0 / HN)
    on = d * lax.rsqrt(var + GN_EPS) * ln_w + ln_b
    bonus = _head_sum(r * k2 * r_k) * v
    return _from_heads(on + bonus) * g


def _gate_fn(pg, ya, yb):
    return _sigmoid(pg[:, :D]) * ya + _sigmoid(pg[:, D:]) * yb


def _bmm(x, y, cx, cy, out_path=False):
    return lax.dot_general(x, y, (((cx,), (cy,)), ((0,), (0,))),
                           precision=SCAN_OUT_PRECISION if out_path else SCAN_PRECISION, preferred_element_type=F32)


def _unit_lower_inverse(M):
    C = M.shape[1]
    ti = lax.broadcasted_iota(jnp.int32, (C, C), 0)
    tj = lax.broadcasted_iota(jnp.int32, (C, C), 1)
    eye = (ti == tj).astype(F32)
    same = lambda b: (ti // b == tj // b).astype(F32)
    X = -(M * same(SOLVE_B))
    inv = eye + X
    span = 1
    while 2 * span < SOLVE_B:
        X = _bmm(X, X, 2, 1)
        inv = inv + _bmm(inv, X, 2, 1)
        span *= 2
    b = SOLVE_B
    while b < C:
        low = M * (same(2 * b) - same(b))
        inv = inv - _bmm(_bmm(inv, low, 2, 1, out_path=True), inv, 2, 1, out_path=True)
        b *= 2
    return inv


@jax.custom_vjp
def _unit_lower_solve(inv, M, y):
    return _bmm(inv, y, 2, 1)


def _unit_lower_solve_fwd(inv, M, y):
    u = _bmm(inv, y, 2, 1)
    return u, (inv, u)


def _unit_lower_solve_bwd(res, du):
    inv, u = res
    dy = _bmm(inv, du, 1, 1)
    return jnp.zeros_like(inv), -_bmm(dy, u, 2, 2), dy


_unit_lower_solve.defvjp(_unit_lower_solve_fwd, _unit_lower_solve_bwd)


@jax.custom_vjp
def _unit_lower_solved(inv, u, M, y):
    return u


_unit_lower_solved.defvjp(lambda inv, u, M, y: (u, (inv, u)),
                          lambda res, du: (jnp.zeros_like(res[0]), jnp.zeros_like(res[1]))
                          + _unit_lower_solve_bwd(res, du)[1:])


@functools.partial(jax.custom_vjp, nondiff_argnums=(0,))
def _kept(fn, value, *args):
    return value


def _kept_fwd(fn, value, *args):
    return value, args


def _kept_bwd(fn, args, d):
    _, vjp = jax.vjp(fn, *args)
    return (jnp.zeros_like(d),) + tuple(vjp(d))


_kept.defvjp(_kept_fwd, _kept_bwd)


def _sum_over_time(x, reverse):
    C = x.shape[1]
    ti = lax.broadcasted_iota(jnp.int32, (C, C), 0)
    tj = lax.broadcasted_iota(jnp.int32, (C, C), 1)
    ones = jnp.broadcast_to(((tj >= ti) if reverse else (tj <= ti)).astype(BF16), (x.shape[0], C, C))
    hi = x.astype(BF16)
    r1 = x - hi.astype(F32)
    mid = r1.astype(BF16)
    lo = (r1 - mid.astype(F32)).astype(BF16)
    dn = (((2,), (1,)), ((0,), (0,)))
    return sum(lax.dot_general(ones, p, dn, preferred_element_type=F32) for p in (lo, mid, hi))


@jax.custom_vjp
def _time_cumsum(lw):
    return _sum_over_time(lw, reverse=False)


_time_cumsum.defvjp(lambda lw: (_sum_over_time(lw, reverse=False), None),
                    lambda _, d: (_sum_over_time(d, reverse=True),))


def _chunk_fn(S0, r, lw, k, v, kk, a, kept=None):
    C = SCAN_C
    bmm = _bmm
    ti = lax.broadcasted_iota(jnp.int32, (C, C), 0)
    tj = lax.broadcasted_iota(jnp.int32, (C, C), 1)
    incl2 = jnp.concatenate([(tj <= ti).astype(F32)] * 2, axis=1)
    strict = (tj < ti).astype(F32)
    n_mask = jnp.concatenate([jnp.zeros((C, C), F32), strict], axis=1)

    def known(name, fn, *args):
        return fn(*args) if kept is None else _kept(fn, kept[name], *args)

    cum = known("cum", _time_cumsum, lw)
    g_in, g_ex, g_inv = jnp.exp(cum), jnp.exp(cum - lw), jnp.exp(-cum)
    kkt, rt = kk * g_ex, r * g_in
    bk = jnp.concatenate([kk * a * g_inv, k * g_inv], axis=1)
    kr = jnp.concatenate([kkt, rt], axis=1)
    ratios = known("ratios", lambda x, y: bmm(x, y, 2, 2), kr, bk)
    A = ratios[:, :C]
    M = A[:, :, :C] * strict
    zv = jnp.concatenate([jnp.zeros_like(v), v], axis=1)
    s0_side = bmm(kr, S0, 2, 2, out_path=True)
    rhs = s0_side[:, :C] + bmm(A * n_mask, zv, 2, 1, out_path=True)
    if kept is None:
        inv = lax.stop_gradient(_unit_lower_inverse(M))
        y = _unit_lower_solve(inv, M, rhs)
    else:
        inv = kept["inv"]
        y = _unit_lower_solved(inv, kept["y"], M, rhs)
    z = jnp.concatenate([-y, v], axis=1)
    O = s0_side[:, C:] + bmm(ratios[:, C:] * incl2, z, 2, 1, out_path=True)
    g_end = g_in[:, C - 1:C, :]
    S1 = S0 * g_end + bmm(z, bk * g_end, 1, 1, out_path=True)
    return O, S1, dict(cum=cum, ratios=ratios, y=y, inv=inv)


def _scan_fwd(r, lw, k, v, kk, a, ex=None, tb=256):
    assert SCAN_C == HN and 2 * SCAN_C == PW
    T = r.shape[1]
    tb = min(tb, T)
    n_chunks = tb // SCAN_C
    nb = T // tb
    nx = ex.nb if ex else 0

    def body(*refs):
        r_ref, lw_ref, k_ref, v_ref, kk_ref, a_ref = refs[:6]
        x_in, (o_ref, s0_ref), x_out = refs[6:6 + nx], refs[6 + nx:8 + nx], refs[8 + nx:8 + 2 * nx]
        s_ref, sems = refs[8 + 2 * nx], refs[9 + 2 * nx:]

        plan = ex.schedule(nb) if ex else []

        @pl.when(pl.program_id(0) == 0)
        def _():
            s_ref[...] = jnp.zeros_like(s_ref)
            for at, action in plan[:1]:
                action(x_in, x_out, sems)

        def step(c, carry):
            sl = pl.ds(pl.multiple_of(c * SCAN_C, SCAN_C), SCAN_C)
            S0 = s_ref[...]
            O, S1, keep = _chunk_fn(S0, *[_split_pairs(ref[:, sl, :])
                                          for ref in (r_ref, lw_ref, k_ref, v_ref, kk_ref, a_ref)])
            o_ref[:, sl, :] = _join_pairs(O)
            s0_ref[c, 0] = jnp.concatenate([S0, keep["inv"]], axis=-1)
            s0_ref[c, 1] = jnp.concatenate([keep["cum"], keep["y"]], axis=-1)
            s0_ref[c, 2] = keep["ratios"][:, :SCAN_C]
            s0_ref[c, 3] = keep["ratios"][:, SCAN_C:]
            s_ref[...] = S1
            return carry

        lax.fori_loop(0, n_chunks, step, 0)

        for at, action in plan[1:]:
            pl.when(pl.program_id(0) == at)(functools.partial(action, x_in, x_out, sems))

    hm = pl.BlockSpec((NP, tb, PW), lambda i: (0, i, 0))
    res = pl.pallas_call(
        body, name="rwkv_scan_fwd", grid=(nb,), in_specs=[hm] * 6 + (ex.any_specs if ex else []),
        out_specs=[hm, pl.BlockSpec((n_chunks, N_KEPT, NH, HN, PW), lambda i: (i, 0, 0, 0, 0))]
        + (ex.any_specs if ex else []),
        out_shape=[jax.ShapeDtypeStruct((NP, T, PW), F32),
                   jax.ShapeDtypeStruct((T // SCAN_C, N_KEPT, NH, HN, PW), F32)]
        + (ex.out_shape if ex else []),
        scratch_shapes=[pltpu.VMEM((NH, HN, HN), F32)] + (ex.sem_shapes if ex else []),
        compiler_params=pltpu.CompilerParams(dimension_semantics=("arbitrary",), vmem_limit_bytes=VMEM_LIMIT),
    )(r, lw, k, v, kk, a, *(ex.bufs if ex else []))
    return res[0], res[1], list(res[2:])


def _scan_bwd(r, lw, k, v, kk, a, s0s, do, ex=None, tb=128):
    T = r.shape[1]
    tb = min(tb, T)
    n_chunks = tb // SCAN_C
    nb = T // tb
    nx = ex.nb if ex else 0

    def body(*refs):
        r_ref, lw_ref, k_ref, v_ref, kk_ref, a_ref, s0_ref, do_ref = refs[:8]
        x_in, (dr, dlw, dk, dv, dkk, da), x_out = refs[8:8 + nx], refs[8 + nx:14 + nx], refs[14 + nx:14 + 2 * nx]
        ds_ref, sems = refs[14 + 2 * nx], refs[15 + 2 * nx:]

        plan = ex.schedule(nb) if ex else []

        @pl.when(pl.program_id(0) == 0)
        def _():
            ds_ref[...] = jnp.zeros_like(ds_ref)
            for at, action in plan[:1]:
                action(x_in, x_out, sems)

        def step(j, carry):
            c = n_chunks - 1 - j
            sl = pl.ds(pl.multiple_of(c * SCAN_C, SCAN_C), SCAN_C)
            s0_inv, cum_y = s0_ref[c, 0], s0_ref[c, 1]
            kept = dict(inv=s0_inv[:, :, HN:], cum=cum_y[:, :, :HN], y=cum_y[:, :, HN:],
                        ratios=jnp.concatenate([s0_ref[c, 2], s0_ref[c, 3]], axis=1))
            _, vjp = jax.vjp(lambda *t: _chunk_fn(*t, kept=kept)[:2], s0_inv[:, :, :HN],
                             *[_split_pairs(ref[:, sl, :]) for ref in (r_ref, lw_ref, k_ref, v_ref, kk_ref, a_ref)])
            g = vjp((_split_pairs(do_ref[:, sl, :]), ds_ref[...]))
            ds_ref[...] = g[0]
            for ref, val in zip((dr, dlw, dk, dv, dkk, da), g[1:]):
                ref[:, sl, :] = _join_pairs(val)
            return carry

        lax.fori_loop(0, n_chunks, step, 0)

        for at, action in plan[1:]:
            pl.when(pl.program_id(0) == at)(functools.partial(action, x_in, x_out, sems))

    hm = pl.BlockSpec((NP, tb, PW), lambda i: (0, nb - 1 - i, 0))
    res = pl.pallas_call(
        body, name="rwkv_scan_bwd", grid=(nb,),
        in_specs=[hm] * 6 + [pl.BlockSpec((n_chunks, N_KEPT, NH, HN, PW), lambda i: (nb - 1 - i, 0, 0, 0, 0)), hm]
        + (ex.any_specs if ex else []),
        out_specs=[hm] * 6 + (ex.any_specs if ex else []),
        out_shape=[jax.ShapeDtypeStruct((NP, T, PW), F32)] * 6 + (ex.out_shape if ex else []),
        scratch_shapes=[pltpu.VMEM((NH, HN, HN), F32)] + (ex.sem_shapes if ex else []),
        compiler_params=pltpu.CompilerParams(dimension_semantics=("arbitrary",), vmem_limit_bytes=VMEM_LIMIT),
    )(r, lw, k, v, kk, a, s0s, do, *(ex.bufs if ex else []))
    return list(res[:6]), list(res[6:])


def _shift_down(i, p, prev8):
    first = jnp.where(i > 0, prev8[7:8, :], 0.0)
    row = lax.broadcasted_iota(jnp.int32, p.shape, 0)
    return jnp.where(row == 0, first, pltpu.roll(p, 1, axis=0))


def _mix_bwd(dq, p, sb, tm=256):
    def fn(i, n, dq, next8, p, prev8, sb):
        ps = _shift_down(i, p, prev8)
        d1 = dq * sb[1:2]
        last = jnp.where(i < n - 1, next8[0:1, :] * sb[1:2], 0.0)
        row = lax.broadcasted_iota(jnp.int32, dq.shape, 0)
        up = jnp.where(row == dq.shape[0] - 1, last, pltpu.roll(d1, dq.shape[0] - 1, axis=0))
        return (dq * sb[0:1] + up, jnp.sum(dq * p, axis=0, keepdims=True), jnp.sum(dq * ps, axis=0, keepdims=True))
    w = p.shape[1]
    return _rows_call("shift_mix_bwd", fn, [Rows(dq), Halo(dq, +1), Rows(p), Halo(p, -1)], [sb], [("rows", w, BF16)],
                      accs=[((1, w), F32), ((1, w), F32)], tm=tm, with_pid=True)


def _local_step(x, target, W, late_weights=None, early_grads=None, w_in_grads_ready=None, ffn_grads_ready=None):
    G = {}
    a = _rows_call("norm_mix_fwd", lambda x, g: (_rms(x, g),), [Rows(x)], [W["g_mix"]], [("rows", D, BF16)])[0]
    p_sgu = _matmul("proj_sgu", a, W["w_sgu_t"], "nt")
    def token_shift(p, sb0, sb1):
        row = lax.broadcasted_iota(jnp.int32, p.shape, 0)
        return p, p * sb0 + jnp.where(row == 0, 0.0, pltpu.roll(p, 1, axis=0)) * sb1
    p_rw, q = _matmul("proj_rwkv", a, W["w_rw_t"], "nt", tn=512, whole_rows=True, epilogue=token_shift,
                      extras=[W["sb"][0:1], W["sb"][1:2]], out_dtypes=(F32, F32))
    p_gate = _matmul("proj_gate", a, W["w_gate_t"], "nt")

    sgu_consts = [W["sgu_ln_w"], W["sgu_ln_b"], W["sgu_w"], W["sgu_bt"]]
    s = _rows_call("sgu_fwd", lambda *t: (_sgu_fn(*t),), [Rows(p_sgu)], sgu_consts, [("rows", D, BF16)], tm=SGU_C)[0]

    q_ins = [Rows(q, D, 0), Rows(q, D, 1), Rows(q, D, 2), Rows(q, 128, 24), Rows(q, 128, 25), Rows(q, 256, 13)]
    pre_consts = [W["w_lora"], W["w0"], W["a_lora"], W["a0"], W["g_lora"], W["k_k"], W["k_a"]]
    r_h, lw_h, k_h, v_h, kk_h, a_h, g_gate = _rows_call(
        "rwkv_pre_fwd", _pre_fn, q_ins, pre_consts, [("heads", F32)] * 6 + [("rows", D, F32)], tm=128)
    o_h, s0s, got = _scan_fwd(r_h, lw_h, k_h, v_h, kk_h, a_h, ex=late_weights[0] if late_weights else None)
    if late_weights:
        W = {**W, **late_weights[1](got)}
    y_a = _matmul("proj_a", s, W["w_proj_a"], "nn")
    post_ins = [Heads(o_h), Heads(r_h), Heads(k_h), Heads(v_h), Rows(g_gate)]
    post_consts = [W[n].reshape(NP, 1, PW) for n in ("ln_x_w", "ln_x_b", "r_k")]
    z_b = _rows_call("rwkv_post_fwd", lambda *t: (_post_fn(*t),), post_ins, post_consts, [("rows", D, BF16)], tm=128)[0]
    y_b, mixed = _matmul("proj_b", z_b, W["w_proj_b"], "nn", extras=[(p_gate, 0), (p_gate, 1), y_a],
                         epilogue=lambda yb, ga, gb, ya: (yb, _sigmoid(ga) * ya + _sigmoid(gb) * yb),
                         out_dtypes=(F32, BF16))

    def res1(mo, x, g):
        h1 = x + mo
        return h1, _rms(h1, g)
    h1, f = _matmul("proj_out", mixed, W["w_out"], "nn", extras=[x, W["g_ffn"]], epilogue=res1,
                    out_dtypes=(F32, BF16))

    def relu_sq(u):
        r = jnp.maximum(u, 0.0)
        return r, r * r
    r1, act = _matmul("ffn_up", f, W["w_ffn1"], "nn", epilogue=relu_sq, out_dtypes=(BF16, BF16))
    ff = _matmul("ffn_down", act, W["w_ffn2"], "nn")

    def head(h1, ff, tgt, g):
        def f_(h1, ff, g):
            y = _rms(h1 + ff, g)
            return 0.5 * jnp.sum(jnp.mean(jnp.square(y - tgt), axis=-1))
        loss, (dh2, _, dg) = jax.value_and_grad(f_, argnums=(0, 1, 2))(h1, ff, g)
        return dh2, jnp.full((8, LANES), loss, F32), dg
    dh2, loss_acc, G["g_final"] = _rows_call("loss_head", head, [Rows(h1), Rows(ff), Rows(target)], [W["g_final"]],
                                             [("rows", D, F32)], accs=[((8, LANES), F32), ((1, D), F32)])

    d_u1 = _matmul("ffn_down_dx", dh2, W["w_ffn2"], "nt", extras=[r1], out_dtypes=(BF16,),
                   epilogue=lambda d_act, r: (d_act * 2.0 * r.astype(F32),))[0]
    G["w_ffn2"] = _matmul("ffn_down_dw", act, dh2, "tn", out_dtype=GRAD_PAYLOAD)
    d_f = _matmul("ffn_up_dx", d_u1, W["w_ffn1"], "nt")
    G["w_ffn1"] = _matmul("ffn_up_dw", f, d_u1, "tn", out_blocks=N_DEV, out_dtype=GRAD_PAYLOAD)

    def res1_bwd(h1, d_f, dh2, g):
        _, vjp = jax.vjp(_rms, h1, g)
        dh, dg = vjp(d_f)
        return dh2 + dh, dg
    dh1, G["g_ffn"] = _rows_call("residual_norm_bwd", res1_bwd, [Rows(h1), Rows(d_f), Rows(dh2)],
                                 [W["g_ffn"]], [("rows", D, F32)], accs=[((1, D), F32)])
    ffn_token = ffn_grads_ready(G) if ffn_grads_ready else None
    def gate_bwd(d_mixed, ga, gb, ya, yb):
        _, vjp = jax.vjp(_gate_fn, jnp.concatenate([ga, gb], axis=-1), ya, yb)
        return vjp(d_mixed)
    d_gate, d_ya, d_yb = _matmul("proj_out_dx", dh1, W["w_out"], "nt", after=ffn_token, epilogue=gate_bwd,
                                 extras=[(p_gate, 0), (p_gate, 1), y_a, y_b], out_dtypes=(BF16, BF16, BF16),
                                 out_widths=(2 * D, D, D))
    G["w_out"] = _matmul("proj_out_dw", mixed, dh1, "tn", out_dtype=GRAD_PAYLOAD)

    d_s = _matmul("proj_a_dx", d_ya, W["w_proj_a"], "nt")
    G["w_proj_a"] = _matmul("proj_a_dw", s, d_ya, "tn", out_dtype=GRAD_PAYLOAD)

    def sgu_bwd(p, ds, *c):
        _, vjp = jax.vjp(_sgu_fn, p, *c)
        return vjp(ds)
    d_p_sgu, G["sgu_ln_w"], G["sgu_ln_b"], G["sgu_w"], G["sgu_bt"] = _rows_call(
        "sgu_bwd", sgu_bwd, [Rows(p_sgu), Rows(d_s)], sgu_consts, [("rows", 2 * D, BF16)],
        accs=[((1, D), F32), ((1, D), F32), ((SGU_G, SGU_C, SGU_C), F32), ((SGU_C, SGU_G), F32)], tm=SGU_C)

    d_zb = _matmul("proj_b_dx", d_yb, W["w_proj_b"], "nt")
    G["w_proj_b"] = _matmul("proj_b_dw", z_b, d_yb, "tn", out_dtype=GRAD_PAYLOAD)

    def post_bwd(o, r, k2, v, g, dz, *c):
        _, vjp = jax.vjp(_post_fn, o, r, k2, v, g, *c)
        return vjp(dz)
    do_h, dr1, dk1, dv1, d_g, g_lnw, g_lnb, g_rk = _rows_call(
        "rwkv_post_bwd", post_bwd, post_ins + [Rows(d_zb)], post_consts, [("heads", F32)] * 4 + [("rows", D, F32)],
        accs=[((NP, 1, PW), F32)] * 3, tm=128)
    G["ln_x_w"], G["ln_x_b"], G["r_k"] = (t.reshape(1, D) for t in (g_lnw, g_lnb, g_rk))
    (dr2, dlw, dk2, dv2, dkk, daa), early = _scan_bwd(r_h, lw_h, k_h, v_h, kk_h, a_h, s0s, do_h,
                                                      ex=early_grads(G) if early_grads else None)

    def pre_bwd(qr, qk, qv, qxw, qxa, qxg, dr1, dr2, dlw, dk1, dk2, dv1, dv2, dkk, daa, dg, *c):
        _, vjp = jax.vjp(_pre_fn, qr, qk, qv, qxw, qxa, qxg, *c)
        g = vjp((dr1 + dr2, dlw, dk1 + dk2, dv1 + dv2, dkk, daa, dg))
        dq = jnp.concatenate(g[:6], axis=-1)
        return (dq,) + tuple(g[6:])
    pre_b_ins = q_ins + [Heads(dr1), Heads(dr2), Heads(dlw), Heads(dk1), Heads(dk2), Heads(dv1), Heads(dv2),
                         Heads(dkk), Heads(daa), Rows(d_g)]
    d_q, G["w_lora"], G["w0"], G["a_lora"], G["a0"], G["g_lora"], G["k_k"], G["k_a"] = _rows_call(
        "rwkv_pre_bwd", pre_bwd, pre_b_ins, pre_consts, [("rows", RW_INT, F32)],
        accs=[((128, D), F32), ((1, D), F32), ((128, D), F32), ((1, D), F32), ((256, D), F32), ((1, D), F32),
              ((1, D), F32)], tm=128)
    d_p_rw, dsb0, dsb1 = _mix_bwd(d_q, p_rw, W["sb"])
    G["sb"] = jnp.concatenate([dsb0, dsb1], axis=0)

    G["w_sgu_t"] = _matmul("proj_sgu_dw", d_p_sgu, a, "tn", out_dtype=GRAD_PAYLOAD)
    G["w_rw_t"] = _matmul("proj_rwkv_dw", d_p_rw, a, "tn", out_dtype=GRAD_PAYLOAD)
    G["w_gate_t"] = _matmul("proj_gate_dw", d_gate, a, "tn", out_dtype=GRAD_PAYLOAD)
    token = w_in_grads_ready(G) if w_in_grads_ready else None
    da1 = _matmul("proj_sgu_dx", d_p_sgu, W["w_sgu_t"], "nn", after=token)
    da2 = _matmul("proj_rwkv_dx", d_p_rw, W["w_rw_t"], "nn", after=token)
    da3 = _matmul("proj_gate_dx", d_gate, W["w_gate_t"], "nn", after=token)

    def norm1_bwd(x, da1, da2, da3, dh1, g):
        _, vjp = jax.vjp(_rms, x, g)
        dx, dg = vjp(da1 + da2 + da3)
        return dh1 + dx, dg
    dx, G["g_mix"] = _rows_call("norm_mix_bwd", norm1_bwd, [Rows(x), Rows(da1), Rows(da2), Rows(da3), Rows(dh1)],
                                [W["g_mix"]], [("rows", D, F32)], accs=[((1, D), F32)])
    return loss_acc[0, 0], dx, G, early


class Exchange:
    def __init__(self, bufs, gathers):
        self.bufs, self.gathers, self.nb = list(bufs), list(gathers), len(bufs)
        self.any_specs = [pl.BlockSpec(memory_space=pl.ANY)] * self.nb
        self.out_shape = [jax.ShapeDtypeStruct((N_DEV,) + (b.shape if g else b.shape[1:]), b.dtype)
                          for b, g in zip(self.bufs, self.gathers)]
        n = (N_DEV - 1) * self.nb
        self.sem_shapes = [pltpu.SemaphoreType.DMA((n,)), pltpu.SemaphoreType.DMA((n,)),
                           pltpu.SemaphoreType.DMA((self.nb,))]

    def _copies(self, in_refs, out_refs, sems):
        send_sems, recv_sems, local_sems = sems
        x, y, c = lax.axis_index("x"), lax.axis_index("y"), lax.axis_index("c")
        me = 4 * x + 2 * y + c

        def src(b, dest):
            return in_refs[b] if self.gathers[b] else in_refs[b].at[dest]

        local = [pltpu.make_async_copy(src(b, me), out_refs[b].at[me], local_sems.at[b]) for b in range(self.nb)]
        sends, recvs = [], []
        for kbits in range(1, N_DEV):
            px = 1 - x if kbits & 4 else x
            py = 1 - y if kbits & 2 else y
            pc = 1 - c if kbits & 1 else c
            peer = 4 * px + 2 * py + pc
            for b in range(self.nb):
                s = (kbits - 1) * self.nb + b
                sends.append(pltpu.make_async_remote_copy(
                    src_ref=src(b, peer), dst_ref=out_refs[b].at[me], send_sem=send_sems.at[s],
                    recv_sem=recv_sems.at[s], device_id=(px, py, pc), device_id_type=pl.DeviceIdType.MESH))
                recvs.append(pltpu.make_async_remote_copy(
                    src_ref=src(b, peer), dst_ref=out_refs[b].at[peer], send_sem=send_sems.at[s],
                    recv_sem=recv_sems.at[s], device_id=(px, py, pc), device_id_type=pl.DeviceIdType.MESH))
        return local, sends, recvs

    def start(self, in_refs, out_refs, sems):
        local, sends, _ = self._copies(in_refs, out_refs, sems)
        for cp in sends + local:
            cp.start()

    def wait(self, in_refs, out_refs, sems):
        local, sends, recvs = self._copies(in_refs, out_refs, sems)
        for cp in recvs:
            cp.wait_recv()
        for cp in sends:
            cp.wait_send()
        for cp in local:
            cp.wait()

    def schedule(self, n_steps):
        return [(0, self.start), (n_steps - 1, self.wait)]


def _exchange(name, bufs, gather):
    ex = Exchange(bufs, gather if isinstance(gather, (list, tuple)) else [gather] * len(bufs))

    def body(*refs):
        in_refs, out_refs, sems = refs[:ex.nb], refs[ex.nb:2 * ex.nb], refs[2 * ex.nb:]
        ex.start(in_refs, out_refs, sems)
        ex.wait(in_refs, out_refs, sems)

    return pl.pallas_call(body, name=name, in_specs=ex.any_specs, out_specs=ex.any_specs, out_shape=ex.out_shape,
                          scratch_shapes=ex.sem_shapes)(*ex.bufs)


N_CHIP = 4


def _pair_exchange(name, blocks):
    def body(b_ref, got_ref, send_sems, recv_sems):
        x, y, c = lax.axis_index("x"), lax.axis_index("y"), lax.axis_index("c")
        copies = [pltpu.make_async_remote_copy(
            src_ref=b_ref.at[2 * q + 1 - c], dst_ref=got_ref.at[q], send_sem=send_sems.at[q],
            recv_sem=recv_sems.at[q], device_id=(x, y, 1 - c), device_id_type=pl.DeviceIdType.MESH)
            for q in range(N_CHIP)]
        for cp in copies:
            cp.start()
        for cp in copies:
            cp.wait_recv()
        for cp in copies:
            cp.wait_send()

    any_spec = pl.BlockSpec(memory_space=pl.ANY)
    return pl.pallas_call(
        body, name=name, in_specs=[any_spec], out_specs=any_spec,
        out_shape=jax.ShapeDtypeStruct((N_CHIP,) + blocks.shape[1:], blocks.dtype),
        scratch_shapes=[pltpu.SemaphoreType.DMA((N_CHIP,))] * 2,
    )(blocks)


def _pair_sum(name, blocks, got, core):
    _, R, Wd = blocks.shape

    def body(c_ref, a_ref, b_ref, o_ref):
        o_ref[...] = (a_ref[...].astype(F32) + b_ref[...].astype(F32)).astype(o_ref.dtype)

    return pl.pallas_call(
        body, name=name,
        grid_spec=pltpu.PrefetchScalarGridSpec(
            num_scalar_prefetch=1, grid=(N_CHIP,),
            in_specs=[pl.BlockSpec((None, R, Wd), lambda q, c_ref: (2 * q + c_ref[0], 0, 0)),
                      pl.BlockSpec((None, R, Wd), lambda q, c_ref: (q, 0, 0))],
            out_specs=pl.BlockSpec((None, R, Wd), lambda q, c_ref: (q, 0, 0))),
        out_shape=jax.ShapeDtypeStruct(got.shape, blocks.dtype),
        compiler_params=pltpu.CompilerParams(dimension_semantics=("parallel",), vmem_limit_bytes=VMEM_LIMIT),
    )(core, blocks, got)


def _chip_copies(s_ref, land_ref, send_sems, recv_sems):
    x, y, c = lax.axis_index("x"), lax.axis_index("y"), lax.axis_index("c")
    my_q = 2 * x + y
    sends, recvs = [], []
    for kbits in range(1, N_CHIP):
        px = 1 - x if kbits & 2 else x
        py = 1 - y if kbits & 1 else y
        peer_q = 2 * px + py
        sends.append(pltpu.make_async_remote_copy(
            src_ref=s_ref.at[peer_q], dst_ref=land_ref.at[my_q], send_sem=send_sems[kbits - 1],
            recv_sem=recv_sems[kbits - 1], device_id=(px, py, c), device_id_type=pl.DeviceIdType.MESH))
        recvs.append(pltpu.make_async_remote_copy(
            src_ref=s_ref.at[peer_q], dst_ref=land_ref.at[peer_q], send_sem=send_sems[kbits - 1],
            recv_sem=recv_sems[kbits - 1], device_id=(px, py, c), device_id_type=pl.DeviceIdType.MESH))
    return sends, recvs


_HBM = pl.BlockSpec(memory_space=pltpu.HBM)
_SEM = pl.BlockSpec(memory_space=pltpu.SEMAPHORE)
N_CHIP_SEMS = 2 * (N_CHIP - 1)


def _scatter_copies(b_refs, land_refs, send_sems, recv_sems):
    x, y, c = lax.axis_index("x"), lax.axis_index("y"), lax.axis_index("c")
    me = 4 * x + 2 * y + c
    sends, recvs = [], []
    for kbits in range(1, N_DEV):
        px = 1 - x if kbits & 4 else x
        py = 1 - y if kbits & 2 else y
        pc = 1 - c if kbits & 1 else c
        peer = 4 * px + 2 * py + pc
        for b in range(len(b_refs)):
            s = (kbits - 1) * len(b_refs) + b
            sends.append(pltpu.make_async_remote_copy(
                src_ref=b_refs[b].at[peer], dst_ref=land_refs[b].at[me], send_sem=send_sems[s],
                recv_sem=recv_sems[s], device_id=(px, py, pc), device_id_type=pl.DeviceIdType.MESH))
            recvs.append(pltpu.make_async_remote_copy(
                src_ref=b_refs[b].at[peer], dst_ref=land_refs[b].at[peer], send_sem=send_sems[s],
                recv_sem=recv_sems[s], device_id=(px, py, pc), device_id_type=pl.DeviceIdType.MESH))
    return sends, recvs


def _scatter_start(name, bufs):
    nb = len(bufs)
    n = (N_DEV - 1) * nb

    def body(*refs):
        b_refs, land_refs, outs = refs[:nb], refs[nb:2 * nb], refs[2 * nb:]
        sends, _ = _scatter_copies(b_refs, land_refs, outs[:n], outs[n:2 * n])
        for cp in sends:
            cp.start()
        token = outs[2 * n + 2 * nb]
        token[...] = jnp.zeros_like(token)

    thru = tuple(pltpu.HBM(b.shape, b.dtype) for b in bufs)
    res = pl.pallas_call(
        body, name=name, in_specs=(_HBM,) * (2 * nb),
        out_specs=(_SEM,) * (2 * n) + (_HBM,) * (2 * nb) + (pl.BlockSpec(memory_space=pltpu.VMEM),),
        out_shape=(pltpu.SemaphoreType.DMA(()),) * (2 * n) + thru + thru + (jax.ShapeDtypeStruct((8, LANES), F32),),
        input_output_aliases={i: 2 * n + i for i in range(2 * nb)},
        compiler_params=pltpu.CompilerParams(has_side_effects=pltpu.SideEffectType.DATAFLOW_SIDE_EFFECTING),
    )(*[pltpu.with_memory_space_constraint(b, pltpu.HBM) for b in bufs],
      *[pltpu.with_memory_space_constraint(lax.empty(b.shape, b.dtype), pltpu.HBM) for b in bufs])
    return res[:2 * n], list(res[2 * n:2 * n + nb]), list(res[2 * n + nb:2 * n + 2 * nb]), res[2 * n + 2 * nb]


def _copy_through_vmem(pairs, stage, sems):
    for into_vmem in (True, False):
        copies = [pltpu.make_async_copy(src if into_vmem else stage[i], stage[i] if into_vmem else dst, sems.at[i])
                  for i, (src, dst) in enumerate(pairs)]
        for cp in copies:
            cp.start()
        for cp in copies:
            cp.wait()


def _scatter_wait(name, sems, bufs_thru, lands_thru, after):
    nb = len(bufs_thru)
    n = (N_DEV - 1) * nb

    def body(*refs):
        b_refs, land_refs, sem_refs = refs[:nb], refs[nb:2 * nb], refs[2 * nb:2 * nb + 2 * n]
        stage, local_sems = refs[-(nb + 1):-1], refs[-1]
        me = 4 * lax.axis_index("x") + 2 * lax.axis_index("y") + lax.axis_index("c")
        _copy_through_vmem([(b_refs[b].at[me], land_refs[b].at[me]) for b in range(nb)], stage, local_sems)
        sends, recvs = _scatter_copies(b_refs, land_refs, sem_refs[:n], sem_refs[n:])
        for cp in sends:
            cp.wait_send()
        for cp in recvs:
            cp.wait_recv()

    thru = tuple(pltpu.HBM(b.shape, b.dtype) for b in bufs_thru)
    res = pl.pallas_call(
        body, name=name, in_specs=(_HBM,) * (2 * nb) + (_SEM,) * (2 * n) + (pl.BlockSpec(memory_space=pl.ANY),),
        out_specs=(_HBM,) * (2 * nb), out_shape=thru + thru,
        input_output_aliases={i: i for i in range(2 * nb)},
        scratch_shapes=[pltpu.VMEM(b.shape[1:], b.dtype) for b in bufs_thru] + [pltpu.SemaphoreType.DMA((nb,))],
        compiler_params=pltpu.CompilerParams(has_side_effects=pltpu.SideEffectType.DATAFLOW_SIDE_EFFECTING,
                                             vmem_limit_bytes=VMEM_LIMIT),
    )(*bufs_thru, *lands_thru, *sems, after)
    return list(res[:nb]), list(res[nb:])


def _chip_exchange_start(name, sums):
    def body(s_ref, land_ref, *outs):
        sems, token = outs[:N_CHIP_SEMS], outs[N_CHIP_SEMS + 2]
        sends, _ = _chip_copies(s_ref, land_ref, sems[:N_CHIP - 1], sems[N_CHIP - 1:])
        for cp in sends:
            cp.start()
        token[...] = jnp.zeros_like(token)

    res = pl.pallas_call(
        body, name=name, in_specs=(_HBM, _HBM),
        out_specs=(_SEM,) * N_CHIP_SEMS + (_HBM, _HBM, pl.BlockSpec(memory_space=pltpu.VMEM)),
        out_shape=(pltpu.SemaphoreType.DMA(()),) * N_CHIP_SEMS
        + (pltpu.HBM(sums.shape, sums.dtype), pltpu.HBM(sums.shape, sums.dtype), jax.ShapeDtypeStruct((8, LANES), F32)),
        input_output_aliases={0: N_CHIP_SEMS, 1: N_CHIP_SEMS + 1},
        compiler_params=pltpu.CompilerParams(has_side_effects=pltpu.SideEffectType.DATAFLOW_SIDE_EFFECTING),
    )(pltpu.with_memory_space_constraint(sums, pltpu.HBM),
      pltpu.with_memory_space_constraint(lax.empty(sums.shape, sums.dtype), pltpu.HBM))
    return res[:N_CHIP_SEMS], res[N_CHIP_SEMS], res[N_CHIP_SEMS + 1], res[N_CHIP_SEMS + 2]


def _chip_exchange_wait(name, sems, sums_thru, land_thru, after):
    def body(s_ref, land_ref, *rest):
        sems, stage, local_sems = rest[:N_CHIP_SEMS], rest[-2], rest[-1]
        my_chip = 2 * lax.axis_index("x") + lax.axis_index("y")
        _copy_through_vmem([(s_ref.at[my_chip], land_ref.at[my_chip])], [stage], local_sems)
        sends, recvs = _chip_copies(s_ref, land_ref, sems[:N_CHIP - 1], sems[N_CHIP - 1:])
        for cp in sends:
            cp.wait_send()
        for cp in recvs:
            cp.wait_recv()

    return pl.pallas_call(
        body, name=name, in_specs=(_HBM, _HBM) + (_SEM,) * N_CHIP_SEMS + (pl.BlockSpec(memory_space=pl.ANY),),
        out_specs=(_HBM, _HBM),
        out_shape=(pltpu.HBM(sums_thru.shape, sums_thru.dtype), pltpu.HBM(sums_thru.shape, sums_thru.dtype)),
        input_output_aliases={0: 0, 1: 1},
        scratch_shapes=[pltpu.VMEM(sums_thru.shape[1:], sums_thru.dtype), pltpu.SemaphoreType.DMA((1,))],
        compiler_params=pltpu.CompilerParams(has_side_effects=pltpu.SideEffectType.DATAFLOW_SIDE_EFFECTING,
                                             vmem_limit_bytes=VMEM_LIMIT),
    )(sums_thru, land_thru, *sems, after)


def _adam_update(g, w, m, v):
    m_new = ADAM_B1 * m + (1.0 - ADAM_B1) * g
    v_new = ADAM_B2 * v + (1.0 - ADAM_B2) * jnp.square(g)
    m_hat = m_new / (1.0 - ADAM_B1 ** ADAM_STEP)
    v_hat = v_new / (1.0 - ADAM_B2 ** ADAM_STEP)
    return -ADAM_LR * (m_hat / (jnp.sqrt(v_hat) + ADAM_EPS) + ADAM_WD * w), m_new, v_new


def _adamw_rows(name, slots, parts):
    n_parts, rows = len(parts), [p[0].shape[0] for p in parts]
    rest = slots.shape[1] - sum(rows)

    def total(s_ref, at, r):
        g = s_ref[0, pl.ds(at, r), :]
        for j in range(1, slots.shape[0]):
            g = g + s_ref[j, pl.ds(at, r), :]
        return g

    def body(s_ref, *refs):
        ins, outs = refs[:3 * n_parts], refs[3 * n_parts:]
        at = 0
        for i, r in enumerate(rows):
            g = total(s_ref, at, r)
            w_ref, m_ref, v_ref = ins[3 * i:3 * i + 3]
            g_out, d_out, m_out, v_out = outs[4 * i:4 * i + 4]
            g_out[...] = g
            d_out[...], m_out[...], v_out[...] = _adam_update(g, w_ref[...], m_ref[...], v_ref[...])
            at += r
        outs[4 * n_parts][...] = total(s_ref, at, rest)

    shapes = [jax.ShapeDtypeStruct(p[0].shape, F32) for p in parts for _ in range(4)]
    res = pl.pallas_call(
        body, name=name, out_shape=shapes + [jax.ShapeDtypeStruct((rest, slots.shape[2]), F32)],
        compiler_params=pltpu.CompilerParams(vmem_limit_bytes=VMEM_LIMIT),
    )(slots, *[t for p in parts for t in p])
    return [res[4 * i:4 * i + 4] for i in range(n_parts)], res[4 * n_parts]


def _adamw_whole(name, items):
    n_items = len(items)

    def body(*refs):
        ins, outs = refs[:4 * n_items], refs[4 * n_items:]
        for i in range(n_items):
            s_ref, w_ref, m_ref, v_ref = ins[4 * i:4 * i + 4]
            g_out, d_out, m_out, v_out = outs[4 * i:4 * i + 4]
            g = s_ref[0].astype(F32)
            for j in range(1, s_ref.shape[0]):
                g = g + s_ref[j].astype(F32)
            g_out[0] = g
            d_out[0], m_out[0], v_out[0] = _adam_update(g, w_ref[0], m_ref[0], v_ref[0])

    res = pl.pallas_call(
        body, name=name, out_shape=[jax.ShapeDtypeStruct(item[1].shape, F32) for item in items for _ in range(4)],
        compiler_params=pltpu.CompilerParams(vmem_limit_bytes=VMEM_LIMIT),
    )(*[t for item in items for t in item])
    return [res[4 * i:4 * i + 4] for i in range(n_items)]


def _adamw(name, slots, w, m, v, tr=256):
    unit_mid = w.ndim == 3 and w.shape[1] == 1 and w.shape[0] > 1
    R, Wd = (w.shape[0], w.shape[2]) if unit_mid else w.shape[-2:]
    depth_axis = w.ndim == 3 and not unit_mid
    if unit_mid:
        tr, tc = 128, Wd
    elif R % tr == 0:
        tc = Wd
    else:
        tr, tc = R, (256 if (Wd % 256 == 0 and R > 256) else Wd)
    at = (slice(None), 0, slice(None)) if unit_mid else Ellipsis

    def body(s_ref, w_ref, m_ref, v_ref, g_out, d_out, m_out, v_out):
        g = s_ref[0].astype(F32)
        for j in range(1, slots.shape[0]):
            g = g + s_ref[j].astype(F32)
        g_out[at] = g
        d_out[at], m_out[at], v_out[at] = _adam_update(g, w_ref[at], m_ref[at], v_ref[at])

    if unit_mid:
        row = pl.BlockSpec((tr, 1, tc), lambda i, j: (i, 0, j))
    elif depth_axis:
        row = pl.BlockSpec((None, tr, tc), lambda i, j: (0, i, j))
    else:
        row = pl.BlockSpec((tr, tc), lambda i, j: (i, j))
    return pl.pallas_call(
        body, name=name, grid=(pl.cdiv(R, tr), Wd // tc),
        in_specs=[pl.BlockSpec((slots.shape[0], tr, tc), lambda i, j: (0, i, j)), row, row, row],
        out_specs=[row] * 4, out_shape=[jax.ShapeDtypeStruct(w.shape, F32)] * 4,
        compiler_params=pltpu.CompilerParams(dimension_semantics=("parallel", "parallel"),
                                             vmem_limit_bytes=VMEM_LIMIT),
    )(slots, w, m, v)


PACK_W = 1024
PACKED = [(n, s) for n, s in REPLICATED if n != "sgu_w"]
_SMALL_SIZES = [int(np.prod(s)) for _, s in PACKED]
_SMALL_ROWS = _round_up(_round_up(sum(_SMALL_SIZES) + PACK_W, PACK_W) // PACK_W, 8)
assert all(size % PACK_W == 0 for size in _SMALL_SIZES)
W_IN_SHARD = P_TOTAL // N_DEV


def _pack_rows(name, parts, rows, after=()):
    parts = [p.reshape(-1, PACK_W) for p in parts]
    assert all(p.dtype == F32 for p in parts)

    def body(*refs):
        at = 0
        for ref in refs[:len(parts)]:
            refs[-1][pl.ds(at, ref.shape[0]), :] = ref[...]
            at += ref.shape[0]
        refs[-1][pl.ds(at, rows - at), :] = jnp.zeros((rows - at, PACK_W), F32)

    return pl.pallas_call(
        body, name=name, out_shape=jax.ShapeDtypeStruct((rows, PACK_W), F32),
        in_specs=[pl.BlockSpec(memory_space=pltpu.VMEM)] * len(parts) + [pl.BlockSpec(memory_space=pl.ANY)] * len(after),
        out_specs=pl.BlockSpec(memory_space=pltpu.VMEM))(*parts, *after)


_W_IN_SEGMENTS = [(2 * D, 0, 0), (3 * D, 1, 0), (L_W, 1, 3 * D), (L_A, 1, 3 * D + 128), (L_G, 1, 3 * D + 256),
                  (2 * D, 2, 0)]
_W_IN_PADS = [(3 * D + L_W, 128 - L_W), (3 * D + 128 + L_A, 128 - L_A), (3 * D + 256 + L_G, 256 - L_G)]
_W_IN_GROUP_ROWS = [2 * D, 3 * D + 128 + 128 + 256, 2 * D]
assert sum(rows for rows, _, _ in _W_IN_SEGMENTS) == P_TOTAL


def _w_in_pieces(j):
    pieces, first = [], 0
    for rows, group, to in _W_IN_SEGMENTS:
        lo, hi = max(first, W_IN_SHARD * j), min(first + rows, W_IN_SHARD * (j + 1))
        if lo < hi:
            pieces.append((lo - W_IN_SHARD * j, hi - lo, group, to + lo - first))
        first += rows
    return pieces


def _w_in_groups_t(blocks):
    def body(x_ref, *o_refs):
        @pl.when(pl.program_id(0) == 0)
        def _():
            for at, rows in _W_IN_PADS:
                o_refs[1][pl.ds(at, rows), :] = jnp.zeros((rows, D), blocks.dtype)
        for j in range(N_DEV):
            @pl.when(pl.program_id(0) == j)
            def _(j=j):
                for at, rows, group, to in _w_in_pieces(j):
                    o_refs[group][pl.ds(to, rows), :] = x_ref[pl.ds(at, rows), :]

    return pl.pallas_call(
        body, name="w_in_groups", grid=(N_DEV,),
        in_specs=[pl.BlockSpec((None, W_IN_SHARD, D), lambda j: (j, 0, 0))],
        out_specs=[pl.BlockSpec((r, D), lambda j: (0, 0)) for r in _W_IN_GROUP_ROWS],
        out_shape=[jax.ShapeDtypeStruct((r, D), blocks.dtype) for r in _W_IN_GROUP_ROWS],
        compiler_params=pltpu.CompilerParams(dimension_semantics=("arbitrary",), vmem_limit_bytes=VMEM_LIMIT),
    )(blocks)


def _w_in_grad_blocks(g_sgu_t, g_rw_t, g_gate_t):
    def body(*refs):
        g_refs, o_ref = refs[:3], refs[3]
        for j in range(N_DEV):
            @pl.when(pl.program_id(0) == j)
            def _(j=j):
                for at, rows, group, to in _w_in_pieces(j):
                    o_ref[pl.ds(at, rows), :] = g_refs[group][pl.ds(to, rows), :]

    return pl.pallas_call(
        body, name="w_in_grad_blocks", grid=(N_DEV,),
        in_specs=[pl.BlockSpec((r, D), lambda j: (0, 0)) for r in _W_IN_GROUP_ROWS],
        out_specs=pl.BlockSpec((None, W_IN_SHARD, D), lambda j: (j, 0, 0)),
        out_shape=jax.ShapeDtypeStruct((N_DEV, W_IN_SHARD, D), g_sgu_t.dtype),
        compiler_params=pltpu.CompilerParams(dimension_semantics=("arbitrary",), vmem_limit_bytes=VMEM_LIMIT),
    )(g_sgu_t, g_rw_t, g_gate_t)


class TwoLevelGather:
    N_COPIES = 8
    PART_ALIGN = 16

    def __init__(self, bufs, stage_own=()):
        self.bufs, self.nb, self.stage_own = list(bufs), len(bufs), tuple(stage_own)
        self.any_specs = [pl.BlockSpec(memory_space=pl.ANY)] * self.nb
        self.out_shape = [jax.ShapeDtypeStruct((N_DEV,) + b.shape, b.dtype) for b in self.bufs]
        n = self.N_COPIES * self.nb
        self.sem_shapes = [pltpu.SemaphoreType.DMA((n,)), pltpu.SemaphoreType.DMA((n,)),
                           pltpu.SemaphoreType.DMA((self.nb,))]
        if self.stage_own:
            self.sem_shapes += [pltpu.SemaphoreType.DMA((len(self.stage_own),))]
            self.sem_shapes += [pltpu.VMEM(self.bufs[b].shape, self.bufs[b].dtype) for b in self.stage_own]

    def _parts(self, b):
        shape = self.bufs[b].shape
        first = shape[0] // 2 // self.PART_ALIGN * self.PART_ALIGN if len(shape) == 2 else 0
        return [(0, first), (first, shape[0] - first)] if first else [None]

    def _copies(self, in_refs, out_refs, sems):
        send_sems, recv_sems, local_sems = sems[:3]
        nb = self.nb
        x, y, c = lax.axis_index("x"), lax.axis_index("y"), lax.axis_index("c")
        me, sibling = (x, y, c), (x, y, 1 - c)
        near, far = [(1 - x, y), (x, 1 - y)], (1 - x, 1 - y)

        def slot(b, dev):
            return out_refs[b].at[4 * dev[0] + 2 * dev[1] + dev[2]]

        def copy(b, k, block, to, own=False, rows=None):
            src, dst = in_refs[b] if own else slot(b, block), slot(b, block)
            if rows is not None:
                src, dst = src.at[pl.ds(*rows)], dst.at[pl.ds(*rows)]
            return pltpu.make_async_remote_copy(
                src_ref=src, dst_ref=dst, send_sem=send_sems.at[self.N_COPIES * b + k],
                recv_sem=recv_sems.at[self.N_COPIES * b + k], device_id=to, device_id_type=pl.DeviceIdType.MESH)

        ways = [(j, b) for j in range(2) for b in range(nb) if j < len(self._parts(b))]
        cp = {}
        cp["local"] = [pltpu.make_async_copy(in_refs[b], slot(b, me), local_sems.at[b]) for b in range(nb)
                       if b not in self.stage_own]
        cp["first"] = [copy(b, 0, me, sibling, own=True) for b in range(nb)]
        cp["first"] += [copy(b, 1 + j, me, (*near[j], c), own=True) for j in range(2) for b in range(nb)]
        cp["from_near"] = [copy(b, 1 + j, (*near[j], c), me) for j in range(2) for b in range(nb)]
        cp["near_on"] = [[copy(b, 5 + j, (*near[j], c), sibling)]
                         + ([copy(b, 3 + j, (*near[j], c), (*near[1 - j], c), rows=self._parts(b)[j])]
                            if (j, b) in ways else []) for j in range(2) for b in range(nb)]
        cp["from_far"] = [copy(b, 3 + j, (*far, c), me, rows=self._parts(b)[j]) for j, b in ways]
        cp["far_on"] = [copy(b, 7, (*far, c), sibling) for b in range(nb)]
        cp["from_sibling"] = [copy(b, 0, sibling, me) for b in range(nb)]
        cp["from_sibling"] += [copy(b, 5 + j, (*near[j], 1 - c), me) for j in range(2) for b in range(nb)]
        cp["from_sibling"] += [copy(b, 7, (*far, 1 - c), me) for b in range(nb)]
        return cp

    def start(self, in_refs, out_refs, sems):
        cp = self._copies(in_refs, out_refs, sems)
        for c in cp["first"] + cp["local"]:
            c.start()
        if self.stage_own:
            me = 4 * lax.axis_index("x") + 2 * lax.axis_index("y") + lax.axis_index("c")
            _copy_through_vmem([(in_refs[b], out_refs[b].at[me]) for b in self.stage_own], sems[4:], sems[3])

    def pass_near(self, in_refs, out_refs, sems):
        cp = self._copies(in_refs, out_refs, sems)
        for arrived, onward in zip(cp["from_near"], cp["near_on"]):
            arrived.wait_recv()
            for c in onward:
                c.start()

    def pass_far(self, in_refs, out_refs, sems):
        cp = self._copies(in_refs, out_refs, sems)
        for c in cp["from_far"]:
            c.wait_recv()
        for c in cp["far_on"]:
            c.start()

    def finish(self, in_refs, out_refs, sems):
        cp = self._copies(in_refs, out_refs, sems)
        for c in cp["from_sibling"]:
            c.wait_recv()
        for c in cp["first"] + sum(cp["near_on"], []) + cp["far_on"]:
            c.wait_send()
        for c in cp["local"]:
            c.wait()

    def schedule(self, n_steps):
        return [(0, self.start), (max(n_steps // 2 - 1, 0), self.pass_near), (max(n_steps - 3, 0), self.pass_far),
                (n_steps - 1, self.finish)]


def _all_gather_two_level(name, bufs, stage_own=()):
    ex = TwoLevelGather(bufs, stage_own)

    def body(*refs):
        args = refs[:ex.nb], refs[ex.nb:2 * ex.nb], refs[2 * ex.nb:]
        for _, action in ex.schedule(1):
            action(*args)

    return pl.pallas_call(body, name=name, in_specs=ex.any_specs, out_specs=ex.any_specs, out_shape=ex.out_shape,
                          scratch_shapes=ex.sem_shapes)(*ex.bufs)


def _cols_from_blocks(blk):
    return jnp.transpose(blk, (1, 0, 2)).reshape(blk.shape[1], -1)


def _cols_to_blocks(g):
    r, c = g.shape
    return jnp.transpose(g.reshape(r, N_DEV, c // N_DEV), (1, 0, 2))


FIRST_WEIGHTS = ["w_in", "shift_b", "w_lora_w", "a_lora_w", "g_lora_w"]
LATE_WEIGHTS = ["w_proj_a", "w_proj_b", "w_out", "w_ffn1", "w_ffn2"]
SCAN_CARRIED = ["w_proj_a", "w_proj_b", "w_out"]
FFN_WEIGHTS = ["w_ffn1", "w_ffn2"]


def _late_weights(shards):
    ex = TwoLevelGather([shards[n][0].astype(BF16) for n in LATE_WEIGHTS])

    def finish(results):
        got = dict(zip(LATE_WEIGHTS, results))
        W = {n: got[n].reshape(-1, D) for n in ("w_proj_a", "w_proj_b", "w_out", "w_ffn2")}
        W["w_ffn1"] = got["w_ffn1"].reshape(N_DEV, D, -1)
        return W
    return ex, finish


def _gather_weights(shards):
    def payload(n):
        if n == "w_in":
            return jnp.transpose(shards[n][0]).astype(BF16)
        return shards[n] if n == "shift_b" else shards[n].astype(BF16)
    payloads = [payload(n) for n in FIRST_WEIGHTS]
    got = dict(zip(FIRST_WEIGHTS, _all_gather_two_level("weight_all_gather", payloads, stage_own=(0,))))
    W = {}
    W["w_sgu_t"], W["w_rw_t"], W["w_gate_t"] = _w_in_groups_t(got["w_in"])
    z = lambda r, c, dt: jnp.zeros((r, c), dt)
    W["w_lora"] = jnp.concatenate([_cols_from_blocks(got["w_lora_w"][:, 0]).astype(F32), z(128 - L_W, D, F32)], axis=0)
    W["a_lora"] = jnp.concatenate([_cols_from_blocks(got["a_lora_w"][:, 0]).astype(F32), z(128 - L_A, D, F32)], axis=0)
    W["g_lora"] = jnp.concatenate([_cols_from_blocks(got["g_lora_w"][:, 0]).astype(F32), z(256 - L_G, D, F32)], axis=0)
    sb = _cols_from_blocks(got["shift_b"][:, 0])
    W["sb"] = jnp.concatenate([sb[:, :3 * D], sb[:, 3 * D:3 * D + L_W], z(2, 128 - L_W, F32),
                               sb[:, 3 * D + L_W:3 * D + L_W + L_A], z(2, 128 - L_A, F32),
                               sb[:, 3 * D + L_W + L_A:], z(2, 256 - L_G, F32)], axis=1)
    return W


def _replicated_weights(rep):
    W = {n: rep[n] for n in ("g_mix", "sgu_ln_w", "sgu_ln_b", "w0", "a0", "k_k", "k_a", "r_k", "ln_x_w", "ln_x_b",
                             "g_ffn")}
    W["g_final"] = rep["g_final"].reshape(1, D)
    W["sgu_w"] = rep["sgu_w"][0]
    W["sgu_bt"] = jnp.transpose(rep["sgu_b"][0])
    return W


def _late_grad_blocks(G):
    return Exchange([G[n].reshape(N_DEV, -1, D) for n in SCAN_CARRIED]
                    + [G["sgu_w"].reshape(SGU_G * SGU_C, SGU_C).astype(GRAD_PAYLOAD)],
                    [False] * len(SCAN_CARRIED) + [True])


def _first_grad_blocks(G):
    sbg = G["sb"]
    c = 3 * D
    sb = jnp.concatenate([sbg[:, :c], sbg[:, c:c + L_W], sbg[:, c + 128:c + 128 + L_A],
                          sbg[:, c + 256:c + 256 + L_G]], axis=1)
    return {
        "shift_b": _cols_to_blocks(sb),
        "w_lora_w": _cols_to_blocks(G["w_lora"][:L_W]), "a_lora_w": _cols_to_blocks(G["a_lora"][:L_A]),
        "g_lora_w": _cols_to_blocks(G["g_lora"][:L_G]),
    }


def _replicated_grads(G):
    small = {n: G[n] for n in ("g_mix", "sgu_ln_w", "sgu_ln_b", "w0", "a0", "k_k", "k_a", "r_k", "ln_x_w", "ln_x_b",
                               "g_ffn", "g_final")}
    small["sgu_w"] = G["sgu_w"]
    small["sgu_b"] = jnp.transpose(G["sgu_bt"])
    return small


def kernel(x, g_mix, w_in, sgu_ln_w, sgu_ln_b, sgu_w, sgu_b, w_proj_a, shift_b, w_lora_w, w0, a_lora_w, a0, g_lora_w, k_k, k_a, r_k, ln_x_w, ln_x_b, w_proj_b, w_out, g_ffn, w_ffn1, w_ffn2, g_final, loss_target, m_g_mix, m_w_in, m_sgu_ln_w, m_sgu_ln_b, m_sgu_w, m_sgu_b, m_w_proj_a, m_shift_b, m_w_lora_w, m_w0, m_a_lora_w, m_a0, m_g_lora_w, m_k_k, m_k_a, m_r_k, m_ln_x_w, m_ln_x_b, m_w_proj_b, m_w_out, m_g_ffn, m_w_ffn1, m_w_ffn2, m_g_final, v_g_mix, v_w_in, v_sgu_ln_w, v_sgu_ln_b, v_sgu_w, v_sgu_b, v_w_proj_a, v_shift_b, v_w_lora_w, v_w0, v_a_lora_w, v_a0, v_g_lora_w, v_k_k, v_k_a, v_r_k, v_ln_x_w, v_ln_x_b, v_w_proj_b, v_w_out, v_g_ffn, v_w_ffn1, v_w_ffn2, v_g_final):
    env = dict(locals())
    weights = {n: env[n] for n in WEIGHT_ORDER}
    moms = {n: env["m_" + n] for n in WEIGHT_ORDER}
    vars_ = {n: env["v_" + n] for n in WEIGHT_ORDER}

    shards = {n: weights[n] for n, _, _ in SHARDED}
    W = _gather_weights(shards)
    W.update(_replicated_weights({n: weights[n] for n, _ in REPLICATED}))
    in_flight = {}

    def send_w_in_grads(G):
        blocks = _w_in_grad_blocks(G["w_sgu_t"], G["w_rw_t"], G["w_gate_t"])
        got = _pair_exchange("grad_pair_exchange", blocks)
        core = lax.axis_index("c").astype(jnp.int32).reshape(1)
        sums = _pair_sum("grad_pair_sum", blocks, got, core)
        in_flight["sems"], in_flight["sums"], in_flight["land"], token = _chip_exchange_start("grad_chip_start", sums)
        return token

    def send_ffn_grads(G):
        in_flight["ffn"] = _scatter_start("grad_ffn_start", [G["w_ffn1"], G["w_ffn2"].reshape(N_DEV, -1, D)])
        return in_flight["ffn"][3]

    loss_part, dx, G, late_slots = _local_step(x[0], loss_target[0], W, late_weights=_late_weights(shards),
                                               early_grads=_late_grad_blocks, w_in_grads_ready=send_w_in_grads,
                                               ffn_grads_ready=send_ffn_grads)

    slots = dict(zip(SCAN_CARRIED, late_slots))
    _, landed = _scatter_wait("grad_ffn_wait", *in_flight["ffn"][:3], after=dx)
    slots.update(zip(FFN_WEIGHTS, landed))
    outs = [dict(), dict(), dict(), dict()]

    def keep(n, res):
        for k in range(4):
            outs[k][n] = res[k]

    def update_group(name, group):
        group_out = _adamw_whole(name, [(slots[n], weights[n], moms[n], vars_[n]) for n in group])
        for n, res in zip(group, group_out):
            keep(n, res)

    for n in FFN_WEIGHTS:
        keep(n, _adamw("adamw_" + n, slots[n], weights[n], moms[n], vars_[n]))
    update_group("adamw_projections", SCAN_CARRIED)
    sgu_shape = (SGU_G * SGU_C, SGU_C)
    res = _adamw("adamw_sgu_w", late_slots[len(SCAN_CARRIED)], *[t.reshape(sgu_shape) for t in
                                                                  (weights["sgu_w"], moms["sgu_w"], vars_["sgu_w"])])
    keep("sgu_w", [t.reshape(weights["sgu_w"].shape) for t in res])
    updated = [outs[1][n] for n in FFN_WEIGHTS + SCAN_CARRIED + ["sgu_w"]]

    blocks = _first_grad_blocks(G)
    small = _replicated_grads(G)
    small_parts = [small[n] for n, _ in PACKED] + [jnp.full((PACK_W,), loss_part, F32)]
    rest = [n for n in FIRST_WEIGHTS if n != "w_in"]
    packed = _pack_rows("grad_pack", small_parts, _SMALL_ROWS, after=updated)
    res = _exchange("grad_exchange", [blocks[n] for n in rest] + [packed], [False] * len(rest) + [True])
    slots.update(zip(rest, res[:-1]))
    small_slots = res[-1]
    _, slots["w_in"] = _chip_exchange_wait("grad_chip_wait", in_flight["sems"], in_flight["sums"], in_flight["land"],
                                           after=small_slots)

    res = _adamw("adamw_w_in", slots["w_in"], *[jnp.transpose(t, (2, 0, 1)) for t in (weights["w_in"], moms["w_in"], vars_["w_in"])])
    keep("w_in", [jnp.transpose(t, (1, 2, 0)) for t in res])
    update_group("adamw_low_rank", rest)

    small_out, after_them = _adamw_rows(
        "adamw_replicated", small_slots,
        [[d[n].reshape(-1, PACK_W) for d in (weights, moms, vars_)] for n, _ in PACKED])
    for (n, s), res in zip(PACKED, small_out):
        for k in range(4):
            outs[k][n] = res[k].reshape(s)
    loss = after_them[0, 0]
    return (loss, dx[None], *[outs[0][n] for n in WEIGHT_ORDER], *[outs[1][n] for n in WEIGHT_ORDER],
            *[outs[2][n] for n in WEIGHT_ORDER], *[outs[3][n] for n in WEIGHT_ORDER])
```

```python
import functools
import numpy as np
import jax
import jax.numpy as jnp
from jax import lax
from jax.experimental import pallas as pl
from jax.experimental.pallas import tpu as pltpu

F32 = jnp.float32
BF16 = jnp.bfloat16

D = 1024
NH, HN = 16, 64
NP, PW = NH // 2, 2 * HN
SGU_G, SGU_C = 8, 128
L_W, L_A, L_G = 64, 64, 160
C_B = 3 * D + L_W + L_A + L_G
P_TOTAL = 2 * D + C_B + 2 * D
D_FF = 4 * D
RW_INT = 3 * D + 128 + 128 + 256
NORM_EPS, LN_EPS, GN_EPS = 1e-6, 1e-5, 64e-5
N_DEV = 8
LANES = 128
SCAN_C = 64
N_KEPT = 4
SOLVE_B = 16
SCAN_PRECISION = lax.Precision.HIGH
SCAN_OUT_PRECISION = lax.Precision.DEFAULT
GRAD_PAYLOAD = BF16
VMEM_LIMIT = 56 * 1024 * 1024
MATMUL_VMEM_BUDGET = 40 * 1024 * 1024
STEP_COST_BYTES = 512 * 1024
HBM_COST_RATIO = 3
ACC_PASS_WEIGHT = 4

ADAM_LR, ADAM_B1, ADAM_B2, ADAM_EPS, ADAM_WD, ADAM_STEP = 0.001, 0.9, 0.999, 1e-08, 0.01, 10

SHARDED = [
    ("w_in", (D, P_TOTAL), 1), ("w_proj_a", (D, D), 0), ("shift_b", (2, C_B), 1), ("w_lora_w", (L_W, D), 1),
    ("a_lora_w", (L_A, D), 1), ("g_lora_w", (L_G, D), 1), ("w_proj_b", (D, D), 0), ("w_out", (D, D), 0),
    ("w_ffn1", (D, D_FF), 1), ("w_ffn2", (D_FF, D), 0),
]
REPLICATED = [
    ("g_mix", (1, D)), ("sgu_ln_w", (1, D)), ("sgu_ln_b", (1, D)), ("sgu_w", (1, SGU_G, SGU_C, SGU_C)),
    ("sgu_b", (1, SGU_G, SGU_C)), ("w0", (1, D)), ("a0", (1, D)), ("k_k", (1, D)), ("k_a", (1, D)), ("r_k", (1, D)),
    ("ln_x_w", (1, D)), ("ln_x_b", (1, D)), ("g_ffn", (1, D)), ("g_final", (D,)),
]
WEIGHT_ORDER = ["g_mix", "w_in", "sgu_ln_w", "sgu_ln_b", "sgu_w", "sgu_b", "w_proj_a", "shift_b", "w_lora_w", "w0",
                "a_lora_w", "a0", "g_lora_w", "k_k", "k_a", "r_k", "ln_x_w", "ln_x_b", "w_proj_b", "w_out", "g_ffn",
                "w_ffn1", "w_ffn2", "g_final"]


def _round_up(n, m):
    return (n + m - 1) // m * m


def _pick(n, target):
    if n <= target:
        return n
    best = None
    for t in range(LANES, target + 1, LANES):
        if n % t == 0:
            best = t
    assert best is not None, (n, target)
    return best


def _matmul(name, a, b, mode, out_dtype=F32, tm=2048, tn=1024, tk=4096, out_blocks=None, epilogue=None, extras=(),
            out_dtypes=(), after=None, whole_rows=False, out_widths=None):
    b_blocks = b.shape[0] if b.ndim == 3 else None
    bshape = b.shape if b.ndim == 2 else (b.shape[1], b.shape[0] * b.shape[2])
    if mode == "nn":
        (M, K), (K2, N) = a.shape, bshape
    elif mode == "nt":
        (M, K), (N, K2) = a.shape, bshape
    else:
        (K, M), (K2, N) = a.shape, bshape
    assert K == K2, (name, a.shape, b.shape)
    assert b_blocks is None or mode != "tn"
    assert out_blocks is None or mode == "tn"
    tn = min(tn, N // (out_blocks or 1), bshape[1] // b_blocks if (b_blocks and mode == "nn") else tn)
    blocked_k = bool(b_blocks) and mode == "nt"
    tm, tn, tk = _pick(M, tm), _pick(N, tn), (K if blocked_k else _pick(K, tk))

    def vmem_bytes(tm, tk):
        tiles = tm * tk * a.dtype.itemsize + tk * tn * b.dtype.itemsize
        for i, dt in enumerate(out_dtypes if epilogue else (out_dtype,)):
            tiles += tm * (out_widths[i] if out_widths else tn) * jnp.dtype(dt).itemsize
        for x in extras:
            arr = x[0] if isinstance(x, tuple) else x
            tiles += (tm if arr.shape[0] > 1 else 1) * tn * arr.dtype.itemsize
        return 2 * tiles + (tm * tn * 4 if K // tk > 1 else 0)

    def cost(tm, tk):
        ni, nj, nk = M // tm, N // tn, K // tk
        steps = ni * nj * nk
        acc_passes = steps * tm * tn * 8 * ACC_PASS_WEIGHT if nk > 1 else 0
        a_reads = M * K * a.dtype.itemsize * (nj if nk > 1 else 1)
        b_reads = K * N * b.dtype.itemsize * (ni if (nj > 1 or nk > 1) else 1)
        return (steps * STEP_COST_BYTES + acc_passes + vmem_bytes(tm, tk) // 2
                + HBM_COST_RATIO * (a_reads + b_reads))

    options = [(m, k) for m in ({M} if whole_rows else {_pick(M, max(t, LANES)) for t in (tm, tm // 2, tm // 4)})
               for k in ({K} if blocked_k else {_pick(K, max(t, LANES)) for t in (tk, tk // 2, tk // 4)})
               if vmem_bytes(m, k) <= MATMUL_VMEM_BUDGET]
    tm, tk = min(options, key=lambda o: cost(*o))
    nk = K // tk
    dims = {"nn": (((1,), (0,)), ((), ())), "nt": (((1,), (1,)), ((), ())), "tn": (((0,), (0,)), ((), ()))}[mode]

    n_x, n_o = len(extras), len(out_dtypes) if epilogue else 1
    n_after = 0 if after is None else 1

    def body(a_ref, b_ref, *rest):
        x_refs, o_refs, acc = rest[:n_x], rest[n_x + n_after:n_x + n_after + n_o], rest[n_x + n_after + n_o:]
        if blocked_k:
            bw = b.shape[2]
            part = sum(lax.dot_general(a_ref[:, blk * bw:(blk + 1) * bw].astype(BF16), b_ref[blk].astype(BF16), dims,
                                       preferred_element_type=F32) for blk in range(b_blocks))
        else:
            part = lax.dot_general(a_ref[...].astype(BF16), b_ref[...].astype(BF16), dims, preferred_element_type=F32)

        def finish(res):
            outs = epilogue(res, *[r[...] for r in x_refs]) if epilogue else (res,)
            for r, v in zip(o_refs, outs):
                r[...] = v.astype(r.dtype)

        if nk == 1:
            finish(part)
            return
        acc_ref, k = acc[0], pl.program_id(2)

        @pl.when(k == 0)
        def _():
            acc_ref[...] = part

        @pl.when(k > 0)
        def _():
            acc_ref[...] += part

        @pl.when(k == nk - 1)
        def _():
            finish(acc_ref[...])

    a_spec = {"nn": pl.BlockSpec((tm, tk), lambda i, j, k: (i, k)), "nt": pl.BlockSpec((tm, tk), lambda i, j, k: (i, k)),
              "tn": pl.BlockSpec((tk, tm), lambda i, j, k: (k, i))}[mode]
    b_spec = {"nn": pl.BlockSpec((tk, tn), lambda i, j, k: (k, j)), "nt": pl.BlockSpec((tn, tk), lambda i, j, k: (j, k)),
              "tn": pl.BlockSpec((tk, tn), lambda i, j, k: (k, j))}[mode]
    if b_blocks and mode == "nn":
        per = b.shape[2] // tn
        b_spec = pl.BlockSpec((None, tk, tn), lambda i, j, k: (j // per, k, j % per))
    elif b_blocks:
        b_spec = pl.BlockSpec((b_blocks, tn, b.shape[2]), lambda i, j, k: (0, j, 0))
    out_spec = pl.BlockSpec((tm, tn), lambda i, j, k: (i, j))
    out_shape = jax.ShapeDtypeStruct((M, N), out_dtype)
    if out_blocks:
        per_o = N // out_blocks // tn
        out_spec = pl.BlockSpec((None, tm, tn), lambda i, j, k: (j // per_o, i, j % per_o))
        out_shape = jax.ShapeDtypeStruct((out_blocks, M, N // out_blocks), out_dtype)
    epi_widths = list(out_widths) if out_widths else [N] * n_o
    assert all(w == N for w in epi_widths) or N == tn
    epi_specs = [pl.BlockSpec((tm, tn if w == N else w), lambda i, j, k: (i, j)) for w in epi_widths]
    x_specs, x_args = [], []
    for x in extras:
        arr, off = x if isinstance(x, tuple) else (x, 0)
        if arr.shape[0] == 1:
            x_specs.append(pl.BlockSpec((1, tn), lambda i, j, k: (0, j)))
        else:
            x_specs.append(pl.BlockSpec((tm, tn), lambda i, j, k, off=off: (i, j + off)))
        x_args.append(arr)
    res = pl.pallas_call(
        body, name=name, grid=(M // tm, N // tn, nk),
        in_specs=[a_spec, b_spec] + x_specs + [pl.BlockSpec(memory_space=pl.ANY)] * n_after,
        out_specs=epi_specs if epilogue else out_spec,
        out_shape=[jax.ShapeDtypeStruct((M, w), dt) for w, dt in zip(epi_widths, out_dtypes)] if epilogue else out_shape,
        scratch_shapes=[pltpu.VMEM((tm, tn), F32)] if nk > 1 else [],
        compiler_params=pltpu.CompilerParams(dimension_semantics=("parallel", "parallel", "arbitrary"),
                                             vmem_limit_bytes=VMEM_LIMIT),
    )(a, b, *x_args, *([after] if n_after else []))
    return res


class Rows:
    def __init__(self, arr, width=None, cb=0):
        self.arr, self.width, self.cb = arr, (arr.shape[1] if width is None else width), cb


class Heads:
    def __init__(self, arr):
        self.arr = arr


class Halo:
    def __init__(self, arr, side):
        self.arr, self.side = arr, side


def _rows_call(name, fn, ins, consts, outs, accs=(), tm=512, with_pid=False):
    T = next(o.arr.shape[1] if isinstance(o, Heads) else o.arr.shape[0] for o in ins if not isinstance(o, Halo))
    tm = min(tm, T)
    n_tiles = T // tm
    n_in, n_c, n_out = len(ins), len(consts), len(outs)
    in_specs, args = [], []
    for o in ins:
        if isinstance(o, Rows):
            in_specs.append(pl.BlockSpec((tm, o.width), lambda i, cb=o.cb: (i, cb)))
        elif isinstance(o, Heads):
            in_specs.append(pl.BlockSpec((NP, tm, PW), lambda i: (0, i, 0)))
        else:
            w = o.arr.shape[1]
            if o.side < 0:
                in_specs.append(pl.BlockSpec((8, w), lambda i: (jnp.maximum(i * (tm // 8) - 1, 0), 0)))
            else:
                in_specs.append(pl.BlockSpec((8, w), lambda i: (jnp.minimum((i + 1) * (tm // 8), T // 8 - 1), 0)))
        args.append(o.arr)
    for c in consts:
        in_specs.append(pl.BlockSpec(c.shape, lambda i, nd=c.ndim: (0,) * nd))
        args.append(c)
    out_specs, out_shape = [], []
    for o in outs:
        if o[0] == "rows":
            out_specs.append(pl.BlockSpec((tm, o[1]), lambda i: (i, 0)))
            out_shape.append(jax.ShapeDtypeStruct((T, o[1]), o[2]))
        else:
            out_specs.append(pl.BlockSpec((NP, tm, PW), lambda i: (0, i, 0)))
            out_shape.append(jax.ShapeDtypeStruct((NP, T, PW), o[1]))
    for shape, dt in accs:
        out_specs.append(pl.BlockSpec(shape, lambda i, nd=len(shape): (0,) * nd))
        out_shape.append(jax.ShapeDtypeStruct(shape, dt))

    def body(*refs):
        i = pl.program_id(0)
        vals = []
        vals = [r[...] for r in refs[:n_in + n_c]]
        res = fn(i, n_tiles, *vals) if with_pid else fn(*vals)
        out_refs = refs[n_in + n_c:]
        for r, v in zip(out_refs[:n_out], res[:n_out]):
            r[...] = v.astype(r.dtype)
        if accs:
            @pl.when(i == 0)
            def _():
                for r in out_refs[n_out:]:
                    r[...] = jnp.zeros_like(r)

            for r, v in zip(out_refs[n_out:], res[n_out:]):
                r[...] += v.astype(r.dtype)

    res = pl.pallas_call(
        body, name=name, grid=(n_tiles,), in_specs=in_specs, out_specs=out_specs, out_shape=out_shape,
        compiler_params=pltpu.CompilerParams(dimension_semantics=("arbitrary",), vmem_limit_bytes=VMEM_LIMIT),
    )(*args)
    return res


def _rms(x, g):
    return x * lax.rsqrt(jnp.mean(x * x, axis=-1, keepdims=True) + NORM_EPS) * g


def _gelu(x):
    return 0.5 * x * (1.0 + lax.erf(x * 0.7071067811865476))


def _sigmoid(x):
    return 1.0 / (1.0 + jnp.exp(-x))


def _bdot(a, b):
    return jnp.dot(a.astype(BF16), b.astype(BF16), preferred_element_type=F32)


def _to_heads(x):
    return jnp.concatenate([x[:, p * PW:(p + 1) * PW][None] for p in range(NP)], axis=0)


def _from_heads(xp):
    return jnp.concatenate([xp[p] for p in range(NP)], axis=-1)


def _head_sum(xp):
    low = lax.broadcasted_iota(jnp.int32, xp.shape, xp.ndim - 1) < HN
    both = jnp.sum(xp, axis=-1, keepdims=True)
    first = jnp.sum(jnp.where(low, xp, 0.0), axis=-1, keepdims=True)
    return jnp.where(low, first, both - first)


def _split_pairs(xp):
    return jnp.concatenate([xp[:, :, :HN], xp[:, :, HN:]], axis=0)


def _join_pairs(xh):
    return jnp.concatenate([xh[:NP], xh[NP:]], axis=-1)


def _sgu_fn(p, ln_w, ln_b, sw, sbt):
    z = _gelu(p)
    u, v = z[:, :D], z[:, D:]
    mu = jnp.mean(v, axis=-1, keepdims=True)
    var = jnp.mean(jnp.square(v - mu), axis=-1, keepdims=True)
    vn = (v - mu) * lax.rsqrt(var + LN_EPS) * ln_w + ln_b
    ri = lax.broadcasted_iota(jnp.int32, (SGU_C, SGU_C), 0)
    ci = lax.broadcasted_iota(jnp.int32, (SGU_C, SGU_C), 1)
    mask = (ci <= ri).astype(F32)
    dg = D // SGU_G
    parts = []
    for g in range(SGU_G):
        parts.append(_bdot(sw[g] * mask, vn[:, g * dg:(g + 1) * dg]) + sbt[:, g:g + 1])
    return u * jnp.concatenate(parts, axis=-1)


def _pre_fn(qr, qk, qv, qxw, qxa, qxg, wl, w0, al, a0, gl, k_k, k_a):
    w = -jax.nn.softplus(-(w0 + _bdot(jnp.tanh(qxw), wl))) - 0.5
    lw = -jnp.exp(w)
    aa = _sigmoid(a0 + _bdot(qxa, al))
    g = _bdot(_sigmoid(qxg), gl)
    kk = _to_heads(qk * k_k)
    kk = kk / jnp.maximum(jnp.sqrt(_head_sum(kk * kk)), 1e-12)
    k2 = qk * (1.0 + (aa - 1.0) * k_a)
    return _to_heads(qr), _to_heads(lw), _to_heads(k2), _to_heads(qv), kk, _to_heads(aa), g


def _post_fn(o, r, k2, v, g, ln_w, ln_b, r_k):
    mu = _head_sum(o) * (1.0 / HN)
    d = o - mu
    var = _head_sum(d * d) * (1.0 / HN)
    on = d * lax.rsqrt(var + GN_EPS) * ln_w + ln_b
    bonus = _head_sum(r * k2 * r_k) * v
    return _from_heads(on + bonus) * g


def _gate_fn(pg, ya, yb):
    return _sigmoid(pg[:, :D]) * ya + _sigmoid(pg[:, D:]) * yb


def _bmm(x, y, cx, cy, out_path=False):
    return lax.dot_general(x, y, (((cx,), (cy,)), ((0,), (0,))),
                           precision=SCAN_OUT_PRECISION if out_path else SCAN_PRECISION, preferred_element_type=F32)


def _unit_lower_inverse(M):
    C = M.shape[1]
    ti = lax.broadcasted_iota(jnp.int32, (C, C), 0)
    tj = lax.broadcasted_iota(jnp.int32, (C, C), 1)
    eye = (ti == tj).astype(F32)
    same = lambda b: (ti // b == tj // b).astype(F32)
    X = -(M * same(SOLVE_B))
    inv = eye + X
    span = 1
    while 2 * span < SOLVE_B:
        X = _bmm(X, X, 2, 1)
        inv = inv + _bmm(inv, X, 2, 1)
        span *= 2
    b = SOLVE_B
    while b < C:
        low = M * (same(2 * b) - same(b))
        inv = inv - _bmm(_bmm(inv, low, 2, 1, out_path=True), inv, 2, 1, out_path=True)
        b *= 2
    return inv


@jax.custom_vjp
def _unit_lower_solve(inv, M, y):
    return _bmm(inv, y, 2, 1)


def _unit_lower_solve_fwd(inv, M, y):
    u = _bmm(inv, y, 2, 1)
    return u, (inv, u)


def _unit_lower_solve_bwd(res, du):
    inv, u = res
    dy = _bmm(inv, du, 1, 1)
    return jnp.zeros_like(inv), -_bmm(dy, u, 2, 2), dy


_unit_lower_solve.defvjp(_unit_lower_solve_fwd, _unit_lower_solve_bwd)


@jax.custom_vjp
def _unit_lower_solved(inv, u, M, y):
    return u


_unit_lower_solved.defvjp(lambda inv, u, M, y: (u, (inv, u)),
                          lambda res, du: (jnp.zeros_like(res[0]), jnp.zeros_like(res[1]))
                          + _unit_lower_solve_bwd(res, du)[1:])


@functools.partial(jax.custom_vjp, nondiff_argnums=(0,))
def _kept(fn, value, *args):
    return value


def _kept_fwd(fn, value, *args):
    return value, args


def _kept_bwd(fn, args, d):
    _, vjp = jax.vjp(fn, *args)
    return (jnp.zeros_like(d),) + tuple(vjp(d))


_kept.defvjp(_kept_fwd, _kept_bwd)


def _sum_over_time(x, reverse):
    C = x.shape[1]
    ti = lax.broadcasted_iota(jnp.int32, (C, C), 0)
    tj = lax.broadcasted_iota(jnp.int32, (C, C), 1)
    ones = jnp.broadcast_to(((tj >= ti) if reverse else (tj <= ti)).astype(BF16), (x.shape[0], C, C))
    hi = x.astype(BF16)
    r1 = x - hi.astype(F32)
    mid = r1.astype(BF16)
    lo = (r1 - mid.astype(F32)).astype(BF16)
    dn = (((2,), (1,)), ((0,), (0,)))
    return sum(lax.dot_general(ones, p, dn, preferred_element_type=F32) for p in (lo, mid, hi))


@jax.custom_vjp
def _time_cumsum(lw):
    return _sum_over_time(lw, reverse=False)


_time_cumsum.defvjp(lambda lw: (_sum_over_time(lw, reverse=False), None),
                    lambda _, d: (_sum_over_time(d, reverse=True),))


def _chunk_fn(S0, r, lw, k, v, kk, a, kept=None):
    C = SCAN_C
    bmm = _bmm
    ti = lax.broadcasted_iota(jnp.int32, (C, C), 0)
    tj = lax.broadcasted_iota(jnp.int32, (C, C), 1)
    incl2 = jnp.concatenate([(tj <= ti).astype(F32)] * 2, axis=1)
    strict = (tj < ti).astype(F32)
    n_mask = jnp.concatenate([jnp.zeros((C, C), F32), strict], axis=1)

    def known(name, fn, *args):
        return fn(*args) if kept is None else _kept(fn, kept[name], *args)

    cum = known("cum", _time_cumsum, lw)
    g_in, g_ex, g_inv = jnp.exp(cum), jnp.exp(cum - lw), jnp.exp(-cum)
    kkt, rt = kk * g_ex, r * g_in
    bk = jnp.concatenate([kk * a * g_inv, k * g_inv], axis=1)
    kr = jnp.concatenate([kkt, rt], axis=1)
    ratios = known("ratios", lambda x, y: bmm(x, y, 2, 2), kr, bk)
    A = ratios[:, :C]
    M = A[:, :, :C] * strict
    zv = jnp.concatenate([jnp.zeros_like(v), v], axis=1)
    s0_side = bmm(kr, S0, 2, 2, out_path=True)
    rhs = s0_side[:, :C] + bmm(A * n_mask, zv, 2, 1, out_path=True)
    if kept is None:
        inv = lax.stop_gradient(_unit_lower_inverse(M))
        y = _unit_lower_solve(inv, M, rhs)
    else:
        inv = kept["inv"]
        y = _unit_lower_solved(inv, kept["y"], M, rhs)
    z = jnp.concatenate([-y, v], axis=1)
    O = s0_side[:, C:] + bmm(ratios[:, C:] * incl2, z, 2, 1, out_path=True)
    g_end = g_in[:, C - 1:C, :]
    S1 = S0 * g_end + bmm(z, bk * g_end, 1, 1, out_path=True)
    return O, S1, dict(cum=cum, ratios=ratios, y=y, inv=inv)


def _scan_fwd(r, lw, k, v, kk, a, ex=None, tb=256):
    assert SCAN_C == HN and 2 * SCAN_C == PW
    T = r.shape[1]
    tb = min(tb, T)
    n_chunks = tb // SCAN_C
    nb = T // tb
    nx = ex.nb if ex else 0

    def body(*refs):
        r_ref, lw_ref, k_ref, v_ref, kk_ref, a_ref = refs[:6]
        x_in, (o_ref, s0_ref), x_out = refs[6:6 + nx], refs[6 + nx:8 + nx], refs[8 + nx:8 + 2 * nx]
        s_ref, sems = refs[8 + 2 * nx], refs[9 + 2 * nx:]

        plan = ex.schedule(nb) if ex else []

        @pl.when(pl.program_id(0) == 0)
        def _():
            s_ref[...] = jnp.zeros_like(s_ref)
            for at, action in plan[:1]:
                action(x_in, x_out, sems)

        def step(c, carry):
            sl = pl.ds(pl.multiple_of(c * SCAN_C, SCAN_C), SCAN_C)
            S0 = s_ref[...]
            O, S1, keep = _chunk_fn(S0, *[_split_pairs(ref[:, sl, :])
                                          for ref in (r_ref, lw_ref, k_ref, v_ref, kk_ref, a_ref)])
            o_ref[:, sl, :] = _join_pairs(O)
            s0_ref[c, 0] = jnp.concatenate([S0, keep["inv"]], axis=-1)
            s0_ref[c, 1] = jnp.concatenate([keep["cum"], keep["y"]], axis=-1)
            s0_ref[c, 2] = keep["ratios"][:, :SCAN_C]
            s0_ref[c, 3] = keep["ratios"][:, SCAN_C:]
            s_ref[...] = S1
            return carry

        lax.fori_loop(0, n_chunks, step, 0)

        for at, action in plan[1:]:
            pl.when(pl.program_id(0) == at)(functools.partial(action, x_in, x_out, sems))

    hm = pl.BlockSpec((NP, tb, PW), lambda i: (0, i, 0))
    res = pl.pallas_call(
        body, name="rwkv_scan_fwd", grid=(nb,), in_specs=[hm] * 6 + (ex.any_specs if ex else []),
        out_specs=[hm, pl.BlockSpec((n_chunks, N_KEPT, NH, HN, PW), lambda i: (i, 0, 0, 0, 0))]
        + (ex.any_specs if ex else []),
        out_shape=[jax.ShapeDtypeStruct((NP, T, PW), F32),
                   jax.ShapeDtypeStruct((T // SCAN_C, N_KEPT, NH, HN, PW), F32)]
        + (ex.out_shape if ex else []),
        scratch_shapes=[pltpu.VMEM((NH, HN, HN), F32)] + (ex.sem_shapes if ex else []),
        compiler_params=pltpu.CompilerParams(dimension_semantics=("arbitrary",), vmem_limit_bytes=VMEM_LIMIT),
    )(r, lw, k, v, kk, a, *(ex.bufs if ex else []))
    return res[0], res[1], list(res[2:])


def _scan_bwd(r, lw, k, v, kk, a, s0s, do, ex=None, tb=128):
    T = r.shape[1]
    tb = min(tb, T)
    n_chunks = tb // SCAN_C
    nb = T // tb
    nx = ex.nb if ex else 0

    def body(*refs):
        r_ref, lw_ref, k_ref, v_ref, kk_ref, a_ref, s0_ref, do_ref = refs[:8]
        x_in, (dr, dlw, dk, dv, dkk, da), x_out = refs[8:8 + nx], refs[8 + nx:14 + nx], refs[14 + nx:14 + 2 * nx]
        ds_ref, sems = refs[14 + 2 * nx], refs[15 + 2 * nx:]

        plan = ex.schedule(nb) if ex else []

        @pl.when(pl.program_id(0) == 0)
        def _():
            ds_ref[...] = jnp.zeros_like(ds_ref)
            for at, action in plan[:1]:
                action(x_in, x_out, sems)

        def step(j, carry):
            c = n_chunks - 1 - j
            sl = pl.ds(pl.multiple_of(c * SCAN_C, SCAN_C), SCAN_C)
            s0_inv, cum_y = s0_ref[c, 0], s0_ref[c, 1]
            kept = dict(inv=s0_inv[:, :, HN:], cum=cum_y[:, :, :HN], y=cum_y[:, :, HN:],
                        ratios=jnp.concatenate([s0_ref[c, 2], s0_ref[c, 3]], axis=1))
            _, vjp = jax.vjp(lambda *t: _chunk_fn(*t, kept=kept)[:2], s0_inv[:, :, :HN],
                             *[_split_pairs(ref[:, sl, :]) for ref in (r_ref, lw_ref, k_ref, v_ref, kk_ref, a_ref)])
            g = vjp((_split_pairs(do_ref[:, sl, :]), ds_ref[...]))
            ds_ref[...] = g[0]
            for ref, val in zip((dr, dlw, dk, dv, dkk, da), g[1:]):
                ref[:, sl, :] = _join_pairs(val)
            return carry

        lax.fori_loop(0, n_chunks, step, 0)

        for at, action in plan[1:]:
            pl.when(pl.program_id(0) == at)(functools.partial(action, x_in, x_out, sems))

    hm = pl.BlockSpec((NP, tb, PW), lambda i: (0, nb - 1 - i, 0))
    res = pl.pallas_call(
        body, name="rwkv_scan_bwd", grid=(nb,),
        in_specs=[hm] * 6 + [pl.BlockSpec((n_chunks, N_KEPT, NH, HN, PW), lambda i: (nb - 1 - i, 0, 0, 0, 0)), hm]
        + (ex.any_specs if ex else []),
        out_specs=[hm] * 6 + (ex.any_specs if ex else []),
        out_shape=[jax.ShapeDtypeStruct((NP, T, PW), F32)] * 6 + (ex.out_shape if ex else []),
        scratch_shapes=[pltpu.VMEM((NH, HN, HN), F32)] + (ex.sem_shapes if ex else []),
        compiler_params=pltpu.CompilerParams(dimension_semantics=("arbitrary",), vmem_limit_bytes=VMEM_LIMIT),
    )(r, lw, k, v, kk, a, s0s, do, *(ex.bufs if ex else []))
    return list(res[:6]), list(res[6:])


def _shift_down(i, p, prev8):
    first = jnp.where(i > 0, prev8[7:8, :], 0.0)
    row = lax.broadcasted_iota(jnp.int32, p.shape, 0)
    return jnp.where(row == 0, first, pltpu.roll(p, 1, axis=0))


def _mix_bwd(dq, p, sb, tm=256):
    def fn(i, n, dq, next8, p, prev8, sb):
        ps = _shift_down(i, p, prev8)
        d1 = dq * sb[1:2]
        last = jnp.where(i < n - 1, next8[0:1, :] * sb[1:2], 0.0)
        row = lax.broadcasted_iota(jnp.int32, dq.shape, 0)
        up = jnp.where(row == dq.shape[0] - 1, last, pltpu.roll(d1, dq.shape[0] - 1, axis=0))
        return (dq * sb[0:1] + up, jnp.sum(dq * p, axis=0, keepdims=True), jnp.sum(dq * ps, axis=0, keepdims=True))
    w = p.shape[1]
    return _rows_call("shift_mix_bwd", fn, [Rows(dq), Halo(dq, +1), Rows(p), Halo(p, -1)], [sb], [("rows", w, BF16)],
                      accs=[((1, w), F32), ((1, w), F32)], tm=tm, with_pid=True)


def _local_step(x, target, W, late_weights=None, early_grads=None, w_in_grads_ready=None, ffn_grads_ready=None):
    G = {}
    a = _rows_call("norm_mix_fwd", lambda x, g: (_rms(x, g),), [Rows(x)], [W["g_mix"]], [("rows", D, BF16)])[0]
    p_sgu = _matmul("proj_sgu", a, W["w_sgu_t"], "nt")
    def token_shift(p, sb0, sb1):
        row = lax.broadcasted_iota(jnp.int32, p.shape, 0)
        return p, p * sb0 + jnp.where(row == 0, 0.0, pltpu.roll(p, 1, axis=0)) * sb1
    p_rw, q = _matmul("proj_rwkv", a, W["w_rw_t"], "nt", tn=512, whole_rows=True, epilogue=token_shift,
                      extras=[W["sb"][0:1], W["sb"][1:2]], out_dtypes=(F32, F32))
    p_gate = _matmul("proj_gate", a, W["w_gate_t"], "nt")

    sgu_consts = [W["sgu_ln_w"], W["sgu_ln_b"], W["sgu_w"], W["sgu_bt"]]
    s = _rows_call("sgu_fwd", lambda *t: (_sgu_fn(*t),), [Rows(p_sgu)], sgu_consts, [("rows", D, BF16)], tm=SGU_C)[0]

    q_ins = [Rows(q, D, 0), Rows(q, D, 1), Rows(q, D, 2), Rows(q, 128, 24), Rows(q, 128, 25), Rows(q, 256, 13)]
    pre_consts = [W["w_lora"], W["w0"], W["a_lora"], W["a0"], W["g_lora"], W["k_k"], W["k_a"]]
    r_h, lw_h, k_h, v_h, kk_h, a_h, g_gate = _rows_call(
        "rwkv_pre_fwd", _pre_fn, q_ins, pre_consts, [("heads", F32)] * 6 + [("rows", D, F32)], tm=128)
    o_h, s0s, got = _scan_fwd(r_h, lw_h, k_h, v_h, kk_h, a_h, ex=late_weights[0] if late_weights else None)
    if late_weights:
        W = {**W, **late_weights[1](got)}
    y_a = _matmul("proj_a", s, W["w_proj_a"], "nn")
    post_ins = [Heads(o_h), Heads(r_h), Heads(k_h), Heads(v_h), Rows(g_gate)]
    post_consts = [W[n].reshape(NP, 1, PW) for n in ("ln_x_w", "ln_x_b", "r_k")]
    z_b = _rows_call("rwkv_post_fwd", lambda *t: (_post_fn(*t),), post_ins, post_consts, [("rows", D, BF16)], tm=128)[0]
    y_b, mixed = _matmul("proj_b", z_b, W["w_proj_b"], "nn", extras=[(p_gate, 0), (p_gate, 1), y_a],
                         epilogue=lambda yb, ga, gb, ya: (yb, _sigmoid(ga) * ya + _sigmoid(gb) * yb),
                         out_dtypes=(F32, BF16))

    def res1(mo, x, g):
        h1 = x + mo
        return h1, _rms(h1, g)
    h1, f = _matmul("proj_out", mixed, W["w_out"], "nn", extras=[x, W["g_ffn"]], epilogue=res1,
                    out_dtypes=(F32, BF16))

    def relu_sq(u):
        r = jnp.maximum(u, 0.0)
        return r, r * r
    r1, act = _matmul("ffn_up", f, W["w_ffn1"], "nn", epilogue=relu_sq, out_dtypes=(BF16, BF16))
    ff = _matmul("ffn_down", act, W["w_ffn2"], "nn")

    def head(h1, ff, tgt, g):
        def f_(h1, ff, g):
            y = _rms(h1 + ff, g)
            return 0.5 * jnp.sum(jnp.mean(jnp.square(y - tgt), axis=-1))
        loss, (dh2, _, dg) = jax.value_and_grad(f_, argnums=(0, 1, 2))(h1, ff, g)
        return dh2, jnp.full((8, LANES), loss, F32), dg
    dh2, loss_acc, G["g_final"] = _rows_call("loss_head", head, [Rows(h1), Rows(ff), Rows(target)], [W["g_final"]],
                                             [("rows", D, F32)], accs=[((8, LANES), F32), ((1, D), F32)])

    d_u1 = _matmul("ffn_down_dx", dh2, W["w_ffn2"], "nt", extras=[r1], out_dtypes=(BF16,),
                   epilogue=lambda d_act, r: (d_act * 2.0 * r.astype(F32),))[0]
    G["w_ffn2"] = _matmul("ffn_down_dw", act, dh2, "tn", out_dtype=GRAD_PAYLOAD)
    d_f = _matmul("ffn_up_dx", d_u1, W["w_ffn1"], "nt")
    G["w_ffn1"] = _matmul("ffn_up_dw", f, d_u1, "tn", out_blocks=N_DEV, out_dtype=GRAD_PAYLOAD)

    def res1_bwd(h1, d_f, dh2, g):
        _, vjp = jax.vjp(_rms, h1, g)
        dh, dg = vjp(d_f)
        return dh2 + dh, dg
    dh1, G["g_ffn"] = _rows_call("residual_norm_bwd", res1_bwd, [Rows(h1), Rows(d_f), Rows(dh2)],
                                 [W["g_ffn"]], [("rows", D, F32)], accs=[((1, D), F32)])
    ffn_token = ffn_grads_ready(G) if ffn_grads_ready else None
    def gate_bwd(d_mixed, ga, gb, ya, yb):
        _, vjp = jax.vjp(_gate_fn, jnp.concatenate([ga, gb], axis=-1), ya, yb)
        return vjp(d_mixed)
    d_gate, d_ya, d_yb = _matmul("proj_out_dx", dh1, W["w_out"], "nt", after=ffn_token, epilogue=gate_bwd,
                                 extras=[(p_gate, 0), (p_gate, 1), y_a, y_b], out_dtypes=(BF16, BF16, BF16),
                                 out_widths=(2 * D, D, D))
    G["w_out"] = _matmul("proj_out_dw", mixed, dh1, "tn", out_dtype=GRAD_PAYLOAD)

    d_s = _matmul("proj_a_dx", d_ya, W["w_proj_a"], "nt")
    G["w_proj_a"] = _matmul("proj_a_dw", s, d_ya, "tn", out_dtype=GRAD_PAYLOAD)

    def sgu_bwd(p, ds, *c):
        _, vjp = jax.vjp(_sgu_fn, p, *c)
        return vjp(ds)
    d_p_sgu, G["sgu_ln_w"], G["sgu_ln_b"], G["sgu_w"], G["sgu_bt"] = _rows_call(
        "sgu_bwd", sgu_bwd, [Rows(p_sgu), Rows(d_s)], sgu_consts, [("rows", 2 * D, BF16)],
        accs=[((1, D), F32), ((1, D), F32), ((SGU_G, SGU_C, SGU_C), F32), ((SGU_C, SGU_G), F32)], tm=SGU_C)

    d_zb = _matmul("proj_b_dx", d_yb, W["w_proj_b"], "nt")
    G["w_proj_b"] = _matmul("proj_b_dw", z_b, d_yb, "tn", out_dtype=GRAD_PAYLOAD)

    def post_bwd(o, r, k2, v, g, dz, *c):
        _, vjp = jax.vjp(_post_fn, o, r, k2, v, g, *c)
        return vjp(dz)
    do_h, dr1, dk1, dv1, d_g, g_lnw, g_lnb, g_rk = _rows_call(
        "rwkv_post_bwd", post_bwd, post_ins + [Rows(d_zb)], post_consts, [("heads", F32)] * 4 + [("rows", D, F32)],
        accs=[((NP, 1, PW), F32)] * 3, tm=128)
    G["ln_x_w"], G["ln_x_b"], G["r_k"] = (t.reshape(1, D) for t in (g_lnw, g_lnb, g_rk))
    (dr2, dlw, dk2, dv2, dkk, daa), early = _scan_bwd(r_h, lw_h, k_h, v_h, kk_h, a_h, s0s, do_h,
                                                      ex=early_grads(G) if early_grads else None)

    def pre_bwd(qr, qk, qv, qxw, qxa, qxg, dr1, dr2, dlw, dk1, dk2, dv1, dv2, dkk, daa, dg, *c):
        _, vjp = jax.vjp(_pre_fn, qr, qk, qv, qxw, qxa, qxg, *c)
        g = vjp((dr1 + dr2, dlw, dk1 + dk2, dv1 + dv2, dkk, daa, dg))
        dq = jnp.concatenate(g[:6], axis=-1)
        return (dq,) + tuple(g[6:])
    pre_b_ins = q_ins + [Heads(dr1), Heads(dr2), Heads(dlw), Heads(dk1), Heads(dk2), Heads(dv1), Heads(dv2),
                         Heads(dkk), Heads(daa), Rows(d_g)]
    d_q, G["w_lora"], G["w0"], G["a_lora"], G["a0"], G["g_lora"], G["k_k"], G["k_a"] = _rows_call(
        "rwkv_pre_bwd", pre_bwd, pre_b_ins, pre_consts, [("rows", RW_INT, F32)],
        accs=[((128, D), F32), ((1, D), F32), ((128, D), F32), ((1, D), F32), ((256, D), F32), ((1, D), F32),
              ((1, D), F32)], tm=128)
    d_p_rw, dsb0, dsb1 = _mix_bwd(d_q, p_rw, W["sb"])
    G["sb"] = jnp.concatenate([dsb0, dsb1], axis=0)

    G["w_sgu_t"] = _matmul("proj_sgu_dw", d_p_sgu, a, "tn", out_dtype=GRAD_PAYLOAD)
    G["w_rw_t"] = _matmul("proj_rwkv_dw", d_p_rw, a, "tn", out_dtype=GRAD_PAYLOAD)
    G["w_gate_t"] = _matmul("proj_gate_dw", d_gate, a, "tn", out_dtype=GRAD_PAYLOAD)
    token = w_in_grads_ready(G) if w_in_grads_ready else None
    da1 = _matmul("proj_sgu_dx", d_p_sgu, W["w_sgu_t"], "nn", after=token)
    da2 = _matmul("proj_rwkv_dx", d_p_rw, W["w_rw_t"], "nn", after=token)
    da3 = _matmul("proj_gate_dx", d_gate, W["w_gate_t"], "nn", after=token)

    def norm1_bwd(x, da1, da2, da3, dh1, g):
        _, vjp = jax.vjp(_rms, x, g)
        dx, dg = vjp(da1 + da2 + da3)
        return dh1 + dx, dg
    dx, G["g_mix"] = _rows_call("norm_mix_bwd", norm1_bwd, [Rows(x), Rows(da1), Rows(da2), Rows(da3), Rows(dh1)],
                                [W["g_mix"]], [("rows", D, F32)], accs=[((1, D), F32)])
    return loss_acc[0, 0], dx, G, early


class Exchange:
    def __init__(self, bufs, gathers):
        self.bufs, self.gathers, self.nb = list(bufs), list(gathers), len(bufs)
        self.any_specs = [pl.BlockSpec(memory_space=pl.ANY)] * self.nb
        self.out_shape = [jax.ShapeDtypeStruct((N_DEV,) + (b.shape if g else b.shape[1:]), b.dtype)
                          for b, g in zip(self.bufs, self.gathers)]
        n = (N_DEV - 1) * self.nb
        self.sem_shapes = [pltpu.SemaphoreType.DMA((n,)), pltpu.SemaphoreType.DMA((n,)),
                           pltpu.SemaphoreType.DMA((self.nb,))]

    def _copies(self, in_refs, out_refs, sems):
        send_sems, recv_sems, local_sems = sems
        x, y, c = lax.axis_index("x"), lax.axis_index("y"), lax.axis_index("c")
        me = 4 * x + 2 * y + c

        def src(b, dest):
            return in_refs[b] if self.gathers[b] else in_refs[b].at[dest]

        local = [pltpu.make_async_copy(src(b, me), out_refs[b].at[me], local_sems.at[b]) for b in range(self.nb)]
        sends, recvs = [], []
        for kbits in range(1, N_DEV):
            px = 1 - x if kbits & 4 else x
            py = 1 - y if kbits & 2 else y
            pc = 1 - c if kbits & 1 else c
            peer = 4 * px + 2 * py + pc
            for b in range(self.nb):
                s = (kbits - 1) * self.nb + b
                sends.append(pltpu.make_async_remote_copy(
                    src_ref=src(b, peer), dst_ref=out_refs[b].at[me], send_sem=send_sems.at[s],
                    recv_sem=recv_sems.at[s], device_id=(px, py, pc), device_id_type=pl.DeviceIdType.MESH))
                recvs.append(pltpu.make_async_remote_copy(
                    src_ref=src(b, peer), dst_ref=out_refs[b].at[peer], send_sem=send_sems.at[s],
                    recv_sem=recv_sems.at[s], device_id=(px, py, pc), device_id_type=pl.DeviceIdType.MESH))
        return local, sends, recvs

    def start(self, in_refs, out_refs, sems):
        local, sends, _ = self._copies(in_refs, out_refs, sems)
        for cp in sends + local:
            cp.start()

    def wait(self, in_refs, out_refs, sems):
        local, sends, recvs = self._copies(in_refs, out_refs, sems)
        for cp in recvs:
            cp.wait_recv()
        for cp in sends:
            cp.wait_send()
        for cp in local:
            cp.wait()

    def schedule(self, n_steps):
        return [(0, self.start), (n_steps - 1, self.wait)]


def _exchange(name, bufs, gather):
    ex = Exchange(bufs, gather if isinstance(gather, (list, tuple)) else [gather] * len(bufs))

    def body(*refs):
        in_refs, out_refs, sems = refs[:ex.nb], refs[ex.nb:2 * ex.nb], refs[2 * ex.nb:]
        ex.start(in_refs, out_refs, sems)
        ex.wait(in_refs, out_refs, sems)

    return pl.pallas_call(body, name=name, in_specs=ex.any_specs, out_specs=ex.any_specs, out_shape=ex.out_shape,
                          scratch_shapes=ex.sem_shapes)(*ex.bufs)


N_CHIP = 4


def _pair_exchange(name, blocks):
    def body(b_ref, got_ref, send_sems, recv_sems):
        x, y, c = lax.axis_index("x"), lax.axis_index("y"), lax.axis_index("c")
        copies = [pltpu.make_async_remote_copy(
            src_ref=b_ref.at[2 * q + 1 - c], dst_ref=got_ref.at[q], send_sem=send_sems.at[q],
            recv_sem=recv_sems.at[q], device_id=(x, y, 1 - c), device_id_type=pl.DeviceIdType.MESH)
            for q in range(N_CHIP)]
        for cp in copies:
            cp.start()
        for cp in copies:
            cp.wait_recv()
        for cp in copies:
            cp.wait_send()

    any_spec = pl.BlockSpec(memory_space=pl.ANY)
    return pl.pallas_call(
        body, name=name, in_specs=[any_spec], out_specs=any_spec,
        out_shape=jax.ShapeDtypeStruct((N_CHIP,) + blocks.shape[1:], blocks.dtype),
        scratch_shapes=[pltpu.SemaphoreType.DMA((N_CHIP,))] * 2,
    )(blocks)


def _pair_sum(name, blocks, got, core):
    _, R, Wd = blocks.shape

    def body(c_ref, a_ref, b_ref, o_ref):
        o_ref[...] = (a_ref[...].astype(F32) + b_ref[...].astype(F32)).astype(o_ref.dtype)

    return pl.pallas_call(
        body, name=name,
        grid_spec=pltpu.PrefetchScalarGridSpec(
            num_scalar_prefetch=1, grid=(N_CHIP,),
            in_specs=[pl.BlockSpec((None, R, Wd), lambda q, c_ref: (2 * q + c_ref[0], 0, 0)),
                      pl.BlockSpec((None, R, Wd), lambda q, c_ref: (q, 0, 0))],
            out_specs=pl.BlockSpec((None, R, Wd), lambda q, c_ref: (q, 0, 0))),
        out_shape=jax.ShapeDtypeStruct(got.shape, blocks.dtype),
        compiler_params=pltpu.CompilerParams(dimension_semantics=("parallel",), vmem_limit_bytes=VMEM_LIMIT),
    )(core, blocks, got)


def _chip_copies(s_ref, land_ref, send_sems, recv_sems):
    x, y, c = lax.axis_index("x"), lax.axis_index("y"), lax.axis_index("c")
    my_q = 2 * x + y
    sends, recvs = [], []
    for kbits in range(1, N_CHIP):
        px = 1 - x if kbits & 2 else x
        py = 1 - y if kbits & 1 else y
        peer_q = 2 * px + py
        sends.append(pltpu.make_async_remote_copy(
            src_ref=s_ref.at[peer_q], dst_ref=land_ref.at[my_q], send_sem=send_sems[kbits - 1],
            recv_sem=recv_sems[kbits - 1], device_id=(px, py, c), device_id_type=pl.DeviceIdType.MESH))
        recvs.append(pltpu.make_async_remote_copy(
            src_ref=s_ref.at[peer_q], dst_ref=land_ref.at[peer_q], send_sem=send_sems[kbits - 1],
            recv_sem=recv_sems[kbits - 1], device_id=(px, py, c), device_id_type=pl.DeviceIdType.MESH))
    return sends, recvs


_HBM = pl.BlockSpec(memory_space=pltpu.HBM)
_SEM = pl.BlockSpec(memory_space=pltpu.SEMAPHORE)
N_CHIP_SEMS = 2 * (N_CHIP - 1)


def _scatter_copies(b_refs, land_refs, send_sems, recv_sems):
    x, y, c = lax.axis_index("x"), lax.axis_index("y"), lax.axis_index("c")
    me = 4 * x + 2 * y + c
    sends, recvs = [], []
    for kbits in range(1, N_DEV):
        px = 1 - x if kbits & 4 else x
        py = 1 - y if kbits & 2 else y
        pc = 1 - c if kbits & 1 else c
        peer = 4 * px + 2 * py + pc
        for b in range(len(b_refs)):
            s = (kbits - 1) * len(b_refs) + b
            sends.append(pltpu.make_async_remote_copy(
                src_ref=b_refs[b].at[peer], dst_ref=land_refs[b].at[me], send_sem=send_sems[s],
                recv_sem=recv_sems[s], device_id=(px, py, pc), device_id_type=pl.DeviceIdType.MESH))
            recvs.append(pltpu.make_async_remote_copy(
                src_ref=b_refs[b].at[peer], dst_ref=land_refs[b].at[peer], send_sem=send_sems[s],
                recv_sem=recv_sems[s], device_id=(px, py, pc), device_id_type=pl.DeviceIdType.MESH))
    return sends, recvs


def _scatter_start(name, bufs):
    nb = len(bufs)
    n = (N_DEV - 1) * nb

    def body(*refs):
        b_refs, land_refs, outs = refs[:nb], refs[nb:2 * nb], refs[2 * nb:]
        sends, _ = _scatter_copies(b_refs, land_refs, outs[:n], outs[n:2 * n])
        for cp in sends:
            cp.start()
        token = outs[2 * n + 2 * nb]
        token[...] = jnp.zeros_like(token)

    thru = tuple(pltpu.HBM(b.shape, b.dtype) for b in bufs)
    res = pl.pallas_call(
        body, name=name, in_specs=(_HBM,) * (2 * nb),
        out_specs=(_SEM,) * (2 * n) + (_HBM,) * (2 * nb) + (pl.BlockSpec(memory_space=pltpu.VMEM),),
        out_shape=(pltpu.SemaphoreType.DMA(()),) * (2 * n) + thru + thru + (jax.ShapeDtypeStruct((8, LANES), F32),),
        input_output_aliases={i: 2 * n + i for i in range(2 * nb)},
        compiler_params=pltpu.CompilerParams(has_side_effects=pltpu.SideEffectType.DATAFLOW_SIDE_EFFECTING),
    )(*[pltpu.with_memory_space_constraint(b, pltpu.HBM) for b in bufs],
      *[pltpu.with_memory_space_constraint(lax.empty(b.shape, b.dtype), pltpu.HBM) for b in bufs])
    return res[:2 * n], list(res[2 * n:2 * n + nb]), list(res[2 * n + nb:2 * n + 2 * nb]), res[2 * n + 2 * nb]


def _copy_through_vmem(pairs, stage, sems):
    for into_vmem in (True, False):
        copies = [pltpu.make_async_copy(src if into_vmem else stage[i], stage[i] if into_vmem else dst, sems.at[i])
                  for i, (src, dst) in enumerate(pairs)]
        for cp in copies:
            cp.start()
        for cp in copies:
            cp.wait()


def _scatter_wait(name, sems, bufs_thru, lands_thru, after):
    nb = len(bufs_thru)
    n = (N_DEV - 1) * nb

    def body(*refs):
        b_refs, land_refs, sem_refs = refs[:nb], refs[nb:2 * nb], refs[2 * nb:2 * nb + 2 * n]
        stage, local_sems = refs[-(nb + 1):-1], refs[-1]
        me = 4 * lax.axis_index("x") + 2 * lax.axis_index("y") + lax.axis_index("c")
        _copy_through_vmem([(b_refs[b].at[me], land_refs[b].at[me]) for b in range(nb)], stage, local_sems)
        sends, recvs = _scatter_copies(b_refs, land_refs, sem_refs[:n], sem_refs[n:])
        for cp in sends:
            cp.wait_send()
        for cp in recvs:
            cp.wait_recv()

    thru = tuple(pltpu.HBM(b.shape, b.dtype) for b in bufs_thru)
    res = pl.pallas_call(
        body, name=name, in_specs=(_HBM,) * (2 * nb) + (_SEM,) * (2 * n) + (pl.BlockSpec(memory_space=pl.ANY),),
        out_specs=(_HBM,) * (2 * nb), out_shape=thru + thru,
        input_output_aliases={i: i for i in range(2 * nb)},
        scratch_shapes=[pltpu.VMEM(b.shape[1:], b.dtype) for b in bufs_thru] + [pltpu.SemaphoreType.DMA((nb,))],
        compiler_params=pltpu.CompilerParams(has_side_effects=pltpu.SideEffectType.DATAFLOW_SIDE_EFFECTING,
                                             vmem_limit_bytes=VMEM_LIMIT),
    )(*bufs_thru, *lands_thru, *sems, after)
    return list(res[:nb]), list(res[nb:])


def _chip_exchange_start(name, sums):
    def body(s_ref, land_ref, *outs):
        sems, token = outs[:N_CHIP_SEMS], outs[N_CHIP_SEMS + 2]
        sends, _ = _chip_copies(s_ref, land_ref, sems[:N_CHIP - 1], sems[N_CHIP - 1:])
        for cp in sends:
            cp.start()
        token[...] = jnp.zeros_like(token)

    res = pl.pallas_call(
        body, name=name, in_specs=(_HBM, _HBM),
        out_specs=(_SEM,) * N_CHIP_SEMS + (_HBM, _HBM, pl.BlockSpec(memory_space=pltpu.VMEM)),
        out_shape=(pltpu.SemaphoreType.DMA(()),) * N_CHIP_SEMS
        + (pltpu.HBM(sums.shape, sums.dtype), pltpu.HBM(sums.shape, sums.dtype), jax.ShapeDtypeStruct((8, LANES), F32)),
        input_output_aliases={0: N_CHIP_SEMS, 1: N_CHIP_SEMS + 1},
        compiler_params=pltpu.CompilerParams(has_side_effects=pltpu.SideEffectType.DATAFLOW_SIDE_EFFECTING),
    )(pltpu.with_memory_space_constraint(sums, pltpu.HBM),
      pltpu.with_memory_space_constraint(lax.empty(sums.shape, sums.dtype), pltpu.HBM))
    return res[:N_CHIP_SEMS], res[N_CHIP_SEMS], res[N_CHIP_SEMS + 1], res[N_CHIP_SEMS + 2]


def _chip_exchange_wait(name, sems, sums_thru, land_thru, after):
    def body(s_ref, land_ref, *rest):
        sems, stage, local_sems = rest[:N_CHIP_SEMS], rest[-2], rest[-1]
        my_chip = 2 * lax.axis_index("x") + lax.axis_index("y")
        _copy_through_vmem([(s_ref.at[my_chip], land_ref.at[my_chip])], [stage], local_sems)
        sends, recvs = _chip_copies(s_ref, land_ref, sems[:N_CHIP - 1], sems[N_CHIP - 1:])
        for cp in sends:
            cp.wait_send()
        for cp in recvs:
            cp.wait_recv()

    return pl.pallas_call(
        body, name=name, in_specs=(_HBM, _HBM) + (_SEM,) * N_CHIP_SEMS + (pl.BlockSpec(memory_space=pl.ANY),),
        out_specs=(_HBM, _HBM),
        out_shape=(pltpu.HBM(sums_thru.shape, sums_thru.dtype), pltpu.HBM(sums_thru.shape, sums_thru.dtype)),
        input_output_aliases={0: 0, 1: 1},
        scratch_shapes=[pltpu.VMEM(sums_thru.shape[1:], sums_thru.dtype), pltpu.SemaphoreType.DMA((1,))],
        compiler_params=pltpu.CompilerParams(has_side_effects=pltpu.SideEffectType.DATAFLOW_SIDE_EFFECTING,
                                             vmem_limit_bytes=VMEM_LIMIT),
    )(sums_thru, land_thru, *sems, after)


def _adam_update(g, w, m, v):
    m_new = ADAM_B1 * m + (1.0 - ADAM_B1) * g
    v_new = ADAM_B2 * v + (1.0 - ADAM_B2) * jnp.square(g)
    m_hat = m_new / (1.0 - ADAM_B1 ** ADAM_STEP)
    v_hat = v_new / (1.0 - ADAM_B2 ** ADAM_STEP)
    return -ADAM_LR * (m_hat / (jnp.sqrt(v_hat) + ADAM_EPS) + ADAM_WD * w), m_new, v_new


def _adamw_rows(name, slots, parts):
    n_parts, rows = len(parts), [p[0].shape[0] for p in parts]
    rest = slots.shape[1] - sum(rows)

    def total(s_ref, at, r):
        g = s_ref[0, pl.ds(at, r), :]
        for j in range(1, slots.shape[0]):
            g = g + s_ref[j, pl.ds(at, r), :]
        return g

    def body(s_ref, *refs):
        ins, outs = refs[:3 * n_parts], refs[3 * n_parts:]
        at = 0
        for i, r in enumerate(rows):
            g = total(s_ref, at, r)
            w_ref, m_ref, v_ref = ins[3 * i:3 * i + 3]
            g_out, d_out, m_out, v_out = outs[4 * i:4 * i + 4]
            g_out[...] = g
            d_out[...], m_out[...], v_out[...] = _adam_update(g, w_ref[...], m_ref[...], v_ref[...])
            at += r
        outs[4 * n_parts][...] = total(s_ref, at, rest)

    shapes = [jax.ShapeDtypeStruct(p[0].shape, F32) for p in parts for _ in range(4)]
    res = pl.pallas_call(
        body, name=name, out_shape=shapes + [jax.ShapeDtypeStruct((rest, slots.shape[2]), F32)],
        compiler_params=pltpu.CompilerParams(vmem_limit_bytes=VMEM_LIMIT),
    )(slots, *[t for p in parts for t in p])
    return [res[4 * i:4 * i + 4] for i in range(n_parts)], res[4 * n_parts]


def _adamw_whole(name, items):
    n_items = len(items)

    def body(*refs):
        ins, outs = refs[:4 * n_items], refs[4 * n_items:]
        for i in range(n_items):
            s_ref, w_ref, m_ref, v_ref = ins[4 * i:4 * i + 4]
            g_out, d_out, m_out, v_out = outs[4 * i:4 * i + 4]
            g = s_ref[0].astype(F32)
            for j in range(1, s_ref.shape[0]):
                g = g + s_ref[j].astype(F32)
            g_out[0] = g
            d_out[0], m_out[0], v_out[0] = _adam_update(g, w_ref[0], m_ref[0], v_ref[0])

    res = pl.pallas_call(
        body, name=name, out_shape=[jax.ShapeDtypeStruct(item[1].shape, F32) for item in items for _ in range(4)],
        compiler_params=pltpu.CompilerParams(vmem_limit_bytes=VMEM_LIMIT),
    )(*[t for item in items for t in item])
    return [res[4 * i:4 * i + 4] for i in range(n_items)]


def _adamw(name, slots, w, m, v, tr=256):
    unit_mid = w.ndim == 3 and w.shape[1] == 1 and w.shape[0] > 1
    R, Wd = (w.shape[0], w.shape[2]) if unit_mid else w.shape[-2:]
    depth_axis = w.ndim == 3 and not unit_mid
    if unit_mid:
        tr, tc = 128, Wd
    elif R % tr == 0:
        tc = Wd
    else:
        tr, tc = R, (256 if (Wd % 256 == 0 and R > 256) else Wd)
    at = (slice(None), 0, slice(None)) if unit_mid else Ellipsis

    def body(s_ref, w_ref, m_ref, v_ref, g_out, d_out, m_out, v_out):
        g = s_ref[0].astype(F32)
        for j in range(1, slots.shape[0]):
            g = g + s_ref[j].astype(F32)
        g_out[at] = g
        d_out[at], m_out[at], v_out[at] = _adam_update(g, w_ref[at], m_ref[at], v_ref[at])

    if unit_mid:
        row = pl.BlockSpec((tr, 1, tc), lambda i, j: (i, 0, j))
    elif depth_axis:
        row = pl.BlockSpec((None, tr, tc), lambda i, j: (0, i, j))
    else:
        row = pl.BlockSpec((tr, tc), lambda i, j: (i, j))
    return pl.pallas_call(
        body, name=name, grid=(pl.cdiv(R, tr), Wd // tc),
        in_specs=[pl.BlockSpec((slots.shape[0], tr, tc), lambda i, j: (0, i, j)), row, row, row],
        out_specs=[row] * 4, out_shape=[jax.ShapeDtypeStruct(w.shape, F32)] * 4,
        compiler_params=pltpu.CompilerParams(dimension_semantics=("parallel", "parallel"),
                                             vmem_limit_bytes=VMEM_LIMIT),
    )(slots, w, m, v)


PACK_W = 1024
PACKED = [(n, s) for n, s in REPLICATED if n != "sgu_w"]
_SMALL_SIZES = [int(np.prod(s)) for _, s in PACKED]
_SMALL_ROWS = _round_up(_round_up(sum(_SMALL_SIZES) + PACK_W, PACK_W) // PACK_W, 8)
assert all(size % PACK_W == 0 for size in _SMALL_SIZES)
W_IN_SHARD = P_TOTAL // N_DEV


def _pack_rows(name, parts, rows, after=()):
    parts = [p.reshape(-1, PACK_W) for p in parts]
    assert all(p.dtype == F32 for p in parts)

    def body(*refs):
        at = 0
        for ref in refs[:len(parts)]:
            refs[-1][pl.ds(at, ref.shape[0]), :] = ref[...]
            at += ref.shape[0]
        refs[-1][pl.ds(at, rows - at), :] = jnp.zeros((rows - at, PACK_W), F32)

    return pl.pallas_call(
        body, name=name, out_shape=jax.ShapeDtypeStruct((rows, PACK_W), F32),
        in_specs=[pl.BlockSpec(memory_space=pltpu.VMEM)] * len(parts) + [pl.BlockSpec(memory_space=pl.ANY)] * len(after),
        out_specs=pl.BlockSpec(memory_space=pltpu.VMEM))(*parts, *after)


_W_IN_SEGMENTS = [(2 * D, 0, 0), (3 * D, 1, 0), (L_W, 1, 3 * D), (L_A, 1, 3 * D + 128), (L_G, 1, 3 * D + 256),
                  (2 * D, 2, 0)]
_W_IN_PADS = [(3 * D + L_W, 128 - L_W), (3 * D + 128 + L_A, 128 - L_A), (3 * D + 256 + L_G, 256 - L_G)]
_W_IN_GROUP_ROWS = [2 * D, 3 * D + 128 + 128 + 256, 2 * D]
assert sum(rows for rows, _, _ in _W_IN_SEGMENTS) == P_TOTAL


def _w_in_pieces(j):
    pieces, first = [], 0
    for rows, group, to in _W_IN_SEGMENTS:
        lo, hi = max(first, W_IN_SHARD * j), min(first + rows, W_IN_SHARD * (j + 1))
        if lo < hi:
            pieces.append((lo - W_IN_SHARD * j, hi - lo, group, to + lo - first))
        first += rows
    return pieces


_W_IN_GROUP_BLOCKS = [(min(js), max(js)) for js in
                      ([j for j in range(N_DEV) if any(p[2] == g for p in _w_in_pieces(j))] for g in range(3))]


def _w_in_groups_t(blocks):
    def body(x_ref, *refs):
        o_refs, built, sems = refs[:3], refs[3:6], refs[6]
        writes = [pltpu.make_async_copy(built[g], o_refs[g], sems.at[g]) for g in range(3)]

        @pl.when(pl.program_id(0) == 0)
        def _():
            for at, rows in _W_IN_PADS:
                built[1][pl.ds(at, rows), :] = jnp.zeros((rows, D), blocks.dtype)
        for j in range(N_DEV):
            @pl.when(pl.program_id(0) == j)
            def _(j=j):
                for at, rows, group, to in _w_in_pieces(j):
                    built[group][pl.ds(to, rows), :] = x_ref[pl.ds(at, rows), :]
                for g in range(3):
                    if j == _W_IN_GROUP_BLOCKS[g][1]:
                        writes[g].start()
                if j == N_DEV - 1:
                    for w in writes:
                        w.wait()

    return pl.pallas_call(
        body, name="w_in_groups", grid=(N_DEV,),
        in_specs=[pl.BlockSpec((None, W_IN_SHARD, D), lambda j: (j, 0, 0))],
        out_specs=[pl.BlockSpec(memory_space=pl.ANY)] * 3,
        out_shape=[jax.ShapeDtypeStruct((r, D), blocks.dtype) for r in _W_IN_GROUP_ROWS],
        scratch_shapes=[pltpu.VMEM((r, D), blocks.dtype) for r in _W_IN_GROUP_ROWS] + [pltpu.SemaphoreType.DMA((3,))],
        compiler_params=pltpu.CompilerParams(dimension_semantics=("arbitrary",), vmem_limit_bytes=VMEM_LIMIT),
    )(blocks)


def _w_in_grad_blocks(g_sgu_t, g_rw_t, g_gate_t):
    def body(*refs):
        g_refs, o_ref, here, sems = refs[:3], refs[3], refs[4:7], refs[7]
        reads = [pltpu.make_async_copy(g_refs[g], here[g], sems.at[g]) for g in range(3)]
        for j in range(N_DEV):
            @pl.when(pl.program_id(0) == j)
            def _(j=j):
                if j == 0:
                    for r in reads:
                        r.start()
                for g in range(3):
                    if j == _W_IN_GROUP_BLOCKS[g][0]:
                        reads[g].wait()
                for at, rows, group, to in _w_in_pieces(j):
                    o_ref[pl.ds(at, rows), :] = here[group][pl.ds(to, rows), :]

    return pl.pallas_call(
        body, name="w_in_grad_blocks", grid=(N_DEV,),
        in_specs=[pl.BlockSpec(memory_space=pl.ANY)] * 3,
        out_specs=pl.BlockSpec((None, W_IN_SHARD, D), lambda j: (j, 0, 0)),
        out_shape=jax.ShapeDtypeStruct((N_DEV, W_IN_SHARD, D), g_sgu_t.dtype),
        scratch_shapes=[pltpu.VMEM((r, D), g_sgu_t.dtype) for r in _W_IN_GROUP_ROWS] + [pltpu.SemaphoreType.DMA((3,))],
        compiler_params=pltpu.CompilerParams(dimension_semantics=("arbitrary",), vmem_limit_bytes=VMEM_LIMIT),
    )(g_sgu_t, g_rw_t, g_gate_t)


class TwoLevelGather:
    N_COPIES = 8
    PART_ALIGN = 16

    def __init__(self, bufs, stage_own=()):
        self.bufs, self.nb, self.stage_own = list(bufs), len(bufs), tuple(stage_own)
        self.any_specs = [pl.BlockSpec(memory_space=pl.ANY)] * self.nb
        self.out_shape = [jax.ShapeDtypeStruct((N_DEV,) + b.shape, b.dtype) for b in self.bufs]
        n = self.N_COPIES * self.nb
        self.sem_shapes = [pltpu.SemaphoreType.DMA((n,)), pltpu.SemaphoreType.DMA((n,)),
                           pltpu.SemaphoreType.DMA((self.nb,))]
        if self.stage_own:
            self.sem_shapes += [pltpu.SemaphoreType.DMA((len(self.stage_own),))]
            self.sem_shapes += [pltpu.VMEM(self.bufs[b].shape, self.bufs[b].dtype) for b in self.stage_own]

    def _parts(self, b):
        shape = self.bufs[b].shape
        first = shape[0] // 2 // self.PART_ALIGN * self.PART_ALIGN if len(shape) == 2 else 0
        return [(0, first), (first, shape[0] - first)] if first else [None]

    def _copies(self, in_refs, out_refs, sems):
        send_sems, recv_sems, local_sems = sems[:3]
        nb = self.nb
        x, y, c = lax.axis_index("x"), lax.axis_index("y"), lax.axis_index("c")
        me, sibling = (x, y, c), (x, y, 1 - c)
        near, far = [(1 - x, y), (x, 1 - y)], (1 - x, 1 - y)

        def slot(b, dev):
            return out_refs[b].at[4 * dev[0] + 2 * dev[1] + dev[2]]

        def copy(b, k, block, to, own=False, rows=None):
            src, dst = in_refs[b] if own else slot(b, block), slot(b, block)
            if rows is not None:
                src, dst = src.at[pl.ds(*rows)], dst.at[pl.ds(*rows)]
            return pltpu.make_async_remote_copy(
                src_ref=src, dst_ref=dst, send_sem=send_sems.at[self.N_COPIES * b + k],
                recv_sem=recv_sems.at[self.N_COPIES * b + k], device_id=to, device_id_type=pl.DeviceIdType.MESH)

        ways = [(j, b) for j in range(2) for b in range(nb) if j < len(self._parts(b))]
        cp = {}
        cp["local"] = [pltpu.make_async_copy(in_refs[b], slot(b, me), local_sems.at[b]) for b in range(nb)
                       if b not in self.stage_own]
        cp["first"] = [copy(b, 0, me, sibling, own=True) for b in range(nb)]
        cp["first"] += [copy(b, 1 + j, me, (*near[j], c), own=True) for j in range(2) for b in range(nb)]
        cp["from_near"] = [copy(b, 1 + j, (*near[j], c), me) for j in range(2) for b in range(nb)]
        cp["near_on"] = [[copy(b, 5 + j, (*near[j], c), sibling)]
                         + ([copy(b, 3 + j, (*near[j], c), (*near[1 - j], c), rows=self._parts(b)[j])]
                            if (j, b) in ways else []) for j in range(2) for b in range(nb)]
        cp["from_far"] = [copy(b, 3 + j, (*far, c), me, rows=self._parts(b)[j]) for j, b in ways]
        cp["far_on"] = [copy(b, 7, (*far, c), sibling) for b in range(nb)]
        cp["from_sibling"] = [copy(b, 0, sibling, me) for b in range(nb)]
        cp["from_sibling"] += [copy(b, 5 + j, (*near[j], 1 - c), me) for j in range(2) for b in range(nb)]
        cp["from_sibling"] += [copy(b, 7, (*far, 1 - c), me) for b in range(nb)]
        return cp

    def start(self, in_refs, out_refs, sems):
        cp = self._copies(in_refs, out_refs, sems)
        for c in cp["first"] + cp["local"]:
            c.start()
        if self.stage_own:
            me = 4 * lax.axis_index("x") + 2 * lax.axis_index("y") + lax.axis_index("c")
            _copy_through_vmem([(in_refs[b], out_refs[b].at[me]) for b in self.stage_own], sems[4:], sems[3])

    def pass_near(self, in_refs, out_refs, sems):
        cp = self._copies(in_refs, out_refs, sems)
        for arrived, onward in zip(cp["from_near"], cp["near_on"]):
            arrived.wait_recv()
            for c in onward:
                c.start()

    def pass_far(self, in_refs, out_refs, sems):
        cp = self._copies(in_refs, out_refs, sems)
        for c in cp["from_far"]:
            c.wait_recv()
        for c in cp["far_on"]:
            c.start()

    def finish(self, in_refs, out_refs, sems):
        cp = self._copies(in_refs, out_refs, sems)
        for c in cp["from_sibling"]:
            c.wait_recv()
        for c in cp["first"] + sum(cp["near_on"], []) + cp["far_on"]:
            c.wait_send()
        for c in cp["local"]:
            c.wait()

    def schedule(self, n_steps):
        return [(0, self.start), (max(n_steps // 2 - 1, 0), self.pass_near), (max(n_steps - 3, 0), self.pass_far),
                (n_steps - 1, self.finish)]


def _all_gather_two_level(name, bufs, stage_own=()):
    ex = TwoLevelGather(bufs, stage_own)

    def body(*refs):
        args = refs[:ex.nb], refs[ex.nb:2 * ex.nb], refs[2 * ex.nb:]
        for _, action in ex.schedule(1):
            action(*args)

    return pl.pallas_call(body, name=name, in_specs=ex.any_specs, out_specs=ex.any_specs, out_shape=ex.out_shape,
                          scratch_shapes=ex.sem_shapes)(*ex.bufs)


def _cols_from_blocks(blk):
    return jnp.transpose(blk, (1, 0, 2)).reshape(blk.shape[1], -1)


def _cols_to_blocks(g):
    r, c = g.shape
    return jnp.transpose(g.reshape(r, N_DEV, c // N_DEV), (1, 0, 2))


FIRST_WEIGHTS = ["w_in", "shift_b", "w_lora_w", "a_lora_w", "g_lora_w"]
LATE_WEIGHTS = ["w_proj_a", "w_proj_b", "w_out", "w_ffn1", "w_ffn2"]
SCAN_CARRIED = ["w_proj_a", "w_proj_b", "w_out"]
FFN_WEIGHTS = ["w_ffn1", "w_ffn2"]


def _late_weights(shards):
    ex = TwoLevelGather([shards[n][0].astype(BF16) for n in LATE_WEIGHTS])

    def finish(results):
        got = dict(zip(LATE_WEIGHTS, results))
        W = {n: got[n].reshape(-1, D) for n in ("w_proj_a", "w_proj_b", "w_out", "w_ffn2")}
        W["w_ffn1"] = got["w_ffn1"].reshape(N_DEV, D, -1)
        return W
    return ex, finish


def _gather_weights(shards):
    def payload(n):
        if n == "w_in":
            return jnp.transpose(shards[n][0]).astype(BF16)
        return shards[n] if n == "shift_b" else shards[n].astype(BF16)
    payloads = [payload(n) for n in FIRST_WEIGHTS]
    got = dict(zip(FIRST_WEIGHTS, _all_gather_two_level("weight_all_gather", payloads, stage_own=(0,))))
    W = {}
    W["w_sgu_t"], W["w_rw_t"], W["w_gate_t"] = _w_in_groups_t(got["w_in"])
    z = lambda r, c, dt: jnp.zeros((r, c), dt)
    W["w_lora"] = jnp.concatenate([_cols_from_blocks(got["w_lora_w"][:, 0]).astype(F32), z(128 - L_W, D, F32)], axis=0)
    W["a_lora"] = jnp.concatenate([_cols_from_blocks(got["a_lora_w"][:, 0]).astype(F32), z(128 - L_A, D, F32)], axis=0)
    W["g_lora"] = jnp.concatenate([_cols_from_blocks(got["g_lora_w"][:, 0]).astype(F32), z(256 - L_G, D, F32)], axis=0)
    sb = _cols_from_blocks(got["shift_b"][:, 0])
    W["sb"] = jnp.concatenate([sb[:, :3 * D], sb[:, 3 * D:3 * D + L_W], z(2, 128 - L_W, F32),
                               sb[:, 3 * D + L_W:3 * D + L_W + L_A], z(2, 128 - L_A, F32),
                               sb[:, 3 * D + L_W + L_A:], z(2, 256 - L_G, F32)], axis=1)
    return W


def _replicated_weights(rep):
    W = {n: rep[n] for n in ("g_mix", "sgu_ln_w", "sgu_ln_b", "w0", "a0", "k_k", "k_a", "r_k", "ln_x_w", "ln_x_b",
                             "g_ffn")}
    W["g_final"] = rep["g_final"].reshape(1, D)
    W["sgu_w"] = rep["sgu_w"][0]
    W["sgu_bt"] = jnp.transpose(rep["sgu_b"][0])
    return W


def _late_grad_blocks(G):
    return Exchange([G[n].reshape(N_DEV, -1, D) for n in SCAN_CARRIED]
                    + [G["sgu_w"].reshape(SGU_G * SGU_C, SGU_C).astype(GRAD_PAYLOAD)],
                    [False] * len(SCAN_CARRIED) + [True])


def _first_grad_blocks(G):
    sbg = G["sb"]
    c = 3 * D
    sb = jnp.concatenate([sbg[:, :c], sbg[:, c:c + L_W], sbg[:, c + 128:c + 128 + L_A],
                          sbg[:, c + 256:c + 256 + L_G]], axis=1)
    return {
        "shift_b": _cols_to_blocks(sb),
        "w_lora_w": _cols_to_blocks(G["w_lora"][:L_W]), "a_lora_w": _cols_to_blocks(G["a_lora"][:L_A]),
        "g_lora_w": _cols_to_blocks(G["g_lora"][:L_G]),
    }


def _replicated_grads(G):
    small = {n: G[n] for n in ("g_mix", "sgu_ln_w", "sgu_ln_b", "w0", "a0", "k_k", "k_a", "r_k", "ln_x_w", "ln_x_b",
                               "g_ffn", "g_final")}
    small["sgu_w"] = G["sgu_w"]
    small["sgu_b"] = jnp.transpose(G["sgu_bt"])
    return small


def kernel(x, g_mix, w_in, sgu_ln_w, sgu_ln_b, sgu_w, sgu_b, w_proj_a, shift_b, w_lora_w, w0, a_lora_w, a0, g_lora_w, k_k, k_a, r_k, ln_x_w, ln_x_b, w_proj_b, w_out, g_ffn, w_ffn1, w_ffn2, g_final, loss_target, m_g_mix, m_w_in, m_sgu_ln_w, m_sgu_ln_b, m_sgu_w, m_sgu_b, m_w_proj_a, m_shift_b, m_w_lora_w, m_w0, m_a_lora_w, m_a0, m_g_lora_w, m_k_k, m_k_a, m_r_k, m_ln_x_w, m_ln_x_b, m_w_proj_b, m_w_out, m_g_ffn, m_w_ffn1, m_w_ffn2, m_g_final, v_g_mix, v_w_in, v_sgu_ln_w, v_sgu_ln_b, v_sgu_w, v_sgu_b, v_w_proj_a, v_shift_b, v_w_lora_w, v_w0, v_a_lora_w, v_a0, v_g_lora_w, v_k_k, v_k_a, v_r_k, v_ln_x_w, v_ln_x_b, v_w_proj_b, v_w_out, v_g_ffn, v_w_ffn1, v_w_ffn2, v_g_final):
    env = dict(locals())
    weights = {n: env[n] for n in WEIGHT_ORDER}
    moms = {n: env["m_" + n] for n in WEIGHT_ORDER}
    vars_ = {n: env["v_" + n] for n in WEIGHT_ORDER}

    shards = {n: weights[n] for n, _, _ in SHARDED}
    W = _gather_weights(shards)
    W.update(_replicated_weights({n: weights[n] for n, _ in REPLICATED}))
    in_flight = {}

    def send_w_in_grads(G):
        blocks = _w_in_grad_blocks(G["w_sgu_t"], G["w_rw_t"], G["w_gate_t"])
        got = _pair_exchange("grad_pair_exchange", blocks)
        core = lax.axis_index("c").astype(jnp.int32).reshape(1)
        sums = _pair_sum("grad_pair_sum", blocks, got, core)
        in_flight["sems"], in_flight["sums"], in_flight["land"], token = _chip_exchange_start("grad_chip_start", sums)
        return token

    def send_ffn_grads(G):
        in_flight["ffn"] = _scatter_start("grad_ffn_start", [G["w_ffn1"], G["w_ffn2"].reshape(N_DEV, -1, D)])
        return in_flight["ffn"][3]

    loss_part, dx, G, late_slots = _local_step(x[0], loss_target[0], W, late_weights=_late_weights(shards),
                                               early_grads=_late_grad_blocks, w_in_grads_ready=send_w_in_grads,
                                               ffn_grads_ready=send_ffn_grads)

    slots = dict(zip(SCAN_CARRIED, late_slots))
    _, landed = _scatter_wait("grad_ffn_wait", *in_flight["ffn"][:3], after=dx)
    slots.update(zip(FFN_WEIGHTS, landed))
    outs = [dict(), dict(), dict(), dict()]

    def keep(n, res):
        for k in range(4):
            outs[k][n] = res[k]

    def update_group(name, group):
        group_out = _adamw_whole(name, [(slots[n], weights[n], moms[n], vars_[n]) for n in group])
        for n, res in zip(group, group_out):
            keep(n, res)

    for n in FFN_WEIGHTS:
        keep(n, _adamw("adamw_" + n, slots[n], weights[n], moms[n], vars_[n]))
    update_group("adamw_projections", SCAN_CARRIED)
    sgu_shape = (SGU_G * SGU_C, SGU_C)
    res = _adamw("adamw_sgu_w", late_slots[len(SCAN_CARRIED)], *[t.reshape(sgu_shape) for t in
                                                                  (weights["sgu_w"], moms["sgu_w"], vars_["sgu_w"])])
    keep("sgu_w", [t.reshape(weights["sgu_w"].shape) for t in res])
    updated = [outs[1][n] for n in FFN_WEIGHTS + SCAN_CARRIED + ["sgu_w"]]

    blocks = _first_grad_blocks(G)
    small = _replicated_grads(G)
    small_parts = [small[n] for n, _ in PACKED] + [jnp.full((PACK_W,), loss_part, F32)]
    rest = [n for n in FIRST_WEIGHTS if n != "w_in"]
    packed = _pack_rows("grad_pack", small_parts, _SMALL_ROWS, after=updated)
    res = _exchange("grad_exchange", [blocks[n] for n in rest] + [packed], [False] * len(rest) + [True])
    slots.update(zip(rest, res[:-1]))
    small_slots = res[-1]
    _, slots["w_in"] = _chip_exchange_wait("grad_chip_wait", in_flight["sems"], in_flight["sums"], in_flight["land"],
                                           after=small_slots)

    res = _adamw("adamw_w_in", slots["w_in"], *[jnp.transpose(t, (2, 0, 1)) for t in (weights["w_in"], moms["w_in"], vars_["w_in"])])
    keep("w_in", [jnp.transpose(t, (1, 2, 0)) for t in res])
    update_group("adamw_low_rank", rest)

    small_out, after_them = _adamw_rows(
        "adamw_replicated", small_slots,
        [[d[n].reshape(-1, PACK_W) for d in (weights, moms, vars_)] for n, _ in PACKED])
    for (n, s), res in zip(PACKED, small_out):
        for k in range(4):
            outs[k][n] = res[k].reshape(s)
    loss = after_them[0, 0]
    return (loss, dx[None], *[outs[0][n] for n in WEIGHT_ORDER], *[outs[1][n] for n in WEIGHT_ORDER],
            *[outs[2][n] for n in WEIGHT_ORDER], *[outs[3][n] for n in WEIGHT_ORDER])
```

```python
import functools
import numpy as np
import jax
import jax.numpy as jnp
from jax import lax
from jax.experimental import pallas as pl
from jax.experimental.pallas import tpu as pltpu

F32 = jnp.float32
BF16 = jnp.bfloat16

D = 1024
NH, HN = 16, 64
NP, PW = NH // 2, 2 * HN
SGU_G, SGU_C = 8, 128
L_W, L_A, L_G = 64, 64, 160
C_B = 3 * D + L_W + L_A + L_G
P_TOTAL = 2 * D + C_B + 2 * D
D_FF = 4 * D
RW_INT = 3 * D + 128 + 128 + 256
NORM_EPS, LN_EPS, GN_EPS = 1e-6, 1e-5, 64e-5
N_DEV = 8
LANES = 128
SCAN_C = 64
N_KEPT = 4
SOLVE_B = 16
SCAN_PRECISION = lax.Precision.HIGH
SCAN_OUT_PRECISION = lax.Precision.DEFAULT
GRAD_PAYLOAD = BF16
VMEM_LIMIT = 56 * 1024 * 1024
MATMUL_VMEM_BUDGET = 40 * 1024 * 1024
STEP_COST_BYTES = 512 * 1024
HBM_COST_RATIO = 3
ACC_PASS_WEIGHT = 4

ADAM_LR, ADAM_B1, ADAM_B2, ADAM_EPS, ADAM_WD, ADAM_STEP = 0.001, 0.9, 0.999, 1e-08, 0.01, 10

SHARDED = [
    ("w_in", (D, P_TOTAL), 1), ("w_proj_a", (D, D), 0), ("shift_b", (2, C_B), 1), ("w_lora_w", (L_W, D), 1),
    ("a_lora_w", (L_A, D), 1), ("g_lora_w", (L_G, D), 1), ("w_proj_b", (D, D), 0), ("w_out", (D, D), 0),
    ("w_ffn1", (D, D_FF), 1), ("w_ffn2", (D_FF, D), 0),
]
REPLICATED = [
    ("g_mix", (1, D)), ("sgu_ln_w", (1, D)), ("sgu_ln_b", (1, D)), ("sgu_w", (1, SGU_G, SGU_C, SGU_C)),
    ("sgu_b", (1, SGU_G, SGU_C)), ("w0", (1, D)), ("a0", (1, D)), ("k_k", (1, D)), ("k_a", (1, D)), ("r_k", (1, D)),
    ("ln_x_w", (1, D)), ("ln_x_b", (1, D)), ("g_ffn", (1, D)), ("g_final", (D,)),
]
WEIGHT_ORDER = ["g_mix", "w_in", "sgu_ln_w", "sgu_ln_b", "sgu_w", "sgu_b", "w_proj_a", "shift_b", "w_lora_w", "w0",
                "a_lora_w", "a0", "g_lora_w", "k_k", "k_a", "r_k", "ln_x_w", "ln_x_b", "w_proj_b", "w_out", "g_ffn",
                "w_ffn1", "w_ffn2", "g_final"]


def _round_up(n, m):
    return (n + m - 1) // m * m


def _pick(n, target):
    if n <= target:
        return n
    best = None
    for t in range(LANES, target + 1, LANES):
        if n % t == 0:
            best = t
    assert best is not None, (n, target)
    return best


def _matmul(name, a, b, mode, out_dtype=F32, tm=2048, tn=1024, tk=4096, out_blocks=None, epilogue=None, extras=(),
            out_dtypes=(), after=None, whole_rows=False, out_widths=None):
    b_blocks = b.shape[0] if b.ndim == 3 else None
    bshape = b.shape if b.ndim == 2 else (b.shape[1], b.shape[0] * b.shape[2])
    if mode == "nn":
        (M, K), (K2, N) = a.shape, bshape
    elif mode == "nt":
        (M, K), (N, K2) = a.shape, bshape
    else:
        (K, M), (K2, N) = a.shape, bshape
    assert K == K2, (name, a.shape, b.shape)
    assert b_blocks is None or mode != "tn"
    assert out_blocks is None or mode == "tn"
    tn = min(tn, N // (out_blocks or 1), bshape[1] // b_blocks if (b_blocks and mode == "nn") else tn)
    blocked_k = bool(b_blocks) and mode == "nt"
    tm, tn, tk = _pick(M, tm), _pick(N, tn), (K if blocked_k else _pick(K, tk))

    def vmem_bytes(tm, tk):
        tiles = tm * tk * a.dtype.itemsize + tk * tn * b.dtype.itemsize
        for i, dt in enumerate(out_dtypes if epilogue else (out_dtype,)):
            tiles += tm * (out_widths[i] if out_widths else tn) * jnp.dtype(dt).itemsize
        for x in extras:
            arr = x[0] if isinstance(x, tuple) else x
            tiles += (tm if arr.shape[0] > 1 else 1) * tn * arr.dtype.itemsize
        return 2 * tiles + (tm * tn * 4 if K // tk > 1 else 0)

    def cost(tm, tk):
        ni, nj, nk = M // tm, N // tn, K // tk
        steps = ni * nj * nk
        acc_passes = steps * tm * tn * 8 * ACC_PASS_WEIGHT if nk > 1 else 0
        a_reads = M * K * a.dtype.itemsize * (nj if nk > 1 else 1)
        b_reads = K * N * b.dtype.itemsize * (ni if (nj > 1 or nk > 1) else 1)
        return (steps * STEP_COST_BYTES + acc_passes + vmem_bytes(tm, tk) // 2
                + HBM_COST_RATIO * (a_reads + b_reads))

    options = [(m, k) for m in ({M} if whole_rows else {_pick(M, max(t, LANES)) for t in (tm, tm // 2, tm // 4)})
               for k in ({K} if blocked_k else {_pick(K, max(t, LANES)) for t in (tk, tk // 2, tk // 4)})
               if vmem_bytes(m, k) <= MATMUL_VMEM_BUDGET]
    tm, tk = min(options, key=lambda o: cost(*o))
    nk = K // tk
    dims = {"nn": (((1,), (0,)), ((), ())), "nt": (((1,), (1,)), ((), ())), "tn": (((0,), (0,)), ((), ()))}[mode]

    n_x, n_o = len(extras), len(out_dtypes) if epilogue else 1
    n_after = 0 if after is None else 1

    def body(a_ref, b_ref, *rest):
        x_refs, o_refs, acc = rest[:n_x], rest[n_x + n_after:n_x + n_after + n_o], rest[n_x + n_after + n_o:]
        if blocked_k:
            bw = b.shape[2]
            part = sum(lax.dot_general(a_ref[:, blk * bw:(blk + 1) * bw].astype(BF16), b_ref[blk].astype(BF16), dims,
                                       preferred_element_type=F32) for blk in range(b_blocks))
        else:
            part = lax.dot_general(a_ref[...].astype(BF16), b_ref[...].astype(BF16), dims, preferred_element_type=F32)

        def finish(res):
            outs = epilogue(res, *[r[...] for r in x_refs]) if epilogue else (res,)
            for r, v in zip(o_refs, outs):
                r[...] = v.astype(r.dtype)

        if nk == 1:
            finish(part)
            return
        acc_ref, k = acc[0], pl.program_id(2)

        @pl.when(k == 0)
        def _():
            acc_ref[...] = part

        @pl.when(k > 0)
        def _():
            acc_ref[...] += part

        @pl.when(k == nk - 1)
        def _():
            finish(acc_ref[...])

    a_spec = {"nn": pl.BlockSpec((tm, tk), lambda i, j, k: (i, k)), "nt": pl.BlockSpec((tm, tk), lambda i, j, k: (i, k)),
              "tn": pl.BlockSpec((tk, tm), lambda i, j, k: (k, i))}[mode]
    b_spec = {"nn": pl.BlockSpec((tk, tn), lambda i, j, k: (k, j)), "nt": pl.BlockSpec((tn, tk), lambda i, j, k: (j, k)),
              "tn": pl.BlockSpec((tk, tn), lambda i, j, k: (k, j))}[mode]
    if b_blocks and mode == "nn":
        per = b.shape[2] // tn
        b_spec = pl.BlockSpec((None, tk, tn), lambda i, j, k: (j // per, k, j % per))
    elif b_blocks:
        b_spec = pl.BlockSpec((b_blocks, tn, b.shape[2]), lambda i, j, k: (0, j, 0))
    out_spec = pl.BlockSpec((tm, tn), lambda i, j, k: (i, j))
    out_shape = jax.ShapeDtypeStruct((M, N), out_dtype)
    if out_blocks:
        per_o = N // out_blocks // tn
        out_spec = pl.BlockSpec((None, tm, tn), lambda i, j, k: (j // per_o, i, j % per_o))
        out_shape = jax.ShapeDtypeStruct((out_blocks, M, N // out_blocks), out_dtype)
    epi_widths = list(out_widths) if out_widths else [N] * n_o
    assert all(w == N for w in epi_widths) or N == tn
    epi_specs = [pl.BlockSpec((tm, tn if w == N else w), lambda i, j, k: (i, j)) for w in epi_widths]
    x_specs, x_args = [], []
    for x in extras:
        arr, off = x if isinstance(x, tuple) else (x, 0)
        if arr.shape[0] == 1:
            x_specs.append(pl.BlockSpec((1, tn), lambda i, j, k: (0, j)))
        else:
            x_specs.append(pl.BlockSpec((tm, tn), lambda i, j, k, off=off: (i, j + off)))
        x_args.append(arr)
    res = pl.pallas_call(
        body, name=name, grid=(M // tm, N // tn, nk),
        in_specs=[a_spec, b_spec] + x_specs + [pl.BlockSpec(memory_space=pl.ANY)] * n_after,
        out_specs=epi_specs if epilogue else out_spec,
        out_shape=[jax.ShapeDtypeStruct((M, w), dt) for w, dt in zip(epi_widths, out_dtypes)] if epilogue else out_shape,
        scratch_shapes=[pltpu.VMEM((tm, tn), F32)] if nk > 1 else [],
        compiler_params=pltpu.CompilerParams(dimension_semantics=("parallel", "parallel", "arbitrary"),
                                             vmem_limit_bytes=VMEM_LIMIT),
    )(a, b, *x_args, *([after] if n_after else []))
    return res


class Rows:
    def __init__(self, arr, width=None, cb=0):
        self.arr, self.width, self.cb = arr, (arr.shape[1] if width is None else width), cb


class Heads:
    def __init__(self, arr):
        self.arr = arr


class Halo:
    def __init__(self, arr, side):
        self.arr, self.side = arr, side


def _rows_call(name, fn, ins, consts, outs, accs=(), tm=512, with_pid=False):
    T = next(o.arr.shape[1] if isinstance(o, Heads) else o.arr.shape[0] for o in ins if not isinstance(o, Halo))
    tm = min(tm, T)
    n_tiles = T // tm
    n_in, n_c, n_out = len(ins), len(consts), len(outs)
    in_specs, args = [], []
    for o in ins:
        if isinstance(o, Rows):
            in_specs.append(pl.BlockSpec((tm, o.width), lambda i, cb=o.cb: (i, cb)))
        elif isinstance(o, Heads):
            in_specs.append(pl.BlockSpec((NP, tm, PW), lambda i: (0, i, 0)))
        else:
            w = o.arr.shape[1]
            if o.side < 0:
                in_specs.append(pl.BlockSpec((8, w), lambda i: (jnp.maximum(i * (tm // 8) - 1, 0), 0)))
            else:
                in_specs.append(pl.BlockSpec((8, w), lambda i: (jnp.minimum((i + 1) * (tm // 8), T // 8 - 1), 0)))
        args.append(o.arr)
    for c in consts:
        in_specs.append(pl.BlockSpec(c.shape, lambda i, nd=c.ndim: (0,) * nd))
        args.append(c)
    out_specs, out_shape = [], []
    for o in outs:
        if o[0] == "rows":
            out_specs.append(pl.BlockSpec((tm, o[1]), lambda i: (i, 0)))
            out_shape.append(jax.ShapeDtypeStruct((T, o[1]), o[2]))
        else:
            out_specs.append(pl.BlockSpec((NP, tm, PW), lambda i: (0, i, 0)))
            out_shape.append(jax.ShapeDtypeStruct((NP, T, PW), o[1]))
    for shape, dt in accs:
        out_specs.append(pl.BlockSpec(shape, lambda i, nd=len(shape): (0,) * nd))
        out_shape.append(jax.ShapeDtypeStruct(shape, dt))

    def body(*refs):
        i = pl.program_id(0)
        vals = []
        vals = [r[...] for r in refs[:n_in + n_c]]
        res = fn(i, n_tiles, *vals) if with_pid else fn(*vals)
        out_refs = refs[n_in + n_c:]
        for r, v in zip(out_refs[:n_out], res[:n_out]):
            r[...] = v.astype(r.dtype)
        if accs:
            @pl.when(i == 0)
            def _():
                for r in out_refs[n_out:]:
                    r[...] = jnp.zeros_like(r)

            for r, v in zip(out_refs[n_out:], res[n_out:]):
                r[...] += v.astype(r.dtype)

    res = pl.pallas_call(
        body, name=name, grid=(n_tiles,), in_specs=in_specs, out_specs=out_specs, out_shape=out_shape,
        compiler_params=pltpu.CompilerParams(dimension_semantics=("arbitrary",), vmem_limit_bytes=VMEM_LIMIT),
    )(*args)
    return res


def _rms(x, g):
    return x * lax.rsqrt(jnp.mean(x * x, axis=-1, keepdims=True) + NORM_EPS) * g


def _gelu(x):
    return 0.5 * x * (1.0 + lax.erf(x * 0.7071067811865476))


def _sigmoid(x):
    return 1.0 / (1.0 + jnp.exp(-x))


def _bdot(a, b):
    return jnp.dot(a.astype(BF16), b.astype(BF16), preferred_element_type=F32)


def _to_heads(x):
    return jnp.concatenate([x[:, p * PW:(p + 1) * PW][None] for p in range(NP)], axis=0)


def _from_heads(xp):
    return jnp.concatenate([xp[p] for p in range(NP)], axis=-1)


def _head_sum(xp):
    low = lax.broadcasted_iota(jnp.int32, xp.shape, xp.ndim - 1) < HN
    both = jnp.sum(xp, axis=-1, keepdims=True)
    first = jnp.sum(jnp.where(low, xp, 0.0), axis=-1, keepdims=True)
    return jnp.where(low, first, both - first)


def _split_pairs(xp):
    return jnp.concatenate([xp[:, :, :HN], xp[:, :, HN:]], axis=0)


def _join_pairs(xh):
    return jnp.concatenate([xh[:NP], xh[NP:]], axis=-1)


def _sgu_fn(p, ln_w, ln_b, sw, sbt):
    z = _gelu(p)
    u, v = z[:, :D], z[:, D:]
    mu = jnp.mean(v, axis=-1, keepdims=True)
    var = jnp.mean(jnp.square(v - mu), axis=-1, keepdims=True)
    vn = (v - mu) * lax.rsqrt(var + LN_EPS) * ln_w + ln_b
    ri = lax.broadcasted_iota(jnp.int32, (SGU_C, SGU_C), 0)
    ci = lax.broadcasted_iota(jnp.int32, (SGU_C, SGU_C), 1)
    mask = (ci <= ri).astype(F32)
    dg = D // SGU_G
    parts = []
    for g in range(SGU_G):
        parts.append(_bdot(sw[g] * mask, vn[:, g * dg:(g + 1) * dg]) + sbt[:, g:g + 1])
    return u * jnp.concatenate(parts, axis=-1)


def _pre_fn(qr, qk, qv, qxw, qxa, qxg, wl, w0, al, a0, gl, k_k, k_a):
    w = -jax.nn.softplus(-(w0 + _bdot(jnp.tanh(qxw), wl))) - 0.5
    lw = -jnp.exp(w)
    aa = _sigmoid(a0 + _bdot(qxa, al))
    g = _bdot(_sigmoid(qxg), gl)
    kk = _to_heads(qk * k_k)
    kk = kk / jnp.maximum(jnp.sqrt(_head_sum(kk * kk)), 1e-12)
    k2 = qk * (1.0 + (aa - 1.0) * k_a)
    return _to_heads(qr), _to_heads(lw), _to_heads(k2), _to_heads(qv), kk, _to_heads(aa), g


def _post_fn(o, r, k2, v, g, ln_w, ln_b, r_k):
    mu = _head_sum(o) * (1.0 / HN)
    d = o - mu
    var = _head_sum(d * d) * (1.0 / HN)
    on = d * lax.rsqrt(var + GN_EPS) * ln_w + ln_b
    bonus = _head_sum(r * k2 * r_k) * v
    return _from_heads(on + bonus) * g


def _gate_fn(pg, ya, yb):
    return _sigmoid(pg[:, :D]) * ya + _sigmoid(pg[:, D:]) * yb


def _bmm(x, y, cx, cy, out_path=False):
    return lax.dot_general(x, y, (((cx,), (cy,)), ((0,), (0,))),
                           precision=SCAN_OUT_PRECISION if out_path else SCAN_PRECISION, preferred_element_type=F32)


def _unit_lower_inverse(M):
    C = M.shape[1]
    ti = lax.broadcasted_iota(jnp.int32, (C, C), 0)
    tj = lax.broadcasted_iota(jnp.int32, (C, C), 1)
    eye = (ti == tj).astype(F32)
    same = lambda b: (ti // b == tj // b).astype(F32)
    X = -(M * same(SOLVE_B))
    inv = eye + X
    span = 1
    while 2 * span < SOLVE_B:
        X = _bmm(X, X, 2, 1)
        inv = inv + _bmm(inv, X, 2, 1)
        span *= 2
    b = SOLVE_B
    while b < C:
        low = M * (same(2 * b) - same(b))
        inv = inv - _bmm(_bmm(inv, low, 2, 1, out_path=True), inv, 2, 1, out_path=True)
        b *= 2
    return inv


@jax.custom_vjp
def _unit_lower_solve(inv, M, y):
    return _bmm(inv, y, 2, 1)


def _unit_lower_solve_fwd(inv, M, y):
    u = _bmm(inv, y, 2, 1)
    return u, (inv, u)


def _unit_lower_solve_bwd(res, du):
    inv, u = res
    dy = _bmm(inv, du, 1, 1)
    return jnp.zeros_like(inv), -_bmm(dy, u, 2, 2), dy


_unit_lower_solve.defvjp(_unit_lower_solve_fwd, _unit_lower_solve_bwd)


@jax.custom_vjp
def _unit_lower_solved(inv, u, M, y):
    return u


_unit_lower_solved.defvjp(lambda inv, u, M, y: (u, (inv, u)),
                          lambda res, du: (jnp.zeros_like(res[0]), jnp.zeros_like(res[1]))
                          + _unit_lower_solve_bwd(res, du)[1:])


@functools.partial(jax.custom_vjp, nondiff_argnums=(0,))
def _kept(fn, value, *args):
    return value


def _kept_fwd(fn, value, *args):
    return value, args


def _kept_bwd(fn, args, d):
    _, vjp = jax.vjp(fn, *args)
    return (jnp.zeros_like(d),) + tuple(vjp(d))


_kept.defvjp(_kept_fwd, _kept_bwd)


def _sum_over_time(x, reverse):
    C = x.shape[1]
    ti = lax.broadcasted_iota(jnp.int32, (C, C), 0)
    tj = lax.broadcasted_iota(jnp.int32, (C, C), 1)
    ones = jnp.broadcast_to(((tj >= ti) if reverse else (tj <= ti)).astype(BF16), (x.shape[0], C, C))
    hi = x.astype(BF16)
    r1 = x - hi.astype(F32)
    mid = r1.astype(BF16)
    lo = (r1 - mid.astype(F32)).astype(BF16)
    dn = (((2,), (1,)), ((0,), (0,)))
    return sum(lax.dot_general(ones, p, dn, preferred_element_type=F32) for p in (lo, mid, hi))


@jax.custom_vjp
def _time_cumsum(lw):
    return _sum_over_time(lw, reverse=False)


_time_cumsum.defvjp(lambda lw: (_sum_over_time(lw, reverse=False), None),
                    lambda _, d: (_sum_over_time(d, reverse=True),))


def _chunk_fn(S0, r, lw, k, v, kk, a, kept=None):
    C = SCAN_C
    bmm = _bmm
    ti = lax.broadcasted_iota(jnp.int32, (C, C), 0)
    tj = lax.broadcasted_iota(jnp.int32, (C, C), 1)
    incl2 = jnp.concatenate([(tj <= ti).astype(F32)] * 2, axis=1)
    strict = (tj < ti).astype(F32)
    n_mask = jnp.concatenate([jnp.zeros((C, C), F32), strict], axis=1)

    def known(name, fn, *args):
        return fn(*args) if kept is None else _kept(fn, kept[name], *args)

    cum = known("cum", _time_cumsum, lw)
    g_in, g_ex, g_inv = jnp.exp(cum), jnp.exp(cum - lw), jnp.exp(-cum)
    kkt, rt = kk * g_ex, r * g_in
    bk = jnp.concatenate([kk * a * g_inv, k * g_inv], axis=1)
    kr = jnp.concatenate([kkt, rt], axis=1)
    ratios = known("ratios", lambda x, y: bmm(x, y, 2, 2), kr, bk)
    A = ratios[:, :C]
    M = A[:, :, :C] * strict
    zv = jnp.concatenate([jnp.zeros_like(v), v], axis=1)
    s0_side = bmm(kr, S0, 2, 2, out_path=True)
    rhs = s0_side[:, :C] + bmm(A * n_mask, zv, 2, 1, out_path=True)
    if kept is None:
        inv = lax.stop_gradient(_unit_lower_inverse(M))
        y = _unit_lower_solve(inv, M, rhs)
    else:
        inv = kept["inv"]
        y = _unit_lower_solved(inv, kept["y"], M, rhs)
    z = jnp.concatenate([-y, v], axis=1)
    O = s0_side[:, C:] + bmm(ratios[:, C:] * incl2, z, 2, 1, out_path=True)
    g_end = g_in[:, C - 1:C, :]
    S1 = S0 * g_end + bmm(z, bk * g_end, 1, 1, out_path=True)
    return O, S1, dict(cum=cum, ratios=ratios, y=y, inv=inv)


def _scan_fwd(r, lw, k, v, kk, a, ex=None, tb=256):
    assert SCAN_C == HN and 2 * SCAN_C == PW
    T = r.shape[1]
    tb = min(tb, T)
    n_chunks = tb // SCAN_C
    nb = T // tb
    nx = ex.nb if ex else 0

    def body(*refs):
        r_ref, lw_ref, k_ref, v_ref, kk_ref, a_ref = refs[:6]
        x_in, (o_ref, s0_ref), x_out = refs[6:6 + nx], refs[6 + nx:8 + nx], refs[8 + nx:8 + 2 * nx]
        s_ref, sems = refs[8 + 2 * nx], refs[9 + 2 * nx:]

        plan = ex.schedule(nb) if ex else []

        @pl.when(pl.program_id(0) == 0)
        def _():
            s_ref[...] = jnp.zeros_like(s_ref)
            for at, action in plan[:1]:
                action(x_in, x_out, sems)

        def step(c, carry):
            sl = pl.ds(pl.multiple_of(c * SCAN_C, SCAN_C), SCAN_C)
            S0 = s_ref[...]
            O, S1, keep = _chunk_fn(S0, *[_split_pairs(ref[:, sl, :])
                                          for ref in (r_ref, lw_ref, k_ref, v_ref, kk_ref, a_ref)])
            o_ref[:, sl, :] = _join_pairs(O)
            s0_ref[c, 0] = jnp.concatenate([S0, keep["inv"]], axis=-1)
            s0_ref[c, 1] = jnp.concatenate([keep["cum"], keep["y"]], axis=-1)
            s0_ref[c, 2] = keep["ratios"][:, :SCAN_C]
            s0_ref[c, 3] = keep["ratios"][:, SCAN_C:]
            s_ref[...] = S1
            return carry

        lax.fori_loop(0, n_chunks, step, 0)

        for at, action in plan[1:]:
            pl.when(pl.program_id(0) == at)(functools.partial(action, x_in, x_out, sems))

    hm = pl.BlockSpec((NP, tb, PW), lambda i: (0, i, 0))
    res = pl.pallas_call(
        body, name="rwkv_scan_fwd", grid=(nb,), in_specs=[hm] * 6 + (ex.any_specs if ex else []),
        out_specs=[hm, pl.BlockSpec((n_chunks, N_KEPT, NH, HN, PW), lambda i: (i, 0, 0, 0, 0))]
        + (ex.any_specs if ex else []),
        out_shape=[jax.ShapeDtypeStruct((NP, T, PW), F32),
                   jax.ShapeDtypeStruct((T // SCAN_C, N_KEPT, NH, HN, PW), F32)]
        + (ex.out_shape if ex else []),
        scratch_shapes=[pltpu.VMEM((NH, HN, HN), F32)] + (ex.sem_shapes if ex else []),
        compiler_params=pltpu.CompilerParams(dimension_semantics=("arbitrary",), vmem_limit_bytes=VMEM_LIMIT),
    )(r, lw, k, v, kk, a, *(ex.bufs if ex else []))
    return res[0], res[1], list(res[2:])


def _scan_bwd(r, lw, k, v, kk, a, s0s, do, ex=None, tb=128):
    T = r.shape[1]
    tb = min(tb, T)
    n_chunks = tb // SCAN_C
    nb = T // tb
    nx = ex.nb if ex else 0

    def body(*refs):
        r_ref, lw_ref, k_ref, v_ref, kk_ref, a_ref, s0_ref, do_ref = refs[:8]
        x_in, (dr, dlw, dk, dv, dkk, da), x_out = refs[8:8 + nx], refs[8 + nx:14 + nx], refs[14 + nx:14 + 2 * nx]
        ds_ref, sems = refs[14 + 2 * nx], refs[15 + 2 * nx:]

        plan = ex.schedule(nb) if ex else []

        @pl.when(pl.program_id(0) == 0)
        def _():
            ds_ref[...] = jnp.zeros_like(ds_ref)
            for at, action in plan[:1]:
                action(x_in, x_out, sems)

        def step(j, carry):
            c = n_chunks - 1 - j
            sl = pl.ds(pl.multiple_of(c * SCAN_C, SCAN_C), SCAN_C)
            s0_inv, cum_y = s0_ref[c, 0], s0_ref[c, 1]
            kept = dict(inv=s0_inv[:, :, HN:], cum=cum_y[:, :, :HN], y=cum_y[:, :, HN:],
                        ratios=jnp.concatenate([s0_ref[c, 2], s0_ref[c, 3]], axis=1))
            _, vjp = jax.vjp(lambda *t: _chunk_fn(*t, kept=kept)[:2], s0_inv[:, :, :HN],
                             *[_split_pairs(ref[:, sl, :]) for ref in (r_ref, lw_ref, k_ref, v_ref, kk_ref, a_ref)])
            g = vjp((_split_pairs(do_ref[:, sl, :]), ds_ref[...]))
            ds_ref[...] = g[0]
            for ref, val in zip((dr, dlw, dk, dv, dkk, da), g[1:]):
                ref[:, sl, :] = _join_pairs(val)
            return carry

        lax.fori_loop(0, n_chunks, step, 0)

        for at, action in plan[1:]:
            pl.when(pl.program_id(0) == at)(functools.partial(action, x_in, x_out, sems))

    hm = pl.BlockSpec((NP, tb, PW), lambda i: (0, nb - 1 - i, 0))
    res = pl.pallas_call(
        body, name="rwkv_scan_bwd", grid=(nb,),
        in_specs=[hm] * 6 + [pl.BlockSpec((n_chunks, N_KEPT, NH, HN, PW), lambda i: (nb - 1 - i, 0, 0, 0, 0)), hm]
        + (ex.any_specs if ex else []),
        out_specs=[hm] * 6 + (ex.any_specs if ex else []),
        out_shape=[jax.ShapeDtypeStruct((NP, T, PW), F32)] * 6 + (ex.out_shape if ex else []),
        scratch_shapes=[pltpu.VMEM((NH, HN, HN), F32)] + (ex.sem_shapes if ex else []),
        compiler_params=pltpu.CompilerParams(dimension_semantics=("arbitrary",), vmem_limit_bytes=VMEM_LIMIT),
    )(r, lw, k, v, kk, a, s0s, do, *(ex.bufs if ex else []))
    return list(res[:6]), list(res[6:])


def _shift_down(i, p, prev8):
    first = jnp.where(i > 0, prev8[7:8, :], 0.0)
    row = lax.broadcasted_iota(jnp.int32, p.shape, 0)
    return jnp.where(row == 0, first, pltpu.roll(p, 1, axis=0))


def _mix_bwd(dq, p, sb, tm=256):
    def fn(i, n, dq, next8, p, prev8, sb):
        ps = _shift_down(i, p, prev8)
        d1 = dq * sb[1:2]
        last = jnp.where(i < n - 1, next8[0:1, :] * sb[1:2], 0.0)
        row = lax.broadcasted_iota(jnp.int32, dq.shape, 0)
        up = jnp.where(row == dq.shape[0] - 1, last, pltpu.roll(d1, dq.shape[0] - 1, axis=0))
        return (dq * sb[0:1] + up, jnp.sum(dq * p, axis=0, keepdims=True), jnp.sum(dq * ps, axis=0, keepdims=True))
    w = p.shape[1]
    return _rows_call("shift_mix_bwd", fn, [Rows(dq), Halo(dq, +1), Rows(p), Halo(p, -1)], [sb], [("rows", w, BF16)],
                      accs=[((1, w), F32), ((1, w), F32)], tm=tm, with_pid=True)


def _local_step(x, target, W, late_weights=None, early_grads=None, w_in_grads_ready=None, ffn_grads_ready=None):
    G = {}
    a = _rows_call("norm_mix_fwd", lambda x, g: (_rms(x, g),), [Rows(x)], [W["g_mix"]], [("rows", D, BF16)])[0]
    p_sgu = _matmul("proj_sgu", a, W["w_sgu_t"], "nt")
    def token_shift(p, sb0, sb1):
        row = lax.broadcasted_iota(jnp.int32, p.shape, 0)
        return p, p * sb0 + jnp.where(row == 0, 0.0, pltpu.roll(p, 1, axis=0)) * sb1
    p_rw, q = _matmul("proj_rwkv", a, W["w_rw_t"], "nt", tn=512, whole_rows=True, epilogue=token_shift,
                      extras=[W["sb"][0:1], W["sb"][1:2]], out_dtypes=(F32, F32))
    p_gate = _matmul("proj_gate", a, W["w_gate_t"], "nt")

    sgu_consts = [W["sgu_ln_w"], W["sgu_ln_b"], W["sgu_w"], W["sgu_bt"]]
    s = _rows_call("sgu_fwd", lambda *t: (_sgu_fn(*t),), [Rows(p_sgu)], sgu_consts, [("rows", D, BF16)], tm=SGU_C)[0]

    q_ins = [Rows(q, D, 0), Rows(q, D, 1), Rows(q, D, 2), Rows(q, 128, 24), Rows(q, 128, 25), Rows(q, 256, 13)]
    pre_consts = [W["w_lora"], W["w0"], W["a_lora"], W["a0"], W["g_lora"], W["k_k"], W["k_a"]]
    r_h, lw_h, k_h, v_h, kk_h, a_h, g_gate = _rows_call(
        "rwkv_pre_fwd", _pre_fn, q_ins, pre_consts, [("heads", F32)] * 6 + [("rows", D, F32)], tm=128)
    o_h, s0s, got = _scan_fwd(r_h, lw_h, k_h, v_h, kk_h, a_h, ex=late_weights[0] if late_weights else None)
    if late_weights:
        W = {**W, **late_weights[1](got)}
    y_a = _matmul("proj_a", s, W["w_proj_a"], "nn")
    post_ins = [Heads(o_h), Heads(r_h), Heads(k_h), Heads(v_h), Rows(g_gate)]
    post_consts = [W[n].reshape(NP, 1, PW) for n in ("ln_x_w", "ln_x_b", "r_k")]
    z_b = _rows_call("rwkv_post_fwd", lambda *t: (_post_fn(*t),), post_ins, post_consts, [("rows", D, BF16)], tm=128)[0]
    y_b, mixed = _matmul("proj_b", z_b, W["w_proj_b"], "nn", extras=[(p_gate, 0), (p_gate, 1), y_a],
                         epilogue=lambda yb, ga, gb, ya: (yb, _sigmoid(ga) * ya + _sigmoid(gb) * yb),
                         out_dtypes=(F32, BF16))

    def res1(mo, x, g):
        h1 = x + mo
        return h1, _rms(h1, g)
    h1, f = _matmul("proj_out", mixed, W["w_out"], "nn", extras=[x, W["g_ffn"]], epilogue=res1,
                    out_dtypes=(F32, BF16))

    def relu_sq(u):
        r = jnp.maximum(u, 0.0)
        return r, r * r
    r1, act = _matmul("ffn_up", f, W["w_ffn1"], "nn", epilogue=relu_sq, out_dtypes=(BF16, BF16))
    ff = _matmul("ffn_down", act, W["w_ffn2"], "nn")

    def head(h1, ff, tgt, g):
        def f_(h1, ff, g):
            y = _rms(h1 + ff, g)
            return 0.5 * jnp.sum(jnp.mean(jnp.square(y - tgt), axis=-1))
        loss, (dh2, _, dg) = jax.value_and_grad(f_, argnums=(0, 1, 2))(h1, ff, g)
        return dh2, jnp.full((8, LANES), loss, F32), dg
    dh2, loss_acc, G["g_final"] = _rows_call("loss_head", head, [Rows(h1), Rows(ff), Rows(target)], [W["g_final"]],
                                             [("rows", D, F32)], accs=[((8, LANES), F32), ((1, D), F32)])

    d_u1 = _matmul("ffn_down_dx", dh2, W["w_ffn2"], "nt", extras=[r1], out_dtypes=(BF16,),
                   epilogue=lambda d_act, r: (d_act * 2.0 * r.astype(F32),))[0]
    G["w_ffn2"] = _matmul("ffn_down_dw", act, dh2, "tn", out_dtype=GRAD_PAYLOAD)
    d_f = _matmul("ffn_up_dx", d_u1, W["w_ffn1"], "nt")
    G["w_ffn1"] = _matmul("ffn_up_dw", f, d_u1, "tn", out_blocks=N_DEV, out_dtype=GRAD_PAYLOAD)

    def res1_bwd(h1, d_f, dh2, g):
        _, vjp = jax.vjp(_rms, h1, g)
        dh, dg = vjp(d_f)
        return dh2 + dh, dg
    dh1, G["g_ffn"] = _rows_call("residual_norm_bwd", res1_bwd, [Rows(h1), Rows(d_f), Rows(dh2)],
                                 [W["g_ffn"]], [("rows", D, F32)], accs=[((1, D), F32)])
    ffn_token = ffn_grads_ready(G) if ffn_grads_ready else None
    def gate_bwd(d_mixed, ga, gb, ya, yb):
        _, vjp = jax.vjp(_gate_fn, jnp.concatenate([ga, gb], axis=-1), ya, yb)
        return vjp(d_mixed)
    d_gate, d_ya, d_yb = _matmul("proj_out_dx", dh1, W["w_out"], "nt", after=ffn_token, epilogue=gate_bwd,
                                 extras=[(p_gate, 0), (p_gate, 1), y_a, y_b], out_dtypes=(BF16, BF16, BF16),
                                 out_widths=(2 * D, D, D))
    G["w_out"] = _matmul("proj_out_dw", mixed, dh1, "tn", out_dtype=GRAD_PAYLOAD)

    d_s = _matmul("proj_a_dx", d_ya, W["w_proj_a"], "nt")
    G["w_proj_a"] = _matmul("proj_a_dw", s, d_ya, "tn", out_dtype=GRAD_PAYLOAD)

    def sgu_bwd(p, ds, *c):
        _, vjp = jax.vjp(_sgu_fn, p, *c)
        return vjp(ds)
    d_p_sgu, G["sgu_ln_w"], G["sgu_ln_b"], G["sgu_w"], G["sgu_bt"] = _rows_call(
        "sgu_bwd", sgu_bwd, [Rows(p_sgu), Rows(d_s)], sgu_consts, [("rows", 2 * D, BF16)],
        accs=[((1, D), F32), ((1, D), F32), ((SGU_G, SGU_C, SGU_C), F32), ((SGU_C, SGU_G), F32)], tm=SGU_C)

    d_zb = _matmul("proj_b_dx", d_yb, W["w_proj_b"], "nt")
    G["w_proj_b"] = _matmul("proj_b_dw", z_b, d_yb, "tn", out_dtype=GRAD_PAYLOAD)

    def post_bwd(o, r, k2, v, g, dz, *c):
        _, vjp = jax.vjp(_post_fn, o, r, k2, v, g, *c)
        return vjp(dz)
    do_h, dr1, dk1, dv1, d_g, g_lnw, g_lnb, g_rk = _rows_call(
        "rwkv_post_bwd", post_bwd, post_ins + [Rows(d_zb)], post_consts, [("heads", F32)] * 4 + [("rows", D, F32)],
        accs=[((NP, 1, PW), F32)] * 3, tm=128)
    G["ln_x_w"], G["ln_x_b"], G["r_k"] = (t.reshape(1, D) for t in (g_lnw, g_lnb, g_rk))
    (dr2, dlw, dk2, dv2, dkk, daa), early = _scan_bwd(r_h, lw_h, k_h, v_h, kk_h, a_h, s0s, do_h,
                                                      ex=early_grads(G) if early_grads else None)

    def pre_bwd(qr, qk, qv, qxw, qxa, qxg, dr1, dr2, dlw, dk1, dk2, dv1, dv2, dkk, daa, dg, *c):
        _, vjp = jax.vjp(_pre_fn, qr, qk, qv, qxw, qxa, qxg, *c)
        g = vjp((dr1 + dr2, dlw, dk1 + dk2, dv1 + dv2, dkk, daa, dg))
        dq = jnp.concatenate(g[:6], axis=-1)
        return (dq,) + tuple(g[6:])
    pre_b_ins = q_ins + [Heads(dr1), Heads(dr2), Heads(dlw), Heads(dk1), Heads(dk2), Heads(dv1), Heads(dv2),
                         Heads(dkk), Heads(daa), Rows(d_g)]
    d_q, G["w_lora"], G["w0"], G["a_lora"], G["a0"], G["g_lora"], G["k_k"], G["k_a"] = _rows_call(
        "rwkv_pre_bwd", pre_bwd, pre_b_ins, pre_consts, [("rows", RW_INT, F32)],
        accs=[((128, D), F32), ((1, D), F32), ((128, D), F32), ((1, D), F32), ((256, D), F32), ((1, D), F32),
              ((1, D), F32)], tm=128)
    d_p_rw, dsb0, dsb1 = _mix_bwd(d_q, p_rw, W["sb"])
    G["sb"] = jnp.concatenate([dsb0, dsb1], axis=0)

    G["w_sgu_t"] = _matmul("proj_sgu_dw", d_p_sgu, a, "tn", out_dtype=GRAD_PAYLOAD)
    G["w_rw_t"] = _matmul("proj_rwkv_dw", d_p_rw, a, "tn", out_dtype=GRAD_PAYLOAD)
    G["w_gate_t"] = _matmul("proj_gate_dw", d_gate, a, "tn", out_dtype=GRAD_PAYLOAD)
    token = w_in_grads_ready(G) if w_in_grads_ready else None
    da1 = _matmul("proj_sgu_dx", d_p_sgu, W["w_sgu_t"], "nn", after=token)
    da2 = _matmul("proj_rwkv_dx", d_p_rw, W["w_rw_t"], "nn", after=token)
    da3 = _matmul("proj_gate_dx", d_gate, W["w_gate_t"], "nn", after=token)

    def norm1_bwd(x, da1, da2, da3, dh1, g):
        _, vjp = jax.vjp(_rms, x, g)
        dx, dg = vjp(da1 + da2 + da3)
        return dh1 + dx, dg
    dx, G["g_mix"] = _rows_call("norm_mix_bwd", norm1_bwd, [Rows(x), Rows(da1), Rows(da2), Rows(da3), Rows(dh1)],
                                [W["g_mix"]], [("rows", D, F32)], accs=[((1, D), F32)])
    return loss_acc[0, 0], dx, G, early


class Exchange:
    def __init__(self, bufs, gathers):
        self.bufs, self.gathers, self.nb = list(bufs), list(gathers), len(bufs)
        self.any_specs = [pl.BlockSpec(memory_space=pl.ANY)] * self.nb
        self.out_shape = [jax.ShapeDtypeStruct((N_DEV,) + (b.shape if g else b.shape[1:]), b.dtype)
                          for b, g in zip(self.bufs, self.gathers)]
        n = (N_DEV - 1) * self.nb
        self.sem_shapes = [pltpu.SemaphoreType.DMA((n,)), pltpu.SemaphoreType.DMA((n,)),
                           pltpu.SemaphoreType.DMA((self.nb,))]

    def _copies(self, in_refs, out_refs, sems):
        send_sems, recv_sems, local_sems = sems
        x, y, c = lax.axis_index("x"), lax.axis_index("y"), lax.axis_index("c")
        me = 4 * x + 2 * y + c

        def src(b, dest):
            return in_refs[b] if self.gathers[b] else in_refs[b].at[dest]

        local = [pltpu.make_async_copy(src(b, me), out_refs[b].at[me], local_sems.at[b]) for b in range(self.nb)]
        sends, recvs = [], []
        for kbits in range(1, N_DEV):
            px = 1 - x if kbits & 4 else x
            py = 1 - y if kbits & 2 else y
            pc = 1 - c if kbits & 1 else c
            peer = 4 * px + 2 * py + pc
            for b in range(self.nb):
                s = (kbits - 1) * self.nb + b
                sends.append(pltpu.make_async_remote_copy(
                    src_ref=src(b, peer), dst_ref=out_refs[b].at[me], send_sem=send_sems.at[s],
                    recv_sem=recv_sems.at[s], device_id=(px, py, pc), device_id_type=pl.DeviceIdType.MESH))
                recvs.append(pltpu.make_async_remote_copy(
                    src_ref=src(b, peer), dst_ref=out_refs[b].at[peer], send_sem=send_sems.at[s],
                    recv_sem=recv_sems.at[s], device_id=(px, py, pc), device_id_type=pl.DeviceIdType.MESH))
        return local, sends, recvs

    def start(self, in_refs, out_refs, sems):
        local, sends, _ = self._copies(in_refs, out_refs, sems)
        for cp in sends + local:
            cp.start()

    def wait(self, in_refs, out_refs, sems):
        local, sends, recvs = self._copies(in_refs, out_refs, sems)
        for cp in recvs:
            cp.wait_recv()
        for cp in sends:
            cp.wait_send()
        for cp in local:
            cp.wait()

    def schedule(self, n_steps):
        return [(0, self.start), (n_steps - 1, self.wait)]


def _exchange(name, bufs, gather):
    ex = Exchange(bufs, gather if isinstance(gather, (list, tuple)) else [gather] * len(bufs))

    def body(*refs):
        in_refs, out_refs, sems = refs[:ex.nb], refs[ex.nb:2 * ex.nb], refs[2 * ex.nb:]
        ex.start(in_refs, out_refs, sems)
        ex.wait(in_refs, out_refs, sems)

    return pl.pallas_call(body, name=name, in_specs=ex.any_specs, out_specs=ex.any_specs, out_shape=ex.out_shape,
                          scratch_shapes=ex.sem_shapes)(*ex.bufs)


N_CHIP = 4


def _pair_exchange(name, blocks):
    def body(b_ref, got_ref, send_sems, recv_sems):
        x, y, c = lax.axis_index("x"), lax.axis_index("y"), lax.axis_index("c")
        copies = [pltpu.make_async_remote_copy(
            src_ref=b_ref.at[2 * q + 1 - c], dst_ref=got_ref.at[q], send_sem=send_sems.at[q],
            recv_sem=recv_sems.at[q], device_id=(x, y, 1 - c), device_id_type=pl.DeviceIdType.MESH)
            for q in range(N_CHIP)]
        for cp in copies:
            cp.start()
        for cp in copies:
            cp.wait_recv()
        for cp in copies:
            cp.wait_send()

    any_spec = pl.BlockSpec(memory_space=pl.ANY)
    return pl.pallas_call(
        body, name=name, in_specs=[any_spec], out_specs=any_spec,
        out_shape=jax.ShapeDtypeStruct((N_CHIP,) + blocks.shape[1:], blocks.dtype),
        scratch_shapes=[pltpu.SemaphoreType.DMA((N_CHIP,))] * 2,
    )(blocks)


def _pair_sum(name, blocks, got, core):
    _, R, Wd = blocks.shape

    def body(c_ref, a_ref, b_ref, o_ref):
        o_ref[...] = (a_ref[...].astype(F32) + b_ref[...].astype(F32)).astype(o_ref.dtype)

    return pl.pallas_call(
        body, name=name,
        grid_spec=pltpu.PrefetchScalarGridSpec(
            num_scalar_prefetch=1, grid=(N_CHIP,),
            in_specs=[pl.BlockSpec((None, R, Wd), lambda q, c_ref: (2 * q + c_ref[0], 0, 0)),
                      pl.BlockSpec((None, R, Wd), lambda q, c_ref: (q, 0, 0))],
            out_specs=pl.BlockSpec((None, R, Wd), lambda q, c_ref: (q, 0, 0))),
        out_shape=jax.ShapeDtypeStruct(got.shape, blocks.dtype),
        compiler_params=pltpu.CompilerParams(dimension_semantics=("parallel",), vmem_limit_bytes=VMEM_LIMIT),
    )(core, blocks, got)


def _chip_copies(s_ref, land_ref, send_sems, recv_sems):
    x, y, c = lax.axis_index("x"), lax.axis_index("y"), lax.axis_index("c")
    my_q = 2 * x + y
    sends, recvs = [], []
    for kbits in range(1, N_CHIP):
        px = 1 - x if kbits & 2 else x
        py = 1 - y if kbits & 1 else y
        peer_q = 2 * px + py
        sends.append(pltpu.make_async_remote_copy(
            src_ref=s_ref.at[peer_q], dst_ref=land_ref.at[my_q], send_sem=send_sems[kbits - 1],
            recv_sem=recv_sems[kbits - 1], device_id=(px, py, c), device_id_type=pl.DeviceIdType.MESH))
        recvs.append(pltpu.make_async_remote_copy(
            src_ref=s_ref.at[peer_q], dst_ref=land_ref.at[peer_q], send_sem=send_sems[kbits - 1],
            recv_sem=recv_sems[kbits - 1], device_id=(px, py, c), device_id_type=pl.DeviceIdType.MESH))
    return sends, recvs


_HBM = pl.BlockSpec(memory_space=pltpu.HBM)
_SEM = pl.BlockSpec(memory_space=pltpu.SEMAPHORE)
N_CHIP_SEMS = 2 * (N_CHIP - 1)


def _scatter_copies(b_refs, land_refs, send_sems, recv_sems):
    x, y, c = lax.axis_index("x"), lax.axis_index("y"), lax.axis_index("c")
    me = 4 * x + 2 * y + c
    sends, recvs = [], []
    for kbits in range(1, N_DEV):
        px = 1 - x if kbits & 4 else x
        py = 1 - y if kbits & 2 else y
        pc = 1 - c if kbits & 1 else c
        peer = 4 * px + 2 * py + pc
        for b in range(len(b_refs)):
            s = (kbits - 1) * len(b_refs) + b
            sends.append(pltpu.make_async_remote_copy(
                src_ref=b_refs[b].at[peer], dst_ref=land_refs[b].at[me], send_sem=send_sems[s],
                recv_sem=recv_sems[s], device_id=(px, py, pc), device_id_type=pl.DeviceIdType.MESH))
            recvs.append(pltpu.make_async_remote_copy(
                src_ref=b_refs[b].at[peer], dst_ref=land_refs[b].at[peer], send_sem=send_sems[s],
                recv_sem=recv_sems[s], device_id=(px, py, pc), device_id_type=pl.DeviceIdType.MESH))
    return sends, recvs


def _scatter_start(name, bufs):
    nb = len(bufs)
    n = (N_DEV - 1) * nb

    def body(*refs):
        b_refs, land_refs, outs = refs[:nb], refs[nb:2 * nb], refs[2 * nb:]
        sends, _ = _scatter_copies(b_refs, land_refs, outs[:n], outs[n:2 * n])
        for cp in sends:
            cp.start()
        token = outs[2 * n + 2 * nb]
        token[...] = jnp.zeros_like(token)

    thru = tuple(pltpu.HBM(b.shape, b.dtype) for b in bufs)
    res = pl.pallas_call(
        body, name=name, in_specs=(_HBM,) * (2 * nb),
        out_specs=(_SEM,) * (2 * n) + (_HBM,) * (2 * nb) + (pl.BlockSpec(memory_space=pltpu.VMEM),),
        out_shape=(pltpu.SemaphoreType.DMA(()),) * (2 * n) + thru + thru + (jax.ShapeDtypeStruct((8, LANES), F32),),
        input_output_aliases={i: 2 * n + i for i in range(2 * nb)},
        compiler_params=pltpu.CompilerParams(has_side_effects=pltpu.SideEffectType.DATAFLOW_SIDE_EFFECTING),
    )(*[pltpu.with_memory_space_constraint(b, pltpu.HBM) for b in bufs],
      *[pltpu.with_memory_space_constraint(lax.empty(b.shape, b.dtype), pltpu.HBM) for b in bufs])
    return res[:2 * n], list(res[2 * n:2 * n + nb]), list(res[2 * n + nb:2 * n + 2 * nb]), res[2 * n + 2 * nb]


def _copy_through_vmem(pairs, stage, sems):
    for into_vmem in (True, False):
        copies = [pltpu.make_async_copy(src if into_vmem else stage[i], stage[i] if into_vmem else dst, sems.at[i])
                  for i, (src, dst) in enumerate(pairs)]
        for cp in copies:
            cp.start()
        for cp in copies:
            cp.wait()


def _scatter_wait(name, sems, bufs_thru, lands_thru, after):
    nb = len(bufs_thru)
    n = (N_DEV - 1) * nb

    def body(*refs):
        b_refs, land_refs, sem_refs = refs[:nb], refs[nb:2 * nb], refs[2 * nb:2 * nb + 2 * n]
        stage, local_sems = refs[-(nb + 1):-1], refs[-1]
        me = 4 * lax.axis_index("x") + 2 * lax.axis_index("y") + lax.axis_index("c")
        _copy_through_vmem([(b_refs[b].at[me], land_refs[b].at[me]) for b in range(nb)], stage, local_sems)
        sends, recvs = _scatter_copies(b_refs, land_refs, sem_refs[:n], sem_refs[n:])
        for cp in sends:
            cp.wait_send()
        for cp in recvs:
            cp.wait_recv()

    thru = tuple(pltpu.HBM(b.shape, b.dtype) for b in bufs_thru)
    res = pl.pallas_call(
        body, name=name, in_specs=(_HBM,) * (2 * nb) + (_SEM,) * (2 * n) + (pl.BlockSpec(memory_space=pl.ANY),),
        out_specs=(_HBM,) * (2 * nb), out_shape=thru + thru,
        input_output_aliases={i: i for i in range(2 * nb)},
        scratch_shapes=[pltpu.VMEM(b.shape[1:], b.dtype) for b in bufs_thru] + [pltpu.SemaphoreType.DMA((nb,))],
        compiler_params=pltpu.CompilerParams(has_side_effects=pltpu.SideEffectType.DATAFLOW_SIDE_EFFECTING,
                                             vmem_limit_bytes=VMEM_LIMIT),
    )(*bufs_thru, *lands_thru, *sems, after)
    return list(res[:nb]), list(res[nb:])


def _chip_exchange_start(name, sums):
    def body(s_ref, land_ref, *outs):
        sems, token = outs[:N_CHIP_SEMS], outs[N_CHIP_SEMS + 2]
        sends, _ = _chip_copies(s_ref, land_ref, sems[:N_CHIP - 1], sems[N_CHIP - 1:])
        for cp in sends:
            cp.start()
        token[...] = jnp.zeros_like(token)

    res = pl.pallas_call(
        body, name=name, in_specs=(_HBM, _HBM),
        out_specs=(_SEM,) * N_CHIP_SEMS + (_HBM, _HBM, pl.BlockSpec(memory_space=pltpu.VMEM)),
        out_shape=(pltpu.SemaphoreType.DMA(()),) * N_CHIP_SEMS
        + (pltpu.HBM(sums.shape, sums.dtype), pltpu.HBM(sums.shape, sums.dtype), jax.ShapeDtypeStruct((8, LANES), F32)),
        input_output_aliases={0: N_CHIP_SEMS, 1: N_CHIP_SEMS + 1},
        compiler_params=pltpu.CompilerParams(has_side_effects=pltpu.SideEffectType.DATAFLOW_SIDE_EFFECTING),
    )(pltpu.with_memory_space_constraint(sums, pltpu.HBM),
      pltpu.with_memory_space_constraint(lax.empty(sums.shape, sums.dtype), pltpu.HBM))
    return res[:N_CHIP_SEMS], res[N_CHIP_SEMS], res[N_CHIP_SEMS + 1], res[N_CHIP_SEMS + 2]


def _chip_exchange_wait(name, sems, sums_thru, land_thru, after):
    def body(s_ref, land_ref, *rest):
        sems, stage, local_sems = rest[:N_CHIP_SEMS], rest[-2], rest[-1]
        my_chip = 2 * lax.axis_index("x") + lax.axis_index("y")
        _copy_through_vmem([(s_ref.at[my_chip], land_ref.at[my_chip])], [stage], local_sems)
        sends, recvs = _chip_copies(s_ref, land_ref, sems[:N_CHIP - 1], sems[N_CHIP - 1:])
        for cp in sends:
            cp.wait_send()
        for cp in recvs:
            cp.wait_recv()

    return pl.pallas_call(
        body, name=name, in_specs=(_HBM, _HBM) + (_SEM,) * N_CHIP_SEMS + (pl.BlockSpec(memory_space=pl.ANY),),
        out_specs=(_HBM, _HBM),
        out_shape=(pltpu.HBM(sums_thru.shape, sums_thru.dtype), pltpu.HBM(sums_thru.shape, sums_thru.dtype)),
        input_output_aliases={0: 0, 1: 1},
        scratch_shapes=[pltpu.VMEM(sums_thru.shape[1:], sums_thru.dtype), pltpu.SemaphoreType.DMA((1,))],
        compiler_params=pltpu.CompilerParams(has_side_effects=pltpu.SideEffectType.DATAFLOW_SIDE_EFFECTING,
                                             vmem_limit_bytes=VMEM_LIMIT),
    )(sums_thru, land_thru, *sems, after)


def _adam_update(g, w, m, v):
    m_new = ADAM_B1 * m + (1.0 - ADAM_B1) * g
    v_new = ADAM_B2 * v + (1.0 - ADAM_B2) * jnp.square(g)
    m_hat = m_new / (1.0 - ADAM_B1 ** ADAM_STEP)
    v_hat = v_new / (1.0 - ADAM_B2 ** ADAM_STEP)
    return -ADAM_LR * (m_hat / (jnp.sqrt(v_hat) + ADAM_EPS) + ADAM_WD * w), m_new, v_new


def _adamw_rows(name, slots, parts):
    n_parts, rows = len(parts), [p[0].shape[0] for p in parts]
    rest = slots.shape[1] - sum(rows)

    def total(s_ref, at, r):
        g = s_ref[0, pl.ds(at, r), :]
        for j in range(1, slots.shape[0]):
            g = g + s_ref[j, pl.ds(at, r), :]
        return g

    def body(s_ref, *refs):
        ins, outs = refs[:3 * n_parts], refs[3 * n_parts:]
        at = 0
        for i, r in enumerate(rows):
            g = total(s_ref, at, r)
            w_ref, m_ref, v_ref = ins[3 * i:3 * i + 3]
            g_out, d_out, m_out, v_out = outs[4 * i:4 * i + 4]
            g_out[...] = g
            d_out[...], m_out[...], v_out[...] = _adam_update(g, w_ref[...], m_ref[...], v_ref[...])
            at += r
        outs[4 * n_parts][...] = total(s_ref, at, rest)

    shapes = [jax.ShapeDtypeStruct(p[0].shape, F32) for p in parts for _ in range(4)]
    res = pl.pallas_call(
        body, name=name, out_shape=shapes + [jax.ShapeDtypeStruct((rest, slots.shape[2]), F32)],
        compiler_params=pltpu.CompilerParams(vmem_limit_bytes=VMEM_LIMIT),
    )(slots, *[t for p in parts for t in p])
    return [res[4 * i:4 * i + 4] for i in range(n_parts)], res[4 * n_parts]


def _adamw_whole(name, items):
    n_items = len(items)

    n = 4 * n_items
    flat = [t for item in items for t in item]
    out_shape = [jax.ShapeDtypeStruct(item[1].shape, F32) for item in items for _ in range(4)]

    def body(*refs):
        hbm_in, hbm_out, ins, outs = refs[:n], refs[n:2 * n], refs[2 * n:3 * n], refs[3 * n:4 * n]
        read_sems, write_sems = refs[4 * n], refs[4 * n + 1]
        reads = [pltpu.make_async_copy(hbm_in[i], ins[i], read_sems.at[i]) for i in range(n)]
        writes = [pltpu.make_async_copy(outs[i], hbm_out[i], write_sems.at[i]) for i in range(n)]
        for cp in reads:
            cp.start()
        for i in range(n_items):
            for cp in reads[4 * i:4 * i + 4]:
                cp.wait()
            s_ref, w_ref, m_ref, v_ref = ins[4 * i:4 * i + 4]
            g_out, d_out, m_out, v_out = outs[4 * i:4 * i + 4]
            g = s_ref[0].astype(F32)
            for j in range(1, s_ref.shape[0]):
                g = g + s_ref[j].astype(F32)
            g_out[0] = g
            d_out[0], m_out[0], v_out[0] = _adam_update(g, w_ref[0], m_ref[0], v_ref[0])
            for cp in writes[4 * i:4 * i + 4]:
                cp.start()
        for cp in writes:
            cp.wait()

    res = pl.pallas_call(
        body, name=name, out_shape=out_shape,
        in_specs=[pl.BlockSpec(memory_space=pl.ANY)] * n, out_specs=[pl.BlockSpec(memory_space=pl.ANY)] * n,
        scratch_shapes=[pltpu.VMEM(t.shape, t.dtype) for t in flat] + [pltpu.VMEM(o.shape, F32) for o in out_shape]
        + [pltpu.SemaphoreType.DMA((n,)), pltpu.SemaphoreType.DMA((n,))],
        compiler_params=pltpu.CompilerParams(vmem_limit_bytes=VMEM_LIMIT),
    )(*flat)
    return [res[4 * i:4 * i + 4] for i in range(n_items)]


def _adamw(name, slots, w, m, v, tr=256):
    unit_mid = w.ndim == 3 and w.shape[1] == 1 and w.shape[0] > 1
    R, Wd = (w.shape[0], w.shape[2]) if unit_mid else w.shape[-2:]
    depth_axis = w.ndim == 3 and not unit_mid
    if unit_mid:
        tr, tc = 128, Wd
    elif R % tr == 0:
        tc = Wd
    else:
        tr, tc = R, (256 if (Wd % 256 == 0 and R > 256) else Wd)
    at = (slice(None), 0, slice(None)) if unit_mid else Ellipsis

    def body(s_ref, w_ref, m_ref, v_ref, g_out, d_out, m_out, v_out):
        g = s_ref[0].astype(F32)
        for j in range(1, slots.shape[0]):
            g = g + s_ref[j].astype(F32)
        g_out[at] = g
        d_out[at], m_out[at], v_out[at] = _adam_update(g, w_ref[at], m_ref[at], v_ref[at])

    if unit_mid:
        row = pl.BlockSpec((tr, 1, tc), lambda i, j: (i, 0, j))
    elif depth_axis:
        row = pl.BlockSpec((None, tr, tc), lambda i, j: (0, i, j))
    else:
        row = pl.BlockSpec((tr, tc), lambda i, j: (i, j))
    return pl.pallas_call(
        body, name=name, grid=(pl.cdiv(R, tr), Wd // tc),
        in_specs=[pl.BlockSpec((slots.shape[0], tr, tc), lambda i, j: (0, i, j)), row, row, row],
        out_specs=[row] * 4, out_shape=[jax.ShapeDtypeStruct(w.shape, F32)] * 4,
        compiler_params=pltpu.CompilerParams(dimension_semantics=("parallel", "parallel"),
                                             vmem_limit_bytes=VMEM_LIMIT),
    )(slots, w, m, v)


PACK_W = 1024
PACKED = [(n, s) for n, s in REPLICATED if n != "sgu_w"]
_SMALL_SIZES = [int(np.prod(s)) for _, s in PACKED]
_SMALL_ROWS = _round_up(_round_up(sum(_SMALL_SIZES) + PACK_W, PACK_W) // PACK_W, 8)
assert all(size % PACK_W == 0 for size in _SMALL_SIZES)
W_IN_SHARD = P_TOTAL // N_DEV


def _pack_rows(name, parts, rows, after=()):
    parts = [p.reshape(-1, PACK_W) for p in parts]
    assert all(p.dtype == F32 for p in parts)

    def body(*refs):
        at = 0
        for ref in refs[:len(parts)]:
            refs[-1][pl.ds(at, ref.shape[0]), :] = ref[...]
            at += ref.shape[0]
        refs[-1][pl.ds(at, rows - at), :] = jnp.zeros((rows - at, PACK_W), F32)

    return pl.pallas_call(
        body, name=name, out_shape=jax.ShapeDtypeStruct((rows, PACK_W), F32),
        in_specs=[pl.BlockSpec(memory_space=pltpu.VMEM)] * len(parts) + [pl.BlockSpec(memory_space=pl.ANY)] * len(after),
        out_specs=pl.BlockSpec(memory_space=pltpu.VMEM))(*parts, *after)


_W_IN_SEGMENTS = [(2 * D, 0, 0), (3 * D, 1, 0), (L_W, 1, 3 * D), (L_A, 1, 3 * D + 128), (L_G, 1, 3 * D + 256),
                  (2 * D, 2, 0)]
_W_IN_PADS = [(3 * D + L_W, 128 - L_W), (3 * D + 128 + L_A, 128 - L_A), (3 * D + 256 + L_G, 256 - L_G)]
_W_IN_GROUP_ROWS = [2 * D, 3 * D + 128 + 128 + 256, 2 * D]
assert sum(rows for rows, _, _ in _W_IN_SEGMENTS) == P_TOTAL


def _w_in_pieces(j):
    pieces, first = [], 0
    for rows, group, to in _W_IN_SEGMENTS:
        lo, hi = max(first, W_IN_SHARD * j), min(first + rows, W_IN_SHARD * (j + 1))
        if lo < hi:
            pieces.append((lo - W_IN_SHARD * j, hi - lo, group, to + lo - first))
        first += rows
    return pieces


_W_IN_GROUP_BLOCKS = [(min(js), max(js)) for js in
                      ([j for j in range(N_DEV) if any(p[2] == g for p in _w_in_pieces(j))] for g in range(3))]


def _w_in_groups_t(blocks):
    def body(x_ref, *refs):
        o_refs, built, sems = refs[:3], refs[3:6], refs[6]
        writes = [pltpu.make_async_copy(built[g], o_refs[g], sems.at[g]) for g in range(3)]

        @pl.when(pl.program_id(0) == 0)
        def _():
            for at, rows in _W_IN_PADS:
                built[1][pl.ds(at, rows), :] = jnp.zeros((rows, D), blocks.dtype)
        for j in range(N_DEV):
            @pl.when(pl.program_id(0) == j)
            def _(j=j):
                for at, rows, group, to in _w_in_pieces(j):
                    built[group][pl.ds(to, rows), :] = x_ref[pl.ds(at, rows), :]
                for g in range(3):
                    if j == _W_IN_GROUP_BLOCKS[g][1]:
                        writes[g].start()
                if j == N_DEV - 1:
                    for w in writes:
                        w.wait()

    return pl.pallas_call(
        body, name="w_in_groups", grid=(N_DEV,),
        in_specs=[pl.BlockSpec((None, W_IN_SHARD, D), lambda j: (j, 0, 0))],
        out_specs=[pl.BlockSpec(memory_space=pl.ANY)] * 3,
        out_shape=[jax.ShapeDtypeStruct((r, D), blocks.dtype) for r in _W_IN_GROUP_ROWS],
        scratch_shapes=[pltpu.VMEM((r, D), blocks.dtype) for r in _W_IN_GROUP_ROWS] + [pltpu.SemaphoreType.DMA((3,))],
        compiler_params=pltpu.CompilerParams(dimension_semantics=("arbitrary",), vmem_limit_bytes=VMEM_LIMIT),
    )(blocks)


def _w_in_grad_blocks(g_sgu_t, g_rw_t, g_gate_t):
    def body(*refs):
        g_refs, o_ref, here, sems = refs[:3], refs[3], refs[4:7], refs[7]
        reads = [pltpu.make_async_copy(g_refs[g], here[g], sems.at[g]) for g in range(3)]
        for j in range(N_DEV):
            @pl.when(pl.program_id(0) == j)
            def _(j=j):
                if j == 0:
                    for r in reads:
                        r.start()
                for g in range(3):
                    if j == _W_IN_GROUP_BLOCKS[g][0]:
                        reads[g].wait()
                for at, rows, group, to in _w_in_pieces(j):
                    o_ref[pl.ds(at, rows), :] = here[group][pl.ds(to, rows), :]

    return pl.pallas_call(
        body, name="w_in_grad_blocks", grid=(N_DEV,),
        in_specs=[pl.BlockSpec(memory_space=pl.ANY)] * 3,
        out_specs=pl.BlockSpec((None, W_IN_SHARD, D), lambda j: (j, 0, 0)),
        out_shape=jax.ShapeDtypeStruct((N_DEV, W_IN_SHARD, D), g_sgu_t.dtype),
        scratch_shapes=[pltpu.VMEM((r, D), g_sgu_t.dtype) for r in _W_IN_GROUP_ROWS] + [pltpu.SemaphoreType.DMA((3,))],
        compiler_params=pltpu.CompilerParams(dimension_semantics=("arbitrary",), vmem_limit_bytes=VMEM_LIMIT),
    )(g_sgu_t, g_rw_t, g_gate_t)


class TwoLevelGather:
    N_COPIES = 8
    PART_ALIGN = 16

    def __init__(self, bufs, stage_own=()):
        self.bufs, self.nb, self.stage_own = list(bufs), len(bufs), tuple(stage_own)
        self.any_specs = [pl.BlockSpec(memory_space=pl.ANY)] * self.nb
        self.out_shape = [jax.ShapeDtypeStruct((N_DEV,) + b.shape, b.dtype) for b in self.bufs]
        n = self.N_COPIES * self.nb
        self.sem_shapes = [pltpu.SemaphoreType.DMA((n,)), pltpu.SemaphoreType.DMA((n,)),
                           pltpu.SemaphoreType.DMA((self.nb,))]
        if self.stage_own:
            self.sem_shapes += [pltpu.SemaphoreType.DMA((len(self.stage_own),))]
            self.sem_shapes += [pltpu.VMEM(self.bufs[b].shape, self.bufs[b].dtype) for b in self.stage_own]

    def _parts(self, b):
        shape = self.bufs[b].shape
        first = shape[0] // 2 // self.PART_ALIGN * self.PART_ALIGN if len(shape) == 2 else 0
        return [(0, first), (first, shape[0] - first)] if first else [None]

    def _copies(self, in_refs, out_refs, sems):
        send_sems, recv_sems, local_sems = sems[:3]
        nb = self.nb
        x, y, c = lax.axis_index("x"), lax.axis_index("y"), lax.axis_index("c")
        me, sibling = (x, y, c), (x, y, 1 - c)
        near, far = [(1 - x, y), (x, 1 - y)], (1 - x, 1 - y)

        def slot(b, dev):
            return out_refs[b].at[4 * dev[0] + 2 * dev[1] + dev[2]]

        def copy(b, k, block, to, own=False, rows=None):
            src, dst = in_refs[b] if own else slot(b, block), slot(b, block)
            if rows is not None:
                src, dst = src.at[pl.ds(*rows)], dst.at[pl.ds(*rows)]
            return pltpu.make_async_remote_copy(
                src_ref=src, dst_ref=dst, send_sem=send_sems.at[self.N_COPIES * b + k],
                recv_sem=recv_sems.at[self.N_COPIES * b + k], device_id=to, device_id_type=pl.DeviceIdType.MESH)

        ways = [(j, b) for j in range(2) for b in range(nb) if j < len(self._parts(b))]
        cp = {}
        cp["local"] = [pltpu.make_async_copy(in_refs[b], slot(b, me), local_sems.at[b]) for b in range(nb)
                       if b not in self.stage_own]
        cp["first"] = [copy(b, 0, me, sibling, own=True) for b in range(nb)]
        cp["first"] += [copy(b, 1 + j, me, (*near[j], c), own=True) for j in range(2) for b in range(nb)]
        cp["from_near"] = [copy(b, 1 + j, (*near[j], c), me) for j in range(2) for b in range(nb)]
        cp["near_on"] = [[copy(b, 5 + j, (*near[j], c), sibling)]
                         + ([copy(b, 3 + j, (*near[j], c), (*near[1 - j], c), rows=self._parts(b)[j])]
                            if (j, b) in ways else []) for j in range(2) for b in range(nb)]
        cp["from_far"] = [copy(b, 3 + j, (*far, c), me, rows=self._parts(b)[j]) for j, b in ways]
        cp["far_on"] = [copy(b, 7, (*far, c), sibling) for b in range(nb)]
        cp["from_sibling"] = [copy(b, 0, sibling, me) for b in range(nb)]
        cp["from_sibling"] += [copy(b, 5 + j, (*near[j], 1 - c), me) for j in range(2) for b in range(nb)]
        cp["from_sibling"] += [copy(b, 7, (*far, 1 - c), me) for b in range(nb)]
        return cp

    def start(self, in_refs, out_refs, sems):
        cp = self._copies(in_refs, out_refs, sems)
        for c in cp["first"] + cp["local"]:
            c.start()
        if self.stage_own:
            me = 4 * lax.axis_index("x") + 2 * lax.axis_index("y") + lax.axis_index("c")
            _copy_through_vmem([(in_refs[b], out_refs[b].at[me]) for b in self.stage_own], sems[4:], sems[3])

    def pass_near(self, in_refs, out_refs, sems):
        cp = self._copies(in_refs, out_refs, sems)
        for arrived, onward in zip(cp["from_near"], cp["near_on"]):
            arrived.wait_recv()
            for c in onward:
                c.start()

    def pass_far(self, in_refs, out_refs, sems):
        cp = self._copies(in_refs, out_refs, sems)
        for c in cp["from_far"]:
            c.wait_recv()
        for c in cp["far_on"]:
            c.start()

    def finish(self, in_refs, out_refs, sems):
        cp = self._copies(in_refs, out_refs, sems)
        for c in cp["from_sibling"]:
            c.wait_recv()
        for c in cp["first"] + sum(cp["near_on"], []) + cp["far_on"]:
            c.wait_send()
        for c in cp["local"]:
            c.wait()

    def schedule(self, n_steps):
        return [(0, self.start), (max(n_steps // 2 - 1, 0), self.pass_near), (max(n_steps - 3, 0), self.pass_far),
                (n_steps - 1, self.finish)]


def _all_gather_two_level(name, bufs, stage_own=()):
    ex = TwoLevelGather(bufs, stage_own)

    def body(*refs):
        args = refs[:ex.nb], refs[ex.nb:2 * ex.nb], refs[2 * ex.nb:]
        for _, action in ex.schedule(1):
            action(*args)

    return pl.pallas_call(body, name=name, in_specs=ex.any_specs, out_specs=ex.any_specs, out_shape=ex.out_shape,
                          scratch_shapes=ex.sem_shapes)(*ex.bufs)


def _cols_from_blocks(blk):
    return jnp.transpose(blk, (1, 0, 2)).reshape(blk.shape[1], -1)


def _cols_to_blocks(g):
    r, c = g.shape
    return jnp.transpose(g.reshape(r, N_DEV, c // N_DEV), (1, 0, 2))


FIRST_WEIGHTS = ["w_in", "shift_b", "w_lora_w", "a_lora_w", "g_lora_w"]
LATE_WEIGHTS = ["w_proj_a", "w_proj_b", "w_out", "w_ffn1", "w_ffn2"]
SCAN_CARRIED = ["w_proj_a", "w_proj_b", "w_out"]
FFN_WEIGHTS = ["w_ffn1", "w_ffn2"]


def _late_weights(shards):
    ex = TwoLevelGather([shards[n][0].astype(BF16) for n in LATE_WEIGHTS])

    def finish(results):
        got = dict(zip(LATE_WEIGHTS, results))
        W = {n: got[n].reshape(-1, D) for n in ("w_proj_a", "w_proj_b", "w_out", "w_ffn2")}
        W["w_ffn1"] = got["w_ffn1"].reshape(N_DEV, D, -1)
        return W
    return ex, finish


def _gather_weights(shards):
    def payload(n):
        if n == "w_in":
            return jnp.transpose(shards[n][0]).astype(BF16)
        return shards[n] if n == "shift_b" else shards[n].astype(BF16)
    payloads = [payload(n) for n in FIRST_WEIGHTS]
    got = dict(zip(FIRST_WEIGHTS, _all_gather_two_level("weight_all_gather", payloads, stage_own=(0,))))
    W = {}
    W["w_sgu_t"], W["w_rw_t"], W["w_gate_t"] = _w_in_groups_t(got["w_in"])
    z = lambda r, c, dt: jnp.zeros((r, c), dt)
    W["w_lora"] = jnp.concatenate([_cols_from_blocks(got["w_lora_w"][:, 0]).astype(F32), z(128 - L_W, D, F32)], axis=0)
    W["a_lora"] = jnp.concatenate([_cols_from_blocks(got["a_lora_w"][:, 0]).astype(F32), z(128 - L_A, D, F32)], axis=0)
    W["g_lora"] = jnp.concatenate([_cols_from_blocks(got["g_lora_w"][:, 0]).astype(F32), z(256 - L_G, D, F32)], axis=0)
    sb = _cols_from_blocks(got["shift_b"][:, 0])
    W["sb"] = jnp.concatenate([sb[:, :3 * D], sb[:, 3 * D:3 * D + L_W], z(2, 128 - L_W, F32),
                               sb[:, 3 * D + L_W:3 * D + L_W + L_A], z(2, 128 - L_A, F32),
                               sb[:, 3 * D + L_W + L_A:], z(2, 256 - L_G, F32)], axis=1)
    return W


def _replicated_weights(rep):
    W = {n: rep[n] for n in ("g_mix", "sgu_ln_w", "sgu_ln_b", "w0", "a0", "k_k", "k_a", "r_k", "ln_x_w", "ln_x_b",
                             "g_ffn")}
    W["g_final"] = rep["g_final"].reshape(1, D)
    W["sgu_w"] = rep["sgu_w"][0]
    W["sgu_bt"] = jnp.transpose(rep["sgu_b"][0])
    return W


def _late_grad_blocks(G):
    return Exchange([G[n].reshape(N_DEV, -1, D) for n in SCAN_CARRIED]
                    + [G["sgu_w"].reshape(SGU_G * SGU_C, SGU_C).astype(GRAD_PAYLOAD)],
                    [False] * len(SCAN_CARRIED) + [True])


def _first_grad_blocks(G):
    sbg = G["sb"]
    c = 3 * D
    sb = jnp.concatenate([sbg[:, :c], sbg[:, c:c + L_W], sbg[:, c + 128:c + 128 + L_A],
                          sbg[:, c + 256:c + 256 + L_G]], axis=1)
    return {
        "shift_b": _cols_to_blocks(sb),
        "w_lora_w": _cols_to_blocks(G["w_lora"][:L_W]), "a_lora_w": _cols_to_blocks(G["a_lora"][:L_A]),
        "g_lora_w": _cols_to_blocks(G["g_lora"][:L_G]),
    }


def _replicated_grads(G):
    small = {n: G[n] for n in ("g_mix", "sgu_ln_w", "sgu_ln_b", "w0", "a0", "k_k", "k_a", "r_k", "ln_x_w", "ln_x_b",
                               "g_ffn", "g_final")}
    small["sgu_w"] = G["sgu_w"]
    small["sgu_b"] = jnp.transpose(G["sgu_bt"])
    return small


def kernel(x, g_mix, w_in, sgu_ln_w, sgu_ln_b, sgu_w, sgu_b, w_proj_a, shift_b, w_lora_w, w0, a_lora_w, a0, g_lora_w, k_k, k_a, r_k, ln_x_w, ln_x_b, w_proj_b, w_out, g_ffn, w_ffn1, w_ffn2, g_final, loss_target, m_g_mix, m_w_in, m_sgu_ln_w, m_sgu_ln_b, m_sgu_w, m_sgu_b, m_w_proj_a, m_shift_b, m_w_lora_w, m_w0, m_a_lora_w, m_a0, m_g_lora_w, m_k_k, m_k_a, m_r_k, m_ln_x_w, m_ln_x_b, m_w_proj_b, m_w_out, m_g_ffn, m_w_ffn1, m_w_ffn2, m_g_final, v_g_mix, v_w_in, v_sgu_ln_w, v_sgu_ln_b, v_sgu_w, v_sgu_b, v_w_proj_a, v_shift_b, v_w_lora_w, v_w0, v_a_lora_w, v_a0, v_g_lora_w, v_k_k, v_k_a, v_r_k, v_ln_x_w, v_ln_x_b, v_w_proj_b, v_w_out, v_g_ffn, v_w_ffn1, v_w_ffn2, v_g_final):
    env = dict(locals())
    weights = {n: env[n] for n in WEIGHT_ORDER}
    moms = {n: env["m_" + n] for n in WEIGHT_ORDER}
    vars_ = {n: env["v_" + n] for n in WEIGHT_ORDER}

    shards = {n: weights[n] for n, _, _ in SHARDED}
    W = _gather_weights(shards)
    W.update(_replicated_weights({n: weights[n] for n, _ in REPLICATED}))
    in_flight = {}

    def send_w_in_grads(G):
        blocks = _w_in_grad_blocks(G["w_sgu_t"], G["w_rw_t"], G["w_gate_t"])
        got = _pair_exchange("grad_pair_exchange", blocks)
        core = lax.axis_index("c").astype(jnp.int32).reshape(1)
        sums = _pair_sum("grad_pair_sum", blocks, got, core)
        in_flight["sems"], in_flight["sums"], in_flight["land"], token = _chip_exchange_start("grad_chip_start", sums)
        return token

    def send_ffn_grads(G):
        in_flight["ffn"] = _scatter_start("grad_ffn_start", [G["w_ffn1"], G["w_ffn2"].reshape(N_DEV, -1, D)])
        return in_flight["ffn"][3]

    loss_part, dx, G, late_slots = _local_step(x[0], loss_target[0], W, late_weights=_late_weights(shards),
                                               early_grads=_late_grad_blocks, w_in_grads_ready=send_w_in_grads,
                                               ffn_grads_ready=send_ffn_grads)

    slots = dict(zip(SCAN_CARRIED, late_slots))
    _, landed = _scatter_wait("grad_ffn_wait", *in_flight["ffn"][:3], after=dx)
    slots.update(zip(FFN_WEIGHTS, landed))
    outs = [dict(), dict(), dict(), dict()]

    def keep(n, res):
        for k in range(4):
            outs[k][n] = res[k]

    def update_group(name, group):
        group_out = _adamw_whole(name, [(slots[n], weights[n], moms[n], vars_[n]) for n in group])
        for n, res in zip(group, group_out):
            keep(n, res)

    for n in FFN_WEIGHTS:
        keep(n, _adamw("adamw_" + n, slots[n], weights[n], moms[n], vars_[n]))
    update_group("adamw_projections", SCAN_CARRIED)
    sgu_shape = (SGU_G * SGU_C, SGU_C)
    res = _adamw("adamw_sgu_w", late_slots[len(SCAN_CARRIED)], *[t.reshape(sgu_shape) for t in
                                                                  (weights["sgu_w"], moms["sgu_w"], vars_["sgu_w"])])
    keep("sgu_w", [t.reshape(weights["sgu_w"].shape) for t in res])
    updated = [outs[1][n] for n in FFN_WEIGHTS + SCAN_CARRIED + ["sgu_w"]]

    blocks = _first_grad_blocks(G)
    small = _replicated_grads(G)
    small_parts = [small[n] for n, _ in PACKED] + [jnp.full((PACK_W,), loss_part, F32)]
    rest = [n for n in FIRST_WEIGHTS if n != "w_in"]
    packed = _pack_rows("grad_pack", small_parts, _SMALL_ROWS, after=updated)
    res = _exchange("grad_exchange", [blocks[n] for n in rest] + [packed], [False] * len(rest) + [True])
    slots.update(zip(rest, res[:-1]))
    small_slots = res[-1]
    _, slots["w_in"] = _chip_exchange_wait("grad_chip_wait", in_flight["sems"], in_flight["sums"], in_flight["land"],
                                           after=small_slots)

    res = _adamw("adamw_w_in", slots["w_in"], *[jnp.transpose(t, (2, 0, 1)) for t in (weights["w_in"], moms["w_in"], vars_["w_in"])])
    keep("w_in", [jnp.transpose(t, (1, 2, 0)) for t in res])
    update_group("adamw_low_rank", rest)

    small_out, after_them = _adamw_rows(
        "adamw_replicated", small_slots,
        [[d[n].reshape(-1, PACK_W) for d in (weights, moms, vars_)] for n, _ in PACKED])
    for (n, s), res in zip(PACKED, small_out):
        for k in range(4):
            outs[k][n] = res[k].reshape(s)
    loss = after_them[0, 0]
    return (loss, dx[None], *[outs[0][n] for n in WEIGHT_ORDER], *[outs[1][n] for n in WEIGHT_ORDER],
            *[outs[2][n] for n in WEIGHT_ORDER], *[outs[3][n] for n in WEIGHT_ORDER])
```
